```python
import jax, jax.numpy as jnp
from jax import lax
import numpy as np

D_MODEL = 1024
BATCH = 8
SEQ = 4096
DEPTH = 1

HEAD_DIM = 128
N_ATTN_HEADS = D_MODEL // HEAD_DIM
D_ATTN = N_ATTN_HEADS * HEAD_DIM
D_LRU = D_MODEL
N_LRU_BLOCKS = 8
LRU_BLOCK = D_LRU // N_LRU_BLOCKS
CONV_WIDTH = 4
LRU_C = 8.0
D_MIX = D_ATTN + D_LRU
D_PLE = 256
Q_BLOCK = 128
RMS_EPS = 1e-6
D_IN = 4 * D_ATTN + N_ATTN_HEADS + 2 * D_LRU
SPLIT_POINTS = [D_ATTN, 2 * D_ATTN, 3 * D_ATTN, 3 * D_ATTN + N_ATTN_HEADS,
                4 * D_ATTN + N_ATTN_HEADS, 4 * D_ATTN + N_ATTN_HEADS + D_LRU]

kernel_name = "hymba_fox_rglru_sandwich_ple"


def rmsnorm(x, g):
    xf = x.astype(jnp.float32)
    y = xf * lax.rsqrt(jnp.mean(xf * xf, axis=-1, keepdims=True) + RMS_EPS) * g.astype(jnp.float32)
    return y.astype(x.dtype)


def forgetting_attention(q, k, v, f_logit):
    B, S, H, Dh = q.shape
    nblk = S // Q_BLOCK
    scale = HEAD_DIM ** -0.5
    q = q.transpose(0, 2, 1, 3)
    k = k.transpose(0, 2, 1, 3)
    v = v.transpose(0, 2, 1, 3)
    c = jnp.cumsum(jax.nn.log_sigmoid(f_logit.astype(jnp.float32)), axis=1).transpose(0, 2, 1)
    qb = q.reshape(B, H, nblk, Q_BLOCK, Dh).transpose(2, 0, 1, 3, 4)
    cb = c.reshape(B, H, nblk, Q_BLOCK).transpose(2, 0, 1, 3)
    kpos = jnp.arange(S)

    def one_block(args):
        qi, ci, bi = args
        qpos = bi * Q_BLOCK + jnp.arange(Q_BLOCK)
        s = jnp.einsum('bhqd,bhkd->bhqk', qi, k, preferred_element_type=jnp.float32) * scale
        s = s + (ci[:, :, :, None] - c[:, :, None, :])
        s = jnp.where(kpos[None, :] <= qpos[:, None], s, -jnp.inf)
        w = jax.nn.softmax(s, axis=-1)
        return jnp.einsum('bhqk,bhkd->bhqd', w.astype(v.dtype), v)

    ob = lax.map(one_block, (qb, cb, jnp.arange(nblk)))
    return ob.transpose(1, 0, 3, 2, 4).reshape(B, S, H * Dh)


def causal_depthwise_conv(x, w, b):
    y = lax.conv_general_dilated(
        x, w[:, None, :].astype(x.dtype), window_strides=(1,),
        padding=[(CONV_WIDTH - 1, 0)], dimension_numbers=('NWC', 'WIO', 'NWC'),
        feature_group_count=x.shape[-1])
    return y + b


def rg_lru(xc, w_r, b_r, w_i, b_i, lam):
    B, S, _ = xc.shape
    xb = xc.reshape(B, S, N_LRU_BLOCKS, LRU_BLOCK)
    r = jax.nn.sigmoid((jnp.einsum('bsnj,njk->bsnk', xb, w_r).reshape(B, S, D_LRU) + b_r).astype(jnp.float32))
    i = jax.nn.sigmoid((jnp.einsum('bsnj,njk->bsnk', xb, w_i).reshape(B, S, D_LRU) + b_i).astype(jnp.float32))
    log_a = -LRU_C * r * jax.nn.softplus(-lam.astype(jnp.float32))
    a = jnp.exp(log_a)
    u = jnp.sqrt(-jnp.expm1(2.0 * log_a)) * (i * xc.astype(jnp.float32))

    def combine(left, right):
        a_l, b_l = left
        a_r, b_r2 = right
        return a_l * a_r, a_r * b_l + b_r2

    _, h = lax.associative_scan(combine, (a, u), axis=1)
    return h.astype(xc.dtype)


def _fwd_setup_inputs(seed: int = 0) -> dict:
    key = jax.random.key(seed)
    ks = jax.random.split(key, 24)
    f32 = jnp.float32
    nrm = lambda k, shape, s: jax.random.normal(k, shape, f32) * s
    x = jax.random.normal(ks[0], (BATCH, SEQ, D_MODEL), f32)
    p = jax.random.normal(ks[1], (DEPTH, BATCH, SEQ, D_PLE), f32)
    w_in = nrm(ks[2], (DEPTH, D_MODEL, D_IN), D_MODEL ** -0.5)
    b_f = jnp.linspace(1.0, 6.0, N_ATTN_HEADS, dtype=f32)[None, :] + nrm(ks[3], (DEPTH, N_ATTN_HEADS), 0.1)
    pre_gain = 1.0 + nrm(ks[4], (DEPTH, D_MODEL), 0.05)
    post_gain = 1.0 + nrm(ks[5], (DEPTH, D_MODEL), 0.05)
    conv_w = nrm(ks[6], (DEPTH, CONV_WIDTH, D_LRU), CONV_WIDTH ** -0.5)
    conv_b = nrm(ks[7], (DEPTH, D_LRU), 0.01)
    w_rgate = nrm(ks[8], (DEPTH, N_LRU_BLOCKS, LRU_BLOCK, LRU_BLOCK), LRU_BLOCK ** -0.5)
    b_rgate = nrm(ks[9], (DEPTH, D_LRU), 0.01)
    w_igate = nrm(ks[10], (DEPTH, N_LRU_BLOCKS, LRU_BLOCK, LRU_BLOCK), LRU_BLOCK ** -0.5)
    b_igate = nrm(ks[11], (DEPTH, D_LRU), 0.01)
    a_pow = jax.random.uniform(ks[12], (DEPTH, D_LRU), f32, minval=0.9, maxval=0.999)
    a0 = a_pow ** (1.0 / LRU_C)
    lru_lambda = jnp.log(a0) - jnp.log1p(-a0)
    attn_out_gain = 1.0 + nrm(ks[13], (DEPTH, D_ATTN), 0.05)
    lru_out_gain = 1.0 + nrm(ks[14], (DEPTH, D_LRU), 0.05)
    w_out = nrm(ks[15], (DEPTH, D_MIX, D_MODEL), D_MIX ** -0.5)
    w_ple = nrm(ks[16], (DEPTH, D_PLE, D_MODEL), D_PLE ** -0.5)
    ple_gain = 1.0 + nrm(ks[17], (DEPTH, D_MODEL), 0.05)
    w_ple_gate = nrm(ks[18], (DEPTH, D_MODEL, D_MODEL), D_MODEL ** -0.5)
    b_ple_gate = nrm(ks[19], (DEPTH, D_MODEL), 0.01)
    return {"x": x, "p": p, "w_in": w_in, "b_f": b_f, "pre_gain": pre_gain,
            "post_gain": post_gain, "conv_w": conv_w, "conv_b": conv_b,
            "w_rgate": w_rgate, "b_rgate": b_rgate, "w_igate": w_igate, "b_igate": b_igate,
            "lru_lambda": lru_lambda, "attn_out_gain": attn_out_gain, "lru_out_gain": lru_out_gain,
            "w_out": w_out, "w_ple": w_ple, "ple_gain": ple_gain,
            "w_ple_gate": w_ple_gate, "b_ple_gate": b_ple_gate}


def _fwd_reference(x, p, w_in, b_f, pre_gain, post_gain, conv_w, conv_b, w_rgate, b_rgate,
              w_igate, b_igate, lru_lambda, attn_out_gain, lru_out_gain, w_out,
              w_ple, ple_gain, w_ple_gate, b_ple_gate):
    B, S, _ = x.shape
    h = x
    for i in range(DEPTH):
        xn = rmsnorm(h, pre_gain[i])
        z = xn @ w_in[i]
        q, k, v, fl, g_attn, x_lru, g_lru = jnp.split(z, SPLIT_POINTS, axis=-1)
        fl = fl + b_f[i]
        o_attn = forgetting_attention(q.reshape(B, S, N_ATTN_HEADS, HEAD_DIM),
                                      k.reshape(B, S, N_ATTN_HEADS, HEAD_DIM),
                                      v.reshape(B, S, N_ATTN_HEADS, HEAD_DIM), fl)
        y_attn = rmsnorm(o_attn, attn_out_gain[i]) * jax.nn.silu(g_attn)
        xc = causal_depthwise_conv(x_lru, conv_w[i], conv_b[i])
        o_lru = rg_lru(xc, w_rgate[i], b_rgate[i], w_igate[i], b_igate[i], lru_lambda[i])
        y_lru = rmsnorm(o_lru, lru_out_gain[i]) * jax.nn.silu(g_lru)
        mix = jnp.concatenate([y_attn, y_lru], axis=-1) @ w_out[i]
        h = h + rmsnorm(mix, post_gain[i])
        e = rmsnorm(p[i] @ w_ple[i], ple_gain[i])
        gate = jax.nn.sigmoid(h @ w_ple_gate[i] + b_ple_gate[i])
        h = h + gate * e
    return h


import jax as _jax
import jax.numpy as _jnp

TWIN_FORMAT = 'train_step'
FWD_PARAMS = ['x', 'p', 'w_in', 'b_f', 'pre_gain', 'post_gain', 'conv_w', 'conv_b', 'w_rgate', 'b_rgate', 'w_igate', 'b_igate', 'lru_lambda', 'attn_out_gain', 'lru_out_gain', 'w_out', 'w_ple', 'ple_gain', 'w_ple_gate', 'b_ple_gate']
TWIN_WEIGHTS = ['w_in', 'b_f', 'pre_gain', 'post_gain', 'conv_w', 'conv_b', 'w_rgate', 'b_rgate', 'w_igate', 'b_igate', 'lru_lambda', 'attn_out_gain', 'lru_out_gain', 'w_out', 'w_ple', 'ple_gain', 'w_ple_gate', 'b_ple_gate']
TWIN_DIFF_INPUT = 'x'
TWIN_INPUTS = ['x', 'p', 'w_in', 'b_f', 'pre_gain', 'post_gain', 'conv_w', 'conv_b', 'w_rgate', 'b_rgate', 'w_igate', 'b_igate', 'lru_lambda', 'attn_out_gain', 'lru_out_gain', 'w_out', 'w_ple', 'ple_gain', 'w_ple_gate', 'b_ple_gate', 'loss_target', 'm_w_in', 'm_b_f', 'm_pre_gain', 'm_post_gain', 'm_conv_w', 'm_conv_b', 'm_w_rgate', 'm_b_rgate', 'm_w_igate', 'm_b_igate', 'm_lru_lambda', 'm_attn_out_gain', 'm_lru_out_gain', 'm_w_out', 'm_w_ple', 'm_ple_gain', 'm_w_ple_gate', 'm_b_ple_gate', 'v_w_in', 'v_b_f', 'v_pre_gain', 'v_post_gain', 'v_conv_w', 'v_conv_b', 'v_w_rgate', 'v_b_rgate', 'v_w_igate', 'v_b_igate', 'v_lru_lambda', 'v_attn_out_gain', 'v_lru_out_gain', 'v_w_out', 'v_w_ple', 'v_ple_gain', 'v_w_ple_gate', 'v_b_ple_gate']
TWIN_OUTPUTS = ['loss', 'grad_x', 'grad_w_in', 'grad_b_f', 'grad_pre_gain', 'grad_post_gain', 'grad_conv_w', 'grad_conv_b', 'grad_w_rgate', 'grad_b_rgate', 'grad_w_igate', 'grad_b_igate', 'grad_lru_lambda', 'grad_attn_out_gain', 'grad_lru_out_gain', 'grad_w_out', 'grad_w_ple', 'grad_ple_gain', 'grad_w_ple_gate', 'grad_b_ple_gate', 'delta_w_in', 'delta_b_f', 'delta_pre_gain', 'delta_post_gain', 'delta_conv_w', 'delta_conv_b', 'delta_w_rgate', 'delta_b_rgate', 'delta_w_igate', 'delta_b_igate', 'delta_lru_lambda', 'delta_attn_out_gain', 'delta_lru_out_gain', 'delta_w_out', 'delta_w_ple', 'delta_ple_gain', 'delta_w_ple_gate', 'delta_b_ple_gate', 'new_m_w_in', 'new_m_b_f', 'new_m_pre_gain', 'new_m_post_gain', 'new_m_conv_w', 'new_m_conv_b', 'new_m_w_rgate', 'new_m_b_rgate', 'new_m_w_igate', 'new_m_b_igate', 'new_m_lru_lambda', 'new_m_attn_out_gain', 'new_m_lru_out_gain', 'new_m_w_out', 'new_m_w_ple', 'new_m_ple_gain', 'new_m_w_ple_gate', 'new_m_b_ple_gate', 'new_v_w_in', 'new_v_b_f', 'new_v_pre_gain', 'new_v_post_gain', 'new_v_conv_w', 'new_v_conv_b', 'new_v_w_rgate', 'new_v_b_rgate', 'new_v_w_igate', 'new_v_b_igate', 'new_v_lru_lambda', 'new_v_attn_out_gain', 'new_v_lru_out_gain', 'new_v_w_out', 'new_v_w_ple', 'new_v_ple_gain', 'new_v_w_ple_gate', 'new_v_b_ple_gate']
TWIN_LEAF_KINDS = {'loss': 'loss', 'grad_x': 'grad_x', 'grad_w_in': 'grad_w', 'grad_b_f': 'grad_w', 'grad_pre_gain': 'grad_w', 'grad_post_gain': 'grad_w', 'grad_conv_w': 'grad_w', 'grad_conv_b': 'grad_w', 'grad_w_rgate': 'grad_w', 'grad_b_rgate': 'grad_w', 'grad_w_igate': 'grad_w', 'grad_b_igate': 'grad_w', 'grad_lru_lambda': 'grad_w', 'grad_attn_out_gain': 'grad_w', 'grad_lru_out_gain': 'grad_w', 'grad_w_out': 'grad_w', 'grad_w_ple': 'grad_w', 'grad_ple_gain': 'grad_w', 'grad_w_ple_gate': 'grad_w', 'grad_b_ple_gate': 'grad_w', 'delta_w_in': 'delta_w', 'delta_b_f': 'delta_w', 'delta_pre_gain': 'delta_w', 'delta_post_gain': 'delta_w', 'delta_conv_w': 'delta_w', 'delta_conv_b': 'delta_w', 'delta_w_rgate': 'delta_w', 'delta_b_rgate': 'delta_w', 'delta_w_igate': 'delta_w', 'delta_b_igate': 'delta_w', 'delta_lru_lambda': 'delta_w', 'delta_attn_out_gain': 'delta_w', 'delta_lru_out_gain': 'delta_w', 'delta_w_out': 'delta_w', 'delta_w_ple': 'delta_w', 'delta_ple_gain': 'delta_w', 'delta_w_ple_gate': 'delta_w', 'delta_b_ple_gate': 'delta_w', 'new_m_w_in': 'new_m', 'new_m_b_f': 'new_m', 'new_m_pre_gain': 'new_m', 'new_m_post_gain': 'new_m', 'new_m_conv_w': 'new_m', 'new_m_conv_b': 'new_m', 'new_m_w_rgate': 'new_m', 'new_m_b_rgate': 'new_m', 'new_m_w_igate': 'new_m', 'new_m_b_igate': 'new_m', 'new_m_lru_lambda': 'new_m', 'new_m_attn_out_gain': 'new_m', 'new_m_lru_out_gain': 'new_m', 'new_m_w_out': 'new_m', 'new_m_w_ple': 'new_m', 'new_m_ple_gain': 'new_m', 'new_m_w_ple_gate': 'new_m', 'new_m_b_ple_gate': 'new_m', 'new_v_w_in': 'new_v', 'new_v_b_f': 'new_v', 'new_v_pre_gain': 'new_v', 'new_v_post_gain': 'new_v', 'new_v_conv_w': 'new_v', 'new_v_conv_b': 'new_v', 'new_v_w_rgate': 'new_v', 'new_v_b_rgate': 'new_v', 'new_v_w_igate': 'new_v', 'new_v_b_igate': 'new_v', 'new_v_lru_lambda': 'new_v', 'new_v_attn_out_gain': 'new_v', 'new_v_lru_out_gain': 'new_v', 'new_v_w_out': 'new_v', 'new_v_w_ple': 'new_v', 'new_v_ple_gain': 'new_v', 'new_v_w_ple_gate': 'new_v', 'new_v_b_ple_gate': 'new_v'}


def _forward(args):
    return _fwd_reference(*[args[k] for k in FWD_PARAMS])


def _output_shape():
    out = _jax.eval_shape(lambda: _forward(_fwd_setup_inputs(0)))
    return out.shape, out.dtype

N_MICROBATCH = 1
ADAM_LR = 0.001
ADAM_B1 = 0.9
ADAM_B2 = 0.999
ADAM_EPS = 1e-08
ADAM_WD = 0.01
ADAM_STEP = 10
PER_EXAMPLE_BATCH_AXIS = {'x': 0, 'p': 1, 'loss_target': 0}
SHARED_INPUTS = []
_WEIGHT_DTYPES = {'w_in': _jnp.float32, 'b_f': _jnp.float32, 'pre_gain': _jnp.float32, 'post_gain': _jnp.float32, 'conv_w': _jnp.float32, 'conv_b': _jnp.float32, 'w_rgate': _jnp.float32, 'b_rgate': _jnp.float32, 'w_igate': _jnp.float32, 'b_igate': _jnp.float32, 'lru_lambda': _jnp.float32, 'attn_out_gain': _jnp.float32, 'lru_out_gain': _jnp.float32, 'w_out': _jnp.float32, 'w_ple': _jnp.float32, 'ple_gain': _jnp.float32, 'w_ple_gate': _jnp.float32, 'b_ple_gate': _jnp.float32}
MOMENT_SCALE = {'w_in': 2.013872e-01, 'b_f': 1.053560e+00, 'pre_gain': 5.335581e-01, 'post_gain': 3.269542e+01, 'conv_w': 3.044744e-01, 'conv_b': 9.646098e+00, 'w_rgate': 1.681577e-01, 'b_rgate': 8.949160e-02, 'w_igate': 3.122116e-01, 'b_igate': 8.741782e-02, 'lru_lambda': 1.432622e-01, 'attn_out_gain': 2.304927e-01, 'lru_out_gain': 3.378312e-01, 'w_out': 3.830251e-01, 'w_ple': 1.853396e-01, 'ple_gain': 1.029163e+01, 'w_ple_gate': 2.382437e-01, 'b_ple_gate': 2.936514e+00}


def _to_microbatches(a, axis):
    t = _jnp.moveaxis(a, axis, 0)
    t = t.reshape((N_MICROBATCH, t.shape[0] // N_MICROBATCH) + t.shape[1:])
    return _jnp.moveaxis(t, 1, axis + 1)


def setup_inputs(seed: int = 0) -> dict:
    inp = _fwd_setup_inputs(seed)
    key = _jax.random.fold_in(_jax.random.key(seed), 7919)
    shape, _ = _output_shape()
    out = dict(inp)
    out["loss_target"] = _jax.random.normal(_jax.random.fold_in(key, 0), shape, _jnp.float32)
    for i, name in enumerate(TWIN_WEIGHTS):
        w = inp[name].astype(_jnp.float32)
        if MOMENT_SCALE is None:
            s = _jnp.sqrt(_jnp.mean(_jnp.square(w)) + 1e-30)
        else:
            s = MOMENT_SCALE[name]
        km, kv = _jax.random.split(_jax.random.fold_in(key, i + 1))
        out[name] = w
        out["m_" + name] = s * _jax.random.normal(km, w.shape, _jnp.float32)
        out["v_" + name] = (s * s) * _jax.random.uniform(kv, w.shape, _jnp.float32, 0.5, 1.5)
    if N_MICROBATCH > 1:
        for name, axis in PER_EXAMPLE_BATCH_AXIS.items():
            out[name] = _to_microbatches(out[name], axis)
    return {'x': out['x'], 'p': out['p'], 'w_in': out['w_in'], 'b_f': out['b_f'], 'pre_gain': out['pre_gain'], 'post_gain': out['post_gain'], 'conv_w': out['conv_w'], 'conv_b': out['conv_b'], 'w_rgate': out['w_rgate'], 'b_rgate': out['b_rgate'], 'w_igate': out['w_igate'], 'b_igate': out['b_igate'], 'lru_lambda': out['lru_lambda'], 'attn_out_gain': out['attn_out_gain'], 'lru_out_gain': out['lru_out_gain'], 'w_out': out['w_out'], 'w_ple': out['w_ple'], 'ple_gain': out['ple_gain'], 'w_ple_gate': out['w_ple_gate'], 'b_ple_gate': out['b_ple_gate'], 'loss_target': out['loss_target'], 'm_w_in': out['m_w_in'], 'm_b_f': out['m_b_f'], 'm_pre_gain': out['m_pre_gain'], 'm_post_gain': out['m_post_gain'], 'm_conv_w': out['m_conv_w'], 'm_conv_b': out['m_conv_b'], 'm_w_rgate': out['m_w_rgate'], 'm_b_rgate': out['m_b_rgate'], 'm_w_igate': out['m_w_igate'], 'm_b_igate': out['m_b_igate'], 'm_lru_lambda': out['m_lru_lambda'], 'm_attn_out_gain': out['m_attn_out_gain'], 'm_lru_out_gain': out['m_lru_out_gain'], 'm_w_out': out['m_w_out'], 'm_w_ple': out['m_w_ple'], 'm_ple_gain': out['m_ple_gain'], 'm_w_ple_gate': out['m_w_ple_gate'], 'm_b_ple_gate': out['m_b_ple_gate'], 'v_w_in': out['v_w_in'], 'v_b_f': out['v_b_f'], 'v_pre_gain': out['v_pre_gain'], 'v_post_gain': out['v_post_gain'], 'v_conv_w': out['v_conv_w'], 'v_conv_b': out['v_conv_b'], 'v_w_rgate': out['v_w_rgate'], 'v_b_rgate': out['v_b_rgate'], 'v_w_igate': out['v_w_igate'], 'v_b_igate': out['v_b_igate'], 'v_lru_lambda': out['v_lru_lambda'], 'v_attn_out_gain': out['v_attn_out_gain'], 'v_lru_out_gain': out['v_lru_out_gain'], 'v_w_out': out['v_w_out'], 'v_w_ple': out['v_w_ple'], 'v_ple_gain': out['v_ple_gain'], 'v_w_ple_gate': out['v_w_ple_gate'], 'v_b_ple_gate': out['v_b_ple_gate']}


def _loss(weights, diff, rest, loss_target):
    with _jax.named_scope("forward"):
        args = {**rest, TWIN_DIFF_INPUT: diff, **{k: w.astype(_WEIGHT_DTYPES[k]) for k, w in weights.items()}}
        y = _forward(args)
    with _jax.named_scope("loss_head"):
        err = _jnp.square(y.astype(_jnp.float32) - loss_target)
        return 0.5 * _jnp.sum(_jnp.mean(err, axis=-1)) if err.ndim else 0.5 * err


def _adamw(w, g, m, v):
    m = ADAM_B1 * m + (1.0 - ADAM_B1) * g
    v = ADAM_B2 * v + (1.0 - ADAM_B2) * _jnp.square(g)
    m_hat = m / (1.0 - ADAM_B1 ** ADAM_STEP)
    v_hat = v / (1.0 - ADAM_B2 ** ADAM_STEP)
    delta = -ADAM_LR * (m_hat / (_jnp.sqrt(v_hat) + ADAM_EPS) + ADAM_WD * w)
    return delta, m, v


def reference(x, p, w_in, b_f, pre_gain, post_gain, conv_w, conv_b, w_rgate, b_rgate, w_igate, b_igate, lru_lambda, attn_out_gain, lru_out_gain, w_out, w_ple, ple_gain, w_ple_gate, b_ple_gate, loss_target, m_w_in, m_b_f, m_pre_gain, m_post_gain, m_conv_w, m_conv_b, m_w_rgate, m_b_rgate, m_w_igate, m_b_igate, m_lru_lambda, m_attn_out_gain, m_lru_out_gain, m_w_out, m_w_ple, m_ple_gain, m_w_ple_gate, m_b_ple_gate, v_w_in, v_b_f, v_pre_gain, v_post_gain, v_conv_w, v_conv_b, v_w_rgate, v_b_rgate, v_w_igate, v_b_igate, v_lru_lambda, v_attn_out_gain, v_lru_out_gain, v_w_out, v_w_ple, v_ple_gain, v_w_ple_gate, v_b_ple_gate):
    given = dict(x=x, p=p, w_in=w_in, b_f=b_f, pre_gain=pre_gain, post_gain=post_gain, conv_w=conv_w, conv_b=conv_b, w_rgate=w_rgate, b_rgate=b_rgate, w_igate=w_igate, b_igate=b_igate, lru_lambda=lru_lambda, attn_out_gain=attn_out_gain, lru_out_gain=lru_out_gain, w_out=w_out, w_ple=w_ple, ple_gain=ple_gain, w_ple_gate=w_ple_gate, b_ple_gate=b_ple_gate, loss_target=loss_target, m_w_in=m_w_in, m_b_f=m_b_f, m_pre_gain=m_pre_gain, m_post_gain=m_post_gain, m_conv_w=m_conv_w, m_conv_b=m_conv_b, m_w_rgate=m_w_rgate, m_b_rgate=m_b_rgate, m_w_igate=m_w_igate, m_b_igate=m_b_igate, m_lru_lambda=m_lru_lambda, m_attn_out_gain=m_attn_out_gain, m_lru_out_gain=m_lru_out_gain, m_w_out=m_w_out, m_w_ple=m_w_ple, m_ple_gain=m_ple_gain, m_w_ple_gate=m_w_ple_gate, m_b_ple_gate=m_b_ple_gate, v_w_in=v_w_in, v_b_f=v_b_f, v_pre_gain=v_pre_gain, v_post_gain=v_post_gain, v_conv_w=v_conv_w, v_conv_b=v_conv_b, v_w_rgate=v_w_rgate, v_b_rgate=v_b_rgate, v_w_igate=v_w_igate, v_b_igate=v_b_igate, v_lru_lambda=v_lru_lambda, v_attn_out_gain=v_attn_out_gain, v_lru_out_gain=v_lru_out_gain, v_w_out=v_w_out, v_w_ple=v_w_ple, v_ple_gain=v_ple_gain, v_w_ple_gate=v_w_ple_gate, v_b_ple_gate=v_b_ple_gate)
    weights = {n: given[n] for n in TWIN_WEIGHTS}
    shared = {n: given[n] for n in SHARED_INPUTS}
    per_example = {n: given[n] for n in ['x', 'p']}
    grad_fn = _jax.value_and_grad(_loss, argnums=(0, 1))

    def one_microbatch(ex, loss_target):
        ex = dict(ex)
        diff = ex.pop(TWIN_DIFF_INPUT)
        return grad_fn(weights, diff, {**shared, **ex}, loss_target)

    if N_MICROBATCH == 1:
        loss, (grad_w, grad_x) = one_microbatch(per_example, given["loss_target"])
    else:
        def body(carry, xs):
            loss_sum, grad_sum = carry
            l_k, (gw_k, gx_k) = one_microbatch(xs[0], xs[1])
            with _jax.named_scope("update"):
                return (loss_sum + l_k, _jax.tree.map(_jnp.add, grad_sum, gw_k)), gx_k

        init = (_jnp.zeros((), _jnp.float32), _jax.tree.map(_jnp.zeros_like, weights))
        (loss, grad_w), grad_x = _jax.lax.scan(body, init, (per_example, given["loss_target"]))
    with _jax.named_scope("update"):
        delta_w, new_m, new_v = {}, {}, {}
        for n in TWIN_WEIGHTS:
            delta_w[n], new_m[n], new_v[n] = _adamw(weights[n], grad_w[n], given["m_" + n], given["v_" + n])
    return (loss, grad_x, *[grad_w[n] for n in TWIN_WEIGHTS], *[delta_w[n] for n in TWIN_WEIGHTS],
            *[new_m[n] for n in TWIN_WEIGHTS], *[new_v[n] for n in TWIN_WEIGHTS])
```

```python
import functools

import jax
import jax.numpy as jnp
from jax import lax
from jax.experimental import pallas as pl
from jax.experimental.pallas import tpu as pltpu

F32 = jnp.float32
BF16 = jnp.bfloat16

D = 1024
H = 8
DH = 128
NB = 8
DPLE = 256
DMIX = 2 * D
D_IN = 4 * D + H + 2 * D
FL0 = 3 * D
RMS_EPS = 1e-6
LRU_C = 8.0
NEG = -1e30
LANES = 128
SUBLANES = 8

ADAM_LR = 0.001
ADAM_B1 = 0.9
ADAM_B2 = 0.999
ADAM_EPS = 1e-08
ADAM_WD = 0.01
ADAM_STEP = 10

TM = 256
TA = 512
VMEM_BIG = 56 * 1024 * 1024
VMEM_MID = 40 * 1024 * 1024

MESH = pl.DeviceIdType.MESH
N_CHIPS = 4
N_DEV = 8


def _cparams(sem, vmem=VMEM_MID):
    return pltpu.CompilerParams(dimension_semantics=sem, vmem_limit_bytes=vmem)


def _sigmoid(x):
    return 1.0 / (1.0 + jnp.exp(-x))


def _rstd(x):
    return lax.rsqrt(jnp.mean(x * x, axis=-1, keepdims=True) + RMS_EPS)


def _rms_bwd(t, xhat, rstd):
    return rstd * (t - xhat * jnp.mean(t * xhat, axis=-1, keepdims=True))


def _dot(a, b):
    return jnp.dot(a, b, preferred_element_type=F32)


def _dot_nt(a, b):
    return lax.dot_general(a, b, (((1,), (1,)), ((), ())), preferred_element_type=F32)


def _dot_tn(a, b):
    return lax.dot_general(a, b, (((0,), (0,)), ((), ())), preferred_element_type=F32)


def _dot_exact(a, b):
    return jnp.dot(a, b, preferred_element_type=F32, precision=lax.Precision.HIGHEST)


def _neg_expm1(x):
    series = x * (1.0 + x * 0.5 * (1.0 + x * (1.0 / 3.0) * (1.0 + x * 0.25 * (1.0 + x * 0.2 * (1.0 + x * (1.0 / 6.0))))))
    return -jnp.where(x > -0.25, series, jnp.exp(x) - 1.0)


def _shift_down(x, j, halo):
    rolled = pltpu.roll(x, j, 0)
    row = lax.broadcasted_iota(jnp.int32, halo.shape, 0)
    top = jnp.where(row < j, pltpu.roll(halo, j, 0), rolled[:SUBLANES])
    return jnp.concatenate([top, rolled[SUBLANES:]], axis=0)


def _shift_up(x, j, nxt):
    tm = x.shape[0]
    rolled = pltpu.roll(x, tm - j, 0)
    row = lax.broadcasted_iota(jnp.int32, nxt.shape, 0)
    bot = jnp.where(row >= SUBLANES - j, pltpu.roll(nxt, SUBLANES - j, 0), rolled[tm - SUBLANES:])
    return jnp.concatenate([rolled[:tm - SUBLANES], bot], axis=0)


def _scan_fwd(a, u):
    tm = a.shape[0]
    row = lax.broadcasted_iota(jnp.int32, a.shape, 0)
    d = 1
    while d < tm:
        keep = row >= d
        a_s = jnp.where(keep, pltpu.roll(a, d, 0), 1.0)
        u_s = jnp.where(keep, pltpu.roll(u, d, 0), 0.0)
        u = u + a * u_s
        a = a * a_s
        d *= 2
    return a, u


def _scan_bwd(b, u):
    tm = b.shape[0]
    row = lax.broadcasted_iota(jnp.int32, b.shape, 0)
    d = 1
    while d < tm:
        keep = row < tm - d
        b_s = jnp.where(keep, pltpu.roll(b, tm - d, 0), 1.0)
        u_s = jnp.where(keep, pltpu.roll(u, tm - d, 0), 0.0)
        u = u + b * u_s
        b = b * b_s
        d *= 2
    return u


def _gate_pre(xc, w_ref):
    outs = []
    for n in range(NB):
        outs.append(_dot(xc[:, n * LANES:(n + 1) * LANES].astype(BF16), w_ref[n]))
    return jnp.concatenate(outs, axis=1)


def _gate_pre_t(d, w_ref):
    outs = []
    for n in range(NB):
        outs.append(_dot_nt(d[:, n * LANES:(n + 1) * LANES].astype(BF16), w_ref[n]))
    return jnp.concatenate(outs, axis=1)


def _softplus_neg(lam):
    return jnp.maximum(-lam, 0.0) + jnp.log(1.0 + jnp.exp(-jnp.abs(lam)))


def _row_spec(tm, width):
    return pl.BlockSpec((tm, width), lambda i: (i, 0))


def _const_spec(shape):
    nd = len(shape)
    return pl.BlockSpec(shape, lambda *_: (0,) * nd)


def _in_proj(x, pre_gain, w_main, w_f, b_f_pad):
    T = x.shape[0]
    tm = TM

    def body(x_ref, g_ref, wm_ref, wf_ref, bf_ref,
             xn_ref, q_ref, k_ref, v_ref, ga_ref, xl_ref, gl_ref, flb_ref, c_ref, carry):
        @pl.when(pl.program_id(0) == 0)
        def _():
            carry[...] = jnp.zeros_like(carry)

        xv = x_ref[...]
        xn = (xv * _rstd(xv) * g_ref[...]).astype(BF16)
        xn_ref[...] = xn
        for s, o_ref in enumerate((q_ref, k_ref, v_ref, ga_ref, xl_ref, gl_ref)):
            o_ref[...] = _dot(xn, wm_ref[:, s * D:(s + 1) * D]).astype(o_ref.dtype)
        flb = _dot(xn, wf_ref[...]) + bf_ref[...]
        flb_ref[...] = flb
        lane = lax.broadcasted_iota(jnp.int32, flb.shape, 1)
        ls = jnp.where(lane < H, jnp.minimum(flb, 0.0) - jnp.log(1.0 + jnp.exp(-jnp.abs(flb))), 0.0)
        r = lax.broadcasted_iota(jnp.int32, (tm, tm), 0)
        c = lax.broadcasted_iota(jnp.int32, (tm, tm), 1)
        cs = _dot_exact((c <= r).astype(F32), ls) + carry[...]
        c_ref[...] = cs
        carry[...] = c_ref[tm - 1:tm, :]

    bf = jax.ShapeDtypeStruct((T, D), BF16)
    f32 = jax.ShapeDtypeStruct((T, D), F32)
    nar = jax.ShapeDtypeStruct((T, LANES), F32)
    return pl.pallas_call(
        body, name="in_proj", grid=(T // tm,),
        in_specs=[_row_spec(tm, D), _const_spec((1, D)), _const_spec((D, 6 * D)), _const_spec((D, LANES)),
                  _const_spec((1, LANES))],
        out_specs=[_row_spec(tm, D)] * 7 + [_row_spec(tm, LANES)] * 2,
        out_shape=[bf, bf, bf, bf, f32, f32, f32, nar, nar],
        scratch_shapes=[pltpu.VMEM((1, LANES), F32)],
        compiler_params=_cparams(("arbitrary",), VMEM_BIG),
    )(x, pre_gain, w_main, w_f, b_f_pad)


def _attn_fwd(q, k, v, c_col, c_row):
    T = q.shape[0]
    t = TA
    n = T // t
    scale = DH ** -0.5

    def body(q_ref, k_ref, v_ref, cq_ref, ck_ref, o_ref, bq_ref, m_s, l_s, acc_s):
        qi = pl.program_id(1)
        ki = pl.program_id(2)

        @pl.when(ki == 0)
        def _():
            m_s[...] = jnp.full(m_s.shape, NEG, F32)
            l_s[...] = jnp.zeros_like(l_s)
            acc_s[...] = jnp.zeros_like(acc_s)

        @pl.when(ki <= qi)
        def _():
            s = _dot_nt(q_ref[...], k_ref[...]) * scale + (cq_ref[0] - ck_ref[0])
            row = qi * t + lax.broadcasted_iota(jnp.int32, (t, t), 0)
            col = ki * t + lax.broadcasted_iota(jnp.int32, (t, t), 1)
            s = jnp.where(col <= row, s, NEG)
            m_prev = m_s[...]
            m_new = jnp.maximum(m_prev, jnp.max(s, axis=1, keepdims=True))
            alpha = jnp.exp(m_prev - m_new)
            pr = jnp.exp(s - m_new)
            l_s[...] = alpha * l_s[...] + jnp.sum(pr, axis=1, keepdims=True)
            acc_s[...] = alpha * acc_s[...] + _dot(pr.astype(BF16), v_ref[...])
            m_s[...] = m_new

        @pl.when(ki == n - 1)
        def _():
            l = l_s[...]
            o_ref[...] = acc_s[...] / l
            bq_ref[0] = cq_ref[0] - (m_s[...] + jnp.log(l))

    q_spec = pl.BlockSpec((t, DH), lambda h, qi, ki: (qi, h))
    kv_spec = pl.BlockSpec((t, DH), lambda h, qi, ki: (jnp.minimum(ki, qi), h))
    return pl.pallas_call(
        body, name="attn_fwd", grid=(H, n, n),
        in_specs=[q_spec, kv_spec, kv_spec,
                  pl.BlockSpec((1, t, 1), lambda h, qi, ki: (h, qi, 0)),
                  pl.BlockSpec((1, 1, t), lambda h, qi, ki: (h, 0, jnp.minimum(ki, qi)))],
        out_specs=[q_spec, pl.BlockSpec((1, t, 1), lambda h, qi, ki: (h, qi, 0))],
        out_shape=[jax.ShapeDtypeStruct((T, D), F32), jax.ShapeDtypeStruct((H, T, 1), F32)],
        scratch_shapes=[pltpu.VMEM((t, 1), F32), pltpu.VMEM((t, 1), F32), pltpu.VMEM((t, DH), F32)],
        compiler_params=_cparams(("parallel", "parallel", "arbitrary")),
    )(q, k, v, c_col, c_row)


def _lru_gates(xc, wr_ref, br_ref, wi_ref, bi_ref, lam_ref):
    r = _sigmoid(_gate_pre(xc, wr_ref) + br_ref[...])
    ig = _sigmoid(_gate_pre(xc, wi_ref) + bi_ref[...])
    sp = _softplus_neg(lam_ref[...])
    la = (-LRU_C) * r * sp
    a = jnp.exp(la)
    sq = jnp.sqrt(_neg_expm1(2.0 * la))
    return r, ig, sp, a, sq


def _branches_fwd(o, g_attn, x_lru, g_lru, gain_a, gain_l, conv_w, conv_b, w_r, b_r, w_i, b_i, lam):
    T = o.shape[0]
    tm = TM

    def body(o_ref, ga_ref, xl_ref, gl_ref, gna_ref, gnl_ref, cw_ref, cb_ref, wr_ref, br_ref, wi_ref, bi_ref,
             lam_ref, ycat_ref, xc_ref, h_ref, halo_s, hc_s):
        @pl.when(pl.program_id(0) == 0)
        def _():
            halo_s[...] = jnp.zeros_like(halo_s)
            hc_s[...] = jnp.zeros_like(hc_s)

        ov = o_ref[...]
        ga = ga_ref[...]
        ya = ov * _rstd(ov) * gna_ref[...] * (ga * _sigmoid(ga))
        ycat_ref[:, :D] = ya.astype(BF16)

        xl = xl_ref[...]
        halo = halo_s[...]
        xc = xl * cw_ref[3:4, :] + cb_ref[...]
        for j in range(3):
            xc = xc + _shift_down(xl, 3 - j, halo) * cw_ref[j:j + 1, :]
        halo_s[...] = xl_ref[tm - SUBLANES:tm, :]
        xc_ref[...] = xc

        _, ig, _, a, sq = _lru_gates(xc, wr_ref, br_ref, wi_ref, bi_ref, lam_ref)
        u = sq * (ig * xc)
        a_cum, h_loc = _scan_fwd(a, u)
        hh = h_loc + a_cum * hc_s[...]
        h_ref[...] = hh
        hc_s[...] = h_ref[tm - 1:tm, :]

        gl = gl_ref[...]
        yl = hh * _rstd(hh) * gnl_ref[...] * (gl * _sigmoid(gl))
        ycat_ref[:, D:] = yl.astype(BF16)

    vec = _const_spec((1, D))
    wspec = _const_spec((NB, LANES, LANES))
    return pl.pallas_call(
        body, name="branches_fwd", grid=(T // tm,),
        in_specs=[_row_spec(tm, D)] * 4 + [vec, vec, _const_spec((4, D)), vec, wspec, vec, wspec, vec, vec],
        out_specs=[_row_spec(tm, DMIX), _row_spec(tm, D), _row_spec(tm, D)],
        out_shape=[jax.ShapeDtypeStruct((T, DMIX), BF16), jax.ShapeDtypeStruct((T, D), F32),
                   jax.ShapeDtypeStruct((T, D), F32)],
        scratch_shapes=[pltpu.VMEM((SUBLANES, D), F32), pltpu.VMEM((1, D), F32)],
        compiler_params=_cparams(("arbitrary",)),
    )(o, g_attn, x_lru, g_lru, gain_a, gain_l, conv_w, conv_b, w_r, b_r, w_i, b_i, lam)


def _tail(ycat, x, p, tgt, w_out, post_gain, w_ple, ple_gain, w_gate, b_gate):
    T = x.shape[0]
    tm = TM

    def body(ycat_ref, x_ref, p_ref, t_ref, wo_ref, pg_ref, wp_ref, eg_ref, wg_ref, bg_ref,
             dh1_ref, dycat_ref, dmix_ref, h1b_ref, dgp_ref, pb_ref, dpe_ref, acc_ref):
        @pl.when(pl.program_id(0) == 0)
        def _():
            acc_ref[...] = jnp.zeros_like(acc_ref)

        mix = _dot(ycat_ref[...], wo_ref[...])
        rstd_m = _rstd(mix)
        mhat = mix * rstd_m
        h1 = x_ref[...] + mhat * pg_ref[...]
        pb = p_ref[...].astype(BF16)
        pb_ref[...] = pb
        pe = _dot(pb, wp_ref[...])
        rstd_p = _rstd(pe)
        pehat = pe * rstd_p
        e = pehat * eg_ref[...]
        h1b = h1.astype(BF16)
        h1b_ref[...] = h1b
        gate = _sigmoid(_dot(h1b, wg_ref[...]) + bg_ref[...])
        diff = (h1 + gate * e) - t_ref[...]

        dy = diff * (1.0 / D)
        de = dy * gate
        dgp = (dy * e) * gate * (1.0 - gate)
        dgpb = dgp.astype(BF16)
        dgp_ref[...] = dgpb
        dh1 = dy + _dot_nt(dgpb, wg_ref[...])
        dh1_ref[...] = dh1
        dpe_ref[...] = _rms_bwd(de * eg_ref[...], pehat, rstd_p).astype(BF16)
        dmix = _rms_bwd(dh1 * pg_ref[...], mhat, rstd_m).astype(BF16)
        dmix_ref[...] = dmix
        dycat_ref[...] = _dot_nt(dmix, wo_ref[...])

        acc_ref[0:1, :] += jnp.sum(dh1 * mhat, axis=0, keepdims=True)
        acc_ref[1:2, :] += jnp.sum(de * pehat, axis=0, keepdims=True)
        acc_ref[2:3, :] += jnp.sum(dgp, axis=0, keepdims=True)
        acc_ref[3:4, :] += jnp.sum(diff * diff, axis=0, keepdims=True) * (0.5 / D)

    vec = _const_spec((1, D))
    bf = jax.ShapeDtypeStruct((T, D), BF16)
    return pl.pallas_call(
        body, name="tail", grid=(T // tm,),
        in_specs=[_row_spec(tm, DMIX), _row_spec(tm, D), _row_spec(tm, DPLE), _row_spec(tm, D),
                  _const_spec((DMIX, D)), vec, _const_spec((DPLE, D)), vec, _const_spec((D, D)), vec],
        out_specs=[_row_spec(tm, D), _row_spec(tm, DMIX), _row_spec(tm, D), _row_spec(tm, D), _row_spec(tm, D),
                   _row_spec(tm, DPLE), _row_spec(tm, D), _const_spec((SUBLANES, D))],
        out_shape=[jax.ShapeDtypeStruct((T, D), F32), jax.ShapeDtypeStruct((T, DMIX), F32), bf, bf, bf,
                   jax.ShapeDtypeStruct((T, DPLE), BF16), bf, jax.ShapeDtypeStruct((SUBLANES, D), F32)],
        compiler_params=_cparams(("arbitrary",), VMEM_BIG),
    )(ycat, x, p, tgt, w_out, post_gain, w_ple, ple_gain, w_gate, b_gate)


def _branches_bwd(dycat, o, g_attn, h, g_lru, gain_a, gain_l):
    T = o.shape[0]
    tm = TM

    def body(dy_ref, o_ref, ga_ref, h_ref, gl_ref, gna_ref, gnl_ref,
             do_ref, dd_ref, dga_ref, dgl_ref, dh_ref, acc_ref):
        @pl.when(pl.program_id(0) == 0)
        def _():
            acc_ref[...] = jnp.zeros_like(acc_ref)

        def branch(val, g, gain, dyv):
            rstd = _rstd(val)
            vhat = val * rstd
            sig = _sigmoid(g)
            dn = dyv * (g * sig)
            dg = dyv * (vhat * gain) * (sig * (1.0 + g * (1.0 - sig)))
            dgain = jnp.sum(dn * vhat, axis=0, keepdims=True)
            return _rms_bwd(dn * gain, vhat, rstd), dg, dgain

        ov = o_ref[...]
        do, dga, dgain_a = branch(ov, ga_ref[...], gna_ref[...], dy_ref[:, :D])
        do_ref[...] = do.astype(BF16)
        dga_ref[...] = dga.astype(BF16)
        prod = do * ov
        lane = lax.broadcasted_iota(jnp.int32, (tm, LANES), 1)
        dd = jnp.zeros((tm, LANES), F32)
        for hd in range(H):
            dd = jnp.where(lane == hd, jnp.sum(prod[:, hd * DH:(hd + 1) * DH], axis=1, keepdims=True), dd)
        dd_ref[...] = dd

        dh, dgl, dgain_l = branch(h_ref[...], gl_ref[...], gnl_ref[...], dy_ref[:, D:])
        dh_ref[...] = dh
        dgl_ref[...] = dgl.astype(BF16)
        acc_ref[0:1, :] += dgain_a
        acc_ref[1:2, :] += dgain_l

    vec = _const_spec((1, D))
    bf = jax.ShapeDtypeStruct((T, D), BF16)
    return pl.pallas_call(
        body, name="branches_bwd", grid=(T // tm,),
        in_specs=[_row_spec(tm, DMIX)] + [_row_spec(tm, D)] * 4 + [vec, vec],
        out_specs=[_row_spec(tm, D), _row_spec(tm, LANES), _row_spec(tm, D), _row_spec(tm, D), _row_spec(tm, D),
                   _const_spec((SUBLANES, D))],
        out_shape=[bf, jax.ShapeDtypeStruct((T, LANES), F32), bf, bf, jax.ShapeDtypeStruct((T, D), F32),
                   jax.ShapeDtypeStruct((SUBLANES, D), F32)],
        compiler_params=_cparams(("arbitrary",)),
    )(dycat, o, g_attn, h, g_lru, gain_a, gain_l)


def _lru_bwd(dh, h, xc, x_lru, conv_w, w_r, b_r, w_i, b_i, lam):
    T = dh.shape[0]
    tm = TM
    nt = T // tm
    per = tm // SUBLANES

    def body(dh_ref, h_ref, hprev_ref, xc_ref, xl_ref, xlprev_ref, cw_ref, wr_ref, br_ref, wi_ref, bi_ref, lam_ref,
             dxl_ref, dpr_ref, dpi_ref, xcb_ref, acc_ref, carry_s, dxc_next_s, top_s):
        i = pl.program_id(0)

        @pl.when(i == 0)
        def _():
            acc_ref[...] = jnp.zeros_like(acc_ref)
            carry_s[...] = jnp.zeros_like(carry_s)
            dxc_next_s[...] = jnp.zeros_like(dxc_next_s)

        inner = jnp.where(i == nt - 1, 0.0, 1.0)
        xc = xc_ref[...]
        xcb_ref[...] = xc.astype(BF16)
        r, ig, sp, a, sq = _lru_gates(xc, wr_ref, br_ref, wi_ref, bi_ref, lam_ref)

        row = lax.broadcasted_iota(jnp.int32, (tm, D), 0)
        u = dh_ref[...] + jnp.where(row == tm - 1, carry_s[...], 0.0)
        dht = _scan_bwd(pltpu.roll(a, tm - 1, 0), u)
        top_s[...] = a[:SUBLANES, :] * dht[:SUBLANES, :]
        carry_s[...] = top_s[0:1, :]

        hprev = hprev_ref[...] * inner
        da = dht * _shift_down(h_ref[...], 1, hprev)
        dig = dht * sq * xc
        dxc = dht * sq * ig
        dsq = dht * ig * xc
        dla = da * a - dsq * (a * a) / sq
        dr = dla * ((-LRU_C) * sp)
        dpr = dr * r * (1.0 - r)
        dpi = dig * ig * (1.0 - ig)
        dpr_ref[...] = dpr.astype(BF16)
        dpi_ref[...] = dpi.astype(BF16)
        dxc = dxc + _gate_pre_t(dpr, wr_ref) + _gate_pre_t(dpi, wi_ref)

        xl = xl_ref[...]
        xlprev = xlprev_ref[...] * inner
        nxt = dxc_next_s[...]
        dxl = dxc * cw_ref[3:4, :]
        acc_ref[3:4, :] += jnp.sum(dxc * xl, axis=0, keepdims=True)
        for j in range(3):
            dxl = dxl + _shift_up(dxc, 3 - j, nxt) * cw_ref[j:j + 1, :]
            acc_ref[j:j + 1, :] += jnp.sum(dxc * _shift_down(xl, 3 - j, xlprev), axis=0, keepdims=True)
        dxc_next_s[...] = dxc[:SUBLANES, :]
        dxl_ref[...] = dxl.astype(BF16)

        acc_ref[4:5, :] += jnp.sum(dxc, axis=0, keepdims=True)
        acc_ref[5:6, :] += jnp.sum(dpr, axis=0, keepdims=True)
        acc_ref[6:7, :] += jnp.sum(dpi, axis=0, keepdims=True)
        acc_ref[7:8, :] += jnp.sum(dla * ((-LRU_C) * r), axis=0, keepdims=True)

        @pl.when(i == nt - 1)
        def _():
            lam_v = lam_ref[...]
            acc_ref[7:8, :] = acc_ref[7:8, :] * (-_sigmoid(-lam_v))

    rev = pl.BlockSpec((tm, D), lambda i: (nt - 1 - i, 0))
    prev8 = pl.BlockSpec((SUBLANES, D), lambda i: (jnp.maximum((nt - 1 - i) * per - 1, 0), 0))
    vec = _const_spec((1, D))
    wspec = _const_spec((NB, LANES, LANES))
    bf = jax.ShapeDtypeStruct((T, D), BF16)
    return pl.pallas_call(
        body, name="lru_bwd", grid=(nt,),
        in_specs=[rev, rev, prev8, rev, rev, prev8, _const_spec((4, D)), wspec, vec, wspec, vec, vec],
        out_specs=[rev, rev, rev, rev, _const_spec((SUBLANES, D))],
        out_shape=[bf, bf, bf, bf, jax.ShapeDtypeStruct((SUBLANES, D), F32)],
        scratch_shapes=[pltpu.VMEM((1, D), F32), pltpu.VMEM((SUBLANES, D), F32), pltpu.VMEM((SUBLANES, D), F32)],
        compiler_params=_cparams(("arbitrary",)),
    )(dh, h, h, xc, x_lru, x_lru, conv_w, w_r, b_r, w_i, b_i, lam)


def _attn_bwd(q, k, v, do, bq_row, dd_row, c_col):
    T = q.shape[0]
    t = TA
    n = T // t
    scale = DH ** -0.5

    def body(q_ref, k_ref, v_ref, do_ref, bq_ref, dd_ref, ck_ref,
             dq_ref, dk_ref, dv_ref, dc_ref, dcq_ref, dq_s, dk_s, dv_s, dc_s, dcq_s):
        ki = pl.program_id(1)
        qi = pl.program_id(2)

        @pl.when((ki == 0) & (qi == 0))
        def _():
            dq_s[...] = jnp.zeros_like(dq_s)
            dcq_s[...] = jnp.zeros_like(dcq_s)

        @pl.when(qi == 0)
        def _():
            dk_s[...] = jnp.zeros_like(dk_s)
            dv_s[...] = jnp.zeros_like(dv_s)
            dc_s[...] = jnp.zeros_like(dc_s)

        @pl.when(qi >= ki)
        def _():
            qv = q_ref[...]
            kv = k_ref[...]
            dov = do_ref[...]
            st = _dot_nt(kv, qv) * scale + (bq_ref[0] - ck_ref[0])
            krow = ki * t + lax.broadcasted_iota(jnp.int32, (t, t), 0)
            qcol = qi * t + lax.broadcasted_iota(jnp.int32, (t, t), 1)
            pt = jnp.exp(jnp.where(krow <= qcol, st, NEG))
            dv_s[...] += _dot(pt.astype(BF16), dov)
            dst = pt * (_dot_nt(v_ref[...], dov) - dd_ref[0])
            dsb = dst.astype(BF16)
            dk_s[...] += _dot(dsb, qv)
            off = pl.multiple_of(qi * t, t)
            dq_s[pl.ds(off, t), :] += _dot_tn(dsb, kv)
            dc_s[...] -= jnp.sum(dst, axis=1, keepdims=True)
            dcq_s[:, pl.ds(off, t)] += jnp.sum(dst, axis=0, keepdims=True)

        @pl.when(qi == n - 1)
        def _():
            dk_ref[...] = (dk_s[...] * scale).astype(BF16)
            dv_ref[...] = dv_s[...].astype(BF16)
            dc_ref[0] = dc_s[...]

        @pl.when((ki == n - 1) & (qi == n - 1))
        def _():
            dq_ref[...] = (dq_s[...] * scale).astype(BF16)
            dcq_ref[0] = dcq_s[...]

    qside = pl.BlockSpec((t, DH), lambda h, ki, qi: (jnp.maximum(qi, ki), h))
    kside = pl.BlockSpec((t, DH), lambda h, ki, qi: (ki, h))
    qrow = pl.BlockSpec((1, 1, t), lambda h, ki, qi: (h, 0, jnp.maximum(qi, ki)))
    kcol = pl.BlockSpec((1, t, 1), lambda h, ki, qi: (h, ki, 0))
    bf = jax.ShapeDtypeStruct((T, D), BF16)
    return pl.pallas_call(
        body, name="attn_bwd", grid=(H, n, n),
        in_specs=[qside, kside, kside, qside, qrow, qrow, kcol],
        out_specs=[pl.BlockSpec((T, DH), lambda h, ki, qi: (0, h)), kside, kside, kcol,
                   pl.BlockSpec((1, 1, T), lambda h, ki, qi: (h, 0, 0))],
        out_shape=[bf, bf, bf, jax.ShapeDtypeStruct((H, T, 1), F32), jax.ShapeDtypeStruct((H, 1, T), F32)],
        scratch_shapes=[pltpu.VMEM((T, DH), F32), pltpu.VMEM((t, DH), F32), pltpu.VMEM((t, DH), F32),
                        pltpu.VMEM((t, 1), F32), pltpu.VMEM((1, T), F32)],
        compiler_params=_cparams(("arbitrary", "arbitrary", "arbitrary")),
    )(q, k, v, do, bq_row, dd_row, c_col)


def _fgate_bwd(dc_key, dc_query, flb):
    T = dc_key.shape[0]
    tm = TM
    nt = T // tm

    def body(dck_ref, dcq_ref, flb_ref, dfl_ref, acc_ref, carry, top_s):
        @pl.when(pl.program_id(0) == 0)
        def _():
            carry[...] = jnp.zeros_like(carry)
            acc_ref[...] = jnp.zeros_like(acc_ref)

        r = lax.broadcasted_iota(jnp.int32, (tm, tm), 0)
        c = lax.broadcasted_iota(jnp.int32, (tm, tm), 1)
        dls = _dot_exact((c >= r).astype(F32), dck_ref[...] + dcq_ref[...]) + carry[...]
        top_s[...] = dls[:SUBLANES, :]
        carry[...] = top_s[0:1, :]
        flb = flb_ref[...]
        lane = lax.broadcasted_iota(jnp.int32, flb.shape, 1)
        dfl = jnp.where(lane < H, dls * _sigmoid(-flb), 0.0)
        dfl_ref[...] = dfl.astype(BF16)
        acc_ref[0:1, :] += jnp.sum(dfl, axis=0, keepdims=True)

    rev = pl.BlockSpec((tm, LANES), lambda i: (nt - 1 - i, 0))
    return pl.pallas_call(
        body, name="fgate_bwd", grid=(nt,),
        in_specs=[rev, rev, rev],
        out_specs=[rev, _const_spec((SUBLANES, LANES))],
        out_shape=[jax.ShapeDtypeStruct((T, LANES), BF16), jax.ShapeDtypeStruct((SUBLANES, LANES), F32)],
        scratch_shapes=[pltpu.VMEM((1, LANES), F32), pltpu.VMEM((SUBLANES, LANES), F32)],
        compiler_params=_cparams(("arbitrary",)),
    )(dc_key, dc_query, flb)


def _dx(dz, dfl, w_main, w_f, x, pre_gain, dh1):
    T = x.shape[0]
    tm = TM

    def body(*refs):
        dz_refs = refs[:6]
        dfl_ref, wm_ref, wf_ref, x_ref, g_ref, dh1_ref, gx_ref, acc_ref = refs[6:]

        @pl.when(pl.program_id(0) == 0)
        def _():
            acc_ref[...] = jnp.zeros_like(acc_ref)

        dxn = _dot_nt(dfl_ref[...], wf_ref[...])
        for s in range(6):
            dxn = dxn + _dot_nt(dz_refs[s][...], wm_ref[:, s * D:(s + 1) * D])
        xv = x_ref[...]
        rstd = _rstd(xv)
        xhat = xv * rstd
        gx_ref[...] = dh1_ref[...] + _rms_bwd(dxn * g_ref[...], xhat, rstd)
        acc_ref[0:1, :] += jnp.sum(dxn * xhat, axis=0, keepdims=True)

    return pl.pallas_call(
        body, name="dx", grid=(T // tm,),
        in_specs=[_row_spec(tm, D)] * 6 + [_row_spec(tm, LANES), _const_spec((D, 6 * D)), _const_spec((D, LANES)),
                                           _row_spec(tm, D), _const_spec((1, D)), _row_spec(tm, D)],
        out_specs=[_row_spec(tm, D), _const_spec((SUBLANES, D))],
        out_shape=[jax.ShapeDtypeStruct((T, D), F32), jax.ShapeDtypeStruct((SUBLANES, D), F32)],
        compiler_params=_cparams(("arbitrary",), VMEM_BIG),
    )(*dz, dfl, w_main, w_f, x, pre_gain, dh1)


def _matmul_tn(a, b, name, bm=512, bn=1024, bt=512):
    T, M = a.shape
    N = b.shape[1]
    bm, bn, bt = min(bm, M), min(bn, N), min(bt, T)

    def body(a_ref, b_ref, o_ref):
        @pl.when(pl.program_id(2) == 0)
        def _():
            o_ref[...] = jnp.zeros_like(o_ref)

        o_ref[...] += _dot_tn(a_ref[...], b_ref[...])

    return pl.pallas_call(
        body, name=name, grid=(M // bm, N // bn, T // bt),
        in_specs=[pl.BlockSpec((bt, bm), lambda i, j, t: (t, i)), pl.BlockSpec((bt, bn), lambda i, j, t: (t, j))],
        out_specs=pl.BlockSpec((bm, bn), lambda i, j, t: (i, j)),
        out_shape=jax.ShapeDtypeStruct((M, N), F32),
        compiler_params=_cparams(("parallel", "parallel", "arbitrary")),
    )(a, b)


def _matmul_tn_blocks(a, b, name, bt=512):
    T = a.shape[0]

    def body(a_ref, b_ref, o_ref):
        @pl.when(pl.program_id(1) == 0)
        def _():
            o_ref[...] = jnp.zeros_like(o_ref)

        o_ref[0] += _dot_tn(a_ref[...], b_ref[...])

    blk = pl.BlockSpec((bt, LANES), lambda n, t: (t, n))
    return pl.pallas_call(
        body, name=name, grid=(NB, T // bt),
        in_specs=[blk, blk],
        out_specs=pl.BlockSpec((1, LANES, LANES), lambda n, t: (n, 0, 0)),
        out_shape=jax.ShapeDtypeStruct((NB, LANES, LANES), F32),
        compiler_params=_cparams(("parallel", "arbitrary")),
    )(a, b)


HBM_SPEC = pl.BlockSpec(memory_space=pltpu.HBM)
VMEM_SPEC = pl.BlockSpec(memory_space=pltpu.VMEM)


def _position():
    return lax.axis_index("x"), lax.axis_index("y"), lax.axis_index("c")


def _other_chips(x, y):
    return [(1 - x, y), (x, 1 - y), (1 - x, 1 - y)]


def _gather_shards(shards):
    na = len(shards)

    def body(*refs):
        srcs, dsts = refs[:na], refs[na:2 * na]
        send_sems, recv_sems, loc_sems = refs[2 * na:]
        x, y, c = _position()
        chip = 2 * x + y
        local = [pltpu.make_async_copy(srcs[a], dsts[a].at[chip], loc_sems.at[a]) for a in range(na)]
        for cp in local:
            cp.start()
        copies = []
        for j, (px, py) in enumerate(_other_chips(x, y)):
            for a in range(na):
                copies.append(pltpu.make_async_remote_copy(
                    src_ref=srcs[a], dst_ref=dsts[a].at[chip], send_sem=send_sems.at[j * na + a],
                    recv_sem=recv_sems.at[j * na + a], device_id=(px, py, c), device_id_type=MESH))
        for cp in copies:
            cp.start()
        for cp in copies:
            cp.wait()
        for cp in local:
            cp.wait()

    return pl.pallas_call(
        body, name="gather_shards",
        in_specs=[HBM_SPEC] * na, out_specs=[HBM_SPEC] * na,
        out_shape=[jax.ShapeDtypeStruct((N_CHIPS,) + s.shape, s.dtype) for s in shards],
        scratch_shapes=[pltpu.SemaphoreType.DMA((3 * na,)), pltpu.SemaphoreType.DMA((3 * na,)),
                        pltpu.SemaphoreType.DMA((na,))],
    )(*shards)


def _pair_exchange(parts):
    na = len(parts)

    def body(*refs):
        srcs, dsts = refs[:na], refs[na:2 * na]
        send_sems, recv_sems = refs[2 * na:]
        x, y, c = _position()
        copies = []
        for a in range(na):
            half = srcs[a].shape[1] // 2
            start = pl.multiple_of((1 - c) * half, SUBLANES)
            copies.append(pltpu.make_async_remote_copy(
                src_ref=srcs[a].at[:, pl.ds(start, half), :], dst_ref=dsts[a], send_sem=send_sems.at[a],
                recv_sem=recv_sems.at[a], device_id=(x, y, 1 - c), device_id_type=MESH))
        for cp in copies:
            cp.start()
        for cp in copies:
            cp.wait()

    return pl.pallas_call(
        body, name="pair_exchange",
        in_specs=[HBM_SPEC] * na, out_specs=[HBM_SPEC] * na,
        out_shape=[jax.ShapeDtypeStruct((s.shape[0], s.shape[1] // 2, s.shape[2]), s.dtype) for s in parts],
        scratch_shapes=[pltpu.SemaphoreType.DMA((na,)), pltpu.SemaphoreType.DMA((na,))],
    )(*parts)


def _pair_sum(part, got, c, name):
    _, R, C = part.shape
    half = R // 2
    br = min(half, 256)
    nr = half // br

    def body(c_ref, a_ref, b_ref, o_ref):
        o_ref[...] = (a_ref[...] + b_ref[...]).astype(BF16)

    grid_spec = pltpu.PrefetchScalarGridSpec(
        num_scalar_prefetch=1, grid=(N_CHIPS, nr),
        in_specs=[pl.BlockSpec((1, br, C), lambda j, r, c_ref: (j, c_ref[0] * nr + r, 0)),
                  pl.BlockSpec((1, br, C), lambda j, r, c_ref: (j, r, 0))],
        out_specs=pl.BlockSpec((1, br, C), lambda j, r, c_ref: (j, r, 0)))
    return pl.pallas_call(
        body, name=name, grid_spec=grid_spec,
        out_shape=jax.ShapeDtypeStruct((N_CHIPS, half, C), BF16),
        compiler_params=_cparams(("parallel", "parallel")),
    )(c.reshape(1).astype(jnp.int32), part, got)


def _chip_exchange(sums):
    na = len(sums)

    def body(*refs):
        srcs, dsts = refs[:na], refs[na:2 * na]
        send_sems, recv_sems, loc_sems = refs[2 * na:]
        x, y, c = _position()
        chip = 2 * x + y
        local = [pltpu.make_async_copy(srcs[a].at[chip], dsts[a].at[chip], loc_sems.at[a]) for a in range(na)]
        for cp in local:
            cp.start()
        copies = []
        for j, (px, py) in enumerate(_other_chips(x, y)):
            for a in range(na):
                copies.append(pltpu.make_async_remote_copy(
                    src_ref=srcs[a].at[2 * px + py], dst_ref=dsts[a].at[chip], send_sem=send_sems.at[j * na + a],
                    recv_sem=recv_sems.at[j * na + a], device_id=(px, py, c), device_id_type=MESH))
        for cp in copies:
            cp.start()
        for cp in copies:
            cp.wait()
        for cp in local:
            cp.wait()

    return pl.pallas_call(
        body, name="chip_exchange",
        in_specs=[HBM_SPEC] * na, out_specs=[HBM_SPEC] * na,
        out_shape=[jax.ShapeDtypeStruct(s.shape, s.dtype) for s in sums],
        scratch_shapes=[pltpu.SemaphoreType.DMA((3 * na,)), pltpu.SemaphoreType.DMA((3 * na,)),
                        pltpu.SemaphoreType.DMA((na,))],
    )(*sums)


def _chip_sum(got, name):
    _, half, C = got.shape
    br = min(half, 256)

    def body(g_ref, o_ref):
        acc = g_ref[0].astype(F32) + g_ref[1].astype(F32)
        acc = acc + g_ref[2].astype(F32)
        o_ref[...] = acc + g_ref[3].astype(F32)

    return pl.pallas_call(
        body, name=name, grid=(half // br,),
        in_specs=[pl.BlockSpec((N_CHIPS, br, C), lambda r: (0, r, 0))],
        out_specs=pl.BlockSpec((br, C), lambda r: (r, 0)),
        out_shape=jax.ShapeDtypeStruct((half, C), F32),
        compiler_params=_cparams(("parallel",)),
    )(got)


def _pair_gather(halves):
    na = len(halves)

    def body(*refs):
        srcs, dsts = refs[:na], refs[na:2 * na]
        send_sems, recv_sems, loc_sems = refs[2 * na:]
        x, y, c = _position()
        local, copies = [], []
        for a in range(na):
            half = srcs[a].shape[0]
            rows = dsts[a].at[pl.ds(pl.multiple_of(c * half, SUBLANES), half), :]
            local.append(pltpu.make_async_copy(srcs[a], rows, loc_sems.at[a]))
            copies.append(pltpu.make_async_remote_copy(
                src_ref=srcs[a], dst_ref=rows, send_sem=send_sems.at[a], recv_sem=recv_sems.at[a],
                device_id=(x, y, 1 - c), device_id_type=MESH))
        for cp in local + copies:
            cp.start()
        for cp in copies:
            cp.wait()
        for cp in local:
            cp.wait()

    return pl.pallas_call(
        body, name="pair_gather",
        in_specs=[HBM_SPEC] * na, out_specs=[HBM_SPEC] * na,
        out_shape=[jax.ShapeDtypeStruct((2 * s.shape[0], s.shape[1]), s.dtype) for s in halves],
        scratch_shapes=[pltpu.SemaphoreType.DMA((na,)), pltpu.SemaphoreType.DMA((na,)),
                        pltpu.SemaphoreType.DMA((na,))],
    )(*halves)


def _allreduce_small(g):
    rows = g.shape[0]
    per = rows // N_DEV

    def body(g_ref, out_ref, got_ref, s1, r1, s2, r2):
        x, y, c = _position()
        me = 4 * x + 2 * y + c
        mine = pl.ds(pl.multiple_of(me * per, SUBLANES), per)
        peers = []
        for j in range(1, N_DEV):
            px = 1 - x if j & 4 else x
            py = 1 - y if j & 2 else y
            pc = 1 - c if j & 1 else c
            peers.append((px, py, pc))

        first = []
        for j, (px, py, pc) in enumerate(peers):
            theirs = pl.ds(pl.multiple_of((4 * px + 2 * py + pc) * per, SUBLANES), per)
            first.append(pltpu.make_async_remote_copy(
                src_ref=g_ref.at[theirs, :], dst_ref=got_ref.at[me], send_sem=s1.at[j], recv_sem=r1.at[j],
                device_id=(px, py, pc), device_id_type=MESH))
        for cp in first:
            cp.start()
        got_ref[me] = g_ref[mine, :]
        for cp in first:
            cp.wait()
        total = got_ref[0]
        for d in range(1, N_DEV):
            total = total + got_ref[d]
        out_ref[mine, :] = total

        second = []
        for j, peer in enumerate(peers):
            second.append(pltpu.make_async_remote_copy(
                src_ref=out_ref.at[mine, :], dst_ref=out_ref.at[mine, :], send_sem=s2.at[j], recv_sem=r2.at[j],
                device_id=peer, device_id_type=MESH))
        for cp in second:
            cp.start()
        for cp in second:
            cp.wait()

    sems = pltpu.SemaphoreType.DMA((N_DEV - 1,))
    return pl.pallas_call(
        body, name="allreduce_small",
        in_specs=[VMEM_SPEC], out_specs=VMEM_SPEC,
        out_shape=jax.ShapeDtypeStruct(g.shape, F32),
        scratch_shapes=[pltpu.VMEM((N_DEV, per, LANES), F32), sems, sems, sems, sems],
    )(g)


def _adamw_math(g, w, m, v):
    m2 = ADAM_B1 * m + (1.0 - ADAM_B1) * g
    v2 = ADAM_B2 * v + (1.0 - ADAM_B2) * (g * g)
    m_hat = m2 / (1.0 - ADAM_B1 ** ADAM_STEP)
    v_hat = v2 / (1.0 - ADAM_B2 ** ADAM_STEP)
    delta = (-ADAM_LR) * (m_hat / (jnp.sqrt(v_hat) + ADAM_EPS) + ADAM_WD * w)
    return delta, m2, v2


def _adamw_big(g, w, m, v, name):
    R, C = g.shape
    br = min(R, 256)

    def body(g_ref, w_ref, m_ref, v_ref, d_ref, m2_ref, v2_ref):
        d_ref[...], m2_ref[...], v2_ref[...] = _adamw_math(g_ref[...], w_ref[...], m_ref[...], v_ref[...])

    spec = pl.BlockSpec((br, C), lambda r: (r, 0))
    out = jax.ShapeDtypeStruct((R, C), F32)
    return pl.pallas_call(
        body, name=name, grid=(R // br,),
        in_specs=[spec] * 4, out_specs=[spec] * 3, out_shape=[out] * 3,
        compiler_params=_cparams(("parallel",)),
    )(g, w, m, v)


def _adamw_small(gs, ws, ms, vs):
    n = len(gs)

    def body(*refs):
        for a in range(n):
            g_ref, w_ref, m_ref, v_ref = (refs[k * n + a] for k in range(4))
            d_ref, m2_ref, v2_ref = (refs[(4 + k) * n + a] for k in range(3))
            d_ref[...], m2_ref[...], v2_ref[...] = _adamw_math(g_ref[...], w_ref[...], m_ref[...], v_ref[...])

    outs = [jax.ShapeDtypeStruct(w.shape, F32) for w in ws]
    return pl.pallas_call(
        body, name="adamw_small",
        in_specs=[VMEM_SPEC] * (4 * n), out_specs=[VMEM_SPEC] * (3 * n), out_shape=outs * 3,
    )(*gs, *ws, *ms, *vs)


def _split_w_in(w_full):
    w_main = jnp.concatenate([w_full[:, :FL0], w_full[:, FL0 + H:]], axis=1)
    w_f = jnp.pad(w_full[:, FL0:FL0 + H], ((0, 0), (0, LANES - H)))
    return w_main, w_f


def _local_step(x, p, tgt, w_in_b, w_out_b, w_ple_b, w_gate_b, conv_w, b_f, pre_gain, post_gain, conv_b,
                w_rgate, b_rgate, w_igate, b_igate, lam, gain_a, gain_l, ple_gain, b_gate):
    w_main, w_f = _split_w_in(w_in_b)
    b_f_pad = jnp.pad(b_f, ((0, 0), (0, LANES - H)))
    w_r = w_rgate.astype(BF16)
    w_i = w_igate.astype(BF16)

    xn, q, k, v, g_attn, x_lru, g_lru, flb, csum = _in_proj(x, pre_gain, w_main, w_f, b_f_pad)
    c_hm = csum[:, :H].T
    c_col, c_row = c_hm[:, :, None], c_hm[:, None, :]
    o, bq = _attn_fwd(q, k, v, c_col, c_row)
    ycat, xc, h = _branches_fwd(o, g_attn, x_lru, g_lru, gain_a, gain_l, conv_w, conv_b, w_r, b_rgate, w_i, b_igate,
                                lam)
    dh1, dycat, dmix, h1b, dgp, pb, dpe, acc_t = _tail(ycat, x, p, tgt, w_out_b, post_gain, w_ple_b, ple_gain,
                                                       w_gate_b, b_gate)
    do, dd, dg_attn, dg_lru, dh, acc_b = _branches_bwd(dycat, o, g_attn, h, g_lru, gain_a, gain_l)
    dx_lru, dpr, dpi, xcb, acc_l = _lru_bwd(dh, h, xc, x_lru, conv_w, w_r, b_rgate, w_i, b_igate, lam)
    dd_row = dd[:, :H].T[:, None, :]
    bq_row = bq[:, :, 0][:, None, :]
    dq, dk, dv, dc, dcq = _attn_bwd(q, k, v, do, bq_row, dd_row, c_col)
    pad_heads = ((0, 0), (0, LANES - H))
    dfl, acc_f = _fgate_bwd(jnp.pad(dc[:, :, 0].T, pad_heads), jnp.pad(dcq[:, 0, :].T, pad_heads), flb)
    dz = (dq, dk, dv, dg_attn, dx_lru, dg_lru)
    grad_x, acc_x = _dx(dz, dfl, w_main, w_f, x, pre_gain, dh1)

    gw_main = [_matmul_tn(xn, dz[s], "dw_in_%d" % s) for s in range(6)]
    gw_f = _matmul_tn(xn, dfl, "dw_in_f")
    gw_in = jnp.concatenate(gw_main[:3] + [gw_f[:, :H]] + gw_main[3:], axis=1)
    grads = dict(
        w_in=gw_in,
        w_out=_matmul_tn(ycat, dmix, "dw_out"),
        w_ple=_matmul_tn(pb, dpe, "dw_ple"),
        w_ple_gate=_matmul_tn(h1b, dgp, "dw_ple_gate"),
        w_rgate=_matmul_tn_blocks(xcb, dpr, "dw_rgate"),
        w_igate=_matmul_tn_blocks(xcb, dpi, "dw_igate"),
        b_f=acc_f[0:1, :H],
        pre_gain=acc_x[0:1],
        post_gain=acc_t[0:1],
        conv_w=acc_l[0:4],
        conv_b=acc_l[4:5],
        b_rgate=acc_l[5:6],
        b_igate=acc_l[6:7],
        lru_lambda=acc_l[7:8],
        attn_out_gain=acc_b[0:1],
        lru_out_gain=acc_b[1:2],
        ple_gain=acc_t[1:2],
        b_ple_gate=acc_t[2:3],
    )
    loss = jnp.sum(acc_t[3])
    return loss, grad_x, grads


SMALL_ROWS = ["b_f", "pre_gain", "post_gain", "conv_w", "conv_b", "b_rgate", "b_igate", "lru_lambda",
              "attn_out_gain", "lru_out_gain", "ple_gain", "b_ple_gate"]
WEIGHTS = ["w_in", "b_f", "pre_gain", "post_gain", "conv_w", "conv_b", "w_rgate", "b_rgate", "w_igate", "b_igate",
           "lru_lambda", "attn_out_gain", "lru_out_gain", "w_out", "w_ple", "ple_gain", "w_ple_gate", "b_ple_gate"]
SHARDED = ["w_in", "w_out", "w_ple", "w_ple_gate"]


def _by_chip_cols(g):
    r, cols = g.shape
    return g.reshape(r, N_CHIPS, cols // N_CHIPS).transpose(1, 0, 2)


def _from_chip_cols(s):
    n, r, cols = s.shape
    return s.transpose(1, 0, 2).reshape(r, n * cols)


def kernel(x, p, w_in, b_f, pre_gain, post_gain, conv_w, conv_b, w_rgate, b_rgate, w_igate, b_igate, lru_lambda, attn_out_gain, lru_out_gain, w_out, w_ple, ple_gain, w_ple_gate, b_ple_gate, loss_target, m_w_in, m_b_f, m_pre_gain, m_post_gain, m_conv_w, m_conv_b, m_w_rgate, m_b_rgate, m_w_igate, m_b_igate, m_lru_lambda, m_attn_out_gain, m_lru_out_gain, m_w_out, m_w_ple, m_ple_gain, m_w_ple_gate, m_b_ple_gate, v_w_in, v_b_f, v_pre_gain, v_post_gain, v_conv_w, v_conv_b, v_w_rgate, v_b_rgate, v_w_igate, v_b_igate, v_lru_lambda, v_attn_out_gain, v_lru_out_gain, v_w_out, v_w_ple, v_ple_gain, v_w_ple_gate, v_b_ple_gate):
    w = dict(w_in=w_in, b_f=b_f, pre_gain=pre_gain, post_gain=post_gain, conv_w=conv_w, conv_b=conv_b,
             w_rgate=w_rgate, b_rgate=b_rgate, w_igate=w_igate, b_igate=b_igate, lru_lambda=lru_lambda,
             attn_out_gain=attn_out_gain, lru_out_gain=lru_out_gain, w_out=w_out, w_ple=w_ple, ple_gain=ple_gain,
             w_ple_gate=w_ple_gate, b_ple_gate=b_ple_gate)
    m = dict(w_in=m_w_in, b_f=m_b_f, pre_gain=m_pre_gain, post_gain=m_post_gain, conv_w=m_conv_w, conv_b=m_conv_b,
             w_rgate=m_w_rgate, b_rgate=m_b_rgate, w_igate=m_w_igate, b_igate=m_b_igate, lru_lambda=m_lru_lambda,
             attn_out_gain=m_attn_out_gain, lru_out_gain=m_lru_out_gain, w_out=m_w_out, w_ple=m_w_ple,
             ple_gain=m_ple_gain, w_ple_gate=m_w_ple_gate, b_ple_gate=m_b_ple_gate)
    v = dict(w_in=v_w_in, b_f=v_b_f, pre_gain=v_pre_gain, post_gain=v_post_gain, conv_w=v_conv_w, conv_b=v_conv_b,
             w_rgate=v_w_rgate, b_rgate=v_b_rgate, w_igate=v_w_igate, b_igate=v_b_igate, lru_lambda=v_lru_lambda,
             attn_out_gain=v_attn_out_gain, lru_out_gain=v_lru_out_gain, w_out=v_w_out, w_ple=v_w_ple,
             ple_gain=v_ple_gain, w_ple_gate=v_w_ple_gate, b_ple_gate=v_b_ple_gate)
    xi, yi, ci = _position()
    chip = 2 * xi + yi

    st_in, st_out, st_ple, st_gate, st_conv = _gather_shards(
        [w_in[0].astype(BF16), w_out[0].astype(BF16), w_ple[0].astype(BF16), w_ple_gate[0].astype(BF16), conv_w[0]])
    w_in_b = _from_chip_cols(st_in)
    w_out_b = st_out.reshape(DMIX, D)
    w_ple_b = _from_chip_cols(st_ple)
    w_gate_b = st_gate.reshape(D, D)
    conv_full = _from_chip_cols(st_conv)

    loss, grad_x, g = _local_step(
        x[0], p[0, 0], loss_target[0], w_in_b, w_out_b, w_ple_b, w_gate_b, conv_full, b_f, pre_gain, post_gain,
        conv_b, w_rgate[0], b_rgate, w_igate[0], b_igate, lru_lambda, attn_out_gain, lru_out_gain, ple_gain,
        b_ple_gate)
    loss = lax.psum(loss, ("x", "y", "c"))

    parts = [_by_chip_cols(g["w_in"]), g["w_out"].reshape(N_CHIPS, DMIX // N_CHIPS, D), _by_chip_cols(g["w_ple"]),
             g["w_ple_gate"].reshape(N_CHIPS, D // N_CHIPS, D)]
    got = _pair_exchange(parts)
    sums = [_pair_sum(parts[a], got[a], ci, "pair_sum_%d" % a) for a in range(4)]
    recv = _chip_exchange(sums)
    halves = [_chip_sum(recv[a], "chip_sum_%d" % a) for a in range(4)]
    full = _pair_gather(halves)
    red = dict(zip(SHARDED, full))

    rows = [jnp.pad(g["b_f"], ((0, 0), (0, D - H)))] + [g[n] for n in SMALL_ROWS[1:]]
    rows.append(jnp.zeros((16 - sum(r.shape[0] for r in rows), D), F32))
    packed = jnp.concatenate([g["w_rgate"].reshape(NB * LANES, LANES), g["w_igate"].reshape(NB * LANES, LANES),
                              jnp.concatenate(rows, axis=0).reshape(LANES, LANES)], axis=0)
    summed = _allreduce_small(packed)
    red["w_rgate"] = summed[:D].reshape(1, NB, LANES, LANES)
    red["w_igate"] = summed[D:2 * D].reshape(1, NB, LANES, LANES)
    vec = summed[2 * D:].reshape(16, D)
    r0 = 0
    for n in SMALL_ROWS:
        nr = 4 if n == "conv_w" else 1
        red[n] = vec[r0:r0 + nr]
        r0 += nr
    red["b_f"] = red["b_f"][:, :H]
    red["conv_w"] = lax.dynamic_slice_in_dim(red["conv_w"], chip * (D // N_CHIPS), D // N_CHIPS, axis=1)[None]

    delta, new_m, new_v = {}, {}, {}
    for n in SHARDED:
        delta[n], new_m[n], new_v[n] = (t[None] for t in _adamw_big(red[n], w[n][0], m[n][0], v[n][0], "adamw_" + n))
        red[n] = red[n][None]
    small = [n for n in WEIGHTS if n not in SHARDED]
    outs = _adamw_small([red[n] for n in small], [w[n] for n in small], [m[n] for n in small],
                        [v[n] for n in small])
    ns = len(small)
    for a, n in enumerate(small):
        delta[n], new_m[n], new_v[n] = outs[a], outs[ns + a], outs[2 * ns + a]

    return (loss, grad_x[None], *[red[n] for n in WEIGHTS], *[delta[n] for n in WEIGHTS],
            *[new_m[n] for n in WEIGHTS], *[new_v[n] for n in WEIGHTS])
```

```python
import functools

import jax
import jax.numpy as jnp
from jax import lax
from jax.experimental import pallas as pl
from jax.experimental.pallas import tpu as pltpu

F32 = jnp.float32
BF16 = jnp.bfloat16

D = 1024
H = 8
DH = 128
NB = 8
DPLE = 256
DMIX = 2 * D
D_IN = 4 * D + H + 2 * D
FL0 = 3 * D
RMS_EPS = 1e-6
LRU_C = 8.0
NEG = -1e30
LANES = 128
SUBLANES = 8

ADAM_LR = 0.001
ADAM_B1 = 0.9
ADAM_B2 = 0.999
ADAM_EPS = 1e-08
ADAM_WD = 0.01
ADAM_STEP = 10

TM = 256
TA = 512
VMEM_BIG = 56 * 1024 * 1024
VMEM_MID = 40 * 1024 * 1024

MESH = pl.DeviceIdType.MESH
N_CHIPS = 4
N_DEV = 8


def _cparams(sem, vmem=VMEM_MID):
    return pltpu.CompilerParams(dimension_semantics=sem, vmem_limit_bytes=vmem)


def _sigmoid(x):
    return 1.0 / (1.0 + jnp.exp(-x))


def _rstd(x):
    return lax.rsqrt(jnp.mean(x * x, axis=-1, keepdims=True) + RMS_EPS)


def _rms_bwd(t, xhat, rstd):
    return rstd * (t - xhat * jnp.mean(t * xhat, axis=-1, keepdims=True))


def _dot(a, b):
    return jnp.dot(a, b, preferred_element_type=F32)


def _dot_nt(a, b):
    return lax.dot_general(a, b, (((1,), (1,)), ((), ())), preferred_element_type=F32)


def _dot_tn(a, b):
    return lax.dot_general(a, b, (((0,), (0,)), ((), ())), preferred_element_type=F32)


def _dot_exact(a, b):
    return jnp.dot(a, b, preferred_element_type=F32, precision=lax.Precision.HIGHEST)


def _neg_expm1(x):
    series = x * (1.0 + x * 0.5 * (1.0 + x * (1.0 / 3.0) * (1.0 + x * 0.25 * (1.0 + x * 0.2 * (1.0 + x * (1.0 / 6.0))))))
    return -jnp.where(x > -0.25, series, jnp.exp(x) - 1.0)


def _shift_down(x, j, halo):
    rolled = pltpu.roll(x, j, 0)
    row = lax.broadcasted_iota(jnp.int32, halo.shape, 0)
    top = jnp.where(row < j, pltpu.roll(halo, j, 0), rolled[:SUBLANES])
    return jnp.concatenate([top, rolled[SUBLANES:]], axis=0)


def _shift_up(x, j, nxt):
    tm = x.shape[0]
    rolled = pltpu.roll(x, tm - j, 0)
    row = lax.broadcasted_iota(jnp.int32, nxt.shape, 0)
    bot = jnp.where(row >= SUBLANES - j, pltpu.roll(nxt, SUBLANES - j, 0), rolled[tm - SUBLANES:])
    return jnp.concatenate([rolled[:tm - SUBLANES], bot], axis=0)


def _scan_fwd(a, u):
    tm = a.shape[0]
    row = lax.broadcasted_iota(jnp.int32, a.shape, 0)
    d = 1
    while d < tm:
        keep = row >= d
        a_s = jnp.where(keep, pltpu.roll(a, d, 0), 1.0)
        u_s = jnp.where(keep, pltpu.roll(u, d, 0), 0.0)
        u = u + a * u_s
        a = a * a_s
        d *= 2
    return a, u


def _scan_bwd(b, u):
    tm = b.shape[0]
    row = lax.broadcasted_iota(jnp.int32, b.shape, 0)
    d = 1
    while d < tm:
        keep = row < tm - d
        b_s = jnp.where(keep, pltpu.roll(b, tm - d, 0), 1.0)
        u_s = jnp.where(keep, pltpu.roll(u, tm - d, 0), 0.0)
        u = u + b * u_s
        b = b * b_s
        d *= 2
    return u


def _gate_pre(xc, w_ref):
    outs = []
    for n in range(NB):
        outs.append(_dot(xc[:, n * LANES:(n + 1) * LANES].astype(BF16), w_ref[n]))
    return jnp.concatenate(outs, axis=1)


def _gate_pre_t(d, w_ref):
    outs = []
    for n in range(NB):
        outs.append(_dot_nt(d[:, n * LANES:(n + 1) * LANES].astype(BF16), w_ref[n]))
    return jnp.concatenate(outs, axis=1)


def _softplus_neg(lam):
    return jnp.maximum(-lam, 0.0) + jnp.log(1.0 + jnp.exp(-jnp.abs(lam)))


def _row_spec(tm, width):
    return pl.BlockSpec((tm, width), lambda i: (i, 0))


def _const_spec(shape):
    nd = len(shape)
    return pl.BlockSpec(shape, lambda *_: (0,) * nd)


def _in_proj(x, pre_gain, w_a, w_f, w_b, b_f_pad):
    T = x.shape[0]
    tm = TM

    def body(x_ref, g_ref, wa_ref, wf_ref, wb_ref, bf_ref,
             xn_ref, q_ref, k_ref, v_ref, ga_ref, xl_ref, gl_ref, flb_ref, c_ref, carry):
        @pl.when(pl.program_id(0) == 0)
        def _():
            carry[...] = jnp.zeros_like(carry)

        xv = x_ref[...]
        xn = (xv * _rstd(xv) * g_ref[...]).astype(BF16)
        xn_ref[...] = xn
        for s, o_ref in enumerate((q_ref, k_ref, v_ref)):
            o_ref[...] = _dot_nt(xn, wa_ref[s * D:(s + 1) * D, :]).astype(o_ref.dtype)
        for s, o_ref in enumerate((ga_ref, xl_ref, gl_ref)):
            o_ref[...] = _dot_nt(xn, wb_ref[s * D:(s + 1) * D, :]).astype(o_ref.dtype)
        flb = _dot_nt(xn, wf_ref[...]) + bf_ref[...]
        flb_ref[...] = flb
        lane = lax.broadcasted_iota(jnp.int32, flb.shape, 1)
        ls = jnp.where(lane < H, jnp.minimum(flb, 0.0) - jnp.log(1.0 + jnp.exp(-jnp.abs(flb))), 0.0)
        r = lax.broadcasted_iota(jnp.int32, (tm, tm), 0)
        c = lax.broadcasted_iota(jnp.int32, (tm, tm), 1)
        cs = _dot_exact((c <= r).astype(F32), ls) + carry[...]
        c_ref[...] = cs
        carry[...] = c_ref[tm - 1:tm, :]

    bf = jax.ShapeDtypeStruct((T, D), BF16)
    f32 = jax.ShapeDtypeStruct((T, D), F32)
    nar = jax.ShapeDtypeStruct((T, LANES), F32)
    return pl.pallas_call(
        body, name="in_proj", grid=(T // tm,),
        in_specs=[_row_spec(tm, D), _const_spec((1, D)), _const_spec((3 * D, D)), _const_spec((LANES, D)),
                  _const_spec((3 * D, D)), _const_spec((1, LANES))],
        out_specs=[_row_spec(tm, D)] * 7 + [_row_spec(tm, LANES)] * 2,
        out_shape=[bf, bf, bf, bf, f32, f32, f32, nar, nar],
        scratch_shapes=[pltpu.VMEM((1, LANES), F32)],
        compiler_params=_cparams(("arbitrary",), VMEM_BIG),
    )(x, pre_gain, w_a, w_f, w_b, b_f_pad)


def _attn_fwd(q, k, v, c_col, c_row):
    T = q.shape[0]
    t = TA
    n = T // t
    scale = DH ** -0.5

    def body(q_ref, k_ref, v_ref, cq_ref, ck_ref, o_ref, bq_ref, m_s, l_s, acc_s):
        qi = pl.program_id(1)
        ki = pl.program_id(2)

        @pl.when(ki == 0)
        def _():
            m_s[...] = jnp.full(m_s.shape, NEG, F32)
            l_s[...] = jnp.zeros_like(l_s)
            acc_s[...] = jnp.zeros_like(acc_s)

        @pl.when(ki <= qi)
        def _():
            s = _dot_nt(q_ref[...], k_ref[...]) * scale + (cq_ref[0] - ck_ref[0])
            row = qi * t + lax.broadcasted_iota(jnp.int32, (t, t), 0)
            col = ki * t + lax.broadcasted_iota(jnp.int32, (t, t), 1)
            s = jnp.where(col <= row, s, NEG)
            m_prev = m_s[...]
            m_new = jnp.maximum(m_prev, jnp.max(s, axis=1, keepdims=True))
            alpha = jnp.exp(m_prev - m_new)
            pr = jnp.exp(s - m_new)
            l_s[...] = alpha * l_s[...] + jnp.sum(pr, axis=1, keepdims=True)
            acc_s[...] = alpha * acc_s[...] + _dot(pr.astype(BF16), v_ref[...])
            m_s[...] = m_new

        @pl.when(ki == n - 1)
        def _():
            l = l_s[...]
            o_ref[...] = acc_s[...] / l
            bq_ref[0] = cq_ref[0] - (m_s[...] + jnp.log(l))

    q_spec = pl.BlockSpec((t, DH), lambda h, qi, ki: (qi, h))
    kv_spec = pl.BlockSpec((t, DH), lambda h, qi, ki: (jnp.minimum(ki, qi), h))
    return pl.pallas_call(
        body, name="attn_fwd", grid=(H, n, n),
        in_specs=[q_spec, kv_spec, kv_spec,
                  pl.BlockSpec((1, t, 1), lambda h, qi, ki: (h, qi, 0)),
                  pl.BlockSpec((1, 1, t), lambda h, qi, ki: (h, 0, jnp.minimum(ki, qi)))],
        out_specs=[q_spec, pl.BlockSpec((1, t, 1), lambda h, qi, ki: (h, qi, 0))],
        out_shape=[jax.ShapeDtypeStruct((T, D), F32), jax.ShapeDtypeStruct((H, T, 1), F32)],
        scratch_shapes=[pltpu.VMEM((t, 1), F32), pltpu.VMEM((t, 1), F32), pltpu.VMEM((t, DH), F32)],
        compiler_params=_cparams(("parallel", "parallel", "arbitrary")),
    )(q, k, v, c_col, c_row)


def _lru_gates(xc, wr_ref, br_ref, wi_ref, bi_ref, lam_ref):
    r = _sigmoid(_gate_pre(xc, wr_ref) + br_ref[...])
    ig = _sigmoid(_gate_pre(xc, wi_ref) + bi_ref[...])
    sp = _softplus_neg(lam_ref[...])
    la = (-LRU_C) * r * sp
    a = jnp.exp(la)
    sq = jnp.sqrt(_neg_expm1(2.0 * la))
    return r, ig, sp, a, sq


def _branches_fwd(o, g_attn, x_lru, g_lru, gain_a, gain_l, conv_w, conv_b, w_r, b_r, w_i, b_i, lam):
    T = o.shape[0]
    tm = TM

    def body(o_ref, ga_ref, xl_ref, gl_ref, gna_ref, gnl_ref, cw_ref, cb_ref, wr_ref, br_ref, wi_ref, bi_ref,
             lam_ref, ycat_ref, xc_ref, h_ref, halo_s, hc_s):
        @pl.when(pl.program_id(0) == 0)
        def _():
            halo_s[...] = jnp.zeros_like(halo_s)
            hc_s[...] = jnp.zeros_like(hc_s)

        ov = o_ref[...]
        ga = ga_ref[...]
        ya = ov * _rstd(ov) * gna_ref[...] * (ga * _sigmoid(ga))
        ycat_ref[:, :D] = ya.astype(BF16)

        xl = xl_ref[...]
        halo = halo_s[...]
        xc = xl * cw_ref[3:4, :] + cb_ref[...]
        for j in range(3):
            xc = xc + _shift_down(xl, 3 - j, halo) * cw_ref[j:j + 1, :]
        halo_s[...] = xl_ref[tm - SUBLANES:tm, :]
        xc_ref[...] = xc

        _, ig, _, a, sq = _lru_gates(xc, wr_ref, br_ref, wi_ref, bi_ref, lam_ref)
        u = sq * (ig * xc)
        a_cum, h_loc = _scan_fwd(a, u)
        hh = h_loc + a_cum * hc_s[...]
        h_ref[...] = hh
        hc_s[...] = h_ref[tm - 1:tm, :]

        gl = gl_ref[...]
        yl = hh * _rstd(hh) * gnl_ref[...] * (gl * _sigmoid(gl))
        ycat_ref[:, D:] = yl.astype(BF16)

    vec = _const_spec((1, D))
    wspec = _const_spec((NB, LANES, LANES))
    return pl.pallas_call(
        body, name="branches_fwd", grid=(T // tm,),
        in_specs=[_row_spec(tm, D)] * 4 + [vec, vec, _const_spec((4, D)), vec, wspec, vec, wspec, vec, vec],
        out_specs=[_row_spec(tm, DMIX), _row_spec(tm, D), _row_spec(tm, D)],
        out_shape=[jax.ShapeDtypeStruct((T, DMIX), BF16), jax.ShapeDtypeStruct((T, D), F32),
                   jax.ShapeDtypeStruct((T, D), F32)],
        scratch_shapes=[pltpu.VMEM((SUBLANES, D), F32), pltpu.VMEM((1, D), F32)],
        compiler_params=_cparams(("arbitrary",)),
    )(o, g_attn, x_lru, g_lru, gain_a, gain_l, conv_w, conv_b, w_r, b_r, w_i, b_i, lam)


def _tail(ycat, x, p, tgt, w_out, post_gain, w_ple, ple_gain, w_gate, b_gate):
    T = x.shape[0]
    tm = TM

    def body(ycat_ref, x_ref, p_ref, t_ref, wo_ref, pg_ref, wp_ref, eg_ref, wg_ref, bg_ref,
             dh1_ref, dycat_ref, dmix_ref, h1b_ref, dgp_ref, pb_ref, dpe_ref, acc_ref):
        @pl.when(pl.program_id(0) == 0)
        def _():
            acc_ref[...] = jnp.zeros_like(acc_ref)

        mix = _dot(ycat_ref[...], wo_ref[...])
        rstd_m = _rstd(mix)
        mhat = mix * rstd_m
        h1 = x_ref[...] + mhat * pg_ref[...]
        pb = p_ref[...].astype(BF16)
        pb_ref[...] = pb
        pe = _dot(pb, wp_ref[...])
        rstd_p = _rstd(pe)
        pehat = pe * rstd_p
        e = pehat * eg_ref[...]
        h1b = h1.astype(BF16)
        h1b_ref[...] = h1b
        gate = _sigmoid(_dot(h1b, wg_ref[...]) + bg_ref[...])
        diff = (h1 + gate * e) - t_ref[...]

        dy = diff * (1.0 / D)
        de = dy * gate
        dgp = (dy * e) * gate * (1.0 - gate)
        dgpb = dgp.astype(BF16)
        dgp_ref[...] = dgpb
        dh1 = dy + _dot_nt(dgpb, wg_ref[...])
        dh1_ref[...] = dh1
        dpe_ref[...] = _rms_bwd(de * eg_ref[...], pehat, rstd_p).astype(BF16)
        dmix = _rms_bwd(dh1 * pg_ref[...], mhat, rstd_m).astype(BF16)
        dmix_ref[...] = dmix
        dycat_ref[...] = _dot_nt(dmix, wo_ref[...])

        acc_ref[0:1, :] += jnp.sum(dh1 * mhat, axis=0, keepdims=True)
        acc_ref[1:2, :] += jnp.sum(de * pehat, axis=0, keepdims=True)
        acc_ref[2:3, :] += jnp.sum(dgp, axis=0, keepdims=True)
        acc_ref[3:4, :] += jnp.sum(diff * diff, axis=0, keepdims=True) * (0.5 / D)

    vec = _const_spec((1, D))
    bf = jax.ShapeDtypeStruct((T, D), BF16)
    return pl.pallas_call(
        body, name="tail", grid=(T // tm,),
        in_specs=[_row_spec(tm, DMIX), _row_spec(tm, D), _row_spec(tm, DPLE), _row_spec(tm, D),
                  _const_spec((DMIX, D)), vec, _const_spec((DPLE, D)), vec, _const_spec((D, D)), vec],
        out_specs=[_row_spec(tm, D), _row_spec(tm, DMIX), _row_spec(tm, D), _row_spec(tm, D), _row_spec(tm, D),
                   _row_spec(tm, DPLE), _row_spec(tm, D), _const_spec((SUBLANES, D))],
        out_shape=[jax.ShapeDtypeStruct((T, D), F32), jax.ShapeDtypeStruct((T, DMIX), F32), bf, bf, bf,
                   jax.ShapeDtypeStruct((T, DPLE), BF16), bf, jax.ShapeDtypeStruct((SUBLANES, D), F32)],
        compiler_params=_cparams(("arbitrary",), VMEM_BIG),
    )(ycat, x, p, tgt, w_out, post_gain, w_ple, ple_gain, w_gate, b_gate)


def _branches_bwd(dycat, o, g_attn, h, g_lru, gain_a, gain_l):
    T = o.shape[0]
    tm = TM

    def body(dy_ref, o_ref, ga_ref, h_ref, gl_ref, gna_ref, gnl_ref,
             do_ref, dd_ref, dga_ref, dgl_ref, dh_ref, acc_ref):
        @pl.when(pl.program_id(0) == 0)
        def _():
            acc_ref[...] = jnp.zeros_like(acc_ref)

        def branch(val, g, gain, dyv):
            rstd = _rstd(val)
            vhat = val * rstd
            sig = _sigmoid(g)
            dn = dyv * (g * sig)
            dg = dyv * (vhat * gain) * (sig * (1.0 + g * (1.0 - sig)))
            dgain = jnp.sum(dn * vhat, axis=0, keepdims=True)
            return _rms_bwd(dn * gain, vhat, rstd), dg, dgain

        ov = o_ref[...]
        do, dga, dgain_a = branch(ov, ga_ref[...], gna_ref[...], dy_ref[:, :D])
        do_ref[...] = do.astype(BF16)
        dga_ref[...] = dga.astype(BF16)
        prod = do * ov
        lane = lax.broadcasted_iota(jnp.int32, (tm, LANES), 1)
        dd = jnp.zeros((tm, LANES), F32)
        for hd in range(H):
            dd = jnp.where(lane == hd, jnp.sum(prod[:, hd * DH:(hd + 1) * DH], axis=1, keepdims=True), dd)
        dd_ref[...] = dd

        dh, dgl, dgain_l = branch(h_ref[...], gl_ref[...], gnl_ref[...], dy_ref[:, D:])
        dh_ref[...] = dh
        dgl_ref[...] = dgl.astype(BF16)
        acc_ref[0:1, :] += dgain_a
        acc_ref[1:2, :] += dgain_l

    vec = _const_spec((1, D))
    bf = jax.ShapeDtypeStruct((T, D), BF16)
    return pl.pallas_call(
        body, name="branches_bwd", grid=(T // tm,),
        in_specs=[_row_spec(tm, DMIX)] + [_row_spec(tm, D)] * 4 + [vec, vec],
        out_specs=[_row_spec(tm, D), _row_spec(tm, LANES), _row_spec(tm, D), _row_spec(tm, D), _row_spec(tm, D),
                   _const_spec((SUBLANES, D))],
        out_shape=[bf, jax.ShapeDtypeStruct((T, LANES), F32), bf, bf, jax.ShapeDtypeStruct((T, D), F32),
                   jax.ShapeDtypeStruct((SUBLANES, D), F32)],
        compiler_params=_cparams(("arbitrary",)),
    )(dycat, o, g_attn, h, g_lru, gain_a, gain_l)


def _lru_bwd(dh, h, xc, x_lru, conv_w, w_r, b_r, w_i, b_i, lam):
    T = dh.shape[0]
    tm = TM
    nt = T // tm
    per = tm // SUBLANES

    def body(dh_ref, h_ref, hprev_ref, xc_ref, xl_ref, xlprev_ref, cw_ref, wr_ref, br_ref, wi_ref, bi_ref, lam_ref,
             dxl_ref, dwr_ref, dwi_ref, acc_ref, carry_s, dxc_next_s, top_s):
        i = pl.program_id(0)

        @pl.when(i == 0)
        def _():
            acc_ref[...] = jnp.zeros_like(acc_ref)
            dwr_ref[...] = jnp.zeros_like(dwr_ref)
            dwi_ref[...] = jnp.zeros_like(dwi_ref)
            carry_s[...] = jnp.zeros_like(carry_s)
            dxc_next_s[...] = jnp.zeros_like(dxc_next_s)

        inner = jnp.where(i == nt - 1, 0.0, 1.0)
        xc = xc_ref[...]
        r, ig, sp, a, sq = _lru_gates(xc, wr_ref, br_ref, wi_ref, bi_ref, lam_ref)

        row = lax.broadcasted_iota(jnp.int32, (tm, D), 0)
        u = dh_ref[...] + jnp.where(row == tm - 1, carry_s[...], 0.0)
        dht = _scan_bwd(pltpu.roll(a, tm - 1, 0), u)
        top_s[...] = a[:SUBLANES, :] * dht[:SUBLANES, :]
        carry_s[...] = top_s[0:1, :]

        hprev = hprev_ref[...] * inner
        da = dht * _shift_down(h_ref[...], 1, hprev)
        dig = dht * sq * xc
        dxc = dht * sq * ig
        dsq = dht * ig * xc
        dla = da * a - dsq * (a * a) / sq
        dr = dla * ((-LRU_C) * sp)
        dpr = dr * r * (1.0 - r)
        dpi = dig * ig * (1.0 - ig)
        for n in range(NB):
            blk = slice(n * LANES, (n + 1) * LANES)
            xcb = xc[:, blk].astype(BF16)
            dwr_ref[n] += _dot_tn(xcb, dpr[:, blk].astype(BF16))
            dwi_ref[n] += _dot_tn(xcb, dpi[:, blk].astype(BF16))
        dxc = dxc + _gate_pre_t(dpr, wr_ref) + _gate_pre_t(dpi, wi_ref)

        xl = xl_ref[...]
        xlprev = xlprev_ref[...] * inner
        nxt = dxc_next_s[...]
        dxl = dxc * cw_ref[3:4, :]
        acc_ref[3:4, :] += jnp.sum(dxc * xl, axis=0, keepdims=True)
        for j in range(3):
            dxl = dxl + _shift_up(dxc, 3 - j, nxt) * cw_ref[j:j + 1, :]
            acc_ref[j:j + 1, :] += jnp.sum(dxc * _shift_down(xl, 3 - j, xlprev), axis=0, keepdims=True)
        dxc_next_s[...] = dxc[:SUBLANES, :]
        dxl_ref[...] = dxl.astype(BF16)

        acc_ref[4:5, :] += jnp.sum(dxc, axis=0, keepdims=True)
        acc_ref[5:6, :] += jnp.sum(dpr, axis=0, keepdims=True)
        acc_ref[6:7, :] += jnp.sum(dpi, axis=0, keepdims=True)
        acc_ref[7:8, :] += jnp.sum(dla * ((-LRU_C) * r), axis=0, keepdims=True)

        @pl.when(i == nt - 1)
        def _():
            lam_v = lam_ref[...]
            acc_ref[7:8, :] = acc_ref[7:8, :] * (-_sigmoid(-lam_v))

    rev = pl.BlockSpec((tm, D), lambda i: (nt - 1 - i, 0))
    prev8 = pl.BlockSpec((SUBLANES, D), lambda i: (jnp.maximum((nt - 1 - i) * per - 1, 0), 0))
    vec = _const_spec((1, D))
    wspec = _const_spec((NB, LANES, LANES))
    bf = jax.ShapeDtypeStruct((T, D), BF16)
    return pl.pallas_call(
        body, name="lru_bwd", grid=(nt,),
        in_specs=[rev, rev, prev8, rev, rev, prev8, _const_spec((4, D)), wspec, vec, wspec, vec, vec],
        out_specs=[rev, wspec, wspec, _const_spec((SUBLANES, D))],
        out_shape=[bf, jax.ShapeDtypeStruct((NB, LANES, LANES), F32), jax.ShapeDtypeStruct((NB, LANES, LANES), F32),
                   jax.ShapeDtypeStruct((SUBLANES, D), F32)],
        scratch_shapes=[pltpu.VMEM((1, D), F32), pltpu.VMEM((SUBLANES, D), F32), pltpu.VMEM((SUBLANES, D), F32)],
        compiler_params=_cparams(("arbitrary",)),
    )(dh, h, h, xc, x_lru, x_lru, conv_w, w_r, b_r, w_i, b_i, lam)


def _attn_bwd(q, k, v, do, bq_row, dd_row, c_col):
    T = q.shape[0]
    t = TA
    n = T // t
    scale = DH ** -0.5

    def body(q_ref, k_ref, v_ref, do_ref, bq_ref, dd_ref, ck_ref,
             dq_ref, dk_ref, dv_ref, dc_ref, dcq_ref, dq_s, dk_s, dv_s, dc_s, dcq_s):
        ki = pl.program_id(1)
        qi = pl.program_id(2)

        @pl.when((ki == 0) & (qi == 0))
        def _():
            dq_s[...] = jnp.zeros_like(dq_s)
            dcq_s[...] = jnp.zeros_like(dcq_s)

        @pl.when(qi == 0)
        def _():
            dk_s[...] = jnp.zeros_like(dk_s)
            dv_s[...] = jnp.zeros_like(dv_s)
            dc_s[...] = jnp.zeros_like(dc_s)

        @pl.when(qi >= ki)
        def _():
            qv = q_ref[...]
            kv = k_ref[...]
            dov = do_ref[...]
            st = _dot_nt(kv, qv) * scale + (bq_ref[0] - ck_ref[0])
            krow = ki * t + lax.broadcasted_iota(jnp.int32, (t, t), 0)
            qcol = qi * t + lax.broadcasted_iota(jnp.int32, (t, t), 1)
            pt = jnp.exp(jnp.where(krow <= qcol, st, NEG))
            dv_s[...] += _dot(pt.astype(BF16), dov)
            dst = pt * (_dot_nt(v_ref[...], dov) - dd_ref[0])
            dsb = dst.astype(BF16)
            dk_s[...] += _dot(dsb, qv)
            off = pl.multiple_of(qi * t, t)
            dq_s[pl.ds(off, t), :] += _dot_tn(dsb, kv)
            dc_s[...] -= jnp.sum(dst, axis=1, keepdims=True)
            dcq_s[:, pl.ds(off, t)] += jnp.sum(dst, axis=0, keepdims=True)

        @pl.when(qi == n - 1)
        def _():
            dk_ref[...] = (dk_s[...] * scale).astype(BF16)
            dv_ref[...] = dv_s[...].astype(BF16)
            dc_ref[0] = dc_s[...]

        @pl.when((ki == n - 1) & (qi == n - 1))
        def _():
            dq_ref[...] = (dq_s[...] * scale).astype(BF16)
            dcq_ref[0] = dcq_s[...]

    qside = pl.BlockSpec((t, DH), lambda h, ki, qi: (jnp.maximum(qi, ki), h))
    kside = pl.BlockSpec((t, DH), lambda h, ki, qi: (ki, h))
    qrow = pl.BlockSpec((1, 1, t), lambda h, ki, qi: (h, 0, jnp.maximum(qi, ki)))
    kcol = pl.BlockSpec((1, t, 1), lambda h, ki, qi: (h, ki, 0))
    bf = jax.ShapeDtypeStruct((T, D), BF16)
    return pl.pallas_call(
        body, name="attn_bwd", grid=(H, n, n),
        in_specs=[qside, kside, kside, qside, qrow, qrow, kcol],
        out_specs=[pl.BlockSpec((T, DH), lambda h, ki, qi: (0, h)), kside, kside, kcol,
                   pl.BlockSpec((1, 1, T), lambda h, ki, qi: (h, 0, 0))],
        out_shape=[bf, bf, bf, jax.ShapeDtypeStruct((H, T, 1), F32), jax.ShapeDtypeStruct((H, 1, T), F32)],
        scratch_shapes=[pltpu.VMEM((T, DH), F32), pltpu.VMEM((t, DH), F32), pltpu.VMEM((t, DH), F32),
                        pltpu.VMEM((t, 1), F32), pltpu.VMEM((1, T), F32)],
        compiler_params=_cparams(("arbitrary", "arbitrary", "arbitrary")),
    )(q, k, v, do, bq_row, dd_row, c_col)


def _fgate_bwd(dc_key, dc_query, flb):
    T = dc_key.shape[0]
    tm = TM
    nt = T // tm

    def body(dck_ref, dcq_ref, flb_ref, dfl_ref, acc_ref, carry, top_s):
        @pl.when(pl.program_id(0) == 0)
        def _():
            carry[...] = jnp.zeros_like(carry)
            acc_ref[...] = jnp.zeros_like(acc_ref)

        r = lax.broadcasted_iota(jnp.int32, (tm, tm), 0)
        c = lax.broadcasted_iota(jnp.int32, (tm, tm), 1)
        dls = _dot_exact((c >= r).astype(F32), dck_ref[...] + dcq_ref[...]) + carry[...]
        top_s[...] = dls[:SUBLANES, :]
        carry[...] = top_s[0:1, :]
        flb = flb_ref[...]
        lane = lax.broadcasted_iota(jnp.int32, flb.shape, 1)
        dfl = jnp.where(lane < H, dls * _sigmoid(-flb), 0.0)
        dfl_ref[...] = dfl.astype(BF16)
        acc_ref[0:1, :] += jnp.sum(dfl, axis=0, keepdims=True)

    rev = pl.BlockSpec((tm, LANES), lambda i: (nt - 1 - i, 0))
    return pl.pallas_call(
        body, name="fgate_bwd", grid=(nt,),
        in_specs=[rev, rev, rev],
        out_specs=[rev, _const_spec((SUBLANES, LANES))],
        out_shape=[jax.ShapeDtypeStruct((T, LANES), BF16), jax.ShapeDtypeStruct((SUBLANES, LANES), F32)],
        scratch_shapes=[pltpu.VMEM((1, LANES), F32), pltpu.VMEM((SUBLANES, LANES), F32)],
        compiler_params=_cparams(("arbitrary",)),
    )(dc_key, dc_query, flb)


def _dx(dz, dfl, w_a, w_f, w_b, x, pre_gain, dh1):
    T = x.shape[0]
    tm = TM

    def body(*refs):
        dz_refs = refs[:6]
        dfl_ref, wa_ref, wf_ref, wb_ref, x_ref, g_ref, dh1_ref, gx_ref, acc_ref = refs[6:]

        @pl.when(pl.program_id(0) == 0)
        def _():
            acc_ref[...] = jnp.zeros_like(acc_ref)

        dxn = _dot(dfl_ref[...], wf_ref[...])
        for s in range(3):
            dxn = dxn + _dot(dz_refs[s][...], wa_ref[s * D:(s + 1) * D, :])
            dxn = dxn + _dot(dz_refs[3 + s][...], wb_ref[s * D:(s + 1) * D, :])
        xv = x_ref[...]
        rstd = _rstd(xv)
        xhat = xv * rstd
        gx_ref[...] = dh1_ref[...] + _rms_bwd(dxn * g_ref[...], xhat, rstd)
        acc_ref[0:1, :] += jnp.sum(dxn * xhat, axis=0, keepdims=True)

    return pl.pallas_call(
        body, name="dx", grid=(T // tm,),
        in_specs=[_row_spec(tm, D)] * 6 + [_row_spec(tm, LANES), _const_spec((3 * D, D)), _const_spec((LANES, D)),
                                           _const_spec((3 * D, D)), _row_spec(tm, D), _const_spec((1, D)),
                                           _row_spec(tm, D)],
        out_specs=[_row_spec(tm, D), _const_spec((SUBLANES, D))],
        out_shape=[jax.ShapeDtypeStruct((T, D), F32), jax.ShapeDtypeStruct((SUBLANES, D), F32)],
        compiler_params=_cparams(("arbitrary",), VMEM_BIG),
    )(*dz, dfl, w_a, w_f, w_b, x, pre_gain, dh1)


GRAD_ROWS = D_IN + SUBLANES


def _seg_row(s):
    return (s * (D // SUBLANES) + jnp.where(s >= 3, H // SUBLANES, 0)) * SUBLANES


def _dw_in_t(dz, dfl, xn, bt=512):
    T = xn.shape[0]
    nt = T // bt

    def main_body(*refs):
        dz_refs, xn_ref, o_ref = refs[:6], refs[6], refs[7]
        s = pl.program_id(0)

        @pl.when(pl.program_id(1) == 0)
        def _():
            o_ref[...] = jnp.zeros_like(o_ref)

        for k in range(6):
            @pl.when(s == k)
            def _(k=k):
                o_ref[...] += _dot_tn(dz_refs[k][...], xn_ref[...])

    def dz_spec(k):
        return pl.BlockSpec((bt, D), lambda s, t: (jnp.where(s == k, t, 0), 0))

    main = pl.pallas_call(
        main_body, name="dw_in_main", grid=(6, nt),
        in_specs=[dz_spec(k) for k in range(6)] + [pl.BlockSpec((bt, D), lambda s, t: (t, 0))],
        out_specs=pl.BlockSpec((pl.Element(D), pl.Element(D)), lambda s, t: (_seg_row(s), 0)),
        out_shape=jax.ShapeDtypeStruct((GRAD_ROWS, D), F32),
        compiler_params=_cparams(("arbitrary", "arbitrary")),
    )(*dz, xn)

    def f_body(dfl_ref, xn_ref, main_ref, o_ref, acc_s):
        p = pl.program_id(0)
        t = pl.program_id(1)

        @pl.when(t == 0)
        def _():
            acc_s[...] = jnp.zeros_like(acc_s)

        @pl.when(p == 0)
        def _():
            acc_s[...] += _dot_tn(dfl_ref[...], xn_ref[...])

        @pl.when(t == nt - 1)
        def _():
            o_ref[...] = acc_s[:SUBLANES, :]

    fl_block = FL0 // SUBLANES
    end_block = D_IN // SUBLANES
    return pl.pallas_call(
        f_body, name="dw_in_f", grid=(2, nt),
        in_specs=[pl.BlockSpec((bt, LANES), lambda p, t: (t, 0)), pl.BlockSpec((bt, D), lambda p, t: (t, 0)),
                  pl.BlockSpec(memory_space=pl.ANY)],
        out_specs=pl.BlockSpec((SUBLANES, D), lambda p, t: (fl_block + p * (end_block - fl_block), 0)),
        out_shape=jax.ShapeDtypeStruct((GRAD_ROWS, D), F32),
        scratch_shapes=[pltpu.VMEM((LANES, D), F32)],
        input_output_aliases={2: 0},
        compiler_params=_cparams(("arbitrary", "arbitrary")),
    )(dfl, xn, main)


def _matmul_tn(a, b, name, bm=512, bn=1024, bt=512):
    T, M = a.shape
    N = b.shape[1]
    bm, bn, bt = min(bm, M), min(bn, N), min(bt, T)

    def body(a_ref, b_ref, o_ref):
        @pl.when(pl.program_id(2) == 0)
        def _():
            o_ref[...] = jnp.zeros_like(o_ref)

        o_ref[...] += _dot_tn(a_ref[...], b_ref[...])

    return pl.pallas_call(
        body, name=name, grid=(M // bm, N // bn, T // bt),
        in_specs=[pl.BlockSpec((bt, bm), lambda i, j, t: (t, i)), pl.BlockSpec((bt, bn), lambda i, j, t: (t, j))],
        out_specs=pl.BlockSpec((bm, bn), lambda i, j, t: (i, j)),
        out_shape=jax.ShapeDtypeStruct((M, N), F32),
        compiler_params=_cparams(("parallel", "parallel", "arbitrary")),
    )(a, b)


HBM_SPEC = pl.BlockSpec(memory_space=pltpu.HBM)
VMEM_SPEC = pl.BlockSpec(memory_space=pltpu.VMEM)


def _position():
    return lax.axis_index("x"), lax.axis_index("y"), lax.axis_index("c")


def _other_chips(x, y):
    return [(1 - x, y), (x, 1 - y), (1 - x, 1 - y)]


def _gather_shards(shards, whole):
    na, nw = len(shards), len(whole)
    nall = na + nw

    def body(*refs):
        srcs, dsts = refs[:nall], refs[nall:2 * nall]
        ici_send, ici_recv, d2d_send, d2d_recv, loc_sems = refs[2 * nall:]
        x, y, c = _position()
        chip = 2 * x + y
        chips = _other_chips(x, y)
        local = [pltpu.make_async_copy(srcs[a], dsts[a].at[chip], loc_sems.at[a]) for a in range(nall)]
        for cp in local:
            cp.start()

        def half(a, which):
            rows = srcs[a].shape[0] // 2
            return pl.ds(pl.multiple_of(which * rows, 16), rows)

        first = []
        for j, (px, py) in enumerate(chips):
            for a in range(nall):
                src = srcs[a].at[half(a, c), :] if a < na else srcs[a]
                dst = dsts[a].at[chip, half(a, c), :] if a < na else dsts[a].at[chip]
                first.append(pltpu.make_async_remote_copy(
                    src_ref=src, dst_ref=dst, send_sem=ici_send.at[j * nall + a], recv_sem=ici_recv.at[j * nall + a],
                    device_id=(px, py, c), device_id_type=MESH))
        for cp in first:
            cp.start()

        passed = []
        for j, (px, py) in enumerate(chips):
            theirs = 2 * px + py
            for a in range(nall):
                if a < na:
                    landed = dsts[a].at[theirs, half(a, c), :]
                    fwd = pltpu.make_async_remote_copy(
                        src_ref=landed, dst_ref=landed, send_sem=d2d_send.at[j * na + a],
                        recv_sem=d2d_recv.at[j * na + a], device_id=(x, y, 1 - c), device_id_type=MESH)
                else:
                    landed = dsts[a].at[theirs]
                pltpu.make_async_remote_copy(
                    src_ref=landed, dst_ref=landed, send_sem=ici_send.at[j * nall + a],
                    recv_sem=ici_recv.at[j * nall + a], device_id=(px, py, c), device_id_type=MESH).wait_recv()
                if a < na:
                    fwd.start()
                    passed.append(fwd)
        for j, (px, py) in enumerate(chips):
            theirs = 2 * px + py
            for a in range(na):
                other = dsts[a].at[theirs, half(a, 1 - c), :]
                pltpu.make_async_remote_copy(
                    src_ref=other, dst_ref=other, send_sem=d2d_send.at[j * na + a], recv_sem=d2d_recv.at[j * na + a],
                    device_id=(x, y, 1 - c), device_id_type=MESH).wait_recv()
        for cp in first + passed:
            cp.wait_send()
        for cp in local:
            cp.wait()

    arrs = list(shards) + list(whole)
    return pl.pallas_call(
        body, name="gather_shards",
        in_specs=[HBM_SPEC] * nall, out_specs=[HBM_SPEC] * nall,
        out_shape=[jax.ShapeDtypeStruct((N_CHIPS,) + s.shape, s.dtype) for s in arrs],
        scratch_shapes=[pltpu.SemaphoreType.DMA((3 * nall,)), pltpu.SemaphoreType.DMA((3 * nall,)),
                        pltpu.SemaphoreType.DMA((3 * na,)), pltpu.SemaphoreType.DMA((3 * na,)),
                        pltpu.SemaphoreType.DMA((nall,))],
    )(*arrs)


W_ROWS = 1568
G_ROWS = 1552
SHARD_ROWS = D_IN // N_CHIPS
WINDOW_STEP = 1536


def _assemble_w_in(cont):
    cb = 256
    half = WINDOW_STEP

    def body(c_ref, wa_ref, wf_ref, wb_ref):
        wa_ref[:half, :] = c_ref[0, :half, :]
        seam = c_ref[0, half:half + 16, :].astype(F32) + c_ref[1, :16, :].astype(F32)
        wa_ref[half:half + 16, :] = seam.astype(BF16)
        wa_ref[half + 16:, :] = c_ref[1, 16:half, :]

        fl = c_ref[1, half:half + 16, :].astype(F32) + c_ref[2, :16, :].astype(F32)
        row = lax.broadcasted_iota(jnp.int32, fl.shape, 0)
        wf_ref[:16, :] = jnp.where(row < H, fl, 0.0).astype(BF16)
        wf_ref[16:, :] = jnp.zeros((LANES - 16, cb), BF16)

        x2 = c_ref[2].astype(F32)
        x3 = c_ref[3].astype(F32)
        mid = x2[half:half + SUBLANES] + x3[:SUBLANES]
        wb = jnp.concatenate([x2[SUBLANES:half], mid, x3[SUBLANES:half + SUBLANES]], axis=0)
        wb_ref[...] = wb.astype(BF16)

    return pl.pallas_call(
        body, name="assemble_w_in", grid=(D // cb,),
        in_specs=[pl.BlockSpec((N_CHIPS, W_ROWS, cb), lambda i: (0, 0, i))],
        out_specs=[pl.BlockSpec((3 * D, cb), lambda i: (0, i)), pl.BlockSpec((LANES, cb), lambda i: (0, i)),
                   pl.BlockSpec((3 * D, cb), lambda i: (0, i))],
        out_shape=[jax.ShapeDtypeStruct((3 * D, D), BF16), jax.ShapeDtypeStruct((LANES, D), BF16),
                   jax.ShapeDtypeStruct((3 * D, D), BF16)],
        compiler_params=_cparams(("parallel",)),
    )(cont)


def _pair_exchange(grad_t, parts):
    na = len(parts)
    n = N_CHIPS + na
    half_g = G_ROWS // 2

    def body(*refs):
        g_ref, srcs = refs[0], refs[1:1 + na]
        mine, got = refs[1 + na:2 + 2 * na], refs[2 + 2 * na:3 + 3 * na]
        send_sems, recv_sems, loc_sems = refs[3 + 3 * na:]
        x, y, c = _position()
        pieces = []
        for j in range(N_CHIPS):
            def rows(which, j=j):
                return pl.ds(pl.multiple_of(j * WINDOW_STEP + which * half_g, SUBLANES), half_g)
            pieces.append((g_ref.at[rows(c), :], g_ref.at[rows(1 - c), :], mine[0].at[j], got[0].at[j]))
        for a in range(na):
            half = srcs[a].shape[1] // 2
            def rows(which, half=half):
                return pl.ds(pl.multiple_of(which * half, SUBLANES), half)
            pieces.append((srcs[a].at[:, rows(c), :], srcs[a].at[:, rows(1 - c), :], mine[1 + a], got[1 + a]))
        local, remote = [], []
        for k, (keep, give, mine_dst, got_dst) in enumerate(pieces):
            local.append(pltpu.make_async_copy(keep, mine_dst, loc_sems.at[k]))
            remote.append(pltpu.make_async_remote_copy(
                src_ref=give, dst_ref=got_dst, send_sem=send_sems.at[k], recv_sem=recv_sems.at[k],
                device_id=(x, y, 1 - c), device_id_type=MESH))
        for cp in remote + local:
            cp.start()
        for cp in remote:
            cp.wait()
        for cp in local:
            cp.wait()

    halves = [jax.ShapeDtypeStruct((N_CHIPS, half_g, D), F32)]
    halves += [jax.ShapeDtypeStruct((s.shape[0], s.shape[1] // 2, s.shape[2]), s.dtype) for s in parts]
    outs = pl.pallas_call(
        body, name="pair_exchange",
        in_specs=[HBM_SPEC] * (1 + na), out_specs=[HBM_SPEC] * (2 + 2 * na),
        out_shape=halves * 2,
        scratch_shapes=[pltpu.SemaphoreType.DMA((n,)), pltpu.SemaphoreType.DMA((n,)), pltpu.SemaphoreType.DMA((n,))],
    )(grad_t, *parts)
    return outs[:1 + na], outs[1 + na:]


def _pair_sum(mine, got, name):
    _, half, C = mine.shape
    cb = min(C, 256)

    def body(a_ref, b_ref, o_ref):
        o_ref[...] = (a_ref[...] + b_ref[...]).astype(BF16)

    spec = pl.BlockSpec((1, half, cb), lambda j, i: (j, 0, i))
    return pl.pallas_call(
        body, name=name, grid=(N_CHIPS, C // cb),
        in_specs=[spec, spec], out_specs=spec,
        out_shape=jax.ShapeDtypeStruct((N_CHIPS, half, C), BF16),
        compiler_params=_cparams(("parallel", "parallel")),
    )(mine, got)


def _chip_exchange(sums):
    na = len(sums)

    def body(*refs):
        srcs, dsts = refs[:na], refs[na:2 * na]
        send_sems, recv_sems, loc_sems = refs[2 * na:]
        x, y, c = _position()
        chip = 2 * x + y
        local = [pltpu.make_async_copy(srcs[a].at[chip], dsts[a].at[chip], loc_sems.at[a]) for a in range(na)]
        for cp in local:
            cp.start()
        copies = []
        for j, (px, py) in enumerate(_other_chips(x, y)):
            for a in range(na):
                copies.append(pltpu.make_async_remote_copy(
                    src_ref=srcs[a].at[2 * px + py], dst_ref=dsts[a].at[chip], send_sem=send_sems.at[j * na + a],
                    recv_sem=recv_sems.at[j * na + a], device_id=(px, py, c), device_id_type=MESH))
        for cp in copies:
            cp.start()
        for cp in copies:
            cp.wait()
        for cp in local:
            cp.wait()

    return pl.pallas_call(
        body, name="chip_exchange",
        in_specs=[HBM_SPEC] * na, out_specs=[HBM_SPEC] * na,
        out_shape=[jax.ShapeDtypeStruct(s.shape, s.dtype) for s in sums],
        scratch_shapes=[pltpu.SemaphoreType.DMA((3 * na,)), pltpu.SemaphoreType.DMA((3 * na,)),
                        pltpu.SemaphoreType.DMA((na,))],
    )(*sums)


def _chip_sum(got, name):
    _, half, C = got.shape
    cb = min(C, 256)

    def body(g_ref, o_ref):
        acc = g_ref[0].astype(F32) + g_ref[1].astype(F32)
        acc = acc + g_ref[2].astype(F32)
        o_ref[...] = acc + g_ref[3].astype(F32)

    return pl.pallas_call(
        body, name=name, grid=(C // cb,),
        in_specs=[pl.BlockSpec((N_CHIPS, half, cb), lambda i: (0, 0, i))],
        out_specs=pl.BlockSpec((half, cb), lambda i: (0, i)),
        out_shape=jax.ShapeDtypeStruct((half, C), F32),
        compiler_params=_cparams(("parallel",)),
    )(got)


def _pair_gather(halves):
    na = len(halves)

    def body(*refs):
        srcs, dsts = refs[:na], refs[na:2 * na]
        send_sems, recv_sems, loc_sems = refs[2 * na:]
        x, y, c = _position()
        local, copies = [], []
        for a in range(na):
            half = srcs[a].shape[0]
            rows = dsts[a].at[pl.ds(pl.multiple_of(c * half, SUBLANES), half), :]
            local.append(pltpu.make_async_copy(srcs[a], rows, loc_sems.at[a]))
            copies.append(pltpu.make_async_remote_copy(
                src_ref=srcs[a], dst_ref=rows, send_sem=send_sems.at[a], recv_sem=recv_sems.at[a],
                device_id=(x, y, 1 - c), device_id_type=MESH))
        for cp in local + copies:
            cp.start()
        for cp in copies:
            cp.wait()
        for cp in local:
            cp.wait()

    return pl.pallas_call(
        body, name="pair_gather",
        in_specs=[HBM_SPEC] * na, out_specs=[HBM_SPEC] * na,
        out_shape=[jax.ShapeDtypeStruct((2 * s.shape[0], s.shape[1]), s.dtype) for s in halves],
        scratch_shapes=[pltpu.SemaphoreType.DMA((na,)), pltpu.SemaphoreType.DMA((na,)),
                        pltpu.SemaphoreType.DMA((na,))],
    )(*halves)


def _allreduce_small(g):
    rows = g.shape[0]
    per = rows // N_DEV

    def body(g_ref, out_ref, got_ref, s1, r1, s2, r2):
        x, y, c = _position()
        me = 4 * x + 2 * y + c
        mine = pl.ds(pl.multiple_of(me * per, SUBLANES), per)
        peers = []
        for j in range(1, N_DEV):
            px = 1 - x if j & 4 else x
            py = 1 - y if j & 2 else y
            pc = 1 - c if j & 1 else c
            peers.append((px, py, pc))

        first = []
        for j, (px, py, pc) in enumerate(peers):
            theirs = pl.ds(pl.multiple_of((4 * px + 2 * py + pc) * per, SUBLANES), per)
            first.append(pltpu.make_async_remote_copy(
                src_ref=g_ref.at[theirs, :], dst_ref=got_ref.at[me], send_sem=s1.at[j], recv_sem=r1.at[j],
                device_id=(px, py, pc), device_id_type=MESH))
        for cp in first:
            cp.start()
        got_ref[me] = g_ref[mine, :]
        for cp in first:
            cp.wait()
        total = got_ref[0]
        for d in range(1, N_DEV):
            total = total + got_ref[d]
        out_ref[mine, :] = total

        second = []
        for j, peer in enumerate(peers):
            second.append(pltpu.make_async_remote_copy(
                src_ref=out_ref.at[mine, :], dst_ref=out_ref.at[mine, :], send_sem=s2.at[j], recv_sem=r2.at[j],
                device_id=peer, device_id_type=MESH))
        for cp in second:
            cp.start()
        for cp in second:
            cp.wait()

    sems = pltpu.SemaphoreType.DMA((N_DEV - 1,))
    return pl.pallas_call(
        body, name="allreduce_small",
        in_specs=[VMEM_SPEC], out_specs=VMEM_SPEC,
        out_shape=jax.ShapeDtypeStruct(g.shape, F32),
        scratch_shapes=[pltpu.VMEM((N_DEV, per, LANES), F32), sems, sems, sems, sems],
    )(g)


def _adamw_math(g, w, m, v):
    m2 = ADAM_B1 * m + (1.0 - ADAM_B1) * g
    v2 = ADAM_B2 * v + (1.0 - ADAM_B2) * (g * g)
    m_hat = m2 / (1.0 - ADAM_B1 ** ADAM_STEP)
    v_hat = v2 / (1.0 - ADAM_B2 ** ADAM_STEP)
    delta = (-ADAM_LR) * (m_hat / (jnp.sqrt(v_hat) + ADAM_EPS) + ADAM_WD * w)
    return delta, m2, v2


def _adamw_big(g, w, m, v, name):
    R, C = g.shape
    cb = min(C, LANES)

    def body(g_ref, w_ref, m_ref, v_ref, d_ref, m2_ref, v2_ref):
        d_ref[...], m2_ref[...], v2_ref[...] = _adamw_math(g_ref[...], w_ref[...], m_ref[...], v_ref[...])

    spec = pl.BlockSpec((R, cb), lambda i: (0, i))
    out = jax.ShapeDtypeStruct((R, C), F32)
    return pl.pallas_call(
        body, name=name, grid=(C // cb,),
        in_specs=[spec] * 4, out_specs=[spec] * 3, out_shape=[out] * 3,
        compiler_params=_cparams(("parallel",)),
    )(g, w, m, v)


def _adamw_small(gs, ws, ms, vs):
    n = len(gs)

    def body(*refs):
        for a in range(n):
            g_ref, w_ref, m_ref, v_ref = (refs[k * n + a] for k in range(4))
            d_ref, m2_ref, v2_ref = (refs[(4 + k) * n + a] for k in range(3))
            d_ref[...], m2_ref[...], v2_ref[...] = _adamw_math(g_ref[...], w_ref[...], m_ref[...], v_ref[...])

    outs = [jax.ShapeDtypeStruct(w.shape, F32) for w in ws]
    return pl.pallas_call(
        body, name="adamw_small",
        in_specs=[VMEM_SPEC] * (4 * n), out_specs=[VMEM_SPEC] * (3 * n), out_shape=outs * 3,
    )(*gs, *ws, *ms, *vs)


def _local_step(x, p, tgt, w_a, w_f, w_b, w_out_b, w_ple_b, w_gate_b, conv_w, b_f, pre_gain, post_gain, conv_b,
                w_rgate, b_rgate, w_igate, b_igate, lam, gain_a, gain_l, ple_gain, b_gate):
    b_f_pad = jnp.pad(b_f, ((0, 0), (0, LANES - H)))
    w_r = w_rgate.astype(BF16)
    w_i = w_igate.astype(BF16)

    xn, q, k, v, g_attn, x_lru, g_lru, flb, csum = _in_proj(x, pre_gain, w_a, w_f, w_b, b_f_pad)
    c_hm = csum[:, :H].T
    c_col, c_row = c_hm[:, :, None], c_hm[:, None, :]
    o, bq = _attn_fwd(q, k, v, c_col, c_row)
    ycat, xc, h = _branches_fwd(o, g_attn, x_lru, g_lru, gain_a, gain_l, conv_w, conv_b, w_r, b_rgate, w_i, b_igate,
                                lam)
    dh1, dycat, dmix, h1b, dgp, pb, dpe, acc_t = _tail(ycat, x, p, tgt, w_out_b, post_gain, w_ple_b, ple_gain,
                                                       w_gate_b, b_gate)
    do, dd, dg_attn, dg_lru, dh, acc_b = _branches_bwd(dycat, o, g_attn, h, g_lru, gain_a, gain_l)
    dx_lru, gw_r, gw_i, acc_l = _lru_bwd(dh, h, xc, x_lru, conv_w, w_r, b_rgate, w_i, b_igate, lam)
    dd_row = dd[:, :H].T[:, None, :]
    bq_row = bq[:, :, 0][:, None, :]
    dq, dk, dv, dc, dcq = _attn_bwd(q, k, v, do, bq_row, dd_row, c_col)
    pad_heads = ((0, 0), (0, LANES - H))
    dfl, acc_f = _fgate_bwd(jnp.pad(dc[:, :, 0].T, pad_heads), jnp.pad(dcq[:, 0, :].T, pad_heads), flb)
    dz = (dq, dk, dv, dg_attn, dx_lru, dg_lru)
    grad_x, acc_x = _dx(dz, dfl, w_a, w_f, w_b, x, pre_gain, dh1)

    grads = dict(
        w_in_t=_dw_in_t(dz, dfl, xn),
        w_out=_matmul_tn(ycat, dmix, "dw_out"),
        w_ple=_matmul_tn(pb, dpe, "dw_ple"),
        w_ple_gate=_matmul_tn(h1b, dgp, "dw_ple_gate"),
        w_rgate=gw_r,
        w_igate=gw_i,
        b_f=acc_f[0:1, :H],
        pre_gain=acc_x[0:1],
        post_gain=acc_t[0:1],
        conv_w=acc_l[0:4],
        conv_b=acc_l[4:5],
        b_rgate=acc_l[5:6],
        b_igate=acc_l[6:7],
        lru_lambda=acc_l[7:8],
        attn_out_gain=acc_b[0:1],
        lru_out_gain=acc_b[1:2],
        ple_gain=acc_t[1:2],
        b_ple_gate=acc_t[2:3],
    )
    loss = jnp.sum(acc_t[3])
    return loss, grad_x, grads


SMALL_ROWS = ["b_f", "pre_gain", "post_gain", "conv_w", "conv_b", "b_rgate", "b_igate", "lru_lambda",
              "attn_out_gain", "lru_out_gain", "ple_gain", "b_ple_gate"]
WEIGHTS = ["w_in", "b_f", "pre_gain", "post_gain", "conv_w", "conv_b", "w_rgate", "b_rgate", "w_igate", "b_igate",
           "lru_lambda", "attn_out_gain", "lru_out_gain", "w_out", "w_ple", "ple_gain", "w_ple_gate", "b_ple_gate"]
SHARDED = ["w_in", "w_out", "w_ple", "w_ple_gate"]


def _by_chip_cols(g):
    r, cols = g.shape
    return g.reshape(r, N_CHIPS, cols // N_CHIPS).transpose(1, 0, 2)


def _from_chip_cols(s):
    n, r, cols = s.shape
    return s.transpose(1, 0, 2).reshape(r, n * cols)


def kernel(x, p, w_in, b_f, pre_gain, post_gain, conv_w, conv_b, w_rgate, b_rgate, w_igate, b_igate, lru_lambda, attn_out_gain, lru_out_gain, w_out, w_ple, ple_gain, w_ple_gate, b_ple_gate, loss_target, m_w_in, m_b_f, m_pre_gain, m_post_gain, m_conv_w, m_conv_b, m_w_rgate, m_b_rgate, m_w_igate, m_b_igate, m_lru_lambda, m_attn_out_gain, m_lru_out_gain, m_w_out, m_w_ple, m_ple_gain, m_w_ple_gate, m_b_ple_gate, v_w_in, v_b_f, v_pre_gain, v_post_gain, v_conv_w, v_conv_b, v_w_rgate, v_b_rgate, v_w_igate, v_b_igate, v_lru_lambda, v_attn_out_gain, v_lru_out_gain, v_w_out, v_w_ple, v_ple_gain, v_w_ple_gate, v_b_ple_gate):
    w = dict(w_in=w_in, b_f=b_f, pre_gain=pre_gain, post_gain=post_gain, conv_w=conv_w, conv_b=conv_b,
             w_rgate=w_rgate, b_rgate=b_rgate, w_igate=w_igate, b_igate=b_igate, lru_lambda=lru_lambda,
             attn_out_gain=attn_out_gain, lru_out_gain=lru_out_gain, w_out=w_out, w_ple=w_ple, ple_gain=ple_gain,
             w_ple_gate=w_ple_gate, b_ple_gate=b_ple_gate)
    m = dict(w_in=m_w_in, b_f=m_b_f, pre_gain=m_pre_gain, post_gain=m_post_gain, conv_w=m_conv_w, conv_b=m_conv_b,
             w_rgate=m_w_rgate, b_rgate=m_b_rgate, w_igate=m_w_igate, b_igate=m_b_igate, lru_lambda=m_lru_lambda,
             attn_out_gain=m_attn_out_gain, lru_out_gain=m_lru_out_gain, w_out=m_w_out, w_ple=m_w_ple,
             ple_gain=m_ple_gain, w_ple_gate=m_w_ple_gate, b_ple_gate=m_b_ple_gate)
    v = dict(w_in=v_w_in, b_f=v_b_f, pre_gain=v_pre_gain, post_gain=v_post_gain, conv_w=v_conv_w, conv_b=v_conv_b,
             w_rgate=v_w_rgate, b_rgate=v_b_rgate, w_igate=v_w_igate, b_igate=v_b_igate, lru_lambda=v_lru_lambda,
             attn_out_gain=v_attn_out_gain, lru_out_gain=v_lru_out_gain, w_out=v_w_out, w_ple=v_w_ple,
             ple_gain=v_ple_gain, w_ple_gate=v_w_ple_gate, b_ple_gate=v_b_ple_gate)
    xi, yi, ci = _position()
    chip = 2 * xi + yi

    w_in_t, m_in_t, v_in_t = (jnp.swapaxes(t[0], 0, 1) for t in (w_in, m_w_in, v_w_in))
    window = lax.dynamic_update_slice(jnp.zeros((W_ROWS, D), BF16), w_in_t.astype(BF16), (2 * chip, 0))

    st_in, st_out, st_ple, st_gate, st_conv = _gather_shards(
        [window, w_out[0].astype(BF16), w_ple[0].astype(BF16), w_ple_gate[0].astype(BF16)], [conv_w[0]])
    w_a, w_f, w_b = _assemble_w_in(st_in)
    w_out_b = st_out.reshape(DMIX, D)
    w_ple_b = _from_chip_cols(st_ple)
    w_gate_b = st_gate.reshape(D, D)
    conv_full = _from_chip_cols(st_conv)

    loss, grad_x, g = _local_step(
        x[0], p[0, 0], loss_target[0], w_a, w_f, w_b, w_out_b, w_ple_b, w_gate_b, conv_full, b_f, pre_gain, post_gain,
        conv_b, w_rgate[0], b_rgate, w_igate[0], b_igate, lru_lambda, attn_out_gain, lru_out_gain, ple_gain,
        b_ple_gate)
    loss = lax.psum(loss, ("x", "y", "c"))

    parts = [g["w_out"].reshape(N_CHIPS, DMIX // N_CHIPS, D), _by_chip_cols(g["w_ple"]),
             g["w_ple_gate"].reshape(N_CHIPS, D // N_CHIPS, D)]
    mine, got = _pair_exchange(g["w_in_t"], parts)
    sums = [_pair_sum(mine[a], got[a], "pair_sum_%d" % a) for a in range(4)]
    recv = _chip_exchange(sums)
    halves = [_chip_sum(recv[a], "chip_sum_%d" % a) for a in range(4)]
    full = _pair_gather(halves)
    red = dict(zip(SHARDED, full))
    red["w_in"] = lax.dynamic_slice_in_dim(red["w_in"], 2 * chip, SHARD_ROWS, axis=0)

    rows = [jnp.pad(g["b_f"], ((0, 0), (0, D - H)))] + [g[n] for n in SMALL_ROWS[1:]]
    rows.append(jnp.zeros((16 - sum(r.shape[0] for r in rows), D), F32))
    packed = jnp.concatenate([g["w_rgate"].reshape(NB * LANES, LANES), g["w_igate"].reshape(NB * LANES, LANES),
                              jnp.concatenate(rows, axis=0).reshape(LANES, LANES)], axis=0)
    summed = _allreduce_small(packed)
    red["w_rgate"] = summed[:D].reshape(1, NB, LANES, LANES)
    red["w_igate"] = summed[D:2 * D].reshape(1, NB, LANES, LANES)
    vec = summed[2 * D:].reshape(16, D)
    r0 = 0
    for n in SMALL_ROWS:
        nr = 4 if n == "conv_w" else 1
        red[n] = vec[r0:r0 + nr]
        r0 += nr
    red["b_f"] = red["b_f"][:, :H]
    red["conv_w"] = lax.dynamic_slice_in_dim(red["conv_w"], chip * (D // N_CHIPS), D // N_CHIPS, axis=1)[None]

    delta, new_m, new_v = {}, {}, {}
    outs_in = _adamw_big(red["w_in"], w_in_t, m_in_t, v_in_t, "adamw_w_in")
    delta["w_in"], new_m["w_in"], new_v["w_in"] = (jnp.swapaxes(t, 0, 1)[None] for t in outs_in)
    red["w_in"] = jnp.swapaxes(red["w_in"], 0, 1)[None]
    for n in SHARDED[1:]:
        delta[n], new_m[n], new_v[n] = (t[None] for t in _adamw_big(red[n], w[n][0], m[n][0], v[n][0], "adamw_" + n))
        red[n] = red[n][None]
    small = [n for n in WEIGHTS if n not in SHARDED]
    outs = _adamw_small([red[n] for n in small], [w[n] for n in small], [m[n] for n in small],
                        [v[n] for n in small])
    ns = len(small)
    for a, n in enumerate(small):
        delta[n], new_m[n], new_v[n] = outs[a], outs[ns + a], outs[2 * ns + a]

    return (loss, grad_x[None], *[red[n] for n in WEIGHTS], *[delta[n] for n in WEIGHTS],
            *[new_m[n] for n in WEIGHTS], *[new_v[n] for n in WEIGHTS])
```

```python
import functools

import jax
import jax.numpy as jnp
from jax import lax
from jax.experimental import pallas as pl
from jax.experimental.pallas import tpu as pltpu

F32 = jnp.float32
BF16 = jnp.bfloat16

D = 1024
H = 8
DH = 128
NB = 8
DPLE = 256
DMIX = 2 * D
D_IN = 4 * D + H + 2 * D
FL0 = 3 * D
RMS_EPS = 1e-6
LRU_C = 8.0
NEG = -1e30
LANES = 128
SUBLANES = 8

ADAM_LR = 0.001
ADAM_B1 = 0.9
ADAM_B2 = 0.999
ADAM_EPS = 1e-08
ADAM_WD = 0.01
ADAM_STEP = 10

TM = 256
TA = 512
VMEM_BIG = 56 * 1024 * 1024
VMEM_MID = 40 * 1024 * 1024

MESH = pl.DeviceIdType.MESH
N_CHIPS = 4
N_DEV = 8


def _cparams(sem, vmem=VMEM_MID):
    return pltpu.CompilerParams(dimension_semantics=sem, vmem_limit_bytes=vmem)


def _sigmoid(x):
    return 1.0 / (1.0 + jnp.exp(-x))


def _rstd(x):
    return lax.rsqrt(jnp.mean(x * x, axis=-1, keepdims=True) + RMS_EPS)


def _rms_bwd(t, xhat, rstd):
    return rstd * (t - xhat * jnp.mean(t * xhat, axis=-1, keepdims=True))


def _dot(a, b):
    return jnp.dot(a, b, preferred_element_type=F32)


def _dot_nt(a, b):
    return lax.dot_general(a, b, (((1,), (1,)), ((), ())), preferred_element_type=F32)


def _dot_tn(a, b):
    return lax.dot_general(a, b, (((0,), (0,)), ((), ())), preferred_element_type=F32)


def _dot_exact(a, b):
    return jnp.dot(a, b, preferred_element_type=F32, precision=lax.Precision.HIGHEST)


def _neg_expm1(x):
    series = x * (1.0 + x * 0.5 * (1.0 + x * (1.0 / 3.0) * (1.0 + x * 0.25 * (1.0 + x * 0.2 * (1.0 + x * (1.0 / 6.0))))))
    return -jnp.where(x > -0.25, series, jnp.exp(x) - 1.0)


def _shift_down(x, j, halo):
    rolled = pltpu.roll(x, j, 0)
    row = lax.broadcasted_iota(jnp.int32, halo.shape, 0)
    top = jnp.where(row < j, pltpu.roll(halo, j, 0), rolled[:SUBLANES])
    return jnp.concatenate([top, rolled[SUBLANES:]], axis=0)


def _shift_up(x, j, nxt):
    tm = x.shape[0]
    rolled = pltpu.roll(x, tm - j, 0)
    row = lax.broadcasted_iota(jnp.int32, nxt.shape, 0)
    bot = jnp.where(row >= SUBLANES - j, pltpu.roll(nxt, SUBLANES - j, 0), rolled[tm - SUBLANES:])
    return jnp.concatenate([rolled[:tm - SUBLANES], bot], axis=0)


def _scan_fwd(a, u):
    tm = a.shape[0]
    row = lax.broadcasted_iota(jnp.int32, a.shape, 0)
    d = 1
    while d < tm:
        keep = row >= d
        a_s = jnp.where(keep, pltpu.roll(a, d, 0), 1.0)
        u_s = jnp.where(keep, pltpu.roll(u, d, 0), 0.0)
        u = u + a * u_s
        a = a * a_s
        d *= 2
    return a, u


def _scan_bwd(b, u):
    tm = b.shape[0]
    row = lax.broadcasted_iota(jnp.int32, b.shape, 0)
    d = 1
    while d < tm:
        keep = row < tm - d
        b_s = jnp.where(keep, pltpu.roll(b, tm - d, 0), 1.0)
        u_s = jnp.where(keep, pltpu.roll(u, tm - d, 0), 0.0)
        u = u + b * u_s
        b = b * b_s
        d *= 2
    return u


def _gate_pre(xc, w_ref):
    outs = []
    for n in range(NB):
        outs.append(_dot(xc[:, n * LANES:(n + 1) * LANES].astype(BF16), w_ref[n]))
    return jnp.concatenate(outs, axis=1)


def _gate_pre_t(d, w_ref):
    outs = []
    for n in range(NB):
        outs.append(_dot_nt(d[:, n * LANES:(n + 1) * LANES].astype(BF16), w_ref[n]))
    return jnp.concatenate(outs, axis=1)


def _softplus_neg(lam):
    return jnp.maximum(-lam, 0.0) + jnp.log(1.0 + jnp.exp(-jnp.abs(lam)))


def _row_spec(tm, width):
    return pl.BlockSpec((tm, width), lambda i: (i, 0))


def _const_spec(shape):
    nd = len(shape)
    return pl.BlockSpec(shape, lambda *_: (0,) * nd)


def _in_proj(x, pre_gain, w_a, w_f, w_b, b_f_pad):
    T = x.shape[0]
    tm = TM

    def body(x_ref, g_ref, wa_ref, wf_ref, wb_ref, bf_ref,
             xn_ref, q_ref, k_ref, v_ref, ga_ref, xl_ref, gl_ref, flb_ref, ccol_ref, crow_ref, c_s, carry):
        @pl.when(pl.program_id(0) == 0)
        def _():
            carry[...] = jnp.zeros_like(carry)

        xv = x_ref[...]
        xn = (xv * _rstd(xv) * g_ref[...]).astype(BF16)
        xn_ref[...] = xn
        for s, o_ref in enumerate((q_ref, k_ref, v_ref)):
            o_ref[...] = _dot_nt(xn, wa_ref[s * D:(s + 1) * D, :]).astype(o_ref.dtype)
        for s, o_ref in enumerate((ga_ref, xl_ref, gl_ref)):
            o_ref[...] = _dot_nt(xn, wb_ref[s * D:(s + 1) * D, :]).astype(o_ref.dtype)
        flb = _dot_nt(xn, wf_ref[...]) + bf_ref[...]
        flb_ref[...] = flb
        lane = lax.broadcasted_iota(jnp.int32, flb.shape, 1)
        ls = jnp.where(lane < H, jnp.minimum(flb, 0.0) - jnp.log(1.0 + jnp.exp(-jnp.abs(flb))), 0.0)
        r = lax.broadcasted_iota(jnp.int32, (tm, tm), 0)
        c = lax.broadcasted_iota(jnp.int32, (tm, tm), 1)
        cs = _dot_exact((c <= r).astype(F32), ls) + carry[...]
        c_s[...] = cs
        carry[...] = c_s[tm - 1:tm, :]
        for hd in range(H):
            ccol_ref[hd] = c_s[:, hd:hd + 1]
        crow_ref[...] = jnp.transpose(cs)[:H, :]

    bf = jax.ShapeDtypeStruct((T, D), BF16)
    f32 = jax.ShapeDtypeStruct((T, D), F32)
    nar = jax.ShapeDtypeStruct((T, LANES), F32)
    return pl.pallas_call(
        body, name="in_proj", grid=(T // tm,),
        in_specs=[_row_spec(tm, D), _const_spec((1, D)), _const_spec((3 * D, D)), _const_spec((LANES, D)),
                  _const_spec((3 * D, D)), _const_spec((1, LANES))],
        out_specs=[_row_spec(tm, D)] * 7 + [_row_spec(tm, LANES), pl.BlockSpec((H, tm, 1), lambda i: (0, i, 0)),
                                            pl.BlockSpec((H, tm), lambda i: (0, i))],
        out_shape=[bf, bf, bf, bf, f32, f32, f32, nar, jax.ShapeDtypeStruct((H, T, 1), F32),
                   jax.ShapeDtypeStruct((H, T), F32)],
        scratch_shapes=[pltpu.VMEM((tm, LANES), F32), pltpu.VMEM((1, LANES), F32)],
        compiler_params=_cparams(("arbitrary",), VMEM_BIG),
    )(x, pre_gain, w_a, w_f, w_b, b_f_pad)


def _attn_fwd(q, k, v, c_col, c_row):
    T = q.shape[0]
    t = TA
    n = T // t
    scale = DH ** -0.5

    def body(q_ref, k_ref, v_ref, cq_ref, ck_ref, o_ref, bq_ref, m_s, l_s, acc_s):
        hd = pl.program_id(0)
        qi = pl.program_id(1)
        ki = pl.program_id(2)

        @pl.when(ki == 0)
        def _():
            m_s[...] = jnp.full(m_s.shape, NEG, F32)
            l_s[...] = jnp.zeros_like(l_s)
            acc_s[...] = jnp.zeros_like(acc_s)

        @pl.when(ki <= qi)
        def _():
            ck = ck_ref[pl.ds(hd, 1), :]
            s = _dot_nt(q_ref[...], k_ref[...]) * scale + (cq_ref[0] - ck)
            row = qi * t + lax.broadcasted_iota(jnp.int32, (t, t), 0)
            col = ki * t + lax.broadcasted_iota(jnp.int32, (t, t), 1)
            s = jnp.where(col <= row, s, NEG)
            m_prev = m_s[...]
            m_new = jnp.maximum(m_prev, jnp.max(s, axis=1, keepdims=True))
            alpha = jnp.exp(m_prev - m_new)
            pr = jnp.exp(s - m_new)
            l_s[...] = alpha * l_s[...] + jnp.sum(pr, axis=1, keepdims=True)
            acc_s[...] = alpha * acc_s[...] + _dot(pr.astype(BF16), v_ref[...])
            m_s[...] = m_new

        @pl.when(ki == n - 1)
        def _():
            l = l_s[...]
            o_ref[...] = acc_s[...] / l
            bq = jnp.broadcast_to(cq_ref[0] - (m_s[...] + jnp.log(l)), (t, LANES))
            bq_ref[0] = jnp.transpose(bq)[:SUBLANES, :]

    q_spec = pl.BlockSpec((t, DH), lambda h, qi, ki: (qi, h))
    kv_spec = pl.BlockSpec((t, DH), lambda h, qi, ki: (jnp.minimum(ki, qi), h))
    return pl.pallas_call(
        body, name="attn_fwd", grid=(H, n, n),
        in_specs=[q_spec, kv_spec, kv_spec,
                  pl.BlockSpec((1, t, 1), lambda h, qi, ki: (h, qi, 0)),
                  pl.BlockSpec((H, t), lambda h, qi, ki: (0, jnp.minimum(ki, qi)))],
        out_specs=[q_spec, pl.BlockSpec((1, SUBLANES, t), lambda h, qi, ki: (h, 0, qi))],
        out_shape=[jax.ShapeDtypeStruct((T, D), F32), jax.ShapeDtypeStruct((H, SUBLANES, T), F32)],
        scratch_shapes=[pltpu.VMEM((t, 1), F32), pltpu.VMEM((t, 1), F32), pltpu.VMEM((t, DH), F32)],
        compiler_params=_cparams(("parallel", "parallel", "arbitrary")),
    )(q, k, v, c_col, c_row)


def _lru_gates(xc, wr_ref, br_ref, wi_ref, bi_ref, lam_ref):
    r = _sigmoid(_gate_pre(xc, wr_ref) + br_ref[...])
    ig = _sigmoid(_gate_pre(xc, wi_ref) + bi_ref[...])
    sp = _softplus_neg(lam_ref[...])
    la = (-LRU_C) * r * sp
    a = jnp.exp(la)
    sq = jnp.sqrt(_neg_expm1(2.0 * la))
    return r, ig, sp, a, sq


def _branches_fwd(o, g_attn, x_lru, g_lru, gain_a, gain_l, conv_w, conv_b, w_r, b_r, w_i, b_i, lam):
    T = o.shape[0]
    tm = TM

    def body(o_ref, ga_ref, xl_ref, gl_ref, gna_ref, gnl_ref, cw_ref, cb_ref, wr_ref, br_ref, wi_ref, bi_ref,
             lam_ref, ycat_ref, xc_ref, h_ref, halo_s, hc_s):
        @pl.when(pl.program_id(0) == 0)
        def _():
            halo_s[...] = jnp.zeros_like(halo_s)
            hc_s[...] = jnp.zeros_like(hc_s)

        ov = o_ref[...]
        ga = ga_ref[...]
        ya = ov * _rstd(ov) * gna_ref[...] * (ga * _sigmoid(ga))
        ycat_ref[:, :D] = ya.astype(BF16)

        xl = xl_ref[...]
        halo = halo_s[...]
        xc = xl * cw_ref[3:4, :] + cb_ref[...]
        for j in range(3):
            xc = xc + _shift_down(xl, 3 - j, halo) * cw_ref[j:j + 1, :]
        halo_s[...] = xl_ref[tm - SUBLANES:tm, :]
        xc_ref[...] = xc

        _, ig, _, a, sq = _lru_gates(xc, wr_ref, br_ref, wi_ref, bi_ref, lam_ref)
        u = sq * (ig * xc)
        a_cum, h_loc = _scan_fwd(a, u)
        hh = h_loc + a_cum * hc_s[...]
        h_ref[...] = hh
        hc_s[...] = h_ref[tm - 1:tm, :]

        gl = gl_ref[...]
        yl = hh * _rstd(hh) * gnl_ref[...] * (gl * _sigmoid(gl))
        ycat_ref[:, D:] = yl.astype(BF16)

    vec = _const_spec((1, D))
    wspec = _const_spec((NB, LANES, LANES))
    return pl.pallas_call(
        body, name="branches_fwd", grid=(T // tm,),
        in_specs=[_row_spec(tm, D)] * 4 + [vec, vec, _const_spec((4, D)), vec, wspec, vec, wspec, vec, vec],
        out_specs=[_row_spec(tm, DMIX), _row_spec(tm, D), _row_spec(tm, D)],
        out_shape=[jax.ShapeDtypeStruct((T, DMIX), BF16), jax.ShapeDtypeStruct((T, D), F32),
                   jax.ShapeDtypeStruct((T, D), F32)],
        scratch_shapes=[pltpu.VMEM((SUBLANES, D), F32), pltpu.VMEM((1, D), F32)],
        compiler_params=_cparams(("arbitrary",)),
    )(o, g_attn, x_lru, g_lru, gain_a, gain_l, conv_w, conv_b, w_r, b_r, w_i, b_i, lam)


def _tail(ycat, x, p, tgt, w_out, post_gain, w_ple, ple_gain, w_gate, b_gate):
    T = x.shape[0]
    tm = TM

    def body(ycat_ref, x_ref, p_ref, t_ref, wo_ref, pg_ref, wp_ref, eg_ref, wg_ref, bg_ref,
             dh1_ref, dycat_ref, dmix_ref, h1b_ref, dgp_ref, pb_ref, dpe_ref, acc_ref):
        @pl.when(pl.program_id(0) == 0)
        def _():
            acc_ref[...] = jnp.zeros_like(acc_ref)

        mix = _dot(ycat_ref[...], wo_ref[...])
        rstd_m = _rstd(mix)
        mhat = mix * rstd_m
        h1 = x_ref[...] + mhat * pg_ref[...]
        pb = p_ref[...].astype(BF16)
        pb_ref[...] = pb
        pe = _dot(pb, wp_ref[...])
        rstd_p = _rstd(pe)
        pehat = pe * rstd_p
        e = pehat * eg_ref[...]
        h1b = h1.astype(BF16)
        h1b_ref[...] = h1b
        gate = _sigmoid(_dot(h1b, wg_ref[...]) + bg_ref[...])
        diff = (h1 + gate * e) - t_ref[...]

        dy = diff * (1.0 / D)
        de = dy * gate
        dgp = (dy * e) * gate * (1.0 - gate)
        dgpb = dgp.astype(BF16)
        dgp_ref[...] = dgpb
        dh1 = dy + _dot_nt(dgpb, wg_ref[...])
        dh1_ref[...] = dh1
        dpe_ref[...] = _rms_bwd(de * eg_ref[...], pehat, rstd_p).astype(BF16)
        dmix = _rms_bwd(dh1 * pg_ref[...], mhat, rstd_m).astype(BF16)
        dmix_ref[...] = dmix
        dycat_ref[...] = _dot_nt(dmix, wo_ref[...])

        acc_ref[0:1, :] += jnp.sum(dh1 * mhat, axis=0, keepdims=True)
        acc_ref[1:2, :] += jnp.sum(de * pehat, axis=0, keepdims=True)
        acc_ref[2:3, :] += jnp.sum(dgp, axis=0, keepdims=True)
        acc_ref[3:4, :] += jnp.sum(diff * diff, axis=0, keepdims=True) * (0.5 / D)

    vec = _const_spec((1, D))
    bf = jax.ShapeDtypeStruct((T, D), BF16)
    return pl.pallas_call(
        body, name="tail", grid=(T // tm,),
        in_specs=[_row_spec(tm, DMIX), _row_spec(tm, D), _row_spec(tm, DPLE), _row_spec(tm, D),
                  _const_spec((DMIX, D)), vec, _const_spec((DPLE, D)), vec, _const_spec((D, D)), vec],
        out_specs=[_row_spec(tm, D), _row_spec(tm, DMIX), _row_spec(tm, D), _row_spec(tm, D), _row_spec(tm, D),
                   _row_spec(tm, DPLE), _row_spec(tm, D), _const_spec((SUBLANES, D))],
        out_shape=[jax.ShapeDtypeStruct((T, D), F32), jax.ShapeDtypeStruct((T, DMIX), F32), bf, bf, bf,
                   jax.ShapeDtypeStruct((T, DPLE), BF16), bf, jax.ShapeDtypeStruct((SUBLANES, D), F32)],
        compiler_params=_cparams(("arbitrary",), VMEM_BIG),
    )(ycat, x, p, tgt, w_out, post_gain, w_ple, ple_gain, w_gate, b_gate)


def _branches_bwd(dycat, o, g_attn, h, g_lru, gain_a, gain_l):
    T = o.shape[0]
    tm = TM

    def body(dy_ref, o_ref, ga_ref, h_ref, gl_ref, gna_ref, gnl_ref,
             do_ref, dd_ref, dga_ref, dgl_ref, dh_ref, acc_ref):
        @pl.when(pl.program_id(0) == 0)
        def _():
            acc_ref[...] = jnp.zeros_like(acc_ref)

        def branch(val, g, gain, dyv):
            rstd = _rstd(val)
            vhat = val * rstd
            sig = _sigmoid(g)
            dn = dyv * (g * sig)
            dg = dyv * (vhat * gain) * (sig * (1.0 + g * (1.0 - sig)))
            dgain = jnp.sum(dn * vhat, axis=0, keepdims=True)
            return _rms_bwd(dn * gain, vhat, rstd), dg, dgain

        ov = o_ref[...]
        do, dga, dgain_a = branch(ov, ga_ref[...], gna_ref[...], dy_ref[:, :D])
        do_ref[...] = do.astype(BF16)
        dga_ref[...] = dga.astype(BF16)
        prod = do * ov
        lane = lax.broadcasted_iota(jnp.int32, (tm, LANES), 1)
        dd = jnp.zeros((tm, LANES), F32)
        for hd in range(H):
            dd = jnp.where(lane == hd, jnp.sum(prod[:, hd * DH:(hd + 1) * DH], axis=1, keepdims=True), dd)
        dd_ref[...] = jnp.transpose(dd)[:H, :]

        dh, dgl, dgain_l = branch(h_ref[...], gl_ref[...], gnl_ref[...], dy_ref[:, D:])
        dh_ref[...] = dh
        dgl_ref[...] = dgl.astype(BF16)
        acc_ref[0:1, :] += dgain_a
        acc_ref[1:2, :] += dgain_l

    vec = _const_spec((1, D))
    bf = jax.ShapeDtypeStruct((T, D), BF16)
    return pl.pallas_call(
        body, name="branches_bwd", grid=(T // tm,),
        in_specs=[_row_spec(tm, DMIX)] + [_row_spec(tm, D)] * 4 + [vec, vec],
        out_specs=[_row_spec(tm, D), pl.BlockSpec((H, tm), lambda i: (0, i)), _row_spec(tm, D), _row_spec(tm, D),
                   _row_spec(tm, D), _const_spec((SUBLANES, D))],
        out_shape=[bf, jax.ShapeDtypeStruct((H, T), F32), bf, bf, jax.ShapeDtypeStruct((T, D), F32),
                   jax.ShapeDtypeStruct((SUBLANES, D), F32)],
        compiler_params=_cparams(("arbitrary",)),
    )(dycat, o, g_attn, h, g_lru, gain_a, gain_l)


def _lru_bwd(dh, h, xc, x_lru, conv_w, w_r, b_r, w_i, b_i, lam):
    T = dh.shape[0]
    tm = TM
    nt = T // tm
    per = tm // SUBLANES

    def body(dh_ref, h_ref, hprev_ref, xc_ref, xl_ref, xlprev_ref, cw_ref, wr_ref, br_ref, wi_ref, bi_ref, lam_ref,
             dxl_ref, dwr_ref, dwi_ref, acc_ref, carry_s, dxc_next_s, top_s):
        i = pl.program_id(0)

        @pl.when(i == 0)
        def _():
            acc_ref[...] = jnp.zeros_like(acc_ref)
            dwr_ref[...] = jnp.zeros_like(dwr_ref)
            dwi_ref[...] = jnp.zeros_like(dwi_ref)
            carry_s[...] = jnp.zeros_like(carry_s)
            dxc_next_s[...] = jnp.zeros_like(dxc_next_s)

        inner = jnp.where(i == nt - 1, 0.0, 1.0)
        xc = xc_ref[...]
        r, ig, sp, a, sq = _lru_gates(xc, wr_ref, br_ref, wi_ref, bi_ref, lam_ref)

        row = lax.broadcasted_iota(jnp.int32, (tm, D), 0)
        u = dh_ref[...] + jnp.where(row == tm - 1, carry_s[...], 0.0)
        dht = _scan_bwd(pltpu.roll(a, tm - 1, 0), u)
        top_s[...] = a[:SUBLANES, :] * dht[:SUBLANES, :]
        carry_s[...] = top_s[0:1, :]

        hprev = hprev_ref[...] * inner
        da = dht * _shift_down(h_ref[...], 1, hprev)
        dig = dht * sq * xc
        dxc = dht * sq * ig
        dsq = dht * ig * xc
        dla = da * a - dsq * (a * a) / sq
        dr = dla * ((-LRU_C) * sp)
        dpr = dr * r * (1.0 - r)
        dpi = dig * ig * (1.0 - ig)
        for n in range(NB):
            blk = slice(n * LANES, (n + 1) * LANES)
            xcb = xc[:, blk].astype(BF16)
            dwr_ref[n] += _dot_tn(xcb, dpr[:, blk].astype(BF16))
            dwi_ref[n] += _dot_tn(xcb, dpi[:, blk].astype(BF16))
        dxc = dxc + _gate_pre_t(dpr, wr_ref) + _gate_pre_t(dpi, wi_ref)

        xl = xl_ref[...]
        xlprev = xlprev_ref[...] * inner
        nxt = dxc_next_s[...]
        dxl = dxc * cw_ref[3:4, :]
        acc_ref[3:4, :] += jnp.sum(dxc * xl, axis=0, keepdims=True)
        for j in range(3):
            dxl = dxl + _shift_up(dxc, 3 - j, nxt) * cw_ref[j:j + 1, :]
            acc_ref[j:j + 1, :] += jnp.sum(dxc * _shift_down(xl, 3 - j, xlprev), axis=0, keepdims=True)
        dxc_next_s[...] = dxc[:SUBLANES, :]
        dxl_ref[...] = dxl.astype(BF16)

        acc_ref[4:5, :] += jnp.sum(dxc, axis=0, keepdims=True)
        acc_ref[5:6, :] += jnp.sum(dpr, axis=0, keepdims=True)
        acc_ref[6:7, :] += jnp.sum(dpi, axis=0, keepdims=True)
        acc_ref[7:8, :] += jnp.sum(dla * ((-LRU_C) * r), axis=0, keepdims=True)

        @pl.when(i == nt - 1)
        def _():
            lam_v = lam_ref[...]
            acc_ref[7:8, :] = acc_ref[7:8, :] * (-_sigmoid(-lam_v))

    rev = pl.BlockSpec((tm, D), lambda i: (nt - 1 - i, 0))
    prev8 = pl.BlockSpec((SUBLANES, D), lambda i: (jnp.maximum((nt - 1 - i) * per - 1, 0), 0))
    vec = _const_spec((1, D))
    wspec = _const_spec((NB, LANES, LANES))
    bf = jax.ShapeDtypeStruct((T, D), BF16)
    return pl.pallas_call(
        body, name="lru_bwd", grid=(nt,),
        in_specs=[rev, rev, prev8, rev, rev, prev8, _const_spec((4, D)), wspec, vec, wspec, vec, vec],
        out_specs=[rev, wspec, wspec, _const_spec((SUBLANES, D))],
        out_shape=[bf, jax.ShapeDtypeStruct((NB, LANES, LANES), F32), jax.ShapeDtypeStruct((NB, LANES, LANES), F32),
                   jax.ShapeDtypeStruct((SUBLANES, D), F32)],
        scratch_shapes=[pltpu.VMEM((1, D), F32), pltpu.VMEM((SUBLANES, D), F32), pltpu.VMEM((SUBLANES, D), F32)],
        compiler_params=_cparams(("arbitrary",)),
    )(dh, h, h, xc, x_lru, x_lru, conv_w, w_r, b_r, w_i, b_i, lam)


def _attn_bwd(q, k, v, do, bq_row, dd_row, c_col):
    T = q.shape[0]
    t = TA
    n = T // t
    scale = DH ** -0.5

    def body(q_ref, k_ref, v_ref, do_ref, bq_ref, dd_ref, ck_ref,
             dq_ref, dk_ref, dv_ref, dc_ref, dcq_ref, dq_s, dk_s, dv_s, dc_s, dcq_s):
        hd = pl.program_id(0)
        ki = pl.program_id(1)
        qi = pl.program_id(2)

        @pl.when((ki == 0) & (qi == 0))
        def _():
            dq_s[...] = jnp.zeros_like(dq_s)
            dcq_s[...] = jnp.zeros_like(dcq_s)

        @pl.when(qi == 0)
        def _():
            dk_s[...] = jnp.zeros_like(dk_s)
            dv_s[...] = jnp.zeros_like(dv_s)
            dc_s[...] = jnp.zeros_like(dc_s)

        @pl.when(qi >= ki)
        def _():
            qv = q_ref[...]
            kv = k_ref[...]
            dov = do_ref[...]
            dd = dd_ref[pl.ds(hd, 1), :]
            st = _dot_nt(kv, qv) * scale + (bq_ref[0, 0:1, :] - ck_ref[0])
            krow = ki * t + lax.broadcasted_iota(jnp.int32, (t, t), 0)
            qcol = qi * t + lax.broadcasted_iota(jnp.int32, (t, t), 1)
            pt = jnp.exp(jnp.where(krow <= qcol, st, NEG))
            dv_s[...] += _dot(pt.astype(BF16), dov)
            dst = pt * (_dot_nt(v_ref[...], dov) - dd)
            dsb = dst.astype(BF16)
            dk_s[...] += _dot(dsb, qv)
            off = pl.multiple_of(qi * t, t)
            dq_s[pl.ds(off, t), :] += _dot_tn(dsb, kv)
            dc_s[...] -= jnp.sum(dst, axis=1, keepdims=True)
            dcq_s[:, pl.ds(off, t)] += jnp.sum(dst, axis=0, keepdims=True)

        @pl.when(qi == n - 1)
        def _():
            dk_ref[...] = (dk_s[...] * scale).astype(BF16)
            dv_ref[...] = dv_s[...].astype(BF16)
            dc_ref[0] = jnp.broadcast_to(dc_s[...], (t, LANES))

        @pl.when((ki == n - 1) & (qi == n - 1))
        def _():
            dq_ref[...] = (dq_s[...] * scale).astype(BF16)
            dcq_ref[0] = dcq_s[...]

    qside = pl.BlockSpec((t, DH), lambda h, ki, qi: (jnp.maximum(qi, ki), h))
    kside = pl.BlockSpec((t, DH), lambda h, ki, qi: (ki, h))
    kcol = pl.BlockSpec((1, t, 1), lambda h, ki, qi: (h, ki, 0))
    bf = jax.ShapeDtypeStruct((T, D), BF16)
    return pl.pallas_call(
        body, name="attn_bwd", grid=(H, n, n),
        in_specs=[qside, kside, kside, qside,
                  pl.BlockSpec((1, SUBLANES, t), lambda h, ki, qi: (h, 0, jnp.maximum(qi, ki))),
                  pl.BlockSpec((H, t), lambda h, ki, qi: (0, jnp.maximum(qi, ki))), kcol],
        out_specs=[pl.BlockSpec((T, DH), lambda h, ki, qi: (0, h)), kside, kside,
                   pl.BlockSpec((1, t, LANES), lambda h, ki, qi: (h, ki, 0)),
                   pl.BlockSpec((1, 1, T), lambda h, ki, qi: (h, 0, 0))],
        out_shape=[bf, bf, bf, jax.ShapeDtypeStruct((H, T, LANES), F32), jax.ShapeDtypeStruct((H, 1, T), F32)],
        scratch_shapes=[pltpu.VMEM((T, DH), F32), pltpu.VMEM((t, DH), F32), pltpu.VMEM((t, DH), F32),
                        pltpu.VMEM((t, 1), F32), pltpu.VMEM((1, T), F32)],
        compiler_params=_cparams(("arbitrary", "arbitrary", "arbitrary")),
    )(q, k, v, do, bq_row, dd_row, c_col)


def _fgate_bwd(dc_key, dc_query, flb):
    T = flb.shape[0]
    tm = TM
    nt = T // tm

    def body(dck_ref, dcq_ref, flb_ref, dfl_ref, acc_ref, carry, top_s):
        @pl.when(pl.program_id(0) == 0)
        def _():
            carry[...] = jnp.zeros_like(carry)
            acc_ref[...] = jnp.zeros_like(acc_ref)

        flb = flb_ref[...]
        lane = lax.broadcasted_iota(jnp.int32, flb.shape, 1)
        dc = dcq_ref[...]
        for hd in range(H):
            dc = dc + jnp.where(lane == hd, dck_ref[hd], 0.0)
        r = lax.broadcasted_iota(jnp.int32, (tm, tm), 0)
        c = lax.broadcasted_iota(jnp.int32, (tm, tm), 1)
        dls = _dot_exact((c >= r).astype(F32), dc) + carry[...]
        top_s[...] = dls[:SUBLANES, :]
        carry[...] = top_s[0:1, :]
        dfl = jnp.where(lane < H, dls * _sigmoid(-flb), 0.0)
        dfl_ref[...] = dfl.astype(BF16)
        acc_ref[0:1, :] += jnp.sum(dfl, axis=0, keepdims=True)

    rev = pl.BlockSpec((tm, LANES), lambda i: (nt - 1 - i, 0))
    return pl.pallas_call(
        body, name="fgate_bwd", grid=(nt,),
        in_specs=[pl.BlockSpec((H, tm, LANES), lambda i: (0, nt - 1 - i, 0)), rev, rev],
        out_specs=[rev, _const_spec((SUBLANES, LANES))],
        out_shape=[jax.ShapeDtypeStruct((T, LANES), BF16), jax.ShapeDtypeStruct((SUBLANES, LANES), F32)],
        scratch_shapes=[pltpu.VMEM((1, LANES), F32), pltpu.VMEM((SUBLANES, LANES), F32)],
        compiler_params=_cparams(("arbitrary",)),
    )(dc_key, dc_query, flb)


def _dx(dz, dfl, w_a, w_f, w_b, x, pre_gain, dh1):
    T = x.shape[0]
    tm = TM

    def body(*refs):
        dz_refs = refs[:6]
        dfl_ref, wa_ref, wf_ref, wb_ref, x_ref, g_ref, dh1_ref, gx_ref, acc_ref = refs[6:]

        @pl.when(pl.program_id(0) == 0)
        def _():
            acc_ref[...] = jnp.zeros_like(acc_ref)

        dxn = _dot(dfl_ref[...], wf_ref[...])
        for s in range(3):
            dxn = dxn + _dot(dz_refs[s][...], wa_ref[s * D:(s + 1) * D, :])
            dxn = dxn + _dot(dz_refs[3 + s][...], wb_ref[s * D:(s + 1) * D, :])
        xv = x_ref[...]
        rstd = _rstd(xv)
        xhat = xv * rstd
        gx_ref[...] = dh1_ref[...] + _rms_bwd(dxn * g_ref[...], xhat, rstd)
        acc_ref[0:1, :] += jnp.sum(dxn * xhat, axis=0, keepdims=True)

    return pl.pallas_call(
        body, name="dx", grid=(T // tm,),
        in_specs=[_row_spec(tm, D)] * 6 + [_row_spec(tm, LANES), _const_spec((3 * D, D)), _const_spec((LANES, D)),
                                           _const_spec((3 * D, D)), _row_spec(tm, D), _const_spec((1, D)),
                                           _row_spec(tm, D)],
        out_specs=[_row_spec(tm, D), _const_spec((SUBLANES, D))],
        out_shape=[jax.ShapeDtypeStruct((T, D), F32), jax.ShapeDtypeStruct((SUBLANES, D), F32)],
        compiler_params=_cparams(("arbitrary",), VMEM_BIG),
    )(*dz, dfl, w_a, w_f, w_b, x, pre_gain, dh1)


GRAD_ROWS = D_IN + SUBLANES


def _seg_row(s):
    return (s * (D // SUBLANES) + jnp.where(s >= 3, H // SUBLANES, 0)) * SUBLANES


def _dw_in_t(dz, dfl, xn, bt=512):
    T = xn.shape[0]
    nt = T // bt

    def main_body(*refs):
        dz_refs, xn_ref, o_ref = refs[:6], refs[6], refs[7]
        s = pl.program_id(0)

        @pl.when(pl.program_id(1) == 0)
        def _():
            o_ref[...] = jnp.zeros_like(o_ref)

        for k in range(6):
            @pl.when(s == k)
            def _(k=k):
                o_ref[...] += _dot_tn(dz_refs[k][...], xn_ref[...])

    def dz_spec(k):
        return pl.BlockSpec((bt, D), lambda s, t: (jnp.where(s == k, t, 0), 0))

    main = pl.pallas_call(
        main_body, name="dw_in_main", grid=(6, nt),
        in_specs=[dz_spec(k) for k in range(6)] + [pl.BlockSpec((bt, D), lambda s, t: (t, 0))],
        out_specs=pl.BlockSpec((pl.Element(D), pl.Element(D)), lambda s, t: (_seg_row(s), 0)),
        out_shape=jax.ShapeDtypeStruct((GRAD_ROWS, D), F32),
        compiler_params=_cparams(("arbitrary", "arbitrary")),
    )(*dz, xn)

    def f_body(dfl_ref, xn_ref, main_ref, o_ref, acc_s):
        p = pl.program_id(0)
        t = pl.program_id(1)

        @pl.when(t == 0)
        def _():
            acc_s[...] = jnp.zeros_like(acc_s)

        @pl.when(p == 0)
        def _():
            acc_s[...] += _dot_tn(dfl_ref[...], xn_ref[...])

        @pl.when(t == nt - 1)
        def _():
            o_ref[...] = acc_s[:SUBLANES, :]

    fl_block = FL0 // SUBLANES
    end_block = D_IN // SUBLANES
    return pl.pallas_call(
        f_body, name="dw_in_f", grid=(2, nt),
        in_specs=[pl.BlockSpec((bt, LANES), lambda p, t: (t, 0)), pl.BlockSpec((bt, D), lambda p, t: (t, 0)),
                  pl.BlockSpec(memory_space=pl.ANY)],
        out_specs=pl.BlockSpec((SUBLANES, D), lambda p, t: (fl_block + p * (end_block - fl_block), 0)),
        out_shape=jax.ShapeDtypeStruct((GRAD_ROWS, D), F32),
        scratch_shapes=[pltpu.VMEM((LANES, D), F32)],
        input_output_aliases={2: 0},
        compiler_params=_cparams(("arbitrary", "arbitrary")),
    )(dfl, xn, main)


def _matmul_tn(a, b, name, bm=512, bn=1024, bt=512):
    T, M = a.shape
    N = b.shape[1]
    bm, bn, bt = min(bm, M), min(bn, N), min(bt, T)

    def body(a_ref, b_ref, o_ref):
        @pl.when(pl.program_id(2) == 0)
        def _():
            o_ref[...] = jnp.zeros_like(o_ref)

        o_ref[...] += _dot_tn(a_ref[...], b_ref[...])

    return pl.pallas_call(
        body, name=name, grid=(M // bm, N // bn, T // bt),
        in_specs=[pl.BlockSpec((bt, bm), lambda i, j, t: (t, i)), pl.BlockSpec((bt, bn), lambda i, j, t: (t, j))],
        out_specs=pl.BlockSpec((bm, bn), lambda i, j, t: (i, j)),
        out_shape=jax.ShapeDtypeStruct((M, N), F32),
        compiler_params=_cparams(("parallel", "parallel", "arbitrary")),
    )(a, b)


HBM_SPEC = pl.BlockSpec(memory_space=pltpu.HBM)
VMEM_SPEC = pl.BlockSpec(memory_space=pltpu.VMEM)


def _position():
    return lax.axis_index("x"), lax.axis_index("y"), lax.axis_index("c")


def _other_chips(x, y):
    return [(1 - x, y), (x, 1 - y), (1 - x, 1 - y)]


def _gather_shards(shards, whole):
    na, nw = len(shards), len(whole)
    nall = na + nw

    def body(*refs):
        srcs, dsts = refs[:nall], refs[nall:2 * nall]
        ici_send, ici_recv, d2d_send, d2d_recv = refs[2 * nall:]
        x, y, c = _position()
        chip = 2 * x + y
        chips = _other_chips(x, y)

        def half(a, which):
            rows = srcs[a].shape[0] // 2
            return pl.ds(pl.multiple_of(which * rows, 16), rows)

        first = []
        for j, (px, py) in enumerate(chips):
            for a in range(nall):
                src = srcs[a].at[half(a, c), :] if a < na else srcs[a]
                dst = dsts[a].at[chip, half(a, c), :] if a < na else dsts[a].at[chip]
                first.append(pltpu.make_async_remote_copy(
                    src_ref=src, dst_ref=dst, send_sem=ici_send.at[j * nall + a], recv_sem=ici_recv.at[j * nall + a],
                    device_id=(px, py, c), device_id_type=MESH))
        for cp in first:
            cp.start()

        passed = []
        for j, (px, py) in enumerate(chips):
            theirs = 2 * px + py
            for a in range(nall):
                if a < na:
                    landed = dsts[a].at[theirs, half(a, c), :]
                    fwd = pltpu.make_async_remote_copy(
                        src_ref=landed, dst_ref=landed, send_sem=d2d_send.at[j * na + a],
                        recv_sem=d2d_recv.at[j * na + a], device_id=(x, y, 1 - c), device_id_type=MESH)
                else:
                    landed = dsts[a].at[theirs]
                pltpu.make_async_remote_copy(
                    src_ref=landed, dst_ref=landed, send_sem=ici_send.at[j * nall + a],
                    recv_sem=ici_recv.at[j * nall + a], device_id=(px, py, c), device_id_type=MESH).wait_recv()
                if a < na:
                    fwd.start()
                    passed.append(fwd)
        for j, (px, py) in enumerate(chips):
            theirs = 2 * px + py
            for a in range(na):
                other = dsts[a].at[theirs, half(a, 1 - c), :]
                pltpu.make_async_remote_copy(
                    src_ref=other, dst_ref=other, send_sem=d2d_send.at[j * na + a], recv_sem=d2d_recv.at[j * na + a],
                    device_id=(x, y, 1 - c), device_id_type=MESH).wait_recv()
        for cp in first + passed:
            cp.wait_send()

    arrs = list(shards) + list(whole)
    outs = pl.pallas_call(
        body, name="gather_shards",
        in_specs=[HBM_SPEC] * nall, out_specs=[HBM_SPEC] * nall,
        out_shape=[jax.ShapeDtypeStruct((N_CHIPS,) + s.shape, s.dtype) for s in arrs],
        scratch_shapes=[pltpu.SemaphoreType.DMA((3 * nall,)), pltpu.SemaphoreType.DMA((3 * nall,)),
                        pltpu.SemaphoreType.DMA((3 * na,)), pltpu.SemaphoreType.DMA((3 * na,))],
    )(*arrs)
    chip = 2 * lax.axis_index("x") + lax.axis_index("y")
    return [lax.dynamic_update_slice(o, a[None], (chip,) + (0,) * a.ndim) for o, a in zip(outs, arrs)]


W_ROWS = 1568
G_ROWS = 1552
SHARD_ROWS = D_IN // N_CHIPS
WINDOW_STEP = 1536


def _assemble_w_in(cont):
    cb = 256
    half = WINDOW_STEP

    def body(c_ref, wa_ref, wf_ref, wb_ref):
        x0 = c_ref[0].astype(F32)
        x1, x2, x3 = (pltpu.roll(c_ref[j].astype(F32), 2 * j, 0) for j in (1, 2, 3))
        wa = jnp.concatenate([x0[:half], x0[half:half + 16] + x1[:16], x1[16:half]], axis=0)
        wa_ref[...] = wa.astype(BF16)

        fl = x1[half:half + 16] + x2[:16]
        row = lax.broadcasted_iota(jnp.int32, fl.shape, 0)
        wf_ref[:16, :] = jnp.where(row < H, fl, 0.0).astype(BF16)
        wf_ref[16:, :] = jnp.zeros((LANES - 16, cb), BF16)

        mid = x2[half:half + SUBLANES] + x3[:SUBLANES]
        wb = jnp.concatenate([x2[SUBLANES:half], mid, x3[SUBLANES:half + SUBLANES]], axis=0)
        wb_ref[...] = wb.astype(BF16)

    return pl.pallas_call(
        body, name="assemble_w_in", grid=(D // cb,),
        in_specs=[pl.BlockSpec((N_CHIPS, W_ROWS, cb), lambda i: (0, 0, i))],
        out_specs=[pl.BlockSpec((3 * D, cb), lambda i: (0, i)), pl.BlockSpec((LANES, cb), lambda i: (0, i)),
                   pl.BlockSpec((3 * D, cb), lambda i: (0, i))],
        out_shape=[jax.ShapeDtypeStruct((3 * D, D), BF16), jax.ShapeDtypeStruct((LANES, D), BF16),
                   jax.ShapeDtypeStruct((3 * D, D), BF16)],
        compiler_params=_cparams(("parallel",)),
    )(cont)


def _pair_exchange(grad_t, parts):
    na = len(parts)
    n = N_CHIPS + na
    half_g = G_ROWS // 2

    def body(*refs):
        g_ref, srcs, got = refs[0], refs[1:1 + na], refs[1 + na:2 + 2 * na]
        send_sems, recv_sems = refs[2 + 2 * na:]
        x, y, c = _position()
        pieces = []
        for j in range(N_CHIPS):
            rows = pl.ds(pl.multiple_of(j * WINDOW_STEP + (1 - c) * half_g, SUBLANES), half_g)
            pieces.append((g_ref.at[rows, :], got[0].at[j]))
        for a in range(na):
            half = srcs[a].shape[1] // 2
            rows = pl.ds(pl.multiple_of((1 - c) * half, SUBLANES), half)
            pieces.append((srcs[a].at[:, rows, :], got[1 + a]))
        copies = [pltpu.make_async_remote_copy(
            src_ref=give, dst_ref=dst, send_sem=send_sems.at[k], recv_sem=recv_sems.at[k],
            device_id=(x, y, 1 - c), device_id_type=MESH) for k, (give, dst) in enumerate(pieces)]
        for cp in copies:
            cp.start()
        for cp in copies:
            cp.wait()

    halves = [jax.ShapeDtypeStruct((N_CHIPS, half_g, D), F32)]
    halves += [jax.ShapeDtypeStruct((s.shape[0], s.shape[1] // 2, s.shape[2]), s.dtype) for s in parts]
    return pl.pallas_call(
        body, name="pair_exchange",
        in_specs=[HBM_SPEC] * (1 + na), out_specs=[HBM_SPEC] * (1 + na),
        out_shape=halves,
        scratch_shapes=[pltpu.SemaphoreType.DMA((n,)), pltpu.SemaphoreType.DMA((n,))],
    )(grad_t, *parts)


def _pair_sum(part, got, c, name):
    _, half, C = got.shape
    cb = min(C, 256)

    def body(c_ref, a_ref, b_ref, o_ref):
        o_ref[...] = (a_ref[...] + b_ref[...]).astype(BF16)

    spec = pl.BlockSpec((1, half, cb), lambda j, i, c_ref: (j, 0, i))
    grid_spec = pltpu.PrefetchScalarGridSpec(
        num_scalar_prefetch=1, grid=(N_CHIPS, C // cb),
        in_specs=[pl.BlockSpec((1, half, cb), lambda j, i, c_ref: (j, c_ref[0], i)), spec], out_specs=spec)
    return pl.pallas_call(
        body, name=name, grid_spec=grid_spec,
        out_shape=jax.ShapeDtypeStruct((N_CHIPS, half, C), BF16),
        compiler_params=_cparams(("parallel", "parallel")),
    )(c.reshape(1), part, got)


def _pair_sum_windows(grad_t, got, c):
    _, half, C = got.shape
    cb = 256

    def body(c_ref, a_ref, b_ref, o_ref):
        o_ref[0] = (a_ref[...] + b_ref[0]).astype(BF16)

    def mine(j, i, c_ref):
        return ((j * (WINDOW_STEP // SUBLANES) + c_ref[0] * (half // SUBLANES)) * SUBLANES, i * cb)

    spec = pl.BlockSpec((1, half, cb), lambda j, i, c_ref: (j, 0, i))
    grid_spec = pltpu.PrefetchScalarGridSpec(
        num_scalar_prefetch=1, grid=(N_CHIPS, C // cb),
        in_specs=[pl.BlockSpec((pl.Element(half), pl.Element(cb)), mine), spec], out_specs=spec)
    return pl.pallas_call(
        body, name="pair_sum_w_in", grid_spec=grid_spec,
        out_shape=jax.ShapeDtypeStruct((N_CHIPS, half, C), BF16),
        compiler_params=_cparams(("parallel", "parallel")),
    )(c.reshape(1), grad_t, got)


def _chip_exchange(sums):
    na = len(sums)

    def body(*refs):
        srcs, dsts = refs[:na], refs[na:2 * na]
        send_sems, recv_sems = refs[2 * na:]
        x, y, c = _position()
        chip = 2 * x + y
        copies = []
        for j, (px, py) in enumerate(_other_chips(x, y)):
            for a in range(na):
                copies.append(pltpu.make_async_remote_copy(
                    src_ref=srcs[a].at[2 * px + py], dst_ref=dsts[a].at[chip], send_sem=send_sems.at[j * na + a],
                    recv_sem=recv_sems.at[j * na + a], device_id=(px, py, c), device_id_type=MESH))
        for cp in copies:
            cp.start()
        for cp in copies:
            cp.wait()

    return pl.pallas_call(
        body, name="chip_exchange",
        in_specs=[HBM_SPEC] * na, out_specs=[HBM_SPEC] * na,
        out_shape=[jax.ShapeDtypeStruct(s.shape, s.dtype) for s in sums],
        scratch_shapes=[pltpu.SemaphoreType.DMA((3 * na,)), pltpu.SemaphoreType.DMA((3 * na,))],
    )(*sums)


def _chip_sum(own, got, chip, name):
    _, half, C = got.shape
    cb = min(C, 256)

    def body(chip_ref, own_ref, g_ref, o_ref):
        for me in range(N_CHIPS):
            @pl.when(chip_ref[0] == me)
            def _(me=me):
                terms = [own_ref[0] if k == me else g_ref[k] for k in range(N_CHIPS)]
                acc = terms[0].astype(F32) + terms[1].astype(F32)
                acc = acc + terms[2].astype(F32)
                o_ref[...] = acc + terms[3].astype(F32)

    grid_spec = pltpu.PrefetchScalarGridSpec(
        num_scalar_prefetch=1, grid=(C // cb,),
        in_specs=[pl.BlockSpec((1, half, cb), lambda i, chip_ref: (chip_ref[0], 0, i)),
                  pl.BlockSpec((N_CHIPS, half, cb), lambda i, chip_ref: (0, 0, i))],
        out_specs=pl.BlockSpec((half, cb), lambda i, chip_ref: (0, i)))
    return pl.pallas_call(
        body, name=name, grid_spec=grid_spec,
        out_shape=jax.ShapeDtypeStruct((half, C), F32),
        compiler_params=_cparams(("parallel",)),
    )(chip.reshape(1), own, got)


def _pair_swap(halves):
    na = len(halves)

    def body(*refs):
        srcs, dsts = refs[:na], refs[na:2 * na]
        send_sems, recv_sems = refs[2 * na:]
        x, y, c = _position()
        copies = [pltpu.make_async_remote_copy(
            src_ref=srcs[a], dst_ref=dsts[a], send_sem=send_sems.at[a], recv_sem=recv_sems.at[a],
            device_id=(x, y, 1 - c), device_id_type=MESH) for a in range(na)]
        for cp in copies:
            cp.start()
        for cp in copies:
            cp.wait()

    return pl.pallas_call(
        body, name="pair_swap",
        in_specs=[HBM_SPEC] * na, out_specs=[HBM_SPEC] * na,
        out_shape=[jax.ShapeDtypeStruct(s.shape, s.dtype) for s in halves],
        scratch_shapes=[pltpu.SemaphoreType.DMA((na,)), pltpu.SemaphoreType.DMA((na,))],
    )(*halves)


def _allreduce_small(g):
    rows = g.shape[0]
    per = rows // N_DEV

    def body(g_ref, out_ref, got_ref, s1, r1, s2, r2):
        x, y, c = _position()
        me = 4 * x + 2 * y + c
        mine = pl.ds(pl.multiple_of(me * per, SUBLANES), per)
        peers = []
        for j in range(1, N_DEV):
            px = 1 - x if j & 4 else x
            py = 1 - y if j & 2 else y
            pc = 1 - c if j & 1 else c
            peers.append((px, py, pc))

        first = []
        for j, (px, py, pc) in enumerate(peers):
            theirs = pl.ds(pl.multiple_of((4 * px + 2 * py + pc) * per, SUBLANES), per)
            first.append(pltpu.make_async_remote_copy(
                src_ref=g_ref.at[theirs, :], dst_ref=got_ref.at[me], send_sem=s1.at[j], recv_sem=r1.at[j],
                device_id=(px, py, pc), device_id_type=MESH))
        for cp in first:
            cp.start()
        got_ref[me] = g_ref[mine, :]
        for cp in first:
            cp.wait()
        total = got_ref[0]
        for d in range(1, N_DEV):
            total = total + got_ref[d]
        out_ref[mine, :] = total

        second = []
        for j, peer in enumerate(peers):
            second.append(pltpu.make_async_remote_copy(
                src_ref=out_ref.at[mine, :], dst_ref=out_ref.at[mine, :], send_sem=s2.at[j], recv_sem=r2.at[j],
                device_id=peer, device_id_type=MESH))
        for cp in second:
            cp.start()
        for cp in second:
            cp.wait()

    sems = pltpu.SemaphoreType.DMA((N_DEV - 1,))
    return pl.pallas_call(
        body, name="allreduce_small",
        in_specs=[VMEM_SPEC], out_specs=VMEM_SPEC,
        out_shape=jax.ShapeDtypeStruct(g.shape, F32),
        scratch_shapes=[pltpu.VMEM((N_DEV, per, LANES), F32), sems, sems, sems, sems],
    )(g)


def _adamw_math(g, w, m, v):
    m2 = ADAM_B1 * m + (1.0 - ADAM_B1) * g
    v2 = ADAM_B2 * v + (1.0 - ADAM_B2) * (g * g)
    m_hat = m2 / (1.0 - ADAM_B1 ** ADAM_STEP)
    v_hat = v2 / (1.0 - ADAM_B2 ** ADAM_STEP)
    delta = (-ADAM_LR) * (m_hat / (jnp.sqrt(v_hat) + ADAM_EPS) + ADAM_WD * w)
    return delta, m2, v2


def _adamw_big(g, w, m, v, name):
    R, C = g.shape
    cb = min(C, LANES)

    def body(g_ref, w_ref, m_ref, v_ref, d_ref, m2_ref, v2_ref):
        d_ref[...], m2_ref[...], v2_ref[...] = _adamw_math(g_ref[...], w_ref[...], m_ref[...], v_ref[...])

    spec = pl.BlockSpec((R, cb), lambda i: (0, i))
    out = jax.ShapeDtypeStruct((R, C), F32)
    return pl.pallas_call(
        body, name=name, grid=(C // cb,),
        in_specs=[spec] * 4, out_specs=[spec] * 3, out_shape=[out] * 3,
        compiler_params=_cparams(("parallel",)),
    )(g, w, m, v)


def _adamw_small(gs, ws, ms, vs):
    n = len(gs)

    def body(*refs):
        for a in range(n):
            g_ref, w_ref, m_ref, v_ref = (refs[k * n + a] for k in range(4))
            d_ref, m2_ref, v2_ref = (refs[(4 + k) * n + a] for k in range(3))
            d_ref[...], m2_ref[...], v2_ref[...] = _adamw_math(g_ref[...], w_ref[...], m_ref[...], v_ref[...])

    outs = [jax.ShapeDtypeStruct(w.shape, F32) for w in ws]
    return pl.pallas_call(
        body, name="adamw_small",
        in_specs=[VMEM_SPEC] * (4 * n), out_specs=[VMEM_SPEC] * (3 * n), out_shape=outs * 3,
    )(*gs, *ws, *ms, *vs)


def _local_step(x, p, tgt, w_a, w_f, w_b, w_out_b, w_ple_b, w_gate_b, conv_w, b_f, pre_gain, post_gain, conv_b,
                w_rgate, b_rgate, w_igate, b_igate, lam, gain_a, gain_l, ple_gain, b_gate):
    b_f_pad = jnp.pad(b_f, ((0, 0), (0, LANES - H)))
    w_r = w_rgate.astype(BF16)
    w_i = w_igate.astype(BF16)

    xn, q, k, v, g_attn, x_lru, g_lru, flb, c_col, c_row = _in_proj(x, pre_gain, w_a, w_f, w_b, b_f_pad)
    o, bq_row = _attn_fwd(q, k, v, c_col, c_row)
    ycat, xc, h = _branches_fwd(o, g_attn, x_lru, g_lru, gain_a, gain_l, conv_w, conv_b, w_r, b_rgate, w_i, b_igate,
                                lam)
    dh1, dycat, dmix, h1b, dgp, pb, dpe, acc_t = _tail(ycat, x, p, tgt, w_out_b, post_gain, w_ple_b, ple_gain,
                                                       w_gate_b, b_gate)
    do, dd_row, dg_attn, dg_lru, dh, acc_b = _branches_bwd(dycat, o, g_attn, h, g_lru, gain_a, gain_l)
    dx_lru, gw_r, gw_i, acc_l = _lru_bwd(dh, h, xc, x_lru, conv_w, w_r, b_rgate, w_i, b_igate, lam)
    dq, dk, dv, dc, dcq = _attn_bwd(q, k, v, do, bq_row, dd_row, c_col)
    dfl, acc_f = _fgate_bwd(dc, jnp.pad(dcq[:, 0, :].T, ((0, 0), (0, LANES - H))), flb)
    dz = (dq, dk, dv, dg_attn, dx_lru, dg_lru)
    grad_x, acc_x = _dx(dz, dfl, w_a, w_f, w_b, x, pre_gain, dh1)

    grads = dict(
        w_in_t=_dw_in_t(dz, dfl, xn),
        w_out=_matmul_tn(ycat, dmix, "dw_out"),
        w_ple=_matmul_tn(pb, dpe, "dw_ple"),
        w_ple_gate=_matmul_tn(h1b, dgp, "dw_ple_gate"),
        w_rgate=gw_r,
        w_igate=gw_i,
        b_f=acc_f[0:1, :H],
        pre_gain=acc_x[0:1],
        post_gain=acc_t[0:1],
        conv_w=acc_l[0:4],
        conv_b=acc_l[4:5],
        b_rgate=acc_l[5:6],
        b_igate=acc_l[6:7],
        lru_lambda=acc_l[7:8],
        attn_out_gain=acc_b[0:1],
        lru_out_gain=acc_b[1:2],
        ple_gain=acc_t[1:2],
        b_ple_gate=acc_t[2:3],
    )
    loss = jnp.sum(acc_t[3])
    return loss, grad_x, grads


SMALL_ROWS = ["b_f", "pre_gain", "post_gain", "conv_w", "conv_b", "b_rgate", "b_igate", "lru_lambda",
              "attn_out_gain", "lru_out_gain", "ple_gain", "b_ple_gate"]
WEIGHTS = ["w_in", "b_f", "pre_gain", "post_gain", "conv_w", "conv_b", "w_rgate", "b_rgate", "w_igate", "b_igate",
           "lru_lambda", "attn_out_gain", "lru_out_gain", "w_out", "w_ple", "ple_gain", "w_ple_gate", "b_ple_gate"]
SHARDED = ["w_in", "w_out", "w_ple", "w_ple_gate"]


def _by_chip_cols(g):
    r, cols = g.shape
    return g.reshape(r, N_CHIPS, cols // N_CHIPS).transpose(1, 0, 2)


def _from_chip_cols(s):
    n, r, cols = s.shape
    return s.transpose(1, 0, 2).reshape(r, n * cols)


def kernel(x, p, w_in, b_f, pre_gain, post_gain, conv_w, conv_b, w_rgate, b_rgate, w_igate, b_igate, lru_lambda, attn_out_gain, lru_out_gain, w_out, w_ple, ple_gain, w_ple_gate, b_ple_gate, loss_target, m_w_in, m_b_f, m_pre_gain, m_post_gain, m_conv_w, m_conv_b, m_w_rgate, m_b_rgate, m_w_igate, m_b_igate, m_lru_lambda, m_attn_out_gain, m_lru_out_gain, m_w_out, m_w_ple, m_ple_gain, m_w_ple_gate, m_b_ple_gate, v_w_in, v_b_f, v_pre_gain, v_post_gain, v_conv_w, v_conv_b, v_w_rgate, v_b_rgate, v_w_igate, v_b_igate, v_lru_lambda, v_attn_out_gain, v_lru_out_gain, v_w_out, v_w_ple, v_ple_gain, v_w_ple_gate, v_b_ple_gate):
    w = dict(w_in=w_in, b_f=b_f, pre_gain=pre_gain, post_gain=post_gain, conv_w=conv_w, conv_b=conv_b,
             w_rgate=w_rgate, b_rgate=b_rgate, w_igate=w_igate, b_igate=b_igate, lru_lambda=lru_lambda,
             attn_out_gain=attn_out_gain, lru_out_gain=lru_out_gain, w_out=w_out, w_ple=w_ple, ple_gain=ple_gain,
             w_ple_gate=w_ple_gate, b_ple_gate=b_ple_gate)
    m = dict(w_in=m_w_in, b_f=m_b_f, pre_gain=m_pre_gain, post_gain=m_post_gain, conv_w=m_conv_w, conv_b=m_conv_b,
             w_rgate=m_w_rgate, b_rgate=m_b_rgate, w_igate=m_w_igate, b_igate=m_b_igate, lru_lambda=m_lru_lambda,
             attn_out_gain=m_attn_out_gain, lru_out_gain=m_lru_out_gain, w_out=m_w_out, w_ple=m_w_ple,
             ple_gain=m_ple_gain, w_ple_gate=m_w_ple_gate, b_ple_gate=m_b_ple_gate)
    v = dict(w_in=v_w_in, b_f=v_b_f, pre_gain=v_pre_gain, post_gain=v_post_gain, conv_w=v_conv_w, conv_b=v_conv_b,
             w_rgate=v_w_rgate, b_rgate=v_b_rgate, w_igate=v_w_igate, b_igate=v_b_igate, lru_lambda=v_lru_lambda,
             attn_out_gain=v_attn_out_gain, lru_out_gain=v_lru_out_gain, w_out=v_w_out, w_ple=v_w_ple,
             ple_gain=v_ple_gain, w_ple_gate=v_w_ple_gate, b_ple_gate=v_b_ple_gate)
    xi, yi, ci = _position()
    chip = 2 * xi + yi

    w_in_t, m_in_t, v_in_t = (jnp.swapaxes(t[0], 0, 1) for t in (w_in, m_w_in, v_w_in))
    window = jnp.pad(w_in_t.astype(BF16), ((0, W_ROWS - SHARD_ROWS), (0, 0)))

    st_in, st_out, st_ple, st_gate, st_conv = _gather_shards(
        [window, w_out[0].astype(BF16), w_ple[0].astype(BF16), w_ple_gate[0].astype(BF16)], [conv_w[0]])
    w_a, w_f, w_b = _assemble_w_in(st_in)
    w_out_b = st_out.reshape(DMIX, D)
    w_ple_b = _from_chip_cols(st_ple)
    w_gate_b = st_gate.reshape(D, D)
    conv_full = _from_chip_cols(st_conv)

    loss, grad_x, g = _local_step(
        x[0], p[0, 0], loss_target[0], w_a, w_f, w_b, w_out_b, w_ple_b, w_gate_b, conv_full, b_f, pre_gain, post_gain,
        conv_b, w_rgate[0], b_rgate, w_igate[0], b_igate, lru_lambda, attn_out_gain, lru_out_gain, ple_gain,
        b_ple_gate)
    loss = lax.psum(loss, ("x", "y", "c"))

    parts = [g["w_out"].reshape(N_CHIPS, DMIX // N_CHIPS, D), _by_chip_cols(g["w_ple"]),
             g["w_ple_gate"].reshape(N_CHIPS, D // N_CHIPS, D)]
    got = _pair_exchange(g["w_in_t"], parts)
    sums = [_pair_sum_windows(g["w_in_t"], got[0], ci)]
    sums += [_pair_sum(parts[a], got[1 + a], ci, "pair_sum_%d" % a) for a in range(3)]
    recv = _chip_exchange(sums)
    halves = [_chip_sum(sums[a], recv[a], chip, "chip_sum_%d" % a) for a in range(4)]
    theirs = _pair_swap(halves)
    full = [jnp.concatenate([jnp.where(ci == 0, a, b), jnp.where(ci == 0, b, a)], axis=0)
            for a, b in zip(halves, theirs)]
    red = dict(zip(SHARDED, full))
    red["w_in"] = lax.dynamic_slice_in_dim(red["w_in"], 2 * chip, SHARD_ROWS, axis=0)

    rows = [jnp.pad(g["b_f"], ((0, 0), (0, D - H)))] + [g[n] for n in SMALL_ROWS[1:]]
    rows.append(jnp.zeros((16 - sum(r.shape[0] for r in rows), D), F32))
    packed = jnp.concatenate([g["w_rgate"].reshape(NB * LANES, LANES), g["w_igate"].reshape(NB * LANES, LANES),
                              jnp.concatenate(rows, axis=0).reshape(LANES, LANES)], axis=0)
    summed = _allreduce_small(packed)
    red["w_rgate"] = summed[:D].reshape(1, NB, LANES, LANES)
    red["w_igate"] = summed[D:2 * D].reshape(1, NB, LANES, LANES)
    vec = summed[2 * D:].reshape(16, D)
    r0 = 0
    for n in SMALL_ROWS:
        nr = 4 if n == "conv_w" else 1
        red[n] = vec[r0:r0 + nr]
        r0 += nr
    red["b_f"] = red["b_f"][:, :H]
    red["conv_w"] = lax.dynamic_slice_in_dim(red["conv_w"], chip * (D // N_CHIPS), D // N_CHIPS, axis=1)[None]

    delta, new_m, new_v = {}, {}, {}
    outs_in = _adamw_big(red["w_in"], w_in_t, m_in_t, v_in_t, "adamw_w_in")
    delta["w_in"], new_m["w_in"], new_v["w_in"] = (jnp.swapaxes(t, 0, 1)[None] for t in outs_in)
    red["w_in"] = jnp.swapaxes(red["w_in"], 0, 1)[None]
    for n in SHARDED[1:]:
        delta[n], new_m[n], new_v[n] = (t[None] for t in _adamw_big(red[n], w[n][0], m[n][0], v[n][0], "adamw_" + n))
        red[n] = red[n][None]
    small = [n for n in WEIGHTS if n not in SHARDED]
    outs = _adamw_small([red[n] for n in small], [w[n] for n in small], [m[n] for n in small],
                        [v[n] for n in small])
    ns = len(small)
    for a, n in enumerate(small):
        delta[n], new_m[n], new_v[n] = outs[a], outs[ns + a], outs[2 * ns + a]

    return (loss, grad_x[None], *[red[n] for n in WEIGHTS], *[delta[n] for n in WEIGHTS],
            *[new_m[n] for n in WEIGHTS], *[new_v[n] for n in WEIGHTS])
```

```python
import functools

import jax
import jax.numpy as jnp
from jax import lax
from jax.experimental import pallas as pl
from jax.experimental.pallas import tpu as pltpu

F32 = jnp.float32
BF16 = jnp.bfloat16

D = 1024
H = 8
DH = 128
NB = 8
DPLE = 256
DMIX = 2 * D
D_IN = 4 * D + H + 2 * D
FL0 = 3 * D
RMS_EPS = 1e-6
LRU_C = 8.0
NEG = -1e30
LANES = 128
SUBLANES = 8

ADAM_LR = 0.001
ADAM_B1 = 0.9
ADAM_B2 = 0.999
ADAM_EPS = 1e-08
ADAM_WD = 0.01
ADAM_STEP = 10

TM = 256
TA = 512
VMEM_BIG = 56 * 1024 * 1024
VMEM_MID = 40 * 1024 * 1024

MESH = pl.DeviceIdType.MESH
N_CHIPS = 4
N_DEV = 8


def _cparams(sem, vmem=VMEM_MID):
    return pltpu.CompilerParams(dimension_semantics=sem, vmem_limit_bytes=vmem)


def _sigmoid(x):
    return 1.0 / (1.0 + jnp.exp(-x))


def _rstd(x):
    return lax.rsqrt(jnp.mean(x * x, axis=-1, keepdims=True) + RMS_EPS)


def _rms_bwd(t, xhat, rstd):
    return rstd * (t - xhat * jnp.mean(t * xhat, axis=-1, keepdims=True))


def _dot(a, b):
    return jnp.dot(a, b, preferred_element_type=F32)


def _dot_nt(a, b):
    return lax.dot_general(a, b, (((1,), (1,)), ((), ())), preferred_element_type=F32)


def _dot_tn(a, b):
    return lax.dot_general(a, b, (((0,), (0,)), ((), ())), preferred_element_type=F32)


def _dot_exact(a, b):
    return jnp.dot(a, b, preferred_element_type=F32, precision=lax.Precision.HIGHEST)


def _neg_expm1(x):
    series = x * (1.0 + x * 0.5 * (1.0 + x * (1.0 / 3.0) * (1.0 + x * 0.25 * (1.0 + x * 0.2 * (1.0 + x * (1.0 / 6.0))))))
    return -jnp.where(x > -0.25, series, jnp.exp(x) - 1.0)


def _shift_down(x, j, halo):
    rolled = pltpu.roll(x, j, 0)
    row = lax.broadcasted_iota(jnp.int32, halo.shape, 0)
    top = jnp.where(row < j, pltpu.roll(halo, j, 0), rolled[:SUBLANES])
    return jnp.concatenate([top, rolled[SUBLANES:]], axis=0)


def _shift_up(x, j, nxt):
    tm = x.shape[0]
    rolled = pltpu.roll(x, tm - j, 0)
    row = lax.broadcasted_iota(jnp.int32, nxt.shape, 0)
    bot = jnp.where(row >= SUBLANES - j, pltpu.roll(nxt, SUBLANES - j, 0), rolled[tm - SUBLANES:])
    return jnp.concatenate([rolled[:tm - SUBLANES], bot], axis=0)


def _scan_fwd(a, u):
    tm = a.shape[0]
    row = lax.broadcasted_iota(jnp.int32, a.shape, 0)
    d = 1
    while d < tm:
        keep = row >= d
        a_s = jnp.where(keep, pltpu.roll(a, d, 0), 1.0)
        u_s = jnp.where(keep, pltpu.roll(u, d, 0), 0.0)
        u = u + a * u_s
        a = a * a_s
        d *= 2
    return a, u


def _scan_bwd(b, u):
    tm = b.shape[0]
    row = lax.broadcasted_iota(jnp.int32, b.shape, 0)
    d = 1
    while d < tm:
        keep = row < tm - d
        b_s = jnp.where(keep, pltpu.roll(b, tm - d, 0), 1.0)
        u_s = jnp.where(keep, pltpu.roll(u, tm - d, 0), 0.0)
        u = u + b * u_s
        b = b * b_s
        d *= 2
    return u


def _gate_pre(xc, w_ref):
    outs = []
    for n in range(NB):
        outs.append(_dot(xc[:, n * LANES:(n + 1) * LANES].astype(BF16), w_ref[n]))
    return jnp.concatenate(outs, axis=1)


def _gate_pre_t(d, w_ref):
    outs = []
    for n in range(NB):
        outs.append(_dot_nt(d[:, n * LANES:(n + 1) * LANES].astype(BF16), w_ref[n]))
    return jnp.concatenate(outs, axis=1)


def _softplus_neg(lam):
    return jnp.maximum(-lam, 0.0) + jnp.log(1.0 + jnp.exp(-jnp.abs(lam)))


def _row_spec(tm, width):
    return pl.BlockSpec((tm, width), lambda i: (i, 0))


def _const_spec(shape):
    nd = len(shape)
    return pl.BlockSpec(shape, lambda *_: (0,) * nd)


AUG = 2 * DH
LOG2E = 1.4426950408889634
LN2 = 0.6931471805599453
Q_SCALE = DH ** -0.5 * LOG2E


def _split3(x):
    hi = x.astype(BF16)
    r1 = x - hi.astype(F32)
    mid = r1.astype(BF16)
    lo = (r1 - mid.astype(F32)).astype(BF16)
    return hi, mid, lo


def _extras(col, ones_from):
    t = col.shape[0]
    hi, mid, lo = _split3(jnp.broadcast_to(col, (t, LANES)))
    lane = lax.broadcasted_iota(jnp.int32, (t, LANES), 1)
    rest = jnp.zeros((t, LANES), BF16)
    if ones_from is not None:
        rest = jnp.where((lane >= ones_from) & (lane < ones_from + 3), 1.0, 0.0).astype(BF16)
    return jnp.where(lane == 0, hi, jnp.where(lane == 1, mid, jnp.where(lane == 2, lo, rest)))


def _selectors():
    import numpy as np
    sel_q = np.zeros((H, 3 * LANES, LANES), np.float32)
    sel_k = np.zeros((H, 3 * LANES, LANES), np.float32)
    for hd in range(H):
        for piece in range(3):
            sel_q[hd, piece * LANES + hd, piece] = 1.0
            sel_k[hd, piece * LANES + hd, 3 + piece] = -1.0
    return jnp.asarray(sel_q, BF16), jnp.asarray(sel_k, BF16)


def _in_proj(x, pre_gain, w_a, w_f, w_b, b_f_pad):
    T = x.shape[0]
    tm = TM
    sel_q, sel_k = _selectors()

    def body(x_ref, g_ref, wa_ref, wf_ref, wb_ref, bf_ref, sq_ref, sk_ref,
             xn_ref, qa_ref, ka_ref, va_ref, ga_ref, xl_ref, gl_ref, flb_ref, c_s, carry):
        @pl.when(pl.program_id(0) == 0)
        def _():
            carry[...] = jnp.zeros_like(carry)

        xv = x_ref[...]
        xn = (xv * _rstd(xv) * g_ref[...]).astype(BF16)
        xn_ref[...] = xn
        for s, o_ref in enumerate((ga_ref, xl_ref, gl_ref)):
            o_ref[...] = _dot_nt(xn, wb_ref[s * D:(s + 1) * D, :]).astype(o_ref.dtype)
        flb = _dot_nt(xn, wf_ref[...]) + bf_ref[...]
        flb_ref[...] = flb
        lane = lax.broadcasted_iota(jnp.int32, flb.shape, 1)
        ls = jnp.where(lane < H, jnp.minimum(flb, 0.0) - jnp.log(1.0 + jnp.exp(-jnp.abs(flb))), 0.0)
        r = lax.broadcasted_iota(jnp.int32, (tm, tm), 0)
        c = lax.broadcasted_iota(jnp.int32, (tm, tm), 1)
        cs = _dot_exact((c <= r).astype(F32), ls) + carry[...]
        c_s[...] = cs
        carry[...] = c_s[tm - 1:tm, :]

        pieces = jnp.concatenate(_split3(cs * LOG2E), axis=1)
        ones_q = jnp.where((lane >= 3) & (lane < 6), 1.0, 0.0)
        ones_k = jnp.where(lane < 3, 1.0, 0.0)
        zq = _dot_nt(xn, wa_ref[0:D, :]) * Q_SCALE
        zk = _dot_nt(xn, wa_ref[D:2 * D, :])
        zv = _dot_nt(xn, wa_ref[2 * D:3 * D, :])
        for hd in range(H):
            head = slice(hd * DH, (hd + 1) * DH)
            lo, hi = hd * AUG, hd * AUG + DH
            qa_ref[:, lo:hi] = zq[:, head].astype(BF16)
            qa_ref[:, hi:hi + DH] = (_dot(pieces, sq_ref[hd]) + ones_q).astype(BF16)
            ka_ref[:, lo:hi] = zk[:, head].astype(BF16)
            ka_ref[:, hi:hi + DH] = (_dot(pieces, sk_ref[hd]) + ones_k).astype(BF16)
            va_ref[:, lo:hi] = zv[:, head].astype(BF16)
            va_ref[:, hi:hi + DH] = ones_k.astype(BF16)

    bf = jax.ShapeDtypeStruct((T, D), BF16)
    aug = jax.ShapeDtypeStruct((T, H * AUG), BF16)
    f32 = jax.ShapeDtypeStruct((T, D), F32)
    sel_spec = _const_spec((H, 3 * LANES, LANES))
    return pl.pallas_call(
        body, name="in_proj", grid=(T // tm,),
        in_specs=[_row_spec(tm, D), _const_spec((1, D)), _const_spec((3 * D, D)), _const_spec((LANES, D)),
                  _const_spec((3 * D, D)), _const_spec((1, LANES)), sel_spec, sel_spec],
        out_specs=[_row_spec(tm, D)] + [_row_spec(tm, H * AUG)] * 3 + [_row_spec(tm, D)] * 3 + [_row_spec(tm, LANES)],
        out_shape=[bf, aug, aug, aug, f32, f32, f32, jax.ShapeDtypeStruct((T, LANES), F32)],
        scratch_shapes=[pltpu.VMEM((tm, LANES), F32), pltpu.VMEM((1, LANES), F32)],
        compiler_params=_cparams(("arbitrary",), VMEM_BIG),
    )(x, pre_gain, w_a, w_f, w_b, b_f_pad, sel_q, sel_k)


def _attn_fwd(q_aug, k_aug, v_aug):
    T = q_aug.shape[0]
    t = TA
    n = T // t

    def body(q_ref, k_ref, v_ref, o_ref, qx_ref, m_s, acc_s):
        qi = pl.program_id(1)
        ki = pl.program_id(2)

        @pl.when(ki == 0)
        def _():
            m_s[...] = jnp.full(m_s.shape, NEG, F32)
            acc_s[...] = jnp.zeros_like(acc_s)

        def step(on_diagonal):
            s = _dot_nt(q_ref[...], k_ref[...])
            if on_diagonal:
                row = lax.broadcasted_iota(jnp.int32, (t, t), 0)
                col = lax.broadcasted_iota(jnp.int32, (t, t), 1)
                s = jnp.where(col <= row, s, NEG)
            m_prev = m_s[...]
            m_new = jnp.maximum(m_prev, jnp.max(s, axis=1, keepdims=True))
            pr = jnp.exp2(s - m_new).astype(BF16)
            acc_s[...] = jnp.exp2(m_prev - m_new) * acc_s[...] + _dot(pr, v_ref[...])
            m_s[...] = m_new

        @pl.when(ki < qi)
        def _():
            step(False)

        @pl.when(ki == qi)
        def _():
            step(True)
            acc = acc_s[...]
            l = acc[:, DH:DH + 1]
            o_ref[...] = acc[:, :DH] / l
            ex = q_ref[:, DH:].astype(F32)
            c2 = ex[:, 0:1] + ex[:, 1:2] + ex[:, 2:3]
            qx_ref[...] = _extras(c2 - (m_s[...] + jnp.log(l) * LOG2E), 3)

    q_spec = pl.BlockSpec((t, AUG), lambda h, qi, ki: (qi, h))
    kv_spec = pl.BlockSpec((t, AUG), lambda h, qi, ki: (jnp.minimum(ki, qi), h))
    out_spec = pl.BlockSpec((t, DH), lambda h, qi, ki: (qi, h))
    return pl.pallas_call(
        body, name="attn_fwd", grid=(H, n, n),
        in_specs=[q_spec, kv_spec, kv_spec],
        out_specs=[out_spec, out_spec],
        out_shape=[jax.ShapeDtypeStruct((T, D), F32), jax.ShapeDtypeStruct((T, D), BF16)],
        scratch_shapes=[pltpu.VMEM((t, 1), F32), pltpu.VMEM((t, AUG), F32)],
        compiler_params=_cparams(("parallel", "parallel", "arbitrary")),
    )(q_aug, k_aug, v_aug)


def _lru_gates(xc, wr_ref, br_ref, wi_ref, bi_ref, lam_ref):
    r = _sigmoid(_gate_pre(xc, wr_ref) + br_ref[...])
    ig = _sigmoid(_gate_pre(xc, wi_ref) + bi_ref[...])
    sp = _softplus_neg(lam_ref[...])
    la = (-LRU_C) * r * sp
    a = jnp.exp(la)
    sq = jnp.sqrt(_neg_expm1(2.0 * la))
    return r, ig, sp, a, sq


def _branches_fwd(o, g_attn, x_lru, g_lru, gain_a, gain_l, conv_w, conv_b, w_r, b_r, w_i, b_i, lam):
    T = o.shape[0]
    tm = TM

    def body(o_ref, ga_ref, xl_ref, gl_ref, gna_ref, gnl_ref, cw_ref, cb_ref, wr_ref, br_ref, wi_ref, bi_ref,
             lam_ref, ycat_ref, xc_ref, h_ref, halo_s, hc_s):
        @pl.when(pl.program_id(0) == 0)
        def _():
            halo_s[...] = jnp.zeros_like(halo_s)
            hc_s[...] = jnp.zeros_like(hc_s)

        ov = o_ref[...]
        ga = ga_ref[...]
        ya = ov * _rstd(ov) * gna_ref[...] * (ga * _sigmoid(ga))
        ycat_ref[:, :D] = ya.astype(BF16)

        xl = xl_ref[...]
        halo = halo_s[...]
        xc = xl * cw_ref[3:4, :] + cb_ref[...]
        for j in range(3):
            xc = xc + _shift_down(xl, 3 - j, halo) * cw_ref[j:j + 1, :]
        halo_s[...] = xl_ref[tm - SUBLANES:tm, :]
        xc_ref[...] = xc

        _, ig, _, a, sq = _lru_gates(xc, wr_ref, br_ref, wi_ref, bi_ref, lam_ref)
        u = sq * (ig * xc)
        a_cum, h_loc = _scan_fwd(a, u)
        hh = h_loc + a_cum * hc_s[...]
        h_ref[...] = hh
        hc_s[...] = h_ref[tm - 1:tm, :]

        gl = gl_ref[...]
        yl = hh * _rstd(hh) * gnl_ref[...] * (gl * _sigmoid(gl))
        ycat_ref[:, D:] = yl.astype(BF16)

    vec = _const_spec((1, D))
    wspec = _const_spec((NB, LANES, LANES))
    return pl.pallas_call(
        body, name="branches_fwd", grid=(T // tm,),
        in_specs=[_row_spec(tm, D)] * 4 + [vec, vec, _const_spec((4, D)), vec, wspec, vec, wspec, vec, vec],
        out_specs=[_row_spec(tm, DMIX), _row_spec(tm, D), _row_spec(tm, D)],
        out_shape=[jax.ShapeDtypeStruct((T, DMIX), BF16), jax.ShapeDtypeStruct((T, D), F32),
                   jax.ShapeDtypeStruct((T, D), F32)],
        scratch_shapes=[pltpu.VMEM((SUBLANES, D), F32), pltpu.VMEM((1, D), F32)],
        compiler_params=_cparams(("arbitrary",)),
    )(o, g_attn, x_lru, g_lru, gain_a, gain_l, conv_w, conv_b, w_r, b_r, w_i, b_i, lam)


def _tail(ycat, x, p, tgt, w_out, post_gain, w_ple, ple_gain, w_gate, b_gate):
    T = x.shape[0]
    tm = TM

    def body(ycat_ref, x_ref, p_ref, t_ref, wo_ref, pg_ref, wp_ref, eg_ref, wg_ref, bg_ref,
             dh1_ref, dycat_ref, dmix_ref, h1b_ref, dgp_ref, pb_ref, dpe_ref, acc_ref):
        @pl.when(pl.program_id(0) == 0)
        def _():
            acc_ref[...] = jnp.zeros_like(acc_ref)

        mix = _dot(ycat_ref[...], wo_ref[...])
        rstd_m = _rstd(mix)
        mhat = mix * rstd_m
        h1 = x_ref[...] + mhat * pg_ref[...]
        pb = p_ref[...].astype(BF16)
        pb_ref[...] = pb
        pe = _dot(pb, wp_ref[...])
        rstd_p = _rstd(pe)
        pehat = pe * rstd_p
        e = pehat * eg_ref[...]
        h1b = h1.astype(BF16)
        h1b_ref[...] = h1b
        gate = _sigmoid(_dot(h1b, wg_ref[...]) + bg_ref[...])
        diff = (h1 + gate * e) - t_ref[...]

        dy = diff * (1.0 / D)
        de = dy * gate
        dgp = (dy * e) * gate * (1.0 - gate)
        dgpb = dgp.astype(BF16)
        dgp_ref[...] = dgpb
        dh1 = dy + _dot_nt(dgpb, wg_ref[...])
        dh1_ref[...] = dh1
        dpe_ref[...] = _rms_bwd(de * eg_ref[...], pehat, rstd_p).astype(BF16)
        dmix = _rms_bwd(dh1 * pg_ref[...], mhat, rstd_m).astype(BF16)
        dmix_ref[...] = dmix
        dycat_ref[...] = _dot_nt(dmix, wo_ref[...])

        acc_ref[0:1, :] += jnp.sum(dh1 * mhat, axis=0, keepdims=True)
        acc_ref[1:2, :] += jnp.sum(de * pehat, axis=0, keepdims=True)
        acc_ref[2:3, :] += jnp.sum(dgp, axis=0, keepdims=True)
        acc_ref[3:4, :] += jnp.sum(diff * diff, axis=0, keepdims=True) * (0.5 / D)

    vec = _const_spec((1, D))
    bf = jax.ShapeDtypeStruct((T, D), BF16)
    return pl.pallas_call(
        body, name="tail", grid=(T // tm,),
        in_specs=[_row_spec(tm, DMIX), _row_spec(tm, D), _row_spec(tm, DPLE), _row_spec(tm, D),
                  _const_spec((DMIX, D)), vec, _const_spec((DPLE, D)), vec, _const_spec((D, D)), vec],
        out_specs=[_row_spec(tm, D), _row_spec(tm, DMIX), _row_spec(tm, D), _row_spec(tm, D), _row_spec(tm, D),
                   _row_spec(tm, DPLE), _row_spec(tm, D), _const_spec((SUBLANES, D))],
        out_shape=[jax.ShapeDtypeStruct((T, D), F32), jax.ShapeDtypeStruct((T, DMIX), F32), bf, bf, bf,
                   jax.ShapeDtypeStruct((T, DPLE), BF16), bf, jax.ShapeDtypeStruct((SUBLANES, D), F32)],
        compiler_params=_cparams(("arbitrary",), VMEM_BIG),
    )(ycat, x, p, tgt, w_out, post_gain, w_ple, ple_gain, w_gate, b_gate)


def _branches_bwd(dycat, o, g_attn, h, g_lru, gain_a, gain_l):
    T = o.shape[0]
    tm = TM

    def body(dy_ref, o_ref, ga_ref, h_ref, gl_ref, gna_ref, gnl_ref,
             do_ref, dga_ref, dgl_ref, dh_ref, acc_ref):
        @pl.when(pl.program_id(0) == 0)
        def _():
            acc_ref[...] = jnp.zeros_like(acc_ref)

        def branch(val, g, gain, dyv):
            rstd = _rstd(val)
            vhat = val * rstd
            sig = _sigmoid(g)
            dn = dyv * (g * sig)
            dg = dyv * (vhat * gain) * (sig * (1.0 + g * (1.0 - sig)))
            dgain = jnp.sum(dn * vhat, axis=0, keepdims=True)
            return _rms_bwd(dn * gain, vhat, rstd), dg, dgain

        ov = o_ref[...]
        do, dga, dgain_a = branch(ov, ga_ref[...], gna_ref[...], dy_ref[:, :D])
        dga_ref[...] = dga.astype(BF16)
        prod = do * ov
        for hd in range(H):
            head = slice(hd * DH, (hd + 1) * DH)
            do_ref[:, hd * AUG:hd * AUG + DH] = do[:, head].astype(BF16)
            do_ref[:, hd * AUG + DH:(hd + 1) * AUG] = _extras(-jnp.sum(prod[:, head], axis=1, keepdims=True), None)

        dh, dgl, dgain_l = branch(h_ref[...], gl_ref[...], gnl_ref[...], dy_ref[:, D:])
        dh_ref[...] = dh
        dgl_ref[...] = dgl.astype(BF16)
        acc_ref[0:1, :] += dgain_a
        acc_ref[1:2, :] += dgain_l

    vec = _const_spec((1, D))
    bf = jax.ShapeDtypeStruct((T, D), BF16)
    return pl.pallas_call(
        body, name="branches_bwd", grid=(T // tm,),
        in_specs=[_row_spec(tm, DMIX)] + [_row_spec(tm, D)] * 4 + [vec, vec],
        out_specs=[_row_spec(tm, H * AUG), _row_spec(tm, D), _row_spec(tm, D), _row_spec(tm, D),
                   _const_spec((SUBLANES, D))],
        out_shape=[jax.ShapeDtypeStruct((T, H * AUG), BF16), bf, bf, jax.ShapeDtypeStruct((T, D), F32),
                   jax.ShapeDtypeStruct((SUBLANES, D), F32)],
        compiler_params=_cparams(("arbitrary",)),
    )(dycat, o, g_attn, h, g_lru, gain_a, gain_l)


def _lru_bwd(dh, h, xc, x_lru, conv_w, w_r, b_r, w_i, b_i, lam):
    T = dh.shape[0]
    tm = TM
    nt = T // tm
    per = tm // SUBLANES

    def body(dh_ref, h_ref, hprev_ref, xc_ref, xl_ref, xlprev_ref, cw_ref, wr_ref, br_ref, wi_ref, bi_ref, lam_ref,
             dxl_ref, dwr_ref, dwi_ref, acc_ref, carry_s, dxc_next_s, top_s):
        i = pl.program_id(0)

        @pl.when(i == 0)
        def _():
            acc_ref[...] = jnp.zeros_like(acc_ref)
            dwr_ref[...] = jnp.zeros_like(dwr_ref)
            dwi_ref[...] = jnp.zeros_like(dwi_ref)
            carry_s[...] = jnp.zeros_like(carry_s)
            dxc_next_s[...] = jnp.zeros_like(dxc_next_s)

        inner = jnp.where(i == nt - 1, 0.0, 1.0)
        xc = xc_ref[...]
        r, ig, sp, a, sq = _lru_gates(xc, wr_ref, br_ref, wi_ref, bi_ref, lam_ref)

        row = lax.broadcasted_iota(jnp.int32, (tm, D), 0)
        u = dh_ref[...] + jnp.where(row == tm - 1, carry_s[...], 0.0)
        dht = _scan_bwd(pltpu.roll(a, tm - 1, 0), u)
        top_s[...] = a[:SUBLANES, :] * dht[:SUBLANES, :]
        carry_s[...] = top_s[0:1, :]

        hprev = hprev_ref[...] * inner
        da = dht * _shift_down(h_ref[...], 1, hprev)
        dig = dht * sq * xc
        dxc = dht * sq * ig
        dsq = dht * ig * xc
        dla = da * a - dsq * (a * a) / sq
        dr = dla * ((-LRU_C) * sp)
        dpr = dr * r * (1.0 - r)
        dpi = dig * ig * (1.0 - ig)
        for n in range(NB):
            blk = slice(n * LANES, (n + 1) * LANES)
            xcb = xc[:, blk].astype(BF16)
            dwr_ref[n] += _dot_tn(xcb, dpr[:, blk].astype(BF16))
            dwi_ref[n] += _dot_tn(xcb, dpi[:, blk].astype(BF16))
        dxc = dxc + _gate_pre_t(dpr, wr_ref) + _gate_pre_t(dpi, wi_ref)

        xl = xl_ref[...]
        xlprev = xlprev_ref[...] * inner
        nxt = dxc_next_s[...]
        dxl = dxc * cw_ref[3:4, :]
        acc_ref[3:4, :] += jnp.sum(dxc * xl, axis=0, keepdims=True)
        for j in range(3):
            dxl = dxl + _shift_up(dxc, 3 - j, nxt) * cw_ref[j:j + 1, :]
            acc_ref[j:j + 1, :] += jnp.sum(dxc * _shift_down(xl, 3 - j, xlprev), axis=0, keepdims=True)
        dxc_next_s[...] = dxc[:SUBLANES, :]
        dxl_ref[...] = dxl.astype(BF16)

        acc_ref[4:5, :] += jnp.sum(dxc, axis=0, keepdims=True)
        acc_ref[5:6, :] += jnp.sum(dpr, axis=0, keepdims=True)
        acc_ref[6:7, :] += jnp.sum(dpi, axis=0, keepdims=True)
        acc_ref[7:8, :] += jnp.sum(dla * ((-LRU_C) * r), axis=0, keepdims=True)

        @pl.when(i == nt - 1)
        def _():
            lam_v = lam_ref[...]
            acc_ref[7:8, :] = acc_ref[7:8, :] * (-_sigmoid(-lam_v))

    rev = pl.BlockSpec((tm, D), lambda i: (nt - 1 - i, 0))
    prev8 = pl.BlockSpec((SUBLANES, D), lambda i: (jnp.maximum((nt - 1 - i) * per - 1, 0), 0))
    vec = _const_spec((1, D))
    wspec = _const_spec((NB, LANES, LANES))
    bf = jax.ShapeDtypeStruct((T, D), BF16)
    return pl.pallas_call(
        body, name="lru_bwd", grid=(nt,),
        in_specs=[rev, rev, prev8, rev, rev, prev8, _const_spec((4, D)), wspec, vec, wspec, vec, vec],
        out_specs=[rev, wspec, wspec, _const_spec((SUBLANES, D))],
        out_shape=[bf, jax.ShapeDtypeStruct((NB, LANES, LANES), F32), jax.ShapeDtypeStruct((NB, LANES, LANES), F32),
                   jax.ShapeDtypeStruct((SUBLANES, D), F32)],
        scratch_shapes=[pltpu.VMEM((1, D), F32), pltpu.VMEM((SUBLANES, D), F32), pltpu.VMEM((SUBLANES, D), F32)],
        compiler_params=_cparams(("arbitrary",)),
    )(dh, h, h, xc, x_lru, x_lru, conv_w, w_r, b_r, w_i, b_i, lam)


def _attn_bwd(q_aug, qx, k_aug, v_aug, do_aug):
    T = q_aug.shape[0]
    t = TA
    n = T // t
    scale = DH ** -0.5

    def body(q_ref, qx_ref, k_ref, v_ref, do_ref, dq_ref, dk_ref, dv_ref, dck_ref, dcq_ref, dq_s, dk_s, dv_s):
        ki = pl.program_id(1)
        qi = pl.program_id(2)

        @pl.when((ki == 0) & (qi == 0))
        def _():
            dq_s[...] = jnp.zeros_like(dq_s)

        @pl.when(qi == 0)
        def _():
            dk_s[...] = jnp.zeros_like(dk_s)
            dv_s[...] = jnp.zeros_like(dv_s)

        def step(on_diagonal):
            qb = jnp.concatenate([q_ref[:, :DH], qx_ref[...]], axis=1)
            kv = k_ref[...]
            dov = do_ref[...]
            st = _dot_nt(kv, qb)
            if on_diagonal:
                krow = lax.broadcasted_iota(jnp.int32, (t, t), 0)
                qcol = lax.broadcasted_iota(jnp.int32, (t, t), 1)
                st = jnp.where(krow <= qcol, st, NEG)
            pt = jnp.exp2(st)
            dv_s[...] += _dot(pt.astype(BF16), dov)
            dsb = (pt * _dot_nt(v_ref[...], dov)).astype(BF16)
            dk_s[...] += _dot(dsb, qb)
            off = pl.multiple_of(qi * t, t)
            dq_s[pl.ds(off, t), :] += _dot_tn(dsb, kv)

        @pl.when(qi > ki)
        def _():
            step(False)

        @pl.when(qi == ki)
        def _():
            step(True)

        @pl.when(qi == n - 1)
        def _():
            dk_ref[...] = (dk_s[:, :DH] * LN2).astype(BF16)
            dv_ref[...] = dv_s[:, :DH].astype(BF16)
            dck_ref[0] = jnp.broadcast_to(dk_s[:, DH + 3:DH + 4], (t, LANES))

        @pl.when((ki == n - 1) & (qi == n - 1))
        def _():
            dq_ref[...] = (dq_s[:, :DH] * scale).astype(BF16)
            dcq_ref[0] = jnp.broadcast_to(dq_s[:, DH:DH + 1], (T, LANES))

    qside = pl.BlockSpec((t, AUG), lambda h, ki, qi: (jnp.maximum(qi, ki), h))
    qxside = pl.BlockSpec((t, DH), lambda h, ki, qi: (jnp.maximum(qi, ki), h))
    kside = pl.BlockSpec((t, AUG), lambda h, ki, qi: (ki, h))
    kout = pl.BlockSpec((t, DH), lambda h, ki, qi: (ki, h))
    bf = jax.ShapeDtypeStruct((T, D), BF16)
    sums = jax.ShapeDtypeStruct((H, T, LANES), F32)
    return pl.pallas_call(
        body, name="attn_bwd", grid=(H, n, n),
        in_specs=[qside, qxside, kside, kside, qside],
        out_specs=[pl.BlockSpec((T, DH), lambda h, ki, qi: (0, h)), kout, kout,
                   pl.BlockSpec((1, t, LANES), lambda h, ki, qi: (h, ki, 0)),
                   pl.BlockSpec((1, T, LANES), lambda h, ki, qi: (h, 0, 0))],
        out_shape=[bf, bf, bf, sums, sums],
        scratch_shapes=[pltpu.VMEM((T, AUG), F32), pltpu.VMEM((t, AUG), F32), pltpu.VMEM((t, AUG), F32)],
        compiler_params=_cparams(("arbitrary", "arbitrary", "arbitrary"), VMEM_BIG),
    )(q_aug, qx, k_aug, v_aug, do_aug)


def _fgate_bwd(dc_key, dc_query, flb):
    T = flb.shape[0]
    tm = TM
    nt = T // tm

    def body(dck_ref, dcq_ref, flb_ref, dfl_ref, acc_ref, carry, top_s):
        @pl.when(pl.program_id(0) == 0)
        def _():
            carry[...] = jnp.zeros_like(carry)
            acc_ref[...] = jnp.zeros_like(acc_ref)

        flb = flb_ref[...]
        lane = lax.broadcasted_iota(jnp.int32, flb.shape, 1)
        dc = jnp.zeros(flb.shape, F32)
        for hd in range(H):
            dc = dc + jnp.where(lane == hd, dcq_ref[hd] - dck_ref[hd], 0.0)
        r = lax.broadcasted_iota(jnp.int32, (tm, tm), 0)
        c = lax.broadcasted_iota(jnp.int32, (tm, tm), 1)
        dls = _dot_exact((c >= r).astype(F32), dc) + carry[...]
        top_s[...] = dls[:SUBLANES, :]
        carry[...] = top_s[0:1, :]
        dfl = jnp.where(lane < H, dls * _sigmoid(-flb), 0.0)
        dfl_ref[...] = dfl.astype(BF16)
        acc_ref[0:1, :] += jnp.sum(dfl, axis=0, keepdims=True)

    rev = pl.BlockSpec((tm, LANES), lambda i: (nt - 1 - i, 0))
    return pl.pallas_call(
        body, name="fgate_bwd", grid=(nt,),
        in_specs=[pl.BlockSpec((H, tm, LANES), lambda i: (0, nt - 1 - i, 0))] * 2 + [rev],
        out_specs=[rev, _const_spec((SUBLANES, LANES))],
        out_shape=[jax.ShapeDtypeStruct((T, LANES), BF16), jax.ShapeDtypeStruct((SUBLANES, LANES), F32)],
        scratch_shapes=[pltpu.VMEM((1, LANES), F32), pltpu.VMEM((SUBLANES, LANES), F32)],
        compiler_params=_cparams(("arbitrary",)),
    )(dc_key, dc_query, flb)


def _dx(dz, dfl, w_a, w_f, w_b, x, pre_gain, dh1):
    T = x.shape[0]
    tm = TM

    def body(*refs):
        dz_refs = refs[:6]
        dfl_ref, wa_ref, wf_ref, wb_ref, x_ref, g_ref, dh1_ref, gx_ref, acc_ref = refs[6:]

        @pl.when(pl.program_id(0) == 0)
        def _():
            acc_ref[...] = jnp.zeros_like(acc_ref)

        dxn = _dot(dfl_ref[...], wf_ref[...])
        for s in range(3):
            dxn = dxn + _dot(dz_refs[s][...], wa_ref[s * D:(s + 1) * D, :])
            dxn = dxn + _dot(dz_refs[3 + s][...], wb_ref[s * D:(s + 1) * D, :])
        xv = x_ref[...]
        rstd = _rstd(xv)
        xhat = xv * rstd
        gx_ref[...] = dh1_ref[...] + _rms_bwd(dxn * g_ref[...], xhat, rstd)
        acc_ref[0:1, :] += jnp.sum(dxn * xhat, axis=0, keepdims=True)

    return pl.pallas_call(
        body, name="dx", grid=(T // tm,),
        in_specs=[_row_spec(tm, D)] * 6 + [_row_spec(tm, LANES), _const_spec((3 * D, D)), _const_spec((LANES, D)),
                                           _const_spec((3 * D, D)), _row_spec(tm, D), _const_spec((1, D)),
                                           _row_spec(tm, D)],
        out_specs=[_row_spec(tm, D), _const_spec((SUBLANES, D))],
        out_shape=[jax.ShapeDtypeStruct((T, D), F32), jax.ShapeDtypeStruct((SUBLANES, D), F32)],
        compiler_params=_cparams(("arbitrary",), VMEM_BIG),
    )(*dz, dfl, w_a, w_f, w_b, x, pre_gain, dh1)


GRAD_ROWS = D_IN + SUBLANES


def _seg_row(s):
    return (s * (D // SUBLANES) + jnp.where(s >= 3, H // SUBLANES, 0)) * SUBLANES


def _dw_in_t(dz, dfl, xn, bt=512):
    T = xn.shape[0]
    nt = T // bt

    def main_body(*refs):
        dz_refs, xn_ref, o_ref = refs[:6], refs[6], refs[7]
        s = pl.program_id(0)

        @pl.when(pl.program_id(1) == 0)
        def _():
            o_ref[...] = jnp.zeros_like(o_ref)

        for k in range(6):
            @pl.when(s == k)
            def _(k=k):
                o_ref[...] += _dot_tn(dz_refs[k][...], xn_ref[...])

    def dz_spec(k):
        return pl.BlockSpec((bt, D), lambda s, t: (jnp.where(s == k, t, 0), 0))

    main = pl.pallas_call(
        main_body, name="dw_in_main", grid=(6, nt),
        in_specs=[dz_spec(k) for k in range(6)] + [pl.BlockSpec((bt, D), lambda s, t: (t, 0))],
        out_specs=pl.BlockSpec((pl.Element(D), pl.Element(D)), lambda s, t: (_seg_row(s), 0)),
        out_shape=jax.ShapeDtypeStruct((GRAD_ROWS, D), F32),
        compiler_params=_cparams(("arbitrary", "arbitrary")),
    )(*dz, xn)

    def f_body(dfl_ref, xn_ref, main_ref, o_ref, acc_s):
        p = pl.program_id(0)
        t = pl.program_id(1)

        @pl.when(t == 0)
        def _():
            acc_s[...] = jnp.zeros_like(acc_s)

        @pl.when(p == 0)
        def _():
            acc_s[...] += _dot_tn(dfl_ref[...], xn_ref[...])

        @pl.when(t == nt - 1)
        def _():
            o_ref[...] = acc_s[:SUBLANES, :]

    fl_block = FL0 // SUBLANES
    end_block = D_IN // SUBLANES
    return pl.pallas_call(
        f_body, name="dw_in_f", grid=(2, nt),
        in_specs=[pl.BlockSpec((bt, LANES), lambda p, t: (t, 0)), pl.BlockSpec((bt, D), lambda p, t: (t, 0)),
                  pl.BlockSpec(memory_space=pl.ANY)],
        out_specs=pl.BlockSpec((SUBLANES, D), lambda p, t: (fl_block + p * (end_block - fl_block), 0)),
        out_shape=jax.ShapeDtypeStruct((GRAD_ROWS, D), F32),
        scratch_shapes=[pltpu.VMEM((LANES, D), F32)],
        input_output_aliases={2: 0},
        compiler_params=_cparams(("arbitrary", "arbitrary")),
    )(dfl, xn, main)


def _matmul_tn(a, b, name, bm=512, bn=1024, bt=512):
    T, M = a.shape
    N = b.shape[1]
    bm, bn, bt = min(bm, M), min(bn, N), min(bt, T)

    def body(a_ref, b_ref, o_ref):
        @pl.when(pl.program_id(2) == 0)
        def _():
            o_ref[...] = jnp.zeros_like(o_ref)

        o_ref[...] += _dot_tn(a_ref[...], b_ref[...])

    return pl.pallas_call(
        body, name=name, grid=(M // bm, N // bn, T // bt),
        in_specs=[pl.BlockSpec((bt, bm), lambda i, j, t: (t, i)), pl.BlockSpec((bt, bn), lambda i, j, t: (t, j))],
        out_specs=pl.BlockSpec((bm, bn), lambda i, j, t: (i, j)),
        out_shape=jax.ShapeDtypeStruct((M, N), F32),
        compiler_params=_cparams(("parallel", "parallel", "arbitrary")),
    )(a, b)


HBM_SPEC = pl.BlockSpec(memory_space=pltpu.HBM)
VMEM_SPEC = pl.BlockSpec(memory_space=pltpu.VMEM)


def _position():
    return lax.axis_index("x"), lax.axis_index("y"), lax.axis_index("c")


def _other_chips(x, y):
    return [(1 - x, y), (x, 1 - y), (1 - x, 1 - y)]


def _gather_shards(shards, whole):
    na, nw = len(shards), len(whole)
    nall = na + nw

    def body(*refs):
        srcs, dsts = refs[:nall], refs[nall:2 * nall]
        ici_send, ici_recv, d2d_send, d2d_recv = refs[2 * nall:]
        x, y, c = _position()
        chip = 2 * x + y
        chips = _other_chips(x, y)

        def half(a, which):
            rows = srcs[a].shape[0] // 2
            return pl.ds(pl.multiple_of(which * rows, 16), rows)

        first = []
        for j, (px, py) in enumerate(chips):
            for a in range(nall):
                src = srcs[a].at[half(a, c), :] if a < na else srcs[a]
                dst = dsts[a].at[chip, half(a, c), :] if a < na else dsts[a].at[chip]
                first.append(pltpu.make_async_remote_copy(
                    src_ref=src, dst_ref=dst, send_sem=ici_send.at[j * nall + a], recv_sem=ici_recv.at[j * nall + a],
                    device_id=(px, py, c), device_id_type=MESH))
        for cp in first:
            cp.start()

        passed = []
        for j, (px, py) in enumerate(chips):
            theirs = 2 * px + py
            for a in range(nall):
                if a < na:
                    landed = dsts[a].at[theirs, half(a, c), :]
                    fwd = pltpu.make_async_remote_copy(
                        src_ref=landed, dst_ref=landed, send_sem=d2d_send.at[j * na + a],
                        recv_sem=d2d_recv.at[j * na + a], device_id=(x, y, 1 - c), device_id_type=MESH)
                else:
                    landed = dsts[a].at[theirs]
                pltpu.make_async_remote_copy(
                    src_ref=landed, dst_ref=landed, send_sem=ici_send.at[j * nall + a],
                    recv_sem=ici_recv.at[j * nall + a], device_id=(px, py, c), device_id_type=MESH).wait_recv()
                if a < na:
                    fwd.start()
                    passed.append(fwd)
        for j, (px, py) in enumerate(chips):
            theirs = 2 * px + py
            for a in range(na):
                other = dsts[a].at[theirs, half(a, 1 - c), :]
                pltpu.make_async_remote_copy(
                    src_ref=other, dst_ref=other, send_sem=d2d_send.at[j * na + a], recv_sem=d2d_recv.at[j * na + a],
                    device_id=(x, y, 1 - c), device_id_type=MESH).wait_recv()
        for cp in first + passed:
            cp.wait_send()

    arrs = list(shards) + list(whole)
    outs = pl.pallas_call(
        body, name="gather_shards",
        in_specs=[HBM_SPEC] * nall, out_specs=[HBM_SPEC] * nall,
        out_shape=[jax.ShapeDtypeStruct((N_CHIPS,) + s.shape, s.dtype) for s in arrs],
        scratch_shapes=[pltpu.SemaphoreType.DMA((3 * nall,)), pltpu.SemaphoreType.DMA((3 * nall,)),
                        pltpu.SemaphoreType.DMA((3 * na,)), pltpu.SemaphoreType.DMA((3 * na,))],
    )(*arrs)
    chip = 2 * lax.axis_index("x") + lax.axis_index("y")
    return [lax.dynamic_update_slice(o, a[None], (chip,) + (0,) * a.ndim) for o, a in zip(outs, arrs)]


W_ROWS = 1568
G_ROWS = 1552
SHARD_ROWS = D_IN // N_CHIPS
WINDOW_STEP = 1536


def _assemble_w_in(cont):
    cb = 256
    half = WINDOW_STEP

    def body(c_ref, wa_ref, wf_ref, wb_ref):
        x0 = c_ref[0].astype(F32)
        x1, x2, x3 = (pltpu.roll(c_ref[j].astype(F32), 2 * j, 0) for j in (1, 2, 3))
        wa = jnp.concatenate([x0[:half], x0[half:half + 16] + x1[:16], x1[16:half]], axis=0)
        wa_ref[...] = wa.astype(BF16)

        fl = x1[half:half + 16] + x2[:16]
        row = lax.broadcasted_iota(jnp.int32, fl.shape, 0)
        wf_ref[:16, :] = jnp.where(row < H, fl, 0.0).astype(BF16)
        wf_ref[16:, :] = jnp.zeros((LANES - 16, cb), BF16)

        mid = x2[half:half + SUBLANES] + x3[:SUBLANES]
        wb = jnp.concatenate([x2[SUBLANES:half], mid, x3[SUBLANES:half + SUBLANES]], axis=0)
        wb_ref[...] = wb.astype(BF16)

    return pl.pallas_call(
        body, name="assemble_w_in", grid=(D // cb,),
        in_specs=[pl.BlockSpec((N_CHIPS, W_ROWS, cb), lambda i: (0, 0, i))],
        out_specs=[pl.BlockSpec((3 * D, cb), lambda i: (0, i)), pl.BlockSpec((LANES, cb), lambda i: (0, i)),
                   pl.BlockSpec((3 * D, cb), lambda i: (0, i))],
        out_shape=[jax.ShapeDtypeStruct((3 * D, D), BF16), jax.ShapeDtypeStruct((LANES, D), BF16),
                   jax.ShapeDtypeStruct((3 * D, D), BF16)],
        compiler_params=_cparams(("parallel",)),
    )(cont)


def _pair_exchange(grad_t, parts):
    na = len(parts)
    n = N_CHIPS + na
    half_g = G_ROWS // 2

    def body(*refs):
        g_ref, srcs, got = refs[0], refs[1:1 + na], refs[1 + na:2 + 2 * na]
        send_sems, recv_sems = refs[2 + 2 * na:]
        x, y, c = _position()
        pieces = []
        for j in range(N_CHIPS):
            rows = pl.ds(pl.multiple_of(j * WINDOW_STEP + (1 - c) * half_g, SUBLANES), half_g)
            pieces.append((g_ref.at[rows, :], got[0].at[j]))
        for a in range(na):
            half = srcs[a].shape[1] // 2
            rows = pl.ds(pl.multiple_of((1 - c) * half, SUBLANES), half)
            pieces.append((srcs[a].at[:, rows, :], got[1 + a]))
        copies = [pltpu.make_async_remote_copy(
            src_ref=give, dst_ref=dst, send_sem=send_sems.at[k], recv_sem=recv_sems.at[k],
            device_id=(x, y, 1 - c), device_id_type=MESH) for k, (give, dst) in enumerate(pieces)]
        for cp in copies:
            cp.start()
        for cp in copies:
            cp.wait()

    halves = [jax.ShapeDtypeStruct((N_CHIPS, half_g, D), F32)]
    halves += [jax.ShapeDtypeStruct((s.shape[0], s.shape[1] // 2, s.shape[2]), s.dtype) for s in parts]
    return pl.pallas_call(
        body, name="pair_exchange",
        in_specs=[HBM_SPEC] * (1 + na), out_specs=[HBM_SPEC] * (1 + na),
        out_shape=halves,
        scratch_shapes=[pltpu.SemaphoreType.DMA((n,)), pltpu.SemaphoreType.DMA((n,))],
    )(grad_t, *parts)


def _pair_sum(part, got, c, name):
    _, half, C = got.shape
    cb = min(C, 256)

    def body(c_ref, a_ref, b_ref, o_ref):
        o_ref[...] = (a_ref[...] + b_ref[...]).astype(BF16)

    spec = pl.BlockSpec((1, half, cb), lambda j, i, c_ref: (j, 0, i))
    grid_spec = pltpu.PrefetchScalarGridSpec(
        num_scalar_prefetch=1, grid=(N_CHIPS, C // cb),
        in_specs=[pl.BlockSpec((1, half, cb), lambda j, i, c_ref: (j, c_ref[0], i)), spec], out_specs=spec)
    return pl.pallas_call(
        body, name=name, grid_spec=grid_spec,
        out_shape=jax.ShapeDtypeStruct((N_CHIPS, half, C), BF16),
        compiler_params=_cparams(("parallel", "parallel")),
    )(c.reshape(1), part, got)


def _pair_sum_windows(grad_t, got, c):
    _, half, C = got.shape
    cb = 256

    def body(c_ref, a_ref, b_ref, o_ref):
        o_ref[0] = (a_ref[...] + b_ref[0]).astype(BF16)

    def mine(j, i, c_ref):
        return ((j * (WINDOW_STEP // SUBLANES) + c_ref[0] * (half // SUBLANES)) * SUBLANES, i * cb)

    spec = pl.BlockSpec((1, half, cb), lambda j, i, c_ref: (j, 0, i))
    grid_spec = pltpu.PrefetchScalarGridSpec(
        num_scalar_prefetch=1, grid=(N_CHIPS, C // cb),
        in_specs=[pl.BlockSpec((pl.Element(half), pl.Element(cb)), mine), spec], out_specs=spec)
    return pl.pallas_call(
        body, name="pair_sum_w_in", grid_spec=grid_spec,
        out_shape=jax.ShapeDtypeStruct((N_CHIPS, half, C), BF16),
        compiler_params=_cparams(("parallel", "parallel")),
    )(c.reshape(1), grad_t, got)


def _chip_exchange(sums):
    na = len(sums)

    def body(*refs):
        srcs, dsts = refs[:na], refs[na:2 * na]
        send_sems, recv_sems = refs[2 * na:]
        x, y, c = _position()
        chip = 2 * x + y
        copies = []
        for j, (px, py) in enumerate(_other_chips(x, y)):
            for a in range(na):
                copies.append(pltpu.make_async_remote_copy(
                    src_ref=srcs[a].at[2 * px + py], dst_ref=dsts[a].at[chip], send_sem=send_sems.at[j * na + a],
                    recv_sem=recv_sems.at[j * na + a], device_id=(px, py, c), device_id_type=MESH))
        for cp in copies:
            cp.start()
        for cp in copies:
            cp.wait()

    return pl.pallas_call(
        body, name="chip_exchange",
        in_specs=[HBM_SPEC] * na, out_specs=[HBM_SPEC] * na,
        out_shape=[jax.ShapeDtypeStruct(s.shape, s.dtype) for s in sums],
        scratch_shapes=[pltpu.SemaphoreType.DMA((3 * na,)), pltpu.SemaphoreType.DMA((3 * na,))],
    )(*sums)


def _chip_sum(own, got, chip, name):
    _, half, C = got.shape
    cb = min(C, 256)

    def body(chip_ref, own_ref, g_ref, o_ref):
        for me in range(N_CHIPS):
            @pl.when(chip_ref[0] == me)
            def _(me=me):
                terms = [own_ref[0] if k == me else g_ref[k] for k in range(N_CHIPS)]
                acc = terms[0].astype(F32) + terms[1].astype(F32)
                acc = acc + terms[2].astype(F32)
                o_ref[...] = acc + terms[3].astype(F32)

    grid_spec = pltpu.PrefetchScalarGridSpec(
        num_scalar_prefetch=1, grid=(C // cb,),
        in_specs=[pl.BlockSpec((1, half, cb), lambda i, chip_ref: (chip_ref[0], 0, i)),
                  pl.BlockSpec((N_CHIPS, half, cb), lambda i, chip_ref: (0, 0, i))],
        out_specs=pl.BlockSpec((half, cb), lambda i, chip_ref: (0, i)))
    return pl.pallas_call(
        body, name=name, grid_spec=grid_spec,
        out_shape=jax.ShapeDtypeStruct((half, C), F32),
        compiler_params=_cparams(("parallel",)),
    )(chip.reshape(1), own, got)


def _pair_swap(halves):
    na = len(halves)

    def body(*refs):
        srcs, dsts = refs[:na], refs[na:2 * na]
        send_sems, recv_sems = refs[2 * na:]
        x, y, c = _position()
        copies = [pltpu.make_async_remote_copy(
            src_ref=srcs[a], dst_ref=dsts[a], send_sem=send_sems.at[a], recv_sem=recv_sems.at[a],
            device_id=(x, y, 1 - c), device_id_type=MESH) for a in range(na)]
        for cp in copies:
            cp.start()
        for cp in copies:
            cp.wait()

    return pl.pallas_call(
        body, name="pair_swap",
        in_specs=[HBM_SPEC] * na, out_specs=[HBM_SPEC] * na,
        out_shape=[jax.ShapeDtypeStruct(s.shape, s.dtype) for s in halves],
        scratch_shapes=[pltpu.SemaphoreType.DMA((na,)), pltpu.SemaphoreType.DMA((na,))],
    )(*halves)


def _allreduce_small(g):
    rows = g.shape[0]
    per = rows // N_DEV

    def body(g_ref, out_ref, got_ref, s1, r1, s2, r2):
        x, y, c = _position()
        me = 4 * x + 2 * y + c
        mine = pl.ds(pl.multiple_of(me * per, SUBLANES), per)
        peers = []
        for j in range(1, N_DEV):
            px = 1 - x if j & 4 else x
            py = 1 - y if j & 2 else y
            pc = 1 - c if j & 1 else c
            peers.append((px, py, pc))

        first = []
        for j, (px, py, pc) in enumerate(peers):
            theirs = pl.ds(pl.multiple_of((4 * px + 2 * py + pc) * per, SUBLANES), per)
            first.append(pltpu.make_async_remote_copy(
                src_ref=g_ref.at[theirs, :], dst_ref=got_ref.at[me], send_sem=s1.at[j], recv_sem=r1.at[j],
                device_id=(px, py, pc), device_id_type=MESH))
        for cp in first:
            cp.start()
        got_ref[me] = g_ref[mine, :]
        for cp in first:
            cp.wait()
        total = got_ref[0]
        for d in range(1, N_DEV):
            total = total + got_ref[d]
        out_ref[mine, :] = total

        second = []
        for j, peer in enumerate(peers):
            second.append(pltpu.make_async_remote_copy(
                src_ref=out_ref.at[mine, :], dst_ref=out_ref.at[mine, :], send_sem=s2.at[j], recv_sem=r2.at[j],
                device_id=peer, device_id_type=MESH))
        for cp in second:
            cp.start()
        for cp in second:
            cp.wait()

    sems = pltpu.SemaphoreType.DMA((N_DEV - 1,))
    return pl.pallas_call(
        body, name="allreduce_small",
        in_specs=[VMEM_SPEC], out_specs=VMEM_SPEC,
        out_shape=jax.ShapeDtypeStruct(g.shape, F32),
        scratch_shapes=[pltpu.VMEM((N_DEV, per, LANES), F32), sems, sems, sems, sems],
    )(g)


def _adamw_math(g, w, m, v):
    m2 = ADAM_B1 * m + (1.0 - ADAM_B1) * g
    v2 = ADAM_B2 * v + (1.0 - ADAM_B2) * (g * g)
    m_hat = m2 / (1.0 - ADAM_B1 ** ADAM_STEP)
    v_hat = v2 / (1.0 - ADAM_B2 ** ADAM_STEP)
    delta = (-ADAM_LR) * (m_hat / (jnp.sqrt(v_hat) + ADAM_EPS) + ADAM_WD * w)
    return delta, m2, v2


def _adamw_big(g, w, m, v, name):
    R, C = g.shape
    cb = min(C, LANES)

    def body(g_ref, w_ref, m_ref, v_ref, d_ref, m2_ref, v2_ref):
        d_ref[...], m2_ref[...], v2_ref[...] = _adamw_math(g_ref[...], w_ref[...], m_ref[...], v_ref[...])

    spec = pl.BlockSpec((R, cb), lambda i: (0, i))
    out = jax.ShapeDtypeStruct((R, C), F32)
    return pl.pallas_call(
        body, name=name, grid=(C // cb,),
        in_specs=[spec] * 4, out_specs=[spec] * 3, out_shape=[out] * 3,
        compiler_params=_cparams(("parallel",)),
    )(g, w, m, v)


def _adamw_small(gs, ws, ms, vs):
    n = len(gs)

    def body(*refs):
        for a in range(n):
            g_ref, w_ref, m_ref, v_ref = (refs[k * n + a] for k in range(4))
            d_ref, m2_ref, v2_ref = (refs[(4 + k) * n + a] for k in range(3))
            d_ref[...], m2_ref[...], v2_ref[...] = _adamw_math(g_ref[...], w_ref[...], m_ref[...], v_ref[...])

    outs = [jax.ShapeDtypeStruct(w.shape, F32) for w in ws]
    return pl.pallas_call(
        body, name="adamw_small",
        in_specs=[VMEM_SPEC] * (4 * n), out_specs=[VMEM_SPEC] * (3 * n), out_shape=outs * 3,
    )(*gs, *ws, *ms, *vs)


def _local_step(x, p, tgt, w_a, w_f, w_b, w_out_b, w_ple_b, w_gate_b, conv_w, b_f, pre_gain, post_gain, conv_b,
                w_rgate, b_rgate, w_igate, b_igate, lam, gain_a, gain_l, ple_gain, b_gate):
    b_f_pad = jnp.pad(b_f, ((0, 0), (0, LANES - H)))
    w_r = w_rgate.astype(BF16)
    w_i = w_igate.astype(BF16)

    xn, q_aug, k_aug, v_aug, g_attn, x_lru, g_lru, flb = _in_proj(x, pre_gain, w_a, w_f, w_b, b_f_pad)
    o, qx = _attn_fwd(q_aug, k_aug, v_aug)
    ycat, xc, h = _branches_fwd(o, g_attn, x_lru, g_lru, gain_a, gain_l, conv_w, conv_b, w_r, b_rgate, w_i, b_igate,
                                lam)
    dh1, dycat, dmix, h1b, dgp, pb, dpe, acc_t = _tail(ycat, x, p, tgt, w_out_b, post_gain, w_ple_b, ple_gain,
                                                       w_gate_b, b_gate)
    do_aug, dg_attn, dg_lru, dh, acc_b = _branches_bwd(dycat, o, g_attn, h, g_lru, gain_a, gain_l)
    dx_lru, gw_r, gw_i, acc_l = _lru_bwd(dh, h, xc, x_lru, conv_w, w_r, b_rgate, w_i, b_igate, lam)
    dq, dk, dv, dc_key, dc_query = _attn_bwd(q_aug, qx, k_aug, v_aug, do_aug)
    dfl, acc_f = _fgate_bwd(dc_key, dc_query, flb)
    dz = (dq, dk, dv, dg_attn, dx_lru, dg_lru)
    grad_x, acc_x = _dx(dz, dfl, w_a, w_f, w_b, x, pre_gain, dh1)

    grads = dict(
        w_in_t=_dw_in_t(dz, dfl, xn),
        w_out=_matmul_tn(ycat, dmix, "dw_out"),
        w_ple=_matmul_tn(pb, dpe, "dw_ple"),
        w_ple_gate=_matmul_tn(h1b, dgp, "dw_ple_gate"),
        w_rgate=gw_r,
        w_igate=gw_i,
        b_f=acc_f[0:1, :H],
        pre_gain=acc_x[0:1],
        post_gain=acc_t[0:1],
        conv_w=acc_l[0:4],
        conv_b=acc_l[4:5],
        b_rgate=acc_l[5:6],
        b_igate=acc_l[6:7],
        lru_lambda=acc_l[7:8],
        attn_out_gain=acc_b[0:1],
        lru_out_gain=acc_b[1:2],
        ple_gain=acc_t[1:2],
        b_ple_gate=acc_t[2:3],
    )
    loss = jnp.sum(acc_t[3])
    return loss, grad_x, grads


SMALL_ROWS = ["b_f", "pre_gain", "post_gain", "conv_w", "conv_b", "b_rgate", "b_igate", "lru_lambda",
              "attn_out_gain", "lru_out_gain", "ple_gain", "b_ple_gate"]
WEIGHTS = ["w_in", "b_f", "pre_gain", "post_gain", "conv_w", "conv_b", "w_rgate", "b_rgate", "w_igate", "b_igate",
           "lru_lambda", "attn_out_gain", "lru_out_gain", "w_out", "w_ple", "ple_gain", "w_ple_gate", "b_ple_gate"]
SHARDED = ["w_in", "w_out", "w_ple", "w_ple_gate"]


def _by_chip_cols(g):
    r, cols = g.shape
    return g.reshape(r, N_CHIPS, cols // N_CHIPS).transpose(1, 0, 2)


def _from_chip_cols(s):
    n, r, cols = s.shape
    return s.transpose(1, 0, 2).reshape(r, n * cols)


def kernel(x, p, w_in, b_f, pre_gain, post_gain, conv_w, conv_b, w_rgate, b_rgate, w_igate, b_igate, lru_lambda, attn_out_gain, lru_out_gain, w_out, w_ple, ple_gain, w_ple_gate, b_ple_gate, loss_target, m_w_in, m_b_f, m_pre_gain, m_post_gain, m_conv_w, m_conv_b, m_w_rgate, m_b_rgate, m_w_igate, m_b_igate, m_lru_lambda, m_attn_out_gain, m_lru_out_gain, m_w_out, m_w_ple, m_ple_gain, m_w_ple_gate, m_b_ple_gate, v_w_in, v_b_f, v_pre_gain, v_post_gain, v_conv_w, v_conv_b, v_w_rgate, v_b_rgate, v_w_igate, v_b_igate, v_lru_lambda, v_attn_out_gain, v_lru_out_gain, v_w_out, v_w_ple, v_ple_gain, v_w_ple_gate, v_b_ple_gate):
    w = dict(w_in=w_in, b_f=b_f, pre_gain=pre_gain, post_gain=post_gain, conv_w=conv_w, conv_b=conv_b,
             w_rgate=w_rgate, b_rgate=b_rgate, w_igate=w_igate, b_igate=b_igate, lru_lambda=lru_lambda,
             attn_out_gain=attn_out_gain, lru_out_gain=lru_out_gain, w_out=w_out, w_ple=w_ple, ple_gain=ple_gain,
             w_ple_gate=w_ple_gate, b_ple_gate=b_ple_gate)
    m = dict(w_in=m_w_in, b_f=m_b_f, pre_gain=m_pre_gain, post_gain=m_post_gain, conv_w=m_conv_w, conv_b=m_conv_b,
             w_rgate=m_w_rgate, b_rgate=m_b_rgate, w_igate=m_w_igate, b_igate=m_b_igate, lru_lambda=m_lru_lambda,
             attn_out_gain=m_attn_out_gain, lru_out_gain=m_lru_out_gain, w_out=m_w_out, w_ple=m_w_ple,
             ple_gain=m_ple_gain, w_ple_gate=m_w_ple_gate, b_ple_gate=m_b_ple_gate)
    v = dict(w_in=v_w_in, b_f=v_b_f, pre_gain=v_pre_gain, post_gain=v_post_gain, conv_w=v_conv_w, conv_b=v_conv_b,
             w_rgate=v_w_rgate, b_rgate=v_b_rgate, w_igate=v_w_igate, b_igate=v_b_igate, lru_lambda=v_lru_lambda,
             attn_out_gain=v_attn_out_gain, lru_out_gain=v_lru_out_gain, w_out=v_w_out, w_ple=v_w_ple,
             ple_gain=v_ple_gain, w_ple_gate=v_w_ple_gate, b_ple_gate=v_b_ple_gate)
    xi, yi, ci = _position()
    chip = 2 * xi + yi

    w_in_t, m_in_t, v_in_t = (jnp.swapaxes(t[0], 0, 1) for t in (w_in, m_w_in, v_w_in))
    window = jnp.pad(w_in_t.astype(BF16), ((0, W_ROWS - SHARD_ROWS), (0, 0)))

    st_in, st_out, st_ple, st_gate, st_conv = _gather_shards(
        [window, w_out[0].astype(BF16), w_ple[0].astype(BF16), w_ple_gate[0].astype(BF16)], [conv_w[0]])
    w_a, w_f, w_b = _assemble_w_in(st_in)
    w_out_b = st_out.reshape(DMIX, D)
    w_ple_b = _from_chip_cols(st_ple)
    w_gate_b = st_gate.reshape(D, D)
    conv_full = _from_chip_cols(st_conv)

    loss, grad_x, g = _local_step(
        x[0], p[0, 0], loss_target[0], w_a, w_f, w_b, w_out_b, w_ple_b, w_gate_b, conv_full, b_f, pre_gain, post_gain,
        conv_b, w_rgate[0], b_rgate, w_igate[0], b_igate, lru_lambda, attn_out_gain, lru_out_gain, ple_gain,
        b_ple_gate)
    loss = lax.psum(loss, ("x", "y", "c"))

    parts = [g["w_out"].reshape(N_CHIPS, DMIX // N_CHIPS, D), _by_chip_cols(g["w_ple"]),
             g["w_ple_gate"].reshape(N_CHIPS, D // N_CHIPS, D)]
    got = _pair_exchange(g["w_in_t"], parts)
    sums = [_pair_sum_windows(g["w_in_t"], got[0], ci)]
    sums += [_pair_sum(parts[a], got[1 + a], ci, "pair_sum_%d" % a) for a in range(3)]
    recv = _chip_exchange(sums)
    halves = [_chip_sum(sums[a], recv[a], chip, "chip_sum_%d" % a) for a in range(4)]
    theirs = _pair_swap(halves)
    full = [jnp.concatenate([jnp.where(ci == 0, a, b), jnp.where(ci == 0, b, a)], axis=0)
            for a, b in zip(halves, theirs)]
    red = dict(zip(SHARDED, full))
    red["w_in"] = lax.dynamic_slice_in_dim(red["w_in"], 2 * chip, SHARD_ROWS, axis=0)

    rows = [jnp.pad(g["b_f"], ((0, 0), (0, D - H)))] + [g[n] for n in SMALL_ROWS[1:]]
    rows.append(jnp.zeros((16 - sum(r.shape[0] for r in rows), D), F32))
    packed = jnp.concatenate([g["w_rgate"].reshape(NB * LANES, LANES), g["w_igate"].reshape(NB * LANES, LANES),
                              jnp.concatenate(rows, axis=0).reshape(LANES, LANES)], axis=0)
    summed = _allreduce_small(packed)
    red["w_rgate"] = summed[:D].reshape(1, NB, LANES, LANES)
    red["w_igate"] = summed[D:2 * D].reshape(1, NB, LANES, LANES)
    vec = summed[2 * D:].reshape(16, D)
    r0 = 0
    for n in SMALL_ROWS:
        nr = 4 if n == "conv_w" else 1
        red[n] = vec[r0:r0 + nr]
        r0 += nr
    red["b_f"] = red["b_f"][:, :H]
    red["conv_w"] = lax.dynamic_slice_in_dim(red["conv_w"], chip * (D // N_CHIPS), D // N_CHIPS, axis=1)[None]

    delta, new_m, new_v = {}, {}, {}
    outs_in = _adamw_big(red["w_in"], w_in_t, m_in_t, v_in_t, "adamw_w_in")
    delta["w_in"], new_m["w_in"], new_v["w_in"] = (jnp.swapaxes(t, 0, 1)[None] for t in outs_in)
    red["w_in"] = jnp.swapaxes(red["w_in"], 0, 1)[None]
    for n in SHARDED[1:]:
        delta[n], new_m[n], new_v[n] = (t[None] for t in _adamw_big(red[n], w[n][0], m[n][0], v[n][0], "adamw_" + n))
        red[n] = red[n][None]
    small = [n for n in WEIGHTS if n not in SHARDED]
    outs = _adamw_small([red[n] for n in small], [w[n] for n in small], [m[n] for n in small],
                        [v[n] for n in small])
    ns = len(small)
    for a, n in enumerate(small):
        delta[n], new_m[n], new_v[n] = outs[a], outs[ns + a], outs[2 * ns + a]

    return (loss, grad_x[None], *[red[n] for n in WEIGHTS], *[delta[n] for n in WEIGHTS],
            *[new_m[n] for n in WEIGHTS], *[new_v[n] for n in WEIGHTS])
```

```python
import functools

import jax
import jax.numpy as jnp
import numpy as np
from jax import lax
from jax.experimental import pallas as pl
from jax.experimental.pallas import tpu as pltpu

F32 = jnp.float32
BF16 = jnp.bfloat16

D = 1024
H = 8
DH = 128
NB = 8
DPLE = 256
DMIX = 2 * D
D_IN = 4 * D + H + 2 * D
FL0 = 3 * D
RMS_EPS = 1e-6
LRU_C = 8.0
NEG = -1e30
LANES = 128
SUBLANES = 8

ADAM_LR = 0.001
ADAM_B1 = 0.9
ADAM_B2 = 0.999
ADAM_EPS = 1e-08
ADAM_WD = 0.01
ADAM_STEP = 10

TM = 256
TA = 512
ATTN_HEADS = 2
VMEM_BIG = 56 * 1024 * 1024
VMEM_MID = 40 * 1024 * 1024

MESH = pl.DeviceIdType.MESH
N_CHIPS = 4
N_DEV = 8


def _cparams(sem, vmem=VMEM_MID):
    return pltpu.CompilerParams(dimension_semantics=sem, vmem_limit_bytes=vmem)


def _sigmoid(x):
    return 1.0 / (1.0 + jnp.exp(-x))


def _rstd(x):
    return lax.rsqrt(jnp.mean(x * x, axis=-1, keepdims=True) + RMS_EPS)


def _rms_bwd(t, xhat, rstd):
    return rstd * (t - xhat * jnp.mean(t * xhat, axis=-1, keepdims=True))


def _dot(a, b):
    return jnp.dot(a, b, preferred_element_type=F32)


def _dot_nt(a, b):
    return lax.dot_general(a, b, (((1,), (1,)), ((), ())), preferred_element_type=F32)


def _dot_tn(a, b):
    return lax.dot_general(a, b, (((0,), (0,)), ((), ())), preferred_element_type=F32)


def _dot_exact(a, b):
    return jnp.dot(a, b, preferred_element_type=F32, precision=lax.Precision.HIGHEST)


def _neg_expm1(x):
    series = x * (1.0 + x * 0.5 * (1.0 + x * (1.0 / 3.0) * (1.0 + x * 0.25 * (1.0 + x * 0.2 * (1.0 + x * (1.0 / 6.0))))))
    return -jnp.where(x > -0.25, series, jnp.exp(x) - 1.0)


def _shift_down(x, j, halo):
    rolled = pltpu.roll(x, j, 0)
    row = lax.broadcasted_iota(jnp.int32, halo.shape, 0)
    top = jnp.where(row < j, pltpu.roll(halo, j, 0), rolled[:SUBLANES])
    return jnp.concatenate([top, rolled[SUBLANES:]], axis=0)


def _shift_up(x, j, nxt):
    tm = x.shape[0]
    rolled = pltpu.roll(x, tm - j, 0)
    row = lax.broadcasted_iota(jnp.int32, nxt.shape, 0)
    bot = jnp.where(row >= SUBLANES - j, pltpu.roll(nxt, SUBLANES - j, 0), rolled[tm - SUBLANES:])
    return jnp.concatenate([rolled[:tm - SUBLANES], bot], axis=0)


def _scan_fwd(a, u):
    tm = a.shape[0]
    row = lax.broadcasted_iota(jnp.int32, a.shape, 0)
    d = 1
    while d < tm:
        keep = row >= d
        a_s = jnp.where(keep, pltpu.roll(a, d, 0), 1.0)
        u_s = jnp.where(keep, pltpu.roll(u, d, 0), 0.0)
        u = u + a * u_s
        a = a * a_s
        d *= 2
    return a, u


def _scan_bwd(b, u):
    tm = b.shape[0]
    row = lax.broadcasted_iota(jnp.int32, b.shape, 0)
    d = 1
    while d < tm:
        keep = row < tm - d
        b_s = jnp.where(keep, pltpu.roll(b, tm - d, 0), 1.0)
        u_s = jnp.where(keep, pltpu.roll(u, tm - d, 0), 0.0)
        u = u + b * u_s
        b = b * b_s
        d *= 2
    return u


def _gate_pre(xc, w_ref):
    outs = []
    for n in range(NB):
        outs.append(_dot(xc[:, n * LANES:(n + 1) * LANES].astype(BF16), w_ref[n]))
    return jnp.concatenate(outs, axis=1)


def _gate_pre_t(d, w_ref):
    outs = []
    for n in range(NB):
        outs.append(_dot_nt(d[:, n * LANES:(n + 1) * LANES].astype(BF16), w_ref[n]))
    return jnp.concatenate(outs, axis=1)


def _softplus_neg(lam):
    return jnp.maximum(-lam, 0.0) + jnp.log(1.0 + jnp.exp(-jnp.abs(lam)))


def _row_spec(tm, width):
    return pl.BlockSpec((tm, width), lambda i: (i, 0))


def _const_spec(shape):
    nd = len(shape)
    return pl.BlockSpec(shape, lambda *_: (0,) * nd)


AUG = 2 * DH
LOG2E = 1.4426950408889634
LN2 = 0.6931471805599453
Q_SCALE = DH ** -0.5 * LOG2E


def _split3(x):
    hi = x.astype(BF16)
    r1 = x - hi.astype(F32)
    mid = r1.astype(BF16)
    lo = (r1 - mid.astype(F32)).astype(BF16)
    return hi, mid, lo


def _extras(col, ones_from):
    t = col.shape[0]
    hi, mid, lo = _split3(jnp.broadcast_to(col, (t, LANES)))
    lane = lax.broadcasted_iota(jnp.int32, (t, LANES), 1)
    rest = jnp.zeros((t, LANES), BF16)
    if ones_from is not None:
        rest = jnp.where((lane >= ones_from) & (lane < ones_from + 3), 1.0, 0.0).astype(BF16)
    return jnp.where(lane == 0, hi, jnp.where(lane == 1, mid, jnp.where(lane == 2, lo, rest)))


def _selectors():
    sel_q = np.zeros((3 * LANES, H * LANES), np.float32)
    sel_k = np.zeros((3 * LANES, H * LANES), np.float32)
    for hd in range(H):
        for piece in range(3):
            sel_q[piece * LANES + hd, hd * LANES + piece] = 1.0
            sel_k[piece * LANES + hd, hd * LANES + 3 + piece] = -1.0
    return jnp.asarray(sel_q, BF16), jnp.asarray(sel_k, BF16)


def _in_proj(x, pre_gain, w_a, w_f, w_b, b_f_pad):
    T = x.shape[0]
    tm = TM
    sel_q, sel_k = _selectors()

    def body(x_ref, g_ref, wa_ref, wf_ref, wb_ref, bf_ref, sq_ref, sk_ref,
             xn_ref, qa_ref, ka_ref, va_ref, ga_ref, xl_ref, gl_ref, flb_ref, c_s, carry):
        @pl.when(pl.program_id(0) == 0)
        def _():
            carry[...] = jnp.zeros_like(carry)

        xv = x_ref[...]
        xn = (xv * _rstd(xv) * g_ref[...]).astype(BF16)
        xn_ref[...] = xn
        for s, o_ref in enumerate((ga_ref, xl_ref, gl_ref)):
            o_ref[...] = _dot_nt(xn, wb_ref[s * D:(s + 1) * D, :]).astype(o_ref.dtype)
        flb = _dot_nt(xn, wf_ref[...]) + bf_ref[...]
        flb_ref[...] = flb
        lane = lax.broadcasted_iota(jnp.int32, flb.shape, 1)
        ls = jnp.where(lane < H, jnp.minimum(flb, 0.0) - jnp.log(1.0 + jnp.exp(-jnp.abs(flb))), 0.0)
        r = lax.broadcasted_iota(jnp.int32, (tm, tm), 0)
        c = lax.broadcasted_iota(jnp.int32, (tm, tm), 1)
        cs = _dot_exact((c <= r).astype(F32), ls) + carry[...]
        c_s[...] = cs
        carry[...] = c_s[tm - 1:tm, :]

        pieces = jnp.concatenate(_split3(cs * LOG2E), axis=1)
        ones_q = jnp.where((lane >= 3) & (lane < 6), 1.0, 0.0)
        ones_k = jnp.where(lane < 3, 1.0, 0.0)
        zq = _dot_nt(xn, wa_ref[0:D, :]) * Q_SCALE
        zk = _dot_nt(xn, wa_ref[D:2 * D, :])
        zv = _dot_nt(xn, wa_ref[2 * D:3 * D, :])
        ex_q = _dot(pieces, sq_ref[...])
        ex_k = _dot(pieces, sk_ref[...])
        for hd in range(H):
            head = slice(hd * DH, (hd + 1) * DH)
            lo, hi = hd * AUG, hd * AUG + DH
            qa_ref[:, lo:hi] = zq[:, head].astype(BF16)
            qa_ref[:, hi:hi + DH] = (ex_q[:, head] + ones_q).astype(BF16)
            ka_ref[:, lo:hi] = zk[:, head].astype(BF16)
            ka_ref[:, hi:hi + DH] = (ex_k[:, head] + ones_k).astype(BF16)
            va_ref[:, lo:hi] = zv[:, head].astype(BF16)
            va_ref[:, hi:hi + DH] = ones_k.astype(BF16)

    bf = jax.ShapeDtypeStruct((T, D), BF16)
    aug = jax.ShapeDtypeStruct((T, H * AUG), BF16)
    f32 = jax.ShapeDtypeStruct((T, D), F32)
    sel_spec = _const_spec((3 * LANES, H * LANES))
    return pl.pallas_call(
        body, name="in_proj", grid=(T // tm,),
        in_specs=[_row_spec(tm, D), _const_spec((1, D)), _const_spec((3 * D, D)), _const_spec((LANES, D)),
                  _const_spec((3 * D, D)), _const_spec((1, LANES)), sel_spec, sel_spec],
        out_specs=[_row_spec(tm, D)] + [_row_spec(tm, H * AUG)] * 3 + [_row_spec(tm, D)] * 3 + [_row_spec(tm, LANES)],
        out_shape=[bf, aug, aug, aug, f32, f32, f32, jax.ShapeDtypeStruct((T, LANES), F32)],
        scratch_shapes=[pltpu.VMEM((tm, LANES), F32), pltpu.VMEM((1, LANES), F32)],
        compiler_params=_cparams(("arbitrary",), VMEM_BIG),
    )(x, pre_gain, w_a, w_f, w_b, b_f_pad, sel_q, sel_k)


def _causal_pairs(n, q_major):
    if q_major:
        pairs = [(qi, ki) for qi in range(n) for ki in range(qi + 1)]
    else:
        pairs = [(ki, qi) for ki in range(n) for qi in range(ki, n)]
    return (jnp.asarray([a for a, _ in pairs], jnp.int32), jnp.asarray([b for _, b in pairs], jnp.int32))


def _attn_fwd(q_aug, k_aug, v_aug):
    T = q_aug.shape[0]
    t = TA
    n = T // t
    hp = ATTN_HEADS
    heads = range(hp)
    qi_tab, ki_tab = _causal_pairs(n, q_major=True)

    def body(qi_ref, ki_ref, q_ref, k_ref, v_ref, o_ref, qx_ref, m_s, acc_s):
        j = pl.program_id(1)
        qi = qi_ref[j]
        ki = ki_ref[j]

        @pl.when(ki == 0)
        def _():
            m_s[...] = jnp.full(m_s.shape, NEG, F32)
            acc_s[...] = jnp.zeros_like(acc_s)

        def step(on_diagonal):
            cols = [slice(a * AUG, (a + 1) * AUG) for a in heads]
            s = [_dot_nt(q_ref[:, cols[a]], k_ref[:, cols[a]]) for a in heads]
            if on_diagonal:
                row = lax.broadcasted_iota(jnp.int32, (t, t), 0)
                col = lax.broadcasted_iota(jnp.int32, (t, t), 1)
                s = [jnp.where(col <= row, s[a], NEG) for a in heads]
            m_prev = [m_s[a] for a in heads]
            m_new = [jnp.maximum(m_prev[a], jnp.max(s[a], axis=1, keepdims=True)) for a in heads]
            pr = [jnp.exp2(s[a] - m_new[a]).astype(BF16) for a in heads]
            for a in heads:
                acc_s[a] = jnp.exp2(m_prev[a] - m_new[a]) * acc_s[a] + _dot(pr[a], v_ref[:, cols[a]])
                m_s[a] = m_new[a]

        @pl.when(ki < qi)
        def _():
            step(False)

        @pl.when(ki == qi)
        def _():
            step(True)
            for a in heads:
                acc = acc_s[a]
                l = acc[:, DH:DH + 1]
                o_ref[:, a * DH:(a + 1) * DH] = acc[:, :DH] / l
                ex = q_ref[:, a * AUG + DH:(a + 1) * AUG].astype(F32)
                c2 = ex[:, 0:1] + ex[:, 1:2] + ex[:, 2:3]
                qx_ref[:, a * DH:(a + 1) * DH] = _extras(c2 - (m_s[a] + jnp.log(l) * LOG2E), 3)

    q_spec = pl.BlockSpec((t, hp * AUG), lambda h, j, qi_ref, ki_ref: (qi_ref[j], h))
    kv_spec = pl.BlockSpec((t, hp * AUG), lambda h, j, qi_ref, ki_ref: (ki_ref[j], h))
    out_spec = pl.BlockSpec((t, hp * DH), lambda h, j, qi_ref, ki_ref: (qi_ref[j], h))
    grid_spec = pltpu.PrefetchScalarGridSpec(
        num_scalar_prefetch=2, grid=(H // hp, qi_tab.shape[0]),
        in_specs=[q_spec, kv_spec, kv_spec], out_specs=[out_spec, out_spec],
        scratch_shapes=[pltpu.VMEM((hp, t, 1), F32), pltpu.VMEM((hp, t, AUG), F32)])
    return pl.pallas_call(
        body, name="attn_fwd", grid_spec=grid_spec,
        out_shape=[jax.ShapeDtypeStruct((T, D), F32), jax.ShapeDtypeStruct((T, D), BF16)],
        compiler_params=_cparams(("parallel", "arbitrary"), VMEM_BIG),
    )(qi_tab, ki_tab, q_aug, k_aug, v_aug)


def _lru_gates(xc, wr_ref, br_ref, wi_ref, bi_ref, lam_ref):
    r = _sigmoid(_gate_pre(xc, wr_ref) + br_ref[...])
    ig = _sigmoid(_gate_pre(xc, wi_ref) + bi_ref[...])
    sp = _softplus_neg(lam_ref[...])
    la = (-LRU_C) * r * sp
    a = jnp.exp(la)
    sq = jnp.sqrt(_neg_expm1(2.0 * la))
    return r, ig, sp, a, sq


def _branches_fwd(o, g_attn, x_lru, g_lru, gain_a, gain_l, conv_w, conv_b, w_r, b_r, w_i, b_i, lam):
    T = o.shape[0]
    tm = TM

    def body(o_ref, ga_ref, xl_ref, gl_ref, gna_ref, gnl_ref, cw_ref, cb_ref, wr_ref, br_ref, wi_ref, bi_ref,
             lam_ref, ycat_ref, xc_ref, h_ref, halo_s, hc_s):
        @pl.when(pl.program_id(0) == 0)
        def _():
            halo_s[...] = jnp.zeros_like(halo_s)
            hc_s[...] = jnp.zeros_like(hc_s)

        ov = o_ref[...]
        ga = ga_ref[...]
        ya = ov * _rstd(ov) * gna_ref[...] * (ga * _sigmoid(ga))
        ycat_ref[:, :D] = ya.astype(BF16)

        xl = xl_ref[...]
        halo = halo_s[...]
        xc = xl * cw_ref[3:4, :] + cb_ref[...]
        for j in range(3):
            xc = xc + _shift_down(xl, 3 - j, halo) * cw_ref[j:j + 1, :]
        halo_s[...] = xl_ref[tm - SUBLANES:tm, :]
        xc_ref[...] = xc

        _, ig, _, a, sq = _lru_gates(xc, wr_ref, br_ref, wi_ref, bi_ref, lam_ref)
        u = sq * (ig * xc)
        a_cum, h_loc = _scan_fwd(a, u)
        hh = h_loc + a_cum * hc_s[...]
        h_ref[...] = hh
        hc_s[...] = h_ref[tm - 1:tm, :]

        gl = gl_ref[...]
        yl = hh * _rstd(hh) * gnl_ref[...] * (gl * _sigmoid(gl))
        ycat_ref[:, D:] = yl.astype(BF16)

    vec = _const_spec((1, D))
    wspec = _const_spec((NB, LANES, LANES))
    return pl.pallas_call(
        body, name="branches_fwd", grid=(T // tm,),
        in_specs=[_row_spec(tm, D)] * 4 + [vec, vec, _const_spec((4, D)), vec, wspec, vec, wspec, vec, vec],
        out_specs=[_row_spec(tm, DMIX), _row_spec(tm, D), _row_spec(tm, D)],
        out_shape=[jax.ShapeDtypeStruct((T, DMIX), BF16), jax.ShapeDtypeStruct((T, D), F32),
                   jax.ShapeDtypeStruct((T, D), F32)],
        scratch_shapes=[pltpu.VMEM((SUBLANES, D), F32), pltpu.VMEM((1, D), F32)],
        compiler_params=_cparams(("arbitrary",)),
    )(o, g_attn, x_lru, g_lru, gain_a, gain_l, conv_w, conv_b, w_r, b_r, w_i, b_i, lam)


def _tail(ycat, x, p, tgt, w_out, post_gain, w_ple, ple_gain, w_gate, b_gate):
    T = x.shape[0]
    tm = TM

    def body(ycat_ref, x_ref, p_ref, t_ref, wo_ref, pg_ref, wp_ref, eg_ref, wg_ref, bg_ref,
             dh1_ref, dycat_ref, dmix_ref, h1b_ref, dgp_ref, pb_ref, dpe_ref, acc_ref):
        @pl.when(pl.program_id(0) == 0)
        def _():
            acc_ref[...] = jnp.zeros_like(acc_ref)

        mix = _dot(ycat_ref[...], wo_ref[...])
        rstd_m = _rstd(mix)
        mhat = mix * rstd_m
        h1 = x_ref[...] + mhat * pg_ref[...]
        pb = p_ref[...].astype(BF16)
        pb_ref[...] = pb
        pe = _dot(pb, wp_ref[...])
        rstd_p = _rstd(pe)
        pehat = pe * rstd_p
        e = pehat * eg_ref[...]
        h1b = h1.astype(BF16)
        h1b_ref[...] = h1b
        gate = _sigmoid(_dot(h1b, wg_ref[...]) + bg_ref[...])
        diff = (h1 + gate * e) - t_ref[...]

        dy = diff * (1.0 / D)
        de = dy * gate
        dgp = (dy * e) * gate * (1.0 - gate)
        dgpb = dgp.astype(BF16)
        dgp_ref[...] = dgpb
        dh1 = dy + _dot_nt(dgpb, wg_ref[...])
        dh1_ref[...] = dh1
        dpe_ref[...] = _rms_bwd(de * eg_ref[...], pehat, rstd_p).astype(BF16)
        dmix = _rms_bwd(dh1 * pg_ref[...], mhat, rstd_m).astype(BF16)
        dmix_ref[...] = dmix
        dycat_ref[...] = _dot_nt(dmix, wo_ref[...])

        acc_ref[0:1, :] += jnp.sum(dh1 * mhat, axis=0, keepdims=True)
        acc_ref[1:2, :] += jnp.sum(de * pehat, axis=0, keepdims=True)
        acc_ref[2:3, :] += jnp.sum(dgp, axis=0, keepdims=True)
        acc_ref[3:4, :] += jnp.sum(diff * diff, axis=0, keepdims=True) * (0.5 / D)

    vec = _const_spec((1, D))
    bf = jax.ShapeDtypeStruct((T, D), BF16)
    return pl.pallas_call(
        body, name="tail", grid=(T // tm,),
        in_specs=[_row_spec(tm, DMIX), _row_spec(tm, D), _row_spec(tm, DPLE), _row_spec(tm, D),
                  _const_spec((DMIX, D)), vec, _const_spec((DPLE, D)), vec, _const_spec((D, D)), vec],
        out_specs=[_row_spec(tm, D), _row_spec(tm, DMIX), _row_spec(tm, D), _row_spec(tm, D), _row_spec(tm, D),
                   _row_spec(tm, DPLE), _row_spec(tm, D), _const_spec((SUBLANES, D))],
        out_shape=[jax.ShapeDtypeStruct((T, D), F32), jax.ShapeDtypeStruct((T, DMIX), F32), bf, bf, bf,
                   jax.ShapeDtypeStruct((T, DPLE), BF16), bf, jax.ShapeDtypeStruct((SUBLANES, D), F32)],
        compiler_params=_cparams(("arbitrary",), VMEM_BIG),
    )(ycat, x, p, tgt, w_out, post_gain, w_ple, ple_gain, w_gate, b_gate)


def _branches_bwd(dycat, o, g_attn, h, g_lru, gain_a, gain_l):
    T = o.shape[0]
    tm = TM

    def body(dy_ref, o_ref, ga_ref, h_ref, gl_ref, gna_ref, gnl_ref,
             do_ref, dga_ref, dgl_ref, dh_ref, acc_ref):
        @pl.when(pl.program_id(0) == 0)
        def _():
            acc_ref[...] = jnp.zeros_like(acc_ref)

        def branch(val, g, gain, dyv):
            rstd = _rstd(val)
            vhat = val * rstd
            sig = _sigmoid(g)
            dn = dyv * (g * sig)
            dg = dyv * (vhat * gain) * (sig * (1.0 + g * (1.0 - sig)))
            dgain = jnp.sum(dn * vhat, axis=0, keepdims=True)
            return _rms_bwd(dn * gain, vhat, rstd), dg, dgain

        ov = o_ref[...]
        do, dga, dgain_a = branch(ov, ga_ref[...], gna_ref[...], dy_ref[:, :D])
        dga_ref[...] = dga.astype(BF16)
        prod = do * ov
        for hd in range(H):
            head = slice(hd * DH, (hd + 1) * DH)
            do_ref[:, hd * AUG:hd * AUG + DH] = do[:, head].astype(BF16)
            do_ref[:, hd * AUG + DH:(hd + 1) * AUG] = _extras(-jnp.sum(prod[:, head], axis=1, keepdims=True), None)

        dh, dgl, dgain_l = branch(h_ref[...], gl_ref[...], gnl_ref[...], dy_ref[:, D:])
        dh_ref[...] = dh
        dgl_ref[...] = dgl.astype(BF16)
        acc_ref[0:1, :] += dgain_a
        acc_ref[1:2, :] += dgain_l

    vec = _const_spec((1, D))
    bf = jax.ShapeDtypeStruct((T, D), BF16)
    return pl.pallas_call(
        body, name="branches_bwd", grid=(T // tm,),
        in_specs=[_row_spec(tm, DMIX)] + [_row_spec(tm, D)] * 4 + [vec, vec],
        out_specs=[_row_spec(tm, H * AUG), _row_spec(tm, D), _row_spec(tm, D), _row_spec(tm, D),
                   _const_spec((SUBLANES, D))],
        out_shape=[jax.ShapeDtypeStruct((T, H * AUG), BF16), bf, bf, jax.ShapeDtypeStruct((T, D), F32),
                   jax.ShapeDtypeStruct((SUBLANES, D), F32)],
        compiler_params=_cparams(("arbitrary",)),
    )(dycat, o, g_attn, h, g_lru, gain_a, gain_l)


def _lru_bwd(dh, h, xc, x_lru, conv_w, w_r, b_r, w_i, b_i, lam):
    T = dh.shape[0]
    tm = TM
    nt = T // tm
    per = tm // SUBLANES

    def body(dh_ref, h_ref, hprev_ref, xc_ref, xl_ref, xlprev_ref, cw_ref, wr_ref, br_ref, wi_ref, bi_ref, lam_ref,
             dxl_ref, dwr_ref, dwi_ref, acc_ref, carry_s, dxc_next_s, top_s):
        i = pl.program_id(0)

        @pl.when(i == 0)
        def _():
            acc_ref[...] = jnp.zeros_like(acc_ref)
            dwr_ref[...] = jnp.zeros_like(dwr_ref)
            dwi_ref[...] = jnp.zeros_like(dwi_ref)
            carry_s[...] = jnp.zeros_like(carry_s)
            dxc_next_s[...] = jnp.zeros_like(dxc_next_s)

        inner = jnp.where(i == nt - 1, 0.0, 1.0)
        xc = xc_ref[...]
        r, ig, sp, a, sq = _lru_gates(xc, wr_ref, br_ref, wi_ref, bi_ref, lam_ref)

        row = lax.broadcasted_iota(jnp.int32, (tm, D), 0)
        u = dh_ref[...] + jnp.where(row == tm - 1, carry_s[...], 0.0)
        dht = _scan_bwd(pltpu.roll(a, tm - 1, 0), u)
        top_s[...] = a[:SUBLANES, :] * dht[:SUBLANES, :]
        carry_s[...] = top_s[0:1, :]

        hprev = hprev_ref[...] * inner
        da = dht * _shift_down(h_ref[...], 1, hprev)
        dig = dht * sq * xc
        dxc = dht * sq * ig
        dsq = dht * ig * xc
        dla = da * a - dsq * (a * a) / sq
        dr = dla * ((-LRU_C) * sp)
        dpr = dr * r * (1.0 - r)
        dpi = dig * ig * (1.0 - ig)
        for n in range(NB):
            blk = slice(n * LANES, (n + 1) * LANES)
            xcb = xc[:, blk].astype(BF16)
            dwr_ref[n] += _dot_tn(xcb, dpr[:, blk].astype(BF16))
            dwi_ref[n] += _dot_tn(xcb, dpi[:, blk].astype(BF16))
        dxc = dxc + _gate_pre_t(dpr, wr_ref) + _gate_pre_t(dpi, wi_ref)

        xl = xl_ref[...]
        xlprev = xlprev_ref[...] * inner
        nxt = dxc_next_s[...]
        dxl = dxc * cw_ref[3:4, :]
        acc_ref[3:4, :] += jnp.sum(dxc * xl, axis=0, keepdims=True)
        for j in range(3):
            dxl = dxl + _shift_up(dxc, 3 - j, nxt) * cw_ref[j:j + 1, :]
            acc_ref[j:j + 1, :] += jnp.sum(dxc * _shift_down(xl, 3 - j, xlprev), axis=0, keepdims=True)
        dxc_next_s[...] = dxc[:SUBLANES, :]
        dxl_ref[...] = dxl.astype(BF16)

        acc_ref[4:5, :] += jnp.sum(dxc, axis=0, keepdims=True)
        acc_ref[5:6, :] += jnp.sum(dpr, axis=0, keepdims=True)
        acc_ref[6:7, :] += jnp.sum(dpi, axis=0, keepdims=True)
        acc_ref[7:8, :] += jnp.sum(dla * ((-LRU_C) * r), axis=0, keepdims=True)

        @pl.when(i == nt - 1)
        def _():
            lam_v = lam_ref[...]
            acc_ref[7:8, :] = acc_ref[7:8, :] * (-_sigmoid(-lam_v))

    rev = pl.BlockSpec((tm, D), lambda i: (nt - 1 - i, 0))
    prev8 = pl.BlockSpec((SUBLANES, D), lambda i: (jnp.maximum((nt - 1 - i) * per - 1, 0), 0))
    vec = _const_spec((1, D))
    wspec = _const_spec((NB, LANES, LANES))
    bf = jax.ShapeDtypeStruct((T, D), BF16)
    return pl.pallas_call(
        body, name="lru_bwd", grid=(nt,),
        in_specs=[rev, rev, prev8, rev, rev, prev8, _const_spec((4, D)), wspec, vec, wspec, vec, vec],
        out_specs=[rev, wspec, wspec, _const_spec((SUBLANES, D))],
        out_shape=[bf, jax.ShapeDtypeStruct((NB, LANES, LANES), F32), jax.ShapeDtypeStruct((NB, LANES, LANES), F32),
                   jax.ShapeDtypeStruct((SUBLANES, D), F32)],
        scratch_shapes=[pltpu.VMEM((1, D), F32), pltpu.VMEM((SUBLANES, D), F32), pltpu.VMEM((SUBLANES, D), F32)],
        compiler_params=_cparams(("arbitrary",)),
    )(dh, h, h, xc, x_lru, x_lru, conv_w, w_r, b_r, w_i, b_i, lam)


def _attn_bwd(q_aug, qx, k_aug, v_aug, do_aug):
    T = q_aug.shape[0]
    t = TA
    n = T // t
    hp = ATTN_HEADS
    heads = range(hp)
    scale = DH ** -0.5
    ki_tab, qi_tab = _causal_pairs(n, q_major=False)
    last = ki_tab.shape[0] - 1

    def body(ki_ref, qi_ref, q_ref, qx_ref, k_ref, v_ref, do_ref, dq_ref, dk_ref, dv_ref, dck_ref, dcq_ref,
             dq_s, dk_s, dv_s):
        j = pl.program_id(1)
        ki = ki_ref[j]
        qi = qi_ref[j]

        @pl.when(j == 0)
        def _():
            dq_s[...] = jnp.zeros_like(dq_s)

        @pl.when(qi == ki)
        def _():
            dk_s[...] = jnp.zeros_like(dk_s)
            dv_s[...] = jnp.zeros_like(dv_s)

        def step(on_diagonal):
            cols = [slice(a * AUG, (a + 1) * AUG) for a in heads]
            qb = [jnp.concatenate([q_ref[:, a * AUG:a * AUG + DH], qx_ref[:, a * DH:(a + 1) * DH]], axis=1)
                  for a in heads]
            st = [_dot_nt(k_ref[:, cols[a]], qb[a]) for a in heads]
            if on_diagonal:
                krow = lax.broadcasted_iota(jnp.int32, (t, t), 0)
                qcol = lax.broadcasted_iota(jnp.int32, (t, t), 1)
                st = [jnp.where(krow <= qcol, st[a], NEG) for a in heads]
            pt = [jnp.exp2(st[a]) for a in heads]
            dsb = [(pt[a] * _dot_nt(v_ref[:, cols[a]], do_ref[:, cols[a]])).astype(BF16) for a in heads]
            off = pl.multiple_of(qi * t, t)
            for a in heads:
                dv_s[a] += _dot(pt[a].astype(BF16), do_ref[:, cols[a]])
                dk_s[a] += _dot(dsb[a], qb[a])
                dq_s[a, pl.ds(off, t), :] += _dot_tn(dsb[a], k_ref[:, cols[a]])

        @pl.when(qi > ki)
        def _():
            step(False)

        @pl.when(qi == ki)
        def _():
            step(True)

        @pl.when(qi == n - 1)
        def _():
            for a in heads:
                dk_ref[:, a * DH:(a + 1) * DH] = (dk_s[a, :, :DH] * LN2).astype(BF16)
                dv_ref[:, a * DH:(a + 1) * DH] = dv_s[a, :, :DH].astype(BF16)
                dck_ref[a] = jnp.broadcast_to(dk_s[a, :, DH + 3:DH + 4], (t, LANES))

        @pl.when(j == last)
        def _():
            for a in heads:
                dq_ref[:, a * DH:(a + 1) * DH] = (dq_s[a, :, :DH] * scale).astype(BF16)
                dcq_ref[a] = jnp.broadcast_to(dq_s[a, :, DH:DH + 1], (T, LANES))

    qside = pl.BlockSpec((t, hp * AUG), lambda h, j, ki_ref, qi_ref: (qi_ref[j], h))
    qxside = pl.BlockSpec((t, hp * DH), lambda h, j, ki_ref, qi_ref: (qi_ref[j], h))
    kside = pl.BlockSpec((t, hp * AUG), lambda h, j, ki_ref, qi_ref: (ki_ref[j], h))
    kout = pl.BlockSpec((t, hp * DH), lambda h, j, ki_ref, qi_ref: (ki_ref[j], h))
    bf = jax.ShapeDtypeStruct((T, D), BF16)
    sums = jax.ShapeDtypeStruct((H, T, LANES), F32)
    grid_spec = pltpu.PrefetchScalarGridSpec(
        num_scalar_prefetch=2, grid=(H // hp, ki_tab.shape[0]),
        in_specs=[qside, qxside, kside, kside, qside],
        out_specs=[pl.BlockSpec((T, hp * DH), lambda h, j, ki_ref, qi_ref: (0, h)), kout, kout,
                   pl.BlockSpec((hp, t, LANES), lambda h, j, ki_ref, qi_ref: (h, ki_ref[j], 0)),
                   pl.BlockSpec((hp, T, LANES), lambda h, j, ki_ref, qi_ref: (h, 0, 0))],
        scratch_shapes=[pltpu.VMEM((hp, T, AUG), F32), pltpu.VMEM((hp, t, AUG), F32), pltpu.VMEM((hp, t, AUG), F32)])
    return pl.pallas_call(
        body, name="attn_bwd", grid_spec=grid_spec,
        out_shape=[bf, bf, bf, sums, sums],
        compiler_params=_cparams(("arbitrary", "arbitrary"), VMEM_BIG),
    )(ki_tab, qi_tab, q_aug, qx, k_aug, v_aug, do_aug)


def _fgate_bwd(dc_key, dc_query, flb):
    T = flb.shape[0]
    tm = TM
    nt = T // tm

    def body(dck_ref, dcq_ref, flb_ref, dfl_ref, acc_ref, carry, top_s):
        @pl.when(pl.program_id(0) == 0)
        def _():
            carry[...] = jnp.zeros_like(carry)
            acc_ref[...] = jnp.zeros_like(acc_ref)

        flb = flb_ref[...]
        lane = lax.broadcasted_iota(jnp.int32, flb.shape, 1)
        dc = jnp.zeros(flb.shape, F32)
        for hd in range(H):
            dc = dc + jnp.where(lane == hd, dcq_ref[hd] - dck_ref[hd], 0.0)
        r = lax.broadcasted_iota(jnp.int32, (tm, tm), 0)
        c = lax.broadcasted_iota(jnp.int32, (tm, tm), 1)
        dls = _dot_exact((c >= r).astype(F32), dc) + carry[...]
        top_s[...] = dls[:SUBLANES, :]
        carry[...] = top_s[0:1, :]
        dfl = jnp.where(lane < H, dls * _sigmoid(-flb), 0.0)
        dfl_ref[...] = dfl.astype(BF16)
        acc_ref[0:1, :] += jnp.sum(dfl, axis=0, keepdims=True)

    rev = pl.BlockSpec((tm, LANES), lambda i: (nt - 1 - i, 0))
    return pl.pallas_call(
        body, name="fgate_bwd", grid=(nt,),
        in_specs=[pl.BlockSpec((H, tm, LANES), lambda i: (0, nt - 1 - i, 0))] * 2 + [rev],
        out_specs=[rev, _const_spec((SUBLANES, LANES))],
        out_shape=[jax.ShapeDtypeStruct((T, LANES), BF16), jax.ShapeDtypeStruct((SUBLANES, LANES), F32)],
        scratch_shapes=[pltpu.VMEM((1, LANES), F32), pltpu.VMEM((SUBLANES, LANES), F32)],
        compiler_params=_cparams(("arbitrary",)),
    )(dc_key, dc_query, flb)


def _dx(dz, dfl, w_a, w_f, w_b, x, pre_gain, dh1):
    T = x.shape[0]
    tm = TM

    def body(*refs):
        dz_refs = refs[:6]
        dfl_ref, wa_ref, wf_ref, wb_ref, x_ref, g_ref, dh1_ref, gx_ref, acc_ref = refs[6:]

        @pl.when(pl.program_id(0) == 0)
        def _():
            acc_ref[...] = jnp.zeros_like(acc_ref)

        dxn = _dot(dfl_ref[...], wf_ref[...])
        for s in range(3):
            dxn = dxn + _dot(dz_refs[s][...], wa_ref[s * D:(s + 1) * D, :])
            dxn = dxn + _dot(dz_refs[3 + s][...], wb_ref[s * D:(s + 1) * D, :])
        xv = x_ref[...]
        rstd = _rstd(xv)
        xhat = xv * rstd
        gx_ref[...] = dh1_ref[...] + _rms_bwd(dxn * g_ref[...], xhat, rstd)
        acc_ref[0:1, :] += jnp.sum(dxn * xhat, axis=0, keepdims=True)

    return pl.pallas_call(
        body, name="dx", grid=(T // tm,),
        in_specs=[_row_spec(tm, D)] * 6 + [_row_spec(tm, LANES), _const_spec((3 * D, D)), _const_spec((LANES, D)),
                                           _const_spec((3 * D, D)), _row_spec(tm, D), _const_spec((1, D)),
                                           _row_spec(tm, D)],
        out_specs=[_row_spec(tm, D), _const_spec((SUBLANES, D))],
        out_shape=[jax.ShapeDtypeStruct((T, D), F32), jax.ShapeDtypeStruct((SUBLANES, D), F32)],
        compiler_params=_cparams(("arbitrary",), VMEM_BIG),
    )(*dz, dfl, w_a, w_f, w_b, x, pre_gain, dh1)


GRAD_ROWS = D_IN + SUBLANES


def _seg_row(s):
    return (s * (D // SUBLANES) + jnp.where(s >= 3, H // SUBLANES, 0)) * SUBLANES


def _dw_in_t(dz, dfl, xn, bt=512):
    T = xn.shape[0]
    nt = T // bt

    def main_body(*refs):
        dz_refs, xn_ref, o_ref = refs[:6], refs[6], refs[7]
        s = pl.program_id(0)

        @pl.when(pl.program_id(1) == 0)
        def _():
            o_ref[...] = jnp.zeros_like(o_ref)

        for k in range(6):
            @pl.when(s == k)
            def _(k=k):
                o_ref[...] += _dot_tn(dz_refs[k][...], xn_ref[...])

    def dz_spec(k):
        return pl.BlockSpec((bt, D), lambda s, t: (jnp.where(s == k, t, 0), 0))

    main = pl.pallas_call(
        main_body, name="dw_in_main", grid=(6, nt),
        in_specs=[dz_spec(k) for k in range(6)] + [pl.BlockSpec((bt, D), lambda s, t: (t, 0))],
        out_specs=pl.BlockSpec((pl.Element(D), pl.Element(D)), lambda s, t: (_seg_row(s), 0)),
        out_shape=jax.ShapeDtypeStruct((GRAD_ROWS, D), F32),
        compiler_params=_cparams(("arbitrary", "arbitrary")),
    )(*dz, xn)

    def f_body(dfl_ref, xn_ref, main_ref, o_ref, acc_s):
        p = pl.program_id(0)
        t = pl.program_id(1)

        @pl.when(t == 0)
        def _():
            acc_s[...] = jnp.zeros_like(acc_s)

        @pl.when(p == 0)
        def _():
            acc_s[...] += _dot_tn(dfl_ref[...], xn_ref[...])

        @pl.when(t == nt - 1)
        def _():
            o_ref[...] = acc_s[:SUBLANES, :]

    fl_block = FL0 // SUBLANES
    end_block = D_IN // SUBLANES
    return pl.pallas_call(
        f_body, name="dw_in_f", grid=(2, nt),
        in_specs=[pl.BlockSpec((bt, LANES), lambda p, t: (t, 0)), pl.BlockSpec((bt, D), lambda p, t: (t, 0)),
                  pl.BlockSpec(memory_space=pl.ANY)],
        out_specs=pl.BlockSpec((SUBLANES, D), lambda p, t: (fl_block + p * (end_block - fl_block), 0)),
        out_shape=jax.ShapeDtypeStruct((GRAD_ROWS, D), F32),
        scratch_shapes=[pltpu.VMEM((LANES, D), F32)],
        input_output_aliases={2: 0},
        compiler_params=_cparams(("arbitrary", "arbitrary")),
    )(dfl, xn, main)


def _matmul_tn(a, b, name, bm=512, bn=1024, bt=512):
    T, M = a.shape
    N = b.shape[1]
    bm, bn, bt = min(bm, M), min(bn, N), min(bt, T)

    def body(a_ref, b_ref, o_ref):
        @pl.when(pl.program_id(2) == 0)
        def _():
            o_ref[...] = jnp.zeros_like(o_ref)

        o_ref[...] += _dot_tn(a_ref[...], b_ref[...])

    return pl.pallas_call(
        body, name=name, grid=(M // bm, N // bn, T // bt),
        in_specs=[pl.BlockSpec((bt, bm), lambda i, j, t: (t, i)), pl.BlockSpec((bt, bn), lambda i, j, t: (t, j))],
        out_specs=pl.BlockSpec((bm, bn), lambda i, j, t: (i, j)),
        out_shape=jax.ShapeDtypeStruct((M, N), F32),
        compiler_params=_cparams(("parallel", "parallel", "arbitrary")),
    )(a, b)


HBM_SPEC = pl.BlockSpec(memory_space=pltpu.HBM)
VMEM_SPEC = pl.BlockSpec(memory_space=pltpu.VMEM)


def _position():
    return lax.axis_index("x"), lax.axis_index("y"), lax.axis_index("c")


def _other_chips(x, y):
    return [(1 - x, y), (x, 1 - y), (1 - x, 1 - y)]


def _gather_shards(shards, whole):
    na, nw = len(shards), len(whole)
    nall = na + nw

    def body(*refs):
        srcs, dsts = refs[:nall], refs[nall:2 * nall]
        ici_send, ici_recv, d2d_send, d2d_recv = refs[2 * nall:]
        x, y, c = _position()
        chip = 2 * x + y
        chips = _other_chips(x, y)

        def half(a, which):
            rows = srcs[a].shape[0] // 2
            return pl.ds(pl.multiple_of(which * rows, 16), rows)

        first = []
        for j, (px, py) in enumerate(chips):
            for a in range(nall):
                src = srcs[a].at[half(a, c), :] if a < na else srcs[a]
                dst = dsts[a].at[chip, half(a, c), :] if a < na else dsts[a].at[chip]
                first.append(pltpu.make_async_remote_copy(
                    src_ref=src, dst_ref=dst, send_sem=ici_send.at[j * nall + a], recv_sem=ici_recv.at[j * nall + a],
                    device_id=(px, py, c), device_id_type=MESH))
        for cp in first:
            cp.start()

        passed = []
        for j, (px, py) in enumerate(chips):
            theirs = 2 * px + py
            for a in range(nall):
                if a < na:
                    landed = dsts[a].at[theirs, half(a, c), :]
                    fwd = pltpu.make_async_remote_copy(
                        src_ref=landed, dst_ref=landed, send_sem=d2d_send.at[j * na + a],
                        recv_sem=d2d_recv.at[j * na + a], device_id=(x, y, 1 - c), device_id_type=MESH)
                else:
                    landed = dsts[a].at[theirs]
                pltpu.make_async_remote_copy(
                    src_ref=landed, dst_ref=landed, send_sem=ici_send.at[j * nall + a],
                    recv_sem=ici_recv.at[j * nall + a], device_id=(px, py, c), device_id_type=MESH).wait_recv()
                if a < na:
                    fwd.start()
                    passed.append(fwd)
        for j, (px, py) in enumerate(chips):
            theirs = 2 * px + py
            for a in range(na):
                other = dsts[a].at[theirs, half(a, 1 - c), :]
                pltpu.make_async_remote_copy(
                    src_ref=other, dst_ref=other, send_sem=d2d_send.at[j * na + a], recv_sem=d2d_recv.at[j * na + a],
                    device_id=(x, y, 1 - c), device_id_type=MESH).wait_recv()
        for cp in first + passed:
            cp.wait_send()

    arrs = list(shards) + list(whole)
    outs = pl.pallas_call(
        body, name="gather_shards",
        in_specs=[HBM_SPEC] * nall, out_specs=[HBM_SPEC] * nall,
        out_shape=[jax.ShapeDtypeStruct((N_CHIPS,) + s.shape, s.dtype) for s in arrs],
        scratch_shapes=[pltpu.SemaphoreType.DMA((3 * nall,)), pltpu.SemaphoreType.DMA((3 * nall,)),
                        pltpu.SemaphoreType.DMA((3 * na,)), pltpu.SemaphoreType.DMA((3 * na,))],
    )(*arrs)
    chip = 2 * lax.axis_index("x") + lax.axis_index("y")
    return [lax.dynamic_update_slice(o, a[None], (chip,) + (0,) * a.ndim) for o, a in zip(outs, arrs)]


W_ROWS = 1568
G_ROWS = 1552
SHARD_ROWS = D_IN // N_CHIPS
WINDOW_STEP = 1536


def _assemble_w_in(cont):
    cb = 256
    half = WINDOW_STEP

    def body(c_ref, wa_ref, wf_ref, wb_ref):
        x0 = c_ref[0].astype(F32)
        x1, x2, x3 = (pltpu.roll(c_ref[j].astype(F32), 2 * j, 0) for j in (1, 2, 3))
        wa = jnp.concatenate([x0[:half], x0[half:half + 16] + x1[:16], x1[16:half]], axis=0)
        wa_ref[...] = wa.astype(BF16)

        fl = x1[half:half + 16] + x2[:16]
        row = lax.broadcasted_iota(jnp.int32, fl.shape, 0)
        wf_ref[:16, :] = jnp.where(row < H, fl, 0.0).astype(BF16)
        wf_ref[16:, :] = jnp.zeros((LANES - 16, cb), BF16)

        mid = x2[half:half + SUBLANES] + x3[:SUBLANES]
        wb = jnp.concatenate([x2[SUBLANES:half], mid, x3[SUBLANES:half + SUBLANES]], axis=0)
        wb_ref[...] = wb.astype(BF16)

    return pl.pallas_call(
        body, name="assemble_w_in", grid=(D // cb,),
        in_specs=[pl.BlockSpec((N_CHIPS, W_ROWS, cb), lambda i: (0, 0, i))],
        out_specs=[pl.BlockSpec((3 * D, cb), lambda i: (0, i)), pl.BlockSpec((LANES, cb), lambda i: (0, i)),
                   pl.BlockSpec((3 * D, cb), lambda i: (0, i))],
        out_shape=[jax.ShapeDtypeStruct((3 * D, D), BF16), jax.ShapeDtypeStruct((LANES, D), BF16),
                   jax.ShapeDtypeStruct((3 * D, D), BF16)],
        compiler_params=_cparams(("parallel",)),
    )(cont)


def _pair_exchange(grad_t, parts):
    na = len(parts)
    n = N_CHIPS + na
    half_g = G_ROWS // 2

    def body(*refs):
        g_ref, srcs, got = refs[0], refs[1:1 + na], refs[1 + na:2 + 2 * na]
        send_sems, recv_sems = refs[2 + 2 * na:]
        x, y, c = _position()
        pieces = []
        for j in range(N_CHIPS):
            rows = pl.ds(pl.multiple_of(j * WINDOW_STEP + (1 - c) * half_g, SUBLANES), half_g)
            pieces.append((g_ref.at[rows, :], got[0].at[j]))
        for a in range(na):
            half = srcs[a].shape[1] // 2
            rows = pl.ds(pl.multiple_of((1 - c) * half, SUBLANES), half)
            pieces.append((srcs[a].at[:, rows, :], got[1 + a]))
        copies = [pltpu.make_async_remote_copy(
            src_ref=give, dst_ref=dst, send_sem=send_sems.at[k], recv_sem=recv_sems.at[k],
            device_id=(x, y, 1 - c), device_id_type=MESH) for k, (give, dst) in enumerate(pieces)]
        for cp in copies:
            cp.start()
        for cp in copies:
            cp.wait()

    halves = [jax.ShapeDtypeStruct((N_CHIPS, half_g, D), F32)]
    halves += [jax.ShapeDtypeStruct((s.shape[0], s.shape[1] // 2, s.shape[2]), s.dtype) for s in parts]
    return pl.pallas_call(
        body, name="pair_exchange",
        in_specs=[HBM_SPEC] * (1 + na), out_specs=[HBM_SPEC] * (1 + na),
        out_shape=halves,
        scratch_shapes=[pltpu.SemaphoreType.DMA((n,)), pltpu.SemaphoreType.DMA((n,))],
    )(grad_t, *parts)


def _pair_sum(part, got, c, name):
    _, half, C = got.shape
    cb = min(C, 256)

    def body(c_ref, a_ref, b_ref, o_ref):
        o_ref[...] = (a_ref[...] + b_ref[...]).astype(BF16)

    spec = pl.BlockSpec((1, half, cb), lambda j, i, c_ref: (j, 0, i))
    grid_spec = pltpu.PrefetchScalarGridSpec(
        num_scalar_prefetch=1, grid=(N_CHIPS, C // cb),
        in_specs=[pl.BlockSpec((1, half, cb), lambda j, i, c_ref: (j, c_ref[0], i)), spec], out_specs=spec)
    return pl.pallas_call(
        body, name=name, grid_spec=grid_spec,
        out_shape=jax.ShapeDtypeStruct((N_CHIPS, half, C), BF16),
        compiler_params=_cparams(("parallel", "parallel")),
    )(c.reshape(1), part, got)


def _pair_sum_windows(grad_t, got, c):
    _, half, C = got.shape
    cb = 256

    def body(c_ref, a_ref, b_ref, o_ref):
        o_ref[0] = (a_ref[...] + b_ref[0]).astype(BF16)

    def mine(j, i, c_ref):
        return ((j * (WINDOW_STEP // SUBLANES) + c_ref[0] * (half // SUBLANES)) * SUBLANES, i * cb)

    spec = pl.BlockSpec((1, half, cb), lambda j, i, c_ref: (j, 0, i))
    grid_spec = pltpu.PrefetchScalarGridSpec(
        num_scalar_prefetch=1, grid=(N_CHIPS, C // cb),
        in_specs=[pl.BlockSpec((pl.Element(half), pl.Element(cb)), mine), spec], out_specs=spec)
    return pl.pallas_call(
        body, name="pair_sum_w_in", grid_spec=grid_spec,
        out_shape=jax.ShapeDtypeStruct((N_CHIPS, half, C), BF16),
        compiler_params=_cparams(("parallel", "parallel")),
    )(c.reshape(1), grad_t, got)


def _chip_exchange(sums):
    na = len(sums)

    def body(*refs):
        srcs, dsts = refs[:na], refs[na:2 * na]
        send_sems, recv_sems = refs[2 * na:]
        x, y, c = _position()
        chip = 2 * x + y
        copies = []
        for j, (px, py) in enumerate(_other_chips(x, y)):
            for a in range(na):
                copies.append(pltpu.make_async_remote_copy(
                    src_ref=srcs[a].at[2 * px + py], dst_ref=dsts[a].at[chip], send_sem=send_sems.at[j * na + a],
                    recv_sem=recv_sems.at[j * na + a], device_id=(px, py, c), device_id_type=MESH))
        for cp in copies:
            cp.start()
        for cp in copies:
            cp.wait()

    return pl.pallas_call(
        body, name="chip_exchange",
        in_specs=[HBM_SPEC] * na, out_specs=[HBM_SPEC] * na,
        out_shape=[jax.ShapeDtypeStruct(s.shape, s.dtype) for s in sums],
        scratch_shapes=[pltpu.SemaphoreType.DMA((3 * na,)), pltpu.SemaphoreType.DMA((3 * na,))],
    )(*sums)


def _chip_sum(own, got, chip, name):
    _, half, C = got.shape
    cb = min(C, 256)

    def body(chip_ref, own_ref, g_ref, o_ref):
        for me in range(N_CHIPS):
            @pl.when(chip_ref[0] == me)
            def _(me=me):
                terms = [own_ref[0] if k == me else g_ref[k] for k in range(N_CHIPS)]
                acc = terms[0].astype(F32) + terms[1].astype(F32)
                acc = acc + terms[2].astype(F32)
                o_ref[...] = acc + terms[3].astype(F32)

    grid_spec = pltpu.PrefetchScalarGridSpec(
        num_scalar_prefetch=1, grid=(C // cb,),
        in_specs=[pl.BlockSpec((1, half, cb), lambda i, chip_ref: (chip_ref[0], 0, i)),
                  pl.BlockSpec((N_CHIPS, half, cb), lambda i, chip_ref: (0, 0, i))],
        out_specs=pl.BlockSpec((half, cb), lambda i, chip_ref: (0, i)))
    return pl.pallas_call(
        body, name=name, grid_spec=grid_spec,
        out_shape=jax.ShapeDtypeStruct((half, C), F32),
        compiler_params=_cparams(("parallel",)),
    )(chip.reshape(1), own, got)


def _pair_swap(halves):
    na = len(halves)

    def body(*refs):
        srcs, dsts = refs[:na], refs[na:2 * na]
        send_sems, recv_sems = refs[2 * na:]
        x, y, c = _position()
        copies = [pltpu.make_async_remote_copy(
            src_ref=srcs[a], dst_ref=dsts[a], send_sem=send_sems.at[a], recv_sem=recv_sems.at[a],
            device_id=(x, y, 1 - c), device_id_type=MESH) for a in range(na)]
        for cp in copies:
            cp.start()
        for cp in copies:
            cp.wait()

    return pl.pallas_call(
        body, name="pair_swap",
        in_specs=[HBM_SPEC] * na, out_specs=[HBM_SPEC] * na,
        out_shape=[jax.ShapeDtypeStruct(s.shape, s.dtype) for s in halves],
        scratch_shapes=[pltpu.SemaphoreType.DMA((na,)), pltpu.SemaphoreType.DMA((na,))],
    )(*halves)


def _allreduce_small(g):
    rows = g.shape[0]
    per = rows // N_DEV

    def body(g_ref, out_ref, got_ref, s1, r1, s2, r2):
        x, y, c = _position()
        me = 4 * x + 2 * y + c
        mine = pl.ds(pl.multiple_of(me * per, SUBLANES), per)
        peers = []
        for j in range(1, N_DEV):
            px = 1 - x if j & 4 else x
            py = 1 - y if j & 2 else y
            pc = 1 - c if j & 1 else c
            peers.append((px, py, pc))

        first = []
        for j, (px, py, pc) in enumerate(peers):
            theirs = pl.ds(pl.multiple_of((4 * px + 2 * py + pc) * per, SUBLANES), per)
            first.append(pltpu.make_async_remote_copy(
                src_ref=g_ref.at[theirs, :], dst_ref=got_ref.at[me], send_sem=s1.at[j], recv_sem=r1.at[j],
                device_id=(px, py, pc), device_id_type=MESH))
        for cp in first:
            cp.start()
        got_ref[me] = g_ref[mine, :]
        for cp in first:
            cp.wait()
        total = got_ref[0]
        for d in range(1, N_DEV):
            total = total + got_ref[d]
        out_ref[mine, :] = total

        second = []
        for j, peer in enumerate(peers):
            second.append(pltpu.make_async_remote_copy(
                src_ref=out_ref.at[mine, :], dst_ref=out_ref.at[mine, :], send_sem=s2.at[j], recv_sem=r2.at[j],
                device_id=peer, device_id_type=MESH))
        for cp in second:
            cp.start()
        for cp in second:
            cp.wait()

    sems = pltpu.SemaphoreType.DMA((N_DEV - 1,))
    return pl.pallas_call(
        body, name="allreduce_small",
        in_specs=[VMEM_SPEC], out_specs=VMEM_SPEC,
        out_shape=jax.ShapeDtypeStruct(g.shape, F32),
        scratch_shapes=[pltpu.VMEM((N_DEV, per, LANES), F32), sems, sems, sems, sems],
    )(g)


def _adamw_math(g, w, m, v):
    m2 = ADAM_B1 * m + (1.0 - ADAM_B1) * g
    v2 = ADAM_B2 * v + (1.0 - ADAM_B2) * (g * g)
    m_hat = m2 / (1.0 - ADAM_B1 ** ADAM_STEP)
    v_hat = v2 / (1.0 - ADAM_B2 ** ADAM_STEP)
    delta = (-ADAM_LR) * (m_hat / (jnp.sqrt(v_hat) + ADAM_EPS) + ADAM_WD * w)
    return delta, m2, v2


def _adamw_big(g, w, m, v, name):
    R, C = g.shape
    cb = min(C, LANES)

    def body(g_ref, w_ref, m_ref, v_ref, d_ref, m2_ref, v2_ref):
        d_ref[...], m2_ref[...], v2_ref[...] = _adamw_math(g_ref[...], w_ref[...], m_ref[...], v_ref[...])

    spec = pl.BlockSpec((R, cb), lambda i: (0, i))
    out = jax.ShapeDtypeStruct((R, C), F32)
    return pl.pallas_call(
        body, name=name, grid=(C // cb,),
        in_specs=[spec] * 4, out_specs=[spec] * 3, out_shape=[out] * 3,
        compiler_params=_cparams(("parallel",)),
    )(g, w, m, v)


def _adamw_small(gs, ws, ms, vs):
    n = len(gs)

    def body(*refs):
        for a in range(n):
            g_ref, w_ref, m_ref, v_ref = (refs[k * n + a] for k in range(4))
            d_ref, m2_ref, v2_ref = (refs[(4 + k) * n + a] for k in range(3))
            d_ref[...], m2_ref[...], v2_ref[...] = _adamw_math(g_ref[...], w_ref[...], m_ref[...], v_ref[...])

    outs = [jax.ShapeDtypeStruct(w.shape, F32) for w in ws]
    return pl.pallas_call(
        body, name="adamw_small",
        in_specs=[VMEM_SPEC] * (4 * n), out_specs=[VMEM_SPEC] * (3 * n), out_shape=outs * 3,
    )(*gs, *ws, *ms, *vs)


def _local_step(x, p, tgt, w_a, w_f, w_b, w_out_b, w_ple_b, w_gate_b, conv_w, b_f, pre_gain, post_gain, conv_b,
                w_rgate, b_rgate, w_igate, b_igate, lam, gain_a, gain_l, ple_gain, b_gate):
    b_f_pad = jnp.pad(b_f, ((0, 0), (0, LANES - H)))
    w_r = w_rgate.astype(BF16)
    w_i = w_igate.astype(BF16)

    xn, q_aug, k_aug, v_aug, g_attn, x_lru, g_lru, flb = _in_proj(x, pre_gain, w_a, w_f, w_b, b_f_pad)
    o, qx = _attn_fwd(q_aug, k_aug, v_aug)
    ycat, xc, h = _branches_fwd(o, g_attn, x_lru, g_lru, gain_a, gain_l, conv_w, conv_b, w_r, b_rgate, w_i, b_igate,
                                lam)
    dh1, dycat, dmix, h1b, dgp, pb, dpe, acc_t = _tail(ycat, x, p, tgt, w_out_b, post_gain, w_ple_b, ple_gain,
                                                       w_gate_b, b_gate)
    do_aug, dg_attn, dg_lru, dh, acc_b = _branches_bwd(dycat, o, g_attn, h, g_lru, gain_a, gain_l)
    dx_lru, gw_r, gw_i, acc_l = _lru_bwd(dh, h, xc, x_lru, conv_w, w_r, b_rgate, w_i, b_igate, lam)
    dq, dk, dv, dc_key, dc_query = _attn_bwd(q_aug, qx, k_aug, v_aug, do_aug)
    dfl, acc_f = _fgate_bwd(dc_key, dc_query, flb)
    dz = (dq, dk, dv, dg_attn, dx_lru, dg_lru)
    grad_x, acc_x = _dx(dz, dfl, w_a, w_f, w_b, x, pre_gain, dh1)

    grads = dict(
        w_in_t=_dw_in_t(dz, dfl, xn),
        w_out=_matmul_tn(ycat, dmix, "dw_out"),
        w_ple=_matmul_tn(pb, dpe, "dw_ple"),
        w_ple_gate=_matmul_tn(h1b, dgp, "dw_ple_gate"),
        w_rgate=gw_r,
        w_igate=gw_i,
        b_f=acc_f[0:1, :H],
        pre_gain=acc_x[0:1],
        post_gain=acc_t[0:1],
        conv_w=acc_l[0:4],
        conv_b=acc_l[4:5],
        b_rgate=acc_l[5:6],
        b_igate=acc_l[6:7],
        lru_lambda=acc_l[7:8],
        attn_out_gain=acc_b[0:1],
        lru_out_gain=acc_b[1:2],
        ple_gain=acc_t[1:2],
        b_ple_gate=acc_t[2:3],
    )
    loss = jnp.sum(acc_t[3])
    return loss, grad_x, grads


SMALL_ROWS = ["b_f", "pre_gain", "post_gain", "conv_w", "conv_b", "b_rgate", "b_igate", "lru_lambda",
              "attn_out_gain", "lru_out_gain", "ple_gain", "b_ple_gate"]
WEIGHTS = ["w_in", "b_f", "pre_gain", "post_gain", "conv_w", "conv_b", "w_rgate", "b_rgate", "w_igate", "b_igate",
           "lru_lambda", "attn_out_gain", "lru_out_gain", "w_out", "w_ple", "ple_gain", "w_ple_gate", "b_ple_gate"]
SHARDED = ["w_in", "w_out", "w_ple", "w_ple_gate"]


def _by_chip_cols(g):
    r, cols = g.shape
    return g.reshape(r, N_CHIPS, cols // N_CHIPS).transpose(1, 0, 2)


def _from_chip_cols(s):
    n, r, cols = s.shape
    return s.transpose(1, 0, 2).reshape(r, n * cols)


def kernel(x, p, w_in, b_f, pre_gain, post_gain, conv_w, conv_b, w_rgate, b_rgate, w_igate, b_igate, lru_lambda, attn_out_gain, lru_out_gain, w_out, w_ple, ple_gain, w_ple_gate, b_ple_gate, loss_target, m_w_in, m_b_f, m_pre_gain, m_post_gain, m_conv_w, m_conv_b, m_w_rgate, m_b_rgate, m_w_igate, m_b_igate, m_lru_lambda, m_attn_out_gain, m_lru_out_gain, m_w_out, m_w_ple, m_ple_gain, m_w_ple_gate, m_b_ple_gate, v_w_in, v_b_f, v_pre_gain, v_post_gain, v_conv_w, v_conv_b, v_w_rgate, v_b_rgate, v_w_igate, v_b_igate, v_lru_lambda, v_attn_out_gain, v_lru_out_gain, v_w_out, v_w_ple, v_ple_gain, v_w_ple_gate, v_b_ple_gate):
    w = dict(w_in=w_in, b_f=b_f, pre_gain=pre_gain, post_gain=post_gain, conv_w=conv_w, conv_b=conv_b,
             w_rgate=w_rgate, b_rgate=b_rgate, w_igate=w_igate, b_igate=b_igate, lru_lambda=lru_lambda,
             attn_out_gain=attn_out_gain, lru_out_gain=lru_out_gain, w_out=w_out, w_ple=w_ple, ple_gain=ple_gain,
             w_ple_gate=w_ple_gate, b_ple_gate=b_ple_gate)
    m = dict(w_in=m_w_in, b_f=m_b_f, pre_gain=m_pre_gain, post_gain=m_post_gain, conv_w=m_conv_w, conv_b=m_conv_b,
             w_rgate=m_w_rgate, b_rgate=m_b_rgate, w_igate=m_w_igate, b_igate=m_b_igate, lru_lambda=m_lru_lambda,
             attn_out_gain=m_attn_out_gain, lru_out_gain=m_lru_out_gain, w_out=m_w_out, w_ple=m_w_ple,
             ple_gain=m_ple_gain, w_ple_gate=m_w_ple_gate, b_ple_gate=m_b_ple_gate)
    v = dict(w_in=v_w_in, b_f=v_b_f, pre_gain=v_pre_gain, post_gain=v_post_gain, conv_w=v_conv_w, conv_b=v_conv_b,
             w_rgate=v_w_rgate, b_rgate=v_b_rgate, w_igate=v_w_igate, b_igate=v_b_igate, lru_lambda=v_lru_lambda,
             attn_out_gain=v_attn_out_gain, lru_out_gain=v_lru_out_gain, w_out=v_w_out, w_ple=v_w_ple,
             ple_gain=v_ple_gain, w_ple_gate=v_w_ple_gate, b_ple_gate=v_b_ple_gate)
    xi, yi, ci = _position()
    chip = 2 * xi + yi

    w_in_t, m_in_t, v_in_t = (jnp.swapaxes(t[0], 0, 1) for t in (w_in, m_w_in, v_w_in))
    window = jnp.pad(w_in_t.astype(BF16), ((0, W_ROWS - SHARD_ROWS), (0, 0)))

    st_in, st_out, st_ple, st_gate, st_conv = _gather_shards(
        [window, w_out[0].astype(BF16), w_ple[0].astype(BF16), w_ple_gate[0].astype(BF16)], [conv_w[0]])
    w_a, w_f, w_b = _assemble_w_in(st_in)
    w_out_b = st_out.reshape(DMIX, D)
    w_ple_b = _from_chip_cols(st_ple)
    w_gate_b = st_gate.reshape(D, D)
    conv_full = _from_chip_cols(st_conv)

    loss, grad_x, g = _local_step(
        x[0], p[0, 0], loss_target[0], w_a, w_f, w_b, w_out_b, w_ple_b, w_gate_b, conv_full, b_f, pre_gain, post_gain,
        conv_b, w_rgate[0], b_rgate, w_igate[0], b_igate, lru_lambda, attn_out_gain, lru_out_gain, ple_gain,
        b_ple_gate)
    loss = lax.psum(loss, ("x", "y", "c"))

    parts = [g["w_out"].reshape(N_CHIPS, DMIX // N_CHIPS, D), _by_chip_cols(g["w_ple"]),
             g["w_ple_gate"].reshape(N_CHIPS, D // N_CHIPS, D)]
    got = _pair_exchange(g["w_in_t"], parts)
    sums = [_pair_sum_windows(g["w_in_t"], got[0], ci)]
    sums += [_pair_sum(parts[a], got[1 + a], ci, "pair_sum_%d" % a) for a in range(3)]
    recv = _chip_exchange(sums)
    halves = [_chip_sum(sums[a], recv[a], chip, "chip_sum_%d" % a) for a in range(4)]
    theirs = _pair_swap(halves)
    full = [jnp.concatenate([jnp.where(ci == 0, a, b), jnp.where(ci == 0, b, a)], axis=0)
            for a, b in zip(halves, theirs)]
    red = dict(zip(SHARDED, full))
    red["w_in"] = lax.dynamic_slice_in_dim(red["w_in"], 2 * chip, SHARD_ROWS, axis=0)

    rows = [jnp.pad(g["b_f"], ((0, 0), (0, D - H)))] + [g[n] for n in SMALL_ROWS[1:]]
    rows.append(jnp.zeros((16 - sum(r.shape[0] for r in rows), D), F32))
    packed = jnp.concatenate([g["w_rgate"].reshape(NB * LANES, LANES), g["w_igate"].reshape(NB * LANES, LANES),
                              jnp.concatenate(rows, axis=0).reshape(LANES, LANES)], axis=0)
    summed = _allreduce_small(packed)
    red["w_rgate"] = summed[:D].reshape(1, NB, LANES, LANES)
    red["w_igate"] = summed[D:2 * D].reshape(1, NB, LANES, LANES)
    vec = summed[2 * D:].reshape(16, D)
    r0 = 0
    for n in SMALL_ROWS:
        nr = 4 if n == "conv_w" else 1
        red[n] = vec[r0:r0 + nr]
        r0 += nr
    red["b_f"] = red["b_f"][:, :H]
    red["conv_w"] = lax.dynamic_slice_in_dim(red["conv_w"], chip * (D // N_CHIPS), D // N_CHIPS, axis=1)[None]

    delta, new_m, new_v = {}, {}, {}
    outs_in = _adamw_big(red["w_in"], w_in_t, m_in_t, v_in_t, "adamw_w_in")
    delta["w_in"], new_m["w_in"], new_v["w_in"] = (jnp.swapaxes(t, 0, 1)[None] for t in outs_in)
    red["w_in"] = jnp.swapaxes(red["w_in"], 0, 1)[None]
    for n in SHARDED[1:]:
        delta[n], new_m[n], new_v[n] = (t[None] for t in _adamw_big(red[n], w[n][0], m[n][0], v[n][0], "adamw_" + n))
        red[n] = red[n][None]
    small = [n for n in WEIGHTS if n not in SHARDED]
    outs = _adamw_small([red[n] for n in small], [w[n] for n in small], [m[n] for n in small],
                        [v[n] for n in small])
    ns = len(small)
    for a, n in enumerate(small):
        delta[n], new_m[n], new_v[n] = outs[a], outs[ns + a], outs[2 * ns + a]

    return (loss, grad_x[None], *[red[n] for n in WEIGHTS], *[delta[n] for n in WEIGHTS],
            *[new_m[n] for n in WEIGHTS], *[new_v[n] for n in WEIGHTS])
```

```python
import functools

import jax
import jax.numpy as jnp
import numpy as np
from jax import lax
from jax.experimental import pallas as pl
from jax.experimental.pallas import tpu as pltpu

F32 = jnp.float32
BF16 = jnp.bfloat16

D = 1024
H = 8
DH = 128
NB = 8
DPLE = 256
DMIX = 2 * D
D_IN = 4 * D + H + 2 * D
FL0 = 3 * D
RMS_EPS = 1e-6
LRU_C = 8.0
NEG = -1e30
LANES = 128
SUBLANES = 8

ADAM_LR = 0.001
ADAM_B1 = 0.9
ADAM_B2 = 0.999
ADAM_EPS = 1e-08
ADAM_WD = 0.01
ADAM_STEP = 10

TM = 256
TA = 512
FWD_HEADS = 4
BWD_HEADS = 2
VMEM_BIG = 56 * 1024 * 1024
VMEM_MID = 40 * 1024 * 1024

MESH = pl.DeviceIdType.MESH
N_CHIPS = 4
N_DEV = 8


def _cparams(sem, vmem=VMEM_MID):
    return pltpu.CompilerParams(dimension_semantics=sem, vmem_limit_bytes=vmem)


def _sigmoid(x):
    return 1.0 / (1.0 + jnp.exp(-x))


def _rstd(x):
    return lax.rsqrt(jnp.mean(x * x, axis=-1, keepdims=True) + RMS_EPS)


def _rms_bwd(t, xhat, rstd):
    return rstd * (t - xhat * jnp.mean(t * xhat, axis=-1, keepdims=True))


def _dot(a, b):
    return jnp.dot(a, b, preferred_element_type=F32)


def _dot_nt(a, b):
    return lax.dot_general(a, b, (((1,), (1,)), ((), ())), preferred_element_type=F32)


def _dot_tn(a, b):
    return lax.dot_general(a, b, (((0,), (0,)), ((), ())), preferred_element_type=F32)


def _dot_exact(a, b):
    return jnp.dot(a, b, preferred_element_type=F32, precision=lax.Precision.HIGHEST)


def _neg_expm1(x):
    series = x * (1.0 + x * 0.5 * (1.0 + x * (1.0 / 3.0) * (1.0 + x * 0.25 * (1.0 + x * 0.2 * (1.0 + x * (1.0 / 6.0))))))
    return -jnp.where(x > -0.25, series, jnp.exp(x) - 1.0)


def _shift_down(x, j, halo):
    rolled = pltpu.roll(x, j, 0)
    row = lax.broadcasted_iota(jnp.int32, halo.shape, 0)
    top = jnp.where(row < j, pltpu.roll(halo, j, 0), rolled[:SUBLANES])
    return jnp.concatenate([top, rolled[SUBLANES:]], axis=0)


def _shift_up(x, j, nxt):
    tm = x.shape[0]
    rolled = pltpu.roll(x, tm - j, 0)
    row = lax.broadcasted_iota(jnp.int32, nxt.shape, 0)
    bot = jnp.where(row >= SUBLANES - j, pltpu.roll(nxt, SUBLANES - j, 0), rolled[tm - SUBLANES:])
    return jnp.concatenate([rolled[:tm - SUBLANES], bot], axis=0)


def _scan_fwd_into(a, u, carry, h_ref):
    tm = a.shape[0]
    sub = lax.broadcasted_iota(jnp.int32, a.shape, 0) & (SUBLANES - 1)
    d = 1
    while d < SUBLANES:
        keep = sub >= d
        a_s = jnp.where(keep, pltpu.roll(a, d, 0), 1.0)
        u_s = jnp.where(keep, pltpu.roll(u, d, 0), 0.0)
        u = u + a * u_s
        a = a * a_s
        d *= 2
    for g in range(tm // SUBLANES):
        rows = slice(g * SUBLANES, (g + 1) * SUBLANES)
        h_ref[rows, :] = u[rows] + a[rows] * carry
        carry = h_ref[(g + 1) * SUBLANES - 1:(g + 1) * SUBLANES, :]
    return carry


def _scan_bwd_into(b, u, g_ref):
    tm = b.shape[0]
    sub = lax.broadcasted_iota(jnp.int32, b.shape, 0) & (SUBLANES - 1)
    d = 1
    while d < SUBLANES:
        keep = sub < SUBLANES - d
        b_s = jnp.where(keep, pltpu.roll(b, tm - d, 0), 1.0)
        u_s = jnp.where(keep, pltpu.roll(u, tm - d, 0), 0.0)
        u = u + b * u_s
        b = b * b_s
        d *= 2
    nxt = jnp.zeros((1, b.shape[1]), F32)
    for g in reversed(range(tm // SUBLANES)):
        rows = slice(g * SUBLANES, (g + 1) * SUBLANES)
        g_ref[rows, :] = u[rows] + b[rows] * nxt
        nxt = g_ref[g * SUBLANES:g * SUBLANES + 1, :]


def _gate_pre(xc, w_ref):
    outs = []
    for n in range(NB):
        outs.append(_dot(xc[:, n * LANES:(n + 1) * LANES].astype(BF16), w_ref[n]))
    return jnp.concatenate(outs, axis=1)


def _gate_pre_t(d, w_ref):
    outs = []
    for n in range(NB):
        outs.append(_dot_nt(d[:, n * LANES:(n + 1) * LANES].astype(BF16), w_ref[n]))
    return jnp.concatenate(outs, axis=1)


def _softplus_neg(lam):
    return jnp.maximum(-lam, 0.0) + jnp.log(1.0 + jnp.exp(-jnp.abs(lam)))


def _row_spec(tm, width):
    return pl.BlockSpec((tm, width), lambda i: (i, 0))


def _const_spec(shape):
    nd = len(shape)
    return pl.BlockSpec(shape, lambda *_: (0,) * nd)


AUG = 2 * DH
LOG2E = 1.4426950408889634
LN2 = 0.6931471805599453
Q_SCALE = DH ** -0.5 * LOG2E


def _split3(x):
    hi = x.astype(BF16)
    r1 = x - hi.astype(F32)
    mid = r1.astype(BF16)
    lo = (r1 - mid.astype(F32)).astype(BF16)
    return hi, mid, lo


def _extras(col, ones_from):
    t = col.shape[0]
    hi, mid, lo = _split3(jnp.broadcast_to(col, (t, LANES)))
    lane = lax.broadcasted_iota(jnp.int32, (t, LANES), 1)
    rest = jnp.zeros((t, LANES), BF16)
    if ones_from is not None:
        rest = jnp.where((lane >= ones_from) & (lane < ones_from + 3), 1.0, 0.0).astype(BF16)
    return jnp.where(lane == 0, hi, jnp.where(lane == 1, mid, jnp.where(lane == 2, lo, rest)))


def _selectors():
    sel_q = np.zeros((3 * LANES, H * LANES), np.float32)
    sel_k = np.zeros((3 * LANES, H * LANES), np.float32)
    for hd in range(H):
        for piece in range(3):
            sel_q[piece * LANES + hd, hd * LANES + piece] = 1.0
            sel_k[piece * LANES + hd, hd * LANES + 3 + piece] = -1.0
    return jnp.asarray(sel_q, BF16), jnp.asarray(sel_k, BF16)


def _in_proj(x, pre_gain, w_a, w_f, w_b, b_f_pad):
    T = x.shape[0]
    tm = TM
    sel_q, sel_k = _selectors()

    def body(x_ref, g_ref, wa_ref, wf_ref, wb_ref, bf_ref, sq_ref, sk_ref,
             xn_ref, qa_ref, ka_ref, va_ref, ga_ref, xl_ref, gl_ref, flb_ref, c_s, carry):
        @pl.when(pl.program_id(0) == 0)
        def _():
            carry[...] = jnp.zeros_like(carry)

        xv = x_ref[...]
        xn = (xv * _rstd(xv) * g_ref[...]).astype(BF16)
        xn_ref[...] = xn
        for s, o_ref in enumerate((ga_ref, xl_ref, gl_ref)):
            o_ref[...] = _dot_nt(xn, wb_ref[s * D:(s + 1) * D, :]).astype(o_ref.dtype)
        flb = _dot_nt(xn, wf_ref[...]) + bf_ref[...]
        flb_ref[...] = flb
        lane = lax.broadcasted_iota(jnp.int32, flb.shape, 1)
        ls = jnp.where(lane < H, jnp.minimum(flb, 0.0) - jnp.log(1.0 + jnp.exp(-jnp.abs(flb))), 0.0)
        r = lax.broadcasted_iota(jnp.int32, (tm, tm), 0)
        c = lax.broadcasted_iota(jnp.int32, (tm, tm), 1)
        cs = _dot_exact((c <= r).astype(F32), ls) + carry[...]
        c_s[...] = cs
        carry[...] = c_s[tm - 1:tm, :]

        pieces = jnp.concatenate(_split3(cs * LOG2E), axis=1)
        ones_q = jnp.where((lane >= 3) & (lane < 6), 1.0, 0.0)
        ones_k = jnp.where(lane < 3, 1.0, 0.0)
        zq = _dot_nt(xn, wa_ref[0:D, :]) * Q_SCALE
        zk = _dot_nt(xn, wa_ref[D:2 * D, :])
        zv = _dot_nt(xn, wa_ref[2 * D:3 * D, :])
        ex_q = _dot(pieces, sq_ref[...])
        ex_k = _dot(pieces, sk_ref[...])
        for hd in range(H):
            head = slice(hd * DH, (hd + 1) * DH)
            lo, hi = hd * AUG, hd * AUG + DH
            qa_ref[:, lo:hi] = zq[:, head].astype(BF16)
            qa_ref[:, hi:hi + DH] = (ex_q[:, head] + ones_q).astype(BF16)
            ka_ref[:, lo:hi] = zk[:, head].astype(BF16)
            ka_ref[:, hi:hi + DH] = (ex_k[:, head] + ones_k).astype(BF16)
            va_ref[:, lo:hi] = zv[:, head].astype(BF16)
            va_ref[:, hi:hi + DH] = ones_k.astype(BF16)

    bf = jax.ShapeDtypeStruct((T, D), BF16)
    aug = jax.ShapeDtypeStruct((T, H * AUG), BF16)
    f32 = jax.ShapeDtypeStruct((T, D), F32)
    sel_spec = _const_spec((3 * LANES, H * LANES))
    return pl.pallas_call(
        body, name="in_proj", grid=(T // tm,),
        in_specs=[_row_spec(tm, D), _const_spec((1, D)), _const_spec((3 * D, D)), _const_spec((LANES, D)),
                  _const_spec((3 * D, D)), _const_spec((1, LANES)), sel_spec, sel_spec],
        out_specs=[_row_spec(tm, D)] + [_row_spec(tm, H * AUG)] * 3 + [_row_spec(tm, D)] * 3 + [_row_spec(tm, LANES)],
        out_shape=[bf, aug, aug, aug, f32, f32, f32, jax.ShapeDtypeStruct((T, LANES), F32)],
        scratch_shapes=[pltpu.VMEM((tm, LANES), F32), pltpu.VMEM((1, LANES), F32)],
        compiler_params=_cparams(("arbitrary",), VMEM_BIG),
    )(x, pre_gain, w_a, w_f, w_b, b_f_pad, sel_q, sel_k)


def _causal_pairs(n, q_major):
    if q_major:
        pairs = [(qi, ki) for qi in range(n) for ki in range(qi + 1)]
    else:
        pairs = [(ki, qi) for ki in range(n) for qi in range(ki, n)]
    return (jnp.asarray([a for a, _ in pairs], jnp.int32), jnp.asarray([b for _, b in pairs], jnp.int32))


def _attn_fwd(q_aug, k_aug, v_aug):
    T = q_aug.shape[0]
    t = TA
    n = T // t
    hp = FWD_HEADS
    heads = range(hp)
    qi_tab, ki_tab = _causal_pairs(n, q_major=True)

    def body(qi_ref, ki_ref, q_ref, k_ref, v_ref, o_ref, qx_ref, m_s, acc_s):
        j = pl.program_id(1)
        qi = qi_ref[j]
        ki = ki_ref[j]

        @pl.when(ki == 0)
        def _():
            m_s[...] = jnp.full(m_s.shape, NEG, F32)
            acc_s[...] = jnp.zeros_like(acc_s)

        def step(on_diagonal):
            cols = [slice(a * AUG, (a + 1) * AUG) for a in heads]
            s = [_dot_nt(q_ref[:, cols[a]], k_ref[:, cols[a]]) for a in heads]
            if on_diagonal:
                row = lax.broadcasted_iota(jnp.int32, (t, t), 0)
                col = lax.broadcasted_iota(jnp.int32, (t, t), 1)
                s = [jnp.where(col <= row, s[a], NEG) for a in heads]
            m_prev = [m_s[a] for a in heads]
            m_new = [jnp.maximum(m_prev[a], jnp.max(s[a], axis=1, keepdims=True)) for a in heads]
            pr = [jnp.exp2(s[a] - m_new[a]).astype(BF16) for a in heads]
            for a in heads:
                acc_s[a] = jnp.exp2(m_prev[a] - m_new[a]) * acc_s[a] + _dot(pr[a], v_ref[:, cols[a]])
                m_s[a] = m_new[a]

        @pl.when(ki < qi)
        def _():
            step(False)

        @pl.when(ki == qi)
        def _():
            step(True)
            for a in heads:
                acc = acc_s[a]
                l = acc[:, DH:DH + 1]
                o_ref[:, a * DH:(a + 1) * DH] = acc[:, :DH] / l
                ex = q_ref[:, a * AUG + DH:(a + 1) * AUG].astype(F32)
                c2 = ex[:, 0:1] + ex[:, 1:2] + ex[:, 2:3]
                qx_ref[:, a * DH:(a + 1) * DH] = _extras(c2 - (m_s[a] + jnp.log(l) * LOG2E), 3)

    q_spec = pl.BlockSpec((t, hp * AUG), lambda h, j, qi_ref, ki_ref: (qi_ref[j], h))
    kv_spec = pl.BlockSpec((t, hp * AUG), lambda h, j, qi_ref, ki_ref: (ki_ref[j], h))
    out_spec = pl.BlockSpec((t, hp * DH), lambda h, j, qi_ref, ki_ref: (qi_ref[j], h))
    grid_spec = pltpu.PrefetchScalarGridSpec(
        num_scalar_prefetch=2, grid=(H // hp, qi_tab.shape[0]),
        in_specs=[q_spec, kv_spec, kv_spec], out_specs=[out_spec, out_spec],
        scratch_shapes=[pltpu.VMEM((hp, t, 1), F32), pltpu.VMEM((hp, t, AUG), F32)])
    return pl.pallas_call(
        body, name="attn_fwd", grid_spec=grid_spec,
        out_shape=[jax.ShapeDtypeStruct((T, D), F32), jax.ShapeDtypeStruct((T, D), BF16)],
        compiler_params=_cparams(("parallel", "arbitrary"), VMEM_BIG),
    )(qi_tab, ki_tab, q_aug, k_aug, v_aug)


def _lru_gates(xc, wr_ref, br_ref, wi_ref, bi_ref, lam_ref):
    r = _sigmoid(_gate_pre(xc, wr_ref) + br_ref[...])
    ig = _sigmoid(_gate_pre(xc, wi_ref) + bi_ref[...])
    sp = _softplus_neg(lam_ref[...])
    la = (-LRU_C) * r * sp
    a = jnp.exp(la)
    sq = jnp.sqrt(_neg_expm1(2.0 * la))
    return r, ig, sp, a, sq


def _branches_fwd(o, g_attn, x_lru, g_lru, gain_a, gain_l, conv_w, conv_b, w_r, b_r, w_i, b_i, lam):
    T = o.shape[0]
    tm = TM

    def body(o_ref, ga_ref, xl_ref, gl_ref, gna_ref, gnl_ref, cw_ref, cb_ref, wr_ref, br_ref, wi_ref, bi_ref,
             lam_ref, ycat_ref, xc_ref, h_ref, halo_s, hc_s):
        @pl.when(pl.program_id(0) == 0)
        def _():
            halo_s[...] = jnp.zeros_like(halo_s)
            hc_s[...] = jnp.zeros_like(hc_s)

        ov = o_ref[...]
        ga = ga_ref[...]
        ya = ov * _rstd(ov) * gna_ref[...] * (ga * _sigmoid(ga))
        ycat_ref[:, :D] = ya.astype(BF16)

        xl = xl_ref[...]
        halo = halo_s[...]
        xc = xl * cw_ref[3:4, :] + cb_ref[...]
        for j in range(3):
            xc = xc + _shift_down(xl, 3 - j, halo) * cw_ref[j:j + 1, :]
        halo_s[...] = xl_ref[tm - SUBLANES:tm, :]
        xc_ref[...] = xc

        _, ig, _, a, sq = _lru_gates(xc, wr_ref, br_ref, wi_ref, bi_ref, lam_ref)
        u = sq * (ig * xc)
        hc_s[...] = _scan_fwd_into(a, u, hc_s[...], h_ref)
        hh = h_ref[...]

        gl = gl_ref[...]
        yl = hh * _rstd(hh) * gnl_ref[...] * (gl * _sigmoid(gl))
        ycat_ref[:, D:] = yl.astype(BF16)

    vec = _const_spec((1, D))
    wspec = _const_spec((NB, LANES, LANES))
    return pl.pallas_call(
        body, name="branches_fwd", grid=(T // tm,),
        in_specs=[_row_spec(tm, D)] * 4 + [vec, vec, _const_spec((4, D)), vec, wspec, vec, wspec, vec, vec],
        out_specs=[_row_spec(tm, DMIX), _row_spec(tm, D), _row_spec(tm, D)],
        out_shape=[jax.ShapeDtypeStruct((T, DMIX), BF16), jax.ShapeDtypeStruct((T, D), F32),
                   jax.ShapeDtypeStruct((T, D), F32)],
        scratch_shapes=[pltpu.VMEM((SUBLANES, D), F32), pltpu.VMEM((1, D), F32)],
        compiler_params=_cparams(("arbitrary",)),
    )(o, g_attn, x_lru, g_lru, gain_a, gain_l, conv_w, conv_b, w_r, b_r, w_i, b_i, lam)


def _tail(ycat, x, p, tgt, w_out, post_gain, w_ple, ple_gain, w_gate, b_gate):
    T = x.shape[0]
    tm = TM

    def body(ycat_ref, x_ref, p_ref, t_ref, wo_ref, pg_ref, wp_ref, eg_ref, wg_ref, bg_ref,
             dh1_ref, dycat_ref, dmix_ref, h1b_ref, dgp_ref, pb_ref, dpe_ref, acc_ref):
        @pl.when(pl.program_id(0) == 0)
        def _():
            acc_ref[...] = jnp.zeros_like(acc_ref)

        mix = _dot(ycat_ref[...], wo_ref[...])
        rstd_m = _rstd(mix)
        mhat = mix * rstd_m
        h1 = x_ref[...] + mhat * pg_ref[...]
        pb = p_ref[...].astype(BF16)
        pb_ref[...] = pb
        pe = _dot(pb, wp_ref[...])
        rstd_p = _rstd(pe)
        pehat = pe * rstd_p
        e = pehat * eg_ref[...]
        h1b = h1.astype(BF16)
        h1b_ref[...] = h1b
        gate = _sigmoid(_dot(h1b, wg_ref[...]) + bg_ref[...])
        diff = (h1 + gate * e) - t_ref[...]

        dy = diff * (1.0 / D)
        de = dy * gate
        dgp = (dy * e) * gate * (1.0 - gate)
        dgpb = dgp.astype(BF16)
        dgp_ref[...] = dgpb
        dh1 = dy + _dot_nt(dgpb, wg_ref[...])
        dh1_ref[...] = dh1
        dpe_ref[...] = _rms_bwd(de * eg_ref[...], pehat, rstd_p).astype(BF16)
        dmix = _rms_bwd(dh1 * pg_ref[...], mhat, rstd_m).astype(BF16)
        dmix_ref[...] = dmix
        dycat_ref[...] = _dot_nt(dmix, wo_ref[...])

        acc_ref[0:1, :] += jnp.sum(dh1 * mhat, axis=0, keepdims=True)
        acc_ref[1:2, :] += jnp.sum(de * pehat, axis=0, keepdims=True)
        acc_ref[2:3, :] += jnp.sum(dgp, axis=0, keepdims=True)
        acc_ref[3:4, :] += jnp.sum(diff * diff, axis=0, keepdims=True) * (0.5 / D)

    vec = _const_spec((1, D))
    bf = jax.ShapeDtypeStruct((T, D), BF16)
    return pl.pallas_call(
        body, name="tail", grid=(T // tm,),
        in_specs=[_row_spec(tm, DMIX), _row_spec(tm, D), _row_spec(tm, DPLE), _row_spec(tm, D),
                  _const_spec((DMIX, D)), vec, _const_spec((DPLE, D)), vec, _const_spec((D, D)), vec],
        out_specs=[_row_spec(tm, D), _row_spec(tm, DMIX), _row_spec(tm, D), _row_spec(tm, D), _row_spec(tm, D),
                   _row_spec(tm, DPLE), _row_spec(tm, D), _const_spec((SUBLANES, D))],
        out_shape=[jax.ShapeDtypeStruct((T, D), F32), jax.ShapeDtypeStruct((T, DMIX), F32), bf, bf, bf,
                   jax.ShapeDtypeStruct((T, DPLE), BF16), bf, jax.ShapeDtypeStruct((SUBLANES, D), F32)],
        compiler_params=_cparams(("arbitrary",), VMEM_BIG),
    )(ycat, x, p, tgt, w_out, post_gain, w_ple, ple_gain, w_gate, b_gate)


def _branches_bwd(dycat, o, g_attn, h, g_lru, gain_a, gain_l):
    T = o.shape[0]
    tm = TM

    def body(dy_ref, o_ref, ga_ref, h_ref, gl_ref, gna_ref, gnl_ref,
             do_ref, dga_ref, dgl_ref, dh_ref, acc_ref):
        @pl.when(pl.program_id(0) == 0)
        def _():
            acc_ref[...] = jnp.zeros_like(acc_ref)

        def branch(val, g, gain, dyv):
            rstd = _rstd(val)
            vhat = val * rstd
            sig = _sigmoid(g)
            dn = dyv * (g * sig)
            dg = dyv * (vhat * gain) * (sig * (1.0 + g * (1.0 - sig)))
            dgain = jnp.sum(dn * vhat, axis=0, keepdims=True)
            return _rms_bwd(dn * gain, vhat, rstd), dg, dgain

        ov = o_ref[...]
        do, dga, dgain_a = branch(ov, ga_ref[...], gna_ref[...], dy_ref[:, :D])
        dga_ref[...] = dga.astype(BF16)
        prod = do * ov
        for hd in range(H):
            head = slice(hd * DH, (hd + 1) * DH)
            do_ref[:, hd * AUG:hd * AUG + DH] = do[:, head].astype(BF16)
            do_ref[:, hd * AUG + DH:(hd + 1) * AUG] = _extras(-jnp.sum(prod[:, head], axis=1, keepdims=True), None)

        dh, dgl, dgain_l = branch(h_ref[...], gl_ref[...], gnl_ref[...], dy_ref[:, D:])
        dh_ref[...] = dh
        dgl_ref[...] = dgl.astype(BF16)
        acc_ref[0:1, :] += dgain_a
        acc_ref[1:2, :] += dgain_l

    vec = _const_spec((1, D))
    bf = jax.ShapeDtypeStruct((T, D), BF16)
    return pl.pallas_call(
        body, name="branches_bwd", grid=(T // tm,),
        in_specs=[_row_spec(tm, DMIX)] + [_row_spec(tm, D)] * 4 + [vec, vec],
        out_specs=[_row_spec(tm, H * AUG), _row_spec(tm, D), _row_spec(tm, D), _row_spec(tm, D),
                   _const_spec((SUBLANES, D))],
        out_shape=[jax.ShapeDtypeStruct((T, H * AUG), BF16), bf, bf, jax.ShapeDtypeStruct((T, D), F32),
                   jax.ShapeDtypeStruct((SUBLANES, D), F32)],
        compiler_params=_cparams(("arbitrary",)),
    )(dycat, o, g_attn, h, g_lru, gain_a, gain_l)


def _lru_bwd(dh, h, xc, x_lru, conv_w, w_r, b_r, w_i, b_i, lam):
    T = dh.shape[0]
    tm = TM
    nt = T // tm
    per = tm // SUBLANES

    def body(dh_ref, h_ref, hprev_ref, xc_ref, xl_ref, xlprev_ref, cw_ref, wr_ref, br_ref, wi_ref, bi_ref, lam_ref,
             dxl_ref, dwr_ref, dwi_ref, acc_ref, carry_s, dxc_next_s, top_s, dht_s):
        i = pl.program_id(0)

        @pl.when(i == 0)
        def _():
            acc_ref[...] = jnp.zeros_like(acc_ref)
            dwr_ref[...] = jnp.zeros_like(dwr_ref)
            dwi_ref[...] = jnp.zeros_like(dwi_ref)
            carry_s[...] = jnp.zeros_like(carry_s)
            dxc_next_s[...] = jnp.zeros_like(dxc_next_s)

        inner = jnp.where(i == nt - 1, 0.0, 1.0)
        xc = xc_ref[...]
        r, ig, sp, a, sq = _lru_gates(xc, wr_ref, br_ref, wi_ref, bi_ref, lam_ref)

        row = lax.broadcasted_iota(jnp.int32, (tm, D), 0)
        u = dh_ref[...] + jnp.where(row == tm - 1, carry_s[...], 0.0)
        _scan_bwd_into(pltpu.roll(a, tm - 1, 0), u, dht_s)
        dht = dht_s[...]
        top_s[...] = a[:SUBLANES, :] * dht[:SUBLANES, :]
        carry_s[...] = top_s[0:1, :]

        hprev = hprev_ref[...] * inner
        da = dht * _shift_down(h_ref[...], 1, hprev)
        dig = dht * sq * xc
        dxc = dht * sq * ig
        dsq = dht * ig * xc
        dla = da * a - dsq * (a * a) / sq
        dr = dla * ((-LRU_C) * sp)
        dpr = dr * r * (1.0 - r)
        dpi = dig * ig * (1.0 - ig)
        for n in range(NB):
            blk = slice(n * LANES, (n + 1) * LANES)
            xcb = xc[:, blk].astype(BF16)
            dwr_ref[n] += _dot_tn(xcb, dpr[:, blk].astype(BF16))
            dwi_ref[n] += _dot_tn(xcb, dpi[:, blk].astype(BF16))
        dxc = dxc + _gate_pre_t(dpr, wr_ref) + _gate_pre_t(dpi, wi_ref)

        xl = xl_ref[...]
        xlprev = xlprev_ref[...] * inner
        nxt = dxc_next_s[...]
        dxl = dxc * cw_ref[3:4, :]
        acc_ref[3:4, :] += jnp.sum(dxc * xl, axis=0, keepdims=True)
        for j in range(3):
            dxl = dxl + _shift_up(dxc, 3 - j, nxt) * cw_ref[j:j + 1, :]
            acc_ref[j:j + 1, :] += jnp.sum(dxc * _shift_down(xl, 3 - j, xlprev), axis=0, keepdims=True)
        dxc_next_s[...] = dxc[:SUBLANES, :]
        dxl_ref[...] = dxl.astype(BF16)

        acc_ref[4:5, :] += jnp.sum(dxc, axis=0, keepdims=True)
        acc_ref[5:6, :] += jnp.sum(dpr, axis=0, keepdims=True)
        acc_ref[6:7, :] += jnp.sum(dpi, axis=0, keepdims=True)
        acc_ref[7:8, :] += jnp.sum(dla * ((-LRU_C) * r), axis=0, keepdims=True)

        @pl.when(i == nt - 1)
        def _():
            lam_v = lam_ref[...]
            acc_ref[7:8, :] = acc_ref[7:8, :] * (-_sigmoid(-lam_v))

    rev = pl.BlockSpec((tm, D), lambda i: (nt - 1 - i, 0))
    prev8 = pl.BlockSpec((SUBLANES, D), lambda i: (jnp.maximum((nt - 1 - i) * per - 1, 0), 0))
    vec = _const_spec((1, D))
    wspec = _const_spec((NB, LANES, LANES))
    bf = jax.ShapeDtypeStruct((T, D), BF16)
    return pl.pallas_call(
        body, name="lru_bwd", grid=(nt,),
        in_specs=[rev, rev, prev8, rev, rev, prev8, _const_spec((4, D)), wspec, vec, wspec, vec, vec],
        out_specs=[rev, wspec, wspec, _const_spec((SUBLANES, D))],
        out_shape=[bf, jax.ShapeDtypeStruct((NB, LANES, LANES), F32), jax.ShapeDtypeStruct((NB, LANES, LANES), F32),
                   jax.ShapeDtypeStruct((SUBLANES, D), F32)],
        scratch_shapes=[pltpu.VMEM((1, D), F32), pltpu.VMEM((SUBLANES, D), F32), pltpu.VMEM((SUBLANES, D), F32),
                        pltpu.VMEM((tm, D), F32)],
        compiler_params=_cparams(("arbitrary",)),
    )(dh, h, h, xc, x_lru, x_lru, conv_w, w_r, b_r, w_i, b_i, lam)


def _attn_bwd(q_aug, qx, k_aug, v_aug, do_aug):
    T = q_aug.shape[0]
    t = TA
    n = T // t
    hp = BWD_HEADS
    heads = range(hp)
    scale = DH ** -0.5
    ki_tab, qi_tab = _causal_pairs(n, q_major=False)
    last = ki_tab.shape[0] - 1

    def body(ki_ref, qi_ref, q_ref, qx_ref, k_ref, v_ref, do_ref, dq_ref, dk_ref, dv_ref, dck_ref, dcq_ref,
             dq_s, dk_s, dv_s):
        j = pl.program_id(1)
        ki = ki_ref[j]
        qi = qi_ref[j]

        @pl.when(j == 0)
        def _():
            dq_s[...] = jnp.zeros_like(dq_s)

        @pl.when(qi == ki)
        def _():
            dk_s[...] = jnp.zeros_like(dk_s)
            dv_s[...] = jnp.zeros_like(dv_s)

        def step(on_diagonal):
            cols = [slice(a * AUG, (a + 1) * AUG) for a in heads]
            qb = [jnp.concatenate([q_ref[:, a * AUG:a * AUG + DH], qx_ref[:, a * DH:(a + 1) * DH]], axis=1)
                  for a in heads]
            st = [_dot_nt(k_ref[:, cols[a]], qb[a]) for a in heads]
            if on_diagonal:
                krow = lax.broadcasted_iota(jnp.int32, (t, t), 0)
                qcol = lax.broadcasted_iota(jnp.int32, (t, t), 1)
                st = [jnp.where(krow <= qcol, st[a], NEG) for a in heads]
            pt = [jnp.exp2(st[a]) for a in heads]
            dsb = [(pt[a] * _dot_nt(v_ref[:, cols[a]], do_ref[:, cols[a]])).astype(BF16) for a in heads]
            off = pl.multiple_of(qi * t, t)
            for a in heads:
                dv_s[a] += _dot(pt[a].astype(BF16), do_ref[:, cols[a]])
                dk_s[a] += _dot(dsb[a], qb[a])
                dq_s[a, pl.ds(off, t), :] += _dot_tn(dsb[a], k_ref[:, cols[a]])

        @pl.when(qi > ki)
        def _():
            step(False)

        @pl.when(qi == ki)
        def _():
            step(True)

        @pl.when(qi == n - 1)
        def _():
            for a in heads:
                dk_ref[:, a * DH:(a + 1) * DH] = (dk_s[a, :, :DH] * LN2).astype(BF16)
                dv_ref[:, a * DH:(a + 1) * DH] = dv_s[a, :, :DH].astype(BF16)
                dck_ref[a] = jnp.broadcast_to(dk_s[a, :, DH + 3:DH + 4], (t, LANES))

        @pl.when(j == last)
        def _():
            for a in heads:
                dq_ref[:, a * DH:(a + 1) * DH] = (dq_s[a, :, :DH] * scale).astype(BF16)
                dcq_ref[a] = jnp.broadcast_to(dq_s[a, :, DH:DH + 1], (T, LANES))

    qside = pl.BlockSpec((t, hp * AUG), lambda h, j, ki_ref, qi_ref: (qi_ref[j], h))
    qxside = pl.BlockSpec((t, hp * DH), lambda h, j, ki_ref, qi_ref: (qi_ref[j], h))
    kside = pl.BlockSpec((t, hp * AUG), lambda h, j, ki_ref, qi_ref: (ki_ref[j], h))
    kout = pl.BlockSpec((t, hp * DH), lambda h, j, ki_ref, qi_ref: (ki_ref[j], h))
    bf = jax.ShapeDtypeStruct((T, D), BF16)
    sums = jax.ShapeDtypeStruct((H, T, LANES), F32)
    grid_spec = pltpu.PrefetchScalarGridSpec(
        num_scalar_prefetch=2, grid=(H // hp, ki_tab.shape[0]),
        in_specs=[qside, qxside, kside, kside, qside],
        out_specs=[pl.BlockSpec((T, hp * DH), lambda h, j, ki_ref, qi_ref: (0, h)), kout, kout,
                   pl.BlockSpec((hp, t, LANES), lambda h, j, ki_ref, qi_ref: (h, ki_ref[j], 0)),
                   pl.BlockSpec((hp, T, LANES), lambda h, j, ki_ref, qi_ref: (h, 0, 0))],
        scratch_shapes=[pltpu.VMEM((hp, T, AUG), F32), pltpu.VMEM((hp, t, AUG), F32), pltpu.VMEM((hp, t, AUG), F32)])
    return pl.pallas_call(
        body, name="attn_bwd", grid_spec=grid_spec,
        out_shape=[bf, bf, bf, sums, sums],
        compiler_params=_cparams(("arbitrary", "arbitrary"), VMEM_BIG),
    )(ki_tab, qi_tab, q_aug, qx, k_aug, v_aug, do_aug)


def _fgate_bwd(dc_key, dc_query, flb):
    T = flb.shape[0]
    tm = TM
    nt = T // tm

    def body(dck_ref, dcq_ref, flb_ref, dfl_ref, acc_ref, carry, top_s):
        @pl.when(pl.program_id(0) == 0)
        def _():
            carry[...] = jnp.zeros_like(carry)
            acc_ref[...] = jnp.zeros_like(acc_ref)

        flb = flb_ref[...]
        lane = lax.broadcasted_iota(jnp.int32, flb.shape, 1)
        dc = jnp.zeros(flb.shape, F32)
        for hd in range(H):
            dc = dc + jnp.where(lane == hd, dcq_ref[hd] - dck_ref[hd], 0.0)
        r = lax.broadcasted_iota(jnp.int32, (tm, tm), 0)
        c = lax.broadcasted_iota(jnp.int32, (tm, tm), 1)
        dls = _dot_exact((c >= r).astype(F32), dc) + carry[...]
        top_s[...] = dls[:SUBLANES, :]
        carry[...] = top_s[0:1, :]
        dfl = jnp.where(lane < H, dls * _sigmoid(-flb), 0.0)
        dfl_ref[...] = dfl.astype(BF16)
        acc_ref[0:1, :] += jnp.sum(dfl, axis=0, keepdims=True)

    rev = pl.BlockSpec((tm, LANES), lambda i: (nt - 1 - i, 0))
    return pl.pallas_call(
        body, name="fgate_bwd", grid=(nt,),
        in_specs=[pl.BlockSpec((H, tm, LANES), lambda i: (0, nt - 1 - i, 0))] * 2 + [rev],
        out_specs=[rev, _const_spec((SUBLANES, LANES))],
        out_shape=[jax.ShapeDtypeStruct((T, LANES), BF16), jax.ShapeDtypeStruct((SUBLANES, LANES), F32)],
        scratch_shapes=[pltpu.VMEM((1, LANES), F32), pltpu.VMEM((SUBLANES, LANES), F32)],
        compiler_params=_cparams(("arbitrary",)),
    )(dc_key, dc_query, flb)


def _dx(dz, dfl, w_a, w_f, w_b, x, pre_gain, dh1):
    T = x.shape[0]
    tm = TM

    def body(*refs):
        dz_refs = refs[:6]
        dfl_ref, wa_ref, wf_ref, wb_ref, x_ref, g_ref, dh1_ref, gx_ref, acc_ref = refs[6:]

        @pl.when(pl.program_id(0) == 0)
        def _():
            acc_ref[...] = jnp.zeros_like(acc_ref)

        dxn = _dot(dfl_ref[...], wf_ref[...])
        for s in range(3):
            dxn = dxn + _dot(dz_refs[s][...], wa_ref[s * D:(s + 1) * D, :])
            dxn = dxn + _dot(dz_refs[3 + s][...], wb_ref[s * D:(s + 1) * D, :])
        xv = x_ref[...]
        rstd = _rstd(xv)
        xhat = xv * rstd
        gx_ref[...] = dh1_ref[...] + _rms_bwd(dxn * g_ref[...], xhat, rstd)
        acc_ref[0:1, :] += jnp.sum(dxn * xhat, axis=0, keepdims=True)

    return pl.pallas_call(
        body, name="dx", grid=(T // tm,),
        in_specs=[_row_spec(tm, D)] * 6 + [_row_spec(tm, LANES), _const_spec((3 * D, D)), _const_spec((LANES, D)),
                                           _const_spec((3 * D, D)), _row_spec(tm, D), _const_spec((1, D)),
                                           _row_spec(tm, D)],
        out_specs=[_row_spec(tm, D), _const_spec((SUBLANES, D))],
        out_shape=[jax.ShapeDtypeStruct((T, D), F32), jax.ShapeDtypeStruct((SUBLANES, D), F32)],
        compiler_params=_cparams(("arbitrary",), VMEM_BIG),
    )(*dz, dfl, w_a, w_f, w_b, x, pre_gain, dh1)


GRAD_ROWS = D_IN + SUBLANES


def _dw_in_segment(dz_s, xn, buf, s, bt):
    T = xn.shape[0]
    row0 = s * D + (H if s >= 3 else 0)

    def body(*refs):
        dz_ref, xn_ref, o_ref = refs[0], refs[1], refs[-1]

        @pl.when(pl.program_id(0) == 0)
        def _():
            o_ref[...] = jnp.zeros_like(o_ref)

        o_ref[...] += _dot_tn(dz_ref[...], xn_ref[...])

    tok = pl.BlockSpec((bt, D), lambda t: (t, 0))
    return pl.pallas_call(
        body, name="dw_in_%d" % s, grid=(T // bt,),
        in_specs=[tok, tok] + ([] if buf is None else [pl.BlockSpec(memory_space=pl.ANY)]),
        out_specs=pl.BlockSpec((pl.Element(D), pl.Element(D)), lambda t: (row0, 0)),
        out_shape=jax.ShapeDtypeStruct((GRAD_ROWS, D), F32),
        input_output_aliases={} if buf is None else {2: 0},
        compiler_params=_cparams(("arbitrary",)),
    )(*((dz_s, xn) if buf is None else (dz_s, xn, buf)))


def _dw_in_t(dz, dfl, xn, bt=512):
    T = xn.shape[0]
    nt = T // bt
    main = None
    for s in range(6):
        main = _dw_in_segment(dz[s], xn, main, s, min(T, 2048))

    def f_body(dfl_ref, xn_ref, main_ref, o_ref, acc_s):
        p = pl.program_id(0)
        t = pl.program_id(1)

        @pl.when(t == 0)
        def _():
            acc_s[...] = jnp.zeros_like(acc_s)

        @pl.when(p == 0)
        def _():
            acc_s[...] += _dot_tn(dfl_ref[...], xn_ref[...])

        @pl.when(t == nt - 1)
        def _():
            o_ref[...] = acc_s[:SUBLANES, :]

    fl_block = FL0 // SUBLANES
    end_block = D_IN // SUBLANES
    return pl.pallas_call(
        f_body, name="dw_in_f", grid=(2, nt),
        in_specs=[pl.BlockSpec((bt, LANES), lambda p, t: (t, 0)), pl.BlockSpec((bt, D), lambda p, t: (t, 0)),
                  pl.BlockSpec(memory_space=pl.ANY)],
        out_specs=pl.BlockSpec((SUBLANES, D), lambda p, t: (fl_block + p * (end_block - fl_block), 0)),
        out_shape=jax.ShapeDtypeStruct((GRAD_ROWS, D), F32),
        scratch_shapes=[pltpu.VMEM((LANES, D), F32)],
        input_output_aliases={2: 0},
        compiler_params=_cparams(("arbitrary", "arbitrary")),
    )(dfl, xn, main)


def _matmul_tn(a, b, name, bm=512, bn=1024, bt=2048):
    T, M = a.shape
    N = b.shape[1]
    bm, bn, bt = min(bm, M), min(bn, N), min(bt, T)

    def body(a_ref, b_ref, o_ref):
        @pl.when(pl.program_id(2) == 0)
        def _():
            o_ref[...] = jnp.zeros_like(o_ref)

        o_ref[...] += _dot_tn(a_ref[...], b_ref[...])

    return pl.pallas_call(
        body, name=name, grid=(M // bm, N // bn, T // bt),
        in_specs=[pl.BlockSpec((bt, bm), lambda i, j, t: (t, i)), pl.BlockSpec((bt, bn), lambda i, j, t: (t, j))],
        out_specs=pl.BlockSpec((bm, bn), lambda i, j, t: (i, j)),
        out_shape=jax.ShapeDtypeStruct((M, N), F32),
        compiler_params=_cparams(("parallel", "parallel", "arbitrary")),
    )(a, b)


HBM_SPEC = pl.BlockSpec(memory_space=pltpu.HBM)
VMEM_SPEC = pl.BlockSpec(memory_space=pltpu.VMEM)


def _position():
    return lax.axis_index("x"), lax.axis_index("y"), lax.axis_index("c")


def _other_chips(x, y):
    return [(1 - x, y), (x, 1 - y), (1 - x, 1 - y)]


def _gather_shards(shards, whole):
    na, nw = len(shards), len(whole)
    nall = na + nw

    def body(*refs):
        srcs, dsts = refs[:nall], refs[nall:2 * nall]
        ici_send, ici_recv, d2d_send, d2d_recv = refs[2 * nall:]
        x, y, c = _position()
        chip = 2 * x + y
        chips = _other_chips(x, y)

        def half(a, which):
            rows = srcs[a].shape[0] // 2
            return pl.ds(pl.multiple_of(which * rows, 16), rows)

        first = []
        for j, (px, py) in enumerate(chips):
            for a in range(nall):
                src = srcs[a].at[half(a, c), :] if a < na else srcs[a]
                dst = dsts[a].at[chip, half(a, c), :] if a < na else dsts[a].at[chip]
                first.append(pltpu.make_async_remote_copy(
                    src_ref=src, dst_ref=dst, send_sem=ici_send.at[j * nall + a], recv_sem=ici_recv.at[j * nall + a],
                    device_id=(px, py, c), device_id_type=MESH))
        for cp in first:
            cp.start()

        passed = []
        for j, (px, py) in enumerate(chips):
            theirs = 2 * px + py
            for a in range(nall):
                if a < na:
                    landed = dsts[a].at[theirs, half(a, c), :]
                    fwd = pltpu.make_async_remote_copy(
                        src_ref=landed, dst_ref=landed, send_sem=d2d_send.at[j * na + a],
                        recv_sem=d2d_recv.at[j * na + a], device_id=(x, y, 1 - c), device_id_type=MESH)
                else:
                    landed = dsts[a].at[theirs]
                pltpu.make_async_remote_copy(
                    src_ref=landed, dst_ref=landed, send_sem=ici_send.at[j * nall + a],
                    recv_sem=ici_recv.at[j * nall + a], device_id=(px, py, c), device_id_type=MESH).wait_recv()
                if a < na:
                    fwd.start()
                    passed.append(fwd)
        for j, (px, py) in enumerate(chips):
            theirs = 2 * px + py
            for a in range(na):
                other = dsts[a].at[theirs, half(a, 1 - c), :]
                pltpu.make_async_remote_copy(
                    src_ref=other, dst_ref=other, send_sem=d2d_send.at[j * na + a], recv_sem=d2d_recv.at[j * na + a],
                    device_id=(x, y, 1 - c), device_id_type=MESH).wait_recv()
        for cp in first + passed:
            cp.wait_send()

    arrs = list(shards) + list(whole)
    outs = pl.pallas_call(
        body, name="gather_shards",
        in_specs=[HBM_SPEC] * nall, out_specs=[HBM_SPEC] * nall,
        out_shape=[jax.ShapeDtypeStruct((N_CHIPS,) + s.shape, s.dtype) for s in arrs],
        scratch_shapes=[pltpu.SemaphoreType.DMA((3 * nall,)), pltpu.SemaphoreType.DMA((3 * nall,)),
                        pltpu.SemaphoreType.DMA((3 * na,)), pltpu.SemaphoreType.DMA((3 * na,))],
    )(*arrs)
    chip = 2 * lax.axis_index("x") + lax.axis_index("y")
    return [lax.dynamic_update_slice(o, a[None], (chip,) + (0,) * a.ndim) for o, a in zip(outs, arrs)]


W_ROWS = 1568
G_ROWS = 1552
SHARD_ROWS = D_IN // N_CHIPS
WINDOW_STEP = 1536


def _assemble_w_in(cont):
    cb = 256
    half = WINDOW_STEP

    def body(c_ref, wa_ref, wf_ref, wb_ref):
        x0 = c_ref[0].astype(F32)
        x1, x2, x3 = (pltpu.roll(c_ref[j].astype(F32), 2 * j, 0) for j in (1, 2, 3))
        wa = jnp.concatenate([x0[:half], x0[half:half + 16] + x1[:16], x1[16:half]], axis=0)
        wa_ref[...] = wa.astype(BF16)

        fl = x1[half:half + 16] + x2[:16]
        row = lax.broadcasted_iota(jnp.int32, fl.shape, 0)
        wf_ref[:16, :] = jnp.where(row < H, fl, 0.0).astype(BF16)
        wf_ref[16:, :] = jnp.zeros((LANES - 16, cb), BF16)

        mid = x2[half:half + SUBLANES] + x3[:SUBLANES]
        wb = jnp.concatenate([x2[SUBLANES:half], mid, x3[SUBLANES:half + SUBLANES]], axis=0)
        wb_ref[...] = wb.astype(BF16)

    return pl.pallas_call(
        body, name="assemble_w_in", grid=(D // cb,),
        in_specs=[pl.BlockSpec((N_CHIPS, W_ROWS, cb), lambda i: (0, 0, i))],
        out_specs=[pl.BlockSpec((3 * D, cb), lambda i: (0, i)), pl.BlockSpec((LANES, cb), lambda i: (0, i)),
                   pl.BlockSpec((3 * D, cb), lambda i: (0, i))],
        out_shape=[jax.ShapeDtypeStruct((3 * D, D), BF16), jax.ShapeDtypeStruct((LANES, D), BF16),
                   jax.ShapeDtypeStruct((3 * D, D), BF16)],
        compiler_params=_cparams(("parallel",)),
    )(cont)


def _pair_exchange(grad_t, parts):
    na = len(parts)
    n = N_CHIPS + na
    half_g = G_ROWS // 2

    def body(*refs):
        g_ref, srcs, got = refs[0], refs[1:1 + na], refs[1 + na:2 + 2 * na]
        send_sems, recv_sems = refs[2 + 2 * na:]
        x, y, c = _position()
        pieces = []
        for j in range(N_CHIPS):
            rows = pl.ds(pl.multiple_of(j * WINDOW_STEP + (1 - c) * half_g, SUBLANES), half_g)
            pieces.append((g_ref.at[rows, :], got[0].at[j]))
        for a in range(na):
            half = srcs[a].shape[1] // 2
            rows = pl.ds(pl.multiple_of((1 - c) * half, SUBLANES), half)
            pieces.append((srcs[a].at[:, rows, :], got[1 + a]))
        copies = [pltpu.make_async_remote_copy(
            src_ref=give, dst_ref=dst, send_sem=send_sems.at[k], recv_sem=recv_sems.at[k],
            device_id=(x, y, 1 - c), device_id_type=MESH) for k, (give, dst) in enumerate(pieces)]
        for cp in copies:
            cp.start()
        for cp in copies:
            cp.wait()

    halves = [jax.ShapeDtypeStruct((N_CHIPS, half_g, D), F32)]
    halves += [jax.ShapeDtypeStruct((s.shape[0], s.shape[1] // 2, s.shape[2]), s.dtype) for s in parts]
    return pl.pallas_call(
        body, name="pair_exchange",
        in_specs=[HBM_SPEC] * (1 + na), out_specs=[HBM_SPEC] * (1 + na),
        out_shape=halves,
        scratch_shapes=[pltpu.SemaphoreType.DMA((n,)), pltpu.SemaphoreType.DMA((n,))],
    )(grad_t, *parts)


def _pair_sum(part, got, c, name):
    _, half, C = got.shape
    cb = min(C, 256)

    def body(c_ref, a_ref, b_ref, o_ref):
        o_ref[...] = (a_ref[...] + b_ref[...]).astype(BF16)

    spec = pl.BlockSpec((1, half, cb), lambda j, i, c_ref: (j, 0, i))
    grid_spec = pltpu.PrefetchScalarGridSpec(
        num_scalar_prefetch=1, grid=(N_CHIPS, C // cb),
        in_specs=[pl.BlockSpec((1, half, cb), lambda j, i, c_ref: (j, c_ref[0], i)), spec], out_specs=spec)
    return pl.pallas_call(
        body, name=name, grid_spec=grid_spec,
        out_shape=jax.ShapeDtypeStruct((N_CHIPS, half, C), BF16),
        compiler_params=_cparams(("parallel", "parallel")),
    )(c.reshape(1), part, got)


def _pair_sum_windows(grad_t, got, c):
    _, half, C = got.shape
    cb = 256

    def body(c_ref, a_ref, b_ref, o_ref):
        o_ref[0] = (a_ref[...] + b_ref[0]).astype(BF16)

    def mine(j, i, c_ref):
        return ((j * (WINDOW_STEP // SUBLANES) + c_ref[0] * (half // SUBLANES)) * SUBLANES, i * cb)

    spec = pl.BlockSpec((1, half, cb), lambda j, i, c_ref: (j, 0, i))
    grid_spec = pltpu.PrefetchScalarGridSpec(
        num_scalar_prefetch=1, grid=(N_CHIPS, C // cb),
        in_specs=[pl.BlockSpec((pl.Element(half), pl.Element(cb)), mine), spec], out_specs=spec)
    return pl.pallas_call(
        body, name="pair_sum_w_in", grid_spec=grid_spec,
        out_shape=jax.ShapeDtypeStruct((N_CHIPS, half, C), BF16),
        compiler_params=_cparams(("parallel", "parallel")),
    )(c.reshape(1), grad_t, got)


def _chip_exchange(sums):
    na = len(sums)

    def body(*refs):
        srcs, dsts = refs[:na], refs[na:2 * na]
        send_sems, recv_sems = refs[2 * na:]
        x, y, c = _position()
        chip = 2 * x + y
        copies = []
        for j, (px, py) in enumerate(_other_chips(x, y)):
            for a in range(na):
                copies.append(pltpu.make_async_remote_copy(
                    src_ref=srcs[a].at[2 * px + py], dst_ref=dsts[a].at[chip], send_sem=send_sems.at[j * na + a],
                    recv_sem=recv_sems.at[j * na + a], device_id=(px, py, c), device_id_type=MESH))
        for cp in copies:
            cp.start()
        for cp in copies:
            cp.wait()

    return pl.pallas_call(
        body, name="chip_exchange",
        in_specs=[HBM_SPEC] * na, out_specs=[HBM_SPEC] * na,
        out_shape=[jax.ShapeDtypeStruct(s.shape, s.dtype) for s in sums],
        scratch_shapes=[pltpu.SemaphoreType.DMA((3 * na,)), pltpu.SemaphoreType.DMA((3 * na,))],
    )(*sums)


def _chip_sum(own, got, chip, name):
    _, half, C = got.shape
    cb = min(C, 256)

    def body(chip_ref, own_ref, g_ref, o_ref):
        for me in range(N_CHIPS):
            @pl.when(chip_ref[0] == me)
            def _(me=me):
                terms = [own_ref[0] if k == me else g_ref[k] for k in range(N_CHIPS)]
                acc = terms[0].astype(F32) + terms[1].astype(F32)
                acc = acc + terms[2].astype(F32)
                o_ref[...] = acc + terms[3].astype(F32)

    grid_spec = pltpu.PrefetchScalarGridSpec(
        num_scalar_prefetch=1, grid=(C // cb,),
        in_specs=[pl.BlockSpec((1, half, cb), lambda i, chip_ref: (chip_ref[0], 0, i)),
                  pl.BlockSpec((N_CHIPS, half, cb), lambda i, chip_ref: (0, 0, i))],
        out_specs=pl.BlockSpec((half, cb), lambda i, chip_ref: (0, i)))
    return pl.pallas_call(
        body, name=name, grid_spec=grid_spec,
        out_shape=jax.ShapeDtypeStruct((half, C), F32),
        compiler_params=_cparams(("parallel",)),
    )(chip.reshape(1), own, got)


def _pair_swap(halves):
    na = len(halves)

    def body(*refs):
        srcs, dsts = refs[:na], refs[na:2 * na]
        send_sems, recv_sems = refs[2 * na:]
        x, y, c = _position()
        copies = [pltpu.make_async_remote_copy(
            src_ref=srcs[a], dst_ref=dsts[a], send_sem=send_sems.at[a], recv_sem=recv_sems.at[a],
            device_id=(x, y, 1 - c), device_id_type=MESH) for a in range(na)]
        for cp in copies:
            cp.start()
        for cp in copies:
            cp.wait()

    return pl.pallas_call(
        body, name="pair_swap",
        in_specs=[HBM_SPEC] * na, out_specs=[HBM_SPEC] * na,
        out_shape=[jax.ShapeDtypeStruct(s.shape, s.dtype) for s in halves],
        scratch_shapes=[pltpu.SemaphoreType.DMA((na,)), pltpu.SemaphoreType.DMA((na,))],
    )(*halves)


def _allreduce_small(g):
    rows = g.shape[0]
    per = rows // N_DEV

    def body(g_ref, out_ref, got_ref, s1, r1, s2, r2):
        x, y, c = _position()
        me = 4 * x + 2 * y + c
        mine = pl.ds(pl.multiple_of(me * per, SUBLANES), per)
        peers = []
        for j in range(1, N_DEV):
            px = 1 - x if j & 4 else x
            py = 1 - y if j & 2 else y
            pc = 1 - c if j & 1 else c
            peers.append((px, py, pc))

        first = []
        for j, (px, py, pc) in enumerate(peers):
            theirs = pl.ds(pl.multiple_of((4 * px + 2 * py + pc) * per, SUBLANES), per)
            first.append(pltpu.make_async_remote_copy(
                src_ref=g_ref.at[theirs, :], dst_ref=got_ref.at[me], send_sem=s1.at[j], recv_sem=r1.at[j],
                device_id=(px, py, pc), device_id_type=MESH))
        for cp in first:
            cp.start()
        got_ref[me] = g_ref[mine, :]
        for cp in first:
            cp.wait()
        total = got_ref[0]
        for d in range(1, N_DEV):
            total = total + got_ref[d]
        out_ref[mine, :] = total

        second = []
        for j, peer in enumerate(peers):
            second.append(pltpu.make_async_remote_copy(
                src_ref=out_ref.at[mine, :], dst_ref=out_ref.at[mine, :], send_sem=s2.at[j], recv_sem=r2.at[j],
                device_id=peer, device_id_type=MESH))
        for cp in second:
            cp.start()
        for cp in second:
            cp.wait()

    sems = pltpu.SemaphoreType.DMA((N_DEV - 1,))
    return pl.pallas_call(
        body, name="allreduce_small",
        in_specs=[VMEM_SPEC], out_specs=VMEM_SPEC,
        out_shape=jax.ShapeDtypeStruct(g.shape, F32),
        scratch_shapes=[pltpu.VMEM((N_DEV, per, LANES), F32), sems, sems, sems, sems],
    )(g)


def _adamw_math(g, w, m, v):
    m2 = ADAM_B1 * m + (1.0 - ADAM_B1) * g
    v2 = ADAM_B2 * v + (1.0 - ADAM_B2) * (g * g)
    m_hat = m2 / (1.0 - ADAM_B1 ** ADAM_STEP)
    v_hat = v2 / (1.0 - ADAM_B2 ** ADAM_STEP)
    delta = (-ADAM_LR) * (m_hat / (jnp.sqrt(v_hat) + ADAM_EPS) + ADAM_WD * w)
    return delta, m2, v2


def _adamw_big(g, w, m, v, name):
    R, C = g.shape
    cb = min(C, LANES)

    def body(g_ref, w_ref, m_ref, v_ref, d_ref, m2_ref, v2_ref):
        d_ref[...], m2_ref[...], v2_ref[...] = _adamw_math(g_ref[...], w_ref[...], m_ref[...], v_ref[...])

    spec = pl.BlockSpec((R, cb), lambda i: (0, i))
    out = jax.ShapeDtypeStruct((R, C), F32)
    return pl.pallas_call(
        body, name=name, grid=(C // cb,),
        in_specs=[spec] * 4, out_specs=[spec] * 3, out_shape=[out] * 3,
        compiler_params=_cparams(("parallel",)),
    )(g, w, m, v)


def _adamw_small(gs, ws, ms, vs):
    n = len(gs)

    def body(*refs):
        for a in range(n):
            g_ref, w_ref, m_ref, v_ref = (refs[k * n + a] for k in range(4))
            d_ref, m2_ref, v2_ref = (refs[(4 + k) * n + a] for k in range(3))
            d_ref[...], m2_ref[...], v2_ref[...] = _adamw_math(g_ref[...], w_ref[...], m_ref[...], v_ref[...])

    outs = [jax.ShapeDtypeStruct(w.shape, F32) for w in ws]
    return pl.pallas_call(
        body, name="adamw_small",
        in_specs=[VMEM_SPEC] * (4 * n), out_specs=[VMEM_SPEC] * (3 * n), out_shape=outs * 3,
    )(*gs, *ws, *ms, *vs)


def _local_step(x, p, tgt, w_a, w_f, w_b, w_out_b, w_ple_b, w_gate_b, conv_w, b_f, pre_gain, post_gain, conv_b,
                w_rgate, b_rgate, w_igate, b_igate, lam, gain_a, gain_l, ple_gain, b_gate):
    b_f_pad = jnp.pad(b_f, ((0, 0), (0, LANES - H)))
    w_r = w_rgate.astype(BF16)
    w_i = w_igate.astype(BF16)

    xn, q_aug, k_aug, v_aug, g_attn, x_lru, g_lru, flb = _in_proj(x, pre_gain, w_a, w_f, w_b, b_f_pad)
    o, qx = _attn_fwd(q_aug, k_aug, v_aug)
    ycat, xc, h = _branches_fwd(o, g_attn, x_lru, g_lru, gain_a, gain_l, conv_w, conv_b, w_r, b_rgate, w_i, b_igate,
                                lam)
    dh1, dycat, dmix, h1b, dgp, pb, dpe, acc_t = _tail(ycat, x, p, tgt, w_out_b, post_gain, w_ple_b, ple_gain,
                                                       w_gate_b, b_gate)
    do_aug, dg_attn, dg_lru, dh, acc_b = _branches_bwd(dycat, o, g_attn, h, g_lru, gain_a, gain_l)
    dx_lru, gw_r, gw_i, acc_l = _lru_bwd(dh, h, xc, x_lru, conv_w, w_r, b_rgate, w_i, b_igate, lam)
    dq, dk, dv, dc_key, dc_query = _attn_bwd(q_aug, qx, k_aug, v_aug, do_aug)
    dfl, acc_f = _fgate_bwd(dc_key, dc_query, flb)
    dz = (dq, dk, dv, dg_attn, dx_lru, dg_lru)
    grad_x, acc_x = _dx(dz, dfl, w_a, w_f, w_b, x, pre_gain, dh1)

    grads = dict(
        w_in_t=_dw_in_t(dz, dfl, xn),
        w_out=_matmul_tn(ycat, dmix, "dw_out"),
        w_ple=_matmul_tn(pb, dpe, "dw_ple"),
        w_ple_gate=_matmul_tn(h1b, dgp, "dw_ple_gate"),
        w_rgate=gw_r,
        w_igate=gw_i,
        b_f=acc_f[0:1, :H],
        pre_gain=acc_x[0:1],
        post_gain=acc_t[0:1],
        conv_w=acc_l[0:4],
        conv_b=acc_l[4:5],
        b_rgate=acc_l[5:6],
        b_igate=acc_l[6:7],
        lru_lambda=acc_l[7:8],
        attn_out_gain=acc_b[0:1],
        lru_out_gain=acc_b[1:2],
        ple_gain=acc_t[1:2],
        b_ple_gate=acc_t[2:3],
    )
    loss = jnp.sum(acc_t[3])
    return loss, grad_x, grads


SMALL_ROWS = ["b_f", "pre_gain", "post_gain", "conv_w", "conv_b", "b_rgate", "b_igate", "lru_lambda",
              "attn_out_gain", "lru_out_gain", "ple_gain", "b_ple_gate"]
WEIGHTS = ["w_in", "b_f", "pre_gain", "post_gain", "conv_w", "conv_b", "w_rgate", "b_rgate", "w_igate", "b_igate",
           "lru_lambda", "attn_out_gain", "lru_out_gain", "w_out", "w_ple", "ple_gain", "w_ple_gate", "b_ple_gate"]
SHARDED = ["w_in", "w_out", "w_ple", "w_ple_gate"]


def _by_chip_cols(g):
    r, cols = g.shape
    return g.reshape(r, N_CHIPS, cols // N_CHIPS).transpose(1, 0, 2)


def _from_chip_cols(s):
    n, r, cols = s.shape
    return s.transpose(1, 0, 2).reshape(r, n * cols)


def kernel(x, p, w_in, b_f, pre_gain, post_gain, conv_w, conv_b, w_rgate, b_rgate, w_igate, b_igate, lru_lambda, attn_out_gain, lru_out_gain, w_out, w_ple, ple_gain, w_ple_gate, b_ple_gate, loss_target, m_w_in, m_b_f, m_pre_gain, m_post_gain, m_conv_w, m_conv_b, m_w_rgate, m_b_rgate, m_w_igate, m_b_igate, m_lru_lambda, m_attn_out_gain, m_lru_out_gain, m_w_out, m_w_ple, m_ple_gain, m_w_ple_gate, m_b_ple_gate, v_w_in, v_b_f, v_pre_gain, v_post_gain, v_conv_w, v_conv_b, v_w_rgate, v_b_rgate, v_w_igate, v_b_igate, v_lru_lambda, v_attn_out_gain, v_lru_out_gain, v_w_out, v_w_ple, v_ple_gain, v_w_ple_gate, v_b_ple_gate):
    w = dict(w_in=w_in, b_f=b_f, pre_gain=pre_gain, post_gain=post_gain, conv_w=conv_w, conv_b=conv_b,
             w_rgate=w_rgate, b_rgate=b_rgate, w_igate=w_igate, b_igate=b_igate, lru_lambda=lru_lambda,
             attn_out_gain=attn_out_gain, lru_out_gain=lru_out_gain, w_out=w_out, w_ple=w_ple, ple_gain=ple_gain,
             w_ple_gate=w_ple_gate, b_ple_gate=b_ple_gate)
    m = dict(w_in=m_w_in, b_f=m_b_f, pre_gain=m_pre_gain, post_gain=m_post_gain, conv_w=m_conv_w, conv_b=m_conv_b,
             w_rgate=m_w_rgate, b_rgate=m_b_rgate, w_igate=m_w_igate, b_igate=m_b_igate, lru_lambda=m_lru_lambda,
             attn_out_gain=m_attn_out_gain, lru_out_gain=m_lru_out_gain, w_out=m_w_out, w_ple=m_w_ple,
             ple_gain=m_ple_gain, w_ple_gate=m_w_ple_gate, b_ple_gate=m_b_ple_gate)
    v = dict(w_in=v_w_in, b_f=v_b_f, pre_gain=v_pre_gain, post_gain=v_post_gain, conv_w=v_conv_w, conv_b=v_conv_b,
             w_rgate=v_w_rgate, b_rgate=v_b_rgate, w_igate=v_w_igate, b_igate=v_b_igate, lru_lambda=v_lru_lambda,
             attn_out_gain=v_attn_out_gain, lru_out_gain=v_lru_out_gain, w_out=v_w_out, w_ple=v_w_ple,
             ple_gain=v_ple_gain, w_ple_gate=v_w_ple_gate, b_ple_gate=v_b_ple_gate)
    xi, yi, ci = _position()
    chip = 2 * xi + yi

    w_in_t, m_in_t, v_in_t = (jnp.swapaxes(t[0], 0, 1) for t in (w_in, m_w_in, v_w_in))
    window = jnp.pad(w_in_t.astype(BF16), ((0, W_ROWS - SHARD_ROWS), (0, 0)))

    st_in, st_out, st_ple, st_gate, st_conv = _gather_shards(
        [window, w_out[0].astype(BF16), w_ple[0].astype(BF16), w_ple_gate[0].astype(BF16)], [conv_w[0]])
    w_a, w_f, w_b = _assemble_w_in(st_in)
    w_out_b = st_out.reshape(DMIX, D)
    w_ple_b = _from_chip_cols(st_ple)
    w_gate_b = st_gate.reshape(D, D)
    conv_full = _from_chip_cols(st_conv)

    loss, grad_x, g = _local_step(
        x[0], p[0, 0], loss_target[0], w_a, w_f, w_b, w_out_b, w_ple_b, w_gate_b, conv_full, b_f, pre_gain, post_gain,
        conv_b, w_rgate[0], b_rgate, w_igate[0], b_igate, lru_lambda, attn_out_gain, lru_out_gain, ple_gain,
        b_ple_gate)
    loss = lax.psum(loss, ("x", "y", "c"))

    parts = [g["w_out"].reshape(N_CHIPS, DMIX // N_CHIPS, D), _by_chip_cols(g["w_ple"]),
             g["w_ple_gate"].reshape(N_CHIPS, D // N_CHIPS, D)]
    got = _pair_exchange(g["w_in_t"], parts)
    sums = [_pair_sum_windows(g["w_in_t"], got[0], ci)]
    sums += [_pair_sum(parts[a], got[1 + a], ci, "pair_sum_%d" % a) for a in range(3)]
    recv = _chip_exchange(sums)
    halves = [_chip_sum(sums[a], recv[a], chip, "chip_sum_%d" % a) for a in range(4)]
    theirs = _pair_swap(halves)
    full = [jnp.concatenate([jnp.where(ci == 0, a, b), jnp.where(ci == 0, b, a)], axis=0)
            for a, b in zip(halves, theirs)]
    red = dict(zip(SHARDED, full))
    red["w_in"] = lax.dynamic_slice_in_dim(red["w_in"], 2 * chip, SHARD_ROWS, axis=0)

    rows = [jnp.pad(g["b_f"], ((0, 0), (0, D - H)))] + [g[n] for n in SMALL_ROWS[1:]]
    rows.append(jnp.zeros((16 - sum(r.shape[0] for r in rows), D), F32))
    packed = jnp.concatenate([g["w_rgate"].reshape(NB * LANES, LANES), g["w_igate"].reshape(NB * LANES, LANES),
                              jnp.concatenate(rows, axis=0).reshape(LANES, LANES)], axis=0)
    summed = _allreduce_small(packed)
    red["w_rgate"] = summed[:D].reshape(1, NB, LANES, LANES)
    red["w_igate"] = summed[D:2 * D].reshape(1, NB, LANES, LANES)
    vec = summed[2 * D:].reshape(16, D)
    r0 = 0
    for n in SMALL_ROWS:
        nr = 4 if n == "conv_w" else 1
        red[n] = vec[r0:r0 + nr]
        r0 += nr
    red["b_f"] = red["b_f"][:, :H]
    red["conv_w"] = lax.dynamic_slice_in_dim(red["conv_w"], chip * (D // N_CHIPS), D // N_CHIPS, axis=1)[None]

    delta, new_m, new_v = {}, {}, {}
    outs_in = _adamw_big(red["w_in"], w_in_t, m_in_t, v_in_t, "adamw_w_in")
    delta["w_in"], new_m["w_in"], new_v["w_in"] = (jnp.swapaxes(t, 0, 1)[None] for t in outs_in)
    red["w_in"] = jnp.swapaxes(red["w_in"], 0, 1)[None]
    for n in SHARDED[1:]:
        delta[n], new_m[n], new_v[n] = (t[None] for t in _adamw_big(red[n], w[n][0], m[n][0], v[n][0], "adamw_" + n))
        red[n] = red[n][None]
    small = [n for n in WEIGHTS if n not in SHARDED]
    outs = _adamw_small([red[n] for n in small], [w[n] for n in small], [m[n] for n in small],
                        [v[n] for n in small])
    ns = len(small)
    for a, n in enumerate(small):
        delta[n], new_m[n], new_v[n] = outs[a], outs[ns + a], outs[2 * ns + a]

    return (loss, grad_x[None], *[red[n] for n in WEIGHTS], *[delta[n] for n in WEIGHTS],
            *[new_m[n] for n in WEIGHTS], *[new_v[n] for n in WEIGHTS])
```

```python
import functools

import jax
import jax.numpy as jnp
import numpy as np
from jax import lax
from jax.experimental import pallas as pl
from jax.experimental.pallas import tpu as pltpu

F32 = jnp.float32
BF16 = jnp.bfloat16

D = 1024
H = 8
DH = 128
NB = 8
DPLE = 256
DMIX = 2 * D
D_IN = 4 * D + H + 2 * D
FL0 = 3 * D
RMS_EPS = 1e-6
LRU_C = 8.0
NEG = -1e30
LANES = 128
SUBLANES = 8

ADAM_LR = 0.001
ADAM_B1 = 0.9
ADAM_B2 = 0.999
ADAM_EPS = 1e-08
ADAM_WD = 0.01
ADAM_STEP = 10

TM = 256
TA = 512
FWD_HEADS = 4
BWD_HEADS = 2
VMEM_BIG = 56 * 1024 * 1024
VMEM_MID = 40 * 1024 * 1024

MESH = pl.DeviceIdType.MESH
N_CHIPS = 4
N_DEV = 8


def _cparams(sem, vmem=VMEM_MID):
    return pltpu.CompilerParams(dimension_semantics=sem, vmem_limit_bytes=vmem)


def _sigmoid(x):
    return 0.5 * jnp.tanh(0.5 * x) + 0.5


def _rstd(x):
    return lax.rsqrt(jnp.mean(x * x, axis=-1, keepdims=True) + RMS_EPS)


def _rms_bwd(t, xhat, rstd):
    return rstd * (t - xhat * jnp.mean(t * xhat, axis=-1, keepdims=True))


def _dot(a, b):
    return jnp.dot(a, b, preferred_element_type=F32)


def _dot_nt(a, b):
    return lax.dot_general(a, b, (((1,), (1,)), ((), ())), preferred_element_type=F32)


def _dot_tn(a, b):
    return lax.dot_general(a, b, (((0,), (0,)), ((), ())), preferred_element_type=F32)


def _dot_exact(a, b):
    return jnp.dot(a, b, preferred_element_type=F32, precision=lax.Precision.HIGHEST)


def _shift_down(x, j, halo):
    rolled = pltpu.roll(x, j, 0)
    row = lax.broadcasted_iota(jnp.int32, halo.shape, 0)
    top = jnp.where(row < j, pltpu.roll(halo, j, 0), rolled[:SUBLANES])
    return jnp.concatenate([top, rolled[SUBLANES:]], axis=0)


def _shift_up(x, j, nxt):
    tm = x.shape[0]
    rolled = pltpu.roll(x, tm - j, 0)
    row = lax.broadcasted_iota(jnp.int32, nxt.shape, 0)
    bot = jnp.where(row >= SUBLANES - j, pltpu.roll(nxt, SUBLANES - j, 0), rolled[tm - SUBLANES:])
    return jnp.concatenate([rolled[:tm - SUBLANES], bot], axis=0)


def _scan_fwd_into(a, u, carry, h_ref):
    tm = a.shape[0]
    sub = lax.broadcasted_iota(jnp.int32, a.shape, 0) & (SUBLANES - 1)
    d = 1
    while d < SUBLANES:
        keep = sub >= d
        a_s = jnp.where(keep, pltpu.roll(a, d, 0), 1.0)
        u_s = jnp.where(keep, pltpu.roll(u, d, 0), 0.0)
        u = u + a * u_s
        a = a * a_s
        d *= 2
    for g in range(tm // SUBLANES):
        rows = slice(g * SUBLANES, (g + 1) * SUBLANES)
        h_ref[rows, :] = u[rows] + a[rows] * carry
        carry = h_ref[(g + 1) * SUBLANES - 1:(g + 1) * SUBLANES, :]
    return carry


def _scan_bwd_into(b, u, g_ref):
    tm = b.shape[0]
    sub = lax.broadcasted_iota(jnp.int32, b.shape, 0) & (SUBLANES - 1)
    d = 1
    while d < SUBLANES:
        keep = sub < SUBLANES - d
        b_s = jnp.where(keep, pltpu.roll(b, tm - d, 0), 1.0)
        u_s = jnp.where(keep, pltpu.roll(u, tm - d, 0), 0.0)
        u = u + b * u_s
        b = b * b_s
        d *= 2
    nxt = jnp.zeros((1, b.shape[1]), F32)
    for g in reversed(range(tm // SUBLANES)):
        rows = slice(g * SUBLANES, (g + 1) * SUBLANES)
        g_ref[rows, :] = u[rows] + b[rows] * nxt
        nxt = g_ref[g * SUBLANES:g * SUBLANES + 1, :]


def _gate_pre(xc, w_ref):
    outs = []
    for n in range(NB):
        outs.append(_dot(xc[:, n * LANES:(n + 1) * LANES].astype(BF16), w_ref[n]))
    return jnp.concatenate(outs, axis=1)


def _gate_pre_t(d, w_ref):
    outs = []
    for n in range(NB):
        outs.append(_dot_nt(d[:, n * LANES:(n + 1) * LANES].astype(BF16), w_ref[n]))
    return jnp.concatenate(outs, axis=1)


def _softplus_neg(lam):
    return jnp.maximum(-lam, 0.0) + jnp.log(1.0 + jnp.exp(-jnp.abs(lam)))


def _row_spec(tm, width):
    return pl.BlockSpec((tm, width), lambda i: (i, 0))


def _const_spec(shape):
    nd = len(shape)
    return pl.BlockSpec(shape, lambda *_: (0,) * nd)


AUG = 2 * DH
LOG2E = 1.4426950408889634
LN2 = 0.6931471805599453
Q_SCALE = DH ** -0.5 * LOG2E


def _split3(x):
    hi = x.astype(BF16)
    r1 = x - hi.astype(F32)
    mid = r1.astype(BF16)
    lo = (r1 - mid.astype(F32)).astype(BF16)
    return hi, mid, lo


def _extras(col, ones_from):
    t = col.shape[0]
    hi, mid, lo = _split3(jnp.broadcast_to(col, (t, LANES)))
    lane = lax.broadcasted_iota(jnp.int32, (t, LANES), 1)
    rest = jnp.zeros((t, LANES), BF16)
    if ones_from is not None:
        rest = jnp.where((lane >= ones_from) & (lane < ones_from + 3), 1.0, 0.0).astype(BF16)
    return jnp.where(lane == 0, hi, jnp.where(lane == 1, mid, jnp.where(lane == 2, lo, rest)))


def _selectors():
    sel_q = np.zeros((3 * LANES, H * LANES), np.float32)
    sel_k = np.zeros((3 * LANES, H * LANES), np.float32)
    for hd in range(H):
        for piece in range(3):
            sel_q[piece * LANES + hd, hd * LANES + piece] = 1.0
            sel_k[piece * LANES + hd, hd * LANES + 3 + piece] = -1.0
    return jnp.asarray(sel_q, BF16), jnp.asarray(sel_k, BF16)


def _in_proj(x, pre_gain, w_a, w_f, w_b, b_f_pad):
    T = x.shape[0]
    tm = TM
    sel_q, sel_k = _selectors()

    def body(x_ref, g_ref, wa_ref, wf_ref, wb_ref, bf_ref, sq_ref, sk_ref,
             xn_ref, qa_ref, ka_ref, va_ref, ga_ref, xl_ref, gl_ref, flb_ref, c_s, carry):
        @pl.when(pl.program_id(0) == 0)
        def _():
            carry[...] = jnp.zeros_like(carry)

        xv = x_ref[...]
        xn = (xv * _rstd(xv) * g_ref[...]).astype(BF16)
        xn_ref[...] = xn
        for s, o_ref in enumerate((ga_ref, xl_ref, gl_ref)):
            o_ref[...] = _dot_nt(xn, wb_ref[s * D:(s + 1) * D, :]).astype(o_ref.dtype)
        flb = _dot_nt(xn, wf_ref[...]) + bf_ref[...]
        flb_ref[...] = flb
        lane = lax.broadcasted_iota(jnp.int32, flb.shape, 1)
        ls = jnp.where(lane < H, jnp.minimum(flb, 0.0) - jnp.log(1.0 + jnp.exp(-jnp.abs(flb))), 0.0)
        r = lax.broadcasted_iota(jnp.int32, (tm, tm), 0)
        c = lax.broadcasted_iota(jnp.int32, (tm, tm), 1)
        cs = _dot_exact((c <= r).astype(F32), ls) + carry[...]
        c_s[...] = cs
        carry[...] = c_s[tm - 1:tm, :]

        pieces = jnp.concatenate(_split3(cs * LOG2E), axis=1)
        ones_q = jnp.where((lane >= 3) & (lane < 6), 1.0, 0.0)
        ones_k = jnp.where(lane < 3, 1.0, 0.0)
        zq = _dot_nt(xn, wa_ref[0:D, :]) * Q_SCALE
        zk = _dot_nt(xn, wa_ref[D:2 * D, :])
        zv = _dot_nt(xn, wa_ref[2 * D:3 * D, :])
        ex_q = _dot(pieces, sq_ref[...])
        ex_k = _dot(pieces, sk_ref[...])
        for hd in range(H):
            head = slice(hd * DH, (hd + 1) * DH)
            lo, hi = hd * AUG, hd * AUG + DH
            qa_ref[:, lo:hi] = zq[:, head].astype(BF16)
            qa_ref[:, hi:hi + DH] = (ex_q[:, head] + ones_q).astype(BF16)
            ka_ref[:, lo:hi] = zk[:, head].astype(BF16)
            ka_ref[:, hi:hi + DH] = (ex_k[:, head] + ones_k).astype(BF16)
            va_ref[:, lo:hi] = zv[:, head].astype(BF16)
            va_ref[:, hi:hi + DH] = ones_k.astype(BF16)

    bf = jax.ShapeDtypeStruct((T, D), BF16)
    aug = jax.ShapeDtypeStruct((T, H * AUG), BF16)
    f32 = jax.ShapeDtypeStruct((T, D), F32)
    sel_spec = _const_spec((3 * LANES, H * LANES))
    return pl.pallas_call(
        body, name="in_proj", grid=(T // tm,),
        in_specs=[_row_spec(tm, D), _const_spec((1, D)), _const_spec((3 * D, D)), _const_spec((LANES, D)),
                  _const_spec((3 * D, D)), _const_spec((1, LANES)), sel_spec, sel_spec],
        out_specs=[_row_spec(tm, D)] + [_row_spec(tm, H * AUG)] * 3 + [_row_spec(tm, D)] * 3 + [_row_spec(tm, LANES)],
        out_shape=[bf, aug, aug, aug, f32, f32, f32, jax.ShapeDtypeStruct((T, LANES), F32)],
        scratch_shapes=[pltpu.VMEM((tm, LANES), F32), pltpu.VMEM((1, LANES), F32)],
        compiler_params=_cparams(("arbitrary",), VMEM_BIG),
    )(x, pre_gain, w_a, w_f, w_b, b_f_pad, sel_q, sel_k)


def _causal_pairs(n, q_major):
    if q_major:
        pairs = [(qi, ki) for qi in range(n) for ki in range(qi + 1)]
    else:
        pairs = [(ki, qi) for ki in range(n) for qi in range(ki, n)]
    return (jnp.asarray([a for a, _ in pairs], jnp.int32), jnp.asarray([b for _, b in pairs], jnp.int32))


def _attn_fwd(q_aug, k_aug, v_aug):
    T = q_aug.shape[0]
    t = TA
    n = T // t
    hp = FWD_HEADS
    heads = range(hp)
    qi_tab, ki_tab = _causal_pairs(n, q_major=True)

    def body(qi_ref, ki_ref, q_ref, k_ref, v_ref, o_ref, qx_ref, m_s, acc_s):
        j = pl.program_id(1)
        qi = qi_ref[j]
        ki = ki_ref[j]

        @pl.when(ki == 0)
        def _():
            m_s[...] = jnp.full(m_s.shape, NEG, F32)
            acc_s[...] = jnp.zeros_like(acc_s)

        def step(on_diagonal):
            cols = [slice(a * AUG, (a + 1) * AUG) for a in heads]
            s = [_dot_nt(q_ref[:, cols[a]], k_ref[:, cols[a]]) for a in heads]
            if on_diagonal:
                row = lax.broadcasted_iota(jnp.int32, (t, t), 0)
                col = lax.broadcasted_iota(jnp.int32, (t, t), 1)
                s = [jnp.where(col <= row, s[a], NEG) for a in heads]
            m_prev = [m_s[a] for a in heads]
            m_new = [jnp.maximum(m_prev[a], jnp.max(s[a], axis=1, keepdims=True)) for a in heads]
            pr = [jnp.exp2(s[a] - m_new[a]).astype(BF16) for a in heads]
            for a in heads:
                acc_s[a] = jnp.exp2(m_prev[a] - m_new[a]) * acc_s[a] + _dot(pr[a], v_ref[:, cols[a]])
                m_s[a] = m_new[a]

        @pl.when(ki < qi)
        def _():
            step(False)

        @pl.when(ki == qi)
        def _():
            step(True)
            for a in heads:
                acc = acc_s[a]
                l = acc[:, DH:DH + 1]
                o_ref[:, a * DH:(a + 1) * DH] = acc[:, :DH] / l
                ex = q_ref[:, a * AUG + DH:(a + 1) * AUG].astype(F32)
                c2 = ex[:, 0:1] + ex[:, 1:2] + ex[:, 2:3]
                qx_ref[:, a * DH:(a + 1) * DH] = _extras(c2 - (m_s[a] + jnp.log(l) * LOG2E), 3)

    q_spec = pl.BlockSpec((t, hp * AUG), lambda h, j, qi_ref, ki_ref: (qi_ref[j], h))
    kv_spec = pl.BlockSpec((t, hp * AUG), lambda h, j, qi_ref, ki_ref: (ki_ref[j], h))
    out_spec = pl.BlockSpec((t, hp * DH), lambda h, j, qi_ref, ki_ref: (qi_ref[j], h))
    grid_spec = pltpu.PrefetchScalarGridSpec(
        num_scalar_prefetch=2, grid=(H // hp, qi_tab.shape[0]),
        in_specs=[q_spec, kv_spec, kv_spec], out_specs=[out_spec, out_spec],
        scratch_shapes=[pltpu.VMEM((hp, t, 1), F32), pltpu.VMEM((hp, t, AUG), F32)])
    return pl.pallas_call(
        body, name="attn_fwd", grid_spec=grid_spec,
        out_shape=[jax.ShapeDtypeStruct((T, D), F32), jax.ShapeDtypeStruct((T, D), BF16)],
        compiler_params=_cparams(("parallel", "arbitrary"), VMEM_BIG),
    )(qi_tab, ki_tab, q_aug, k_aug, v_aug)


def _lru_gates(xc, wr_ref, br_ref, wi_ref, bi_ref, lam_ref):
    r = _sigmoid(_gate_pre(xc, wr_ref) + br_ref[...])
    ig = _sigmoid(_gate_pre(xc, wi_ref) + bi_ref[...])
    sp = _softplus_neg(lam_ref[...])
    la = (-LRU_C) * r * sp
    a = jnp.exp(la)
    y = -jnp.tanh(la) * (a * a + 1.0)
    return r, ig, sp, a, jnp.sqrt(y), lax.rsqrt(y)


def _branches_fwd(o, g_attn, x_lru, g_lru, gain_a, gain_l, conv_w, conv_b, w_r, b_r, w_i, b_i, lam):
    T = o.shape[0]
    tm = TM

    def body(o_ref, ga_ref, xl_ref, gl_ref, gna_ref, gnl_ref, cw_ref, cb_ref, wr_ref, br_ref, wi_ref, bi_ref,
             lam_ref, ycat_ref, xc_ref, h_ref, halo_s, hc_s):
        @pl.when(pl.program_id(0) == 0)
        def _():
            halo_s[...] = jnp.zeros_like(halo_s)
            hc_s[...] = jnp.zeros_like(hc_s)

        ov = o_ref[...]
        ga = ga_ref[...]
        ya = ov * _rstd(ov) * gna_ref[...] * (ga * _sigmoid(ga))
        ycat_ref[:, :D] = ya.astype(BF16)

        xl = xl_ref[...]
        halo = halo_s[...]
        xc = xl * cw_ref[3:4, :] + cb_ref[...]
        for j in range(3):
            xc = xc + _shift_down(xl, 3 - j, halo) * cw_ref[j:j + 1, :]
        halo_s[...] = xl_ref[tm - SUBLANES:tm, :]
        xc_ref[...] = xc

        _, ig, _, a, sq, _ = _lru_gates(xc, wr_ref, br_ref, wi_ref, bi_ref, lam_ref)
        u = sq * (ig * xc)
        hc_s[...] = _scan_fwd_into(a, u, hc_s[...], h_ref)
        hh = h_ref[...]

        gl = gl_ref[...]
        yl = hh * _rstd(hh) * gnl_ref[...] * (gl * _sigmoid(gl))
        ycat_ref[:, D:] = yl.astype(BF16)

    vec = _const_spec((1, D))
    wspec = _const_spec((NB, LANES, LANES))
    return pl.pallas_call(
        body, name="branches_fwd", grid=(T // tm,),
        in_specs=[_row_spec(tm, D)] * 4 + [vec, vec, _const_spec((4, D)), vec, wspec, vec, wspec, vec, vec],
        out_specs=[_row_spec(tm, DMIX), _row_spec(tm, D), _row_spec(tm, D)],
        out_shape=[jax.ShapeDtypeStruct((T, DMIX), BF16), jax.ShapeDtypeStruct((T, D), F32),
                   jax.ShapeDtypeStruct((T, D), F32)],
        scratch_shapes=[pltpu.VMEM((SUBLANES, D), F32), pltpu.VMEM((1, D), F32)],
        compiler_params=_cparams(("arbitrary",)),
    )(o, g_attn, x_lru, g_lru, gain_a, gain_l, conv_w, conv_b, w_r, b_r, w_i, b_i, lam)


def _tail(ycat, x, p, tgt, w_out, post_gain, w_ple, ple_gain, w_gate, b_gate):
    T = x.shape[0]
    tm = TM

    def body(ycat_ref, x_ref, p_ref, t_ref, wo_ref, pg_ref, wp_ref, eg_ref, wg_ref, bg_ref,
             dh1_ref, dycat_ref, dmix_ref, h1b_ref, dgp_ref, pb_ref, dpe_ref, acc_ref):
        @pl.when(pl.program_id(0) == 0)
        def _():
            acc_ref[...] = jnp.zeros_like(acc_ref)

        mix = _dot(ycat_ref[...], wo_ref[...])
        rstd_m = _rstd(mix)
        mhat = mix * rstd_m
        h1 = x_ref[...] + mhat * pg_ref[...]
        pb = p_ref[...].astype(BF16)
        pb_ref[...] = pb
        pe = _dot(pb, wp_ref[...])
        rstd_p = _rstd(pe)
        pehat = pe * rstd_p
        e = pehat * eg_ref[...]
        h1b = h1.astype(BF16)
        h1b_ref[...] = h1b
        gate = _sigmoid(_dot(h1b, wg_ref[...]) + bg_ref[...])
        diff = (h1 + gate * e) - t_ref[...]

        dy = diff * (1.0 / D)
        de = dy * gate
        dgp = (dy * e) * gate * (1.0 - gate)
        dgpb = dgp.astype(BF16)
        dgp_ref[...] = dgpb
        dh1 = dy + _dot_nt(dgpb, wg_ref[...])
        dh1_ref[...] = dh1
        dpe_ref[...] = _rms_bwd(de * eg_ref[...], pehat, rstd_p).astype(BF16)
        dmix = _rms_bwd(dh1 * pg_ref[...], mhat, rstd_m).astype(BF16)
        dmix_ref[...] = dmix
        dycat_ref[...] = _dot_nt(dmix, wo_ref[...])

        acc_ref[0:1, :] += jnp.sum(dh1 * mhat, axis=0, keepdims=True)
        acc_ref[1:2, :] += jnp.sum(de * pehat, axis=0, keepdims=True)
        acc_ref[2:3, :] += jnp.sum(dgp, axis=0, keepdims=True)
        acc_ref[3:4, :] += jnp.sum(diff * diff, axis=0, keepdims=True) * (0.5 / D)

    vec = _const_spec((1, D))
    bf = jax.ShapeDtypeStruct((T, D), BF16)
    return pl.pallas_call(
        body, name="tail", grid=(T // tm,),
        in_specs=[_row_spec(tm, DMIX), _row_spec(tm, D), _row_spec(tm, DPLE), _row_spec(tm, D),
                  _const_spec((DMIX, D)), vec, _const_spec((DPLE, D)), vec, _const_spec((D, D)), vec],
        out_specs=[_row_spec(tm, D), _row_spec(tm, DMIX), _row_spec(tm, D), _row_spec(tm, D), _row_spec(tm, D),
                   _row_spec(tm, DPLE), _row_spec(tm, D), _const_spec((SUBLANES, D))],
        out_shape=[jax.ShapeDtypeStruct((T, D), F32), jax.ShapeDtypeStruct((T, DMIX), F32), bf, bf, bf,
                   jax.ShapeDtypeStruct((T, DPLE), BF16), bf, jax.ShapeDtypeStruct((SUBLANES, D), F32)],
        compiler_params=_cparams(("arbitrary",), VMEM_BIG),
    )(ycat, x, p, tgt, w_out, post_gain, w_ple, ple_gain, w_gate, b_gate)


def _branches_bwd(dycat, o, g_attn, h, g_lru, gain_a, gain_l):
    T = o.shape[0]
    tm = TM

    def body(dy_ref, o_ref, ga_ref, h_ref, gl_ref, gna_ref, gnl_ref,
             do_ref, dga_ref, dgl_ref, dh_ref, acc_ref):
        @pl.when(pl.program_id(0) == 0)
        def _():
            acc_ref[...] = jnp.zeros_like(acc_ref)

        def branch(val, g, gain, dyv):
            rstd = _rstd(val)
            vhat = val * rstd
            sig = _sigmoid(g)
            dn = dyv * (g * sig)
            dg = dyv * (vhat * gain) * (sig * (1.0 + g * (1.0 - sig)))
            dgain = jnp.sum(dn * vhat, axis=0, keepdims=True)
            return _rms_bwd(dn * gain, vhat, rstd), dg, dgain

        ov = o_ref[...]
        do, dga, dgain_a = branch(ov, ga_ref[...], gna_ref[...], dy_ref[:, :D])
        dga_ref[...] = dga.astype(BF16)
        prod = do * ov
        for hd in range(H):
            head = slice(hd * DH, (hd + 1) * DH)
            do_ref[:, hd * AUG:hd * AUG + DH] = do[:, head].astype(BF16)
            do_ref[:, hd * AUG + DH:(hd + 1) * AUG] = _extras(-jnp.sum(prod[:, head], axis=1, keepdims=True), None)

        dh, dgl, dgain_l = branch(h_ref[...], gl_ref[...], gnl_ref[...], dy_ref[:, D:])
        dh_ref[...] = dh
        dgl_ref[...] = dgl.astype(BF16)
        acc_ref[0:1, :] += dgain_a
        acc_ref[1:2, :] += dgain_l

    vec = _const_spec((1, D))
    bf = jax.ShapeDtypeStruct((T, D), BF16)
    return pl.pallas_call(
        body, name="branches_bwd", grid=(T // tm,),
        in_specs=[_row_spec(tm, DMIX)] + [_row_spec(tm, D)] * 4 + [vec, vec],
        out_specs=[_row_spec(tm, H * AUG), _row_spec(tm, D), _row_spec(tm, D), _row_spec(tm, D),
                   _const_spec((SUBLANES, D))],
        out_shape=[jax.ShapeDtypeStruct((T, H * AUG), BF16), bf, bf, jax.ShapeDtypeStruct((T, D), F32),
                   jax.ShapeDtypeStruct((SUBLANES, D), F32)],
        compiler_params=_cparams(("arbitrary",)),
    )(dycat, o, g_attn, h, g_lru, gain_a, gain_l)


def _lru_bwd(dh, h, xc, x_lru, conv_w, w_r, b_r, w_i, b_i, lam):
    T = dh.shape[0]
    tm = TM
    nt = T // tm
    per = tm // SUBLANES

    def body(dh_ref, h_ref, hprev_ref, xc_ref, xl_ref, xlprev_ref, cw_ref, wr_ref, br_ref, wi_ref, bi_ref, lam_ref,
             dxl_ref, dwr_ref, dwi_ref, acc_ref, carry_s, dxc_next_s, top_s, dht_s):
        i = pl.program_id(0)

        @pl.when(i == 0)
        def _():
            acc_ref[...] = jnp.zeros_like(acc_ref)
            dwr_ref[...] = jnp.zeros_like(dwr_ref)
            dwi_ref[...] = jnp.zeros_like(dwi_ref)
            carry_s[...] = jnp.zeros_like(carry_s)
            dxc_next_s[...] = jnp.zeros_like(dxc_next_s)

        inner = jnp.where(i == nt - 1, 0.0, 1.0)
        xc = xc_ref[...]
        r, ig, sp, a, sq, inv_sq = _lru_gates(xc, wr_ref, br_ref, wi_ref, bi_ref, lam_ref)

        row = lax.broadcasted_iota(jnp.int32, (tm, D), 0)
        u = dh_ref[...] + jnp.where(row == tm - 1, carry_s[...], 0.0)
        _scan_bwd_into(pltpu.roll(a, tm - 1, 0), u, dht_s)
        dht = dht_s[...]
        top_s[...] = a[:SUBLANES, :] * dht[:SUBLANES, :]
        carry_s[...] = top_s[0:1, :]

        hprev = hprev_ref[...] * inner
        da = dht * _shift_down(h_ref[...], 1, hprev)
        dig = dht * sq * xc
        dxc = dht * sq * ig
        dsq = dht * ig * xc
        dla = da * a - dsq * (a * a) * inv_sq
        dr = dla * ((-LRU_C) * sp)
        dpr = dr * r * (1.0 - r)
        dpi = dig * ig * (1.0 - ig)
        for n in range(NB):
            blk = slice(n * LANES, (n + 1) * LANES)
            xcb = xc[:, blk].astype(BF16)
            dwr_ref[n] += _dot_tn(xcb, dpr[:, blk].astype(BF16))
            dwi_ref[n] += _dot_tn(xcb, dpi[:, blk].astype(BF16))
        dxc = dxc + _gate_pre_t(dpr, wr_ref) + _gate_pre_t(dpi, wi_ref)

        xl = xl_ref[...]
        xlprev = xlprev_ref[...] * inner
        nxt = dxc_next_s[...]
        dxl = dxc * cw_ref[3:4, :]
        acc_ref[3:4, :] += jnp.sum(dxc * xl, axis=0, keepdims=True)
        for j in range(3):
            dxl = dxl + _shift_up(dxc, 3 - j, nxt) * cw_ref[j:j + 1, :]
            acc_ref[j:j + 1, :] += jnp.sum(dxc * _shift_down(xl, 3 - j, xlprev), axis=0, keepdims=True)
        dxc_next_s[...] = dxc[:SUBLANES, :]
        dxl_ref[...] = dxl.astype(BF16)

        acc_ref[4:5, :] += jnp.sum(dxc, axis=0, keepdims=True)
        acc_ref[5:6, :] += jnp.sum(dpr, axis=0, keepdims=True)
        acc_ref[6:7, :] += jnp.sum(dpi, axis=0, keepdims=True)
        acc_ref[7:8, :] += jnp.sum(dla * ((-LRU_C) * r), axis=0, keepdims=True)

        @pl.when(i == nt - 1)
        def _():
            lam_v = lam_ref[...]
            acc_ref[7:8, :] = acc_ref[7:8, :] * (-_sigmoid(-lam_v))

    rev = pl.BlockSpec((tm, D), lambda i: (nt - 1 - i, 0))
    prev8 = pl.BlockSpec((SUBLANES, D), lambda i: (jnp.maximum((nt - 1 - i) * per - 1, 0), 0))
    vec = _const_spec((1, D))
    wspec = _const_spec((NB, LANES, LANES))
    bf = jax.ShapeDtypeStruct((T, D), BF16)
    return pl.pallas_call(
        body, name="lru_bwd", grid=(nt,),
        in_specs=[rev, rev, prev8, rev, rev, prev8, _const_spec((4, D)), wspec, vec, wspec, vec, vec],
        out_specs=[rev, wspec, wspec, _const_spec((SUBLANES, D))],
        out_shape=[bf, jax.ShapeDtypeStruct((NB, LANES, LANES), F32), jax.ShapeDtypeStruct((NB, LANES, LANES), F32),
                   jax.ShapeDtypeStruct((SUBLANES, D), F32)],
        scratch_shapes=[pltpu.VMEM((1, D), F32), pltpu.VMEM((SUBLANES, D), F32), pltpu.VMEM((SUBLANES, D), F32),
                        pltpu.VMEM((tm, D), F32)],
        compiler_params=_cparams(("arbitrary",)),
    )(dh, h, h, xc, x_lru, x_lru, conv_w, w_r, b_r, w_i, b_i, lam)


def _attn_bwd(q_aug, qx, k_aug, v_aug, do_aug):
    T = q_aug.shape[0]
    t = TA
    n = T // t
    hp = BWD_HEADS
    heads = range(hp)
    scale = DH ** -0.5
    ki_tab, qi_tab = _causal_pairs(n, q_major=False)
    last = ki_tab.shape[0] - 1

    def body(ki_ref, qi_ref, q_ref, qx_ref, k_ref, v_ref, do_ref, dq_ref, dk_ref, dv_ref, dck_ref, dcq_ref,
             dq_s, dk_s, dv_s):
        j = pl.program_id(1)
        ki = ki_ref[j]
        qi = qi_ref[j]

        @pl.when(j == 0)
        def _():
            dq_s[...] = jnp.zeros_like(dq_s)

        @pl.when(qi == ki)
        def _():
            dk_s[...] = jnp.zeros_like(dk_s)
            dv_s[...] = jnp.zeros_like(dv_s)

        def step(on_diagonal):
            cols = [slice(a * AUG, (a + 1) * AUG) for a in heads]
            qb = [jnp.concatenate([q_ref[:, a * AUG:a * AUG + DH], qx_ref[:, a * DH:(a + 1) * DH]], axis=1)
                  for a in heads]
            st = [_dot_nt(k_ref[:, cols[a]], qb[a]) for a in heads]
            if on_diagonal:
                krow = lax.broadcasted_iota(jnp.int32, (t, t), 0)
                qcol = lax.broadcasted_iota(jnp.int32, (t, t), 1)
                st = [jnp.where(krow <= qcol, st[a], NEG) for a in heads]
            pt = [jnp.exp2(st[a]) for a in heads]
            dsb = [(pt[a] * _dot_nt(v_ref[:, cols[a]], do_ref[:, cols[a]])).astype(BF16) for a in heads]
            off = pl.multiple_of(qi * t, t)
            for a in heads:
                dv_s[a] += _dot(pt[a].astype(BF16), do_ref[:, cols[a]])
                dk_s[a] += _dot(dsb[a], qb[a])
                dq_s[a, pl.ds(off, t), :] += _dot_tn(dsb[a], k_ref[:, cols[a]])

        @pl.when(qi > ki)
        def _():
            step(False)

        @pl.when(qi == ki)
        def _():
            step(True)

        @pl.when(qi == n - 1)
        def _():
            for a in heads:
                dk_ref[:, a * DH:(a + 1) * DH] = (dk_s[a, :, :DH] * LN2).astype(BF16)
                dv_ref[:, a * DH:(a + 1) * DH] = dv_s[a, :, :DH].astype(BF16)
                dck_ref[a] = jnp.broadcast_to(dk_s[a, :, DH + 3:DH + 4], (t, LANES))

        @pl.when(j == last)
        def _():
            for a in heads:
                dq_ref[:, a * DH:(a + 1) * DH] = (dq_s[a, :, :DH] * scale).astype(BF16)
                dcq_ref[a] = jnp.broadcast_to(dq_s[a, :, DH:DH + 1], (T, LANES))

    qside = pl.BlockSpec((t, hp * AUG), lambda h, j, ki_ref, qi_ref: (qi_ref[j], h))
    qxside = pl.BlockSpec((t, hp * DH), lambda h, j, ki_ref, qi_ref: (qi_ref[j], h))
    kside = pl.BlockSpec((t, hp * AUG), lambda h, j, ki_ref, qi_ref: (ki_ref[j], h))
    kout = pl.BlockSpec((t, hp * DH), lambda h, j, ki_ref, qi_ref: (ki_ref[j], h))
    bf = jax.ShapeDtypeStruct((T, D), BF16)
    sums = jax.ShapeDtypeStruct((H, T, LANES), F32)
    grid_spec = pltpu.PrefetchScalarGridSpec(
        num_scalar_prefetch=2, grid=(H // hp, ki_tab.shape[0]),
        in_specs=[qside, qxside, kside, kside, qside],
        out_specs=[pl.BlockSpec((T, hp * DH), lambda h, j, ki_ref, qi_ref: (0, h)), kout, kout,
                   pl.BlockSpec((hp, t, LANES), lambda h, j, ki_ref, qi_ref: (h, ki_ref[j], 0)),
                   pl.BlockSpec((hp, T, LANES), lambda h, j, ki_ref, qi_ref: (h, 0, 0))],
        scratch_shapes=[pltpu.VMEM((hp, T, AUG), F32), pltpu.VMEM((hp, t, AUG), F32), pltpu.VMEM((hp, t, AUG), F32)])
    return pl.pallas_call(
        body, name="attn_bwd", grid_spec=grid_spec,
        out_shape=[bf, bf, bf, sums, sums],
        compiler_params=_cparams(("arbitrary", "arbitrary"), VMEM_BIG),
    )(ki_tab, qi_tab, q_aug, qx, k_aug, v_aug, do_aug)


def _fgate_bwd(dc_key, dc_query, flb):
    T = flb.shape[0]
    tm = TM
    nt = T // tm

    def body(dck_ref, dcq_ref, flb_ref, dfl_ref, acc_ref, carry, top_s):
        @pl.when(pl.program_id(0) == 0)
        def _():
            carry[...] = jnp.zeros_like(carry)
            acc_ref[...] = jnp.zeros_like(acc_ref)

        flb = flb_ref[...]
        lane = lax.broadcasted_iota(jnp.int32, flb.shape, 1)
        dc = jnp.zeros(flb.shape, F32)
        for hd in range(H):
            dc = dc + jnp.where(lane == hd, dcq_ref[hd] - dck_ref[hd], 0.0)
        r = lax.broadcasted_iota(jnp.int32, (tm, tm), 0)
        c = lax.broadcasted_iota(jnp.int32, (tm, tm), 1)
        dls = _dot_exact((c >= r).astype(F32), dc) + carry[...]
        top_s[...] = dls[:SUBLANES, :]
        carry[...] = top_s[0:1, :]
        dfl = jnp.where(lane < H, dls * _sigmoid(-flb), 0.0)
        dfl_ref[...] = dfl.astype(BF16)
        acc_ref[0:1, :] += jnp.sum(dfl, axis=0, keepdims=True)

    rev = pl.BlockSpec((tm, LANES), lambda i: (nt - 1 - i, 0))
    return pl.pallas_call(
        body, name="fgate_bwd", grid=(nt,),
        in_specs=[pl.BlockSpec((H, tm, LANES), lambda i: (0, nt - 1 - i, 0))] * 2 + [rev],
        out_specs=[rev, _const_spec((SUBLANES, LANES))],
        out_shape=[jax.ShapeDtypeStruct((T, LANES), BF16), jax.ShapeDtypeStruct((SUBLANES, LANES), F32)],
        scratch_shapes=[pltpu.VMEM((1, LANES), F32), pltpu.VMEM((SUBLANES, LANES), F32)],
        compiler_params=_cparams(("arbitrary",)),
    )(dc_key, dc_query, flb)


def _dx(dz, dfl, w_a, w_f, w_b, x, pre_gain, dh1):
    T = x.shape[0]
    tm = TM

    def body(*refs):
        dz_refs = refs[:6]
        dfl_ref, wa_ref, wf_ref, wb_ref, x_ref, g_ref, dh1_ref, gx_ref, acc_ref = refs[6:]

        @pl.when(pl.program_id(0) == 0)
        def _():
            acc_ref[...] = jnp.zeros_like(acc_ref)

        dxn = _dot(dfl_ref[...], wf_ref[...])
        for s in range(3):
            dxn = dxn + _dot(dz_refs[s][...], wa_ref[s * D:(s + 1) * D, :])
            dxn = dxn + _dot(dz_refs[3 + s][...], wb_ref[s * D:(s + 1) * D, :])
        xv = x_ref[...]
        rstd = _rstd(xv)
        xhat = xv * rstd
        gx_ref[...] = dh1_ref[...] + _rms_bwd(dxn * g_ref[...], xhat, rstd)
        acc_ref[0:1, :] += jnp.sum(dxn * xhat, axis=0, keepdims=True)

    return pl.pallas_call(
        body, name="dx", grid=(T // tm,),
        in_specs=[_row_spec(tm, D)] * 6 + [_row_spec(tm, LANES), _const_spec((3 * D, D)), _const_spec((LANES, D)),
                                           _const_spec((3 * D, D)), _row_spec(tm, D), _const_spec((1, D)),
                                           _row_spec(tm, D)],
        out_specs=[_row_spec(tm, D), _const_spec((SUBLANES, D))],
        out_shape=[jax.ShapeDtypeStruct((T, D), F32), jax.ShapeDtypeStruct((SUBLANES, D), F32)],
        compiler_params=_cparams(("arbitrary",), VMEM_BIG),
    )(*dz, dfl, w_a, w_f, w_b, x, pre_gain, dh1)


GRAD_ROWS = D_IN + SUBLANES


def _dw_in_segment(dz_s, xn, buf, s, bt):
    T = xn.shape[0]
    row0 = s * D + (H if s >= 3 else 0)

    def body(*refs):
        dz_ref, xn_ref, o_ref = refs[0], refs[1], refs[-1]

        @pl.when(pl.program_id(0) == 0)
        def _():
            o_ref[...] = jnp.zeros_like(o_ref)

        o_ref[...] += _dot_tn(dz_ref[...], xn_ref[...])

    tok = pl.BlockSpec((bt, D), lambda t: (t, 0))
    return pl.pallas_call(
        body, name="dw_in_%d" % s, grid=(T // bt,),
        in_specs=[tok, tok] + ([] if buf is None else [pl.BlockSpec(memory_space=pl.ANY)]),
        out_specs=pl.BlockSpec((pl.Element(D), pl.Element(D)), lambda t: (row0, 0)),
        out_shape=jax.ShapeDtypeStruct((GRAD_ROWS, D), F32),
        input_output_aliases={} if buf is None else {2: 0},
        compiler_params=_cparams(("arbitrary",)),
    )(*((dz_s, xn) if buf is None else (dz_s, xn, buf)))


def _dw_in_t(dz, dfl, xn, bt=512):
    T = xn.shape[0]
    nt = T // bt
    main = None
    for s in range(6):
        main = _dw_in_segment(dz[s], xn, main, s, min(T, 2048))

    def f_body(dfl_ref, xn_ref, main_ref, o_ref, acc_s):
        p = pl.program_id(0)
        t = pl.program_id(1)

        @pl.when(t == 0)
        def _():
            acc_s[...] = jnp.zeros_like(acc_s)

        @pl.when(p == 0)
        def _():
            acc_s[...] += _dot_tn(dfl_ref[...], xn_ref[...])

        @pl.when(t == nt - 1)
        def _():
            o_ref[...] = acc_s[:SUBLANES, :]

    fl_block = FL0 // SUBLANES
    end_block = D_IN // SUBLANES
    return pl.pallas_call(
        f_body, name="dw_in_f", grid=(2, nt),
        in_specs=[pl.BlockSpec((bt, LANES), lambda p, t: (t, 0)), pl.BlockSpec((bt, D), lambda p, t: (t, 0)),
                  pl.BlockSpec(memory_space=pl.ANY)],
        out_specs=pl.BlockSpec((SUBLANES, D), lambda p, t: (fl_block + p * (end_block - fl_block), 0)),
        out_shape=jax.ShapeDtypeStruct((GRAD_ROWS, D), F32),
        scratch_shapes=[pltpu.VMEM((LANES, D), F32)],
        input_output_aliases={2: 0},
        compiler_params=_cparams(("arbitrary", "arbitrary")),
    )(dfl, xn, main)


def _matmul_tn(a, b, name, bm=512, bn=1024, bt=2048):
    T, M = a.shape
    N = b.shape[1]
    bm, bn, bt = min(bm, M), min(bn, N), min(bt, T)

    def body(a_ref, b_ref, o_ref):
        @pl.when(pl.program_id(2) == 0)
        def _():
            o_ref[...] = jnp.zeros_like(o_ref)

        o_ref[...] += _dot_tn(a_ref[...], b_ref[...])

    return pl.pallas_call(
        body, name=name, grid=(M // bm, N // bn, T // bt),
        in_specs=[pl.BlockSpec((bt, bm), lambda i, j, t: (t, i)), pl.BlockSpec((bt, bn), lambda i, j, t: (t, j))],
        out_specs=pl.BlockSpec((bm, bn), lambda i, j, t: (i, j)),
        out_shape=jax.ShapeDtypeStruct((M, N), F32),
        compiler_params=_cparams(("parallel", "parallel", "arbitrary")),
    )(a, b)


HBM_SPEC = pl.BlockSpec(memory_space=pltpu.HBM)
VMEM_SPEC = pl.BlockSpec(memory_space=pltpu.VMEM)


def _position():
    return lax.axis_index("x"), lax.axis_index("y"), lax.axis_index("c")


def _other_chips(x, y):
    return [(1 - x, y), (x, 1 - y), (1 - x, 1 - y)]


def _gather_shards(shards, whole):
    na, nw = len(shards), len(whole)
    nall = na + nw

    def body(*refs):
        srcs, dsts = refs[:nall], refs[nall:2 * nall]
        ici_send, ici_recv, d2d_send, d2d_recv = refs[2 * nall:]
        x, y, c = _position()
        chip = 2 * x + y
        chips = _other_chips(x, y)

        def half(a, which):
            rows = srcs[a].shape[0] // 2
            return pl.ds(pl.multiple_of(which * rows, 16), rows)

        first = []
        for j, (px, py) in enumerate(chips):
            for a in range(nall):
                src = srcs[a].at[half(a, c), :] if a < na else srcs[a]
                dst = dsts[a].at[chip, half(a, c), :] if a < na else dsts[a].at[chip]
                first.append(pltpu.make_async_remote_copy(
                    src_ref=src, dst_ref=dst, send_sem=ici_send.at[j * nall + a], recv_sem=ici_recv.at[j * nall + a],
                    device_id=(px, py, c), device_id_type=MESH))
        for cp in first:
            cp.start()

        passed = []
        for j, (px, py) in enumerate(chips):
            theirs = 2 * px + py
            for a in range(nall):
                if a < na:
                    landed = dsts[a].at[theirs, half(a, c), :]
                    fwd = pltpu.make_async_remote_copy(
                        src_ref=landed, dst_ref=landed, send_sem=d2d_send.at[j * na + a],
                        recv_sem=d2d_recv.at[j * na + a], device_id=(x, y, 1 - c), device_id_type=MESH)
                else:
                    landed = dsts[a].at[theirs]
                pltpu.make_async_remote_copy(
                    src_ref=landed, dst_ref=landed, send_sem=ici_send.at[j * nall + a],
                    recv_sem=ici_recv.at[j * nall + a], device_id=(px, py, c), device_id_type=MESH).wait_recv()
                if a < na:
                    fwd.start()
                    passed.append(fwd)
        for j, (px, py) in enumerate(chips):
            theirs = 2 * px + py
            for a in range(na):
                other = dsts[a].at[theirs, half(a, 1 - c), :]
                pltpu.make_async_remote_copy(
                    src_ref=other, dst_ref=other, send_sem=d2d_send.at[j * na + a], recv_sem=d2d_recv.at[j * na + a],
                    device_id=(x, y, 1 - c), device_id_type=MESH).wait_recv()
        for cp in first + passed:
            cp.wait_send()

    arrs = list(shards) + list(whole)
    outs = pl.pallas_call(
        body, name="gather_shards",
        in_specs=[HBM_SPEC] * nall, out_specs=[HBM_SPEC] * nall,
        out_shape=[jax.ShapeDtypeStruct((N_CHIPS,) + s.shape, s.dtype) for s in arrs],
        scratch_shapes=[pltpu.SemaphoreType.DMA((3 * nall,)), pltpu.SemaphoreType.DMA((3 * nall,)),
                        pltpu.SemaphoreType.DMA((3 * na,)), pltpu.SemaphoreType.DMA((3 * na,))],
    )(*arrs)
    chip = 2 * lax.axis_index("x") + lax.axis_index("y")
    return [lax.dynamic_update_slice(o, a[None], (chip,) + (0,) * a.ndim) for o, a in zip(outs, arrs)]


W_ROWS = 1568
G_ROWS = 1552
SHARD_ROWS = D_IN // N_CHIPS
WINDOW_STEP = 1536


def _assemble_w_in(cont):
    cb = 256
    half = WINDOW_STEP

    def body(c_ref, wa_ref, wf_ref, wb_ref):
        x0 = c_ref[0].astype(F32)
        x1, x2, x3 = (pltpu.roll(c_ref[j].astype(F32), 2 * j, 0) for j in (1, 2, 3))
        wa = jnp.concatenate([x0[:half], x0[half:half + 16] + x1[:16], x1[16:half]], axis=0)
        wa_ref[...] = wa.astype(BF16)

        fl = x1[half:half + 16] + x2[:16]
        row = lax.broadcasted_iota(jnp.int32, fl.shape, 0)
        wf_ref[:16, :] = jnp.where(row < H, fl, 0.0).astype(BF16)
        wf_ref[16:, :] = jnp.zeros((LANES - 16, cb), BF16)

        mid = x2[half:half + SUBLANES] + x3[:SUBLANES]
        wb = jnp.concatenate([x2[SUBLANES:half], mid, x3[SUBLANES:half + SUBLANES]], axis=0)
        wb_ref[...] = wb.astype(BF16)

    return pl.pallas_call(
        body, name="assemble_w_in", grid=(D // cb,),
        in_specs=[pl.BlockSpec((N_CHIPS, W_ROWS, cb), lambda i: (0, 0, i))],
        out_specs=[pl.BlockSpec((3 * D, cb), lambda i: (0, i)), pl.BlockSpec((LANES, cb), lambda i: (0, i)),
                   pl.BlockSpec((3 * D, cb), lambda i: (0, i))],
        out_shape=[jax.ShapeDtypeStruct((3 * D, D), BF16), jax.ShapeDtypeStruct((LANES, D), BF16),
                   jax.ShapeDtypeStruct((3 * D, D), BF16)],
        compiler_params=_cparams(("parallel",)),
    )(cont)


def _pair_exchange(grad_t, parts):
    na = len(parts)
    n = N_CHIPS + na
    half_g = G_ROWS // 2

    def body(*refs):
        g_ref, srcs, got = refs[0], refs[1:1 + na], refs[1 + na:2 + 2 * na]
        send_sems, recv_sems = refs[2 + 2 * na:]
        x, y, c = _position()
        pieces = []
        for j in range(N_CHIPS):
            rows = pl.ds(pl.multiple_of(j * WINDOW_STEP + (1 - c) * half_g, SUBLANES), half_g)
            pieces.append((g_ref.at[rows, :], got[0].at[j]))
        for a in range(na):
            half = srcs[a].shape[1] // 2
            rows = pl.ds(pl.multiple_of((1 - c) * half, SUBLANES), half)
            pieces.append((srcs[a].at[:, rows, :], got[1 + a]))
        copies = [pltpu.make_async_remote_copy(
            src_ref=give, dst_ref=dst, send_sem=send_sems.at[k], recv_sem=recv_sems.at[k],
            device_id=(x, y, 1 - c), device_id_type=MESH) for k, (give, dst) in enumerate(pieces)]
        for cp in copies:
            cp.start()
        for cp in copies:
            cp.wait()

    halves = [jax.ShapeDtypeStruct((N_CHIPS, half_g, D), F32)]
    halves += [jax.ShapeDtypeStruct((s.shape[0], s.shape[1] // 2, s.shape[2]), s.dtype) for s in parts]
    return pl.pallas_call(
        body, name="pair_exchange",
        in_specs=[HBM_SPEC] * (1 + na), out_specs=[HBM_SPEC] * (1 + na),
        out_shape=halves,
        scratch_shapes=[pltpu.SemaphoreType.DMA((n,)), pltpu.SemaphoreType.DMA((n,))],
    )(grad_t, *parts)


def _pair_sum(part, got, c, name):
    _, half, C = got.shape
    cb = min(C, 256)

    def body(c_ref, a_ref, b_ref, o_ref):
        o_ref[...] = (a_ref[...] + b_ref[...]).astype(BF16)

    spec = pl.BlockSpec((1, half, cb), lambda j, i, c_ref: (j, 0, i))
    grid_spec = pltpu.PrefetchScalarGridSpec(
        num_scalar_prefetch=1, grid=(N_CHIPS, C // cb),
        in_specs=[pl.BlockSpec((1, half, cb), lambda j, i, c_ref: (j, c_ref[0], i)), spec], out_specs=spec)
    return pl.pallas_call(
        body, name=name, grid_spec=grid_spec,
        out_shape=jax.ShapeDtypeStruct((N_CHIPS, half, C), BF16),
        compiler_params=_cparams(("parallel", "parallel")),
    )(c.reshape(1), part, got)


def _pair_sum_windows(grad_t, got, c):
    _, half, C = got.shape
    cb = 256

    def body(c_ref, a_ref, b_ref, o_ref):
        o_ref[0] = (a_ref[...] + b_ref[0]).astype(BF16)

    def mine(j, i, c_ref):
        return ((j * (WINDOW_STEP // SUBLANES) + c_ref[0] * (half // SUBLANES)) * SUBLANES, i * cb)

    spec = pl.BlockSpec((1, half, cb), lambda j, i, c_ref: (j, 0, i))
    grid_spec = pltpu.PrefetchScalarGridSpec(
        num_scalar_prefetch=1, grid=(N_CHIPS, C // cb),
        in_specs=[pl.BlockSpec((pl.Element(half), pl.Element(cb)), mine), spec], out_specs=spec)
    return pl.pallas_call(
        body, name="pair_sum_w_in", grid_spec=grid_spec,
        out_shape=jax.ShapeDtypeStruct((N_CHIPS, half, C), BF16),
        compiler_params=_cparams(("parallel", "parallel")),
    )(c.reshape(1), grad_t, got)


def _chip_exchange(sums):
    na = len(sums)

    def body(*refs):
        srcs, dsts = refs[:na], refs[na:2 * na]
        send_sems, recv_sems = refs[2 * na:]
        x, y, c = _position()
        chip = 2 * x + y
        copies = []
        for j, (px, py) in enumerate(_other_chips(x, y)):
            for a in range(na):
                copies.append(pltpu.make_async_remote_copy(
                    src_ref=srcs[a].at[2 * px + py], dst_ref=dsts[a].at[chip], send_sem=send_sems.at[j * na + a],
                    recv_sem=recv_sems.at[j * na + a], device_id=(px, py, c), device_id_type=MESH))
        for cp in copies:
            cp.start()
        for cp in copies:
            cp.wait()

    return pl.pallas_call(
        body, name="chip_exchange",
        in_specs=[HBM_SPEC] * na, out_specs=[HBM_SPEC] * na,
        out_shape=[jax.ShapeDtypeStruct(s.shape, s.dtype) for s in sums],
        scratch_shapes=[pltpu.SemaphoreType.DMA((3 * na,)), pltpu.SemaphoreType.DMA((3 * na,))],
    )(*sums)


def _chip_sum(own, got, chip, name):
    _, half, C = got.shape
    cb = min(C, 256)

    def body(chip_ref, own_ref, g_ref, o_ref):
        for me in range(N_CHIPS):
            @pl.when(chip_ref[0] == me)
            def _(me=me):
                terms = [own_ref[0] if k == me else g_ref[k] for k in range(N_CHIPS)]
                acc = terms[0].astype(F32) + terms[1].astype(F32)
                acc = acc + terms[2].astype(F32)
                o_ref[...] = acc + terms[3].astype(F32)

    grid_spec = pltpu.PrefetchScalarGridSpec(
        num_scalar_prefetch=1, grid=(C // cb,),
        in_specs=[pl.BlockSpec((1, half, cb), lambda i, chip_ref: (chip_ref[0], 0, i)),
                  pl.BlockSpec((N_CHIPS, half, cb), lambda i, chip_ref: (0, 0, i))],
        out_specs=pl.BlockSpec((half, cb), lambda i, chip_ref: (0, i)))
    return pl.pallas_call(
        body, name=name, grid_spec=grid_spec,
        out_shape=jax.ShapeDtypeStruct((half, C), F32),
        compiler_params=_cparams(("parallel",)),
    )(chip.reshape(1), own, got)


def _pair_swap(halves):
    na = len(halves)

    def body(*refs):
        srcs, dsts = refs[:na], refs[na:2 * na]
        send_sems, recv_sems = refs[2 * na:]
        x, y, c = _position()
        copies = [pltpu.make_async_remote_copy(
            src_ref=srcs[a], dst_ref=dsts[a], send_sem=send_sems.at[a], recv_sem=recv_sems.at[a],
            device_id=(x, y, 1 - c), device_id_type=MESH) for a in range(na)]
        for cp in copies:
            cp.start()
        for cp in copies:
            cp.wait()

    return pl.pallas_call(
        body, name="pair_swap",
        in_specs=[HBM_SPEC] * na, out_specs=[HBM_SPEC] * na,
        out_shape=[jax.ShapeDtypeStruct(s.shape, s.dtype) for s in halves],
        scratch_shapes=[pltpu.SemaphoreType.DMA((na,)), pltpu.SemaphoreType.DMA((na,))],
    )(*halves)


def _allreduce_small(g):
    rows = g.shape[0]
    per = rows // N_DEV

    def body(g_ref, out_ref, got_ref, s1, r1, s2, r2):
        x, y, c = _position()
        me = 4 * x + 2 * y + c
        mine = pl.ds(pl.multiple_of(me * per, SUBLANES), per)
        peers = []
        for j in range(1, N_DEV):
            px = 1 - x if j & 4 else x
            py = 1 - y if j & 2 else y
            pc = 1 - c if j & 1 else c
            peers.append((px, py, pc))

        first = []
        for j, (px, py, pc) in enumerate(peers):
            theirs = pl.ds(pl.multiple_of((4 * px + 2 * py + pc) * per, SUBLANES), per)
            first.append(pltpu.make_async_remote_copy(
                src_ref=g_ref.at[theirs, :], dst_ref=got_ref.at[me], send_sem=s1.at[j], recv_sem=r1.at[j],
                device_id=(px, py, pc), device_id_type=MESH))
        for cp in first:
            cp.start()
        got_ref[me] = g_ref[mine, :]
        for cp in first:
            cp.wait()
        total = got_ref[0]
        for d in range(1, N_DEV):
            total = total + got_ref[d]
        out_ref[mine, :] = total

        second = []
        for j, peer in enumerate(peers):
            second.append(pltpu.make_async_remote_copy(
                src_ref=out_ref.at[mine, :], dst_ref=out_ref.at[mine, :], send_sem=s2.at[j], recv_sem=r2.at[j],
                device_id=peer, device_id_type=MESH))
        for cp in second:
            cp.start()
        for cp in second:
            cp.wait()

    sems = pltpu.SemaphoreType.DMA((N_DEV - 1,))
    return pl.pallas_call(
        body, name="allreduce_small",
        in_specs=[VMEM_SPEC], out_specs=VMEM_SPEC,
        out_shape=jax.ShapeDtypeStruct(g.shape, F32),
        scratch_shapes=[pltpu.VMEM((N_DEV, per, LANES), F32), sems, sems, sems, sems],
    )(g)


def _adamw_math(g, w, m, v):
    m2 = ADAM_B1 * m + (1.0 - ADAM_B1) * g
    v2 = ADAM_B2 * v + (1.0 - ADAM_B2) * (g * g)
    m_hat = m2 / (1.0 - ADAM_B1 ** ADAM_STEP)
    v_hat = v2 / (1.0 - ADAM_B2 ** ADAM_STEP)
    delta = (-ADAM_LR) * (m_hat / (jnp.sqrt(v_hat) + ADAM_EPS) + ADAM_WD * w)
    return delta, m2, v2


def _adamw_big(g, w, m, v, name):
    R, C = g.shape
    cb = min(C, LANES)

    def body(g_ref, w_ref, m_ref, v_ref, d_ref, m2_ref, v2_ref):
        d_ref[...], m2_ref[...], v2_ref[...] = _adamw_math(g_ref[...], w_ref[...], m_ref[...], v_ref[...])

    spec = pl.BlockSpec((R, cb), lambda i: (0, i))
    out = jax.ShapeDtypeStruct((R, C), F32)
    return pl.pallas_call(
        body, name=name, grid=(C // cb,),
        in_specs=[spec] * 4, out_specs=[spec] * 3, out_shape=[out] * 3,
        compiler_params=_cparams(("parallel",)),
    )(g, w, m, v)


def _adamw_small(gs, ws, ms, vs):
    n = len(gs)

    def body(*refs):
        for a in range(n):
            g_ref, w_ref, m_ref, v_ref = (refs[k * n + a] for k in range(4))
            d_ref, m2_ref, v2_ref = (refs[(4 + k) * n + a] for k in range(3))
            d_ref[...], m2_ref[...], v2_ref[...] = _adamw_math(g_ref[...], w_ref[...], m_ref[...], v_ref[...])

    outs = [jax.ShapeDtypeStruct(w.shape, F32) for w in ws]
    return pl.pallas_call(
        body, name="adamw_small",
        in_specs=[VMEM_SPEC] * (4 * n), out_specs=[VMEM_SPEC] * (3 * n), out_shape=outs * 3,
    )(*gs, *ws, *ms, *vs)


def _local_step(x, p, tgt, w_a, w_f, w_b, w_out_b, w_ple_b, w_gate_b, conv_w, b_f, pre_gain, post_gain, conv_b,
                w_rgate, b_rgate, w_igate, b_igate, lam, gain_a, gain_l, ple_gain, b_gate):
    b_f_pad = jnp.pad(b_f, ((0, 0), (0, LANES - H)))
    w_r = w_rgate.astype(BF16)
    w_i = w_igate.astype(BF16)

    xn, q_aug, k_aug, v_aug, g_attn, x_lru, g_lru, flb = _in_proj(x, pre_gain, w_a, w_f, w_b, b_f_pad)
    o, qx = _attn_fwd(q_aug, k_aug, v_aug)
    ycat, xc, h = _branches_fwd(o, g_attn, x_lru, g_lru, gain_a, gain_l, conv_w, conv_b, w_r, b_rgate, w_i, b_igate,
                                lam)
    dh1, dycat, dmix, h1b, dgp, pb, dpe, acc_t = _tail(ycat, x, p, tgt, w_out_b, post_gain, w_ple_b, ple_gain,
                                                       w_gate_b, b_gate)
    do_aug, dg_attn, dg_lru, dh, acc_b = _branches_bwd(dycat, o, g_attn, h, g_lru, gain_a, gain_l)
    dx_lru, gw_r, gw_i, acc_l = _lru_bwd(dh, h, xc, x_lru, conv_w, w_r, b_rgate, w_i, b_igate, lam)
    dq, dk, dv, dc_key, dc_query = _attn_bwd(q_aug, qx, k_aug, v_aug, do_aug)
    dfl, acc_f = _fgate_bwd(dc_key, dc_query, flb)
    dz = (dq, dk, dv, dg_attn, dx_lru, dg_lru)
    grad_x, acc_x = _dx(dz, dfl, w_a, w_f, w_b, x, pre_gain, dh1)

    grads = dict(
        w_in_t=_dw_in_t(dz, dfl, xn),
        w_out=_matmul_tn(ycat, dmix, "dw_out"),
        w_ple=_matmul_tn(pb, dpe, "dw_ple"),
        w_ple_gate=_matmul_tn(h1b, dgp, "dw_ple_gate"),
        w_rgate=gw_r,
        w_igate=gw_i,
        b_f=acc_f[0:1, :H],
        pre_gain=acc_x[0:1],
        post_gain=acc_t[0:1],
        conv_w=acc_l[0:4],
        conv_b=acc_l[4:5],
        b_rgate=acc_l[5:6],
        b_igate=acc_l[6:7],
        lru_lambda=acc_l[7:8],
        attn_out_gain=acc_b[0:1],
        lru_out_gain=acc_b[1:2],
        ple_gain=acc_t[1:2],
        b_ple_gate=acc_t[2:3],
    )
    loss = jnp.sum(acc_t[3])
    return loss, grad_x, grads


SMALL_ROWS = ["b_f", "pre_gain", "post_gain", "conv_w", "conv_b", "b_rgate", "b_igate", "lru_lambda",
              "attn_out_gain", "lru_out_gain", "ple_gain", "b_ple_gate"]
WEIGHTS = ["w_in", "b_f", "pre_gain", "post_gain", "conv_w", "conv_b", "w_rgate", "b_rgate", "w_igate", "b_igate",
           "lru_lambda", "attn_out_gain", "lru_out_gain", "w_out", "w_ple", "ple_gain", "w_ple_gate", "b_ple_gate"]
SHARDED = ["w_in", "w_out", "w_ple", "w_ple_gate"]


def _by_chip_cols(g):
    r, cols = g.shape
    return g.reshape(r, N_CHIPS, cols // N_CHIPS).transpose(1, 0, 2)


def _from_chip_cols(s):
    n, r, cols = s.shape
    return s.transpose(1, 0, 2).reshape(r, n * cols)


def kernel(x, p, w_in, b_f, pre_gain, post_gain, conv_w, conv_b, w_rgate, b_rgate, w_igate, b_igate, lru_lambda, attn_out_gain, lru_out_gain, w_out, w_ple, ple_gain, w_ple_gate, b_ple_gate, loss_target, m_w_in, m_b_f, m_pre_gain, m_post_gain, m_conv_w, m_conv_b, m_w_rgate, m_b_rgate, m_w_igate, m_b_igate, m_lru_lambda, m_attn_out_gain, m_lru_out_gain, m_w_out, m_w_ple, m_ple_gain, m_w_ple_gate, m_b_ple_gate, v_w_in, v_b_f, v_pre_gain, v_post_gain, v_conv_w, v_conv_b, v_w_rgate, v_b_rgate, v_w_igate, v_b_igate, v_lru_lambda, v_attn_out_gain, v_lru_out_gain, v_w_out, v_w_ple, v_ple_gain, v_w_ple_gate, v_b_ple_gate):
    w = dict(w_in=w_in, b_f=b_f, pre_gain=pre_gain, post_gain=post_gain, conv_w=conv_w, conv_b=conv_b,
             w_rgate=w_rgate, b_rgate=b_rgate, w_igate=w_igate, b_igate=b_igate, lru_lambda=lru_lambda,
             attn_out_gain=attn_out_gain, lru_out_gain=lru_out_gain, w_out=w_out, w_ple=w_ple, ple_gain=ple_gain,
             w_ple_gate=w_ple_gate, b_ple_gate=b_ple_gate)
    m = dict(w_in=m_w_in, b_f=m_b_f, pre_gain=m_pre_gain, post_gain=m_post_gain, conv_w=m_conv_w, conv_b=m_conv_b,
             w_rgate=m_w_rgate, b_rgate=m_b_rgate, w_igate=m_w_igate, b_igate=m_b_igate, lru_lambda=m_lru_lambda,
             attn_out_gain=m_attn_out_gain, lru_out_gain=m_lru_out_gain, w_out=m_w_out, w_ple=m_w_ple,
             ple_gain=m_ple_gain, w_ple_gate=m_w_ple_gate, b_ple_gate=m_b_ple_gate)
    v = dict(w_in=v_w_in, b_f=v_b_f, pre_gain=v_pre_gain, post_gain=v_post_gain, conv_w=v_conv_w, conv_b=v_conv_b,
             w_rgate=v_w_rgate, b_rgate=v_b_rgate, w_igate=v_w_igate, b_igate=v_b_igate, lru_lambda=v_lru_lambda,
             attn_out_gain=v_attn_out_gain, lru_out_gain=v_lru_out_gain, w_out=v_w_out, w_ple=v_w_ple,
             ple_gain=v_ple_gain, w_ple_gate=v_w_ple_gate, b_ple_gate=v_b_ple_gate)
    xi, yi, ci = _position()
    chip = 2 * xi + yi

    w_in_t, m_in_t, v_in_t = (jnp.swapaxes(t[0], 0, 1) for t in (w_in, m_w_in, v_w_in))
    window = jnp.pad(w_in_t.astype(BF16), ((0, W_ROWS - SHARD_ROWS), (0, 0)))

    st_in, st_out, st_ple, st_gate, st_conv = _gather_shards(
        [window, w_out[0].astype(BF16), w_ple[0].astype(BF16), w_ple_gate[0].astype(BF16)], [conv_w[0]])
    w_a, w_f, w_b = _assemble_w_in(st_in)
    w_out_b = st_out.reshape(DMIX, D)
    w_ple_b = _from_chip_cols(st_ple)
    w_gate_b = st_gate.reshape(D, D)
    conv_full = _from_chip_cols(st_conv)

    loss, grad_x, g = _local_step(
        x[0], p[0, 0], loss_target[0], w_a, w_f, w_b, w_out_b, w_ple_b, w_gate_b, conv_full, b_f, pre_gain, post_gain,
        conv_b, w_rgate[0], b_rgate, w_igate[0], b_igate, lru_lambda, attn_out_gain, lru_out_gain, ple_gain,
        b_ple_gate)
    loss = lax.psum(loss, ("x", "y", "c"))

    parts = [g["w_out"].reshape(N_CHIPS, DMIX // N_CHIPS, D), _by_chip_cols(g["w_ple"]),
             g["w_ple_gate"].reshape(N_CHIPS, D // N_CHIPS, D)]
    got = _pair_exchange(g["w_in_t"], parts)
    sums = [_pair_sum_windows(g["w_in_t"], got[0], ci)]
    sums += [_pair_sum(parts[a], got[1 + a], ci, "pair_sum_%d" % a) for a in range(3)]
    recv = _chip_exchange(sums)
    halves = [_chip_sum(sums[a], recv[a], chip, "chip_sum_%d" % a) for a in range(4)]
    theirs = _pair_swap(halves)
    full = [jnp.concatenate([jnp.where(ci == 0, a, b), jnp.where(ci == 0, b, a)], axis=0)
            for a, b in zip(halves, theirs)]
    red = dict(zip(SHARDED, full))
    red["w_in"] = lax.dynamic_slice_in_dim(red["w_in"], 2 * chip, SHARD_ROWS, axis=0)

    rows = [jnp.pad(g["b_f"], ((0, 0), (0, D - H)))] + [g[n] for n in SMALL_ROWS[1:]]
    rows.append(jnp.zeros((16 - sum(r.shape[0] for r in rows), D), F32))
    packed = jnp.concatenate([g["w_rgate"].reshape(NB * LANES, LANES), g["w_igate"].reshape(NB * LANES, LANES),
                              jnp.concatenate(rows, axis=0).reshape(LANES, LANES)], axis=0)
    summed = _allreduce_small(packed)
    red["w_rgate"] = summed[:D].reshape(1, NB, LANES, LANES)
    red["w_igate"] = summed[D:2 * D].reshape(1, NB, LANES, LANES)
    vec = summed[2 * D:].reshape(16, D)
    r0 = 0
    for n in SMALL_ROWS:
        nr = 4 if n == "conv_w" else 1
        red[n] = vec[r0:r0 + nr]
        r0 += nr
    red["b_f"] = red["b_f"][:, :H]
    red["conv_w"] = lax.dynamic_slice_in_dim(red["conv_w"], chip * (D // N_CHIPS), D // N_CHIPS, axis=1)[None]

    delta, new_m, new_v = {}, {}, {}
    outs_in = _adamw_big(red["w_in"], w_in_t, m_in_t, v_in_t, "adamw_w_in")
    delta["w_in"], new_m["w_in"], new_v["w_in"] = (jnp.swapaxes(t, 0, 1)[None] for t in outs_in)
    red["w_in"] = jnp.swapaxes(red["w_in"], 0, 1)[None]
    for n in SHARDED[1:]:
        delta[n], new_m[n], new_v[n] = (t[None] for t in _adamw_big(red[n], w[n][0], m[n][0], v[n][0], "adamw_" + n))
        red[n] = red[n][None]
    small = [n for n in WEIGHTS if n not in SHARDED]
    outs = _adamw_small([red[n] for n in small], [w[n] for n in small], [m[n] for n in small],
                        [v[n] for n in small])
    ns = len(small)
    for a, n in enumerate(small):
        delta[n], new_m[n], new_v[n] = outs[a], outs[ns + a], outs[2 * ns + a]

    return (loss, grad_x[None], *[red[n] for n in WEIGHTS], *[delta[n] for n in WEIGHTS],
            *[new_m[n] for n in WEIGHTS], *[new_v[n] for n in WEIGHTS])
```

```python
import functools

import jax
import jax.numpy as jnp
import numpy as np
from jax import lax
from jax.experimental import pallas as pl
from jax.experimental.pallas import tpu as pltpu

F32 = jnp.float32
BF16 = jnp.bfloat16

D = 1024
H = 8
DH = 128
NB = 8
DPLE = 256
DMIX = 2 * D
D_IN = 4 * D + H + 2 * D
FL0 = 3 * D
RMS_EPS = 1e-6
LRU_C = 8.0
NEG = -1e30
LANES = 128
SUBLANES = 8

ADAM_LR = 0.001
ADAM_B1 = 0.9
ADAM_B2 = 0.999
ADAM_EPS = 1e-08
ADAM_WD = 0.01
ADAM_STEP = 10

TM = 256
TA = 512
FWD_HEADS = 4
BWD_HEADS = 2
VMEM_BIG = 56 * 1024 * 1024
VMEM_MID = 40 * 1024 * 1024

MESH = pl.DeviceIdType.MESH
N_CHIPS = 4
N_DEV = 8


def _call(body, *, out_shape, in_hbm=True, **kwargs):
    if not in_hbm:
        return pl.pallas_call(body, out_shape=out_shape, **kwargs)

    def pin(shape):
        return pltpu.HBM(shape.shape, shape.dtype) if isinstance(shape, jax.ShapeDtypeStruct) else shape

    fn = pl.pallas_call(body, out_shape=jax.tree.map(pin, out_shape), **kwargs)

    def run(*args):
        return fn(*[a if a.dtype == jnp.int32 else pltpu.with_memory_space_constraint(a, pltpu.HBM) for a in args])

    return run


def _cparams(sem, vmem=VMEM_MID):
    return pltpu.CompilerParams(dimension_semantics=sem, vmem_limit_bytes=vmem)


def _sigmoid(x):
    return 0.5 * jnp.tanh(0.5 * x) + 0.5


def _rstd(x):
    return lax.rsqrt(jnp.mean(x * x, axis=-1, keepdims=True) + RMS_EPS)


def _rms_bwd(t, xhat, rstd):
    return rstd * (t - xhat * jnp.mean(t * xhat, axis=-1, keepdims=True))


def _dot(a, b):
    return jnp.dot(a, b, preferred_element_type=F32)


def _dot_nt(a, b):
    return lax.dot_general(a, b, (((1,), (1,)), ((), ())), preferred_element_type=F32)


def _dot_tn(a, b):
    return lax.dot_general(a, b, (((0,), (0,)), ((), ())), preferred_element_type=F32)


def _dot_exact(a, b):
    return jnp.dot(a, b, preferred_element_type=F32, precision=lax.Precision.HIGHEST)


def _shift_down(x, j, halo):
    rolled = pltpu.roll(x, j, 0)
    row = lax.broadcasted_iota(jnp.int32, halo.shape, 0)
    top = jnp.where(row < j, pltpu.roll(halo, j, 0), rolled[:SUBLANES])
    return jnp.concatenate([top, rolled[SUBLANES:]], axis=0)


def _shift_up(x, j, nxt):
    tm = x.shape[0]
    rolled = pltpu.roll(x, tm - j, 0)
    row = lax.broadcasted_iota(jnp.int32, nxt.shape, 0)
    bot = jnp.where(row >= SUBLANES - j, pltpu.roll(nxt, SUBLANES - j, 0), rolled[tm - SUBLANES:])
    return jnp.concatenate([rolled[:tm - SUBLANES], bot], axis=0)


def _scan_fwd_into(a, u, carry, h_ref):
    tm = a.shape[0]
    sub = lax.broadcasted_iota(jnp.int32, a.shape, 0) & (SUBLANES - 1)
    d = 1
    while d < SUBLANES:
        keep = sub >= d
        a_s = jnp.where(keep, pltpu.roll(a, d, 0), 1.0)
        u_s = jnp.where(keep, pltpu.roll(u, d, 0), 0.0)
        u = u + a * u_s
        a = a * a_s
        d *= 2
    for g in range(tm // SUBLANES):
        rows = slice(g * SUBLANES, (g + 1) * SUBLANES)
        h_ref[rows, :] = u[rows] + a[rows] * carry
        carry = h_ref[(g + 1) * SUBLANES - 1:(g + 1) * SUBLANES, :]
    return carry


def _scan_bwd_into(b, u, g_ref):
    tm = b.shape[0]
    sub = lax.broadcasted_iota(jnp.int32, b.shape, 0) & (SUBLANES - 1)
    d = 1
    while d < SUBLANES:
        keep = sub < SUBLANES - d
        b_s = jnp.where(keep, pltpu.roll(b, tm - d, 0), 1.0)
        u_s = jnp.where(keep, pltpu.roll(u, tm - d, 0), 0.0)
        u = u + b * u_s
        b = b * b_s
        d *= 2
    nxt = jnp.zeros((1, b.shape[1]), F32)
    for g in reversed(range(tm // SUBLANES)):
        rows = slice(g * SUBLANES, (g + 1) * SUBLANES)
        g_ref[rows, :] = u[rows] + b[rows] * nxt
        nxt = g_ref[g * SUBLANES:g * SUBLANES + 1, :]


def _gate_pre(xc, w_ref):
    outs = []
    for n in range(NB):
        outs.append(_dot(xc[:, n * LANES:(n + 1) * LANES].astype(BF16), w_ref[n]))
    return jnp.concatenate(outs, axis=1)


def _gate_pre_t(d, w_ref):
    outs = []
    for n in range(NB):
        outs.append(_dot_nt(d[:, n * LANES:(n + 1) * LANES].astype(BF16), w_ref[n]))
    return jnp.concatenate(outs, axis=1)


def _softplus_neg(lam):
    return jnp.maximum(-lam, 0.0) + jnp.log(1.0 + jnp.exp(-jnp.abs(lam)))


def _row_spec(tm, width):
    return pl.BlockSpec((tm, width), lambda i: (i, 0))


def _const_spec(shape):
    nd = len(shape)
    return pl.BlockSpec(shape, lambda *_: (0,) * nd)


AUG = 2 * DH
LOG2E = 1.4426950408889634
LN2 = 0.6931471805599453
Q_SCALE = DH ** -0.5 * LOG2E


def _split3(x):
    hi = x.astype(BF16)
    r1 = x - hi.astype(F32)
    mid = r1.astype(BF16)
    lo = (r1 - mid.astype(F32)).astype(BF16)
    return hi, mid, lo


def _extras(col, ones_from):
    t = col.shape[0]
    hi, mid, lo = _split3(jnp.broadcast_to(col, (t, LANES)))
    lane = lax.broadcasted_iota(jnp.int32, (t, LANES), 1)
    rest = jnp.zeros((t, LANES), BF16)
    if ones_from is not None:
        rest = jnp.where((lane >= ones_from) & (lane < ones_from + 3), 1.0, 0.0).astype(BF16)
    return jnp.where(lane == 0, hi, jnp.where(lane == 1, mid, jnp.where(lane == 2, lo, rest)))


def _selectors():
    sel_q = np.zeros((3 * LANES, H * LANES), np.float32)
    sel_k = np.zeros((3 * LANES, H * LANES), np.float32)
    for hd in range(H):
        for piece in range(3):
            sel_q[piece * LANES + hd, hd * LANES + piece] = 1.0
            sel_k[piece * LANES + hd, hd * LANES + 3 + piece] = -1.0
    return jnp.asarray(sel_q, BF16), jnp.asarray(sel_k, BF16)


def _in_proj(x, pre_gain, w_a, w_f, w_b, b_f_pad):
    T = x.shape[0]
    tm = TM
    sel_q, sel_k = _selectors()

    def body(x_ref, g_ref, wa_ref, wf_ref, wb_ref, bf_ref, sq_ref, sk_ref,
             xn_ref, qa_ref, ka_ref, va_ref, ga_ref, xl_ref, gl_ref, flb_ref, c_s, carry):
        @pl.when(pl.program_id(0) == 0)
        def _():
            carry[...] = jnp.zeros_like(carry)

        xv = x_ref[...]
        xn = (xv * _rstd(xv) * g_ref[...]).astype(BF16)
        xn_ref[...] = xn
        for s, o_ref in enumerate((ga_ref, xl_ref, gl_ref)):
            o_ref[...] = _dot_nt(xn, wb_ref[s * D:(s + 1) * D, :]).astype(o_ref.dtype)
        flb = _dot_nt(xn, wf_ref[...]) + bf_ref[...]
        flb_ref[...] = flb
        lane = lax.broadcasted_iota(jnp.int32, flb.shape, 1)
        ls = jnp.where(lane < H, jnp.minimum(flb, 0.0) - jnp.log(1.0 + jnp.exp(-jnp.abs(flb))), 0.0)
        r = lax.broadcasted_iota(jnp.int32, (tm, tm), 0)
        c = lax.broadcasted_iota(jnp.int32, (tm, tm), 1)
        cs = _dot_exact((c <= r).astype(F32), ls) + carry[...]
        c_s[...] = cs
        carry[...] = c_s[tm - 1:tm, :]

        pieces = jnp.concatenate(_split3(cs * LOG2E), axis=1)
        ones_q = jnp.where((lane >= 3) & (lane < 6), 1.0, 0.0)
        ones_k = jnp.where(lane < 3, 1.0, 0.0)
        zq = _dot_nt(xn, wa_ref[0:D, :]) * Q_SCALE
        zk = _dot_nt(xn, wa_ref[D:2 * D, :])
        zv = _dot_nt(xn, wa_ref[2 * D:3 * D, :])
        ex_q = _dot(pieces, sq_ref[...])
        ex_k = _dot(pieces, sk_ref[...])
        for hd in range(H):
            head = slice(hd * DH, (hd + 1) * DH)
            lo, hi = hd * AUG, hd * AUG + DH
            qa_ref[:, lo:hi] = zq[:, head].astype(BF16)
            qa_ref[:, hi:hi + DH] = (ex_q[:, head] + ones_q).astype(BF16)
            ka_ref[:, lo:hi] = zk[:, head].astype(BF16)
            ka_ref[:, hi:hi + DH] = (ex_k[:, head] + ones_k).astype(BF16)
            va_ref[:, lo:hi] = zv[:, head].astype(BF16)
            va_ref[:, hi:hi + DH] = ones_k.astype(BF16)

    bf = jax.ShapeDtypeStruct((T, D), BF16)
    aug = jax.ShapeDtypeStruct((T, H * AUG), BF16)
    f32 = jax.ShapeDtypeStruct((T, D), F32)
    sel_spec = _const_spec((3 * LANES, H * LANES))
    return _call(
        body, name="in_proj", grid=(T // tm,),
        in_specs=[_row_spec(tm, D), _const_spec((1, D)), _const_spec((3 * D, D)), _const_spec((LANES, D)),
                  _const_spec((3 * D, D)), _const_spec((1, LANES)), sel_spec, sel_spec],
        out_specs=[_row_spec(tm, D)] + [_row_spec(tm, H * AUG)] * 3 + [_row_spec(tm, D)] * 3 + [_row_spec(tm, LANES)],
        out_shape=[bf, aug, aug, aug, f32, f32, f32, jax.ShapeDtypeStruct((T, LANES), F32)],
        scratch_shapes=[pltpu.VMEM((tm, LANES), F32), pltpu.VMEM((1, LANES), F32)],
        compiler_params=_cparams(("arbitrary",), VMEM_BIG),
    )(x, pre_gain, w_a, w_f, w_b, b_f_pad, sel_q, sel_k)


def _causal_pairs(n, q_major):
    if q_major:
        pairs = [(qi, ki) for qi in range(n) for ki in range(qi + 1)]
    else:
        pairs = [(ki, qi) for ki in range(n) for qi in range(ki, n)]
    return (jnp.asarray([a for a, _ in pairs], jnp.int32), jnp.asarray([b for _, b in pairs], jnp.int32))


def _attn_fwd(q_aug, k_aug, v_aug):
    T = q_aug.shape[0]
    t = TA
    n = T // t
    hp = FWD_HEADS
    heads = range(hp)
    qi_tab, ki_tab = _causal_pairs(n, q_major=True)

    def body(qi_ref, ki_ref, q_ref, k_ref, v_ref, o_ref, qx_ref, m_s, acc_s):
        j = pl.program_id(1)
        qi = qi_ref[j]
        ki = ki_ref[j]

        @pl.when(ki == 0)
        def _():
            m_s[...] = jnp.full(m_s.shape, NEG, F32)
            acc_s[...] = jnp.zeros_like(acc_s)

        def step(on_diagonal):
            cols = [slice(a * AUG, (a + 1) * AUG) for a in heads]
            s = [_dot_nt(q_ref[:, cols[a]], k_ref[:, cols[a]]) for a in heads]
            if on_diagonal:
                row = lax.broadcasted_iota(jnp.int32, (t, t), 0)
                col = lax.broadcasted_iota(jnp.int32, (t, t), 1)
                s = [jnp.where(col <= row, s[a], NEG) for a in heads]
            m_prev = [m_s[a] for a in heads]
            m_new = [jnp.maximum(m_prev[a], jnp.max(s[a], axis=1, keepdims=True)) for a in heads]
            pr = [jnp.exp2(s[a] - m_new[a]).astype(BF16) for a in heads]
            for a in heads:
                acc_s[a] = jnp.exp2(m_prev[a] - m_new[a]) * acc_s[a] + _dot(pr[a], v_ref[:, cols[a]])
                m_s[a] = m_new[a]

        @pl.when(ki < qi)
        def _():
            step(False)

        @pl.when(ki == qi)
        def _():
            step(True)
            for a in heads:
                acc = acc_s[a]
                l = acc[:, DH:DH + 1]
                o_ref[:, a * DH:(a + 1) * DH] = acc[:, :DH] / l
                ex = q_ref[:, a * AUG + DH:(a + 1) * AUG].astype(F32)
                c2 = ex[:, 0:1] + ex[:, 1:2] + ex[:, 2:3]
                qx_ref[:, a * DH:(a + 1) * DH] = _extras(c2 - (m_s[a] + jnp.log(l) * LOG2E), 3)

    q_spec = pl.BlockSpec((t, hp * AUG), lambda h, j, qi_ref, ki_ref: (qi_ref[j], h))
    kv_spec = pl.BlockSpec((t, hp * AUG), lambda h, j, qi_ref, ki_ref: (ki_ref[j], h))
    out_spec = pl.BlockSpec((t, hp * DH), lambda h, j, qi_ref, ki_ref: (qi_ref[j], h))
    grid_spec = pltpu.PrefetchScalarGridSpec(
        num_scalar_prefetch=2, grid=(H // hp, qi_tab.shape[0]),
        in_specs=[q_spec, kv_spec, kv_spec], out_specs=[out_spec, out_spec],
        scratch_shapes=[pltpu.VMEM((hp, t, 1), F32), pltpu.VMEM((hp, t, AUG), F32)])
    return _call(
        body, name="attn_fwd", grid_spec=grid_spec,
        out_shape=[jax.ShapeDtypeStruct((T, D), F32), jax.ShapeDtypeStruct((T, D), BF16)],
        compiler_params=_cparams(("parallel", "arbitrary"), VMEM_BIG),
    )(qi_tab, ki_tab, q_aug, k_aug, v_aug)


def _lru_gates(xc, wr_ref, br_ref, wi_ref, bi_ref, lam_ref):
    r = _sigmoid(_gate_pre(xc, wr_ref) + br_ref[...])
    ig = _sigmoid(_gate_pre(xc, wi_ref) + bi_ref[...])
    sp = _softplus_neg(lam_ref[...])
    la = (-LRU_C) * r * sp
    a = jnp.exp(la)
    y = -jnp.tanh(la) * (a * a + 1.0)
    return r, ig, sp, a, jnp.sqrt(y), lax.rsqrt(y)


def _branches_fwd(o, g_attn, x_lru, g_lru, gain_a, gain_l, conv_w, conv_b, w_r, b_r, w_i, b_i, lam):
    T = o.shape[0]
    tm = TM

    def body(o_ref, ga_ref, xl_ref, gl_ref, gna_ref, gnl_ref, cw_ref, cb_ref, wr_ref, br_ref, wi_ref, bi_ref,
             lam_ref, ycat_ref, xc_ref, h_ref, halo_s, hc_s):
        @pl.when(pl.program_id(0) == 0)
        def _():
            halo_s[...] = jnp.zeros_like(halo_s)
            hc_s[...] = jnp.zeros_like(hc_s)

        ov = o_ref[...]
        ga = ga_ref[...]
        ya = ov * _rstd(ov) * gna_ref[...] * (ga * _sigmoid(ga))
        ycat_ref[:, :D] = ya.astype(BF16)

        xl = xl_ref[...]
        halo = halo_s[...]
        xc = xl * cw_ref[3:4, :] + cb_ref[...]
        for j in range(3):
            xc = xc + _shift_down(xl, 3 - j, halo) * cw_ref[j:j + 1, :]
        halo_s[...] = xl_ref[tm - SUBLANES:tm, :]
        xc_ref[...] = xc

        _, ig, _, a, sq, _ = _lru_gates(xc, wr_ref, br_ref, wi_ref, bi_ref, lam_ref)
        u = sq * (ig * xc)
        hc_s[...] = _scan_fwd_into(a, u, hc_s[...], h_ref)
        hh = h_ref[...]

        gl = gl_ref[...]
        yl = hh * _rstd(hh) * gnl_ref[...] * (gl * _sigmoid(gl))
        ycat_ref[:, D:] = yl.astype(BF16)

    vec = _const_spec((1, D))
    wspec = _const_spec((NB, LANES, LANES))
    return _call(
        body, name="branches_fwd", grid=(T // tm,),
        in_specs=[_row_spec(tm, D)] * 4 + [vec, vec, _const_spec((4, D)), vec, wspec, vec, wspec, vec, vec],
        out_specs=[_row_spec(tm, DMIX), _row_spec(tm, D), _row_spec(tm, D)],
        out_shape=[jax.ShapeDtypeStruct((T, DMIX), BF16), jax.ShapeDtypeStruct((T, D), F32),
                   jax.ShapeDtypeStruct((T, D), F32)],
        scratch_shapes=[pltpu.VMEM((SUBLANES, D), F32), pltpu.VMEM((1, D), F32)],
        compiler_params=_cparams(("arbitrary",)),
    )(o, g_attn, x_lru, g_lru, gain_a, gain_l, conv_w, conv_b, w_r, b_r, w_i, b_i, lam)


def _tail(ycat, x, p, tgt, w_out, post_gain, w_ple, ple_gain, w_gate, b_gate):
    T = x.shape[0]
    tm = TM

    def body(ycat_ref, x_ref, p_ref, t_ref, wo_ref, pg_ref, wp_ref, eg_ref, wg_ref, bg_ref,
             dh1_ref, dycat_ref, dmix_ref, h1b_ref, dgp_ref, pb_ref, dpe_ref, acc_ref):
        @pl.when(pl.program_id(0) == 0)
        def _():
            acc_ref[...] = jnp.zeros_like(acc_ref)

        mix = _dot(ycat_ref[...], wo_ref[...])
        rstd_m = _rstd(mix)
        mhat = mix * rstd_m
        h1 = x_ref[...] + mhat * pg_ref[...]
        pb = p_ref[...].astype(BF16)
        pb_ref[...] = pb
        pe = _dot(pb, wp_ref[...])
        rstd_p = _rstd(pe)
        pehat = pe * rstd_p
        e = pehat * eg_ref[...]
        h1b = h1.astype(BF16)
        h1b_ref[...] = h1b
        gate = _sigmoid(_dot(h1b, wg_ref[...]) + bg_ref[...])
        diff = (h1 + gate * e) - t_ref[...]

        dy = diff * (1.0 / D)
        de = dy * gate
        dgp = (dy * e) * gate * (1.0 - gate)
        dgpb = dgp.astype(BF16)
        dgp_ref[...] = dgpb
        dh1 = dy + _dot_nt(dgpb, wg_ref[...])
        dh1_ref[...] = dh1
        dpe_ref[...] = _rms_bwd(de * eg_ref[...], pehat, rstd_p).astype(BF16)
        dmix = _rms_bwd(dh1 * pg_ref[...], mhat, rstd_m).astype(BF16)
        dmix_ref[...] = dmix
        dycat_ref[...] = _dot_nt(dmix, wo_ref[...])

        acc_ref[0:1, :] += jnp.sum(dh1 * mhat, axis=0, keepdims=True)
        acc_ref[1:2, :] += jnp.sum(de * pehat, axis=0, keepdims=True)
        acc_ref[2:3, :] += jnp.sum(dgp, axis=0, keepdims=True)
        acc_ref[3:4, :] += jnp.sum(diff * diff, axis=0, keepdims=True) * (0.5 / D)

    vec = _const_spec((1, D))
    bf = jax.ShapeDtypeStruct((T, D), BF16)
    return _call(
        body, name="tail", grid=(T // tm,),
        in_specs=[_row_spec(tm, DMIX), _row_spec(tm, D), _row_spec(tm, DPLE), _row_spec(tm, D),
                  _const_spec((DMIX, D)), vec, _const_spec((DPLE, D)), vec, _const_spec((D, D)), vec],
        out_specs=[_row_spec(tm, D), _row_spec(tm, DMIX), _row_spec(tm, D), _row_spec(tm, D), _row_spec(tm, D),
                   _row_spec(tm, DPLE), _row_spec(tm, D), _const_spec((SUBLANES, D))],
        out_shape=[jax.ShapeDtypeStruct((T, D), F32), jax.ShapeDtypeStruct((T, DMIX), F32), bf, bf, bf,
                   jax.ShapeDtypeStruct((T, DPLE), BF16), bf, jax.ShapeDtypeStruct((SUBLANES, D), F32)],
        compiler_params=_cparams(("arbitrary",), VMEM_BIG),
    )(ycat, x, p, tgt, w_out, post_gain, w_ple, ple_gain, w_gate, b_gate)


def _branches_bwd(dycat, o, g_attn, h, g_lru, gain_a, gain_l):
    T = o.shape[0]
    tm = TM

    def body(dy_ref, o_ref, ga_ref, h_ref, gl_ref, gna_ref, gnl_ref,
             do_ref, dga_ref, dgl_ref, dh_ref, acc_ref):
        @pl.when(pl.program_id(0) == 0)
        def _():
            acc_ref[...] = jnp.zeros_like(acc_ref)

        def branch(val, g, gain, dyv):
            rstd = _rstd(val)
            vhat = val * rstd
            sig = _sigmoid(g)
            dn = dyv * (g * sig)
            dg = dyv * (vhat * gain) * (sig * (1.0 + g * (1.0 - sig)))
            dgain = jnp.sum(dn * vhat, axis=0, keepdims=True)
            return _rms_bwd(dn * gain, vhat, rstd), dg, dgain

        ov = o_ref[...]
        do, dga, dgain_a = branch(ov, ga_ref[...], gna_ref[...], dy_ref[:, :D])
        dga_ref[...] = dga.astype(BF16)
        prod = do * ov
        for hd in range(H):
            head = slice(hd * DH, (hd + 1) * DH)
            do_ref[:, hd * AUG:hd * AUG + DH] = do[:, head].astype(BF16)
            do_ref[:, hd * AUG + DH:(hd + 1) * AUG] = _extras(-jnp.sum(prod[:, head], axis=1, keepdims=True), None)

        dh, dgl, dgain_l = branch(h_ref[...], gl_ref[...], gnl_ref[...], dy_ref[:, D:])
        dh_ref[...] = dh
        dgl_ref[...] = dgl.astype(BF16)
        acc_ref[0:1, :] += dgain_a
        acc_ref[1:2, :] += dgain_l

    vec = _const_spec((1, D))
    bf = jax.ShapeDtypeStruct((T, D), BF16)
    return _call(
        body, name="branches_bwd", grid=(T // tm,),
        in_specs=[_row_spec(tm, DMIX)] + [_row_spec(tm, D)] * 4 + [vec, vec],
        out_specs=[_row_spec(tm, H * AUG), _row_spec(tm, D), _row_spec(tm, D), _row_spec(tm, D),
                   _const_spec((SUBLANES, D))],
        out_shape=[jax.ShapeDtypeStruct((T, H * AUG), BF16), bf, bf, jax.ShapeDtypeStruct((T, D), F32),
                   jax.ShapeDtypeStruct((SUBLANES, D), F32)],
        compiler_params=_cparams(("arbitrary",)),
    )(dycat, o, g_attn, h, g_lru, gain_a, gain_l)


def _lru_bwd(dh, h, xc, x_lru, conv_w, w_r, b_r, w_i, b_i, lam):
    T = dh.shape[0]
    tm = TM
    nt = T // tm
    per = tm // SUBLANES

    def body(dh_ref, h_ref, hprev_ref, xc_ref, xl_ref, xlprev_ref, cw_ref, wr_ref, br_ref, wi_ref, bi_ref, lam_ref,
             dxl_ref, dwr_ref, dwi_ref, acc_ref, carry_s, dxc_next_s, top_s, dht_s):
        i = pl.program_id(0)

        @pl.when(i == 0)
        def _():
            acc_ref[...] = jnp.zeros_like(acc_ref)
            dwr_ref[...] = jnp.zeros_like(dwr_ref)
            dwi_ref[...] = jnp.zeros_like(dwi_ref)
            carry_s[...] = jnp.zeros_like(carry_s)
            dxc_next_s[...] = jnp.zeros_like(dxc_next_s)

        inner = jnp.where(i == nt - 1, 0.0, 1.0)
        xc = xc_ref[...]
        r, ig, sp, a, sq, inv_sq = _lru_gates(xc, wr_ref, br_ref, wi_ref, bi_ref, lam_ref)

        row = lax.broadcasted_iota(jnp.int32, (tm, D), 0)
        u = dh_ref[...] + jnp.where(row == tm - 1, carry_s[...], 0.0)
        _scan_bwd_into(pltpu.roll(a, tm - 1, 0), u, dht_s)
        dht = dht_s[...]
        top_s[...] = a[:SUBLANES, :] * dht[:SUBLANES, :]
        carry_s[...] = top_s[0:1, :]

        hprev = hprev_ref[...] * inner
        da = dht * _shift_down(h_ref[...], 1, hprev)
        dig = dht * sq * xc
        dxc = dht * sq * ig
        dsq = dht * ig * xc
        dla = da * a - dsq * (a * a) * inv_sq
        dr = dla * ((-LRU_C) * sp)
        dpr = dr * r * (1.0 - r)
        dpi = dig * ig * (1.0 - ig)
        for n in range(NB):
            blk = slice(n * LANES, (n + 1) * LANES)
            xcb = xc[:, blk].astype(BF16)
            dwr_ref[n] += _dot_tn(xcb, dpr[:, blk].astype(BF16))
            dwi_ref[n] += _dot_tn(xcb, dpi[:, blk].astype(BF16))
        dxc = dxc + _gate_pre_t(dpr, wr_ref) + _gate_pre_t(dpi, wi_ref)

        xl = xl_ref[...]
        xlprev = xlprev_ref[...] * inner
        nxt = dxc_next_s[...]
        dxl = dxc * cw_ref[3:4, :]
        acc_ref[3:4, :] += jnp.sum(dxc * xl, axis=0, keepdims=True)
        for j in range(3):
            dxl = dxl + _shift_up(dxc, 3 - j, nxt) * cw_ref[j:j + 1, :]
            acc_ref[j:j + 1, :] += jnp.sum(dxc * _shift_down(xl, 3 - j, xlprev), axis=0, keepdims=True)
        dxc_next_s[...] = dxc[:SUBLANES, :]
        dxl_ref[...] = dxl.astype(BF16)

        acc_ref[4:5, :] += jnp.sum(dxc, axis=0, keepdims=True)
        acc_ref[5:6, :] += jnp.sum(dpr, axis=0, keepdims=True)
        acc_ref[6:7, :] += jnp.sum(dpi, axis=0, keepdims=True)
        acc_ref[7:8, :] += jnp.sum(dla * ((-LRU_C) * r), axis=0, keepdims=True)

        @pl.when(i == nt - 1)
        def _():
            lam_v = lam_ref[...]
            acc_ref[7:8, :] = acc_ref[7:8, :] * (-_sigmoid(-lam_v))

    rev = pl.BlockSpec((tm, D), lambda i: (nt - 1 - i, 0))
    prev8 = pl.BlockSpec((SUBLANES, D), lambda i: (jnp.maximum((nt - 1 - i) * per - 1, 0), 0))
    vec = _const_spec((1, D))
    wspec = _const_spec((NB, LANES, LANES))
    bf = jax.ShapeDtypeStruct((T, D), BF16)
    return _call(
        body, name="lru_bwd", grid=(nt,),
        in_specs=[rev, rev, prev8, rev, rev, prev8, _const_spec((4, D)), wspec, vec, wspec, vec, vec],
        out_specs=[rev, wspec, wspec, _const_spec((SUBLANES, D))],
        out_shape=[bf, jax.ShapeDtypeStruct((NB, LANES, LANES), F32), jax.ShapeDtypeStruct((NB, LANES, LANES), F32),
                   jax.ShapeDtypeStruct((SUBLANES, D), F32)],
        scratch_shapes=[pltpu.VMEM((1, D), F32), pltpu.VMEM((SUBLANES, D), F32), pltpu.VMEM((SUBLANES, D), F32),
                        pltpu.VMEM((tm, D), F32)],
        compiler_params=_cparams(("arbitrary",)),
    )(dh, h, h, xc, x_lru, x_lru, conv_w, w_r, b_r, w_i, b_i, lam)


def _attn_bwd(q_aug, qx, k_aug, v_aug, do_aug):
    T = q_aug.shape[0]
    t = TA
    n = T // t
    hp = BWD_HEADS
    heads = range(hp)
    scale = DH ** -0.5
    ki_tab, qi_tab = _causal_pairs(n, q_major=False)
    last = ki_tab.shape[0] - 1

    def body(ki_ref, qi_ref, q_ref, qx_ref, k_ref, v_ref, do_ref, dq_ref, dk_ref, dv_ref, dck_ref, dcq_ref,
             dq_s, dk_s, dv_s):
        j = pl.program_id(1)
        ki = ki_ref[j]
        qi = qi_ref[j]

        @pl.when(j == 0)
        def _():
            dq_s[...] = jnp.zeros_like(dq_s)

        @pl.when(qi == ki)
        def _():
            dk_s[...] = jnp.zeros_like(dk_s)
            dv_s[...] = jnp.zeros_like(dv_s)

        def step(on_diagonal):
            cols = [slice(a * AUG, (a + 1) * AUG) for a in heads]
            qb = [jnp.concatenate([q_ref[:, a * AUG:a * AUG + DH], qx_ref[:, a * DH:(a + 1) * DH]], axis=1)
                  for a in heads]
            st = [_dot_nt(k_ref[:, cols[a]], qb[a]) for a in heads]
            if on_diagonal:
                krow = lax.broadcasted_iota(jnp.int32, (t, t), 0)
                qcol = lax.broadcasted_iota(jnp.int32, (t, t), 1)
                st = [jnp.where(krow <= qcol, st[a], NEG) for a in heads]
            pt = [jnp.exp2(st[a]) for a in heads]
            dsb = [(pt[a] * _dot_nt(v_ref[:, cols[a]], do_ref[:, cols[a]])).astype(BF16) for a in heads]
            off = pl.multiple_of(qi * t, t)
            for a in heads:
                dv_s[a] += _dot(pt[a].astype(BF16), do_ref[:, cols[a]])
                dk_s[a] += _dot(dsb[a], qb[a])
                dq_s[a, pl.ds(off, t), :] += _dot_tn(dsb[a], k_ref[:, cols[a]])

        @pl.when(qi > ki)
        def _():
            step(False)

        @pl.when(qi == ki)
        def _():
            step(True)

        @pl.when(qi == n - 1)
        def _():
            for a in heads:
                dk_ref[:, a * DH:(a + 1) * DH] = (dk_s[a, :, :DH] * LN2).astype(BF16)
                dv_ref[:, a * DH:(a + 1) * DH] = dv_s[a, :, :DH].astype(BF16)
                dck_ref[a] = jnp.broadcast_to(dk_s[a, :, DH + 3:DH + 4], (t, LANES))

        @pl.when(j == last)
        def _():
            for a in heads:
                dq_ref[:, a * DH:(a + 1) * DH] = (dq_s[a, :, :DH] * scale).astype(BF16)
                dcq_ref[a] = jnp.broadcast_to(dq_s[a, :, DH:DH + 1], (T, LANES))

    qside = pl.BlockSpec((t, hp * AUG), lambda h, j, ki_ref, qi_ref: (qi_ref[j], h))
    qxside = pl.BlockSpec((t, hp * DH), lambda h, j, ki_ref, qi_ref: (qi_ref[j], h))
    kside = pl.BlockSpec((t, hp * AUG), lambda h, j, ki_ref, qi_ref: (ki_ref[j], h))
    kout = pl.BlockSpec((t, hp * DH), lambda h, j, ki_ref, qi_ref: (ki_ref[j], h))
    bf = jax.ShapeDtypeStruct((T, D), BF16)
    sums = jax.ShapeDtypeStruct((H, T, LANES), F32)
    grid_spec = pltpu.PrefetchScalarGridSpec(
        num_scalar_prefetch=2, grid=(H // hp, ki_tab.shape[0]),
        in_specs=[qside, qxside, kside, kside, qside],
        out_specs=[pl.BlockSpec((T, hp * DH), lambda h, j, ki_ref, qi_ref: (0, h)), kout, kout,
                   pl.BlockSpec((hp, t, LANES), lambda h, j, ki_ref, qi_ref: (h, ki_ref[j], 0)),
                   pl.BlockSpec((hp, T, LANES), lambda h, j, ki_ref, qi_ref: (h, 0, 0))],
        scratch_shapes=[pltpu.VMEM((hp, T, AUG), F32), pltpu.VMEM((hp, t, AUG), F32), pltpu.VMEM((hp, t, AUG), F32)])
    return _call(
        body, name="attn_bwd", grid_spec=grid_spec,
        out_shape=[bf, bf, bf, sums, sums],
        compiler_params=_cparams(("arbitrary", "arbitrary"), VMEM_BIG),
    )(ki_tab, qi_tab, q_aug, qx, k_aug, v_aug, do_aug)


def _fgate_bwd(dc_key, dc_query, flb):
    T = flb.shape[0]
    tm = TM
    nt = T // tm

    def body(dck_ref, dcq_ref, flb_ref, dfl_ref, acc_ref, carry, top_s):
        @pl.when(pl.program_id(0) == 0)
        def _():
            carry[...] = jnp.zeros_like(carry)
            acc_ref[...] = jnp.zeros_like(acc_ref)

        flb = flb_ref[...]
        lane = lax.broadcasted_iota(jnp.int32, flb.shape, 1)
        dc = jnp.zeros(flb.shape, F32)
        for hd in range(H):
            dc = dc + jnp.where(lane == hd, dcq_ref[hd] - dck_ref[hd], 0.0)
        r = lax.broadcasted_iota(jnp.int32, (tm, tm), 0)
        c = lax.broadcasted_iota(jnp.int32, (tm, tm), 1)
        dls = _dot_exact((c >= r).astype(F32), dc) + carry[...]
        top_s[...] = dls[:SUBLANES, :]
        carry[...] = top_s[0:1, :]
        dfl = jnp.where(lane < H, dls * _sigmoid(-flb), 0.0)
        dfl_ref[...] = dfl.astype(BF16)
        acc_ref[0:1, :] += jnp.sum(dfl, axis=0, keepdims=True)

    rev = pl.BlockSpec((tm, LANES), lambda i: (nt - 1 - i, 0))
    return _call(
        body, name="fgate_bwd", grid=(nt,),
        in_specs=[pl.BlockSpec((H, tm, LANES), lambda i: (0, nt - 1 - i, 0))] * 2 + [rev],
        out_specs=[rev, _const_spec((SUBLANES, LANES))],
        out_shape=[jax.ShapeDtypeStruct((T, LANES), BF16), jax.ShapeDtypeStruct((SUBLANES, LANES), F32)],
        scratch_shapes=[pltpu.VMEM((1, LANES), F32), pltpu.VMEM((SUBLANES, LANES), F32)],
        compiler_params=_cparams(("arbitrary",)),
    )(dc_key, dc_query, flb)


def _dx(dz, dfl, w_a, w_f, w_b, x, pre_gain, dh1):
    T = x.shape[0]
    tm = TM

    def body(*refs):
        dz_refs = refs[:6]
        dfl_ref, wa_ref, wf_ref, wb_ref, x_ref, g_ref, dh1_ref, gx_ref, acc_ref = refs[6:]

        @pl.when(pl.program_id(0) == 0)
        def _():
            acc_ref[...] = jnp.zeros_like(acc_ref)

        dxn = _dot(dfl_ref[...], wf_ref[...])
        for s in range(3):
            dxn = dxn + _dot(dz_refs[s][...], wa_ref[s * D:(s + 1) * D, :])
            dxn = dxn + _dot(dz_refs[3 + s][...], wb_ref[s * D:(s + 1) * D, :])
        xv = x_ref[...]
        rstd = _rstd(xv)
        xhat = xv * rstd
        gx_ref[...] = dh1_ref[...] + _rms_bwd(dxn * g_ref[...], xhat, rstd)
        acc_ref[0:1, :] += jnp.sum(dxn * xhat, axis=0, keepdims=True)

    return _call(
        body, name="dx", grid=(T // tm,),
        in_specs=[_row_spec(tm, D)] * 6 + [_row_spec(tm, LANES), _const_spec((3 * D, D)), _const_spec((LANES, D)),
                                           _const_spec((3 * D, D)), _row_spec(tm, D), _const_spec((1, D)),
                                           _row_spec(tm, D)],
        out_specs=[_row_spec(tm, D), _const_spec((SUBLANES, D))],
        out_shape=[jax.ShapeDtypeStruct((T, D), F32), jax.ShapeDtypeStruct((SUBLANES, D), F32)],
        compiler_params=_cparams(("arbitrary",), VMEM_BIG),
    )(*dz, dfl, w_a, w_f, w_b, x, pre_gain, dh1)


GRAD_ROWS = D_IN + SUBLANES


def _dw_in_segment(dz_s, xn, buf, s, bt):
    T = xn.shape[0]
    row0 = s * D + (H if s >= 3 else 0)

    def body(*refs):
        dz_ref, xn_ref, o_ref = refs[0], refs[1], refs[-1]

        @pl.when(pl.program_id(0) == 0)
        def _():
            o_ref[...] = jnp.zeros_like(o_ref)

        o_ref[...] += _dot_tn(dz_ref[...], xn_ref[...])

    tok = pl.BlockSpec((bt, D), lambda t: (t, 0))
    return _call(
        body, name="dw_in_%d" % s, grid=(T // bt,),
        in_specs=[tok, tok] + ([] if buf is None else [pl.BlockSpec(memory_space=pl.ANY)]),
        out_specs=pl.BlockSpec((pl.Element(D), pl.Element(D)), lambda t: (row0, 0)),
        out_shape=jax.ShapeDtypeStruct((GRAD_ROWS, D), F32),
        input_output_aliases={} if buf is None else {2: 0},
        compiler_params=_cparams(("arbitrary",)),
    )(*((dz_s, xn) if buf is None else (dz_s, xn, buf)))


def _dw_in_t(dz, dfl, xn, bt=512):
    T = xn.shape[0]
    nt = T // bt
    main = None
    for s in range(6):
        main = _dw_in_segment(dz[s], xn, main, s, min(T, 2048))

    def f_body(dfl_ref, xn_ref, main_ref, o_ref, acc_s):
        p = pl.program_id(0)
        t = pl.program_id(1)

        @pl.when(t == 0)
        def _():
            acc_s[...] = jnp.zeros_like(acc_s)

        @pl.when(p == 0)
        def _():
            acc_s[...] += _dot_tn(dfl_ref[...], xn_ref[...])

        @pl.when(t == nt - 1)
        def _():
            o_ref[...] = acc_s[:SUBLANES, :]

    fl_block = FL0 // SUBLANES
    end_block = D_IN // SUBLANES
    return _call(
        f_body, name="dw_in_f", grid=(2, nt),
        in_specs=[pl.BlockSpec((bt, LANES), lambda p, t: (t, 0)), pl.BlockSpec((bt, D), lambda p, t: (t, 0)),
                  pl.BlockSpec(memory_space=pl.ANY)],
        out_specs=pl.BlockSpec((SUBLANES, D), lambda p, t: (fl_block + p * (end_block - fl_block), 0)),
        out_shape=jax.ShapeDtypeStruct((GRAD_ROWS, D), F32),
        scratch_shapes=[pltpu.VMEM((LANES, D), F32)],
        input_output_aliases={2: 0},
        compiler_params=_cparams(("arbitrary", "arbitrary")),
    )(dfl, xn, main)


def _matmul_tn(a, b, name, bm=512, bn=1024, bt=2048):
    T, M = a.shape
    N = b.shape[1]
    bm, bn, bt = min(bm, M), min(bn, N), min(bt, T)

    def body(a_ref, b_ref, o_ref):
        @pl.when(pl.program_id(2) == 0)
        def _():
            o_ref[...] = jnp.zeros_like(o_ref)

        o_ref[...] += _dot_tn(a_ref[...], b_ref[...])

    return _call(
        body, name=name, grid=(M // bm, N // bn, T // bt),
        in_specs=[pl.BlockSpec((bt, bm), lambda i, j, t: (t, i)), pl.BlockSpec((bt, bn), lambda i, j, t: (t, j))],
        out_specs=pl.BlockSpec((bm, bn), lambda i, j, t: (i, j)),
        out_shape=jax.ShapeDtypeStruct((M, N), F32),
        compiler_params=_cparams(("parallel", "parallel", "arbitrary")),
    )(a, b)


HBM_SPEC = pl.BlockSpec(memory_space=pltpu.HBM)
VMEM_SPEC = pl.BlockSpec(memory_space=pltpu.VMEM)


def _position():
    return lax.axis_index("x"), lax.axis_index("y"), lax.axis_index("c")


def _other_chips(x, y):
    return [(1 - x, y), (x, 1 - y), (1 - x, 1 - y)]


def _gather_shards(shards, whole):
    na, nw = len(shards), len(whole)
    nall = na + nw

    def body(*refs):
        srcs, dsts = refs[:nall], refs[nall:2 * nall]
        ici_send, ici_recv, d2d_send, d2d_recv = refs[2 * nall:]
        x, y, c = _position()
        chip = 2 * x + y
        chips = _other_chips(x, y)

        def half(a, which):
            rows = srcs[a].shape[0] // 2
            return pl.ds(pl.multiple_of(which * rows, 16), rows)

        first = []
        for j, (px, py) in enumerate(chips):
            for a in range(nall):
                src = srcs[a].at[half(a, c), :] if a < na else srcs[a]
                dst = dsts[a].at[chip, half(a, c), :] if a < na else dsts[a].at[chip]
                first.append(pltpu.make_async_remote_copy(
                    src_ref=src, dst_ref=dst, send_sem=ici_send.at[j * nall + a], recv_sem=ici_recv.at[j * nall + a],
                    device_id=(px, py, c), device_id_type=MESH))
        for cp in first:
            cp.start()

        passed = []
        for j, (px, py) in enumerate(chips):
            theirs = 2 * px + py
            for a in range(nall):
                if a < na:
                    landed = dsts[a].at[theirs, half(a, c), :]
                    fwd = pltpu.make_async_remote_copy(
                        src_ref=landed, dst_ref=landed, send_sem=d2d_send.at[j * na + a],
                        recv_sem=d2d_recv.at[j * na + a], device_id=(x, y, 1 - c), device_id_type=MESH)
                else:
                    landed = dsts[a].at[theirs]
                pltpu.make_async_remote_copy(
                    src_ref=landed, dst_ref=landed, send_sem=ici_send.at[j * nall + a],
                    recv_sem=ici_recv.at[j * nall + a], device_id=(px, py, c), device_id_type=MESH).wait_recv()
                if a < na:
                    fwd.start()
                    passed.append(fwd)
        for j, (px, py) in enumerate(chips):
            theirs = 2 * px + py
            for a in range(na):
                other = dsts[a].at[theirs, half(a, 1 - c), :]
                pltpu.make_async_remote_copy(
                    src_ref=other, dst_ref=other, send_sem=d2d_send.at[j * na + a], recv_sem=d2d_recv.at[j * na + a],
                    device_id=(x, y, 1 - c), device_id_type=MESH).wait_recv()
        for cp in first + passed:
            cp.wait_send()

    arrs = list(shards) + list(whole)
    outs = _call(
        body, name="gather_shards",
        in_specs=[HBM_SPEC] * nall, out_specs=[HBM_SPEC] * nall,
        out_shape=[jax.ShapeDtypeStruct((N_CHIPS,) + s.shape, s.dtype) for s in arrs],
        scratch_shapes=[pltpu.SemaphoreType.DMA((3 * nall,)), pltpu.SemaphoreType.DMA((3 * nall,)),
                        pltpu.SemaphoreType.DMA((3 * na,)), pltpu.SemaphoreType.DMA((3 * na,))],
    )(*arrs)
    chip = 2 * lax.axis_index("x") + lax.axis_index("y")
    return [lax.dynamic_update_slice(o, a[None], (chip,) + (0,) * a.ndim) for o, a in zip(outs, arrs)]


W_ROWS = 1568
G_ROWS = 1552
SHARD_ROWS = D_IN // N_CHIPS
WINDOW_STEP = 1536


def _assemble_w_in(cont):
    cb = 256
    half = WINDOW_STEP

    def body(c_ref, wa_ref, wf_ref, wb_ref):
        x0 = c_ref[0].astype(F32)
        x1, x2, x3 = (pltpu.roll(c_ref[j].astype(F32), 2 * j, 0) for j in (1, 2, 3))
        wa = jnp.concatenate([x0[:half], x0[half:half + 16] + x1[:16], x1[16:half]], axis=0)
        wa_ref[...] = wa.astype(BF16)

        fl = x1[half:half + 16] + x2[:16]
        row = lax.broadcasted_iota(jnp.int32, fl.shape, 0)
        wf_ref[:16, :] = jnp.where(row < H, fl, 0.0).astype(BF16)
        wf_ref[16:, :] = jnp.zeros((LANES - 16, cb), BF16)

        mid = x2[half:half + SUBLANES] + x3[:SUBLANES]
        wb = jnp.concatenate([x2[SUBLANES:half], mid, x3[SUBLANES:half + SUBLANES]], axis=0)
        wb_ref[...] = wb.astype(BF16)

    return _call(
        body, name="assemble_w_in", grid=(D // cb,),
        in_specs=[pl.BlockSpec((N_CHIPS, W_ROWS, cb), lambda i: (0, 0, i))],
        out_specs=[pl.BlockSpec((3 * D, cb), lambda i: (0, i)), pl.BlockSpec((LANES, cb), lambda i: (0, i)),
                   pl.BlockSpec((3 * D, cb), lambda i: (0, i))],
        out_shape=[jax.ShapeDtypeStruct((3 * D, D), BF16), jax.ShapeDtypeStruct((LANES, D), BF16),
                   jax.ShapeDtypeStruct((3 * D, D), BF16)],
        compiler_params=_cparams(("parallel",)),
    )(cont)


def _pair_exchange(grad_t, parts):
    na = len(parts)
    n = N_CHIPS + na
    half_g = G_ROWS // 2

    def body(*refs):
        g_ref, srcs, got = refs[0], refs[1:1 + na], refs[1 + na:2 + 2 * na]
        send_sems, recv_sems = refs[2 + 2 * na:]
        x, y, c = _position()
        pieces = []
        for j in range(N_CHIPS):
            rows = pl.ds(pl.multiple_of(j * WINDOW_STEP + (1 - c) * half_g, SUBLANES), half_g)
            pieces.append((g_ref.at[rows, :], got[0].at[j]))
        for a in range(na):
            half = srcs[a].shape[1] // 2
            rows = pl.ds(pl.multiple_of((1 - c) * half, SUBLANES), half)
            pieces.append((srcs[a].at[:, rows, :], got[1 + a]))
        copies = [pltpu.make_async_remote_copy(
            src_ref=give, dst_ref=dst, send_sem=send_sems.at[k], recv_sem=recv_sems.at[k],
            device_id=(x, y, 1 - c), device_id_type=MESH) for k, (give, dst) in enumerate(pieces)]
        for cp in copies:
            cp.start()
        for cp in copies:
            cp.wait()

    halves = [jax.ShapeDtypeStruct((N_CHIPS, half_g, D), F32)]
    halves += [jax.ShapeDtypeStruct((s.shape[0], s.shape[1] // 2, s.shape[2]), s.dtype) for s in parts]
    return _call(
        body, name="pair_exchange",
        in_specs=[HBM_SPEC] * (1 + na), out_specs=[HBM_SPEC] * (1 + na),
        out_shape=halves,
        scratch_shapes=[pltpu.SemaphoreType.DMA((n,)), pltpu.SemaphoreType.DMA((n,))],
    )(grad_t, *parts)


def _pair_sum(part, got, c, name):
    _, half, C = got.shape
    cb = min(C, 256)

    def body(c_ref, a_ref, b_ref, o_ref):
        o_ref[...] = (a_ref[...] + b_ref[...]).astype(BF16)

    spec = pl.BlockSpec((1, half, cb), lambda j, i, c_ref: (j, 0, i))
    grid_spec = pltpu.PrefetchScalarGridSpec(
        num_scalar_prefetch=1, grid=(N_CHIPS, C // cb),
        in_specs=[pl.BlockSpec((1, half, cb), lambda j, i, c_ref: (j, c_ref[0], i)), spec], out_specs=spec)
    return _call(
        body, name=name, grid_spec=grid_spec,
        out_shape=jax.ShapeDtypeStruct((N_CHIPS, half, C), BF16),
        compiler_params=_cparams(("parallel", "parallel")),
    )(c.reshape(1), part, got)


def _pair_sum_windows(grad_t, got, c):
    _, half, C = got.shape
    cb = 256

    def body(c_ref, a_ref, b_ref, o_ref):
        o_ref[0] = (a_ref[...] + b_ref[0]).astype(BF16)

    def mine(j, i, c_ref):
        return ((j * (WINDOW_STEP // SUBLANES) + c_ref[0] * (half // SUBLANES)) * SUBLANES, i * cb)

    spec = pl.BlockSpec((1, half, cb), lambda j, i, c_ref: (j, 0, i))
    grid_spec = pltpu.PrefetchScalarGridSpec(
        num_scalar_prefetch=1, grid=(N_CHIPS, C // cb),
        in_specs=[pl.BlockSpec((pl.Element(half), pl.Element(cb)), mine), spec], out_specs=spec)
    return _call(
        body, name="pair_sum_w_in", grid_spec=grid_spec,
        out_shape=jax.ShapeDtypeStruct((N_CHIPS, half, C), BF16),
        compiler_params=_cparams(("parallel", "parallel")),
    )(c.reshape(1), grad_t, got)


def _chip_exchange(sums):
    na = len(sums)

    def body(*refs):
        srcs, dsts = refs[:na], refs[na:2 * na]
        send_sems, recv_sems = refs[2 * na:]
        x, y, c = _position()
        chip = 2 * x + y
        copies = []
        for j, (px, py) in enumerate(_other_chips(x, y)):
            for a in range(na):
                copies.append(pltpu.make_async_remote_copy(
                    src_ref=srcs[a].at[2 * px + py], dst_ref=dsts[a].at[chip], send_sem=send_sems.at[j * na + a],
                    recv_sem=recv_sems.at[j * na + a], device_id=(px, py, c), device_id_type=MESH))
        for cp in copies:
            cp.start()
        for cp in copies:
            cp.wait()

    return _call(
        body, name="chip_exchange",
        in_specs=[HBM_SPEC] * na, out_specs=[HBM_SPEC] * na,
        out_shape=[jax.ShapeDtypeStruct(s.shape, s.dtype) for s in sums],
        scratch_shapes=[pltpu.SemaphoreType.DMA((3 * na,)), pltpu.SemaphoreType.DMA((3 * na,))],
    )(*sums)


def _chip_sum(own, got, chip, name):
    _, half, C = got.shape
    cb = min(C, 256)

    def body(chip_ref, own_ref, g_ref, o_ref):
        for me in range(N_CHIPS):
            @pl.when(chip_ref[0] == me)
            def _(me=me):
                terms = [own_ref[0] if k == me else g_ref[k] for k in range(N_CHIPS)]
                acc = terms[0].astype(F32) + terms[1].astype(F32)
                acc = acc + terms[2].astype(F32)
                o_ref[...] = acc + terms[3].astype(F32)

    grid_spec = pltpu.PrefetchScalarGridSpec(
        num_scalar_prefetch=1, grid=(C // cb,),
        in_specs=[pl.BlockSpec((1, half, cb), lambda i, chip_ref: (chip_ref[0], 0, i)),
                  pl.BlockSpec((N_CHIPS, half, cb), lambda i, chip_ref: (0, 0, i))],
        out_specs=pl.BlockSpec((half, cb), lambda i, chip_ref: (0, i)))
    return _call(
        body, name=name, grid_spec=grid_spec,
        out_shape=jax.ShapeDtypeStruct((half, C), F32),
        compiler_params=_cparams(("parallel",)),
    )(chip.reshape(1), own, got)


def _pair_swap(halves):
    na = len(halves)

    def body(*refs):
        srcs, dsts = refs[:na], refs[na:2 * na]
        send_sems, recv_sems = refs[2 * na:]
        x, y, c = _position()
        copies = [pltpu.make_async_remote_copy(
            src_ref=srcs[a], dst_ref=dsts[a], send_sem=send_sems.at[a], recv_sem=recv_sems.at[a],
            device_id=(x, y, 1 - c), device_id_type=MESH) for a in range(na)]
        for cp in copies:
            cp.start()
        for cp in copies:
            cp.wait()

    return _call(
        body, name="pair_swap",
        in_specs=[HBM_SPEC] * na, out_specs=[HBM_SPEC] * na,
        out_shape=[jax.ShapeDtypeStruct(s.shape, s.dtype) for s in halves],
        scratch_shapes=[pltpu.SemaphoreType.DMA((na,)), pltpu.SemaphoreType.DMA((na,))],
    )(*halves)


def _allreduce_small(g):
    rows = g.shape[0]
    per = rows // N_DEV

    def body(g_ref, out_ref, got_ref, s1, r1, s2, r2):
        x, y, c = _position()
        me = 4 * x + 2 * y + c
        mine = pl.ds(pl.multiple_of(me * per, SUBLANES), per)
        peers = []
        for j in range(1, N_DEV):
            px = 1 - x if j & 4 else x
            py = 1 - y if j & 2 else y
            pc = 1 - c if j & 1 else c
            peers.append((px, py, pc))

        first = []
        for j, (px, py, pc) in enumerate(peers):
            theirs = pl.ds(pl.multiple_of((4 * px + 2 * py + pc) * per, SUBLANES), per)
            first.append(pltpu.make_async_remote_copy(
                src_ref=g_ref.at[theirs, :], dst_ref=got_ref.at[me], send_sem=s1.at[j], recv_sem=r1.at[j],
                device_id=(px, py, pc), device_id_type=MESH))
        for cp in first:
            cp.start()
        got_ref[me] = g_ref[mine, :]
        for cp in first:
            cp.wait()
        total = got_ref[0]
        for d in range(1, N_DEV):
            total = total + got_ref[d]
        out_ref[mine, :] = total

        second = []
        for j, peer in enumerate(peers):
            second.append(pltpu.make_async_remote_copy(
                src_ref=out_ref.at[mine, :], dst_ref=out_ref.at[mine, :], send_sem=s2.at[j], recv_sem=r2.at[j],
                device_id=peer, device_id_type=MESH))
        for cp in second:
            cp.start()
        for cp in second:
            cp.wait()

    sems = pltpu.SemaphoreType.DMA((N_DEV - 1,))
    return _call(
        body, name="allreduce_small", in_hbm=False,
        in_specs=[VMEM_SPEC], out_specs=VMEM_SPEC,
        out_shape=jax.ShapeDtypeStruct(g.shape, F32),
        scratch_shapes=[pltpu.VMEM((N_DEV, per, LANES), F32), sems, sems, sems, sems],
    )(g)


def _adamw_math(g, w, m, v):
    m2 = ADAM_B1 * m + (1.0 - ADAM_B1) * g
    v2 = ADAM_B2 * v + (1.0 - ADAM_B2) * (g * g)
    m_hat = m2 / (1.0 - ADAM_B1 ** ADAM_STEP)
    v_hat = v2 / (1.0 - ADAM_B2 ** ADAM_STEP)
    delta = (-ADAM_LR) * (m_hat / (jnp.sqrt(v_hat) + ADAM_EPS) + ADAM_WD * w)
    return delta, m2, v2


def _adamw_big(g, w, m, v, name):
    R, C = g.shape
    cb = min(C, LANES)

    def body(g_ref, w_ref, m_ref, v_ref, d_ref, m2_ref, v2_ref):
        d_ref[...], m2_ref[...], v2_ref[...] = _adamw_math(g_ref[...], w_ref[...], m_ref[...], v_ref[...])

    spec = pl.BlockSpec((R, cb), lambda i: (0, i))
    out = jax.ShapeDtypeStruct((R, C), F32)
    return _call(
        body, name=name, grid=(C // cb,),
        in_specs=[spec] * 4, out_specs=[spec] * 3, out_shape=[out] * 3,
        compiler_params=_cparams(("parallel",)),
    )(g, w, m, v)


def _adamw_small(gs, ws, ms, vs):
    n = len(gs)

    def body(*refs):
        for a in range(n):
            g_ref, w_ref, m_ref, v_ref = (refs[k * n + a] for k in range(4))
            d_ref, m2_ref, v2_ref = (refs[(4 + k) * n + a] for k in range(3))
            d_ref[...], m2_ref[...], v2_ref[...] = _adamw_math(g_ref[...], w_ref[...], m_ref[...], v_ref[...])

    outs = [jax.ShapeDtypeStruct(w.shape, F32) for w in ws]
    specs = [_const_spec(w.shape) for w in ws]
    return _call(
        body, name="adamw_small", grid=(1,),
        in_specs=specs * 4, out_specs=specs * 3, out_shape=outs * 3,
    )(*gs, *ws, *ms, *vs)


def _local_step(x, p, tgt, w_a, w_f, w_b, w_out_b, w_ple_b, w_gate_b, conv_w, b_f, pre_gain, post_gain, conv_b,
                w_rgate, b_rgate, w_igate, b_igate, lam, gain_a, gain_l, ple_gain, b_gate):
    b_f_pad = jnp.pad(b_f, ((0, 0), (0, LANES - H)))
    w_r = w_rgate.astype(BF16)
    w_i = w_igate.astype(BF16)

    xn, q_aug, k_aug, v_aug, g_attn, x_lru, g_lru, flb = _in_proj(x, pre_gain, w_a, w_f, w_b, b_f_pad)
    o, qx = _attn_fwd(q_aug, k_aug, v_aug)
    ycat, xc, h = _branches_fwd(o, g_attn, x_lru, g_lru, gain_a, gain_l, conv_w, conv_b, w_r, b_rgate, w_i, b_igate,
                                lam)
    dh1, dycat, dmix, h1b, dgp, pb, dpe, acc_t = _tail(ycat, x, p, tgt, w_out_b, post_gain, w_ple_b, ple_gain,
                                                       w_gate_b, b_gate)
    do_aug, dg_attn, dg_lru, dh, acc_b = _branches_bwd(dycat, o, g_attn, h, g_lru, gain_a, gain_l)
    dx_lru, gw_r, gw_i, acc_l = _lru_bwd(dh, h, xc, x_lru, conv_w, w_r, b_rgate, w_i, b_igate, lam)
    dq, dk, dv, dc_key, dc_query = _attn_bwd(q_aug, qx, k_aug, v_aug, do_aug)
    dfl, acc_f = _fgate_bwd(dc_key, dc_query, flb)
    dz = (dq, dk, dv, dg_attn, dx_lru, dg_lru)
    grad_x, acc_x = _dx(dz, dfl, w_a, w_f, w_b, x, pre_gain, dh1)

    grads = dict(
        w_in_t=_dw_in_t(dz, dfl, xn),
        w_out=_matmul_tn(ycat, dmix, "dw_out"),
        w_ple=_matmul_tn(pb, dpe, "dw_ple"),
        w_ple_gate=_matmul_tn(h1b, dgp, "dw_ple_gate"),
        w_rgate=gw_r,
        w_igate=gw_i,
        b_f=acc_f[0:1, :H],
        pre_gain=acc_x[0:1],
        post_gain=acc_t[0:1],
        conv_w=acc_l[0:4],
        conv_b=acc_l[4:5],
        b_rgate=acc_l[5:6],
        b_igate=acc_l[6:7],
        lru_lambda=acc_l[7:8],
        attn_out_gain=acc_b[0:1],
        lru_out_gain=acc_b[1:2],
        ple_gain=acc_t[1:2],
        b_ple_gate=acc_t[2:3],
    )
    loss = jnp.sum(acc_t[3])
    return loss, grad_x, grads


SMALL_ROWS = ["b_f", "pre_gain", "post_gain", "conv_w", "conv_b", "b_rgate", "b_igate", "lru_lambda",
              "attn_out_gain", "lru_out_gain", "ple_gain", "b_ple_gate"]
WEIGHTS = ["w_in", "b_f", "pre_gain", "post_gain", "conv_w", "conv_b", "w_rgate", "b_rgate", "w_igate", "b_igate",
           "lru_lambda", "attn_out_gain", "lru_out_gain", "w_out", "w_ple", "ple_gain", "w_ple_gate", "b_ple_gate"]
SHARDED = ["w_in", "w_out", "w_ple", "w_ple_gate"]


def _by_chip_cols(g):
    r, cols = g.shape
    return g.reshape(r, N_CHIPS, cols // N_CHIPS).transpose(1, 0, 2)


def _from_chip_cols(s):
    n, r, cols = s.shape
    return s.transpose(1, 0, 2).reshape(r, n * cols)


def kernel(x, p, w_in, b_f, pre_gain, post_gain, conv_w, conv_b, w_rgate, b_rgate, w_igate, b_igate, lru_lambda, attn_out_gain, lru_out_gain, w_out, w_ple, ple_gain, w_ple_gate, b_ple_gate, loss_target, m_w_in, m_b_f, m_pre_gain, m_post_gain, m_conv_w, m_conv_b, m_w_rgate, m_b_rgate, m_w_igate, m_b_igate, m_lru_lambda, m_attn_out_gain, m_lru_out_gain, m_w_out, m_w_ple, m_ple_gain, m_w_ple_gate, m_b_ple_gate, v_w_in, v_b_f, v_pre_gain, v_post_gain, v_conv_w, v_conv_b, v_w_rgate, v_b_rgate, v_w_igate, v_b_igate, v_lru_lambda, v_attn_out_gain, v_lru_out_gain, v_w_out, v_w_ple, v_ple_gain, v_w_ple_gate, v_b_ple_gate):
    w = dict(w_in=w_in, b_f=b_f, pre_gain=pre_gain, post_gain=post_gain, conv_w=conv_w, conv_b=conv_b,
             w_rgate=w_rgate, b_rgate=b_rgate, w_igate=w_igate, b_igate=b_igate, lru_lambda=lru_lambda,
             attn_out_gain=attn_out_gain, lru_out_gain=lru_out_gain, w_out=w_out, w_ple=w_ple, ple_gain=ple_gain,
             w_ple_gate=w_ple_gate, b_ple_gate=b_ple_gate)
    m = dict(w_in=m_w_in, b_f=m_b_f, pre_gain=m_pre_gain, post_gain=m_post_gain, conv_w=m_conv_w, conv_b=m_conv_b,
             w_rgate=m_w_rgate, b_rgate=m_b_rgate, w_igate=m_w_igate, b_igate=m_b_igate, lru_lambda=m_lru_lambda,
             attn_out_gain=m_attn_out_gain, lru_out_gain=m_lru_out_gain, w_out=m_w_out, w_ple=m_w_ple,
             ple_gain=m_ple_gain, w_ple_gate=m_w_ple_gate, b_ple_gate=m_b_ple_gate)
    v = dict(w_in=v_w_in, b_f=v_b_f, pre_gain=v_pre_gain, post_gain=v_post_gain, conv_w=v_conv_w, conv_b=v_conv_b,
             w_rgate=v_w_rgate, b_rgate=v_b_rgate, w_igate=v_w_igate, b_igate=v_b_igate, lru_lambda=v_lru_lambda,
             attn_out_gain=v_attn_out_gain, lru_out_gain=v_lru_out_gain, w_out=v_w_out, w_ple=v_w_ple,
             ple_gain=v_ple_gain, w_ple_gate=v_w_ple_gate, b_ple_gate=v_b_ple_gate)
    xi, yi, ci = _position()
    chip = 2 * xi + yi

    w_in_t, m_in_t, v_in_t = (jnp.swapaxes(t[0], 0, 1) for t in (w_in, m_w_in, v_w_in))
    window = jnp.pad(w_in_t.astype(BF16), ((0, W_ROWS - SHARD_ROWS), (0, 0)))

    st_in, st_out, st_ple, st_gate, st_conv = _gather_shards(
        [window, w_out[0].astype(BF16), w_ple[0].astype(BF16), w_ple_gate[0].astype(BF16)], [conv_w[0]])
    w_a, w_f, w_b = _assemble_w_in(st_in)
    w_out_b = st_out.reshape(DMIX, D)
    w_ple_b = _from_chip_cols(st_ple)
    w_gate_b = st_gate.reshape(D, D)
    conv_full = _from_chip_cols(st_conv)

    loss, grad_x, g = _local_step(
        x[0], p[0, 0], loss_target[0], w_a, w_f, w_b, w_out_b, w_ple_b, w_gate_b, conv_full, b_f, pre_gain, post_gain,
        conv_b, w_rgate[0], b_rgate, w_igate[0], b_igate, lru_lambda, attn_out_gain, lru_out_gain, ple_gain,
        b_ple_gate)
    loss = lax.psum(loss, ("x", "y", "c"))

    parts = [g["w_out"].reshape(N_CHIPS, DMIX // N_CHIPS, D), _by_chip_cols(g["w_ple"]),
             g["w_ple_gate"].reshape(N_CHIPS, D // N_CHIPS, D)]
    got = _pair_exchange(g["w_in_t"], parts)
    sums = [_pair_sum_windows(g["w_in_t"], got[0], ci)]
    sums += [_pair_sum(parts[a], got[1 + a], ci, "pair_sum_%d" % a) for a in range(3)]
    recv = _chip_exchange(sums)
    halves = [_chip_sum(sums[a], recv[a], chip, "chip_sum_%d" % a) for a in range(4)]
    theirs = _pair_swap(halves)
    full = [jnp.concatenate([jnp.where(ci == 0, a, b), jnp.where(ci == 0, b, a)], axis=0)
            for a, b in zip(halves, theirs)]
    red = dict(zip(SHARDED, full))
    red["w_in"] = lax.dynamic_slice_in_dim(red["w_in"], 2 * chip, SHARD_ROWS, axis=0)

    rows = [jnp.pad(g["b_f"], ((0, 0), (0, D - H)))] + [g[n] for n in SMALL_ROWS[1:]]
    rows.append(jnp.zeros((16 - sum(r.shape[0] for r in rows), D), F32))
    packed = jnp.concatenate([g["w_rgate"].reshape(NB * LANES, LANES), g["w_igate"].reshape(NB * LANES, LANES),
                              jnp.concatenate(rows, axis=0).reshape(LANES, LANES)], axis=0)
    summed = _allreduce_small(packed)
    red["w_rgate"] = summed[:D].reshape(1, NB, LANES, LANES)
    red["w_igate"] = summed[D:2 * D].reshape(1, NB, LANES, LANES)
    vec = summed[2 * D:].reshape(16, D)
    r0 = 0
    for n in SMALL_ROWS:
        nr = 4 if n == "conv_w" else 1
        red[n] = vec[r0:r0 + nr]
        r0 += nr
    red["b_f"] = red["b_f"][:, :H]
    red["conv_w"] = lax.dynamic_slice_in_dim(red["conv_w"], chip * (D // N_CHIPS), D // N_CHIPS, axis=1)[None]

    delta, new_m, new_v = {}, {}, {}
    outs_in = _adamw_big(red["w_in"], w_in_t, m_in_t, v_in_t, "adamw_w_in")
    delta["w_in"], new_m["w_in"], new_v["w_in"] = (jnp.swapaxes(t, 0, 1)[None] for t in outs_in)
    red["w_in"] = jnp.swapaxes(red["w_in"], 0, 1)[None]
    for n in SHARDED[1:]:
        delta[n], new_m[n], new_v[n] = (t[None] for t in _adamw_big(red[n], w[n][0], m[n][0], v[n][0], "adamw_" + n))
        red[n] = red[n][None]
    small = [n for n in WEIGHTS if n not in SHARDED]
    outs = _adamw_small([red[n] for n in small], [w[n] for n in small], [m[n] for n in small],
                        [v[n] for n in small])
    ns = len(small)
    for a, n in enumerate(small):
        delta[n], new_m[n], new_v[n] = outs[a], outs[ns + a], outs[2 * ns + a]

    return (loss, grad_x[None], *[red[n] for n in WEIGHTS], *[delta[n] for n in WEIGHTS],
            *[new_m[n] for n in WEIGHTS], *[new_v[n] for n in WEIGHTS])
```

```python
import functools

import jax
import jax.numpy as jnp
import numpy as np
from jax import lax
from jax.experimental import pallas as pl
from jax.experimental.pallas import tpu as pltpu

F32 = jnp.float32
BF16 = jnp.bfloat16

D = 1024
H = 8
DH = 128
NB = 8
DPLE = 256
DMIX = 2 * D
D_IN = 4 * D + H + 2 * D
FL0 = 3 * D
RMS_EPS = 1e-6
LRU_C = 8.0
NEG = -1e30
LANES = 128
SUBLANES = 8

ADAM_LR = 0.001
ADAM_B1 = 0.9
ADAM_B2 = 0.999
ADAM_EPS = 1e-08
ADAM_WD = 0.01
ADAM_STEP = 10

TM = 256
TA = 512
FWD_HEADS = 4
BWD_HEADS = 2
VMEM_BIG = 56 * 1024 * 1024
VMEM_MID = 40 * 1024 * 1024

MESH = pl.DeviceIdType.MESH
N_CHIPS = 4
N_DEV = 8


def _call(body, *, out_shape, in_hbm=True, **kwargs):
    if not in_hbm:
        return pl.pallas_call(body, out_shape=out_shape, **kwargs)

    def pin(shape):
        return pltpu.HBM(shape.shape, shape.dtype) if isinstance(shape, jax.ShapeDtypeStruct) else shape

    fn = pl.pallas_call(body, out_shape=jax.tree.map(pin, out_shape), **kwargs)

    def run(*args):
        return fn(*[a if a.dtype == jnp.int32 else pltpu.with_memory_space_constraint(a, pltpu.HBM) for a in args])

    return run


def _cparams(sem, vmem=VMEM_MID):
    return pltpu.CompilerParams(dimension_semantics=sem, vmem_limit_bytes=vmem)


def _sigmoid(x):
    return 0.5 * jnp.tanh(0.5 * x) + 0.5


def _rstd(x):
    return lax.rsqrt(jnp.mean(x * x, axis=-1, keepdims=True) + RMS_EPS)


def _rms_bwd(t, xhat, rstd):
    return rstd * (t - xhat * jnp.mean(t * xhat, axis=-1, keepdims=True))


def _dot(a, b):
    return jnp.dot(a, b, preferred_element_type=F32)


def _dot_nt(a, b):
    return lax.dot_general(a, b, (((1,), (1,)), ((), ())), preferred_element_type=F32)


def _dot_tn(a, b):
    return lax.dot_general(a, b, (((0,), (0,)), ((), ())), preferred_element_type=F32)


def _dot_exact(a, b):
    return jnp.dot(a, b, preferred_element_type=F32, precision=lax.Precision.HIGHEST)


def _shift_down(x, j, halo):
    rolled = pltpu.roll(x, j, 0)
    row = lax.broadcasted_iota(jnp.int32, halo.shape, 0)
    top = jnp.where(row < j, pltpu.roll(halo, j, 0), rolled[:SUBLANES])
    return jnp.concatenate([top, rolled[SUBLANES:]], axis=0)


def _shift_up(x, j, nxt):
    tm = x.shape[0]
    rolled = pltpu.roll(x, tm - j, 0)
    row = lax.broadcasted_iota(jnp.int32, nxt.shape, 0)
    bot = jnp.where(row >= SUBLANES - j, pltpu.roll(nxt, SUBLANES - j, 0), rolled[tm - SUBLANES:])
    return jnp.concatenate([rolled[:tm - SUBLANES], bot], axis=0)


def _scan_fwd_into(a, u, carry, h_ref):
    tm = a.shape[0]
    sub = lax.broadcasted_iota(jnp.int32, a.shape, 0) & (SUBLANES - 1)
    d = 1
    while d < SUBLANES:
        keep = sub >= d
        a_s = jnp.where(keep, pltpu.roll(a, d, 0), 1.0)
        u_s = jnp.where(keep, pltpu.roll(u, d, 0), 0.0)
        u = u + a * u_s
        a = a * a_s
        d *= 2
    for g in range(tm // SUBLANES):
        rows = slice(g * SUBLANES, (g + 1) * SUBLANES)
        h_ref[rows, :] = u[rows] + a[rows] * carry
        carry = h_ref[(g + 1) * SUBLANES - 1:(g + 1) * SUBLANES, :]
    return carry


def _scan_bwd_into(b, u, g_ref):
    tm = b.shape[0]
    sub = lax.broadcasted_iota(jnp.int32, b.shape, 0) & (SUBLANES - 1)
    d = 1
    while d < SUBLANES:
        keep = sub < SUBLANES - d
        b_s = jnp.where(keep, pltpu.roll(b, tm - d, 0), 1.0)
        u_s = jnp.where(keep, pltpu.roll(u, tm - d, 0), 0.0)
        u = u + b * u_s
        b = b * b_s
        d *= 2
    nxt = jnp.zeros((1, b.shape[1]), F32)
    for g in reversed(range(tm // SUBLANES)):
        rows = slice(g * SUBLANES, (g + 1) * SUBLANES)
        g_ref[rows, :] = u[rows] + b[rows] * nxt
        nxt = g_ref[g * SUBLANES:g * SUBLANES + 1, :]


def _gate_pre(xc, w_ref):
    outs = []
    for n in range(NB):
        outs.append(_dot(xc[:, n * LANES:(n + 1) * LANES].astype(BF16), w_ref[n]))
    return jnp.concatenate(outs, axis=1)


def _gate_pre_t(d, w_ref):
    outs = []
    for n in range(NB):
        outs.append(_dot_nt(d[:, n * LANES:(n + 1) * LANES].astype(BF16), w_ref[n]))
    return jnp.concatenate(outs, axis=1)


def _softplus_neg(lam):
    return jnp.maximum(-lam, 0.0) + jnp.log(1.0 + jnp.exp(-jnp.abs(lam)))


def _row_spec(tm, width):
    return pl.BlockSpec((tm, width), lambda i: (i, 0))


def _const_spec(shape):
    nd = len(shape)
    return pl.BlockSpec(shape, lambda *_: (0,) * nd)


AUG = 2 * DH
LOG2E = 1.4426950408889634
LN2 = 0.6931471805599453
Q_SCALE = DH ** -0.5 * LOG2E


def _split3(x):
    hi = x.astype(BF16)
    r1 = x - hi.astype(F32)
    mid = r1.astype(BF16)
    lo = (r1 - mid.astype(F32)).astype(BF16)
    return hi, mid, lo


def _extras(col, ones_from):
    t = col.shape[0]
    hi, mid, lo = _split3(jnp.broadcast_to(col, (t, LANES)))
    lane = lax.broadcasted_iota(jnp.int32, (t, LANES), 1)
    rest = jnp.zeros((t, LANES), BF16)
    if ones_from is not None:
        rest = jnp.where((lane >= ones_from) & (lane < ones_from + 3), 1.0, 0.0).astype(BF16)
    return jnp.where(lane == 0, hi, jnp.where(lane == 1, mid, jnp.where(lane == 2, lo, rest)))


def _selectors():
    sel_q = np.zeros((3 * LANES, H * LANES), np.float32)
    sel_k = np.zeros((3 * LANES, H * LANES), np.float32)
    for hd in range(H):
        for piece in range(3):
            sel_q[piece * LANES + hd, hd * LANES + piece] = 1.0
            sel_k[piece * LANES + hd, hd * LANES + 3 + piece] = -1.0
    return jnp.asarray(sel_q, BF16), jnp.asarray(sel_k, BF16)


def _in_proj(x, pre_gain, w_a, w_f, w_b, b_f_pad):
    T = x.shape[0]
    tm = TM
    sel_q, sel_k = _selectors()

    def body(x_ref, g_ref, wa_ref, wf_ref, wb_ref, bf_ref, sq_ref, sk_ref,
             xn_ref, qa_ref, ka_ref, va_ref, ga_ref, xl_ref, gl_ref, flb_ref, c_s, carry):
        @pl.when(pl.program_id(0) == 0)
        def _():
            carry[...] = jnp.zeros_like(carry)

        xv = x_ref[...]
        xn = (xv * _rstd(xv) * g_ref[...]).astype(BF16)
        xn_ref[...] = xn
        for s, o_ref in enumerate((ga_ref, xl_ref, gl_ref)):
            o_ref[...] = _dot_nt(xn, wb_ref[s * D:(s + 1) * D, :]).astype(o_ref.dtype)
        flb = _dot_nt(xn, wf_ref[...]) + bf_ref[...]
        flb_ref[...] = flb
        lane = lax.broadcasted_iota(jnp.int32, flb.shape, 1)
        ls = jnp.where(lane < H, jnp.minimum(flb, 0.0) - jnp.log(1.0 + jnp.exp(-jnp.abs(flb))), 0.0)
        r = lax.broadcasted_iota(jnp.int32, (tm, tm), 0)
        c = lax.broadcasted_iota(jnp.int32, (tm, tm), 1)
        cs = _dot_exact((c <= r).astype(F32), ls) + carry[...]
        c_s[...] = cs
        carry[...] = c_s[tm - 1:tm, :]

        pieces = jnp.concatenate(_split3(cs * LOG2E), axis=1)
        ones_q = jnp.where((lane >= 3) & (lane < 6), 1.0, 0.0)
        ones_k = jnp.where(lane < 3, 1.0, 0.0)
        zq = _dot_nt(xn, wa_ref[0:D, :]) * Q_SCALE
        zk = _dot_nt(xn, wa_ref[D:2 * D, :])
        zv = _dot_nt(xn, wa_ref[2 * D:3 * D, :])
        ex_q = _dot(pieces, sq_ref[...])
        ex_k = _dot(pieces, sk_ref[...])
        for hd in range(H):
            head = slice(hd * DH, (hd + 1) * DH)
            lo, hi = hd * AUG, hd * AUG + DH
            qa_ref[:, lo:hi] = zq[:, head].astype(BF16)
            qa_ref[:, hi:hi + DH] = (ex_q[:, head] + ones_q).astype(BF16)
            ka_ref[:, lo:hi] = zk[:, head].astype(BF16)
            ka_ref[:, hi:hi + DH] = (ex_k[:, head] + ones_k).astype(BF16)
            va_ref[:, lo:hi] = zv[:, head].astype(BF16)
            va_ref[:, hi:hi + DH] = ones_k.astype(BF16)

    bf = jax.ShapeDtypeStruct((T, D), BF16)
    aug = jax.ShapeDtypeStruct((T, H * AUG), BF16)
    f32 = jax.ShapeDtypeStruct((T, D), F32)
    sel_spec = _const_spec((3 * LANES, H * LANES))
    return _call(
        body, name="in_proj", grid=(T // tm,),
        in_specs=[_row_spec(tm, D), _const_spec((1, D)), _const_spec((3 * D, D)), _const_spec((LANES, D)),
                  _const_spec((3 * D, D)), _const_spec((1, LANES)), sel_spec, sel_spec],
        out_specs=[_row_spec(tm, D)] + [_row_spec(tm, H * AUG)] * 3 + [_row_spec(tm, D)] * 3 + [_row_spec(tm, LANES)],
        out_shape=[bf, aug, aug, aug, f32, f32, f32, jax.ShapeDtypeStruct((T, LANES), F32)],
        scratch_shapes=[pltpu.VMEM((tm, LANES), F32), pltpu.VMEM((1, LANES), F32)],
        compiler_params=_cparams(("arbitrary",), VMEM_BIG),
    )(x, pre_gain, w_a, w_f, w_b, b_f_pad, sel_q, sel_k)


def _causal_pairs(n, q_major):
    if q_major:
        pairs = [(qi, ki) for qi in range(n) for ki in range(qi + 1)]
    else:
        pairs = [(ki, qi) for ki in range(n) for qi in range(ki, n)]
    return (jnp.asarray([a for a, _ in pairs], jnp.int32), jnp.asarray([b for _, b in pairs], jnp.int32))


def _attn_fwd(q_aug, k_aug, v_aug, shards=(), whole=()):
    T = q_aug.shape[0]
    t = TA
    n = T // t
    hp = FWD_HEADS
    heads = range(hp)
    qi_tab, ki_tab = _causal_pairs(n, q_major=True)
    na, nall = len(shards), len(shards) + len(whole)
    n_h, n_j = H // hp, qi_tab.shape[0]

    def body(qi_ref, ki_ref, q_ref, k_ref, v_ref, *rest):
        srcs, rest = rest[:nall], rest[nall:]
        o_ref, qx_ref = rest[:2]
        dsts, rest = rest[2:2 + nall], rest[2 + nall:]
        m_s, acc_s = rest[:2]
        h = pl.program_id(0)
        j = pl.program_id(1)
        qi = qi_ref[j]
        ki = ki_ref[j]

        if nall:
            gather = _GatherPlan(srcs, dsts, rest[2:], na)
            pl.when((h == 0) & (j == 0))(gather.send)
            pl.when((h == n_h - 1) & (j == 0))(gather.forward)
            pl.when((h == n_h - 1) & (j == n_j - 1))(gather.finish)

        @pl.when(ki == 0)
        def _():
            m_s[...] = jnp.full(m_s.shape, NEG, F32)
            acc_s[...] = jnp.zeros_like(acc_s)

        def step(on_diagonal):
            cols = [slice(a * AUG, (a + 1) * AUG) for a in heads]
            s = [_dot_nt(q_ref[:, cols[a]], k_ref[:, cols[a]]) for a in heads]
            if on_diagonal:
                row = lax.broadcasted_iota(jnp.int32, (t, t), 0)
                col = lax.broadcasted_iota(jnp.int32, (t, t), 1)
                s = [jnp.where(col <= row, s[a], NEG) for a in heads]
            m_prev = [m_s[a] for a in heads]
            m_new = [jnp.maximum(m_prev[a], jnp.max(s[a], axis=1, keepdims=True)) for a in heads]
            pr = [jnp.exp2(s[a] - m_new[a]).astype(BF16) for a in heads]
            for a in heads:
                acc_s[a] = jnp.exp2(m_prev[a] - m_new[a]) * acc_s[a] + _dot(pr[a], v_ref[:, cols[a]])
                m_s[a] = m_new[a]

        @pl.when(ki < qi)
        def _():
            step(False)

        @pl.when(ki == qi)
        def _():
            step(True)
            for a in heads:
                acc = acc_s[a]
                l = acc[:, DH:DH + 1]
                o_ref[:, a * DH:(a + 1) * DH] = acc[:, :DH] / l
                ex = q_ref[:, a * AUG + DH:(a + 1) * AUG].astype(F32)
                c2 = ex[:, 0:1] + ex[:, 1:2] + ex[:, 2:3]
                qx_ref[:, a * DH:(a + 1) * DH] = _extras(c2 - (m_s[a] + jnp.log(l) * LOG2E), 3)

    q_spec = pl.BlockSpec((t, hp * AUG), lambda h, j, qi_ref, ki_ref: (qi_ref[j], h))
    kv_spec = pl.BlockSpec((t, hp * AUG), lambda h, j, qi_ref, ki_ref: (ki_ref[j], h))
    out_spec = pl.BlockSpec((t, hp * DH), lambda h, j, qi_ref, ki_ref: (qi_ref[j], h))
    arrs = list(shards) + list(whole)
    grid_spec = pltpu.PrefetchScalarGridSpec(
        num_scalar_prefetch=2, grid=(n_h, n_j),
        in_specs=[q_spec, kv_spec, kv_spec] + [HBM_SPEC] * nall, out_specs=[out_spec, out_spec] + [HBM_SPEC] * nall,
        scratch_shapes=[pltpu.VMEM((hp, t, 1), F32), pltpu.VMEM((hp, t, AUG), F32)]
        + (_gather_semaphores(na, nall) if nall else []))
    outs = _call(
        body, name="attn_fwd", grid_spec=grid_spec,
        out_shape=[jax.ShapeDtypeStruct((T, D), F32), jax.ShapeDtypeStruct((T, D), BF16)] + _gather_out_shapes(arrs),
        compiler_params=_cparams(("arbitrary", "arbitrary"), VMEM_BIG),
    )(qi_tab, ki_tab, q_aug, k_aug, v_aug, *arrs)
    return outs[0], outs[1], _place_own(outs[2:], arrs)


def _lru_gates(xc, wr_ref, br_ref, wi_ref, bi_ref, lam_ref):
    r = _sigmoid(_gate_pre(xc, wr_ref) + br_ref[...])
    ig = _sigmoid(_gate_pre(xc, wi_ref) + bi_ref[...])
    sp = _softplus_neg(lam_ref[...])
    la = (-LRU_C) * r * sp
    a = jnp.exp(la)
    y = -jnp.tanh(la) * (a * a + 1.0)
    return r, ig, sp, a, jnp.sqrt(y), lax.rsqrt(y)


def _branches_fwd(o, g_attn, x_lru, g_lru, gain_a, gain_l, conv_w, conv_b, w_r, b_r, w_i, b_i, lam):
    T = o.shape[0]
    tm = TM

    def body(o_ref, ga_ref, xl_ref, gl_ref, gna_ref, gnl_ref, cw_ref, cb_ref, wr_ref, br_ref, wi_ref, bi_ref,
             lam_ref, ycat_ref, xc_ref, h_ref, halo_s, hc_s):
        @pl.when(pl.program_id(0) == 0)
        def _():
            halo_s[...] = jnp.zeros_like(halo_s)
            hc_s[...] = jnp.zeros_like(hc_s)

        ov = o_ref[...]
        ga = ga_ref[...]
        ya = ov * _rstd(ov) * gna_ref[...] * (ga * _sigmoid(ga))
        ycat_ref[:, :D] = ya.astype(BF16)

        xl = xl_ref[...]
        halo = halo_s[...]
        xc = xl * cw_ref[3:4, :] + cb_ref[...]
        for j in range(3):
            xc = xc + _shift_down(xl, 3 - j, halo) * cw_ref[j:j + 1, :]
        halo_s[...] = xl_ref[tm - SUBLANES:tm, :]
        xc_ref[...] = xc

        _, ig, _, a, sq, _ = _lru_gates(xc, wr_ref, br_ref, wi_ref, bi_ref, lam_ref)
        u = sq * (ig * xc)
        hc_s[...] = _scan_fwd_into(a, u, hc_s[...], h_ref)
        hh = h_ref[...]

        gl = gl_ref[...]
        yl = hh * _rstd(hh) * gnl_ref[...] * (gl * _sigmoid(gl))
        ycat_ref[:, D:] = yl.astype(BF16)

    vec = _const_spec((1, D))
    wspec = _const_spec((NB, LANES, LANES))
    return _call(
        body, name="branches_fwd", grid=(T // tm,),
        in_specs=[_row_spec(tm, D)] * 4 + [vec, vec, _const_spec((4, D)), vec, wspec, vec, wspec, vec, vec],
        out_specs=[_row_spec(tm, DMIX), _row_spec(tm, D), _row_spec(tm, D)],
        out_shape=[jax.ShapeDtypeStruct((T, DMIX), BF16), jax.ShapeDtypeStruct((T, D), F32),
                   jax.ShapeDtypeStruct((T, D), F32)],
        scratch_shapes=[pltpu.VMEM((SUBLANES, D), F32), pltpu.VMEM((1, D), F32)],
        compiler_params=_cparams(("arbitrary",)),
    )(o, g_attn, x_lru, g_lru, gain_a, gain_l, conv_w, conv_b, w_r, b_r, w_i, b_i, lam)


def _tail(ycat, x, p, tgt, w_out, post_gain, w_ple, ple_gain, w_gate, b_gate):
    T = x.shape[0]
    tm = TM

    def body(ycat_ref, x_ref, p_ref, t_ref, wo_ref, pg_ref, wp_ref, eg_ref, wg_ref, bg_ref,
             dh1_ref, dycat_ref, dmix_ref, h1b_ref, dgp_ref, pb_ref, dpe_ref, acc_ref):
        @pl.when(pl.program_id(0) == 0)
        def _():
            acc_ref[...] = jnp.zeros_like(acc_ref)

        mix = _dot(ycat_ref[...], wo_ref[...])
        rstd_m = _rstd(mix)
        mhat = mix * rstd_m
        h1 = x_ref[...] + mhat * pg_ref[...]
        pb = p_ref[...].astype(BF16)
        pb_ref[...] = pb
        pe = _dot(pb, wp_ref[...])
        rstd_p = _rstd(pe)
        pehat = pe * rstd_p
        e = pehat * eg_ref[...]
        h1b = h1.astype(BF16)
        h1b_ref[...] = h1b
        gate = _sigmoid(_dot(h1b, wg_ref[...]) + bg_ref[...])
        diff = (h1 + gate * e) - t_ref[...]

        dy = diff * (1.0 / D)
        de = dy * gate
        dgp = (dy * e) * gate * (1.0 - gate)
        dgpb = dgp.astype(BF16)
        dgp_ref[...] = dgpb
        dh1 = dy + _dot_nt(dgpb, wg_ref[...])
        dh1_ref[...] = dh1
        dpe_ref[...] = _rms_bwd(de * eg_ref[...], pehat, rstd_p).astype(BF16)
        dmix = _rms_bwd(dh1 * pg_ref[...], mhat, rstd_m).astype(BF16)
        dmix_ref[...] = dmix
        dycat_ref[...] = _dot_nt(dmix, wo_ref[...])

        acc_ref[0:1, :] += jnp.sum(dh1 * mhat, axis=0, keepdims=True)
        acc_ref[1:2, :] += jnp.sum(de * pehat, axis=0, keepdims=True)
        acc_ref[2:3, :] += jnp.sum(dgp, axis=0, keepdims=True)
        acc_ref[3:4, :] += jnp.sum(diff * diff, axis=0, keepdims=True) * (0.5 / D)

    vec = _const_spec((1, D))
    bf = jax.ShapeDtypeStruct((T, D), BF16)
    return _call(
        body, name="tail", grid=(T // tm,),
        in_specs=[_row_spec(tm, DMIX), _row_spec(tm, D), _row_spec(tm, DPLE), _row_spec(tm, D),
                  _const_spec((DMIX, D)), vec, _const_spec((DPLE, D)), vec, _const_spec((D, D)), vec],
        out_specs=[_row_spec(tm, D), _row_spec(tm, DMIX), _row_spec(tm, D), _row_spec(tm, D), _row_spec(tm, D),
                   _row_spec(tm, DPLE), _row_spec(tm, D), _const_spec((SUBLANES, D))],
        out_shape=[jax.ShapeDtypeStruct((T, D), F32), jax.ShapeDtypeStruct((T, DMIX), F32), bf, bf, bf,
                   jax.ShapeDtypeStruct((T, DPLE), BF16), bf, jax.ShapeDtypeStruct((SUBLANES, D), F32)],
        compiler_params=_cparams(("arbitrary",), VMEM_BIG),
    )(ycat, x, p, tgt, w_out, post_gain, w_ple, ple_gain, w_gate, b_gate)


def _branches_bwd(dycat, o, g_attn, h, g_lru, gain_a, gain_l):
    T = o.shape[0]
    tm = TM

    def body(dy_ref, o_ref, ga_ref, h_ref, gl_ref, gna_ref, gnl_ref,
             do_ref, dga_ref, dgl_ref, dh_ref, acc_ref):
        @pl.when(pl.program_id(0) == 0)
        def _():
            acc_ref[...] = jnp.zeros_like(acc_ref)

        def branch(val, g, gain, dyv):
            rstd = _rstd(val)
            vhat = val * rstd
            sig = _sigmoid(g)
            dn = dyv * (g * sig)
            dg = dyv * (vhat * gain) * (sig * (1.0 + g * (1.0 - sig)))
            dgain = jnp.sum(dn * vhat, axis=0, keepdims=True)
            return _rms_bwd(dn * gain, vhat, rstd), dg, dgain

        ov = o_ref[...]
        do, dga, dgain_a = branch(ov, ga_ref[...], gna_ref[...], dy_ref[:, :D])
        dga_ref[...] = dga.astype(BF16)
        prod = do * ov
        for hd in range(H):
            head = slice(hd * DH, (hd + 1) * DH)
            do_ref[:, hd * AUG:hd * AUG + DH] = do[:, head].astype(BF16)
            do_ref[:, hd * AUG + DH:(hd + 1) * AUG] = _extras(-jnp.sum(prod[:, head], axis=1, keepdims=True), None)

        dh, dgl, dgain_l = branch(h_ref[...], gl_ref[...], gnl_ref[...], dy_ref[:, D:])
        dh_ref[...] = dh
        dgl_ref[...] = dgl.astype(BF16)
        acc_ref[0:1, :] += dgain_a
        acc_ref[1:2, :] += dgain_l

    vec = _const_spec((1, D))
    bf = jax.ShapeDtypeStruct((T, D), BF16)
    return _call(
        body, name="branches_bwd", grid=(T // tm,),
        in_specs=[_row_spec(tm, DMIX)] + [_row_spec(tm, D)] * 4 + [vec, vec],
        out_specs=[_row_spec(tm, H * AUG), _row_spec(tm, D), _row_spec(tm, D), _row_spec(tm, D),
                   _const_spec((SUBLANES, D))],
        out_shape=[jax.ShapeDtypeStruct((T, H * AUG), BF16), bf, bf, jax.ShapeDtypeStruct((T, D), F32),
                   jax.ShapeDtypeStruct((SUBLANES, D), F32)],
        compiler_params=_cparams(("arbitrary",)),
    )(dycat, o, g_attn, h, g_lru, gain_a, gain_l)


def _lru_bwd(dh, h, xc, x_lru, conv_w, w_r, b_r, w_i, b_i, lam):
    T = dh.shape[0]
    tm = TM
    nt = T // tm
    per = tm // SUBLANES

    def body(dh_ref, h_ref, hprev_ref, xc_ref, xl_ref, xlprev_ref, cw_ref, wr_ref, br_ref, wi_ref, bi_ref, lam_ref,
             dxl_ref, dwr_ref, dwi_ref, acc_ref, carry_s, dxc_next_s, top_s, dht_s):
        i = pl.program_id(0)

        @pl.when(i == 0)
        def _():
            acc_ref[...] = jnp.zeros_like(acc_ref)
            dwr_ref[...] = jnp.zeros_like(dwr_ref)
            dwi_ref[...] = jnp.zeros_like(dwi_ref)
            carry_s[...] = jnp.zeros_like(carry_s)
            dxc_next_s[...] = jnp.zeros_like(dxc_next_s)

        inner = jnp.where(i == nt - 1, 0.0, 1.0)
        xc = xc_ref[...]
        r, ig, sp, a, sq, inv_sq = _lru_gates(xc, wr_ref, br_ref, wi_ref, bi_ref, lam_ref)

        row = lax.broadcasted_iota(jnp.int32, (tm, D), 0)
        u = dh_ref[...] + jnp.where(row == tm - 1, carry_s[...], 0.0)
        _scan_bwd_into(pltpu.roll(a, tm - 1, 0), u, dht_s)
        dht = dht_s[...]
        top_s[...] = a[:SUBLANES, :] * dht[:SUBLANES, :]
        carry_s[...] = top_s[0:1, :]

        hprev = hprev_ref[...] * inner
        da = dht * _shift_down(h_ref[...], 1, hprev)
        dig = dht * sq * xc
        dxc = dht * sq * ig
        dsq = dht * ig * xc
        dla = da * a - dsq * (a * a) * inv_sq
        dr = dla * ((-LRU_C) * sp)
        dpr = dr * r * (1.0 - r)
        dpi = dig * ig * (1.0 - ig)
        for n in range(NB):
            blk = slice(n * LANES, (n + 1) * LANES)
            xcb = xc[:, blk].astype(BF16)
            dwr_ref[n] += _dot_tn(xcb, dpr[:, blk].astype(BF16))
            dwi_ref[n] += _dot_tn(xcb, dpi[:, blk].astype(BF16))
        dxc = dxc + _gate_pre_t(dpr, wr_ref) + _gate_pre_t(dpi, wi_ref)

        xl = xl_ref[...]
        xlprev = xlprev_ref[...] * inner
        nxt = dxc_next_s[...]
        dxl = dxc * cw_ref[3:4, :]
        acc_ref[3:4, :] += jnp.sum(dxc * xl, axis=0, keepdims=True)
        for j in range(3):
            dxl = dxl + _shift_up(dxc, 3 - j, nxt) * cw_ref[j:j + 1, :]
            acc_ref[j:j + 1, :] += jnp.sum(dxc * _shift_down(xl, 3 - j, xlprev), axis=0, keepdims=True)
        dxc_next_s[...] = dxc[:SUBLANES, :]
        dxl_ref[...] = dxl.astype(BF16)

        acc_ref[4:5, :] += jnp.sum(dxc, axis=0, keepdims=True)
        acc_ref[5:6, :] += jnp.sum(dpr, axis=0, keepdims=True)
        acc_ref[6:7, :] += jnp.sum(dpi, axis=0, keepdims=True)
        acc_ref[7:8, :] += jnp.sum(dla * ((-LRU_C) * r), axis=0, keepdims=True)

        @pl.when(i == nt - 1)
        def _():
            lam_v = lam_ref[...]
            acc_ref[7:8, :] = acc_ref[7:8, :] * (-_sigmoid(-lam_v))

    rev = pl.BlockSpec((tm, D), lambda i: (nt - 1 - i, 0))
    prev8 = pl.BlockSpec((SUBLANES, D), lambda i: (jnp.maximum((nt - 1 - i) * per - 1, 0), 0))
    vec = _const_spec((1, D))
    wspec = _const_spec((NB, LANES, LANES))
    bf = jax.ShapeDtypeStruct((T, D), BF16)
    return _call(
        body, name="lru_bwd", grid=(nt,),
        in_specs=[rev, rev, prev8, rev, rev, prev8, _const_spec((4, D)), wspec, vec, wspec, vec, vec],
        out_specs=[rev, wspec, wspec, _const_spec((SUBLANES, D))],
        out_shape=[bf, jax.ShapeDtypeStruct((NB, LANES, LANES), F32), jax.ShapeDtypeStruct((NB, LANES, LANES), F32),
                   jax.ShapeDtypeStruct((SUBLANES, D), F32)],
        scratch_shapes=[pltpu.VMEM((1, D), F32), pltpu.VMEM((SUBLANES, D), F32), pltpu.VMEM((SUBLANES, D), F32),
                        pltpu.VMEM((tm, D), F32)],
        compiler_params=_cparams(("arbitrary",)),
    )(dh, h, h, xc, x_lru, x_lru, conv_w, w_r, b_r, w_i, b_i, lam)


def _chip_copies(srcs, dsts, send_sems, recv_sems):
    x, y, c = _position()
    chip = 2 * x + y
    na = len(srcs)
    return [pltpu.make_async_remote_copy(
        src_ref=srcs[a].at[2 * px + py], dst_ref=dsts[a].at[chip], send_sem=send_sems.at[j * na + a],
        recv_sem=recv_sems.at[j * na + a], device_id=(px, py, c), device_id_type=MESH)
        for j, (px, py) in enumerate(_other_chips(x, y)) for a in range(na)]


def _attn_bwd(q_aug, qx, k_aug, v_aug, do_aug, exchange=()):
    T = q_aug.shape[0]
    t = TA
    n = T // t
    hp = BWD_HEADS
    heads = range(hp)
    scale = DH ** -0.5
    ki_tab, qi_tab = _causal_pairs(n, q_major=False)
    last = ki_tab.shape[0] - 1
    ne = len(exchange)
    n_h = H // hp

    def body(ki_ref, qi_ref, q_ref, qx_ref, k_ref, v_ref, do_ref, *rest):
        sent, rest = rest[:ne], rest[ne:]
        dq_ref, dk_ref, dv_ref, dck_ref, dcq_ref = rest[:5]
        received, rest = rest[5:5 + ne], rest[5 + ne:]
        dq_s, dk_s, dv_s = rest[:3]
        j = pl.program_id(1)
        ki = ki_ref[j]
        qi = qi_ref[j]

        if ne:
            first_step = (pl.program_id(0) == 0) & (j == 0)
            last_step = (pl.program_id(0) == n_h - 1) & (j == last)

            @pl.when(first_step)
            def _():
                for cp in _chip_copies(sent, received, *rest[3:]):
                    cp.start()

            @pl.when(last_step)
            def _():
                for cp in _chip_copies(sent, received, *rest[3:]):
                    cp.wait()

        @pl.when(j == 0)
        def _():
            dq_s[...] = jnp.zeros_like(dq_s)

        @pl.when(qi == ki)
        def _():
            dk_s[...] = jnp.zeros_like(dk_s)
            dv_s[...] = jnp.zeros_like(dv_s)

        def step(on_diagonal):
            cols = [slice(a * AUG, (a + 1) * AUG) for a in heads]
            qb = [jnp.concatenate([q_ref[:, a * AUG:a * AUG + DH], qx_ref[:, a * DH:(a + 1) * DH]], axis=1)
                  for a in heads]
            st = [_dot_nt(k_ref[:, cols[a]], qb[a]) for a in heads]
            if on_diagonal:
                krow = lax.broadcasted_iota(jnp.int32, (t, t), 0)
                qcol = lax.broadcasted_iota(jnp.int32, (t, t), 1)
                st = [jnp.where(krow <= qcol, st[a], NEG) for a in heads]
            pt = [jnp.exp2(st[a]) for a in heads]
            dsb = [(pt[a] * _dot_nt(v_ref[:, cols[a]], do_ref[:, cols[a]])).astype(BF16) for a in heads]
            off = pl.multiple_of(qi * t, t)
            for a in heads:
                dv_s[a] += _dot(pt[a].astype(BF16), do_ref[:, cols[a]])
                dk_s[a] += _dot(dsb[a], qb[a])
                dq_s[a, pl.ds(off, t), :] += _dot_tn(dsb[a], k_ref[:, cols[a]])

        @pl.when(qi > ki)
        def _():
            step(False)

        @pl.when(qi == ki)
        def _():
            step(True)

        @pl.when(qi == n - 1)
        def _():
            for a in heads:
                dk_ref[:, a * DH:(a + 1) * DH] = (dk_s[a, :, :DH] * LN2).astype(BF16)
                dv_ref[:, a * DH:(a + 1) * DH] = dv_s[a, :, :DH].astype(BF16)
                dck_ref[a] = jnp.broadcast_to(dk_s[a, :, DH + 3:DH + 4], (t, LANES))

        @pl.when(j == last)
        def _():
            for a in heads:
                dq_ref[:, a * DH:(a + 1) * DH] = (dq_s[a, :, :DH] * scale).astype(BF16)
                dcq_ref[a] = jnp.broadcast_to(dq_s[a, :, DH:DH + 1], (T, LANES))

    qside = pl.BlockSpec((t, hp * AUG), lambda h, j, ki_ref, qi_ref: (qi_ref[j], h))
    qxside = pl.BlockSpec((t, hp * DH), lambda h, j, ki_ref, qi_ref: (qi_ref[j], h))
    kside = pl.BlockSpec((t, hp * AUG), lambda h, j, ki_ref, qi_ref: (ki_ref[j], h))
    kout = pl.BlockSpec((t, hp * DH), lambda h, j, ki_ref, qi_ref: (ki_ref[j], h))
    bf = jax.ShapeDtypeStruct((T, D), BF16)
    sums = jax.ShapeDtypeStruct((H, T, LANES), F32)
    grid_spec = pltpu.PrefetchScalarGridSpec(
        num_scalar_prefetch=2, grid=(n_h, ki_tab.shape[0]),
        in_specs=[qside, qxside, kside, kside, qside] + [HBM_SPEC] * ne,
        out_specs=[pl.BlockSpec((T, hp * DH), lambda h, j, ki_ref, qi_ref: (0, h)), kout, kout,
                   pl.BlockSpec((hp, t, LANES), lambda h, j, ki_ref, qi_ref: (h, ki_ref[j], 0)),
                   pl.BlockSpec((hp, T, LANES), lambda h, j, ki_ref, qi_ref: (h, 0, 0))] + [HBM_SPEC] * ne,
        scratch_shapes=[pltpu.VMEM((hp, T, AUG), F32), pltpu.VMEM((hp, t, AUG), F32), pltpu.VMEM((hp, t, AUG), F32)]
        + ([pltpu.SemaphoreType.DMA((3 * ne,)), pltpu.SemaphoreType.DMA((3 * ne,))] if ne else []))
    outs = _call(
        body, name="attn_bwd", grid_spec=grid_spec,
        out_shape=[bf, bf, bf, sums, sums] + [jax.ShapeDtypeStruct(s.shape, s.dtype) for s in exchange],
        compiler_params=_cparams(("arbitrary", "arbitrary"), VMEM_BIG),
    )(ki_tab, qi_tab, q_aug, qx, k_aug, v_aug, do_aug, *exchange)
    return (*outs[:5], list(outs[5:]))


def _fgate_bwd(dc_key, dc_query, flb):
    T = flb.shape[0]
    tm = TM
    nt = T // tm

    def body(dck_ref, dcq_ref, flb_ref, dfl_ref, acc_ref, carry, top_s):
        @pl.when(pl.program_id(0) == 0)
        def _():
            carry[...] = jnp.zeros_like(carry)
            acc_ref[...] = jnp.zeros_like(acc_ref)

        flb = flb_ref[...]
        lane = lax.broadcasted_iota(jnp.int32, flb.shape, 1)
        dc = jnp.zeros(flb.shape, F32)
        for hd in range(H):
            dc = dc + jnp.where(lane == hd, dcq_ref[hd] - dck_ref[hd], 0.0)
        r = lax.broadcasted_iota(jnp.int32, (tm, tm), 0)
        c = lax.broadcasted_iota(jnp.int32, (tm, tm), 1)
        dls = _dot_exact((c >= r).astype(F32), dc) + carry[...]
        top_s[...] = dls[:SUBLANES, :]
        carry[...] = top_s[0:1, :]
        dfl = jnp.where(lane < H, dls * _sigmoid(-flb), 0.0)
        dfl_ref[...] = dfl.astype(BF16)
        acc_ref[0:1, :] += jnp.sum(dfl, axis=0, keepdims=True)

    rev = pl.BlockSpec((tm, LANES), lambda i: (nt - 1 - i, 0))
    return _call(
        body, name="fgate_bwd", grid=(nt,),
        in_specs=[pl.BlockSpec((H, tm, LANES), lambda i: (0, nt - 1 - i, 0))] * 2 + [rev],
        out_specs=[rev, _const_spec((SUBLANES, LANES))],
        out_shape=[jax.ShapeDtypeStruct((T, LANES), BF16), jax.ShapeDtypeStruct((SUBLANES, LANES), F32)],
        scratch_shapes=[pltpu.VMEM((1, LANES), F32), pltpu.VMEM((SUBLANES, LANES), F32)],
        compiler_params=_cparams(("arbitrary",)),
    )(dc_key, dc_query, flb)


def _dx(dz, dfl, w_a, w_f, w_b, x, pre_gain, dh1):
    T = x.shape[0]
    tm = TM

    def body(*refs):
        dz_refs = refs[:6]
        dfl_ref, wa_ref, wf_ref, wb_ref, x_ref, g_ref, dh1_ref, gx_ref, acc_ref = refs[6:]

        @pl.when(pl.program_id(0) == 0)
        def _():
            acc_ref[...] = jnp.zeros_like(acc_ref)

        dxn = _dot(dfl_ref[...], wf_ref[...])
        for s in range(3):
            dxn = dxn + _dot(dz_refs[s][...], wa_ref[s * D:(s + 1) * D, :])
            dxn = dxn + _dot(dz_refs[3 + s][...], wb_ref[s * D:(s + 1) * D, :])
        xv = x_ref[...]
        rstd = _rstd(xv)
        xhat = xv * rstd
        gx_ref[...] = dh1_ref[...] + _rms_bwd(dxn * g_ref[...], xhat, rstd)
        acc_ref[0:1, :] += jnp.sum(dxn * xhat, axis=0, keepdims=True)

    return _call(
        body, name="dx", grid=(T // tm,),
        in_specs=[_row_spec(tm, D)] * 6 + [_row_spec(tm, LANES), _const_spec((3 * D, D)), _const_spec((LANES, D)),
                                           _const_spec((3 * D, D)), _row_spec(tm, D), _const_spec((1, D)),
                                           _row_spec(tm, D)],
        out_specs=[_row_spec(tm, D), _const_spec((SUBLANES, D))],
        out_shape=[jax.ShapeDtypeStruct((T, D), F32), jax.ShapeDtypeStruct((SUBLANES, D), F32)],
        compiler_params=_cparams(("arbitrary",), VMEM_BIG),
    )(*dz, dfl, w_a, w_f, w_b, x, pre_gain, dh1)


GRAD_ROWS = D_IN + SUBLANES


def _dw_in_segment(dz_s, xn, buf, s, bt):
    T = xn.shape[0]
    row0 = s * D + (H if s >= 3 else 0)

    def body(*refs):
        dz_ref, xn_ref, o_ref = refs[0], refs[1], refs[-1]

        @pl.when(pl.program_id(0) == 0)
        def _():
            o_ref[...] = jnp.zeros_like(o_ref)

        o_ref[...] += _dot_tn(dz_ref[...], xn_ref[...])

    tok = pl.BlockSpec((bt, D), lambda t: (t, 0))
    return _call(
        body, name="dw_in_%d" % s, grid=(T // bt,),
        in_specs=[tok, tok] + ([] if buf is None else [pl.BlockSpec(memory_space=pl.ANY)]),
        out_specs=pl.BlockSpec((pl.Element(D), pl.Element(D)), lambda t: (row0, 0)),
        out_shape=jax.ShapeDtypeStruct((GRAD_ROWS, D), F32),
        input_output_aliases={} if buf is None else {2: 0},
        compiler_params=_cparams(("arbitrary",)),
    )(*((dz_s, xn) if buf is None else (dz_s, xn, buf)))


def _dw_in_t(dz, dfl, xn, bt=512):
    T = xn.shape[0]
    nt = T // bt
    main = None
    for s in range(6):
        main = _dw_in_segment(dz[s], xn, main, s, min(T, 2048))

    def f_body(dfl_ref, xn_ref, main_ref, o_ref, acc_s):
        p = pl.program_id(0)
        t = pl.program_id(1)

        @pl.when(t == 0)
        def _():
            acc_s[...] = jnp.zeros_like(acc_s)

        @pl.when(p == 0)
        def _():
            acc_s[...] += _dot_tn(dfl_ref[...], xn_ref[...])

        @pl.when(t == nt - 1)
        def _():
            o_ref[...] = acc_s[:SUBLANES, :]

    fl_block = FL0 // SUBLANES
    end_block = D_IN // SUBLANES
    return _call(
        f_body, name="dw_in_f", grid=(2, nt),
        in_specs=[pl.BlockSpec((bt, LANES), lambda p, t: (t, 0)), pl.BlockSpec((bt, D), lambda p, t: (t, 0)),
                  pl.BlockSpec(memory_space=pl.ANY)],
        out_specs=pl.BlockSpec((SUBLANES, D), lambda p, t: (fl_block + p * (end_block - fl_block), 0)),
        out_shape=jax.ShapeDtypeStruct((GRAD_ROWS, D), F32),
        scratch_shapes=[pltpu.VMEM((LANES, D), F32)],
        input_output_aliases={2: 0},
        compiler_params=_cparams(("arbitrary", "arbitrary")),
    )(dfl, xn, main)


def _matmul_tn(a, b, name, bm=512, bn=1024, bt=2048):
    T, M = a.shape
    N = b.shape[1]
    bm, bn, bt = min(bm, M), min(bn, N), min(bt, T)

    def body(a_ref, b_ref, o_ref):
        @pl.when(pl.program_id(2) == 0)
        def _():
            o_ref[...] = jnp.zeros_like(o_ref)

        o_ref[...] += _dot_tn(a_ref[...], b_ref[...])

    return _call(
        body, name=name, grid=(M // bm, N // bn, T // bt),
        in_specs=[pl.BlockSpec((bt, bm), lambda i, j, t: (t, i)), pl.BlockSpec((bt, bn), lambda i, j, t: (t, j))],
        out_specs=pl.BlockSpec((bm, bn), lambda i, j, t: (i, j)),
        out_shape=jax.ShapeDtypeStruct((M, N), F32),
        compiler_params=_cparams(("parallel", "parallel", "arbitrary")),
    )(a, b)


HBM_SPEC = pl.BlockSpec(memory_space=pltpu.HBM)
VMEM_SPEC = pl.BlockSpec(memory_space=pltpu.VMEM)


def _position():
    return lax.axis_index("x"), lax.axis_index("y"), lax.axis_index("c")


def _other_chips(x, y):
    return [(1 - x, y), (x, 1 - y), (1 - x, 1 - y)]


def _gather_shards(shards, whole):
    na, nw = len(shards), len(whole)
    nall = na + nw

    def body(*refs):
        gather = _GatherPlan(refs[:nall], refs[nall:2 * nall], refs[2 * nall:], na)
        gather.send()
        gather.forward()
        gather.finish()

    arrs = list(shards) + list(whole)
    outs = _call(
        body, name="gather_shards",
        in_specs=[HBM_SPEC] * nall, out_specs=[HBM_SPEC] * nall,
        out_shape=_gather_out_shapes(arrs), scratch_shapes=_gather_semaphores(na, nall),
    )(*arrs)
    return _place_own(outs, arrs)


def _gather_out_shapes(arrs):
    return [jax.ShapeDtypeStruct((N_CHIPS,) + s.shape, s.dtype) for s in arrs]


def _gather_semaphores(na, nall):
    return [pltpu.SemaphoreType.DMA((3 * nall,)), pltpu.SemaphoreType.DMA((3 * nall,)),
            pltpu.SemaphoreType.DMA((3 * na,)), pltpu.SemaphoreType.DMA((3 * na,))]


def _place_own(outs, arrs):
    if not arrs:
        return []
    chip = 2 * lax.axis_index("x") + lax.axis_index("y")
    return [lax.dynamic_update_slice(o, a[None], (chip,) + (0,) * a.ndim) for o, a in zip(outs, arrs)]


class _GatherPlan:
    def __init__(self, srcs, dsts, sems, na):
        ici_send, ici_recv, d2d_send, d2d_recv = sems
        x, y, c = _position()
        chip = 2 * x + y
        nall = len(srcs)

        def half(a, which):
            rows = srcs[a].shape[0] // 2
            return pl.ds(pl.multiple_of(which * rows, 16), rows)

        def copy(src, dst, send, recv, k, to):
            return pltpu.make_async_remote_copy(src_ref=src, dst_ref=dst, send_sem=send.at[k], recv_sem=recv.at[k],
                                                device_id=to, device_id_type=MESH)

        self.first, self.landed, self.passed, self.returned = [], [], [], []
        for j, (px, py) in enumerate(_other_chips(x, y)):
            theirs = 2 * px + py
            for a in range(nall):
                k = j * nall + a
                if a < na:
                    self.first.append(copy(srcs[a].at[half(a, c), :], dsts[a].at[chip, half(a, c), :],
                                           ici_send, ici_recv, k, (px, py, c)))
                    mine = dsts[a].at[theirs, half(a, c), :]
                    other = dsts[a].at[theirs, half(a, 1 - c), :]
                    self.landed.append(copy(mine, mine, ici_send, ici_recv, k, (px, py, c)))
                    self.passed.append(copy(mine, mine, d2d_send, d2d_recv, j * na + a, (x, y, 1 - c)))
                    self.returned.append(copy(other, other, d2d_send, d2d_recv, j * na + a, (x, y, 1 - c)))
                else:
                    self.first.append(copy(srcs[a], dsts[a].at[chip], ici_send, ici_recv, k, (px, py, c)))
                    got = dsts[a].at[theirs]
                    self.landed.append(copy(got, got, ici_send, ici_recv, k, (px, py, c)))
                    self.passed.append(None)

    def send(self):
        for cp in self.first:
            cp.start()

    def forward(self):
        for arrival, fwd in zip(self.landed, self.passed):
            arrival.wait_recv()
            if fwd is not None:
                fwd.start()

    def finish(self):
        for cp in self.returned:
            cp.wait_recv()
        for cp in self.first + [f for f in self.passed if f is not None]:
            cp.wait_send()


W_ROWS = 1568
G_ROWS = 1552
SHARD_ROWS = D_IN // N_CHIPS
WINDOW_STEP = 1536


def _assemble_w_in(cont):
    cb = 256
    half = WINDOW_STEP

    def body(c_ref, wa_ref, wf_ref, wb_ref):
        x0 = c_ref[0].astype(F32)
        x1, x2, x3 = (pltpu.roll(c_ref[j].astype(F32), 2 * j, 0) for j in (1, 2, 3))
        wa = jnp.concatenate([x0[:half], x0[half:half + 16] + x1[:16], x1[16:half]], axis=0)
        wa_ref[...] = wa.astype(BF16)

        fl = x1[half:half + 16] + x2[:16]
        row = lax.broadcasted_iota(jnp.int32, fl.shape, 0)
        wf_ref[:16, :] = jnp.where(row < H, fl, 0.0).astype(BF16)
        wf_ref[16:, :] = jnp.zeros((LANES - 16, cb), BF16)

        mid = x2[half:half + SUBLANES] + x3[:SUBLANES]
        wb = jnp.concatenate([x2[SUBLANES:half], mid, x3[SUBLANES:half + SUBLANES]], axis=0)
        wb_ref[...] = wb.astype(BF16)

    return _call(
        body, name="assemble_w_in", grid=(D // cb,),
        in_specs=[pl.BlockSpec((N_CHIPS, W_ROWS, cb), lambda i: (0, 0, i))],
        out_specs=[pl.BlockSpec((3 * D, cb), lambda i: (0, i)), pl.BlockSpec((LANES, cb), lambda i: (0, i)),
                   pl.BlockSpec((3 * D, cb), lambda i: (0, i))],
        out_shape=[jax.ShapeDtypeStruct((3 * D, D), BF16), jax.ShapeDtypeStruct((LANES, D), BF16),
                   jax.ShapeDtypeStruct((3 * D, D), BF16)],
        compiler_params=_cparams(("parallel",)),
    )(cont)


def _pair_exchange_windows(grad_t):
    half_g = G_ROWS // 2

    def body(g_ref, got, send_sems, recv_sems):
        x, y, c = _position()
        copies = []
        for j in range(N_CHIPS):
            rows = pl.ds(pl.multiple_of(j * WINDOW_STEP + (1 - c) * half_g, SUBLANES), half_g)
            copies.append(pltpu.make_async_remote_copy(
                src_ref=g_ref.at[rows, :], dst_ref=got.at[j], send_sem=send_sems.at[j], recv_sem=recv_sems.at[j],
                device_id=(x, y, 1 - c), device_id_type=MESH))
        for cp in copies:
            cp.start()
        for cp in copies:
            cp.wait()

    return _call(
        body, name="pair_exchange_w_in",
        in_specs=[HBM_SPEC], out_specs=HBM_SPEC,
        out_shape=jax.ShapeDtypeStruct((N_CHIPS, half_g, D), F32),
        scratch_shapes=[pltpu.SemaphoreType.DMA((N_CHIPS,)), pltpu.SemaphoreType.DMA((N_CHIPS,))],
    )(grad_t)


def _pair_exchange(parts):
    na = len(parts)

    def body(*refs):
        srcs, got = refs[:na], refs[na:2 * na]
        send_sems, recv_sems = refs[2 * na:]
        x, y, c = _position()
        copies = []
        for a in range(na):
            half = srcs[a].shape[1] // 2
            rows = pl.ds(pl.multiple_of((1 - c) * half, SUBLANES), half)
            copies.append(pltpu.make_async_remote_copy(
                src_ref=srcs[a].at[:, rows, :], dst_ref=got[a], send_sem=send_sems.at[a], recv_sem=recv_sems.at[a],
                device_id=(x, y, 1 - c), device_id_type=MESH))
        for cp in copies:
            cp.start()
        for cp in copies:
            cp.wait()

    return _call(
        body, name="pair_exchange",
        in_specs=[HBM_SPEC] * na, out_specs=[HBM_SPEC] * na,
        out_shape=[jax.ShapeDtypeStruct((s.shape[0], s.shape[1] // 2, s.shape[2]), s.dtype) for s in parts],
        scratch_shapes=[pltpu.SemaphoreType.DMA((na,)), pltpu.SemaphoreType.DMA((na,))],
    )(*parts)


def _pair_sum(part, got, c, name):
    _, half, C = got.shape
    cb = min(C, 256)

    def body(c_ref, a_ref, b_ref, o_ref):
        o_ref[...] = (a_ref[...] + b_ref[...]).astype(BF16)

    spec = pl.BlockSpec((1, half, cb), lambda j, i, c_ref: (j, 0, i))
    grid_spec = pltpu.PrefetchScalarGridSpec(
        num_scalar_prefetch=1, grid=(N_CHIPS, C // cb),
        in_specs=[pl.BlockSpec((1, half, cb), lambda j, i, c_ref: (j, c_ref[0], i)), spec], out_specs=spec)
    return _call(
        body, name=name, grid_spec=grid_spec,
        out_shape=jax.ShapeDtypeStruct((N_CHIPS, half, C), BF16),
        compiler_params=_cparams(("parallel", "parallel")),
    )(c.reshape(1), part, got)


def _pair_sum_windows(grad_t, got, c):
    _, half, C = got.shape
    cb = 256

    def body(c_ref, a_ref, b_ref, o_ref):
        o_ref[0] = (a_ref[...] + b_ref[0]).astype(BF16)

    def mine(j, i, c_ref):
        return ((j * (WINDOW_STEP // SUBLANES) + c_ref[0] * (half // SUBLANES)) * SUBLANES, i * cb)

    spec = pl.BlockSpec((1, half, cb), lambda j, i, c_ref: (j, 0, i))
    grid_spec = pltpu.PrefetchScalarGridSpec(
        num_scalar_prefetch=1, grid=(N_CHIPS, C // cb),
        in_specs=[pl.BlockSpec((pl.Element(half), pl.Element(cb)), mine), spec], out_specs=spec)
    return _call(
        body, name="pair_sum_w_in", grid_spec=grid_spec,
        out_shape=jax.ShapeDtypeStruct((N_CHIPS, half, C), BF16),
        compiler_params=_cparams(("parallel", "parallel")),
    )(c.reshape(1), grad_t, got)


def _chip_exchange(sums):
    na = len(sums)

    def body(*refs):
        copies = _chip_copies(refs[:na], refs[na:2 * na], *refs[2 * na:])
        for cp in copies:
            cp.start()
        for cp in copies:
            cp.wait()

    return _call(
        body, name="chip_exchange",
        in_specs=[HBM_SPEC] * na, out_specs=[HBM_SPEC] * na,
        out_shape=[jax.ShapeDtypeStruct(s.shape, s.dtype) for s in sums],
        scratch_shapes=[pltpu.SemaphoreType.DMA((3 * na,)), pltpu.SemaphoreType.DMA((3 * na,))],
    )(*sums)


def _chip_sum(own, got, chip, name):
    _, half, C = got.shape
    cb = min(C, 256)

    def body(chip_ref, own_ref, g_ref, o_ref):
        for me in range(N_CHIPS):
            @pl.when(chip_ref[0] == me)
            def _(me=me):
                terms = [own_ref[0] if k == me else g_ref[k] for k in range(N_CHIPS)]
                acc = terms[0].astype(F32) + terms[1].astype(F32)
                acc = acc + terms[2].astype(F32)
                o_ref[...] = acc + terms[3].astype(F32)

    grid_spec = pltpu.PrefetchScalarGridSpec(
        num_scalar_prefetch=1, grid=(C // cb,),
        in_specs=[pl.BlockSpec((1, half, cb), lambda i, chip_ref: (chip_ref[0], 0, i)),
                  pl.BlockSpec((N_CHIPS, half, cb), lambda i, chip_ref: (0, 0, i))],
        out_specs=pl.BlockSpec((half, cb), lambda i, chip_ref: (0, i)))
    return _call(
        body, name=name, grid_spec=grid_spec,
        out_shape=jax.ShapeDtypeStruct((half, C), F32),
        compiler_params=_cparams(("parallel",)),
    )(chip.reshape(1), own, got)


def _pair_swap(halves):
    na = len(halves)

    def body(*refs):
        srcs, dsts = refs[:na], refs[na:2 * na]
        send_sems, recv_sems = refs[2 * na:]
        x, y, c = _position()
        copies = [pltpu.make_async_remote_copy(
            src_ref=srcs[a], dst_ref=dsts[a], send_sem=send_sems.at[a], recv_sem=recv_sems.at[a],
            device_id=(x, y, 1 - c), device_id_type=MESH) for a in range(na)]
        for cp in copies:
            cp.start()
        for cp in copies:
            cp.wait()

    return _call(
        body, name="pair_swap",
        in_specs=[HBM_SPEC] * na, out_specs=[HBM_SPEC] * na,
        out_shape=[jax.ShapeDtypeStruct(s.shape, s.dtype) for s in halves],
        scratch_shapes=[pltpu.SemaphoreType.DMA((na,)), pltpu.SemaphoreType.DMA((na,))],
    )(*halves)


def _allreduce_small(g):
    rows = g.shape[0]
    per = rows // N_DEV

    def body(g_ref, out_ref, got_ref, s1, r1, s2, r2):
        x, y, c = _position()
        me = 4 * x + 2 * y + c
        mine = pl.ds(pl.multiple_of(me * per, SUBLANES), per)
        peers = []
        for j in range(1, N_DEV):
            px = 1 - x if j & 4 else x
            py = 1 - y if j & 2 else y
            pc = 1 - c if j & 1 else c
            peers.append((px, py, pc))

        first = []
        for j, (px, py, pc) in enumerate(peers):
            theirs = pl.ds(pl.multiple_of((4 * px + 2 * py + pc) * per, SUBLANES), per)
            first.append(pltpu.make_async_remote_copy(
                src_ref=g_ref.at[theirs, :], dst_ref=got_ref.at[me], send_sem=s1.at[j], recv_sem=r1.at[j],
                device_id=(px, py, pc), device_id_type=MESH))
        for cp in first:
            cp.start()
        got_ref[me] = g_ref[mine, :]
        for cp in first:
            cp.wait()
        total = got_ref[0]
        for d in range(1, N_DEV):
            total = total + got_ref[d]
        out_ref[mine, :] = total

        second = []
        for j, peer in enumerate(peers):
            second.append(pltpu.make_async_remote_copy(
                src_ref=out_ref.at[mine, :], dst_ref=out_ref.at[mine, :], send_sem=s2.at[j], recv_sem=r2.at[j],
                device_id=peer, device_id_type=MESH))
        for cp in second:
            cp.start()
        for cp in second:
            cp.wait()

    sems = pltpu.SemaphoreType.DMA((N_DEV - 1,))
    return _call(
        body, name="allreduce_small", in_hbm=False,
        in_specs=[VMEM_SPEC], out_specs=VMEM_SPEC,
        out_shape=jax.ShapeDtypeStruct(g.shape, F32),
        scratch_shapes=[pltpu.VMEM((N_DEV, per, LANES), F32), sems, sems, sems, sems],
    )(g)


def _adamw_math(g, w, m, v):
    m2 = ADAM_B1 * m + (1.0 - ADAM_B1) * g
    v2 = ADAM_B2 * v + (1.0 - ADAM_B2) * (g * g)
    m_hat = m2 / (1.0 - ADAM_B1 ** ADAM_STEP)
    v_hat = v2 / (1.0 - ADAM_B2 ** ADAM_STEP)
    delta = (-ADAM_LR) * (m_hat / (jnp.sqrt(v_hat) + ADAM_EPS) + ADAM_WD * w)
    return delta, m2, v2


def _adamw_big(g, w, m, v, name):
    R, C = g.shape
    cb = min(C, LANES)

    def body(g_ref, w_ref, m_ref, v_ref, d_ref, m2_ref, v2_ref):
        d_ref[...], m2_ref[...], v2_ref[...] = _adamw_math(g_ref[...], w_ref[...], m_ref[...], v_ref[...])

    spec = pl.BlockSpec((R, cb), lambda i: (0, i))
    out = jax.ShapeDtypeStruct((R, C), F32)
    return _call(
        body, name=name, grid=(C // cb,),
        in_specs=[spec] * 4, out_specs=[spec] * 3, out_shape=[out] * 3,
        compiler_params=_cparams(("parallel",)),
    )(g, w, m, v)


def _adamw_small(gs, ws, ms, vs):
    n = len(gs)

    def body(*refs):
        for a in range(n):
            g_ref, w_ref, m_ref, v_ref = (refs[k * n + a] for k in range(4))
            d_ref, m2_ref, v2_ref = (refs[(4 + k) * n + a] for k in range(3))
            d_ref[...], m2_ref[...], v2_ref[...] = _adamw_math(g_ref[...], w_ref[...], m_ref[...], v_ref[...])

    outs = [jax.ShapeDtypeStruct(w.shape, F32) for w in ws]
    specs = [_const_spec(w.shape) for w in ws]
    return _call(
        body, name="adamw_small", grid=(1,),
        in_specs=specs * 4, out_specs=specs * 3, out_shape=outs * 3,
    )(*gs, *ws, *ms, *vs)


def _late_weights(st_out, st_ple, st_gate, st_conv):
    return st_out.reshape(DMIX, D), _from_chip_cols(st_ple), st_gate.reshape(D, D), _from_chip_cols(st_conv)


def _local_step(x, p, tgt, w_a, w_f, w_b, late, b_f, pre_gain, post_gain, conv_b,
                w_rgate, b_rgate, w_igate, b_igate, lam, gain_a, gain_l, ple_gain, b_gate,
                gather_late=False, early_reduce=None):
    b_f_pad = jnp.pad(b_f, ((0, 0), (0, LANES - H)))
    w_r = w_rgate.astype(BF16)
    w_i = w_igate.astype(BF16)

    xn, q_aug, k_aug, v_aug, g_attn, x_lru, g_lru, flb = _in_proj(x, pre_gain, w_a, w_f, w_b, b_f_pad)
    if gather_late:
        o, qx, stacks = _attn_fwd(q_aug, k_aug, v_aug, late[:3], late[3:])
        late = _late_weights(*stacks)
    else:
        o, qx, _ = _attn_fwd(q_aug, k_aug, v_aug)
    w_out_b, w_ple_b, w_gate_b, conv_w = late
    ycat, xc, h = _branches_fwd(o, g_attn, x_lru, g_lru, gain_a, gain_l, conv_w, conv_b, w_r, b_rgate, w_i, b_igate,
                                lam)
    dh1, dycat, dmix, h1b, dgp, pb, dpe, acc_t = _tail(ycat, x, p, tgt, w_out_b, post_gain, w_ple_b, ple_gain,
                                                       w_gate_b, b_gate)
    late_grads = [_matmul_tn(ycat, dmix, "dw_out"), _matmul_tn(pb, dpe, "dw_ple"),
                  _matmul_tn(h1b, dgp, "dw_ple_gate")]
    do_aug, dg_attn, dg_lru, dh, acc_b = _branches_bwd(dycat, o, g_attn, h, g_lru, gain_a, gain_l)
    dx_lru, gw_r, gw_i, acc_l = _lru_bwd(dh, h, xc, x_lru, conv_w, w_r, b_rgate, w_i, b_igate, lam)
    if early_reduce is None:
        dq, dk, dv, dc_key, dc_query, _ = _attn_bwd(q_aug, qx, k_aug, v_aug, do_aug)
    else:
        sent = early_reduce(late_grads)
        dq, dk, dv, dc_key, dc_query, received = _attn_bwd(q_aug, qx, k_aug, v_aug, do_aug, sent)
        late_grads = list(zip(sent, received))
    dfl, acc_f = _fgate_bwd(dc_key, dc_query, flb)
    dz = (dq, dk, dv, dg_attn, dx_lru, dg_lru)
    grad_x, acc_x = _dx(dz, dfl, w_a, w_f, w_b, x, pre_gain, dh1)

    grads = dict(
        w_in_t=_dw_in_t(dz, dfl, xn),
        w_out=late_grads[0],
        w_ple=late_grads[1],
        w_ple_gate=late_grads[2],
        w_rgate=gw_r,
        w_igate=gw_i,
        b_f=acc_f[0:1, :H],
        pre_gain=acc_x[0:1],
        post_gain=acc_t[0:1],
        conv_w=acc_l[0:4],
        conv_b=acc_l[4:5],
        b_rgate=acc_l[5:6],
        b_igate=acc_l[6:7],
        lru_lambda=acc_l[7:8],
        attn_out_gain=acc_b[0:1],
        lru_out_gain=acc_b[1:2],
        ple_gain=acc_t[1:2],
        b_ple_gate=acc_t[2:3],
    )
    loss = jnp.sum(acc_t[3])
    return loss, grad_x, grads


SMALL_ROWS = ["b_f", "pre_gain", "post_gain", "conv_w", "conv_b", "b_rgate", "b_igate", "lru_lambda",
              "attn_out_gain", "lru_out_gain", "ple_gain", "b_ple_gate"]
WEIGHTS = ["w_in", "b_f", "pre_gain", "post_gain", "conv_w", "conv_b", "w_rgate", "b_rgate", "w_igate", "b_igate",
           "lru_lambda", "attn_out_gain", "lru_out_gain", "w_out", "w_ple", "ple_gain", "w_ple_gate", "b_ple_gate"]
SHARDED = ["w_in", "w_out", "w_ple", "w_ple_gate"]


def _by_chip_cols(g):
    r, cols = g.shape
    return g.reshape(r, N_CHIPS, cols // N_CHIPS).transpose(1, 0, 2)


def _from_chip_cols(s):
    n, r, cols = s.shape
    return s.transpose(1, 0, 2).reshape(r, n * cols)


def kernel(x, p, w_in, b_f, pre_gain, post_gain, conv_w, conv_b, w_rgate, b_rgate, w_igate, b_igate, lru_lambda, attn_out_gain, lru_out_gain, w_out, w_ple, ple_gain, w_ple_gate, b_ple_gate, loss_target, m_w_in, m_b_f, m_pre_gain, m_post_gain, m_conv_w, m_conv_b, m_w_rgate, m_b_rgate, m_w_igate, m_b_igate, m_lru_lambda, m_attn_out_gain, m_lru_out_gain, m_w_out, m_w_ple, m_ple_gain, m_w_ple_gate, m_b_ple_gate, v_w_in, v_b_f, v_pre_gain, v_post_gain, v_conv_w, v_conv_b, v_w_rgate, v_b_rgate, v_w_igate, v_b_igate, v_lru_lambda, v_attn_out_gain, v_lru_out_gain, v_w_out, v_w_ple, v_ple_gain, v_w_ple_gate, v_b_ple_gate):
    w = dict(w_in=w_in, b_f=b_f, pre_gain=pre_gain, post_gain=post_gain, conv_w=conv_w, conv_b=conv_b,
             w_rgate=w_rgate, b_rgate=b_rgate, w_igate=w_igate, b_igate=b_igate, lru_lambda=lru_lambda,
             attn_out_gain=attn_out_gain, lru_out_gain=lru_out_gain, w_out=w_out, w_ple=w_ple, ple_gain=ple_gain,
             w_ple_gate=w_ple_gate, b_ple_gate=b_ple_gate)
    m = dict(w_in=m_w_in, b_f=m_b_f, pre_gain=m_pre_gain, post_gain=m_post_gain, conv_w=m_conv_w, conv_b=m_conv_b,
             w_rgate=m_w_rgate, b_rgate=m_b_rgate, w_igate=m_w_igate, b_igate=m_b_igate, lru_lambda=m_lru_lambda,
             attn_out_gain=m_attn_out_gain, lru_out_gain=m_lru_out_gain, w_out=m_w_out, w_ple=m_w_ple,
             ple_gain=m_ple_gain, w_ple_gate=m_w_ple_gate, b_ple_gate=m_b_ple_gate)
    v = dict(w_in=v_w_in, b_f=v_b_f, pre_gain=v_pre_gain, post_gain=v_post_gain, conv_w=v_conv_w, conv_b=v_conv_b,
             w_rgate=v_w_rgate, b_rgate=v_b_rgate, w_igate=v_w_igate, b_igate=v_b_igate, lru_lambda=v_lru_lambda,
             attn_out_gain=v_attn_out_gain, lru_out_gain=v_lru_out_gain, w_out=v_w_out, w_ple=v_w_ple,
             ple_gain=v_ple_gain, w_ple_gate=v_w_ple_gate, b_ple_gate=v_b_ple_gate)
    xi, yi, ci = _position()
    chip = 2 * xi + yi

    w_in_t, m_in_t, v_in_t = (jnp.swapaxes(t[0], 0, 1) for t in (w_in, m_w_in, v_w_in))
    window = jnp.pad(w_in_t.astype(BF16), ((0, W_ROWS - SHARD_ROWS), (0, 0)))

    (st_in,) = _gather_shards([window], [])
    w_a, w_f, w_b = _assemble_w_in(st_in)
    late_shards = (w_out[0].astype(BF16), w_ple[0].astype(BF16), w_ple_gate[0].astype(BF16), conv_w[0])

    def early_reduce(local):
        parts = [local[0].reshape(N_CHIPS, DMIX // N_CHIPS, D), _by_chip_cols(local[1]),
                 local[2].reshape(N_CHIPS, D // N_CHIPS, D)]
        got = _pair_exchange(parts)
        return [_pair_sum(parts[a], got[a], ci, "pair_sum_%d" % a) for a in range(3)]

    loss, grad_x, g = _local_step(
        x[0], p[0, 0], loss_target[0], w_a, w_f, w_b, late_shards, b_f, pre_gain, post_gain,
        conv_b, w_rgate[0], b_rgate, w_igate[0], b_igate, lru_lambda, attn_out_gain, lru_out_gain, ple_gain,
        b_ple_gate, gather_late=True, early_reduce=early_reduce)
    loss = lax.psum(loss, ("x", "y", "c"))

    sum_in = _pair_sum_windows(g["w_in_t"], _pair_exchange_windows(g["w_in_t"]), ci)
    (recv_in,) = _chip_exchange([sum_in])
    sums = [sum_in] + [g[n][0] for n in SHARDED[1:]]
    recv = [recv_in] + [g[n][1] for n in SHARDED[1:]]
    halves = [_chip_sum(sums[a], recv[a], chip, "chip_sum_%d" % a) for a in range(4)]
    theirs = _pair_swap(halves)
    full = [jnp.concatenate([jnp.where(ci == 0, a, b), jnp.where(ci == 0, b, a)], axis=0)
            for a, b in zip(halves, theirs)]
    red = dict(zip(SHARDED, full))
    red["w_in"] = lax.dynamic_slice_in_dim(red["w_in"], 2 * chip, SHARD_ROWS, axis=0)

    rows = [jnp.pad(g["b_f"], ((0, 0), (0, D - H)))] + [g[n] for n in SMALL_ROWS[1:]]
    rows.append(jnp.zeros((16 - sum(r.shape[0] for r in rows), D), F32))
    packed = jnp.concatenate([g["w_rgate"].reshape(NB * LANES, LANES), g["w_igate"].reshape(NB * LANES, LANES),
                              jnp.concatenate(rows, axis=0).reshape(LANES, LANES)], axis=0)
    summed = _allreduce_small(packed)
    red["w_rgate"] = summed[:D].reshape(1, NB, LANES, LANES)
    red["w_igate"] = summed[D:2 * D].reshape(1, NB, LANES, LANES)
    vec = summed[2 * D:].reshape(16, D)
    r0 = 0
    for n in SMALL_ROWS:
        nr = 4 if n == "conv_w" else 1
        red[n] = vec[r0:r0 + nr]
        r0 += nr
    red["b_f"] = red["b_f"][:, :H]
    red["conv_w"] = lax.dynamic_slice_in_dim(red["conv_w"], chip * (D // N_CHIPS), D // N_CHIPS, axis=1)[None]

    delta, new_m, new_v = {}, {}, {}
    outs_in = _adamw_big(red["w_in"], w_in_t, m_in_t, v_in_t, "adamw_w_in")
    delta["w_in"], new_m["w_in"], new_v["w_in"] = (jnp.swapaxes(t, 0, 1)[None] for t in outs_in)
    red["w_in"] = jnp.swapaxes(red["w_in"], 0, 1)[None]
    for n in SHARDED[1:]:
        delta[n], new_m[n], new_v[n] = (t[None] for t in _adamw_big(red[n], w[n][0], m[n][0], v[n][0], "adamw_" + n))
        red[n] = red[n][None]
    small = [n for n in WEIGHTS if n not in SHARDED]
    outs = _adamw_small([red[n] for n in small], [w[n] for n in small], [m[n] for n in small],
                        [v[n] for n in small])
    ns = len(small)
    for a, n in enumerate(small):
        delta[n], new_m[n], new_v[n] = outs[a], outs[ns + a], outs[2 * ns + a]

    return (loss, grad_x[None], *[red[n] for n in WEIGHTS], *[delta[n] for n in WEIGHTS],
            *[new_m[n] for n in WEIGHTS], *[new_v[n] for n in WEIGHTS])
```

```python
import functools

import jax
import jax.numpy as jnp
import numpy as np
from jax import lax
from jax.experimental import pallas as pl
from jax.experimental.pallas import tpu as pltpu

F32 = jnp.float32
BF16 = jnp.bfloat16

D = 1024
H = 8
DH = 128
NB = 8
DPLE = 256
DMIX = 2 * D
D_IN = 4 * D + H + 2 * D
FL0 = 3 * D
RMS_EPS = 1e-6
LRU_C = 8.0
NEG = -1e30
LANES = 128
SUBLANES = 8

ADAM_LR = 0.001
ADAM_B1 = 0.9
ADAM_B2 = 0.999
ADAM_EPS = 1e-08
ADAM_WD = 0.01
ADAM_STEP = 10

TM = 256
TA = 512
FWD_HEADS = 4
BWD_HEADS = 2
VMEM_BIG = 56 * 1024 * 1024
VMEM_MID = 40 * 1024 * 1024

MESH = pl.DeviceIdType.MESH
N_CHIPS = 4
N_DEV = 8


def _call(body, *, out_shape, in_hbm=True, **kwargs):
    if not in_hbm:
        return pl.pallas_call(body, out_shape=out_shape, **kwargs)

    def pin(shape):
        return pltpu.HBM(shape.shape, shape.dtype) if isinstance(shape, jax.ShapeDtypeStruct) else shape

    fn = pl.pallas_call(body, out_shape=jax.tree.map(pin, out_shape), **kwargs)

    def run(*args):
        return fn(*[a if a.dtype == jnp.int32 else pltpu.with_memory_space_constraint(a, pltpu.HBM) for a in args])

    return run


def _cparams(sem, vmem=VMEM_MID):
    return pltpu.CompilerParams(dimension_semantics=sem, vmem_limit_bytes=vmem)


def _sigmoid(x):
    return 0.5 * jnp.tanh(0.5 * x) + 0.5


def _rstd(x):
    return lax.rsqrt(jnp.mean(x * x, axis=-1, keepdims=True) + RMS_EPS)


def _rms_bwd(t, xhat, rstd):
    return rstd * (t - xhat * jnp.mean(t * xhat, axis=-1, keepdims=True))


def _dot(a, b):
    return jnp.dot(a, b, preferred_element_type=F32)


def _dot_nt(a, b):
    return lax.dot_general(a, b, (((1,), (1,)), ((), ())), preferred_element_type=F32)


def _dot_tn(a, b):
    return lax.dot_general(a, b, (((0,), (0,)), ((), ())), preferred_element_type=F32)


def _dot_exact(a, b):
    return jnp.dot(a, b, preferred_element_type=F32, precision=lax.Precision.HIGHEST)


def _shift_down(x, j, halo):
    rolled = pltpu.roll(x, j, 0)
    row = lax.broadcasted_iota(jnp.int32, halo.shape, 0)
    top = jnp.where(row < j, pltpu.roll(halo, j, 0), rolled[:SUBLANES])
    return jnp.concatenate([top, rolled[SUBLANES:]], axis=0)


def _shift_up(x, j, nxt):
    tm = x.shape[0]
    rolled = pltpu.roll(x, tm - j, 0)
    row = lax.broadcasted_iota(jnp.int32, nxt.shape, 0)
    bot = jnp.where(row >= SUBLANES - j, pltpu.roll(nxt, SUBLANES - j, 0), rolled[tm - SUBLANES:])
    return jnp.concatenate([rolled[:tm - SUBLANES], bot], axis=0)


def _scan_fwd_into(a, u, carry, h_ref):
    tm = a.shape[0]
    sub = lax.broadcasted_iota(jnp.int32, a.shape, 0) & (SUBLANES - 1)
    d = 1
    while d < SUBLANES:
        keep = sub >= d
        a_s = jnp.where(keep, pltpu.roll(a, d, 0), 1.0)
        u_s = jnp.where(keep, pltpu.roll(u, d, 0), 0.0)
        u = u + a * u_s
        a = a * a_s
        d *= 2
    for g in range(tm // SUBLANES):
        rows = slice(g * SUBLANES, (g + 1) * SUBLANES)
        h_ref[rows, :] = u[rows] + a[rows] * carry
        carry = h_ref[(g + 1) * SUBLANES - 1:(g + 1) * SUBLANES, :]
    return carry


def _scan_bwd_into(b, u, g_ref):
    tm = b.shape[0]
    sub = lax.broadcasted_iota(jnp.int32, b.shape, 0) & (SUBLANES - 1)
    d = 1
    while d < SUBLANES:
        keep = sub < SUBLANES - d
        b_s = jnp.where(keep, pltpu.roll(b, tm - d, 0), 1.0)
        u_s = jnp.where(keep, pltpu.roll(u, tm - d, 0), 0.0)
        u = u + b * u_s
        b = b * b_s
        d *= 2
    nxt = jnp.zeros((1, b.shape[1]), F32)
    for g in reversed(range(tm // SUBLANES)):
        rows = slice(g * SUBLANES, (g + 1) * SUBLANES)
        g_ref[rows, :] = u[rows] + b[rows] * nxt
        nxt = g_ref[g * SUBLANES:g * SUBLANES + 1, :]


def _gate_pre(xc, w_ref):
    outs = []
    for n in range(NB):
        outs.append(_dot(xc[:, n * LANES:(n + 1) * LANES].astype(BF16), w_ref[n]))
    return jnp.concatenate(outs, axis=1)


def _gate_pre_t(d, w_ref):
    outs = []
    for n in range(NB):
        outs.append(_dot_nt(d[:, n * LANES:(n + 1) * LANES].astype(BF16), w_ref[n]))
    return jnp.concatenate(outs, axis=1)


def _softplus_neg(lam):
    return jnp.maximum(-lam, 0.0) + jnp.log(1.0 + jnp.exp(-jnp.abs(lam)))


def _row_spec(tm, width):
    return pl.BlockSpec((tm, width), lambda i: (i, 0))


def _const_spec(shape):
    nd = len(shape)
    return pl.BlockSpec(shape, lambda *_: (0,) * nd)


AUG = 2 * DH
LOG2E = 1.4426950408889634
LN2 = 0.6931471805599453
Q_SCALE = DH ** -0.5 * LOG2E


def _split3(x):
    hi = x.astype(BF16)
    r1 = x - hi.astype(F32)
    mid = r1.astype(BF16)
    lo = (r1 - mid.astype(F32)).astype(BF16)
    return hi, mid, lo


def _extras(col, ones_from):
    t = col.shape[0]
    hi, mid, lo = _split3(jnp.broadcast_to(col, (t, LANES)))
    lane = lax.broadcasted_iota(jnp.int32, (t, LANES), 1)
    rest = jnp.zeros((t, LANES), BF16)
    if ones_from is not None:
        rest = jnp.where((lane >= ones_from) & (lane < ones_from + 3), 1.0, 0.0).astype(BF16)
    return jnp.where(lane == 0, hi, jnp.where(lane == 1, mid, jnp.where(lane == 2, lo, rest)))


def _selectors():
    sel_q = np.zeros((3 * LANES, H * LANES), np.float32)
    sel_k = np.zeros((3 * LANES, H * LANES), np.float32)
    for hd in range(H):
        for piece in range(3):
            sel_q[piece * LANES + hd, hd * LANES + piece] = 1.0
            sel_k[piece * LANES + hd, hd * LANES + 3 + piece] = -1.0
    return jnp.asarray(sel_q, BF16), jnp.asarray(sel_k, BF16)


def _in_proj(x, pre_gain, w_a, w_f, w_b, b_f_pad):
    T = x.shape[0]
    tm = TM
    sel_q, sel_k = _selectors()

    def body(x_ref, g_ref, wa_ref, wf_ref, wb_ref, bf_ref, sq_ref, sk_ref,
             xn_ref, qa_ref, ka_ref, va_ref, ga_ref, xl_ref, gl_ref, flb_ref, vt_ref, c_s, carry):
        @pl.when(pl.program_id(0) == 0)
        def _():
            carry[...] = jnp.zeros_like(carry)

        xv = x_ref[...]
        xn = (xv * _rstd(xv) * g_ref[...]).astype(BF16)
        xn_ref[...] = xn
        for s, o_ref in enumerate((ga_ref, xl_ref, gl_ref)):
            o_ref[...] = _dot_nt(xn, wb_ref[s * D:(s + 1) * D, :]).astype(o_ref.dtype)
        flb = _dot_nt(xn, wf_ref[...]) + bf_ref[...]
        flb_ref[...] = flb
        lane = lax.broadcasted_iota(jnp.int32, flb.shape, 1)
        ls = jnp.where(lane < H, jnp.minimum(flb, 0.0) - jnp.log(1.0 + jnp.exp(-jnp.abs(flb))), 0.0)
        r = lax.broadcasted_iota(jnp.int32, (tm, tm), 0)
        c = lax.broadcasted_iota(jnp.int32, (tm, tm), 1)
        cs = _dot_exact((c <= r).astype(F32), ls) + carry[...]
        c_s[...] = cs
        carry[...] = c_s[tm - 1:tm, :]

        pieces = jnp.concatenate(_split3(cs * LOG2E), axis=1)
        ones_q = jnp.where((lane >= 3) & (lane < 6), 1.0, 0.0)
        ones_k = jnp.where(lane < 3, 1.0, 0.0)
        zq = _dot_nt(xn, wa_ref[0:D, :]) * Q_SCALE
        zk = _dot_nt(xn, wa_ref[D:2 * D, :])
        zv = _dot_nt(xn, wa_ref[2 * D:3 * D, :])
        ex_q = _dot(pieces, sq_ref[...])
        ex_k = _dot(pieces, sk_ref[...])
        for hd in range(H):
            head = slice(hd * DH, (hd + 1) * DH)
            lo, hi = hd * AUG, hd * AUG + DH
            qa_ref[:, lo:hi] = zq[:, head].astype(BF16)
            qa_ref[:, hi:hi + DH] = (ex_q[:, head] + ones_q).astype(BF16)
            ka_ref[:, lo:hi] = zk[:, head].astype(BF16)
            ka_ref[:, hi:hi + DH] = (ex_k[:, head] + ones_k).astype(BF16)
            va_ref[:, lo:hi] = zv[:, head].astype(BF16)
            va_ref[:, hi:hi + DH] = ones_k.astype(BF16)
            vt_ref[lo:hi, :] = jnp.transpose(zv[:, head]).astype(BF16)
            vt_ref[hi:hi + DH, :] = jnp.where(lax.broadcasted_iota(jnp.int32, (DH, tm), 0) < 3, 1.0, 0.0).astype(BF16)

    bf = jax.ShapeDtypeStruct((T, D), BF16)
    aug = jax.ShapeDtypeStruct((T, H * AUG), BF16)
    f32 = jax.ShapeDtypeStruct((T, D), F32)
    sel_spec = _const_spec((3 * LANES, H * LANES))
    return _call(
        body, name="in_proj", grid=(T // tm,),
        in_specs=[_row_spec(tm, D), _const_spec((1, D)), _const_spec((3 * D, D)), _const_spec((LANES, D)),
                  _const_spec((3 * D, D)), _const_spec((1, LANES)), sel_spec, sel_spec],
        out_specs=[_row_spec(tm, D)] + [_row_spec(tm, H * AUG)] * 3 + [_row_spec(tm, D)] * 3 + [_row_spec(tm, LANES)]
        + [pl.BlockSpec((H * AUG, tm), lambda i: (0, i))],
        out_shape=[bf, aug, aug, aug, f32, f32, f32, jax.ShapeDtypeStruct((T, LANES), F32),
                   jax.ShapeDtypeStruct((H * AUG, T), BF16)],
        scratch_shapes=[pltpu.VMEM((tm, LANES), F32), pltpu.VMEM((1, LANES), F32)],
        compiler_params=_cparams(("arbitrary",), VMEM_BIG),
    )(x, pre_gain, w_a, w_f, w_b, b_f_pad, sel_q, sel_k)


def _causal_pairs(n, q_major):
    if q_major:
        pairs = [(qi, ki) for qi in range(n) for ki in range(qi + 1)]
    else:
        pairs = [(ki, qi) for ki in range(n) for qi in range(ki, n)]
    return (jnp.asarray([a for a, _ in pairs], jnp.int32), jnp.asarray([b for _, b in pairs], jnp.int32))


def _attn_fwd(q_aug, k_aug, vt_aug, shards=(), whole=()):
    T = q_aug.shape[0]
    t = TA
    n = T // t
    hp = FWD_HEADS
    heads = range(hp)
    qi_tab, ki_tab = _causal_pairs(n, q_major=True)
    na, nall = len(shards), len(shards) + len(whole)
    n_h, n_j = H // hp, qi_tab.shape[0]

    def body(qi_ref, ki_ref, q_ref, k_ref, vt_ref, *rest):
        srcs, rest = rest[:nall], rest[nall:]
        o_ref, qx_ref = rest[:2]
        dsts, rest = rest[2:2 + nall], rest[2 + nall:]
        m_s, acc_s = rest[:2]
        h = pl.program_id(0)
        j = pl.program_id(1)
        qi = qi_ref[j]
        ki = ki_ref[j]

        if nall:
            gather = _GatherPlan(srcs, dsts, rest[2:], na)
            pl.when((h == 0) & (j == 0))(gather.send)
            pl.when((h == n_h - 1) & (j == 0))(gather.forward)
            pl.when((h == n_h - 1) & (j == n_j - 1))(gather.finish)

        @pl.when(ki == 0)
        def _():
            m_s[...] = jnp.full(m_s.shape, NEG, F32)
            acc_s[...] = jnp.zeros_like(acc_s)

        def step(on_diagonal):
            cols = [slice(a * AUG, (a + 1) * AUG) for a in heads]
            if on_diagonal:
                krow = lax.broadcasted_iota(jnp.int32, (t, t), 0)
                qcol = lax.broadcasted_iota(jnp.int32, (t, t), 1)
            for a in heads:
                st = _dot_nt(k_ref[:, cols[a]], q_ref[:, cols[a]])
                if on_diagonal:
                    st = jnp.where(krow <= qcol, st, NEG)
                m_prev = m_s[a]
                m_new = jnp.maximum(m_prev, jnp.max(st, axis=0, keepdims=True))
                pt = jnp.exp2(st - m_new).astype(BF16)
                acc_s[a] = jnp.exp2(m_prev - m_new) * acc_s[a] + _dot(vt_ref[cols[a], :], pt)
                m_s[a] = m_new

        @pl.when(ki < qi)
        def _():
            step(False)

        @pl.when(ki == qi)
        def _():
            step(True)
            piece = lax.broadcasted_iota(jnp.int32, (DH, t), 0)
            for a in heads:
                l = acc_s[a, DH:DH + 1, :]
                ex = jnp.transpose(q_ref[:, a * AUG + DH:(a + 1) * AUG].astype(F32))
                c2 = jnp.sum(jnp.where(piece < 3, ex, 0.0), axis=0, keepdims=True)
                hi, mid, lo = _split3(jnp.broadcast_to(c2 - (m_s[a] + jnp.log(l) * LOG2E), (DH, t)))
                ones = jnp.where((piece >= 3) & (piece < 6), 1.0, 0.0).astype(BF16)
                ex_t = jnp.where(piece == 0, hi, jnp.where(piece == 1, mid, jnp.where(piece == 2, lo, ones)))
                o_ref[:, a * DH:(a + 1) * DH] = jnp.transpose(acc_s[a, :DH, :] / l)
                qx_ref[:, a * DH:(a + 1) * DH] = jnp.transpose(ex_t.astype(F32)).astype(BF16)

    q_spec = pl.BlockSpec((t, hp * AUG), lambda h, j, qi_ref, ki_ref: (qi_ref[j], h))
    k_spec = pl.BlockSpec((t, hp * AUG), lambda h, j, qi_ref, ki_ref: (ki_ref[j], h))
    vt_spec = pl.BlockSpec((hp * AUG, t), lambda h, j, qi_ref, ki_ref: (h, ki_ref[j]))
    out_spec = pl.BlockSpec((t, hp * DH), lambda h, j, qi_ref, ki_ref: (qi_ref[j], h))
    arrs = list(shards) + list(whole)
    grid_spec = pltpu.PrefetchScalarGridSpec(
        num_scalar_prefetch=2, grid=(n_h, n_j),
        in_specs=[q_spec, k_spec, vt_spec] + [HBM_SPEC] * nall, out_specs=[out_spec, out_spec] + [HBM_SPEC] * nall,
        scratch_shapes=[pltpu.VMEM((hp, 1, t), F32), pltpu.VMEM((hp, AUG, t), F32)]
        + (_gather_semaphores(na, nall) if nall else []))
    outs = _call(
        body, name="attn_fwd", grid_spec=grid_spec,
        out_shape=[jax.ShapeDtypeStruct((T, D), F32), jax.ShapeDtypeStruct((T, D), BF16)] + _gather_out_shapes(arrs),
        compiler_params=_cparams(("arbitrary", "arbitrary"), VMEM_BIG),
    )(qi_tab, ki_tab, q_aug, k_aug, vt_aug, *arrs)
    return outs[0], outs[1], _place_own(outs[2:], arrs)


def _lru_gates(xc, wr_ref, br_ref, wi_ref, bi_ref, lam_ref):
    r = _sigmoid(_gate_pre(xc, wr_ref) + br_ref[...])
    ig = _sigmoid(_gate_pre(xc, wi_ref) + bi_ref[...])
    sp = _softplus_neg(lam_ref[...])
    la = (-LRU_C) * r * sp
    a = jnp.exp(la)
    y = -jnp.tanh(la) * (a * a + 1.0)
    return r, ig, sp, a, jnp.sqrt(y), lax.rsqrt(y)


def _branches_fwd(o, g_attn, x_lru, g_lru, gain_a, gain_l, conv_w, conv_b, w_r, b_r, w_i, b_i, lam):
    T = o.shape[0]
    tm = TM

    def body(o_ref, ga_ref, xl_ref, gl_ref, gna_ref, gnl_ref, cw_ref, cb_ref, wr_ref, br_ref, wi_ref, bi_ref,
             lam_ref, ycat_ref, xc_ref, h_ref, halo_s, hc_s):
        @pl.when(pl.program_id(0) == 0)
        def _():
            halo_s[...] = jnp.zeros_like(halo_s)
            hc_s[...] = jnp.zeros_like(hc_s)

        ov = o_ref[...]
        ga = ga_ref[...]
        ya = ov * _rstd(ov) * gna_ref[...] * (ga * _sigmoid(ga))
        ycat_ref[:, :D] = ya.astype(BF16)

        xl = xl_ref[...]
        halo = halo_s[...]
        xc = xl * cw_ref[3:4, :] + cb_ref[...]
        for j in range(3):
            xc = xc + _shift_down(xl, 3 - j, halo) * cw_ref[j:j + 1, :]
        halo_s[...] = xl_ref[tm - SUBLANES:tm, :]
        xc_ref[...] = xc

        _, ig, _, a, sq, _ = _lru_gates(xc, wr_ref, br_ref, wi_ref, bi_ref, lam_ref)
        u = sq * (ig * xc)
        hc_s[...] = _scan_fwd_into(a, u, hc_s[...], h_ref)
        hh = h_ref[...]

        gl = gl_ref[...]
        yl = hh * _rstd(hh) * gnl_ref[...] * (gl * _sigmoid(gl))
        ycat_ref[:, D:] = yl.astype(BF16)

    vec = _const_spec((1, D))
    wspec = _const_spec((NB, LANES, LANES))
    return _call(
        body, name="branches_fwd", grid=(T // tm,),
        in_specs=[_row_spec(tm, D)] * 4 + [vec, vec, _const_spec((4, D)), vec, wspec, vec, wspec, vec, vec],
        out_specs=[_row_spec(tm, DMIX), _row_spec(tm, D), _row_spec(tm, D)],
        out_shape=[jax.ShapeDtypeStruct((T, DMIX), BF16), jax.ShapeDtypeStruct((T, D), F32),
                   jax.ShapeDtypeStruct((T, D), F32)],
        scratch_shapes=[pltpu.VMEM((SUBLANES, D), F32), pltpu.VMEM((1, D), F32)],
        compiler_params=_cparams(("arbitrary",)),
    )(o, g_attn, x_lru, g_lru, gain_a, gain_l, conv_w, conv_b, w_r, b_r, w_i, b_i, lam)


def _tail(ycat, x, p, tgt, w_out, post_gain, w_ple, ple_gain, w_gate, b_gate):
    T = x.shape[0]
    tm = TM

    def body(ycat_ref, x_ref, p_ref, t_ref, wo_ref, pg_ref, wp_ref, eg_ref, wg_ref, bg_ref,
             dh1_ref, dycat_ref, dmix_ref, h1b_ref, dgp_ref, pb_ref, dpe_ref, acc_ref):
        @pl.when(pl.program_id(0) == 0)
        def _():
            acc_ref[...] = jnp.zeros_like(acc_ref)

        mix = _dot(ycat_ref[...], wo_ref[...])
        rstd_m = _rstd(mix)
        mhat = mix * rstd_m
        h1 = x_ref[...] + mhat * pg_ref[...]
        pb = p_ref[...].astype(BF16)
        pb_ref[...] = pb
        pe = _dot(pb, wp_ref[...])
        rstd_p = _rstd(pe)
        pehat = pe * rstd_p
        e = pehat * eg_ref[...]
        h1b = h1.astype(BF16)
        h1b_ref[...] = h1b
        gate = _sigmoid(_dot(h1b, wg_ref[...]) + bg_ref[...])
        diff = (h1 + gate * e) - t_ref[...]

        dy = diff * (1.0 / D)
        de = dy * gate
        dgp = (dy * e) * gate * (1.0 - gate)
        dgpb = dgp.astype(BF16)
        dgp_ref[...] = dgpb
        dh1 = dy + _dot_nt(dgpb, wg_ref[...])
        dh1_ref[...] = dh1
        dpe_ref[...] = _rms_bwd(de * eg_ref[...], pehat, rstd_p).astype(BF16)
        dmix = _rms_bwd(dh1 * pg_ref[...], mhat, rstd_m).astype(BF16)
        dmix_ref[...] = dmix
        dycat_ref[...] = _dot_nt(dmix, wo_ref[...])

        acc_ref[0:1, :] += jnp.sum(dh1 * mhat, axis=0, keepdims=True)
        acc_ref[1:2, :] += jnp.sum(de * pehat, axis=0, keepdims=True)
        acc_ref[2:3, :] += jnp.sum(dgp, axis=0, keepdims=True)
        acc_ref[3:4, :] += jnp.sum(diff * diff, axis=0, keepdims=True) * (0.5 / D)

    vec = _const_spec((1, D))
    bf = jax.ShapeDtypeStruct((T, D), BF16)
    return _call(
        body, name="tail", grid=(T // tm,),
        in_specs=[_row_spec(tm, DMIX), _row_spec(tm, D), _row_spec(tm, DPLE), _row_spec(tm, D),
                  _const_spec((DMIX, D)), vec, _const_spec((DPLE, D)), vec, _const_spec((D, D)), vec],
        out_specs=[_row_spec(tm, D), _row_spec(tm, DMIX), _row_spec(tm, D), _row_spec(tm, D), _row_spec(tm, D),
                   _row_spec(tm, DPLE), _row_spec(tm, D), _const_spec((SUBLANES, D))],
        out_shape=[jax.ShapeDtypeStruct((T, D), F32), jax.ShapeDtypeStruct((T, DMIX), F32), bf, bf, bf,
                   jax.ShapeDtypeStruct((T, DPLE), BF16), bf, jax.ShapeDtypeStruct((SUBLANES, D), F32)],
        compiler_params=_cparams(("arbitrary",), VMEM_BIG),
    )(ycat, x, p, tgt, w_out, post_gain, w_ple, ple_gain, w_gate, b_gate)


def _branches_bwd(dycat, o, g_attn, h, g_lru, gain_a, gain_l):
    T = o.shape[0]
    tm = TM

    def body(dy_ref, o_ref, ga_ref, h_ref, gl_ref, gna_ref, gnl_ref,
             do_ref, dga_ref, dgl_ref, dh_ref, acc_ref):
        @pl.when(pl.program_id(0) == 0)
        def _():
            acc_ref[...] = jnp.zeros_like(acc_ref)

        def branch(val, g, gain, dyv):
            rstd = _rstd(val)
            vhat = val * rstd
            sig = _sigmoid(g)
            dn = dyv * (g * sig)
            dg = dyv * (vhat * gain) * (sig * (1.0 + g * (1.0 - sig)))
            dgain = jnp.sum(dn * vhat, axis=0, keepdims=True)
            return _rms_bwd(dn * gain, vhat, rstd), dg, dgain

        ov = o_ref[...]
        do, dga, dgain_a = branch(ov, ga_ref[...], gna_ref[...], dy_ref[:, :D])
        dga_ref[...] = dga.astype(BF16)
        prod = do * ov
        for hd in range(H):
            head = slice(hd * DH, (hd + 1) * DH)
            do_ref[:, hd * AUG:hd * AUG + DH] = do[:, head].astype(BF16)
            do_ref[:, hd * AUG + DH:(hd + 1) * AUG] = _extras(-jnp.sum(prod[:, head], axis=1, keepdims=True), None)

        dh, dgl, dgain_l = branch(h_ref[...], gl_ref[...], gnl_ref[...], dy_ref[:, D:])
        dh_ref[...] = dh
        dgl_ref[...] = dgl.astype(BF16)
        acc_ref[0:1, :] += dgain_a
        acc_ref[1:2, :] += dgain_l

    vec = _const_spec((1, D))
    bf = jax.ShapeDtypeStruct((T, D), BF16)
    return _call(
        body, name="branches_bwd", grid=(T // tm,),
        in_specs=[_row_spec(tm, DMIX)] + [_row_spec(tm, D)] * 4 + [vec, vec],
        out_specs=[_row_spec(tm, H * AUG), _row_spec(tm, D), _row_spec(tm, D), _row_spec(tm, D),
                   _const_spec((SUBLANES, D))],
        out_shape=[jax.ShapeDtypeStruct((T, H * AUG), BF16), bf, bf, jax.ShapeDtypeStruct((T, D), F32),
                   jax.ShapeDtypeStruct((SUBLANES, D), F32)],
        compiler_params=_cparams(("arbitrary",)),
    )(dycat, o, g_attn, h, g_lru, gain_a, gain_l)


def _lru_bwd(dh, h, xc, x_lru, conv_w, w_r, b_r, w_i, b_i, lam):
    T = dh.shape[0]
    tm = TM
    nt = T // tm
    per = tm // SUBLANES

    def body(dh_ref, h_ref, hprev_ref, xc_ref, xl_ref, xlprev_ref, cw_ref, wr_ref, br_ref, wi_ref, bi_ref, lam_ref,
             dxl_ref, dwr_ref, dwi_ref, acc_ref, carry_s, dxc_next_s, top_s, dht_s):
        i = pl.program_id(0)

        @pl.when(i == 0)
        def _():
            acc_ref[...] = jnp.zeros_like(acc_ref)
            dwr_ref[...] = jnp.zeros_like(dwr_ref)
            dwi_ref[...] = jnp.zeros_like(dwi_ref)
            carry_s[...] = jnp.zeros_like(carry_s)
            dxc_next_s[...] = jnp.zeros_like(dxc_next_s)

        inner = jnp.where(i == nt - 1, 0.0, 1.0)
        xc = xc_ref[...]
        r, ig, sp, a, sq, inv_sq = _lru_gates(xc, wr_ref, br_ref, wi_ref, bi_ref, lam_ref)

        row = lax.broadcasted_iota(jnp.int32, (tm, D), 0)
        u = dh_ref[...] + jnp.where(row == tm - 1, carry_s[...], 0.0)
        _scan_bwd_into(pltpu.roll(a, tm - 1, 0), u, dht_s)
        dht = dht_s[...]
        top_s[...] = a[:SUBLANES, :] * dht[:SUBLANES, :]
        carry_s[...] = top_s[0:1, :]

        hprev = hprev_ref[...] * inner
        da = dht * _shift_down(h_ref[...], 1, hprev)
        dig = dht * sq * xc
        dxc = dht * sq * ig
        dsq = dht * ig * xc
        dla = da * a - dsq * (a * a) * inv_sq
        dr = dla * ((-LRU_C) * sp)
        dpr = dr * r * (1.0 - r)
        dpi = dig * ig * (1.0 - ig)
        for n in range(NB):
            blk = slice(n * LANES, (n + 1) * LANES)
            xcb = xc[:, blk].astype(BF16)
            dwr_ref[n] += _dot_tn(xcb, dpr[:, blk].astype(BF16))
            dwi_ref[n] += _dot_tn(xcb, dpi[:, blk].astype(BF16))
        dxc = dxc + _gate_pre_t(dpr, wr_ref) + _gate_pre_t(dpi, wi_ref)

        xl = xl_ref[...]
        xlprev = xlprev_ref[...] * inner
        nxt = dxc_next_s[...]
        dxl = dxc * cw_ref[3:4, :]
        acc_ref[3:4, :] += jnp.sum(dxc * xl, axis=0, keepdims=True)
        for j in range(3):
            dxl = dxl + _shift_up(dxc, 3 - j, nxt) * cw_ref[j:j + 1, :]
            acc_ref[j:j + 1, :] += jnp.sum(dxc * _shift_down(xl, 3 - j, xlprev), axis=0, keepdims=True)
        dxc_next_s[...] = dxc[:SUBLANES, :]
        dxl_ref[...] = dxl.astype(BF16)

        acc_ref[4:5, :] += jnp.sum(dxc, axis=0, keepdims=True)
        acc_ref[5:6, :] += jnp.sum(dpr, axis=0, keepdims=True)
        acc_ref[6:7, :] += jnp.sum(dpi, axis=0, keepdims=True)
        acc_ref[7:8, :] += jnp.sum(dla * ((-LRU_C) * r), axis=0, keepdims=True)

        @pl.when(i == nt - 1)
        def _():
            lam_v = lam_ref[...]
            acc_ref[7:8, :] = acc_ref[7:8, :] * (-_sigmoid(-lam_v))

    rev = pl.BlockSpec((tm, D), lambda i: (nt - 1 - i, 0))
    prev8 = pl.BlockSpec((SUBLANES, D), lambda i: (jnp.maximum((nt - 1 - i) * per - 1, 0), 0))
    vec = _const_spec((1, D))
    wspec = _const_spec((NB, LANES, LANES))
    bf = jax.ShapeDtypeStruct((T, D), BF16)
    return _call(
        body, name="lru_bwd", grid=(nt,),
        in_specs=[rev, rev, prev8, rev, rev, prev8, _const_spec((4, D)), wspec, vec, wspec, vec, vec],
        out_specs=[rev, wspec, wspec, _const_spec((SUBLANES, D))],
        out_shape=[bf, jax.ShapeDtypeStruct((NB, LANES, LANES), F32), jax.ShapeDtypeStruct((NB, LANES, LANES), F32),
                   jax.ShapeDtypeStruct((SUBLANES, D), F32)],
        scratch_shapes=[pltpu.VMEM((1, D), F32), pltpu.VMEM((SUBLANES, D), F32), pltpu.VMEM((SUBLANES, D), F32),
                        pltpu.VMEM((tm, D), F32)],
        compiler_params=_cparams(("arbitrary",)),
    )(dh, h, h, xc, x_lru, x_lru, conv_w, w_r, b_r, w_i, b_i, lam)


def _chip_copies(srcs, dsts, send_sems, recv_sems):
    x, y, c = _position()
    chip = 2 * x + y
    na = len(srcs)
    return [pltpu.make_async_remote_copy(
        src_ref=srcs[a].at[2 * px + py], dst_ref=dsts[a].at[chip], send_sem=send_sems.at[j * na + a],
        recv_sem=recv_sems.at[j * na + a], device_id=(px, py, c), device_id_type=MESH)
        for j, (px, py) in enumerate(_other_chips(x, y)) for a in range(na)]


def _attn_bwd(q_aug, qx, k_aug, v_aug, do_aug, exchange=()):
    T = q_aug.shape[0]
    t = TA
    n = T // t
    hp = BWD_HEADS
    heads = range(hp)
    scale = DH ** -0.5
    ki_tab, qi_tab = _causal_pairs(n, q_major=False)
    last = ki_tab.shape[0] - 1
    ne = len(exchange)
    n_h = H // hp

    def body(ki_ref, qi_ref, q_ref, qx_ref, k_ref, v_ref, do_ref, *rest):
        sent, rest = rest[:ne], rest[ne:]
        dq_ref, dk_ref, dv_ref, dck_ref, dcq_ref = rest[:5]
        received, rest = rest[5:5 + ne], rest[5 + ne:]
        dq_s, dk_s, dv_s = rest[:3]
        j = pl.program_id(1)
        ki = ki_ref[j]
        qi = qi_ref[j]

        if ne:
            first_step = (pl.program_id(0) == 0) & (j == 0)
            last_step = (pl.program_id(0) == n_h - 1) & (j == last)

            @pl.when(first_step)
            def _():
                for cp in _chip_copies(sent, received, *rest[3:]):
                    cp.start()

            @pl.when(last_step)
            def _():
                for cp in _chip_copies(sent, received, *rest[3:]):
                    cp.wait()

        @pl.when(j == 0)
        def _():
            dq_s[...] = jnp.zeros_like(dq_s)

        @pl.when(qi == ki)
        def _():
            dk_s[...] = jnp.zeros_like(dk_s)
            dv_s[...] = jnp.zeros_like(dv_s)

        def step(on_diagonal):
            cols = [slice(a * AUG, (a + 1) * AUG) for a in heads]
            qb = [jnp.concatenate([q_ref[:, a * AUG:a * AUG + DH], qx_ref[:, a * DH:(a + 1) * DH]], axis=1)
                  for a in heads]
            st = [_dot_nt(k_ref[:, cols[a]], qb[a]) for a in heads]
            if on_diagonal:
                krow = lax.broadcasted_iota(jnp.int32, (t, t), 0)
                qcol = lax.broadcasted_iota(jnp.int32, (t, t), 1)
                st = [jnp.where(krow <= qcol, st[a], NEG) for a in heads]
            pt = [jnp.exp2(st[a]) for a in heads]
            dsb = [(pt[a] * _dot_nt(v_ref[:, cols[a]], do_ref[:, cols[a]])).astype(BF16) for a in heads]
            off = pl.multiple_of(qi * t, t)
            for a in heads:
                dv_s[a] += _dot(pt[a].astype(BF16), do_ref[:, cols[a]])
                dk_s[a] += _dot(dsb[a], qb[a])
                dq_s[a, pl.ds(off, t), :] += _dot_tn(dsb[a], k_ref[:, cols[a]])

        @pl.when(qi > ki)
        def _():
            step(False)

        @pl.when(qi == ki)
        def _():
            step(True)

        @pl.when(qi == n - 1)
        def _():
            for a in heads:
                dk_ref[:, a * DH:(a + 1) * DH] = (dk_s[a, :, :DH] * LN2).astype(BF16)
                dv_ref[:, a * DH:(a + 1) * DH] = dv_s[a, :, :DH].astype(BF16)
                dck_ref[a] = jnp.broadcast_to(dk_s[a, :, DH + 3:DH + 4], (t, LANES))

        @pl.when(j == last)
        def _():
            for a in heads:
                dq_ref[:, a * DH:(a + 1) * DH] = (dq_s[a, :, :DH] * scale).astype(BF16)
                dcq_ref[a] = jnp.broadcast_to(dq_s[a, :, DH:DH + 1], (T, LANES))

    qside = pl.BlockSpec((t, hp * AUG), lambda h, j, ki_ref, qi_ref: (qi_ref[j], h))
    qxside = pl.BlockSpec((t, hp * DH), lambda h, j, ki_ref, qi_ref: (qi_ref[j], h))
    kside = pl.BlockSpec((t, hp * AUG), lambda h, j, ki_ref, qi_ref: (ki_ref[j], h))
    kout = pl.BlockSpec((t, hp * DH), lambda h, j, ki_ref, qi_ref: (ki_ref[j], h))
    bf = jax.ShapeDtypeStruct((T, D), BF16)
    sums = jax.ShapeDtypeStruct((H, T, LANES), F32)
    grid_spec = pltpu.PrefetchScalarGridSpec(
        num_scalar_prefetch=2, grid=(n_h, ki_tab.shape[0]),
        in_specs=[qside, qxside, kside, kside, qside] + [HBM_SPEC] * ne,
        out_specs=[pl.BlockSpec((T, hp * DH), lambda h, j, ki_ref, qi_ref: (0, h)), kout, kout,
                   pl.BlockSpec((hp, t, LANES), lambda h, j, ki_ref, qi_ref: (h, ki_ref[j], 0)),
                   pl.BlockSpec((hp, T, LANES), lambda h, j, ki_ref, qi_ref: (h, 0, 0))] + [HBM_SPEC] * ne,
        scratch_shapes=[pltpu.VMEM((hp, T, AUG), F32), pltpu.VMEM((hp, t, AUG), F32), pltpu.VMEM((hp, t, AUG), F32)]
        + ([pltpu.SemaphoreType.DMA((3 * ne,)), pltpu.SemaphoreType.DMA((3 * ne,))] if ne else []))
    outs = _call(
        body, name="attn_bwd", grid_spec=grid_spec,
        out_shape=[bf, bf, bf, sums, sums] + [jax.ShapeDtypeStruct(s.shape, s.dtype) for s in exchange],
        compiler_params=_cparams(("arbitrary", "arbitrary"), VMEM_BIG),
    )(ki_tab, qi_tab, q_aug, qx, k_aug, v_aug, do_aug, *exchange)
    return (*outs[:5], list(outs[5:]))


def _fgate_bwd(dc_key, dc_query, flb):
    T = flb.shape[0]
    tm = TM
    nt = T // tm

    def body(dck_ref, dcq_ref, flb_ref, dfl_ref, acc_ref, carry, top_s):
        @pl.when(pl.program_id(0) == 0)
        def _():
            carry[...] = jnp.zeros_like(carry)
            acc_ref[...] = jnp.zeros_like(acc_ref)

        flb = flb_ref[...]
        lane = lax.broadcasted_iota(jnp.int32, flb.shape, 1)
        dc = jnp.zeros(flb.shape, F32)
        for hd in range(H):
            dc = dc + jnp.where(lane == hd, dcq_ref[hd] - dck_ref[hd], 0.0)
        r = lax.broadcasted_iota(jnp.int32, (tm, tm), 0)
        c = lax.broadcasted_iota(jnp.int32, (tm, tm), 1)
        dls = _dot_exact((c >= r).astype(F32), dc) + carry[...]
        top_s[...] = dls[:SUBLANES, :]
        carry[...] = top_s[0:1, :]
        dfl = jnp.where(lane < H, dls * _sigmoid(-flb), 0.0)
        dfl_ref[...] = dfl.astype(BF16)
        acc_ref[0:1, :] += jnp.sum(dfl, axis=0, keepdims=True)

    rev = pl.BlockSpec((tm, LANES), lambda i: (nt - 1 - i, 0))
    return _call(
        body, name="fgate_bwd", grid=(nt,),
        in_specs=[pl.BlockSpec((H, tm, LANES), lambda i: (0, nt - 1 - i, 0))] * 2 + [rev],
        out_specs=[rev, _const_spec((SUBLANES, LANES))],
        out_shape=[jax.ShapeDtypeStruct((T, LANES), BF16), jax.ShapeDtypeStruct((SUBLANES, LANES), F32)],
        scratch_shapes=[pltpu.VMEM((1, LANES), F32), pltpu.VMEM((SUBLANES, LANES), F32)],
        compiler_params=_cparams(("arbitrary",)),
    )(dc_key, dc_query, flb)


def _dx(dz, dfl, w_a, w_f, w_b, x, pre_gain, dh1):
    T = x.shape[0]
    tm = TM

    def body(*refs):
        dz_refs = refs[:6]
        dfl_ref, wa_ref, wf_ref, wb_ref, x_ref, g_ref, dh1_ref, gx_ref, acc_ref = refs[6:]

        @pl.when(pl.program_id(0) == 0)
        def _():
            acc_ref[...] = jnp.zeros_like(acc_ref)

        dxn = _dot(dfl_ref[...], wf_ref[...])
        for s in range(3):
            dxn = dxn + _dot(dz_refs[s][...], wa_ref[s * D:(s + 1) * D, :])
            dxn = dxn + _dot(dz_refs[3 + s][...], wb_ref[s * D:(s + 1) * D, :])
        xv = x_ref[...]
        rstd = _rstd(xv)
        xhat = xv * rstd
        gx_ref[...] = dh1_ref[...] + _rms_bwd(dxn * g_ref[...], xhat, rstd)
        acc_ref[0:1, :] += jnp.sum(dxn * xhat, axis=0, keepdims=True)

    return _call(
        body, name="dx", grid=(T // tm,),
        in_specs=[_row_spec(tm, D)] * 6 + [_row_spec(tm, LANES), _const_spec((3 * D, D)), _const_spec((LANES, D)),
                                           _const_spec((3 * D, D)), _row_spec(tm, D), _const_spec((1, D)),
                                           _row_spec(tm, D)],
        out_specs=[_row_spec(tm, D), _const_spec((SUBLANES, D))],
        out_shape=[jax.ShapeDtypeStruct((T, D), F32), jax.ShapeDtypeStruct((SUBLANES, D), F32)],
        compiler_params=_cparams(("arbitrary",), VMEM_BIG),
    )(*dz, dfl, w_a, w_f, w_b, x, pre_gain, dh1)


GRAD_ROWS = D_IN + SUBLANES


def _dw_in_segment(dz_s, xn, buf, s, bt):
    T = xn.shape[0]
    row0 = s * D + (H if s >= 3 else 0)

    def body(*refs):
        dz_ref, xn_ref, o_ref = refs[0], refs[1], refs[-1]

        @pl.when(pl.program_id(0) == 0)
        def _():
            o_ref[...] = jnp.zeros_like(o_ref)

        o_ref[...] += _dot_tn(dz_ref[...], xn_ref[...])

    tok = pl.BlockSpec((bt, D), lambda t: (t, 0))
    return _call(
        body, name="dw_in_%d" % s, grid=(T // bt,),
        in_specs=[tok, tok] + ([] if buf is None else [pl.BlockSpec(memory_space=pl.ANY)]),
        out_specs=pl.BlockSpec((pl.Element(D), pl.Element(D)), lambda t: (row0, 0)),
        out_shape=jax.ShapeDtypeStruct((GRAD_ROWS, D), F32),
        input_output_aliases={} if buf is None else {2: 0},
        compiler_params=_cparams(("arbitrary",)),
    )(*((dz_s, xn) if buf is None else (dz_s, xn, buf)))


def _dw_in_t(dz, dfl, xn, bt=512):
    T = xn.shape[0]
    nt = T // bt
    main = None
    for s in range(6):
        main = _dw_in_segment(dz[s], xn, main, s, min(T, 2048))

    def f_body(dfl_ref, xn_ref, main_ref, o_ref, acc_s):
        p = pl.program_id(0)
        t = pl.program_id(1)

        @pl.when(t == 0)
        def _():
            acc_s[...] = jnp.zeros_like(acc_s)

        @pl.when(p == 0)
        def _():
            acc_s[...] += _dot_tn(dfl_ref[...], xn_ref[...])

        @pl.when(t == nt - 1)
        def _():
            o_ref[...] = acc_s[:SUBLANES, :]

    fl_block = FL0 // SUBLANES
    end_block = D_IN // SUBLANES
    return _call(
        f_body, name="dw_in_f", grid=(2, nt),
        in_specs=[pl.BlockSpec((bt, LANES), lambda p, t: (t, 0)), pl.BlockSpec((bt, D), lambda p, t: (t, 0)),
                  pl.BlockSpec(memory_space=pl.ANY)],
        out_specs=pl.BlockSpec((SUBLANES, D), lambda p, t: (fl_block + p * (end_block - fl_block), 0)),
        out_shape=jax.ShapeDtypeStruct((GRAD_ROWS, D), F32),
        scratch_shapes=[pltpu.VMEM((LANES, D), F32)],
        input_output_aliases={2: 0},
        compiler_params=_cparams(("arbitrary", "arbitrary")),
    )(dfl, xn, main)


def _matmul_tn(a, b, name, bm=512, bn=1024, bt=2048):
    T, M = a.shape
    N = b.shape[1]
    bm, bn, bt = min(bm, M), min(bn, N), min(bt, T)

    def body(a_ref, b_ref, o_ref):
        @pl.when(pl.program_id(2) == 0)
        def _():
            o_ref[...] = jnp.zeros_like(o_ref)

        o_ref[...] += _dot_tn(a_ref[...], b_ref[...])

    return _call(
        body, name=name, grid=(M // bm, N // bn, T // bt),
        in_specs=[pl.BlockSpec((bt, bm), lambda i, j, t: (t, i)), pl.BlockSpec((bt, bn), lambda i, j, t: (t, j))],
        out_specs=pl.BlockSpec((bm, bn), lambda i, j, t: (i, j)),
        out_shape=jax.ShapeDtypeStruct((M, N), F32),
        compiler_params=_cparams(("parallel", "parallel", "arbitrary")),
    )(a, b)


HBM_SPEC = pl.BlockSpec(memory_space=pltpu.HBM)
VMEM_SPEC = pl.BlockSpec(memory_space=pltpu.VMEM)


def _position():
    return lax.axis_index("x"), lax.axis_index("y"), lax.axis_index("c")


def _other_chips(x, y):
    return [(1 - x, y), (x, 1 - y), (1 - x, 1 - y)]


def _gather_shards(shards, whole):
    na, nw = len(shards), len(whole)
    nall = na + nw

    def body(*refs):
        gather = _GatherPlan(refs[:nall], refs[nall:2 * nall], refs[2 * nall:], na)
        gather.send()
        gather.forward()
        gather.finish()

    arrs = list(shards) + list(whole)
    outs = _call(
        body, name="gather_shards",
        in_specs=[HBM_SPEC] * nall, out_specs=[HBM_SPEC] * nall,
        out_shape=_gather_out_shapes(arrs), scratch_shapes=_gather_semaphores(na, nall),
    )(*arrs)
    return _place_own(outs, arrs)


def _gather_out_shapes(arrs):
    return [jax.ShapeDtypeStruct((N_CHIPS,) + s.shape, s.dtype) for s in arrs]


def _gather_semaphores(na, nall):
    return [pltpu.SemaphoreType.DMA((3 * nall,)), pltpu.SemaphoreType.DMA((3 * nall,)),
            pltpu.SemaphoreType.DMA((3 * na,)), pltpu.SemaphoreType.DMA((3 * na,))]


def _place_own(outs, arrs):
    if not arrs:
        return []
    chip = 2 * lax.axis_index("x") + lax.axis_index("y")
    return [lax.dynamic_update_slice(o, a[None], (chip,) + (0,) * a.ndim) for o, a in zip(outs, arrs)]


class _GatherPlan:
    def __init__(self, srcs, dsts, sems, na):
        ici_send, ici_recv, d2d_send, d2d_recv = sems
        x, y, c = _position()
        chip = 2 * x + y
        nall = len(srcs)

        def half(a, which):
            rows = srcs[a].shape[0] // 2
            return pl.ds(pl.multiple_of(which * rows, 16), rows)

        def copy(src, dst, send, recv, k, to):
            return pltpu.make_async_remote_copy(src_ref=src, dst_ref=dst, send_sem=send.at[k], recv_sem=recv.at[k],
                                                device_id=to, device_id_type=MESH)

        self.first, self.landed, self.passed, self.returned = [], [], [], []
        for j, (px, py) in enumerate(_other_chips(x, y)):
            theirs = 2 * px + py
            for a in range(nall):
                k = j * nall + a
                if a < na:
                    self.first.append(copy(srcs[a].at[half(a, c), :], dsts[a].at[chip, half(a, c), :],
                                           ici_send, ici_recv, k, (px, py, c)))
                    mine = dsts[a].at[theirs, half(a, c), :]
                    other = dsts[a].at[theirs, half(a, 1 - c), :]
                    self.landed.append(copy(mine, mine, ici_send, ici_recv, k, (px, py, c)))
                    self.passed.append(copy(mine, mine, d2d_send, d2d_recv, j * na + a, (x, y, 1 - c)))
                    self.returned.append(copy(other, other, d2d_send, d2d_recv, j * na + a, (x, y, 1 - c)))
                else:
                    self.first.append(copy(srcs[a], dsts[a].at[chip], ici_send, ici_recv, k, (px, py, c)))
                    got = dsts[a].at[theirs]
                    self.landed.append(copy(got, got, ici_send, ici_recv, k, (px, py, c)))
                    self.passed.append(None)

    def send(self):
        for cp in self.first:
            cp.start()

    def forward(self):
        for arrival, fwd in zip(self.landed, self.passed):
            arrival.wait_recv()
            if fwd is not None:
                fwd.start()

    def finish(self):
        for cp in self.returned:
            cp.wait_recv()
        for cp in self.first + [f for f in self.passed if f is not None]:
            cp.wait_send()


W_ROWS = 1568
G_ROWS = 1552
SHARD_ROWS = D_IN // N_CHIPS
WINDOW_STEP = 1536


def _assemble_w_in(cont):
    cb = 256
    half = WINDOW_STEP

    def body(c_ref, wa_ref, wf_ref, wb_ref):
        x0 = c_ref[0].astype(F32)
        x1, x2, x3 = (pltpu.roll(c_ref[j].astype(F32), 2 * j, 0) for j in (1, 2, 3))
        wa = jnp.concatenate([x0[:half], x0[half:half + 16] + x1[:16], x1[16:half]], axis=0)
        wa_ref[...] = wa.astype(BF16)

        fl = x1[half:half + 16] + x2[:16]
        row = lax.broadcasted_iota(jnp.int32, fl.shape, 0)
        wf_ref[:16, :] = jnp.where(row < H, fl, 0.0).astype(BF16)
        wf_ref[16:, :] = jnp.zeros((LANES - 16, cb), BF16)

        mid = x2[half:half + SUBLANES] + x3[:SUBLANES]
        wb = jnp.concatenate([x2[SUBLANES:half], mid, x3[SUBLANES:half + SUBLANES]], axis=0)
        wb_ref[...] = wb.astype(BF16)

    return _call(
        body, name="assemble_w_in", grid=(D // cb,),
        in_specs=[pl.BlockSpec((N_CHIPS, W_ROWS, cb), lambda i: (0, 0, i))],
        out_specs=[pl.BlockSpec((3 * D, cb), lambda i: (0, i)), pl.BlockSpec((LANES, cb), lambda i: (0, i)),
                   pl.BlockSpec((3 * D, cb), lambda i: (0, i))],
        out_shape=[jax.ShapeDtypeStruct((3 * D, D), BF16), jax.ShapeDtypeStruct((LANES, D), BF16),
                   jax.ShapeDtypeStruct((3 * D, D), BF16)],
        compiler_params=_cparams(("parallel",)),
    )(cont)


def _pair_exchange_windows(grad_t):
    half_g = G_ROWS // 2

    def body(g_ref, got, send_sems, recv_sems):
        x, y, c = _position()
        copies = []
        for j in range(N_CHIPS):
            rows = pl.ds(pl.multiple_of(j * WINDOW_STEP + (1 - c) * half_g, SUBLANES), half_g)
            copies.append(pltpu.make_async_remote_copy(
                src_ref=g_ref.at[rows, :], dst_ref=got.at[j], send_sem=send_sems.at[j], recv_sem=recv_sems.at[j],
                device_id=(x, y, 1 - c), device_id_type=MESH))
        for cp in copies:
            cp.start()
        for cp in copies:
            cp.wait()

    return _call(
        body, name="pair_exchange_w_in",
        in_specs=[HBM_SPEC], out_specs=HBM_SPEC,
        out_shape=jax.ShapeDtypeStruct((N_CHIPS, half_g, D), F32),
        scratch_shapes=[pltpu.SemaphoreType.DMA((N_CHIPS,)), pltpu.SemaphoreType.DMA((N_CHIPS,))],
    )(grad_t)


def _pair_exchange(parts):
    na = len(parts)

    def body(*refs):
        srcs, got = refs[:na], refs[na:2 * na]
        send_sems, recv_sems = refs[2 * na:]
        x, y, c = _position()
        copies = []
        for a in range(na):
            half = srcs[a].shape[1] // 2
            rows = pl.ds(pl.multiple_of((1 - c) * half, SUBLANES), half)
            copies.append(pltpu.make_async_remote_copy(
                src_ref=srcs[a].at[:, rows, :], dst_ref=got[a], send_sem=send_sems.at[a], recv_sem=recv_sems.at[a],
                device_id=(x, y, 1 - c), device_id_type=MESH))
        for cp in copies:
            cp.start()
        for cp in copies:
            cp.wait()

    return _call(
        body, name="pair_exchange",
        in_specs=[HBM_SPEC] * na, out_specs=[HBM_SPEC] * na,
        out_shape=[jax.ShapeDtypeStruct((s.shape[0], s.shape[1] // 2, s.shape[2]), s.dtype) for s in parts],
        scratch_shapes=[pltpu.SemaphoreType.DMA((na,)), pltpu.SemaphoreType.DMA((na,))],
    )(*parts)


def _pair_sum(part, got, c, name):
    _, half, C = got.shape
    cb = min(C, 256)

    def body(c_ref, a_ref, b_ref, o_ref):
        o_ref[...] = (a_ref[...] + b_ref[...]).astype(BF16)

    spec = pl.BlockSpec((1, half, cb), lambda j, i, c_ref: (j, 0, i))
    grid_spec = pltpu.PrefetchScalarGridSpec(
        num_scalar_prefetch=1, grid=(N_CHIPS, C // cb),
        in_specs=[pl.BlockSpec((1, half, cb), lambda j, i, c_ref: (j, c_ref[0], i)), spec], out_specs=spec)
    return _call(
        body, name=name, grid_spec=grid_spec,
        out_shape=jax.ShapeDtypeStruct((N_CHIPS, half, C), BF16),
        compiler_params=_cparams(("parallel", "parallel")),
    )(c.reshape(1), part, got)


def _pair_sum_windows(grad_t, got, c):
    _, half, C = got.shape
    cb = 256

    def body(c_ref, a_ref, b_ref, o_ref):
        o_ref[0] = (a_ref[...] + b_ref[0]).astype(BF16)

    def mine(j, i, c_ref):
        return ((j * (WINDOW_STEP // SUBLANES) + c_ref[0] * (half // SUBLANES)) * SUBLANES, i * cb)

    spec = pl.BlockSpec((1, half, cb), lambda j, i, c_ref: (j, 0, i))
    grid_spec = pltpu.PrefetchScalarGridSpec(
        num_scalar_prefetch=1, grid=(N_CHIPS, C // cb),
        in_specs=[pl.BlockSpec((pl.Element(half), pl.Element(cb)), mine), spec], out_specs=spec)
    return _call(
        body, name="pair_sum_w_in", grid_spec=grid_spec,
        out_shape=jax.ShapeDtypeStruct((N_CHIPS, half, C), BF16),
        compiler_params=_cparams(("parallel", "parallel")),
    )(c.reshape(1), grad_t, got)


def _chip_exchange(sums):
    na = len(sums)

    def body(*refs):
        copies = _chip_copies(refs[:na], refs[na:2 * na], *refs[2 * na:])
        for cp in copies:
            cp.start()
        for cp in copies:
            cp.wait()

    return _call(
        body, name="chip_exchange",
        in_specs=[HBM_SPEC] * na, out_specs=[HBM_SPEC] * na,
        out_shape=[jax.ShapeDtypeStruct(s.shape, s.dtype) for s in sums],
        scratch_shapes=[pltpu.SemaphoreType.DMA((3 * na,)), pltpu.SemaphoreType.DMA((3 * na,))],
    )(*sums)


def _chip_sum(own, got, chip, name):
    _, half, C = got.shape
    cb = min(C, 256)

    def body(chip_ref, own_ref, g_ref, o_ref):
        for me in range(N_CHIPS):
            @pl.when(chip_ref[0] == me)
            def _(me=me):
                terms = [own_ref[0] if k == me else g_ref[k] for k in range(N_CHIPS)]
                acc = terms[0].astype(F32) + terms[1].astype(F32)
                acc = acc + terms[2].astype(F32)
                o_ref[...] = acc + terms[3].astype(F32)

    grid_spec = pltpu.PrefetchScalarGridSpec(
        num_scalar_prefetch=1, grid=(C // cb,),
        in_specs=[pl.BlockSpec((1, half, cb), lambda i, chip_ref: (chip_ref[0], 0, i)),
                  pl.BlockSpec((N_CHIPS, half, cb), lambda i, chip_ref: (0, 0, i))],
        out_specs=pl.BlockSpec((half, cb), lambda i, chip_ref: (0, i)))
    return _call(
        body, name=name, grid_spec=grid_spec,
        out_shape=jax.ShapeDtypeStruct((half, C), F32),
        compiler_params=_cparams(("parallel",)),
    )(chip.reshape(1), own, got)


def _pair_swap(halves):
    na = len(halves)

    def body(*refs):
        srcs, dsts = refs[:na], refs[na:2 * na]
        send_sems, recv_sems = refs[2 * na:]
        x, y, c = _position()
        copies = [pltpu.make_async_remote_copy(
            src_ref=srcs[a], dst_ref=dsts[a], send_sem=send_sems.at[a], recv_sem=recv_sems.at[a],
            device_id=(x, y, 1 - c), device_id_type=MESH) for a in range(na)]
        for cp in copies:
            cp.start()
        for cp in copies:
            cp.wait()

    return _call(
        body, name="pair_swap",
        in_specs=[HBM_SPEC] * na, out_specs=[HBM_SPEC] * na,
        out_shape=[jax.ShapeDtypeStruct(s.shape, s.dtype) for s in halves],
        scratch_shapes=[pltpu.SemaphoreType.DMA((na,)), pltpu.SemaphoreType.DMA((na,))],
    )(*halves)


def _allreduce_small(g):
    rows = g.shape[0]
    per = rows // N_DEV

    def body(g_ref, out_ref, got_ref, s1, r1, s2, r2):
        x, y, c = _position()
        me = 4 * x + 2 * y + c
        mine = pl.ds(pl.multiple_of(me * per, SUBLANES), per)
        peers = []
        for j in range(1, N_DEV):
            px = 1 - x if j & 4 else x
            py = 1 - y if j & 2 else y
            pc = 1 - c if j & 1 else c
            peers.append((px, py, pc))

        first = []
        for j, (px, py, pc) in enumerate(peers):
            theirs = pl.ds(pl.multiple_of((4 * px + 2 * py + pc) * per, SUBLANES), per)
            first.append(pltpu.make_async_remote_copy(
                src_ref=g_ref.at[theirs, :], dst_ref=got_ref.at[me], send_sem=s1.at[j], recv_sem=r1.at[j],
                device_id=(px, py, pc), device_id_type=MESH))
        for cp in first:
            cp.start()
        got_ref[me] = g_ref[mine, :]
        for cp in first:
            cp.wait()
        total = got_ref[0]
        for d in range(1, N_DEV):
            total = total + got_ref[d]
        out_ref[mine, :] = total

        second = []
        for j, peer in enumerate(peers):
            second.append(pltpu.make_async_remote_copy(
                src_ref=out_ref.at[mine, :], dst_ref=out_ref.at[mine, :], send_sem=s2.at[j], recv_sem=r2.at[j],
                device_id=peer, device_id_type=MESH))
        for cp in second:
            cp.start()
        for cp in second:
            cp.wait()

    sems = pltpu.SemaphoreType.DMA((N_DEV - 1,))
    return _call(
        body, name="allreduce_small", in_hbm=False,
        in_specs=[VMEM_SPEC], out_specs=VMEM_SPEC,
        out_shape=jax.ShapeDtypeStruct(g.shape, F32),
        scratch_shapes=[pltpu.VMEM((N_DEV, per, LANES), F32), sems, sems, sems, sems],
    )(g)


def _adamw_math(g, w, m, v):
    m2 = ADAM_B1 * m + (1.0 - ADAM_B1) * g
    v2 = ADAM_B2 * v + (1.0 - ADAM_B2) * (g * g)
    m_hat = m2 / (1.0 - ADAM_B1 ** ADAM_STEP)
    v_hat = v2 / (1.0 - ADAM_B2 ** ADAM_STEP)
    delta = (-ADAM_LR) * (m_hat / (jnp.sqrt(v_hat) + ADAM_EPS) + ADAM_WD * w)
    return delta, m2, v2


def _adamw_big(g, w, m, v, name):
    R, C = g.shape
    cb = min(C, LANES)

    def body(g_ref, w_ref, m_ref, v_ref, d_ref, m2_ref, v2_ref):
        d_ref[...], m2_ref[...], v2_ref[...] = _adamw_math(g_ref[...], w_ref[...], m_ref[...], v_ref[...])

    spec = pl.BlockSpec((R, cb), lambda i: (0, i))
    out = jax.ShapeDtypeStruct((R, C), F32)
    return _call(
        body, name=name, grid=(C // cb,),
        in_specs=[spec] * 4, out_specs=[spec] * 3, out_shape=[out] * 3,
        compiler_params=_cparams(("parallel",)),
    )(g, w, m, v)


def _adamw_small(gs, ws, ms, vs):
    n = len(gs)

    def body(*refs):
        for a in range(n):
            g_ref, w_ref, m_ref, v_ref = (refs[k * n + a] for k in range(4))
            d_ref, m2_ref, v2_ref = (refs[(4 + k) * n + a] for k in range(3))
            d_ref[...], m2_ref[...], v2_ref[...] = _adamw_math(g_ref[...], w_ref[...], m_ref[...], v_ref[...])

    outs = [jax.ShapeDtypeStruct(w.shape, F32) for w in ws]
    specs = [_const_spec(w.shape) for w in ws]
    return _call(
        body, name="adamw_small", grid=(1,),
        in_specs=specs * 4, out_specs=specs * 3, out_shape=outs * 3,
    )(*gs, *ws, *ms, *vs)


def _late_weights(st_out, st_ple, st_gate, st_conv):
    return st_out.reshape(DMIX, D), _from_chip_cols(st_ple), st_gate.reshape(D, D), _from_chip_cols(st_conv)


def _local_step(x, p, tgt, w_a, w_f, w_b, late, b_f, pre_gain, post_gain, conv_b,
                w_rgate, b_rgate, w_igate, b_igate, lam, gain_a, gain_l, ple_gain, b_gate,
                gather_late=False, early_reduce=None):
    b_f_pad = jnp.pad(b_f, ((0, 0), (0, LANES - H)))
    w_r = w_rgate.astype(BF16)
    w_i = w_igate.astype(BF16)

    xn, q_aug, k_aug, v_aug, g_attn, x_lru, g_lru, flb, vt_aug = _in_proj(x, pre_gain, w_a, w_f, w_b, b_f_pad)
    if gather_late:
        o, qx, stacks = _attn_fwd(q_aug, k_aug, vt_aug, late[:3], late[3:])
        late = _late_weights(*stacks)
    else:
        o, qx, _ = _attn_fwd(q_aug, k_aug, vt_aug)
    w_out_b, w_ple_b, w_gate_b, conv_w = late
    ycat, xc, h = _branches_fwd(o, g_attn, x_lru, g_lru, gain_a, gain_l, conv_w, conv_b, w_r, b_rgate, w_i, b_igate,
                                lam)
    dh1, dycat, dmix, h1b, dgp, pb, dpe, acc_t = _tail(ycat, x, p, tgt, w_out_b, post_gain, w_ple_b, ple_gain,
                                                       w_gate_b, b_gate)
    late_grads = [_matmul_tn(ycat, dmix, "dw_out"), _matmul_tn(pb, dpe, "dw_ple"),
                  _matmul_tn(h1b, dgp, "dw_ple_gate")]
    do_aug, dg_attn, dg_lru, dh, acc_b = _branches_bwd(dycat, o, g_attn, h, g_lru, gain_a, gain_l)
    dx_lru, gw_r, gw_i, acc_l = _lru_bwd(dh, h, xc, x_lru, conv_w, w_r, b_rgate, w_i, b_igate, lam)
    if early_reduce is None:
        dq, dk, dv, dc_key, dc_query, _ = _attn_bwd(q_aug, qx, k_aug, v_aug, do_aug)
    else:
        sent = early_reduce(late_grads)
        dq, dk, dv, dc_key, dc_query, received = _attn_bwd(q_aug, qx, k_aug, v_aug, do_aug, sent)
        late_grads = list(zip(sent, received))
    dfl, acc_f = _fgate_bwd(dc_key, dc_query, flb)
    dz = (dq, dk, dv, dg_attn, dx_lru, dg_lru)
    grad_x, acc_x = _dx(dz, dfl, w_a, w_f, w_b, x, pre_gain, dh1)

    grads = dict(
        w_in_t=_dw_in_t(dz, dfl, xn),
        w_out=late_grads[0],
        w_ple=late_grads[1],
        w_ple_gate=late_grads[2],
        w_rgate=gw_r,
        w_igate=gw_i,
        b_f=acc_f[0:1, :H],
        pre_gain=acc_x[0:1],
        post_gain=acc_t[0:1],
        conv_w=acc_l[0:4],
        conv_b=acc_l[4:5],
        b_rgate=acc_l[5:6],
        b_igate=acc_l[6:7],
        lru_lambda=acc_l[7:8],
        attn_out_gain=acc_b[0:1],
        lru_out_gain=acc_b[1:2],
        ple_gain=acc_t[1:2],
        b_ple_gate=acc_t[2:3],
    )
    loss = jnp.sum(acc_t[3])
    return loss, grad_x, grads


SMALL_ROWS = ["b_f", "pre_gain", "post_gain", "conv_w", "conv_b", "b_rgate", "b_igate", "lru_lambda",
              "attn_out_gain", "lru_out_gain", "ple_gain", "b_ple_gate"]
WEIGHTS = ["w_in", "b_f", "pre_gain", "post_gain", "conv_w", "conv_b", "w_rgate", "b_rgate", "w_igate", "b_igate",
           "lru_lambda", "attn_out_gain", "lru_out_gain", "w_out", "w_ple", "ple_gain", "w_ple_gate", "b_ple_gate"]
SHARDED = ["w_in", "w_out", "w_ple", "w_ple_gate"]


def _by_chip_cols(g):
    r, cols = g.shape
    return g.reshape(r, N_CHIPS, cols // N_CHIPS).transpose(1, 0, 2)


def _from_chip_cols(s):
    n, r, cols = s.shape
    return s.transpose(1, 0, 2).reshape(r, n * cols)


def kernel(x, p, w_in, b_f, pre_gain, post_gain, conv_w, conv_b, w_rgate, b_rgate, w_igate, b_igate, lru_lambda, attn_out_gain, lru_out_gain, w_out, w_ple, ple_gain, w_ple_gate, b_ple_gate, loss_target, m_w_in, m_b_f, m_pre_gain, m_post_gain, m_conv_w, m_conv_b, m_w_rgate, m_b_rgate, m_w_igate, m_b_igate, m_lru_lambda, m_attn_out_gain, m_lru_out_gain, m_w_out, m_w_ple, m_ple_gain, m_w_ple_gate, m_b_ple_gate, v_w_in, v_b_f, v_pre_gain, v_post_gain, v_conv_w, v_conv_b, v_w_rgate, v_b_rgate, v_w_igate, v_b_igate, v_lru_lambda, v_attn_out_gain, v_lru_out_gain, v_w_out, v_w_ple, v_ple_gain, v_w_ple_gate, v_b_ple_gate):
    w = dict(w_in=w_in, b_f=b_f, pre_gain=pre_gain, post_gain=post_gain, conv_w=conv_w, conv_b=conv_b,
             w_rgate=w_rgate, b_rgate=b_rgate, w_igate=w_igate, b_igate=b_igate, lru_lambda=lru_lambda,
             attn_out_gain=attn_out_gain, lru_out_gain=lru_out_gain, w_out=w_out, w_ple=w_ple, ple_gain=ple_gain,
             w_ple_gate=w_ple_gate, b_ple_gate=b_ple_gate)
    m = dict(w_in=m_w_in, b_f=m_b_f, pre_gain=m_pre_gain, post_gain=m_post_gain, conv_w=m_conv_w, conv_b=m_conv_b,
             w_rgate=m_w_rgate, b_rgate=m_b_rgate, w_igate=m_w_igate, b_igate=m_b_igate, lru_lambda=m_lru_lambda,
             attn_out_gain=m_attn_out_gain, lru_out_gain=m_lru_out_gain, w_out=m_w_out, w_ple=m_w_ple,
             ple_gain=m_ple_gain, w_ple_gate=m_w_ple_gate, b_ple_gate=m_b_ple_gate)
    v = dict(w_in=v_w_in, b_f=v_b_f, pre_gain=v_pre_gain, post_gain=v_post_gain, conv_w=v_conv_w, conv_b=v_conv_b,
             w_rgate=v_w_rgate, b_rgate=v_b_rgate, w_igate=v_w_igate, b_igate=v_b_igate, lru_lambda=v_lru_lambda,
             attn_out_gain=v_attn_out_gain, lru_out_gain=v_lru_out_gain, w_out=v_w_out, w_ple=v_w_ple,
             ple_gain=v_ple_gain, w_ple_gate=v_w_ple_gate, b_ple_gate=v_b_ple_gate)
    xi, yi, ci = _position()
    chip = 2 * xi + yi

    w_in_t, m_in_t, v_in_t = (jnp.swapaxes(t[0], 0, 1) for t in (w_in, m_w_in, v_w_in))
    window = jnp.pad(w_in_t.astype(BF16), ((0, W_ROWS - SHARD_ROWS), (0, 0)))

    (st_in,) = _gather_shards([window], [])
    w_a, w_f, w_b = _assemble_w_in(st_in)
    late_shards = (w_out[0].astype(BF16), w_ple[0].astype(BF16), w_ple_gate[0].astype(BF16), conv_w[0])

    def early_reduce(local):
        parts = [local[0].reshape(N_CHIPS, DMIX // N_CHIPS, D), _by_chip_cols(local[1]),
                 local[2].reshape(N_CHIPS, D // N_CHIPS, D)]
        got = _pair_exchange(parts)
        return [_pair_sum(parts[a], got[a], ci, "pair_sum_%d" % a) for a in range(3)]

    loss, grad_x, g = _local_step(
        x[0], p[0, 0], loss_target[0], w_a, w_f, w_b, late_shards, b_f, pre_gain, post_gain,
        conv_b, w_rgate[0], b_rgate, w_igate[0], b_igate, lru_lambda, attn_out_gain, lru_out_gain, ple_gain,
        b_ple_gate, gather_late=True, early_reduce=early_reduce)
    loss = lax.psum(loss, ("x", "y", "c"))

    sum_in = _pair_sum_windows(g["w_in_t"], _pair_exchange_windows(g["w_in_t"]), ci)
    (recv_in,) = _chip_exchange([sum_in])
    sums = [sum_in] + [g[n][0] for n in SHARDED[1:]]
    recv = [recv_in] + [g[n][1] for n in SHARDED[1:]]
    halves = [_chip_sum(sums[a], recv[a], chip, "chip_sum_%d" % a) for a in range(4)]
    theirs = _pair_swap(halves)
    full = [jnp.concatenate([jnp.where(ci == 0, a, b), jnp.where(ci == 0, b, a)], axis=0)
            for a, b in zip(halves, theirs)]
    red = dict(zip(SHARDED, full))
    red["w_in"] = lax.dynamic_slice_in_dim(red["w_in"], 2 * chip, SHARD_ROWS, axis=0)

    rows = [jnp.pad(g["b_f"], ((0, 0), (0, D - H)))] + [g[n] for n in SMALL_ROWS[1:]]
    rows.append(jnp.zeros((16 - sum(r.shape[0] for r in rows), D), F32))
    packed = jnp.concatenate([g["w_rgate"].reshape(NB * LANES, LANES), g["w_igate"].reshape(NB * LANES, LANES),
                              jnp.concatenate(rows, axis=0).reshape(LANES, LANES)], axis=0)
    summed = _allreduce_small(packed)
    red["w_rgate"] = summed[:D].reshape(1, NB, LANES, LANES)
    red["w_igate"] = summed[D:2 * D].reshape(1, NB, LANES, LANES)
    vec = summed[2 * D:].reshape(16, D)
    r0 = 0
    for n in SMALL_ROWS:
        nr = 4 if n == "conv_w" else 1
        red[n] = vec[r0:r0 + nr]
        r0 += nr
    red["b_f"] = red["b_f"][:, :H]
    red["conv_w"] = lax.dynamic_slice_in_dim(red["conv_w"], chip * (D // N_CHIPS), D // N_CHIPS, axis=1)[None]

    delta, new_m, new_v = {}, {}, {}
    outs_in = _adamw_big(red["w_in"], w_in_t, m_in_t, v_in_t, "adamw_w_in")
    delta["w_in"], new_m["w_in"], new_v["w_in"] = (jnp.swapaxes(t, 0, 1)[None] for t in outs_in)
    red["w_in"] = jnp.swapaxes(red["w_in"], 0, 1)[None]
    for n in SHARDED[1:]:
        delta[n], new_m[n], new_v[n] = (t[None] for t in _adamw_big(red[n], w[n][0], m[n][0], v[n][0], "adamw_" + n))
        red[n] = red[n][None]
    small = [n for n in WEIGHTS if n not in SHARDED]
    outs = _adamw_small([red[n] for n in small], [w[n] for n in small], [m[n] for n in small],
                        [v[n] for n in small])
    ns = len(small)
    for a, n in enumerate(small):
        delta[n], new_m[n], new_v[n] = outs[a], outs[ns + a], outs[2 * ns + a]

    return (loss, grad_x[None], *[red[n] for n in WEIGHTS], *[delta[n] for n in WEIGHTS],
            *[new_m[n] for n in WEIGHTS], *[new_v[n] for n in WEIGHTS])
```

```python
import functools

import jax
import jax.numpy as jnp
import numpy as np
from jax import lax
from jax.experimental import pallas as pl
from jax.experimental.pallas import tpu as pltpu

F32 = jnp.float32
BF16 = jnp.bfloat16

D = 1024
H = 8
DH = 128
NB = 8
DPLE = 256
DMIX = 2 * D
D_IN = 4 * D + H + 2 * D
FL0 = 3 * D
RMS_EPS = 1e-6
LRU_C = 8.0
NEG = -1e30
LANES = 128
SUBLANES = 8

ADAM_LR = 0.001
ADAM_B1 = 0.9
ADAM_B2 = 0.999
ADAM_EPS = 1e-08
ADAM_WD = 0.01
ADAM_STEP = 10

TM = 256
TA = 512
FWD_HEADS = 4
BWD_HEADS = 2
VMEM_BIG = 56 * 1024 * 1024
VMEM_MID = 40 * 1024 * 1024

MESH = pl.DeviceIdType.MESH
N_CHIPS = 4
N_DEV = 8


def _call(body, *, out_shape, in_hbm=True, **kwargs):
    if not in_hbm:
        return pl.pallas_call(body, out_shape=out_shape, **kwargs)

    def pin(shape):
        return pltpu.HBM(shape.shape, shape.dtype) if isinstance(shape, jax.ShapeDtypeStruct) else shape

    fn = pl.pallas_call(body, out_shape=jax.tree.map(pin, out_shape), **kwargs)

    def run(*args):
        return fn(*[a if a.dtype == jnp.int32 else pltpu.with_memory_space_constraint(a, pltpu.HBM) for a in args])

    return run


def _cparams(sem, vmem=VMEM_MID):
    return pltpu.CompilerParams(dimension_semantics=sem, vmem_limit_bytes=vmem)


def _sigmoid(x):
    return 0.5 * jnp.tanh(0.5 * x) + 0.5


def _rstd(x):
    return lax.rsqrt(jnp.mean(x * x, axis=-1, keepdims=True) + RMS_EPS)


def _rms_bwd(t, xhat, rstd):
    return rstd * (t - xhat * jnp.mean(t * xhat, axis=-1, keepdims=True))


def _dot(a, b):
    return jnp.dot(a, b, preferred_element_type=F32)


def _dot_nt(a, b):
    return lax.dot_general(a, b, (((1,), (1,)), ((), ())), preferred_element_type=F32)


def _dot_tn(a, b):
    return lax.dot_general(a, b, (((0,), (0,)), ((), ())), preferred_element_type=F32)


def _dot_exact(a, b):
    return jnp.dot(a, b, preferred_element_type=F32, precision=lax.Precision.HIGHEST)


def _shift_down(x, j, halo):
    rolled = pltpu.roll(x, j, 0)
    row = lax.broadcasted_iota(jnp.int32, halo.shape, 0)
    top = jnp.where(row < j, pltpu.roll(halo, j, 0), rolled[:SUBLANES])
    return jnp.concatenate([top, rolled[SUBLANES:]], axis=0)


def _shift_up(x, j, nxt):
    tm = x.shape[0]
    rolled = pltpu.roll(x, tm - j, 0)
    row = lax.broadcasted_iota(jnp.int32, nxt.shape, 0)
    bot = jnp.where(row >= SUBLANES - j, pltpu.roll(nxt, SUBLANES - j, 0), rolled[tm - SUBLANES:])
    return jnp.concatenate([rolled[:tm - SUBLANES], bot], axis=0)


def _scan_fwd_into(a, u, carry, h_ref):
    tm = a.shape[0]
    sub = lax.broadcasted_iota(jnp.int32, a.shape, 0) & (SUBLANES - 1)
    d = 1
    while d < SUBLANES:
        keep = sub >= d
        a_s = jnp.where(keep, pltpu.roll(a, d, 0), 1.0)
        u_s = jnp.where(keep, pltpu.roll(u, d, 0), 0.0)
        u = u + a * u_s
        a = a * a_s
        d *= 2
    for g in range(tm // SUBLANES):
        rows = slice(g * SUBLANES, (g + 1) * SUBLANES)
        h_ref[rows, :] = u[rows] + a[rows] * carry
        carry = h_ref[(g + 1) * SUBLANES - 1:(g + 1) * SUBLANES, :]
    return carry


def _scan_bwd_into(b, u, g_ref):
    tm = b.shape[0]
    sub = lax.broadcasted_iota(jnp.int32, b.shape, 0) & (SUBLANES - 1)
    d = 1
    while d < SUBLANES:
        keep = sub < SUBLANES - d
        b_s = jnp.where(keep, pltpu.roll(b, tm - d, 0), 1.0)
        u_s = jnp.where(keep, pltpu.roll(u, tm - d, 0), 0.0)
        u = u + b * u_s
        b = b * b_s
        d *= 2
    nxt = jnp.zeros((1, b.shape[1]), F32)
    for g in reversed(range(tm // SUBLANES)):
        rows = slice(g * SUBLANES, (g + 1) * SUBLANES)
        g_ref[rows, :] = u[rows] + b[rows] * nxt
        nxt = g_ref[g * SUBLANES:g * SUBLANES + 1, :]


def _gate_pre(xc, w_ref):
    outs = []
    for n in range(NB):
        outs.append(_dot(xc[:, n * LANES:(n + 1) * LANES].astype(BF16), w_ref[n]))
    return jnp.concatenate(outs, axis=1)


def _gate_pre_t(d, w_ref):
    outs = []
    for n in range(NB):
        outs.append(_dot_nt(d[:, n * LANES:(n + 1) * LANES].astype(BF16), w_ref[n]))
    return jnp.concatenate(outs, axis=1)


def _softplus_neg(lam):
    return jnp.maximum(-lam, 0.0) + jnp.log(1.0 + jnp.exp(-jnp.abs(lam)))


def _row_spec(tm, width):
    return pl.BlockSpec((tm, width), lambda i: (i, 0))


def _const_spec(shape):
    nd = len(shape)
    return pl.BlockSpec(shape, lambda *_: (0,) * nd)


AUG = 2 * DH
LOG2E = 1.4426950408889634
LN2 = 0.6931471805599453
Q_SCALE = DH ** -0.5 * LOG2E


def _split3(x):
    hi = x.astype(BF16)
    r1 = x - hi.astype(F32)
    mid = r1.astype(BF16)
    lo = (r1 - mid.astype(F32)).astype(BF16)
    return hi, mid, lo


def _extras(col, ones_from):
    t = col.shape[0]
    hi, mid, lo = _split3(jnp.broadcast_to(col, (t, LANES)))
    lane = lax.broadcasted_iota(jnp.int32, (t, LANES), 1)
    rest = jnp.zeros((t, LANES), BF16)
    if ones_from is not None:
        rest = jnp.where((lane >= ones_from) & (lane < ones_from + 3), 1.0, 0.0).astype(BF16)
    return jnp.where(lane == 0, hi, jnp.where(lane == 1, mid, jnp.where(lane == 2, lo, rest)))


def _selectors():
    sel_q = np.zeros((3 * LANES, H * LANES), np.float32)
    sel_k = np.zeros((3 * LANES, H * LANES), np.float32)
    for hd in range(H):
        for piece in range(3):
            sel_q[piece * LANES + hd, hd * LANES + piece] = 1.0
            sel_k[piece * LANES + hd, hd * LANES + 3 + piece] = -1.0
    return jnp.asarray(sel_q, BF16), jnp.asarray(sel_k, BF16)


def _in_proj(x, pre_gain, w_a, w_f, w_b, b_f_pad):
    T = x.shape[0]
    tm = TM
    sel_q, sel_k = _selectors()

    def body(x_ref, g_ref, wa_ref, wf_ref, wb_ref, bf_ref, sq_ref, sk_ref,
             xn_ref, qa_ref, ka_ref, va_ref, ga_ref, xl_ref, gl_ref, flb_ref, vt_ref, c_s, carry):
        @pl.when(pl.program_id(0) == 0)
        def _():
            carry[...] = jnp.zeros_like(carry)

        xv = x_ref[...]
        xn = (xv * _rstd(xv) * g_ref[...]).astype(BF16)
        xn_ref[...] = xn
        for s, o_ref in enumerate((ga_ref, xl_ref, gl_ref)):
            o_ref[...] = _dot_nt(xn, wb_ref[s * D:(s + 1) * D, :]).astype(o_ref.dtype)
        flb = _dot_nt(xn, wf_ref[...]) + bf_ref[...]
        flb_ref[...] = flb
        lane = lax.broadcasted_iota(jnp.int32, flb.shape, 1)
        ls = jnp.where(lane < H, jnp.minimum(flb, 0.0) - jnp.log(1.0 + jnp.exp(-jnp.abs(flb))), 0.0)
        r = lax.broadcasted_iota(jnp.int32, (tm, tm), 0)
        c = lax.broadcasted_iota(jnp.int32, (tm, tm), 1)
        cs = _dot_exact((c <= r).astype(F32), ls) + carry[...]
        c_s[...] = cs
        carry[...] = c_s[tm - 1:tm, :]

        pieces = jnp.concatenate(_split3(cs * LOG2E), axis=1)
        ones_q = jnp.where((lane >= 3) & (lane < 6), 1.0, 0.0)
        ones_k = jnp.where(lane < 3, 1.0, 0.0)
        zq = _dot_nt(xn, wa_ref[0:D, :]) * Q_SCALE
        zk = _dot_nt(xn, wa_ref[D:2 * D, :])
        zv = _dot_nt(xn, wa_ref[2 * D:3 * D, :])
        ex_q = _dot(pieces, sq_ref[...])
        ex_k = _dot(pieces, sk_ref[...])
        for hd in range(H):
            head = slice(hd * DH, (hd + 1) * DH)
            lo, hi = hd * AUG, hd * AUG + DH
            qa_ref[:, lo:hi] = zq[:, head].astype(BF16)
            qa_ref[:, hi:hi + DH] = (ex_q[:, head] + ones_q).astype(BF16)
            ka_ref[:, lo:hi] = zk[:, head].astype(BF16)
            ka_ref[:, hi:hi + DH] = (ex_k[:, head] + ones_k).astype(BF16)
            va_ref[:, lo:hi] = zv[:, head].astype(BF16)
            va_ref[:, hi:hi + DH] = ones_k.astype(BF16)
            vt_ref[lo:hi, :] = jnp.transpose(zv[:, head]).astype(BF16)
            vt_ref[hi:hi + DH, :] = jnp.where(lax.broadcasted_iota(jnp.int32, (DH, tm), 0) < 3, 1.0, 0.0).astype(BF16)

    bf = jax.ShapeDtypeStruct((T, D), BF16)
    aug = jax.ShapeDtypeStruct((T, H * AUG), BF16)
    f32 = jax.ShapeDtypeStruct((T, D), F32)
    sel_spec = _const_spec((3 * LANES, H * LANES))
    return _call(
        body, name="in_proj", grid=(T // tm,),
        in_specs=[_row_spec(tm, D), _const_spec((1, D)), _const_spec((3 * D, D)), _const_spec((LANES, D)),
                  _const_spec((3 * D, D)), _const_spec((1, LANES)), sel_spec, sel_spec],
        out_specs=[_row_spec(tm, D)] + [_row_spec(tm, H * AUG)] * 3 + [_row_spec(tm, D)] * 3 + [_row_spec(tm, LANES)]
        + [pl.BlockSpec((H * AUG, tm), lambda i: (0, i))],
        out_shape=[bf, aug, aug, aug, f32, f32, f32, jax.ShapeDtypeStruct((T, LANES), F32),
                   jax.ShapeDtypeStruct((H * AUG, T), BF16)],
        scratch_shapes=[pltpu.VMEM((tm, LANES), F32), pltpu.VMEM((1, LANES), F32)],
        compiler_params=_cparams(("arbitrary",), VMEM_BIG),
    )(x, pre_gain, w_a, w_f, w_b, b_f_pad, sel_q, sel_k)


def _causal_pairs(n, q_major):
    if q_major:
        pairs = [(qi, ki) for qi in range(n) for ki in range(qi + 1)]
    else:
        pairs = [(ki, qi) for ki in range(n) for qi in range(ki, n)]
    return (jnp.asarray([a for a, _ in pairs], jnp.int32), jnp.asarray([b for _, b in pairs], jnp.int32))


def _attn_fwd(q_aug, k_aug, vt_aug, shards=(), whole=()):
    T = q_aug.shape[0]
    t = TA
    n = T // t
    hp = FWD_HEADS
    heads = range(hp)
    qi_tab, ki_tab = _causal_pairs(n, q_major=True)
    na, nall = len(shards), len(shards) + len(whole)
    n_h, n_j = H // hp, qi_tab.shape[0]

    def body(qi_ref, ki_ref, q_ref, k_ref, vt_ref, *rest):
        srcs, rest = rest[:nall], rest[nall:]
        o_ref, qx_ref = rest[:2]
        dsts, rest = rest[2:2 + nall], rest[2 + nall:]
        m_s, acc_s = rest[:2]
        h = pl.program_id(0)
        j = pl.program_id(1)
        qi = qi_ref[j]
        ki = ki_ref[j]

        if nall:
            gather = _GatherPlan(srcs, dsts, rest[2:], na)
            pl.when((h == 0) & (j == 0))(gather.send)
            pl.when((h == n_h - 1) & (j == 0))(gather.forward)
            pl.when((h == n_h - 1) & (j == n_j - 1))(gather.finish)

        @pl.when(ki == 0)
        def _():
            m_s[...] = jnp.full(m_s.shape, NEG, F32)
            acc_s[...] = jnp.zeros_like(acc_s)

        def step(on_diagonal):
            cols = [slice(a * AUG, (a + 1) * AUG) for a in heads]
            if on_diagonal:
                krow = lax.broadcasted_iota(jnp.int32, (t, t), 0)
                qcol = lax.broadcasted_iota(jnp.int32, (t, t), 1)
            for a in heads:
                st = _dot_nt(k_ref[:, cols[a]], q_ref[:, cols[a]])
                if on_diagonal:
                    st = jnp.where(krow <= qcol, st, NEG)
                m_prev = m_s[a]
                m_new = jnp.maximum(m_prev, jnp.max(st, axis=0, keepdims=True))
                pt = jnp.exp2(st - m_new).astype(BF16)
                acc_s[a] = jnp.exp2(m_prev - m_new) * acc_s[a] + _dot(vt_ref[cols[a], :], pt)
                m_s[a] = m_new

        @pl.when(ki < qi)
        def _():
            step(False)

        @pl.when(ki == qi)
        def _():
            step(True)
            piece = lax.broadcasted_iota(jnp.int32, (DH, t), 0)
            for a in heads:
                l = acc_s[a, DH:DH + 1, :]
                ex = jnp.transpose(q_ref[:, a * AUG + DH:(a + 1) * AUG].astype(F32))
                c2 = jnp.sum(jnp.where(piece < 3, ex, 0.0), axis=0, keepdims=True)
                hi, mid, lo = _split3(jnp.broadcast_to(c2 - (m_s[a] + jnp.log(l) * LOG2E), (DH, t)))
                ones = jnp.where((piece >= 3) & (piece < 6), 1.0, 0.0).astype(BF16)
                ex_t = jnp.where(piece == 0, hi, jnp.where(piece == 1, mid, jnp.where(piece == 2, lo, ones)))
                o_ref[:, a * DH:(a + 1) * DH] = jnp.transpose(acc_s[a, :DH, :] / l)
                qx_ref[:, a * DH:(a + 1) * DH] = jnp.transpose(ex_t.astype(F32)).astype(BF16)

    q_spec = pl.BlockSpec((t, hp * AUG), lambda h, j, qi_ref, ki_ref: (qi_ref[j], h))
    k_spec = pl.BlockSpec((t, hp * AUG), lambda h, j, qi_ref, ki_ref: (ki_ref[j], h))
    vt_spec = pl.BlockSpec((hp * AUG, t), lambda h, j, qi_ref, ki_ref: (h, ki_ref[j]))
    out_spec = pl.BlockSpec((t, hp * DH), lambda h, j, qi_ref, ki_ref: (qi_ref[j], h))
    arrs = list(shards) + list(whole)
    grid_spec = pltpu.PrefetchScalarGridSpec(
        num_scalar_prefetch=2, grid=(n_h, n_j),
        in_specs=[q_spec, k_spec, vt_spec] + [HBM_SPEC] * nall, out_specs=[out_spec, out_spec] + [HBM_SPEC] * nall,
        scratch_shapes=[pltpu.VMEM((hp, 1, t), F32), pltpu.VMEM((hp, AUG, t), F32)]
        + (_gather_semaphores(na, nall) if nall else []))
    outs = _call(
        body, name="attn_fwd", grid_spec=grid_spec,
        out_shape=[jax.ShapeDtypeStruct((T, D), F32), jax.ShapeDtypeStruct((T, D), BF16)] + _gather_out_shapes(arrs),
        compiler_params=_cparams(("arbitrary", "arbitrary"), VMEM_BIG),
    )(qi_tab, ki_tab, q_aug, k_aug, vt_aug, *arrs)
    return outs[0], outs[1], _place_own(outs[2:], arrs)


def _lru_gates(xc, wr_ref, br_ref, wi_ref, bi_ref, lam_ref):
    r = _sigmoid(_gate_pre(xc, wr_ref) + br_ref[...])
    ig = _sigmoid(_gate_pre(xc, wi_ref) + bi_ref[...])
    sp = _softplus_neg(lam_ref[...])
    la = (-LRU_C) * r * sp
    a = jnp.exp(la)
    y = -jnp.tanh(la) * (a * a + 1.0)
    return r, ig, sp, a, jnp.sqrt(y), lax.rsqrt(y)


def _branches_fwd(o, g_attn, x_lru, g_lru, gain_a, gain_l, conv_w, conv_b, w_r, b_r, w_i, b_i, lam):
    T = o.shape[0]
    tm = TM

    def body(o_ref, ga_ref, xl_ref, gl_ref, gna_ref, gnl_ref, cw_ref, cb_ref, wr_ref, br_ref, wi_ref, bi_ref,
             lam_ref, ycat_ref, xc_ref, h_ref, halo_s, hc_s):
        @pl.when(pl.program_id(0) == 0)
        def _():
            halo_s[...] = jnp.zeros_like(halo_s)
            hc_s[...] = jnp.zeros_like(hc_s)

        ov = o_ref[...]
        ga = ga_ref[...]
        ya = ov * _rstd(ov) * gna_ref[...] * (ga * _sigmoid(ga))
        ycat_ref[:, :D] = ya.astype(BF16)

        xl = xl_ref[...]
        halo = halo_s[...]
        xc = xl * cw_ref[3:4, :] + cb_ref[...]
        for j in range(3):
            xc = xc + _shift_down(xl, 3 - j, halo) * cw_ref[j:j + 1, :]
        halo_s[...] = xl_ref[tm - SUBLANES:tm, :]
        xc_ref[...] = xc

        _, ig, _, a, sq, _ = _lru_gates(xc, wr_ref, br_ref, wi_ref, bi_ref, lam_ref)
        u = sq * (ig * xc)
        hc_s[...] = _scan_fwd_into(a, u, hc_s[...], h_ref)
        hh = h_ref[...]

        gl = gl_ref[...]
        yl = hh * _rstd(hh) * gnl_ref[...] * (gl * _sigmoid(gl))
        ycat_ref[:, D:] = yl.astype(BF16)

    vec = _const_spec((1, D))
    wspec = _const_spec((NB, LANES, LANES))
    return _call(
        body, name="branches_fwd", grid=(T // tm,),
        in_specs=[_row_spec(tm, D)] * 4 + [vec, vec, _const_spec((4, D)), vec, wspec, vec, wspec, vec, vec],
        out_specs=[_row_spec(tm, DMIX), _row_spec(tm, D), _row_spec(tm, D)],
        out_shape=[jax.ShapeDtypeStruct((T, DMIX), BF16), jax.ShapeDtypeStruct((T, D), F32),
                   jax.ShapeDtypeStruct((T, D), F32)],
        scratch_shapes=[pltpu.VMEM((SUBLANES, D), F32), pltpu.VMEM((1, D), F32)],
        compiler_params=_cparams(("arbitrary",)),
    )(o, g_attn, x_lru, g_lru, gain_a, gain_l, conv_w, conv_b, w_r, b_r, w_i, b_i, lam)


def _tail(ycat, x, p, tgt, w_out, post_gain, w_ple, ple_gain, w_gate, b_gate):
    T = x.shape[0]
    tm = TM

    def body(ycat_ref, x_ref, p_ref, t_ref, wo_ref, pg_ref, wp_ref, eg_ref, wg_ref, bg_ref,
             dh1_ref, dycat_ref, dmix_ref, h1b_ref, dgp_ref, pb_ref, dpe_ref, acc_ref):
        @pl.when(pl.program_id(0) == 0)
        def _():
            acc_ref[...] = jnp.zeros_like(acc_ref)

        mix = _dot(ycat_ref[...], wo_ref[...])
        rstd_m = _rstd(mix)
        mhat = mix * rstd_m
        h1 = x_ref[...] + mhat * pg_ref[...]
        pb = p_ref[...].astype(BF16)
        pb_ref[...] = pb
        pe = _dot(pb, wp_ref[...])
        rstd_p = _rstd(pe)
        pehat = pe * rstd_p
        e = pehat * eg_ref[...]
        h1b = h1.astype(BF16)
        h1b_ref[...] = h1b
        gate = _sigmoid(_dot(h1b, wg_ref[...]) + bg_ref[...])
        diff = (h1 + gate * e) - t_ref[...]

        dy = diff * (1.0 / D)
        de = dy * gate
        dgp = (dy * e) * gate * (1.0 - gate)
        dgpb = dgp.astype(BF16)
        dgp_ref[...] = dgpb
        dh1 = dy + _dot_nt(dgpb, wg_ref[...])
        dh1_ref[...] = dh1
        dpe_ref[...] = _rms_bwd(de * eg_ref[...], pehat, rstd_p).astype(BF16)
        dmix = _rms_bwd(dh1 * pg_ref[...], mhat, rstd_m).astype(BF16)
        dmix_ref[...] = dmix
        dycat_ref[...] = _dot_nt(dmix, wo_ref[...])

        acc_ref[0:1, :] += jnp.sum(dh1 * mhat, axis=0, keepdims=True)
        acc_ref[1:2, :] += jnp.sum(de * pehat, axis=0, keepdims=True)
        acc_ref[2:3, :] += jnp.sum(dgp, axis=0, keepdims=True)
        acc_ref[3:4, :] += jnp.sum(diff * diff, axis=0, keepdims=True) * (0.5 / D)

    vec = _const_spec((1, D))
    bf = jax.ShapeDtypeStruct((T, D), BF16)
    return _call(
        body, name="tail", grid=(T // tm,),
        in_specs=[_row_spec(tm, DMIX), _row_spec(tm, D), _row_spec(tm, DPLE), _row_spec(tm, D),
                  _const_spec((DMIX, D)), vec, _const_spec((DPLE, D)), vec, _const_spec((D, D)), vec],
        out_specs=[_row_spec(tm, D), _row_spec(tm, DMIX), _row_spec(tm, D), _row_spec(tm, D), _row_spec(tm, D),
                   _row_spec(tm, DPLE), _row_spec(tm, D), _const_spec((SUBLANES, D))],
        out_shape=[jax.ShapeDtypeStruct((T, D), F32), jax.ShapeDtypeStruct((T, DMIX), F32), bf, bf, bf,
                   jax.ShapeDtypeStruct((T, DPLE), BF16), bf, jax.ShapeDtypeStruct((SUBLANES, D), F32)],
        compiler_params=_cparams(("arbitrary",), VMEM_BIG),
    )(ycat, x, p, tgt, w_out, post_gain, w_ple, ple_gain, w_gate, b_gate)


def _branches_bwd(dycat, o, g_attn, h, g_lru, gain_a, gain_l):
    T = o.shape[0]
    tm = TM

    def body(dy_ref, o_ref, ga_ref, h_ref, gl_ref, gna_ref, gnl_ref,
             do_ref, dga_ref, dgl_ref, dh_ref, acc_ref):
        @pl.when(pl.program_id(0) == 0)
        def _():
            acc_ref[...] = jnp.zeros_like(acc_ref)

        def branch(val, g, gain, dyv):
            rstd = _rstd(val)
            vhat = val * rstd
            sig = _sigmoid(g)
            dn = dyv * (g * sig)
            dg = dyv * (vhat * gain) * (sig * (1.0 + g * (1.0 - sig)))
            dgain = jnp.sum(dn * vhat, axis=0, keepdims=True)
            return _rms_bwd(dn * gain, vhat, rstd), dg, dgain

        ov = o_ref[...]
        do, dga, dgain_a = branch(ov, ga_ref[...], gna_ref[...], dy_ref[:, :D])
        dga_ref[...] = dga.astype(BF16)
        prod = do * ov
        for hd in range(H):
            head = slice(hd * DH, (hd + 1) * DH)
            do_ref[:, hd * AUG:hd * AUG + DH] = do[:, head].astype(BF16)
            do_ref[:, hd * AUG + DH:(hd + 1) * AUG] = _extras(-jnp.sum(prod[:, head], axis=1, keepdims=True), None)

        dh, dgl, dgain_l = branch(h_ref[...], gl_ref[...], gnl_ref[...], dy_ref[:, D:])
        dh_ref[...] = dh
        dgl_ref[...] = dgl.astype(BF16)
        acc_ref[0:1, :] += dgain_a
        acc_ref[1:2, :] += dgain_l

    vec = _const_spec((1, D))
    bf = jax.ShapeDtypeStruct((T, D), BF16)
    return _call(
        body, name="branches_bwd", grid=(T // tm,),
        in_specs=[_row_spec(tm, DMIX)] + [_row_spec(tm, D)] * 4 + [vec, vec],
        out_specs=[_row_spec(tm, H * AUG), _row_spec(tm, D), _row_spec(tm, D), _row_spec(tm, D),
                   _const_spec((SUBLANES, D))],
        out_shape=[jax.ShapeDtypeStruct((T, H * AUG), BF16), bf, bf, jax.ShapeDtypeStruct((T, D), F32),
                   jax.ShapeDtypeStruct((SUBLANES, D), F32)],
        compiler_params=_cparams(("arbitrary",)),
    )(dycat, o, g_attn, h, g_lru, gain_a, gain_l)


def _lru_bwd(dh, h, xc, x_lru, conv_w, w_r, b_r, w_i, b_i, lam):
    T = dh.shape[0]
    tm = TM
    nt = T // tm
    per = tm // SUBLANES

    def body(dh_ref, h_ref, hprev_ref, xc_ref, xl_ref, xlprev_ref, cw_ref, wr_ref, br_ref, wi_ref, bi_ref, lam_ref,
             dxl_ref, dwr_ref, dwi_ref, acc_ref, carry_s, dxc_next_s, top_s, dht_s):
        i = pl.program_id(0)

        @pl.when(i == 0)
        def _():
            acc_ref[...] = jnp.zeros_like(acc_ref)
            dwr_ref[...] = jnp.zeros_like(dwr_ref)
            dwi_ref[...] = jnp.zeros_like(dwi_ref)
            carry_s[...] = jnp.zeros_like(carry_s)
            dxc_next_s[...] = jnp.zeros_like(dxc_next_s)

        inner = jnp.where(i == nt - 1, 0.0, 1.0)
        xc = xc_ref[...]
        r, ig, sp, a, sq, inv_sq = _lru_gates(xc, wr_ref, br_ref, wi_ref, bi_ref, lam_ref)

        row = lax.broadcasted_iota(jnp.int32, (tm, D), 0)
        u = dh_ref[...] + jnp.where(row == tm - 1, carry_s[...], 0.0)
        _scan_bwd_into(pltpu.roll(a, tm - 1, 0), u, dht_s)
        dht = dht_s[...]
        top_s[...] = a[:SUBLANES, :] * dht[:SUBLANES, :]
        carry_s[...] = top_s[0:1, :]

        hprev = hprev_ref[...] * inner
        da = dht * _shift_down(h_ref[...], 1, hprev)
        dig = dht * sq * xc
        dxc = dht * sq * ig
        dsq = dht * ig * xc
        dla = da * a - dsq * (a * a) * inv_sq
        dr = dla * ((-LRU_C) * sp)
        dpr = dr * r * (1.0 - r)
        dpi = dig * ig * (1.0 - ig)
        for n in range(NB):
            blk = slice(n * LANES, (n + 1) * LANES)
            xcb = xc[:, blk].astype(BF16)
            dwr_ref[n] += _dot_tn(xcb, dpr[:, blk].astype(BF16))
            dwi_ref[n] += _dot_tn(xcb, dpi[:, blk].astype(BF16))
        dxc = dxc + _gate_pre_t(dpr, wr_ref) + _gate_pre_t(dpi, wi_ref)

        xl = xl_ref[...]
        xlprev = xlprev_ref[...] * inner
        nxt = dxc_next_s[...]
        dxl = dxc * cw_ref[3:4, :]
        acc_ref[3:4, :] += jnp.sum(dxc * xl, axis=0, keepdims=True)
        for j in range(3):
            dxl = dxl + _shift_up(dxc, 3 - j, nxt) * cw_ref[j:j + 1, :]
            acc_ref[j:j + 1, :] += jnp.sum(dxc * _shift_down(xl, 3 - j, xlprev), axis=0, keepdims=True)
        dxc_next_s[...] = dxc[:SUBLANES, :]
        dxl_ref[...] = dxl.astype(BF16)

        acc_ref[4:5, :] += jnp.sum(dxc, axis=0, keepdims=True)
        acc_ref[5:6, :] += jnp.sum(dpr, axis=0, keepdims=True)
        acc_ref[6:7, :] += jnp.sum(dpi, axis=0, keepdims=True)
        acc_ref[7:8, :] += jnp.sum(dla * ((-LRU_C) * r), axis=0, keepdims=True)

        @pl.when(i == nt - 1)
        def _():
            lam_v = lam_ref[...]
            acc_ref[7:8, :] = acc_ref[7:8, :] * (-_sigmoid(-lam_v))

    rev = pl.BlockSpec((tm, D), lambda i: (nt - 1 - i, 0))
    prev8 = pl.BlockSpec((SUBLANES, D), lambda i: (jnp.maximum((nt - 1 - i) * per - 1, 0), 0))
    vec = _const_spec((1, D))
    wspec = _const_spec((NB, LANES, LANES))
    bf = jax.ShapeDtypeStruct((T, D), BF16)
    return _call(
        body, name="lru_bwd", grid=(nt,),
        in_specs=[rev, rev, prev8, rev, rev, prev8, _const_spec((4, D)), wspec, vec, wspec, vec, vec],
        out_specs=[rev, wspec, wspec, _const_spec((SUBLANES, D))],
        out_shape=[bf, jax.ShapeDtypeStruct((NB, LANES, LANES), F32), jax.ShapeDtypeStruct((NB, LANES, LANES), F32),
                   jax.ShapeDtypeStruct((SUBLANES, D), F32)],
        scratch_shapes=[pltpu.VMEM((1, D), F32), pltpu.VMEM((SUBLANES, D), F32), pltpu.VMEM((SUBLANES, D), F32),
                        pltpu.VMEM((tm, D), F32)],
        compiler_params=_cparams(("arbitrary",)),
    )(dh, h, h, xc, x_lru, x_lru, conv_w, w_r, b_r, w_i, b_i, lam)


def _chip_copies(srcs, dsts, send_sems, recv_sems):
    x, y, c = _position()
    chip = 2 * x + y
    na = len(srcs)
    return [pltpu.make_async_remote_copy(
        src_ref=srcs[a].at[2 * px + py], dst_ref=dsts[a].at[chip], send_sem=send_sems.at[j * na + a],
        recv_sem=recv_sems.at[j * na + a], device_id=(px, py, c), device_id_type=MESH)
        for j, (px, py) in enumerate(_other_chips(x, y)) for a in range(na)]


def _attn_bwd(q_aug, qx, k_aug, v_aug, do_aug, exchange=()):
    T = q_aug.shape[0]
    t = TA
    n = T // t
    hp = BWD_HEADS
    heads = range(hp)
    scale = DH ** -0.5
    ki_tab, qi_tab = _causal_pairs(n, q_major=False)
    last = ki_tab.shape[0] - 1
    ne = len(exchange)
    n_h = H // hp

    def body(ki_ref, qi_ref, q_ref, qx_ref, k_ref, v_ref, do_ref, *rest):
        sent, rest = rest[:ne], rest[ne:]
        dq_ref, dk_ref, dv_ref, dck_ref, dcq_ref = rest[:5]
        received, rest = rest[5:5 + ne], rest[5 + ne:]
        dq_s, dk_s, dv_s = rest[:3]
        j = pl.program_id(1)
        ki = ki_ref[j]
        qi = qi_ref[j]

        if ne:
            first_step = (pl.program_id(0) == 0) & (j == 0)
            last_step = (pl.program_id(0) == n_h - 1) & (j == last)

            @pl.when(first_step)
            def _():
                for cp in _chip_copies(sent, received, *rest[3:]):
                    cp.start()

            @pl.when(last_step)
            def _():
                for cp in _chip_copies(sent, received, *rest[3:]):
                    cp.wait()

        @pl.when(j == 0)
        def _():
            dq_s[...] = jnp.zeros_like(dq_s)

        @pl.when(qi == ki)
        def _():
            dk_s[...] = jnp.zeros_like(dk_s)
            dv_s[...] = jnp.zeros_like(dv_s)

        def step(on_diagonal):
            cols = [slice(a * AUG, (a + 1) * AUG) for a in heads]
            qb = [jnp.concatenate([q_ref[:, a * AUG:a * AUG + DH], qx_ref[:, a * DH:(a + 1) * DH]], axis=1)
                  for a in heads]
            st = [_dot_nt(k_ref[:, cols[a]], qb[a]) for a in heads]
            dpd = [_dot_nt(v_ref[:, cols[a]], do_ref[:, cols[a]]) for a in heads]
            if on_diagonal:
                krow = lax.broadcasted_iota(jnp.int32, (t, t), 0)
                qcol = lax.broadcasted_iota(jnp.int32, (t, t), 1)
                st = [jnp.where(krow <= qcol, st[a], NEG) for a in heads]
            pt = [jnp.exp2(st[a]) for a in heads]
            dsb = [(pt[a] * dpd[a]).astype(BF16) for a in heads]
            ptb = [pt[a].astype(BF16) for a in heads]
            off = pl.multiple_of(qi * t, t)
            for a in heads:
                dv_s[a] += _dot(ptb[a], do_ref[:, a * AUG:a * AUG + DH])
                dk_s[a] += _dot(dsb[a], qb[a])
            for a in heads:
                dq_s[a, pl.ds(off, t), :] += _dot_tn(dsb[a], k_ref[:, cols[a]])

        @pl.when(qi > ki)
        def _():
            step(False)

        @pl.when(qi == ki)
        def _():
            step(True)

        @pl.when(qi == n - 1)
        def _():
            for a in heads:
                dk_ref[:, a * DH:(a + 1) * DH] = (dk_s[a, :, :DH] * LN2).astype(BF16)
                dv_ref[:, a * DH:(a + 1) * DH] = dv_s[a].astype(BF16)
                dck_ref[a] = jnp.broadcast_to(dk_s[a, :, DH + 3:DH + 4], (t, LANES))

        @pl.when(j == last)
        def _():
            for a in heads:
                dq_ref[:, a * DH:(a + 1) * DH] = (dq_s[a, :, :DH] * scale).astype(BF16)
                dcq_ref[a] = jnp.broadcast_to(dq_s[a, :, DH:DH + 1], (T, LANES))

    qside = pl.BlockSpec((t, hp * AUG), lambda h, j, ki_ref, qi_ref: (qi_ref[j], h))
    qxside = pl.BlockSpec((t, hp * DH), lambda h, j, ki_ref, qi_ref: (qi_ref[j], h))
    kside = pl.BlockSpec((t, hp * AUG), lambda h, j, ki_ref, qi_ref: (ki_ref[j], h))
    kout = pl.BlockSpec((t, hp * DH), lambda h, j, ki_ref, qi_ref: (ki_ref[j], h))
    bf = jax.ShapeDtypeStruct((T, D), BF16)
    sums = jax.ShapeDtypeStruct((H, T, LANES), F32)
    grid_spec = pltpu.PrefetchScalarGridSpec(
        num_scalar_prefetch=2, grid=(n_h, ki_tab.shape[0]),
        in_specs=[qside, qxside, kside, kside, qside] + [HBM_SPEC] * ne,
        out_specs=[pl.BlockSpec((T, hp * DH), lambda h, j, ki_ref, qi_ref: (0, h)), kout, kout,
                   pl.BlockSpec((hp, t, LANES), lambda h, j, ki_ref, qi_ref: (h, ki_ref[j], 0)),
                   pl.BlockSpec((hp, T, LANES), lambda h, j, ki_ref, qi_ref: (h, 0, 0))] + [HBM_SPEC] * ne,
        scratch_shapes=[pltpu.VMEM((hp, T, AUG), F32), pltpu.VMEM((hp, t, AUG), F32), pltpu.VMEM((hp, t, DH), F32)]
        + ([pltpu.SemaphoreType.DMA((3 * ne,)), pltpu.SemaphoreType.DMA((3 * ne,))] if ne else []))
    outs = _call(
        body, name="attn_bwd", grid_spec=grid_spec,
        out_shape=[bf, bf, bf, sums, sums] + [jax.ShapeDtypeStruct(s.shape, s.dtype) for s in exchange],
        compiler_params=_cparams(("arbitrary", "arbitrary"), VMEM_BIG),
    )(ki_tab, qi_tab, q_aug, qx, k_aug, v_aug, do_aug, *exchange)
    return (*outs[:5], list(outs[5:]))


def _fgate_bwd(dc_key, dc_query, flb):
    T = flb.shape[0]
    tm = TM
    nt = T // tm

    def body(dck_ref, dcq_ref, flb_ref, dfl_ref, acc_ref, carry, top_s):
        @pl.when(pl.program_id(0) == 0)
        def _():
            carry[...] = jnp.zeros_like(carry)
            acc_ref[...] = jnp.zeros_like(acc_ref)

        flb = flb_ref[...]
        lane = lax.broadcasted_iota(jnp.int32, flb.shape, 1)
        dc = jnp.zeros(flb.shape, F32)
        for hd in range(H):
            dc = dc + jnp.where(lane == hd, dcq_ref[hd] - dck_ref[hd], 0.0)
        r = lax.broadcasted_iota(jnp.int32, (tm, tm), 0)
        c = lax.broadcasted_iota(jnp.int32, (tm, tm), 1)
        dls = _dot_exact((c >= r).astype(F32), dc) + carry[...]
        top_s[...] = dls[:SUBLANES, :]
        carry[...] = top_s[0:1, :]
        dfl = jnp.where(lane < H, dls * _sigmoid(-flb), 0.0)
        dfl_ref[...] = dfl.astype(BF16)
        acc_ref[0:1, :] += jnp.sum(dfl, axis=0, keepdims=True)

    rev = pl.BlockSpec((tm, LANES), lambda i: (nt - 1 - i, 0))
    return _call(
        body, name="fgate_bwd", grid=(nt,),
        in_specs=[pl.BlockSpec((H, tm, LANES), lambda i: (0, nt - 1 - i, 0))] * 2 + [rev],
        out_specs=[rev, _const_spec((SUBLANES, LANES))],
        out_shape=[jax.ShapeDtypeStruct((T, LANES), BF16), jax.ShapeDtypeStruct((SUBLANES, LANES), F32)],
        scratch_shapes=[pltpu.VMEM((1, LANES), F32), pltpu.VMEM((SUBLANES, LANES), F32)],
        compiler_params=_cparams(("arbitrary",)),
    )(dc_key, dc_query, flb)


def _dx(dz, dfl, w_a, w_f, w_b, x, pre_gain, dh1):
    T = x.shape[0]
    tm = TM

    def body(*refs):
        dz_refs = refs[:6]
        dfl_ref, wa_ref, wf_ref, wb_ref, x_ref, g_ref, dh1_ref, gx_ref, acc_ref = refs[6:]

        @pl.when(pl.program_id(0) == 0)
        def _():
            acc_ref[...] = jnp.zeros_like(acc_ref)

        dxn = _dot(dfl_ref[...], wf_ref[...])
        for s in range(3):
            dxn = dxn + _dot(dz_refs[s][...], wa_ref[s * D:(s + 1) * D, :])
            dxn = dxn + _dot(dz_refs[3 + s][...], wb_ref[s * D:(s + 1) * D, :])
        xv = x_ref[...]
        rstd = _rstd(xv)
        xhat = xv * rstd
        gx_ref[...] = dh1_ref[...] + _rms_bwd(dxn * g_ref[...], xhat, rstd)
        acc_ref[0:1, :] += jnp.sum(dxn * xhat, axis=0, keepdims=True)

    return _call(
        body, name="dx", grid=(T // tm,),
        in_specs=[_row_spec(tm, D)] * 6 + [_row_spec(tm, LANES), _const_spec((3 * D, D)), _const_spec((LANES, D)),
                                           _const_spec((3 * D, D)), _row_spec(tm, D), _const_spec((1, D)),
                                           _row_spec(tm, D)],
        out_specs=[_row_spec(tm, D), _const_spec((SUBLANES, D))],
        out_shape=[jax.ShapeDtypeStruct((T, D), F32), jax.ShapeDtypeStruct((SUBLANES, D), F32)],
        compiler_params=_cparams(("arbitrary",), VMEM_BIG),
    )(*dz, dfl, w_a, w_f, w_b, x, pre_gain, dh1)


GRAD_ROWS = D_IN + SUBLANES


def _dw_in_segment(dz_s, xn, buf, s, bt):
    T = xn.shape[0]
    row0 = s * D + (H if s >= 3 else 0)

    def body(*refs):
        dz_ref, xn_ref, o_ref = refs[0], refs[1], refs[-1]

        @pl.when(pl.program_id(0) == 0)
        def _():
            o_ref[...] = jnp.zeros_like(o_ref)

        o_ref[...] += _dot_tn(dz_ref[...], xn_ref[...])

    tok = pl.BlockSpec((bt, D), lambda t: (t, 0))
    return _call(
        body, name="dw_in_%d" % s, grid=(T // bt,),
        in_specs=[tok, tok] + ([] if buf is None else [pl.BlockSpec(memory_space=pl.ANY)]),
        out_specs=pl.BlockSpec((pl.Element(D), pl.Element(D)), lambda t: (row0, 0)),
        out_shape=jax.ShapeDtypeStruct((GRAD_ROWS, D), F32),
        input_output_aliases={} if buf is None else {2: 0},
        compiler_params=_cparams(("arbitrary",)),
    )(*((dz_s, xn) if buf is None else (dz_s, xn, buf)))


def _dw_in_t(dz, dfl, xn, bt=512):
    T = xn.shape[0]
    nt = T // bt
    main = None
    for s in range(6):
        main = _dw_in_segment(dz[s], xn, main, s, min(T, 2048))

    def f_body(dfl_ref, xn_ref, main_ref, o_ref, acc_s):
        p = pl.program_id(0)
        t = pl.program_id(1)

        @pl.when(t == 0)
        def _():
            acc_s[...] = jnp.zeros_like(acc_s)

        @pl.when(p == 0)
        def _():
            acc_s[...] += _dot_tn(dfl_ref[...], xn_ref[...])

        @pl.when(t == nt - 1)
        def _():
            o_ref[...] = acc_s[:SUBLANES, :]

    fl_block = FL0 // SUBLANES
    end_block = D_IN // SUBLANES
    return _call(
        f_body, name="dw_in_f", grid=(2, nt),
        in_specs=[pl.BlockSpec((bt, LANES), lambda p, t: (t, 0)), pl.BlockSpec((bt, D), lambda p, t: (t, 0)),
                  pl.BlockSpec(memory_space=pl.ANY)],
        out_specs=pl.BlockSpec((SUBLANES, D), lambda p, t: (fl_block + p * (end_block - fl_block), 0)),
        out_shape=jax.ShapeDtypeStruct((GRAD_ROWS, D), F32),
        scratch_shapes=[pltpu.VMEM((LANES, D), F32)],
        input_output_aliases={2: 0},
        compiler_params=_cparams(("arbitrary", "arbitrary")),
    )(dfl, xn, main)


def _matmul_tn(a, b, name, bm=512, bn=1024, bt=2048):
    T, M = a.shape
    N = b.shape[1]
    bm, bn, bt = min(bm, M), min(bn, N), min(bt, T)

    def body(a_ref, b_ref, o_ref):
        @pl.when(pl.program_id(2) == 0)
        def _():
            o_ref[...] = jnp.zeros_like(o_ref)

        o_ref[...] += _dot_tn(a_ref[...], b_ref[...])

    return _call(
        body, name=name, grid=(M // bm, N // bn, T // bt),
        in_specs=[pl.BlockSpec((bt, bm), lambda i, j, t: (t, i)), pl.BlockSpec((bt, bn), lambda i, j, t: (t, j))],
        out_specs=pl.BlockSpec((bm, bn), lambda i, j, t: (i, j)),
        out_shape=jax.ShapeDtypeStruct((M, N), F32),
        compiler_params=_cparams(("parallel", "parallel", "arbitrary")),
    )(a, b)


HBM_SPEC = pl.BlockSpec(memory_space=pltpu.HBM)
VMEM_SPEC = pl.BlockSpec(memory_space=pltpu.VMEM)


def _position():
    return lax.axis_index("x"), lax.axis_index("y"), lax.axis_index("c")


def _other_chips(x, y):
    return [(1 - x, y), (x, 1 - y), (1 - x, 1 - y)]


def _gather_shards(shards, whole):
    na, nw = len(shards), len(whole)
    nall = na + nw

    def body(*refs):
        gather = _GatherPlan(refs[:nall], refs[nall:2 * nall], refs[2 * nall:], na)
        gather.send()
        gather.forward()
        gather.finish()

    arrs = list(shards) + list(whole)
    outs = _call(
        body, name="gather_shards",
        in_specs=[HBM_SPEC] * nall, out_specs=[HBM_SPEC] * nall,
        out_shape=_gather_out_shapes(arrs), scratch_shapes=_gather_semaphores(na, nall),
    )(*arrs)
    return _place_own(outs, arrs)


def _gather_out_shapes(arrs):
    return [jax.ShapeDtypeStruct((N_CHIPS,) + s.shape, s.dtype) for s in arrs]


def _gather_semaphores(na, nall):
    return [pltpu.SemaphoreType.DMA((3 * nall,)), pltpu.SemaphoreType.DMA((3 * nall,)),
            pltpu.SemaphoreType.DMA((3 * na,)), pltpu.SemaphoreType.DMA((3 * na,))]


def _place_own(outs, arrs):
    if not arrs:
        return []
    chip = 2 * lax.axis_index("x") + lax.axis_index("y")
    return [lax.dynamic_update_slice(o, a[None], (chip,) + (0,) * a.ndim) for o, a in zip(outs, arrs)]


class _GatherPlan:
    def __init__(self, srcs, dsts, sems, na):
        ici_send, ici_recv, d2d_send, d2d_recv = sems
        x, y, c = _position()
        chip = 2 * x + y
        nall = len(srcs)

        def half(a, which):
            rows = srcs[a].shape[0] // 2
            return pl.ds(pl.multiple_of(which * rows, 16), rows)

        def copy(src, dst, send, recv, k, to):
            return pltpu.make_async_remote_copy(src_ref=src, dst_ref=dst, send_sem=send.at[k], recv_sem=recv.at[k],
                                                device_id=to, device_id_type=MESH)

        self.first, self.landed, self.passed, self.returned = [], [], [], []
        for j, (px, py) in enumerate(_other_chips(x, y)):
            theirs = 2 * px + py
            for a in range(nall):
                k = j * nall + a
                if a < na:
                    self.first.append(copy(srcs[a].at[half(a, c), :], dsts[a].at[chip, half(a, c), :],
                                           ici_send, ici_recv, k, (px, py, c)))
                    mine = dsts[a].at[theirs, half(a, c), :]
                    other = dsts[a].at[theirs, half(a, 1 - c), :]
                    self.landed.append(copy(mine, mine, ici_send, ici_recv, k, (px, py, c)))
                    self.passed.append(copy(mine, mine, d2d_send, d2d_recv, j * na + a, (x, y, 1 - c)))
                    self.returned.append(copy(other, other, d2d_send, d2d_recv, j * na + a, (x, y, 1 - c)))
                else:
                    self.first.append(copy(srcs[a], dsts[a].at[chip], ici_send, ici_recv, k, (px, py, c)))
                    got = dsts[a].at[theirs]
                    self.landed.append(copy(got, got, ici_send, ici_recv, k, (px, py, c)))
                    self.passed.append(None)

    def send(self):
        for cp in self.first:
            cp.start()

    def forward(self):
        for arrival, fwd in zip(self.landed, self.passed):
            arrival.wait_recv()
            if fwd is not None:
                fwd.start()

    def finish(self):
        for cp in self.returned:
            cp.wait_recv()
        for cp in self.first + [f for f in self.passed if f is not None]:
            cp.wait_send()


W_ROWS = 1568
G_ROWS = 1552
SHARD_ROWS = D_IN // N_CHIPS
WINDOW_STEP = 1536


def _assemble_w_in(cont):
    cb = 256
    half = WINDOW_STEP

    def body(c_ref, wa_ref, wf_ref, wb_ref):
        x0 = c_ref[0].astype(F32)
        x1, x2, x3 = (pltpu.roll(c_ref[j].astype(F32), 2 * j, 0) for j in (1, 2, 3))
        wa = jnp.concatenate([x0[:half], x0[half:half + 16] + x1[:16], x1[16:half]], axis=0)
        wa_ref[...] = wa.astype(BF16)

        fl = x1[half:half + 16] + x2[:16]
        row = lax.broadcasted_iota(jnp.int32, fl.shape, 0)
        wf_ref[:16, :] = jnp.where(row < H, fl, 0.0).astype(BF16)
        wf_ref[16:, :] = jnp.zeros((LANES - 16, cb), BF16)

        mid = x2[half:half + SUBLANES] + x3[:SUBLANES]
        wb = jnp.concatenate([x2[SUBLANES:half], mid, x3[SUBLANES:half + SUBLANES]], axis=0)
        wb_ref[...] = wb.astype(BF16)

    return _call(
        body, name="assemble_w_in", grid=(D // cb,),
        in_specs=[pl.BlockSpec((N_CHIPS, W_ROWS, cb), lambda i: (0, 0, i))],
        out_specs=[pl.BlockSpec((3 * D, cb), lambda i: (0, i)), pl.BlockSpec((LANES, cb), lambda i: (0, i)),
                   pl.BlockSpec((3 * D, cb), lambda i: (0, i))],
        out_shape=[jax.ShapeDtypeStruct((3 * D, D), BF16), jax.ShapeDtypeStruct((LANES, D), BF16),
                   jax.ShapeDtypeStruct((3 * D, D), BF16)],
        compiler_params=_cparams(("parallel",)),
    )(cont)


def _pair_exchange_windows(grad_t):
    half_g = G_ROWS // 2

    def body(g_ref, got, send_sems, recv_sems):
        x, y, c = _position()
        copies = []
        for j in range(N_CHIPS):
            rows = pl.ds(pl.multiple_of(j * WINDOW_STEP + (1 - c) * half_g, SUBLANES), half_g)
            copies.append(pltpu.make_async_remote_copy(
                src_ref=g_ref.at[rows, :], dst_ref=got.at[j], send_sem=send_sems.at[j], recv_sem=recv_sems.at[j],
                device_id=(x, y, 1 - c), device_id_type=MESH))
        for cp in copies:
            cp.start()
        for cp in copies:
            cp.wait()

    return _call(
        body, name="pair_exchange_w_in",
        in_specs=[HBM_SPEC], out_specs=HBM_SPEC,
        out_shape=jax.ShapeDtypeStruct((N_CHIPS, half_g, D), F32),
        scratch_shapes=[pltpu.SemaphoreType.DMA((N_CHIPS,)), pltpu.SemaphoreType.DMA((N_CHIPS,))],
    )(grad_t)


def _pair_exchange(parts):
    na = len(parts)

    def body(*refs):
        srcs, got = refs[:na], refs[na:2 * na]
        send_sems, recv_sems = refs[2 * na:]
        x, y, c = _position()
        copies = []
        for a in range(na):
            half = srcs[a].shape[1] // 2
            rows = pl.ds(pl.multiple_of((1 - c) * half, SUBLANES), half)
            copies.append(pltpu.make_async_remote_copy(
                src_ref=srcs[a].at[:, rows, :], dst_ref=got[a], send_sem=send_sems.at[a], recv_sem=recv_sems.at[a],
                device_id=(x, y, 1 - c), device_id_type=MESH))
        for cp in copies:
            cp.start()
        for cp in copies:
            cp.wait()

    return _call(
        body, name="pair_exchange",
        in_specs=[HBM_SPEC] * na, out_specs=[HBM_SPEC] * na,
        out_shape=[jax.ShapeDtypeStruct((s.shape[0], s.shape[1] // 2, s.shape[2]), s.dtype) for s in parts],
        scratch_shapes=[pltpu.SemaphoreType.DMA((na,)), pltpu.SemaphoreType.DMA((na,))],
    )(*parts)


def _pair_sum(parts, gots, c):
    na = len(parts)

    def body(c_ref, *refs):
        for a in range(na):
            refs[2 * na + a][...] = (refs[a][...] + refs[na + a][...]).astype(BF16)

    mine = [pl.BlockSpec(g.shape, lambda i, c_ref: (0, c_ref[0], 0)) for g in gots]
    whole = [pl.BlockSpec(g.shape, lambda i, c_ref: (0, 0, 0)) for g in gots]
    grid_spec = pltpu.PrefetchScalarGridSpec(
        num_scalar_prefetch=1, grid=(1,), in_specs=mine + whole, out_specs=whole)
    return _call(
        body, name="pair_sum", grid_spec=grid_spec,
        out_shape=[jax.ShapeDtypeStruct(g.shape, BF16) for g in gots],
        compiler_params=_cparams(("arbitrary",), VMEM_BIG),
    )(c.reshape(1), *parts, *gots)


def _pair_sum_windows(grad_t, got, c):
    _, half, C = got.shape
    cb = 256

    def body(c_ref, a_ref, b_ref, o_ref):
        o_ref[0] = (a_ref[...] + b_ref[0]).astype(BF16)

    def mine(j, i, c_ref):
        return ((j * (WINDOW_STEP // SUBLANES) + c_ref[0] * (half // SUBLANES)) * SUBLANES, i * cb)

    spec = pl.BlockSpec((1, half, cb), lambda j, i, c_ref: (j, 0, i))
    grid_spec = pltpu.PrefetchScalarGridSpec(
        num_scalar_prefetch=1, grid=(N_CHIPS, C // cb),
        in_specs=[pl.BlockSpec((pl.Element(half), pl.Element(cb)), mine), spec], out_specs=spec)
    return _call(
        body, name="pair_sum_w_in", grid_spec=grid_spec,
        out_shape=jax.ShapeDtypeStruct((N_CHIPS, half, C), BF16),
        compiler_params=_cparams(("parallel", "parallel")),
    )(c.reshape(1), grad_t, got)


def _chip_exchange(sums):
    na = len(sums)

    def body(*refs):
        copies = _chip_copies(refs[:na], refs[na:2 * na], *refs[2 * na:])
        for cp in copies:
            cp.start()
        for cp in copies:
            cp.wait()

    return _call(
        body, name="chip_exchange",
        in_specs=[HBM_SPEC] * na, out_specs=[HBM_SPEC] * na,
        out_shape=[jax.ShapeDtypeStruct(s.shape, s.dtype) for s in sums],
        scratch_shapes=[pltpu.SemaphoreType.DMA((3 * na,)), pltpu.SemaphoreType.DMA((3 * na,))],
    )(*sums)


def _chip_sum(own, got, chip, name):
    _, half, C = got.shape
    cb = min(C, 256)

    def body(chip_ref, own_ref, g_ref, o_ref):
        for me in range(N_CHIPS):
            @pl.when(chip_ref[0] == me)
            def _(me=me):
                terms = [own_ref[0] if k == me else g_ref[k] for k in range(N_CHIPS)]
                acc = terms[0].astype(F32) + terms[1].astype(F32)
                acc = acc + terms[2].astype(F32)
                o_ref[...] = acc + terms[3].astype(F32)

    grid_spec = pltpu.PrefetchScalarGridSpec(
        num_scalar_prefetch=1, grid=(C // cb,),
        in_specs=[pl.BlockSpec((1, half, cb), lambda i, chip_ref: (chip_ref[0], 0, i)),
                  pl.BlockSpec((N_CHIPS, half, cb), lambda i, chip_ref: (0, 0, i))],
        out_specs=pl.BlockSpec((half, cb), lambda i, chip_ref: (0, i)))
    return _call(
        body, name=name, grid_spec=grid_spec,
        out_shape=jax.ShapeDtypeStruct((half, C), F32),
        compiler_params=_cparams(("parallel",)),
    )(chip.reshape(1), own, got)


def _pair_swap(halves):
    na = len(halves)

    def body(*refs):
        srcs, dsts = refs[:na], refs[na:2 * na]
        send_sems, recv_sems = refs[2 * na:]
        x, y, c = _position()
        copies = [pltpu.make_async_remote_copy(
            src_ref=srcs[a], dst_ref=dsts[a], send_sem=send_sems.at[a], recv_sem=recv_sems.at[a],
            device_id=(x, y, 1 - c), device_id_type=MESH) for a in range(na)]
        for cp in copies:
            cp.start()
        for cp in copies:
            cp.wait()

    return _call(
        body, name="pair_swap",
        in_specs=[HBM_SPEC] * na, out_specs=[HBM_SPEC] * na,
        out_shape=[jax.ShapeDtypeStruct(s.shape, s.dtype) for s in halves],
        scratch_shapes=[pltpu.SemaphoreType.DMA((na,)), pltpu.SemaphoreType.DMA((na,))],
    )(*halves)


def _allreduce_small(g):
    rows = g.shape[0]
    per = rows // N_DEV

    def body(g_ref, out_ref, got_ref, s1, r1, s2, r2):
        x, y, c = _position()
        me = 4 * x + 2 * y + c
        mine = pl.ds(pl.multiple_of(me * per, SUBLANES), per)
        peers = []
        for j in range(1, N_DEV):
            px = 1 - x if j & 4 else x
            py = 1 - y if j & 2 else y
            pc = 1 - c if j & 1 else c
            peers.append((px, py, pc))

        first = []
        for j, (px, py, pc) in enumerate(peers):
            theirs = pl.ds(pl.multiple_of((4 * px + 2 * py + pc) * per, SUBLANES), per)
            first.append(pltpu.make_async_remote_copy(
                src_ref=g_ref.at[theirs, :], dst_ref=got_ref.at[me], send_sem=s1.at[j], recv_sem=r1.at[j],
                device_id=(px, py, pc), device_id_type=MESH))
        for cp in first:
            cp.start()
        got_ref[me] = g_ref[mine, :]
        for cp in first:
            cp.wait()
        total = got_ref[0]
        for d in range(1, N_DEV):
            total = total + got_ref[d]
        out_ref[mine, :] = total

        second = []
        for j, peer in enumerate(peers):
            second.append(pltpu.make_async_remote_copy(
                src_ref=out_ref.at[mine, :], dst_ref=out_ref.at[mine, :], send_sem=s2.at[j], recv_sem=r2.at[j],
                device_id=peer, device_id_type=MESH))
        for cp in second:
            cp.start()
        for cp in second:
            cp.wait()

    sems = pltpu.SemaphoreType.DMA((N_DEV - 1,))
    return _call(
        body, name="allreduce_small", in_hbm=False,
        in_specs=[VMEM_SPEC], out_specs=VMEM_SPEC,
        out_shape=jax.ShapeDtypeStruct(g.shape, F32),
        scratch_shapes=[pltpu.VMEM((N_DEV, per, LANES), F32), sems, sems, sems, sems],
    )(g)


def _adamw_math(g, w, m, v):
    m2 = ADAM_B1 * m + (1.0 - ADAM_B1) * g
    v2 = ADAM_B2 * v + (1.0 - ADAM_B2) * (g * g)
    m_hat = m2 / (1.0 - ADAM_B1 ** ADAM_STEP)
    v_hat = v2 / (1.0 - ADAM_B2 ** ADAM_STEP)
    delta = (-ADAM_LR) * (m_hat / (jnp.sqrt(v_hat) + ADAM_EPS) + ADAM_WD * w)
    return delta, m2, v2


ADAMW_BLOCK_BYTES = 1 << 20


def _adamw_big(g, w, m, v, name, copy_g=False):
    R, C = g.shape
    if C == LANES:
        br, bc = min(R, ADAMW_BLOCK_BYTES // (4 * LANES)), LANES
    else:
        br, bc = R, min(C, max(LANES, ADAMW_BLOCK_BYTES // (4 * R) // LANES * LANES))
    n_out = 4 if copy_g else 3

    def body(g_ref, w_ref, m_ref, v_ref, d_ref, m2_ref, v2_ref, *g_out):
        gv = g_ref[...]
        d_ref[...], m2_ref[...], v2_ref[...] = _adamw_math(gv, w_ref[...], m_ref[...], v_ref[...])
        if copy_g:
            g_out[0][...] = gv

    spec = pl.BlockSpec((br, bc), lambda i, j: (i, j))
    out = jax.ShapeDtypeStruct((R, C), F32)
    return _call(
        body, name=name, grid=(pl.cdiv(R, br), C // bc),
        in_specs=[spec] * 4, out_specs=[spec] * n_out, out_shape=[out] * n_out,
        compiler_params=_cparams(("parallel", "parallel")),
    )(g, w, m, v)


def _adamw_small(gs, ws, ms, vs):
    n = len(gs)

    def body(*refs):
        for a in range(n):
            g_ref, w_ref, m_ref, v_ref = (refs[k * n + a] for k in range(4))
            d_ref, m2_ref, v2_ref = (refs[(4 + k) * n + a] for k in range(3))
            d_ref[...], m2_ref[...], v2_ref[...] = _adamw_math(g_ref[...], w_ref[...], m_ref[...], v_ref[...])

    outs = [jax.ShapeDtypeStruct(w.shape, F32) for w in ws]
    specs = [_const_spec(w.shape) for w in ws]
    return _call(
        body, name="adamw_small", grid=(1,),
        in_specs=specs * 4, out_specs=specs * 3, out_shape=outs * 3,
    )(*gs, *ws, *ms, *vs)


def _late_weights(st_out, st_ple, st_gate, st_conv):
    return st_out.reshape(DMIX, D), _from_chip_cols(st_ple), st_gate.reshape(D, D), _from_chip_cols(st_conv)


def _local_step(x, p, tgt, w_a, w_f, w_b, late, b_f, pre_gain, post_gain, conv_b,
                w_rgate, b_rgate, w_igate, b_igate, lam, gain_a, gain_l, ple_gain, b_gate,
                gather_late=False, early_reduce=None):
    b_f_pad = jnp.pad(b_f, ((0, 0), (0, LANES - H)))
    w_r = w_rgate.astype(BF16)
    w_i = w_igate.astype(BF16)

    xn, q_aug, k_aug, v_aug, g_attn, x_lru, g_lru, flb, vt_aug = _in_proj(x, pre_gain, w_a, w_f, w_b, b_f_pad)
    if gather_late:
        o, qx, stacks = _attn_fwd(q_aug, k_aug, vt_aug, late[:3], late[3:])
        late = _late_weights(*stacks)
    else:
        o, qx, _ = _attn_fwd(q_aug, k_aug, vt_aug)
    w_out_b, w_ple_b, w_gate_b, conv_w = late
    ycat, xc, h = _branches_fwd(o, g_attn, x_lru, g_lru, gain_a, gain_l, conv_w, conv_b, w_r, b_rgate, w_i, b_igate,
                                lam)
    dh1, dycat, dmix, h1b, dgp, pb, dpe, acc_t = _tail(ycat, x, p, tgt, w_out_b, post_gain, w_ple_b, ple_gain,
                                                       w_gate_b, b_gate)
    late_grads = [_matmul_tn(ycat, dmix, "dw_out"), _matmul_tn(pb, dpe, "dw_ple"),
                  _matmul_tn(h1b, dgp, "dw_ple_gate")]
    do_aug, dg_attn, dg_lru, dh, acc_b = _branches_bwd(dycat, o, g_attn, h, g_lru, gain_a, gain_l)
    dx_lru, gw_r, gw_i, acc_l = _lru_bwd(dh, h, xc, x_lru, conv_w, w_r, b_rgate, w_i, b_igate, lam)
    if early_reduce is None:
        dq, dk, dv, dc_key, dc_query, _ = _attn_bwd(q_aug, qx, k_aug, v_aug, do_aug)
    else:
        sent = early_reduce(late_grads)
        dq, dk, dv, dc_key, dc_query, received = _attn_bwd(q_aug, qx, k_aug, v_aug, do_aug, sent)
        late_grads = list(zip(sent, received))
    dfl, acc_f = _fgate_bwd(dc_key, dc_query, flb)
    dz = (dq, dk, dv, dg_attn, dx_lru, dg_lru)
    grad_x, acc_x = _dx(dz, dfl, w_a, w_f, w_b, x, pre_gain, dh1)

    grads = dict(
        w_in_t=_dw_in_t(dz, dfl, xn),
        w_out=late_grads[0],
        w_ple=late_grads[1],
        w_ple_gate=late_grads[2],
        w_rgate=gw_r,
        w_igate=gw_i,
        b_f=acc_f[0:1, :H],
        pre_gain=acc_x[0:1],
        post_gain=acc_t[0:1],
        conv_w=acc_l[0:4],
        conv_b=acc_l[4:5],
        b_rgate=acc_l[5:6],
        b_igate=acc_l[6:7],
        lru_lambda=acc_l[7:8],
        attn_out_gain=acc_b[0:1],
        lru_out_gain=acc_b[1:2],
        ple_gain=acc_t[1:2],
        b_ple_gate=acc_t[2:3],
    )
    loss = jnp.sum(acc_t[3])
    return loss, grad_x, grads


SMALL_ROWS = ["b_f", "pre_gain", "post_gain", "conv_w", "conv_b", "b_rgate", "b_igate", "lru_lambda",
              "attn_out_gain", "lru_out_gain", "ple_gain", "b_ple_gate"]
WEIGHTS = ["w_in", "b_f", "pre_gain", "post_gain", "conv_w", "conv_b", "w_rgate", "b_rgate", "w_igate", "b_igate",
           "lru_lambda", "attn_out_gain", "lru_out_gain", "w_out", "w_ple", "ple_gain", "w_ple_gate", "b_ple_gate"]
SHARDED = ["w_in", "w_out", "w_ple", "w_ple_gate"]


def _by_chip_cols(g):
    r, cols = g.shape
    return g.reshape(r, N_CHIPS, cols // N_CHIPS).transpose(1, 0, 2)


def _from_chip_cols(s):
    n, r, cols = s.shape
    return s.transpose(1, 0, 2).reshape(r, n * cols)


def kernel(x, p, w_in, b_f, pre_gain, post_gain, conv_w, conv_b, w_rgate, b_rgate, w_igate, b_igate, lru_lambda, attn_out_gain, lru_out_gain, w_out, w_ple, ple_gain, w_ple_gate, b_ple_gate, loss_target, m_w_in, m_b_f, m_pre_gain, m_post_gain, m_conv_w, m_conv_b, m_w_rgate, m_b_rgate, m_w_igate, m_b_igate, m_lru_lambda, m_attn_out_gain, m_lru_out_gain, m_w_out, m_w_ple, m_ple_gain, m_w_ple_gate, m_b_ple_gate, v_w_in, v_b_f, v_pre_gain, v_post_gain, v_conv_w, v_conv_b, v_w_rgate, v_b_rgate, v_w_igate, v_b_igate, v_lru_lambda, v_attn_out_gain, v_lru_out_gain, v_w_out, v_w_ple, v_ple_gain, v_w_ple_gate, v_b_ple_gate):
    w = dict(w_in=w_in, b_f=b_f, pre_gain=pre_gain, post_gain=post_gain, conv_w=conv_w, conv_b=conv_b,
             w_rgate=w_rgate, b_rgate=b_rgate, w_igate=w_igate, b_igate=b_igate, lru_lambda=lru_lambda,
             attn_out_gain=attn_out_gain, lru_out_gain=lru_out_gain, w_out=w_out, w_ple=w_ple, ple_gain=ple_gain,
             w_ple_gate=w_ple_gate, b_ple_gate=b_ple_gate)
    m = dict(w_in=m_w_in, b_f=m_b_f, pre_gain=m_pre_gain, post_gain=m_post_gain, conv_w=m_conv_w, conv_b=m_conv_b,
             w_rgate=m_w_rgate, b_rgate=m_b_rgate, w_igate=m_w_igate, b_igate=m_b_igate, lru_lambda=m_lru_lambda,
             attn_out_gain=m_attn_out_gain, lru_out_gain=m_lru_out_gain, w_out=m_w_out, w_ple=m_w_ple,
             ple_gain=m_ple_gain, w_ple_gate=m_w_ple_gate, b_ple_gate=m_b_ple_gate)
    v = dict(w_in=v_w_in, b_f=v_b_f, pre_gain=v_pre_gain, post_gain=v_post_gain, conv_w=v_conv_w, conv_b=v_conv_b,
             w_rgate=v_w_rgate, b_rgate=v_b_rgate, w_igate=v_w_igate, b_igate=v_b_igate, lru_lambda=v_lru_lambda,
             attn_out_gain=v_attn_out_gain, lru_out_gain=v_lru_out_gain, w_out=v_w_out, w_ple=v_w_ple,
             ple_gain=v_ple_gain, w_ple_gate=v_w_ple_gate, b_ple_gate=v_b_ple_gate)
    xi, yi, ci = _position()
    chip = 2 * xi + yi

    w_in_t, m_in_t, v_in_t = (jnp.swapaxes(t[0], 0, 1) for t in (w_in, m_w_in, v_w_in))
    window = jnp.pad(w_in_t.astype(BF16), ((0, W_ROWS - SHARD_ROWS), (0, 0)))

    (st_in,) = _gather_shards([window], [])
    w_a, w_f, w_b = _assemble_w_in(st_in)
    late_shards = (w_out[0].astype(BF16), w_ple[0].astype(BF16), w_ple_gate[0].astype(BF16), conv_w[0])

    def early_reduce(local):
        parts = [local[0].reshape(N_CHIPS, DMIX // N_CHIPS, D), _by_chip_cols(local[1]),
                 local[2].reshape(N_CHIPS, D // N_CHIPS, D)]
        return _pair_sum(parts, _pair_exchange(parts), ci)

    loss, grad_x, g = _local_step(
        x[0], p[0, 0], loss_target[0], w_a, w_f, w_b, late_shards, b_f, pre_gain, post_gain,
        conv_b, w_rgate[0], b_rgate, w_igate[0], b_igate, lru_lambda, attn_out_gain, lru_out_gain, ple_gain,
        b_ple_gate, gather_late=True, early_reduce=early_reduce)

    sum_in = _pair_sum_windows(g["w_in_t"], _pair_exchange_windows(g["w_in_t"]), ci)
    (recv_in,) = _chip_exchange([sum_in])
    sums = [sum_in] + [g[n][0] for n in SHARDED[1:]]
    recv = [recv_in] + [g[n][1] for n in SHARDED[1:]]
    halves = [_chip_sum(sums[a], recv[a], chip, "chip_sum_%d" % a) for a in range(4)]
    theirs = _pair_swap(halves)
    full = [jnp.concatenate([jnp.where(ci == 0, a, b), jnp.where(ci == 0, b, a)], axis=0)
            for a, b in zip(halves, theirs)]
    red = dict(zip(SHARDED, full))
    red["w_in"] = lax.dynamic_slice_in_dim(red["w_in"], 2 * chip, SHARD_ROWS, axis=0)

    rows = [jnp.pad(g["b_f"], ((0, 0), (0, D - H)))] + [g[n] for n in SMALL_ROWS[1:]]
    rows.append(jnp.pad(loss.reshape(1, 1), ((0, 0), (0, D - 1))))
    packed = jnp.concatenate([g["w_rgate"].reshape(NB * LANES, LANES), g["w_igate"].reshape(NB * LANES, LANES),
                              jnp.concatenate(rows, axis=0).reshape(LANES, LANES)], axis=0)
    summed = _allreduce_small(packed)
    red["w_rgate"] = summed[:D].reshape(1, NB, LANES, LANES)
    red["w_igate"] = summed[D:2 * D].reshape(1, NB, LANES, LANES)
    vec = summed[2 * D:].reshape(16, D)
    loss = vec[15, 0]
    r0 = 0
    for n in SMALL_ROWS:
        nr = 4 if n == "conv_w" else 1
        red[n] = vec[r0:r0 + nr]
        r0 += nr
    red["b_f"] = red["b_f"][:, :H]
    red["conv_w"] = lax.dynamic_slice_in_dim(red["conv_w"], chip * (D // N_CHIPS), D // N_CHIPS, axis=1)[None]

    delta, new_m, new_v = {}, {}, {}
    flat = [t.reshape(SHARD_ROWS * (D // LANES), LANES) for t in (red["w_in"], w_in_t, m_in_t, v_in_t)]
    outs_in = _adamw_big(*flat, "adamw_w_in", copy_g=True)
    delta["w_in"], new_m["w_in"], new_v["w_in"], red["w_in"] = (
        jnp.swapaxes(t.reshape(SHARD_ROWS, D), 0, 1)[None] for t in outs_in)
    for n in SHARDED[1:]:
        delta[n], new_m[n], new_v[n] = (t[None] for t in _adamw_big(red[n], w[n][0], m[n][0], v[n][0], "adamw_" + n))
        red[n] = red[n][None]
    small = [n for n in WEIGHTS if n not in SHARDED]
    outs = _adamw_small([red[n] for n in small], [w[n] for n in small], [m[n] for n in small],
                        [v[n] for n in small])
    ns = len(small)
    for a, n in enumerate(small):
        delta[n], new_m[n], new_v[n] = outs[a], outs[ns + a], outs[2 * ns + a]

    return (loss, grad_x[None], *[red[n] for n in WEIGHTS], *[delta[n] for n in WEIGHTS],
            *[new_m[n] for n in WEIGHTS], *[new_v[n] for n in WEIGHTS])
```

```python
import functools

import jax
import jax.numpy as jnp
import numpy as np
from jax import lax
from jax.experimental import pallas as pl
from jax.experimental.pallas import tpu as pltpu

F32 = jnp.float32
BF16 = jnp.bfloat16

D = 1024
H = 8
DH = 128
NB = 8
DPLE = 256
DMIX = 2 * D
D_IN = 4 * D + H + 2 * D
FL0 = 3 * D
RMS_EPS = 1e-6
LRU_C = 8.0
NEG = -1e30
LANES = 128
SUBLANES = 8

ADAM_LR = 0.001
ADAM_B1 = 0.9
ADAM_B2 = 0.999
ADAM_EPS = 1e-08
ADAM_WD = 0.01
ADAM_STEP = 10

TM = 256
TA = 512
FWD_HEADS = 4
BWD_HEADS = 2
VMEM_BIG = 56 * 1024 * 1024
VMEM_MID = 40 * 1024 * 1024

MESH = pl.DeviceIdType.MESH
N_CHIPS = 4
N_DEV = 8


def _call(body, *, out_shape, in_hbm=True, **kwargs):
    if not in_hbm:
        return pl.pallas_call(body, out_shape=out_shape, **kwargs)

    def pin(shape):
        return pltpu.HBM(shape.shape, shape.dtype) if isinstance(shape, jax.ShapeDtypeStruct) else shape

    fn = pl.pallas_call(body, out_shape=jax.tree.map(pin, out_shape), **kwargs)

    def run(*args):
        return fn(*[a if a.dtype == jnp.int32 else pltpu.with_memory_space_constraint(a, pltpu.HBM) for a in args])

    return run


def _cparams(sem, vmem=VMEM_MID):
    return pltpu.CompilerParams(dimension_semantics=sem, vmem_limit_bytes=vmem)


def _sigmoid(x):
    return 0.5 * jnp.tanh(0.5 * x) + 0.5


def _rstd(x):
    return lax.rsqrt(jnp.mean(x * x, axis=-1, keepdims=True) + RMS_EPS)


def _rms_bwd(t, xhat, rstd):
    return rstd * (t - xhat * jnp.mean(t * xhat, axis=-1, keepdims=True))


def _dot(a, b):
    return jnp.dot(a, b, preferred_element_type=F32)


def _dot_nt(a, b):
    return lax.dot_general(a, b, (((1,), (1,)), ((), ())), preferred_element_type=F32)


def _dot_tn(a, b):
    return lax.dot_general(a, b, (((0,), (0,)), ((), ())), preferred_element_type=F32)


def _dot_exact(a, b):
    return jnp.dot(a, b, preferred_element_type=F32, precision=lax.Precision.HIGHEST)


def _shift_down(x, j, halo):
    rolled = pltpu.roll(x, j, 0)
    row = lax.broadcasted_iota(jnp.int32, halo.shape, 0)
    top = jnp.where(row < j, pltpu.roll(halo, j, 0), rolled[:SUBLANES])
    return jnp.concatenate([top, rolled[SUBLANES:]], axis=0)


def _shift_up(x, j, nxt):
    tm = x.shape[0]
    rolled = pltpu.roll(x, tm - j, 0)
    row = lax.broadcasted_iota(jnp.int32, nxt.shape, 0)
    bot = jnp.where(row >= SUBLANES - j, pltpu.roll(nxt, SUBLANES - j, 0), rolled[tm - SUBLANES:])
    return jnp.concatenate([rolled[:tm - SUBLANES], bot], axis=0)


def _scan_fwd_into(a, u, carry, h_ref):
    tm = a.shape[0]
    sub = lax.broadcasted_iota(jnp.int32, a.shape, 0) & (SUBLANES - 1)
    d = 1
    while d < SUBLANES:
        keep = sub >= d
        a_s = jnp.where(keep, pltpu.roll(a, d, 0), 1.0)
        u_s = jnp.where(keep, pltpu.roll(u, d, 0), 0.0)
        u = u + a * u_s
        a = a * a_s
        d *= 2
    for g in range(tm // SUBLANES):
        rows = slice(g * SUBLANES, (g + 1) * SUBLANES)
        h_ref[rows, :] = u[rows] + a[rows] * carry
        carry = h_ref[(g + 1) * SUBLANES - 1:(g + 1) * SUBLANES, :]
    return carry


def _scan_bwd_into(b, u, g_ref):
    tm = b.shape[0]
    sub = lax.broadcasted_iota(jnp.int32, b.shape, 0) & (SUBLANES - 1)
    d = 1
    while d < SUBLANES:
        keep = sub < SUBLANES - d
        b_s = jnp.where(keep, pltpu.roll(b, tm - d, 0), 1.0)
        u_s = jnp.where(keep, pltpu.roll(u, tm - d, 0), 0.0)
        u = u + b * u_s
        b = b * b_s
        d *= 2
    nxt = jnp.zeros((1, b.shape[1]), F32)
    for g in reversed(range(tm // SUBLANES)):
        rows = slice(g * SUBLANES, (g + 1) * SUBLANES)
        g_ref[rows, :] = u[rows] + b[rows] * nxt
        nxt = g_ref[g * SUBLANES:g * SUBLANES + 1, :]


def _gate_pre(xc, w_ref):
    outs = []
    for n in range(NB):
        outs.append(_dot(xc[:, n * LANES:(n + 1) * LANES].astype(BF16), w_ref[n]))
    return jnp.concatenate(outs, axis=1)


def _gate_pre_t(d, w_ref):
    outs = []
    for n in range(NB):
        outs.append(_dot_nt(d[:, n * LANES:(n + 1) * LANES].astype(BF16), w_ref[n]))
    return jnp.concatenate(outs, axis=1)


def _softplus_neg(lam):
    return jnp.maximum(-lam, 0.0) + jnp.log(1.0 + jnp.exp(-jnp.abs(lam)))


def _row_spec(tm, width):
    return pl.BlockSpec((tm, width), lambda i: (i, 0))


def _const_spec(shape):
    nd = len(shape)
    return pl.BlockSpec(shape, lambda *_: (0,) * nd)


AUG = 2 * DH
LOG2E = 1.4426950408889634
LN2 = 0.6931471805599453
Q_SCALE = DH ** -0.5 * LOG2E


def _split3(x):
    hi = x.astype(BF16)
    r1 = x - hi.astype(F32)
    mid = r1.astype(BF16)
    lo = (r1 - mid.astype(F32)).astype(BF16)
    return hi, mid, lo


def _extras(col, ones_from):
    t = col.shape[0]
    hi, mid, lo = _split3(jnp.broadcast_to(col, (t, LANES)))
    lane = lax.broadcasted_iota(jnp.int32, (t, LANES), 1)
    rest = jnp.zeros((t, LANES), BF16)
    if ones_from is not None:
        rest = jnp.where((lane >= ones_from) & (lane < ones_from + 3), 1.0, 0.0).astype(BF16)
    return jnp.where(lane == 0, hi, jnp.where(lane == 1, mid, jnp.where(lane == 2, lo, rest)))


def _selectors():
    sel_q = np.zeros((3 * LANES, H * LANES), np.float32)
    sel_k = np.zeros((3 * LANES, H * LANES), np.float32)
    for hd in range(H):
        for piece in range(3):
            sel_q[piece * LANES + hd, hd * LANES + piece] = 1.0
            sel_k[piece * LANES + hd, hd * LANES + 3 + piece] = -1.0
    return jnp.asarray(sel_q, BF16), jnp.asarray(sel_k, BF16)


def _in_proj(x, pre_gain, w_a, w_f, w_b, b_f_pad):
    T = x.shape[0]
    tm = TM
    sel_q, sel_k = _selectors()

    def body(x_ref, g_ref, wa_ref, wf_ref, wb_ref, bf_ref, sq_ref, sk_ref,
             xn_ref, qa_ref, ka_ref, va_ref, ga_ref, xl_ref, gl_ref, flb_ref, vt_ref, c_s, carry):
        @pl.when(pl.program_id(0) == 0)
        def _():
            carry[...] = jnp.zeros_like(carry)

        xv = x_ref[...]
        xn = (xv * _rstd(xv) * g_ref[...]).astype(BF16)
        xn_ref[...] = xn
        for s, o_ref in enumerate((ga_ref, xl_ref, gl_ref)):
            o_ref[...] = _dot_nt(xn, wb_ref[s * D:(s + 1) * D, :]).astype(o_ref.dtype)
        flb = _dot_nt(xn, wf_ref[...]) + bf_ref[...]
        flb_ref[...] = flb
        lane = lax.broadcasted_iota(jnp.int32, flb.shape, 1)
        ls = jnp.where(lane < H, jnp.minimum(flb, 0.0) - jnp.log(1.0 + jnp.exp(-jnp.abs(flb))), 0.0)
        r = lax.broadcasted_iota(jnp.int32, (tm, tm), 0)
        c = lax.broadcasted_iota(jnp.int32, (tm, tm), 1)
        cs = _dot_exact((c <= r).astype(F32), ls) + carry[...]
        c_s[...] = cs
        carry[...] = c_s[tm - 1:tm, :]

        pieces = jnp.concatenate(_split3(cs * LOG2E), axis=1)
        ones_q = jnp.where((lane >= 3) & (lane < 6), 1.0, 0.0)
        ones_k = jnp.where(lane < 3, 1.0, 0.0)
        zq = _dot_nt(xn, wa_ref[0:D, :]) * Q_SCALE
        zk = _dot_nt(xn, wa_ref[D:2 * D, :])
        zv = _dot_nt(xn, wa_ref[2 * D:3 * D, :])
        ex_q = _dot(pieces, sq_ref[...])
        ex_k = _dot(pieces, sk_ref[...])
        for hd in range(H):
            head = slice(hd * DH, (hd + 1) * DH)
            lo, hi = hd * AUG, hd * AUG + DH
            qa_ref[:, lo:hi] = zq[:, head].astype(BF16)
            qa_ref[:, hi:hi + DH] = (ex_q[:, head] + ones_q).astype(BF16)
            ka_ref[:, lo:hi] = zk[:, head].astype(BF16)
            ka_ref[:, hi:hi + DH] = (ex_k[:, head] + ones_k).astype(BF16)
            va_ref[:, lo:hi] = zv[:, head].astype(BF16)
            va_ref[:, hi:hi + DH] = ones_k.astype(BF16)
            vt_ref[lo:hi, :] = jnp.transpose(zv[:, head]).astype(BF16)
            vt_ref[hi:hi + DH, :] = jnp.where(lax.broadcasted_iota(jnp.int32, (DH, tm), 0) < 3, 1.0, 0.0).astype(BF16)

    bf = jax.ShapeDtypeStruct((T, D), BF16)
    aug = jax.ShapeDtypeStruct((T, H * AUG), BF16)
    f32 = jax.ShapeDtypeStruct((T, D), F32)
    sel_spec = _const_spec((3 * LANES, H * LANES))
    return _call(
        body, name="in_proj", grid=(T // tm,),
        in_specs=[_row_spec(tm, D), _const_spec((1, D)), _const_spec((3 * D, D)), _const_spec((LANES, D)),
                  _const_spec((3 * D, D)), _const_spec((1, LANES)), sel_spec, sel_spec],
        out_specs=[_row_spec(tm, D)] + [_row_spec(tm, H * AUG)] * 3 + [_row_spec(tm, D)] * 3 + [_row_spec(tm, LANES)]
        + [pl.BlockSpec((H * AUG, tm), lambda i: (0, i))],
        out_shape=[bf, aug, aug, aug, f32, f32, f32, jax.ShapeDtypeStruct((T, LANES), F32),
                   jax.ShapeDtypeStruct((H * AUG, T), BF16)],
        scratch_shapes=[pltpu.VMEM((tm, LANES), F32), pltpu.VMEM((1, LANES), F32)],
        compiler_params=_cparams(("arbitrary",), VMEM_BIG),
    )(x, pre_gain, w_a, w_f, w_b, b_f_pad, sel_q, sel_k)


def _causal_pairs(n, q_major):
    if q_major:
        pairs = [(qi, ki) for qi in range(n) for ki in range(qi + 1)]
    else:
        pairs = [(ki, qi) for ki in range(n) for qi in range(ki, n)]
    return (jnp.asarray([a for a, _ in pairs], jnp.int32), jnp.asarray([b for _, b in pairs], jnp.int32))


def _attn_fwd(q_aug, k_aug, vt_aug, shards=(), whole=()):
    T = q_aug.shape[0]
    t = TA
    n = T // t
    hp = FWD_HEADS
    heads = range(hp)
    qi_tab, ki_tab = _causal_pairs(n, q_major=True)
    na, nall = len(shards), len(shards) + len(whole)
    n_h, n_j = H // hp, qi_tab.shape[0]

    def body(qi_ref, ki_ref, q_ref, k_ref, vt_ref, *rest):
        srcs, rest = rest[:nall], rest[nall:]
        o_ref, qx_ref = rest[:2]
        dsts, rest = rest[2:2 + nall], rest[2 + nall:]
        m_s, acc_s = rest[:2]
        h = pl.program_id(0)
        j = pl.program_id(1)
        qi = qi_ref[j]
        ki = ki_ref[j]

        if nall:
            gather = _GatherPlan(srcs, dsts, rest[2:], na)
            pl.when((h == 0) & (j == 0))(gather.send)
            pl.when((h == n_h - 1) & (j == 0))(gather.forward)
            pl.when((h == n_h - 1) & (j == n_j - 1))(gather.finish)

        @pl.when(ki == 0)
        def _():
            m_s[...] = jnp.full(m_s.shape, NEG, F32)
            acc_s[...] = jnp.zeros_like(acc_s)

        def step(on_diagonal):
            cols = [slice(a * AUG, (a + 1) * AUG) for a in heads]
            if on_diagonal:
                krow = lax.broadcasted_iota(jnp.int32, (t, t), 0)
                qcol = lax.broadcasted_iota(jnp.int32, (t, t), 1)
            for a in heads:
                st = _dot_nt(k_ref[:, cols[a]], q_ref[:, cols[a]])
                if on_diagonal:
                    st = jnp.where(krow <= qcol, st, NEG)
                m_prev = m_s[a]
                m_new = jnp.maximum(m_prev, jnp.max(st, axis=0, keepdims=True))
                pt = jnp.exp2(st - m_new).astype(BF16)
                acc_s[a] = jnp.exp2(m_prev - m_new) * acc_s[a] + _dot(vt_ref[cols[a], :], pt)
                m_s[a] = m_new

        @pl.when(ki < qi)
        def _():
            step(False)

        @pl.when(ki == qi)
        def _():
            step(True)
            piece = lax.broadcasted_iota(jnp.int32, (DH, t), 0)
            for a in heads:
                l = acc_s[a, DH:DH + 1, :]
                ex = jnp.transpose(q_ref[:, a * AUG + DH:(a + 1) * AUG].astype(F32))
                c2 = jnp.sum(jnp.where(piece < 3, ex, 0.0), axis=0, keepdims=True)
                hi, mid, lo = _split3(jnp.broadcast_to(c2 - (m_s[a] + jnp.log(l) * LOG2E), (DH, t)))
                ones = jnp.where((piece >= 3) & (piece < 6), 1.0, 0.0).astype(BF16)
                ex_t = jnp.where(piece == 0, hi, jnp.where(piece == 1, mid, jnp.where(piece == 2, lo, ones)))
                o_ref[:, a * DH:(a + 1) * DH] = jnp.transpose(acc_s[a, :DH, :] / l)
                qx_ref[:, a * DH:(a + 1) * DH] = jnp.transpose(ex_t.astype(F32)).astype(BF16)

    q_spec = pl.BlockSpec((t, hp * AUG), lambda h, j, qi_ref, ki_ref: (qi_ref[j], h))
    k_spec = pl.BlockSpec((t, hp * AUG), lambda h, j, qi_ref, ki_ref: (ki_ref[j], h))
    vt_spec = pl.BlockSpec((hp * AUG, t), lambda h, j, qi_ref, ki_ref: (h, ki_ref[j]))
    out_spec = pl.BlockSpec((t, hp * DH), lambda h, j, qi_ref, ki_ref: (qi_ref[j], h))
    arrs = list(shards) + list(whole)
    grid_spec = pltpu.PrefetchScalarGridSpec(
        num_scalar_prefetch=2, grid=(n_h, n_j),
        in_specs=[q_spec, k_spec, vt_spec] + [HBM_SPEC] * nall, out_specs=[out_spec, out_spec] + [HBM_SPEC] * nall,
        scratch_shapes=[pltpu.VMEM((hp, 1, t), F32), pltpu.VMEM((hp, AUG, t), F32)]
        + (_gather_semaphores(na, nall) if nall else []))
    outs = _call(
        body, name="attn_fwd", grid_spec=grid_spec,
        out_shape=[jax.ShapeDtypeStruct((T, D), F32), jax.ShapeDtypeStruct((T, D), BF16)] + _gather_out_shapes(arrs),
        compiler_params=_cparams(("arbitrary", "arbitrary"), VMEM_BIG),
    )(qi_tab, ki_tab, q_aug, k_aug, vt_aug, *arrs)
    return outs[0], outs[1], _place_own(outs[2:], arrs)


def _lru_gates(xc, wr_ref, br_ref, wi_ref, bi_ref, lam_ref):
    r = _sigmoid(_gate_pre(xc, wr_ref) + br_ref[...])
    ig = _sigmoid(_gate_pre(xc, wi_ref) + bi_ref[...])
    sp = _softplus_neg(lam_ref[...])
    la = (-LRU_C) * r * sp
    a = jnp.exp(la)
    y = -jnp.tanh(la) * (a * a + 1.0)
    return r, ig, sp, a, jnp.sqrt(y), lax.rsqrt(y)


def _branches_fwd(o, g_attn, x_lru, g_lru, gain_a, gain_l, conv_w, conv_b, w_r, b_r, w_i, b_i, lam):
    T = o.shape[0]
    tm = TM

    def body(o_ref, ga_ref, xl_ref, gl_ref, gna_ref, gnl_ref, cw_ref, cb_ref, wr_ref, br_ref, wi_ref, bi_ref,
             lam_ref, ycat_ref, xc_ref, h_ref, halo_s, hc_s):
        @pl.when(pl.program_id(0) == 0)
        def _():
            halo_s[...] = jnp.zeros_like(halo_s)
            hc_s[...] = jnp.zeros_like(hc_s)

        ov = o_ref[...]
        ga = ga_ref[...]
        ya = ov * _rstd(ov) * gna_ref[...] * (ga * _sigmoid(ga))
        ycat_ref[:, :D] = ya.astype(BF16)

        xl = xl_ref[...]
        halo = halo_s[...]
        xc = xl * cw_ref[3:4, :] + cb_ref[...]
        for j in range(3):
            xc = xc + _shift_down(xl, 3 - j, halo) * cw_ref[j:j + 1, :]
        halo_s[...] = xl_ref[tm - SUBLANES:tm, :]
        xc_ref[...] = xc

        _, ig, _, a, sq, _ = _lru_gates(xc, wr_ref, br_ref, wi_ref, bi_ref, lam_ref)
        u = sq * (ig * xc)
        hc_s[...] = _scan_fwd_into(a, u, hc_s[...], h_ref)
        hh = h_ref[...]

        gl = gl_ref[...]
        yl = hh * _rstd(hh) * gnl_ref[...] * (gl * _sigmoid(gl))
        ycat_ref[:, D:] = yl.astype(BF16)

    vec = _const_spec((1, D))
    wspec = _const_spec((NB, LANES, LANES))
    return _call(
        body, name="branches_fwd", grid=(T // tm,),
        in_specs=[_row_spec(tm, D)] * 4 + [vec, vec, _const_spec((4, D)), vec, wspec, vec, wspec, vec, vec],
        out_specs=[_row_spec(tm, DMIX), _row_spec(tm, D), _row_spec(tm, D)],
        out_shape=[jax.ShapeDtypeStruct((T, DMIX), BF16), jax.ShapeDtypeStruct((T, D), F32),
                   jax.ShapeDtypeStruct((T, D), F32)],
        scratch_shapes=[pltpu.VMEM((SUBLANES, D), F32), pltpu.VMEM((1, D), F32)],
        compiler_params=_cparams(("arbitrary",)),
    )(o, g_attn, x_lru, g_lru, gain_a, gain_l, conv_w, conv_b, w_r, b_r, w_i, b_i, lam)


def _tail(ycat, x, p, tgt, w_out, post_gain, w_ple, ple_gain, w_gate, b_gate):
    T = x.shape[0]
    tm = TM

    def body(ycat_ref, x_ref, p_ref, t_ref, wo_ref, pg_ref, wp_ref, eg_ref, wg_ref, bg_ref,
             dh1_ref, dycat_ref, dmix_ref, h1b_ref, dgp_ref, pb_ref, dpe_ref, acc_ref):
        @pl.when(pl.program_id(0) == 0)
        def _():
            acc_ref[...] = jnp.zeros_like(acc_ref)

        mix = _dot(ycat_ref[...], wo_ref[...])
        rstd_m = _rstd(mix)
        mhat = mix * rstd_m
        h1 = x_ref[...] + mhat * pg_ref[...]
        pb = p_ref[...].astype(BF16)
        pb_ref[...] = pb
        pe = _dot(pb, wp_ref[...])
        rstd_p = _rstd(pe)
        pehat = pe * rstd_p
        e = pehat * eg_ref[...]
        h1b = h1.astype(BF16)
        h1b_ref[...] = h1b
        gate = _sigmoid(_dot(h1b, wg_ref[...]) + bg_ref[...])
        diff = (h1 + gate * e) - t_ref[...]

        dy = diff * (1.0 / D)
        de = dy * gate
        dgp = (dy * e) * gate * (1.0 - gate)
        dgpb = dgp.astype(BF16)
        dgp_ref[...] = dgpb
        dh1 = dy + _dot_nt(dgpb, wg_ref[...])
        dh1_ref[...] = dh1
        dpe_ref[...] = _rms_bwd(de * eg_ref[...], pehat, rstd_p).astype(BF16)
        dmix = _rms_bwd(dh1 * pg_ref[...], mhat, rstd_m).astype(BF16)
        dmix_ref[...] = dmix
        dycat_ref[...] = _dot_nt(dmix, wo_ref[...])

        acc_ref[0:1, :] += jnp.sum(dh1 * mhat, axis=0, keepdims=True)
        acc_ref[1:2, :] += jnp.sum(de * pehat, axis=0, keepdims=True)
        acc_ref[2:3, :] += jnp.sum(dgp, axis=0, keepdims=True)
        acc_ref[3:4, :] += jnp.sum(diff * diff, axis=0, keepdims=True) * (0.5 / D)

    vec = _const_spec((1, D))
    bf = jax.ShapeDtypeStruct((T, D), BF16)
    return _call(
        body, name="tail", grid=(T // tm,),
        in_specs=[_row_spec(tm, DMIX), _row_spec(tm, D), _row_spec(tm, DPLE), _row_spec(tm, D),
                  _const_spec((DMIX, D)), vec, _const_spec((DPLE, D)), vec, _const_spec((D, D)), vec],
        out_specs=[_row_spec(tm, D), _row_spec(tm, DMIX), _row_spec(tm, D), _row_spec(tm, D), _row_spec(tm, D),
                   _row_spec(tm, DPLE), _row_spec(tm, D), _const_spec((SUBLANES, D))],
        out_shape=[jax.ShapeDtypeStruct((T, D), F32), jax.ShapeDtypeStruct((T, DMIX), F32), bf, bf, bf,
                   jax.ShapeDtypeStruct((T, DPLE), BF16), bf, jax.ShapeDtypeStruct((SUBLANES, D), F32)],
        compiler_params=_cparams(("arbitrary",), VMEM_BIG),
    )(ycat, x, p, tgt, w_out, post_gain, w_ple, ple_gain, w_gate, b_gate)


def _branches_bwd(dycat, o, g_attn, h, g_lru, gain_a, gain_l):
    T = o.shape[0]
    tm = TM

    def body(dy_ref, o_ref, ga_ref, h_ref, gl_ref, gna_ref, gnl_ref,
             do_ref, dga_ref, dgl_ref, dh_ref, acc_ref):
        @pl.when(pl.program_id(0) == 0)
        def _():
            acc_ref[...] = jnp.zeros_like(acc_ref)

        def branch(val, g, gain, dyv):
            rstd = _rstd(val)
            vhat = val * rstd
            sig = _sigmoid(g)
            dn = dyv * (g * sig)
            dg = dyv * (vhat * gain) * (sig * (1.0 + g * (1.0 - sig)))
            dgain = jnp.sum(dn * vhat, axis=0, keepdims=True)
            return _rms_bwd(dn * gain, vhat, rstd), dg, dgain

        ov = o_ref[...]
        do, dga, dgain_a = branch(ov, ga_ref[...], gna_ref[...], dy_ref[:, :D])
        dga_ref[...] = dga.astype(BF16)
        prod = do * ov
        for hd in range(H):
            head = slice(hd * DH, (hd + 1) * DH)
            do_ref[:, hd * AUG:hd * AUG + DH] = do[:, head].astype(BF16)
            do_ref[:, hd * AUG + DH:(hd + 1) * AUG] = _extras(-jnp.sum(prod[:, head], axis=1, keepdims=True), None)

        dh, dgl, dgain_l = branch(h_ref[...], gl_ref[...], gnl_ref[...], dy_ref[:, D:])
        dh_ref[...] = dh
        dgl_ref[...] = dgl.astype(BF16)
        acc_ref[0:1, :] += dgain_a
        acc_ref[1:2, :] += dgain_l

    vec = _const_spec((1, D))
    bf = jax.ShapeDtypeStruct((T, D), BF16)
    return _call(
        body, name="branches_bwd", grid=(T // tm,),
        in_specs=[_row_spec(tm, DMIX)] + [_row_spec(tm, D)] * 4 + [vec, vec],
        out_specs=[_row_spec(tm, H * AUG), _row_spec(tm, D), _row_spec(tm, D), _row_spec(tm, D),
                   _const_spec((SUBLANES, D))],
        out_shape=[jax.ShapeDtypeStruct((T, H * AUG), BF16), bf, bf, jax.ShapeDtypeStruct((T, D), F32),
                   jax.ShapeDtypeStruct((SUBLANES, D), F32)],
        compiler_params=_cparams(("arbitrary",)),
    )(dycat, o, g_attn, h, g_lru, gain_a, gain_l)


def _lru_bwd(dh, h, xc, x_lru, conv_w, w_r, b_r, w_i, b_i, lam):
    T = dh.shape[0]
    tm = TM
    nt = T // tm
    per = tm // SUBLANES

    def body(dh_ref, h_ref, hprev_ref, xc_ref, xl_ref, xlprev_ref, cw_ref, wr_ref, br_ref, wi_ref, bi_ref, lam_ref,
             dxl_ref, dwr_ref, dwi_ref, acc_ref, carry_s, dxc_next_s, top_s, dht_s):
        i = pl.program_id(0)

        @pl.when(i == 0)
        def _():
            acc_ref[...] = jnp.zeros_like(acc_ref)
            dwr_ref[...] = jnp.zeros_like(dwr_ref)
            dwi_ref[...] = jnp.zeros_like(dwi_ref)
            carry_s[...] = jnp.zeros_like(carry_s)
            dxc_next_s[...] = jnp.zeros_like(dxc_next_s)

        inner = jnp.where(i == nt - 1, 0.0, 1.0)
        xc = xc_ref[...]
        r, ig, sp, a, sq, inv_sq = _lru_gates(xc, wr_ref, br_ref, wi_ref, bi_ref, lam_ref)

        row = lax.broadcasted_iota(jnp.int32, (tm, D), 0)
        u = dh_ref[...] + jnp.where(row == tm - 1, carry_s[...], 0.0)
        _scan_bwd_into(pltpu.roll(a, tm - 1, 0), u, dht_s)
        dht = dht_s[...]
        top_s[...] = a[:SUBLANES, :] * dht[:SUBLANES, :]
        carry_s[...] = top_s[0:1, :]

        hprev = hprev_ref[...] * inner
        da = dht * _shift_down(h_ref[...], 1, hprev)
        dig = dht * sq * xc
        dxc = dht * sq * ig
        dsq = dht * ig * xc
        dla = da * a - dsq * (a * a) * inv_sq
        dr = dla * ((-LRU_C) * sp)
        dpr = dr * r * (1.0 - r)
        dpi = dig * ig * (1.0 - ig)
        for n in range(NB):
            blk = slice(n * LANES, (n + 1) * LANES)
            xcb = xc[:, blk].astype(BF16)
            dwr_ref[n] += _dot_tn(xcb, dpr[:, blk].astype(BF16))
            dwi_ref[n] += _dot_tn(xcb, dpi[:, blk].astype(BF16))
        dxc = dxc + _gate_pre_t(dpr, wr_ref) + _gate_pre_t(dpi, wi_ref)

        xl = xl_ref[...]
        xlprev = xlprev_ref[...] * inner
        nxt = dxc_next_s[...]
        dxl = dxc * cw_ref[3:4, :]
        acc_ref[3:4, :] += jnp.sum(dxc * xl, axis=0, keepdims=True)
        for j in range(3):
            dxl = dxl + _shift_up(dxc, 3 - j, nxt) * cw_ref[j:j + 1, :]
            acc_ref[j:j + 1, :] += jnp.sum(dxc * _shift_down(xl, 3 - j, xlprev), axis=0, keepdims=True)
        dxc_next_s[...] = dxc[:SUBLANES, :]
        dxl_ref[...] = dxl.astype(BF16)

        acc_ref[4:5, :] += jnp.sum(dxc, axis=0, keepdims=True)
        acc_ref[5:6, :] += jnp.sum(dpr, axis=0, keepdims=True)
        acc_ref[6:7, :] += jnp.sum(dpi, axis=0, keepdims=True)
        acc_ref[7:8, :] += jnp.sum(dla * ((-LRU_C) * r), axis=0, keepdims=True)

        @pl.when(i == nt - 1)
        def _():
            lam_v = lam_ref[...]
            acc_ref[7:8, :] = acc_ref[7:8, :] * (-_sigmoid(-lam_v))

    rev = pl.BlockSpec((tm, D), lambda i: (nt - 1 - i, 0))
    prev8 = pl.BlockSpec((SUBLANES, D), lambda i: (jnp.maximum((nt - 1 - i) * per - 1, 0), 0))
    vec = _const_spec((1, D))
    wspec = _const_spec((NB, LANES, LANES))
    bf = jax.ShapeDtypeStruct((T, D), BF16)
    return _call(
        body, name="lru_bwd", grid=(nt,),
        in_specs=[rev, rev, prev8, rev, rev, prev8, _const_spec((4, D)), wspec, vec, wspec, vec, vec],
        out_specs=[rev, wspec, wspec, _const_spec((SUBLANES, D))],
        out_shape=[bf, jax.ShapeDtypeStruct((NB, LANES, LANES), F32), jax.ShapeDtypeStruct((NB, LANES, LANES), F32),
                   jax.ShapeDtypeStruct((SUBLANES, D), F32)],
        scratch_shapes=[pltpu.VMEM((1, D), F32), pltpu.VMEM((SUBLANES, D), F32), pltpu.VMEM((SUBLANES, D), F32),
                        pltpu.VMEM((tm, D), F32)],
        compiler_params=_cparams(("arbitrary",)),
    )(dh, h, h, xc, x_lru, x_lru, conv_w, w_r, b_r, w_i, b_i, lam)


def _chip_copies(srcs, dsts, send_sems, recv_sems):
    x, y, c = _position()
    chip = 2 * x + y
    na = len(srcs)
    return [pltpu.make_async_remote_copy(
        src_ref=srcs[a].at[2 * px + py], dst_ref=dsts[a].at[chip], send_sem=send_sems.at[j * na + a],
        recv_sem=recv_sems.at[j * na + a], device_id=(px, py, c), device_id_type=MESH)
        for j, (px, py) in enumerate(_other_chips(x, y)) for a in range(na)]


def _attn_bwd(q_aug, qx, k_aug, v_aug, do_aug, exchange=()):
    T = q_aug.shape[0]
    t = TA
    n = T // t
    hp = BWD_HEADS
    heads = range(hp)
    scale = DH ** -0.5
    ki_tab, qi_tab = _causal_pairs(n, q_major=False)
    last = ki_tab.shape[0] - 1
    ne = len(exchange)
    n_h = H // hp

    def body(ki_ref, qi_ref, q_ref, qx_ref, k_ref, v_ref, do_ref, *rest):
        sent, rest = rest[:ne], rest[ne:]
        dq_ref, dk_ref, dv_ref, dck_ref, dcq_ref = rest[:5]
        received, rest = rest[5:5 + ne], rest[5 + ne:]
        dq_s, dk_s, dv_s = rest[:3]
        j = pl.program_id(1)
        ki = ki_ref[j]
        qi = qi_ref[j]

        if ne:
            first_step = (pl.program_id(0) == 0) & (j == 0)
            last_step = (pl.program_id(0) == n_h - 1) & (j == last)

            @pl.when(first_step)
            def _():
                for cp in _chip_copies(sent, received, *rest[3:]):
                    cp.start()

            @pl.when(last_step)
            def _():
                for cp in _chip_copies(sent, received, *rest[3:]):
                    cp.wait()

        @pl.when(j == 0)
        def _():
            dq_s[...] = jnp.zeros_like(dq_s)

        @pl.when(qi == ki)
        def _():
            dk_s[...] = jnp.zeros_like(dk_s)
            dv_s[...] = jnp.zeros_like(dv_s)

        def step(on_diagonal):
            cols = [slice(a * AUG, (a + 1) * AUG) for a in heads]
            qb = [jnp.concatenate([q_ref[:, a * AUG:a * AUG + DH], qx_ref[:, a * DH:(a + 1) * DH]], axis=1)
                  for a in heads]
            st = [_dot_nt(k_ref[:, cols[a]], qb[a]) for a in heads]
            dpd = [_dot_nt(v_ref[:, cols[a]], do_ref[:, cols[a]]) for a in heads]
            if on_diagonal:
                krow = lax.broadcasted_iota(jnp.int32, (t, t), 0)
                qcol = lax.broadcasted_iota(jnp.int32, (t, t), 1)
                st = [jnp.where(krow <= qcol, st[a], NEG) for a in heads]
            pt = [jnp.exp2(st[a]) for a in heads]
            dsb = [(pt[a] * dpd[a]).astype(BF16) for a in heads]
            ptb = [pt[a].astype(BF16) for a in heads]
            off = pl.multiple_of(qi * t, t)
            for a in heads:
                dv_s[a] += _dot(ptb[a], do_ref[:, a * AUG:a * AUG + DH])
                dk_s[a] += _dot(dsb[a], qb[a])
            for a in heads:
                dq_s[a, pl.ds(off, t), :] += _dot_tn(dsb[a], k_ref[:, cols[a]])

        @pl.when(qi > ki)
        def _():
            step(False)

        @pl.when(qi == ki)
        def _():
            step(True)

        @pl.when(qi == n - 1)
        def _():
            for a in heads:
                dk_ref[:, a * DH:(a + 1) * DH] = (dk_s[a, :, :DH] * LN2).astype(BF16)
                dv_ref[:, a * DH:(a + 1) * DH] = dv_s[a].astype(BF16)
                dck_ref[a] = jnp.broadcast_to(dk_s[a, :, DH + 3:DH + 4], (t, LANES))

        @pl.when(j == last)
        def _():
            for a in heads:
                dq_ref[:, a * DH:(a + 1) * DH] = (dq_s[a, :, :DH] * scale).astype(BF16)
                dcq_ref[a] = jnp.broadcast_to(dq_s[a, :, DH:DH + 1], (T, LANES))

    qside = pl.BlockSpec((t, hp * AUG), lambda h, j, ki_ref, qi_ref: (qi_ref[j], h))
    qxside = pl.BlockSpec((t, hp * DH), lambda h, j, ki_ref, qi_ref: (qi_ref[j], h))
    kside = pl.BlockSpec((t, hp * AUG), lambda h, j, ki_ref, qi_ref: (ki_ref[j], h))
    kout = pl.BlockSpec((t, hp * DH), lambda h, j, ki_ref, qi_ref: (ki_ref[j], h))
    bf = jax.ShapeDtypeStruct((T, D), BF16)
    sums = jax.ShapeDtypeStruct((H, T, LANES), F32)
    grid_spec = pltpu.PrefetchScalarGridSpec(
        num_scalar_prefetch=2, grid=(n_h, ki_tab.shape[0]),
        in_specs=[qside, qxside, kside, kside, qside] + [HBM_SPEC] * ne,
        out_specs=[pl.BlockSpec((T, hp * DH), lambda h, j, ki_ref, qi_ref: (0, h)), kout, kout,
                   pl.BlockSpec((hp, t, LANES), lambda h, j, ki_ref, qi_ref: (h, ki_ref[j], 0)),
                   pl.BlockSpec((hp, T, LANES), lambda h, j, ki_ref, qi_ref: (h, 0, 0))] + [HBM_SPEC] * ne,
        scratch_shapes=[pltpu.VMEM((hp, T, AUG), F32), pltpu.VMEM((hp, t, AUG), F32), pltpu.VMEM((hp, t, DH), F32)]
        + ([pltpu.SemaphoreType.DMA((3 * ne,)), pltpu.SemaphoreType.DMA((3 * ne,))] if ne else []))
    outs = _call(
        body, name="attn_bwd", grid_spec=grid_spec,
        out_shape=[bf, bf, bf, sums, sums] + [jax.ShapeDtypeStruct(s.shape, s.dtype) for s in exchange],
        compiler_params=_cparams(("arbitrary", "arbitrary"), VMEM_BIG),
    )(ki_tab, qi_tab, q_aug, qx, k_aug, v_aug, do_aug, *exchange)
    return (*outs[:5], list(outs[5:]))


def _fgate_bwd(dc_key, dc_query, flb):
    T = flb.shape[0]
    tm = TM
    nt = T // tm

    def body(dck_ref, dcq_ref, flb_ref, dfl_ref, acc_ref, carry, top_s):
        @pl.when(pl.program_id(0) == 0)
        def _():
            carry[...] = jnp.zeros_like(carry)
            acc_ref[...] = jnp.zeros_like(acc_ref)

        flb = flb_ref[...]
        lane = lax.broadcasted_iota(jnp.int32, flb.shape, 1)
        dc = jnp.zeros(flb.shape, F32)
        for hd in range(H):
            dc = dc + jnp.where(lane == hd, dcq_ref[hd] - dck_ref[hd], 0.0)
        r = lax.broadcasted_iota(jnp.int32, (tm, tm), 0)
        c = lax.broadcasted_iota(jnp.int32, (tm, tm), 1)
        dls = _dot_exact((c >= r).astype(F32), dc) + carry[...]
        top_s[...] = dls[:SUBLANES, :]
        carry[...] = top_s[0:1, :]
        dfl = jnp.where(lane < H, dls * _sigmoid(-flb), 0.0)
        dfl_ref[...] = dfl.astype(BF16)
        acc_ref[0:1, :] += jnp.sum(dfl, axis=0, keepdims=True)

    rev = pl.BlockSpec((tm, LANES), lambda i: (nt - 1 - i, 0))
    return _call(
        body, name="fgate_bwd", grid=(nt,),
        in_specs=[pl.BlockSpec((H, tm, LANES), lambda i: (0, nt - 1 - i, 0))] * 2 + [rev],
        out_specs=[rev, _const_spec((SUBLANES, LANES))],
        out_shape=[jax.ShapeDtypeStruct((T, LANES), BF16), jax.ShapeDtypeStruct((SUBLANES, LANES), F32)],
        scratch_shapes=[pltpu.VMEM((1, LANES), F32), pltpu.VMEM((SUBLANES, LANES), F32)],
        compiler_params=_cparams(("arbitrary",)),
    )(dc_key, dc_query, flb)


def _dx(dz, dfl, w_a, w_f, w_b, x, pre_gain, dh1, exchange=()):
    T = x.shape[0]
    tm = TM
    nt = T // tm
    ne = len(exchange)

    def body(*refs):
        dz_refs = refs[:6]
        dfl_ref, wa_ref, wf_ref, wb_ref, x_ref, g_ref, dh1_ref = refs[6:13]
        sent = refs[13:13 + ne]
        gx_ref, acc_ref = refs[13 + ne:15 + ne]
        received, sems = refs[15 + ne:15 + 2 * ne], refs[15 + 2 * ne:]

        @pl.when(pl.program_id(0) == 0)
        def _():
            acc_ref[...] = jnp.zeros_like(acc_ref)
            for cp in _chip_copies(sent, received, *sems) if ne else ():
                cp.start()

        if ne:
            @pl.when(pl.program_id(0) == nt - 1)
            def _():
                for cp in _chip_copies(sent, received, *sems):
                    cp.wait()

        dxn = _dot(dfl_ref[...], wf_ref[...])
        for s in range(3):
            dxn = dxn + _dot(dz_refs[s][...], wa_ref[s * D:(s + 1) * D, :])
            dxn = dxn + _dot(dz_refs[3 + s][...], wb_ref[s * D:(s + 1) * D, :])
        xv = x_ref[...]
        rstd = _rstd(xv)
        xhat = xv * rstd
        gx_ref[...] = dh1_ref[...] + _rms_bwd(dxn * g_ref[...], xhat, rstd)
        acc_ref[0:1, :] += jnp.sum(dxn * xhat, axis=0, keepdims=True)

    outs = _call(
        body, name="dx", grid=(nt,),
        in_specs=[_row_spec(tm, D)] * 6 + [_row_spec(tm, LANES), _const_spec((3 * D, D)), _const_spec((LANES, D)),
                                           _const_spec((3 * D, D)), _row_spec(tm, D), _const_spec((1, D)),
                                           _row_spec(tm, D)] + [HBM_SPEC] * ne,
        out_specs=[_row_spec(tm, D), _const_spec((SUBLANES, D))] + [HBM_SPEC] * ne,
        out_shape=[jax.ShapeDtypeStruct((T, D), F32), jax.ShapeDtypeStruct((SUBLANES, D), F32)]
        + [jax.ShapeDtypeStruct(s.shape, s.dtype) for s in exchange],
        scratch_shapes=[pltpu.SemaphoreType.DMA((3 * ne,)), pltpu.SemaphoreType.DMA((3 * ne,))] if ne else [],
        compiler_params=_cparams(("arbitrary",), VMEM_BIG),
    )(*dz, dfl, w_a, w_f, w_b, x, pre_gain, dh1, *exchange)
    return outs[0], outs[1], list(outs[2:])


GRAD_ROWS = D_IN + SUBLANES


def _dw_in_segment(dz_s, xn, buf, s, bt):
    T = xn.shape[0]
    row0 = s * D + (H if s >= 3 else 0)

    def body(*refs):
        dz_ref, xn_ref, o_ref = refs[0], refs[1], refs[-1]

        @pl.when(pl.program_id(0) == 0)
        def _():
            o_ref[...] = jnp.zeros_like(o_ref)

        o_ref[...] += _dot_tn(dz_ref[...], xn_ref[...])

    tok = pl.BlockSpec((bt, D), lambda t: (t, 0))
    return _call(
        body, name="dw_in_%d" % s, grid=(T // bt,),
        in_specs=[tok, tok] + ([] if buf is None else [pl.BlockSpec(memory_space=pl.ANY)]),
        out_specs=pl.BlockSpec((pl.Element(D), pl.Element(D)), lambda t: (row0, 0)),
        out_shape=jax.ShapeDtypeStruct((GRAD_ROWS, D), F32),
        input_output_aliases={} if buf is None else {2: 0},
        compiler_params=_cparams(("arbitrary",)),
    )(*((dz_s, xn) if buf is None else (dz_s, xn, buf)))


def _dw_in_t(dz, dfl, xn, bt=512):
    T = xn.shape[0]
    nt = T // bt
    main = None
    for s in range(6):
        main = _dw_in_segment(dz[s], xn, main, s, min(T, 2048))

    def f_body(dfl_ref, xn_ref, main_ref, o_ref, acc_s):
        p = pl.program_id(0)
        t = pl.program_id(1)

        @pl.when(t == 0)
        def _():
            acc_s[...] = jnp.zeros_like(acc_s)

        @pl.when(p == 0)
        def _():
            acc_s[...] += _dot_tn(dfl_ref[...], xn_ref[...])

        @pl.when(t == nt - 1)
        def _():
            o_ref[...] = acc_s[:SUBLANES, :]

    fl_block = FL0 // SUBLANES
    end_block = D_IN // SUBLANES
    return _call(
        f_body, name="dw_in_f", grid=(2, nt),
        in_specs=[pl.BlockSpec((bt, LANES), lambda p, t: (t, 0)), pl.BlockSpec((bt, D), lambda p, t: (t, 0)),
                  pl.BlockSpec(memory_space=pl.ANY)],
        out_specs=pl.BlockSpec((SUBLANES, D), lambda p, t: (fl_block + p * (end_block - fl_block), 0)),
        out_shape=jax.ShapeDtypeStruct((GRAD_ROWS, D), F32),
        scratch_shapes=[pltpu.VMEM((LANES, D), F32)],
        input_output_aliases={2: 0},
        compiler_params=_cparams(("arbitrary", "arbitrary")),
    )(dfl, xn, main)


def _matmul_tn(a, b, name, bm=512, bn=1024, bt=2048):
    T, M = a.shape
    N = b.shape[1]
    bm, bn, bt = min(bm, M), min(bn, N), min(bt, T)

    def body(a_ref, b_ref, o_ref):
        @pl.when(pl.program_id(2) == 0)
        def _():
            o_ref[...] = jnp.zeros_like(o_ref)

        o_ref[...] += _dot_tn(a_ref[...], b_ref[...])

    return _call(
        body, name=name, grid=(M // bm, N // bn, T // bt),
        in_specs=[pl.BlockSpec((bt, bm), lambda i, j, t: (t, i)), pl.BlockSpec((bt, bn), lambda i, j, t: (t, j))],
        out_specs=pl.BlockSpec((bm, bn), lambda i, j, t: (i, j)),
        out_shape=jax.ShapeDtypeStruct((M, N), F32),
        compiler_params=_cparams(("parallel", "parallel", "arbitrary")),
    )(a, b)


HBM_SPEC = pl.BlockSpec(memory_space=pltpu.HBM)
VMEM_SPEC = pl.BlockSpec(memory_space=pltpu.VMEM)


def _position():
    return lax.axis_index("x"), lax.axis_index("y"), lax.axis_index("c")


def _other_chips(x, y):
    return [(1 - x, y), (x, 1 - y), (1 - x, 1 - y)]


def _gather_shards(shards, whole):
    na, nw = len(shards), len(whole)
    nall = na + nw

    def body(*refs):
        gather = _GatherPlan(refs[:nall], refs[nall:2 * nall], refs[2 * nall:], na)
        gather.send()
        gather.forward()
        gather.finish()

    arrs = list(shards) + list(whole)
    outs = _call(
        body, name="gather_shards",
        in_specs=[HBM_SPEC] * nall, out_specs=[HBM_SPEC] * nall,
        out_shape=_gather_out_shapes(arrs), scratch_shapes=_gather_semaphores(na, nall),
    )(*arrs)
    return _place_own(outs, arrs)


def _gather_out_shapes(arrs):
    return [jax.ShapeDtypeStruct((N_CHIPS,) + s.shape, s.dtype) for s in arrs]


def _gather_semaphores(na, nall):
    return [pltpu.SemaphoreType.DMA((3 * nall,)), pltpu.SemaphoreType.DMA((3 * nall,)),
            pltpu.SemaphoreType.DMA((3 * na,)), pltpu.SemaphoreType.DMA((3 * na,))]


def _place_own(outs, arrs):
    if not arrs:
        return []
    chip = 2 * lax.axis_index("x") + lax.axis_index("y")
    return [lax.dynamic_update_slice(o, a[None], (chip,) + (0,) * a.ndim) for o, a in zip(outs, arrs)]


class _GatherPlan:
    def __init__(self, srcs, dsts, sems, na):
        ici_send, ici_recv, d2d_send, d2d_recv = sems
        x, y, c = _position()
        chip = 2 * x + y
        nall = len(srcs)

        def half(a, which):
            rows = srcs[a].shape[0] // 2
            return pl.ds(pl.multiple_of(which * rows, 16), rows)

        def copy(src, dst, send, recv, k, to):
            return pltpu.make_async_remote_copy(src_ref=src, dst_ref=dst, send_sem=send.at[k], recv_sem=recv.at[k],
                                                device_id=to, device_id_type=MESH)

        self.first, self.landed, self.passed, self.returned = [], [], [], []
        for j, (px, py) in enumerate(_other_chips(x, y)):
            theirs = 2 * px + py
            for a in range(nall):
                k = j * nall + a
                if a < na:
                    self.first.append(copy(srcs[a].at[half(a, c), :], dsts[a].at[chip, half(a, c), :],
                                           ici_send, ici_recv, k, (px, py, c)))
                    mine = dsts[a].at[theirs, half(a, c), :]
                    other = dsts[a].at[theirs, half(a, 1 - c), :]
                    self.landed.append(copy(mine, mine, ici_send, ici_recv, k, (px, py, c)))
                    self.passed.append(copy(mine, mine, d2d_send, d2d_recv, j * na + a, (x, y, 1 - c)))
                    self.returned.append(copy(other, other, d2d_send, d2d_recv, j * na + a, (x, y, 1 - c)))
                else:
                    self.first.append(copy(srcs[a], dsts[a].at[chip], ici_send, ici_recv, k, (px, py, c)))
                    got = dsts[a].at[theirs]
                    self.landed.append(copy(got, got, ici_send, ici_recv, k, (px, py, c)))
                    self.passed.append(None)

    def send(self):
        for cp in self.first:
            cp.start()

    def forward(self):
        for arrival, fwd in zip(self.landed, self.passed):
            arrival.wait_recv()
            if fwd is not None:
                fwd.start()

    def finish(self):
        for cp in self.returned:
            cp.wait_recv()
        for cp in self.first + [f for f in self.passed if f is not None]:
            cp.wait_send()


W_ROWS = 1568
G_ROWS = 1552
SHARD_ROWS = D_IN // N_CHIPS
WINDOW_STEP = 1536


def _assemble_w_in(cont):
    cb = 256
    half = WINDOW_STEP

    def body(c_ref, wa_ref, wf_ref, wb_ref):
        x0 = c_ref[0].astype(F32)
        x1, x2, x3 = (pltpu.roll(c_ref[j].astype(F32), 2 * j, 0) for j in (1, 2, 3))
        wa = jnp.concatenate([x0[:half], x0[half:half + 16] + x1[:16], x1[16:half]], axis=0)
        wa_ref[...] = wa.astype(BF16)

        fl = x1[half:half + 16] + x2[:16]
        row = lax.broadcasted_iota(jnp.int32, fl.shape, 0)
        wf_ref[:16, :] = jnp.where(row < H, fl, 0.0).astype(BF16)
        wf_ref[16:, :] = jnp.zeros((LANES - 16, cb), BF16)

        mid = x2[half:half + SUBLANES] + x3[:SUBLANES]
        wb = jnp.concatenate([x2[SUBLANES:half], mid, x3[SUBLANES:half + SUBLANES]], axis=0)
        wb_ref[...] = wb.astype(BF16)

    return _call(
        body, name="assemble_w_in", grid=(D // cb,),
        in_specs=[pl.BlockSpec((N_CHIPS, W_ROWS, cb), lambda i: (0, 0, i))],
        out_specs=[pl.BlockSpec((3 * D, cb), lambda i: (0, i)), pl.BlockSpec((LANES, cb), lambda i: (0, i)),
                   pl.BlockSpec((3 * D, cb), lambda i: (0, i))],
        out_shape=[jax.ShapeDtypeStruct((3 * D, D), BF16), jax.ShapeDtypeStruct((LANES, D), BF16),
                   jax.ShapeDtypeStruct((3 * D, D), BF16)],
        compiler_params=_cparams(("parallel",)),
    )(cont)


def _pair_exchange_windows(grad_t):
    half_g = G_ROWS // 2

    def body(g_ref, got, send_sems, recv_sems):
        x, y, c = _position()
        copies = []
        for j in range(N_CHIPS):
            rows = pl.ds(pl.multiple_of(j * WINDOW_STEP + (1 - c) * half_g, SUBLANES), half_g)
            copies.append(pltpu.make_async_remote_copy(
                src_ref=g_ref.at[rows, :], dst_ref=got.at[j], send_sem=send_sems.at[j], recv_sem=recv_sems.at[j],
                device_id=(x, y, 1 - c), device_id_type=MESH))
        for cp in copies:
            cp.start()
        for cp in copies:
            cp.wait()

    return _call(
        body, name="pair_exchange_w_in",
        in_specs=[HBM_SPEC], out_specs=HBM_SPEC,
        out_shape=jax.ShapeDtypeStruct((N_CHIPS, half_g, D), F32),
        scratch_shapes=[pltpu.SemaphoreType.DMA((N_CHIPS,)), pltpu.SemaphoreType.DMA((N_CHIPS,))],
    )(grad_t)


def _pair_exchange(parts):
    na = len(parts)

    def body(*refs):
        srcs, got = refs[:na], refs[na:2 * na]
        send_sems, recv_sems = refs[2 * na:]
        x, y, c = _position()
        copies = []
        for a in range(na):
            half = srcs[a].shape[1] // 2
            rows = pl.ds(pl.multiple_of((1 - c) * half, SUBLANES), half)
            copies.append(pltpu.make_async_remote_copy(
                src_ref=srcs[a].at[:, rows, :], dst_ref=got[a], send_sem=send_sems.at[a], recv_sem=recv_sems.at[a],
                device_id=(x, y, 1 - c), device_id_type=MESH))
        for cp in copies:
            cp.start()
        for cp in copies:
            cp.wait()

    return _call(
        body, name="pair_exchange",
        in_specs=[HBM_SPEC] * na, out_specs=[HBM_SPEC] * na,
        out_shape=[jax.ShapeDtypeStruct((s.shape[0], s.shape[1] // 2, s.shape[2]), s.dtype) for s in parts],
        scratch_shapes=[pltpu.SemaphoreType.DMA((na,)), pltpu.SemaphoreType.DMA((na,))],
    )(*parts)


def _pair_sum(parts, gots, c):
    na = len(parts)

    def body(c_ref, *refs):
        for a in range(na):
            refs[2 * na + a][...] = (refs[a][...] + refs[na + a][...]).astype(BF16)

    mine = [pl.BlockSpec(g.shape, lambda i, c_ref: (0, c_ref[0], 0)) for g in gots]
    whole = [pl.BlockSpec(g.shape, lambda i, c_ref: (0, 0, 0)) for g in gots]
    grid_spec = pltpu.PrefetchScalarGridSpec(
        num_scalar_prefetch=1, grid=(1,), in_specs=mine + whole, out_specs=whole)
    return _call(
        body, name="pair_sum", grid_spec=grid_spec,
        out_shape=[jax.ShapeDtypeStruct(g.shape, BF16) for g in gots],
        compiler_params=_cparams(("arbitrary",), VMEM_BIG),
    )(c.reshape(1), *parts, *gots)


def _pair_sum_windows(grad_t, got, c):
    _, half, C = got.shape
    cb = 256

    def body(c_ref, a_ref, b_ref, o_ref):
        o_ref[0] = (a_ref[...] + b_ref[0]).astype(BF16)

    def mine(j, i, c_ref):
        return ((j * (WINDOW_STEP // SUBLANES) + c_ref[0] * (half // SUBLANES)) * SUBLANES, i * cb)

    spec = pl.BlockSpec((1, half, cb), lambda j, i, c_ref: (j, 0, i))
    grid_spec = pltpu.PrefetchScalarGridSpec(
        num_scalar_prefetch=1, grid=(N_CHIPS, C // cb),
        in_specs=[pl.BlockSpec((pl.Element(half), pl.Element(cb)), mine), spec], out_specs=spec)
    return _call(
        body, name="pair_sum_w_in", grid_spec=grid_spec,
        out_shape=jax.ShapeDtypeStruct((N_CHIPS, half, C), BF16),
        compiler_params=_cparams(("parallel", "parallel")),
    )(c.reshape(1), grad_t, got)


def _chip_exchange(sums):
    na = len(sums)

    def body(*refs):
        copies = _chip_copies(refs[:na], refs[na:2 * na], *refs[2 * na:])
        for cp in copies:
            cp.start()
        for cp in copies:
            cp.wait()

    return _call(
        body, name="chip_exchange",
        in_specs=[HBM_SPEC] * na, out_specs=[HBM_SPEC] * na,
        out_shape=[jax.ShapeDtypeStruct(s.shape, s.dtype) for s in sums],
        scratch_shapes=[pltpu.SemaphoreType.DMA((3 * na,)), pltpu.SemaphoreType.DMA((3 * na,))],
    )(*sums)


def _chip_sum(own, got, chip, name):
    _, half, C = got.shape
    cb = min(C, 256)

    def body(chip_ref, own_ref, g_ref, o_ref):
        for me in range(N_CHIPS):
            @pl.when(chip_ref[0] == me)
            def _(me=me):
                terms = [own_ref[0] if k == me else g_ref[k] for k in range(N_CHIPS)]
                acc = terms[0].astype(F32) + terms[1].astype(F32)
                acc = acc + terms[2].astype(F32)
                o_ref[...] = acc + terms[3].astype(F32)

    grid_spec = pltpu.PrefetchScalarGridSpec(
        num_scalar_prefetch=1, grid=(C // cb,),
        in_specs=[pl.BlockSpec((1, half, cb), lambda i, chip_ref: (chip_ref[0], 0, i)),
                  pl.BlockSpec((N_CHIPS, half, cb), lambda i, chip_ref: (0, 0, i))],
        out_specs=pl.BlockSpec((half, cb), lambda i, chip_ref: (0, i)))
    return _call(
        body, name=name, grid_spec=grid_spec,
        out_shape=jax.ShapeDtypeStruct((half, C), F32),
        compiler_params=_cparams(("parallel",)),
    )(chip.reshape(1), own, got)


def _pair_swap(halves):
    na = len(halves)

    def body(*refs):
        srcs, dsts = refs[:na], refs[na:2 * na]
        send_sems, recv_sems = refs[2 * na:]
        x, y, c = _position()
        copies = [pltpu.make_async_remote_copy(
            src_ref=srcs[a], dst_ref=dsts[a], send_sem=send_sems.at[a], recv_sem=recv_sems.at[a],
            device_id=(x, y, 1 - c), device_id_type=MESH) for a in range(na)]
        for cp in copies:
            cp.start()
        for cp in copies:
            cp.wait()

    return _call(
        body, name="pair_swap",
        in_specs=[HBM_SPEC] * na, out_specs=[HBM_SPEC] * na,
        out_shape=[jax.ShapeDtypeStruct(s.shape, s.dtype) for s in halves],
        scratch_shapes=[pltpu.SemaphoreType.DMA((na,)), pltpu.SemaphoreType.DMA((na,))],
    )(*halves)


def _allreduce_small(g):
    rows = g.shape[0]
    per = rows // N_DEV

    def body(g_ref, out_ref, got_ref, s1, r1, s2, r2):
        x, y, c = _position()
        me = 4 * x + 2 * y + c
        mine = pl.ds(pl.multiple_of(me * per, SUBLANES), per)
        peers = []
        for j in range(1, N_DEV):
            px = 1 - x if j & 4 else x
            py = 1 - y if j & 2 else y
            pc = 1 - c if j & 1 else c
            peers.append((px, py, pc))

        first = []
        for j, (px, py, pc) in enumerate(peers):
            theirs = pl.ds(pl.multiple_of((4 * px + 2 * py + pc) * per, SUBLANES), per)
            first.append(pltpu.make_async_remote_copy(
                src_ref=g_ref.at[theirs, :], dst_ref=got_ref.at[me], send_sem=s1.at[j], recv_sem=r1.at[j],
                device_id=(px, py, pc), device_id_type=MESH))
        for cp in first:
            cp.start()
        got_ref[me] = g_ref[mine, :]
        for cp in first:
            cp.wait()
        total = got_ref[0]
        for d in range(1, N_DEV):
            total = total + got_ref[d]
        out_ref[mine, :] = total

        second = []
        for j, peer in enumerate(peers):
            second.append(pltpu.make_async_remote_copy(
                src_ref=out_ref.at[mine, :], dst_ref=out_ref.at[mine, :], send_sem=s2.at[j], recv_sem=r2.at[j],
                device_id=peer, device_id_type=MESH))
        for cp in second:
            cp.start()
        for cp in second:
            cp.wait()

    sems = pltpu.SemaphoreType.DMA((N_DEV - 1,))
    return _call(
        body, name="allreduce_small", in_hbm=False,
        in_specs=[VMEM_SPEC], out_specs=VMEM_SPEC,
        out_shape=jax.ShapeDtypeStruct(g.shape, F32),
        scratch_shapes=[pltpu.VMEM((N_DEV, per, LANES), F32), sems, sems, sems, sems],
    )(g)


def _adamw_math(g, w, m, v):
    m2 = ADAM_B1 * m + (1.0 - ADAM_B1) * g
    v2 = ADAM_B2 * v + (1.0 - ADAM_B2) * (g * g)
    m_hat = m2 / (1.0 - ADAM_B1 ** ADAM_STEP)
    v_hat = v2 / (1.0 - ADAM_B2 ** ADAM_STEP)
    delta = (-ADAM_LR) * (m_hat / (jnp.sqrt(v_hat) + ADAM_EPS) + ADAM_WD * w)
    return delta, m2, v2


ADAMW_BLOCK_BYTES = 1 << 20


def _adamw_big(g, w, m, v, name, copy_g=False):
    R, C = g.shape
    if C == LANES:
        br, bc = min(R, ADAMW_BLOCK_BYTES // (4 * LANES)), LANES
    else:
        br, bc = R, min(C, max(LANES, ADAMW_BLOCK_BYTES // (4 * R) // LANES * LANES))
    n_out = 4 if copy_g else 3

    def body(g_ref, w_ref, m_ref, v_ref, d_ref, m2_ref, v2_ref, *g_out):
        gv = g_ref[...]
        d_ref[...], m2_ref[...], v2_ref[...] = _adamw_math(gv, w_ref[...], m_ref[...], v_ref[...])
        if copy_g:
            g_out[0][...] = gv

    spec = pl.BlockSpec((br, bc), lambda i, j: (i, j))
    out = jax.ShapeDtypeStruct((R, C), F32)
    return _call(
        body, name=name, grid=(pl.cdiv(R, br), C // bc),
        in_specs=[spec] * 4, out_specs=[spec] * n_out, out_shape=[out] * n_out,
        compiler_params=_cparams(("parallel", "parallel")),
    )(g, w, m, v)


def _adamw_small(gs, ws, ms, vs):
    n = len(gs)

    def body(*refs):
        for a in range(n):
            g_ref, w_ref, m_ref, v_ref = (refs[k * n + a] for k in range(4))
            d_ref, m2_ref, v2_ref = (refs[(4 + k) * n + a] for k in range(3))
            d_ref[...], m2_ref[...], v2_ref[...] = _adamw_math(g_ref[...], w_ref[...], m_ref[...], v_ref[...])

    outs = [jax.ShapeDtypeStruct(w.shape, F32) for w in ws]
    specs = [_const_spec(w.shape) for w in ws]
    return _call(
        body, name="adamw_small", grid=(1,),
        in_specs=specs * 4, out_specs=specs * 3, out_shape=outs * 3,
    )(*gs, *ws, *ms, *vs)


def _late_weights(st_out, st_ple, st_gate, st_conv):
    return st_out.reshape(DMIX, D), _from_chip_cols(st_ple), st_gate.reshape(D, D), _from_chip_cols(st_conv)


def _local_step(x, p, tgt, w_a, w_f, w_b, late, b_f, pre_gain, post_gain, conv_b,
                w_rgate, b_rgate, w_igate, b_igate, lam, gain_a, gain_l, ple_gain, b_gate,
                gather_late=False, early_reduce=None, w_in_reduce=None):
    b_f_pad = jnp.pad(b_f, ((0, 0), (0, LANES - H)))
    w_r = w_rgate.astype(BF16)
    w_i = w_igate.astype(BF16)

    xn, q_aug, k_aug, v_aug, g_attn, x_lru, g_lru, flb, vt_aug = _in_proj(x, pre_gain, w_a, w_f, w_b, b_f_pad)
    if gather_late:
        o, qx, stacks = _attn_fwd(q_aug, k_aug, vt_aug, late[:3], late[3:])
        late = _late_weights(*stacks)
    else:
        o, qx, _ = _attn_fwd(q_aug, k_aug, vt_aug)
    w_out_b, w_ple_b, w_gate_b, conv_w = late
    ycat, xc, h = _branches_fwd(o, g_attn, x_lru, g_lru, gain_a, gain_l, conv_w, conv_b, w_r, b_rgate, w_i, b_igate,
                                lam)
    dh1, dycat, dmix, h1b, dgp, pb, dpe, acc_t = _tail(ycat, x, p, tgt, w_out_b, post_gain, w_ple_b, ple_gain,
                                                       w_gate_b, b_gate)
    late_grads = [_matmul_tn(ycat, dmix, "dw_out"), _matmul_tn(pb, dpe, "dw_ple"),
                  _matmul_tn(h1b, dgp, "dw_ple_gate")]
    do_aug, dg_attn, dg_lru, dh, acc_b = _branches_bwd(dycat, o, g_attn, h, g_lru, gain_a, gain_l)
    dx_lru, gw_r, gw_i, acc_l = _lru_bwd(dh, h, xc, x_lru, conv_w, w_r, b_rgate, w_i, b_igate, lam)
    if early_reduce is None:
        dq, dk, dv, dc_key, dc_query, _ = _attn_bwd(q_aug, qx, k_aug, v_aug, do_aug)
    else:
        sent = early_reduce(late_grads)
        dq, dk, dv, dc_key, dc_query, received = _attn_bwd(q_aug, qx, k_aug, v_aug, do_aug, sent)
        late_grads = list(zip(sent, received))
    dfl, acc_f = _fgate_bwd(dc_key, dc_query, flb)
    dz = (dq, dk, dv, dg_attn, dx_lru, dg_lru)
    grad_t = _dw_in_t(dz, dfl, xn)
    if w_in_reduce is None:
        grad_x, acc_x, _ = _dx(dz, dfl, w_a, w_f, w_b, x, pre_gain, dh1)
    else:
        sent = w_in_reduce(grad_t)
        grad_x, acc_x, (received,) = _dx(dz, dfl, w_a, w_f, w_b, x, pre_gain, dh1, [sent])
        grad_t = (sent, received)

    grads = dict(
        w_in_t=grad_t,
        w_out=late_grads[0],
        w_ple=late_grads[1],
        w_ple_gate=late_grads[2],
        w_rgate=gw_r,
        w_igate=gw_i,
        b_f=acc_f[0:1, :H],
        pre_gain=acc_x[0:1],
        post_gain=acc_t[0:1],
        conv_w=acc_l[0:4],
        conv_b=acc_l[4:5],
        b_rgate=acc_l[5:6],
        b_igate=acc_l[6:7],
        lru_lambda=acc_l[7:8],
        attn_out_gain=acc_b[0:1],
        lru_out_gain=acc_b[1:2],
        ple_gain=acc_t[1:2],
        b_ple_gate=acc_t[2:3],
    )
    loss = jnp.sum(acc_t[3])
    return loss, grad_x, grads


SMALL_ROWS = ["b_f", "pre_gain", "post_gain", "conv_w", "conv_b", "b_rgate", "b_igate", "lru_lambda",
              "attn_out_gain", "lru_out_gain", "ple_gain", "b_ple_gate"]
WEIGHTS = ["w_in", "b_f", "pre_gain", "post_gain", "conv_w", "conv_b", "w_rgate", "b_rgate", "w_igate", "b_igate",
           "lru_lambda", "attn_out_gain", "lru_out_gain", "w_out", "w_ple", "ple_gain", "w_ple_gate", "b_ple_gate"]
SHARDED = ["w_in", "w_out", "w_ple", "w_ple_gate"]


def _by_chip_cols(g):
    r, cols = g.shape
    return g.reshape(r, N_CHIPS, cols // N_CHIPS).transpose(1, 0, 2)


def _from_chip_cols(s):
    n, r, cols = s.shape
    return s.transpose(1, 0, 2).reshape(r, n * cols)


def kernel(x, p, w_in, b_f, pre_gain, post_gain, conv_w, conv_b, w_rgate, b_rgate, w_igate, b_igate, lru_lambda, attn_out_gain, lru_out_gain, w_out, w_ple, ple_gain, w_ple_gate, b_ple_gate, loss_target, m_w_in, m_b_f, m_pre_gain, m_post_gain, m_conv_w, m_conv_b, m_w_rgate, m_b_rgate, m_w_igate, m_b_igate, m_lru_lambda, m_attn_out_gain, m_lru_out_gain, m_w_out, m_w_ple, m_ple_gain, m_w_ple_gate, m_b_ple_gate, v_w_in, v_b_f, v_pre_gain, v_post_gain, v_conv_w, v_conv_b, v_w_rgate, v_b_rgate, v_w_igate, v_b_igate, v_lru_lambda, v_attn_out_gain, v_lru_out_gain, v_w_out, v_w_ple, v_ple_gain, v_w_ple_gate, v_b_ple_gate):
    w = dict(w_in=w_in, b_f=b_f, pre_gain=pre_gain, post_gain=post_gain, conv_w=conv_w, conv_b=conv_b,
             w_rgate=w_rgate, b_rgate=b_rgate, w_igate=w_igate, b_igate=b_igate, lru_lambda=lru_lambda,
             attn_out_gain=attn_out_gain, lru_out_gain=lru_out_gain, w_out=w_out, w_ple=w_ple, ple_gain=ple_gain,
             w_ple_gate=w_ple_gate, b_ple_gate=b_ple_gate)
    m = dict(w_in=m_w_in, b_f=m_b_f, pre_gain=m_pre_gain, post_gain=m_post_gain, conv_w=m_conv_w, conv_b=m_conv_b,
             w_rgate=m_w_rgate, b_rgate=m_b_rgate, w_igate=m_w_igate, b_igate=m_b_igate, lru_lambda=m_lru_lambda,
             attn_out_gain=m_attn_out_gain, lru_out_gain=m_lru_out_gain, w_out=m_w_out, w_ple=m_w_ple,
             ple_gain=m_ple_gain, w_ple_gate=m_w_ple_gate, b_ple_gate=m_b_ple_gate)
    v = dict(w_in=v_w_in, b_f=v_b_f, pre_gain=v_pre_gain, post_gain=v_post_gain, conv_w=v_conv_w, conv_b=v_conv_b,
             w_rgate=v_w_rgate, b_rgate=v_b_rgate, w_igate=v_w_igate, b_igate=v_b_igate, lru_lambda=v_lru_lambda,
             attn_out_gain=v_attn_out_gain, lru_out_gain=v_lru_out_gain, w_out=v_w_out, w_ple=v_w_ple,
             ple_gain=v_ple_gain, w_ple_gate=v_w_ple_gate, b_ple_gate=v_b_ple_gate)
    xi, yi, ci = _position()
    chip = 2 * xi + yi

    w_in_t, m_in_t, v_in_t = (jnp.swapaxes(t[0], 0, 1) for t in (w_in, m_w_in, v_w_in))
    window = jnp.pad(w_in_t.astype(BF16), ((0, W_ROWS - SHARD_ROWS), (0, 0)))

    (st_in,) = _gather_shards([window], [])
    w_a, w_f, w_b = _assemble_w_in(st_in)
    late_shards = (w_out[0].astype(BF16), w_ple[0].astype(BF16), w_ple_gate[0].astype(BF16), conv_w[0])

    def early_reduce(local):
        parts = [local[0].reshape(N_CHIPS, DMIX // N_CHIPS, D), _by_chip_cols(local[1]),
                 local[2].reshape(N_CHIPS, D // N_CHIPS, D)]
        return _pair_sum(parts, _pair_exchange(parts), ci)

    loss, grad_x, g = _local_step(
        x[0], p[0, 0], loss_target[0], w_a, w_f, w_b, late_shards, b_f, pre_gain, post_gain,
        conv_b, w_rgate[0], b_rgate, w_igate[0], b_igate, lru_lambda, attn_out_gain, lru_out_gain, ple_gain,
        b_ple_gate, gather_late=True, early_reduce=early_reduce,
        w_in_reduce=lambda grad_t: _pair_sum_windows(grad_t, _pair_exchange_windows(grad_t), ci))

    sums = [g["w_in_t"][0]] + [g[n][0] for n in SHARDED[1:]]
    recv = [g["w_in_t"][1]] + [g[n][1] for n in SHARDED[1:]]
    halves = [_chip_sum(sums[a], recv[a], chip, "chip_sum_%d" % a) for a in range(4)]
    theirs = _pair_swap(halves)
    full = [jnp.concatenate([jnp.where(ci == 0, a, b), jnp.where(ci == 0, b, a)], axis=0)
            for a, b in zip(halves, theirs)]
    red = dict(zip(SHARDED, full))
    red["w_in"] = lax.dynamic_slice_in_dim(red["w_in"], 2 * chip, SHARD_ROWS, axis=0)

    rows = [jnp.pad(g["b_f"], ((0, 0), (0, D - H)))] + [g[n] for n in SMALL_ROWS[1:]]
    rows.append(jnp.pad(loss.reshape(1, 1), ((0, 0), (0, D - 1))))
    packed = jnp.concatenate([g["w_rgate"].reshape(NB * LANES, LANES), g["w_igate"].reshape(NB * LANES, LANES),
                              jnp.concatenate(rows, axis=0).reshape(LANES, LANES)], axis=0)
    summed = _allreduce_small(packed)
    red["w_rgate"] = summed[:D].reshape(1, NB, LANES, LANES)
    red["w_igate"] = summed[D:2 * D].reshape(1, NB, LANES, LANES)
    vec = summed[2 * D:].reshape(16, D)
    loss = vec[15, 0]
    r0 = 0
    for n in SMALL_ROWS:
        nr = 4 if n == "conv_w" else 1
        red[n] = vec[r0:r0 + nr]
        r0 += nr
    red["b_f"] = red["b_f"][:, :H]
    red["conv_w"] = lax.dynamic_slice_in_dim(red["conv_w"], chip * (D // N_CHIPS), D // N_CHIPS, axis=1)[None]

    delta, new_m, new_v = {}, {}, {}
    outs_in = _adamw_big(red["w_in"], w_in_t, m_in_t, v_in_t, "adamw_w_in")
    delta["w_in"], new_m["w_in"], new_v["w_in"] = (jnp.swapaxes(t, 0, 1)[None] for t in outs_in)
    red["w_in"] = jnp.swapaxes(red["w_in"], 0, 1)[None]
    for n in SHARDED[1:]:
        delta[n], new_m[n], new_v[n] = (t[None] for t in _adamw_big(red[n], w[n][0], m[n][0], v[n][0], "adamw_" + n))
        red[n] = red[n][None]
    small = [n for n in WEIGHTS if n not in SHARDED]
    outs = _adamw_small([red[n] for n in small], [w[n] for n in small], [m[n] for n in small],
                        [v[n] for n in small])
    ns = len(small)
    for a, n in enumerate(small):
        delta[n], new_m[n], new_v[n] = outs[a], outs[ns + a], outs[2 * ns + a]

    return (loss, grad_x[None], *[red[n] for n in WEIGHTS], *[delta[n] for n in WEIGHTS],
            *[new_m[n] for n in WEIGHTS], *[new_v[n] for n in WEIGHTS])
```

```python
import functools

import jax
import jax.numpy as jnp
import numpy as np
from jax import lax
from jax.experimental import pallas as pl
from jax.experimental.pallas import tpu as pltpu

F32 = jnp.float32
BF16 = jnp.bfloat16

D = 1024
H = 8
DH = 128
NB = 8
DPLE = 256
DMIX = 2 * D
D_IN = 4 * D + H + 2 * D
FL0 = 3 * D
RMS_EPS = 1e-6
LRU_C = 8.0
NEG = -1e30
LANES = 128
SUBLANES = 8

ADAM_LR = 0.001
ADAM_B1 = 0.9
ADAM_B2 = 0.999
ADAM_EPS = 1e-08
ADAM_WD = 0.01
ADAM_STEP = 10

TM = 256
TM_WIDE = 512
TA = 512
FWD_HEADS = 4
BWD_HEADS = 2
VMEM_BIG = 56 * 1024 * 1024
VMEM_MID = 40 * 1024 * 1024

MESH = pl.DeviceIdType.MESH
N_CHIPS = 4
N_DEV = 8


def _call(body, *, out_shape, in_hbm=True, **kwargs):
    if not in_hbm:
        return pl.pallas_call(body, out_shape=out_shape, **kwargs)

    def pin(shape):
        return pltpu.HBM(shape.shape, shape.dtype) if isinstance(shape, jax.ShapeDtypeStruct) else shape

    fn = pl.pallas_call(body, out_shape=jax.tree.map(pin, out_shape), **kwargs)

    def run(*args):
        return fn(*[a if a.dtype == jnp.int32 else pltpu.with_memory_space_constraint(a, pltpu.HBM) for a in args])

    return run


def _cparams(sem, vmem=VMEM_MID):
    return pltpu.CompilerParams(dimension_semantics=sem, vmem_limit_bytes=vmem)


def _sigmoid(x):
    return 0.5 * jnp.tanh(0.5 * x) + 0.5


def _rstd(x):
    return lax.rsqrt(jnp.mean(x * x, axis=-1, keepdims=True) + RMS_EPS)


def _rms_bwd(t, xhat, rstd):
    return rstd * (t - xhat * jnp.mean(t * xhat, axis=-1, keepdims=True))


def _dot(a, b):
    return jnp.dot(a, b, preferred_element_type=F32)


def _dot_nt(a, b):
    return lax.dot_general(a, b, (((1,), (1,)), ((), ())), preferred_element_type=F32)


def _dot_tn(a, b):
    return lax.dot_general(a, b, (((0,), (0,)), ((), ())), preferred_element_type=F32)


def _dot_exact(a, b):
    return jnp.dot(a, b, preferred_element_type=F32, precision=lax.Precision.HIGHEST)


def _shift_down(x, j, halo):
    rolled = pltpu.roll(x, j, 0)
    row = lax.broadcasted_iota(jnp.int32, halo.shape, 0)
    top = jnp.where(row < j, pltpu.roll(halo, j, 0), rolled[:SUBLANES])
    return jnp.concatenate([top, rolled[SUBLANES:]], axis=0)


def _shift_up(x, j, nxt):
    tm = x.shape[0]
    rolled = pltpu.roll(x, tm - j, 0)
    row = lax.broadcasted_iota(jnp.int32, nxt.shape, 0)
    bot = jnp.where(row >= SUBLANES - j, pltpu.roll(nxt, SUBLANES - j, 0), rolled[tm - SUBLANES:])
    return jnp.concatenate([rolled[:tm - SUBLANES], bot], axis=0)


def _scan_fwd_into(a, u, carry, h_ref):
    tm = a.shape[0]
    sub = lax.broadcasted_iota(jnp.int32, a.shape, 0) & (SUBLANES - 1)
    d = 1
    while d < SUBLANES:
        keep = sub >= d
        a_s = jnp.where(keep, pltpu.roll(a, d, 0), 1.0)
        u_s = jnp.where(keep, pltpu.roll(u, d, 0), 0.0)
        u = u + a * u_s
        a = a * a_s
        d *= 2
    for g in range(tm // SUBLANES):
        rows = slice(g * SUBLANES, (g + 1) * SUBLANES)
        h_ref[rows, :] = u[rows] + a[rows] * carry
        carry = h_ref[(g + 1) * SUBLANES - 1:(g + 1) * SUBLANES, :]
    return carry


def _scan_bwd_into(b, u, g_ref):
    tm = b.shape[0]
    sub = lax.broadcasted_iota(jnp.int32, b.shape, 0) & (SUBLANES - 1)
    d = 1
    while d < SUBLANES:
        keep = sub < SUBLANES - d
        b_s = jnp.where(keep, pltpu.roll(b, tm - d, 0), 1.0)
        u_s = jnp.where(keep, pltpu.roll(u, tm - d, 0), 0.0)
        u = u + b * u_s
        b = b * b_s
        d *= 2
    nxt = jnp.zeros((1, b.shape[1]), F32)
    for g in reversed(range(tm // SUBLANES)):
        rows = slice(g * SUBLANES, (g + 1) * SUBLANES)
        g_ref[rows, :] = u[rows] + b[rows] * nxt
        nxt = g_ref[g * SUBLANES:g * SUBLANES + 1, :]


def _gate_pre(xc, w_ref):
    outs = []
    for n in range(NB):
        outs.append(_dot(xc[:, n * LANES:(n + 1) * LANES].astype(BF16), w_ref[n]))
    return jnp.concatenate(outs, axis=1)


def _gate_pre_t(d, w_ref):
    outs = []
    for n in range(NB):
        outs.append(_dot_nt(d[:, n * LANES:(n + 1) * LANES].astype(BF16), w_ref[n]))
    return jnp.concatenate(outs, axis=1)


def _softplus_neg(lam):
    return jnp.maximum(-lam, 0.0) + jnp.log(1.0 + jnp.exp(-jnp.abs(lam)))


def _row_spec(tm, width):
    return pl.BlockSpec((tm, width), lambda i: (i, 0))


def _const_spec(shape):
    nd = len(shape)
    return pl.BlockSpec(shape, lambda *_: (0,) * nd)


def _weight_spec(shape):
    nd = len(shape)
    return pl.BlockSpec(shape, lambda *_: (0,) * nd, pipeline_mode=pl.Buffered(1))


AUG = 2 * DH
LOG2E = 1.4426950408889634
LN2 = 0.6931471805599453
Q_SCALE = DH ** -0.5 * LOG2E


def _split3(x):
    hi = x.astype(BF16)
    r1 = x - hi.astype(F32)
    mid = r1.astype(BF16)
    lo = (r1 - mid.astype(F32)).astype(BF16)
    return hi, mid, lo


def _extras(col, ones_from):
    t = col.shape[0]
    hi, mid, lo = _split3(jnp.broadcast_to(col, (t, LANES)))
    lane = lax.broadcasted_iota(jnp.int32, (t, LANES), 1)
    rest = jnp.zeros((t, LANES), BF16)
    if ones_from is not None:
        rest = jnp.where((lane >= ones_from) & (lane < ones_from + 3), 1.0, 0.0).astype(BF16)
    return jnp.where(lane == 0, hi, jnp.where(lane == 1, mid, jnp.where(lane == 2, lo, rest)))


def _selectors():
    sel_q = np.zeros((3 * LANES, H * LANES), np.float32)
    sel_k = np.zeros((3 * LANES, H * LANES), np.float32)
    for hd in range(H):
        for piece in range(3):
            sel_q[piece * LANES + hd, hd * LANES + piece] = 1.0
            sel_k[piece * LANES + hd, hd * LANES + 3 + piece] = -1.0
    return jnp.asarray(sel_q, BF16), jnp.asarray(sel_k, BF16)


def _in_proj(x, pre_gain, w_a, w_f, w_b, b_f_pad):
    T = x.shape[0]
    tm = TM
    sel_q, sel_k = _selectors()

    def body(x_ref, g_ref, wa_ref, wf_ref, wb_ref, bf_ref, sq_ref, sk_ref,
             xn_ref, qa_ref, ka_ref, va_ref, ga_ref, xl_ref, gl_ref, flb_ref, vt_ref, c_s, carry):
        @pl.when(pl.program_id(0) == 0)
        def _():
            carry[...] = jnp.zeros_like(carry)

        xv = x_ref[...]
        xn = (xv * _rstd(xv) * g_ref[...]).astype(BF16)
        xn_ref[...] = xn
        for s, o_ref in enumerate((ga_ref, xl_ref, gl_ref)):
            o_ref[...] = _dot_nt(xn, wb_ref[s * D:(s + 1) * D, :]).astype(o_ref.dtype)
        flb = _dot_nt(xn, wf_ref[...]) + bf_ref[...]
        flb_ref[...] = flb
        lane = lax.broadcasted_iota(jnp.int32, flb.shape, 1)
        ls = jnp.where(lane < H, jnp.minimum(flb, 0.0) - jnp.log(1.0 + jnp.exp(-jnp.abs(flb))), 0.0)
        r = lax.broadcasted_iota(jnp.int32, (tm, tm), 0)
        c = lax.broadcasted_iota(jnp.int32, (tm, tm), 1)
        cs = _dot_exact((c <= r).astype(F32), ls) + carry[...]
        c_s[...] = cs
        carry[...] = c_s[tm - 1:tm, :]

        pieces = jnp.concatenate(_split3(cs * LOG2E), axis=1)
        ones_q = jnp.where((lane >= 3) & (lane < 6), 1.0, 0.0)
        ones_k = jnp.where(lane < 3, 1.0, 0.0)
        zq = _dot_nt(xn, wa_ref[0:D, :]) * Q_SCALE
        zk = _dot_nt(xn, wa_ref[D:2 * D, :])
        zv = _dot_nt(xn, wa_ref[2 * D:3 * D, :])
        ex_q = _dot(pieces, sq_ref[...])
        ex_k = _dot(pieces, sk_ref[...])
        for hd in range(H):
            head = slice(hd * DH, (hd + 1) * DH)
            lo, hi = hd * AUG, hd * AUG + DH
            qa_ref[:, lo:hi] = zq[:, head].astype(BF16)
            qa_ref[:, hi:hi + DH] = (ex_q[:, head] + ones_q).astype(BF16)
            ka_ref[:, lo:hi] = zk[:, head].astype(BF16)
            ka_ref[:, hi:hi + DH] = (ex_k[:, head] + ones_k).astype(BF16)
            va_ref[:, lo:hi] = zv[:, head].astype(BF16)
            va_ref[:, hi:hi + DH] = ones_k.astype(BF16)
            vt_ref[lo:hi, :] = jnp.transpose(zv[:, head]).astype(BF16)
            vt_ref[hi:hi + DH, :] = jnp.where(lax.broadcasted_iota(jnp.int32, (DH, tm), 0) < 3, 1.0, 0.0).astype(BF16)

    bf = jax.ShapeDtypeStruct((T, D), BF16)
    aug = jax.ShapeDtypeStruct((T, H * AUG), BF16)
    f32 = jax.ShapeDtypeStruct((T, D), F32)
    sel_spec = _const_spec((3 * LANES, H * LANES))
    return _call(
        body, name="in_proj", grid=(T // tm,),
        in_specs=[_row_spec(tm, D), _const_spec((1, D)), _const_spec((3 * D, D)), _const_spec((LANES, D)),
                  _const_spec((3 * D, D)), _const_spec((1, LANES)), sel_spec, sel_spec],
        out_specs=[_row_spec(tm, D)] + [_row_spec(tm, H * AUG)] * 3 + [_row_spec(tm, D)] * 3 + [_row_spec(tm, LANES)]
        + [pl.BlockSpec((H * AUG, tm), lambda i: (0, i))],
        out_shape=[bf, aug, aug, aug, f32, f32, f32, jax.ShapeDtypeStruct((T, LANES), F32),
                   jax.ShapeDtypeStruct((H * AUG, T), BF16)],
        scratch_shapes=[pltpu.VMEM((tm, LANES), F32), pltpu.VMEM((1, LANES), F32)],
        compiler_params=_cparams(("arbitrary",), VMEM_BIG),
    )(x, pre_gain, w_a, w_f, w_b, b_f_pad, sel_q, sel_k)


def _causal_pairs(n, q_major):
    if q_major:
        pairs = [(qi, ki) for qi in range(n) for ki in range(qi + 1)]
    else:
        pairs = [(ki, qi) for ki in range(n) for qi in range(ki, n)]
    return (jnp.asarray([a for a, _ in pairs], jnp.int32), jnp.asarray([b for _, b in pairs], jnp.int32))


def _attn_fwd(q_aug, k_aug, vt_aug, shards=(), whole=()):
    T = q_aug.shape[0]
    t = TA
    n = T // t
    hp = FWD_HEADS
    heads = range(hp)
    qi_tab, ki_tab = _causal_pairs(n, q_major=True)
    na, nall = len(shards), len(shards) + len(whole)
    n_h, n_j = H // hp, qi_tab.shape[0]

    def body(qi_ref, ki_ref, q_ref, k_ref, vt_ref, *rest):
        srcs, rest = rest[:nall], rest[nall:]
        o_ref, qx_ref = rest[:2]
        dsts, rest = rest[2:2 + nall], rest[2 + nall:]
        m_s, acc_s = rest[:2]
        h = pl.program_id(0)
        j = pl.program_id(1)
        qi = qi_ref[j]
        ki = ki_ref[j]

        if nall:
            gather = _GatherPlan(srcs, dsts, rest[2:], na)
            pl.when((h == 0) & (j == 0))(gather.send)
            pl.when((h == n_h - 1) & (j == 0))(gather.forward)
            pl.when((h == n_h - 1) & (j == n_j - 1))(gather.finish)

        @pl.when(ki == 0)
        def _():
            m_s[...] = jnp.full(m_s.shape, NEG, F32)
            acc_s[...] = jnp.zeros_like(acc_s)

        def step(on_diagonal):
            cols = [slice(a * AUG, (a + 1) * AUG) for a in heads]
            if on_diagonal:
                krow = lax.broadcasted_iota(jnp.int32, (t, t), 0)
                qcol = lax.broadcasted_iota(jnp.int32, (t, t), 1)
            def logits(a):
                st = _dot_nt(k_ref[:, cols[a]], q_ref[:, cols[a]])
                return jnp.where(krow <= qcol, st, NEG) if on_diagonal else st

            st_next = logits(0)
            for a in heads:
                st = st_next
                if a + 1 < hp:
                    st_next = logits(a + 1)
                m_prev = m_s[a]
                m_new = jnp.maximum(m_prev, jnp.max(st, axis=0, keepdims=True))
                pt = jnp.exp2(st - m_new).astype(BF16)
                acc_s[a] = jnp.exp2(m_prev - m_new) * acc_s[a] + _dot(vt_ref[cols[a], :], pt)
                m_s[a] = m_new

        @pl.when(ki < qi)
        def _():
            step(False)

        @pl.when(ki == qi)
        def _():
            step(True)
            piece = lax.broadcasted_iota(jnp.int32, (DH, t), 0)
            for a in heads:
                l = acc_s[a, DH:DH + 1, :]
                ex = jnp.transpose(q_ref[:, a * AUG + DH:(a + 1) * AUG].astype(F32))
                c2 = jnp.sum(jnp.where(piece < 3, ex, 0.0), axis=0, keepdims=True)
                hi, mid, lo = _split3(jnp.broadcast_to(c2 - (m_s[a] + jnp.log(l) * LOG2E), (DH, t)))
                ones = jnp.where((piece >= 3) & (piece < 6), 1.0, 0.0).astype(BF16)
                ex_t = jnp.where(piece == 0, hi, jnp.where(piece == 1, mid, jnp.where(piece == 2, lo, ones)))
                o_ref[:, a * DH:(a + 1) * DH] = jnp.transpose(acc_s[a, :DH, :] / l)
                qx_ref[:, a * DH:(a + 1) * DH] = jnp.transpose(ex_t.astype(F32)).astype(BF16)

    q_spec = pl.BlockSpec((t, hp * AUG), lambda h, j, qi_ref, ki_ref: (qi_ref[j], h))
    k_spec = pl.BlockSpec((t, hp * AUG), lambda h, j, qi_ref, ki_ref: (ki_ref[j], h))
    vt_spec = pl.BlockSpec((hp * AUG, t), lambda h, j, qi_ref, ki_ref: (h, ki_ref[j]))
    out_spec = pl.BlockSpec((t, hp * DH), lambda h, j, qi_ref, ki_ref: (qi_ref[j], h))
    arrs = list(shards) + list(whole)
    grid_spec = pltpu.PrefetchScalarGridSpec(
        num_scalar_prefetch=2, grid=(n_h, n_j),
        in_specs=[q_spec, k_spec, vt_spec] + [HBM_SPEC] * nall, out_specs=[out_spec, out_spec] + [HBM_SPEC] * nall,
        scratch_shapes=[pltpu.VMEM((hp, 1, t), F32), pltpu.VMEM((hp, AUG, t), F32)]
        + (_gather_semaphores(na, nall) if nall else []))
    outs = _call(
        body, name="attn_fwd", grid_spec=grid_spec,
        out_shape=[jax.ShapeDtypeStruct((T, D), F32), jax.ShapeDtypeStruct((T, D), BF16)] + _gather_out_shapes(arrs),
        compiler_params=_cparams(("arbitrary", "arbitrary"), VMEM_BIG),
    )(qi_tab, ki_tab, q_aug, k_aug, vt_aug, *arrs)
    return outs[0], outs[1], _place_own(outs[2:], arrs)


def _lru_gates(xc, wr_ref, br_ref, wi_ref, bi_ref, lam_ref):
    r = _sigmoid(_gate_pre(xc, wr_ref) + br_ref[...])
    ig = _sigmoid(_gate_pre(xc, wi_ref) + bi_ref[...])
    sp = _softplus_neg(lam_ref[...])
    la = (-LRU_C) * r * sp
    a = jnp.exp(la)
    y = -jnp.tanh(la) * (a * a + 1.0)
    return r, ig, sp, a, jnp.sqrt(y), lax.rsqrt(y)


def _branches_fwd(o, g_attn, x_lru, g_lru, gain_a, gain_l, conv_w, conv_b, w_r, b_r, w_i, b_i, lam):
    T = o.shape[0]
    tm = TM

    def body(o_ref, ga_ref, xl_ref, gl_ref, gna_ref, gnl_ref, cw_ref, cb_ref, wr_ref, br_ref, wi_ref, bi_ref,
             lam_ref, ycat_ref, xc_ref, h_ref, halo_s, hc_s):
        @pl.when(pl.program_id(0) == 0)
        def _():
            halo_s[...] = jnp.zeros_like(halo_s)
            hc_s[...] = jnp.zeros_like(hc_s)

        ov = o_ref[...]
        ga = ga_ref[...]
        ya = ov * _rstd(ov) * gna_ref[...] * (ga * _sigmoid(ga))
        ycat_ref[:, :D] = ya.astype(BF16)

        xl = xl_ref[...]
        halo = halo_s[...]
        xc = xl * cw_ref[3:4, :] + cb_ref[...]
        for j in range(3):
            xc = xc + _shift_down(xl, 3 - j, halo) * cw_ref[j:j + 1, :]
        halo_s[...] = xl_ref[tm - SUBLANES:tm, :]
        xc_ref[...] = xc

        _, ig, _, a, sq, _ = _lru_gates(xc, wr_ref, br_ref, wi_ref, bi_ref, lam_ref)
        u = sq * (ig * xc)
        hc_s[...] = _scan_fwd_into(a, u, hc_s[...], h_ref)
        hh = h_ref[...]

        gl = gl_ref[...]
        yl = hh * _rstd(hh) * gnl_ref[...] * (gl * _sigmoid(gl))
        ycat_ref[:, D:] = yl.astype(BF16)

    vec = _const_spec((1, D))
    wspec = _const_spec((NB, LANES, LANES))
    return _call(
        body, name="branches_fwd", grid=(T // tm,),
        in_specs=[_row_spec(tm, D)] * 4 + [vec, vec, _const_spec((4, D)), vec, wspec, vec, wspec, vec, vec],
        out_specs=[_row_spec(tm, DMIX), _row_spec(tm, D), _row_spec(tm, D)],
        out_shape=[jax.ShapeDtypeStruct((T, DMIX), BF16), jax.ShapeDtypeStruct((T, D), F32),
                   jax.ShapeDtypeStruct((T, D), F32)],
        scratch_shapes=[pltpu.VMEM((SUBLANES, D), F32), pltpu.VMEM((1, D), F32)],
        compiler_params=_cparams(("arbitrary",)),
    )(o, g_attn, x_lru, g_lru, gain_a, gain_l, conv_w, conv_b, w_r, b_r, w_i, b_i, lam)


def _tail(ycat, x, p, tgt, w_out, post_gain, w_ple, ple_gain, w_gate, b_gate):
    T = x.shape[0]
    tm = TM

    def body(ycat_ref, x_ref, p_ref, t_ref, wo_ref, pg_ref, wp_ref, eg_ref, wg_ref, bg_ref,
             dh1_ref, dycat_ref, dmix_ref, h1b_ref, dgp_ref, pb_ref, dpe_ref, acc_ref):
        @pl.when(pl.program_id(0) == 0)
        def _():
            acc_ref[...] = jnp.zeros_like(acc_ref)

        mix = _dot(ycat_ref[...], wo_ref[...])
        rstd_m = _rstd(mix)
        mhat = mix * rstd_m
        h1 = x_ref[...] + mhat * pg_ref[...]
        pb = p_ref[...].astype(BF16)
        pb_ref[...] = pb
        pe = _dot(pb, wp_ref[...])
        rstd_p = _rstd(pe)
        pehat = pe * rstd_p
        e = pehat * eg_ref[...]
        h1b = h1.astype(BF16)
        h1b_ref[...] = h1b
        gate = _sigmoid(_dot(h1b, wg_ref[...]) + bg_ref[...])
        diff = (h1 + gate * e) - t_ref[...]

        dy = diff * (1.0 / D)
        de = dy * gate
        dgp = (dy * e) * gate * (1.0 - gate)
        dgpb = dgp.astype(BF16)
        dgp_ref[...] = dgpb
        dh1 = dy + _dot_nt(dgpb, wg_ref[...])
        dh1_ref[...] = dh1
        dpe_ref[...] = _rms_bwd(de * eg_ref[...], pehat, rstd_p).astype(BF16)
        dmix = _rms_bwd(dh1 * pg_ref[...], mhat, rstd_m).astype(BF16)
        dmix_ref[...] = dmix
        dycat_ref[...] = _dot_nt(dmix, wo_ref[...])

        acc_ref[0:1, :] += jnp.sum(dh1 * mhat, axis=0, keepdims=True)
        acc_ref[1:2, :] += jnp.sum(de * pehat, axis=0, keepdims=True)
        acc_ref[2:3, :] += jnp.sum(dgp, axis=0, keepdims=True)
        acc_ref[3:4, :] += jnp.sum(diff * diff, axis=0, keepdims=True) * (0.5 / D)

    vec = _const_spec((1, D))
    bf = jax.ShapeDtypeStruct((T, D), BF16)
    return _call(
        body, name="tail", grid=(T // tm,),
        in_specs=[_row_spec(tm, DMIX), _row_spec(tm, D), _row_spec(tm, DPLE), _row_spec(tm, D),
                  _const_spec((DMIX, D)), vec, _const_spec((DPLE, D)), vec, _const_spec((D, D)), vec],
        out_specs=[_row_spec(tm, D), _row_spec(tm, DMIX), _row_spec(tm, D), _row_spec(tm, D), _row_spec(tm, D),
                   _row_spec(tm, DPLE), _row_spec(tm, D), _const_spec((SUBLANES, D))],
        out_shape=[jax.ShapeDtypeStruct((T, D), F32), jax.ShapeDtypeStruct((T, DMIX), F32), bf, bf, bf,
                   jax.ShapeDtypeStruct((T, DPLE), BF16), bf, jax.ShapeDtypeStruct((SUBLANES, D), F32)],
        compiler_params=_cparams(("arbitrary",), VMEM_BIG),
    )(ycat, x, p, tgt, w_out, post_gain, w_ple, ple_gain, w_gate, b_gate)


def _branches_bwd(dycat, o, g_attn, h, g_lru, gain_a, gain_l):
    T = o.shape[0]
    tm = TM

    def body(dy_ref, o_ref, ga_ref, h_ref, gl_ref, gna_ref, gnl_ref,
             do_ref, dga_ref, dgl_ref, dh_ref, acc_ref):
        @pl.when(pl.program_id(0) == 0)
        def _():
            acc_ref[...] = jnp.zeros_like(acc_ref)

        def branch(val, g, gain, dyv):
            rstd = _rstd(val)
            vhat = val * rstd
            sig = _sigmoid(g)
            dn = dyv * (g * sig)
            dg = dyv * (vhat * gain) * (sig * (1.0 + g * (1.0 - sig)))
            dgain = jnp.sum(dn * vhat, axis=0, keepdims=True)
            return _rms_bwd(dn * gain, vhat, rstd), dg, dgain

        ov = o_ref[...]
        do, dga, dgain_a = branch(ov, ga_ref[...], gna_ref[...], dy_ref[:, :D])
        dga_ref[...] = dga.astype(BF16)
        prod = do * ov
        for hd in range(H):
            head = slice(hd * DH, (hd + 1) * DH)
            do_ref[:, hd * AUG:hd * AUG + DH] = do[:, head].astype(BF16)
            do_ref[:, hd * AUG + DH:(hd + 1) * AUG] = _extras(-jnp.sum(prod[:, head], axis=1, keepdims=True), None)

        dh, dgl, dgain_l = branch(h_ref[...], gl_ref[...], gnl_ref[...], dy_ref[:, D:])
        dh_ref[...] = dh
        dgl_ref[...] = dgl.astype(BF16)
        acc_ref[0:1, :] += dgain_a
        acc_ref[1:2, :] += dgain_l

    vec = _const_spec((1, D))
    bf = jax.ShapeDtypeStruct((T, D), BF16)
    return _call(
        body, name="branches_bwd", grid=(T // tm,),
        in_specs=[_row_spec(tm, DMIX)] + [_row_spec(tm, D)] * 4 + [vec, vec],
        out_specs=[_row_spec(tm, H * AUG), _row_spec(tm, D), _row_spec(tm, D), _row_spec(tm, D),
                   _const_spec((SUBLANES, D))],
        out_shape=[jax.ShapeDtypeStruct((T, H * AUG), BF16), bf, bf, jax.ShapeDtypeStruct((T, D), F32),
                   jax.ShapeDtypeStruct((SUBLANES, D), F32)],
        compiler_params=_cparams(("arbitrary",)),
    )(dycat, o, g_attn, h, g_lru, gain_a, gain_l)


def _lru_bwd(dh, h, xc, x_lru, conv_w, w_r, b_r, w_i, b_i, lam):
    T = dh.shape[0]
    tm = TM
    nt = T // tm
    per = tm // SUBLANES

    def body(dh_ref, h_ref, hprev_ref, xc_ref, xl_ref, cw_ref, wr_ref, br_ref, wi_ref, bi_ref, lam_ref,
             dxl_ref, dwr_ref, dwi_ref, acc_ref, carry_s, dxc_next_s, top_s, dht_s):
        i = pl.program_id(0)

        @pl.when(i == 0)
        def _():
            acc_ref[...] = jnp.zeros_like(acc_ref)
            dwr_ref[...] = jnp.zeros_like(dwr_ref)
            dwi_ref[...] = jnp.zeros_like(dwi_ref)
            carry_s[...] = jnp.zeros_like(carry_s)
            dxc_next_s[...] = jnp.zeros_like(dxc_next_s)

        inner = jnp.where(i == nt - 1, 0.0, 1.0)
        xc = xc_ref[...]
        r, ig, sp, a, sq, inv_sq = _lru_gates(xc, wr_ref, br_ref, wi_ref, bi_ref, lam_ref)

        row = lax.broadcasted_iota(jnp.int32, (tm, D), 0)
        u = dh_ref[...] + jnp.where(row == tm - 1, carry_s[...], 0.0)
        _scan_bwd_into(pltpu.roll(a, tm - 1, 0), u, dht_s)
        dht = dht_s[...]
        top_s[...] = a[:SUBLANES, :] * dht[:SUBLANES, :]
        carry_s[...] = top_s[0:1, :]

        hprev = hprev_ref[...] * inner
        da = dht * _shift_down(h_ref[...], 1, hprev)
        dig = dht * sq * xc
        dxc = dht * sq * ig
        dsq = dht * ig * xc
        dla = da * a - dsq * (a * a) * inv_sq
        dr = dla * ((-LRU_C) * sp)
        dpr = dr * r * (1.0 - r)
        dpi = dig * ig * (1.0 - ig)
        for n in range(NB):
            blk = slice(n * LANES, (n + 1) * LANES)
            xcb = xc[:, blk].astype(BF16)
            dwr_ref[n] += _dot_tn(xcb, dpr[:, blk].astype(BF16))
            dwi_ref[n] += _dot_tn(xcb, dpi[:, blk].astype(BF16))
        dxc = dxc + _gate_pre_t(dpr, wr_ref) + _gate_pre_t(dpi, wi_ref)

        xl = xl_ref[...]
        nxt = dxc_next_s[...]
        dxl = dxc * cw_ref[3:4, :]
        acc_ref[3:4, :] += jnp.sum(dxc * xl, axis=0, keepdims=True)
        for j in range(3):
            ahead = _shift_up(dxc, 3 - j, nxt)
            dxl = dxl + ahead * cw_ref[j:j + 1, :]
            acc_ref[j:j + 1, :] += jnp.sum(ahead * xl, axis=0, keepdims=True)
        dxc_next_s[...] = dxc[:SUBLANES, :]
        dxl_ref[...] = dxl.astype(BF16)

        acc_ref[4:5, :] += jnp.sum(dxc, axis=0, keepdims=True)
        acc_ref[5:6, :] += jnp.sum(dpr, axis=0, keepdims=True)
        acc_ref[6:7, :] += jnp.sum(dpi, axis=0, keepdims=True)
        acc_ref[7:8, :] += jnp.sum(dla * ((-LRU_C) * r), axis=0, keepdims=True)

        @pl.when(i == nt - 1)
        def _():
            lam_v = lam_ref[...]
            acc_ref[7:8, :] = acc_ref[7:8, :] * (-_sigmoid(-lam_v))

    rev = pl.BlockSpec((tm, D), lambda i: (nt - 1 - i, 0))
    prev8 = pl.BlockSpec((SUBLANES, D), lambda i: (jnp.maximum((nt - 1 - i) * per - 1, 0), 0))
    vec = _const_spec((1, D))
    wspec = _const_spec((NB, LANES, LANES))
    bf = jax.ShapeDtypeStruct((T, D), BF16)
    return _call(
        body, name="lru_bwd", grid=(nt,),
        in_specs=[rev, rev, prev8, rev, rev, _const_spec((4, D)), wspec, vec, wspec, vec, vec],
        out_specs=[rev, wspec, wspec, _const_spec((SUBLANES, D))],
        out_shape=[bf, jax.ShapeDtypeStruct((NB, LANES, LANES), F32), jax.ShapeDtypeStruct((NB, LANES, LANES), F32),
                   jax.ShapeDtypeStruct((SUBLANES, D), F32)],
        scratch_shapes=[pltpu.VMEM((1, D), F32), pltpu.VMEM((SUBLANES, D), F32), pltpu.VMEM((SUBLANES, D), F32),
                        pltpu.VMEM((tm, D), F32)],
        compiler_params=_cparams(("arbitrary",)),
    )(dh, h, h, xc, x_lru, conv_w, w_r, b_r, w_i, b_i, lam)


def _chip_copies(srcs, dsts, send_sems, recv_sems):
    x, y, c = _position()
    chip = 2 * x + y
    na = len(srcs)
    return [pltpu.make_async_remote_copy(
        src_ref=srcs[a].at[2 * px + py], dst_ref=dsts[a].at[chip], send_sem=send_sems.at[j * na + a],
        recv_sem=recv_sems.at[j * na + a], device_id=(px, py, c), device_id_type=MESH)
        for j, (px, py) in enumerate(_other_chips(x, y)) for a in range(na)]


def _attn_bwd(q_aug, qx, k_aug, v_aug, do_aug, exchange=()):
    T = q_aug.shape[0]
    t = TA
    n = T // t
    hp = BWD_HEADS
    heads = range(hp)
    scale = DH ** -0.5
    ki_tab, qi_tab = _causal_pairs(n, q_major=False)
    last = ki_tab.shape[0] - 1
    ne = len(exchange)
    n_h = H // hp

    def body(ki_ref, qi_ref, q_ref, qx_ref, k_ref, v_ref, do_ref, *rest):
        sent, rest = rest[:ne], rest[ne:]
        dq_ref, dk_ref, dv_ref, dck_ref, dcq_ref = rest[:5]
        received, rest = rest[5:5 + ne], rest[5 + ne:]
        dq_s, dk_s, dv_s = rest[:3]
        j = pl.program_id(1)
        ki = ki_ref[j]
        qi = qi_ref[j]

        if ne:
            first_step = (pl.program_id(0) == 0) & (j == 0)
            last_step = (pl.program_id(0) == n_h - 1) & (j == last)

            @pl.when(first_step)
            def _():
                for cp in _chip_copies(sent, received, *rest[3:]):
                    cp.start()

            @pl.when(last_step)
            def _():
                for cp in _chip_copies(sent, received, *rest[3:]):
                    cp.wait()

        @pl.when(j == 0)
        def _():
            dq_s[...] = jnp.zeros_like(dq_s)

        @pl.when(qi == ki)
        def _():
            dk_s[...] = jnp.zeros_like(dk_s)
            dv_s[...] = jnp.zeros_like(dv_s)

        def step(on_diagonal):
            cols = [slice(a * AUG, (a + 1) * AUG) for a in heads]
            qb = [jnp.concatenate([q_ref[:, a * AUG:a * AUG + DH], qx_ref[:, a * DH:(a + 1) * DH]], axis=1)
                  for a in heads]
            if on_diagonal:
                krow = lax.broadcasted_iota(jnp.int32, (t, t), 0)
                qcol = lax.broadcasted_iota(jnp.int32, (t, t), 1)

            def scores(a):
                st = _dot_nt(k_ref[:, cols[a]], qb[a])
                dpd = _dot_nt(v_ref[:, cols[a]], do_ref[:, cols[a]])
                return (jnp.where(krow <= qcol, st, NEG) if on_diagonal else st), dpd

            off = pl.multiple_of(qi * t, t)
            ahead = scores(0)
            for a in heads:
                st, dpd = ahead
                if a + 1 < hp:
                    ahead = scores(a + 1)
                pt = jnp.exp2(st)
                dsb = (pt * dpd).astype(BF16)
                dv_s[a] += _dot(pt.astype(BF16), do_ref[:, a * AUG:a * AUG + DH])
                dk_s[a] += _dot(dsb, qb[a])
                dq_s[a, pl.ds(off, t), :] += _dot_tn(dsb, k_ref[:, cols[a]])

        @pl.when(qi > ki)
        def _():
            step(False)

        @pl.when(qi == ki)
        def _():
            step(True)

        @pl.when(qi == n - 1)
        def _():
            for a in heads:
                dk_ref[:, a * DH:(a + 1) * DH] = (dk_s[a, :, :DH] * LN2).astype(BF16)
                dv_ref[:, a * DH:(a + 1) * DH] = dv_s[a].astype(BF16)
                dck_ref[a] = jnp.broadcast_to(dk_s[a, :, DH + 3:DH + 4], (t, LANES))

        @pl.when(j == last)
        def _():
            for a in heads:
                dq_ref[:, a * DH:(a + 1) * DH] = (dq_s[a, :, :DH] * scale).astype(BF16)
                dcq_ref[a] = jnp.broadcast_to(dq_s[a, :, DH:DH + 1], (T, LANES))

    qside = pl.BlockSpec((t, hp * AUG), lambda h, j, ki_ref, qi_ref: (qi_ref[j], h))
    qxside = pl.BlockSpec((t, hp * DH), lambda h, j, ki_ref, qi_ref: (qi_ref[j], h))
    kside = pl.BlockSpec((t, hp * AUG), lambda h, j, ki_ref, qi_ref: (ki_ref[j], h))
    kout = pl.BlockSpec((t, hp * DH), lambda h, j, ki_ref, qi_ref: (ki_ref[j], h))
    bf = jax.ShapeDtypeStruct((T, D), BF16)
    sums = jax.ShapeDtypeStruct((H, T, LANES), F32)
    grid_spec = pltpu.PrefetchScalarGridSpec(
        num_scalar_prefetch=2, grid=(n_h, ki_tab.shape[0]),
        in_specs=[qside, qxside, kside, kside, qside] + [HBM_SPEC] * ne,
        out_specs=[pl.BlockSpec((T, hp * DH), lambda h, j, ki_ref, qi_ref: (0, h)), kout, kout,
                   pl.BlockSpec((hp, t, LANES), lambda h, j, ki_ref, qi_ref: (h, ki_ref[j], 0)),
                   pl.BlockSpec((hp, T, LANES), lambda h, j, ki_ref, qi_ref: (h, 0, 0))] + [HBM_SPEC] * ne,
        scratch_shapes=[pltpu.VMEM((hp, T, AUG), F32), pltpu.VMEM((hp, t, AUG), F32), pltpu.VMEM((hp, t, DH), F32)]
        + ([pltpu.SemaphoreType.DMA((3 * ne,)), pltpu.SemaphoreType.DMA((3 * ne,))] if ne else []))
    outs = _call(
        body, name="attn_bwd", grid_spec=grid_spec,
        out_shape=[bf, bf, bf, sums, sums] + [jax.ShapeDtypeStruct(s.shape, s.dtype) for s in exchange],
        compiler_params=_cparams(("arbitrary", "arbitrary"), VMEM_BIG),
    )(ki_tab, qi_tab, q_aug, qx, k_aug, v_aug, do_aug, *exchange)
    return (*outs[:5], list(outs[5:]))


def _fgate_bwd(dc_key, dc_query, flb):
    T = flb.shape[0]
    tm = TM
    nt = T // tm

    def body(dck_ref, dcq_ref, flb_ref, dfl_ref, acc_ref, carry, top_s):
        @pl.when(pl.program_id(0) == 0)
        def _():
            carry[...] = jnp.zeros_like(carry)
            acc_ref[...] = jnp.zeros_like(acc_ref)

        flb = flb_ref[...]
        lane = lax.broadcasted_iota(jnp.int32, flb.shape, 1)
        dc = jnp.zeros(flb.shape, F32)
        for hd in range(H):
            dc = dc + jnp.where(lane == hd, dcq_ref[hd] - dck_ref[hd], 0.0)
        r = lax.broadcasted_iota(jnp.int32, (tm, tm), 0)
        c = lax.broadcasted_iota(jnp.int32, (tm, tm), 1)
        dls = _dot_exact((c >= r).astype(F32), dc) + carry[...]
        top_s[...] = dls[:SUBLANES, :]
        carry[...] = top_s[0:1, :]
        dfl = jnp.where(lane < H, dls * _sigmoid(-flb), 0.0)
        dfl_ref[...] = dfl.astype(BF16)
        acc_ref[0:1, :] += jnp.sum(dfl, axis=0, keepdims=True)

    rev = pl.BlockSpec((tm, LANES), lambda i: (nt - 1 - i, 0))
    return _call(
        body, name="fgate_bwd", grid=(nt,),
        in_specs=[pl.BlockSpec((H, tm, LANES), lambda i: (0, nt - 1 - i, 0))] * 2 + [rev],
        out_specs=[rev, _const_spec((SUBLANES, LANES))],
        out_shape=[jax.ShapeDtypeStruct((T, LANES), BF16), jax.ShapeDtypeStruct((SUBLANES, LANES), F32)],
        scratch_shapes=[pltpu.VMEM((1, LANES), F32), pltpu.VMEM((SUBLANES, LANES), F32)],
        compiler_params=_cparams(("arbitrary",)),
    )(dc_key, dc_query, flb)


def _dx(dz, dfl, w_a, w_f, w_b, x, pre_gain, dh1, exchange=()):
    T = x.shape[0]
    tm = min(T, TM_WIDE)
    nt = T // tm
    ne = len(exchange)

    def body(*refs):
        dz_refs = refs[:6]
        dfl_ref, wa_ref, wf_ref, wb_ref, x_ref, g_ref, dh1_ref = refs[6:13]
        sent = refs[13:13 + ne]
        gx_ref, acc_ref = refs[13 + ne:15 + ne]
        received, sems = refs[15 + ne:15 + 2 * ne], refs[15 + 2 * ne:]

        @pl.when(pl.program_id(0) == 0)
        def _():
            acc_ref[...] = jnp.zeros_like(acc_ref)
            for cp in _chip_copies(sent, received, *sems) if ne else ():
                cp.start()

        if ne:
            @pl.when(pl.program_id(0) == nt - 1)
            def _():
                for cp in _chip_copies(sent, received, *sems):
                    cp.wait()

        dxn = _dot(dfl_ref[...], wf_ref[...])
        for s in range(3):
            dxn = dxn + _dot(dz_refs[s][...], wa_ref[s * D:(s + 1) * D, :])
            dxn = dxn + _dot(dz_refs[3 + s][...], wb_ref[s * D:(s + 1) * D, :])
        xv = x_ref[...]
        rstd = _rstd(xv)
        xhat = xv * rstd
        gx_ref[...] = dh1_ref[...] + _rms_bwd(dxn * g_ref[...], xhat, rstd)
        acc_ref[0:1, :] += jnp.sum(dxn * xhat, axis=0, keepdims=True)

    outs = _call(
        body, name="dx", grid=(nt,),
        in_specs=[_row_spec(tm, D)] * 6 + [_row_spec(tm, LANES), _weight_spec((3 * D, D)), _weight_spec((LANES, D)),
                                           _weight_spec((3 * D, D)), _row_spec(tm, D), _const_spec((1, D)),
                                           _row_spec(tm, D)] + [HBM_SPEC] * ne,
        out_specs=[_row_spec(tm, D), _const_spec((SUBLANES, D))] + [HBM_SPEC] * ne,
        out_shape=[jax.ShapeDtypeStruct((T, D), F32), jax.ShapeDtypeStruct((SUBLANES, D), F32)]
        + [jax.ShapeDtypeStruct(s.shape, s.dtype) for s in exchange],
        scratch_shapes=[pltpu.SemaphoreType.DMA((3 * ne,)), pltpu.SemaphoreType.DMA((3 * ne,))] if ne else [],
        compiler_params=_cparams(("arbitrary",), VMEM_BIG),
    )(*dz, dfl, w_a, w_f, w_b, x, pre_gain, dh1, *exchange)
    return outs[0], outs[1], list(outs[2:])


GRAD_ROWS = D_IN + SUBLANES


def _dw_in_segment(dz_s, xn, buf, s, bt):
    T = xn.shape[0]
    row0 = s * D + (H if s >= 3 else 0)

    def body(*refs):
        dz_ref, xn_ref, o_ref = refs[0], refs[1], refs[-1]

        @pl.when(pl.program_id(0) == 0)
        def _():
            o_ref[...] = jnp.zeros_like(o_ref)

        o_ref[...] += _dot_tn(dz_ref[...], xn_ref[...])

    tok = pl.BlockSpec((bt, D), lambda t: (t, 0))
    return _call(
        body, name="dw_in_%d" % s, grid=(T // bt,),
        in_specs=[tok, tok] + ([] if buf is None else [pl.BlockSpec(memory_space=pl.ANY)]),
        out_specs=pl.BlockSpec((pl.Element(D), pl.Element(D)), lambda t: (row0, 0)),
        out_shape=jax.ShapeDtypeStruct((GRAD_ROWS, D), F32),
        input_output_aliases={} if buf is None else {2: 0},
        compiler_params=_cparams(("arbitrary",)),
    )(*((dz_s, xn) if buf is None else (dz_s, xn, buf)))


def _dw_in_t(dz, dfl, xn, bt=2048):
    T = xn.shape[0]
    bt = min(bt, T)
    nt = T // bt
    main = None
    for s in range(6):
        main = _dw_in_segment(dz[s], xn, main, s, min(T, 2048))

    def f_body(dfl_ref, xn_ref, main_ref, o_ref, acc_s):
        p = pl.program_id(0)
        t = pl.program_id(1)

        @pl.when(t == 0)
        def _():
            acc_s[...] = jnp.zeros_like(acc_s)

        @pl.when(p == 0)
        def _():
            acc_s[...] += _dot_tn(dfl_ref[...], xn_ref[...])

        @pl.when(t == nt - 1)
        def _():
            o_ref[...] = acc_s[:SUBLANES, :]

    fl_block = FL0 // SUBLANES
    end_block = D_IN // SUBLANES
    return _call(
        f_body, name="dw_in_f", grid=(2, nt),
        in_specs=[pl.BlockSpec((bt, LANES), lambda p, t: (t, 0)), pl.BlockSpec((bt, D), lambda p, t: (t, 0)),
                  pl.BlockSpec(memory_space=pl.ANY)],
        out_specs=pl.BlockSpec((SUBLANES, D), lambda p, t: (fl_block + p * (end_block - fl_block), 0)),
        out_shape=jax.ShapeDtypeStruct((GRAD_ROWS, D), F32),
        scratch_shapes=[pltpu.VMEM((LANES, D), F32)],
        input_output_aliases={2: 0},
        compiler_params=_cparams(("arbitrary", "arbitrary")),
    )(dfl, xn, main)


def _matmul_tn(a, b, name, bm=512, bn=1024, bt=2048):
    T, M = a.shape
    N = b.shape[1]
    bm, bn, bt = min(bm, M), min(bn, N), min(bt, T)

    def body(a_ref, b_ref, o_ref):
        @pl.when(pl.program_id(2) == 0)
        def _():
            o_ref[...] = jnp.zeros_like(o_ref)

        o_ref[...] += _dot_tn(a_ref[...], b_ref[...])

    return _call(
        body, name=name, grid=(M // bm, N // bn, T // bt),
        in_specs=[pl.BlockSpec((bt, bm), lambda i, j, t: (t, i)), pl.BlockSpec((bt, bn), lambda i, j, t: (t, j))],
        out_specs=pl.BlockSpec((bm, bn), lambda i, j, t: (i, j)),
        out_shape=jax.ShapeDtypeStruct((M, N), F32),
        compiler_params=_cparams(("parallel", "parallel", "arbitrary")),
    )(a, b)


HBM_SPEC = pl.BlockSpec(memory_space=pltpu.HBM)
VMEM_SPEC = pl.BlockSpec(memory_space=pltpu.VMEM)


def _position():
    return lax.axis_index("x"), lax.axis_index("y"), lax.axis_index("c")


def _other_chips(x, y):
    return [(1 - x, y), (x, 1 - y), (1 - x, 1 - y)]


def _gather_shards(shards, whole):
    na, nw = len(shards), len(whole)
    nall = na + nw

    def body(*refs):
        gather = _GatherPlan(refs[:nall], refs[nall:2 * nall], refs[2 * nall:], na)
        gather.send()
        gather.forward()
        gather.finish()

    arrs = list(shards) + list(whole)
    outs = _call(
        body, name="gather_shards",
        in_specs=[HBM_SPEC] * nall, out_specs=[HBM_SPEC] * nall,
        out_shape=_gather_out_shapes(arrs), scratch_shapes=_gather_semaphores(na, nall),
    )(*arrs)
    return _place_own(outs, arrs)


def _gather_out_shapes(arrs):
    return [jax.ShapeDtypeStruct((N_CHIPS,) + s.shape, s.dtype) for s in arrs]


def _gather_semaphores(na, nall):
    return [pltpu.SemaphoreType.DMA((3 * nall,)), pltpu.SemaphoreType.DMA((3 * nall,)),
            pltpu.SemaphoreType.DMA((3 * na,)), pltpu.SemaphoreType.DMA((3 * na,))]


def _place_own(outs, arrs):
    if not arrs:
        return []
    chip = 2 * lax.axis_index("x") + lax.axis_index("y")
    return [lax.dynamic_update_slice(o, a[None], (chip,) + (0,) * a.ndim) for o, a in zip(outs, arrs)]


class _GatherPlan:
    def __init__(self, srcs, dsts, sems, na):
        ici_send, ici_recv, d2d_send, d2d_recv = sems
        x, y, c = _position()
        chip = 2 * x + y
        nall = len(srcs)

        def half(a, which):
            rows = srcs[a].shape[0] // 2
            return pl.ds(pl.multiple_of(which * rows, 16), rows)

        def copy(src, dst, send, recv, k, to):
            return pltpu.make_async_remote_copy(src_ref=src, dst_ref=dst, send_sem=send.at[k], recv_sem=recv.at[k],
                                                device_id=to, device_id_type=MESH)

        self.first, self.landed, self.passed, self.returned = [], [], [], []
        for j, (px, py) in enumerate(_other_chips(x, y)):
            theirs = 2 * px + py
            for a in range(nall):
                k = j * nall + a
                if a < na:
                    self.first.append(copy(srcs[a].at[half(a, c), :], dsts[a].at[chip, half(a, c), :],
                                           ici_send, ici_recv, k, (px, py, c)))
                    mine = dsts[a].at[theirs, half(a, c), :]
                    other = dsts[a].at[theirs, half(a, 1 - c), :]
                    self.landed.append(copy(mine, mine, ici_send, ici_recv, k, (px, py, c)))
                    self.passed.append(copy(mine, mine, d2d_send, d2d_recv, j * na + a, (x, y, 1 - c)))
                    self.returned.append(copy(other, other, d2d_send, d2d_recv, j * na + a, (x, y, 1 - c)))
                else:
                    self.first.append(copy(srcs[a], dsts[a].at[chip], ici_send, ici_recv, k, (px, py, c)))
                    got = dsts[a].at[theirs]
                    self.landed.append(copy(got, got, ici_send, ici_recv, k, (px, py, c)))
                    self.passed.append(None)

    def send(self):
        for cp in self.first:
            cp.start()

    def forward(self):
        for arrival, fwd in zip(self.landed, self.passed):
            arrival.wait_recv()
            if fwd is not None:
                fwd.start()

    def finish(self):
        for cp in self.returned:
            cp.wait_recv()
        for cp in self.first + [f for f in self.passed if f is not None]:
            cp.wait_send()


W_ROWS = 1568
G_ROWS = 1552
SHARD_ROWS = D_IN // N_CHIPS
WINDOW_STEP = 1536


def _assemble_w_in(cont):
    cb = 256
    half = WINDOW_STEP

    def body(c_ref, wa_ref, wf_ref, wb_ref):
        x0 = c_ref[0].astype(F32)
        x1, x2, x3 = (pltpu.roll(c_ref[j].astype(F32), 2 * j, 0) for j in (1, 2, 3))
        wa = jnp.concatenate([x0[:half], x0[half:half + 16] + x1[:16], x1[16:half]], axis=0)
        wa_ref[...] = wa.astype(BF16)

        fl = x1[half:half + 16] + x2[:16]
        row = lax.broadcasted_iota(jnp.int32, fl.shape, 0)
        wf_ref[:16, :] = jnp.where(row < H, fl, 0.0).astype(BF16)
        wf_ref[16:, :] = jnp.zeros((LANES - 16, cb), BF16)

        mid = x2[half:half + SUBLANES] + x3[:SUBLANES]
        wb = jnp.concatenate([x2[SUBLANES:half], mid, x3[SUBLANES:half + SUBLANES]], axis=0)
        wb_ref[...] = wb.astype(BF16)

    return _call(
        body, name="assemble_w_in", grid=(D // cb,),
        in_specs=[pl.BlockSpec((N_CHIPS, W_ROWS, cb), lambda i: (0, 0, i))],
        out_specs=[pl.BlockSpec((3 * D, cb), lambda i: (0, i)), pl.BlockSpec((LANES, cb), lambda i: (0, i)),
                   pl.BlockSpec((3 * D, cb), lambda i: (0, i))],
        out_shape=[jax.ShapeDtypeStruct((3 * D, D), BF16), jax.ShapeDtypeStruct((LANES, D), BF16),
                   jax.ShapeDtypeStruct((3 * D, D), BF16)],
        compiler_params=_cparams(("parallel",)),
    )(cont)


def _pair_exchange_windows(grad_t):
    half_g = G_ROWS // 2

    def body(g_ref, got, send_sems, recv_sems):
        x, y, c = _position()
        copies = []
        for j in range(N_CHIPS):
            rows = pl.ds(pl.multiple_of(j * WINDOW_STEP + (1 - c) * half_g, SUBLANES), half_g)
            copies.append(pltpu.make_async_remote_copy(
                src_ref=g_ref.at[rows, :], dst_ref=got.at[j], send_sem=send_sems.at[j], recv_sem=recv_sems.at[j],
                device_id=(x, y, 1 - c), device_id_type=MESH))
        for cp in copies:
            cp.start()
        for cp in copies:
            cp.wait()

    return _call(
        body, name="pair_exchange_w_in",
        in_specs=[HBM_SPEC], out_specs=HBM_SPEC,
        out_shape=jax.ShapeDtypeStruct((N_CHIPS, half_g, D), F32),
        scratch_shapes=[pltpu.SemaphoreType.DMA((N_CHIPS,)), pltpu.SemaphoreType.DMA((N_CHIPS,))],
    )(grad_t)


def _pair_exchange(parts):
    na = len(parts)

    def body(*refs):
        srcs, got = refs[:na], refs[na:2 * na]
        send_sems, recv_sems = refs[2 * na:]
        x, y, c = _position()
        copies = []
        for a in range(na):
            half = srcs[a].shape[1] // 2
            rows = pl.ds(pl.multiple_of((1 - c) * half, SUBLANES), half)
            copies.append(pltpu.make_async_remote_copy(
                src_ref=srcs[a].at[:, rows, :], dst_ref=got[a], send_sem=send_sems.at[a], recv_sem=recv_sems.at[a],
                device_id=(x, y, 1 - c), device_id_type=MESH))
        for cp in copies:
            cp.start()
        for cp in copies:
            cp.wait()

    return _call(
        body, name="pair_exchange",
        in_specs=[HBM_SPEC] * na, out_specs=[HBM_SPEC] * na,
        out_shape=[jax.ShapeDtypeStruct((s.shape[0], s.shape[1] // 2, s.shape[2]), s.dtype) for s in parts],
        scratch_shapes=[pltpu.SemaphoreType.DMA((na,)), pltpu.SemaphoreType.DMA((na,))],
    )(*parts)


def _pair_sum(parts, gots, c):
    na = len(parts)

    def body(c_ref, *refs):
        for a in range(na):
            refs[2 * na + a][...] = (refs[a][...] + refs[na + a][...]).astype(BF16)

    mine = [pl.BlockSpec(g.shape, lambda i, c_ref: (0, c_ref[0], 0)) for g in gots]
    whole = [pl.BlockSpec(g.shape, lambda i, c_ref: (0, 0, 0)) for g in gots]
    grid_spec = pltpu.PrefetchScalarGridSpec(
        num_scalar_prefetch=1, grid=(1,), in_specs=mine + whole, out_specs=whole)
    return _call(
        body, name="pair_sum", grid_spec=grid_spec,
        out_shape=[jax.ShapeDtypeStruct(g.shape, BF16) for g in gots],
        compiler_params=_cparams(("arbitrary",), VMEM_BIG),
    )(c.reshape(1), *parts, *gots)


def _pair_sum_windows(grad_t, got, c):
    _, half, C = got.shape
    cb = 256

    def body(c_ref, a_ref, b_ref, o_ref):
        o_ref[0] = (a_ref[...] + b_ref[0]).astype(BF16)

    def mine(j, i, c_ref):
        return ((j * (WINDOW_STEP // SUBLANES) + c_ref[0] * (half // SUBLANES)) * SUBLANES, i * cb)

    spec = pl.BlockSpec((1, half, cb), lambda j, i, c_ref: (j, 0, i))
    grid_spec = pltpu.PrefetchScalarGridSpec(
        num_scalar_prefetch=1, grid=(N_CHIPS, C // cb),
        in_specs=[pl.BlockSpec((pl.Element(half), pl.Element(cb)), mine), spec], out_specs=spec)
    return _call(
        body, name="pair_sum_w_in", grid_spec=grid_spec,
        out_shape=jax.ShapeDtypeStruct((N_CHIPS, half, C), BF16),
        compiler_params=_cparams(("parallel", "parallel")),
    )(c.reshape(1), grad_t, got)


def _chip_exchange(sums):
    na = len(sums)

    def body(*refs):
        copies = _chip_copies(refs[:na], refs[na:2 * na], *refs[2 * na:])
        for cp in copies:
            cp.start()
        for cp in copies:
            cp.wait()

    return _call(
        body, name="chip_exchange",
        in_specs=[HBM_SPEC] * na, out_specs=[HBM_SPEC] * na,
        out_shape=[jax.ShapeDtypeStruct(s.shape, s.dtype) for s in sums],
        scratch_shapes=[pltpu.SemaphoreType.DMA((3 * na,)), pltpu.SemaphoreType.DMA((3 * na,))],
    )(*sums)


def _chip_sum(own, got, chip, name):
    _, half, C = got.shape
    cb = min(C, 256)

    def body(chip_ref, own_ref, g_ref, o_ref):
        for me in range(N_CHIPS):
            @pl.when(chip_ref[0] == me)
            def _(me=me):
                terms = [own_ref[0] if k == me else g_ref[k] for k in range(N_CHIPS)]
                acc = terms[0].astype(F32) + terms[1].astype(F32)
                acc = acc + terms[2].astype(F32)
                o_ref[...] = acc + terms[3].astype(F32)

    grid_spec = pltpu.PrefetchScalarGridSpec(
        num_scalar_prefetch=1, grid=(C // cb,),
        in_specs=[pl.BlockSpec((1, half, cb), lambda i, chip_ref: (chip_ref[0], 0, i)),
                  pl.BlockSpec((N_CHIPS, half, cb), lambda i, chip_ref: (0, 0, i))],
        out_specs=pl.BlockSpec((half, cb), lambda i, chip_ref: (0, i)))
    return _call(
        body, name=name, grid_spec=grid_spec,
        out_shape=jax.ShapeDtypeStruct((half, C), F32),
        compiler_params=_cparams(("parallel",)),
    )(chip.reshape(1), own, got)


def _pair_swap(halves):
    na = len(halves)

    def body(*refs):
        srcs, dsts = refs[:na], refs[na:2 * na]
        send_sems, recv_sems = refs[2 * na:]
        x, y, c = _position()
        copies = [pltpu.make_async_remote_copy(
            src_ref=srcs[a], dst_ref=dsts[a], send_sem=send_sems.at[a], recv_sem=recv_sems.at[a],
            device_id=(x, y, 1 - c), device_id_type=MESH) for a in range(na)]
        for cp in copies:
            cp.start()
        for cp in copies:
            cp.wait()

    return _call(
        body, name="pair_swap",
        in_specs=[HBM_SPEC] * na, out_specs=[HBM_SPEC] * na,
        out_shape=[jax.ShapeDtypeStruct(s.shape, s.dtype) for s in halves],
        scratch_shapes=[pltpu.SemaphoreType.DMA((na,)), pltpu.SemaphoreType.DMA((na,))],
    )(*halves)


def _allreduce_small(g):
    rows = g.shape[0]
    per = rows // N_DEV

    def body(g_ref, out_ref, got_ref, s1, r1, s2, r2):
        x, y, c = _position()
        me = 4 * x + 2 * y + c
        mine = pl.ds(pl.multiple_of(me * per, SUBLANES), per)
        peers = []
        for j in range(1, N_DEV):
            px = 1 - x if j & 4 else x
            py = 1 - y if j & 2 else y
            pc = 1 - c if j & 1 else c
            peers.append((px, py, pc))

        first = []
        for j, (px, py, pc) in enumerate(peers):
            theirs = pl.ds(pl.multiple_of((4 * px + 2 * py + pc) * per, SUBLANES), per)
            first.append(pltpu.make_async_remote_copy(
                src_ref=g_ref.at[theirs, :], dst_ref=got_ref.at[me], send_sem=s1.at[j], recv_sem=r1.at[j],
                device_id=(px, py, pc), device_id_type=MESH))
        for cp in first:
            cp.start()
        got_ref[me] = g_ref[mine, :]
        for cp in first:
            cp.wait()
        total = got_ref[0]
        for d in range(1, N_DEV):
            total = total + got_ref[d]
        out_ref[mine, :] = total

        second = []
        for j, peer in enumerate(peers):
            second.append(pltpu.make_async_remote_copy(
                src_ref=out_ref.at[mine, :], dst_ref=out_ref.at[mine, :], send_sem=s2.at[j], recv_sem=r2.at[j],
                device_id=peer, device_id_type=MESH))
        for cp in second:
            cp.start()
        for cp in second:
            cp.wait()

    sems = pltpu.SemaphoreType.DMA((N_DEV - 1,))
    return _call(
        body, name="allreduce_small", in_hbm=False,
        in_specs=[VMEM_SPEC], out_specs=VMEM_SPEC,
        out_shape=jax.ShapeDtypeStruct(g.shape, F32),
        scratch_shapes=[pltpu.VMEM((N_DEV, per, LANES), F32), sems, sems, sems, sems],
    )(g)


def _adamw_math(g, w, m, v):
    m2 = ADAM_B1 * m + (1.0 - ADAM_B1) * g
    v2 = ADAM_B2 * v + (1.0 - ADAM_B2) * (g * g)
    m_hat = m2 / (1.0 - ADAM_B1 ** ADAM_STEP)
    v_hat = v2 / (1.0 - ADAM_B2 ** ADAM_STEP)
    delta = (-ADAM_LR) * (m_hat / (jnp.sqrt(v_hat) + ADAM_EPS) + ADAM_WD * w)
    return delta, m2, v2


ADAMW_BLOCK_BYTES = 1 << 20


def _adamw_big(g, w, m, v, name, copy_g=False):
    R, C = g.shape
    if C == LANES:
        br, bc = min(R, ADAMW_BLOCK_BYTES // (4 * LANES)), LANES
    else:
        br, bc = R, min(C, max(LANES, ADAMW_BLOCK_BYTES // (4 * R) // LANES * LANES))
    n_out = 4 if copy_g else 3

    def body(g_ref, w_ref, m_ref, v_ref, d_ref, m2_ref, v2_ref, *g_out):
        gv = g_ref[...]
        d_ref[...], m2_ref[...], v2_ref[...] = _adamw_math(gv, w_ref[...], m_ref[...], v_ref[...])
        if copy_g:
            g_out[0][...] = gv

    spec = pl.BlockSpec((br, bc), lambda i, j: (i, j))
    out = jax.ShapeDtypeStruct((R, C), F32)
    return _call(
        body, name=name, grid=(pl.cdiv(R, br), C // bc),
        in_specs=[spec] * 4, out_specs=[spec] * n_out, out_shape=[out] * n_out,
        compiler_params=_cparams(("parallel", "parallel")),
    )(g, w, m, v)


def _adamw_small(gs, ws, ms, vs):
    n = len(gs)

    def body(*refs):
        for a in range(n):
            g_ref, w_ref, m_ref, v_ref = (refs[k * n + a] for k in range(4))
            d_ref, m2_ref, v2_ref = (refs[(4 + k) * n + a] for k in range(3))
            d_ref[...], m2_ref[...], v2_ref[...] = _adamw_math(g_ref[...], w_ref[...], m_ref[...], v_ref[...])

    outs = [jax.ShapeDtypeStruct(w.shape, F32) for w in ws]
    specs = [_const_spec(w.shape) for w in ws]
    return _call(
        body, name="adamw_small", grid=(1,),
        in_specs=specs * 4, out_specs=specs * 3, out_shape=outs * 3,
    )(*gs, *ws, *ms, *vs)


def _late_weights(st_out, st_ple, st_gate, st_conv):
    return st_out.reshape(DMIX, D), _from_chip_cols(st_ple), st_gate.reshape(D, D), _from_chip_cols(st_conv)


def _local_step(x, p, tgt, w_a, w_f, w_b, late, b_f, pre_gain, post_gain, conv_b,
                w_rgate, b_rgate, w_igate, b_igate, lam, gain_a, gain_l, ple_gain, b_gate,
                gather_late=False, early_reduce=None, w_in_reduce=None):
    b_f_pad = jnp.pad(b_f, ((0, 0), (0, LANES - H)))
    w_r = w_rgate.astype(BF16)
    w_i = w_igate.astype(BF16)

    xn, q_aug, k_aug, v_aug, g_attn, x_lru, g_lru, flb, vt_aug = _in_proj(x, pre_gain, w_a, w_f, w_b, b_f_pad)
    if gather_late:
        o, qx, stacks = _attn_fwd(q_aug, k_aug, vt_aug, late[:3], late[3:])
        late = _late_weights(*stacks)
    else:
        o, qx, _ = _attn_fwd(q_aug, k_aug, vt_aug)
    w_out_b, w_ple_b, w_gate_b, conv_w = late
    ycat, xc, h = _branches_fwd(o, g_attn, x_lru, g_lru, gain_a, gain_l, conv_w, conv_b, w_r, b_rgate, w_i, b_igate,
                                lam)
    dh1, dycat, dmix, h1b, dgp, pb, dpe, acc_t = _tail(ycat, x, p, tgt, w_out_b, post_gain, w_ple_b, ple_gain,
                                                       w_gate_b, b_gate)
    late_grads = [_matmul_tn(ycat, dmix, "dw_out"), _matmul_tn(pb, dpe, "dw_ple"),
                  _matmul_tn(h1b, dgp, "dw_ple_gate")]
    do_aug, dg_attn, dg_lru, dh, acc_b = _branches_bwd(dycat, o, g_attn, h, g_lru, gain_a, gain_l)
    dx_lru, gw_r, gw_i, acc_l = _lru_bwd(dh, h, xc, x_lru, conv_w, w_r, b_rgate, w_i, b_igate, lam)
    if early_reduce is None:
        dq, dk, dv, dc_key, dc_query, _ = _attn_bwd(q_aug, qx, k_aug, v_aug, do_aug)
    else:
        sent = early_reduce(late_grads)
        dq, dk, dv, dc_key, dc_query, received = _attn_bwd(q_aug, qx, k_aug, v_aug, do_aug, sent)
        late_grads = list(zip(sent, received))
    dfl, acc_f = _fgate_bwd(dc_key, dc_query, flb)
    dz = (dq, dk, dv, dg_attn, dx_lru, dg_lru)
    grad_t = _dw_in_t(dz, dfl, xn)
    if w_in_reduce is None:
        grad_x, acc_x, _ = _dx(dz, dfl, w_a, w_f, w_b, x, pre_gain, dh1)
    else:
        sent = w_in_reduce(grad_t)
        grad_x, acc_x, (received,) = _dx(dz, dfl, w_a, w_f, w_b, x, pre_gain, dh1, [sent])
        grad_t = (sent, received)

    grads = dict(
        w_in_t=grad_t,
        w_out=late_grads[0],
        w_ple=late_grads[1],
        w_ple_gate=late_grads[2],
        w_rgate=gw_r,
        w_igate=gw_i,
        b_f=acc_f[0:1, :H],
        pre_gain=acc_x[0:1],
        post_gain=acc_t[0:1],
        conv_w=acc_l[0:4],
        conv_b=acc_l[4:5],
        b_rgate=acc_l[5:6],
        b_igate=acc_l[6:7],
        lru_lambda=acc_l[7:8],
        attn_out_gain=acc_b[0:1],
        lru_out_gain=acc_b[1:2],
        ple_gain=acc_t[1:2],
        b_ple_gate=acc_t[2:3],
    )
    loss = jnp.sum(acc_t[3])
    return loss, grad_x, grads


SMALL_ROWS = ["b_f", "pre_gain", "post_gain", "conv_w", "conv_b", "b_rgate", "b_igate", "lru_lambda",
              "attn_out_gain", "lru_out_gain", "ple_gain", "b_ple_gate"]
WEIGHTS = ["w_in", "b_f", "pre_gain", "post_gain", "conv_w", "conv_b", "w_rgate", "b_rgate", "w_igate", "b_igate",
           "lru_lambda", "attn_out_gain", "lru_out_gain", "w_out", "w_ple", "ple_gain", "w_ple_gate", "b_ple_gate"]
SHARDED = ["w_in", "w_out", "w_ple", "w_ple_gate"]


def _by_chip_cols(g):
    r, cols = g.shape
    return g.reshape(r, N_CHIPS, cols // N_CHIPS).transpose(1, 0, 2)


def _from_chip_cols(s):
    n, r, cols = s.shape
    return s.transpose(1, 0, 2).reshape(r, n * cols)


def kernel(x, p, w_in, b_f, pre_gain, post_gain, conv_w, conv_b, w_rgate, b_rgate, w_igate, b_igate, lru_lambda, attn_out_gain, lru_out_gain, w_out, w_ple, ple_gain, w_ple_gate, b_ple_gate, loss_target, m_w_in, m_b_f, m_pre_gain, m_post_gain, m_conv_w, m_conv_b, m_w_rgate, m_b_rgate, m_w_igate, m_b_igate, m_lru_lambda, m_attn_out_gain, m_lru_out_gain, m_w_out, m_w_ple, m_ple_gain, m_w_ple_gate, m_b_ple_gate, v_w_in, v_b_f, v_pre_gain, v_post_gain, v_conv_w, v_conv_b, v_w_rgate, v_b_rgate, v_w_igate, v_b_igate, v_lru_lambda, v_attn_out_gain, v_lru_out_gain, v_w_out, v_w_ple, v_ple_gain, v_w_ple_gate, v_b_ple_gate):
    w = dict(w_in=w_in, b_f=b_f, pre_gain=pre_gain, post_gain=post_gain, conv_w=conv_w, conv_b=conv_b,
             w_rgate=w_rgate, b_rgate=b_rgate, w_igate=w_igate, b_igate=b_igate, lru_lambda=lru_lambda,
             attn_out_gain=attn_out_gain, lru_out_gain=lru_out_gain, w_out=w_out, w_ple=w_ple, ple_gain=ple_gain,
             w_ple_gate=w_ple_gate, b_ple_gate=b_ple_gate)
    m = dict(w_in=m_w_in, b_f=m_b_f, pre_gain=m_pre_gain, post_gain=m_post_gain, conv_w=m_conv_w, conv_b=m_conv_b,
             w_rgate=m_w_rgate, b_rgate=m_b_rgate, w_igate=m_w_igate, b_igate=m_b_igate, lru_lambda=m_lru_lambda,
             attn_out_gain=m_attn_out_gain, lru_out_gain=m_lru_out_gain, w_out=m_w_out, w_ple=m_w_ple,
             ple_gain=m_ple_gain, w_ple_gate=m_w_ple_gate, b_ple_gate=m_b_ple_gate)
    v = dict(w_in=v_w_in, b_f=v_b_f, pre_gain=v_pre_gain, post_gain=v_post_gain, conv_w=v_conv_w, conv_b=v_conv_b,
             w_rgate=v_w_rgate, b_rgate=v_b_rgate, w_igate=v_w_igate, b_igate=v_b_igate, lru_lambda=v_lru_lambda,
             attn_out_gain=v_attn_out_gain, lru_out_gain=v_lru_out_gain, w_out=v_w_out, w_ple=v_w_ple,
             ple_gain=v_ple_gain, w_ple_gate=v_w_ple_gate, b_ple_gate=v_b_ple_gate)
    xi, yi, ci = _position()
    chip = 2 * xi + yi

    w_in_t, m_in_t, v_in_t = (jnp.swapaxes(t[0], 0, 1) for t in (w_in, m_w_in, v_w_in))
    window = jnp.pad(w_in_t.astype(BF16), ((0, W_ROWS - SHARD_ROWS), (0, 0)))

    (st_in,) = _gather_shards([window], [])
    w_a, w_f, w_b = _assemble_w_in(st_in)
    late_shards = (w_out[0].astype(BF16), w_ple[0].astype(BF16), w_ple_gate[0].astype(BF16), conv_w[0])

    def early_reduce(local):
        parts = [local[0].reshape(N_CHIPS, DMIX // N_CHIPS, D), _by_chip_cols(local[1]),
                 local[2].reshape(N_CHIPS, D // N_CHIPS, D)]
        return _pair_sum(parts, _pair_exchange(parts), ci)

    loss, grad_x, g = _local_step(
        x[0], p[0, 0], loss_target[0], w_a, w_f, w_b, late_shards, b_f, pre_gain, post_gain,
        conv_b, w_rgate[0], b_rgate, w_igate[0], b_igate, lru_lambda, attn_out_gain, lru_out_gain, ple_gain,
        b_ple_gate, gather_late=True, early_reduce=early_reduce,
        w_in_reduce=lambda grad_t: _pair_sum_windows(grad_t, _pair_exchange_windows(grad_t), ci))

    sums = [g["w_in_t"][0]] + [g[n][0] for n in SHARDED[1:]]
    recv = [g["w_in_t"][1]] + [g[n][1] for n in SHARDED[1:]]
    halves = [_chip_sum(sums[a], recv[a], chip, "chip_sum_%d" % a) for a in range(4)]
    theirs = _pair_swap(halves)
    full = [jnp.concatenate([jnp.where(ci == 0, a, b), jnp.where(ci == 0, b, a)], axis=0)
            for a, b in zip(halves, theirs)]
    red = dict(zip(SHARDED, full))
    red["w_in"] = lax.dynamic_slice_in_dim(red["w_in"], 2 * chip, SHARD_ROWS, axis=0)

    rows = [jnp.pad(g["b_f"], ((0, 0), (0, D - H)))] + [g[n] for n in SMALL_ROWS[1:]]
    rows.append(jnp.pad(loss.reshape(1, 1), ((0, 0), (0, D - 1))))
    packed = jnp.concatenate([g["w_rgate"].reshape(NB * LANES, LANES), g["w_igate"].reshape(NB * LANES, LANES),
                              jnp.concatenate(rows, axis=0).reshape(LANES, LANES)], axis=0)
    summed = _allreduce_small(packed)
    red["w_rgate"] = summed[:D].reshape(1, NB, LANES, LANES)
    red["w_igate"] = summed[D:2 * D].reshape(1, NB, LANES, LANES)
    vec = summed[2 * D:].reshape(16, D)
    loss = vec[15, 0]
    r0 = 0
    for n in SMALL_ROWS:
        nr = 4 if n == "conv_w" else 1
        red[n] = vec[r0:r0 + nr]
        r0 += nr
    red["b_f"] = red["b_f"][:, :H]
    red["conv_w"] = lax.dynamic_slice_in_dim(red["conv_w"], chip * (D // N_CHIPS), D // N_CHIPS, axis=1)[None]

    delta, new_m, new_v = {}, {}, {}
    outs_in = _adamw_big(red["w_in"], w_in_t, m_in_t, v_in_t, "adamw_w_in")
    delta["w_in"], new_m["w_in"], new_v["w_in"] = (jnp.swapaxes(t, 0, 1)[None] for t in outs_in)
    red["w_in"] = jnp.swapaxes(red["w_in"], 0, 1)[None]
    for n in SHARDED[1:]:
        delta[n], new_m[n], new_v[n] = (t[None] for t in _adamw_big(red[n], w[n][0], m[n][0], v[n][0], "adamw_" + n))
        red[n] = red[n][None]
    small = [n for n in WEIGHTS if n not in SHARDED]
    outs = _adamw_small([red[n] for n in small], [w[n] for n in small], [m[n] for n in small],
                        [v[n] for n in small])
    ns = len(small)
    for a, n in enumerate(small):
        delta[n], new_m[n], new_v[n] = outs[a], outs[ns + a], outs[2 * ns + a]

    return (loss, grad_x[None], *[red[n] for n in WEIGHTS], *[delta[n] for n in WEIGHTS],
            *[new_m[n] for n in WEIGHTS], *[new_v[n] for n in WEIGHTS])
```

```python
import functools

import jax
import jax.numpy as jnp
import numpy as np
from jax import lax
from jax.experimental import pallas as pl
from jax.experimental.pallas import tpu as pltpu

F32 = jnp.float32
BF16 = jnp.bfloat16

D = 1024
H = 8
DH = 128
NB = 8
DPLE = 256
DMIX = 2 * D
D_IN = 4 * D + H + 2 * D
FL0 = 3 * D
RMS_EPS = 1e-6
LRU_C = 8.0
NEG = -1e30
LANES = 128
SUBLANES = 8

ADAM_LR = 0.001
ADAM_B1 = 0.9
ADAM_B2 = 0.999
ADAM_EPS = 1e-08
ADAM_WD = 0.01
ADAM_STEP = 10

TM = 256
TA = 512
FWD_HEADS = 4
BWD_HEADS = 2
VMEM_BIG = 56 * 1024 * 1024
VMEM_MID = 40 * 1024 * 1024

MESH = pl.DeviceIdType.MESH
N_CHIPS = 4
N_DEV = 8


def _call(body, *, out_shape, in_hbm=True, **kwargs):
    if not in_hbm:
        return pl.pallas_call(body, out_shape=out_shape, **kwargs)

    def pin(shape):
        return pltpu.HBM(shape.shape, shape.dtype) if isinstance(shape, jax.ShapeDtypeStruct) else shape

    fn = pl.pallas_call(body, out_shape=jax.tree.map(pin, out_shape), **kwargs)

    def run(*args):
        return fn(*[a if a.dtype == jnp.int32 else pltpu.with_memory_space_constraint(a, pltpu.HBM) for a in args])

    return run


def _cparams(sem, vmem=VMEM_MID):
    return pltpu.CompilerParams(dimension_semantics=sem, vmem_limit_bytes=vmem)


def _sigmoid(x):
    return 0.5 * jnp.tanh(0.5 * x) + 0.5


def _rstd(x):
    return lax.rsqrt(jnp.mean(x * x, axis=-1, keepdims=True) + RMS_EPS)


def _rms_bwd(t, xhat, rstd):
    return rstd * (t - xhat * jnp.mean(t * xhat, axis=-1, keepdims=True))


def _dot(a, b):
    return jnp.dot(a, b, preferred_element_type=F32)


def _dot_nt(a, b):
    return lax.dot_general(a, b, (((1,), (1,)), ((), ())), preferred_element_type=F32)


def _dot_tn(a, b):
    return lax.dot_general(a, b, (((0,), (0,)), ((), ())), preferred_element_type=F32)


def _dot_exact(a, b):
    return jnp.dot(a, b, preferred_element_type=F32, precision=lax.Precision.HIGHEST)


def _shift_down(x, j, halo):
    rolled = pltpu.roll(x, j, 0)
    row = lax.broadcasted_iota(jnp.int32, halo.shape, 0)
    top = jnp.where(row < j, pltpu.roll(halo, j, 0), rolled[:SUBLANES])
    return jnp.concatenate([top, rolled[SUBLANES:]], axis=0)


def _shift_up(x, j, nxt):
    tm = x.shape[0]
    rolled = pltpu.roll(x, tm - j, 0)
    row = lax.broadcasted_iota(jnp.int32, nxt.shape, 0)
    bot = jnp.where(row >= SUBLANES - j, pltpu.roll(nxt, SUBLANES - j, 0), rolled[tm - SUBLANES:])
    return jnp.concatenate([rolled[:tm - SUBLANES], bot], axis=0)


def _scan_fwd_into(a, u, carry, h_ref):
    tm, width = a.shape
    groups = (tm // SUBLANES, SUBLANES, width)
    a, u = a.reshape(groups), u.reshape(groups)
    sub = lax.broadcasted_iota(jnp.int32, groups, 1)
    d = 1
    while d < SUBLANES:
        keep = sub >= d
        a_s = jnp.where(keep, pltpu.roll(a, d, 1), 1.0)
        u_s = jnp.where(keep, pltpu.roll(u, d, 1), 0.0)
        u = u + a * u_s
        a = a * a_s
        d *= 2
    a, u = a.reshape(tm, width), u.reshape(tm, width)
    for g in range(tm // SUBLANES):
        rows = slice(g * SUBLANES, (g + 1) * SUBLANES)
        h_ref[rows, :] = u[rows] + a[rows] * carry
        carry = h_ref[(g + 1) * SUBLANES - 1:(g + 1) * SUBLANES, :]
    return carry


def _scan_bwd_into(b, u, g_ref):
    tm, width = b.shape
    groups = (tm // SUBLANES, SUBLANES, width)
    b, u = b.reshape(groups), u.reshape(groups)
    sub = lax.broadcasted_iota(jnp.int32, groups, 1)
    d = 1
    while d < SUBLANES:
        keep = sub < SUBLANES - d
        b_s = jnp.where(keep, pltpu.roll(b, SUBLANES - d, 1), 1.0)
        u_s = jnp.where(keep, pltpu.roll(u, SUBLANES - d, 1), 0.0)
        u = u + b * u_s
        b = b * b_s
        d *= 2
    b, u = b.reshape(tm, width), u.reshape(tm, width)
    nxt = jnp.zeros((1, width), F32)
    for g in reversed(range(tm // SUBLANES)):
        rows = slice(g * SUBLANES, (g + 1) * SUBLANES)
        g_ref[rows, :] = u[rows] + b[rows] * nxt
        nxt = g_ref[g * SUBLANES:g * SUBLANES + 1, :]


def _gate_pre(xc, w_ref):
    outs = []
    for n in range(NB):
        outs.append(_dot(xc[:, n * LANES:(n + 1) * LANES].astype(BF16), w_ref[n]))
    return jnp.concatenate(outs, axis=1)


def _gate_pre_t(d, w_ref):
    outs = []
    for n in range(NB):
        outs.append(_dot_nt(d[:, n * LANES:(n + 1) * LANES].astype(BF16), w_ref[n]))
    return jnp.concatenate(outs, axis=1)


def _softplus_neg(lam):
    return jnp.maximum(-lam, 0.0) + jnp.log(1.0 + jnp.exp(-jnp.abs(lam)))


def _row_spec(tm, width):
    return pl.BlockSpec((tm, width), lambda i: (i, 0))


def _const_spec(shape):
    nd = len(shape)
    return pl.BlockSpec(shape, lambda *_: (0,) * nd)


def _weight_spec(shape):
    nd = len(shape)
    return pl.BlockSpec(shape, lambda *_: (0,) * nd, pipeline_mode=pl.Buffered(1))


AUG = 2 * DH
LOG2E = 1.4426950408889634
LN2 = 0.6931471805599453
Q_SCALE = DH ** -0.5 * LOG2E


def _split3(x):
    hi = x.astype(BF16)
    r1 = x - hi.astype(F32)
    mid = r1.astype(BF16)
    lo = (r1 - mid.astype(F32)).astype(BF16)
    return hi, mid, lo


def _extras(col, ones_from):
    t = col.shape[0]
    hi, mid, lo = _split3(jnp.broadcast_to(col, (t, LANES)))
    lane = lax.broadcasted_iota(jnp.int32, (t, LANES), 1)
    rest = jnp.zeros((t, LANES), BF16)
    if ones_from is not None:
        rest = jnp.where((lane >= ones_from) & (lane < ones_from + 3), 1.0, 0.0).astype(BF16)
    return jnp.where(lane == 0, hi, jnp.where(lane == 1, mid, jnp.where(lane == 2, lo, rest)))


def _selectors():
    sel_q = np.zeros((3 * LANES, H * LANES), np.float32)
    sel_k = np.zeros((3 * LANES, H * LANES), np.float32)
    for hd in range(H):
        for piece in range(3):
            sel_q[piece * LANES + hd, hd * LANES + piece] = 1.0
            sel_k[piece * LANES + hd, hd * LANES + 3 + piece] = -1.0
    return jnp.asarray(sel_q, BF16), jnp.asarray(sel_k, BF16)


def _in_proj(x, pre_gain, w_a, w_f, w_b, b_f_pad):
    T = x.shape[0]
    tm = TM
    sel_q, sel_k = _selectors()

    def body(x_ref, g_ref, wa_ref, wf_ref, wb_ref, bf_ref, sq_ref, sk_ref,
             xn_ref, qa_ref, ka_ref, va_ref, ga_ref, xl_ref, gl_ref, flb_ref, vt_ref, c_s, carry):
        @pl.when(pl.program_id(0) == 0)
        def _():
            carry[...] = jnp.zeros_like(carry)

        xv = x_ref[...]
        xn = (xv * _rstd(xv) * g_ref[...]).astype(BF16)
        xn_ref[...] = xn
        for s, o_ref in enumerate((ga_ref, xl_ref, gl_ref)):
            o_ref[...] = _dot_nt(xn, wb_ref[s * D:(s + 1) * D, :]).astype(o_ref.dtype)
        flb = _dot_nt(xn, wf_ref[...]) + bf_ref[...]
        flb_ref[...] = flb
        lane = lax.broadcasted_iota(jnp.int32, flb.shape, 1)
        ls = jnp.where(lane < H, jnp.minimum(flb, 0.0) - jnp.log(1.0 + jnp.exp(-jnp.abs(flb))), 0.0)
        r = lax.broadcasted_iota(jnp.int32, (tm, tm), 0)
        c = lax.broadcasted_iota(jnp.int32, (tm, tm), 1)
        cs = _dot_exact((c <= r).astype(F32), ls) + carry[...]
        c_s[...] = cs
        carry[...] = c_s[tm - 1:tm, :]

        pieces = jnp.concatenate(_split3(cs * LOG2E), axis=1)
        ones_q = jnp.where((lane >= 3) & (lane < 6), 1.0, 0.0)
        ones_k = jnp.where(lane < 3, 1.0, 0.0)
        zq = _dot_nt(xn, wa_ref[0:D, :]) * Q_SCALE
        zk = _dot_nt(xn, wa_ref[D:2 * D, :])
        zv = _dot_nt(xn, wa_ref[2 * D:3 * D, :])
        ex_q = _dot(pieces, sq_ref[...])
        ex_k = _dot(pieces, sk_ref[...])
        for hd in range(H):
            head = slice(hd * DH, (hd + 1) * DH)
            lo, hi = hd * AUG, hd * AUG + DH
            qa_ref[:, lo:hi] = zq[:, head].astype(BF16)
            qa_ref[:, hi:hi + DH] = (ex_q[:, head] + ones_q).astype(BF16)
            ka_ref[:, lo:hi] = zk[:, head].astype(BF16)
            ka_ref[:, hi:hi + DH] = (ex_k[:, head] + ones_k).astype(BF16)
            va_ref[:, lo:hi] = zv[:, head].astype(BF16)
            va_ref[:, hi:hi + DH] = ones_k.astype(BF16)
            vt_ref[lo:hi, :] = jnp.transpose(zv[:, head]).astype(BF16)
            vt_ref[hi:hi + DH, :] = jnp.where(lax.broadcasted_iota(jnp.int32, (DH, tm), 0) < 3, 1.0, 0.0).astype(BF16)

    bf = jax.ShapeDtypeStruct((T, D), BF16)
    aug = jax.ShapeDtypeStruct((T, H * AUG), BF16)
    f32 = jax.ShapeDtypeStruct((T, D), F32)
    sel_spec = _const_spec((3 * LANES, H * LANES))
    return _call(
        body, name="in_proj", grid=(T // tm,),
        in_specs=[_row_spec(tm, D), _const_spec((1, D)), _const_spec((3 * D, D)), _const_spec((LANES, D)),
                  _const_spec((3 * D, D)), _const_spec((1, LANES)), sel_spec, sel_spec],
        out_specs=[_row_spec(tm, D)] + [_row_spec(tm, H * AUG)] * 3 + [_row_spec(tm, D)] * 3 + [_row_spec(tm, LANES)]
        + [pl.BlockSpec((H * AUG, tm), lambda i: (0, i))],
        out_shape=[bf, aug, aug, aug, f32, f32, f32, jax.ShapeDtypeStruct((T, LANES), F32),
                   jax.ShapeDtypeStruct((H * AUG, T), BF16)],
        scratch_shapes=[pltpu.VMEM((tm, LANES), F32), pltpu.VMEM((1, LANES), F32)],
        compiler_params=_cparams(("arbitrary",), VMEM_BIG),
    )(x, pre_gain, w_a, w_f, w_b, b_f_pad, sel_q, sel_k)


def _causal_pairs(n, q_major):
    if q_major:
        pairs = [(qi, ki) for qi in range(n) for ki in range(qi + 1)]
    else:
        pairs = [(ki, qi) for ki in range(n) for qi in range(ki, n)]
    return (jnp.asarray([a for a, _ in pairs], jnp.int32), jnp.asarray([b for _, b in pairs], jnp.int32))


def _attn_fwd(q_aug, k_aug, vt_aug, shards=(), whole=()):
    T = q_aug.shape[0]
    t = TA
    n = T // t
    hp = FWD_HEADS
    heads = range(hp)
    qi_tab, ki_tab = _causal_pairs(n, q_major=True)
    na, nall = len(shards), len(shards) + len(whole)
    n_h, n_j = H // hp, qi_tab.shape[0]

    def body(qi_ref, ki_ref, q_ref, k_ref, vt_ref, *rest):
        srcs, rest = rest[:nall], rest[nall:]
        o_ref, qx_ref = rest[:2]
        dsts, rest = rest[2:2 + nall], rest[2 + nall:]
        m_s, acc_s = rest[:2]
        h = pl.program_id(0)
        j = pl.program_id(1)
        qi = qi_ref[j]
        ki = ki_ref[j]

        if nall:
            gather = _GatherPlan(srcs, dsts, rest[2:], na)
            pl.when((h == 0) & (j == 0))(gather.send)
            pl.when((h == n_h - 1) & (j == 0))(gather.forward)
            pl.when((h == n_h - 1) & (j == n_j - 1))(gather.finish)

        @pl.when(ki == 0)
        def _():
            m_s[...] = jnp.full(m_s.shape, NEG, F32)
            acc_s[...] = jnp.zeros_like(acc_s)

        def step(on_diagonal):
            cols = [slice(a * AUG, (a + 1) * AUG) for a in heads]
            if on_diagonal:
                krow = lax.broadcasted_iota(jnp.int32, (t, t), 0)
                qcol = lax.broadcasted_iota(jnp.int32, (t, t), 1)
            def logits(a):
                st = _dot_nt(k_ref[:, cols[a]], q_ref[:, cols[a]])
                return jnp.where(krow <= qcol, st, NEG) if on_diagonal else st

            st_next = logits(0)
            for a in heads:
                st = st_next
                if a + 1 < hp:
                    st_next = logits(a + 1)
                m_prev = m_s[a]
                m_new = jnp.maximum(m_prev, jnp.max(st, axis=0, keepdims=True))
                pt = jnp.exp2(st - m_new).astype(BF16)
                acc_s[a] = jnp.exp2(m_prev - m_new) * acc_s[a] + _dot(vt_ref[cols[a], :], pt)
                m_s[a] = m_new

        @pl.when(ki < qi)
        def _():
            step(False)

        @pl.when(ki == qi)
        def _():
            step(True)
            piece = lax.broadcasted_iota(jnp.int32, (DH, t), 0)
            for a in heads:
                l = acc_s[a, DH:DH + 1, :]
                ex = jnp.transpose(q_ref[:, a * AUG + DH:(a + 1) * AUG].astype(F32))
                c2 = jnp.sum(jnp.where(piece < 3, ex, 0.0), axis=0, keepdims=True)
                hi, mid, lo = _split3(jnp.broadcast_to(c2 - (m_s[a] + jnp.log(l) * LOG2E), (DH, t)))
                ones = jnp.where((piece >= 3) & (piece < 6), 1.0, 0.0).astype(BF16)
                ex_t = jnp.where(piece == 0, hi, jnp.where(piece == 1, mid, jnp.where(piece == 2, lo, ones)))
                o_ref[:, a * DH:(a + 1) * DH] = jnp.transpose(acc_s[a, :DH, :] / l)
                qx_ref[:, a * DH:(a + 1) * DH] = jnp.transpose(ex_t.astype(F32)).astype(BF16)

    q_spec = pl.BlockSpec((t, hp * AUG), lambda h, j, qi_ref, ki_ref: (qi_ref[j], h))
    k_spec = pl.BlockSpec((t, hp * AUG), lambda h, j, qi_ref, ki_ref: (ki_ref[j], h))
    vt_spec = pl.BlockSpec((hp * AUG, t), lambda h, j, qi_ref, ki_ref: (h, ki_ref[j]))
    out_spec = pl.BlockSpec((t, hp * DH), lambda h, j, qi_ref, ki_ref: (qi_ref[j], h))
    arrs = list(shards) + list(whole)
    grid_spec = pltpu.PrefetchScalarGridSpec(
        num_scalar_prefetch=2, grid=(n_h, n_j),
        in_specs=[q_spec, k_spec, vt_spec] + [HBM_SPEC] * nall, out_specs=[out_spec, out_spec] + [HBM_SPEC] * nall,
        scratch_shapes=[pltpu.VMEM((hp, 1, t), F32), pltpu.VMEM((hp, AUG, t), F32)]
        + (_gather_semaphores(na, nall) if nall else []))
    outs = _call(
        body, name="attn_fwd", grid_spec=grid_spec,
        out_shape=[jax.ShapeDtypeStruct((T, D), F32), jax.ShapeDtypeStruct((T, D), BF16)] + _gather_out_shapes(arrs),
        compiler_params=_cparams(("arbitrary", "arbitrary"), VMEM_BIG),
    )(qi_tab, ki_tab, q_aug, k_aug, vt_aug, *arrs)
    return outs[0], outs[1], _place_own(outs[2:], arrs)


def _lru_gates(xc, wr_ref, br_ref, wi_ref, bi_ref, lam_ref):
    r = _sigmoid(_gate_pre(xc, wr_ref) + br_ref[...])
    ig = _sigmoid(_gate_pre(xc, wi_ref) + bi_ref[...])
    sp = _softplus_neg(lam_ref[...])
    la = (-LRU_C) * r * sp
    a = jnp.exp(la)
    y = -jnp.tanh(la) * (a * a + 1.0)
    return r, ig, sp, a, jnp.sqrt(y), lax.rsqrt(y)


def _branches_fwd(o, g_attn, x_lru, g_lru, gain_a, gain_l, conv_w, conv_b, w_r, b_r, w_i, b_i, lam):
    T = o.shape[0]
    tm = TM

    def body(o_ref, ga_ref, xl_ref, gl_ref, gna_ref, gnl_ref, cw_ref, cb_ref, wr_ref, br_ref, wi_ref, bi_ref,
             lam_ref, ycat_ref, xc_ref, h_ref, halo_s, hc_s):
        @pl.when(pl.program_id(0) == 0)
        def _():
            halo_s[...] = jnp.zeros_like(halo_s)
            hc_s[...] = jnp.zeros_like(hc_s)

        ov = o_ref[...]
        ga = ga_ref[...]
        ya = ov * _rstd(ov) * gna_ref[...] * (ga * _sigmoid(ga))
        ycat_ref[:, :D] = ya.astype(BF16)

        xl = xl_ref[...]
        halo = halo_s[...]
        xc = xl * cw_ref[3:4, :] + cb_ref[...]
        for j in range(3):
            xc = xc + _shift_down(xl, 3 - j, halo) * cw_ref[j:j + 1, :]
        halo_s[...] = xl_ref[tm - SUBLANES:tm, :]
        xc_ref[...] = xc

        _, ig, _, a, sq, _ = _lru_gates(xc, wr_ref, br_ref, wi_ref, bi_ref, lam_ref)
        u = sq * (ig * xc)
        hc_s[...] = _scan_fwd_into(a, u, hc_s[...], h_ref)
        hh = h_ref[...]

        gl = gl_ref[...]
        yl = hh * _rstd(hh) * gnl_ref[...] * (gl * _sigmoid(gl))
        ycat_ref[:, D:] = yl.astype(BF16)

    vec = _const_spec((1, D))
    wspec = _const_spec((NB, LANES, LANES))
    return _call(
        body, name="branches_fwd", grid=(T // tm,),
        in_specs=[_row_spec(tm, D)] * 4 + [vec, vec, _const_spec((4, D)), vec, wspec, vec, wspec, vec, vec],
        out_specs=[_row_spec(tm, DMIX), _row_spec(tm, D), _row_spec(tm, D)],
        out_shape=[jax.ShapeDtypeStruct((T, DMIX), BF16), jax.ShapeDtypeStruct((T, D), F32),
                   jax.ShapeDtypeStruct((T, D), F32)],
        scratch_shapes=[pltpu.VMEM((SUBLANES, D), F32), pltpu.VMEM((1, D), F32)],
        compiler_params=_cparams(("arbitrary",)),
    )(o, g_attn, x_lru, g_lru, gain_a, gain_l, conv_w, conv_b, w_r, b_r, w_i, b_i, lam)


def _tail(ycat, x, p, tgt, w_out, post_gain, w_ple, ple_gain, w_gate, b_gate):
    T = x.shape[0]
    tm = TM

    def body(ycat_ref, x_ref, p_ref, t_ref, wo_ref, pg_ref, wp_ref, eg_ref, wg_ref, bg_ref,
             dh1_ref, dycat_ref, dmix_ref, h1b_ref, dgp_ref, pb_ref, dpe_ref, acc_ref):
        @pl.when(pl.program_id(0) == 0)
        def _():
            acc_ref[...] = jnp.zeros_like(acc_ref)

        mix = _dot(ycat_ref[...], wo_ref[...])
        rstd_m = _rstd(mix)
        mhat = mix * rstd_m
        h1 = x_ref[...] + mhat * pg_ref[...]
        pb = p_ref[...].astype(BF16)
        pb_ref[...] = pb
        pe = _dot(pb, wp_ref[...])
        rstd_p = _rstd(pe)
        pehat = pe * rstd_p
        e = pehat * eg_ref[...]
        h1b = h1.astype(BF16)
        h1b_ref[...] = h1b
        gate = _sigmoid(_dot(h1b, wg_ref[...]) + bg_ref[...])
        diff = (h1 + gate * e) - t_ref[...]

        dy = diff * (1.0 / D)
        de = dy * gate
        dgp = (dy * e) * gate * (1.0 - gate)
        dgpb = dgp.astype(BF16)
        dgp_ref[...] = dgpb
        dh1 = dy + _dot_nt(dgpb, wg_ref[...])
        dh1_ref[...] = dh1
        dpe_ref[...] = _rms_bwd(de * eg_ref[...], pehat, rstd_p).astype(BF16)
        dmix = _rms_bwd(dh1 * pg_ref[...], mhat, rstd_m).astype(BF16)
        dmix_ref[...] = dmix
        dycat_ref[...] = _dot_nt(dmix, wo_ref[...])

        acc_ref[0:1, :] += jnp.sum(dh1 * mhat, axis=0, keepdims=True)
        acc_ref[1:2, :] += jnp.sum(de * pehat, axis=0, keepdims=True)
        acc_ref[2:3, :] += jnp.sum(dgp, axis=0, keepdims=True)
        acc_ref[3:4, :] += jnp.sum(diff * diff, axis=0, keepdims=True) * (0.5 / D)

    vec = _const_spec((1, D))
    bf = jax.ShapeDtypeStruct((T, D), BF16)
    return _call(
        body, name="tail", grid=(T // tm,),
        in_specs=[_row_spec(tm, DMIX), _row_spec(tm, D), _row_spec(tm, DPLE), _row_spec(tm, D),
                  _const_spec((DMIX, D)), vec, _const_spec((DPLE, D)), vec, _const_spec((D, D)), vec],
        out_specs=[_row_spec(tm, D), _row_spec(tm, DMIX), _row_spec(tm, D), _row_spec(tm, D), _row_spec(tm, D),
                   _row_spec(tm, DPLE), _row_spec(tm, D), _const_spec((SUBLANES, D))],
        out_shape=[jax.ShapeDtypeStruct((T, D), F32), jax.ShapeDtypeStruct((T, DMIX), F32), bf, bf, bf,
                   jax.ShapeDtypeStruct((T, DPLE), BF16), bf, jax.ShapeDtypeStruct((SUBLANES, D), F32)],
        compiler_params=_cparams(("arbitrary",), VMEM_BIG),
    )(ycat, x, p, tgt, w_out, post_gain, w_ple, ple_gain, w_gate, b_gate)


def _branches_bwd(dycat, o, g_attn, h, g_lru, gain_a, gain_l):
    T = o.shape[0]
    tm = TM

    def body(dy_ref, o_ref, ga_ref, h_ref, gl_ref, gna_ref, gnl_ref,
             do_ref, dga_ref, dgl_ref, dh_ref, acc_ref):
        @pl.when(pl.program_id(0) == 0)
        def _():
            acc_ref[...] = jnp.zeros_like(acc_ref)

        def branch(val, g, gain, dyv):
            rstd = _rstd(val)
            vhat = val * rstd
            sig = _sigmoid(g)
            dn = dyv * (g * sig)
            dg = dyv * (vhat * gain) * (sig * (1.0 + g * (1.0 - sig)))
            dgain = jnp.sum(dn * vhat, axis=0, keepdims=True)
            return _rms_bwd(dn * gain, vhat, rstd), dg, dgain

        ov = o_ref[...]
        do, dga, dgain_a = branch(ov, ga_ref[...], gna_ref[...], dy_ref[:, :D])
        dga_ref[...] = dga.astype(BF16)
        prod = do * ov
        for hd in range(H):
            head = slice(hd * DH, (hd + 1) * DH)
            do_ref[:, hd * AUG:hd * AUG + DH] = do[:, head].astype(BF16)
            do_ref[:, hd * AUG + DH:(hd + 1) * AUG] = _extras(-jnp.sum(prod[:, head], axis=1, keepdims=True), None)

        dh, dgl, dgain_l = branch(h_ref[...], gl_ref[...], gnl_ref[...], dy_ref[:, D:])
        dh_ref[...] = dh
        dgl_ref[...] = dgl.astype(BF16)
        acc_ref[0:1, :] += dgain_a
        acc_ref[1:2, :] += dgain_l

    vec = _const_spec((1, D))
    bf = jax.ShapeDtypeStruct((T, D), BF16)
    return _call(
        body, name="branches_bwd", grid=(T // tm,),
        in_specs=[_row_spec(tm, DMIX)] + [_row_spec(tm, D)] * 4 + [vec, vec],
        out_specs=[_row_spec(tm, H * AUG), _row_spec(tm, D), _row_spec(tm, D), _row_spec(tm, D),
                   _const_spec((SUBLANES, D))],
        out_shape=[jax.ShapeDtypeStruct((T, H * AUG), BF16), bf, bf, jax.ShapeDtypeStruct((T, D), F32),
                   jax.ShapeDtypeStruct((SUBLANES, D), F32)],
        compiler_params=_cparams(("arbitrary",)),
    )(dycat, o, g_attn, h, g_lru, gain_a, gain_l)


def _lru_bwd(dh, h, xc, x_lru, conv_w, w_r, b_r, w_i, b_i, lam):
    T = dh.shape[0]
    tm = TM
    nt = T // tm
    per = tm // SUBLANES

    def body(dh_ref, h_ref, hprev_ref, xc_ref, xl_ref, cw_ref, wr_ref, br_ref, wi_ref, bi_ref, lam_ref,
             dxl_ref, dwr_ref, dwi_ref, acc_ref, carry_s, dxc_next_s, top_s, dht_s):
        i = pl.program_id(0)

        @pl.when(i == 0)
        def _():
            acc_ref[...] = jnp.zeros_like(acc_ref)
            dwr_ref[...] = jnp.zeros_like(dwr_ref)
            dwi_ref[...] = jnp.zeros_like(dwi_ref)
            carry_s[...] = jnp.zeros_like(carry_s)
            dxc_next_s[...] = jnp.zeros_like(dxc_next_s)

        inner = jnp.where(i == nt - 1, 0.0, 1.0)
        xc = xc_ref[...]
        r, ig, sp, a, sq, inv_sq = _lru_gates(xc, wr_ref, br_ref, wi_ref, bi_ref, lam_ref)

        row = lax.broadcasted_iota(jnp.int32, (tm, D), 0)
        u = dh_ref[...] + jnp.where(row == tm - 1, carry_s[...], 0.0)
        _scan_bwd_into(pltpu.roll(a, tm - 1, 0), u, dht_s)
        dht = dht_s[...]
        top_s[...] = a[:SUBLANES, :] * dht[:SUBLANES, :]
        carry_s[...] = top_s[0:1, :]

        hprev = hprev_ref[...] * inner
        da = dht * _shift_down(h_ref[...], 1, hprev)
        dig = dht * sq * xc
        dxc = dht * sq * ig
        dsq = dht * ig * xc
        dla = da * a - dsq * (a * a) * inv_sq
        dr = dla * ((-LRU_C) * sp)
        dpr = dr * r * (1.0 - r)
        dpi = dig * ig * (1.0 - ig)
        for n in range(NB):
            blk = slice(n * LANES, (n + 1) * LANES)
            xcb = xc[:, blk].astype(BF16)
            dwr_ref[n] += _dot_tn(xcb, dpr[:, blk].astype(BF16))
            dwi_ref[n] += _dot_tn(xcb, dpi[:, blk].astype(BF16))
        dxc = dxc + _gate_pre_t(dpr, wr_ref) + _gate_pre_t(dpi, wi_ref)

        xl = xl_ref[...]
        nxt = dxc_next_s[...]
        dxl = dxc * cw_ref[3:4, :]
        acc_ref[3:4, :] += jnp.sum(dxc * xl, axis=0, keepdims=True)
        for j in range(3):
            ahead = _shift_up(dxc, 3 - j, nxt)
            dxl = dxl + ahead * cw_ref[j:j + 1, :]
            acc_ref[j:j + 1, :] += jnp.sum(ahead * xl, axis=0, keepdims=True)
        dxc_next_s[...] = dxc[:SUBLANES, :]
        dxl_ref[...] = dxl.astype(BF16)

        acc_ref[4:5, :] += jnp.sum(dxc, axis=0, keepdims=True)
        acc_ref[5:6, :] += jnp.sum(dpr, axis=0, keepdims=True)
        acc_ref[6:7, :] += jnp.sum(dpi, axis=0, keepdims=True)
        acc_ref[7:8, :] += jnp.sum(dla * ((-LRU_C) * r), axis=0, keepdims=True)

        @pl.when(i == nt - 1)
        def _():
            lam_v = lam_ref[...]
            acc_ref[7:8, :] = acc_ref[7:8, :] * (-_sigmoid(-lam_v))

    rev = pl.BlockSpec((tm, D), lambda i: (nt - 1 - i, 0))
    prev8 = pl.BlockSpec((SUBLANES, D), lambda i: (jnp.maximum((nt - 1 - i) * per - 1, 0), 0))
    vec = _const_spec((1, D))
    wspec = _const_spec((NB, LANES, LANES))
    bf = jax.ShapeDtypeStruct((T, D), BF16)
    return _call(
        body, name="lru_bwd", grid=(nt,),
        in_specs=[rev, rev, prev8, rev, rev, _const_spec((4, D)), wspec, vec, wspec, vec, vec],
        out_specs=[rev, wspec, wspec, _const_spec((SUBLANES, D))],
        out_shape=[bf, jax.ShapeDtypeStruct((NB, LANES, LANES), F32), jax.ShapeDtypeStruct((NB, LANES, LANES), F32),
                   jax.ShapeDtypeStruct((SUBLANES, D), F32)],
        scratch_shapes=[pltpu.VMEM((1, D), F32), pltpu.VMEM((SUBLANES, D), F32), pltpu.VMEM((SUBLANES, D), F32),
                        pltpu.VMEM((tm, D), F32)],
        compiler_params=_cparams(("arbitrary",)),
    )(dh, h, h, xc, x_lru, conv_w, w_r, b_r, w_i, b_i, lam)


def _chip_copies(srcs, dsts, send_sems, recv_sems):
    x, y, c = _position()
    chip = 2 * x + y
    na = len(srcs)
    return [pltpu.make_async_remote_copy(
        src_ref=srcs[a].at[2 * px + py], dst_ref=dsts[a].at[chip], send_sem=send_sems.at[j * na + a],
        recv_sem=recv_sems.at[j * na + a], device_id=(px, py, c), device_id_type=MESH)
        for j, (px, py) in enumerate(_other_chips(x, y)) for a in range(na)]


def _attn_bwd(q_aug, qx, k_aug, v_aug, do_aug, exchange=()):
    T = q_aug.shape[0]
    t = TA
    n = T // t
    hp = BWD_HEADS
    heads = range(hp)
    scale = DH ** -0.5
    ki_tab, qi_tab = _causal_pairs(n, q_major=False)
    last = ki_tab.shape[0] - 1
    ne = len(exchange)
    n_h = H // hp

    def body(ki_ref, qi_ref, q_ref, qx_ref, k_ref, v_ref, do_ref, *rest):
        sent, rest = rest[:ne], rest[ne:]
        dq_ref, dk_ref, dv_ref, dck_ref, dcq_ref = rest[:5]
        received, rest = rest[5:5 + ne], rest[5 + ne:]
        dq_s, dk_s, dv_s = rest[:3]
        j = pl.program_id(1)
        ki = ki_ref[j]
        qi = qi_ref[j]

        if ne:
            first_step = (pl.program_id(0) == 0) & (j == 0)
            last_step = (pl.program_id(0) == n_h - 1) & (j == last)

            @pl.when(first_step)
            def _():
                for cp in _chip_copies(sent, received, *rest[3:]):
                    cp.start()

            @pl.when(last_step)
            def _():
                for cp in _chip_copies(sent, received, *rest[3:]):
                    cp.wait()

        @pl.when(j == 0)
        def _():
            dq_s[...] = jnp.zeros_like(dq_s)

        @pl.when(qi == ki)
        def _():
            dk_s[...] = jnp.zeros_like(dk_s)
            dv_s[...] = jnp.zeros_like(dv_s)

        def step(on_diagonal):
            cols = [slice(a * AUG, (a + 1) * AUG) for a in heads]
            qb = [jnp.concatenate([q_ref[:, a * AUG:a * AUG + DH], qx_ref[:, a * DH:(a + 1) * DH]], axis=1)
                  for a in heads]
            if on_diagonal:
                krow = lax.broadcasted_iota(jnp.int32, (t, t), 0)
                qcol = lax.broadcasted_iota(jnp.int32, (t, t), 1)

            def scores(a):
                st = _dot_nt(k_ref[:, cols[a]], qb[a])
                dpd = _dot_nt(v_ref[:, cols[a]], do_ref[:, cols[a]])
                return (jnp.where(krow <= qcol, st, NEG) if on_diagonal else st), dpd

            off = pl.multiple_of(qi * t, t)
            ahead = scores(0)
            for a in heads:
                st, dpd = ahead
                if a + 1 < hp:
                    ahead = scores(a + 1)
                pt = jnp.exp2(st)
                dsb = (pt * dpd).astype(BF16)
                dv_s[a] += _dot(pt.astype(BF16), do_ref[:, a * AUG:a * AUG + DH])
                dk_s[a] += _dot(dsb, qb[a])
                dq_s[a, pl.ds(off, t), :] += _dot_tn(dsb, k_ref[:, cols[a]])

        @pl.when(qi > ki)
        def _():
            step(False)

        @pl.when(qi == ki)
        def _():
            step(True)

        @pl.when(qi == n - 1)
        def _():
            for a in heads:
                dk_ref[:, a * DH:(a + 1) * DH] = (dk_s[a, :, :DH] * LN2).astype(BF16)
                dv_ref[:, a * DH:(a + 1) * DH] = dv_s[a].astype(BF16)
                dck_ref[a] = jnp.broadcast_to(dk_s[a, :, DH + 3:DH + 4], (t, LANES))

        @pl.when(j == last)
        def _():
            for a in heads:
                dq_ref[:, a * DH:(a + 1) * DH] = (dq_s[a, :, :DH] * scale).astype(BF16)
                dcq_ref[a] = jnp.broadcast_to(dq_s[a, :, DH:DH + 1], (T, LANES))

    qside = pl.BlockSpec((t, hp * AUG), lambda h, j, ki_ref, qi_ref: (qi_ref[j], h))
    qxside = pl.BlockSpec((t, hp * DH), lambda h, j, ki_ref, qi_ref: (qi_ref[j], h))
    kside = pl.BlockSpec((t, hp * AUG), lambda h, j, ki_ref, qi_ref: (ki_ref[j], h))
    kout = pl.BlockSpec((t, hp * DH), lambda h, j, ki_ref, qi_ref: (ki_ref[j], h))
    bf = jax.ShapeDtypeStruct((T, D), BF16)
    sums = jax.ShapeDtypeStruct((H, T, LANES), F32)
    grid_spec = pltpu.PrefetchScalarGridSpec(
        num_scalar_prefetch=2, grid=(n_h, ki_tab.shape[0]),
        in_specs=[qside, qxside, kside, kside, qside] + [HBM_SPEC] * ne,
        out_specs=[pl.BlockSpec((T, hp * DH), lambda h, j, ki_ref, qi_ref: (0, h)), kout, kout,
                   pl.BlockSpec((hp, t, LANES), lambda h, j, ki_ref, qi_ref: (h, ki_ref[j], 0)),
                   pl.BlockSpec((hp, T, LANES), lambda h, j, ki_ref, qi_ref: (h, 0, 0))] + [HBM_SPEC] * ne,
        scratch_shapes=[pltpu.VMEM((hp, T, AUG), F32), pltpu.VMEM((hp, t, AUG), F32), pltpu.VMEM((hp, t, DH), F32)]
        + ([pltpu.SemaphoreType.DMA((3 * ne,)), pltpu.SemaphoreType.DMA((3 * ne,))] if ne else []))
    outs = _call(
        body, name="attn_bwd", grid_spec=grid_spec,
        out_shape=[bf, bf, bf, sums, sums] + [jax.ShapeDtypeStruct(s.shape, s.dtype) for s in exchange],
        compiler_params=_cparams(("arbitrary", "arbitrary"), VMEM_BIG),
    )(ki_tab, qi_tab, q_aug, qx, k_aug, v_aug, do_aug, *exchange)
    return (*outs[:5], list(outs[5:]))


def _fgate_bwd(dc_key, dc_query, flb):
    T = flb.shape[0]
    tm = TM
    nt = T // tm

    def body(dck_ref, dcq_ref, flb_ref, dfl_ref, acc_ref, carry, top_s):
        @pl.when(pl.program_id(0) == 0)
        def _():
            carry[...] = jnp.zeros_like(carry)
            acc_ref[...] = jnp.zeros_like(acc_ref)

        flb = flb_ref[...]
        lane = lax.broadcasted_iota(jnp.int32, flb.shape, 1)
        dc = jnp.zeros(flb.shape, F32)
        for hd in range(H):
            dc = dc + jnp.where(lane == hd, dcq_ref[hd] - dck_ref[hd], 0.0)
        r = lax.broadcasted_iota(jnp.int32, (tm, tm), 0)
        c = lax.broadcasted_iota(jnp.int32, (tm, tm), 1)
        dls = _dot_exact((c >= r).astype(F32), dc) + carry[...]
        top_s[...] = dls[:SUBLANES, :]
        carry[...] = top_s[0:1, :]
        dfl = jnp.where(lane < H, dls * _sigmoid(-flb), 0.0)
        dfl_ref[...] = dfl.astype(BF16)
        acc_ref[0:1, :] += jnp.sum(dfl, axis=0, keepdims=True)

    rev = pl.BlockSpec((tm, LANES), lambda i: (nt - 1 - i, 0))
    return _call(
        body, name="fgate_bwd", grid=(nt,),
        in_specs=[pl.BlockSpec((H, tm, LANES), lambda i: (0, nt - 1 - i, 0))] * 2 + [rev],
        out_specs=[rev, _const_spec((SUBLANES, LANES))],
        out_shape=[jax.ShapeDtypeStruct((T, LANES), BF16), jax.ShapeDtypeStruct((SUBLANES, LANES), F32)],
        scratch_shapes=[pltpu.VMEM((1, LANES), F32), pltpu.VMEM((SUBLANES, LANES), F32)],
        compiler_params=_cparams(("arbitrary",)),
    )(dc_key, dc_query, flb)


def _dx(dz, dfl, w_a, w_f, w_b, x, pre_gain, dh1, exchange=()):
    T = x.shape[0]
    tm = TM
    nt = T // tm
    ne = len(exchange)

    def body(*refs):
        dz_refs = refs[:6]
        dfl_ref, wa_ref, wf_ref, wb_ref, x_ref, g_ref, dh1_ref = refs[6:13]
        sent = refs[13:13 + ne]
        gx_ref, acc_ref = refs[13 + ne:15 + ne]
        received, sems = refs[15 + ne:15 + 2 * ne], refs[15 + 2 * ne:]

        @pl.when(pl.program_id(0) == 0)
        def _():
            acc_ref[...] = jnp.zeros_like(acc_ref)
            for cp in _chip_copies(sent, received, *sems) if ne else ():
                cp.start()

        if ne:
            @pl.when(pl.program_id(0) == nt - 1)
            def _():
                for cp in _chip_copies(sent, received, *sems):
                    cp.wait()

        dxn = _dot(dfl_ref[...], wf_ref[...])
        for s in range(3):
            dxn = dxn + _dot(dz_refs[s][...], wa_ref[s * D:(s + 1) * D, :])
            dxn = dxn + _dot(dz_refs[3 + s][...], wb_ref[s * D:(s + 1) * D, :])
        xv = x_ref[...]
        rstd = _rstd(xv)
        xhat = xv * rstd
        gx_ref[...] = dh1_ref[...] + _rms_bwd(dxn * g_ref[...], xhat, rstd)
        acc_ref[0:1, :] += jnp.sum(dxn * xhat, axis=0, keepdims=True)

    outs = _call(
        body, name="dx", grid=(nt,),
        in_specs=[_row_spec(tm, D)] * 6 + [_row_spec(tm, LANES), _weight_spec((3 * D, D)), _weight_spec((LANES, D)),
                                           _weight_spec((3 * D, D)), _row_spec(tm, D), _const_spec((1, D)),
                                           _row_spec(tm, D)] + [HBM_SPEC] * ne,
        out_specs=[_row_spec(tm, D), _const_spec((SUBLANES, D))] + [HBM_SPEC] * ne,
        out_shape=[jax.ShapeDtypeStruct((T, D), F32), jax.ShapeDtypeStruct((SUBLANES, D), F32)]
        + [jax.ShapeDtypeStruct(s.shape, s.dtype) for s in exchange],
        scratch_shapes=[pltpu.SemaphoreType.DMA((3 * ne,)), pltpu.SemaphoreType.DMA((3 * ne,))] if ne else [],
        compiler_params=_cparams(("arbitrary",), VMEM_BIG),
    )(*dz, dfl, w_a, w_f, w_b, x, pre_gain, dh1, *exchange)
    return outs[0], outs[1], list(outs[2:])


GRAD_ROWS = D_IN + SUBLANES


def _dw_in_segment(dz_s, xn, buf, s, bt):
    T = xn.shape[0]
    row0 = s * D + (H if s >= 3 else 0)

    def body(*refs):
        dz_ref, xn_ref, o_ref = refs[0], refs[1], refs[-1]

        @pl.when(pl.program_id(0) == 0)
        def _():
            o_ref[...] = jnp.zeros_like(o_ref)

        o_ref[...] += _dot_tn(dz_ref[...], xn_ref[...])

    tok = pl.BlockSpec((bt, D), lambda t: (t, 0))
    return _call(
        body, name="dw_in_%d" % s, grid=(T // bt,),
        in_specs=[tok, tok] + ([] if buf is None else [pl.BlockSpec(memory_space=pl.ANY)]),
        out_specs=pl.BlockSpec((pl.Element(D), pl.Element(D)), lambda t: (row0, 0)),
        out_shape=jax.ShapeDtypeStruct((GRAD_ROWS, D), F32),
        input_output_aliases={} if buf is None else {2: 0},
        compiler_params=_cparams(("arbitrary",)),
    )(*((dz_s, xn) if buf is None else (dz_s, xn, buf)))


def _dw_in_t(dz, dfl, xn, bt=2048):
    T = xn.shape[0]
    bt = min(bt, T)
    nt = T // bt
    main = None
    for s in range(6):
        main = _dw_in_segment(dz[s], xn, main, s, min(T, 2048))

    def f_body(dfl_ref, xn_ref, main_ref, o_ref, acc_s):
        p = pl.program_id(0)
        t = pl.program_id(1)

        @pl.when(t == 0)
        def _():
            acc_s[...] = jnp.zeros_like(acc_s)

        @pl.when(p == 0)
        def _():
            acc_s[...] += _dot_tn(dfl_ref[...], xn_ref[...])

        @pl.when(t == nt - 1)
        def _():
            o_ref[...] = acc_s[:SUBLANES, :]

    fl_block = FL0 // SUBLANES
    end_block = D_IN // SUBLANES
    return _call(
        f_body, name="dw_in_f", grid=(2, nt),
        in_specs=[pl.BlockSpec((bt, LANES), lambda p, t: (t, 0)), pl.BlockSpec((bt, D), lambda p, t: (t, 0)),
                  pl.BlockSpec(memory_space=pl.ANY)],
        out_specs=pl.BlockSpec((SUBLANES, D), lambda p, t: (fl_block + p * (end_block - fl_block), 0)),
        out_shape=jax.ShapeDtypeStruct((GRAD_ROWS, D), F32),
        scratch_shapes=[pltpu.VMEM((LANES, D), F32)],
        input_output_aliases={2: 0},
        compiler_params=_cparams(("arbitrary", "arbitrary")),
    )(dfl, xn, main)


def _matmul_tn(a, b, name, bm=512, bn=1024, bt=2048):
    T, M = a.shape
    N = b.shape[1]
    bm, bn, bt = min(bm, M), min(bn, N), min(bt, T)

    def body(a_ref, b_ref, o_ref):
        @pl.when(pl.program_id(2) == 0)
        def _():
            o_ref[...] = jnp.zeros_like(o_ref)

        o_ref[...] += _dot_tn(a_ref[...], b_ref[...])

    return _call(
        body, name=name, grid=(M // bm, N // bn, T // bt),
        in_specs=[pl.BlockSpec((bt, bm), lambda i, j, t: (t, i)), pl.BlockSpec((bt, bn), lambda i, j, t: (t, j))],
        out_specs=pl.BlockSpec((bm, bn), lambda i, j, t: (i, j)),
        out_shape=jax.ShapeDtypeStruct((M, N), F32),
        compiler_params=_cparams(("parallel", "parallel", "arbitrary")),
    )(a, b)


HBM_SPEC = pl.BlockSpec(memory_space=pltpu.HBM)
VMEM_SPEC = pl.BlockSpec(memory_space=pltpu.VMEM)


def _position():
    return lax.axis_index("x"), lax.axis_index("y"), lax.axis_index("c")


def _other_chips(x, y):
    return [(1 - x, y), (x, 1 - y), (1 - x, 1 - y)]


def _gather_shards(shards, whole):
    na, nw = len(shards), len(whole)
    nall = na + nw

    def body(*refs):
        gather = _GatherPlan(refs[:nall], refs[nall:2 * nall], refs[2 * nall:], na)
        gather.send()
        gather.forward()
        gather.finish()

    arrs = list(shards) + list(whole)
    outs = _call(
        body, name="gather_shards",
        in_specs=[HBM_SPEC] * nall, out_specs=[HBM_SPEC] * nall,
        out_shape=_gather_out_shapes(arrs), scratch_shapes=_gather_semaphores(na, nall),
    )(*arrs)
    return _place_own(outs, arrs)


def _gather_out_shapes(arrs):
    return [jax.ShapeDtypeStruct((N_CHIPS,) + s.shape, s.dtype) for s in arrs]


def _gather_semaphores(na, nall):
    return [pltpu.SemaphoreType.DMA((3 * nall,)), pltpu.SemaphoreType.DMA((3 * nall,)),
            pltpu.SemaphoreType.DMA((3 * na,)), pltpu.SemaphoreType.DMA((3 * na,))]


def _place_own(outs, arrs):
    if not arrs:
        return []
    chip = 2 * lax.axis_index("x") + lax.axis_index("y")
    return [lax.dynamic_update_slice(o, a[None], (chip,) + (0,) * a.ndim) for o, a in zip(outs, arrs)]


class _GatherPlan:
    def __init__(self, srcs, dsts, sems, na):
        ici_send, ici_recv, d2d_send, d2d_recv = sems
        x, y, c = _position()
        chip = 2 * x + y
        nall = len(srcs)

        def half(a, which):
            rows = srcs[a].shape[0] // 2
            return pl.ds(pl.multiple_of(which * rows, 16), rows)

        def copy(src, dst, send, recv, k, to):
            return pltpu.make_async_remote_copy(src_ref=src, dst_ref=dst, send_sem=send.at[k], recv_sem=recv.at[k],
                                                device_id=to, device_id_type=MESH)

        self.first, self.landed, self.passed, self.returned = [], [], [], []
        for j, (px, py) in enumerate(_other_chips(x, y)):
            theirs = 2 * px + py
            for a in range(nall):
                k = j * nall + a
                if a < na:
                    self.first.append(copy(srcs[a].at[half(a, c), :], dsts[a].at[chip, half(a, c), :],
                                           ici_send, ici_recv, k, (px, py, c)))
                    mine = dsts[a].at[theirs, half(a, c), :]
                    other = dsts[a].at[theirs, half(a, 1 - c), :]
                    self.landed.append(copy(mine, mine, ici_send, ici_recv, k, (px, py, c)))
                    self.passed.append(copy(mine, mine, d2d_send, d2d_recv, j * na + a, (x, y, 1 - c)))
                    self.returned.append(copy(other, other, d2d_send, d2d_recv, j * na + a, (x, y, 1 - c)))
                else:
                    self.first.append(copy(srcs[a], dsts[a].at[chip], ici_send, ici_recv, k, (px, py, c)))
                    got = dsts[a].at[theirs]
                    self.landed.append(copy(got, got, ici_send, ici_recv, k, (px, py, c)))
                    self.passed.append(None)

    def send(self):
        for cp in self.first:
            cp.start()

    def forward(self):
        for arrival, fwd in zip(self.landed, self.passed):
            arrival.wait_recv()
            if fwd is not None:
                fwd.start()

    def finish(self):
        for cp in self.returned:
            cp.wait_recv()
        for cp in self.first + [f for f in self.passed if f is not None]:
            cp.wait_send()


W_ROWS = 1568
G_ROWS = 1552
SHARD_ROWS = D_IN // N_CHIPS
WINDOW_STEP = 1536


def _assemble_w_in(cont):
    cb = 256
    half = WINDOW_STEP

    def body(c_ref, wa_ref, wf_ref, wb_ref):
        x0 = c_ref[0].astype(F32)
        x1, x2, x3 = (pltpu.roll(c_ref[j].astype(F32), 2 * j, 0) for j in (1, 2, 3))
        wa = jnp.concatenate([x0[:half], x0[half:half + 16] + x1[:16], x1[16:half]], axis=0)
        wa_ref[...] = wa.astype(BF16)

        fl = x1[half:half + 16] + x2[:16]
        row = lax.broadcasted_iota(jnp.int32, fl.shape, 0)
        wf_ref[:16, :] = jnp.where(row < H, fl, 0.0).astype(BF16)
        wf_ref[16:, :] = jnp.zeros((LANES - 16, cb), BF16)

        mid = x2[half:half + SUBLANES] + x3[:SUBLANES]
        wb = jnp.concatenate([x2[SUBLANES:half], mid, x3[SUBLANES:half + SUBLANES]], axis=0)
        wb_ref[...] = wb.astype(BF16)

    return _call(
        body, name="assemble_w_in", grid=(D // cb,),
        in_specs=[pl.BlockSpec((N_CHIPS, W_ROWS, cb), lambda i: (0, 0, i))],
        out_specs=[pl.BlockSpec((3 * D, cb), lambda i: (0, i)), pl.BlockSpec((LANES, cb), lambda i: (0, i)),
                   pl.BlockSpec((3 * D, cb), lambda i: (0, i))],
        out_shape=[jax.ShapeDtypeStruct((3 * D, D), BF16), jax.ShapeDtypeStruct((LANES, D), BF16),
                   jax.ShapeDtypeStruct((3 * D, D), BF16)],
        compiler_params=_cparams(("parallel",)),
    )(cont)


def _pair_exchange_windows(grad_t):
    half_g = G_ROWS // 2

    def body(g_ref, got, send_sems, recv_sems):
        x, y, c = _position()
        copies = []
        for j in range(N_CHIPS):
            rows = pl.ds(pl.multiple_of(j * WINDOW_STEP + (1 - c) * half_g, SUBLANES), half_g)
            copies.append(pltpu.make_async_remote_copy(
                src_ref=g_ref.at[rows, :], dst_ref=got.at[j], send_sem=send_sems.at[j], recv_sem=recv_sems.at[j],
                device_id=(x, y, 1 - c), device_id_type=MESH))
        for cp in copies:
            cp.start()
        for cp in copies:
            cp.wait()

    return _call(
        body, name="pair_exchange_w_in",
        in_specs=[HBM_SPEC], out_specs=HBM_SPEC,
        out_shape=jax.ShapeDtypeStruct((N_CHIPS, half_g, D), F32),
        scratch_shapes=[pltpu.SemaphoreType.DMA((N_CHIPS,)), pltpu.SemaphoreType.DMA((N_CHIPS,))],
    )(grad_t)


def _pair_exchange(parts):
    na = len(parts)

    def body(*refs):
        srcs, got = refs[:na], refs[na:2 * na]
        send_sems, recv_sems = refs[2 * na:]
        x, y, c = _position()
        copies = []
        for a in range(na):
            half = srcs[a].shape[1] // 2
            rows = pl.ds(pl.multiple_of((1 - c) * half, SUBLANES), half)
            copies.append(pltpu.make_async_remote_copy(
                src_ref=srcs[a].at[:, rows, :], dst_ref=got[a], send_sem=send_sems.at[a], recv_sem=recv_sems.at[a],
                device_id=(x, y, 1 - c), device_id_type=MESH))
        for cp in copies:
            cp.start()
        for cp in copies:
            cp.wait()

    return _call(
        body, name="pair_exchange",
        in_specs=[HBM_SPEC] * na, out_specs=[HBM_SPEC] * na,
        out_shape=[jax.ShapeDtypeStruct((s.shape[0], s.shape[1] // 2, s.shape[2]), s.dtype) for s in parts],
        scratch_shapes=[pltpu.SemaphoreType.DMA((na,)), pltpu.SemaphoreType.DMA((na,))],
    )(*parts)


def _pair_sum(parts, gots, c):
    na = len(parts)

    def body(c_ref, *refs):
        for a in range(na):
            refs[2 * na + a][...] = (refs[a][...] + refs[na + a][...]).astype(BF16)

    mine = [pl.BlockSpec(g.shape, lambda i, c_ref: (0, c_ref[0], 0)) for g in gots]
    whole = [pl.BlockSpec(g.shape, lambda i, c_ref: (0, 0, 0)) for g in gots]
    grid_spec = pltpu.PrefetchScalarGridSpec(
        num_scalar_prefetch=1, grid=(1,), in_specs=mine + whole, out_specs=whole)
    return _call(
        body, name="pair_sum", grid_spec=grid_spec,
        out_shape=[jax.ShapeDtypeStruct(g.shape, BF16) for g in gots],
        compiler_params=_cparams(("arbitrary",), VMEM_BIG),
    )(c.reshape(1), *parts, *gots)


def _pair_sum_windows(grad_t, got, c):
    _, half, C = got.shape
    cb = 256

    def body(c_ref, a_ref, b_ref, o_ref):
        o_ref[0] = (a_ref[...] + b_ref[0]).astype(BF16)

    def mine(j, i, c_ref):
        return ((j * (WINDOW_STEP // SUBLANES) + c_ref[0] * (half // SUBLANES)) * SUBLANES, i * cb)

    spec = pl.BlockSpec((1, half, cb), lambda j, i, c_ref: (j, 0, i))
    grid_spec = pltpu.PrefetchScalarGridSpec(
        num_scalar_prefetch=1, grid=(N_CHIPS, C // cb),
        in_specs=[pl.BlockSpec((pl.Element(half), pl.Element(cb)), mine), spec], out_specs=spec)
    return _call(
        body, name="pair_sum_w_in", grid_spec=grid_spec,
        out_shape=jax.ShapeDtypeStruct((N_CHIPS, half, C), BF16),
        compiler_params=_cparams(("parallel", "parallel")),
    )(c.reshape(1), grad_t, got)


def _chip_exchange(sums):
    na = len(sums)

    def body(*refs):
        copies = _chip_copies(refs[:na], refs[na:2 * na], *refs[2 * na:])
        for cp in copies:
            cp.start()
        for cp in copies:
            cp.wait()

    return _call(
        body, name="chip_exchange",
        in_specs=[HBM_SPEC] * na, out_specs=[HBM_SPEC] * na,
        out_shape=[jax.ShapeDtypeStruct(s.shape, s.dtype) for s in sums],
        scratch_shapes=[pltpu.SemaphoreType.DMA((3 * na,)), pltpu.SemaphoreType.DMA((3 * na,))],
    )(*sums)


def _chip_sum(own, got, chip, name):
    _, half, C = got.shape
    cb = min(C, 256)

    def body(chip_ref, own_ref, g_ref, o_ref):
        for me in range(N_CHIPS):
            @pl.when(chip_ref[0] == me)
            def _(me=me):
                terms = [own_ref[0] if k == me else g_ref[k] for k in range(N_CHIPS)]
                acc = terms[0].astype(F32) + terms[1].astype(F32)
                acc = acc + terms[2].astype(F32)
                o_ref[...] = acc + terms[3].astype(F32)

    grid_spec = pltpu.PrefetchScalarGridSpec(
        num_scalar_prefetch=1, grid=(C // cb,),
        in_specs=[pl.BlockSpec((1, half, cb), lambda i, chip_ref: (chip_ref[0], 0, i)),
                  pl.BlockSpec((N_CHIPS, half, cb), lambda i, chip_ref: (0, 0, i))],
        out_specs=pl.BlockSpec((half, cb), lambda i, chip_ref: (0, i)))
    return _call(
        body, name=name, grid_spec=grid_spec,
        out_shape=jax.ShapeDtypeStruct((half, C), F32),
        compiler_params=_cparams(("parallel",)),
    )(chip.reshape(1), own, got)


def _pair_swap(halves):
    na = len(halves)

    def body(*refs):
        srcs, dsts = refs[:na], refs[na:2 * na]
        send_sems, recv_sems = refs[2 * na:]
        x, y, c = _position()
        copies = [pltpu.make_async_remote_copy(
            src_ref=srcs[a], dst_ref=dsts[a], send_sem=send_sems.at[a], recv_sem=recv_sems.at[a],
            device_id=(x, y, 1 - c), device_id_type=MESH) for a in range(na)]
        for cp in copies:
            cp.start()
        for cp in copies:
            cp.wait()

    return _call(
        body, name="pair_swap",
        in_specs=[HBM_SPEC] * na, out_specs=[HBM_SPEC] * na,
        out_shape=[jax.ShapeDtypeStruct(s.shape, s.dtype) for s in halves],
        scratch_shapes=[pltpu.SemaphoreType.DMA((na,)), pltpu.SemaphoreType.DMA((na,))],
    )(*halves)


def _allreduce_small(g):
    rows = g.shape[0]
    per = rows // N_DEV

    def body(g_ref, out_ref, got_ref, s1, r1, s2, r2):
        x, y, c = _position()
        me = 4 * x + 2 * y + c
        mine = pl.ds(pl.multiple_of(me * per, SUBLANES), per)
        peers = []
        for j in range(1, N_DEV):
            px = 1 - x if j & 4 else x
            py = 1 - y if j & 2 else y
            pc = 1 - c if j & 1 else c
            peers.append((px, py, pc))

        first = []
        for j, (px, py, pc) in enumerate(peers):
            theirs = pl.ds(pl.multiple_of((4 * px + 2 * py + pc) * per, SUBLANES), per)
            first.append(pltpu.make_async_remote_copy(
                src_ref=g_ref.at[theirs, :], dst_ref=got_ref.at[me], send_sem=s1.at[j], recv_sem=r1.at[j],
                device_id=(px, py, pc), device_id_type=MESH))
        for cp in first:
            cp.start()
        got_ref[me] = g_ref[mine, :]
        for cp in first:
            cp.wait()
        total = got_ref[0]
        for d in range(1, N_DEV):
            total = total + got_ref[d]
        out_ref[mine, :] = total

        second = []
        for j, peer in enumerate(peers):
            second.append(pltpu.make_async_remote_copy(
                src_ref=out_ref.at[mine, :], dst_ref=out_ref.at[mine, :], send_sem=s2.at[j], recv_sem=r2.at[j],
                device_id=peer, device_id_type=MESH))
        for cp in second:
            cp.start()
        for cp in second:
            cp.wait()

    sems = pltpu.SemaphoreType.DMA((N_DEV - 1,))
    return _call(
        body, name="allreduce_small", in_hbm=False,
        in_specs=[VMEM_SPEC], out_specs=VMEM_SPEC,
        out_shape=jax.ShapeDtypeStruct(g.shape, F32),
        scratch_shapes=[pltpu.VMEM((N_DEV, per, LANES), F32), sems, sems, sems, sems],
    )(g)


def _adamw_math(g, w, m, v):
    m2 = ADAM_B1 * m + (1.0 - ADAM_B1) * g
    v2 = ADAM_B2 * v + (1.0 - ADAM_B2) * (g * g)
    m_hat = m2 / (1.0 - ADAM_B1 ** ADAM_STEP)
    v_hat = v2 / (1.0 - ADAM_B2 ** ADAM_STEP)
    delta = (-ADAM_LR) * (m_hat / (jnp.sqrt(v_hat) + ADAM_EPS) + ADAM_WD * w)
    return delta, m2, v2


ADAMW_BLOCK_BYTES = 1 << 20


def _adamw_big(g, w, m, v, name, copy_g=False):
    R, C = g.shape
    if C == LANES:
        br, bc = min(R, ADAMW_BLOCK_BYTES // (4 * LANES)), LANES
    else:
        br, bc = R, min(C, max(LANES, ADAMW_BLOCK_BYTES // (4 * R) // LANES * LANES))
    n_out = 4 if copy_g else 3

    def body(g_ref, w_ref, m_ref, v_ref, d_ref, m2_ref, v2_ref, *g_out):
        gv = g_ref[...]
        d_ref[...], m2_ref[...], v2_ref[...] = _adamw_math(gv, w_ref[...], m_ref[...], v_ref[...])
        if copy_g:
            g_out[0][...] = gv

    spec = pl.BlockSpec((br, bc), lambda i, j: (i, j))
    out = jax.ShapeDtypeStruct((R, C), F32)
    return _call(
        body, name=name, grid=(pl.cdiv(R, br), C // bc),
        in_specs=[spec] * 4, out_specs=[spec] * n_out, out_shape=[out] * n_out,
        compiler_params=_cparams(("parallel", "parallel")),
    )(g, w, m, v)


def _adamw_small(gs, ws, ms, vs):
    n = len(gs)

    def body(*refs):
        for a in range(n):
            g_ref, w_ref, m_ref, v_ref = (refs[k * n + a] for k in range(4))
            d_ref, m2_ref, v2_ref = (refs[(4 + k) * n + a] for k in range(3))
            d_ref[...], m2_ref[...], v2_ref[...] = _adamw_math(g_ref[...], w_ref[...], m_ref[...], v_ref[...])

    outs = [jax.ShapeDtypeStruct(w.shape, F32) for w in ws]
    specs = [_const_spec(w.shape) for w in ws]
    return _call(
        body, name="adamw_small", grid=(1,),
        in_specs=specs * 4, out_specs=specs * 3, out_shape=outs * 3,
    )(*gs, *ws, *ms, *vs)


def _late_weights(st_out, st_ple, st_gate, st_conv):
    return st_out.reshape(DMIX, D), _from_chip_cols(st_ple), st_gate.reshape(D, D), _from_chip_cols(st_conv)


def _local_step(x, p, tgt, w_a, w_f, w_b, late, b_f, pre_gain, post_gain, conv_b,
                w_rgate, b_rgate, w_igate, b_igate, lam, gain_a, gain_l, ple_gain, b_gate,
                gather_late=False, early_reduce=None, w_in_reduce=None):
    b_f_pad = jnp.pad(b_f, ((0, 0), (0, LANES - H)))
    w_r = w_rgate.astype(BF16)
    w_i = w_igate.astype(BF16)

    xn, q_aug, k_aug, v_aug, g_attn, x_lru, g_lru, flb, vt_aug = _in_proj(x, pre_gain, w_a, w_f, w_b, b_f_pad)
    if gather_late:
        o, qx, stacks = _attn_fwd(q_aug, k_aug, vt_aug, late[:3], late[3:])
        late = _late_weights(*stacks)
    else:
        o, qx, _ = _attn_fwd(q_aug, k_aug, vt_aug)
    w_out_b, w_ple_b, w_gate_b, conv_w = late
    ycat, xc, h = _branches_fwd(o, g_attn, x_lru, g_lru, gain_a, gain_l, conv_w, conv_b, w_r, b_rgate, w_i, b_igate,
                                lam)
    dh1, dycat, dmix, h1b, dgp, pb, dpe, acc_t = _tail(ycat, x, p, tgt, w_out_b, post_gain, w_ple_b, ple_gain,
                                                       w_gate_b, b_gate)
    late_grads = [_matmul_tn(ycat, dmix, "dw_out"), _matmul_tn(pb, dpe, "dw_ple"),
                  _matmul_tn(h1b, dgp, "dw_ple_gate")]
    do_aug, dg_attn, dg_lru, dh, acc_b = _branches_bwd(dycat, o, g_attn, h, g_lru, gain_a, gain_l)
    dx_lru, gw_r, gw_i, acc_l = _lru_bwd(dh, h, xc, x_lru, conv_w, w_r, b_rgate, w_i, b_igate, lam)
    if early_reduce is None:
        dq, dk, dv, dc_key, dc_query, _ = _attn_bwd(q_aug, qx, k_aug, v_aug, do_aug)
    else:
        sent = early_reduce(late_grads)
        dq, dk, dv, dc_key, dc_query, received = _attn_bwd(q_aug, qx, k_aug, v_aug, do_aug, sent)
        late_grads = list(zip(sent, received))
    dfl, acc_f = _fgate_bwd(dc_key, dc_query, flb)
    dz = (dq, dk, dv, dg_attn, dx_lru, dg_lru)
    grad_t = _dw_in_t(dz, dfl, xn)
    if w_in_reduce is None:
        grad_x, acc_x, _ = _dx(dz, dfl, w_a, w_f, w_b, x, pre_gain, dh1)
    else:
        sent = w_in_reduce(grad_t)
        grad_x, acc_x, (received,) = _dx(dz, dfl, w_a, w_f, w_b, x, pre_gain, dh1, [sent])
        grad_t = (sent, received)

    grads = dict(
        w_in_t=grad_t,
        w_out=late_grads[0],
        w_ple=late_grads[1],
        w_ple_gate=late_grads[2],
        w_rgate=gw_r,
        w_igate=gw_i,
        b_f=acc_f[0:1, :H],
        pre_gain=acc_x[0:1],
        post_gain=acc_t[0:1],
        conv_w=acc_l[0:4],
        conv_b=acc_l[4:5],
        b_rgate=acc_l[5:6],
        b_igate=acc_l[6:7],
        lru_lambda=acc_l[7:8],
        attn_out_gain=acc_b[0:1],
        lru_out_gain=acc_b[1:2],
        ple_gain=acc_t[1:2],
        b_ple_gate=acc_t[2:3],
    )
    loss = jnp.sum(acc_t[3])
    return loss, grad_x, grads


SMALL_ROWS = ["b_f", "pre_gain", "post_gain", "conv_w", "conv_b", "b_rgate", "b_igate", "lru_lambda",
              "attn_out_gain", "lru_out_gain", "ple_gain", "b_ple_gate"]
WEIGHTS = ["w_in", "b_f", "pre_gain", "post_gain", "conv_w", "conv_b", "w_rgate", "b_rgate", "w_igate", "b_igate",
           "lru_lambda", "attn_out_gain", "lru_out_gain", "w_out", "w_ple", "ple_gain", "w_ple_gate", "b_ple_gate"]
SHARDED = ["w_in", "w_out", "w_ple", "w_ple_gate"]


def _by_chip_cols(g):
    r, cols = g.shape
    return g.reshape(r, N_CHIPS, cols // N_CHIPS).transpose(1, 0, 2)


def _from_chip_cols(s):
    n, r, cols = s.shape
    return s.transpose(1, 0, 2).reshape(r, n * cols)


def kernel(x, p, w_in, b_f, pre_gain, post_gain, conv_w, conv_b, w_rgate, b_rgate, w_igate, b_igate, lru_lambda, attn_out_gain, lru_out_gain, w_out, w_ple, ple_gain, w_ple_gate, b_ple_gate, loss_target, m_w_in, m_b_f, m_pre_gain, m_post_gain, m_conv_w, m_conv_b, m_w_rgate, m_b_rgate, m_w_igate, m_b_igate, m_lru_lambda, m_attn_out_gain, m_lru_out_gain, m_w_out, m_w_ple, m_ple_gain, m_w_ple_gate, m_b_ple_gate, v_w_in, v_b_f, v_pre_gain, v_post_gain, v_conv_w, v_conv_b, v_w_rgate, v_b_rgate, v_w_igate, v_b_igate, v_lru_lambda, v_attn_out_gain, v_lru_out_gain, v_w_out, v_w_ple, v_ple_gain, v_w_ple_gate, v_b_ple_gate):
    w = dict(w_in=w_in, b_f=b_f, pre_gain=pre_gain, post_gain=post_gain, conv_w=conv_w, conv_b=conv_b,
             w_rgate=w_rgate, b_rgate=b_rgate, w_igate=w_igate, b_igate=b_igate, lru_lambda=lru_lambda,
             attn_out_gain=attn_out_gain, lru_out_gain=lru_out_gain, w_out=w_out, w_ple=w_ple, ple_gain=ple_gain,
             w_ple_gate=w_ple_gate, b_ple_gate=b_ple_gate)
    m = dict(w_in=m_w_in, b_f=m_b_f, pre_gain=m_pre_gain, post_gain=m_post_gain, conv_w=m_conv_w, conv_b=m_conv_b,
             w_rgate=m_w_rgate, b_rgate=m_b_rgate, w_igate=m_w_igate, b_igate=m_b_igate, lru_lambda=m_lru_lambda,
             attn_out_gain=m_attn_out_gain, lru_out_gain=m_lru_out_gain, w_out=m_w_out, w_ple=m_w_ple,
             ple_gain=m_ple_gain, w_ple_gate=m_w_ple_gate, b_ple_gate=m_b_ple_gate)
    v = dict(w_in=v_w_in, b_f=v_b_f, pre_gain=v_pre_gain, post_gain=v_post_gain, conv_w=v_conv_w, conv_b=v_conv_b,
             w_rgate=v_w_rgate, b_rgate=v_b_rgate, w_igate=v_w_igate, b_igate=v_b_igate, lru_lambda=v_lru_lambda,
             attn_out_gain=v_attn_out_gain, lru_out_gain=v_lru_out_gain, w_out=v_w_out, w_ple=v_w_ple,
             ple_gain=v_ple_gain, w_ple_gate=v_w_ple_gate, b_ple_gate=v_b_ple_gate)
    xi, yi, ci = _position()
    chip = 2 * xi + yi

    w_in_t, m_in_t, v_in_t = (jnp.swapaxes(t[0], 0, 1) for t in (w_in, m_w_in, v_w_in))
    window = jnp.pad(w_in_t.astype(BF16), ((0, W_ROWS - SHARD_ROWS), (0, 0)))

    (st_in,) = _gather_shards([window], [])
    w_a, w_f, w_b = _assemble_w_in(st_in)
    late_shards = (w_out[0].astype(BF16), w_ple[0].astype(BF16), w_ple_gate[0].astype(BF16), conv_w[0])

    def early_reduce(local):
        parts = [local[0].reshape(N_CHIPS, DMIX // N_CHIPS, D), _by_chip_cols(local[1]),
                 local[2].reshape(N_CHIPS, D // N_CHIPS, D)]
        return _pair_sum(parts, _pair_exchange(parts), ci)

    loss, grad_x, g = _local_step(
        x[0], p[0, 0], loss_target[0], w_a, w_f, w_b, late_shards, b_f, pre_gain, post_gain,
        conv_b, w_rgate[0], b_rgate, w_igate[0], b_igate, lru_lambda, attn_out_gain, lru_out_gain, ple_gain,
        b_ple_gate, gather_late=True, early_reduce=early_reduce,
        w_in_reduce=lambda grad_t: _pair_sum_windows(grad_t, _pair_exchange_windows(grad_t), ci))

    sums = [g["w_in_t"][0]] + [g[n][0] for n in SHARDED[1:]]
    recv = [g["w_in_t"][1]] + [g[n][1] for n in SHARDED[1:]]
    halves = [_chip_sum(sums[a], recv[a], chip, "chip_sum_%d" % a) for a in range(4)]
    theirs = _pair_swap(halves)
    full = [jnp.concatenate([jnp.where(ci == 0, a, b), jnp.where(ci == 0, b, a)], axis=0)
            for a, b in zip(halves, theirs)]
    red = dict(zip(SHARDED, full))
    red["w_in"] = lax.dynamic_slice_in_dim(red["w_in"], 2 * chip, SHARD_ROWS, axis=0)

    rows = [jnp.pad(g["b_f"], ((0, 0), (0, D - H)))] + [g[n] for n in SMALL_ROWS[1:]]
    rows.append(jnp.pad(loss.reshape(1, 1), ((0, 0), (0, D - 1))))
    packed = jnp.concatenate([g["w_rgate"].reshape(NB * LANES, LANES), g["w_igate"].reshape(NB * LANES, LANES),
                              jnp.concatenate(rows, axis=0).reshape(LANES, LANES)], axis=0)
    summed = _allreduce_small(packed)
    red["w_rgate"] = summed[:D].reshape(1, NB, LANES, LANES)
    red["w_igate"] = summed[D:2 * D].reshape(1, NB, LANES, LANES)
    vec = summed[2 * D:].reshape(16, D)
    loss = vec[15, 0]
    r0 = 0
    for n in SMALL_ROWS:
        nr = 4 if n == "conv_w" else 1
        red[n] = vec[r0:r0 + nr]
        r0 += nr
    red["b_f"] = red["b_f"][:, :H]
    red["conv_w"] = lax.dynamic_slice_in_dim(red["conv_w"], chip * (D // N_CHIPS), D // N_CHIPS, axis=1)[None]

    delta, new_m, new_v = {}, {}, {}
    outs_in = _adamw_big(red["w_in"], w_in_t, m_in_t, v_in_t, "adamw_w_in")
    delta["w_in"], new_m["w_in"], new_v["w_in"] = (jnp.swapaxes(t, 0, 1)[None] for t in outs_in)
    red["w_in"] = jnp.swapaxes(red["w_in"], 0, 1)[None]
    for n in SHARDED[1:]:
        delta[n], new_m[n], new_v[n] = (t[None] for t in _adamw_big(red[n], w[n][0], m[n][0], v[n][0], "adamw_" + n))
        red[n] = red[n][None]
    small = [n for n in WEIGHTS if n not in SHARDED]
    outs = _adamw_small([red[n] for n in small], [w[n] for n in small], [m[n] for n in small],
                        [v[n] for n in small])
    ns = len(small)
    for a, n in enumerate(small):
        delta[n], new_m[n], new_v[n] = outs[a], outs[ns + a], outs[2 * ns + a]

    return (loss, grad_x[None], *[red[n] for n in WEIGHTS], *[delta[n] for n in WEIGHTS],
            *[new_m[n] for n in WEIGHTS], *[new_v[n] for n in WEIGHTS])
```

```python
import functools

import jax
import jax.numpy as jnp
import numpy as np
from jax import lax
from jax.experimental import pallas as pl
from jax.experimental.pallas import tpu as pltpu

F32 = jnp.float32
BF16 = jnp.bfloat16

D = 1024
H = 8
DH = 128
NB = 8
DPLE = 256
DMIX = 2 * D
D_IN = 4 * D + H + 2 * D
FL0 = 3 * D
RMS_EPS = 1e-6
LRU_C = 8.0
NEG = -1e30
LANES = 128
SUBLANES = 8

ADAM_LR = 0.001
ADAM_B1 = 0.9
ADAM_B2 = 0.999
ADAM_EPS = 1e-08
ADAM_WD = 0.01
ADAM_STEP = 10

TM = 256
TA = 512
FWD_HEADS = 4
BWD_HEADS = 2
VMEM_BIG = 56 * 1024 * 1024
VMEM_MID = 40 * 1024 * 1024

MESH = pl.DeviceIdType.MESH
N_CHIPS = 4
N_DEV = 8


def _call(body, *, out_shape, in_hbm=True, **kwargs):
    if not in_hbm:
        return pl.pallas_call(body, out_shape=out_shape, **kwargs)

    def pin(shape):
        return pltpu.HBM(shape.shape, shape.dtype) if isinstance(shape, jax.ShapeDtypeStruct) else shape

    fn = pl.pallas_call(body, out_shape=jax.tree.map(pin, out_shape), **kwargs)

    def run(*args):
        return fn(*[a if a.dtype == jnp.int32 else pltpu.with_memory_space_constraint(a, pltpu.HBM) for a in args])

    return run


def _cparams(sem, vmem=VMEM_MID):
    return pltpu.CompilerParams(dimension_semantics=sem, vmem_limit_bytes=vmem)


def _sigmoid(x):
    return 0.5 * jnp.tanh(0.5 * x) + 0.5


def _rstd(x):
    return lax.rsqrt(jnp.mean(x * x, axis=-1, keepdims=True) + RMS_EPS)


def _rms_bwd(t, xhat, rstd):
    return rstd * (t - xhat * jnp.mean(t * xhat, axis=-1, keepdims=True))


def _dot(a, b):
    return jnp.dot(a, b, preferred_element_type=F32)


def _dot_nt(a, b):
    return lax.dot_general(a, b, (((1,), (1,)), ((), ())), preferred_element_type=F32)


def _dot_tn(a, b):
    return lax.dot_general(a, b, (((0,), (0,)), ((), ())), preferred_element_type=F32)


def _dot_exact(a, b):
    return jnp.dot(a, b, preferred_element_type=F32, precision=lax.Precision.HIGHEST)


def _shift_down(x, j, halo):
    rolled = pltpu.roll(x, j, 0)
    row = lax.broadcasted_iota(jnp.int32, halo.shape, 0)
    top = jnp.where(row < j, pltpu.roll(halo, j, 0), rolled[:SUBLANES])
    return jnp.concatenate([top, rolled[SUBLANES:]], axis=0)


def _shift_up(x, j, nxt):
    tm = x.shape[0]
    rolled = pltpu.roll(x, tm - j, 0)
    row = lax.broadcasted_iota(jnp.int32, nxt.shape, 0)
    bot = jnp.where(row >= SUBLANES - j, pltpu.roll(nxt, SUBLANES - j, 0), rolled[tm - SUBLANES:])
    return jnp.concatenate([rolled[:tm - SUBLANES], bot], axis=0)


def _scan_fwd_into(a, u, carry, h_ref):
    tm, width = a.shape
    groups = (tm // SUBLANES, SUBLANES, width)
    a, u = a.reshape(groups), u.reshape(groups)
    sub = lax.broadcasted_iota(jnp.int32, groups, 1)
    d = 1
    while d < SUBLANES:
        keep = sub >= d
        a_s = jnp.where(keep, pltpu.roll(a, d, 1), 1.0)
        u_s = jnp.where(keep, pltpu.roll(u, d, 1), 0.0)
        u = u + a * u_s
        a = a * a_s
        d *= 2
    a, u = a.reshape(tm, width), u.reshape(tm, width)
    for g in range(tm // SUBLANES):
        rows = slice(g * SUBLANES, (g + 1) * SUBLANES)
        h_ref[rows, :] = u[rows] + a[rows] * carry
        carry = h_ref[(g + 1) * SUBLANES - 1:(g + 1) * SUBLANES, :]
    return carry


def _scan_bwd_into(b, u, g_ref):
    tm, width = b.shape
    groups = (tm // SUBLANES, SUBLANES, width)
    b, u = b.reshape(groups), u.reshape(groups)
    sub = lax.broadcasted_iota(jnp.int32, groups, 1)
    d = 1
    while d < SUBLANES:
        keep = sub < SUBLANES - d
        b_s = jnp.where(keep, pltpu.roll(b, SUBLANES - d, 1), 1.0)
        u_s = jnp.where(keep, pltpu.roll(u, SUBLANES - d, 1), 0.0)
        u = u + b * u_s
        b = b * b_s
        d *= 2
    b, u = b.reshape(tm, width), u.reshape(tm, width)
    nxt = jnp.zeros((1, width), F32)
    for g in reversed(range(tm // SUBLANES)):
        rows = slice(g * SUBLANES, (g + 1) * SUBLANES)
        g_ref[rows, :] = u[rows] + b[rows] * nxt
        nxt = g_ref[g * SUBLANES:g * SUBLANES + 1, :]


def _gate_pre(xc, w_ref):
    outs = []
    for n in range(NB):
        outs.append(_dot(xc[:, n * LANES:(n + 1) * LANES].astype(BF16), w_ref[n]))
    return jnp.concatenate(outs, axis=1)


def _gate_pre_t(d, w_ref):
    outs = []
    for n in range(NB):
        outs.append(_dot_nt(d[:, n * LANES:(n + 1) * LANES].astype(BF16), w_ref[n]))
    return jnp.concatenate(outs, axis=1)


def _softplus_neg(lam):
    return jnp.maximum(-lam, 0.0) + jnp.log(1.0 + jnp.exp(-jnp.abs(lam)))


def _row_spec(tm, width):
    return pl.BlockSpec((tm, width), lambda i: (i, 0))


def _const_spec(shape):
    nd = len(shape)
    return pl.BlockSpec(shape, lambda *_: (0,) * nd)


def _weight_spec(shape):
    nd = len(shape)
    return pl.BlockSpec(shape, lambda *_: (0,) * nd, pipeline_mode=pl.Buffered(1))


AUG = 2 * DH
LOG2E = 1.4426950408889634
LN2 = 0.6931471805599453
Q_SCALE = DH ** -0.5 * LOG2E


def _split3(x):
    hi = x.astype(BF16)
    r1 = x - hi.astype(F32)
    mid = r1.astype(BF16)
    lo = (r1 - mid.astype(F32)).astype(BF16)
    return hi, mid, lo


def _extras(col, ones_from):
    t = col.shape[0]
    hi, mid, lo = _split3(jnp.broadcast_to(col, (t, LANES)))
    lane = lax.broadcasted_iota(jnp.int32, (t, LANES), 1)
    rest = jnp.zeros((t, LANES), BF16)
    if ones_from is not None:
        rest = jnp.where((lane >= ones_from) & (lane < ones_from + 3), 1.0, 0.0).astype(BF16)
    return jnp.where(lane == 0, hi, jnp.where(lane == 1, mid, jnp.where(lane == 2, lo, rest)))


def _selectors():
    sel_q = np.zeros((3 * LANES, H * LANES), np.float32)
    sel_k = np.zeros((3 * LANES, H * LANES), np.float32)
    for hd in range(H):
        for piece in range(3):
            sel_q[piece * LANES + hd, hd * LANES + piece] = 1.0
            sel_k[piece * LANES + hd, hd * LANES + 3 + piece] = -1.0
    return jnp.asarray(sel_q, BF16), jnp.asarray(sel_k, BF16)


def _in_proj(x, pre_gain, w_a, w_f, w_b, b_f_pad):
    T = x.shape[0]
    tm = TM
    sel_q, sel_k = _selectors()

    def body(x_ref, g_ref, wa_ref, wf_ref, wb_ref, bf_ref, sq_ref, sk_ref,
             xn_ref, qa_ref, ka_ref, va_ref, ga_ref, xl_ref, gl_ref, flb_ref, vt_ref, c_s, carry):
        @pl.when(pl.program_id(0) == 0)
        def _():
            carry[...] = jnp.zeros_like(carry)

        xv = x_ref[...]
        xn = (xv * _rstd(xv) * g_ref[...]).astype(BF16)
        xn_ref[...] = xn
        for s, o_ref in enumerate((ga_ref, xl_ref, gl_ref)):
            o_ref[...] = _dot_nt(xn, wb_ref[s * D:(s + 1) * D, :]).astype(o_ref.dtype)
        flb = _dot_nt(xn, wf_ref[...]) + bf_ref[...]
        flb_ref[...] = flb
        lane = lax.broadcasted_iota(jnp.int32, flb.shape, 1)
        ls = jnp.where(lane < H, jnp.minimum(flb, 0.0) - jnp.log(1.0 + jnp.exp(-jnp.abs(flb))), 0.0)
        r = lax.broadcasted_iota(jnp.int32, (tm, tm), 0)
        c = lax.broadcasted_iota(jnp.int32, (tm, tm), 1)
        cs = _dot_exact((c <= r).astype(F32), ls) + carry[...]
        c_s[...] = cs
        carry[...] = c_s[tm - 1:tm, :]

        pieces = jnp.concatenate(_split3(cs * LOG2E), axis=1)
        ones_q = jnp.where((lane >= 3) & (lane < 6), 1.0, 0.0)
        ones_k = jnp.where(lane < 3, 1.0, 0.0)
        zq = _dot_nt(xn, wa_ref[0:D, :]) * Q_SCALE
        zk = _dot_nt(xn, wa_ref[D:2 * D, :])
        zv = _dot_nt(xn, wa_ref[2 * D:3 * D, :])
        ex_q = _dot(pieces, sq_ref[...])
        ex_k = _dot(pieces, sk_ref[...])
        for hd in range(H):
            head = slice(hd * DH, (hd + 1) * DH)
            lo, hi = hd * AUG, hd * AUG + DH
            qa_ref[:, lo:hi] = zq[:, head].astype(BF16)
            qa_ref[:, hi:hi + DH] = (ex_q[:, head] + ones_q).astype(BF16)
            ka_ref[:, lo:hi] = zk[:, head].astype(BF16)
            ka_ref[:, hi:hi + DH] = (ex_k[:, head] + ones_k).astype(BF16)
            va_ref[:, lo:hi] = zv[:, head].astype(BF16)
            va_ref[:, hi:hi + DH] = ones_k.astype(BF16)
            vt_ref[lo:hi, :] = jnp.transpose(zv[:, head]).astype(BF16)
            vt_ref[hi:hi + DH, :] = jnp.where(lax.broadcasted_iota(jnp.int32, (DH, tm), 0) < 3, 1.0, 0.0).astype(BF16)

    bf = jax.ShapeDtypeStruct((T, D), BF16)
    aug = jax.ShapeDtypeStruct((T, H * AUG), BF16)
    f32 = jax.ShapeDtypeStruct((T, D), F32)
    sel_spec = _const_spec((3 * LANES, H * LANES))
    return _call(
        body, name="in_proj", grid=(T // tm,),
        in_specs=[_row_spec(tm, D), _const_spec((1, D)), _const_spec((3 * D, D)), _const_spec((LANES, D)),
                  _const_spec((3 * D, D)), _const_spec((1, LANES)), sel_spec, sel_spec],
        out_specs=[_row_spec(tm, D)] + [_row_spec(tm, H * AUG)] * 3 + [_row_spec(tm, D)] * 3 + [_row_spec(tm, LANES)]
        + [pl.BlockSpec((H * AUG, tm), lambda i: (0, i))],
        out_shape=[bf, aug, aug, aug, f32, f32, f32, jax.ShapeDtypeStruct((T, LANES), F32),
                   jax.ShapeDtypeStruct((H * AUG, T), BF16)],
        scratch_shapes=[pltpu.VMEM((tm, LANES), F32), pltpu.VMEM((1, LANES), F32)],
        compiler_params=_cparams(("arbitrary",), VMEM_BIG),
    )(x, pre_gain, w_a, w_f, w_b, b_f_pad, sel_q, sel_k)


def _causal_pairs(n, q_major):
    if q_major:
        pairs = [(qi, ki) for qi in range(n) for ki in range(qi + 1)]
    else:
        pairs = [(ki, qi) for ki in range(n) for qi in range(ki, n)]
    return (jnp.asarray([a for a, _ in pairs], jnp.int32), jnp.asarray([b for _, b in pairs], jnp.int32))


def _attn_fwd(q_aug, k_aug, vt_aug, shards=(), whole=()):
    T = q_aug.shape[0]
    t = TA
    n = T // t
    hp = FWD_HEADS
    heads = range(hp)
    qi_tab, ki_tab = _causal_pairs(n, q_major=True)
    na, nall = len(shards), len(shards) + len(whole)
    n_h, n_j = H // hp, qi_tab.shape[0]

    def body(qi_ref, ki_ref, q_ref, k_ref, vt_ref, *rest):
        srcs, rest = rest[:nall], rest[nall:]
        o_ref, qx_ref = rest[:2]
        dsts, rest = rest[2:2 + nall], rest[2 + nall:]
        m_s, acc_s = rest[:2]
        h = pl.program_id(0)
        j = pl.program_id(1)
        qi = qi_ref[j]
        ki = ki_ref[j]

        if nall:
            gather = _GatherPlan(srcs, dsts, rest[2:], na)
            pl.when((h == 0) & (j == 0))(gather.send)
            pl.when((h == n_h - 1) & (j == 0))(gather.forward)
            pl.when((h == n_h - 1) & (j == n_j - 1))(gather.finish)

        @pl.when(ki == 0)
        def _():
            m_s[...] = jnp.full(m_s.shape, NEG, F32)
            acc_s[...] = jnp.zeros_like(acc_s)

        def step(on_diagonal):
            cols = [slice(a * AUG, (a + 1) * AUG) for a in heads]
            if on_diagonal:
                krow = lax.broadcasted_iota(jnp.int32, (t, t), 0)
                qcol = lax.broadcasted_iota(jnp.int32, (t, t), 1)
            def logits(a):
                st = _dot_nt(k_ref[:, cols[a]], q_ref[:, cols[a]])
                return jnp.where(krow <= qcol, st, NEG) if on_diagonal else st

            st_next = logits(0)
            for a in heads:
                st = st_next
                if a + 1 < hp:
                    st_next = logits(a + 1)
                m_prev = m_s[a]
                m_new = jnp.maximum(m_prev, jnp.max(st, axis=0, keepdims=True))
                pt = jnp.exp2(st - m_new).astype(BF16)
                acc_s[a] = jnp.exp2(m_prev - m_new) * acc_s[a] + _dot(vt_ref[cols[a], :], pt)
                m_s[a] = m_new

        @pl.when(ki < qi)
        def _():
            step(False)

        @pl.when(ki == qi)
        def _():
            step(True)
            piece = lax.broadcasted_iota(jnp.int32, (DH, t), 0)
            for a in heads:
                l = acc_s[a, DH:DH + 1, :]
                ex = jnp.transpose(q_ref[:, a * AUG + DH:(a + 1) * AUG].astype(F32))
                c2 = jnp.sum(jnp.where(piece < 3, ex, 0.0), axis=0, keepdims=True)
                hi, mid, lo = _split3(jnp.broadcast_to(c2 - (m_s[a] + jnp.log(l) * LOG2E), (DH, t)))
                ones = jnp.where((piece >= 3) & (piece < 6), 1.0, 0.0).astype(BF16)
                ex_t = jnp.where(piece == 0, hi, jnp.where(piece == 1, mid, jnp.where(piece == 2, lo, ones)))
                o_ref[:, a * DH:(a + 1) * DH] = jnp.transpose(acc_s[a, :DH, :] / l)
                qx_ref[:, a * DH:(a + 1) * DH] = jnp.transpose(ex_t.astype(F32)).astype(BF16)

    q_spec = pl.BlockSpec((t, hp * AUG), lambda h, j, qi_ref, ki_ref: (qi_ref[j], h))
    k_spec = pl.BlockSpec((t, hp * AUG), lambda h, j, qi_ref, ki_ref: (ki_ref[j], h))
    vt_spec = pl.BlockSpec((hp * AUG, t), lambda h, j, qi_ref, ki_ref: (h, ki_ref[j]))
    out_spec = pl.BlockSpec((t, hp * DH), lambda h, j, qi_ref, ki_ref: (qi_ref[j], h))
    arrs = list(shards) + list(whole)
    grid_spec = pltpu.PrefetchScalarGridSpec(
        num_scalar_prefetch=2, grid=(n_h, n_j),
        in_specs=[q_spec, k_spec, vt_spec] + [HBM_SPEC] * nall, out_specs=[out_spec, out_spec] + [HBM_SPEC] * nall,
        scratch_shapes=[pltpu.VMEM((hp, 1, t), F32), pltpu.VMEM((hp, AUG, t), F32)]
        + (_gather_semaphores(na, nall) if nall else []))
    outs = _call(
        body, name="attn_fwd", grid_spec=grid_spec,
        out_shape=[jax.ShapeDtypeStruct((T, D), F32), jax.ShapeDtypeStruct((T, D), BF16)] + _gather_out_shapes(arrs),
        compiler_params=_cparams(("arbitrary", "arbitrary"), VMEM_BIG),
    )(qi_tab, ki_tab, q_aug, k_aug, vt_aug, *arrs)
    return outs[0], outs[1], _place_own(outs[2:], arrs)


def _lru_gates(xc, wr_ref, br_ref, wi_ref, bi_ref, lam_ref):
    r = _sigmoid(_gate_pre(xc, wr_ref) + br_ref[...])
    ig = _sigmoid(_gate_pre(xc, wi_ref) + bi_ref[...])
    sp = _softplus_neg(lam_ref[...])
    la = (-LRU_C) * r * sp
    a = jnp.exp(la)
    y = -jnp.tanh(la) * (a * a + 1.0)
    return r, ig, sp, a, jnp.sqrt(y), lax.rsqrt(y)


def _branches_fwd(o, g_attn, x_lru, g_lru, gain_a, gain_l, conv_w, conv_b, w_r, b_r, w_i, b_i, lam):
    T = o.shape[0]
    tm = TM

    def body(o_ref, ga_ref, xl_ref, gl_ref, gna_ref, gnl_ref, cw_ref, cb_ref, wr_ref, br_ref, wi_ref, bi_ref,
             lam_ref, ycat_ref, xc_ref, h_ref, halo_s, hc_s):
        @pl.when(pl.program_id(0) == 0)
        def _():
            halo_s[...] = jnp.zeros_like(halo_s)
            hc_s[...] = jnp.zeros_like(hc_s)

        ov = o_ref[...]
        ga = ga_ref[...]
        ya = ov * _rstd(ov) * gna_ref[...] * (ga * _sigmoid(ga))
        ycat_ref[:, :D] = ya.astype(BF16)

        xl = xl_ref[...]
        halo = halo_s[...]
        xc = xl * cw_ref[3:4, :] + cb_ref[...]
        for j in range(3):
            xc = xc + _shift_down(xl, 3 - j, halo) * cw_ref[j:j + 1, :]
        halo_s[...] = xl_ref[tm - SUBLANES:tm, :]
        xc_ref[...] = xc

        _, ig, _, a, sq, _ = _lru_gates(xc, wr_ref, br_ref, wi_ref, bi_ref, lam_ref)
        u = sq * (ig * xc)
        hc_s[...] = _scan_fwd_into(a, u, hc_s[...], h_ref)
        hh = h_ref[...]

        gl = gl_ref[...]
        yl = hh * _rstd(hh) * gnl_ref[...] * (gl * _sigmoid(gl))
        ycat_ref[:, D:] = yl.astype(BF16)

    vec = _const_spec((1, D))
    wspec = _const_spec((NB, LANES, LANES))
    return _call(
        body, name="branches_fwd", grid=(T // tm,),
        in_specs=[_row_spec(tm, D)] * 4 + [vec, vec, _const_spec((4, D)), vec, wspec, vec, wspec, vec, vec],
        out_specs=[_row_spec(tm, DMIX), _row_spec(tm, D), _row_spec(tm, D)],
        out_shape=[jax.ShapeDtypeStruct((T, DMIX), BF16), jax.ShapeDtypeStruct((T, D), F32),
                   jax.ShapeDtypeStruct((T, D), F32)],
        scratch_shapes=[pltpu.VMEM((SUBLANES, D), F32), pltpu.VMEM((1, D), F32)],
        compiler_params=_cparams(("arbitrary",)),
    )(o, g_attn, x_lru, g_lru, gain_a, gain_l, conv_w, conv_b, w_r, b_r, w_i, b_i, lam)


def _tail(ycat, x, p, tgt, w_out, post_gain, w_ple, ple_gain, w_gate, b_gate):
    T = x.shape[0]
    tm = TM

    def body(ycat_ref, x_ref, p_ref, t_ref, wo_ref, pg_ref, wp_ref, eg_ref, wg_ref, bg_ref,
             dh1_ref, dycat_ref, dmix_ref, h1b_ref, dgp_ref, pb_ref, dpe_ref, acc_ref):
        @pl.when(pl.program_id(0) == 0)
        def _():
            acc_ref[...] = jnp.zeros_like(acc_ref)

        mix = _dot(ycat_ref[...], wo_ref[...])
        rstd_m = _rstd(mix)
        mhat = mix * rstd_m
        h1 = x_ref[...] + mhat * pg_ref[...]
        pb = p_ref[...].astype(BF16)
        pb_ref[...] = pb
        pe = _dot(pb, wp_ref[...])
        rstd_p = _rstd(pe)
        pehat = pe * rstd_p
        e = pehat * eg_ref[...]
        h1b = h1.astype(BF16)
        h1b_ref[...] = h1b
        gate = _sigmoid(_dot(h1b, wg_ref[...]) + bg_ref[...])
        diff = (h1 + gate * e) - t_ref[...]

        dy = diff * (1.0 / D)
        de = dy * gate
        dgp = (dy * e) * gate * (1.0 - gate)
        dgpb = dgp.astype(BF16)
        dgp_ref[...] = dgpb
        dh1 = dy + _dot_nt(dgpb, wg_ref[...])
        dh1_ref[...] = dh1
        dpe_ref[...] = _rms_bwd(de * eg_ref[...], pehat, rstd_p).astype(BF16)
        dmix = _rms_bwd(dh1 * pg_ref[...], mhat, rstd_m).astype(BF16)
        dmix_ref[...] = dmix
        dycat_ref[...] = _dot_nt(dmix, wo_ref[...])

        acc_ref[0:1, :] += jnp.sum(dh1 * mhat, axis=0, keepdims=True)
        acc_ref[1:2, :] += jnp.sum(de * pehat, axis=0, keepdims=True)
        acc_ref[2:3, :] += jnp.sum(dgp, axis=0, keepdims=True)
        acc_ref[3:4, :] += jnp.sum(diff * diff, axis=0, keepdims=True) * (0.5 / D)

    vec = _const_spec((1, D))
    bf = jax.ShapeDtypeStruct((T, D), BF16)
    return _call(
        body, name="tail", grid=(T // tm,),
        in_specs=[_row_spec(tm, DMIX), _row_spec(tm, D), _row_spec(tm, DPLE), _row_spec(tm, D),
                  _const_spec((DMIX, D)), vec, _const_spec((DPLE, D)), vec, _const_spec((D, D)), vec],
        out_specs=[_row_spec(tm, D), _row_spec(tm, DMIX), _row_spec(tm, D), _row_spec(tm, D), _row_spec(tm, D),
                   _row_spec(tm, DPLE), _row_spec(tm, D), _const_spec((SUBLANES, D))],
        out_shape=[jax.ShapeDtypeStruct((T, D), F32), jax.ShapeDtypeStruct((T, DMIX), F32), bf, bf, bf,
                   jax.ShapeDtypeStruct((T, DPLE), BF16), bf, jax.ShapeDtypeStruct((SUBLANES, D), F32)],
        compiler_params=_cparams(("arbitrary",), VMEM_BIG),
    )(ycat, x, p, tgt, w_out, post_gain, w_ple, ple_gain, w_gate, b_gate)


def _branches_bwd(dycat, o, g_attn, h, g_lru, gain_a, gain_l):
    T = o.shape[0]
    tm = TM

    def body(dy_ref, o_ref, ga_ref, h_ref, gl_ref, gna_ref, gnl_ref,
             do_ref, dga_ref, dgl_ref, dh_ref, acc_ref):
        @pl.when(pl.program_id(0) == 0)
        def _():
            acc_ref[...] = jnp.zeros_like(acc_ref)

        def branch(val, g, gain, dyv):
            rstd = _rstd(val)
            vhat = val * rstd
            sig = _sigmoid(g)
            dn = dyv * (g * sig)
            dg = dyv * (vhat * gain) * (sig * (1.0 + g * (1.0 - sig)))
            dgain = jnp.sum(dn * vhat, axis=0, keepdims=True)
            return _rms_bwd(dn * gain, vhat, rstd), dg, dgain

        ov = o_ref[...]
        do, dga, dgain_a = branch(ov, ga_ref[...], gna_ref[...], dy_ref[:, :D])
        dga_ref[...] = dga.astype(BF16)
        prod = do * ov
        for hd in range(H):
            head = slice(hd * DH, (hd + 1) * DH)
            do_ref[:, hd * AUG:hd * AUG + DH] = do[:, head].astype(BF16)
            do_ref[:, hd * AUG + DH:(hd + 1) * AUG] = _extras(-jnp.sum(prod[:, head], axis=1, keepdims=True), None)

        dh, dgl, dgain_l = branch(h_ref[...], gl_ref[...], gnl_ref[...], dy_ref[:, D:])
        dh_ref[...] = dh
        dgl_ref[...] = dgl.astype(BF16)
        acc_ref[0:1, :] += dgain_a
        acc_ref[1:2, :] += dgain_l

    vec = _const_spec((1, D))
    bf = jax.ShapeDtypeStruct((T, D), BF16)
    return _call(
        body, name="branches_bwd", grid=(T // tm,),
        in_specs=[_row_spec(tm, DMIX)] + [_row_spec(tm, D)] * 4 + [vec, vec],
        out_specs=[_row_spec(tm, H * AUG), _row_spec(tm, D), _row_spec(tm, D), _row_spec(tm, D),
                   _const_spec((SUBLANES, D))],
        out_shape=[jax.ShapeDtypeStruct((T, H * AUG), BF16), bf, bf, jax.ShapeDtypeStruct((T, D), F32),
                   jax.ShapeDtypeStruct((SUBLANES, D), F32)],
        compiler_params=_cparams(("arbitrary",)),
    )(dycat, o, g_attn, h, g_lru, gain_a, gain_l)


def _lru_bwd(dh, h, xc, x_lru, conv_w, w_r, b_r, w_i, b_i, lam):
    T = dh.shape[0]
    tm = TM
    nt = T // tm
    per = tm // SUBLANES

    def body(dh_ref, h_ref, hprev_ref, xc_ref, xl_ref, cw_ref, wr_ref, br_ref, wi_ref, bi_ref, lam_ref,
             dxl_ref, dwr_ref, dwi_ref, acc_ref, carry_s, dxc_next_s, top_s, dht_s):
        i = pl.program_id(0)

        @pl.when(i == 0)
        def _():
            acc_ref[...] = jnp.zeros_like(acc_ref)
            dwr_ref[...] = jnp.zeros_like(dwr_ref)
            dwi_ref[...] = jnp.zeros_like(dwi_ref)
            carry_s[...] = jnp.zeros_like(carry_s)
            dxc_next_s[...] = jnp.zeros_like(dxc_next_s)

        inner = jnp.where(i == nt - 1, 0.0, 1.0)
        xc = xc_ref[...]
        r, ig, sp, a, sq, inv_sq = _lru_gates(xc, wr_ref, br_ref, wi_ref, bi_ref, lam_ref)

        row = lax.broadcasted_iota(jnp.int32, (tm, D), 0)
        u = dh_ref[...] + jnp.where(row == tm - 1, carry_s[...], 0.0)
        _scan_bwd_into(pltpu.roll(a, tm - 1, 0), u, dht_s)
        dht = dht_s[...]
        top_s[...] = a[:SUBLANES, :] * dht[:SUBLANES, :]
        carry_s[...] = top_s[0:1, :]

        hprev = hprev_ref[...] * inner
        da = dht * _shift_down(h_ref[...], 1, hprev)
        dig = dht * sq * xc
        dxc = dht * sq * ig
        dsq = dht * ig * xc
        dla = da * a - dsq * (a * a) * inv_sq
        dr = dla * ((-LRU_C) * sp)
        dpr = dr * r * (1.0 - r)
        dpi = dig * ig * (1.0 - ig)
        for n in range(NB):
            blk = slice(n * LANES, (n + 1) * LANES)
            xcb = xc[:, blk].astype(BF16)
            dwr_ref[n] += _dot_tn(xcb, dpr[:, blk].astype(BF16))
            dwi_ref[n] += _dot_tn(xcb, dpi[:, blk].astype(BF16))
        dxc = dxc + _gate_pre_t(dpr, wr_ref) + _gate_pre_t(dpi, wi_ref)

        xl = xl_ref[...]
        nxt = dxc_next_s[...]
        dxl = dxc * cw_ref[3:4, :]
        acc_ref[3:4, :] += jnp.sum(dxc * xl, axis=0, keepdims=True)
        for j in range(3):
            ahead = _shift_up(dxc, 3 - j, nxt)
            dxl = dxl + ahead * cw_ref[j:j + 1, :]
            acc_ref[j:j + 1, :] += jnp.sum(ahead * xl, axis=0, keepdims=True)
        dxc_next_s[...] = dxc[:SUBLANES, :]
        dxl_ref[...] = dxl.astype(BF16)

        acc_ref[4:5, :] += jnp.sum(dxc, axis=0, keepdims=True)
        acc_ref[5:6, :] += jnp.sum(dpr, axis=0, keepdims=True)
        acc_ref[6:7, :] += jnp.sum(dpi, axis=0, keepdims=True)
        acc_ref[7:8, :] += jnp.sum(dla * ((-LRU_C) * r), axis=0, keepdims=True)

        @pl.when(i == nt - 1)
        def _():
            lam_v = lam_ref[...]
            acc_ref[7:8, :] = acc_ref[7:8, :] * (-_sigmoid(-lam_v))

    rev = pl.BlockSpec((tm, D), lambda i: (nt - 1 - i, 0))
    prev8 = pl.BlockSpec((SUBLANES, D), lambda i: (jnp.maximum((nt - 1 - i) * per - 1, 0), 0))
    vec = _const_spec((1, D))
    wspec = _const_spec((NB, LANES, LANES))
    bf = jax.ShapeDtypeStruct((T, D), BF16)
    return _call(
        body, name="lru_bwd", grid=(nt,),
        in_specs=[rev, rev, prev8, rev, rev, _const_spec((4, D)), wspec, vec, wspec, vec, vec],
        out_specs=[rev, wspec, wspec, _const_spec((SUBLANES, D))],
        out_shape=[bf, jax.ShapeDtypeStruct((NB, LANES, LANES), F32), jax.ShapeDtypeStruct((NB, LANES, LANES), F32),
                   jax.ShapeDtypeStruct((SUBLANES, D), F32)],
        scratch_shapes=[pltpu.VMEM((1, D), F32), pltpu.VMEM((SUBLANES, D), F32), pltpu.VMEM((SUBLANES, D), F32),
                        pltpu.VMEM((tm, D), F32)],
        compiler_params=_cparams(("arbitrary",)),
    )(dh, h, h, xc, x_lru, conv_w, w_r, b_r, w_i, b_i, lam)


def _chip_copies(srcs, dsts, send_sems, recv_sems):
    x, y, c = _position()
    chip = 2 * x + y
    na = len(srcs)
    return [pltpu.make_async_remote_copy(
        src_ref=srcs[a].at[2 * px + py], dst_ref=dsts[a].at[chip], send_sem=send_sems.at[j * na + a],
        recv_sem=recv_sems.at[j * na + a], device_id=(px, py, c), device_id_type=MESH)
        for j, (px, py) in enumerate(_other_chips(x, y)) for a in range(na)]


def _attn_bwd(q_aug, qx, k_aug, v_aug, do_aug, exchange=()):
    T = q_aug.shape[0]
    t = TA
    n = T // t
    hp = BWD_HEADS
    heads = range(hp)
    scale = DH ** -0.5
    ki_tab, qi_tab = _causal_pairs(n, q_major=False)
    last = ki_tab.shape[0] - 1
    ne = len(exchange)
    n_h = H // hp

    def body(ki_ref, qi_ref, q_ref, qx_ref, k_ref, v_ref, do_ref, *rest):
        sent, rest = rest[:ne], rest[ne:]
        dq_ref, dk_ref, dv_ref, dck_ref, dcq_ref = rest[:5]
        received, rest = rest[5:5 + ne], rest[5 + ne:]
        dq_s, dk_s, dv_s = rest[:3]
        j = pl.program_id(1)
        ki = ki_ref[j]
        qi = qi_ref[j]

        if ne:
            first_step = (pl.program_id(0) == 0) & (j == 0)
            last_step = (pl.program_id(0) == n_h - 1) & (j == last)

            @pl.when(first_step)
            def _():
                for cp in _chip_copies(sent, received, *rest[3:]):
                    cp.start()

            @pl.when(last_step)
            def _():
                for cp in _chip_copies(sent, received, *rest[3:]):
                    cp.wait()

        @pl.when(j == 0)
        def _():
            dq_s[...] = jnp.zeros_like(dq_s)

        @pl.when(qi == ki)
        def _():
            dk_s[...] = jnp.zeros_like(dk_s)
            dv_s[...] = jnp.zeros_like(dv_s)

        def step(on_diagonal):
            cols = [slice(a * AUG, (a + 1) * AUG) for a in heads]
            qb = [jnp.concatenate([q_ref[:, a * AUG:a * AUG + DH], qx_ref[:, a * DH:(a + 1) * DH]], axis=1)
                  for a in heads]
            if on_diagonal:
                krow = lax.broadcasted_iota(jnp.int32, (t, t), 0)
                qcol = lax.broadcasted_iota(jnp.int32, (t, t), 1)

            def scores(a):
                st = _dot_nt(k_ref[:, cols[a]], qb[a])
                dpd = _dot_nt(v_ref[:, cols[a]], do_ref[:, cols[a]])
                return (jnp.where(krow <= qcol, st, NEG) if on_diagonal else st), dpd

            off = pl.multiple_of(qi * t, t)
            ahead = scores(0)
            for a in heads:
                st, dpd = ahead
                if a + 1 < hp:
                    ahead = scores(a + 1)
                pt = jnp.exp2(st)
                dsb = (pt * dpd).astype(BF16)
                dv_s[a] += _dot(pt.astype(BF16), do_ref[:, a * AUG:a * AUG + DH])
                dk_s[a] += _dot(dsb, qb[a])
                dq_s[a, pl.ds(off, t), :] += _dot_tn(dsb, k_ref[:, cols[a]])

        @pl.when(qi > ki)
        def _():
            step(False)

        @pl.when(qi == ki)
        def _():
            step(True)

        @pl.when(qi == n - 1)
        def _():
            for a in heads:
                dk_ref[:, a * DH:(a + 1) * DH] = (dk_s[a, :, :DH] * LN2).astype(BF16)
                dv_ref[:, a * DH:(a + 1) * DH] = dv_s[a].astype(BF16)
                dck_ref[a] = jnp.broadcast_to(dk_s[a, :, DH + 3:DH + 4], (t, LANES))

        @pl.when(j == last)
        def _():
            for a in heads:
                dq_ref[:, a * DH:(a + 1) * DH] = (dq_s[a, :, :DH] * scale).astype(BF16)
                dcq_ref[a] = jnp.broadcast_to(dq_s[a, :, DH:DH + 1], (T, LANES))

    qside = pl.BlockSpec((t, hp * AUG), lambda h, j, ki_ref, qi_ref: (qi_ref[j], h))
    qxside = pl.BlockSpec((t, hp * DH), lambda h, j, ki_ref, qi_ref: (qi_ref[j], h))
    kside = pl.BlockSpec((t, hp * AUG), lambda h, j, ki_ref, qi_ref: (ki_ref[j], h))
    kout = pl.BlockSpec((t, hp * DH), lambda h, j, ki_ref, qi_ref: (ki_ref[j], h))
    bf = jax.ShapeDtypeStruct((T, D), BF16)
    sums = jax.ShapeDtypeStruct((H, T, LANES), F32)
    grid_spec = pltpu.PrefetchScalarGridSpec(
        num_scalar_prefetch=2, grid=(n_h, ki_tab.shape[0]),
        in_specs=[qside, qxside, kside, kside, qside] + [HBM_SPEC] * ne,
        out_specs=[pl.BlockSpec((T, hp * DH), lambda h, j, ki_ref, qi_ref: (0, h)), kout, kout,
                   pl.BlockSpec((hp, t, LANES), lambda h, j, ki_ref, qi_ref: (h, ki_ref[j], 0)),
                   pl.BlockSpec((hp, T, LANES), lambda h, j, ki_ref, qi_ref: (h, 0, 0))] + [HBM_SPEC] * ne,
        scratch_shapes=[pltpu.VMEM((hp, T, AUG), F32), pltpu.VMEM((hp, t, AUG), F32), pltpu.VMEM((hp, t, DH), F32)]
        + ([pltpu.SemaphoreType.DMA((3 * ne,)), pltpu.SemaphoreType.DMA((3 * ne,))] if ne else []))
    outs = _call(
        body, name="attn_bwd", grid_spec=grid_spec,
        out_shape=[bf, bf, bf, sums, sums] + [jax.ShapeDtypeStruct(s.shape, s.dtype) for s in exchange],
        compiler_params=_cparams(("arbitrary", "arbitrary"), VMEM_BIG),
    )(ki_tab, qi_tab, q_aug, qx, k_aug, v_aug, do_aug, *exchange)
    return (*outs[:5], list(outs[5:]))


def _fgate_bwd(dc_key, dc_query, flb):
    T = flb.shape[0]
    tm = TM
    nt = T // tm

    def body(dck_ref, dcq_ref, flb_ref, dfl_ref, acc_ref, carry, top_s):
        @pl.when(pl.program_id(0) == 0)
        def _():
            carry[...] = jnp.zeros_like(carry)
            acc_ref[...] = jnp.zeros_like(acc_ref)

        flb = flb_ref[...]
        lane = lax.broadcasted_iota(jnp.int32, flb.shape, 1)
        dc = jnp.zeros(flb.shape, F32)
        for hd in range(H):
            dc = dc + jnp.where(lane == hd, dcq_ref[hd] - dck_ref[hd], 0.0)
        r = lax.broadcasted_iota(jnp.int32, (tm, tm), 0)
        c = lax.broadcasted_iota(jnp.int32, (tm, tm), 1)
        dls = _dot_exact((c >= r).astype(F32), dc) + carry[...]
        top_s[...] = dls[:SUBLANES, :]
        carry[...] = top_s[0:1, :]
        dfl = jnp.where(lane < H, dls * _sigmoid(-flb), 0.0)
        dfl_ref[...] = dfl.astype(BF16)
        acc_ref[0:1, :] += jnp.sum(dfl, axis=0, keepdims=True)

    rev = pl.BlockSpec((tm, LANES), lambda i: (nt - 1 - i, 0))
    return _call(
        body, name="fgate_bwd", grid=(nt,),
        in_specs=[pl.BlockSpec((H, tm, LANES), lambda i: (0, nt - 1 - i, 0))] * 2 + [rev],
        out_specs=[rev, _const_spec((SUBLANES, LANES))],
        out_shape=[jax.ShapeDtypeStruct((T, LANES), BF16), jax.ShapeDtypeStruct((SUBLANES, LANES), F32)],
        scratch_shapes=[pltpu.VMEM((1, LANES), F32), pltpu.VMEM((SUBLANES, LANES), F32)],
        compiler_params=_cparams(("arbitrary",)),
    )(dc_key, dc_query, flb)


def _dx(dz, dfl, w_a, w_f, w_b, x, pre_gain, dh1, exchange=()):
    T = x.shape[0]
    tm = TM
    nt = T // tm
    ne = len(exchange)

    def body(*refs):
        dz_refs = refs[:6]
        dfl_ref, wa_ref, wf_ref, wb_ref, x_ref, g_ref, dh1_ref = refs[6:13]
        sent = refs[13:13 + ne]
        gx_ref, acc_ref = refs[13 + ne:15 + ne]
        received, sems = refs[15 + ne:15 + 2 * ne], refs[15 + 2 * ne:]

        @pl.when(pl.program_id(0) == 0)
        def _():
            acc_ref[...] = jnp.zeros_like(acc_ref)
            for cp in _chip_copies(sent, received, *sems) if ne else ():
                cp.start()

        if ne:
            @pl.when(pl.program_id(0) == nt - 1)
            def _():
                for cp in _chip_copies(sent, received, *sems):
                    cp.wait()

        dxn = _dot(dfl_ref[...], wf_ref[...])
        for s in range(3):
            dxn = dxn + _dot(dz_refs[s][...], wa_ref[s * D:(s + 1) * D, :])
            dxn = dxn + _dot(dz_refs[3 + s][...], wb_ref[s * D:(s + 1) * D, :])
        xv = x_ref[...]
        rstd = _rstd(xv)
        xhat = xv * rstd
        gx_ref[...] = dh1_ref[...] + _rms_bwd(dxn * g_ref[...], xhat, rstd)
        acc_ref[0:1, :] += jnp.sum(dxn * xhat, axis=0, keepdims=True)

    outs = _call(
        body, name="dx", grid=(nt,),
        in_specs=[_row_spec(tm, D)] * 6 + [_row_spec(tm, LANES), _weight_spec((3 * D, D)), _weight_spec((LANES, D)),
                                           _weight_spec((3 * D, D)), _row_spec(tm, D), _const_spec((1, D)),
                                           _row_spec(tm, D)] + [HBM_SPEC] * ne,
        out_specs=[_row_spec(tm, D), _const_spec((SUBLANES, D))] + [HBM_SPEC] * ne,
        out_shape=[jax.ShapeDtypeStruct((T, D), F32), jax.ShapeDtypeStruct((SUBLANES, D), F32)]
        + [jax.ShapeDtypeStruct(s.shape, s.dtype) for s in exchange],
        scratch_shapes=[pltpu.SemaphoreType.DMA((3 * ne,)), pltpu.SemaphoreType.DMA((3 * ne,))] if ne else [],
        compiler_params=_cparams(("arbitrary",), VMEM_BIG),
    )(*dz, dfl, w_a, w_f, w_b, x, pre_gain, dh1, *exchange)
    return outs[0], outs[1], list(outs[2:])


GRAD_ROWS = D_IN + SUBLANES


def _dw_in_segment(dz_s, xn, buf, s, bt):
    T = xn.shape[0]
    row0 = s * D + (H if s >= 3 else 0)

    def body(*refs):
        dz_ref, xn_ref, o_ref = refs[0], refs[1], refs[-1]

        @pl.when(pl.program_id(0) == 0)
        def _():
            o_ref[...] = jnp.zeros_like(o_ref)

        o_ref[...] += _dot_tn(dz_ref[...], xn_ref[...])

    tok = pl.BlockSpec((bt, D), lambda t: (t, 0))
    return _call(
        body, name="dw_in_%d" % s, grid=(T // bt,),
        in_specs=[tok, tok] + ([] if buf is None else [pl.BlockSpec(memory_space=pl.ANY)]),
        out_specs=pl.BlockSpec((pl.Element(D), pl.Element(D)), lambda t: (row0, 0)),
        out_shape=jax.ShapeDtypeStruct((GRAD_ROWS, D), F32),
        input_output_aliases={} if buf is None else {2: 0},
        compiler_params=_cparams(("arbitrary",)),
    )(*((dz_s, xn) if buf is None else (dz_s, xn, buf)))


def _dw_in_t(dz, dfl, xn, bt=2048):
    T = xn.shape[0]
    bt = min(bt, T)
    nt = T // bt
    main = None
    for s in range(6):
        main = _dw_in_segment(dz[s], xn, main, s, min(T, 2048))

    def f_body(dfl_ref, xn_ref, main_ref, o_ref, acc_s):
        p = pl.program_id(0)
        t = pl.program_id(1)

        @pl.when(t == 0)
        def _():
            acc_s[...] = jnp.zeros_like(acc_s)

        @pl.when(p == 0)
        def _():
            acc_s[...] += _dot_tn(dfl_ref[...], xn_ref[...])

        @pl.when(t == nt - 1)
        def _():
            o_ref[...] = acc_s[:SUBLANES, :]

    fl_block = FL0 // SUBLANES
    end_block = D_IN // SUBLANES
    return _call(
        f_body, name="dw_in_f", grid=(2, nt),
        in_specs=[pl.BlockSpec((bt, LANES), lambda p, t: (t, 0)), pl.BlockSpec((bt, D), lambda p, t: (t, 0)),
                  pl.BlockSpec(memory_space=pl.ANY)],
        out_specs=pl.BlockSpec((SUBLANES, D), lambda p, t: (fl_block + p * (end_block - fl_block), 0)),
        out_shape=jax.ShapeDtypeStruct((GRAD_ROWS, D), F32),
        scratch_shapes=[pltpu.VMEM((LANES, D), F32)],
        input_output_aliases={2: 0},
        compiler_params=_cparams(("arbitrary", "arbitrary")),
    )(dfl, xn, main)


def _matmul_tn(a, b, name, bm=512, bn=1024, bt=2048):
    T, M = a.shape
    N = b.shape[1]
    bm, bn, bt = min(bm, M), min(bn, N), min(bt, T)

    def body(a_ref, b_ref, o_ref):
        @pl.when(pl.program_id(2) == 0)
        def _():
            o_ref[...] = jnp.zeros_like(o_ref)

        o_ref[...] += _dot_tn(a_ref[...], b_ref[...])

    return _call(
        body, name=name, grid=(M // bm, N // bn, T // bt),
        in_specs=[pl.BlockSpec((bt, bm), lambda i, j, t: (t, i)), pl.BlockSpec((bt, bn), lambda i, j, t: (t, j))],
        out_specs=pl.BlockSpec((bm, bn), lambda i, j, t: (i, j)),
        out_shape=jax.ShapeDtypeStruct((M, N), F32),
        compiler_params=_cparams(("parallel", "parallel", "arbitrary")),
    )(a, b)


HBM_SPEC = pl.BlockSpec(memory_space=pltpu.HBM)
VMEM_SPEC = pl.BlockSpec(memory_space=pltpu.VMEM)


def _position():
    return lax.axis_index("x"), lax.axis_index("y"), lax.axis_index("c")


def _other_chips(x, y):
    return [(1 - x, y), (x, 1 - y), (1 - x, 1 - y)]


def _gather_shards(shards, whole):
    na, nw = len(shards), len(whole)
    nall = na + nw

    def body(*refs):
        gather = _GatherPlan(refs[:nall], refs[nall:2 * nall], refs[2 * nall:], na)
        gather.send()
        gather.forward()
        gather.finish()

    arrs = list(shards) + list(whole)
    outs = _call(
        body, name="gather_shards",
        in_specs=[HBM_SPEC] * nall, out_specs=[HBM_SPEC] * nall,
        out_shape=_gather_out_shapes(arrs), scratch_shapes=_gather_semaphores(na, nall),
    )(*arrs)
    return _place_own(outs, arrs)


def _gather_out_shapes(arrs):
    return [jax.ShapeDtypeStruct((N_CHIPS,) + s.shape, s.dtype) for s in arrs]


def _gather_semaphores(na, nall):
    return [pltpu.SemaphoreType.DMA((3 * nall,)), pltpu.SemaphoreType.DMA((3 * nall,)),
            pltpu.SemaphoreType.DMA((3 * na,)), pltpu.SemaphoreType.DMA((3 * na,))]


def _place_own(outs, arrs):
    if not arrs:
        return []
    chip = 2 * lax.axis_index("x") + lax.axis_index("y")
    return [lax.dynamic_update_slice(o, a[None], (chip,) + (0,) * a.ndim) for o, a in zip(outs, arrs)]


class _GatherPlan:
    def __init__(self, srcs, dsts, sems, na):
        ici_send, ici_recv, d2d_send, d2d_recv = sems
        x, y, c = _position()
        chip = 2 * x + y
        nall = len(srcs)

        def half(a, which):
            rows = srcs[a].shape[0] // 2
            return pl.ds(pl.multiple_of(which * rows, 16), rows)

        def copy(src, dst, send, recv, k, to):
            return pltpu.make_async_remote_copy(src_ref=src, dst_ref=dst, send_sem=send.at[k], recv_sem=recv.at[k],
                                                device_id=to, device_id_type=MESH)

        self.first, self.landed, self.passed, self.returned = [], [], [], []
        for j, (px, py) in enumerate(_other_chips(x, y)):
            theirs = 2 * px + py
            for a in range(nall):
                k = j * nall + a
                if a < na:
                    self.first.append(copy(srcs[a].at[half(a, c), :], dsts[a].at[chip, half(a, c), :],
                                           ici_send, ici_recv, k, (px, py, c)))
                    mine = dsts[a].at[theirs, half(a, c), :]
                    other = dsts[a].at[theirs, half(a, 1 - c), :]
                    self.landed.append(copy(mine, mine, ici_send, ici_recv, k, (px, py, c)))
                    self.passed.append(copy(mine, mine, d2d_send, d2d_recv, j * na + a, (x, y, 1 - c)))
                    self.returned.append(copy(other, other, d2d_send, d2d_recv, j * na + a, (x, y, 1 - c)))
                else:
                    self.first.append(copy(srcs[a], dsts[a].at[chip], ici_send, ici_recv, k, (px, py, c)))
                    got = dsts[a].at[theirs]
                    self.landed.append(copy(got, got, ici_send, ici_recv, k, (px, py, c)))
                    self.passed.append(None)

    def send(self):
        for cp in self.first:
            cp.start()

    def forward(self):
        for arrival, fwd in zip(self.landed, self.passed):
            arrival.wait_recv()
            if fwd is not None:
                fwd.start()

    def finish(self):
        for cp in self.returned:
            cp.wait_recv()
        for cp in self.first + [f for f in self.passed if f is not None]:
            cp.wait_send()


W_ROWS = 1568
G_ROWS = 1552
SHARD_ROWS = D_IN // N_CHIPS
WINDOW_STEP = 1536


def _assemble_w_in(cont):
    cb = 256
    half = WINDOW_STEP

    def body(c_ref, wa_ref, wf_ref, wb_ref):
        x0 = c_ref[0].astype(F32)
        x1, x2, x3 = (pltpu.roll(c_ref[j].astype(F32), 2 * j, 0) for j in (1, 2, 3))
        wa = jnp.concatenate([x0[:half], x0[half:half + 16] + x1[:16], x1[16:half]], axis=0)
        wa_ref[...] = wa.astype(BF16)

        fl = x1[half:half + 16] + x2[:16]
        row = lax.broadcasted_iota(jnp.int32, fl.shape, 0)
        wf_ref[:16, :] = jnp.where(row < H, fl, 0.0).astype(BF16)
        wf_ref[16:, :] = jnp.zeros((LANES - 16, cb), BF16)

        mid = x2[half:half + SUBLANES] + x3[:SUBLANES]
        wb = jnp.concatenate([x2[SUBLANES:half], mid, x3[SUBLANES:half + SUBLANES]], axis=0)
        wb_ref[...] = wb.astype(BF16)

    return _call(
        body, name="assemble_w_in", grid=(D // cb,),
        in_specs=[pl.BlockSpec((N_CHIPS, W_ROWS, cb), lambda i: (0, 0, i))],
        out_specs=[pl.BlockSpec((3 * D, cb), lambda i: (0, i)), pl.BlockSpec((LANES, cb), lambda i: (0, i)),
                   pl.BlockSpec((3 * D, cb), lambda i: (0, i))],
        out_shape=[jax.ShapeDtypeStruct((3 * D, D), BF16), jax.ShapeDtypeStruct((LANES, D), BF16),
                   jax.ShapeDtypeStruct((3 * D, D), BF16)],
        compiler_params=_cparams(("parallel",)),
    )(cont)


def _pair_exchange_windows(grad_t):
    half_g = G_ROWS // 2

    def body(g_ref, got, send_sems, recv_sems):
        x, y, c = _position()
        copies = []
        for j in range(N_CHIPS):
            rows = pl.ds(pl.multiple_of(j * WINDOW_STEP + (1 - c) * half_g, SUBLANES), half_g)
            copies.append(pltpu.make_async_remote_copy(
                src_ref=g_ref.at[rows, :], dst_ref=got.at[j], send_sem=send_sems.at[j], recv_sem=recv_sems.at[j],
                device_id=(x, y, 1 - c), device_id_type=MESH))
        for cp in copies:
            cp.start()
        for cp in copies:
            cp.wait()

    return _call(
        body, name="pair_exchange_w_in",
        in_specs=[HBM_SPEC], out_specs=HBM_SPEC,
        out_shape=jax.ShapeDtypeStruct((N_CHIPS, half_g, D), F32),
        scratch_shapes=[pltpu.SemaphoreType.DMA((N_CHIPS,)), pltpu.SemaphoreType.DMA((N_CHIPS,))],
    )(grad_t)


def _pair_exchange(parts):
    na = len(parts)

    def body(*refs):
        srcs, got = refs[:na], refs[na:2 * na]
        send_sems, recv_sems = refs[2 * na:]
        x, y, c = _position()
        copies = []
        for a in range(na):
            half = srcs[a].shape[1] // 2
            rows = pl.ds(pl.multiple_of((1 - c) * half, SUBLANES), half)
            copies.append(pltpu.make_async_remote_copy(
                src_ref=srcs[a].at[:, rows, :], dst_ref=got[a], send_sem=send_sems.at[a], recv_sem=recv_sems.at[a],
                device_id=(x, y, 1 - c), device_id_type=MESH))
        for cp in copies:
            cp.start()
        for cp in copies:
            cp.wait()

    return _call(
        body, name="pair_exchange",
        in_specs=[HBM_SPEC] * na, out_specs=[HBM_SPEC] * na,
        out_shape=[jax.ShapeDtypeStruct((s.shape[0], s.shape[1] // 2, s.shape[2]), s.dtype) for s in parts],
        scratch_shapes=[pltpu.SemaphoreType.DMA((na,)), pltpu.SemaphoreType.DMA((na,))],
    )(*parts)


def _pair_sum(parts, gots, c):
    na = len(parts)

    def body(c_ref, *refs):
        for a in range(na):
            refs[2 * na + a][...] = (refs[a][...] + refs[na + a][...]).astype(BF16)

    mine = [pl.BlockSpec(g.shape, lambda i, c_ref: (0, c_ref[0], 0)) for g in gots]
    whole = [pl.BlockSpec(g.shape, lambda i, c_ref: (0, 0, 0)) for g in gots]
    grid_spec = pltpu.PrefetchScalarGridSpec(
        num_scalar_prefetch=1, grid=(1,), in_specs=mine + whole, out_specs=whole)
    return _call(
        body, name="pair_sum", grid_spec=grid_spec,
        out_shape=[jax.ShapeDtypeStruct(g.shape, BF16) for g in gots],
        compiler_params=_cparams(("arbitrary",), VMEM_BIG),
    )(c.reshape(1), *parts, *gots)


def _pair_sum_windows(grad_t, got, c):
    _, half, C = got.shape
    cb = 256

    def body(c_ref, a_ref, b_ref, o_ref):
        o_ref[0] = (a_ref[...] + b_ref[0]).astype(BF16)

    def mine(j, i, c_ref):
        return ((j * (WINDOW_STEP // SUBLANES) + c_ref[0] * (half // SUBLANES)) * SUBLANES, i * cb)

    spec = pl.BlockSpec((1, half, cb), lambda j, i, c_ref: (j, 0, i))
    grid_spec = pltpu.PrefetchScalarGridSpec(
        num_scalar_prefetch=1, grid=(N_CHIPS, C // cb),
        in_specs=[pl.BlockSpec((pl.Element(half), pl.Element(cb)), mine), spec], out_specs=spec)
    return _call(
        body, name="pair_sum_w_in", grid_spec=grid_spec,
        out_shape=jax.ShapeDtypeStruct((N_CHIPS, half, C), BF16),
        compiler_params=_cparams(("parallel", "parallel")),
    )(c.reshape(1), grad_t, got)


def _chip_exchange(sums):
    na = len(sums)

    def body(*refs):
        copies = _chip_copies(refs[:na], refs[na:2 * na], *refs[2 * na:])
        for cp in copies:
            cp.start()
        for cp in copies:
            cp.wait()

    return _call(
        body, name="chip_exchange",
        in_specs=[HBM_SPEC] * na, out_specs=[HBM_SPEC] * na,
        out_shape=[jax.ShapeDtypeStruct(s.shape, s.dtype) for s in sums],
        scratch_shapes=[pltpu.SemaphoreType.DMA((3 * na,)), pltpu.SemaphoreType.DMA((3 * na,))],
    )(*sums)


def _chip_sum(own, got, chip, name):
    _, half, C = got.shape
    cb = min(C, 256)

    def body(chip_ref, own_ref, g_ref, o_ref):
        for me in range(N_CHIPS):
            @pl.when(chip_ref[0] == me)
            def _(me=me):
                terms = [own_ref[0] if k == me else g_ref[k] for k in range(N_CHIPS)]
                acc = terms[0].astype(F32) + terms[1].astype(F32)
                acc = acc + terms[2].astype(F32)
                o_ref[...] = acc + terms[3].astype(F32)

    grid_spec = pltpu.PrefetchScalarGridSpec(
        num_scalar_prefetch=1, grid=(C // cb,),
        in_specs=[pl.BlockSpec((1, half, cb), lambda i, chip_ref: (chip_ref[0], 0, i)),
                  pl.BlockSpec((N_CHIPS, half, cb), lambda i, chip_ref: (0, 0, i))],
        out_specs=pl.BlockSpec((half, cb), lambda i, chip_ref: (0, i)))
    return _call(
        body, name=name, grid_spec=grid_spec,
        out_shape=jax.ShapeDtypeStruct((half, C), F32),
        compiler_params=_cparams(("parallel",)),
    )(chip.reshape(1), own, got)


def _final_exchange(halves, g):
    na = len(halves)
    rows = g.shape[0]
    per = rows // N_DEV

    def body(*refs):
        srcs, g_ref = refs[:na], refs[na]
        dsts, out_ref = refs[na + 1:2 * na + 1], refs[2 * na + 1]
        got_ref, s1, r1, s2, r2, swap_send, swap_recv = refs[2 * na + 2:]
        x, y, c = _position()
        swaps = [pltpu.make_async_remote_copy(
            src_ref=srcs[a], dst_ref=dsts[a], send_sem=swap_send.at[a], recv_sem=swap_recv.at[a],
            device_id=(x, y, 1 - c), device_id_type=MESH) for a in range(na)]
        for cp in swaps:
            cp.start()
        me = 4 * x + 2 * y + c
        mine = pl.ds(pl.multiple_of(me * per, SUBLANES), per)
        peers = []
        for j in range(1, N_DEV):
            px = 1 - x if j & 4 else x
            py = 1 - y if j & 2 else y
            pc = 1 - c if j & 1 else c
            peers.append((px, py, pc))

        first = []
        for j, (px, py, pc) in enumerate(peers):
            theirs = pl.ds(pl.multiple_of((4 * px + 2 * py + pc) * per, SUBLANES), per)
            first.append(pltpu.make_async_remote_copy(
                src_ref=g_ref.at[theirs, :], dst_ref=got_ref.at[me], send_sem=s1.at[j], recv_sem=r1.at[j],
                device_id=(px, py, pc), device_id_type=MESH))
        for cp in first:
            cp.start()
        got_ref[me] = g_ref[mine, :]
        for cp in first:
            cp.wait()
        total = got_ref[0]
        for d in range(1, N_DEV):
            total = total + got_ref[d]
        out_ref[mine, :] = total

        second = []
        for j, peer in enumerate(peers):
            second.append(pltpu.make_async_remote_copy(
                src_ref=out_ref.at[mine, :], dst_ref=out_ref.at[mine, :], send_sem=s2.at[j], recv_sem=r2.at[j],
                device_id=peer, device_id_type=MESH))
        for cp in second:
            cp.start()
        for cp in second + swaps:
            cp.wait()

    sems = pltpu.SemaphoreType.DMA((N_DEV - 1,))
    swap_sems = pltpu.SemaphoreType.DMA((na,))
    outs = _call(
        body, name="final_exchange", in_hbm=False,
        in_specs=[HBM_SPEC] * na + [VMEM_SPEC], out_specs=[HBM_SPEC] * na + [VMEM_SPEC],
        out_shape=[jax.ShapeDtypeStruct(s.shape, s.dtype) for s in halves] + [jax.ShapeDtypeStruct(g.shape, F32)],
        scratch_shapes=[pltpu.VMEM((N_DEV, per, LANES), F32), sems, sems, sems, sems, swap_sems, swap_sems],
    )(*halves, g)
    return outs[:na], outs[na]


def _adamw_math(g, w, m, v):
    m2 = ADAM_B1 * m + (1.0 - ADAM_B1) * g
    v2 = ADAM_B2 * v + (1.0 - ADAM_B2) * (g * g)
    m_hat = m2 / (1.0 - ADAM_B1 ** ADAM_STEP)
    v_hat = v2 / (1.0 - ADAM_B2 ** ADAM_STEP)
    delta = (-ADAM_LR) * (m_hat / (jnp.sqrt(v_hat) + ADAM_EPS) + ADAM_WD * w)
    return delta, m2, v2


ADAMW_BLOCK_BYTES = 1 << 20


def _adamw_big(g, w, m, v, name, copy_g=False):
    R, C = g.shape
    if C == LANES:
        br, bc = min(R, ADAMW_BLOCK_BYTES // (4 * LANES)), LANES
    else:
        br, bc = R, min(C, max(LANES, ADAMW_BLOCK_BYTES // (4 * R) // LANES * LANES))
    n_out = 4 if copy_g else 3

    def body(g_ref, w_ref, m_ref, v_ref, d_ref, m2_ref, v2_ref, *g_out):
        gv = g_ref[...]
        d_ref[...], m2_ref[...], v2_ref[...] = _adamw_math(gv, w_ref[...], m_ref[...], v_ref[...])
        if copy_g:
            g_out[0][...] = gv

    spec = pl.BlockSpec((br, bc), lambda i, j: (i, j))
    out = jax.ShapeDtypeStruct((R, C), F32)
    return _call(
        body, name=name, grid=(pl.cdiv(R, br), C // bc),
        in_specs=[spec] * 4, out_specs=[spec] * n_out, out_shape=[out] * n_out,
        compiler_params=_cparams(("parallel", "parallel")),
    )(g, w, m, v)


def _adamw_small(gs, ws, ms, vs):
    n = len(gs)

    def body(*refs):
        for a in range(n):
            g_ref, w_ref, m_ref, v_ref = (refs[k * n + a] for k in range(4))
            d_ref, m2_ref, v2_ref = (refs[(4 + k) * n + a] for k in range(3))
            d_ref[...], m2_ref[...], v2_ref[...] = _adamw_math(g_ref[...], w_ref[...], m_ref[...], v_ref[...])

    outs = [jax.ShapeDtypeStruct(w.shape, F32) for w in ws]
    specs = [_const_spec(w.shape) for w in ws]
    return _call(
        body, name="adamw_small", grid=(1,),
        in_specs=specs * 4, out_specs=specs * 3, out_shape=outs * 3,
    )(*gs, *ws, *ms, *vs)


def _late_weights(st_out, st_ple, st_gate, st_conv):
    return st_out.reshape(DMIX, D), _from_chip_cols(st_ple), st_gate.reshape(D, D), _from_chip_cols(st_conv)


def _local_step(x, p, tgt, w_a, w_f, w_b, late, b_f, pre_gain, post_gain, conv_b,
                w_rgate, b_rgate, w_igate, b_igate, lam, gain_a, gain_l, ple_gain, b_gate,
                gather_late=False, early_reduce=None, w_in_reduce=None):
    b_f_pad = jnp.pad(b_f, ((0, 0), (0, LANES - H)))
    w_r = w_rgate.astype(BF16)
    w_i = w_igate.astype(BF16)

    xn, q_aug, k_aug, v_aug, g_attn, x_lru, g_lru, flb, vt_aug = _in_proj(x, pre_gain, w_a, w_f, w_b, b_f_pad)
    if gather_late:
        o, qx, stacks = _attn_fwd(q_aug, k_aug, vt_aug, late[:3], late[3:])
        late = _late_weights(*stacks)
    else:
        o, qx, _ = _attn_fwd(q_aug, k_aug, vt_aug)
    w_out_b, w_ple_b, w_gate_b, conv_w = late
    ycat, xc, h = _branches_fwd(o, g_attn, x_lru, g_lru, gain_a, gain_l, conv_w, conv_b, w_r, b_rgate, w_i, b_igate,
                                lam)
    dh1, dycat, dmix, h1b, dgp, pb, dpe, acc_t = _tail(ycat, x, p, tgt, w_out_b, post_gain, w_ple_b, ple_gain,
                                                       w_gate_b, b_gate)
    late_grads = [_matmul_tn(ycat, dmix, "dw_out"), _matmul_tn(pb, dpe, "dw_ple"),
                  _matmul_tn(h1b, dgp, "dw_ple_gate")]
    do_aug, dg_attn, dg_lru, dh, acc_b = _branches_bwd(dycat, o, g_attn, h, g_lru, gain_a, gain_l)
    dx_lru, gw_r, gw_i, acc_l = _lru_bwd(dh, h, xc, x_lru, conv_w, w_r, b_rgate, w_i, b_igate, lam)
    if early_reduce is None:
        dq, dk, dv, dc_key, dc_query, _ = _attn_bwd(q_aug, qx, k_aug, v_aug, do_aug)
    else:
        sent = early_reduce(late_grads)
        dq, dk, dv, dc_key, dc_query, received = _attn_bwd(q_aug, qx, k_aug, v_aug, do_aug, sent)
        late_grads = list(zip(sent, received))
    dfl, acc_f = _fgate_bwd(dc_key, dc_query, flb)
    dz = (dq, dk, dv, dg_attn, dx_lru, dg_lru)
    grad_t = _dw_in_t(dz, dfl, xn)
    if w_in_reduce is None:
        grad_x, acc_x, _ = _dx(dz, dfl, w_a, w_f, w_b, x, pre_gain, dh1)
    else:
        sent = w_in_reduce(grad_t)
        grad_x, acc_x, (received,) = _dx(dz, dfl, w_a, w_f, w_b, x, pre_gain, dh1, [sent])
        grad_t = (sent, received)

    grads = dict(
        w_in_t=grad_t,
        w_out=late_grads[0],
        w_ple=late_grads[1],
        w_ple_gate=late_grads[2],
        w_rgate=gw_r,
        w_igate=gw_i,
        b_f=acc_f[0:1, :H],
        pre_gain=acc_x[0:1],
        post_gain=acc_t[0:1],
        conv_w=acc_l[0:4],
        conv_b=acc_l[4:5],
        b_rgate=acc_l[5:6],
        b_igate=acc_l[6:7],
        lru_lambda=acc_l[7:8],
        attn_out_gain=acc_b[0:1],
        lru_out_gain=acc_b[1:2],
        ple_gain=acc_t[1:2],
        b_ple_gate=acc_t[2:3],
    )
    loss = jnp.sum(acc_t[3])
    return loss, grad_x, grads


SMALL_ROWS = ["b_f", "pre_gain", "post_gain", "conv_w", "conv_b", "b_rgate", "b_igate", "lru_lambda",
              "attn_out_gain", "lru_out_gain", "ple_gain", "b_ple_gate"]
WEIGHTS = ["w_in", "b_f", "pre_gain", "post_gain", "conv_w", "conv_b", "w_rgate", "b_rgate", "w_igate", "b_igate",
           "lru_lambda", "attn_out_gain", "lru_out_gain", "w_out", "w_ple", "ple_gain", "w_ple_gate", "b_ple_gate"]
SHARDED = ["w_in", "w_out", "w_ple", "w_ple_gate"]


def _by_chip_cols(g):
    r, cols = g.shape
    return g.reshape(r, N_CHIPS, cols // N_CHIPS).transpose(1, 0, 2)


def _from_chip_cols(s):
    n, r, cols = s.shape
    return s.transpose(1, 0, 2).reshape(r, n * cols)


def kernel(x, p, w_in, b_f, pre_gain, post_gain, conv_w, conv_b, w_rgate, b_rgate, w_igate, b_igate, lru_lambda, attn_out_gain, lru_out_gain, w_out, w_ple, ple_gain, w_ple_gate, b_ple_gate, loss_target, m_w_in, m_b_f, m_pre_gain, m_post_gain, m_conv_w, m_conv_b, m_w_rgate, m_b_rgate, m_w_igate, m_b_igate, m_lru_lambda, m_attn_out_gain, m_lru_out_gain, m_w_out, m_w_ple, m_ple_gain, m_w_ple_gate, m_b_ple_gate, v_w_in, v_b_f, v_pre_gain, v_post_gain, v_conv_w, v_conv_b, v_w_rgate, v_b_rgate, v_w_igate, v_b_igate, v_lru_lambda, v_attn_out_gain, v_lru_out_gain, v_w_out, v_w_ple, v_ple_gain, v_w_ple_gate, v_b_ple_gate):
    w = dict(w_in=w_in, b_f=b_f, pre_gain=pre_gain, post_gain=post_gain, conv_w=conv_w, conv_b=conv_b,
             w_rgate=w_rgate, b_rgate=b_rgate, w_igate=w_igate, b_igate=b_igate, lru_lambda=lru_lambda,
             attn_out_gain=attn_out_gain, lru_out_gain=lru_out_gain, w_out=w_out, w_ple=w_ple, ple_gain=ple_gain,
             w_ple_gate=w_ple_gate, b_ple_gate=b_ple_gate)
    m = dict(w_in=m_w_in, b_f=m_b_f, pre_gain=m_pre_gain, post_gain=m_post_gain, conv_w=m_conv_w, conv_b=m_conv_b,
             w_rgate=m_w_rgate, b_rgate=m_b_rgate, w_igate=m_w_igate, b_igate=m_b_igate, lru_lambda=m_lru_lambda,
             attn_out_gain=m_attn_out_gain, lru_out_gain=m_lru_out_gain, w_out=m_w_out, w_ple=m_w_ple,
             ple_gain=m_ple_gain, w_ple_gate=m_w_ple_gate, b_ple_gate=m_b_ple_gate)
    v = dict(w_in=v_w_in, b_f=v_b_f, pre_gain=v_pre_gain, post_gain=v_post_gain, conv_w=v_conv_w, conv_b=v_conv_b,
             w_rgate=v_w_rgate, b_rgate=v_b_rgate, w_igate=v_w_igate, b_igate=v_b_igate, lru_lambda=v_lru_lambda,
             attn_out_gain=v_attn_out_gain, lru_out_gain=v_lru_out_gain, w_out=v_w_out, w_ple=v_w_ple,
             ple_gain=v_ple_gain, w_ple_gate=v_w_ple_gate, b_ple_gate=v_b_ple_gate)
    xi, yi, ci = _position()
    chip = 2 * xi + yi

    w_in_t, m_in_t, v_in_t = (jnp.swapaxes(t[0], 0, 1) for t in (w_in, m_w_in, v_w_in))
    window = jnp.pad(w_in_t.astype(BF16), ((0, W_ROWS - SHARD_ROWS), (0, 0)))

    (st_in,) = _gather_shards([window], [])
    w_a, w_f, w_b = _assemble_w_in(st_in)
    late_shards = (w_out[0].astype(BF16), w_ple[0].astype(BF16), w_ple_gate[0].astype(BF16), conv_w[0])

    def early_reduce(local):
        parts = [local[0].reshape(N_CHIPS, DMIX // N_CHIPS, D), _by_chip_cols(local[1]),
                 local[2].reshape(N_CHIPS, D // N_CHIPS, D)]
        return _pair_sum(parts, _pair_exchange(parts), ci)

    loss, grad_x, g = _local_step(
        x[0], p[0, 0], loss_target[0], w_a, w_f, w_b, late_shards, b_f, pre_gain, post_gain,
        conv_b, w_rgate[0], b_rgate, w_igate[0], b_igate, lru_lambda, attn_out_gain, lru_out_gain, ple_gain,
        b_ple_gate, gather_late=True, early_reduce=early_reduce,
        w_in_reduce=lambda grad_t: _pair_sum_windows(grad_t, _pair_exchange_windows(grad_t), ci))

    sums = [g["w_in_t"][0]] + [g[n][0] for n in SHARDED[1:]]
    recv = [g["w_in_t"][1]] + [g[n][1] for n in SHARDED[1:]]
    halves = [_chip_sum(sums[a], recv[a], chip, "chip_sum_%d" % a) for a in range(4)]

    rows = [jnp.pad(g["b_f"], ((0, 0), (0, D - H)))] + [g[n] for n in SMALL_ROWS[1:]]
    rows.append(jnp.pad(loss.reshape(1, 1), ((0, 0), (0, D - 1))))
    packed = jnp.concatenate([g["w_rgate"].reshape(NB * LANES, LANES), g["w_igate"].reshape(NB * LANES, LANES),
                              jnp.concatenate(rows, axis=0).reshape(LANES, LANES)], axis=0)
    theirs, summed = _final_exchange(halves, packed)
    full = [jnp.concatenate([jnp.where(ci == 0, a, b), jnp.where(ci == 0, b, a)], axis=0)
            for a, b in zip(halves, theirs)]
    red = dict(zip(SHARDED, full))
    red["w_in"] = lax.dynamic_slice_in_dim(red["w_in"], 2 * chip, SHARD_ROWS, axis=0)
    red["w_rgate"] = summed[:D].reshape(1, NB, LANES, LANES)
    red["w_igate"] = summed[D:2 * D].reshape(1, NB, LANES, LANES)
    vec = summed[2 * D:].reshape(16, D)
    loss = vec[15, 0]
    r0 = 0
    for n in SMALL_ROWS:
        nr = 4 if n == "conv_w" else 1
        red[n] = vec[r0:r0 + nr]
        r0 += nr
    red["b_f"] = red["b_f"][:, :H]
    red["conv_w"] = lax.dynamic_slice_in_dim(red["conv_w"], chip * (D // N_CHIPS), D // N_CHIPS, axis=1)[None]

    delta, new_m, new_v = {}, {}, {}
    outs_in = _adamw_big(red["w_in"], w_in_t, m_in_t, v_in_t, "adamw_w_in")
    delta["w_in"], new_m["w_in"], new_v["w_in"] = (jnp.swapaxes(t, 0, 1)[None] for t in outs_in)
    red["w_in"] = jnp.swapaxes(red["w_in"], 0, 1)[None]
    for n in SHARDED[1:]:
        delta[n], new_m[n], new_v[n] = (t[None] for t in _adamw_big(red[n], w[n][0], m[n][0], v[n][0], "adamw_" + n))
        red[n] = red[n][None]
    small = [n for n in WEIGHTS if n not in SHARDED]
    outs = _adamw_small([red[n] for n in small], [w[n] for n in small], [m[n] for n in small],
                        [v[n] for n in small])
    ns = len(small)
    for a, n in enumerate(small):
        delta[n], new_m[n], new_v[n] = outs[a], outs[ns + a], outs[2 * ns + a]

    return (loss, grad_x[None], *[red[n] for n in WEIGHTS], *[delta[n] for n in WEIGHTS],
            *[new_m[n] for n in WEIGHTS], *[new_v[n] for n in WEIGHTS])
```

```python
import jax
import jax.numpy as jnp
import numpy as np
from jax import lax
from jax.experimental import pallas as pl
from jax.experimental.pallas import tpu as pltpu

F32 = jnp.float32
BF16 = jnp.bfloat16

D = 1024
H = 8
DH = 128
NB = 8
DPLE = 256
DMIX = 2 * D
D_IN = 4 * D + H + 2 * D
FL0 = 3 * D
RMS_EPS = 1e-6
LRU_C = 8.0
NEG = -1e30
LANES = 128
SUBLANES = 8

ADAM_LR = 0.001
ADAM_B1 = 0.9
ADAM_B2 = 0.999
ADAM_EPS = 1e-08
ADAM_WD = 0.01
ADAM_STEP = 10

TM = 256
TA = 512
FWD_HEADS = 4
BWD_HEADS = 2
VMEM_BIG = 56 * 1024 * 1024
VMEM_MID = 40 * 1024 * 1024

MESH = pl.DeviceIdType.MESH
N_CHIPS = 4
N_DEV = 8


def _call(body, *, out_shape, in_hbm=True, **kwargs):
    if not in_hbm:
        return pl.pallas_call(body, out_shape=out_shape, **kwargs)

    def pin(shape):
        return pltpu.HBM(shape.shape, shape.dtype) if isinstance(shape, jax.ShapeDtypeStruct) else shape

    fn = pl.pallas_call(body, out_shape=jax.tree.map(pin, out_shape), **kwargs)

    def run(*args):
        return fn(*[a if a.dtype == jnp.int32 else pltpu.with_memory_space_constraint(a, pltpu.HBM) for a in args])

    return run


def _cparams(sem, vmem=VMEM_MID):
    return pltpu.CompilerParams(dimension_semantics=sem, vmem_limit_bytes=vmem)


def _sigmoid(x):
    return 0.5 * jnp.tanh(0.5 * x) + 0.5


def _rstd(x):
    return lax.rsqrt(jnp.mean(x * x, axis=-1, keepdims=True) + RMS_EPS)


def _rms_bwd(t, xhat, rstd):
    return rstd * (t - xhat * jnp.mean(t * xhat, axis=-1, keepdims=True))


def _dot(a, b):
    return jnp.dot(a, b, preferred_element_type=F32)


def _dot_nt(a, b):
    return lax.dot_general(a, b, (((1,), (1,)), ((), ())), preferred_element_type=F32)


def _dot_tn(a, b):
    return lax.dot_general(a, b, (((0,), (0,)), ((), ())), preferred_element_type=F32)


def _dot_exact(a, b):
    return jnp.dot(a, b, preferred_element_type=F32, precision=lax.Precision.HIGHEST)


def _shift_down(x, j, halo):
    rolled = pltpu.roll(x, j, 0)
    row = lax.broadcasted_iota(jnp.int32, halo.shape, 0)
    top = jnp.where(row < j, pltpu.roll(halo, j, 0), rolled[:SUBLANES])
    return jnp.concatenate([top, rolled[SUBLANES:]], axis=0)


def _shift_up(x, j, nxt):
    tm = x.shape[0]
    rolled = pltpu.roll(x, tm - j, 0)
    row = lax.broadcasted_iota(jnp.int32, nxt.shape, 0)
    bot = jnp.where(row >= SUBLANES - j, pltpu.roll(nxt, SUBLANES - j, 0), rolled[tm - SUBLANES:])
    return jnp.concatenate([rolled[:tm - SUBLANES], bot], axis=0)


def _scan_fwd_into(a, u, carry, h_ref):
    tm, width = a.shape
    groups = (tm // SUBLANES, SUBLANES, width)
    a, u = a.reshape(groups), u.reshape(groups)
    sub = lax.broadcasted_iota(jnp.int32, groups, 1)
    d = 1
    while d < SUBLANES:
        keep = sub >= d
        a_s = jnp.where(keep, pltpu.roll(a, d, 1), 1.0)
        u_s = jnp.where(keep, pltpu.roll(u, d, 1), 0.0)
        u = u + a * u_s
        a = a * a_s
        d *= 2
    a, u = a.reshape(tm, width), u.reshape(tm, width)
    for g in range(tm // SUBLANES):
        rows = slice(g * SUBLANES, (g + 1) * SUBLANES)
        h_ref[rows, :] = u[rows] + a[rows] * carry
        carry = h_ref[(g + 1) * SUBLANES - 1:(g + 1) * SUBLANES, :]
    return carry


def _scan_bwd_into(b, u, g_ref):
    tm, width = b.shape
    groups = (tm // SUBLANES, SUBLANES, width)
    b, u = b.reshape(groups), u.reshape(groups)
    sub = lax.broadcasted_iota(jnp.int32, groups, 1)
    d = 1
    while d < SUBLANES:
        keep = sub < SUBLANES - d
        b_s = jnp.where(keep, pltpu.roll(b, SUBLANES - d, 1), 1.0)
        u_s = jnp.where(keep, pltpu.roll(u, SUBLANES - d, 1), 0.0)
        u = u + b * u_s
        b = b * b_s
        d *= 2
    b, u = b.reshape(tm, width), u.reshape(tm, width)
    nxt = jnp.zeros((1, width), F32)
    for g in reversed(range(tm // SUBLANES)):
        rows = slice(g * SUBLANES, (g + 1) * SUBLANES)
        g_ref[rows, :] = u[rows] + b[rows] * nxt
        nxt = g_ref[g * SUBLANES:g * SUBLANES + 1, :]


def _gate_pre(xc, w_ref):
    outs = []
    for n in range(NB):
        outs.append(_dot(xc[:, n * LANES:(n + 1) * LANES].astype(BF16), w_ref[n]))
    return jnp.concatenate(outs, axis=1)


def _gate_pre_t(d, w_ref):
    outs = []
    for n in range(NB):
        outs.append(_dot_nt(d[:, n * LANES:(n + 1) * LANES].astype(BF16), w_ref[n]))
    return jnp.concatenate(outs, axis=1)


def _softplus_neg(lam):
    return jnp.maximum(-lam, 0.0) + jnp.log(1.0 + jnp.exp(-jnp.abs(lam)))


def _row_spec(tm, width):
    return pl.BlockSpec((tm, width), lambda i: (i, 0))


def _const_spec(shape):
    nd = len(shape)
    return pl.BlockSpec(shape, lambda *_: (0,) * nd)


def _weight_spec(shape):
    nd = len(shape)
    return pl.BlockSpec(shape, lambda *_: (0,) * nd, pipeline_mode=pl.Buffered(1))


AUG = 2 * DH
LOG2E = 1.4426950408889634
LN2 = 0.6931471805599453
Q_SCALE = DH ** -0.5 * LOG2E


def _split3(x):
    hi = x.astype(BF16)
    r1 = x - hi.astype(F32)
    mid = r1.astype(BF16)
    lo = (r1 - mid.astype(F32)).astype(BF16)
    return hi, mid, lo


def _extras(col, ones_from):
    t = col.shape[0]
    hi, mid, lo = _split3(jnp.broadcast_to(col, (t, LANES)))
    lane = lax.broadcasted_iota(jnp.int32, (t, LANES), 1)
    rest = jnp.zeros((t, LANES), BF16)
    if ones_from is not None:
        rest = jnp.where((lane >= ones_from) & (lane < ones_from + 3), 1.0, 0.0).astype(BF16)
    return jnp.where(lane == 0, hi, jnp.where(lane == 1, mid, jnp.where(lane == 2, lo, rest)))


def _selectors():
    sel_q = np.zeros((3 * LANES, H * LANES), np.float32)
    sel_k = np.zeros((3 * LANES, H * LANES), np.float32)
    for hd in range(H):
        for piece in range(3):
            sel_q[piece * LANES + hd, hd * LANES + piece] = 1.0
            sel_k[piece * LANES + hd, hd * LANES + 3 + piece] = -1.0
    return jnp.asarray(sel_q, BF16), jnp.asarray(sel_k, BF16)


def _in_proj(x, pre_gain, w_a, w_f, w_b, b_f_pad):
    T = x.shape[0]
    tm = TM
    sel_q, sel_k = _selectors()

    def body(x_ref, g_ref, wa_ref, wf_ref, wb_ref, bf_ref, sq_ref, sk_ref,
             xn_ref, qa_ref, ka_ref, va_ref, ga_ref, xl_ref, gl_ref, flb_ref, vt_ref, c_s, carry):
        @pl.when(pl.program_id(0) == 0)
        def _():
            carry[...] = jnp.zeros_like(carry)

        xv = x_ref[...]
        xn = (xv * _rstd(xv) * g_ref[...]).astype(BF16)
        xn_ref[...] = xn
        for s, o_ref in enumerate((ga_ref, xl_ref, gl_ref)):
            o_ref[...] = _dot_nt(xn, wb_ref[s * D:(s + 1) * D, :]).astype(o_ref.dtype)
        flb = _dot_nt(xn, wf_ref[...]) + bf_ref[...]
        flb_ref[...] = flb
        lane = lax.broadcasted_iota(jnp.int32, flb.shape, 1)
        ls = jnp.where(lane < H, jnp.minimum(flb, 0.0) - jnp.log(1.0 + jnp.exp(-jnp.abs(flb))), 0.0)
        r = lax.broadcasted_iota(jnp.int32, (tm, tm), 0)
        c = lax.broadcasted_iota(jnp.int32, (tm, tm), 1)
        cs = _dot_exact((c <= r).astype(F32), ls) + carry[...]
        c_s[...] = cs
        carry[...] = c_s[tm - 1:tm, :]

        pieces = jnp.concatenate(_split3(cs * LOG2E), axis=1)
        ones_q = jnp.where((lane >= 3) & (lane < 6), 1.0, 0.0)
        ones_k = jnp.where(lane < 3, 1.0, 0.0)
        zq = _dot_nt(xn, wa_ref[0:D, :]) * Q_SCALE
        zk = _dot_nt(xn, wa_ref[D:2 * D, :])
        zv = _dot_nt(xn, wa_ref[2 * D:3 * D, :])
        ex_q = _dot(pieces, sq_ref[...])
        ex_k = _dot(pieces, sk_ref[...])
        for hd in range(H):
            head = slice(hd * DH, (hd + 1) * DH)
            lo, hi = hd * AUG, hd * AUG + DH
            qa_ref[:, lo:hi] = zq[:, head].astype(BF16)
            qa_ref[:, hi:hi + DH] = (ex_q[:, head] + ones_q).astype(BF16)
            ka_ref[:, lo:hi] = zk[:, head].astype(BF16)
            ka_ref[:, hi:hi + DH] = (ex_k[:, head] + ones_k).astype(BF16)
            va_ref[:, lo:hi] = zv[:, head].astype(BF16)
            va_ref[:, hi:hi + DH] = ones_k.astype(BF16)
            vt_ref[lo:hi, :] = jnp.transpose(zv[:, head]).astype(BF16)
            vt_ref[hi:hi + DH, :] = jnp.where(lax.broadcasted_iota(jnp.int32, (DH, tm), 0) < 3, 1.0, 0.0).astype(BF16)

    bf = jax.ShapeDtypeStruct((T, D), BF16)
    aug = jax.ShapeDtypeStruct((T, H * AUG), BF16)
    f32 = jax.ShapeDtypeStruct((T, D), F32)
    sel_spec = _const_spec((3 * LANES, H * LANES))
    return _call(
        body, name="in_proj", grid=(T // tm,),
        in_specs=[_row_spec(tm, D), _const_spec((1, D)), _const_spec((3 * D, D)), _const_spec((LANES, D)),
                  _const_spec((3 * D, D)), _const_spec((1, LANES)), sel_spec, sel_spec],
        out_specs=[_row_spec(tm, D)] + [_row_spec(tm, H * AUG)] * 3 + [_row_spec(tm, D)] * 3 + [_row_spec(tm, LANES)]
        + [pl.BlockSpec((H * AUG, tm), lambda i: (0, i))],
        out_shape=[bf, aug, aug, aug, f32, f32, f32, jax.ShapeDtypeStruct((T, LANES), F32),
                   jax.ShapeDtypeStruct((H * AUG, T), BF16)],
        scratch_shapes=[pltpu.VMEM((tm, LANES), F32), pltpu.VMEM((1, LANES), F32)],
        compiler_params=_cparams(("arbitrary",), VMEM_BIG),
    )(x, pre_gain, w_a, w_f, w_b, b_f_pad, sel_q, sel_k)


def _causal_pairs(n, q_major):
    if q_major:
        pairs = [(qi, ki) for qi in range(n) for ki in range(qi + 1)]
    else:
        pairs = [(ki, qi) for ki in range(n) for qi in range(ki, n)]
    return (jnp.asarray([a for a, _ in pairs], jnp.int32), jnp.asarray([b for _, b in pairs], jnp.int32))


def _attn_fwd(q_aug, k_aug, vt_aug, shards=(), whole=()):
    T = q_aug.shape[0]
    t = TA
    n = T // t
    hp = FWD_HEADS
    heads = range(hp)
    qi_tab, ki_tab = _causal_pairs(n, q_major=True)
    na, nall = len(shards), len(shards) + len(whole)
    n_h, n_j = H // hp, qi_tab.shape[0]

    def body(qi_ref, ki_ref, q_ref, k_ref, vt_ref, *rest):
        srcs, rest = rest[:nall], rest[nall:]
        o_ref, qx_ref = rest[:2]
        dsts, rest = rest[2:2 + nall], rest[2 + nall:]
        m_s, acc_s = rest[:2]
        h = pl.program_id(0)
        j = pl.program_id(1)
        qi = qi_ref[j]
        ki = ki_ref[j]

        if nall:
            gather = _GatherPlan(srcs, dsts, rest[2:], na)
            pl.when((h == 0) & (j == 0))(gather.send)
            pl.when((h == n_h - 1) & (j == 0))(gather.forward)
            pl.when((h == n_h - 1) & (j == n_j - 1))(gather.finish)

        @pl.when(ki == 0)
        def _():
            m_s[...] = jnp.full(m_s.shape, NEG, F32)
            acc_s[...] = jnp.zeros_like(acc_s)

        def step(on_diagonal):
            cols = [slice(a * AUG, (a + 1) * AUG) for a in heads]
            if on_diagonal:
                krow = lax.broadcasted_iota(jnp.int32, (t, t), 0)
                qcol = lax.broadcasted_iota(jnp.int32, (t, t), 1)
            def logits(a):
                st = _dot_nt(k_ref[:, cols[a]], q_ref[:, cols[a]])
                return jnp.where(krow <= qcol, st, NEG) if on_diagonal else st

            st_next = logits(0)
            for a in heads:
                st = st_next
                if a + 1 < hp:
                    st_next = logits(a + 1)
                m_prev = m_s[a]
                m_new = jnp.maximum(m_prev, jnp.max(st, axis=0, keepdims=True))
                pt = jnp.exp2(st - m_new).astype(BF16)
                acc_s[a] = jnp.exp2(m_prev - m_new) * acc_s[a] + _dot(vt_ref[cols[a], :], pt)
                m_s[a] = m_new

        @pl.when(ki < qi)
        def _():
            step(False)

        @pl.when(ki == qi)
        def _():
            step(True)
            piece = lax.broadcasted_iota(jnp.int32, (DH, t), 0)
            for a in heads:
                l = acc_s[a, DH:DH + 1, :]
                ex = jnp.transpose(q_ref[:, a * AUG + DH:(a + 1) * AUG].astype(F32))
                c2 = jnp.sum(jnp.where(piece < 3, ex, 0.0), axis=0, keepdims=True)
                hi, mid, lo = _split3(jnp.broadcast_to(c2 - (m_s[a] + jnp.log(l) * LOG2E), (DH, t)))
                ones = jnp.where((piece >= 3) & (piece < 6), 1.0, 0.0).astype(BF16)
                ex_t = jnp.where(piece == 0, hi, jnp.where(piece == 1, mid, jnp.where(piece == 2, lo, ones)))
                o_ref[:, a * DH:(a + 1) * DH] = jnp.transpose(acc_s[a, :DH, :] / l)
                qx_ref[:, a * DH:(a + 1) * DH] = jnp.transpose(ex_t.astype(F32)).astype(BF16)

    q_spec = pl.BlockSpec((t, hp * AUG), lambda h, j, qi_ref, ki_ref: (qi_ref[j], h))
    k_spec = pl.BlockSpec((t, hp * AUG), lambda h, j, qi_ref, ki_ref: (ki_ref[j], h))
    vt_spec = pl.BlockSpec((hp * AUG, t), lambda h, j, qi_ref, ki_ref: (h, ki_ref[j]))
    out_spec = pl.BlockSpec((t, hp * DH), lambda h, j, qi_ref, ki_ref: (qi_ref[j], h))
    arrs = list(shards) + list(whole)
    grid_spec = pltpu.PrefetchScalarGridSpec(
        num_scalar_prefetch=2, grid=(n_h, n_j),
        in_specs=[q_spec, k_spec, vt_spec] + [HBM_SPEC] * nall, out_specs=[out_spec, out_spec] + [HBM_SPEC] * nall,
        scratch_shapes=[pltpu.VMEM((hp, 1, t), F32), pltpu.VMEM((hp, AUG, t), F32)]
        + (_gather_semaphores(na, nall) if nall else []))
    outs = _call(
        body, name="attn_fwd", grid_spec=grid_spec,
        out_shape=[jax.ShapeDtypeStruct((T, D), F32), jax.ShapeDtypeStruct((T, D), BF16)] + _gather_out_shapes(arrs),
        compiler_params=_cparams(("arbitrary", "arbitrary"), VMEM_BIG),
    )(qi_tab, ki_tab, q_aug, k_aug, vt_aug, *arrs)
    return outs[0], outs[1], _place_own(outs[2:], arrs)


def _lru_gates(xc, wr_ref, br_ref, wi_ref, bi_ref, lam_ref):
    r = _sigmoid(_gate_pre(xc, wr_ref) + br_ref[...])
    ig = _sigmoid(_gate_pre(xc, wi_ref) + bi_ref[...])
    sp = _softplus_neg(lam_ref[...])
    la = (-LRU_C) * r * sp
    a = jnp.exp(la)
    y = -jnp.tanh(la) * (a * a + 1.0)
    return r, ig, sp, a, jnp.sqrt(y), lax.rsqrt(y)


def _branches_fwd(o, g_attn, x_lru, g_lru, gain_a, gain_l, conv_w, conv_b, w_r, b_r, w_i, b_i, lam):
    T = o.shape[0]
    tm = TM

    def body(o_ref, ga_ref, xl_ref, gl_ref, gna_ref, gnl_ref, cw_ref, cb_ref, wr_ref, br_ref, wi_ref, bi_ref,
             lam_ref, ycat_ref, xc_ref, h_ref, halo_s, hc_s):
        @pl.when(pl.program_id(0) == 0)
        def _():
            halo_s[...] = jnp.zeros_like(halo_s)
            hc_s[...] = jnp.zeros_like(hc_s)

        ov = o_ref[...]
        ga = ga_ref[...]
        ya = ov * _rstd(ov) * gna_ref[...] * (ga * _sigmoid(ga))
        ycat_ref[:, :D] = ya.astype(BF16)

        xl = xl_ref[...]
        halo = halo_s[...]
        xc = xl * cw_ref[3:4, :] + cb_ref[...]
        for j in range(3):
            xc = xc + _shift_down(xl, 3 - j, halo) * cw_ref[j:j + 1, :]
        halo_s[...] = xl_ref[tm - SUBLANES:tm, :]
        xc_ref[...] = xc

        _, ig, _, a, sq, _ = _lru_gates(xc, wr_ref, br_ref, wi_ref, bi_ref, lam_ref)
        u = sq * (ig * xc)
        hc_s[...] = _scan_fwd_into(a, u, hc_s[...], h_ref)
        hh = h_ref[...]

        gl = gl_ref[...]
        yl = hh * _rstd(hh) * gnl_ref[...] * (gl * _sigmoid(gl))
        ycat_ref[:, D:] = yl.astype(BF16)

    vec = _const_spec((1, D))
    wspec = _const_spec((NB, LANES, LANES))
    return _call(
        body, name="branches_fwd", grid=(T // tm,),
        in_specs=[_row_spec(tm, D)] * 4 + [vec, vec, _const_spec((4, D)), vec, wspec, vec, wspec, vec, vec],
        out_specs=[_row_spec(tm, DMIX), _row_spec(tm, D), _row_spec(tm, D)],
        out_shape=[jax.ShapeDtypeStruct((T, DMIX), BF16), jax.ShapeDtypeStruct((T, D), F32),
                   jax.ShapeDtypeStruct((T, D), F32)],
        scratch_shapes=[pltpu.VMEM((SUBLANES, D), F32), pltpu.VMEM((1, D), F32)],
        compiler_params=_cparams(("arbitrary",)),
    )(o, g_attn, x_lru, g_lru, gain_a, gain_l, conv_w, conv_b, w_r, b_r, w_i, b_i, lam)


def _tail(ycat, x, p, tgt, w_out, post_gain, w_ple, ple_gain, w_gate, b_gate):
    T = x.shape[0]
    tm = TM

    def body(ycat_ref, x_ref, p_ref, t_ref, wo_ref, pg_ref, wp_ref, eg_ref, wg_ref, bg_ref,
             dh1_ref, dycat_ref, dmix_ref, h1b_ref, dgp_ref, pb_ref, dpe_ref, acc_ref):
        @pl.when(pl.program_id(0) == 0)
        def _():
            acc_ref[...] = jnp.zeros_like(acc_ref)

        mix = _dot(ycat_ref[...], wo_ref[...])
        rstd_m = _rstd(mix)
        mhat = mix * rstd_m
        h1 = x_ref[...] + mhat * pg_ref[...]
        pb = p_ref[...].astype(BF16)
        pb_ref[...] = pb
        pe = _dot(pb, wp_ref[...])
        rstd_p = _rstd(pe)
        pehat = pe * rstd_p
        e = pehat * eg_ref[...]
        h1b = h1.astype(BF16)
        h1b_ref[...] = h1b
        gate = _sigmoid(_dot(h1b, wg_ref[...]) + bg_ref[...])
        diff = (h1 + gate * e) - t_ref[...]

        dy = diff * (1.0 / D)
        de = dy * gate
        dgp = (dy * e) * gate * (1.0 - gate)
        dgpb = dgp.astype(BF16)
        dgp_ref[...] = dgpb
        dh1 = dy + _dot_nt(dgpb, wg_ref[...])
        dh1_ref[...] = dh1
        dpe_ref[...] = _rms_bwd(de * eg_ref[...], pehat, rstd_p).astype(BF16)
        dmix = _rms_bwd(dh1 * pg_ref[...], mhat, rstd_m).astype(BF16)
        dmix_ref[...] = dmix
        dycat_ref[...] = _dot_nt(dmix, wo_ref[...])

        acc_ref[0:1, :] += jnp.sum(dh1 * mhat, axis=0, keepdims=True)
        acc_ref[1:2, :] += jnp.sum(de * pehat, axis=0, keepdims=True)
        acc_ref[2:3, :] += jnp.sum(dgp, axis=0, keepdims=True)
        acc_ref[3:4, :] += jnp.sum(diff * diff, axis=0, keepdims=True) * (0.5 / D)

    vec = _const_spec((1, D))
    bf = jax.ShapeDtypeStruct((T, D), BF16)
    return _call(
        body, name="tail", grid=(T // tm,),
        in_specs=[_row_spec(tm, DMIX), _row_spec(tm, D), _row_spec(tm, DPLE), _row_spec(tm, D),
                  _const_spec((DMIX, D)), vec, _const_spec((DPLE, D)), vec, _const_spec((D, D)), vec],
        out_specs=[_row_spec(tm, D), _row_spec(tm, DMIX), _row_spec(tm, D), _row_spec(tm, D), _row_spec(tm, D),
                   _row_spec(tm, DPLE), _row_spec(tm, D), _const_spec((SUBLANES, D))],
        out_shape=[jax.ShapeDtypeStruct((T, D), F32), jax.ShapeDtypeStruct((T, DMIX), F32), bf, bf, bf,
                   jax.ShapeDtypeStruct((T, DPLE), BF16), bf, jax.ShapeDtypeStruct((SUBLANES, D), F32)],
        compiler_params=_cparams(("arbitrary",), VMEM_BIG),
    )(ycat, x, p, tgt, w_out, post_gain, w_ple, ple_gain, w_gate, b_gate)


def _branches_bwd(dycat, o, g_attn, h, g_lru, gain_a, gain_l):
    T = o.shape[0]
    tm = TM

    def body(dy_ref, o_ref, ga_ref, h_ref, gl_ref, gna_ref, gnl_ref,
             do_ref, dga_ref, dgl_ref, dh_ref, acc_ref):
        @pl.when(pl.program_id(0) == 0)
        def _():
            acc_ref[...] = jnp.zeros_like(acc_ref)

        def branch(val, g, gain, dyv):
            rstd = _rstd(val)
            vhat = val * rstd
            sig = _sigmoid(g)
            dn = dyv * (g * sig)
            dg = dyv * (vhat * gain) * (sig * (1.0 + g * (1.0 - sig)))
            dgain = jnp.sum(dn * vhat, axis=0, keepdims=True)
            return _rms_bwd(dn * gain, vhat, rstd), dg, dgain

        ov = o_ref[...]
        do, dga, dgain_a = branch(ov, ga_ref[...], gna_ref[...], dy_ref[:, :D])
        dga_ref[...] = dga.astype(BF16)
        prod = do * ov
        for hd in range(H):
            head = slice(hd * DH, (hd + 1) * DH)
            do_ref[:, hd * AUG:hd * AUG + DH] = do[:, head].astype(BF16)
            do_ref[:, hd * AUG + DH:(hd + 1) * AUG] = _extras(-jnp.sum(prod[:, head], axis=1, keepdims=True), None)

        dh, dgl, dgain_l = branch(h_ref[...], gl_ref[...], gnl_ref[...], dy_ref[:, D:])
        dh_ref[...] = dh
        dgl_ref[...] = dgl.astype(BF16)
        acc_ref[0:1, :] += dgain_a
        acc_ref[1:2, :] += dgain_l

    vec = _const_spec((1, D))
    bf = jax.ShapeDtypeStruct((T, D), BF16)
    return _call(
        body, name="branches_bwd", grid=(T // tm,),
        in_specs=[_row_spec(tm, DMIX)] + [_row_spec(tm, D)] * 4 + [vec, vec],
        out_specs=[_row_spec(tm, H * AUG), _row_spec(tm, D), _row_spec(tm, D), _row_spec(tm, D),
                   _const_spec((SUBLANES, D))],
        out_shape=[jax.ShapeDtypeStruct((T, H * AUG), BF16), bf, bf, jax.ShapeDtypeStruct((T, D), F32),
                   jax.ShapeDtypeStruct((SUBLANES, D), F32)],
        compiler_params=_cparams(("arbitrary",)),
    )(dycat, o, g_attn, h, g_lru, gain_a, gain_l)


def _lru_bwd(dh, h, xc, x_lru, conv_w, w_r, b_r, w_i, b_i, lam):
    T = dh.shape[0]
    tm = TM
    nt = T // tm
    per = tm // SUBLANES

    def body(dh_ref, h_ref, hprev_ref, xc_ref, xl_ref, cw_ref, wr_ref, br_ref, wi_ref, bi_ref, lam_ref,
             dxl_ref, dwr_ref, dwi_ref, acc_ref, carry_s, dxc_next_s, top_s, dht_s):
        i = pl.program_id(0)

        @pl.when(i == 0)
        def _():
            acc_ref[...] = jnp.zeros_like(acc_ref)
            dwr_ref[...] = jnp.zeros_like(dwr_ref)
            dwi_ref[...] = jnp.zeros_like(dwi_ref)
            carry_s[...] = jnp.zeros_like(carry_s)
            dxc_next_s[...] = jnp.zeros_like(dxc_next_s)

        inner = jnp.where(i == nt - 1, 0.0, 1.0)
        xc = xc_ref[...]
        r, ig, sp, a, sq, inv_sq = _lru_gates(xc, wr_ref, br_ref, wi_ref, bi_ref, lam_ref)

        row = lax.broadcasted_iota(jnp.int32, (tm, D), 0)
        u = dh_ref[...] + jnp.where(row == tm - 1, carry_s[...], 0.0)
        _scan_bwd_into(pltpu.roll(a, tm - 1, 0), u, dht_s)
        dht = dht_s[...]
        top_s[...] = a[:SUBLANES, :] * dht[:SUBLANES, :]
        carry_s[...] = top_s[0:1, :]

        hprev = hprev_ref[...] * inner
        da = dht * _shift_down(h_ref[...], 1, hprev)
        dig = dht * sq * xc
        dxc = dht * sq * ig
        dsq = dht * ig * xc
        dla = da * a - dsq * (a * a) * inv_sq
        dr = dla * ((-LRU_C) * sp)
        dpr = dr * r * (1.0 - r)
        dpi = dig * ig * (1.0 - ig)
        for n in range(NB):
            blk = slice(n * LANES, (n + 1) * LANES)
            xcb = xc[:, blk].astype(BF16)
            dwr_ref[n] += _dot_tn(xcb, dpr[:, blk].astype(BF16))
            dwi_ref[n] += _dot_tn(xcb, dpi[:, blk].astype(BF16))
        dxc = dxc + _gate_pre_t(dpr, wr_ref) + _gate_pre_t(dpi, wi_ref)

        xl = xl_ref[...]
        nxt = dxc_next_s[...]
        dxl = dxc * cw_ref[3:4, :]
        acc_ref[3:4, :] += jnp.sum(dxc * xl, axis=0, keepdims=True)
        for j in range(3):
            ahead = _shift_up(dxc, 3 - j, nxt)
            dxl = dxl + ahead * cw_ref[j:j + 1, :]
            acc_ref[j:j + 1, :] += jnp.sum(ahead * xl, axis=0, keepdims=True)
        dxc_next_s[...] = dxc[:SUBLANES, :]
        dxl_ref[...] = dxl.astype(BF16)

        acc_ref[4:5, :] += jnp.sum(dxc, axis=0, keepdims=True)
        acc_ref[5:6, :] += jnp.sum(dpr, axis=0, keepdims=True)
        acc_ref[6:7, :] += jnp.sum(dpi, axis=0, keepdims=True)
        acc_ref[7:8, :] += jnp.sum(dla * ((-LRU_C) * r), axis=0, keepdims=True)

        @pl.when(i == nt - 1)
        def _():
            lam_v = lam_ref[...]
            acc_ref[7:8, :] = acc_ref[7:8, :] * (-_sigmoid(-lam_v))

    rev = pl.BlockSpec((tm, D), lambda i: (nt - 1 - i, 0))
    prev8 = pl.BlockSpec((SUBLANES, D), lambda i: (jnp.maximum((nt - 1 - i) * per - 1, 0), 0))
    vec = _const_spec((1, D))
    wspec = _const_spec((NB, LANES, LANES))
    bf = jax.ShapeDtypeStruct((T, D), BF16)
    return _call(
        body, name="lru_bwd", grid=(nt,),
        in_specs=[rev, rev, prev8, rev, rev, _const_spec((4, D)), wspec, vec, wspec, vec, vec],
        out_specs=[rev, wspec, wspec, _const_spec((SUBLANES, D))],
        out_shape=[bf, jax.ShapeDtypeStruct((NB, LANES, LANES), F32), jax.ShapeDtypeStruct((NB, LANES, LANES), F32),
                   jax.ShapeDtypeStruct((SUBLANES, D), F32)],
        scratch_shapes=[pltpu.VMEM((1, D), F32), pltpu.VMEM((SUBLANES, D), F32), pltpu.VMEM((SUBLANES, D), F32),
                        pltpu.VMEM((tm, D), F32)],
        compiler_params=_cparams(("arbitrary",)),
    )(dh, h, h, xc, x_lru, conv_w, w_r, b_r, w_i, b_i, lam)


def _chip_copies(srcs, dsts, send_sems, recv_sems):
    x, y, c = _position()
    chip = 2 * x + y
    na = len(srcs)
    return [pltpu.make_async_remote_copy(
        src_ref=srcs[a].at[2 * px + py], dst_ref=dsts[a].at[chip], send_sem=send_sems.at[j * na + a],
        recv_sem=recv_sems.at[j * na + a], device_id=(px, py, c), device_id_type=MESH)
        for j, (px, py) in enumerate(_other_chips(x, y)) for a in range(na)]


def _attn_bwd(q_aug, qx, k_aug, v_aug, do_aug, exchange=()):
    T = q_aug.shape[0]
    t = TA
    n = T // t
    hp = BWD_HEADS
    heads = range(hp)
    scale = DH ** -0.5
    ki_tab, qi_tab = _causal_pairs(n, q_major=False)
    last = ki_tab.shape[0] - 1
    ne = len(exchange)
    n_h = H // hp

    def body(ki_ref, qi_ref, q_ref, qx_ref, k_ref, v_ref, do_ref, *rest):
        sent, rest = rest[:ne], rest[ne:]
        dq_ref, dk_ref, dv_ref, dc_ref = rest[:4]
        received, rest = rest[4:4 + ne], rest[4 + ne:]
        dq_s, dk_s, dv_s = rest[:3]
        j = pl.program_id(1)
        ki = ki_ref[j]
        qi = qi_ref[j]

        if ne:
            first_step = (pl.program_id(0) == 0) & (j == 0)
            last_step = (pl.program_id(0) == n_h - 1) & (j == last)

            @pl.when(first_step)
            def _():
                for cp in _chip_copies(sent, received, *rest[3:]):
                    cp.start()

            @pl.when(last_step)
            def _():
                for cp in _chip_copies(sent, received, *rest[3:]):
                    cp.wait()

        @pl.when(j == 0)
        def _():
            dq_s[...] = jnp.zeros_like(dq_s)

        @pl.when(qi == ki)
        def _():
            dk_s[...] = jnp.zeros_like(dk_s)
            dv_s[...] = jnp.zeros_like(dv_s)

        def step(on_diagonal):
            cols = [slice(a * AUG, (a + 1) * AUG) for a in heads]
            qb = [jnp.concatenate([q_ref[:, a * AUG:a * AUG + DH], qx_ref[:, a * DH:(a + 1) * DH]], axis=1)
                  for a in heads]
            if on_diagonal:
                krow = lax.broadcasted_iota(jnp.int32, (t, t), 0)
                qcol = lax.broadcasted_iota(jnp.int32, (t, t), 1)

            def scores(a):
                st = _dot_nt(k_ref[:, cols[a]], qb[a])
                dpd = _dot_nt(v_ref[:, cols[a]], do_ref[:, cols[a]])
                return (jnp.where(krow <= qcol, st, NEG) if on_diagonal else st), dpd

            off = pl.multiple_of(qi * t, t)
            ahead = scores(0)
            for a in heads:
                st, dpd = ahead
                if a + 1 < hp:
                    ahead = scores(a + 1)
                pt = jnp.exp2(st)
                dsb = (pt * dpd).astype(BF16)
                dv_s[a] += _dot(pt.astype(BF16), do_ref[:, a * AUG:a * AUG + DH])
                dk_s[a] += _dot(dsb, qb[a])
                dq_s[a, pl.ds(off, t), :] += _dot_tn(dsb, k_ref[:, cols[a]])

        @pl.when(qi > ki)
        def _():
            step(False)

        @pl.when(qi == ki)
        def _():
            step(True)

        @pl.when(qi == n - 1)
        def _():
            rows = pl.ds(pl.multiple_of(ki * t, t), t)
            for a in heads:
                dk_ref[:, a * DH:(a + 1) * DH] = (dk_s[a, :, :DH] * LN2).astype(BF16)
                dv_ref[:, a * DH:(a + 1) * DH] = dv_s[a].astype(BF16)
                dc_ref[a, rows, :] = jnp.broadcast_to(-dk_s[a, :, DH + 3:DH + 4], (t, LANES))

        @pl.when(j == last)
        def _():
            for a in heads:
                dq_ref[:, a * DH:(a + 1) * DH] = (dq_s[a, :, :DH] * scale).astype(BF16)
                dc_ref[a] = dc_ref[a] + jnp.broadcast_to(dq_s[a, :, DH:DH + 1], (T, LANES))

    qside = pl.BlockSpec((t, hp * AUG), lambda h, j, ki_ref, qi_ref: (qi_ref[j], h))
    qxside = pl.BlockSpec((t, hp * DH), lambda h, j, ki_ref, qi_ref: (qi_ref[j], h))
    kside = pl.BlockSpec((t, hp * AUG), lambda h, j, ki_ref, qi_ref: (ki_ref[j], h))
    kout = pl.BlockSpec((t, hp * DH), lambda h, j, ki_ref, qi_ref: (ki_ref[j], h))
    bf = jax.ShapeDtypeStruct((T, D), BF16)
    sums = jax.ShapeDtypeStruct((H, T, LANES), F32)
    grid_spec = pltpu.PrefetchScalarGridSpec(
        num_scalar_prefetch=2, grid=(n_h, ki_tab.shape[0]),
        in_specs=[qside, qxside, kside, kside, qside] + [HBM_SPEC] * ne,
        out_specs=[pl.BlockSpec((T, hp * DH), lambda h, j, ki_ref, qi_ref: (0, h)), kout, kout,
                   pl.BlockSpec((hp, T, LANES), lambda h, j, ki_ref, qi_ref: (h, 0, 0))] + [HBM_SPEC] * ne,
        scratch_shapes=[pltpu.VMEM((hp, T, AUG), F32), pltpu.VMEM((hp, t, AUG), F32), pltpu.VMEM((hp, t, DH), F32)]
        + ([pltpu.SemaphoreType.DMA((3 * ne,)), pltpu.SemaphoreType.DMA((3 * ne,))] if ne else []))
    outs = _call(
        body, name="attn_bwd", grid_spec=grid_spec,
        out_shape=[bf, bf, bf, sums] + [jax.ShapeDtypeStruct(s.shape, s.dtype) for s in exchange],
        compiler_params=_cparams(("arbitrary", "arbitrary"), VMEM_BIG),
    )(ki_tab, qi_tab, q_aug, qx, k_aug, v_aug, do_aug, *exchange)
    return (*outs[:4], list(outs[4:]))


def _fgate_bwd(dc_heads, flb):
    T = flb.shape[0]
    tm = TM
    nt = T // tm

    def body(dch_ref, flb_ref, dfl_ref, acc_ref, carry, top_s):
        @pl.when(pl.program_id(0) == 0)
        def _():
            carry[...] = jnp.zeros_like(carry)
            acc_ref[...] = jnp.zeros_like(acc_ref)

        flb = flb_ref[...]
        lane = lax.broadcasted_iota(jnp.int32, flb.shape, 1)
        dc = jnp.zeros(flb.shape, F32)
        for hd in range(H):
            dc = dc + jnp.where(lane == hd, dch_ref[hd], 0.0)
        r = lax.broadcasted_iota(jnp.int32, (tm, tm), 0)
        c = lax.broadcasted_iota(jnp.int32, (tm, tm), 1)
        dls = _dot_exact((c >= r).astype(F32), dc) + carry[...]
        top_s[...] = dls[:SUBLANES, :]
        carry[...] = top_s[0:1, :]
        dfl = jnp.where(lane < H, dls * _sigmoid(-flb), 0.0)
        dfl_ref[...] = dfl.astype(BF16)
        acc_ref[0:1, :] += jnp.sum(dfl, axis=0, keepdims=True)

    rev = pl.BlockSpec((tm, LANES), lambda i: (nt - 1 - i, 0))
    return _call(
        body, name="fgate_bwd", grid=(nt,),
        in_specs=[pl.BlockSpec((H, tm, LANES), lambda i: (0, nt - 1 - i, 0)), rev],
        out_specs=[rev, _const_spec((SUBLANES, LANES))],
        out_shape=[jax.ShapeDtypeStruct((T, LANES), BF16), jax.ShapeDtypeStruct((SUBLANES, LANES), F32)],
        scratch_shapes=[pltpu.VMEM((1, LANES), F32), pltpu.VMEM((SUBLANES, LANES), F32)],
        compiler_params=_cparams(("arbitrary",)),
    )(dc_heads, flb)


def _dx(dz, dfl, w_a, w_f, w_b, x, pre_gain, dh1, exchange=()):
    T = x.shape[0]
    tm = TM
    nt = T // tm
    ne = len(exchange)

    def body(*refs):
        dz_refs = refs[:6]
        dfl_ref, wa_ref, wf_ref, wb_ref, x_ref, g_ref, dh1_ref = refs[6:13]
        sent = refs[13:13 + ne]
        gx_ref, acc_ref = refs[13 + ne:15 + ne]
        received, sems = refs[15 + ne:15 + 2 * ne], refs[15 + 2 * ne:]

        @pl.when(pl.program_id(0) == 0)
        def _():
            acc_ref[...] = jnp.zeros_like(acc_ref)
            for cp in _chip_copies(sent, received, *sems) if ne else ():
                cp.start()

        if ne:
            @pl.when(pl.program_id(0) == nt - 1)
            def _():
                for cp in _chip_copies(sent, received, *sems):
                    cp.wait()

        dxn = _dot(dfl_ref[...], wf_ref[...])
        for s in range(3):
            dxn = dxn + _dot(dz_refs[s][...], wa_ref[s * D:(s + 1) * D, :])
            dxn = dxn + _dot(dz_refs[3 + s][...], wb_ref[s * D:(s + 1) * D, :])
        xv = x_ref[...]
        rstd = _rstd(xv)
        xhat = xv * rstd
        gx_ref[...] = dh1_ref[...] + _rms_bwd(dxn * g_ref[...], xhat, rstd)
        acc_ref[0:1, :] += jnp.sum(dxn * xhat, axis=0, keepdims=True)

    outs = _call(
        body, name="dx", grid=(nt,),
        in_specs=[_row_spec(tm, D)] * 6 + [_row_spec(tm, LANES), _weight_spec((3 * D, D)), _weight_spec((LANES, D)),
                                           _weight_spec((3 * D, D)), _row_spec(tm, D), _const_spec((1, D)),
                                           _row_spec(tm, D)] + [HBM_SPEC] * ne,
        out_specs=[_row_spec(tm, D), _const_spec((SUBLANES, D))] + [HBM_SPEC] * ne,
        out_shape=[jax.ShapeDtypeStruct((T, D), F32), jax.ShapeDtypeStruct((SUBLANES, D), F32)]
        + [jax.ShapeDtypeStruct(s.shape, s.dtype) for s in exchange],
        scratch_shapes=[pltpu.SemaphoreType.DMA((3 * ne,)), pltpu.SemaphoreType.DMA((3 * ne,))] if ne else [],
        compiler_params=_cparams(("arbitrary",), VMEM_BIG),
    )(*dz, dfl, w_a, w_f, w_b, x, pre_gain, dh1, *exchange)
    return outs[0], outs[1], list(outs[2:])


GRAD_ROWS = D_IN + SUBLANES


def _dw_in_segments(dz_a, dz_b, xn, buf, pair, bt):
    T = xn.shape[0]
    nt = T // bt
    first, second = [(2 * pair + k) * D + (H if 2 * pair + k >= 3 else 0) for k in (0, 1)]
    step8 = (second - first) // SUBLANES

    def body(*refs):
        dza_ref, dzb_ref, xn_ref, o_ref = refs[0], refs[1], refs[2], refs[-1]

        @pl.when(pl.program_id(1) == 0)
        def _():
            o_ref[...] = jnp.zeros_like(o_ref)

        @pl.when(pl.program_id(0) == 0)
        def _():
            o_ref[...] += _dot_tn(dza_ref[...], xn_ref[...])

        @pl.when(pl.program_id(0) == 1)
        def _():
            o_ref[...] += _dot_tn(dzb_ref[...], xn_ref[...])

    spec_a = pl.BlockSpec((bt, D), lambda s, t: (jnp.where(s == 0, t, nt - 1), 0))
    spec_b = pl.BlockSpec((bt, D), lambda s, t: (jnp.where(s == 1, t, 0), 0))
    return _call(
        body, name="dw_in_%d" % pair, grid=(2, nt),
        in_specs=[spec_a, spec_b, pl.BlockSpec((bt, D), lambda s, t: (t, 0))]
        + ([] if buf is None else [pl.BlockSpec(memory_space=pl.ANY)]),
        out_specs=pl.BlockSpec((pl.Element(D), pl.Element(D)),
                               lambda s, t: ((first // SUBLANES + s * step8) * SUBLANES, 0)),
        out_shape=jax.ShapeDtypeStruct((GRAD_ROWS, D), F32),
        input_output_aliases={} if buf is None else {3: 0},
        compiler_params=_cparams(("arbitrary", "arbitrary"), VMEM_BIG),
    )(*((dz_a, dz_b, xn) if buf is None else (dz_a, dz_b, xn, buf)))


def _dw_in_t(dz, dfl, xn, bt=2048):
    T = xn.shape[0]
    bt = min(bt, T)
    nt = T // bt
    main = None
    for pair in range(3):
        main = _dw_in_segments(dz[2 * pair], dz[2 * pair + 1], xn, main, pair, bt)

    def f_body(dfl_ref, xn_ref, main_ref, o_ref, acc_s):
        p = pl.program_id(0)
        t = pl.program_id(1)

        @pl.when(t == 0)
        def _():
            acc_s[...] = jnp.zeros_like(acc_s)

        @pl.when(p == 0)
        def _():
            acc_s[...] += _dot_tn(dfl_ref[...], xn_ref[...])

        @pl.when(t == nt - 1)
        def _():
            o_ref[...] = acc_s[:SUBLANES, :]

    fl_block = FL0 // SUBLANES
    end_block = D_IN // SUBLANES
    return _call(
        f_body, name="dw_in_f", grid=(2, nt),
        in_specs=[pl.BlockSpec((bt, LANES), lambda p, t: (t, 0)), pl.BlockSpec((bt, D), lambda p, t: (t, 0)),
                  pl.BlockSpec(memory_space=pl.ANY)],
        out_specs=pl.BlockSpec((SUBLANES, D), lambda p, t: (fl_block + p * (end_block - fl_block), 0)),
        out_shape=jax.ShapeDtypeStruct((GRAD_ROWS, D), F32),
        scratch_shapes=[pltpu.VMEM((LANES, D), F32)],
        input_output_aliases={2: 0},
        compiler_params=_cparams(("arbitrary", "arbitrary")),
    )(dfl, xn, main)


def _matmul_tn(a, b, name, bm=512, bn=1024, bt=2048):
    T, M = a.shape
    N = b.shape[1]
    bm, bn, bt = min(bm, M), min(bn, N), min(bt, T)

    def body(a_ref, b_ref, o_ref):
        @pl.when(pl.program_id(2) == 0)
        def _():
            o_ref[...] = jnp.zeros_like(o_ref)

        o_ref[...] += _dot_tn(a_ref[...], b_ref[...])

    return _call(
        body, name=name, grid=(M // bm, N // bn, T // bt),
        in_specs=[pl.BlockSpec((bt, bm), lambda i, j, t: (t, i)), pl.BlockSpec((bt, bn), lambda i, j, t: (t, j))],
        out_specs=pl.BlockSpec((bm, bn), lambda i, j, t: (i, j)),
        out_shape=jax.ShapeDtypeStruct((M, N), F32),
        compiler_params=_cparams(("parallel", "parallel", "arbitrary")),
    )(a, b)


HBM_SPEC = pl.BlockSpec(memory_space=pltpu.HBM)
VMEM_SPEC = pl.BlockSpec(memory_space=pltpu.VMEM)


def _position():
    return lax.axis_index("x"), lax.axis_index("y"), lax.axis_index("c")


def _other_chips(x, y):
    return [(1 - x, y), (x, 1 - y), (1 - x, 1 - y)]


def _gather_shards(shards, whole):
    na, nw = len(shards), len(whole)
    nall = na + nw

    def body(*refs):
        gather = _GatherPlan(refs[:nall], refs[nall:2 * nall], refs[2 * nall:], na)
        gather.send()
        gather.forward()
        gather.finish()

    arrs = list(shards) + list(whole)
    outs = _call(
        body, name="gather_shards",
        in_specs=[HBM_SPEC] * nall, out_specs=[HBM_SPEC] * nall,
        out_shape=_gather_out_shapes(arrs), scratch_shapes=_gather_semaphores(na, nall),
    )(*arrs)
    return _place_own(outs, arrs)


def _gather_out_shapes(arrs):
    return [jax.ShapeDtypeStruct((N_CHIPS,) + s.shape, s.dtype) for s in arrs]


def _gather_semaphores(na, nall):
    return [pltpu.SemaphoreType.DMA((3 * nall,)), pltpu.SemaphoreType.DMA((3 * nall,)),
            pltpu.SemaphoreType.DMA((3 * na,)), pltpu.SemaphoreType.DMA((3 * na,))]


def _place_own(outs, arrs):
    if not arrs:
        return []
    chip = 2 * lax.axis_index("x") + lax.axis_index("y")
    return [lax.dynamic_update_slice(o, a[None], (chip,) + (0,) * a.ndim) for o, a in zip(outs, arrs)]


class _GatherPlan:
    def __init__(self, srcs, dsts, sems, na):
        ici_send, ici_recv, d2d_send, d2d_recv = sems
        x, y, c = _position()
        chip = 2 * x + y
        nall = len(srcs)

        def half(a, which):
            rows = srcs[a].shape[0] // 2
            return pl.ds(pl.multiple_of(which * rows, 16), rows)

        def copy(src, dst, send, recv, k, to):
            return pltpu.make_async_remote_copy(src_ref=src, dst_ref=dst, send_sem=send.at[k], recv_sem=recv.at[k],
                                                device_id=to, device_id_type=MESH)

        self.first, self.landed, self.passed, self.returned = [], [], [], []
        for j, (px, py) in enumerate(_other_chips(x, y)):
            theirs = 2 * px + py
            for a in range(nall):
                k = j * nall + a
                if a < na:
                    self.first.append(copy(srcs[a].at[half(a, c), :], dsts[a].at[chip, half(a, c), :],
                                           ici_send, ici_recv, k, (px, py, c)))
                    mine = dsts[a].at[theirs, half(a, c), :]
                    other = dsts[a].at[theirs, half(a, 1 - c), :]
                    self.landed.append(copy(mine, mine, ici_send, ici_recv, k, (px, py, c)))
                    self.passed.append(copy(mine, mine, d2d_send, d2d_recv, j * na + a, (x, y, 1 - c)))
                    self.returned.append(copy(other, other, d2d_send, d2d_recv, j * na + a, (x, y, 1 - c)))
                else:
                    self.first.append(copy(srcs[a], dsts[a].at[chip], ici_send, ici_recv, k, (px, py, c)))
                    got = dsts[a].at[theirs]
                    self.landed.append(copy(got, got, ici_send, ici_recv, k, (px, py, c)))
                    self.passed.append(None)

    def send(self):
        for cp in self.first:
            cp.start()

    def forward(self):
        for arrival, fwd in zip(self.landed, self.passed):
            arrival.wait_recv()
            if fwd is not None:
                fwd.start()

    def finish(self):
        for cp in self.returned:
            cp.wait_recv()
        for cp in self.first + [f for f in self.passed if f is not None]:
            cp.wait_send()


W_ROWS = 1568
G_ROWS = 1552
SHARD_ROWS = D_IN // N_CHIPS
WINDOW_STEP = 1536


def _assemble_w_in(cont):
    cb = 256
    half = WINDOW_STEP

    def body(c_ref, wa_ref, wf_ref, wb_ref):
        x0 = c_ref[0].astype(F32)
        x1, x2, x3 = (pltpu.roll(c_ref[j].astype(F32), 2 * j, 0) for j in (1, 2, 3))
        wa = jnp.concatenate([x0[:half], x0[half:half + 16] + x1[:16], x1[16:half]], axis=0)
        wa_ref[...] = wa.astype(BF16)

        fl = x1[half:half + 16] + x2[:16]
        row = lax.broadcasted_iota(jnp.int32, fl.shape, 0)
        wf_ref[:16, :] = jnp.where(row < H, fl, 0.0).astype(BF16)
        wf_ref[16:, :] = jnp.zeros((LANES - 16, cb), BF16)

        mid = x2[half:half + SUBLANES] + x3[:SUBLANES]
        wb = jnp.concatenate([x2[SUBLANES:half], mid, x3[SUBLANES:half + SUBLANES]], axis=0)
        wb_ref[...] = wb.astype(BF16)

    return _call(
        body, name="assemble_w_in", grid=(D // cb,),
        in_specs=[pl.BlockSpec((N_CHIPS, W_ROWS, cb), lambda i: (0, 0, i))],
        out_specs=[pl.BlockSpec((3 * D, cb), lambda i: (0, i)), pl.BlockSpec((LANES, cb), lambda i: (0, i)),
                   pl.BlockSpec((3 * D, cb), lambda i: (0, i))],
        out_shape=[jax.ShapeDtypeStruct((3 * D, D), BF16), jax.ShapeDtypeStruct((LANES, D), BF16),
                   jax.ShapeDtypeStruct((3 * D, D), BF16)],
        compiler_params=_cparams(("parallel",)),
    )(cont)


def _pair_exchange_windows(grad_t):
    half_g = G_ROWS // 2

    def body(g_ref, got, send_sems, recv_sems):
        x, y, c = _position()
        copies = []
        for j in range(N_CHIPS):
            rows = pl.ds(pl.multiple_of(j * WINDOW_STEP + (1 - c) * half_g, SUBLANES), half_g)
            copies.append(pltpu.make_async_remote_copy(
                src_ref=g_ref.at[rows, :], dst_ref=got.at[j], send_sem=send_sems.at[j], recv_sem=recv_sems.at[j],
                device_id=(x, y, 1 - c), device_id_type=MESH))
        for cp in copies:
            cp.start()
        for cp in copies:
            cp.wait()

    return _call(
        body, name="pair_exchange_w_in",
        in_specs=[HBM_SPEC], out_specs=HBM_SPEC,
        out_shape=jax.ShapeDtypeStruct((N_CHIPS, half_g, D), F32),
        scratch_shapes=[pltpu.SemaphoreType.DMA((N_CHIPS,)), pltpu.SemaphoreType.DMA((N_CHIPS,))],
    )(grad_t)


def _pair_exchange(parts):
    na = len(parts)

    def body(*refs):
        srcs, got = refs[:na], refs[na:2 * na]
        send_sems, recv_sems = refs[2 * na:]
        x, y, c = _position()
        copies = []
        for a in range(na):
            half = srcs[a].shape[1] // 2
            rows = pl.ds(pl.multiple_of((1 - c) * half, SUBLANES), half)
            copies.append(pltpu.make_async_remote_copy(
                src_ref=srcs[a].at[:, rows, :], dst_ref=got[a], send_sem=send_sems.at[a], recv_sem=recv_sems.at[a],
                device_id=(x, y, 1 - c), device_id_type=MESH))
        for cp in copies:
            cp.start()
        for cp in copies:
            cp.wait()

    return _call(
        body, name="pair_exchange",
        in_specs=[HBM_SPEC] * na, out_specs=[HBM_SPEC] * na,
        out_shape=[jax.ShapeDtypeStruct((s.shape[0], s.shape[1] // 2, s.shape[2]), s.dtype) for s in parts],
        scratch_shapes=[pltpu.SemaphoreType.DMA((na,)), pltpu.SemaphoreType.DMA((na,))],
    )(*parts)


def _pair_sum(parts, gots, c):
    na = len(parts)

    def body(c_ref, *refs):
        for a in range(na):
            refs[2 * na + a][...] = (refs[a][...] + refs[na + a][...]).astype(BF16)

    mine = [pl.BlockSpec(g.shape, lambda i, c_ref: (0, c_ref[0], 0)) for g in gots]
    whole = [pl.BlockSpec(g.shape, lambda i, c_ref: (0, 0, 0)) for g in gots]
    grid_spec = pltpu.PrefetchScalarGridSpec(
        num_scalar_prefetch=1, grid=(1,), in_specs=mine + whole, out_specs=whole)
    return _call(
        body, name="pair_sum", grid_spec=grid_spec,
        out_shape=[jax.ShapeDtypeStruct(g.shape, BF16) for g in gots],
        compiler_params=_cparams(("arbitrary",), VMEM_BIG),
    )(c.reshape(1), *parts, *gots)


def _pair_sum_windows(grad_t, got, c):
    _, half, C = got.shape
    cb = 256

    def body(c_ref, a_ref, b_ref, o_ref):
        o_ref[0] = (a_ref[...] + b_ref[0]).astype(BF16)

    def mine(j, i, c_ref):
        return ((j * (WINDOW_STEP // SUBLANES) + c_ref[0] * (half // SUBLANES)) * SUBLANES, i * cb)

    spec = pl.BlockSpec((1, half, cb), lambda j, i, c_ref: (j, 0, i))
    grid_spec = pltpu.PrefetchScalarGridSpec(
        num_scalar_prefetch=1, grid=(N_CHIPS, C // cb),
        in_specs=[pl.BlockSpec((pl.Element(half), pl.Element(cb)), mine), spec], out_specs=spec)
    return _call(
        body, name="pair_sum_w_in", grid_spec=grid_spec,
        out_shape=jax.ShapeDtypeStruct((N_CHIPS, half, C), BF16),
        compiler_params=_cparams(("parallel", "parallel")),
    )(c.reshape(1), grad_t, got)


def _chip_sum(own, got, chip, name):
    _, half, C = got.shape
    cb = min(C, 256)

    def body(chip_ref, own_ref, g_ref, o_ref):
        for me in range(N_CHIPS):
            @pl.when(chip_ref[0] == me)
            def _(me=me):
                terms = [own_ref[0] if k == me else g_ref[k] for k in range(N_CHIPS)]
                acc = terms[0].astype(F32) + terms[1].astype(F32)
                acc = acc + terms[2].astype(F32)
                o_ref[...] = acc + terms[3].astype(F32)

    grid_spec = pltpu.PrefetchScalarGridSpec(
        num_scalar_prefetch=1, grid=(C // cb,),
        in_specs=[pl.BlockSpec((1, half, cb), lambda i, chip_ref: (chip_ref[0], 0, i)),
                  pl.BlockSpec((N_CHIPS, half, cb), lambda i, chip_ref: (0, 0, i))],
        out_specs=pl.BlockSpec((half, cb), lambda i, chip_ref: (0, i)))
    return _call(
        body, name=name, grid_spec=grid_spec,
        out_shape=jax.ShapeDtypeStruct((half, C), F32),
        compiler_params=_cparams(("parallel",)),
    )(chip.reshape(1), own, got)


def _final_exchange(halves, g):
    na = len(halves)
    rows = g.shape[0]
    per = rows // N_DEV

    def body(*refs):
        srcs, g_ref = refs[:na], refs[na]
        dsts, out_ref = refs[na + 1:2 * na + 1], refs[2 * na + 1]
        got_ref, s1, r1, s2, r2, swap_send, swap_recv = refs[2 * na + 2:]
        x, y, c = _position()
        swaps = [pltpu.make_async_remote_copy(
            src_ref=srcs[a], dst_ref=dsts[a], send_sem=swap_send.at[a], recv_sem=swap_recv.at[a],
            device_id=(x, y, 1 - c), device_id_type=MESH) for a in range(na)]
        for cp in swaps:
            cp.start()
        me = 4 * x + 2 * y + c
        mine = pl.ds(pl.multiple_of(me * per, SUBLANES), per)
        peers = []
        for j in range(1, N_DEV):
            px = 1 - x if j & 4 else x
            py = 1 - y if j & 2 else y
            pc = 1 - c if j & 1 else c
            peers.append((px, py, pc))

        first = []
        for j, (px, py, pc) in enumerate(peers):
            theirs = pl.ds(pl.multiple_of((4 * px + 2 * py + pc) * per, SUBLANES), per)
            first.append(pltpu.make_async_remote_copy(
                src_ref=g_ref.at[theirs, :], dst_ref=got_ref.at[me], send_sem=s1.at[j], recv_sem=r1.at[j],
                device_id=(px, py, pc), device_id_type=MESH))
        for cp in first:
            cp.start()
        got_ref[me] = g_ref[mine, :]
        for cp in first:
            cp.wait()
        total = got_ref[0]
        for d in range(1, N_DEV):
            total = total + got_ref[d]
        out_ref[mine, :] = total

        second = []
        for j, peer in enumerate(peers):
            second.append(pltpu.make_async_remote_copy(
                src_ref=out_ref.at[mine, :], dst_ref=out_ref.at[mine, :], send_sem=s2.at[j], recv_sem=r2.at[j],
                device_id=peer, device_id_type=MESH))
        for cp in second:
            cp.start()
        for cp in second + swaps:
            cp.wait()

    sems = pltpu.SemaphoreType.DMA((N_DEV - 1,))
    swap_sems = pltpu.SemaphoreType.DMA((na,))
    outs = _call(
        body, name="final_exchange", in_hbm=False,
        in_specs=[HBM_SPEC] * na + [VMEM_SPEC], out_specs=[HBM_SPEC] * na + [VMEM_SPEC],
        out_shape=[jax.ShapeDtypeStruct(s.shape, s.dtype) for s in halves] + [jax.ShapeDtypeStruct(g.shape, F32)],
        scratch_shapes=[pltpu.VMEM((N_DEV, per, LANES), F32), sems, sems, sems, sems, swap_sems, swap_sems],
    )(*halves, g)
    return outs[:na], outs[na]


def _adamw_math(g, w, m, v):
    m2 = ADAM_B1 * m + (1.0 - ADAM_B1) * g
    v2 = ADAM_B2 * v + (1.0 - ADAM_B2) * (g * g)
    m_hat = m2 / (1.0 - ADAM_B1 ** ADAM_STEP)
    v_hat = v2 / (1.0 - ADAM_B2 ** ADAM_STEP)
    delta = (-ADAM_LR) * (m_hat / (jnp.sqrt(v_hat) + ADAM_EPS) + ADAM_WD * w)
    return delta, m2, v2


ADAMW_BLOCK_BYTES = 1 << 20


def _adamw_big(g, w, m, v, name):
    R, C = g.shape
    bc = min(C, max(LANES, ADAMW_BLOCK_BYTES // (4 * R) // LANES * LANES))

    def body(g_ref, w_ref, m_ref, v_ref, d_ref, m2_ref, v2_ref):
        d_ref[...], m2_ref[...], v2_ref[...] = _adamw_math(g_ref[...], w_ref[...], m_ref[...], v_ref[...])

    spec = pl.BlockSpec((R, bc), lambda j: (0, j))
    out = jax.ShapeDtypeStruct((R, C), F32)
    return _call(
        body, name=name, grid=(C // bc,),
        in_specs=[spec] * 4, out_specs=[spec] * 3, out_shape=[out] * 3,
        compiler_params=_cparams(("parallel",)),
    )(g, w, m, v)


def _adamw_small(gs, ws, ms, vs):
    n = len(gs)

    def body(*refs):
        for a in range(n):
            g_ref, w_ref, m_ref, v_ref = (refs[k * n + a] for k in range(4))
            d_ref, m2_ref, v2_ref = (refs[(4 + k) * n + a] for k in range(3))
            d_ref[...], m2_ref[...], v2_ref[...] = _adamw_math(g_ref[...], w_ref[...], m_ref[...], v_ref[...])

    outs = [jax.ShapeDtypeStruct(w.shape, F32) for w in ws]
    specs = [_const_spec(w.shape) for w in ws]
    return _call(
        body, name="adamw_small", grid=(1,),
        in_specs=specs * 4, out_specs=specs * 3, out_shape=outs * 3,
    )(*gs, *ws, *ms, *vs)


def _late_weights(st_out, st_ple, st_gate, st_conv):
    return st_out.reshape(DMIX, D), _from_chip_cols(st_ple), st_gate.reshape(D, D), _from_chip_cols(st_conv)


def _local_step(x, p, tgt, w_a, w_f, w_b, late, b_f, pre_gain, post_gain, conv_b,
                w_rgate, b_rgate, w_igate, b_igate, lam, gain_a, gain_l, ple_gain, b_gate,
                gather_late=False, early_reduce=None, w_in_reduce=None):
    b_f_pad = jnp.pad(b_f, ((0, 0), (0, LANES - H)))
    w_r = w_rgate.astype(BF16)
    w_i = w_igate.astype(BF16)

    xn, q_aug, k_aug, v_aug, g_attn, x_lru, g_lru, flb, vt_aug = _in_proj(x, pre_gain, w_a, w_f, w_b, b_f_pad)
    if gather_late:
        o, qx, stacks = _attn_fwd(q_aug, k_aug, vt_aug, late[:3], late[3:])
        late = _late_weights(*stacks)
    else:
        o, qx, _ = _attn_fwd(q_aug, k_aug, vt_aug)
    w_out_b, w_ple_b, w_gate_b, conv_w = late
    ycat, xc, h = _branches_fwd(o, g_attn, x_lru, g_lru, gain_a, gain_l, conv_w, conv_b, w_r, b_rgate, w_i, b_igate,
                                lam)
    dh1, dycat, dmix, h1b, dgp, pb, dpe, acc_t = _tail(ycat, x, p, tgt, w_out_b, post_gain, w_ple_b, ple_gain,
                                                       w_gate_b, b_gate)
    late_grads = [_matmul_tn(ycat, dmix, "dw_out"), _matmul_tn(pb, dpe, "dw_ple"),
                  _matmul_tn(h1b, dgp, "dw_ple_gate")]
    do_aug, dg_attn, dg_lru, dh, acc_b = _branches_bwd(dycat, o, g_attn, h, g_lru, gain_a, gain_l)
    dx_lru, gw_r, gw_i, acc_l = _lru_bwd(dh, h, xc, x_lru, conv_w, w_r, b_rgate, w_i, b_igate, lam)
    if early_reduce is None:
        dq, dk, dv, dc_heads, _ = _attn_bwd(q_aug, qx, k_aug, v_aug, do_aug)
    else:
        sent = early_reduce(late_grads)
        dq, dk, dv, dc_heads, received = _attn_bwd(q_aug, qx, k_aug, v_aug, do_aug, sent)
        late_grads = list(zip(sent, received))
    dfl, acc_f = _fgate_bwd(dc_heads, flb)
    dz = (dq, dk, dv, dg_attn, dx_lru, dg_lru)
    grad_t = _dw_in_t(dz, dfl, xn)
    if w_in_reduce is None:
        grad_x, acc_x, _ = _dx(dz, dfl, w_a, w_f, w_b, x, pre_gain, dh1)
    else:
        sent = w_in_reduce(grad_t)
        grad_x, acc_x, (received,) = _dx(dz, dfl, w_a, w_f, w_b, x, pre_gain, dh1, [sent])
        grad_t = (sent, received)

    grads = dict(
        w_in_t=grad_t,
        w_out=late_grads[0],
        w_ple=late_grads[1],
        w_ple_gate=late_grads[2],
        w_rgate=gw_r,
        w_igate=gw_i,
        b_f=acc_f[0:1, :H],
        pre_gain=acc_x[0:1],
        post_gain=acc_t[0:1],
        conv_w=acc_l[0:4],
        conv_b=acc_l[4:5],
        b_rgate=acc_l[5:6],
        b_igate=acc_l[6:7],
        lru_lambda=acc_l[7:8],
        attn_out_gain=acc_b[0:1],
        lru_out_gain=acc_b[1:2],
        ple_gain=acc_t[1:2],
        b_ple_gate=acc_t[2:3],
    )
    loss = jnp.sum(acc_t[3])
    return loss, grad_x, grads


SMALL_ROWS = ["b_f", "pre_gain", "post_gain", "conv_w", "conv_b", "b_rgate", "b_igate", "lru_lambda",
              "attn_out_gain", "lru_out_gain", "ple_gain", "b_ple_gate"]
WEIGHTS = ["w_in", "b_f", "pre_gain", "post_gain", "conv_w", "conv_b", "w_rgate", "b_rgate", "w_igate", "b_igate",
           "lru_lambda", "attn_out_gain", "lru_out_gain", "w_out", "w_ple", "ple_gain", "w_ple_gate", "b_ple_gate"]
SHARDED = ["w_in", "w_out", "w_ple", "w_ple_gate"]


def _by_chip_cols(g):
    r, cols = g.shape
    return g.reshape(r, N_CHIPS, cols // N_CHIPS).transpose(1, 0, 2)


def _from_chip_cols(s):
    n, r, cols = s.shape
    return s.transpose(1, 0, 2).reshape(r, n * cols)


def kernel(x, p, w_in, b_f, pre_gain, post_gain, conv_w, conv_b, w_rgate, b_rgate, w_igate, b_igate, lru_lambda, attn_out_gain, lru_out_gain, w_out, w_ple, ple_gain, w_ple_gate, b_ple_gate, loss_target, m_w_in, m_b_f, m_pre_gain, m_post_gain, m_conv_w, m_conv_b, m_w_rgate, m_b_rgate, m_w_igate, m_b_igate, m_lru_lambda, m_attn_out_gain, m_lru_out_gain, m_w_out, m_w_ple, m_ple_gain, m_w_ple_gate, m_b_ple_gate, v_w_in, v_b_f, v_pre_gain, v_post_gain, v_conv_w, v_conv_b, v_w_rgate, v_b_rgate, v_w_igate, v_b_igate, v_lru_lambda, v_attn_out_gain, v_lru_out_gain, v_w_out, v_w_ple, v_ple_gain, v_w_ple_gate, v_b_ple_gate):
    w = dict(w_in=w_in, b_f=b_f, pre_gain=pre_gain, post_gain=post_gain, conv_w=conv_w, conv_b=conv_b,
             w_rgate=w_rgate, b_rgate=b_rgate, w_igate=w_igate, b_igate=b_igate, lru_lambda=lru_lambda,
             attn_out_gain=attn_out_gain, lru_out_gain=lru_out_gain, w_out=w_out, w_ple=w_ple, ple_gain=ple_gain,
             w_ple_gate=w_ple_gate, b_ple_gate=b_ple_gate)
    m = dict(w_in=m_w_in, b_f=m_b_f, pre_gain=m_pre_gain, post_gain=m_post_gain, conv_w=m_conv_w, conv_b=m_conv_b,
             w_rgate=m_w_rgate, b_rgate=m_b_rgate, w_igate=m_w_igate, b_igate=m_b_igate, lru_lambda=m_lru_lambda,
             attn_out_gain=m_attn_out_gain, lru_out_gain=m_lru_out_gain, w_out=m_w_out, w_ple=m_w_ple,
             ple_gain=m_ple_gain, w_ple_gate=m_w_ple_gate, b_ple_gate=m_b_ple_gate)
    v = dict(w_in=v_w_in, b_f=v_b_f, pre_gain=v_pre_gain, post_gain=v_post_gain, conv_w=v_conv_w, conv_b=v_conv_b,
             w_rgate=v_w_rgate, b_rgate=v_b_rgate, w_igate=v_w_igate, b_igate=v_b_igate, lru_lambda=v_lru_lambda,
             attn_out_gain=v_attn_out_gain, lru_out_gain=v_lru_out_gain, w_out=v_w_out, w_ple=v_w_ple,
             ple_gain=v_ple_gain, w_ple_gate=v_w_ple_gate, b_ple_gate=v_b_ple_gate)
    xi, yi, ci = _position()
    chip = 2 * xi + yi

    w_in_t, m_in_t, v_in_t = (jnp.swapaxes(t[0], 0, 1) for t in (w_in, m_w_in, v_w_in))
    window = jnp.pad(w_in_t.astype(BF16), ((0, W_ROWS - SHARD_ROWS), (0, 0)))

    (st_in,) = _gather_shards([window], [])
    w_a, w_f, w_b = _assemble_w_in(st_in)
    late_shards = (w_out[0].astype(BF16), w_ple[0].astype(BF16), w_ple_gate[0].astype(BF16), conv_w[0])

    def early_reduce(local):
        parts = [local[0].reshape(N_CHIPS, DMIX // N_CHIPS, D), _by_chip_cols(local[1]),
                 local[2].reshape(N_CHIPS, D // N_CHIPS, D)]
        return _pair_sum(parts, _pair_exchange(parts), ci)

    loss, grad_x, g = _local_step(
        x[0], p[0, 0], loss_target[0], w_a, w_f, w_b, late_shards, b_f, pre_gain, post_gain,
        conv_b, w_rgate[0], b_rgate, w_igate[0], b_igate, lru_lambda, attn_out_gain, lru_out_gain, ple_gain,
        b_ple_gate, gather_late=True, early_reduce=early_reduce,
        w_in_reduce=lambda grad_t: _pair_sum_windows(grad_t, _pair_exchange_windows(grad_t), ci))

    sums = [g["w_in_t"][0]] + [g[n][0] for n in SHARDED[1:]]
    recv = [g["w_in_t"][1]] + [g[n][1] for n in SHARDED[1:]]
    halves = [_chip_sum(sums[a], recv[a], chip, "chip_sum_%d" % a) for a in range(4)]

    rows = [jnp.pad(g["b_f"], ((0, 0), (0, D - H)))] + [g[n] for n in SMALL_ROWS[1:]]
    rows.append(jnp.pad(loss.reshape(1, 1), ((0, 0), (0, D - 1))))
    packed = jnp.concatenate([g["w_rgate"].reshape(NB * LANES, LANES), g["w_igate"].reshape(NB * LANES, LANES),
                              jnp.concatenate(rows, axis=0).reshape(LANES, LANES)], axis=0)
    theirs, summed = _final_exchange(halves, packed)
    full = [jnp.concatenate([jnp.where(ci == 0, a, b), jnp.where(ci == 0, b, a)], axis=0)
            for a, b in zip(halves, theirs)]
    red = dict(zip(SHARDED, full))
    red["w_in"] = lax.dynamic_slice_in_dim(red["w_in"], 2 * chip, SHARD_ROWS, axis=0)
    red["w_rgate"] = summed[:D].reshape(1, NB, LANES, LANES)
    red["w_igate"] = summed[D:2 * D].reshape(1, NB, LANES, LANES)
    vec = summed[2 * D:].reshape(16, D)
    loss = vec[15, 0]
    r0 = 0
    for n in SMALL_ROWS:
        nr = 4 if n == "conv_w" else 1
        red[n] = vec[r0:r0 + nr]
        r0 += nr
    red["b_f"] = red["b_f"][:, :H]
    red["conv_w"] = lax.dynamic_slice_in_dim(red["conv_w"], chip * (D // N_CHIPS), D // N_CHIPS, axis=1)[None]

    delta, new_m, new_v = {}, {}, {}
    outs_in = _adamw_big(red["w_in"], w_in_t, m_in_t, v_in_t, "adamw_w_in")
    delta["w_in"], new_m["w_in"], new_v["w_in"] = (jnp.swapaxes(t, 0, 1)[None] for t in outs_in)
    red["w_in"] = jnp.swapaxes(red["w_in"], 0, 1)[None]
    for n in SHARDED[1:]:
        delta[n], new_m[n], new_v[n] = (t[None] for t in _adamw_big(red[n], w[n][0], m[n][0], v[n][0], "adamw_" + n))
        red[n] = red[n][None]
    small = [n for n in WEIGHTS if n not in SHARDED]
    outs = _adamw_small([red[n] for n in small], [w[n] for n in small], [m[n] for n in small],
                        [v[n] for n in small])
    ns = len(small)
    for a, n in enumerate(small):
        delta[n], new_m[n], new_v[n] = outs[a], outs[ns + a], outs[2 * ns + a]

    return (loss, grad_x[None], *[red[n] for n in WEIGHTS], *[delta[n] for n in WEIGHTS],
            *[new_m[n] for n in WEIGHTS], *[new_v[n] for n in WEIGHTS])
```

```python
import jax
import jax.numpy as jnp
import numpy as np
from jax import lax
from jax.experimental import pallas as pl
from jax.experimental.pallas import tpu as pltpu

F32 = jnp.float32
BF16 = jnp.bfloat16

D = 1024
H = 8
DH = 128
NB = 8
DPLE = 256
DMIX = 2 * D
D_IN = 4 * D + H + 2 * D
FL0 = 3 * D
RMS_EPS = 1e-6
LRU_C = 8.0
NEG = -1e30
LANES = 128
SUBLANES = 8
BF16_ROWS = 16
COL_BLOCK = 256
DW_TOKENS = 2048

ADAM_LR = 0.001
ADAM_B1 = 0.9
ADAM_B2 = 0.999
ADAM_EPS = 1e-08
ADAM_WD = 0.01
ADAM_STEP = 10

TM = 256
TA = 512
FWD_HEADS = 8
BWD_HEADS = 2
VMEM_BIG = 56 * 1024 * 1024
VMEM_MID = 40 * 1024 * 1024

MESH = pl.DeviceIdType.MESH
N_CHIPS = 4
N_DEV = 8


def _call(body, *, out_shape, in_hbm=True, **kwargs):
    if not in_hbm:
        return pl.pallas_call(body, out_shape=out_shape, **kwargs)

    def pin(shape):
        return pltpu.HBM(shape.shape, shape.dtype) if isinstance(shape, jax.ShapeDtypeStruct) else shape

    fn = pl.pallas_call(body, out_shape=jax.tree.map(pin, out_shape), **kwargs)

    def run(*args):
        return fn(*[a if a.dtype == jnp.int32 else pltpu.with_memory_space_constraint(a, pltpu.HBM) for a in args])

    return run


def _cparams(sem, vmem=VMEM_MID):
    return pltpu.CompilerParams(dimension_semantics=sem, vmem_limit_bytes=vmem)


def _sigmoid(x):
    return 0.5 * jnp.tanh(0.5 * x) + 0.5


def _rstd(x):
    return lax.rsqrt(jnp.mean(x * x, axis=-1, keepdims=True) + RMS_EPS)


def _rms_bwd(t, xhat, rstd):
    return rstd * (t - xhat * jnp.mean(t * xhat, axis=-1, keepdims=True))


def _dot(a, b):
    return jnp.dot(a, b, preferred_element_type=F32)


def _dot_nt(a, b):
    return lax.dot_general(a, b, (((1,), (1,)), ((), ())), preferred_element_type=F32)


def _dot_tn(a, b):
    return lax.dot_general(a, b, (((0,), (0,)), ((), ())), preferred_element_type=F32)


def _dot_exact(a, b):
    return jnp.dot(a, b, preferred_element_type=F32, precision=lax.Precision.HIGHEST)


def _shift_down(x, j, halo):
    rolled = pltpu.roll(x, j, 0)
    row = lax.broadcasted_iota(jnp.int32, halo.shape, 0)
    top = jnp.where(row < j, pltpu.roll(halo, j, 0), rolled[:SUBLANES])
    return jnp.concatenate([top, rolled[SUBLANES:]], axis=0)


def _shift_up(x, j, nxt):
    tm = x.shape[0]
    rolled = pltpu.roll(x, tm - j, 0)
    row = lax.broadcasted_iota(jnp.int32, nxt.shape, 0)
    bot = jnp.where(row >= SUBLANES - j, pltpu.roll(nxt, SUBLANES - j, 0), rolled[tm - SUBLANES:])
    return jnp.concatenate([rolled[:tm - SUBLANES], bot], axis=0)


def _scan_fwd_into(a, u, carry, h_ref):
    tm, width = a.shape
    groups = (tm // SUBLANES, SUBLANES, width)
    a, u = a.reshape(groups), u.reshape(groups)
    sub = lax.broadcasted_iota(jnp.int32, groups, 1)
    d = 1
    while d < SUBLANES:
        keep = sub >= d
        a_s = jnp.where(keep, pltpu.roll(a, d, 1), 1.0)
        u_s = jnp.where(keep, pltpu.roll(u, d, 1), 0.0)
        u = u + a * u_s
        a = a * a_s
        d *= 2
    a, u = a.reshape(tm, width), u.reshape(tm, width)
    for g in range(tm // SUBLANES):
        rows = slice(g * SUBLANES, (g + 1) * SUBLANES)
        h_ref[rows, :] = u[rows] + a[rows] * carry
        carry = h_ref[(g + 1) * SUBLANES - 1:(g + 1) * SUBLANES, :]
    return carry


def _scan_bwd_into(b, u, g_ref):
    tm, width = b.shape
    groups = (tm // SUBLANES, SUBLANES, width)
    b, u = b.reshape(groups), u.reshape(groups)
    sub = lax.broadcasted_iota(jnp.int32, groups, 1)
    d = 1
    while d < SUBLANES:
        keep = sub < SUBLANES - d
        b_s = jnp.where(keep, pltpu.roll(b, SUBLANES - d, 1), 1.0)
        u_s = jnp.where(keep, pltpu.roll(u, SUBLANES - d, 1), 0.0)
        u = u + b * u_s
        b = b * b_s
        d *= 2
    b, u = b.reshape(tm, width), u.reshape(tm, width)
    nxt = jnp.zeros((1, width), F32)
    for g in reversed(range(tm // SUBLANES)):
        rows = slice(g * SUBLANES, (g + 1) * SUBLANES)
        g_ref[rows, :] = u[rows] + b[rows] * nxt
        nxt = g_ref[g * SUBLANES:g * SUBLANES + 1, :]


def _gate_pre(xc, w_ref):
    outs = []
    for n in range(NB):
        outs.append(_dot(xc[:, n * LANES:(n + 1) * LANES].astype(BF16), w_ref[n]))
    return jnp.concatenate(outs, axis=1)


def _gate_pre_t(d, w_ref):
    outs = []
    for n in range(NB):
        outs.append(_dot_nt(d[:, n * LANES:(n + 1) * LANES].astype(BF16), w_ref[n]))
    return jnp.concatenate(outs, axis=1)


def _softplus_neg(lam):
    return jnp.maximum(-lam, 0.0) + jnp.log(1.0 + jnp.exp(-jnp.abs(lam)))


def _row_spec(tm, width):
    return pl.BlockSpec((tm, width), lambda i: (i, 0))


def _const_spec(shape):
    nd = len(shape)
    return pl.BlockSpec(shape, lambda *_: (0,) * nd)


def _weight_spec(shape):
    nd = len(shape)
    return pl.BlockSpec(shape, lambda *_: (0,) * nd, pipeline_mode=pl.Buffered(1))


AUG = 2 * DH
LOG2E = 1.4426950408889634
LN2 = 0.6931471805599453
Q_SCALE = DH ** -0.5 * LOG2E


def _split3(x):
    hi = x.astype(BF16)
    r1 = x - hi.astype(F32)
    mid = r1.astype(BF16)
    lo = (r1 - mid.astype(F32)).astype(BF16)
    return hi, mid, lo


def _extras(col, ones_from):
    t = col.shape[0]
    hi, mid, lo = _split3(jnp.broadcast_to(col, (t, LANES)))
    lane = lax.broadcasted_iota(jnp.int32, (t, LANES), 1)
    rest = jnp.zeros((t, LANES), BF16)
    if ones_from is not None:
        rest = jnp.where((lane >= ones_from) & (lane < ones_from + 3), 1.0, 0.0).astype(BF16)
    return jnp.where(lane == 0, hi, jnp.where(lane == 1, mid, jnp.where(lane == 2, lo, rest)))


def _selectors():
    sel_q = np.zeros((3 * LANES, H * LANES), np.float32)
    sel_k = np.zeros((3 * LANES, H * LANES), np.float32)
    for hd in range(H):
        for piece in range(3):
            sel_q[piece * LANES + hd, hd * LANES + piece] = 1.0
            sel_k[piece * LANES + hd, hd * LANES + 3 + piece] = -1.0
    return jnp.asarray(sel_q, BF16), jnp.asarray(sel_k, BF16)


def _in_proj(x, pre_gain, w_a, w_f, w_b, b_f_pad):
    T = x.shape[0]
    tm = TM
    sel_q, sel_k = _selectors()

    def body(x_ref, g_ref, wa_ref, wf_ref, wb_ref, bf_ref, sq_ref, sk_ref,
             xn_ref, qa_ref, ka_ref, va_ref, ga_ref, xl_ref, gl_ref, flb_ref, vt_ref, c_s, carry):
        @pl.when(pl.program_id(0) == 0)
        def _():
            carry[...] = jnp.zeros_like(carry)

        xv = x_ref[...]
        xn = (xv * _rstd(xv) * g_ref[...]).astype(BF16)
        xn_ref[...] = xn
        for s, o_ref in enumerate((ga_ref, xl_ref, gl_ref)):
            o_ref[...] = _dot_nt(xn, wb_ref[s * D:(s + 1) * D, :]).astype(o_ref.dtype)
        flb = _dot_nt(xn, wf_ref[...]) + bf_ref[...]
        flb_ref[...] = flb
        lane = lax.broadcasted_iota(jnp.int32, flb.shape, 1)
        ls = jnp.where(lane < H, jnp.minimum(flb, 0.0) - jnp.log(1.0 + jnp.exp(-jnp.abs(flb))), 0.0)
        r = lax.broadcasted_iota(jnp.int32, (tm, tm), 0)
        c = lax.broadcasted_iota(jnp.int32, (tm, tm), 1)
        cs = _dot_exact((c <= r).astype(F32), ls) + carry[...]
        c_s[...] = cs
        carry[...] = c_s[tm - 1:tm, :]

        pieces = jnp.concatenate(_split3(cs * LOG2E), axis=1)
        ones_q = jnp.where((lane >= 3) & (lane < 6), 1.0, 0.0)
        ones_k = jnp.where(lane < 3, 1.0, 0.0)
        zq = _dot_nt(xn, wa_ref[0:D, :]) * Q_SCALE
        zk = _dot_nt(xn, wa_ref[D:2 * D, :])
        zv = _dot_nt(xn, wa_ref[2 * D:3 * D, :])
        ex_q = _dot(pieces, sq_ref[...])
        ex_k = _dot(pieces, sk_ref[...])
        for hd in range(H):
            head = slice(hd * DH, (hd + 1) * DH)
            lo, hi = hd * AUG, hd * AUG + DH
            qa_ref[:, lo:hi] = zq[:, head].astype(BF16)
            qa_ref[:, hi:hi + DH] = (ex_q[:, head] + ones_q).astype(BF16)
            ka_ref[:, lo:hi] = zk[:, head].astype(BF16)
            ka_ref[:, hi:hi + DH] = (ex_k[:, head] + ones_k).astype(BF16)
            va_ref[:, lo:hi] = zv[:, head].astype(BF16)
            va_ref[:, hi:hi + DH] = ones_k.astype(BF16)
            vt_ref[lo:hi, :] = jnp.transpose(zv[:, head]).astype(BF16)
            vt_ref[hi:hi + DH, :] = jnp.where(lax.broadcasted_iota(jnp.int32, (DH, tm), 0) < 3, 1.0, 0.0).astype(BF16)

    bf = jax.ShapeDtypeStruct((T, D), BF16)
    aug = jax.ShapeDtypeStruct((T, H * AUG), BF16)
    f32 = jax.ShapeDtypeStruct((T, D), F32)
    sel_spec = _const_spec((3 * LANES, H * LANES))
    return _call(
        body, name="in_proj", grid=(T // tm,),
        in_specs=[_row_spec(tm, D), _const_spec((1, D)), _const_spec((3 * D, D)), _const_spec((LANES, D)),
                  _const_spec((3 * D, D)), _const_spec((1, LANES)), sel_spec, sel_spec],
        out_specs=[_row_spec(tm, D)] + [_row_spec(tm, H * AUG)] * 3 + [_row_spec(tm, D)] * 3 + [_row_spec(tm, LANES)]
        + [pl.BlockSpec((H * AUG, tm), lambda i: (0, i))],
        out_shape=[bf, aug, aug, aug, f32, f32, f32, jax.ShapeDtypeStruct((T, LANES), F32),
                   jax.ShapeDtypeStruct((H * AUG, T), BF16)],
        scratch_shapes=[pltpu.VMEM((tm, LANES), F32), pltpu.VMEM((1, LANES), F32)],
        compiler_params=_cparams(("arbitrary",), VMEM_BIG),
    )(x, pre_gain, w_a, w_f, w_b, b_f_pad, sel_q, sel_k)


def _causal_pairs(n, q_major):
    if q_major:
        pairs = [(qi, ki) for qi in range(n) for ki in range(qi + 1)]
    else:
        pairs = [(ki, qi) for ki in range(n) for qi in range(ki, n)]
    return (jnp.asarray([a for a, _ in pairs], jnp.int32), jnp.asarray([b for _, b in pairs], jnp.int32))


def _attn_fwd(q_aug, k_aug, vt_aug, shards=(), whole=()):
    T = q_aug.shape[0]
    t = TA
    n = T // t
    hp = FWD_HEADS
    heads = range(hp)
    qi_tab, ki_tab = _causal_pairs(n, q_major=True)
    na, nall = len(shards), len(shards) + len(whole)
    n_h, n_j = H // hp, qi_tab.shape[0]

    def body(qi_ref, ki_ref, q_ref, k_ref, vt_ref, *rest):
        srcs, rest = rest[:nall], rest[nall:]
        o_ref, qx_ref = rest[:2]
        dsts, rest = rest[2:2 + nall], rest[2 + nall:]
        m_s, acc_s = rest[:2]
        h = pl.program_id(0)
        j = pl.program_id(1)
        qi = qi_ref[j]
        ki = ki_ref[j]

        if nall:
            gather = _GatherPlan(srcs, dsts, rest[2:], na)
            step = h * n_j + j
            pl.when(step == 0)(gather.send)
            pl.when(step == n_h * n_j // 2)(gather.forward)
            pl.when(step == n_h * n_j - 1)(gather.finish)

        @pl.when(ki == 0)
        def _():
            m_s[...] = jnp.full(m_s.shape, NEG, F32)
            acc_s[...] = jnp.zeros_like(acc_s)

        def step(on_diagonal):
            cols = [slice(a * AUG, (a + 1) * AUG) for a in heads]
            if on_diagonal:
                krow = lax.broadcasted_iota(jnp.int32, (t, t), 0)
                qcol = lax.broadcasted_iota(jnp.int32, (t, t), 1)
            def logits(a):
                st = _dot_nt(k_ref[:, cols[a]], q_ref[:, cols[a]])
                return jnp.where(krow <= qcol, st, NEG) if on_diagonal else st

            st_next = logits(0)
            for a in heads:
                st = st_next
                if a + 1 < hp:
                    st_next = logits(a + 1)
                m_prev = m_s[a]
                m_new = jnp.maximum(m_prev, jnp.max(st, axis=0, keepdims=True))
                pt = jnp.exp2(st - m_new).astype(BF16)
                acc_s[a] = jnp.exp2(m_prev - m_new) * acc_s[a] + _dot(vt_ref[cols[a], :], pt)
                m_s[a] = m_new

        @pl.when(ki < qi)
        def _():
            step(False)

        @pl.when(ki == qi)
        def _():
            step(True)
            piece = lax.broadcasted_iota(jnp.int32, (DH, t), 0)
            for a in heads:
                l = acc_s[a, DH:DH + 1, :]
                ex = jnp.transpose(q_ref[:, a * AUG + DH:(a + 1) * AUG].astype(F32))
                c2 = jnp.sum(jnp.where(piece < 3, ex, 0.0), axis=0, keepdims=True)
                hi, mid, lo = _split3(jnp.broadcast_to(c2 - (m_s[a] + jnp.log(l) * LOG2E), (DH, t)))
                ones = jnp.where((piece >= 3) & (piece < 6), 1.0, 0.0).astype(BF16)
                ex_t = jnp.where(piece == 0, hi, jnp.where(piece == 1, mid, jnp.where(piece == 2, lo, ones)))
                o_ref[:, a * DH:(a + 1) * DH] = jnp.transpose(acc_s[a, :DH, :] / l)
                qx_ref[:, a * DH:(a + 1) * DH] = jnp.transpose(ex_t.astype(F32)).astype(BF16)

    q_spec = pl.BlockSpec((t, hp * AUG), lambda h, j, qi_ref, ki_ref: (qi_ref[j], h))
    k_spec = pl.BlockSpec((t, hp * AUG), lambda h, j, qi_ref, ki_ref: (ki_ref[j], h))
    vt_spec = pl.BlockSpec((hp * AUG, t), lambda h, j, qi_ref, ki_ref: (h, ki_ref[j]))
    out_spec = pl.BlockSpec((t, hp * DH), lambda h, j, qi_ref, ki_ref: (qi_ref[j], h))
    arrs = list(shards) + list(whole)
    grid_spec = pltpu.PrefetchScalarGridSpec(
        num_scalar_prefetch=2, grid=(n_h, n_j),
        in_specs=[q_spec, k_spec, vt_spec] + [HBM_SPEC] * nall, out_specs=[out_spec, out_spec] + [HBM_SPEC] * nall,
        scratch_shapes=[pltpu.VMEM((hp, 1, t), F32), pltpu.VMEM((hp, AUG, t), F32)]
        + (_gather_semaphores(na, nall) if nall else []))
    outs = _call(
        body, name="attn_fwd", grid_spec=grid_spec,
        out_shape=[jax.ShapeDtypeStruct((T, D), F32), jax.ShapeDtypeStruct((T, D), BF16)] + _gather_out_shapes(arrs),
        compiler_params=_cparams(("arbitrary", "arbitrary"), VMEM_BIG),
    )(qi_tab, ki_tab, q_aug, k_aug, vt_aug, *arrs)
    return outs[0], outs[1], _place_own(outs[2:], arrs)


def _lru_gates(xc, wr_ref, br_ref, wi_ref, bi_ref, lam_ref):
    r = _sigmoid(_gate_pre(xc, wr_ref) + br_ref[...])
    ig = _sigmoid(_gate_pre(xc, wi_ref) + bi_ref[...])
    sp = _softplus_neg(lam_ref[...])
    la = (-LRU_C) * r * sp
    a = jnp.exp(la)
    y = -jnp.tanh(la) * (a * a + 1.0)
    return r, ig, sp, a, jnp.sqrt(y), lax.rsqrt(y)


def _branches_fwd(o, g_attn, x_lru, g_lru, gain_a, gain_l, conv_w, conv_b, w_r, b_r, w_i, b_i, lam):
    T = o.shape[0]
    tm = TM

    def body(o_ref, ga_ref, xl_ref, gl_ref, gna_ref, gnl_ref, cw_ref, cb_ref, wr_ref, br_ref, wi_ref, bi_ref,
             lam_ref, ycat_ref, xc_ref, h_ref, halo_s, hc_s):
        @pl.when(pl.program_id(0) == 0)
        def _():
            halo_s[...] = jnp.zeros_like(halo_s)
            hc_s[...] = jnp.zeros_like(hc_s)

        ov = o_ref[...]
        ga = ga_ref[...]
        ya = ov * _rstd(ov) * gna_ref[...] * (ga * _sigmoid(ga))
        ycat_ref[:, :D] = ya.astype(BF16)

        xl = xl_ref[...]
        halo = halo_s[...]
        xc = xl * cw_ref[3:4, :] + cb_ref[...]
        for j in range(3):
            xc = xc + _shift_down(xl, 3 - j, halo) * cw_ref[j:j + 1, :]
        halo_s[...] = xl_ref[tm - SUBLANES:tm, :]
        xc_ref[...] = xc

        _, ig, _, a, sq, _ = _lru_gates(xc, wr_ref, br_ref, wi_ref, bi_ref, lam_ref)
        u = sq * (ig * xc)
        hc_s[...] = _scan_fwd_into(a, u, hc_s[...], h_ref)
        hh = h_ref[...]

        gl = gl_ref[...]
        yl = hh * _rstd(hh) * gnl_ref[...] * (gl * _sigmoid(gl))
        ycat_ref[:, D:] = yl.astype(BF16)

    vec = _const_spec((1, D))
    wspec = _const_spec((NB, LANES, LANES))
    return _call(
        body, name="branches_fwd", grid=(T // tm,),
        in_specs=[_row_spec(tm, D)] * 4 + [vec, vec, _const_spec((4, D)), vec, wspec, vec, wspec, vec, vec],
        out_specs=[_row_spec(tm, DMIX), _row_spec(tm, D), _row_spec(tm, D)],
        out_shape=[jax.ShapeDtypeStruct((T, DMIX), BF16), jax.ShapeDtypeStruct((T, D), F32),
                   jax.ShapeDtypeStruct((T, D), F32)],
        scratch_shapes=[pltpu.VMEM((SUBLANES, D), F32), pltpu.VMEM((1, D), F32)],
        compiler_params=_cparams(("arbitrary",)),
    )(o, g_attn, x_lru, g_lru, gain_a, gain_l, conv_w, conv_b, w_r, b_r, w_i, b_i, lam)


def _tail(ycat, x, p, tgt, w_out, post_gain, w_ple, ple_gain, w_gate, b_gate):
    T = x.shape[0]
    tm = TM

    def body(ycat_ref, x_ref, p_ref, t_ref, wo_ref, pg_ref, wp_ref, eg_ref, wg_ref, bg_ref,
             dh1_ref, dycat_ref, dmix_ref, h1b_ref, dgp_ref, pb_ref, dpe_ref, acc_ref):
        @pl.when(pl.program_id(0) == 0)
        def _():
            acc_ref[...] = jnp.zeros_like(acc_ref)

        mix = _dot(ycat_ref[...], wo_ref[...])
        rstd_m = _rstd(mix)
        mhat = mix * rstd_m
        h1 = x_ref[...] + mhat * pg_ref[...]
        pb = p_ref[...].astype(BF16)
        pb_ref[...] = pb
        pe = _dot(pb, wp_ref[...])
        rstd_p = _rstd(pe)
        pehat = pe * rstd_p
        e = pehat * eg_ref[...]
        h1b = h1.astype(BF16)
        h1b_ref[...] = h1b
        gate = _sigmoid(_dot(h1b, wg_ref[...]) + bg_ref[...])
        diff = (h1 + gate * e) - t_ref[...]

        dy = diff * (1.0 / D)
        de = dy * gate
        dgp = (dy * e) * gate * (1.0 - gate)
        dgpb = dgp.astype(BF16)
        dgp_ref[...] = dgpb
        dh1 = dy + _dot_nt(dgpb, wg_ref[...])
        dh1_ref[...] = dh1
        dpe_ref[...] = _rms_bwd(de * eg_ref[...], pehat, rstd_p).astype(BF16)
        dmix = _rms_bwd(dh1 * pg_ref[...], mhat, rstd_m).astype(BF16)
        dmix_ref[...] = dmix
        dycat_ref[...] = _dot_nt(dmix, wo_ref[...])

        acc_ref[0:1, :] += jnp.sum(dh1 * mhat, axis=0, keepdims=True)
        acc_ref[1:2, :] += jnp.sum(de * pehat, axis=0, keepdims=True)
        acc_ref[2:3, :] += jnp.sum(dgp, axis=0, keepdims=True)
        acc_ref[3:4, :] += jnp.sum(diff * diff, axis=0, keepdims=True) * (0.5 / D)

    vec = _const_spec((1, D))
    bf = jax.ShapeDtypeStruct((T, D), BF16)
    return _call(
        body, name="tail", grid=(T // tm,),
        in_specs=[_row_spec(tm, DMIX), _row_spec(tm, D), _row_spec(tm, DPLE), _row_spec(tm, D),
                  _const_spec((DMIX, D)), vec, _const_spec((DPLE, D)), vec, _const_spec((D, D)), vec],
        out_specs=[_row_spec(tm, D), _row_spec(tm, DMIX), _row_spec(tm, D), _row_spec(tm, D), _row_spec(tm, D),
                   _row_spec(tm, DPLE), _row_spec(tm, D), _const_spec((SUBLANES, D))],
        out_shape=[jax.ShapeDtypeStruct((T, D), F32), jax.ShapeDtypeStruct((T, DMIX), F32), bf, bf, bf,
                   jax.ShapeDtypeStruct((T, DPLE), BF16), bf, jax.ShapeDtypeStruct((SUBLANES, D), F32)],
        compiler_params=_cparams(("arbitrary",), VMEM_BIG),
    )(ycat, x, p, tgt, w_out, post_gain, w_ple, ple_gain, w_gate, b_gate)


def _branches_bwd(dycat, o, g_attn, h, g_lru, gain_a, gain_l):
    T = o.shape[0]
    tm = TM

    def body(dy_ref, o_ref, ga_ref, h_ref, gl_ref, gna_ref, gnl_ref,
             do_ref, dga_ref, dgl_ref, dh_ref, acc_ref):
        @pl.when(pl.program_id(0) == 0)
        def _():
            acc_ref[...] = jnp.zeros_like(acc_ref)

        def branch(val, g, gain, dyv):
            rstd = _rstd(val)
            vhat = val * rstd
            sig = _sigmoid(g)
            dn = dyv * (g * sig)
            dg = dyv * (vhat * gain) * (sig * (1.0 + g * (1.0 - sig)))
            dgain = jnp.sum(dn * vhat, axis=0, keepdims=True)
            return _rms_bwd(dn * gain, vhat, rstd), dg, dgain

        ov = o_ref[...]
        do, dga, dgain_a = branch(ov, ga_ref[...], gna_ref[...], dy_ref[:, :D])
        dga_ref[...] = dga.astype(BF16)
        prod = do * ov
        for hd in range(H):
            head = slice(hd * DH, (hd + 1) * DH)
            do_ref[:, hd * AUG:hd * AUG + DH] = do[:, head].astype(BF16)
            do_ref[:, hd * AUG + DH:(hd + 1) * AUG] = _extras(-jnp.sum(prod[:, head], axis=1, keepdims=True), None)

        dh, dgl, dgain_l = branch(h_ref[...], gl_ref[...], gnl_ref[...], dy_ref[:, D:])
        dh_ref[...] = dh
        dgl_ref[...] = dgl.astype(BF16)
        acc_ref[0:1, :] += dgain_a
        acc_ref[1:2, :] += dgain_l

    vec = _const_spec((1, D))
    bf = jax.ShapeDtypeStruct((T, D), BF16)
    return _call(
        body, name="branches_bwd", grid=(T // tm,),
        in_specs=[_row_spec(tm, DMIX)] + [_row_spec(tm, D)] * 4 + [vec, vec],
        out_specs=[_row_spec(tm, H * AUG), _row_spec(tm, D), _row_spec(tm, D), _row_spec(tm, D),
                   _const_spec((SUBLANES, D))],
        out_shape=[jax.ShapeDtypeStruct((T, H * AUG), BF16), bf, bf, jax.ShapeDtypeStruct((T, D), F32),
                   jax.ShapeDtypeStruct((SUBLANES, D), F32)],
        compiler_params=_cparams(("arbitrary",)),
    )(dycat, o, g_attn, h, g_lru, gain_a, gain_l)


def _lru_bwd(dh, h, xc, x_lru, conv_w, w_r, b_r, w_i, b_i, lam):
    T = dh.shape[0]
    tm = TM
    nt = T // tm
    per = tm // SUBLANES

    def body(dh_ref, h_ref, hprev_ref, xc_ref, xl_ref, cw_ref, wr_ref, br_ref, wi_ref, bi_ref, lam_ref,
             dxl_ref, dwr_ref, dwi_ref, acc_ref, carry_s, dxc_next_s, top_s, dht_s):
        i = pl.program_id(0)

        @pl.when(i == 0)
        def _():
            acc_ref[...] = jnp.zeros_like(acc_ref)
            dwr_ref[...] = jnp.zeros_like(dwr_ref)
            dwi_ref[...] = jnp.zeros_like(dwi_ref)
            carry_s[...] = jnp.zeros_like(carry_s)
            dxc_next_s[...] = jnp.zeros_like(dxc_next_s)

        inner = jnp.where(i == nt - 1, 0.0, 1.0)
        xc = xc_ref[...]
        r, ig, sp, a, sq, inv_sq = _lru_gates(xc, wr_ref, br_ref, wi_ref, bi_ref, lam_ref)

        row = lax.broadcasted_iota(jnp.int32, (tm, D), 0)
        u = dh_ref[...] + jnp.where(row == tm - 1, carry_s[...], 0.0)
        _scan_bwd_into(pltpu.roll(a, tm - 1, 0), u, dht_s)
        dht = dht_s[...]
        top_s[...] = a[:SUBLANES, :] * dht[:SUBLANES, :]
        carry_s[...] = top_s[0:1, :]

        hprev = hprev_ref[...] * inner
        da = dht * _shift_down(h_ref[...], 1, hprev)
        dig = dht * sq * xc
        dxc = dht * sq * ig
        dsq = dht * ig * xc
        dla = da * a - dsq * (a * a) * inv_sq
        dr = dla * ((-LRU_C) * sp)
        dpr = dr * r * (1.0 - r)
        dpi = dig * ig * (1.0 - ig)
        for n in range(NB):
            blk = slice(n * LANES, (n + 1) * LANES)
            xcb = xc[:, blk].astype(BF16)
            dwr_ref[n] += _dot_tn(xcb, dpr[:, blk].astype(BF16))
            dwi_ref[n] += _dot_tn(xcb, dpi[:, blk].astype(BF16))
        dxc = dxc + _gate_pre_t(dpr, wr_ref) + _gate_pre_t(dpi, wi_ref)

        xl = xl_ref[...]
        nxt = dxc_next_s[...]
        dxl = dxc * cw_ref[3:4, :]
        acc_ref[3:4, :] += jnp.sum(dxc * xl, axis=0, keepdims=True)
        for j in range(3):
            ahead = _shift_up(dxc, 3 - j, nxt)
            dxl = dxl + ahead * cw_ref[j:j + 1, :]
            acc_ref[j:j + 1, :] += jnp.sum(ahead * xl, axis=0, keepdims=True)
        dxc_next_s[...] = dxc[:SUBLANES, :]
        dxl_ref[...] = dxl.astype(BF16)

        acc_ref[4:5, :] += jnp.sum(dxc, axis=0, keepdims=True)
        acc_ref[5:6, :] += jnp.sum(dpr, axis=0, keepdims=True)
        acc_ref[6:7, :] += jnp.sum(dpi, axis=0, keepdims=True)
        acc_ref[7:8, :] += jnp.sum(dla * ((-LRU_C) * r), axis=0, keepdims=True)

        @pl.when(i == nt - 1)
        def _():
            lam_v = lam_ref[...]
            acc_ref[7:8, :] = acc_ref[7:8, :] * (-_sigmoid(-lam_v))

    rev = pl.BlockSpec((tm, D), lambda i: (nt - 1 - i, 0))
    prev8 = pl.BlockSpec((SUBLANES, D), lambda i: (jnp.maximum((nt - 1 - i) * per - 1, 0), 0))
    vec = _const_spec((1, D))
    wspec = _const_spec((NB, LANES, LANES))
    bf = jax.ShapeDtypeStruct((T, D), BF16)
    return _call(
        body, name="lru_bwd", grid=(nt,),
        in_specs=[rev, rev, prev8, rev, rev, _const_spec((4, D)), wspec, vec, wspec, vec, vec],
        out_specs=[rev, wspec, wspec, _const_spec((SUBLANES, D))],
        out_shape=[bf, jax.ShapeDtypeStruct((NB, LANES, LANES), F32), jax.ShapeDtypeStruct((NB, LANES, LANES), F32),
                   jax.ShapeDtypeStruct((SUBLANES, D), F32)],
        scratch_shapes=[pltpu.VMEM((1, D), F32), pltpu.VMEM((SUBLANES, D), F32), pltpu.VMEM((SUBLANES, D), F32),
                        pltpu.VMEM((tm, D), F32)],
        compiler_params=_cparams(("arbitrary",)),
    )(dh, h, h, xc, x_lru, conv_w, w_r, b_r, w_i, b_i, lam)


def _chip_copies(srcs, dsts, send_sems, recv_sems):
    x, y, c = _position()
    chip = 2 * x + y
    na = len(srcs)
    return [pltpu.make_async_remote_copy(
        src_ref=srcs[a].at[2 * px + py], dst_ref=dsts[a].at[chip], send_sem=send_sems.at[j * na + a],
        recv_sem=recv_sems.at[j * na + a], device_id=(px, py, c), device_id_type=MESH)
        for j, (px, py) in enumerate(_other_chips(x, y)) for a in range(na)]


def _attn_bwd(q_aug, qx, k_aug, v_aug, do_aug, exchange=()):
    T = q_aug.shape[0]
    t = TA
    n = T // t
    hp = BWD_HEADS
    heads = range(hp)
    scale = DH ** -0.5
    ki_tab, qi_tab = _causal_pairs(n, q_major=False)
    last = ki_tab.shape[0] - 1
    ne = len(exchange)
    n_h = H // hp

    def body(ki_ref, qi_ref, q_ref, qx_ref, k_ref, v_ref, do_ref, *rest):
        sent, rest = rest[:ne], rest[ne:]
        dq_ref, dk_ref, dv_ref, dc_ref = rest[:4]
        received, rest = rest[4:4 + ne], rest[4 + ne:]
        dq_s, dk_s, dv_s = rest[:3]
        j = pl.program_id(1)
        ki = ki_ref[j]
        qi = qi_ref[j]

        if ne:
            first_step = (pl.program_id(0) == 0) & (j == 0)
            last_step = (pl.program_id(0) == n_h - 1) & (j == last)

            @pl.when(first_step)
            def _():
                for cp in _chip_copies(sent, received, *rest[3:]):
                    cp.start()

            @pl.when(last_step)
            def _():
                for cp in _chip_copies(sent, received, *rest[3:]):
                    cp.wait()

        @pl.when(j == 0)
        def _():
            dq_s[...] = jnp.zeros_like(dq_s)

        @pl.when(qi == ki)
        def _():
            dk_s[...] = jnp.zeros_like(dk_s)
            dv_s[...] = jnp.zeros_like(dv_s)

        def step(on_diagonal):
            cols = [slice(a * AUG, (a + 1) * AUG) for a in heads]
            qb = [jnp.concatenate([q_ref[:, a * AUG:a * AUG + DH], qx_ref[:, a * DH:(a + 1) * DH]], axis=1)
                  for a in heads]
            if on_diagonal:
                krow = lax.broadcasted_iota(jnp.int32, (t, t), 0)
                qcol = lax.broadcasted_iota(jnp.int32, (t, t), 1)

            def scores(a):
                st = _dot_nt(k_ref[:, cols[a]], qb[a])
                dpd = _dot_nt(v_ref[:, cols[a]], do_ref[:, cols[a]])
                return (jnp.where(krow <= qcol, st, NEG) if on_diagonal else st), dpd

            off = pl.multiple_of(qi * t, t)
            ahead = scores(0)
            for a in heads:
                st, dpd = ahead
                if a + 1 < hp:
                    ahead = scores(a + 1)
                pt = jnp.exp2(st)
                dsb = (pt * dpd).astype(BF16)
                dv_s[a] += _dot(pt.astype(BF16), do_ref[:, a * AUG:a * AUG + DH])
                dk_s[a] += _dot(dsb, qb[a])
                dq_s[a, pl.ds(off, t), :] += _dot_tn(dsb, k_ref[:, cols[a]])

        @pl.when(qi > ki)
        def _():
            step(False)

        @pl.when(qi == ki)
        def _():
            step(True)

        @pl.when(qi == n - 1)
        def _():
            rows = pl.ds(pl.multiple_of(ki * t, t), t)
            for a in heads:
                dk_ref[:, a * DH:(a + 1) * DH] = (dk_s[a, :, :DH] * LN2).astype(BF16)
                dv_ref[:, a * DH:(a + 1) * DH] = dv_s[a].astype(BF16)
                dc_ref[a, rows, :] = jnp.broadcast_to(-dk_s[a, :, DH + 3:DH + 4], (t, LANES))

        @pl.when(j == last)
        def _():
            for a in heads:
                dq_ref[:, a * DH:(a + 1) * DH] = (dq_s[a, :, :DH] * scale).astype(BF16)
                dc_ref[a] = dc_ref[a] + jnp.broadcast_to(dq_s[a, :, DH:DH + 1], (T, LANES))

    qside = pl.BlockSpec((t, hp * AUG), lambda h, j, ki_ref, qi_ref: (qi_ref[j], h))
    qxside = pl.BlockSpec((t, hp * DH), lambda h, j, ki_ref, qi_ref: (qi_ref[j], h))
    kside = pl.BlockSpec((t, hp * AUG), lambda h, j, ki_ref, qi_ref: (ki_ref[j], h))
    kout = pl.BlockSpec((t, hp * DH), lambda h, j, ki_ref, qi_ref: (ki_ref[j], h))
    bf = jax.ShapeDtypeStruct((T, D), BF16)
    sums = jax.ShapeDtypeStruct((H, T, LANES), F32)
    grid_spec = pltpu.PrefetchScalarGridSpec(
        num_scalar_prefetch=2, grid=(n_h, ki_tab.shape[0]),
        in_specs=[qside, qxside, kside, kside, qside] + [HBM_SPEC] * ne,
        out_specs=[pl.BlockSpec((T, hp * DH), lambda h, j, ki_ref, qi_ref: (0, h)), kout, kout,
                   pl.BlockSpec((hp, T, LANES), lambda h, j, ki_ref, qi_ref: (h, 0, 0))] + [HBM_SPEC] * ne,
        scratch_shapes=[pltpu.VMEM((hp, T, AUG), F32), pltpu.VMEM((hp, t, AUG), F32), pltpu.VMEM((hp, t, DH), F32)]
        + ([pltpu.SemaphoreType.DMA((3 * ne,)), pltpu.SemaphoreType.DMA((3 * ne,))] if ne else []))
    outs = _call(
        body, name="attn_bwd", grid_spec=grid_spec,
        out_shape=[bf, bf, bf, sums] + [jax.ShapeDtypeStruct(s.shape, s.dtype) for s in exchange],
        compiler_params=_cparams(("arbitrary", "arbitrary"), VMEM_BIG),
    )(ki_tab, qi_tab, q_aug, qx, k_aug, v_aug, do_aug, *exchange)
    return (*outs[:4], list(outs[4:]))


def _fgate_bwd(dc_heads, flb):
    T = flb.shape[0]
    tm = TM
    nt = T // tm

    def body(dch_ref, flb_ref, dfl_ref, acc_ref, carry, top_s):
        @pl.when(pl.program_id(0) == 0)
        def _():
            carry[...] = jnp.zeros_like(carry)
            acc_ref[...] = jnp.zeros_like(acc_ref)

        flb = flb_ref[...]
        lane = lax.broadcasted_iota(jnp.int32, flb.shape, 1)
        dc = jnp.zeros(flb.shape, F32)
        for hd in range(H):
            dc = dc + jnp.where(lane == hd, dch_ref[hd], 0.0)
        r = lax.broadcasted_iota(jnp.int32, (tm, tm), 0)
        c = lax.broadcasted_iota(jnp.int32, (tm, tm), 1)
        dls = _dot_exact((c >= r).astype(F32), dc) + carry[...]
        top_s[...] = dls[:SUBLANES, :]
        carry[...] = top_s[0:1, :]
        dfl = jnp.where(lane < H, dls * _sigmoid(-flb), 0.0)
        dfl_ref[...] = dfl.astype(BF16)
        acc_ref[0:1, :] += jnp.sum(dfl, axis=0, keepdims=True)

    rev = pl.BlockSpec((tm, LANES), lambda i: (nt - 1 - i, 0))
    return _call(
        body, name="fgate_bwd", grid=(nt,),
        in_specs=[pl.BlockSpec((H, tm, LANES), lambda i: (0, nt - 1 - i, 0)), rev],
        out_specs=[rev, _const_spec((SUBLANES, LANES))],
        out_shape=[jax.ShapeDtypeStruct((T, LANES), BF16), jax.ShapeDtypeStruct((SUBLANES, LANES), F32)],
        scratch_shapes=[pltpu.VMEM((1, LANES), F32), pltpu.VMEM((SUBLANES, LANES), F32)],
        compiler_params=_cparams(("arbitrary",)),
    )(dc_heads, flb)


def _dx(dz, dfl, w_a, w_f, w_b, x, pre_gain, dh1, exchange=()):
    T = x.shape[0]
    tm = TM
    nt = T // tm
    ne = len(exchange)

    def body(*refs):
        dz_refs = refs[:6]
        dfl_ref, wa_ref, wf_ref, wb_ref, x_ref, g_ref, dh1_ref = refs[6:13]
        sent = refs[13:13 + ne]
        gx_ref, acc_ref = refs[13 + ne:15 + ne]
        received, sems = refs[15 + ne:15 + 2 * ne], refs[15 + 2 * ne:]

        @pl.when(pl.program_id(0) == 0)
        def _():
            acc_ref[...] = jnp.zeros_like(acc_ref)
            for cp in _chip_copies(sent, received, *sems) if ne else ():
                cp.start()

        if ne:
            @pl.when(pl.program_id(0) == nt - 1)
            def _():
                for cp in _chip_copies(sent, received, *sems):
                    cp.wait()

        dxn = _dot(dfl_ref[...], wf_ref[...])
        for s in range(3):
            dxn = dxn + _dot(dz_refs[s][...], wa_ref[s * D:(s + 1) * D, :])
            dxn = dxn + _dot(dz_refs[3 + s][...], wb_ref[s * D:(s + 1) * D, :])
        xv = x_ref[...]
        rstd = _rstd(xv)
        xhat = xv * rstd
        gx_ref[...] = dh1_ref[...] + _rms_bwd(dxn * g_ref[...], xhat, rstd)
        acc_ref[0:1, :] += jnp.sum(dxn * xhat, axis=0, keepdims=True)

    outs = _call(
        body, name="dx", grid=(nt,),
        in_specs=[_row_spec(tm, D)] * 6 + [_row_spec(tm, LANES), _weight_spec((3 * D, D)), _weight_spec((LANES, D)),
                                           _weight_spec((3 * D, D)), _row_spec(tm, D), _const_spec((1, D)),
                                           _row_spec(tm, D)] + [HBM_SPEC] * ne,
        out_specs=[_row_spec(tm, D), _const_spec((SUBLANES, D))] + [HBM_SPEC] * ne,
        out_shape=[jax.ShapeDtypeStruct((T, D), F32), jax.ShapeDtypeStruct((SUBLANES, D), F32)]
        + [jax.ShapeDtypeStruct(s.shape, s.dtype) for s in exchange],
        scratch_shapes=[pltpu.SemaphoreType.DMA((3 * ne,)), pltpu.SemaphoreType.DMA((3 * ne,))] if ne else [],
        compiler_params=_cparams(("arbitrary",), VMEM_BIG),
    )(*dz, dfl, w_a, w_f, w_b, x, pre_gain, dh1, *exchange)
    return outs[0], outs[1], list(outs[2:])


GRAD_ROWS = D_IN + SUBLANES


def _dw_in_segments(dz_a, dz_b, xn, buf, pair, bt):
    T = xn.shape[0]
    nt = T // bt
    first, second = [(2 * pair + k) * D + (H if 2 * pair + k >= 3 else 0) for k in (0, 1)]
    step8 = (second - first) // SUBLANES

    def body(*refs):
        dza_ref, dzb_ref, xn_ref, o_ref = refs[0], refs[1], refs[2], refs[-1]

        @pl.when(pl.program_id(1) == 0)
        def _():
            o_ref[...] = jnp.zeros_like(o_ref)

        @pl.when(pl.program_id(0) == 0)
        def _():
            o_ref[...] += _dot_tn(dza_ref[...], xn_ref[...])

        @pl.when(pl.program_id(0) == 1)
        def _():
            o_ref[...] += _dot_tn(dzb_ref[...], xn_ref[...])

    spec_a = pl.BlockSpec((bt, D), lambda s, t: (jnp.where(s == 0, t, nt - 1), 0))
    spec_b = pl.BlockSpec((bt, D), lambda s, t: (jnp.where(s == 1, t, 0), 0))
    return _call(
        body, name="dw_in_%d" % pair, grid=(2, nt),
        in_specs=[spec_a, spec_b, pl.BlockSpec((bt, D), lambda s, t: (t, 0))]
        + ([] if buf is None else [pl.BlockSpec(memory_space=pl.ANY)]),
        out_specs=pl.BlockSpec((pl.Element(D), pl.Element(D)),
                               lambda s, t: ((first // SUBLANES + s * step8) * SUBLANES, 0)),
        out_shape=jax.ShapeDtypeStruct((GRAD_ROWS, D), F32),
        input_output_aliases={} if buf is None else {3: 0},
        compiler_params=_cparams(("arbitrary", "arbitrary"), VMEM_BIG),
    )(*((dz_a, dz_b, xn) if buf is None else (dz_a, dz_b, xn, buf)))


def _dw_in_t(dz, dfl, xn, bt=DW_TOKENS):
    T = xn.shape[0]
    bt = min(bt, T)
    nt = T // bt
    main = None
    for pair in range(3):
        main = _dw_in_segments(dz[2 * pair], dz[2 * pair + 1], xn, main, pair, bt)

    def f_body(dfl_ref, xn_ref, main_ref, o_ref, acc_s):
        p = pl.program_id(0)
        t = pl.program_id(1)

        @pl.when(t == 0)
        def _():
            acc_s[...] = jnp.zeros_like(acc_s)

        @pl.when(p == 0)
        def _():
            acc_s[...] += _dot_tn(dfl_ref[...], xn_ref[...])

        @pl.when(t == nt - 1)
        def _():
            o_ref[...] = acc_s[:SUBLANES, :]

    fl_block = FL0 // SUBLANES
    end_block = D_IN // SUBLANES
    return _call(
        f_body, name="dw_in_f", grid=(2, nt),
        in_specs=[pl.BlockSpec((bt, LANES), lambda p, t: (t, 0)), pl.BlockSpec((bt, D), lambda p, t: (t, 0)),
                  pl.BlockSpec(memory_space=pl.ANY)],
        out_specs=pl.BlockSpec((SUBLANES, D), lambda p, t: (fl_block + p * (end_block - fl_block), 0)),
        out_shape=jax.ShapeDtypeStruct((GRAD_ROWS, D), F32),
        scratch_shapes=[pltpu.VMEM((LANES, D), F32)],
        input_output_aliases={2: 0},
        compiler_params=_cparams(("arbitrary", "arbitrary")),
    )(dfl, xn, main)


def _matmul_tn(a, b, name, bm=512, bn=1024, bt=DW_TOKENS):
    T, M = a.shape
    N = b.shape[1]
    bm, bn, bt = min(bm, M), min(bn, N), min(bt, T)

    def body(a_ref, b_ref, o_ref):
        @pl.when(pl.program_id(2) == 0)
        def _():
            o_ref[...] = jnp.zeros_like(o_ref)

        o_ref[...] += _dot_tn(a_ref[...], b_ref[...])

    return _call(
        body, name=name, grid=(M // bm, N // bn, T // bt),
        in_specs=[pl.BlockSpec((bt, bm), lambda i, j, t: (t, i)), pl.BlockSpec((bt, bn), lambda i, j, t: (t, j))],
        out_specs=pl.BlockSpec((bm, bn), lambda i, j, t: (i, j)),
        out_shape=jax.ShapeDtypeStruct((M, N), F32),
        compiler_params=_cparams(("parallel", "parallel", "arbitrary")),
    )(a, b)


HBM_SPEC = pl.BlockSpec(memory_space=pltpu.HBM)
VMEM_SPEC = pl.BlockSpec(memory_space=pltpu.VMEM)


def _position():
    return lax.axis_index("x"), lax.axis_index("y"), lax.axis_index("c")


def _other_chips(x, y):
    return [(1 - x, y), (x, 1 - y), (1 - x, 1 - y)]


def _gather_shards(shards, whole):
    na, nw = len(shards), len(whole)
    nall = na + nw

    def body(*refs):
        gather = _GatherPlan(refs[:nall], refs[nall:2 * nall], refs[2 * nall:], na)
        gather.send()
        gather.forward()
        gather.finish()

    arrs = list(shards) + list(whole)
    outs = _call(
        body, name="gather_shards",
        in_specs=[HBM_SPEC] * nall, out_specs=[HBM_SPEC] * nall,
        out_shape=_gather_out_shapes(arrs), scratch_shapes=_gather_semaphores(na, nall),
    )(*arrs)
    return _place_own(outs, arrs)


def _gather_out_shapes(arrs):
    return [jax.ShapeDtypeStruct((N_CHIPS,) + s.shape, s.dtype) for s in arrs]


def _gather_semaphores(na, nall):
    return [pltpu.SemaphoreType.DMA((3 * nall,)), pltpu.SemaphoreType.DMA((3 * nall,)),
            pltpu.SemaphoreType.DMA((3 * na,)), pltpu.SemaphoreType.DMA((3 * na,))]


def _place_own(outs, arrs):
    if not arrs:
        return []
    chip = 2 * lax.axis_index("x") + lax.axis_index("y")
    return [lax.dynamic_update_slice(o, a[None], (chip,) + (0,) * a.ndim) for o, a in zip(outs, arrs)]


class _GatherPlan:
    def __init__(self, srcs, dsts, sems, na):
        ici_send, ici_recv, d2d_send, d2d_recv = sems
        x, y, c = _position()
        chip = 2 * x + y
        nall = len(srcs)

        def half(a, which):
            rows = srcs[a].shape[0] // 2
            return pl.ds(pl.multiple_of(which * rows, BF16_ROWS), rows)

        def copy(src, dst, send, recv, k, to):
            return pltpu.make_async_remote_copy(src_ref=src, dst_ref=dst, send_sem=send.at[k], recv_sem=recv.at[k],
                                                device_id=to, device_id_type=MESH)

        self.first, self.landed, self.passed, self.returned = [], [], [], []
        for j, (px, py) in enumerate(_other_chips(x, y)):
            theirs = 2 * px + py
            for a in range(nall):
                k = j * nall + a
                if a < na:
                    self.first.append(copy(srcs[a].at[half(a, c), :], dsts[a].at[chip, half(a, c), :],
                                           ici_send, ici_recv, k, (px, py, c)))
                    mine = dsts[a].at[theirs, half(a, c), :]
                    other = dsts[a].at[theirs, half(a, 1 - c), :]
                    self.landed.append(copy(mine, mine, ici_send, ici_recv, k, (px, py, c)))
                    self.passed.append(copy(mine, mine, d2d_send, d2d_recv, j * na + a, (x, y, 1 - c)))
                    self.returned.append(copy(other, other, d2d_send, d2d_recv, j * na + a, (x, y, 1 - c)))
                else:
                    self.first.append(copy(srcs[a], dsts[a].at[chip], ici_send, ici_recv, k, (px, py, c)))
                    got = dsts[a].at[theirs]
                    self.landed.append(copy(got, got, ici_send, ici_recv, k, (px, py, c)))
                    self.passed.append(None)

    def send(self):
        for cp in self.first:
            cp.start()

    def forward(self):
        for arrival, fwd in zip(self.landed, self.passed):
            arrival.wait_recv()
            if fwd is not None:
                fwd.start()

    def finish(self):
        for cp in self.returned:
            cp.wait_recv()
        for cp in self.first + [f for f in self.passed if f is not None]:
            cp.wait_send()


W_ROWS = 1568
G_ROWS = 1552
SHARD_ROWS = D_IN // N_CHIPS
WINDOW_STEP = 1536


def _assemble_w_in(cont):
    cb = COL_BLOCK
    half = WINDOW_STEP
    seam = BF16_ROWS

    def body(c_ref, wa_ref, wf_ref, wb_ref):
        x0 = c_ref[0].astype(F32)
        x1, x2, x3 = (pltpu.roll(c_ref[j].astype(F32), 2 * j, 0) for j in (1, 2, 3))
        wa = jnp.concatenate([x0[:half], x0[half:half + seam] + x1[:seam], x1[seam:half]], axis=0)
        wa_ref[...] = wa.astype(BF16)

        fl = x1[half:half + seam] + x2[:seam]
        row = lax.broadcasted_iota(jnp.int32, fl.shape, 0)
        wf_ref[:seam, :] = jnp.where(row < H, fl, 0.0).astype(BF16)
        wf_ref[seam:, :] = jnp.zeros((LANES - seam, cb), BF16)

        mid = x2[half:half + SUBLANES] + x3[:SUBLANES]
        wb = jnp.concatenate([x2[SUBLANES:half], mid, x3[SUBLANES:half + SUBLANES]], axis=0)
        wb_ref[...] = wb.astype(BF16)

    return _call(
        body, name="assemble_w_in", grid=(D // cb,),
        in_specs=[pl.BlockSpec((N_CHIPS, W_ROWS, cb), lambda i: (0, 0, i))],
        out_specs=[pl.BlockSpec((3 * D, cb), lambda i: (0, i)), pl.BlockSpec((LANES, cb), lambda i: (0, i)),
                   pl.BlockSpec((3 * D, cb), lambda i: (0, i))],
        out_shape=[jax.ShapeDtypeStruct((3 * D, D), BF16), jax.ShapeDtypeStruct((LANES, D), BF16),
                   jax.ShapeDtypeStruct((3 * D, D), BF16)],
        compiler_params=_cparams(("parallel",)),
    )(cont)


def _pair_exchange_windows(grad_t):
    half_g = G_ROWS // 2

    def body(g_ref, got, send_sems, recv_sems):
        x, y, c = _position()
        copies = []
        for j in range(N_CHIPS):
            rows = pl.ds(pl.multiple_of(j * WINDOW_STEP + (1 - c) * half_g, SUBLANES), half_g)
            copies.append(pltpu.make_async_remote_copy(
                src_ref=g_ref.at[rows, :], dst_ref=got.at[j], send_sem=send_sems.at[j], recv_sem=recv_sems.at[j],
                device_id=(x, y, 1 - c), device_id_type=MESH))
        for cp in copies:
            cp.start()
        for cp in copies:
            cp.wait()

    return _call(
        body, name="pair_exchange_w_in",
        in_specs=[HBM_SPEC], out_specs=HBM_SPEC,
        out_shape=jax.ShapeDtypeStruct((N_CHIPS, half_g, D), F32),
        scratch_shapes=[pltpu.SemaphoreType.DMA((N_CHIPS,)), pltpu.SemaphoreType.DMA((N_CHIPS,))],
    )(grad_t)


def _pair_exchange(parts):
    na = len(parts)

    def body(*refs):
        srcs, got = refs[:na], refs[na:2 * na]
        send_sems, recv_sems = refs[2 * na:]
        x, y, c = _position()
        copies = []
        for a in range(na):
            half = srcs[a].shape[1] // 2
            rows = pl.ds(pl.multiple_of((1 - c) * half, SUBLANES), half)
            copies.append(pltpu.make_async_remote_copy(
                src_ref=srcs[a].at[:, rows, :], dst_ref=got[a], send_sem=send_sems.at[a], recv_sem=recv_sems.at[a],
                device_id=(x, y, 1 - c), device_id_type=MESH))
        for cp in copies:
            cp.start()
        for cp in copies:
            cp.wait()

    return _call(
        body, name="pair_exchange",
        in_specs=[HBM_SPEC] * na, out_specs=[HBM_SPEC] * na,
        out_shape=[jax.ShapeDtypeStruct((s.shape[0], s.shape[1] // 2, s.shape[2]), s.dtype) for s in parts],
        scratch_shapes=[pltpu.SemaphoreType.DMA((na,)), pltpu.SemaphoreType.DMA((na,))],
    )(*parts)


def _pair_sum(parts, gots, c):
    na = len(parts)

    def body(c_ref, *refs):
        for a in range(na):
            refs[2 * na + a][...] = (refs[a][...] + refs[na + a][...]).astype(BF16)

    mine = [pl.BlockSpec(g.shape, lambda i, c_ref: (0, c_ref[0], 0)) for g in gots]
    whole = [pl.BlockSpec(g.shape, lambda i, c_ref: (0, 0, 0)) for g in gots]
    grid_spec = pltpu.PrefetchScalarGridSpec(
        num_scalar_prefetch=1, grid=(1,), in_specs=mine + whole, out_specs=whole)
    return _call(
        body, name="pair_sum", grid_spec=grid_spec,
        out_shape=[jax.ShapeDtypeStruct(g.shape, BF16) for g in gots],
        compiler_params=_cparams(("arbitrary",), VMEM_BIG),
    )(c.reshape(1), *parts, *gots)


def _pair_sum_windows(grad_t, got, c):
    _, half, C = got.shape
    cb = COL_BLOCK

    def body(c_ref, a_ref, b_ref, o_ref):
        o_ref[0] = (a_ref[...] + b_ref[0]).astype(BF16)

    def mine(j, i, c_ref):
        return ((j * (WINDOW_STEP // SUBLANES) + c_ref[0] * (half // SUBLANES)) * SUBLANES, i * cb)

    spec = pl.BlockSpec((1, half, cb), lambda j, i, c_ref: (j, 0, i))
    grid_spec = pltpu.PrefetchScalarGridSpec(
        num_scalar_prefetch=1, grid=(N_CHIPS, C // cb),
        in_specs=[pl.BlockSpec((pl.Element(half), pl.Element(cb)), mine), spec], out_specs=spec)
    return _call(
        body, name="pair_sum_w_in", grid_spec=grid_spec,
        out_shape=jax.ShapeDtypeStruct((N_CHIPS, half, C), BF16),
        compiler_params=_cparams(("parallel", "parallel")),
    )(c.reshape(1), grad_t, got)


def _chip_sum(own, got, chip, name):
    _, half, C = got.shape
    cb = min(C, COL_BLOCK)

    def body(chip_ref, own_ref, g_ref, o_ref):
        for me in range(N_CHIPS):
            @pl.when(chip_ref[0] == me)
            def _(me=me):
                terms = [own_ref[0] if k == me else g_ref[k] for k in range(N_CHIPS)]
                acc = terms[0].astype(F32) + terms[1].astype(F32)
                acc = acc + terms[2].astype(F32)
                o_ref[...] = acc + terms[3].astype(F32)

    grid_spec = pltpu.PrefetchScalarGridSpec(
        num_scalar_prefetch=1, grid=(C // cb,),
        in_specs=[pl.BlockSpec((1, half, cb), lambda i, chip_ref: (chip_ref[0], 0, i)),
                  pl.BlockSpec((N_CHIPS, half, cb), lambda i, chip_ref: (0, 0, i))],
        out_specs=pl.BlockSpec((half, cb), lambda i, chip_ref: (0, i)))
    return _call(
        body, name=name, grid_spec=grid_spec,
        out_shape=jax.ShapeDtypeStruct((half, C), F32),
        compiler_params=_cparams(("parallel",)),
    )(chip.reshape(1), own, got)


def _final_exchange(halves, g):
    na = len(halves)
    rows = g.shape[0]
    per = rows // N_DEV

    def body(*refs):
        srcs, g_ref = refs[:na], refs[na]
        dsts, out_ref = refs[na + 1:2 * na + 1], refs[2 * na + 1]
        got_ref, s1, r1, s2, r2, swap_send, swap_recv = refs[2 * na + 2:]
        x, y, c = _position()
        swaps = [pltpu.make_async_remote_copy(
            src_ref=srcs[a], dst_ref=dsts[a], send_sem=swap_send.at[a], recv_sem=swap_recv.at[a],
            device_id=(x, y, 1 - c), device_id_type=MESH) for a in range(na)]
        for cp in swaps:
            cp.start()
        me = 4 * x + 2 * y + c
        mine = pl.ds(pl.multiple_of(me * per, SUBLANES), per)
        peers = []
        for j in range(1, N_DEV):
            px = 1 - x if j & 4 else x
            py = 1 - y if j & 2 else y
            pc = 1 - c if j & 1 else c
            peers.append((px, py, pc))

        first = []
        for j, (px, py, pc) in enumerate(peers):
            theirs = pl.ds(pl.multiple_of((4 * px + 2 * py + pc) * per, SUBLANES), per)
            first.append(pltpu.make_async_remote_copy(
                src_ref=g_ref.at[theirs, :], dst_ref=got_ref.at[me], send_sem=s1.at[j], recv_sem=r1.at[j],
                device_id=(px, py, pc), device_id_type=MESH))
        for cp in first:
            cp.start()
        got_ref[me] = g_ref[mine, :]
        for cp in first:
            cp.wait()
        total = got_ref[0]
        for d in range(1, N_DEV):
            total = total + got_ref[d]
        out_ref[mine, :] = total

        second = []
        for j, peer in enumerate(peers):
            second.append(pltpu.make_async_remote_copy(
                src_ref=out_ref.at[mine, :], dst_ref=out_ref.at[mine, :], send_sem=s2.at[j], recv_sem=r2.at[j],
                device_id=peer, device_id_type=MESH))
        for cp in second:
            cp.start()
        for cp in second + swaps:
            cp.wait()

    sems = pltpu.SemaphoreType.DMA((N_DEV - 1,))
    swap_sems = pltpu.SemaphoreType.DMA((na,))
    outs = _call(
        body, name="final_exchange", in_hbm=False,
        in_specs=[HBM_SPEC] * na + [VMEM_SPEC], out_specs=[HBM_SPEC] * na + [VMEM_SPEC],
        out_shape=[jax.ShapeDtypeStruct(s.shape, s.dtype) for s in halves] + [jax.ShapeDtypeStruct(g.shape, F32)],
        scratch_shapes=[pltpu.VMEM((N_DEV, per, LANES), F32), sems, sems, sems, sems, swap_sems, swap_sems],
    )(*halves, g)
    return outs[:na], outs[na]


def _adamw_math(g, w, m, v):
    m2 = ADAM_B1 * m + (1.0 - ADAM_B1) * g
    v2 = ADAM_B2 * v + (1.0 - ADAM_B2) * (g * g)
    m_hat = m2 / (1.0 - ADAM_B1 ** ADAM_STEP)
    v_hat = v2 / (1.0 - ADAM_B2 ** ADAM_STEP)
    delta = (-ADAM_LR) * (m_hat / (jnp.sqrt(v_hat) + ADAM_EPS) + ADAM_WD * w)
    return delta, m2, v2


ADAMW_BLOCK_BYTES = 1 << 20


def _adamw_big(g, w, m, v, name):
    R, C = g.shape
    bc = min(C, max(LANES, ADAMW_BLOCK_BYTES // (4 * R) // LANES * LANES))

    def body(g_ref, w_ref, m_ref, v_ref, d_ref, m2_ref, v2_ref):
        d_ref[...], m2_ref[...], v2_ref[...] = _adamw_math(g_ref[...], w_ref[...], m_ref[...], v_ref[...])

    spec = pl.BlockSpec((R, bc), lambda j: (0, j))
    out = jax.ShapeDtypeStruct((R, C), F32)
    return _call(
        body, name=name, grid=(C // bc,),
        in_specs=[spec] * 4, out_specs=[spec] * 3, out_shape=[out] * 3,
        compiler_params=_cparams(("parallel",)),
    )(g, w, m, v)


def _adamw_small(gs, ws, ms, vs):
    n = len(gs)

    def body(*refs):
        for a in range(n):
            g_ref, w_ref, m_ref, v_ref = (refs[k * n + a] for k in range(4))
            d_ref, m2_ref, v2_ref = (refs[(4 + k) * n + a] for k in range(3))
            d_ref[...], m2_ref[...], v2_ref[...] = _adamw_math(g_ref[...], w_ref[...], m_ref[...], v_ref[...])

    outs = [jax.ShapeDtypeStruct(w.shape, F32) for w in ws]
    specs = [_const_spec(w.shape) for w in ws]
    return _call(
        body, name="adamw_small", grid=(1,),
        in_specs=specs * 4, out_specs=specs * 3, out_shape=outs * 3,
    )(*gs, *ws, *ms, *vs)


def _late_weights(st_out, st_ple, st_gate, st_conv):
    return st_out.reshape(DMIX, D), _from_chip_cols(st_ple), st_gate.reshape(D, D), _from_chip_cols(st_conv)


def _local_step(x, p, tgt, w_a, w_f, w_b, late, b_f, pre_gain, post_gain, conv_b,
                w_rgate, b_rgate, w_igate, b_igate, lam, gain_a, gain_l, ple_gain, b_gate,
                gather_late=False, early_reduce=None, w_in_reduce=None):
    b_f_pad = jnp.pad(b_f, ((0, 0), (0, LANES - H)))
    w_r = w_rgate.astype(BF16)
    w_i = w_igate.astype(BF16)

    xn, q_aug, k_aug, v_aug, g_attn, x_lru, g_lru, flb, vt_aug = _in_proj(x, pre_gain, w_a, w_f, w_b, b_f_pad)
    if gather_late:
        o, qx, stacks = _attn_fwd(q_aug, k_aug, vt_aug, late[:3], late[3:])
        late = _late_weights(*stacks)
    else:
        o, qx, _ = _attn_fwd(q_aug, k_aug, vt_aug)
    w_out_b, w_ple_b, w_gate_b, conv_w = late
    ycat, xc, h = _branches_fwd(o, g_attn, x_lru, g_lru, gain_a, gain_l, conv_w, conv_b, w_r, b_rgate, w_i, b_igate,
                                lam)
    dh1, dycat, dmix, h1b, dgp, pb, dpe, acc_t = _tail(ycat, x, p, tgt, w_out_b, post_gain, w_ple_b, ple_gain,
                                                       w_gate_b, b_gate)
    late_grads = [_matmul_tn(ycat, dmix, "dw_out"), _matmul_tn(pb, dpe, "dw_ple"),
                  _matmul_tn(h1b, dgp, "dw_ple_gate")]
    do_aug, dg_attn, dg_lru, dh, acc_b = _branches_bwd(dycat, o, g_attn, h, g_lru, gain_a, gain_l)
    dx_lru, gw_r, gw_i, acc_l = _lru_bwd(dh, h, xc, x_lru, conv_w, w_r, b_rgate, w_i, b_igate, lam)
    if early_reduce is None:
        dq, dk, dv, dc_heads, _ = _attn_bwd(q_aug, qx, k_aug, v_aug, do_aug)
    else:
        sent = early_reduce(late_grads)
        dq, dk, dv, dc_heads, received = _attn_bwd(q_aug, qx, k_aug, v_aug, do_aug, sent)
        late_grads = list(zip(sent, received))
    dfl, acc_f = _fgate_bwd(dc_heads, flb)
    dz = (dq, dk, dv, dg_attn, dx_lru, dg_lru)
    grad_t = _dw_in_t(dz, dfl, xn)
    if w_in_reduce is None:
        grad_x, acc_x, _ = _dx(dz, dfl, w_a, w_f, w_b, x, pre_gain, dh1)
    else:
        sent = w_in_reduce(grad_t)
        grad_x, acc_x, (received,) = _dx(dz, dfl, w_a, w_f, w_b, x, pre_gain, dh1, [sent])
        grad_t = (sent, received)

    grads = dict(
        w_in_t=grad_t,
        w_out=late_grads[0],
        w_ple=late_grads[1],
        w_ple_gate=late_grads[2],
        w_rgate=gw_r,
        w_igate=gw_i,
        b_f=acc_f[0:1, :H],
        pre_gain=acc_x[0:1],
        post_gain=acc_t[0:1],
        conv_w=acc_l[0:4],
        conv_b=acc_l[4:5],
        b_rgate=acc_l[5:6],
        b_igate=acc_l[6:7],
        lru_lambda=acc_l[7:8],
        attn_out_gain=acc_b[0:1],
        lru_out_gain=acc_b[1:2],
        ple_gain=acc_t[1:2],
        b_ple_gate=acc_t[2:3],
    )
    loss = jnp.sum(acc_t[3])
    return loss, grad_x, grads


SMALL_ROWS = ["b_f", "pre_gain", "post_gain", "conv_w", "conv_b", "b_rgate", "b_igate", "lru_lambda",
              "attn_out_gain", "lru_out_gain", "ple_gain", "b_ple_gate"]
WEIGHTS = ["w_in", "b_f", "pre_gain", "post_gain", "conv_w", "conv_b", "w_rgate", "b_rgate", "w_igate", "b_igate",
           "lru_lambda", "attn_out_gain", "lru_out_gain", "w_out", "w_ple", "ple_gain", "w_ple_gate", "b_ple_gate"]
SHARDED = ["w_in", "w_out", "w_ple", "w_ple_gate"]


def _by_chip_cols(g):
    r, cols = g.shape
    return g.reshape(r, N_CHIPS, cols // N_CHIPS).transpose(1, 0, 2)


def _from_chip_cols(s):
    n, r, cols = s.shape
    return s.transpose(1, 0, 2).reshape(r, n * cols)


def kernel(x, p, w_in, b_f, pre_gain, post_gain, conv_w, conv_b, w_rgate, b_rgate, w_igate, b_igate, lru_lambda, attn_out_gain, lru_out_gain, w_out, w_ple, ple_gain, w_ple_gate, b_ple_gate, loss_target, m_w_in, m_b_f, m_pre_gain, m_post_gain, m_conv_w, m_conv_b, m_w_rgate, m_b_rgate, m_w_igate, m_b_igate, m_lru_lambda, m_attn_out_gain, m_lru_out_gain, m_w_out, m_w_ple, m_ple_gain, m_w_ple_gate, m_b_ple_gate, v_w_in, v_b_f, v_pre_gain, v_post_gain, v_conv_w, v_conv_b, v_w_rgate, v_b_rgate, v_w_igate, v_b_igate, v_lru_lambda, v_attn_out_gain, v_lru_out_gain, v_w_out, v_w_ple, v_ple_gain, v_w_ple_gate, v_b_ple_gate):
    w = dict(w_in=w_in, b_f=b_f, pre_gain=pre_gain, post_gain=post_gain, conv_w=conv_w, conv_b=conv_b,
             w_rgate=w_rgate, b_rgate=b_rgate, w_igate=w_igate, b_igate=b_igate, lru_lambda=lru_lambda,
             attn_out_gain=attn_out_gain, lru_out_gain=lru_out_gain, w_out=w_out, w_ple=w_ple, ple_gain=ple_gain,
             w_ple_gate=w_ple_gate, b_ple_gate=b_ple_gate)
    m = dict(w_in=m_w_in, b_f=m_b_f, pre_gain=m_pre_gain, post_gain=m_post_gain, conv_w=m_conv_w, conv_b=m_conv_b,
             w_rgate=m_w_rgate, b_rgate=m_b_rgate, w_igate=m_w_igate, b_igate=m_b_igate, lru_lambda=m_lru_lambda,
             attn_out_gain=m_attn_out_gain, lru_out_gain=m_lru_out_gain, w_out=m_w_out, w_ple=m_w_ple,
             ple_gain=m_ple_gain, w_ple_gate=m_w_ple_gate, b_ple_gate=m_b_ple_gate)
    v = dict(w_in=v_w_in, b_f=v_b_f, pre_gain=v_pre_gain, post_gain=v_post_gain, conv_w=v_conv_w, conv_b=v_conv_b,
             w_rgate=v_w_rgate, b_rgate=v_b_rgate, w_igate=v_w_igate, b_igate=v_b_igate, lru_lambda=v_lru_lambda,
             attn_out_gain=v_attn_out_gain, lru_out_gain=v_lru_out_gain, w_out=v_w_out, w_ple=v_w_ple,
             ple_gain=v_ple_gain, w_ple_gate=v_w_ple_gate, b_ple_gate=v_b_ple_gate)
    xi, yi, ci = _position()
    chip = 2 * xi + yi

    w_in_t, m_in_t, v_in_t = (jnp.swapaxes(t[0], 0, 1) for t in (w_in, m_w_in, v_w_in))
    window = jnp.pad(w_in_t.astype(BF16), ((0, W_ROWS - SHARD_ROWS), (0, 0)))

    (st_in,) = _gather_shards([window], [])
    w_a, w_f, w_b = _assemble_w_in(st_in)
    late_shards = (w_out[0].astype(BF16), w_ple[0].astype(BF16), w_ple_gate[0].astype(BF16), conv_w[0])

    def early_reduce(local):
        parts = [local[0].reshape(N_CHIPS, DMIX // N_CHIPS, D), _by_chip_cols(local[1]),
                 local[2].reshape(N_CHIPS, D // N_CHIPS, D)]
        return _pair_sum(parts, _pair_exchange(parts), ci)

    loss, grad_x, g = _local_step(
        x[0], p[0, 0], loss_target[0], w_a, w_f, w_b, late_shards, b_f, pre_gain, post_gain,
        conv_b, w_rgate[0], b_rgate, w_igate[0], b_igate, lru_lambda, attn_out_gain, lru_out_gain, ple_gain,
        b_ple_gate, gather_late=True, early_reduce=early_reduce,
        w_in_reduce=lambda grad_t: _pair_sum_windows(grad_t, _pair_exchange_windows(grad_t), ci))

    sums = [g["w_in_t"][0]] + [g[n][0] for n in SHARDED[1:]]
    recv = [g["w_in_t"][1]] + [g[n][1] for n in SHARDED[1:]]
    halves = [_chip_sum(sums[a], recv[a], chip, "chip_sum_%d" % a) for a in range(4)]

    rows = [jnp.pad(g["b_f"], ((0, 0), (0, D - H)))] + [g[n] for n in SMALL_ROWS[1:]]
    rows.append(jnp.pad(loss.reshape(1, 1), ((0, 0), (0, D - 1))))
    packed = jnp.concatenate([g["w_rgate"].reshape(NB * LANES, LANES), g["w_igate"].reshape(NB * LANES, LANES),
                              jnp.concatenate(rows, axis=0).reshape(LANES, LANES)], axis=0)
    theirs, summed = _final_exchange(halves, packed)
    full = [jnp.concatenate([jnp.where(ci == 0, a, b), jnp.where(ci == 0, b, a)], axis=0)
            for a, b in zip(halves, theirs)]
    red = dict(zip(SHARDED, full))
    red["w_in"] = lax.dynamic_slice_in_dim(red["w_in"], 2 * chip, SHARD_ROWS, axis=0)
    red["w_rgate"] = summed[:D].reshape(1, NB, LANES, LANES)
    red["w_igate"] = summed[D:2 * D].reshape(1, NB, LANES, LANES)
    vec = summed[2 * D:].reshape(16, D)
    loss = vec[15, 0]
    r0 = 0
    for n in SMALL_ROWS:
        nr = 4 if n == "conv_w" else 1
        red[n] = vec[r0:r0 + nr]
        r0 += nr
    red["b_f"] = red["b_f"][:, :H]
    red["conv_w"] = lax.dynamic_slice_in_dim(red["conv_w"], chip * (D // N_CHIPS), D // N_CHIPS, axis=1)[None]

    delta, new_m, new_v = {}, {}, {}
    outs_in = _adamw_big(red["w_in"], w_in_t, m_in_t, v_in_t, "adamw_w_in")
    delta["w_in"], new_m["w_in"], new_v["w_in"] = (jnp.swapaxes(t, 0, 1)[None] for t in outs_in)
    red["w_in"] = jnp.swapaxes(red["w_in"], 0, 1)[None]
    for n in SHARDED[1:]:
        delta[n], new_m[n], new_v[n] = (t[None] for t in _adamw_big(red[n], w[n][0], m[n][0], v[n][0], "adamw_" + n))
        red[n] = red[n][None]
    small = [n for n in WEIGHTS if n not in SHARDED]
    outs = _adamw_small([red[n] for n in small], [w[n] for n in small], [m[n] for n in small],
                        [v[n] for n in small])
    ns = len(small)
    for a, n in enumerate(small):
        delta[n], new_m[n], new_v[n] = outs[a], outs[ns + a], outs[2 * ns + a]

    return (loss, grad_x[None], *[red[n] for n in WEIGHTS], *[delta[n] for n in WEIGHTS],
            *[new_m[n] for n in WEIGHTS], *[new_v[n] for n in WEIGHTS])
```

```python
import jax
import jax.numpy as jnp
import numpy as np
from jax import lax
from jax.experimental import pallas as pl
from jax.experimental.pallas import tpu as pltpu

F32 = jnp.float32
BF16 = jnp.bfloat16

D = 1024
H = 8
DH = 128
NB = 8
DPLE = 256
DMIX = 2 * D
D_IN = 4 * D + H + 2 * D
FL0 = 3 * D
RMS_EPS = 1e-6
LRU_C = 8.0
NEG = -1e30
LANES = 128
SUBLANES = 8
BF16_ROWS = 16
COL_BLOCK = 256
DW_TOKENS = 2048

ADAM_LR = 0.001
ADAM_B1 = 0.9
ADAM_B2 = 0.999
ADAM_EPS = 1e-08
ADAM_WD = 0.01
ADAM_STEP = 10

TM = 256
TA = 512
FWD_HEADS = 8
BWD_HEADS = 2
VMEM_BIG = 56 * 1024 * 1024
VMEM_MID = 40 * 1024 * 1024

MESH = pl.DeviceIdType.MESH
N_CHIPS = 4
N_DEV = 8


def _call(body, *, out_shape, in_hbm=True, **kwargs):
    if not in_hbm:
        return pl.pallas_call(body, out_shape=out_shape, **kwargs)

    def pin(shape):
        return pltpu.HBM(shape.shape, shape.dtype) if isinstance(shape, jax.ShapeDtypeStruct) else shape

    fn = pl.pallas_call(body, out_shape=jax.tree.map(pin, out_shape), **kwargs)

    def run(*args):
        return fn(*[a if a.dtype == jnp.int32 else pltpu.with_memory_space_constraint(a, pltpu.HBM) for a in args])

    return run


def _cparams(sem, vmem=VMEM_MID):
    return pltpu.CompilerParams(dimension_semantics=sem, vmem_limit_bytes=vmem)


def _sigmoid(x):
    return 0.5 * jnp.tanh(0.5 * x) + 0.5


def _rstd(x):
    return lax.rsqrt(jnp.mean(x * x, axis=-1, keepdims=True) + RMS_EPS)


def _rms_bwd(t, xhat, rstd):
    return rstd * (t - xhat * jnp.mean(t * xhat, axis=-1, keepdims=True))


def _dot(a, b):
    return jnp.dot(a, b, preferred_element_type=F32)


def _dot_nt(a, b):
    return lax.dot_general(a, b, (((1,), (1,)), ((), ())), preferred_element_type=F32)


def _dot_tn(a, b):
    return lax.dot_general(a, b, (((0,), (0,)), ((), ())), preferred_element_type=F32)


def _dot_exact(a, b):
    return jnp.dot(a, b, preferred_element_type=F32, precision=lax.Precision.HIGHEST)


def _shift_down(x, j, halo):
    rolled = pltpu.roll(x, j, 0)
    row = lax.broadcasted_iota(jnp.int32, halo.shape, 0)
    top = jnp.where(row < j, pltpu.roll(halo, j, 0), rolled[:SUBLANES])
    return jnp.concatenate([top, rolled[SUBLANES:]], axis=0)


def _shift_up(x, j, nxt):
    tm = x.shape[0]
    rolled = pltpu.roll(x, tm - j, 0)
    row = lax.broadcasted_iota(jnp.int32, nxt.shape, 0)
    bot = jnp.where(row >= SUBLANES - j, pltpu.roll(nxt, SUBLANES - j, 0), rolled[tm - SUBLANES:])
    return jnp.concatenate([rolled[:tm - SUBLANES], bot], axis=0)


def _scan_fwd_into(a, u, carry, h_ref):
    tm, width = a.shape
    groups = (tm // SUBLANES, SUBLANES, width)
    a, u = a.reshape(groups), u.reshape(groups)
    sub = lax.broadcasted_iota(jnp.int32, groups, 1)
    d = 1
    while d < SUBLANES:
        keep = sub >= d
        a_s = jnp.where(keep, pltpu.roll(a, d, 1), 1.0)
        u_s = jnp.where(keep, pltpu.roll(u, d, 1), 0.0)
        u = u + a * u_s
        a = a * a_s
        d *= 2
    a, u = a.reshape(tm, width), u.reshape(tm, width)
    for g in range(tm // SUBLANES):
        rows = slice(g * SUBLANES, (g + 1) * SUBLANES)
        h_ref[rows, :] = u[rows] + a[rows] * carry
        carry = h_ref[(g + 1) * SUBLANES - 1:(g + 1) * SUBLANES, :]
    return carry


def _scan_bwd_into(b, u, g_ref):
    tm, width = b.shape
    groups = (tm // SUBLANES, SUBLANES, width)
    b, u = b.reshape(groups), u.reshape(groups)
    sub = lax.broadcasted_iota(jnp.int32, groups, 1)
    d = 1
    while d < SUBLANES:
        keep = sub < SUBLANES - d
        b_s = jnp.where(keep, pltpu.roll(b, SUBLANES - d, 1), 1.0)
        u_s = jnp.where(keep, pltpu.roll(u, SUBLANES - d, 1), 0.0)
        u = u + b * u_s
        b = b * b_s
        d *= 2
    b, u = b.reshape(tm, width), u.reshape(tm, width)
    nxt = jnp.zeros((1, width), F32)
    for g in reversed(range(tm // SUBLANES)):
        rows = slice(g * SUBLANES, (g + 1) * SUBLANES)
        g_ref[rows, :] = u[rows] + b[rows] * nxt
        nxt = g_ref[g * SUBLANES:g * SUBLANES + 1, :]


def _gate_pre(xc, w_ref):
    outs = []
    for n in range(NB):
        outs.append(_dot(xc[:, n * LANES:(n + 1) * LANES].astype(BF16), w_ref[n]))
    return jnp.concatenate(outs, axis=1)


def _gate_pre_t(d, w_ref):
    outs = []
    for n in range(NB):
        outs.append(_dot_nt(d[:, n * LANES:(n + 1) * LANES].astype(BF16), w_ref[n]))
    return jnp.concatenate(outs, axis=1)


def _softplus_neg(lam):
    return jnp.maximum(-lam, 0.0) + jnp.log(1.0 + jnp.exp(-jnp.abs(lam)))


def _row_spec(tm, width):
    return pl.BlockSpec((tm, width), lambda i: (i, 0))


def _const_spec(shape):
    nd = len(shape)
    return pl.BlockSpec(shape, lambda *_: (0,) * nd)


def _weight_spec(shape):
    nd = len(shape)
    return pl.BlockSpec(shape, lambda *_: (0,) * nd, pipeline_mode=pl.Buffered(1))


AUG = 2 * DH
LOG2E = 1.4426950408889634
LN2 = 0.6931471805599453
Q_SCALE = DH ** -0.5 * LOG2E


def _split3(x):
    hi = x.astype(BF16)
    r1 = x - hi.astype(F32)
    mid = r1.astype(BF16)
    lo = (r1 - mid.astype(F32)).astype(BF16)
    return hi, mid, lo


def _extras(col, ones_from):
    t = col.shape[0]
    hi, mid, lo = _split3(jnp.broadcast_to(col, (t, LANES)))
    lane = lax.broadcasted_iota(jnp.int32, (t, LANES), 1)
    rest = jnp.zeros((t, LANES), BF16)
    if ones_from is not None:
        rest = jnp.where((lane >= ones_from) & (lane < ones_from + 3), 1.0, 0.0).astype(BF16)
    return jnp.where(lane == 0, hi, jnp.where(lane == 1, mid, jnp.where(lane == 2, lo, rest)))


def _selectors():
    sel_q = np.zeros((3 * LANES, H * LANES), np.float32)
    sel_k = np.zeros((3 * LANES, H * LANES), np.float32)
    for hd in range(H):
        for piece in range(3):
            sel_q[piece * LANES + hd, hd * LANES + piece] = 1.0
            sel_k[piece * LANES + hd, hd * LANES + 3 + piece] = -1.0
    return jnp.asarray(sel_q, BF16), jnp.asarray(sel_k, BF16)


def _in_proj(x, pre_gain, w_a, w_f, w_b, b_f_pad):
    T = x.shape[0]
    tm = TM
    sel_q, sel_k = _selectors()

    def body(x_ref, g_ref, wa_ref, wf_ref, wb_ref, bf_ref, sq_ref, sk_ref,
             xn_ref, qa_ref, ka_ref, va_ref, ga_ref, xl_ref, gl_ref, flb_ref, vt_ref, c_s, carry):
        @pl.when(pl.program_id(0) == 0)
        def _():
            carry[...] = jnp.zeros_like(carry)

        xv = x_ref[...]
        xn = (xv * _rstd(xv) * g_ref[...]).astype(BF16)
        xn_ref[...] = xn
        for s, o_ref in enumerate((ga_ref, xl_ref, gl_ref)):
            o_ref[...] = _dot_nt(xn, wb_ref[s * D:(s + 1) * D, :]).astype(o_ref.dtype)
        flb = _dot_nt(xn, wf_ref[...]) + bf_ref[...]
        flb_ref[...] = flb
        lane = lax.broadcasted_iota(jnp.int32, flb.shape, 1)
        ls = jnp.where(lane < H, jnp.minimum(flb, 0.0) - jnp.log(1.0 + jnp.exp(-jnp.abs(flb))), 0.0)
        r = lax.broadcasted_iota(jnp.int32, (tm, tm), 0)
        c = lax.broadcasted_iota(jnp.int32, (tm, tm), 1)
        cs = _dot_exact((c <= r).astype(F32), ls) + carry[...]
        c_s[...] = cs
        carry[...] = c_s[tm - 1:tm, :]

        pieces = jnp.concatenate(_split3(cs * LOG2E), axis=1)
        ones_q = jnp.where((lane >= 3) & (lane < 6), 1.0, 0.0)
        ones_k = jnp.where(lane < 3, 1.0, 0.0)
        zq = _dot_nt(xn, wa_ref[0:D, :]) * Q_SCALE
        zk = _dot_nt(xn, wa_ref[D:2 * D, :])
        zv = _dot_nt(xn, wa_ref[2 * D:3 * D, :])
        ex_q = _dot(pieces, sq_ref[...])
        ex_k = _dot(pieces, sk_ref[...])
        for hd in range(H):
            head = slice(hd * DH, (hd + 1) * DH)
            lo, hi = hd * AUG, hd * AUG + DH
            qa_ref[:, lo:hi] = zq[:, head].astype(BF16)
            qa_ref[:, hi:hi + DH] = (ex_q[:, head] + ones_q).astype(BF16)
            ka_ref[:, lo:hi] = zk[:, head].astype(BF16)
            ka_ref[:, hi:hi + DH] = (ex_k[:, head] + ones_k).astype(BF16)
            va_ref[:, lo:hi] = zv[:, head].astype(BF16)
            va_ref[:, hi:hi + DH] = ones_k.astype(BF16)
            vt_ref[lo:hi, :] = jnp.transpose(zv[:, head]).astype(BF16)
            vt_ref[hi:hi + DH, :] = jnp.where(lax.broadcasted_iota(jnp.int32, (DH, tm), 0) < 3, 1.0, 0.0).astype(BF16)

    bf = jax.ShapeDtypeStruct((T, D), BF16)
    aug = jax.ShapeDtypeStruct((T, H * AUG), BF16)
    f32 = jax.ShapeDtypeStruct((T, D), F32)
    sel_spec = _const_spec((3 * LANES, H * LANES))
    return _call(
        body, name="in_proj", grid=(T // tm,),
        in_specs=[_row_spec(tm, D), _const_spec((1, D)), _const_spec((3 * D, D)), _const_spec((LANES, D)),
                  _const_spec((3 * D, D)), _const_spec((1, LANES)), sel_spec, sel_spec],
        out_specs=[_row_spec(tm, D)] + [_row_spec(tm, H * AUG)] * 3 + [_row_spec(tm, D)] * 3 + [_row_spec(tm, LANES)]
        + [pl.BlockSpec((H * AUG, tm), lambda i: (0, i))],
        out_shape=[bf, aug, aug, aug, f32, f32, f32, jax.ShapeDtypeStruct((T, LANES), F32),
                   jax.ShapeDtypeStruct((H * AUG, T), BF16)],
        scratch_shapes=[pltpu.VMEM((tm, LANES), F32), pltpu.VMEM((1, LANES), F32)],
        compiler_params=_cparams(("arbitrary",), VMEM_BIG),
    )(x, pre_gain, w_a, w_f, w_b, b_f_pad, sel_q, sel_k)


def _causal_pairs(n, q_major):
    if q_major:
        pairs = [(qi, ki) for qi in range(n) for ki in range(qi + 1)]
    else:
        pairs = [(ki, qi) for ki in range(n) for qi in range(ki, n)]
    return (jnp.asarray([a for a, _ in pairs], jnp.int32), jnp.asarray([b for _, b in pairs], jnp.int32))


def _attn_fwd(q_aug, k_aug, vt_aug, shards=(), whole=()):
    T = q_aug.shape[0]
    t = TA
    n = T // t
    hp = FWD_HEADS
    heads = range(hp)
    qi_tab, ki_tab = _causal_pairs(n, q_major=True)
    na, nall = len(shards), len(shards) + len(whole)
    n_h, n_j = H // hp, qi_tab.shape[0]

    def body(qi_ref, ki_ref, q_ref, k_ref, vt_ref, *rest):
        srcs, rest = rest[:nall], rest[nall:]
        o_ref, qx_ref = rest[:2]
        dsts, rest = rest[2:2 + nall], rest[2 + nall:]
        m_s, acc_s = rest[:2]
        h = pl.program_id(0)
        j = pl.program_id(1)
        qi = qi_ref[j]
        ki = ki_ref[j]

        if nall:
            gather = _GatherPlan(srcs, dsts, rest[2:], na)
            step = h * n_j + j
            pl.when(step == 0)(gather.send)
            pl.when(step == n_h * n_j // 2)(gather.forward)
            pl.when(step == n_h * n_j - 1)(gather.finish)

        @pl.when(ki == 0)
        def _():
            m_s[...] = jnp.full(m_s.shape, NEG, F32)
            acc_s[...] = jnp.zeros_like(acc_s)

        def step(on_diagonal):
            cols = [slice(a * AUG, (a + 1) * AUG) for a in heads]
            if on_diagonal:
                krow = lax.broadcasted_iota(jnp.int32, (t, t), 0)
                qcol = lax.broadcasted_iota(jnp.int32, (t, t), 1)
            def logits(a):
                st = _dot_nt(k_ref[:, cols[a]], q_ref[:, cols[a]])
                return jnp.where(krow <= qcol, st, NEG) if on_diagonal else st

            st_next = logits(0)
            for a in heads:
                st = st_next
                if a + 1 < hp:
                    st_next = logits(a + 1)
                m_prev = m_s[a]
                m_new = jnp.maximum(m_prev, jnp.max(st, axis=0, keepdims=True))
                pt = jnp.exp2(st - m_new).astype(BF16)
                acc_s[a] = jnp.exp2(m_prev - m_new) * acc_s[a] + _dot(vt_ref[cols[a], :], pt)
                m_s[a] = m_new

        @pl.when(ki < qi)
        def _():
            step(False)

        @pl.when(ki == qi)
        def _():
            step(True)
            piece = lax.broadcasted_iota(jnp.int32, (DH, t), 0)
            for a in heads:
                l = acc_s[a, DH:DH + 1, :]
                ex = jnp.transpose(q_ref[:, a * AUG + DH:(a + 1) * AUG].astype(F32))
                c2 = jnp.sum(jnp.where(piece < 3, ex, 0.0), axis=0, keepdims=True)
                hi, mid, lo = _split3(jnp.broadcast_to(c2 - (m_s[a] + jnp.log(l) * LOG2E), (DH, t)))
                ones = jnp.where((piece >= 3) & (piece < 6), 1.0, 0.0).astype(BF16)
                ex_t = jnp.where(piece == 0, hi, jnp.where(piece == 1, mid, jnp.where(piece == 2, lo, ones)))
                o_ref[:, a * DH:(a + 1) * DH] = jnp.transpose(acc_s[a, :DH, :] / l)
                qx_ref[:, a * DH:(a + 1) * DH] = jnp.transpose(ex_t.astype(F32)).astype(BF16)

    q_spec = pl.BlockSpec((t, hp * AUG), lambda h, j, qi_ref, ki_ref: (qi_ref[j], h))
    k_spec = pl.BlockSpec((t, hp * AUG), lambda h, j, qi_ref, ki_ref: (ki_ref[j], h))
    vt_spec = pl.BlockSpec((hp * AUG, t), lambda h, j, qi_ref, ki_ref: (h, ki_ref[j]))
    out_spec = pl.BlockSpec((t, hp * DH), lambda h, j, qi_ref, ki_ref: (qi_ref[j], h))
    arrs = list(shards) + list(whole)
    grid_spec = pltpu.PrefetchScalarGridSpec(
        num_scalar_prefetch=2, grid=(n_h, n_j),
        in_specs=[q_spec, k_spec, vt_spec] + [HBM_SPEC] * nall, out_specs=[out_spec, out_spec] + [HBM_SPEC] * nall,
        scratch_shapes=[pltpu.VMEM((hp, 1, t), F32), pltpu.VMEM((hp, AUG, t), F32)]
        + (_gather_semaphores(na, nall) if nall else []))
    outs = _call(
        body, name="attn_fwd", grid_spec=grid_spec,
        out_shape=[jax.ShapeDtypeStruct((T, D), F32), jax.ShapeDtypeStruct((T, D), BF16)] + _gather_out_shapes(arrs),
        compiler_params=_cparams(("arbitrary", "arbitrary"), VMEM_BIG),
    )(qi_tab, ki_tab, q_aug, k_aug, vt_aug, *arrs)
    return outs[0], outs[1], _place_own(outs[2:], arrs)


def _lru_gates(xc, wr_ref, br_ref, wi_ref, bi_ref, lam_ref):
    r = _sigmoid(_gate_pre(xc, wr_ref) + br_ref[...])
    ig = _sigmoid(_gate_pre(xc, wi_ref) + bi_ref[...])
    sp = _softplus_neg(lam_ref[...])
    la = (-LRU_C) * r * sp
    a = jnp.exp(la)
    y = -jnp.tanh(la) * (a * a + 1.0)
    return r, ig, sp, a, jnp.sqrt(y), lax.rsqrt(y)


def _branches_fwd(o, g_attn, x_lru, g_lru, gain_a, gain_l, conv_w, conv_b, w_r, b_r, w_i, b_i, lam):
    T = o.shape[0]
    tm = TM

    def body(o_ref, ga_ref, xl_ref, gl_ref, gna_ref, gnl_ref, cw_ref, cb_ref, wr_ref, br_ref, wi_ref, bi_ref,
             lam_ref, ycat_ref, xc_ref, h_ref, halo_s, hc_s):
        @pl.when(pl.program_id(0) == 0)
        def _():
            halo_s[...] = jnp.zeros_like(halo_s)
            hc_s[...] = jnp.zeros_like(hc_s)

        ov = o_ref[...]
        ga = ga_ref[...]
        ya = ov * _rstd(ov) * gna_ref[...] * (ga * _sigmoid(ga))
        ycat_ref[:, :D] = ya.astype(BF16)

        xl = xl_ref[...]
        halo = halo_s[...]
        xc = xl * cw_ref[3:4, :] + cb_ref[...]
        for j in range(3):
            xc = xc + _shift_down(xl, 3 - j, halo) * cw_ref[j:j + 1, :]
        halo_s[...] = xl_ref[tm - SUBLANES:tm, :]
        xc_ref[...] = xc

        _, ig, _, a, sq, _ = _lru_gates(xc, wr_ref, br_ref, wi_ref, bi_ref, lam_ref)
        u = sq * (ig * xc)
        hc_s[...] = _scan_fwd_into(a, u, hc_s[...], h_ref)
        hh = h_ref[...]

        gl = gl_ref[...]
        yl = hh * _rstd(hh) * gnl_ref[...] * (gl * _sigmoid(gl))
        ycat_ref[:, D:] = yl.astype(BF16)

    vec = _const_spec((1, D))
    wspec = _const_spec((NB, LANES, LANES))
    return _call(
        body, name="branches_fwd", grid=(T // tm,),
        in_specs=[_row_spec(tm, D)] * 4 + [vec, vec, _const_spec((4, D)), vec, wspec, vec, wspec, vec, vec],
        out_specs=[_row_spec(tm, DMIX), _row_spec(tm, D), _row_spec(tm, D)],
        out_shape=[jax.ShapeDtypeStruct((T, DMIX), BF16), jax.ShapeDtypeStruct((T, D), F32),
                   jax.ShapeDtypeStruct((T, D), F32)],
        scratch_shapes=[pltpu.VMEM((SUBLANES, D), F32), pltpu.VMEM((1, D), F32)],
        compiler_params=_cparams(("arbitrary",)),
    )(o, g_attn, x_lru, g_lru, gain_a, gain_l, conv_w, conv_b, w_r, b_r, w_i, b_i, lam)


def _tail(ycat, x, p, tgt, w_out, post_gain, w_ple, ple_gain, w_gate, b_gate):
    T = x.shape[0]
    tm = TM

    def body(ycat_ref, x_ref, p_ref, t_ref, wo_ref, pg_ref, wp_ref, eg_ref, wg_ref, bg_ref,
             dh1_ref, dycat_ref, dmix_ref, h1b_ref, dgp_ref, pb_ref, dpe_ref, acc_ref):
        @pl.when(pl.program_id(0) == 0)
        def _():
            acc_ref[...] = jnp.zeros_like(acc_ref)

        mix = _dot(ycat_ref[...], wo_ref[...])
        rstd_m = _rstd(mix)
        mhat = mix * rstd_m
        h1 = x_ref[...] + mhat * pg_ref[...]
        pb = p_ref[...].astype(BF16)
        pb_ref[...] = pb
        pe = _dot(pb, wp_ref[...])
        rstd_p = _rstd(pe)
        pehat = pe * rstd_p
        e = pehat * eg_ref[...]
        h1b = h1.astype(BF16)
        h1b_ref[...] = h1b
        gate = _sigmoid(_dot(h1b, wg_ref[...]) + bg_ref[...])
        diff = (h1 + gate * e) - t_ref[...]

        dy = diff * (1.0 / D)
        de = dy * gate
        dgp = (dy * e) * gate * (1.0 - gate)
        dgpb = dgp.astype(BF16)
        dgp_ref[...] = dgpb
        dh1 = dy + _dot_nt(dgpb, wg_ref[...])
        dh1_ref[...] = dh1
        dpe_ref[...] = _rms_bwd(de * eg_ref[...], pehat, rstd_p).astype(BF16)
        dmix = _rms_bwd(dh1 * pg_ref[...], mhat, rstd_m).astype(BF16)
        dmix_ref[...] = dmix
        dycat_ref[...] = _dot_nt(dmix, wo_ref[...])

        acc_ref[0:1, :] += jnp.sum(dh1 * mhat, axis=0, keepdims=True)
        acc_ref[1:2, :] += jnp.sum(de * pehat, axis=0, keepdims=True)
        acc_ref[2:3, :] += jnp.sum(dgp, axis=0, keepdims=True)
        acc_ref[3:4, :] += jnp.sum(diff * diff, axis=0, keepdims=True) * (0.5 / D)

    vec = _const_spec((1, D))
    bf = jax.ShapeDtypeStruct((T, D), BF16)
    return _call(
        body, name="tail", grid=(T // tm,),
        in_specs=[_row_spec(tm, DMIX), _row_spec(tm, D), _row_spec(tm, DPLE), _row_spec(tm, D),
                  _const_spec((DMIX, D)), vec, _const_spec((DPLE, D)), vec, _const_spec((D, D)), vec],
        out_specs=[_row_spec(tm, D), _row_spec(tm, DMIX), _row_spec(tm, D), _row_spec(tm, D), _row_spec(tm, D),
                   _row_spec(tm, DPLE), _row_spec(tm, D), _const_spec((SUBLANES, D))],
        out_shape=[jax.ShapeDtypeStruct((T, D), F32), jax.ShapeDtypeStruct((T, DMIX), F32), bf, bf, bf,
                   jax.ShapeDtypeStruct((T, DPLE), BF16), bf, jax.ShapeDtypeStruct((SUBLANES, D), F32)],
        compiler_params=_cparams(("arbitrary",), VMEM_BIG),
    )(ycat, x, p, tgt, w_out, post_gain, w_ple, ple_gain, w_gate, b_gate)


def _pair_copies(srcs, gots, send_sems, recv_sems):
    x, y, c = _position()
    copies = []
    for a, (src, got) in enumerate(zip(srcs, gots)):
        half = src.shape[1] // 2
        rows = pl.ds(pl.multiple_of((1 - c) * half, SUBLANES), half)
        copies.append(pltpu.make_async_remote_copy(
            src_ref=src.at[:, rows, :], dst_ref=got, send_sem=send_sems.at[a], recv_sem=recv_sems.at[a],
            device_id=(x, y, 1 - c), device_id_type=MESH))
    return copies


def _branches_bwd(dycat, o, g_attn, h, g_lru, gain_a, gain_l, pair_parts=()):
    T = o.shape[0]
    tm = TM
    nt = T // tm
    npair = len(pair_parts)

    def body(dy_ref, o_ref, ga_ref, h_ref, gl_ref, gna_ref, gnl_ref, *rest):
        parts, rest = rest[:npair], rest[npair:]
        do_ref, dga_ref, dgl_ref, dh_ref, acc_ref = rest[:5]
        gots, sems = rest[5:5 + npair], rest[5 + npair:]

        @pl.when(pl.program_id(0) == 0)
        def _():
            acc_ref[...] = jnp.zeros_like(acc_ref)
            for cp in _pair_copies(parts, gots, *sems) if npair else ():
                cp.start()

        if npair:
            @pl.when(pl.program_id(0) == nt - 1)
            def _():
                for cp in _pair_copies(parts, gots, *sems):
                    cp.wait()

        def branch(val, g, gain, dyv):
            rstd = _rstd(val)
            vhat = val * rstd
            sig = _sigmoid(g)
            dn = dyv * (g * sig)
            dg = dyv * (vhat * gain) * (sig * (1.0 + g * (1.0 - sig)))
            dgain = jnp.sum(dn * vhat, axis=0, keepdims=True)
            return _rms_bwd(dn * gain, vhat, rstd), dg, dgain

        ov = o_ref[...]
        do, dga, dgain_a = branch(ov, ga_ref[...], gna_ref[...], dy_ref[:, :D])
        dga_ref[...] = dga.astype(BF16)
        prod = do * ov
        for hd in range(H):
            head = slice(hd * DH, (hd + 1) * DH)
            do_ref[:, hd * AUG:hd * AUG + DH] = do[:, head].astype(BF16)
            do_ref[:, hd * AUG + DH:(hd + 1) * AUG] = _extras(-jnp.sum(prod[:, head], axis=1, keepdims=True), None)

        dh, dgl, dgain_l = branch(h_ref[...], gl_ref[...], gnl_ref[...], dy_ref[:, D:])
        dh_ref[...] = dh
        dgl_ref[...] = dgl.astype(BF16)
        acc_ref[0:1, :] += dgain_a
        acc_ref[1:2, :] += dgain_l

    vec = _const_spec((1, D))
    bf = jax.ShapeDtypeStruct((T, D), BF16)
    halves = [jax.ShapeDtypeStruct((s.shape[0], s.shape[1] // 2, s.shape[2]), s.dtype) for s in pair_parts]
    outs = _call(
        body, name="branches_bwd", grid=(nt,),
        in_specs=[_row_spec(tm, DMIX)] + [_row_spec(tm, D)] * 4 + [vec, vec] + [HBM_SPEC] * npair,
        out_specs=[_row_spec(tm, H * AUG), _row_spec(tm, D), _row_spec(tm, D), _row_spec(tm, D),
                   _const_spec((SUBLANES, D))] + [HBM_SPEC] * npair,
        out_shape=[jax.ShapeDtypeStruct((T, H * AUG), BF16), bf, bf, jax.ShapeDtypeStruct((T, D), F32),
                   jax.ShapeDtypeStruct((SUBLANES, D), F32)] + halves,
        scratch_shapes=[pltpu.SemaphoreType.DMA((npair,)), pltpu.SemaphoreType.DMA((npair,))] if npair else [],
        compiler_params=_cparams(("arbitrary",)),
    )(dycat, o, g_attn, h, g_lru, gain_a, gain_l, *pair_parts)
    return (*outs[:5], list(outs[5:]))


def _lru_bwd(dh, h, xc, x_lru, conv_w, w_r, b_r, w_i, b_i, lam):
    T = dh.shape[0]
    tm = TM
    nt = T // tm
    per = tm // SUBLANES

    def body(dh_ref, h_ref, hprev_ref, xc_ref, xl_ref, cw_ref, wr_ref, br_ref, wi_ref, bi_ref, lam_ref,
             dxl_ref, dwr_ref, dwi_ref, acc_ref, carry_s, dxc_next_s, top_s, dht_s):
        i = pl.program_id(0)

        @pl.when(i == 0)
        def _():
            acc_ref[...] = jnp.zeros_like(acc_ref)
            dwr_ref[...] = jnp.zeros_like(dwr_ref)
            dwi_ref[...] = jnp.zeros_like(dwi_ref)
            carry_s[...] = jnp.zeros_like(carry_s)
            dxc_next_s[...] = jnp.zeros_like(dxc_next_s)

        inner = jnp.where(i == nt - 1, 0.0, 1.0)
        xc = xc_ref[...]
        r, ig, sp, a, sq, inv_sq = _lru_gates(xc, wr_ref, br_ref, wi_ref, bi_ref, lam_ref)

        row = lax.broadcasted_iota(jnp.int32, (tm, D), 0)
        u = dh_ref[...] + jnp.where(row == tm - 1, carry_s[...], 0.0)
        _scan_bwd_into(pltpu.roll(a, tm - 1, 0), u, dht_s)
        dht = dht_s[...]
        top_s[...] = a[:SUBLANES, :] * dht[:SUBLANES, :]
        carry_s[...] = top_s[0:1, :]

        hprev = hprev_ref[...] * inner
        da = dht * _shift_down(h_ref[...], 1, hprev)
        dig = dht * sq * xc
        dxc = dht * sq * ig
        dsq = dht * ig * xc
        dla = da * a - dsq * (a * a) * inv_sq
        dr = dla * ((-LRU_C) * sp)
        dpr = dr * r * (1.0 - r)
        dpi = dig * ig * (1.0 - ig)
        for n in range(NB):
            blk = slice(n * LANES, (n + 1) * LANES)
            xcb = xc[:, blk].astype(BF16)
            dwr_ref[n] += _dot_tn(xcb, dpr[:, blk].astype(BF16))
            dwi_ref[n] += _dot_tn(xcb, dpi[:, blk].astype(BF16))
        dxc = dxc + _gate_pre_t(dpr, wr_ref) + _gate_pre_t(dpi, wi_ref)

        xl = xl_ref[...]
        nxt = dxc_next_s[...]
        dxl = dxc * cw_ref[3:4, :]
        acc_ref[3:4, :] += jnp.sum(dxc * xl, axis=0, keepdims=True)
        for j in range(3):
            ahead = _shift_up(dxc, 3 - j, nxt)
            dxl = dxl + ahead * cw_ref[j:j + 1, :]
            acc_ref[j:j + 1, :] += jnp.sum(ahead * xl, axis=0, keepdims=True)
        dxc_next_s[...] = dxc[:SUBLANES, :]
        dxl_ref[...] = dxl.astype(BF16)

        acc_ref[4:5, :] += jnp.sum(dxc, axis=0, keepdims=True)
        acc_ref[5:6, :] += jnp.sum(dpr, axis=0, keepdims=True)
        acc_ref[6:7, :] += jnp.sum(dpi, axis=0, keepdims=True)
        acc_ref[7:8, :] += jnp.sum(dla * ((-LRU_C) * r), axis=0, keepdims=True)

        @pl.when(i == nt - 1)
        def _():
            lam_v = lam_ref[...]
            acc_ref[7:8, :] = acc_ref[7:8, :] * (-_sigmoid(-lam_v))

    rev = pl.BlockSpec((tm, D), lambda i: (nt - 1 - i, 0))
    prev8 = pl.BlockSpec((SUBLANES, D), lambda i: (jnp.maximum((nt - 1 - i) * per - 1, 0), 0))
    vec = _const_spec((1, D))
    wspec = _const_spec((NB, LANES, LANES))
    bf = jax.ShapeDtypeStruct((T, D), BF16)
    return _call(
        body, name="lru_bwd", grid=(nt,),
        in_specs=[rev, rev, prev8, rev, rev, _const_spec((4, D)), wspec, vec, wspec, vec, vec],
        out_specs=[rev, wspec, wspec, _const_spec((SUBLANES, D))],
        out_shape=[bf, jax.ShapeDtypeStruct((NB, LANES, LANES), F32), jax.ShapeDtypeStruct((NB, LANES, LANES), F32),
                   jax.ShapeDtypeStruct((SUBLANES, D), F32)],
        scratch_shapes=[pltpu.VMEM((1, D), F32), pltpu.VMEM((SUBLANES, D), F32), pltpu.VMEM((SUBLANES, D), F32),
                        pltpu.VMEM((tm, D), F32)],
        compiler_params=_cparams(("arbitrary",)),
    )(dh, h, h, xc, x_lru, conv_w, w_r, b_r, w_i, b_i, lam)


def _chip_copies(srcs, dsts, send_sems, recv_sems):
    x, y, c = _position()
    chip = 2 * x + y
    na = len(srcs)
    return [pltpu.make_async_remote_copy(
        src_ref=srcs[a].at[2 * px + py], dst_ref=dsts[a].at[chip], send_sem=send_sems.at[j * na + a],
        recv_sem=recv_sems.at[j * na + a], device_id=(px, py, c), device_id_type=MESH)
        for j, (px, py) in enumerate(_other_chips(x, y)) for a in range(na)]


def _attn_bwd(q_aug, qx, k_aug, v_aug, do_aug, exchange=()):
    T = q_aug.shape[0]
    t = TA
    n = T // t
    hp = BWD_HEADS
    heads = range(hp)
    scale = DH ** -0.5
    ki_tab, qi_tab = _causal_pairs(n, q_major=False)
    last = ki_tab.shape[0] - 1
    ne = len(exchange)
    n_h = H // hp

    def body(ki_ref, qi_ref, q_ref, qx_ref, k_ref, v_ref, do_ref, *rest):
        sent, rest = rest[:ne], rest[ne:]
        dq_ref, dk_ref, dv_ref, dc_ref = rest[:4]
        received, rest = rest[4:4 + ne], rest[4 + ne:]
        dq_s, dk_s, dv_s = rest[:3]
        j = pl.program_id(1)
        ki = ki_ref[j]
        qi = qi_ref[j]

        if ne:
            first_step = (pl.program_id(0) == 0) & (j == 0)
            last_step = (pl.program_id(0) == n_h - 1) & (j == last)

            @pl.when(first_step)
            def _():
                for cp in _chip_copies(sent, received, *rest[3:]):
                    cp.start()

            @pl.when(last_step)
            def _():
                for cp in _chip_copies(sent, received, *rest[3:]):
                    cp.wait()

        @pl.when(j == 0)
        def _():
            dq_s[...] = jnp.zeros_like(dq_s)

        @pl.when(qi == ki)
        def _():
            dk_s[...] = jnp.zeros_like(dk_s)
            dv_s[...] = jnp.zeros_like(dv_s)

        def step(on_diagonal):
            cols = [slice(a * AUG, (a + 1) * AUG) for a in heads]
            qb = [jnp.concatenate([q_ref[:, a * AUG:a * AUG + DH], qx_ref[:, a * DH:(a + 1) * DH]], axis=1)
                  for a in heads]
            if on_diagonal:
                krow = lax.broadcasted_iota(jnp.int32, (t, t), 0)
                qcol = lax.broadcasted_iota(jnp.int32, (t, t), 1)

            def scores(a):
                st = _dot_nt(k_ref[:, cols[a]], qb[a])
                dpd = _dot_nt(v_ref[:, cols[a]], do_ref[:, cols[a]])
                return (jnp.where(krow <= qcol, st, NEG) if on_diagonal else st), dpd

            off = pl.multiple_of(qi * t, t)
            ahead = scores(0)
            for a in heads:
                st, dpd = ahead
                if a + 1 < hp:
                    ahead = scores(a + 1)
                pt = jnp.exp2(st)
                dsb = (pt * dpd).astype(BF16)
                dv_s[a] += _dot(pt.astype(BF16), do_ref[:, a * AUG:a * AUG + DH])
                dk_s[a] += _dot(dsb, qb[a])
                dq_s[a, pl.ds(off, t), :] += _dot_tn(dsb, k_ref[:, cols[a]])

        @pl.when(qi > ki)
        def _():
            step(False)

        @pl.when(qi == ki)
        def _():
            step(True)

        @pl.when(qi == n - 1)
        def _():
            rows = pl.ds(pl.multiple_of(ki * t, t), t)
            for a in heads:
                dk_ref[:, a * DH:(a + 1) * DH] = (dk_s[a, :, :DH] * LN2).astype(BF16)
                dv_ref[:, a * DH:(a + 1) * DH] = dv_s[a].astype(BF16)
                dc_ref[a, rows, :] = jnp.broadcast_to(-dk_s[a, :, DH + 3:DH + 4], (t, LANES))

        @pl.when(j == last)
        def _():
            for a in heads:
                dq_ref[:, a * DH:(a + 1) * DH] = (dq_s[a, :, :DH] * scale).astype(BF16)
                dc_ref[a] = dc_ref[a] + jnp.broadcast_to(dq_s[a, :, DH:DH + 1], (T, LANES))

    qside = pl.BlockSpec((t, hp * AUG), lambda h, j, ki_ref, qi_ref: (qi_ref[j], h))
    qxside = pl.BlockSpec((t, hp * DH), lambda h, j, ki_ref, qi_ref: (qi_ref[j], h))
    kside = pl.BlockSpec((t, hp * AUG), lambda h, j, ki_ref, qi_ref: (ki_ref[j], h))
    kout = pl.BlockSpec((t, hp * DH), lambda h, j, ki_ref, qi_ref: (ki_ref[j], h))
    bf = jax.ShapeDtypeStruct((T, D), BF16)
    sums = jax.ShapeDtypeStruct((H, T, LANES), F32)
    grid_spec = pltpu.PrefetchScalarGridSpec(
        num_scalar_prefetch=2, grid=(n_h, ki_tab.shape[0]),
        in_specs=[qside, qxside, kside, kside, qside] + [HBM_SPEC] * ne,
        out_specs=[pl.BlockSpec((T, hp * DH), lambda h, j, ki_ref, qi_ref: (0, h)), kout, kout,
                   pl.BlockSpec((hp, T, LANES), lambda h, j, ki_ref, qi_ref: (h, 0, 0))] + [HBM_SPEC] * ne,
        scratch_shapes=[pltpu.VMEM((hp, T, AUG), F32), pltpu.VMEM((hp, t, AUG), F32), pltpu.VMEM((hp, t, DH), F32)]
        + ([pltpu.SemaphoreType.DMA((3 * ne,)), pltpu.SemaphoreType.DMA((3 * ne,))] if ne else []))
    outs = _call(
        body, name="attn_bwd", grid_spec=grid_spec,
        out_shape=[bf, bf, bf, sums] + [jax.ShapeDtypeStruct(s.shape, s.dtype) for s in exchange],
        compiler_params=_cparams(("arbitrary", "arbitrary"), VMEM_BIG),
    )(ki_tab, qi_tab, q_aug, qx, k_aug, v_aug, do_aug, *exchange)
    return (*outs[:4], list(outs[4:]))


def _fgate_bwd(dc_heads, flb):
    T = flb.shape[0]
    tm = TM
    nt = T // tm

    def body(dch_ref, flb_ref, dfl_ref, acc_ref, carry, top_s):
        @pl.when(pl.program_id(0) == 0)
        def _():
            carry[...] = jnp.zeros_like(carry)
            acc_ref[...] = jnp.zeros_like(acc_ref)

        flb = flb_ref[...]
        lane = lax.broadcasted_iota(jnp.int32, flb.shape, 1)
        dc = jnp.zeros(flb.shape, F32)
        for hd in range(H):
            dc = dc + jnp.where(lane == hd, dch_ref[hd], 0.0)
        r = lax.broadcasted_iota(jnp.int32, (tm, tm), 0)
        c = lax.broadcasted_iota(jnp.int32, (tm, tm), 1)
        dls = _dot_exact((c >= r).astype(F32), dc) + carry[...]
        top_s[...] = dls[:SUBLANES, :]
        carry[...] = top_s[0:1, :]
        dfl = jnp.where(lane < H, dls * _sigmoid(-flb), 0.0)
        dfl_ref[...] = dfl.astype(BF16)
        acc_ref[0:1, :] += jnp.sum(dfl, axis=0, keepdims=True)

    rev = pl.BlockSpec((tm, LANES), lambda i: (nt - 1 - i, 0))
    return _call(
        body, name="fgate_bwd", grid=(nt,),
        in_specs=[pl.BlockSpec((H, tm, LANES), lambda i: (0, nt - 1 - i, 0)), rev],
        out_specs=[rev, _const_spec((SUBLANES, LANES))],
        out_shape=[jax.ShapeDtypeStruct((T, LANES), BF16), jax.ShapeDtypeStruct((SUBLANES, LANES), F32)],
        scratch_shapes=[pltpu.VMEM((1, LANES), F32), pltpu.VMEM((SUBLANES, LANES), F32)],
        compiler_params=_cparams(("arbitrary",)),
    )(dc_heads, flb)


def _dx(dz, dfl, w_a, w_f, w_b, x, pre_gain, dh1, exchange=()):
    T = x.shape[0]
    tm = TM
    nt = T // tm
    ne = len(exchange)

    def body(*refs):
        dz_refs = refs[:6]
        dfl_ref, wa_ref, wf_ref, wb_ref, x_ref, g_ref, dh1_ref = refs[6:13]
        sent = refs[13:13 + ne]
        gx_ref, acc_ref = refs[13 + ne:15 + ne]
        received, sems = refs[15 + ne:15 + 2 * ne], refs[15 + 2 * ne:]

        @pl.when(pl.program_id(0) == 0)
        def _():
            acc_ref[...] = jnp.zeros_like(acc_ref)
            for cp in _chip_copies(sent, received, *sems) if ne else ():
                cp.start()

        if ne:
            @pl.when(pl.program_id(0) == nt - 1)
            def _():
                for cp in _chip_copies(sent, received, *sems):
                    cp.wait()

        dxn = _dot(dfl_ref[...], wf_ref[...])
        for s in range(3):
            dxn = dxn + _dot(dz_refs[s][...], wa_ref[s * D:(s + 1) * D, :])
            dxn = dxn + _dot(dz_refs[3 + s][...], wb_ref[s * D:(s + 1) * D, :])
        xv = x_ref[...]
        rstd = _rstd(xv)
        xhat = xv * rstd
        gx_ref[...] = dh1_ref[...] + _rms_bwd(dxn * g_ref[...], xhat, rstd)
        acc_ref[0:1, :] += jnp.sum(dxn * xhat, axis=0, keepdims=True)

    outs = _call(
        body, name="dx", grid=(nt,),
        in_specs=[_row_spec(tm, D)] * 6 + [_row_spec(tm, LANES), _weight_spec((3 * D, D)), _weight_spec((LANES, D)),
                                           _weight_spec((3 * D, D)), _row_spec(tm, D), _const_spec((1, D)),
                                           _row_spec(tm, D)] + [HBM_SPEC] * ne,
        out_specs=[_row_spec(tm, D), _const_spec((SUBLANES, D))] + [HBM_SPEC] * ne,
        out_shape=[jax.ShapeDtypeStruct((T, D), F32), jax.ShapeDtypeStruct((SUBLANES, D), F32)]
        + [jax.ShapeDtypeStruct(s.shape, s.dtype) for s in exchange],
        scratch_shapes=[pltpu.SemaphoreType.DMA((3 * ne,)), pltpu.SemaphoreType.DMA((3 * ne,))] if ne else [],
        compiler_params=_cparams(("arbitrary",), VMEM_BIG),
    )(*dz, dfl, w_a, w_f, w_b, x, pre_gain, dh1, *exchange)
    return outs[0], outs[1], list(outs[2:])


GRAD_ROWS = D_IN + SUBLANES


def _dw_in_segments(dz_a, dz_b, xn, buf, pair, bt):
    T = xn.shape[0]
    nt = T // bt
    first, second = [(2 * pair + k) * D + (H if 2 * pair + k >= 3 else 0) for k in (0, 1)]
    step8 = (second - first) // SUBLANES

    def body(*refs):
        dza_ref, dzb_ref, xn_ref, o_ref = refs[0], refs[1], refs[2], refs[-1]

        @pl.when(pl.program_id(1) == 0)
        def _():
            o_ref[...] = jnp.zeros_like(o_ref)

        @pl.when(pl.program_id(0) == 0)
        def _():
            o_ref[...] += _dot_tn(dza_ref[...], xn_ref[...])

        @pl.when(pl.program_id(0) == 1)
        def _():
            o_ref[...] += _dot_tn(dzb_ref[...], xn_ref[...])

    spec_a = pl.BlockSpec((bt, D), lambda s, t: (jnp.where(s == 0, t, nt - 1), 0))
    spec_b = pl.BlockSpec((bt, D), lambda s, t: (jnp.where(s == 1, t, 0), 0))
    return _call(
        body, name="dw_in_%d" % pair, grid=(2, nt),
        in_specs=[spec_a, spec_b, pl.BlockSpec((bt, D), lambda s, t: (t, 0))]
        + ([] if buf is None else [pl.BlockSpec(memory_space=pl.ANY)]),
        out_specs=pl.BlockSpec((pl.Element(D), pl.Element(D)),
                               lambda s, t: ((first // SUBLANES + s * step8) * SUBLANES, 0)),
        out_shape=jax.ShapeDtypeStruct((GRAD_ROWS, D), F32),
        input_output_aliases={} if buf is None else {3: 0},
        compiler_params=_cparams(("arbitrary", "arbitrary"), VMEM_BIG),
    )(*((dz_a, dz_b, xn) if buf is None else (dz_a, dz_b, xn, buf)))


def _dw_in_t(dz, dfl, xn, bt=DW_TOKENS):
    T = xn.shape[0]
    bt = min(bt, T)
    nt = T // bt
    main = None
    for pair in range(3):
        main = _dw_in_segments(dz[2 * pair], dz[2 * pair + 1], xn, main, pair, bt)

    def f_body(dfl_ref, xn_ref, main_ref, o_ref, acc_s):
        p = pl.program_id(0)
        t = pl.program_id(1)

        @pl.when(t == 0)
        def _():
            acc_s[...] = jnp.zeros_like(acc_s)

        @pl.when(p == 0)
        def _():
            acc_s[...] += _dot_tn(dfl_ref[...], xn_ref[...])

        @pl.when(t == nt - 1)
        def _():
            o_ref[...] = acc_s[:SUBLANES, :]

    fl_block = FL0 // SUBLANES
    end_block = D_IN // SUBLANES
    return _call(
        f_body, name="dw_in_f", grid=(2, nt),
        in_specs=[pl.BlockSpec((bt, LANES), lambda p, t: (t, 0)), pl.BlockSpec((bt, D), lambda p, t: (t, 0)),
                  pl.BlockSpec(memory_space=pl.ANY)],
        out_specs=pl.BlockSpec((SUBLANES, D), lambda p, t: (fl_block + p * (end_block - fl_block), 0)),
        out_shape=jax.ShapeDtypeStruct((GRAD_ROWS, D), F32),
        scratch_shapes=[pltpu.VMEM((LANES, D), F32)],
        input_output_aliases={2: 0},
        compiler_params=_cparams(("arbitrary", "arbitrary")),
    )(dfl, xn, main)


def _matmul_tn(a, b, name, bm=512, bn=1024, bt=DW_TOKENS):
    T, M = a.shape
    N = b.shape[1]
    bm, bn, bt = min(bm, M), min(bn, N), min(bt, T)

    def body(a_ref, b_ref, o_ref):
        @pl.when(pl.program_id(2) == 0)
        def _():
            o_ref[...] = jnp.zeros_like(o_ref)

        o_ref[...] += _dot_tn(a_ref[...], b_ref[...])

    return _call(
        body, name=name, grid=(M // bm, N // bn, T // bt),
        in_specs=[pl.BlockSpec((bt, bm), lambda i, j, t: (t, i)), pl.BlockSpec((bt, bn), lambda i, j, t: (t, j))],
        out_specs=pl.BlockSpec((bm, bn), lambda i, j, t: (i, j)),
        out_shape=jax.ShapeDtypeStruct((M, N), F32),
        compiler_params=_cparams(("parallel", "parallel", "arbitrary")),
    )(a, b)


HBM_SPEC = pl.BlockSpec(memory_space=pltpu.HBM)
VMEM_SPEC = pl.BlockSpec(memory_space=pltpu.VMEM)


def _position():
    return lax.axis_index("x"), lax.axis_index("y"), lax.axis_index("c")


def _other_chips(x, y):
    return [(1 - x, y), (x, 1 - y), (1 - x, 1 - y)]


def _gather_shards(shards, whole):
    na, nw = len(shards), len(whole)
    nall = na + nw

    def body(*refs):
        gather = _GatherPlan(refs[:nall], refs[nall:2 * nall], refs[2 * nall:], na)
        gather.send()
        gather.forward()
        gather.finish()

    arrs = list(shards) + list(whole)
    outs = _call(
        body, name="gather_shards",
        in_specs=[HBM_SPEC] * nall, out_specs=[HBM_SPEC] * nall,
        out_shape=_gather_out_shapes(arrs), scratch_shapes=_gather_semaphores(na, nall),
    )(*arrs)
    return _place_own(outs, arrs)


def _gather_out_shapes(arrs):
    return [jax.ShapeDtypeStruct((N_CHIPS,) + s.shape, s.dtype) for s in arrs]


def _gather_semaphores(na, nall):
    return [pltpu.SemaphoreType.DMA((3 * nall,)), pltpu.SemaphoreType.DMA((3 * nall,)),
            pltpu.SemaphoreType.DMA((3 * na,)), pltpu.SemaphoreType.DMA((3 * na,))]


def _place_own(outs, arrs):
    if not arrs:
        return []
    chip = 2 * lax.axis_index("x") + lax.axis_index("y")
    return [lax.dynamic_update_slice(o, a[None], (chip,) + (0,) * a.ndim) for o, a in zip(outs, arrs)]


class _GatherPlan:
    def __init__(self, srcs, dsts, sems, na):
        ici_send, ici_recv, d2d_send, d2d_recv = sems
        x, y, c = _position()
        chip = 2 * x + y
        nall = len(srcs)

        def half(a, which):
            rows = srcs[a].shape[0] // 2
            return pl.ds(pl.multiple_of(which * rows, BF16_ROWS), rows)

        def copy(src, dst, send, recv, k, to):
            return pltpu.make_async_remote_copy(src_ref=src, dst_ref=dst, send_sem=send.at[k], recv_sem=recv.at[k],
                                                device_id=to, device_id_type=MESH)

        self.first, self.landed, self.passed, self.returned = [], [], [], []
        for j, (px, py) in enumerate(_other_chips(x, y)):
            theirs = 2 * px + py
            for a in range(nall):
                k = j * nall + a
                if a < na:
                    self.first.append(copy(srcs[a].at[half(a, c), :], dsts[a].at[chip, half(a, c), :],
                                           ici_send, ici_recv, k, (px, py, c)))
                    mine = dsts[a].at[theirs, half(a, c), :]
                    other = dsts[a].at[theirs, half(a, 1 - c), :]
                    self.landed.append(copy(mine, mine, ici_send, ici_recv, k, (px, py, c)))
                    self.passed.append(copy(mine, mine, d2d_send, d2d_recv, j * na + a, (x, y, 1 - c)))
                    self.returned.append(copy(other, other, d2d_send, d2d_recv, j * na + a, (x, y, 1 - c)))
                else:
                    self.first.append(copy(srcs[a], dsts[a].at[chip], ici_send, ici_recv, k, (px, py, c)))
                    got = dsts[a].at[theirs]
                    self.landed.append(copy(got, got, ici_send, ici_recv, k, (px, py, c)))
                    self.passed.append(None)

    def send(self):
        for cp in self.first:
            cp.start()

    def forward(self):
        for arrival, fwd in zip(self.landed, self.passed):
            arrival.wait_recv()
            if fwd is not None:
                fwd.start()

    def finish(self):
        for cp in self.returned:
            cp.wait_recv()
        for cp in self.first + [f for f in self.passed if f is not None]:
            cp.wait_send()


W_ROWS = 1568
G_ROWS = 1552
SHARD_ROWS = D_IN // N_CHIPS
WINDOW_STEP = 1536


def _assemble_w_in(cont):
    cb = COL_BLOCK
    half = WINDOW_STEP
    seam = BF16_ROWS

    def body(c_ref, wa_ref, wf_ref, wb_ref):
        x0 = c_ref[0].astype(F32)
        x1, x2, x3 = (pltpu.roll(c_ref[j].astype(F32), 2 * j, 0) for j in (1, 2, 3))
        wa = jnp.concatenate([x0[:half], x0[half:half + seam] + x1[:seam], x1[seam:half]], axis=0)
        wa_ref[...] = wa.astype(BF16)

        fl = x1[half:half + seam] + x2[:seam]
        row = lax.broadcasted_iota(jnp.int32, fl.shape, 0)
        wf_ref[:seam, :] = jnp.where(row < H, fl, 0.0).astype(BF16)
        wf_ref[seam:, :] = jnp.zeros((LANES - seam, cb), BF16)

        mid = x2[half:half + SUBLANES] + x3[:SUBLANES]
        wb = jnp.concatenate([x2[SUBLANES:half], mid, x3[SUBLANES:half + SUBLANES]], axis=0)
        wb_ref[...] = wb.astype(BF16)

    return _call(
        body, name="assemble_w_in", grid=(D // cb,),
        in_specs=[pl.BlockSpec((N_CHIPS, W_ROWS, cb), lambda i: (0, 0, i))],
        out_specs=[pl.BlockSpec((3 * D, cb), lambda i: (0, i)), pl.BlockSpec((LANES, cb), lambda i: (0, i)),
                   pl.BlockSpec((3 * D, cb), lambda i: (0, i))],
        out_shape=[jax.ShapeDtypeStruct((3 * D, D), BF16), jax.ShapeDtypeStruct((LANES, D), BF16),
                   jax.ShapeDtypeStruct((3 * D, D), BF16)],
        compiler_params=_cparams(("parallel",)),
    )(cont)


def _pair_exchange_windows(grad_t):
    half_g = G_ROWS // 2

    def body(g_ref, got, send_sems, recv_sems):
        x, y, c = _position()
        copies = []
        for j in range(N_CHIPS):
            rows = pl.ds(pl.multiple_of(j * WINDOW_STEP + (1 - c) * half_g, SUBLANES), half_g)
            copies.append(pltpu.make_async_remote_copy(
                src_ref=g_ref.at[rows, :], dst_ref=got.at[j], send_sem=send_sems.at[j], recv_sem=recv_sems.at[j],
                device_id=(x, y, 1 - c), device_id_type=MESH))
        for cp in copies:
            cp.start()
        for cp in copies:
            cp.wait()

    return _call(
        body, name="pair_exchange_w_in",
        in_specs=[HBM_SPEC], out_specs=HBM_SPEC,
        out_shape=jax.ShapeDtypeStruct((N_CHIPS, half_g, D), F32),
        scratch_shapes=[pltpu.SemaphoreType.DMA((N_CHIPS,)), pltpu.SemaphoreType.DMA((N_CHIPS,))],
    )(grad_t)


def _pair_sum(parts, gots, c):
    na = len(parts)

    def body(c_ref, *refs):
        for a in range(na):
            refs[2 * na + a][...] = (refs[a][...] + refs[na + a][...]).astype(BF16)

    mine = [pl.BlockSpec(g.shape, lambda i, c_ref: (0, c_ref[0], 0)) for g in gots]
    whole = [pl.BlockSpec(g.shape, lambda i, c_ref: (0, 0, 0)) for g in gots]
    grid_spec = pltpu.PrefetchScalarGridSpec(
        num_scalar_prefetch=1, grid=(1,), in_specs=mine + whole, out_specs=whole)
    return _call(
        body, name="pair_sum", grid_spec=grid_spec,
        out_shape=[jax.ShapeDtypeStruct(g.shape, BF16) for g in gots],
        compiler_params=_cparams(("arbitrary",), VMEM_BIG),
    )(c.reshape(1), *parts, *gots)


def _pair_sum_windows(grad_t, got, c):
    _, half, C = got.shape
    cb = COL_BLOCK

    def body(c_ref, a_ref, b_ref, o_ref):
        o_ref[0] = (a_ref[...] + b_ref[0]).astype(BF16)

    def mine(j, i, c_ref):
        return ((j * (WINDOW_STEP // SUBLANES) + c_ref[0] * (half // SUBLANES)) * SUBLANES, i * cb)

    spec = pl.BlockSpec((1, half, cb), lambda j, i, c_ref: (j, 0, i))
    grid_spec = pltpu.PrefetchScalarGridSpec(
        num_scalar_prefetch=1, grid=(N_CHIPS, C // cb),
        in_specs=[pl.BlockSpec((pl.Element(half), pl.Element(cb)), mine), spec], out_specs=spec)
    return _call(
        body, name="pair_sum_w_in", grid_spec=grid_spec,
        out_shape=jax.ShapeDtypeStruct((N_CHIPS, half, C), BF16),
        compiler_params=_cparams(("parallel", "parallel")),
    )(c.reshape(1), grad_t, got)


def _chip_sum(own, got, chip, name):
    _, half, C = got.shape
    cb = min(C, COL_BLOCK)

    def body(chip_ref, own_ref, g_ref, o_ref):
        for me in range(N_CHIPS):
            @pl.when(chip_ref[0] == me)
            def _(me=me):
                terms = [own_ref[0] if k == me else g_ref[k] for k in range(N_CHIPS)]
                acc = terms[0].astype(F32) + terms[1].astype(F32)
                acc = acc + terms[2].astype(F32)
                o_ref[...] = acc + terms[3].astype(F32)

    grid_spec = pltpu.PrefetchScalarGridSpec(
        num_scalar_prefetch=1, grid=(C // cb,),
        in_specs=[pl.BlockSpec((1, half, cb), lambda i, chip_ref: (chip_ref[0], 0, i)),
                  pl.BlockSpec((N_CHIPS, half, cb), lambda i, chip_ref: (0, 0, i))],
        out_specs=pl.BlockSpec((half, cb), lambda i, chip_ref: (0, i)))
    return _call(
        body, name=name, grid_spec=grid_spec,
        out_shape=jax.ShapeDtypeStruct((half, C), F32),
        compiler_params=_cparams(("parallel",)),
    )(chip.reshape(1), own, got)


def _final_exchange(halves, g):
    na = len(halves)
    rows = g.shape[0]
    per = rows // N_DEV

    def body(*refs):
        srcs, g_ref = refs[:na], refs[na]
        dsts, out_ref = refs[na + 1:2 * na + 1], refs[2 * na + 1]
        got_ref, s1, r1, s2, r2, swap_send, swap_recv = refs[2 * na + 2:]
        x, y, c = _position()
        swaps = [pltpu.make_async_remote_copy(
            src_ref=srcs[a], dst_ref=dsts[a], send_sem=swap_send.at[a], recv_sem=swap_recv.at[a],
            device_id=(x, y, 1 - c), device_id_type=MESH) for a in range(na)]
        for cp in swaps:
            cp.start()
        me = 4 * x + 2 * y + c
        mine = pl.ds(pl.multiple_of(me * per, SUBLANES), per)
        peers = []
        for j in range(1, N_DEV):
            px = 1 - x if j & 4 else x
            py = 1 - y if j & 2 else y
            pc = 1 - c if j & 1 else c
            peers.append((px, py, pc))

        first = []
        for j, (px, py, pc) in enumerate(peers):
            theirs = pl.ds(pl.multiple_of((4 * px + 2 * py + pc) * per, SUBLANES), per)
            first.append(pltpu.make_async_remote_copy(
                src_ref=g_ref.at[theirs, :], dst_ref=got_ref.at[me], send_sem=s1.at[j], recv_sem=r1.at[j],
                device_id=(px, py, pc), device_id_type=MESH))
        for cp in first:
            cp.start()
        got_ref[me] = g_ref[mine, :]
        for cp in first:
            cp.wait()
        total = got_ref[0]
        for d in range(1, N_DEV):
            total = total + got_ref[d]
        out_ref[mine, :] = total

        second = []
        for j, peer in enumerate(peers):
            second.append(pltpu.make_async_remote_copy(
                src_ref=out_ref.at[mine, :], dst_ref=out_ref.at[mine, :], send_sem=s2.at[j], recv_sem=r2.at[j],
                device_id=peer, device_id_type=MESH))
        for cp in second:
            cp.start()
        for cp in second + swaps:
            cp.wait()

    sems = pltpu.SemaphoreType.DMA((N_DEV - 1,))
    swap_sems = pltpu.SemaphoreType.DMA((na,))
    outs = _call(
        body, name="final_exchange", in_hbm=False,
        in_specs=[HBM_SPEC] * na + [VMEM_SPEC], out_specs=[HBM_SPEC] * na + [VMEM_SPEC],
        out_shape=[jax.ShapeDtypeStruct(s.shape, s.dtype) for s in halves] + [jax.ShapeDtypeStruct(g.shape, F32)],
        scratch_shapes=[pltpu.VMEM((N_DEV, per, LANES), F32), sems, sems, sems, sems, swap_sems, swap_sems],
    )(*halves, g)
    return outs[:na], outs[na]


def _adamw_math(g, w, m, v):
    m2 = ADAM_B1 * m + (1.0 - ADAM_B1) * g
    v2 = ADAM_B2 * v + (1.0 - ADAM_B2) * (g * g)
    m_hat = m2 / (1.0 - ADAM_B1 ** ADAM_STEP)
    v_hat = v2 / (1.0 - ADAM_B2 ** ADAM_STEP)
    delta = (-ADAM_LR) * (m_hat / (jnp.sqrt(v_hat) + ADAM_EPS) + ADAM_WD * w)
    return delta, m2, v2


ADAMW_BLOCK_BYTES = 1 << 20


def _adamw_big(g, w, m, v, name):
    R, C = g.shape
    bc = min(C, max(LANES, ADAMW_BLOCK_BYTES // (4 * R) // LANES * LANES))

    def body(g_ref, w_ref, m_ref, v_ref, d_ref, m2_ref, v2_ref):
        d_ref[...], m2_ref[...], v2_ref[...] = _adamw_math(g_ref[...], w_ref[...], m_ref[...], v_ref[...])

    spec = pl.BlockSpec((R, bc), lambda j: (0, j))
    out = jax.ShapeDtypeStruct((R, C), F32)
    return _call(
        body, name=name, grid=(C // bc,),
        in_specs=[spec] * 4, out_specs=[spec] * 3, out_shape=[out] * 3,
        compiler_params=_cparams(("parallel",)),
    )(g, w, m, v)


def _adamw_small(gs, ws, ms, vs):
    n = len(gs)

    def body(*refs):
        for a in range(n):
            g_ref, w_ref, m_ref, v_ref = (refs[k * n + a] for k in range(4))
            d_ref, m2_ref, v2_ref = (refs[(4 + k) * n + a] for k in range(3))
            d_ref[...], m2_ref[...], v2_ref[...] = _adamw_math(g_ref[...], w_ref[...], m_ref[...], v_ref[...])

    outs = [jax.ShapeDtypeStruct(w.shape, F32) for w in ws]
    specs = [_const_spec(w.shape) for w in ws]
    return _call(
        body, name="adamw_small", grid=(1,),
        in_specs=specs * 4, out_specs=specs * 3, out_shape=outs * 3,
    )(*gs, *ws, *ms, *vs)


def _late_weights(st_out, st_ple, st_gate, st_conv):
    return st_out.reshape(DMIX, D), _from_chip_cols(st_ple), st_gate.reshape(D, D), _from_chip_cols(st_conv)


def _local_step(x, p, tgt, w_a, w_f, w_b, late, b_f, pre_gain, post_gain, conv_b,
                w_rgate, b_rgate, w_igate, b_igate, lam, gain_a, gain_l, ple_gain, b_gate,
                gather_late=False, early_reduce=None, w_in_reduce=None):
    b_f_pad = jnp.pad(b_f, ((0, 0), (0, LANES - H)))
    w_r = w_rgate.astype(BF16)
    w_i = w_igate.astype(BF16)

    xn, q_aug, k_aug, v_aug, g_attn, x_lru, g_lru, flb, vt_aug = _in_proj(x, pre_gain, w_a, w_f, w_b, b_f_pad)
    if gather_late:
        o, qx, stacks = _attn_fwd(q_aug, k_aug, vt_aug, late[:3], late[3:])
        late = _late_weights(*stacks)
    else:
        o, qx, _ = _attn_fwd(q_aug, k_aug, vt_aug)
    w_out_b, w_ple_b, w_gate_b, conv_w = late
    ycat, xc, h = _branches_fwd(o, g_attn, x_lru, g_lru, gain_a, gain_l, conv_w, conv_b, w_r, b_rgate, w_i, b_igate,
                                lam)
    dh1, dycat, dmix, h1b, dgp, pb, dpe, acc_t = _tail(ycat, x, p, tgt, w_out_b, post_gain, w_ple_b, ple_gain,
                                                       w_gate_b, b_gate)
    late_grads = [_matmul_tn(ycat, dmix, "dw_out"), _matmul_tn(pb, dpe, "dw_ple"),
                  _matmul_tn(h1b, dgp, "dw_ple_gate")]
    if early_reduce is None:
        do_aug, dg_attn, dg_lru, dh, acc_b, _ = _branches_bwd(dycat, o, g_attn, h, g_lru, gain_a, gain_l)
    else:
        parts = [late_grads[0].reshape(N_CHIPS, DMIX // N_CHIPS, D), _by_chip_cols(late_grads[1]),
                 late_grads[2].reshape(N_CHIPS, D // N_CHIPS, D)]
        do_aug, dg_attn, dg_lru, dh, acc_b, got = _branches_bwd(dycat, o, g_attn, h, g_lru, gain_a, gain_l, parts)
        sent = _pair_sum(parts, got, early_reduce)
    dx_lru, gw_r, gw_i, acc_l = _lru_bwd(dh, h, xc, x_lru, conv_w, w_r, b_rgate, w_i, b_igate, lam)
    if early_reduce is None:
        dq, dk, dv, dc_heads, _ = _attn_bwd(q_aug, qx, k_aug, v_aug, do_aug)
    else:
        dq, dk, dv, dc_heads, received = _attn_bwd(q_aug, qx, k_aug, v_aug, do_aug, sent)
        late_grads = list(zip(sent, received))
    dfl, acc_f = _fgate_bwd(dc_heads, flb)
    dz = (dq, dk, dv, dg_attn, dx_lru, dg_lru)
    grad_t = _dw_in_t(dz, dfl, xn)
    if w_in_reduce is None:
        grad_x, acc_x, _ = _dx(dz, dfl, w_a, w_f, w_b, x, pre_gain, dh1)
    else:
        sent = w_in_reduce(grad_t)
        grad_x, acc_x, (received,) = _dx(dz, dfl, w_a, w_f, w_b, x, pre_gain, dh1, [sent])
        grad_t = (sent, received)

    grads = dict(
        w_in_t=grad_t,
        w_out=late_grads[0],
        w_ple=late_grads[1],
        w_ple_gate=late_grads[2],
        w_rgate=gw_r,
        w_igate=gw_i,
        b_f=acc_f[0:1, :H],
        pre_gain=acc_x[0:1],
        post_gain=acc_t[0:1],
        conv_w=acc_l[0:4],
        conv_b=acc_l[4:5],
        b_rgate=acc_l[5:6],
        b_igate=acc_l[6:7],
        lru_lambda=acc_l[7:8],
        attn_out_gain=acc_b[0:1],
        lru_out_gain=acc_b[1:2],
        ple_gain=acc_t[1:2],
        b_ple_gate=acc_t[2:3],
    )
    loss = jnp.sum(acc_t[3])
    return loss, grad_x, grads


SMALL_ROWS = ["b_f", "pre_gain", "post_gain", "conv_w", "conv_b", "b_rgate", "b_igate", "lru_lambda",
              "attn_out_gain", "lru_out_gain", "ple_gain", "b_ple_gate"]
WEIGHTS = ["w_in", "b_f", "pre_gain", "post_gain", "conv_w", "conv_b", "w_rgate", "b_rgate", "w_igate", "b_igate",
           "lru_lambda", "attn_out_gain", "lru_out_gain", "w_out", "w_ple", "ple_gain", "w_ple_gate", "b_ple_gate"]
SHARDED = ["w_in", "w_out", "w_ple", "w_ple_gate"]


def _by_chip_cols(g):
    r, cols = g.shape
    return g.reshape(r, N_CHIPS, cols // N_CHIPS).transpose(1, 0, 2)


def _from_chip_cols(s):
    n, r, cols = s.shape
    return s.transpose(1, 0, 2).reshape(r, n * cols)


def kernel(x, p, w_in, b_f, pre_gain, post_gain, conv_w, conv_b, w_rgate, b_rgate, w_igate, b_igate, lru_lambda, attn_out_gain, lru_out_gain, w_out, w_ple, ple_gain, w_ple_gate, b_ple_gate, loss_target, m_w_in, m_b_f, m_pre_gain, m_post_gain, m_conv_w, m_conv_b, m_w_rgate, m_b_rgate, m_w_igate, m_b_igate, m_lru_lambda, m_attn_out_gain, m_lru_out_gain, m_w_out, m_w_ple, m_ple_gain, m_w_ple_gate, m_b_ple_gate, v_w_in, v_b_f, v_pre_gain, v_post_gain, v_conv_w, v_conv_b, v_w_rgate, v_b_rgate, v_w_igate, v_b_igate, v_lru_lambda, v_attn_out_gain, v_lru_out_gain, v_w_out, v_w_ple, v_ple_gain, v_w_ple_gate, v_b_ple_gate):
    w = dict(w_in=w_in, b_f=b_f, pre_gain=pre_gain, post_gain=post_gain, conv_w=conv_w, conv_b=conv_b,
             w_rgate=w_rgate, b_rgate=b_rgate, w_igate=w_igate, b_igate=b_igate, lru_lambda=lru_lambda,
             attn_out_gain=attn_out_gain, lru_out_gain=lru_out_gain, w_out=w_out, w_ple=w_ple, ple_gain=ple_gain,
             w_ple_gate=w_ple_gate, b_ple_gate=b_ple_gate)
    m = dict(w_in=m_w_in, b_f=m_b_f, pre_gain=m_pre_gain, post_gain=m_post_gain, conv_w=m_conv_w, conv_b=m_conv_b,
             w_rgate=m_w_rgate, b_rgate=m_b_rgate, w_igate=m_w_igate, b_igate=m_b_igate, lru_lambda=m_lru_lambda,
             attn_out_gain=m_attn_out_gain, lru_out_gain=m_lru_out_gain, w_out=m_w_out, w_ple=m_w_ple,
             ple_gain=m_ple_gain, w_ple_gate=m_w_ple_gate, b_ple_gate=m_b_ple_gate)
    v = dict(w_in=v_w_in, b_f=v_b_f, pre_gain=v_pre_gain, post_gain=v_post_gain, conv_w=v_conv_w, conv_b=v_conv_b,
             w_rgate=v_w_rgate, b_rgate=v_b_rgate, w_igate=v_w_igate, b_igate=v_b_igate, lru_lambda=v_lru_lambda,
             attn_out_gain=v_attn_out_gain, lru_out_gain=v_lru_out_gain, w_out=v_w_out, w_ple=v_w_ple,
             ple_gain=v_ple_gain, w_ple_gate=v_w_ple_gate, b_ple_gate=v_b_ple_gate)
    xi, yi, ci = _position()
    chip = 2 * xi + yi

    w_in_t, m_in_t, v_in_t = (jnp.swapaxes(t[0], 0, 1) for t in (w_in, m_w_in, v_w_in))
    window = jnp.pad(w_in_t.astype(BF16), ((0, W_ROWS - SHARD_ROWS), (0, 0)))

    (st_in,) = _gather_shards([window], [])
    w_a, w_f, w_b = _assemble_w_in(st_in)
    late_shards = (w_out[0].astype(BF16), w_ple[0].astype(BF16), w_ple_gate[0].astype(BF16), conv_w[0])

    loss, grad_x, g = _local_step(
        x[0], p[0, 0], loss_target[0], w_a, w_f, w_b, late_shards, b_f, pre_gain, post_gain,
        conv_b, w_rgate[0], b_rgate, w_igate[0], b_igate, lru_lambda, attn_out_gain, lru_out_gain, ple_gain,
        b_ple_gate, gather_late=True, early_reduce=ci,
        w_in_reduce=lambda grad_t: _pair_sum_windows(grad_t, _pair_exchange_windows(grad_t), ci))

    sums = [g["w_in_t"][0]] + [g[n][0] for n in SHARDED[1:]]
    recv = [g["w_in_t"][1]] + [g[n][1] for n in SHARDED[1:]]
    halves = [_chip_sum(sums[a], recv[a], chip, "chip_sum_%d" % a) for a in range(4)]

    rows = [jnp.pad(g["b_f"], ((0, 0), (0, D - H)))] + [g[n] for n in SMALL_ROWS[1:]]
    rows.append(jnp.pad(loss.reshape(1, 1), ((0, 0), (0, D - 1))))
    packed = jnp.concatenate([g["w_rgate"].reshape(NB * LANES, LANES), g["w_igate"].reshape(NB * LANES, LANES),
                              jnp.concatenate(rows, axis=0).reshape(LANES, LANES)], axis=0)
    theirs, summed = _final_exchange(halves, packed)
    full = [jnp.concatenate([jnp.where(ci == 0, a, b), jnp.where(ci == 0, b, a)], axis=0)
            for a, b in zip(halves, theirs)]
    red = dict(zip(SHARDED, full))
    red["w_in"] = lax.dynamic_slice_in_dim(red["w_in"], 2 * chip, SHARD_ROWS, axis=0)
    red["w_rgate"] = summed[:D].reshape(1, NB, LANES, LANES)
    red["w_igate"] = summed[D:2 * D].reshape(1, NB, LANES, LANES)
    vec = summed[2 * D:].reshape(16, D)
    loss = vec[15, 0]
    r0 = 0
    for n in SMALL_ROWS:
        nr = 4 if n == "conv_w" else 1
        red[n] = vec[r0:r0 + nr]
        r0 += nr
    red["b_f"] = red["b_f"][:, :H]
    red["conv_w"] = lax.dynamic_slice_in_dim(red["conv_w"], chip * (D // N_CHIPS), D // N_CHIPS, axis=1)[None]

    delta, new_m, new_v = {}, {}, {}
    outs_in = _adamw_big(red["w_in"], w_in_t, m_in_t, v_in_t, "adamw_w_in")
    delta["w_in"], new_m["w_in"], new_v["w_in"] = (jnp.swapaxes(t, 0, 1)[None] for t in outs_in)
    red["w_in"] = jnp.swapaxes(red["w_in"], 0, 1)[None]
    for n in SHARDED[1:]:
        delta[n], new_m[n], new_v[n] = (t[None] for t in _adamw_big(red[n], w[n][0], m[n][0], v[n][0], "adamw_" + n))
        red[n] = red[n][None]
    small = [n for n in WEIGHTS if n not in SHARDED]
    outs = _adamw_small([red[n] for n in small], [w[n] for n in small], [m[n] for n in small],
                        [v[n] for n in small])
    ns = len(small)
    for a, n in enumerate(small):
        delta[n], new_m[n], new_v[n] = outs[a], outs[ns + a], outs[2 * ns + a]

    return (loss, grad_x[None], *[red[n] for n in WEIGHTS], *[delta[n] for n in WEIGHTS],
            *[new_m[n] for n in WEIGHTS], *[new_v[n] for n in WEIGHTS])
```

```python
import jax
import jax.numpy as jnp
import numpy as np
from jax import lax
from jax.experimental import pallas as pl
from jax.experimental.pallas import tpu as pltpu

F32 = jnp.float32
BF16 = jnp.bfloat16

D = 1024
H = 8
DH = 128
NB = 8
DPLE = 256
DMIX = 2 * D
D_IN = 4 * D + H + 2 * D
FL0 = 3 * D
RMS_EPS = 1e-6
LRU_C = 8.0
NEG = -1e30
LANES = 128
SUBLANES = 8
BF16_ROWS = 16
COL_BLOCK = 256
DW_TOKENS = 2048

ADAM_LR = 0.001
ADAM_B1 = 0.9
ADAM_B2 = 0.999
ADAM_EPS = 1e-08
ADAM_WD = 0.01
ADAM_STEP = 10

TM = 256
TA = 512
FWD_HEADS = 8
BWD_HEADS = 2
VMEM_BIG = 56 * 1024 * 1024
VMEM_MID = 40 * 1024 * 1024

MESH = pl.DeviceIdType.MESH
N_CHIPS = 4
N_DEV = 8


def _call(body, *, out_shape, in_hbm=True, **kwargs):
    if not in_hbm:
        return pl.pallas_call(body, out_shape=out_shape, **kwargs)

    def pin(shape):
        return pltpu.HBM(shape.shape, shape.dtype) if isinstance(shape, jax.ShapeDtypeStruct) else shape

    fn = pl.pallas_call(body, out_shape=jax.tree.map(pin, out_shape), **kwargs)

    def run(*args):
        return fn(*[a if a.dtype == jnp.int32 else pltpu.with_memory_space_constraint(a, pltpu.HBM) for a in args])

    return run


def _cparams(sem, vmem=VMEM_MID):
    return pltpu.CompilerParams(dimension_semantics=sem, vmem_limit_bytes=vmem)


def _sigmoid(x):
    return 0.5 * jnp.tanh(0.5 * x) + 0.5


def _rstd(x):
    return lax.rsqrt(jnp.mean(x * x, axis=-1, keepdims=True) + RMS_EPS)


def _rms_bwd(t, xhat, rstd):
    return rstd * (t - xhat * jnp.mean(t * xhat, axis=-1, keepdims=True))


def _dot(a, b):
    return jnp.dot(a, b, preferred_element_type=F32)


def _dot_nt(a, b):
    return lax.dot_general(a, b, (((1,), (1,)), ((), ())), preferred_element_type=F32)


def _dot_tn(a, b):
    return lax.dot_general(a, b, (((0,), (0,)), ((), ())), preferred_element_type=F32)


def _dot_exact(a, b):
    return jnp.dot(a, b, preferred_element_type=F32, precision=lax.Precision.HIGHEST)


def _shift_down(x, j, halo):
    rolled = pltpu.roll(x, j, 0)
    row = lax.broadcasted_iota(jnp.int32, halo.shape, 0)
    top = jnp.where(row < j, pltpu.roll(halo, j, 0), rolled[:SUBLANES])
    return jnp.concatenate([top, rolled[SUBLANES:]], axis=0)


def _shift_up(x, j, nxt):
    tm = x.shape[0]
    rolled = pltpu.roll(x, tm - j, 0)
    row = lax.broadcasted_iota(jnp.int32, nxt.shape, 0)
    bot = jnp.where(row >= SUBLANES - j, pltpu.roll(nxt, SUBLANES - j, 0), rolled[tm - SUBLANES:])
    return jnp.concatenate([rolled[:tm - SUBLANES], bot], axis=0)


def _scan_fwd_into(a, u, carry, h_ref):
    tm, width = a.shape
    groups = (tm // SUBLANES, SUBLANES, width)
    a, u = a.reshape(groups), u.reshape(groups)
    sub = lax.broadcasted_iota(jnp.int32, groups, 1)
    d = 1
    while d < SUBLANES:
        keep = sub >= d
        a_s = jnp.where(keep, pltpu.roll(a, d, 1), 1.0)
        u_s = jnp.where(keep, pltpu.roll(u, d, 1), 0.0)
        u = u + a * u_s
        a = a * a_s
        d *= 2
    a, u = a.reshape(tm, width), u.reshape(tm, width)
    for g in range(tm // SUBLANES):
        rows = slice(g * SUBLANES, (g + 1) * SUBLANES)
        h_ref[rows, :] = u[rows] + a[rows] * carry
        carry = h_ref[(g + 1) * SUBLANES - 1:(g + 1) * SUBLANES, :]
    return carry


def _scan_bwd_into(b, u, g_ref):
    tm, width = b.shape
    groups = (tm // SUBLANES, SUBLANES, width)
    b, u = b.reshape(groups), u.reshape(groups)
    sub = lax.broadcasted_iota(jnp.int32, groups, 1)
    d = 1
    while d < SUBLANES:
        keep = sub < SUBLANES - d
        b_s = jnp.where(keep, pltpu.roll(b, SUBLANES - d, 1), 1.0)
        u_s = jnp.where(keep, pltpu.roll(u, SUBLANES - d, 1), 0.0)
        u = u + b * u_s
        b = b * b_s
        d *= 2
    b, u = b.reshape(tm, width), u.reshape(tm, width)
    nxt = jnp.zeros((1, width), F32)
    for g in reversed(range(tm // SUBLANES)):
        rows = slice(g * SUBLANES, (g + 1) * SUBLANES)
        g_ref[rows, :] = u[rows] + b[rows] * nxt
        nxt = g_ref[g * SUBLANES:g * SUBLANES + 1, :]


def _gate_pre(xc, w_ref):
    outs = []
    for n in range(NB):
        outs.append(_dot(xc[:, n * LANES:(n + 1) * LANES].astype(BF16), w_ref[n]))
    return jnp.concatenate(outs, axis=1)


def _gate_pre_t(d, w_ref):
    outs = []
    for n in range(NB):
        outs.append(_dot_nt(d[:, n * LANES:(n + 1) * LANES].astype(BF16), w_ref[n]))
    return jnp.concatenate(outs, axis=1)


def _softplus_neg(lam):
    return jnp.maximum(-lam, 0.0) + jnp.log(1.0 + jnp.exp(-jnp.abs(lam)))


def _row_spec(tm, width):
    return pl.BlockSpec((tm, width), lambda i: (i, 0))


def _const_spec(shape):
    nd = len(shape)
    return pl.BlockSpec(shape, lambda *_: (0,) * nd)


def _weight_spec(shape):
    nd = len(shape)
    return pl.BlockSpec(shape, lambda *_: (0,) * nd, pipeline_mode=pl.Buffered(1))


AUG = 2 * DH
LOG2E = 1.4426950408889634
LN2 = 0.6931471805599453
Q_SCALE = DH ** -0.5 * LOG2E


def _split3(x):
    hi = x.astype(BF16)
    r1 = x - hi.astype(F32)
    mid = r1.astype(BF16)
    lo = (r1 - mid.astype(F32)).astype(BF16)
    return hi, mid, lo


def _extras(col, ones_from):
    t = col.shape[0]
    hi, mid, lo = _split3(jnp.broadcast_to(col, (t, LANES)))
    lane = lax.broadcasted_iota(jnp.int32, (t, LANES), 1)
    rest = jnp.zeros((t, LANES), BF16)
    if ones_from is not None:
        rest = jnp.where((lane >= ones_from) & (lane < ones_from + 3), 1.0, 0.0).astype(BF16)
    return jnp.where(lane == 0, hi, jnp.where(lane == 1, mid, jnp.where(lane == 2, lo, rest)))


def _selectors():
    sel_q = np.zeros((3 * LANES, H * LANES), np.float32)
    sel_k = np.zeros((3 * LANES, H * LANES), np.float32)
    for hd in range(H):
        for piece in range(3):
            sel_q[piece * LANES + hd, hd * LANES + piece] = 1.0
            sel_k[piece * LANES + hd, hd * LANES + 3 + piece] = -1.0
    return jnp.asarray(sel_q, BF16), jnp.asarray(sel_k, BF16)


def _in_proj(x, pre_gain, w_a, w_f, w_b, b_f_pad):
    T = x.shape[0]
    tm = TM
    sel_q, sel_k = _selectors()

    def body(x_ref, g_ref, wa_ref, wf_ref, wb_ref, bf_ref, sq_ref, sk_ref,
             xn_ref, qa_ref, ka_ref, va_ref, ga_ref, xl_ref, gl_ref, flb_ref, vt_ref, c_s, carry):
        @pl.when(pl.program_id(0) == 0)
        def _():
            carry[...] = jnp.zeros_like(carry)

        xv = x_ref[...]
        xn = (xv * _rstd(xv) * g_ref[...]).astype(BF16)
        xn_ref[...] = xn
        for s, o_ref in enumerate((ga_ref, xl_ref, gl_ref)):
            o_ref[...] = _dot_nt(xn, wb_ref[s * D:(s + 1) * D, :]).astype(o_ref.dtype)
        flb = _dot_nt(xn, wf_ref[...]) + bf_ref[...]
        flb_ref[...] = flb
        lane = lax.broadcasted_iota(jnp.int32, flb.shape, 1)
        ls = jnp.where(lane < H, jnp.minimum(flb, 0.0) - jnp.log(1.0 + jnp.exp(-jnp.abs(flb))), 0.0)
        r = lax.broadcasted_iota(jnp.int32, (tm, tm), 0)
        c = lax.broadcasted_iota(jnp.int32, (tm, tm), 1)
        cs = _dot_exact((c <= r).astype(F32), ls) + carry[...]
        c_s[...] = cs
        carry[...] = c_s[tm - 1:tm, :]

        pieces = jnp.concatenate(_split3(cs * LOG2E), axis=1)
        ones_q = jnp.where((lane >= 3) & (lane < 6), 1.0, 0.0)
        ones_k = jnp.where(lane < 3, 1.0, 0.0)
        zq = _dot_nt(xn, wa_ref[0:D, :]) * Q_SCALE
        zk = _dot_nt(xn, wa_ref[D:2 * D, :])
        zv = _dot_nt(xn, wa_ref[2 * D:3 * D, :])
        ex_q = _dot(pieces, sq_ref[...])
        ex_k = _dot(pieces, sk_ref[...])
        for hd in range(H):
            head = slice(hd * DH, (hd + 1) * DH)
            lo, hi = hd * AUG, hd * AUG + DH
            qa_ref[:, lo:hi] = zq[:, head].astype(BF16)
            qa_ref[:, hi:hi + DH] = (ex_q[:, head] + ones_q).astype(BF16)
            ka_ref[:, lo:hi] = zk[:, head].astype(BF16)
            ka_ref[:, hi:hi + DH] = (ex_k[:, head] + ones_k).astype(BF16)
            va_ref[:, lo:hi] = zv[:, head].astype(BF16)
            va_ref[:, hi:hi + DH] = ones_k.astype(BF16)
            vt_ref[lo:hi, :] = jnp.transpose(zv[:, head]).astype(BF16)
            vt_ref[hi:hi + DH, :] = jnp.where(lax.broadcasted_iota(jnp.int32, (DH, tm), 0) < 3, 1.0, 0.0).astype(BF16)

    bf = jax.ShapeDtypeStruct((T, D), BF16)
    aug = jax.ShapeDtypeStruct((T, H * AUG), BF16)
    f32 = jax.ShapeDtypeStruct((T, D), F32)
    sel_spec = _const_spec((3 * LANES, H * LANES))
    return _call(
        body, name="in_proj", grid=(T // tm,),
        in_specs=[_row_spec(tm, D), _const_spec((1, D)), _const_spec((3 * D, D)), _const_spec((LANES, D)),
                  _const_spec((3 * D, D)), _const_spec((1, LANES)), sel_spec, sel_spec],
        out_specs=[_row_spec(tm, D)] + [_row_spec(tm, H * AUG)] * 3 + [_row_spec(tm, D)] * 3 + [_row_spec(tm, LANES)]
        + [pl.BlockSpec((H * AUG, tm), lambda i: (0, i))],
        out_shape=[bf, aug, aug, aug, f32, f32, f32, jax.ShapeDtypeStruct((T, LANES), F32),
                   jax.ShapeDtypeStruct((H * AUG, T), BF16)],
        scratch_shapes=[pltpu.VMEM((tm, LANES), F32), pltpu.VMEM((1, LANES), F32)],
        compiler_params=_cparams(("arbitrary",), VMEM_BIG),
    )(x, pre_gain, w_a, w_f, w_b, b_f_pad, sel_q, sel_k)


def _causal_pairs(n, q_major):
    if q_major:
        pairs = [(qi, ki) for qi in range(n) for ki in range(qi + 1)]
    else:
        pairs = [(ki, qi) for ki in range(n) for qi in range(ki, n)]
    return (jnp.asarray([a for a, _ in pairs], jnp.int32), jnp.asarray([b for _, b in pairs], jnp.int32))


def _attn_fwd(q_aug, k_aug, vt_aug, shards=(), whole=()):
    T = q_aug.shape[0]
    t = TA
    n = T // t
    hp = FWD_HEADS
    heads = range(hp)
    qi_tab, ki_tab = _causal_pairs(n, q_major=True)
    na, nall = len(shards), len(shards) + len(whole)
    n_h, n_j = H // hp, qi_tab.shape[0]

    def body(qi_ref, ki_ref, q_ref, k_ref, vt_ref, *rest):
        srcs, rest = rest[:nall], rest[nall:]
        o_ref, qx_ref = rest[:2]
        dsts, rest = rest[2:2 + nall], rest[2 + nall:]
        m_s, acc_s = rest[:2]
        h = pl.program_id(0)
        j = pl.program_id(1)
        qi = qi_ref[j]
        ki = ki_ref[j]

        if nall:
            gather = _GatherPlan(srcs, dsts, rest[2:], na)
            step = h * n_j + j
            pl.when(step == 0)(gather.send)
            pl.when(step == n_h * n_j // 2)(gather.forward)
            pl.when(step == n_h * n_j - 1)(gather.finish)

        @pl.when(ki == 0)
        def _():
            m_s[...] = jnp.full(m_s.shape, NEG, F32)
            acc_s[...] = jnp.zeros_like(acc_s)

        def step(on_diagonal):
            cols = [slice(a * AUG, (a + 1) * AUG) for a in heads]
            if on_diagonal:
                krow = lax.broadcasted_iota(jnp.int32, (t, t), 0)
                qcol = lax.broadcasted_iota(jnp.int32, (t, t), 1)
            def logits(a):
                st = _dot_nt(k_ref[:, cols[a]], q_ref[:, cols[a]])
                return jnp.where(krow <= qcol, st, NEG) if on_diagonal else st

            st_next = logits(0)
            for a in heads:
                st = st_next
                if a + 1 < hp:
                    st_next = logits(a + 1)
                m_prev = m_s[a]
                m_new = jnp.maximum(m_prev, jnp.max(st, axis=0, keepdims=True))
                pt = jnp.exp2(st - m_new).astype(BF16)
                acc_s[a] = jnp.exp2(m_prev - m_new) * acc_s[a] + _dot(vt_ref[cols[a], :], pt)
                m_s[a] = m_new

        @pl.when(ki < qi)
        def _():
            step(False)

        @pl.when(ki == qi)
        def _():
            step(True)
            piece = lax.broadcasted_iota(jnp.int32, (DH, t), 0)
            for a in heads:
                l = acc_s[a, DH:DH + 1, :]
                ex = jnp.transpose(q_ref[:, a * AUG + DH:(a + 1) * AUG].astype(F32))
                c2 = jnp.sum(jnp.where(piece < 3, ex, 0.0), axis=0, keepdims=True)
                hi, mid, lo = _split3(jnp.broadcast_to(c2 - (m_s[a] + jnp.log(l) * LOG2E), (DH, t)))
                ones = jnp.where((piece >= 3) & (piece < 6), 1.0, 0.0).astype(BF16)
                ex_t = jnp.where(piece == 0, hi, jnp.where(piece == 1, mid, jnp.where(piece == 2, lo, ones)))
                o_ref[:, a * DH:(a + 1) * DH] = jnp.transpose(acc_s[a, :DH, :] / l)
                qx_ref[:, a * DH:(a + 1) * DH] = jnp.transpose(ex_t.astype(F32)).astype(BF16)

    q_spec = pl.BlockSpec((t, hp * AUG), lambda h, j, qi_ref, ki_ref: (qi_ref[j], h))
    k_spec = pl.BlockSpec((t, hp * AUG), lambda h, j, qi_ref, ki_ref: (ki_ref[j], h))
    vt_spec = pl.BlockSpec((hp * AUG, t), lambda h, j, qi_ref, ki_ref: (h, ki_ref[j]))
    out_spec = pl.BlockSpec((t, hp * DH), lambda h, j, qi_ref, ki_ref: (qi_ref[j], h))
    arrs = list(shards) + list(whole)
    grid_spec = pltpu.PrefetchScalarGridSpec(
        num_scalar_prefetch=2, grid=(n_h, n_j),
        in_specs=[q_spec, k_spec, vt_spec] + [HBM_SPEC] * nall, out_specs=[out_spec, out_spec] + [HBM_SPEC] * nall,
        scratch_shapes=[pltpu.VMEM((hp, 1, t), F32), pltpu.VMEM((hp, AUG, t), F32)]
        + (_gather_semaphores(na, nall) if nall else []))
    outs = _call(
        body, name="attn_fwd", grid_spec=grid_spec,
        out_shape=[jax.ShapeDtypeStruct((T, D), F32), jax.ShapeDtypeStruct((T, D), BF16)] + _gather_out_shapes(arrs),
        compiler_params=_cparams(("arbitrary", "arbitrary"), VMEM_BIG),
    )(qi_tab, ki_tab, q_aug, k_aug, vt_aug, *arrs)
    return outs[0], outs[1], _place_own(outs[2:], arrs)


def _lru_gates(xc, wr_ref, br_ref, wi_ref, bi_ref, lam_ref):
    r = _sigmoid(_gate_pre(xc, wr_ref) + br_ref[...])
    ig = _sigmoid(_gate_pre(xc, wi_ref) + bi_ref[...])
    sp = _softplus_neg(lam_ref[...])
    la = (-LRU_C) * r * sp
    a = jnp.exp(la)
    y = -jnp.tanh(la) * (a * a + 1.0)
    return r, ig, sp, a, jnp.sqrt(y), lax.rsqrt(y)


def _branches_fwd(o, g_attn, x_lru, g_lru, gain_a, gain_l, conv_w, conv_b, w_r, b_r, w_i, b_i, lam):
    T = o.shape[0]
    tm = TM

    def body(o_ref, ga_ref, xl_ref, gl_ref, gna_ref, gnl_ref, cw_ref, cb_ref, wr_ref, br_ref, wi_ref, bi_ref,
             lam_ref, ycat_ref, xc_ref, h_ref, halo_s, hc_s):
        @pl.when(pl.program_id(0) == 0)
        def _():
            halo_s[...] = jnp.zeros_like(halo_s)
            hc_s[...] = jnp.zeros_like(hc_s)

        ov = o_ref[...]
        ga = ga_ref[...]
        ya = ov * _rstd(ov) * gna_ref[...] * (ga * _sigmoid(ga))
        ycat_ref[:, :D] = ya.astype(BF16)

        xl = xl_ref[...]
        halo = halo_s[...]
        xc = xl * cw_ref[3:4, :] + cb_ref[...]
        for j in range(3):
            xc = xc + _shift_down(xl, 3 - j, halo) * cw_ref[j:j + 1, :]
        halo_s[...] = xl_ref[tm - SUBLANES:tm, :]
        xc_ref[...] = xc

        _, ig, _, a, sq, _ = _lru_gates(xc, wr_ref, br_ref, wi_ref, bi_ref, lam_ref)
        u = sq * (ig * xc)
        hc_s[...] = _scan_fwd_into(a, u, hc_s[...], h_ref)
        hh = h_ref[...]

        gl = gl_ref[...]
        yl = hh * _rstd(hh) * gnl_ref[...] * (gl * _sigmoid(gl))
        ycat_ref[:, D:] = yl.astype(BF16)

    vec = _const_spec((1, D))
    wspec = _const_spec((NB, LANES, LANES))
    return _call(
        body, name="branches_fwd", grid=(T // tm,),
        in_specs=[_row_spec(tm, D)] * 4 + [vec, vec, _const_spec((4, D)), vec, wspec, vec, wspec, vec, vec],
        out_specs=[_row_spec(tm, DMIX), _row_spec(tm, D), _row_spec(tm, D)],
        out_shape=[jax.ShapeDtypeStruct((T, DMIX), BF16), jax.ShapeDtypeStruct((T, D), F32),
                   jax.ShapeDtypeStruct((T, D), F32)],
        scratch_shapes=[pltpu.VMEM((SUBLANES, D), F32), pltpu.VMEM((1, D), F32)],
        compiler_params=_cparams(("arbitrary",)),
    )(o, g_attn, x_lru, g_lru, gain_a, gain_l, conv_w, conv_b, w_r, b_r, w_i, b_i, lam)


def _tail(ycat, x, p, tgt, w_out, post_gain, w_ple, ple_gain, w_gate, b_gate):
    T = x.shape[0]
    tm = TM

    def body(ycat_ref, x_ref, p_ref, t_ref, wo_ref, pg_ref, wp_ref, eg_ref, wg_ref, bg_ref,
             dh1_ref, dycat_ref, dmix_ref, h1b_ref, dgp_ref, pb_ref, dpe_ref, acc_ref):
        @pl.when(pl.program_id(0) == 0)
        def _():
            acc_ref[...] = jnp.zeros_like(acc_ref)

        mix = _dot(ycat_ref[...], wo_ref[...])
        rstd_m = _rstd(mix)
        mhat = mix * rstd_m
        h1 = x_ref[...] + mhat * pg_ref[...]
        pb = p_ref[...].astype(BF16)
        pb_ref[...] = pb
        pe = _dot(pb, wp_ref[...])
        rstd_p = _rstd(pe)
        pehat = pe * rstd_p
        e = pehat * eg_ref[...]
        h1b = h1.astype(BF16)
        h1b_ref[...] = h1b
        gate = _sigmoid(_dot(h1b, wg_ref[...]) + bg_ref[...])
        diff = (h1 + gate * e) - t_ref[...]

        dy = diff * (1.0 / D)
        de = dy * gate
        dgp = (dy * e) * gate * (1.0 - gate)
        dgpb = dgp.astype(BF16)
        dgp_ref[...] = dgpb
        dh1 = dy + _dot_nt(dgpb, wg_ref[...])
        dh1_ref[...] = dh1
        dpe_ref[...] = _rms_bwd(de * eg_ref[...], pehat, rstd_p).astype(BF16)
        dmix = _rms_bwd(dh1 * pg_ref[...], mhat, rstd_m).astype(BF16)
        dmix_ref[...] = dmix
        dycat_ref[...] = _dot_nt(dmix, wo_ref[...])

        acc_ref[0:1, :] += jnp.sum(dh1 * mhat, axis=0, keepdims=True)
        acc_ref[1:2, :] += jnp.sum(de * pehat, axis=0, keepdims=True)
        acc_ref[2:3, :] += jnp.sum(dgp, axis=0, keepdims=True)
        acc_ref[3:4, :] += jnp.sum(diff * diff, axis=0, keepdims=True) * (0.5 / D)

    vec = _const_spec((1, D))
    bf = jax.ShapeDtypeStruct((T, D), BF16)
    return _call(
        body, name="tail", grid=(T // tm,),
        in_specs=[_row_spec(tm, DMIX), _row_spec(tm, D), _row_spec(tm, DPLE), _row_spec(tm, D),
                  _const_spec((DMIX, D)), vec, _const_spec((DPLE, D)), vec, _const_spec((D, D)), vec],
        out_specs=[_row_spec(tm, D), _row_spec(tm, DMIX), _row_spec(tm, D), _row_spec(tm, D), _row_spec(tm, D),
                   _row_spec(tm, DPLE), _row_spec(tm, D), _const_spec((SUBLANES, D))],
        out_shape=[jax.ShapeDtypeStruct((T, D), F32), jax.ShapeDtypeStruct((T, DMIX), F32), bf, bf, bf,
                   jax.ShapeDtypeStruct((T, DPLE), BF16), bf, jax.ShapeDtypeStruct((SUBLANES, D), F32)],
        compiler_params=_cparams(("arbitrary",), VMEM_BIG),
    )(ycat, x, p, tgt, w_out, post_gain, w_ple, ple_gain, w_gate, b_gate)


def _pair_copies(srcs, gots, send_sems, recv_sems):
    x, y, c = _position()
    copies = []
    for a, (src, got) in enumerate(zip(srcs, gots)):
        half = src.shape[1] // 2
        rows = pl.ds(pl.multiple_of((1 - c) * half, SUBLANES), half)
        copies.append(pltpu.make_async_remote_copy(
            src_ref=src.at[:, rows, :], dst_ref=got, send_sem=send_sems.at[a], recv_sem=recv_sems.at[a],
            device_id=(x, y, 1 - c), device_id_type=MESH))
    return copies


def _branches_bwd(dycat, o, g_attn, h, g_lru, gain_a, gain_l, ycat, dmix, h1b, dgp, pb, dpe):
    T = o.shape[0]
    tm = TM
    nt = T // tm
    n_gate = nt // 2
    n_ple = max(nt // 8, 1)
    br_out, br_gate, br_ple = DMIX // nt, D // n_gate, DPLE // n_ple

    def body(dy_ref, o_ref, ga_ref, h_ref, gl_ref, gna_ref, gnl_ref, yc_ref, dmix_ref, h1_ref, dgp_ref, pb_ref,
             dpe_ref, do_ref, dga_ref, dgl_ref, dh_ref, acc_ref, gwo_ref, gwg_ref, gwp_ref):
        i = pl.program_id(0)

        @pl.when(i == 0)
        def _():
            acc_ref[...] = jnp.zeros_like(acc_ref)

        gwo_ref[...] = _dot_tn(yc_ref[...], dmix_ref[...])

        @pl.when(i < n_gate)
        def _():
            gwg_ref[...] = _dot_tn(h1_ref[...], dgp_ref[...])

        @pl.when((i >= n_gate) & (i < n_gate + n_ple))
        def _():
            gwp_ref[...] = _dot_tn(pb_ref[...], dpe_ref[...])

        def branch(val, g, gain, dyv):
            rstd = _rstd(val)
            vhat = val * rstd
            sig = _sigmoid(g)
            dn = dyv * (g * sig)
            dg = dyv * (vhat * gain) * (sig * (1.0 + g * (1.0 - sig)))
            dgain = jnp.sum(dn * vhat, axis=0, keepdims=True)
            return _rms_bwd(dn * gain, vhat, rstd), dg, dgain

        ov = o_ref[...]
        do, dga, dgain_a = branch(ov, ga_ref[...], gna_ref[...], dy_ref[:, :D])
        dga_ref[...] = dga.astype(BF16)
        prod = do * ov
        for hd in range(H):
            head = slice(hd * DH, (hd + 1) * DH)
            do_ref[:, hd * AUG:hd * AUG + DH] = do[:, head].astype(BF16)
            do_ref[:, hd * AUG + DH:(hd + 1) * AUG] = _extras(-jnp.sum(prod[:, head], axis=1, keepdims=True), None)

        dh, dgl, dgain_l = branch(h_ref[...], gl_ref[...], gnl_ref[...], dy_ref[:, D:])
        dh_ref[...] = dh
        dgl_ref[...] = dgl.astype(BF16)
        acc_ref[0:1, :] += dgain_a
        acc_ref[1:2, :] += dgain_l

    vec = _const_spec((1, D))
    bf = jax.ShapeDtypeStruct((T, D), BF16)
    def gate_block(i):
        return jnp.minimum(i, n_gate - 1)

    def ple_block(i):
        return jnp.clip(i - n_gate, 0, n_ple - 1)

    tokens = _weight_spec((T, D))
    return _call(
        body, name="branches_bwd", grid=(nt,),
        in_specs=[_row_spec(tm, DMIX)] + [_row_spec(tm, D)] * 4 + [vec, vec]
        + [pl.BlockSpec((T, br_out), lambda i: (0, i)), tokens,
           pl.BlockSpec((T, br_gate), lambda i: (0, gate_block(i))), tokens,
           pl.BlockSpec((T, br_ple), lambda i: (0, ple_block(i))), tokens],
        out_specs=[_row_spec(tm, H * AUG), _row_spec(tm, D), _row_spec(tm, D), _row_spec(tm, D),
                   _const_spec((SUBLANES, D)),
                   pl.BlockSpec((br_out, D), lambda i: (i, 0)),
                   pl.BlockSpec((br_gate, D), lambda i: (gate_block(i), 0)),
                   pl.BlockSpec((br_ple, D), lambda i: (ple_block(i), 0))],
        out_shape=[jax.ShapeDtypeStruct((T, H * AUG), BF16), bf, bf, jax.ShapeDtypeStruct((T, D), F32),
                   jax.ShapeDtypeStruct((SUBLANES, D), F32), jax.ShapeDtypeStruct((DMIX, D), F32),
                   jax.ShapeDtypeStruct((D, D), F32), jax.ShapeDtypeStruct((DPLE, D), F32)],
        compiler_params=_cparams(("arbitrary",), VMEM_BIG),
    )(dycat, o, g_attn, h, g_lru, gain_a, gain_l, ycat, dmix, h1b, dgp, pb, dpe)


def _lru_bwd(dh, h, xc, x_lru, conv_w, w_r, b_r, w_i, b_i, lam, pair_parts=()):
    T = dh.shape[0]
    tm = TM
    nt = T // tm
    per = tm // SUBLANES
    npair = len(pair_parts)

    def body(dh_ref, h_ref, hprev_ref, xc_ref, xl_ref, cw_ref, wr_ref, br_ref, wi_ref, bi_ref, lam_ref, *rest):
        parts, rest = rest[:npair], rest[npair:]
        dxl_ref, dwr_ref, dwi_ref, acc_ref = rest[:4]
        gots, rest = rest[4:4 + npair], rest[4 + npair:]
        carry_s, dxc_next_s, top_s, dht_s = rest[:4]
        i = pl.program_id(0)

        @pl.when(i == 0)
        def _():
            acc_ref[...] = jnp.zeros_like(acc_ref)
            dwr_ref[...] = jnp.zeros_like(dwr_ref)
            dwi_ref[...] = jnp.zeros_like(dwi_ref)
            carry_s[...] = jnp.zeros_like(carry_s)
            dxc_next_s[...] = jnp.zeros_like(dxc_next_s)
            for cp in _pair_copies(parts, gots, *rest[4:]) if npair else ():
                cp.start()

        if npair:
            @pl.when(i == nt - 1)
            def _():
                for cp in _pair_copies(parts, gots, *rest[4:]):
                    cp.wait()

        inner = jnp.where(i == nt - 1, 0.0, 1.0)
        xc = xc_ref[...]
        r, ig, sp, a, sq, inv_sq = _lru_gates(xc, wr_ref, br_ref, wi_ref, bi_ref, lam_ref)

        row = lax.broadcasted_iota(jnp.int32, (tm, D), 0)
        u = dh_ref[...] + jnp.where(row == tm - 1, carry_s[...], 0.0)
        _scan_bwd_into(pltpu.roll(a, tm - 1, 0), u, dht_s)
        dht = dht_s[...]
        top_s[...] = a[:SUBLANES, :] * dht[:SUBLANES, :]
        carry_s[...] = top_s[0:1, :]

        hprev = hprev_ref[...] * inner
        da = dht * _shift_down(h_ref[...], 1, hprev)
        dig = dht * sq * xc
        dxc = dht * sq * ig
        dsq = dht * ig * xc
        dla = da * a - dsq * (a * a) * inv_sq
        dr = dla * ((-LRU_C) * sp)
        dpr = dr * r * (1.0 - r)
        dpi = dig * ig * (1.0 - ig)
        for n in range(NB):
            blk = slice(n * LANES, (n + 1) * LANES)
            xcb = xc[:, blk].astype(BF16)
            dwr_ref[n] += _dot_tn(xcb, dpr[:, blk].astype(BF16))
            dwi_ref[n] += _dot_tn(xcb, dpi[:, blk].astype(BF16))
        dxc = dxc + _gate_pre_t(dpr, wr_ref) + _gate_pre_t(dpi, wi_ref)

        xl = xl_ref[...]
        nxt = dxc_next_s[...]
        dxl = dxc * cw_ref[3:4, :]
        acc_ref[3:4, :] += jnp.sum(dxc * xl, axis=0, keepdims=True)
        for j in range(3):
            ahead = _shift_up(dxc, 3 - j, nxt)
            dxl = dxl + ahead * cw_ref[j:j + 1, :]
            acc_ref[j:j + 1, :] += jnp.sum(ahead * xl, axis=0, keepdims=True)
        dxc_next_s[...] = dxc[:SUBLANES, :]
        dxl_ref[...] = dxl.astype(BF16)

        acc_ref[4:5, :] += jnp.sum(dxc, axis=0, keepdims=True)
        acc_ref[5:6, :] += jnp.sum(dpr, axis=0, keepdims=True)
        acc_ref[6:7, :] += jnp.sum(dpi, axis=0, keepdims=True)
        acc_ref[7:8, :] += jnp.sum(dla * ((-LRU_C) * r), axis=0, keepdims=True)

        @pl.when(i == nt - 1)
        def _():
            lam_v = lam_ref[...]
            acc_ref[7:8, :] = acc_ref[7:8, :] * (-_sigmoid(-lam_v))

    rev = pl.BlockSpec((tm, D), lambda i: (nt - 1 - i, 0))
    prev8 = pl.BlockSpec((SUBLANES, D), lambda i: (jnp.maximum((nt - 1 - i) * per - 1, 0), 0))
    vec = _const_spec((1, D))
    wspec = _const_spec((NB, LANES, LANES))
    bf = jax.ShapeDtypeStruct((T, D), BF16)
    halves = [jax.ShapeDtypeStruct((s.shape[0], s.shape[1] // 2, s.shape[2]), s.dtype) for s in pair_parts]
    outs = _call(
        body, name="lru_bwd", grid=(nt,),
        in_specs=[rev, rev, prev8, rev, rev, _const_spec((4, D)), wspec, vec, wspec, vec, vec] + [HBM_SPEC] * npair,
        out_specs=[rev, wspec, wspec, _const_spec((SUBLANES, D))] + [HBM_SPEC] * npair,
        out_shape=[bf, jax.ShapeDtypeStruct((NB, LANES, LANES), F32), jax.ShapeDtypeStruct((NB, LANES, LANES), F32),
                   jax.ShapeDtypeStruct((SUBLANES, D), F32)] + halves,
        scratch_shapes=[pltpu.VMEM((1, D), F32), pltpu.VMEM((SUBLANES, D), F32), pltpu.VMEM((SUBLANES, D), F32),
                        pltpu.VMEM((tm, D), F32)]
        + ([pltpu.SemaphoreType.DMA((npair,)), pltpu.SemaphoreType.DMA((npair,))] if npair else []),
        compiler_params=_cparams(("arbitrary",)),
    )(dh, h, h, xc, x_lru, conv_w, w_r, b_r, w_i, b_i, lam, *pair_parts)
    return (*outs[:4], list(outs[4:]))


def _chip_copies(srcs, dsts, send_sems, recv_sems):
    x, y, c = _position()
    chip = 2 * x + y
    na = len(srcs)
    return [pltpu.make_async_remote_copy(
        src_ref=srcs[a].at[2 * px + py], dst_ref=dsts[a].at[chip], send_sem=send_sems.at[j * na + a],
        recv_sem=recv_sems.at[j * na + a], device_id=(px, py, c), device_id_type=MESH)
        for j, (px, py) in enumerate(_other_chips(x, y)) for a in range(na)]


def _attn_bwd(q_aug, qx, k_aug, v_aug, do_aug, exchange=()):
    T = q_aug.shape[0]
    t = TA
    n = T // t
    hp = BWD_HEADS
    heads = range(hp)
    scale = DH ** -0.5
    ki_tab, qi_tab = _causal_pairs(n, q_major=False)
    last = ki_tab.shape[0] - 1
    ne = len(exchange)
    n_h = H // hp

    def body(ki_ref, qi_ref, q_ref, qx_ref, k_ref, v_ref, do_ref, *rest):
        sent, rest = rest[:ne], rest[ne:]
        dq_ref, dk_ref, dv_ref, dc_ref = rest[:4]
        received, rest = rest[4:4 + ne], rest[4 + ne:]
        dq_s, dk_s, dv_s = rest[:3]
        j = pl.program_id(1)
        ki = ki_ref[j]
        qi = qi_ref[j]

        if ne:
            first_step = (pl.program_id(0) == 0) & (j == 0)
            last_step = (pl.program_id(0) == n_h - 1) & (j == last)

            @pl.when(first_step)
            def _():
                for cp in _chip_copies(sent, received, *rest[3:]):
                    cp.start()

            @pl.when(last_step)
            def _():
                for cp in _chip_copies(sent, received, *rest[3:]):
                    cp.wait()

        @pl.when(j == 0)
        def _():
            dq_s[...] = jnp.zeros_like(dq_s)

        @pl.when(qi == ki)
        def _():
            dk_s[...] = jnp.zeros_like(dk_s)
            dv_s[...] = jnp.zeros_like(dv_s)

        def step(on_diagonal):
            cols = [slice(a * AUG, (a + 1) * AUG) for a in heads]
            qb = [jnp.concatenate([q_ref[:, a * AUG:a * AUG + DH], qx_ref[:, a * DH:(a + 1) * DH]], axis=1)
                  for a in heads]
            if on_diagonal:
                krow = lax.broadcasted_iota(jnp.int32, (t, t), 0)
                qcol = lax.broadcasted_iota(jnp.int32, (t, t), 1)

            def scores(a):
                st = _dot_nt(k_ref[:, cols[a]], qb[a])
                dpd = _dot_nt(v_ref[:, cols[a]], do_ref[:, cols[a]])
                return (jnp.where(krow <= qcol, st, NEG) if on_diagonal else st), dpd

            off = pl.multiple_of(qi * t, t)
            ahead = scores(0)
            for a in heads:
                st, dpd = ahead
                if a + 1 < hp:
                    ahead = scores(a + 1)
                pt = jnp.exp2(st)
                dsb = (pt * dpd).astype(BF16)
                dv_s[a] += _dot(pt.astype(BF16), do_ref[:, a * AUG:a * AUG + DH])
                dk_s[a] += _dot(dsb, qb[a])
                dq_s[a, pl.ds(off, t), :] += _dot_tn(dsb, k_ref[:, cols[a]])

        @pl.when(qi > ki)
        def _():
            step(False)

        @pl.when(qi == ki)
        def _():
            step(True)

        @pl.when(qi == n - 1)
        def _():
            rows = pl.ds(pl.multiple_of(ki * t, t), t)
            for a in heads:
                dk_ref[:, a * DH:(a + 1) * DH] = (dk_s[a, :, :DH] * LN2).astype(BF16)
                dv_ref[:, a * DH:(a + 1) * DH] = dv_s[a].astype(BF16)
                dc_ref[a, rows, :] = jnp.broadcast_to(-dk_s[a, :, DH + 3:DH + 4], (t, LANES))

        @pl.when(j == last)
        def _():
            for a in heads:
                dq_ref[:, a * DH:(a + 1) * DH] = (dq_s[a, :, :DH] * scale).astype(BF16)
                dc_ref[a] = dc_ref[a] + jnp.broadcast_to(dq_s[a, :, DH:DH + 1], (T, LANES))

    qside = pl.BlockSpec((t, hp * AUG), lambda h, j, ki_ref, qi_ref: (qi_ref[j], h))
    qxside = pl.BlockSpec((t, hp * DH), lambda h, j, ki_ref, qi_ref: (qi_ref[j], h))
    kside = pl.BlockSpec((t, hp * AUG), lambda h, j, ki_ref, qi_ref: (ki_ref[j], h))
    kout = pl.BlockSpec((t, hp * DH), lambda h, j, ki_ref, qi_ref: (ki_ref[j], h))
    bf = jax.ShapeDtypeStruct((T, D), BF16)
    sums = jax.ShapeDtypeStruct((H, T, LANES), F32)
    grid_spec = pltpu.PrefetchScalarGridSpec(
        num_scalar_prefetch=2, grid=(n_h, ki_tab.shape[0]),
        in_specs=[qside, qxside, kside, kside, qside] + [HBM_SPEC] * ne,
        out_specs=[pl.BlockSpec((T, hp * DH), lambda h, j, ki_ref, qi_ref: (0, h)), kout, kout,
                   pl.BlockSpec((hp, T, LANES), lambda h, j, ki_ref, qi_ref: (h, 0, 0))] + [HBM_SPEC] * ne,
        scratch_shapes=[pltpu.VMEM((hp, T, AUG), F32), pltpu.VMEM((hp, t, AUG), F32), pltpu.VMEM((hp, t, DH), F32)]
        + ([pltpu.SemaphoreType.DMA((3 * ne,)), pltpu.SemaphoreType.DMA((3 * ne,))] if ne else []))
    outs = _call(
        body, name="attn_bwd", grid_spec=grid_spec,
        out_shape=[bf, bf, bf, sums] + [jax.ShapeDtypeStruct(s.shape, s.dtype) for s in exchange],
        compiler_params=_cparams(("arbitrary", "arbitrary"), VMEM_BIG),
    )(ki_tab, qi_tab, q_aug, qx, k_aug, v_aug, do_aug, *exchange)
    return (*outs[:4], list(outs[4:]))


def _fgate_bwd(dc_heads, flb):
    T = flb.shape[0]
    tm = TM
    nt = T // tm

    def body(dch_ref, flb_ref, dfl_ref, acc_ref, carry, top_s):
        @pl.when(pl.program_id(0) == 0)
        def _():
            carry[...] = jnp.zeros_like(carry)
            acc_ref[...] = jnp.zeros_like(acc_ref)

        flb = flb_ref[...]
        lane = lax.broadcasted_iota(jnp.int32, flb.shape, 1)
        dc = jnp.zeros(flb.shape, F32)
        for hd in range(H):
            dc = dc + jnp.where(lane == hd, dch_ref[hd], 0.0)
        r = lax.broadcasted_iota(jnp.int32, (tm, tm), 0)
        c = lax.broadcasted_iota(jnp.int32, (tm, tm), 1)
        dls = _dot_exact((c >= r).astype(F32), dc) + carry[...]
        top_s[...] = dls[:SUBLANES, :]
        carry[...] = top_s[0:1, :]
        dfl = jnp.where(lane < H, dls * _sigmoid(-flb), 0.0)
        dfl_ref[...] = dfl.astype(BF16)
        acc_ref[0:1, :] += jnp.sum(dfl, axis=0, keepdims=True)

    rev = pl.BlockSpec((tm, LANES), lambda i: (nt - 1 - i, 0))
    return _call(
        body, name="fgate_bwd", grid=(nt,),
        in_specs=[pl.BlockSpec((H, tm, LANES), lambda i: (0, nt - 1 - i, 0)), rev],
        out_specs=[rev, _const_spec((SUBLANES, LANES))],
        out_shape=[jax.ShapeDtypeStruct((T, LANES), BF16), jax.ShapeDtypeStruct((SUBLANES, LANES), F32)],
        scratch_shapes=[pltpu.VMEM((1, LANES), F32), pltpu.VMEM((SUBLANES, LANES), F32)],
        compiler_params=_cparams(("arbitrary",)),
    )(dc_heads, flb)


def _dx(dz, dfl, w_a, w_f, w_b, x, pre_gain, dh1, exchange=()):
    T = x.shape[0]
    tm = TM
    nt = T // tm
    ne = len(exchange)

    def body(*refs):
        dz_refs = refs[:6]
        dfl_ref, wa_ref, wf_ref, wb_ref, x_ref, g_ref, dh1_ref = refs[6:13]
        sent = refs[13:13 + ne]
        gx_ref, acc_ref = refs[13 + ne:15 + ne]
        received, sems = refs[15 + ne:15 + 2 * ne], refs[15 + 2 * ne:]

        @pl.when(pl.program_id(0) == 0)
        def _():
            acc_ref[...] = jnp.zeros_like(acc_ref)
            for cp in _chip_copies(sent, received, *sems) if ne else ():
                cp.start()

        if ne:
            @pl.when(pl.program_id(0) == nt - 1)
            def _():
                for cp in _chip_copies(sent, received, *sems):
                    cp.wait()

        dxn = _dot(dfl_ref[...], wf_ref[...])
        for s in range(3):
            dxn = dxn + _dot(dz_refs[s][...], wa_ref[s * D:(s + 1) * D, :])
            dxn = dxn + _dot(dz_refs[3 + s][...], wb_ref[s * D:(s + 1) * D, :])
        xv = x_ref[...]
        rstd = _rstd(xv)
        xhat = xv * rstd
        gx_ref[...] = dh1_ref[...] + _rms_bwd(dxn * g_ref[...], xhat, rstd)
        acc_ref[0:1, :] += jnp.sum(dxn * xhat, axis=0, keepdims=True)

    outs = _call(
        body, name="dx", grid=(nt,),
        in_specs=[_row_spec(tm, D)] * 6 + [_row_spec(tm, LANES), _weight_spec((3 * D, D)), _weight_spec((LANES, D)),
                                           _weight_spec((3 * D, D)), _row_spec(tm, D), _const_spec((1, D)),
                                           _row_spec(tm, D)] + [HBM_SPEC] * ne,
        out_specs=[_row_spec(tm, D), _const_spec((SUBLANES, D))] + [HBM_SPEC] * ne,
        out_shape=[jax.ShapeDtypeStruct((T, D), F32), jax.ShapeDtypeStruct((SUBLANES, D), F32)]
        + [jax.ShapeDtypeStruct(s.shape, s.dtype) for s in exchange],
        scratch_shapes=[pltpu.SemaphoreType.DMA((3 * ne,)), pltpu.SemaphoreType.DMA((3 * ne,))] if ne else [],
        compiler_params=_cparams(("arbitrary",), VMEM_BIG),
    )(*dz, dfl, w_a, w_f, w_b, x, pre_gain, dh1, *exchange)
    return outs[0], outs[1], list(outs[2:])


GRAD_ROWS = D_IN + SUBLANES


def _dw_in_segments(dz_a, dz_b, xn, buf, pair, bt):
    T = xn.shape[0]
    nt = T // bt
    first, second = [(2 * pair + k) * D + (H if 2 * pair + k >= 3 else 0) for k in (0, 1)]
    step8 = (second - first) // SUBLANES

    def body(*refs):
        dza_ref, dzb_ref, xn_ref, o_ref = refs[0], refs[1], refs[2], refs[-1]

        @pl.when(pl.program_id(1) == 0)
        def _():
            o_ref[...] = jnp.zeros_like(o_ref)

        @pl.when(pl.program_id(0) == 0)
        def _():
            o_ref[...] += _dot_tn(dza_ref[...], xn_ref[...])

        @pl.when(pl.program_id(0) == 1)
        def _():
            o_ref[...] += _dot_tn(dzb_ref[...], xn_ref[...])

    spec_a = pl.BlockSpec((bt, D), lambda s, t: (jnp.where(s == 0, t, nt - 1), 0))
    spec_b = pl.BlockSpec((bt, D), lambda s, t: (jnp.where(s == 1, t, 0), 0))
    return _call(
        body, name="dw_in_%d" % pair, grid=(2, nt),
        in_specs=[spec_a, spec_b, pl.BlockSpec((bt, D), lambda s, t: (t, 0))]
        + ([] if buf is None else [pl.BlockSpec(memory_space=pl.ANY)]),
        out_specs=pl.BlockSpec((pl.Element(D), pl.Element(D)),
                               lambda s, t: ((first // SUBLANES + s * step8) * SUBLANES, 0)),
        out_shape=jax.ShapeDtypeStruct((GRAD_ROWS, D), F32),
        input_output_aliases={} if buf is None else {3: 0},
        compiler_params=_cparams(("arbitrary", "arbitrary"), VMEM_BIG),
    )(*((dz_a, dz_b, xn) if buf is None else (dz_a, dz_b, xn, buf)))


def _dw_in_t(dz, dfl, xn, bt=DW_TOKENS):
    T = xn.shape[0]
    bt = min(bt, T)
    nt = T // bt
    main = None
    for pair in range(3):
        main = _dw_in_segments(dz[2 * pair], dz[2 * pair + 1], xn, main, pair, bt)

    def f_body(dfl_ref, xn_ref, main_ref, o_ref, acc_s):
        p = pl.program_id(0)
        t = pl.program_id(1)

        @pl.when(t == 0)
        def _():
            acc_s[...] = jnp.zeros_like(acc_s)

        @pl.when(p == 0)
        def _():
            acc_s[...] += _dot_tn(dfl_ref[...], xn_ref[...])

        @pl.when(t == nt - 1)
        def _():
            o_ref[...] = acc_s[:SUBLANES, :]

    fl_block = FL0 // SUBLANES
    end_block = D_IN // SUBLANES
    return _call(
        f_body, name="dw_in_f", grid=(2, nt),
        in_specs=[pl.BlockSpec((bt, LANES), lambda p, t: (t, 0)), pl.BlockSpec((bt, D), lambda p, t: (t, 0)),
                  pl.BlockSpec(memory_space=pl.ANY)],
        out_specs=pl.BlockSpec((SUBLANES, D), lambda p, t: (fl_block + p * (end_block - fl_block), 0)),
        out_shape=jax.ShapeDtypeStruct((GRAD_ROWS, D), F32),
        scratch_shapes=[pltpu.VMEM((LANES, D), F32)],
        input_output_aliases={2: 0},
        compiler_params=_cparams(("arbitrary", "arbitrary")),
    )(dfl, xn, main)


def _matmul_tn(a, b, name, bm=512, bn=1024, bt=DW_TOKENS):
    T, M = a.shape
    N = b.shape[1]
    bm, bn, bt = min(bm, M), min(bn, N), min(bt, T)

    def body(a_ref, b_ref, o_ref):
        @pl.when(pl.program_id(2) == 0)
        def _():
            o_ref[...] = jnp.zeros_like(o_ref)

        o_ref[...] += _dot_tn(a_ref[...], b_ref[...])

    return _call(
        body, name=name, grid=(M // bm, N // bn, T // bt),
        in_specs=[pl.BlockSpec((bt, bm), lambda i, j, t: (t, i)), pl.BlockSpec((bt, bn), lambda i, j, t: (t, j))],
        out_specs=pl.BlockSpec((bm, bn), lambda i, j, t: (i, j)),
        out_shape=jax.ShapeDtypeStruct((M, N), F32),
        compiler_params=_cparams(("parallel", "parallel", "arbitrary")),
    )(a, b)


HBM_SPEC = pl.BlockSpec(memory_space=pltpu.HBM)
VMEM_SPEC = pl.BlockSpec(memory_space=pltpu.VMEM)


def _position():
    return lax.axis_index("x"), lax.axis_index("y"), lax.axis_index("c")


def _other_chips(x, y):
    return [(1 - x, y), (x, 1 - y), (1 - x, 1 - y)]


def _gather_shards(shards, whole):
    na, nw = len(shards), len(whole)
    nall = na + nw

    def body(*refs):
        gather = _GatherPlan(refs[:nall], refs[nall:2 * nall], refs[2 * nall:], na)
        gather.send()
        gather.forward()
        gather.finish()

    arrs = list(shards) + list(whole)
    outs = _call(
        body, name="gather_shards",
        in_specs=[HBM_SPEC] * nall, out_specs=[HBM_SPEC] * nall,
        out_shape=_gather_out_shapes(arrs), scratch_shapes=_gather_semaphores(na, nall),
    )(*arrs)
    return _place_own(outs, arrs)


def _gather_out_shapes(arrs):
    return [jax.ShapeDtypeStruct((N_CHIPS,) + s.shape, s.dtype) for s in arrs]


def _gather_semaphores(na, nall):
    return [pltpu.SemaphoreType.DMA((3 * nall,)), pltpu.SemaphoreType.DMA((3 * nall,)),
            pltpu.SemaphoreType.DMA((3 * na,)), pltpu.SemaphoreType.DMA((3 * na,))]


def _place_own(outs, arrs):
    if not arrs:
        return []
    chip = 2 * lax.axis_index("x") + lax.axis_index("y")
    return [lax.dynamic_update_slice(o, a[None], (chip,) + (0,) * a.ndim) for o, a in zip(outs, arrs)]


class _GatherPlan:
    def __init__(self, srcs, dsts, sems, na):
        ici_send, ici_recv, d2d_send, d2d_recv = sems
        x, y, c = _position()
        chip = 2 * x + y
        nall = len(srcs)

        def half(a, which):
            rows = srcs[a].shape[0] // 2
            return pl.ds(pl.multiple_of(which * rows, BF16_ROWS), rows)

        def copy(src, dst, send, recv, k, to):
            return pltpu.make_async_remote_copy(src_ref=src, dst_ref=dst, send_sem=send.at[k], recv_sem=recv.at[k],
                                                device_id=to, device_id_type=MESH)

        self.first, self.landed, self.passed, self.returned = [], [], [], []
        for j, (px, py) in enumerate(_other_chips(x, y)):
            theirs = 2 * px + py
            for a in range(nall):
                k = j * nall + a
                if a < na:
                    self.first.append(copy(srcs[a].at[half(a, c), :], dsts[a].at[chip, half(a, c), :],
                                           ici_send, ici_recv, k, (px, py, c)))
                    mine = dsts[a].at[theirs, half(a, c), :]
                    other = dsts[a].at[theirs, half(a, 1 - c), :]
                    self.landed.append(copy(mine, mine, ici_send, ici_recv, k, (px, py, c)))
                    self.passed.append(copy(mine, mine, d2d_send, d2d_recv, j * na + a, (x, y, 1 - c)))
                    self.returned.append(copy(other, other, d2d_send, d2d_recv, j * na + a, (x, y, 1 - c)))
                else:
                    self.first.append(copy(srcs[a], dsts[a].at[chip], ici_send, ici_recv, k, (px, py, c)))
                    got = dsts[a].at[theirs]
                    self.landed.append(copy(got, got, ici_send, ici_recv, k, (px, py, c)))
                    self.passed.append(None)

    def send(self):
        for cp in self.first:
            cp.start()

    def forward(self):
        for arrival, fwd in zip(self.landed, self.passed):
            arrival.wait_recv()
            if fwd is not None:
                fwd.start()

    def finish(self):
        for cp in self.returned:
            cp.wait_recv()
        for cp in self.first + [f for f in self.passed if f is not None]:
            cp.wait_send()


W_ROWS = 1568
G_ROWS = 1552
SHARD_ROWS = D_IN // N_CHIPS
WINDOW_STEP = 1536


def _assemble_w_in(cont):
    cb = COL_BLOCK
    half = WINDOW_STEP
    seam = BF16_ROWS

    def body(c_ref, wa_ref, wf_ref, wb_ref):
        x0 = c_ref[0].astype(F32)
        x1, x2, x3 = (pltpu.roll(c_ref[j].astype(F32), 2 * j, 0) for j in (1, 2, 3))
        wa = jnp.concatenate([x0[:half], x0[half:half + seam] + x1[:seam], x1[seam:half]], axis=0)
        wa_ref[...] = wa.astype(BF16)

        fl = x1[half:half + seam] + x2[:seam]
        row = lax.broadcasted_iota(jnp.int32, fl.shape, 0)
        wf_ref[:seam, :] = jnp.where(row < H, fl, 0.0).astype(BF16)
        wf_ref[seam:, :] = jnp.zeros((LANES - seam, cb), BF16)

        mid = x2[half:half + SUBLANES] + x3[:SUBLANES]
        wb = jnp.concatenate([x2[SUBLANES:half], mid, x3[SUBLANES:half + SUBLANES]], axis=0)
        wb_ref[...] = wb.astype(BF16)

    return _call(
        body, name="assemble_w_in", grid=(D // cb,),
        in_specs=[pl.BlockSpec((N_CHIPS, W_ROWS, cb), lambda i: (0, 0, i))],
        out_specs=[pl.BlockSpec((3 * D, cb), lambda i: (0, i)), pl.BlockSpec((LANES, cb), lambda i: (0, i)),
                   pl.BlockSpec((3 * D, cb), lambda i: (0, i))],
        out_shape=[jax.ShapeDtypeStruct((3 * D, D), BF16), jax.ShapeDtypeStruct((LANES, D), BF16),
                   jax.ShapeDtypeStruct((3 * D, D), BF16)],
        compiler_params=_cparams(("parallel",)),
    )(cont)


def _pair_exchange_windows(grad_t):
    half_g = G_ROWS // 2

    def body(g_ref, got, send_sems, recv_sems):
        x, y, c = _position()
        copies = []
        for j in range(N_CHIPS):
            rows = pl.ds(pl.multiple_of(j * WINDOW_STEP + (1 - c) * half_g, SUBLANES), half_g)
            copies.append(pltpu.make_async_remote_copy(
                src_ref=g_ref.at[rows, :], dst_ref=got.at[j], send_sem=send_sems.at[j], recv_sem=recv_sems.at[j],
                device_id=(x, y, 1 - c), device_id_type=MESH))
        for cp in copies:
            cp.start()
        for cp in copies:
            cp.wait()

    return _call(
        body, name="pair_exchange_w_in",
        in_specs=[HBM_SPEC], out_specs=HBM_SPEC,
        out_shape=jax.ShapeDtypeStruct((N_CHIPS, half_g, D), F32),
        scratch_shapes=[pltpu.SemaphoreType.DMA((N_CHIPS,)), pltpu.SemaphoreType.DMA((N_CHIPS,))],
    )(grad_t)


def _pair_sum(parts, gots, c):
    na = len(parts)

    def body(c_ref, *refs):
        for a in range(na):
            refs[2 * na + a][...] = (refs[a][...] + refs[na + a][...]).astype(BF16)

    mine = [pl.BlockSpec(g.shape, lambda i, c_ref: (0, c_ref[0], 0)) for g in gots]
    whole = [pl.BlockSpec(g.shape, lambda i, c_ref: (0, 0, 0)) for g in gots]
    grid_spec = pltpu.PrefetchScalarGridSpec(
        num_scalar_prefetch=1, grid=(1,), in_specs=mine + whole, out_specs=whole)
    return _call(
        body, name="pair_sum", grid_spec=grid_spec,
        out_shape=[jax.ShapeDtypeStruct(g.shape, BF16) for g in gots],
        compiler_params=_cparams(("arbitrary",), VMEM_BIG),
    )(c.reshape(1), *parts, *gots)


def _pair_sum_windows(grad_t, got, c):
    _, half, C = got.shape
    cb = COL_BLOCK

    def body(c_ref, a_ref, b_ref, o_ref):
        o_ref[0] = (a_ref[...] + b_ref[0]).astype(BF16)

    def mine(j, i, c_ref):
        return ((j * (WINDOW_STEP // SUBLANES) + c_ref[0] * (half // SUBLANES)) * SUBLANES, i * cb)

    spec = pl.BlockSpec((1, half, cb), lambda j, i, c_ref: (j, 0, i))
    grid_spec = pltpu.PrefetchScalarGridSpec(
        num_scalar_prefetch=1, grid=(N_CHIPS, C // cb),
        in_specs=[pl.BlockSpec((pl.Element(half), pl.Element(cb)), mine), spec], out_specs=spec)
    return _call(
        body, name="pair_sum_w_in", grid_spec=grid_spec,
        out_shape=jax.ShapeDtypeStruct((N_CHIPS, half, C), BF16),
        compiler_params=_cparams(("parallel", "parallel")),
    )(c.reshape(1), grad_t, got)


def _chip_sum(own, got, chip, name):
    _, half, C = got.shape
    cb = min(C, COL_BLOCK)

    def body(chip_ref, own_ref, g_ref, o_ref):
        for me in range(N_CHIPS):
            @pl.when(chip_ref[0] == me)
            def _(me=me):
                terms = [own_ref[0] if k == me else g_ref[k] for k in range(N_CHIPS)]
                acc = terms[0].astype(F32) + terms[1].astype(F32)
                acc = acc + terms[2].astype(F32)
                o_ref[...] = acc + terms[3].astype(F32)

    grid_spec = pltpu.PrefetchScalarGridSpec(
        num_scalar_prefetch=1, grid=(C // cb,),
        in_specs=[pl.BlockSpec((1, half, cb), lambda i, chip_ref: (chip_ref[0], 0, i)),
                  pl.BlockSpec((N_CHIPS, half, cb), lambda i, chip_ref: (0, 0, i))],
        out_specs=pl.BlockSpec((half, cb), lambda i, chip_ref: (0, i)))
    return _call(
        body, name=name, grid_spec=grid_spec,
        out_shape=jax.ShapeDtypeStruct((half, C), F32),
        compiler_params=_cparams(("parallel",)),
    )(chip.reshape(1), own, got)


def _final_exchange(halves, g):
    na = len(halves)
    rows = g.shape[0]
    per = rows // N_DEV

    def body(*refs):
        srcs, g_ref = refs[:na], refs[na]
        dsts, out_ref = refs[na + 1:2 * na + 1], refs[2 * na + 1]
        got_ref, s1, r1, s2, r2, swap_send, swap_recv = refs[2 * na + 2:]
        x, y, c = _position()
        swaps = [pltpu.make_async_remote_copy(
            src_ref=srcs[a], dst_ref=dsts[a], send_sem=swap_send.at[a], recv_sem=swap_recv.at[a],
            device_id=(x, y, 1 - c), device_id_type=MESH) for a in range(na)]
        for cp in swaps:
            cp.start()
        me = 4 * x + 2 * y + c
        mine = pl.ds(pl.multiple_of(me * per, SUBLANES), per)
        peers = []
        for j in range(1, N_DEV):
            px = 1 - x if j & 4 else x
            py = 1 - y if j & 2 else y
            pc = 1 - c if j & 1 else c
            peers.append((px, py, pc))

        first = []
        for j, (px, py, pc) in enumerate(peers):
            theirs = pl.ds(pl.multiple_of((4 * px + 2 * py + pc) * per, SUBLANES), per)
            first.append(pltpu.make_async_remote_copy(
                src_ref=g_ref.at[theirs, :], dst_ref=got_ref.at[me], send_sem=s1.at[j], recv_sem=r1.at[j],
                device_id=(px, py, pc), device_id_type=MESH))
        for cp in first:
            cp.start()
        got_ref[me] = g_ref[mine, :]
        for cp in first:
            cp.wait()
        total = got_ref[0]
        for d in range(1, N_DEV):
            total = total + got_ref[d]
        out_ref[mine, :] = total

        second = []
        for j, peer in enumerate(peers):
            second.append(pltpu.make_async_remote_copy(
                src_ref=out_ref.at[mine, :], dst_ref=out_ref.at[mine, :], send_sem=s2.at[j], recv_sem=r2.at[j],
                device_id=peer, device_id_type=MESH))
        for cp in second:
            cp.start()
        for cp in second + swaps:
            cp.wait()

    sems = pltpu.SemaphoreType.DMA((N_DEV - 1,))
    swap_sems = pltpu.SemaphoreType.DMA((na,))
    outs = _call(
        body, name="final_exchange", in_hbm=False,
        in_specs=[HBM_SPEC] * na + [VMEM_SPEC], out_specs=[HBM_SPEC] * na + [VMEM_SPEC],
        out_shape=[jax.ShapeDtypeStruct(s.shape, s.dtype) for s in halves] + [jax.ShapeDtypeStruct(g.shape, F32)],
        scratch_shapes=[pltpu.VMEM((N_DEV, per, LANES), F32), sems, sems, sems, sems, swap_sems, swap_sems],
    )(*halves, g)
    return outs[:na], outs[na]


def _adamw_math(g, w, m, v):
    m2 = ADAM_B1 * m + (1.0 - ADAM_B1) * g
    v2 = ADAM_B2 * v + (1.0 - ADAM_B2) * (g * g)
    m_hat = m2 / (1.0 - ADAM_B1 ** ADAM_STEP)
    v_hat = v2 / (1.0 - ADAM_B2 ** ADAM_STEP)
    delta = (-ADAM_LR) * (m_hat / (jnp.sqrt(v_hat) + ADAM_EPS) + ADAM_WD * w)
    return delta, m2, v2


ADAMW_BLOCK_BYTES = 1 << 20


def _adamw_big(g, w, m, v, name):
    R, C = g.shape
    bc = min(C, max(LANES, ADAMW_BLOCK_BYTES // (4 * R) // LANES * LANES))

    def body(g_ref, w_ref, m_ref, v_ref, d_ref, m2_ref, v2_ref):
        d_ref[...], m2_ref[...], v2_ref[...] = _adamw_math(g_ref[...], w_ref[...], m_ref[...], v_ref[...])

    spec = pl.BlockSpec((R, bc), lambda j: (0, j))
    out = jax.ShapeDtypeStruct((R, C), F32)
    return _call(
        body, name=name, grid=(C // bc,),
        in_specs=[spec] * 4, out_specs=[spec] * 3, out_shape=[out] * 3,
        compiler_params=_cparams(("parallel",)),
    )(g, w, m, v)


def _adamw_small(gs, ws, ms, vs):
    n = len(gs)

    def body(*refs):
        for a in range(n):
            g_ref, w_ref, m_ref, v_ref = (refs[k * n + a] for k in range(4))
            d_ref, m2_ref, v2_ref = (refs[(4 + k) * n + a] for k in range(3))
            d_ref[...], m2_ref[...], v2_ref[...] = _adamw_math(g_ref[...], w_ref[...], m_ref[...], v_ref[...])

    outs = [jax.ShapeDtypeStruct(w.shape, F32) for w in ws]
    specs = [_const_spec(w.shape) for w in ws]
    return _call(
        body, name="adamw_small", grid=(1,),
        in_specs=specs * 4, out_specs=specs * 3, out_shape=outs * 3,
    )(*gs, *ws, *ms, *vs)


def _late_weights(st_out, st_ple, st_gate, st_conv):
    return st_out.reshape(DMIX, D), _from_chip_cols(st_ple), st_gate.reshape(D, D), _from_chip_cols(st_conv)


def _local_step(x, p, tgt, w_a, w_f, w_b, late, b_f, pre_gain, post_gain, conv_b,
                w_rgate, b_rgate, w_igate, b_igate, lam, gain_a, gain_l, ple_gain, b_gate,
                gather_late=False, early_reduce=None, w_in_reduce=None):
    b_f_pad = jnp.pad(b_f, ((0, 0), (0, LANES - H)))
    w_r = w_rgate.astype(BF16)
    w_i = w_igate.astype(BF16)

    xn, q_aug, k_aug, v_aug, g_attn, x_lru, g_lru, flb, vt_aug = _in_proj(x, pre_gain, w_a, w_f, w_b, b_f_pad)
    if gather_late:
        o, qx, stacks = _attn_fwd(q_aug, k_aug, vt_aug, late[:3], late[3:])
        late = _late_weights(*stacks)
    else:
        o, qx, _ = _attn_fwd(q_aug, k_aug, vt_aug)
    w_out_b, w_ple_b, w_gate_b, conv_w = late
    ycat, xc, h = _branches_fwd(o, g_attn, x_lru, g_lru, gain_a, gain_l, conv_w, conv_b, w_r, b_rgate, w_i, b_igate,
                                lam)
    dh1, dycat, dmix, h1b, dgp, pb, dpe, acc_t = _tail(ycat, x, p, tgt, w_out_b, post_gain, w_ple_b, ple_gain,
                                                       w_gate_b, b_gate)
    do_aug, dg_attn, dg_lru, dh, acc_b, gw_out, gw_gate, gw_ple = _branches_bwd(
        dycat, o, g_attn, h, g_lru, gain_a, gain_l, ycat, dmix, h1b, dgp, pb, dpe)
    late_grads = [gw_out, gw_ple, gw_gate]
    if early_reduce is None:
        dx_lru, gw_r, gw_i, acc_l, _ = _lru_bwd(dh, h, xc, x_lru, conv_w, w_r, b_rgate, w_i, b_igate, lam)
    else:
        parts = [gw_out.reshape(N_CHIPS, DMIX // N_CHIPS, D), _by_chip_cols(gw_ple),
                 gw_gate.reshape(N_CHIPS, D // N_CHIPS, D)]
        dx_lru, gw_r, gw_i, acc_l, got = _lru_bwd(dh, h, xc, x_lru, conv_w, w_r, b_rgate, w_i, b_igate, lam, parts)
        sent = _pair_sum(parts, got, early_reduce)
    if early_reduce is None:
        dq, dk, dv, dc_heads, _ = _attn_bwd(q_aug, qx, k_aug, v_aug, do_aug)
    else:
        dq, dk, dv, dc_heads, received = _attn_bwd(q_aug, qx, k_aug, v_aug, do_aug, sent)
        late_grads = list(zip(sent, received))
    dfl, acc_f = _fgate_bwd(dc_heads, flb)
    dz = (dq, dk, dv, dg_attn, dx_lru, dg_lru)
    grad_t = _dw_in_t(dz, dfl, xn)
    if w_in_reduce is None:
        grad_x, acc_x, _ = _dx(dz, dfl, w_a, w_f, w_b, x, pre_gain, dh1)
    else:
        sent = w_in_reduce(grad_t)
        grad_x, acc_x, (received,) = _dx(dz, dfl, w_a, w_f, w_b, x, pre_gain, dh1, [sent])
        grad_t = (sent, received)

    grads = dict(
        w_in_t=grad_t,
        w_out=late_grads[0],
        w_ple=late_grads[1],
        w_ple_gate=late_grads[2],
        w_rgate=gw_r,
        w_igate=gw_i,
        b_f=acc_f[0:1, :H],
        pre_gain=acc_x[0:1],
        post_gain=acc_t[0:1],
        conv_w=acc_l[0:4],
        conv_b=acc_l[4:5],
        b_rgate=acc_l[5:6],
        b_igate=acc_l[6:7],
        lru_lambda=acc_l[7:8],
        attn_out_gain=acc_b[0:1],
        lru_out_gain=acc_b[1:2],
        ple_gain=acc_t[1:2],
        b_ple_gate=acc_t[2:3],
    )
    loss = jnp.sum(acc_t[3])
    return loss, grad_x, grads


SMALL_ROWS = ["b_f", "pre_gain", "post_gain", "conv_w", "conv_b", "b_rgate", "b_igate", "lru_lambda",
              "attn_out_gain", "lru_out_gain", "ple_gain", "b_ple_gate"]
WEIGHTS = ["w_in", "b_f", "pre_gain", "post_gain", "conv_w", "conv_b", "w_rgate", "b_rgate", "w_igate", "b_igate",
           "lru_lambda", "attn_out_gain", "lru_out_gain", "w_out", "w_ple", "ple_gain", "w_ple_gate", "b_ple_gate"]
SHARDED = ["w_in", "w_out", "w_ple", "w_ple_gate"]


def _by_chip_cols(g):
    r, cols = g.shape
    return g.reshape(r, N_CHIPS, cols // N_CHIPS).transpose(1, 0, 2)


def _from_chip_cols(s):
    n, r, cols = s.shape
    return s.transpose(1, 0, 2).reshape(r, n * cols)


def kernel(x, p, w_in, b_f, pre_gain, post_gain, conv_w, conv_b, w_rgate, b_rgate, w_igate, b_igate, lru_lambda, attn_out_gain, lru_out_gain, w_out, w_ple, ple_gain, w_ple_gate, b_ple_gate, loss_target, m_w_in, m_b_f, m_pre_gain, m_post_gain, m_conv_w, m_conv_b, m_w_rgate, m_b_rgate, m_w_igate, m_b_igate, m_lru_lambda, m_attn_out_gain, m_lru_out_gain, m_w_out, m_w_ple, m_ple_gain, m_w_ple_gate, m_b_ple_gate, v_w_in, v_b_f, v_pre_gain, v_post_gain, v_conv_w, v_conv_b, v_w_rgate, v_b_rgate, v_w_igate, v_b_igate, v_lru_lambda, v_attn_out_gain, v_lru_out_gain, v_w_out, v_w_ple, v_ple_gain, v_w_ple_gate, v_b_ple_gate):
    w = dict(w_in=w_in, b_f=b_f, pre_gain=pre_gain, post_gain=post_gain, conv_w=conv_w, conv_b=conv_b,
             w_rgate=w_rgate, b_rgate=b_rgate, w_igate=w_igate, b_igate=b_igate, lru_lambda=lru_lambda,
             attn_out_gain=attn_out_gain, lru_out_gain=lru_out_gain, w_out=w_out, w_ple=w_ple, ple_gain=ple_gain,
             w_ple_gate=w_ple_gate, b_ple_gate=b_ple_gate)
    m = dict(w_in=m_w_in, b_f=m_b_f, pre_gain=m_pre_gain, post_gain=m_post_gain, conv_w=m_conv_w, conv_b=m_conv_b,
             w_rgate=m_w_rgate, b_rgate=m_b_rgate, w_igate=m_w_igate, b_igate=m_b_igate, lru_lambda=m_lru_lambda,
             attn_out_gain=m_attn_out_gain, lru_out_gain=m_lru_out_gain, w_out=m_w_out, w_ple=m_w_ple,
             ple_gain=m_ple_gain, w_ple_gate=m_w_ple_gate, b_ple_gate=m_b_ple_gate)
    v = dict(w_in=v_w_in, b_f=v_b_f, pre_gain=v_pre_gain, post_gain=v_post_gain, conv_w=v_conv_w, conv_b=v_conv_b,
             w_rgate=v_w_rgate, b_rgate=v_b_rgate, w_igate=v_w_igate, b_igate=v_b_igate, lru_lambda=v_lru_lambda,
             attn_out_gain=v_attn_out_gain, lru_out_gain=v_lru_out_gain, w_out=v_w_out, w_ple=v_w_ple,
             ple_gain=v_ple_gain, w_ple_gate=v_w_ple_gate, b_ple_gate=v_b_ple_gate)
    xi, yi, ci = _position()
    chip = 2 * xi + yi

    w_in_t, m_in_t, v_in_t = (jnp.swapaxes(t[0], 0, 1) for t in (w_in, m_w_in, v_w_in))
    window = jnp.pad(w_in_t.astype(BF16), ((0, W_ROWS - SHARD_ROWS), (0, 0)))

    (st_in,) = _gather_shards([window], [])
    w_a, w_f, w_b = _assemble_w_in(st_in)
    late_shards = (w_out[0].astype(BF16), w_ple[0].astype(BF16), w_ple_gate[0].astype(BF16), conv_w[0])

    loss, grad_x, g = _local_step(
        x[0], p[0, 0], loss_target[0], w_a, w_f, w_b, late_shards, b_f, pre_gain, post_gain,
        conv_b, w_rgate[0], b_rgate, w_igate[0], b_igate, lru_lambda, attn_out_gain, lru_out_gain, ple_gain,
        b_ple_gate, gather_late=True, early_reduce=ci,
        w_in_reduce=lambda grad_t: _pair_sum_windows(grad_t, _pair_exchange_windows(grad_t), ci))

    sums = [g["w_in_t"][0]] + [g[n][0] for n in SHARDED[1:]]
    recv = [g["w_in_t"][1]] + [g[n][1] for n in SHARDED[1:]]
    halves = [_chip_sum(sums[a], recv[a], chip, "chip_sum_%d" % a) for a in range(4)]

    rows = [jnp.pad(g["b_f"], ((0, 0), (0, D - H)))] + [g[n] for n in SMALL_ROWS[1:]]
    rows.append(jnp.pad(loss.reshape(1, 1), ((0, 0), (0, D - 1))))
    packed = jnp.concatenate([g["w_rgate"].reshape(NB * LANES, LANES), g["w_igate"].reshape(NB * LANES, LANES),
                              jnp.concatenate(rows, axis=0).reshape(LANES, LANES)], axis=0)
    theirs, summed = _final_exchange(halves, packed)
    full = [jnp.concatenate([jnp.where(ci == 0, a, b), jnp.where(ci == 0, b, a)], axis=0)
            for a, b in zip(halves, theirs)]
    red = dict(zip(SHARDED, full))
    red["w_in"] = lax.dynamic_slice_in_dim(red["w_in"], 2 * chip, SHARD_ROWS, axis=0)
    red["w_rgate"] = summed[:D].reshape(1, NB, LANES, LANES)
    red["w_igate"] = summed[D:2 * D].reshape(1, NB, LANES, LANES)
    vec = summed[2 * D:].reshape(16, D)
    loss = vec[15, 0]
    r0 = 0
    for n in SMALL_ROWS:
        nr = 4 if n == "conv_w" else 1
        red[n] = vec[r0:r0 + nr]
        r0 += nr
    red["b_f"] = red["b_f"][:, :H]
    red["conv_w"] = lax.dynamic_slice_in_dim(red["conv_w"], chip * (D // N_CHIPS), D // N_CHIPS, axis=1)[None]

    delta, new_m, new_v = {}, {}, {}
    outs_in = _adamw_big(red["w_in"], w_in_t, m_in_t, v_in_t, "adamw_w_in")
    delta["w_in"], new_m["w_in"], new_v["w_in"] = (jnp.swapaxes(t, 0, 1)[None] for t in outs_in)
    red["w_in"] = jnp.swapaxes(red["w_in"], 0, 1)[None]
    for n in SHARDED[1:]:
        delta[n], new_m[n], new_v[n] = (t[None] for t in _adamw_big(red[n], w[n][0], m[n][0], v[n][0], "adamw_" + n))
        red[n] = red[n][None]
    small = [n for n in WEIGHTS if n not in SHARDED]
    outs = _adamw_small([red[n] for n in small], [w[n] for n in small], [m[n] for n in small],
                        [v[n] for n in small])
    ns = len(small)
    for a, n in enumerate(small):
        delta[n], new_m[n], new_v[n] = outs[a], outs[ns + a], outs[2 * ns + a]

    return (loss, grad_x[None], *[red[n] for n in WEIGHTS], *[delta[n] for n in WEIGHTS],
            *[new_m[n] for n in WEIGHTS], *[new_v[n] for n in WEIGHTS])
```

```python
import jax
import jax.numpy as jnp
import numpy as np
from jax import lax
from jax.experimental import pallas as pl
from jax.experimental.pallas import tpu as pltpu

F32 = jnp.float32
BF16 = jnp.bfloat16

D = 1024
H = 8
DH = 128
NB = 8
DPLE = 256
DMIX = 2 * D
D_IN = 4 * D + H + 2 * D
FL0 = 3 * D
RMS_EPS = 1e-6
LRU_C = 8.0
NEG = -1e30
LANES = 128
SUBLANES = 8
BF16_ROWS = 16
COL_BLOCK = 256
DW_TOKENS = 2048

ADAM_LR = 0.001
ADAM_B1 = 0.9
ADAM_B2 = 0.999
ADAM_EPS = 1e-08
ADAM_WD = 0.01
ADAM_STEP = 10

TM = 256
TA = 512
FWD_HEADS = 8
BWD_HEADS = 2
VMEM_BIG = 56 * 1024 * 1024
VMEM_MID = 40 * 1024 * 1024

MESH = pl.DeviceIdType.MESH
N_CHIPS = 4
N_DEV = 8


def _call(body, *, out_shape, in_hbm=True, **kwargs):
    if not in_hbm:
        return pl.pallas_call(body, out_shape=out_shape, **kwargs)

    def pin(shape):
        return pltpu.HBM(shape.shape, shape.dtype) if isinstance(shape, jax.ShapeDtypeStruct) else shape

    fn = pl.pallas_call(body, out_shape=jax.tree.map(pin, out_shape), **kwargs)

    def run(*args):
        return fn(*[a if a.dtype == jnp.int32 else pltpu.with_memory_space_constraint(a, pltpu.HBM) for a in args])

    return run


def _cparams(sem, vmem=VMEM_MID):
    return pltpu.CompilerParams(dimension_semantics=sem, vmem_limit_bytes=vmem)


def _sigmoid(x):
    return 0.5 * jnp.tanh(0.5 * x) + 0.5


def _rstd(x):
    return lax.rsqrt(jnp.mean(x * x, axis=-1, keepdims=True) + RMS_EPS)


def _rms_bwd(t, xhat, rstd):
    return rstd * (t - xhat * jnp.mean(t * xhat, axis=-1, keepdims=True))


def _dot(a, b):
    return jnp.dot(a, b, preferred_element_type=F32)


def _dot_nt(a, b):
    return lax.dot_general(a, b, (((1,), (1,)), ((), ())), preferred_element_type=F32)


def _dot_tn(a, b):
    return lax.dot_general(a, b, (((0,), (0,)), ((), ())), preferred_element_type=F32)


def _dot_exact(a, b):
    return jnp.dot(a, b, preferred_element_type=F32, precision=lax.Precision.HIGHEST)


def _shift_down(x, j, halo):
    rolled = pltpu.roll(x, j, 0)
    row = lax.broadcasted_iota(jnp.int32, halo.shape, 0)
    top = jnp.where(row < j, pltpu.roll(halo, j, 0), rolled[:SUBLANES])
    return jnp.concatenate([top, rolled[SUBLANES:]], axis=0)


def _shift_up(x, j, nxt):
    tm = x.shape[0]
    rolled = pltpu.roll(x, tm - j, 0)
    row = lax.broadcasted_iota(jnp.int32, nxt.shape, 0)
    bot = jnp.where(row >= SUBLANES - j, pltpu.roll(nxt, SUBLANES - j, 0), rolled[tm - SUBLANES:])
    return jnp.concatenate([rolled[:tm - SUBLANES], bot], axis=0)


def _scan_fwd_into(a, u, carry, h_ref):
    tm, width = a.shape
    groups = (tm // SUBLANES, SUBLANES, width)
    a, u = a.reshape(groups), u.reshape(groups)
    sub = lax.broadcasted_iota(jnp.int32, groups, 1)
    d = 1
    while d < SUBLANES:
        keep = sub >= d
        a_s = jnp.where(keep, pltpu.roll(a, d, 1), 1.0)
        u_s = jnp.where(keep, pltpu.roll(u, d, 1), 0.0)
        u = u + a * u_s
        a = a * a_s
        d *= 2
    a, u = a.reshape(tm, width), u.reshape(tm, width)
    for g in range(tm // SUBLANES):
        rows = slice(g * SUBLANES, (g + 1) * SUBLANES)
        h_ref[rows, :] = u[rows] + a[rows] * carry
        carry = h_ref[(g + 1) * SUBLANES - 1:(g + 1) * SUBLANES, :]
    return carry


def _scan_bwd_into(b, u, g_ref):
    tm, width = b.shape
    groups = (tm // SUBLANES, SUBLANES, width)
    b, u = b.reshape(groups), u.reshape(groups)
    sub = lax.broadcasted_iota(jnp.int32, groups, 1)
    d = 1
    while d < SUBLANES:
        keep = sub < SUBLANES - d
        b_s = jnp.where(keep, pltpu.roll(b, SUBLANES - d, 1), 1.0)
        u_s = jnp.where(keep, pltpu.roll(u, SUBLANES - d, 1), 0.0)
        u = u + b * u_s
        b = b * b_s
        d *= 2
    b, u = b.reshape(tm, width), u.reshape(tm, width)
    nxt = jnp.zeros((1, width), F32)
    for g in reversed(range(tm // SUBLANES)):
        rows = slice(g * SUBLANES, (g + 1) * SUBLANES)
        g_ref[rows, :] = u[rows] + b[rows] * nxt
        nxt = g_ref[g * SUBLANES:g * SUBLANES + 1, :]


def _gate_pre(xc, w_ref):
    outs = []
    for n in range(NB):
        outs.append(_dot(xc[:, n * LANES:(n + 1) * LANES].astype(BF16), w_ref[n]))
    return jnp.concatenate(outs, axis=1)


def _gate_pre_t(d, w_ref):
    outs = []
    for n in range(NB):
        outs.append(_dot_nt(d[:, n * LANES:(n + 1) * LANES].astype(BF16), w_ref[n]))
    return jnp.concatenate(outs, axis=1)


def _softplus_neg(lam):
    return jnp.maximum(-lam, 0.0) + jnp.log(1.0 + jnp.exp(-jnp.abs(lam)))


def _row_spec(tm, width):
    return pl.BlockSpec((tm, width), lambda i: (i, 0))


def _const_spec(shape):
    nd = len(shape)
    return pl.BlockSpec(shape, lambda *_: (0,) * nd)


def _weight_spec(shape):
    nd = len(shape)
    return pl.BlockSpec(shape, lambda *_: (0,) * nd, pipeline_mode=pl.Buffered(1))


AUG = 2 * DH
LOG2E = 1.4426950408889634
LN2 = 0.6931471805599453
Q_SCALE = DH ** -0.5 * LOG2E


def _split3(x):
    hi = x.astype(BF16)
    r1 = x - hi.astype(F32)
    mid = r1.astype(BF16)
    lo = (r1 - mid.astype(F32)).astype(BF16)
    return hi, mid, lo


def _extras(col, ones_from):
    t = col.shape[0]
    hi, mid, lo = _split3(jnp.broadcast_to(col, (t, LANES)))
    lane = lax.broadcasted_iota(jnp.int32, (t, LANES), 1)
    rest = jnp.zeros((t, LANES), BF16)
    if ones_from is not None:
        rest = jnp.where((lane >= ones_from) & (lane < ones_from + 3), 1.0, 0.0).astype(BF16)
    return jnp.where(lane == 0, hi, jnp.where(lane == 1, mid, jnp.where(lane == 2, lo, rest)))


def _selectors():
    sel_q = np.zeros((3 * LANES, H * LANES), np.float32)
    sel_k = np.zeros((3 * LANES, H * LANES), np.float32)
    for hd in range(H):
        for piece in range(3):
            sel_q[piece * LANES + hd, hd * LANES + piece] = 1.0
            sel_k[piece * LANES + hd, hd * LANES + 3 + piece] = -1.0
    return jnp.asarray(sel_q, BF16), jnp.asarray(sel_k, BF16)


def _in_proj(x, pre_gain, w_a, w_f, w_b, b_f_pad):
    T = x.shape[0]
    tm = TM
    sel_q, sel_k = _selectors()

    def body(x_ref, g_ref, wa_ref, wf_ref, wb_ref, bf_ref, sq_ref, sk_ref,
             xn_ref, qa_ref, ka_ref, va_ref, ga_ref, xl_ref, gl_ref, flb_ref, vt_ref, c_s, carry):
        @pl.when(pl.program_id(0) == 0)
        def _():
            carry[...] = jnp.zeros_like(carry)

        xv = x_ref[...]
        xn = (xv * _rstd(xv) * g_ref[...]).astype(BF16)
        xn_ref[...] = xn
        for s, o_ref in enumerate((ga_ref, xl_ref, gl_ref)):
            o_ref[...] = _dot_nt(xn, wb_ref[s * D:(s + 1) * D, :]).astype(o_ref.dtype)
        flb = _dot_nt(xn, wf_ref[...]) + bf_ref[...]
        flb_ref[...] = flb
        lane = lax.broadcasted_iota(jnp.int32, flb.shape, 1)
        ls = jnp.where(lane < H, jnp.minimum(flb, 0.0) - jnp.log(1.0 + jnp.exp(-jnp.abs(flb))), 0.0)
        r = lax.broadcasted_iota(jnp.int32, (tm, tm), 0)
        c = lax.broadcasted_iota(jnp.int32, (tm, tm), 1)
        cs = _dot_exact((c <= r).astype(F32), ls) + carry[...]
        c_s[...] = cs
        carry[...] = c_s[tm - 1:tm, :]

        pieces = jnp.concatenate(_split3(cs * LOG2E), axis=1)
        ones_q = jnp.where((lane >= 3) & (lane < 6), 1.0, 0.0)
        ones_k = jnp.where(lane < 3, 1.0, 0.0)
        zq = _dot_nt(xn, wa_ref[0:D, :]) * Q_SCALE
        zk = _dot_nt(xn, wa_ref[D:2 * D, :])
        zv = _dot_nt(xn, wa_ref[2 * D:3 * D, :])
        ex_q = _dot(pieces, sq_ref[...])
        ex_k = _dot(pieces, sk_ref[...])
        for hd in range(H):
            head = slice(hd * DH, (hd + 1) * DH)
            lo, hi = hd * AUG, hd * AUG + DH
            qa_ref[:, lo:hi] = zq[:, head].astype(BF16)
            qa_ref[:, hi:hi + DH] = (ex_q[:, head] + ones_q).astype(BF16)
            ka_ref[:, lo:hi] = zk[:, head].astype(BF16)
            ka_ref[:, hi:hi + DH] = (ex_k[:, head] + ones_k).astype(BF16)
            va_ref[:, lo:hi] = zv[:, head].astype(BF16)
            va_ref[:, hi:hi + DH] = ones_k.astype(BF16)
            vt_ref[lo:hi, :] = jnp.transpose(zv[:, head]).astype(BF16)
            vt_ref[hi:hi + DH, :] = jnp.where(lax.broadcasted_iota(jnp.int32, (DH, tm), 0) < 3, 1.0, 0.0).astype(BF16)

    bf = jax.ShapeDtypeStruct((T, D), BF16)
    aug = jax.ShapeDtypeStruct((T, H * AUG), BF16)
    f32 = jax.ShapeDtypeStruct((T, D), F32)
    sel_spec = _const_spec((3 * LANES, H * LANES))
    return _call(
        body, name="in_proj", grid=(T // tm,),
        in_specs=[_row_spec(tm, D), _const_spec((1, D)), _const_spec((3 * D, D)), _const_spec((LANES, D)),
                  _const_spec((3 * D, D)), _const_spec((1, LANES)), sel_spec, sel_spec],
        out_specs=[_row_spec(tm, D)] + [_row_spec(tm, H * AUG)] * 3 + [_row_spec(tm, D)] * 3 + [_row_spec(tm, LANES)]
        + [pl.BlockSpec((H * AUG, tm), lambda i: (0, i))],
        out_shape=[bf, aug, aug, aug, f32, f32, f32, jax.ShapeDtypeStruct((T, LANES), F32),
                   jax.ShapeDtypeStruct((H * AUG, T), BF16)],
        scratch_shapes=[pltpu.VMEM((tm, LANES), F32), pltpu.VMEM((1, LANES), F32)],
        compiler_params=_cparams(("arbitrary",), VMEM_BIG),
    )(x, pre_gain, w_a, w_f, w_b, b_f_pad, sel_q, sel_k)


def _causal_pairs(n, q_major):
    if q_major:
        pairs = [(qi, ki) for qi in range(n) for ki in range(qi + 1)]
    else:
        pairs = [(ki, qi) for ki in range(n) for qi in range(ki, n)]
    return (jnp.asarray([a for a, _ in pairs], jnp.int32), jnp.asarray([b for _, b in pairs], jnp.int32))


def _attn_fwd(q_aug, k_aug, vt_aug, shards=(), whole=()):
    T = q_aug.shape[0]
    t = TA
    n = T // t
    hp = FWD_HEADS
    heads = range(hp)
    qi_tab, ki_tab = _causal_pairs(n, q_major=True)
    na, nall = len(shards), len(shards) + len(whole)
    n_h, n_j = H // hp, qi_tab.shape[0]

    def body(qi_ref, ki_ref, q_ref, k_ref, vt_ref, *rest):
        srcs, rest = rest[:nall], rest[nall:]
        o_ref, qx_ref = rest[:2]
        dsts, rest = rest[2:2 + nall], rest[2 + nall:]
        m_s, acc_s = rest[:2]
        h = pl.program_id(0)
        j = pl.program_id(1)
        qi = qi_ref[j]
        ki = ki_ref[j]

        if nall:
            gather = _GatherPlan(srcs, dsts, rest[2:], na)
            step = h * n_j + j
            pl.when(step == 0)(gather.send)
            pl.when(step == n_h * n_j // 2)(gather.forward)
            pl.when(step == n_h * n_j - 1)(gather.finish)

        @pl.when(ki == 0)
        def _():
            m_s[...] = jnp.full(m_s.shape, NEG, F32)
            acc_s[...] = jnp.zeros_like(acc_s)

        def step(on_diagonal):
            cols = [slice(a * AUG, (a + 1) * AUG) for a in heads]
            if on_diagonal:
                krow = lax.broadcasted_iota(jnp.int32, (t, t), 0)
                qcol = lax.broadcasted_iota(jnp.int32, (t, t), 1)
            def logits(a):
                st = _dot_nt(k_ref[:, cols[a]], q_ref[:, cols[a]])
                return jnp.where(krow <= qcol, st, NEG) if on_diagonal else st

            st_next = logits(0)
            for a in heads:
                st = st_next
                if a + 1 < hp:
                    st_next = logits(a + 1)
                m_prev = m_s[a]
                m_new = jnp.maximum(m_prev, jnp.max(st, axis=0, keepdims=True))
                pt = jnp.exp2(st - m_new).astype(BF16)
                acc_s[a] = jnp.exp2(m_prev - m_new) * acc_s[a] + _dot(vt_ref[cols[a], :], pt)
                m_s[a] = m_new

        @pl.when(ki < qi)
        def _():
            step(False)

        @pl.when(ki == qi)
        def _():
            step(True)
            piece = lax.broadcasted_iota(jnp.int32, (DH, t), 0)
            for a in heads:
                l = acc_s[a, DH:DH + 1, :]
                ex = jnp.transpose(q_ref[:, a * AUG + DH:(a + 1) * AUG].astype(F32))
                c2 = jnp.sum(jnp.where(piece < 3, ex, 0.0), axis=0, keepdims=True)
                hi, mid, lo = _split3(jnp.broadcast_to(c2 - (m_s[a] + jnp.log(l) * LOG2E), (DH, t)))
                ones = jnp.where((piece >= 3) & (piece < 6), 1.0, 0.0).astype(BF16)
                ex_t = jnp.where(piece == 0, hi, jnp.where(piece == 1, mid, jnp.where(piece == 2, lo, ones)))
                o_ref[:, a * DH:(a + 1) * DH] = jnp.transpose(acc_s[a, :DH, :] / l)
                qx_ref[:, a * DH:(a + 1) * DH] = jnp.transpose(ex_t.astype(F32)).astype(BF16)

    q_spec = pl.BlockSpec((t, hp * AUG), lambda h, j, qi_ref, ki_ref: (qi_ref[j], h))
    k_spec = pl.BlockSpec((t, hp * AUG), lambda h, j, qi_ref, ki_ref: (ki_ref[j], h))
    vt_spec = pl.BlockSpec((hp * AUG, t), lambda h, j, qi_ref, ki_ref: (h, ki_ref[j]))
    out_spec = pl.BlockSpec((t, hp * DH), lambda h, j, qi_ref, ki_ref: (qi_ref[j], h))
    arrs = list(shards) + list(whole)
    grid_spec = pltpu.PrefetchScalarGridSpec(
        num_scalar_prefetch=2, grid=(n_h, n_j),
        in_specs=[q_spec, k_spec, vt_spec] + [HBM_SPEC] * nall, out_specs=[out_spec, out_spec] + [HBM_SPEC] * nall,
        scratch_shapes=[pltpu.VMEM((hp, 1, t), F32), pltpu.VMEM((hp, AUG, t), F32)]
        + (_gather_semaphores(na, nall) if nall else []))
    outs = _call(
        body, name="attn_fwd", grid_spec=grid_spec,
        out_shape=[jax.ShapeDtypeStruct((T, D), F32), jax.ShapeDtypeStruct((T, D), BF16)] + _gather_out_shapes(arrs),
        compiler_params=_cparams(("arbitrary", "arbitrary"), VMEM_BIG),
    )(qi_tab, ki_tab, q_aug, k_aug, vt_aug, *arrs)
    return outs[0], outs[1], _place_own(outs[2:], arrs)


def _lru_gates(xc, wr_ref, br_ref, wi_ref, bi_ref, lam_ref):
    r = _sigmoid(_gate_pre(xc, wr_ref) + br_ref[...])
    ig = _sigmoid(_gate_pre(xc, wi_ref) + bi_ref[...])
    sp = _softplus_neg(lam_ref[...])
    la = (-LRU_C) * r * sp
    a = jnp.exp(la)
    y = -jnp.tanh(la) * (a * a + 1.0)
    return r, ig, sp, a, jnp.sqrt(y), lax.rsqrt(y)


def _branches_fwd(o, g_attn, x_lru, g_lru, gain_a, gain_l, conv_w, conv_b, w_r, b_r, w_i, b_i, lam):
    T = o.shape[0]
    tm = TM

    def body(o_ref, ga_ref, xl_ref, gl_ref, gna_ref, gnl_ref, cw_ref, cb_ref, wr_ref, br_ref, wi_ref, bi_ref,
             lam_ref, ycat_ref, xc_ref, h_ref, halo_s, hc_s):
        @pl.when(pl.program_id(0) == 0)
        def _():
            halo_s[...] = jnp.zeros_like(halo_s)
            hc_s[...] = jnp.zeros_like(hc_s)

        ov = o_ref[...]
        ga = ga_ref[...]
        ya = ov * _rstd(ov) * gna_ref[...] * (ga * _sigmoid(ga))
        ycat_ref[:, :D] = ya.astype(BF16)

        xl = xl_ref[...]
        halo = halo_s[...]
        xc = xl * cw_ref[3:4, :] + cb_ref[...]
        for j in range(3):
            xc = xc + _shift_down(xl, 3 - j, halo) * cw_ref[j:j + 1, :]
        halo_s[...] = xl_ref[tm - SUBLANES:tm, :]
        xc_ref[...] = xc

        _, ig, _, a, sq, _ = _lru_gates(xc, wr_ref, br_ref, wi_ref, bi_ref, lam_ref)
        u = sq * (ig * xc)
        hc_s[...] = _scan_fwd_into(a, u, hc_s[...], h_ref)
        hh = h_ref[...]

        gl = gl_ref[...]
        yl = hh * _rstd(hh) * gnl_ref[...] * (gl * _sigmoid(gl))
        ycat_ref[:, D:] = yl.astype(BF16)

    vec = _const_spec((1, D))
    wspec = _const_spec((NB, LANES, LANES))
    return _call(
        body, name="branches_fwd", grid=(T // tm,),
        in_specs=[_row_spec(tm, D)] * 4 + [vec, vec, _const_spec((4, D)), vec, wspec, vec, wspec, vec, vec],
        out_specs=[_row_spec(tm, DMIX), _row_spec(tm, D), _row_spec(tm, D)],
        out_shape=[jax.ShapeDtypeStruct((T, DMIX), BF16), jax.ShapeDtypeStruct((T, D), F32),
                   jax.ShapeDtypeStruct((T, D), F32)],
        scratch_shapes=[pltpu.VMEM((SUBLANES, D), F32), pltpu.VMEM((1, D), F32)],
        compiler_params=_cparams(("arbitrary",)),
    )(o, g_attn, x_lru, g_lru, gain_a, gain_l, conv_w, conv_b, w_r, b_r, w_i, b_i, lam)


def _tail(ycat, x, p, tgt, w_out, post_gain, w_ple, ple_gain, w_gate, b_gate):
    T = x.shape[0]
    tm = TM

    def body(ycat_ref, x_ref, p_ref, t_ref, wo_ref, pg_ref, wp_ref, eg_ref, wg_ref, bg_ref,
             dh1_ref, dycat_ref, dmix_ref, h1b_ref, dgp_ref, pb_ref, dpe_ref, acc_ref):
        @pl.when(pl.program_id(0) == 0)
        def _():
            acc_ref[...] = jnp.zeros_like(acc_ref)

        mix = _dot(ycat_ref[...], wo_ref[...])
        rstd_m = _rstd(mix)
        mhat = mix * rstd_m
        h1 = x_ref[...] + mhat * pg_ref[...]
        pb = p_ref[...].astype(BF16)
        pb_ref[...] = pb
        pe = _dot(pb, wp_ref[...])
        rstd_p = _rstd(pe)
        pehat = pe * rstd_p
        e = pehat * eg_ref[...]
        h1b = h1.astype(BF16)
        h1b_ref[...] = h1b
        gate = _sigmoid(_dot(h1b, wg_ref[...]) + bg_ref[...])
        diff = (h1 + gate * e) - t_ref[...]

        dy = diff * (1.0 / D)
        de = dy * gate
        dgp = (dy * e) * gate * (1.0 - gate)
        dgpb = dgp.astype(BF16)
        dgp_ref[...] = dgpb
        dh1 = dy + _dot_nt(dgpb, wg_ref[...])
        dh1_ref[...] = dh1
        dpe_ref[...] = _rms_bwd(de * eg_ref[...], pehat, rstd_p).astype(BF16)
        dmix = _rms_bwd(dh1 * pg_ref[...], mhat, rstd_m).astype(BF16)
        dmix_ref[...] = dmix
        dycat_ref[...] = _dot_nt(dmix, wo_ref[...])

        acc_ref[0:1, :] += jnp.sum(dh1 * mhat, axis=0, keepdims=True)
        acc_ref[1:2, :] += jnp.sum(de * pehat, axis=0, keepdims=True)
        acc_ref[2:3, :] += jnp.sum(dgp, axis=0, keepdims=True)
        acc_ref[3:4, :] += jnp.sum(diff * diff, axis=0, keepdims=True) * (0.5 / D)

    vec = _const_spec((1, D))
    bf = jax.ShapeDtypeStruct((T, D), BF16)
    return _call(
        body, name="tail", grid=(T // tm,),
        in_specs=[_row_spec(tm, DMIX), _row_spec(tm, D), _row_spec(tm, DPLE), _row_spec(tm, D),
                  _const_spec((DMIX, D)), vec, _const_spec((DPLE, D)), vec, _const_spec((D, D)), vec],
        out_specs=[_row_spec(tm, D), _row_spec(tm, DMIX), _row_spec(tm, D), _row_spec(tm, D), _row_spec(tm, D),
                   _row_spec(tm, DPLE), _row_spec(tm, D), _const_spec((SUBLANES, D))],
        out_shape=[jax.ShapeDtypeStruct((T, D), F32), jax.ShapeDtypeStruct((T, DMIX), F32), bf, bf, bf,
                   jax.ShapeDtypeStruct((T, DPLE), BF16), bf, jax.ShapeDtypeStruct((SUBLANES, D), F32)],
        compiler_params=_cparams(("arbitrary",), VMEM_BIG),
    )(ycat, x, p, tgt, w_out, post_gain, w_ple, ple_gain, w_gate, b_gate)


def _pair_copies(srcs, gots, send_sems, recv_sems):
    x, y, c = _position()
    copies = []
    for a, (src, got) in enumerate(zip(srcs, gots)):
        half = src.shape[1] // 2
        rows = pl.ds(pl.multiple_of((1 - c) * half, SUBLANES), half)
        copies.append(pltpu.make_async_remote_copy(
            src_ref=src.at[:, rows, :], dst_ref=got, send_sem=send_sems.at[a], recv_sem=recv_sems.at[a],
            device_id=(x, y, 1 - c), device_id_type=MESH))
    return copies


def _branches_bwd(dycat, o, g_attn, h, g_lru, gain_a, gain_l, ycat, dmix, h1b, dgp, pb, dpe):
    T = o.shape[0]
    tm = TM
    nt = T // tm
    nb_gate, nb_ple = min(nt, 8), min(nt, 2)
    ns_gate, ns_ple = nt // nb_gate, nt // nb_ple
    br_out, br_gate, br_ple = DMIX // nt, D // nb_gate, DPLE // nb_ple
    tk_gate, tk_ple = T // ns_gate, T // ns_ple

    def body(dy_ref, o_ref, ga_ref, h_ref, gl_ref, gna_ref, gnl_ref, yc_ref, dmix_ref, h1_ref, dgp_ref, pb_ref,
             dpe_ref, do_ref, dga_ref, dgl_ref, dh_ref, acc_ref, gwo_ref, gwg_ref, gwp_ref):
        i = pl.program_id(0)

        @pl.when(i == 0)
        def _():
            acc_ref[...] = jnp.zeros_like(acc_ref)

        def accumulate(out_ref, lhs_ref, rhs_ref, tokens, slices):
            s = i % slices
            part = _dot_tn(lhs_ref[...], rhs_ref[pl.ds(pl.multiple_of(s * tokens, tokens), tokens), :])
            out_ref[...] = part + jnp.where(s == 0, 0.0, out_ref[...])

        gwo_ref[...] = _dot_tn(yc_ref[...], dmix_ref[...])

        def branch(val, g, gain, dyv):
            rstd = _rstd(val)
            vhat = val * rstd
            sig = _sigmoid(g)
            dn = dyv * (g * sig)
            dg = dyv * (vhat * gain) * (sig * (1.0 + g * (1.0 - sig)))
            dgain = jnp.sum(dn * vhat, axis=0, keepdims=True)
            return _rms_bwd(dn * gain, vhat, rstd), dg, dgain

        ov = o_ref[...]
        do, dga, dgain_a = branch(ov, ga_ref[...], gna_ref[...], dy_ref[:, :D])
        dga_ref[...] = dga.astype(BF16)
        prod = do * ov
        for hd in range(H):
            head = slice(hd * DH, (hd + 1) * DH)
            do_ref[:, hd * AUG:hd * AUG + DH] = do[:, head].astype(BF16)
            do_ref[:, hd * AUG + DH:(hd + 1) * AUG] = _extras(-jnp.sum(prod[:, head], axis=1, keepdims=True), None)

        accumulate(gwg_ref, h1_ref, dgp_ref, tk_gate, ns_gate)
        accumulate(gwp_ref, pb_ref, dpe_ref, tk_ple, ns_ple)
        dh, dgl, dgain_l = branch(h_ref[...], gl_ref[...], gnl_ref[...], dy_ref[:, D:])
        dh_ref[...] = dh
        dgl_ref[...] = dgl.astype(BF16)
        acc_ref[0:1, :] += dgain_a
        acc_ref[1:2, :] += dgain_l

    vec = _const_spec((1, D))
    bf = jax.ShapeDtypeStruct((T, D), BF16)
    tokens = _weight_spec((T, D))
    return _call(
        body, name="branches_bwd", grid=(nt,),
        in_specs=[_row_spec(tm, DMIX)] + [_row_spec(tm, D)] * 4 + [vec, vec]
        + [pl.BlockSpec((T, br_out), lambda i: (0, i)), tokens,
           pl.BlockSpec((tk_gate, br_gate), lambda i: (i % ns_gate, i // ns_gate)), tokens,
           pl.BlockSpec((tk_ple, br_ple), lambda i: (i % ns_ple, i // ns_ple)), tokens],
        out_specs=[_row_spec(tm, H * AUG), _row_spec(tm, D), _row_spec(tm, D), _row_spec(tm, D),
                   _const_spec((SUBLANES, D)),
                   pl.BlockSpec((br_out, D), lambda i: (i, 0)),
                   pl.BlockSpec((br_gate, D), lambda i: (i // ns_gate, 0)),
                   pl.BlockSpec((br_ple, D), lambda i: (i // ns_ple, 0))],
        out_shape=[jax.ShapeDtypeStruct((T, H * AUG), BF16), bf, bf, jax.ShapeDtypeStruct((T, D), F32),
                   jax.ShapeDtypeStruct((SUBLANES, D), F32), jax.ShapeDtypeStruct((DMIX, D), F32),
                   jax.ShapeDtypeStruct((D, D), F32), jax.ShapeDtypeStruct((DPLE, D), F32)],
        compiler_params=_cparams(("arbitrary",), VMEM_BIG),
    )(dycat, o, g_attn, h, g_lru, gain_a, gain_l, ycat, dmix, h1b, dgp, pb, dpe)


def _lru_bwd(dh, h, xc, x_lru, conv_w, w_r, b_r, w_i, b_i, lam, pair_parts=()):
    T = dh.shape[0]
    tm = TM
    nt = T // tm
    per = tm // SUBLANES
    npair = len(pair_parts)

    def body(dh_ref, h_ref, hprev_ref, xc_ref, xl_ref, cw_ref, wr_ref, br_ref, wi_ref, bi_ref, lam_ref, *rest):
        parts, rest = rest[:npair], rest[npair:]
        dxl_ref, dwr_ref, dwi_ref, acc_ref = rest[:4]
        gots, rest = rest[4:4 + npair], rest[4 + npair:]
        carry_s, dxc_next_s, top_s, dht_s = rest[:4]
        i = pl.program_id(0)

        @pl.when(i == 0)
        def _():
            acc_ref[...] = jnp.zeros_like(acc_ref)
            dwr_ref[...] = jnp.zeros_like(dwr_ref)
            dwi_ref[...] = jnp.zeros_like(dwi_ref)
            carry_s[...] = jnp.zeros_like(carry_s)
            dxc_next_s[...] = jnp.zeros_like(dxc_next_s)
            for cp in _pair_copies(parts, gots, *rest[4:]) if npair else ():
                cp.start()

        if npair:
            @pl.when(i == nt - 1)
            def _():
                for cp in _pair_copies(parts, gots, *rest[4:]):
                    cp.wait()

        inner = jnp.where(i == nt - 1, 0.0, 1.0)
        xc = xc_ref[...]
        r, ig, sp, a, sq, inv_sq = _lru_gates(xc, wr_ref, br_ref, wi_ref, bi_ref, lam_ref)

        row = lax.broadcasted_iota(jnp.int32, (tm, D), 0)
        u = dh_ref[...] + jnp.where(row == tm - 1, carry_s[...], 0.0)
        _scan_bwd_into(pltpu.roll(a, tm - 1, 0), u, dht_s)
        dht = dht_s[...]
        top_s[...] = a[:SUBLANES, :] * dht[:SUBLANES, :]
        carry_s[...] = top_s[0:1, :]

        hprev = hprev_ref[...] * inner
        da = dht * _shift_down(h_ref[...], 1, hprev)
        dig = dht * sq * xc
        dxc = dht * sq * ig
        dsq = dht * ig * xc
        dla = da * a - dsq * (a * a) * inv_sq
        dr = dla * ((-LRU_C) * sp)
        dpr = dr * r * (1.0 - r)
        dpi = dig * ig * (1.0 - ig)
        for n in range(NB):
            blk = slice(n * LANES, (n + 1) * LANES)
            xcb = xc[:, blk].astype(BF16)
            dwr_ref[n] += _dot_tn(xcb, dpr[:, blk].astype(BF16))
            dwi_ref[n] += _dot_tn(xcb, dpi[:, blk].astype(BF16))
        dxc = dxc + _gate_pre_t(dpr, wr_ref) + _gate_pre_t(dpi, wi_ref)

        xl = xl_ref[...]
        nxt = dxc_next_s[...]
        dxl = dxc * cw_ref[3:4, :]
        acc_ref[3:4, :] += jnp.sum(dxc * xl, axis=0, keepdims=True)
        for j in range(3):
            ahead = _shift_up(dxc, 3 - j, nxt)
            dxl = dxl + ahead * cw_ref[j:j + 1, :]
            acc_ref[j:j + 1, :] += jnp.sum(ahead * xl, axis=0, keepdims=True)
        dxc_next_s[...] = dxc[:SUBLANES, :]
        dxl_ref[...] = dxl.astype(BF16)

        acc_ref[4:5, :] += jnp.sum(dxc, axis=0, keepdims=True)
        acc_ref[5:6, :] += jnp.sum(dpr, axis=0, keepdims=True)
        acc_ref[6:7, :] += jnp.sum(dpi, axis=0, keepdims=True)
        acc_ref[7:8, :] += jnp.sum(dla * ((-LRU_C) * r), axis=0, keepdims=True)

        @pl.when(i == nt - 1)
        def _():
            lam_v = lam_ref[...]
            acc_ref[7:8, :] = acc_ref[7:8, :] * (-_sigmoid(-lam_v))

    rev = pl.BlockSpec((tm, D), lambda i: (nt - 1 - i, 0))
    prev8 = pl.BlockSpec((SUBLANES, D), lambda i: (jnp.maximum((nt - 1 - i) * per - 1, 0), 0))
    vec = _const_spec((1, D))
    wspec = _const_spec((NB, LANES, LANES))
    bf = jax.ShapeDtypeStruct((T, D), BF16)
    halves = [jax.ShapeDtypeStruct((s.shape[0], s.shape[1] // 2, s.shape[2]), s.dtype) for s in pair_parts]
    outs = _call(
        body, name="lru_bwd", grid=(nt,),
        in_specs=[rev, rev, prev8, rev, rev, _const_spec((4, D)), wspec, vec, wspec, vec, vec] + [HBM_SPEC] * npair,
        out_specs=[rev, wspec, wspec, _const_spec((SUBLANES, D))] + [HBM_SPEC] * npair,
        out_shape=[bf, jax.ShapeDtypeStruct((NB, LANES, LANES), F32), jax.ShapeDtypeStruct((NB, LANES, LANES), F32),
                   jax.ShapeDtypeStruct((SUBLANES, D), F32)] + halves,
        scratch_shapes=[pltpu.VMEM((1, D), F32), pltpu.VMEM((SUBLANES, D), F32), pltpu.VMEM((SUBLANES, D), F32),
                        pltpu.VMEM((tm, D), F32)]
        + ([pltpu.SemaphoreType.DMA((npair,)), pltpu.SemaphoreType.DMA((npair,))] if npair else []),
        compiler_params=_cparams(("arbitrary",)),
    )(dh, h, h, xc, x_lru, conv_w, w_r, b_r, w_i, b_i, lam, *pair_parts)
    return (*outs[:4], list(outs[4:]))


def _chip_copies(srcs, dsts, send_sems, recv_sems):
    x, y, c = _position()
    chip = 2 * x + y
    na = len(srcs)
    return [pltpu.make_async_remote_copy(
        src_ref=srcs[a].at[2 * px + py], dst_ref=dsts[a].at[chip], send_sem=send_sems.at[j * na + a],
        recv_sem=recv_sems.at[j * na + a], device_id=(px, py, c), device_id_type=MESH)
        for j, (px, py) in enumerate(_other_chips(x, y)) for a in range(na)]


def _attn_bwd(q_aug, qx, k_aug, v_aug, do_aug, exchange=()):
    T = q_aug.shape[0]
    t = TA
    n = T // t
    hp = BWD_HEADS
    heads = range(hp)
    scale = DH ** -0.5
    ki_tab, qi_tab = _causal_pairs(n, q_major=False)
    last = ki_tab.shape[0] - 1
    ne = len(exchange)
    n_h = H // hp

    def body(ki_ref, qi_ref, q_ref, qx_ref, k_ref, v_ref, do_ref, *rest):
        sent, rest = rest[:ne], rest[ne:]
        dq_ref, dk_ref, dv_ref, dc_ref = rest[:4]
        received, rest = rest[4:4 + ne], rest[4 + ne:]
        dq_s, dk_s, dv_s = rest[:3]
        j = pl.program_id(1)
        ki = ki_ref[j]
        qi = qi_ref[j]

        if ne:
            first_step = (pl.program_id(0) == 0) & (j == 0)
            last_step = (pl.program_id(0) == n_h - 1) & (j == last)

            @pl.when(first_step)
            def _():
                for cp in _chip_copies(sent, received, *rest[3:]):
                    cp.start()

            @pl.when(last_step)
            def _():
                for cp in _chip_copies(sent, received, *rest[3:]):
                    cp.wait()

        @pl.when(j == 0)
        def _():
            dq_s[...] = jnp.zeros_like(dq_s)

        @pl.when(qi == ki)
        def _():
            dk_s[...] = jnp.zeros_like(dk_s)
            dv_s[...] = jnp.zeros_like(dv_s)

        def step(on_diagonal):
            cols = [slice(a * AUG, (a + 1) * AUG) for a in heads]
            qb = [jnp.concatenate([q_ref[:, a * AUG:a * AUG + DH], qx_ref[:, a * DH:(a + 1) * DH]], axis=1)
                  for a in heads]
            if on_diagonal:
                krow = lax.broadcasted_iota(jnp.int32, (t, t), 0)
                qcol = lax.broadcasted_iota(jnp.int32, (t, t), 1)

            def scores(a):
                st = _dot_nt(k_ref[:, cols[a]], qb[a])
                dpd = _dot_nt(v_ref[:, cols[a]], do_ref[:, cols[a]])
                return (jnp.where(krow <= qcol, st, NEG) if on_diagonal else st), dpd

            off = pl.multiple_of(qi * t, t)
            ahead = scores(0)
            for a in heads:
                st, dpd = ahead
                if a + 1 < hp:
                    ahead = scores(a + 1)
                pt = jnp.exp2(st)
                dsb = (pt * dpd).astype(BF16)
                dv_s[a] += _dot(pt.astype(BF16), do_ref[:, a * AUG:a * AUG + DH])
                dk_s[a] += _dot(dsb, qb[a])
                dq_s[a, pl.ds(off, t), :] += _dot_tn(dsb, k_ref[:, cols[a]])

        @pl.when(qi > ki)
        def _():
            step(False)

        @pl.when(qi == ki)
        def _():
            step(True)

        @pl.when(qi == n - 1)
        def _():
            rows = pl.ds(pl.multiple_of(ki * t, t), t)
            for a in heads:
                dk_ref[:, a * DH:(a + 1) * DH] = (dk_s[a, :, :DH] * LN2).astype(BF16)
                dv_ref[:, a * DH:(a + 1) * DH] = dv_s[a].astype(BF16)
                dc_ref[a, rows, :] = jnp.broadcast_to(-dk_s[a, :, DH + 3:DH + 4], (t, LANES))

        @pl.when(j == last)
        def _():
            for a in heads:
                dq_ref[:, a * DH:(a + 1) * DH] = (dq_s[a, :, :DH] * scale).astype(BF16)
                dc_ref[a] = dc_ref[a] + jnp.broadcast_to(dq_s[a, :, DH:DH + 1], (T, LANES))

    qside = pl.BlockSpec((t, hp * AUG), lambda h, j, ki_ref, qi_ref: (qi_ref[j], h))
    qxside = pl.BlockSpec((t, hp * DH), lambda h, j, ki_ref, qi_ref: (qi_ref[j], h))
    kside = pl.BlockSpec((t, hp * AUG), lambda h, j, ki_ref, qi_ref: (ki_ref[j], h))
    kout = pl.BlockSpec((t, hp * DH), lambda h, j, ki_ref, qi_ref: (ki_ref[j], h))
    bf = jax.ShapeDtypeStruct((T, D), BF16)
    sums = jax.ShapeDtypeStruct((H, T, LANES), F32)
    grid_spec = pltpu.PrefetchScalarGridSpec(
        num_scalar_prefetch=2, grid=(n_h, ki_tab.shape[0]),
        in_specs=[qside, qxside, kside, kside, qside] + [HBM_SPEC] * ne,
        out_specs=[pl.BlockSpec((T, hp * DH), lambda h, j, ki_ref, qi_ref: (0, h)), kout, kout,
                   pl.BlockSpec((hp, T, LANES), lambda h, j, ki_ref, qi_ref: (h, 0, 0))] + [HBM_SPEC] * ne,
        scratch_shapes=[pltpu.VMEM((hp, T, AUG), F32), pltpu.VMEM((hp, t, AUG), F32), pltpu.VMEM((hp, t, DH), F32)]
        + ([pltpu.SemaphoreType.DMA((3 * ne,)), pltpu.SemaphoreType.DMA((3 * ne,))] if ne else []))
    outs = _call(
        body, name="attn_bwd", grid_spec=grid_spec,
        out_shape=[bf, bf, bf, sums] + [jax.ShapeDtypeStruct(s.shape, s.dtype) for s in exchange],
        compiler_params=_cparams(("arbitrary", "arbitrary"), VMEM_BIG),
    )(ki_tab, qi_tab, q_aug, qx, k_aug, v_aug, do_aug, *exchange)
    return (*outs[:4], list(outs[4:]))


def _fgate_bwd(dc_heads, flb):
    T = flb.shape[0]
    tm = TM
    nt = T // tm

    def body(dch_ref, flb_ref, dfl_ref, acc_ref, carry, top_s):
        @pl.when(pl.program_id(0) == 0)
        def _():
            carry[...] = jnp.zeros_like(carry)
            acc_ref[...] = jnp.zeros_like(acc_ref)

        flb = flb_ref[...]
        lane = lax.broadcasted_iota(jnp.int32, flb.shape, 1)
        dc = jnp.zeros(flb.shape, F32)
        for hd in range(H):
            dc = dc + jnp.where(lane == hd, dch_ref[hd], 0.0)
        r = lax.broadcasted_iota(jnp.int32, (tm, tm), 0)
        c = lax.broadcasted_iota(jnp.int32, (tm, tm), 1)
        dls = _dot_exact((c >= r).astype(F32), dc) + carry[...]
        top_s[...] = dls[:SUBLANES, :]
        carry[...] = top_s[0:1, :]
        dfl = jnp.where(lane < H, dls * _sigmoid(-flb), 0.0)
        dfl_ref[...] = dfl.astype(BF16)
        acc_ref[0:1, :] += jnp.sum(dfl, axis=0, keepdims=True)

    rev = pl.BlockSpec((tm, LANES), lambda i: (nt - 1 - i, 0))
    return _call(
        body, name="fgate_bwd", grid=(nt,),
        in_specs=[pl.BlockSpec((H, tm, LANES), lambda i: (0, nt - 1 - i, 0)), rev],
        out_specs=[rev, _const_spec((SUBLANES, LANES))],
        out_shape=[jax.ShapeDtypeStruct((T, LANES), BF16), jax.ShapeDtypeStruct((SUBLANES, LANES), F32)],
        scratch_shapes=[pltpu.VMEM((1, LANES), F32), pltpu.VMEM((SUBLANES, LANES), F32)],
        compiler_params=_cparams(("arbitrary",)),
    )(dc_heads, flb)


def _dx(dz, dfl, w_a, w_f, w_b, x, pre_gain, dh1, exchange=()):
    T = x.shape[0]
    tm = TM
    nt = T // tm
    ne = len(exchange)

    def body(*refs):
        dz_refs = refs[:6]
        dfl_ref, wa_ref, wf_ref, wb_ref, x_ref, g_ref, dh1_ref = refs[6:13]
        sent = refs[13:13 + ne]
        gx_ref, acc_ref = refs[13 + ne:15 + ne]
        received, sems = refs[15 + ne:15 + 2 * ne], refs[15 + 2 * ne:]

        @pl.when(pl.program_id(0) == 0)
        def _():
            acc_ref[...] = jnp.zeros_like(acc_ref)
            for cp in _chip_copies(sent, received, *sems) if ne else ():
                cp.start()

        if ne:
            @pl.when(pl.program_id(0) == nt - 1)
            def _():
                for cp in _chip_copies(sent, received, *sems):
                    cp.wait()

        dxn = _dot(dfl_ref[...], wf_ref[...])
        for s in range(3):
            dxn = dxn + _dot(dz_refs[s][...], wa_ref[s * D:(s + 1) * D, :])
            dxn = dxn + _dot(dz_refs[3 + s][...], wb_ref[s * D:(s + 1) * D, :])
        xv = x_ref[...]
        rstd = _rstd(xv)
        xhat = xv * rstd
        gx_ref[...] = dh1_ref[...] + _rms_bwd(dxn * g_ref[...], xhat, rstd)
        acc_ref[0:1, :] += jnp.sum(dxn * xhat, axis=0, keepdims=True)

    outs = _call(
        body, name="dx", grid=(nt,),
        in_specs=[_row_spec(tm, D)] * 6 + [_row_spec(tm, LANES), _weight_spec((3 * D, D)), _weight_spec((LANES, D)),
                                           _weight_spec((3 * D, D)), _row_spec(tm, D), _const_spec((1, D)),
                                           _row_spec(tm, D)] + [HBM_SPEC] * ne,
        out_specs=[_row_spec(tm, D), _const_spec((SUBLANES, D))] + [HBM_SPEC] * ne,
        out_shape=[jax.ShapeDtypeStruct((T, D), F32), jax.ShapeDtypeStruct((SUBLANES, D), F32)]
        + [jax.ShapeDtypeStruct(s.shape, s.dtype) for s in exchange],
        scratch_shapes=[pltpu.SemaphoreType.DMA((3 * ne,)), pltpu.SemaphoreType.DMA((3 * ne,))] if ne else [],
        compiler_params=_cparams(("arbitrary",), VMEM_BIG),
    )(*dz, dfl, w_a, w_f, w_b, x, pre_gain, dh1, *exchange)
    return outs[0], outs[1], list(outs[2:])


GRAD_ROWS = D_IN + SUBLANES


def _dw_in_segments(dz_a, dz_b, xn, buf, pair, bt):
    T = xn.shape[0]
    nt = T // bt
    first, second = [(2 * pair + k) * D + (H if 2 * pair + k >= 3 else 0) for k in (0, 1)]
    step8 = (second - first) // SUBLANES

    def body(*refs):
        dza_ref, dzb_ref, xn_ref, o_ref = refs[0], refs[1], refs[2], refs[-1]

        @pl.when(pl.program_id(1) == 0)
        def _():
            o_ref[...] = jnp.zeros_like(o_ref)

        @pl.when(pl.program_id(0) == 0)
        def _():
            o_ref[...] += _dot_tn(dza_ref[...], xn_ref[...])

        @pl.when(pl.program_id(0) == 1)
        def _():
            o_ref[...] += _dot_tn(dzb_ref[...], xn_ref[...])

    spec_a = pl.BlockSpec((bt, D), lambda s, t: (jnp.where(s == 0, t, nt - 1), 0))
    spec_b = pl.BlockSpec((bt, D), lambda s, t: (jnp.where(s == 1, t, 0), 0))
    return _call(
        body, name="dw_in_%d" % pair, grid=(2, nt),
        in_specs=[spec_a, spec_b, pl.BlockSpec((bt, D), lambda s, t: (t, 0))]
        + ([] if buf is None else [pl.BlockSpec(memory_space=pl.ANY)]),
        out_specs=pl.BlockSpec((pl.Element(D), pl.Element(D)),
                               lambda s, t: ((first // SUBLANES + s * step8) * SUBLANES, 0)),
        out_shape=jax.ShapeDtypeStruct((GRAD_ROWS, D), F32),
        input_output_aliases={} if buf is None else {3: 0},
        compiler_params=_cparams(("arbitrary", "arbitrary"), VMEM_BIG),
    )(*((dz_a, dz_b, xn) if buf is None else (dz_a, dz_b, xn, buf)))


def _dw_in_t(dz, dfl, xn, bt=DW_TOKENS):
    T = xn.shape[0]
    bt = min(bt, T)
    nt = T // bt
    main = None
    for pair in range(3):
        main = _dw_in_segments(dz[2 * pair], dz[2 * pair + 1], xn, main, pair, bt)

    def f_body(dfl_ref, xn_ref, main_ref, o_ref, acc_s):
        p = pl.program_id(0)
        t = pl.program_id(1)

        @pl.when(t == 0)
        def _():
            acc_s[...] = jnp.zeros_like(acc_s)

        @pl.when(p == 0)
        def _():
            acc_s[...] += _dot_tn(dfl_ref[...], xn_ref[...])

        @pl.when(t == nt - 1)
        def _():
            o_ref[...] = acc_s[:SUBLANES, :]

    fl_block = FL0 // SUBLANES
    end_block = D_IN // SUBLANES
    return _call(
        f_body, name="dw_in_f", grid=(2, nt),
        in_specs=[pl.BlockSpec((bt, LANES), lambda p, t: (t, 0)), pl.BlockSpec((bt, D), lambda p, t: (t, 0)),
                  pl.BlockSpec(memory_space=pl.ANY)],
        out_specs=pl.BlockSpec((SUBLANES, D), lambda p, t: (fl_block + p * (end_block - fl_block), 0)),
        out_shape=jax.ShapeDtypeStruct((GRAD_ROWS, D), F32),
        scratch_shapes=[pltpu.VMEM((LANES, D), F32)],
        input_output_aliases={2: 0},
        compiler_params=_cparams(("arbitrary", "arbitrary")),
    )(dfl, xn, main)


HBM_SPEC = pl.BlockSpec(memory_space=pltpu.HBM)
VMEM_SPEC = pl.BlockSpec(memory_space=pltpu.VMEM)


def _position():
    return lax.axis_index("x"), lax.axis_index("y"), lax.axis_index("c")


def _other_chips(x, y):
    return [(1 - x, y), (x, 1 - y), (1 - x, 1 - y)]


def _gather_shards(shards, whole):
    na, nw = len(shards), len(whole)
    nall = na + nw

    def body(*refs):
        gather = _GatherPlan(refs[:nall], refs[nall:2 * nall], refs[2 * nall:], na)
        gather.send()
        gather.forward()
        gather.finish()

    arrs = list(shards) + list(whole)
    outs = _call(
        body, name="gather_shards",
        in_specs=[HBM_SPEC] * nall, out_specs=[HBM_SPEC] * nall,
        out_shape=_gather_out_shapes(arrs), scratch_shapes=_gather_semaphores(na, nall),
    )(*arrs)
    return _place_own(outs, arrs)


def _gather_out_shapes(arrs):
    return [jax.ShapeDtypeStruct((N_CHIPS,) + s.shape, s.dtype) for s in arrs]


def _gather_semaphores(na, nall):
    return [pltpu.SemaphoreType.DMA((3 * nall,)), pltpu.SemaphoreType.DMA((3 * nall,)),
            pltpu.SemaphoreType.DMA((3 * na,)), pltpu.SemaphoreType.DMA((3 * na,))]


def _place_own(outs, arrs):
    if not arrs:
        return []
    chip = 2 * lax.axis_index("x") + lax.axis_index("y")
    return [lax.dynamic_update_slice(o, a[None], (chip,) + (0,) * a.ndim) for o, a in zip(outs, arrs)]


class _GatherPlan:
    def __init__(self, srcs, dsts, sems, na):
        ici_send, ici_recv, d2d_send, d2d_recv = sems
        x, y, c = _position()
        chip = 2 * x + y
        nall = len(srcs)

        def half(a, which):
            rows = srcs[a].shape[0] // 2
            return pl.ds(pl.multiple_of(which * rows, BF16_ROWS), rows)

        def copy(src, dst, send, recv, k, to):
            return pltpu.make_async_remote_copy(src_ref=src, dst_ref=dst, send_sem=send.at[k], recv_sem=recv.at[k],
                                                device_id=to, device_id_type=MESH)

        self.first, self.landed, self.passed, self.returned = [], [], [], []
        for j, (px, py) in enumerate(_other_chips(x, y)):
            theirs = 2 * px + py
            for a in range(nall):
                k = j * nall + a
                if a < na:
                    self.first.append(copy(srcs[a].at[half(a, c), :], dsts[a].at[chip, half(a, c), :],
                                           ici_send, ici_recv, k, (px, py, c)))
                    mine = dsts[a].at[theirs, half(a, c), :]
                    other = dsts[a].at[theirs, half(a, 1 - c), :]
                    self.landed.append(copy(mine, mine, ici_send, ici_recv, k, (px, py, c)))
                    self.passed.append(copy(mine, mine, d2d_send, d2d_recv, j * na + a, (x, y, 1 - c)))
                    self.returned.append(copy(other, other, d2d_send, d2d_recv, j * na + a, (x, y, 1 - c)))
                else:
                    self.first.append(copy(srcs[a], dsts[a].at[chip], ici_send, ici_recv, k, (px, py, c)))
                    got = dsts[a].at[theirs]
                    self.landed.append(copy(got, got, ici_send, ici_recv, k, (px, py, c)))
                    self.passed.append(None)

    def send(self):
        for cp in self.first:
            cp.start()

    def forward(self):
        for arrival, fwd in zip(self.landed, self.passed):
            arrival.wait_recv()
            if fwd is not None:
                fwd.start()

    def finish(self):
        for cp in self.returned:
            cp.wait_recv()
        for cp in self.first + [f for f in self.passed if f is not None]:
            cp.wait_send()


W_ROWS = 1568
G_ROWS = 1552
SHARD_ROWS = D_IN // N_CHIPS
WINDOW_STEP = 1536


def _assemble_w_in(cont):
    cb = COL_BLOCK
    half = WINDOW_STEP
    seam = BF16_ROWS

    def body(c_ref, wa_ref, wf_ref, wb_ref):
        x0 = c_ref[0].astype(F32)
        x1, x2, x3 = (pltpu.roll(c_ref[j].astype(F32), 2 * j, 0) for j in (1, 2, 3))
        wa = jnp.concatenate([x0[:half], x0[half:half + seam] + x1[:seam], x1[seam:half]], axis=0)
        wa_ref[...] = wa.astype(BF16)

        fl = x1[half:half + seam] + x2[:seam]
        row = lax.broadcasted_iota(jnp.int32, fl.shape, 0)
        wf_ref[:seam, :] = jnp.where(row < H, fl, 0.0).astype(BF16)
        wf_ref[seam:, :] = jnp.zeros((LANES - seam, cb), BF16)

        mid = x2[half:half + SUBLANES] + x3[:SUBLANES]
        wb = jnp.concatenate([x2[SUBLANES:half], mid, x3[SUBLANES:half + SUBLANES]], axis=0)
        wb_ref[...] = wb.astype(BF16)

    return _call(
        body, name="assemble_w_in", grid=(D // cb,),
        in_specs=[pl.BlockSpec((N_CHIPS, W_ROWS, cb), lambda i: (0, 0, i))],
        out_specs=[pl.BlockSpec((3 * D, cb), lambda i: (0, i)), pl.BlockSpec((LANES, cb), lambda i: (0, i)),
                   pl.BlockSpec((3 * D, cb), lambda i: (0, i))],
        out_shape=[jax.ShapeDtypeStruct((3 * D, D), BF16), jax.ShapeDtypeStruct((LANES, D), BF16),
                   jax.ShapeDtypeStruct((3 * D, D), BF16)],
        compiler_params=_cparams(("parallel",)),
    )(cont)


def _pair_exchange_windows(grad_t):
    half_g = G_ROWS // 2

    def body(g_ref, got, send_sems, recv_sems):
        x, y, c = _position()
        copies = []
        for j in range(N_CHIPS):
            rows = pl.ds(pl.multiple_of(j * WINDOW_STEP + (1 - c) * half_g, SUBLANES), half_g)
            copies.append(pltpu.make_async_remote_copy(
                src_ref=g_ref.at[rows, :], dst_ref=got.at[j], send_sem=send_sems.at[j], recv_sem=recv_sems.at[j],
                device_id=(x, y, 1 - c), device_id_type=MESH))
        for cp in copies:
            cp.start()
        for cp in copies:
            cp.wait()

    return _call(
        body, name="pair_exchange_w_in",
        in_specs=[HBM_SPEC], out_specs=HBM_SPEC,
        out_shape=jax.ShapeDtypeStruct((N_CHIPS, half_g, D), F32),
        scratch_shapes=[pltpu.SemaphoreType.DMA((N_CHIPS,)), pltpu.SemaphoreType.DMA((N_CHIPS,))],
    )(grad_t)


def _pair_sum(parts, gots, c):
    na = len(parts)

    def body(c_ref, *refs):
        for a in range(na):
            refs[2 * na + a][...] = (refs[a][...] + refs[na + a][...]).astype(BF16)

    mine = [pl.BlockSpec(g.shape, lambda i, c_ref: (0, c_ref[0], 0)) for g in gots]
    whole = [pl.BlockSpec(g.shape, lambda i, c_ref: (0, 0, 0)) for g in gots]
    grid_spec = pltpu.PrefetchScalarGridSpec(
        num_scalar_prefetch=1, grid=(1,), in_specs=mine + whole, out_specs=whole)
    return _call(
        body, name="pair_sum", grid_spec=grid_spec,
        out_shape=[jax.ShapeDtypeStruct(g.shape, BF16) for g in gots],
        compiler_params=_cparams(("arbitrary",), VMEM_BIG),
    )(c.reshape(1), *parts, *gots)


def _pair_sum_windows(grad_t, got, c):
    _, half, C = got.shape
    cb = COL_BLOCK

    def body(c_ref, a_ref, b_ref, o_ref):
        o_ref[0] = (a_ref[...] + b_ref[0]).astype(BF16)

    def mine(j, i, c_ref):
        return ((j * (WINDOW_STEP // SUBLANES) + c_ref[0] * (half // SUBLANES)) * SUBLANES, i * cb)

    spec = pl.BlockSpec((1, half, cb), lambda j, i, c_ref: (j, 0, i))
    grid_spec = pltpu.PrefetchScalarGridSpec(
        num_scalar_prefetch=1, grid=(N_CHIPS, C // cb),
        in_specs=[pl.BlockSpec((pl.Element(half), pl.Element(cb)), mine), spec], out_specs=spec)
    return _call(
        body, name="pair_sum_w_in", grid_spec=grid_spec,
        out_shape=jax.ShapeDtypeStruct((N_CHIPS, half, C), BF16),
        compiler_params=_cparams(("parallel", "parallel")),
    )(c.reshape(1), grad_t, got)


def _chip_sum(own, got, chip, name):
    _, half, C = got.shape
    cb = min(C, COL_BLOCK)

    def body(chip_ref, own_ref, g_ref, o_ref):
        for me in range(N_CHIPS):
            @pl.when(chip_ref[0] == me)
            def _(me=me):
                terms = [own_ref[0] if k == me else g_ref[k] for k in range(N_CHIPS)]
                acc = terms[0].astype(F32) + terms[1].astype(F32)
                acc = acc + terms[2].astype(F32)
                o_ref[...] = acc + terms[3].astype(F32)

    grid_spec = pltpu.PrefetchScalarGridSpec(
        num_scalar_prefetch=1, grid=(C // cb,),
        in_specs=[pl.BlockSpec((1, half, cb), lambda i, chip_ref: (chip_ref[0], 0, i)),
                  pl.BlockSpec((N_CHIPS, half, cb), lambda i, chip_ref: (0, 0, i))],
        out_specs=pl.BlockSpec((half, cb), lambda i, chip_ref: (0, i)))
    return _call(
        body, name=name, grid_spec=grid_spec,
        out_shape=jax.ShapeDtypeStruct((half, C), F32),
        compiler_params=_cparams(("parallel",)),
    )(chip.reshape(1), own, got)


def _final_exchange(halves, g):
    na = len(halves)
    rows = g.shape[0]
    per = rows // N_DEV

    def body(*refs):
        srcs, g_ref = refs[:na], refs[na]
        dsts, out_ref = refs[na + 1:2 * na + 1], refs[2 * na + 1]
        got_ref, s1, r1, s2, r2, swap_send, swap_recv = refs[2 * na + 2:]
        x, y, c = _position()
        swaps = [pltpu.make_async_remote_copy(
            src_ref=srcs[a], dst_ref=dsts[a], send_sem=swap_send.at[a], recv_sem=swap_recv.at[a],
            device_id=(x, y, 1 - c), device_id_type=MESH) for a in range(na)]
        for cp in swaps:
            cp.start()
        me = 4 * x + 2 * y + c
        mine = pl.ds(pl.multiple_of(me * per, SUBLANES), per)
        peers = []
        for j in range(1, N_DEV):
            px = 1 - x if j & 4 else x
            py = 1 - y if j & 2 else y
            pc = 1 - c if j & 1 else c
            peers.append((px, py, pc))

        first = []
        for j, (px, py, pc) in enumerate(peers):
            theirs = pl.ds(pl.multiple_of((4 * px + 2 * py + pc) * per, SUBLANES), per)
            first.append(pltpu.make_async_remote_copy(
                src_ref=g_ref.at[theirs, :], dst_ref=got_ref.at[me], send_sem=s1.at[j], recv_sem=r1.at[j],
                device_id=(px, py, pc), device_id_type=MESH))
        for cp in first:
            cp.start()
        got_ref[me] = g_ref[mine, :]
        for cp in first:
            cp.wait()
        total = got_ref[0]
        for d in range(1, N_DEV):
            total = total + got_ref[d]
        out_ref[mine, :] = total

        second = []
        for j, peer in enumerate(peers):
            second.append(pltpu.make_async_remote_copy(
                src_ref=out_ref.at[mine, :], dst_ref=out_ref.at[mine, :], send_sem=s2.at[j], recv_sem=r2.at[j],
                device_id=peer, device_id_type=MESH))
        for cp in second:
            cp.start()
        for cp in second + swaps:
            cp.wait()

    sems = pltpu.SemaphoreType.DMA((N_DEV - 1,))
    swap_sems = pltpu.SemaphoreType.DMA((na,))
    outs = _call(
        body, name="final_exchange", in_hbm=False,
        in_specs=[HBM_SPEC] * na + [VMEM_SPEC], out_specs=[HBM_SPEC] * na + [VMEM_SPEC],
        out_shape=[jax.ShapeDtypeStruct(s.shape, s.dtype) for s in halves] + [jax.ShapeDtypeStruct(g.shape, F32)],
        scratch_shapes=[pltpu.VMEM((N_DEV, per, LANES), F32), sems, sems, sems, sems, swap_sems, swap_sems],
    )(*halves, g)
    return outs[:na], outs[na]


def _adamw_math(g, w, m, v):
    m2 = ADAM_B1 * m + (1.0 - ADAM_B1) * g
    v2 = ADAM_B2 * v + (1.0 - ADAM_B2) * (g * g)
    m_hat = m2 / (1.0 - ADAM_B1 ** ADAM_STEP)
    v_hat = v2 / (1.0 - ADAM_B2 ** ADAM_STEP)
    delta = (-ADAM_LR) * (m_hat / (jnp.sqrt(v_hat) + ADAM_EPS) + ADAM_WD * w)
    return delta, m2, v2


ADAMW_BLOCK_BYTES = 1 << 20


def _adamw_big(g, w, m, v, name):
    R, C = g.shape
    bc = min(C, max(LANES, ADAMW_BLOCK_BYTES // (4 * R) // LANES * LANES))

    def body(g_ref, w_ref, m_ref, v_ref, d_ref, m2_ref, v2_ref):
        d_ref[...], m2_ref[...], v2_ref[...] = _adamw_math(g_ref[...], w_ref[...], m_ref[...], v_ref[...])

    spec = pl.BlockSpec((R, bc), lambda j: (0, j))
    out = jax.ShapeDtypeStruct((R, C), F32)
    return _call(
        body, name=name, grid=(C // bc,),
        in_specs=[spec] * 4, out_specs=[spec] * 3, out_shape=[out] * 3,
        compiler_params=_cparams(("parallel",)),
    )(g, w, m, v)


def _adamw_small(gs, ws, ms, vs):
    n = len(gs)

    def body(*refs):
        for a in range(n):
            g_ref, w_ref, m_ref, v_ref = (refs[k * n + a] for k in range(4))
            d_ref, m2_ref, v2_ref = (refs[(4 + k) * n + a] for k in range(3))
            d_ref[...], m2_ref[...], v2_ref[...] = _adamw_math(g_ref[...], w_ref[...], m_ref[...], v_ref[...])

    outs = [jax.ShapeDtypeStruct(w.shape, F32) for w in ws]
    specs = [_const_spec(w.shape) for w in ws]
    return _call(
        body, name="adamw_small", grid=(1,),
        in_specs=specs * 4, out_specs=specs * 3, out_shape=outs * 3,
    )(*gs, *ws, *ms, *vs)


def _late_weights(st_out, st_ple, st_gate, st_conv):
    return st_out.reshape(DMIX, D), _from_chip_cols(st_ple), st_gate.reshape(D, D), _from_chip_cols(st_conv)


def _local_step(x, p, tgt, w_a, w_f, w_b, late, b_f, pre_gain, post_gain, conv_b,
                w_rgate, b_rgate, w_igate, b_igate, lam, gain_a, gain_l, ple_gain, b_gate,
                gather_late=False, early_reduce=None, w_in_reduce=None):
    b_f_pad = jnp.pad(b_f, ((0, 0), (0, LANES - H)))
    w_r = w_rgate.astype(BF16)
    w_i = w_igate.astype(BF16)

    xn, q_aug, k_aug, v_aug, g_attn, x_lru, g_lru, flb, vt_aug = _in_proj(x, pre_gain, w_a, w_f, w_b, b_f_pad)
    if gather_late:
        o, qx, stacks = _attn_fwd(q_aug, k_aug, vt_aug, late[:3], late[3:])
        late = _late_weights(*stacks)
    else:
        o, qx, _ = _attn_fwd(q_aug, k_aug, vt_aug)
    w_out_b, w_ple_b, w_gate_b, conv_w = late
    ycat, xc, h = _branches_fwd(o, g_attn, x_lru, g_lru, gain_a, gain_l, conv_w, conv_b, w_r, b_rgate, w_i, b_igate,
                                lam)
    dh1, dycat, dmix, h1b, dgp, pb, dpe, acc_t = _tail(ycat, x, p, tgt, w_out_b, post_gain, w_ple_b, ple_gain,
                                                       w_gate_b, b_gate)
    do_aug, dg_attn, dg_lru, dh, acc_b, gw_out, gw_gate, gw_ple = _branches_bwd(
        dycat, o, g_attn, h, g_lru, gain_a, gain_l, ycat, dmix, h1b, dgp, pb, dpe)
    late_grads = [gw_out, gw_ple, gw_gate]
    if early_reduce is None:
        dx_lru, gw_r, gw_i, acc_l, _ = _lru_bwd(dh, h, xc, x_lru, conv_w, w_r, b_rgate, w_i, b_igate, lam)
    else:
        parts = [gw_out.reshape(N_CHIPS, DMIX // N_CHIPS, D), _by_chip_cols(gw_ple),
                 gw_gate.reshape(N_CHIPS, D // N_CHIPS, D)]
        dx_lru, gw_r, gw_i, acc_l, got = _lru_bwd(dh, h, xc, x_lru, conv_w, w_r, b_rgate, w_i, b_igate, lam, parts)
        sent = _pair_sum(parts, got, early_reduce)
    if early_reduce is None:
        dq, dk, dv, dc_heads, _ = _attn_bwd(q_aug, qx, k_aug, v_aug, do_aug)
    else:
        dq, dk, dv, dc_heads, received = _attn_bwd(q_aug, qx, k_aug, v_aug, do_aug, sent)
        late_grads = list(zip(sent, received))
    dfl, acc_f = _fgate_bwd(dc_heads, flb)
    dz = (dq, dk, dv, dg_attn, dx_lru, dg_lru)
    grad_t = _dw_in_t(dz, dfl, xn)
    if w_in_reduce is None:
        grad_x, acc_x, _ = _dx(dz, dfl, w_a, w_f, w_b, x, pre_gain, dh1)
    else:
        sent = w_in_reduce(grad_t)
        grad_x, acc_x, (received,) = _dx(dz, dfl, w_a, w_f, w_b, x, pre_gain, dh1, [sent])
        grad_t = (sent, received)

    grads = dict(
        w_in_t=grad_t,
        w_out=late_grads[0],
        w_ple=late_grads[1],
        w_ple_gate=late_grads[2],
        w_rgate=gw_r,
        w_igate=gw_i,
        b_f=acc_f[0:1, :H],
        pre_gain=acc_x[0:1],
        post_gain=acc_t[0:1],
        conv_w=acc_l[0:4],
        conv_b=acc_l[4:5],
        b_rgate=acc_l[5:6],
        b_igate=acc_l[6:7],
        lru_lambda=acc_l[7:8],
        attn_out_gain=acc_b[0:1],
        lru_out_gain=acc_b[1:2],
        ple_gain=acc_t[1:2],
        b_ple_gate=acc_t[2:3],
    )
    loss = jnp.sum(acc_t[3])
    return loss, grad_x, grads


SMALL_ROWS = ["b_f", "pre_gain", "post_gain", "conv_w", "conv_b", "b_rgate", "b_igate", "lru_lambda",
              "attn_out_gain", "lru_out_gain", "ple_gain", "b_ple_gate"]
WEIGHTS = ["w_in", "b_f", "pre_gain", "post_gain", "conv_w", "conv_b", "w_rgate", "b_rgate", "w_igate", "b_igate",
           "lru_lambda", "attn_out_gain", "lru_out_gain", "w_out", "w_ple", "ple_gain", "w_ple_gate", "b_ple_gate"]
SHARDED = ["w_in", "w_out", "w_ple", "w_ple_gate"]


def _by_chip_cols(g):
    r, cols = g.shape
    return g.reshape(r, N_CHIPS, cols // N_CHIPS).transpose(1, 0, 2)


def _from_chip_cols(s):
    n, r, cols = s.shape
    return s.transpose(1, 0, 2).reshape(r, n * cols)


def kernel(x, p, w_in, b_f, pre_gain, post_gain, conv_w, conv_b, w_rgate, b_rgate, w_igate, b_igate, lru_lambda, attn_out_gain, lru_out_gain, w_out, w_ple, ple_gain, w_ple_gate, b_ple_gate, loss_target, m_w_in, m_b_f, m_pre_gain, m_post_gain, m_conv_w, m_conv_b, m_w_rgate, m_b_rgate, m_w_igate, m_b_igate, m_lru_lambda, m_attn_out_gain, m_lru_out_gain, m_w_out, m_w_ple, m_ple_gain, m_w_ple_gate, m_b_ple_gate, v_w_in, v_b_f, v_pre_gain, v_post_gain, v_conv_w, v_conv_b, v_w_rgate, v_b_rgate, v_w_igate, v_b_igate, v_lru_lambda, v_attn_out_gain, v_lru_out_gain, v_w_out, v_w_ple, v_ple_gain, v_w_ple_gate, v_b_ple_gate):
    w = dict(w_in=w_in, b_f=b_f, pre_gain=pre_gain, post_gain=post_gain, conv_w=conv_w, conv_b=conv_b,
             w_rgate=w_rgate, b_rgate=b_rgate, w_igate=w_igate, b_igate=b_igate, lru_lambda=lru_lambda,
             attn_out_gain=attn_out_gain, lru_out_gain=lru_out_gain, w_out=w_out, w_ple=w_ple, ple_gain=ple_gain,
             w_ple_gate=w_ple_gate, b_ple_gate=b_ple_gate)
    m = dict(w_in=m_w_in, b_f=m_b_f, pre_gain=m_pre_gain, post_gain=m_post_gain, conv_w=m_conv_w, conv_b=m_conv_b,
             w_rgate=m_w_rgate, b_rgate=m_b_rgate, w_igate=m_w_igate, b_igate=m_b_igate, lru_lambda=m_lru_lambda,
             attn_out_gain=m_attn_out_gain, lru_out_gain=m_lru_out_gain, w_out=m_w_out, w_ple=m_w_ple,
             ple_gain=m_ple_gain, w_ple_gate=m_w_ple_gate, b_ple_gate=m_b_ple_gate)
    v = dict(w_in=v_w_in, b_f=v_b_f, pre_gain=v_pre_gain, post_gain=v_post_gain, conv_w=v_conv_w, conv_b=v_conv_b,
             w_rgate=v_w_rgate, b_rgate=v_b_rgate, w_igate=v_w_igate, b_igate=v_b_igate, lru_lambda=v_lru_lambda,
             attn_out_gain=v_attn_out_gain, lru_out_gain=v_lru_out_gain, w_out=v_w_out, w_ple=v_w_ple,
             ple_gain=v_ple_gain, w_ple_gate=v_w_ple_gate, b_ple_gate=v_b_ple_gate)
    xi, yi, ci = _position()
    chip = 2 * xi + yi

    w_in_t, m_in_t, v_in_t = (jnp.swapaxes(t[0], 0, 1) for t in (w_in, m_w_in, v_w_in))
    window = jnp.pad(w_in_t.astype(BF16), ((0, W_ROWS - SHARD_ROWS), (0, 0)))

    (st_in,) = _gather_shards([window], [])
    w_a, w_f, w_b = _assemble_w_in(st_in)
    late_shards = (w_out[0].astype(BF16), w_ple[0].astype(BF16), w_ple_gate[0].astype(BF16), conv_w[0])

    loss, grad_x, g = _local_step(
        x[0], p[0, 0], loss_target[0], w_a, w_f, w_b, late_shards, b_f, pre_gain, post_gain,
        conv_b, w_rgate[0], b_rgate, w_igate[0], b_igate, lru_lambda, attn_out_gain, lru_out_gain, ple_gain,
        b_ple_gate, gather_late=True, early_reduce=ci,
        w_in_reduce=lambda grad_t: _pair_sum_windows(grad_t, _pair_exchange_windows(grad_t), ci))

    sums = [g["w_in_t"][0]] + [g[n][0] for n in SHARDED[1:]]
    recv = [g["w_in_t"][1]] + [g[n][1] for n in SHARDED[1:]]
    halves = [_chip_sum(sums[a], recv[a], chip, "chip_sum_%d" % a) for a in range(4)]

    rows = [jnp.pad(g["b_f"], ((0, 0), (0, D - H)))] + [g[n] for n in SMALL_ROWS[1:]]
    rows.append(jnp.pad(loss.reshape(1, 1), ((0, 0), (0, D - 1))))
    packed = jnp.concatenate([g["w_rgate"].reshape(NB * LANES, LANES), g["w_igate"].reshape(NB * LANES, LANES),
                              jnp.concatenate(rows, axis=0).reshape(LANES, LANES)], axis=0)
    theirs, summed = _final_exchange(halves, packed)
    full = [jnp.concatenate([jnp.where(ci == 0, a, b), jnp.where(ci == 0, b, a)], axis=0)
            for a, b in zip(halves, theirs)]
    red = dict(zip(SHARDED, full))
    red["w_in"] = lax.dynamic_slice_in_dim(red["w_in"], 2 * chip, SHARD_ROWS, axis=0)
    red["w_rgate"] = summed[:D].reshape(1, NB, LANES, LANES)
    red["w_igate"] = summed[D:2 * D].reshape(1, NB, LANES, LANES)
    vec = summed[2 * D:].reshape(16, D)
    loss = vec[15, 0]
    r0 = 0
    for n in SMALL_ROWS:
        nr = 4 if n == "conv_w" else 1
        red[n] = vec[r0:r0 + nr]
        r0 += nr
    red["b_f"] = red["b_f"][:, :H]
    red["conv_w"] = lax.dynamic_slice_in_dim(red["conv_w"], chip * (D // N_CHIPS), D // N_CHIPS, axis=1)[None]

    delta, new_m, new_v = {}, {}, {}
    outs_in = _adamw_big(red["w_in"], w_in_t, m_in_t, v_in_t, "adamw_w_in")
    delta["w_in"], new_m["w_in"], new_v["w_in"] = (jnp.swapaxes(t, 0, 1)[None] for t in outs_in)
    red["w_in"] = jnp.swapaxes(red["w_in"], 0, 1)[None]
    for n in SHARDED[1:]:
        delta[n], new_m[n], new_v[n] = (t[None] for t in _adamw_big(red[n], w[n][0], m[n][0], v[n][0], "adamw_" + n))
        red[n] = red[n][None]
    small = [n for n in WEIGHTS if n not in SHARDED]
    outs = _adamw_small([red[n] for n in small], [w[n] for n in small], [m[n] for n in small],
                        [v[n] for n in small])
    ns = len(small)
    for a, n in enumerate(small):
        delta[n], new_m[n], new_v[n] = outs[a], outs[ns + a], outs[2 * ns + a]

    return (loss, grad_x[None], *[red[n] for n in WEIGHTS], *[delta[n] for n in WEIGHTS],
            *[new_m[n] for n in WEIGHTS], *[new_v[n] for n in WEIGHTS])
```

```python
import jax
import jax.numpy as jnp
import numpy as np
from jax import lax
from jax.experimental import pallas as pl
from jax.experimental.pallas import tpu as pltpu

F32 = jnp.float32
BF16 = jnp.bfloat16

D = 1024
H = 8
DH = 128
NB = 8
DPLE = 256
DMIX = 2 * D
D_IN = 4 * D + H + 2 * D
FL0 = 3 * D
RMS_EPS = 1e-6
LRU_C = 8.0
NEG = -1e30
LANES = 128
SUBLANES = 8
BF16_ROWS = 16
COL_BLOCK = 256
DW_TOKENS = 2048

ADAM_LR = 0.001
ADAM_B1 = 0.9
ADAM_B2 = 0.999
ADAM_EPS = 1e-08
ADAM_WD = 0.01
ADAM_STEP = 10

TM = 256
TA = 512
FWD_HEADS = 8
BWD_HEADS = 2
VMEM_BIG = 56 * 1024 * 1024
VMEM_MID = 40 * 1024 * 1024

MESH = pl.DeviceIdType.MESH
N_CHIPS = 4
N_DEV = 8


def _call(body, *, out_shape, in_hbm=True, **kwargs):
    if not in_hbm:
        return pl.pallas_call(body, out_shape=out_shape, **kwargs)

    def pin(shape):
        return pltpu.HBM(shape.shape, shape.dtype) if isinstance(shape, jax.ShapeDtypeStruct) else shape

    fn = pl.pallas_call(body, out_shape=jax.tree.map(pin, out_shape), **kwargs)

    def run(*args):
        return fn(*[a if a.dtype == jnp.int32 else pltpu.with_memory_space_constraint(a, pltpu.HBM) for a in args])

    return run


def _cparams(sem, vmem=VMEM_MID):
    return pltpu.CompilerParams(dimension_semantics=sem, vmem_limit_bytes=vmem)


def _sigmoid(x):
    return 0.5 * jnp.tanh(0.5 * x) + 0.5


def _rstd(x):
    return lax.rsqrt(jnp.mean(x * x, axis=-1, keepdims=True) + RMS_EPS)


def _rms_bwd(t, xhat, rstd):
    return rstd * (t - xhat * jnp.mean(t * xhat, axis=-1, keepdims=True))


def _dot(a, b):
    return jnp.dot(a, b, preferred_element_type=F32)


def _dot_nt(a, b):
    return lax.dot_general(a, b, (((1,), (1,)), ((), ())), preferred_element_type=F32)


def _dot_tn(a, b):
    return lax.dot_general(a, b, (((0,), (0,)), ((), ())), preferred_element_type=F32)


def _dot_exact(a, b):
    return jnp.dot(a, b, preferred_element_type=F32, precision=lax.Precision.HIGHEST)


def _shift_down(x, j, halo):
    rolled = pltpu.roll(x, j, 0)
    row = lax.broadcasted_iota(jnp.int32, halo.shape, 0)
    top = jnp.where(row < j, pltpu.roll(halo, j, 0), rolled[:SUBLANES])
    return jnp.concatenate([top, rolled[SUBLANES:]], axis=0)


def _shift_up(x, j, nxt):
    tm = x.shape[0]
    rolled = pltpu.roll(x, tm - j, 0)
    row = lax.broadcasted_iota(jnp.int32, nxt.shape, 0)
    bot = jnp.where(row >= SUBLANES - j, pltpu.roll(nxt, SUBLANES - j, 0), rolled[tm - SUBLANES:])
    return jnp.concatenate([rolled[:tm - SUBLANES], bot], axis=0)


def _scan_fwd_into(a, u, carry, h_ref):
    tm, width = a.shape
    groups = (tm // SUBLANES, SUBLANES, width)
    a, u = a.reshape(groups), u.reshape(groups)
    sub = lax.broadcasted_iota(jnp.int32, groups, 1)
    d = 1
    while d < SUBLANES:
        keep = sub >= d
        a_s = jnp.where(keep, pltpu.roll(a, d, 1), 1.0)
        u_s = jnp.where(keep, pltpu.roll(u, d, 1), 0.0)
        u = u + a * u_s
        a = a * a_s
        d *= 2
    a, u = a.reshape(tm, width), u.reshape(tm, width)
    for g in range(tm // SUBLANES):
        rows = slice(g * SUBLANES, (g + 1) * SUBLANES)
        h_ref[rows, :] = u[rows] + a[rows] * carry
        carry = h_ref[(g + 1) * SUBLANES - 1:(g + 1) * SUBLANES, :]
    return carry


def _scan_bwd_into(b, u, g_ref):
    tm, width = b.shape
    groups = (tm // SUBLANES, SUBLANES, width)
    b, u = b.reshape(groups), u.reshape(groups)
    sub = lax.broadcasted_iota(jnp.int32, groups, 1)
    d = 1
    while d < SUBLANES:
        keep = sub < SUBLANES - d
        b_s = jnp.where(keep, pltpu.roll(b, SUBLANES - d, 1), 1.0)
        u_s = jnp.where(keep, pltpu.roll(u, SUBLANES - d, 1), 0.0)
        u = u + b * u_s
        b = b * b_s
        d *= 2
    b, u = b.reshape(tm, width), u.reshape(tm, width)
    nxt = jnp.zeros((1, width), F32)
    for g in reversed(range(tm // SUBLANES)):
        rows = slice(g * SUBLANES, (g + 1) * SUBLANES)
        g_ref[rows, :] = u[rows] + b[rows] * nxt
        nxt = g_ref[g * SUBLANES:g * SUBLANES + 1, :]


def _gate_pre(xc, w_ref):
    outs = []
    for n in range(NB):
        outs.append(_dot(xc[:, n * LANES:(n + 1) * LANES].astype(BF16), w_ref[n]))
    return jnp.concatenate(outs, axis=1)


def _gate_pre_t(d, w_ref):
    outs = []
    for n in range(NB):
        outs.append(_dot_nt(d[:, n * LANES:(n + 1) * LANES].astype(BF16), w_ref[n]))
    return jnp.concatenate(outs, axis=1)


def _softplus_neg(lam):
    return jnp.maximum(-lam, 0.0) + jnp.log(1.0 + jnp.exp(-jnp.abs(lam)))


def _row_spec(tm, width):
    return pl.BlockSpec((tm, width), lambda i: (i, 0))


def _const_spec(shape):
    nd = len(shape)
    return pl.BlockSpec(shape, lambda *_: (0,) * nd)


def _weight_spec(shape):
    nd = len(shape)
    return pl.BlockSpec(shape, lambda *_: (0,) * nd, pipeline_mode=pl.Buffered(1))


AUG = 2 * DH
LOG2E = 1.4426950408889634
LN2 = 0.6931471805599453
Q_SCALE = DH ** -0.5 * LOG2E


def _split3(x):
    hi = x.astype(BF16)
    r1 = x - hi.astype(F32)
    mid = r1.astype(BF16)
    lo = (r1 - mid.astype(F32)).astype(BF16)
    return hi, mid, lo


def _extras(col, ones_from):
    t = col.shape[0]
    hi, mid, lo = _split3(jnp.broadcast_to(col, (t, LANES)))
    lane = lax.broadcasted_iota(jnp.int32, (t, LANES), 1)
    rest = jnp.zeros((t, LANES), BF16)
    if ones_from is not None:
        rest = jnp.where((lane >= ones_from) & (lane < ones_from + 3), 1.0, 0.0).astype(BF16)
    return jnp.where(lane == 0, hi, jnp.where(lane == 1, mid, jnp.where(lane == 2, lo, rest)))


def _selectors():
    sel_q = np.zeros((3 * LANES, H * LANES), np.float32)
    sel_k = np.zeros((3 * LANES, H * LANES), np.float32)
    for hd in range(H):
        for piece in range(3):
            sel_q[piece * LANES + hd, hd * LANES + piece] = 1.0
            sel_k[piece * LANES + hd, hd * LANES + 3 + piece] = -1.0
    return jnp.asarray(sel_q, BF16), jnp.asarray(sel_k, BF16)


def _in_proj(x, pre_gain, w_a, w_f, w_b, b_f_pad):
    T = x.shape[0]
    tm = TM
    sel_q, sel_k = _selectors()

    def body(x_ref, g_ref, wa_ref, wf_ref, wb_ref, bf_ref, sq_ref, sk_ref,
             xn_ref, qa_ref, ka_ref, va_ref, ga_ref, xl_ref, gl_ref, flb_ref, vt_ref, c_s, carry):
        @pl.when(pl.program_id(0) == 0)
        def _():
            carry[...] = jnp.zeros_like(carry)

        xv = x_ref[...]
        xn = (xv * _rstd(xv) * g_ref[...]).astype(BF16)
        xn_ref[...] = xn
        for s, o_ref in enumerate((ga_ref, xl_ref, gl_ref)):
            o_ref[...] = _dot_nt(xn, wb_ref[s * D:(s + 1) * D, :]).astype(o_ref.dtype)
        flb = _dot_nt(xn, wf_ref[...]) + bf_ref[...]
        flb_ref[...] = flb
        lane = lax.broadcasted_iota(jnp.int32, flb.shape, 1)
        ls = jnp.where(lane < H, jnp.minimum(flb, 0.0) - jnp.log(1.0 + jnp.exp(-jnp.abs(flb))), 0.0)
        r = lax.broadcasted_iota(jnp.int32, (tm, tm), 0)
        c = lax.broadcasted_iota(jnp.int32, (tm, tm), 1)
        cs = _dot_exact((c <= r).astype(F32), ls) + carry[...]
        c_s[...] = cs
        carry[...] = c_s[tm - 1:tm, :]

        pieces = jnp.concatenate(_split3(cs * LOG2E), axis=1)
        ones_q = jnp.where((lane >= 3) & (lane < 6), 1.0, 0.0)
        ones_k = jnp.where(lane < 3, 1.0, 0.0)
        zq = _dot_nt(xn, wa_ref[0:D, :]) * Q_SCALE
        zk = _dot_nt(xn, wa_ref[D:2 * D, :])
        zv = _dot_nt(xn, wa_ref[2 * D:3 * D, :])
        ex_q = _dot(pieces, sq_ref[...])
        ex_k = _dot(pieces, sk_ref[...])
        for hd in range(H):
            head = slice(hd * DH, (hd + 1) * DH)
            lo, hi = hd * AUG, hd * AUG + DH
            qa_ref[:, lo:hi] = zq[:, head].astype(BF16)
            qa_ref[:, hi:hi + DH] = (ex_q[:, head] + ones_q).astype(BF16)
            ka_ref[:, lo:hi] = zk[:, head].astype(BF16)
            ka_ref[:, hi:hi + DH] = (ex_k[:, head] + ones_k).astype(BF16)
            va_ref[:, lo:hi] = zv[:, head].astype(BF16)
            va_ref[:, hi:hi + DH] = ones_k.astype(BF16)
            vt_ref[lo:hi, :] = jnp.transpose(zv[:, head]).astype(BF16)
            vt_ref[hi:hi + DH, :] = jnp.where(lax.broadcasted_iota(jnp.int32, (DH, tm), 0) < 3, 1.0, 0.0).astype(BF16)

    bf = jax.ShapeDtypeStruct((T, D), BF16)
    aug = jax.ShapeDtypeStruct((T, H * AUG), BF16)
    f32 = jax.ShapeDtypeStruct((T, D), F32)
    sel_spec = _const_spec((3 * LANES, H * LANES))
    return _call(
        body, name="in_proj", grid=(T // tm,),
        in_specs=[_row_spec(tm, D), _const_spec((1, D)), _const_spec((3 * D, D)), _const_spec((LANES, D)),
                  _const_spec((3 * D, D)), _const_spec((1, LANES)), sel_spec, sel_spec],
        out_specs=[_row_spec(tm, D)] + [_row_spec(tm, H * AUG)] * 3 + [_row_spec(tm, D)] * 3 + [_row_spec(tm, LANES)]
        + [pl.BlockSpec((H * AUG, tm), lambda i: (0, i))],
        out_shape=[bf, aug, aug, aug, f32, f32, f32, jax.ShapeDtypeStruct((T, LANES), F32),
                   jax.ShapeDtypeStruct((H * AUG, T), BF16)],
        scratch_shapes=[pltpu.VMEM((tm, LANES), F32), pltpu.VMEM((1, LANES), F32)],
        compiler_params=_cparams(("arbitrary",), VMEM_BIG),
    )(x, pre_gain, w_a, w_f, w_b, b_f_pad, sel_q, sel_k)


def _causal_pairs(n, q_major):
    if q_major:
        pairs = [(qi, ki) for qi in range(n) for ki in range(qi + 1)]
    else:
        pairs = [(ki, qi) for ki in range(n) for qi in range(ki, n)]
    return (jnp.asarray([a for a, _ in pairs], jnp.int32), jnp.asarray([b for _, b in pairs], jnp.int32))


def _attn_fwd(q_aug, k_aug, vt_aug, shards=(), whole=()):
    T = q_aug.shape[0]
    t = TA
    n = T // t
    hp = FWD_HEADS
    heads = range(hp)
    qi_tab, ki_tab = _causal_pairs(n, q_major=True)
    na, nall = len(shards), len(shards) + len(whole)
    n_h, n_j = H // hp, qi_tab.shape[0]

    def body(qi_ref, ki_ref, q_ref, k_ref, vt_ref, *rest):
        srcs, rest = rest[:nall], rest[nall:]
        o_ref, qx_ref = rest[:2]
        dsts, rest = rest[2:2 + nall], rest[2 + nall:]
        m_s, acc_s = rest[:2]
        h = pl.program_id(0)
        j = pl.program_id(1)
        qi = qi_ref[j]
        ki = ki_ref[j]

        if nall:
            gather = _GatherPlan(srcs, dsts, rest[2:], na)
            step = h * n_j + j
            pl.when(step == 0)(gather.send)
            pl.when(step == n_h * n_j // 2)(gather.forward)
            pl.when(step == n_h * n_j - 1)(gather.finish)

        @pl.when(ki == 0)
        def _():
            m_s[...] = jnp.full(m_s.shape, NEG, F32)
            acc_s[...] = jnp.zeros_like(acc_s)

        def step(on_diagonal):
            cols = [slice(a * AUG, (a + 1) * AUG) for a in heads]
            if on_diagonal:
                krow = lax.broadcasted_iota(jnp.int32, (t, t), 0)
                qcol = lax.broadcasted_iota(jnp.int32, (t, t), 1)
            def logits(a):
                st = _dot_nt(k_ref[:, cols[a]], q_ref[:, cols[a]])
                return jnp.where(krow <= qcol, st, NEG) if on_diagonal else st

            st_next = logits(0)
            for a in heads:
                st = st_next
                if a + 1 < hp:
                    st_next = logits(a + 1)
                m_prev = m_s[a]
                m_new = jnp.maximum(m_prev, jnp.max(st, axis=0, keepdims=True))
                pt = jnp.exp2(st - m_new).astype(BF16)
                acc_s[a] = jnp.exp2(m_prev - m_new) * acc_s[a] + _dot(vt_ref[cols[a], :], pt)
                m_s[a] = m_new

        @pl.when(ki < qi)
        def _():
            step(False)

        @pl.when(ki == qi)
        def _():
            step(True)
            piece = lax.broadcasted_iota(jnp.int32, (DH, t), 0)
            for a in heads:
                l = acc_s[a, DH:DH + 1, :]
                ex = jnp.transpose(q_ref[:, a * AUG + DH:(a + 1) * AUG].astype(F32))
                c2 = jnp.sum(jnp.where(piece < 3, ex, 0.0), axis=0, keepdims=True)
                hi, mid, lo = _split3(jnp.broadcast_to(c2 - (m_s[a] + jnp.log(l) * LOG2E), (DH, t)))
                ones = jnp.where((piece >= 3) & (piece < 6), 1.0, 0.0).astype(BF16)
                ex_t = jnp.where(piece == 0, hi, jnp.where(piece == 1, mid, jnp.where(piece == 2, lo, ones)))
                o_ref[:, a * DH:(a + 1) * DH] = jnp.transpose(acc_s[a, :DH, :] / l)
                qx_ref[:, a * DH:(a + 1) * DH] = jnp.transpose(ex_t.astype(F32)).astype(BF16)

    q_spec = pl.BlockSpec((t, hp * AUG), lambda h, j, qi_ref, ki_ref: (qi_ref[j], h))
    k_spec = pl.BlockSpec((t, hp * AUG), lambda h, j, qi_ref, ki_ref: (ki_ref[j], h))
    vt_spec = pl.BlockSpec((hp * AUG, t), lambda h, j, qi_ref, ki_ref: (h, ki_ref[j]))
    out_spec = pl.BlockSpec((t, hp * DH), lambda h, j, qi_ref, ki_ref: (qi_ref[j], h))
    arrs = list(shards) + list(whole)
    grid_spec = pltpu.PrefetchScalarGridSpec(
        num_scalar_prefetch=2, grid=(n_h, n_j),
        in_specs=[q_spec, k_spec, vt_spec] + [HBM_SPEC] * nall, out_specs=[out_spec, out_spec] + [HBM_SPEC] * nall,
        scratch_shapes=[pltpu.VMEM((hp, 1, t), F32), pltpu.VMEM((hp, AUG, t), F32)]
        + (_gather_semaphores(na, nall) if nall else []))
    outs = _call(
        body, name="attn_fwd", grid_spec=grid_spec,
        out_shape=[jax.ShapeDtypeStruct((T, D), F32), jax.ShapeDtypeStruct((T, D), BF16)] + _gather_out_shapes(arrs),
        compiler_params=_cparams(("arbitrary", "arbitrary"), VMEM_BIG),
    )(qi_tab, ki_tab, q_aug, k_aug, vt_aug, *arrs)
    return outs[0], outs[1], _place_own(outs[2:], arrs)


def _lru_gates(xc, wr_ref, br_ref, wi_ref, bi_ref, lam_ref):
    r = _sigmoid(_gate_pre(xc, wr_ref) + br_ref[...])
    ig = _sigmoid(_gate_pre(xc, wi_ref) + bi_ref[...])
    sp = _softplus_neg(lam_ref[...])
    la = (-LRU_C) * r * sp
    a = jnp.exp(la)
    y = -jnp.tanh(la) * (a * a + 1.0)
    return r, ig, sp, a, jnp.sqrt(y), lax.rsqrt(y)


def _branches_fwd(o, g_attn, x_lru, g_lru, gain_a, gain_l, conv_w, conv_b, w_r, b_r, w_i, b_i, lam):
    T = o.shape[0]
    tm = TM

    def body(o_ref, ga_ref, xl_ref, gl_ref, gna_ref, gnl_ref, cw_ref, cb_ref, wr_ref, br_ref, wi_ref, bi_ref,
             lam_ref, ycat_ref, xc_ref, h_ref, yct_ref, halo_s, hc_s):
        @pl.when(pl.program_id(0) == 0)
        def _():
            halo_s[...] = jnp.zeros_like(halo_s)
            hc_s[...] = jnp.zeros_like(hc_s)

        ov = o_ref[...]
        ga = ga_ref[...]
        ya = ov * _rstd(ov) * gna_ref[...] * (ga * _sigmoid(ga))
        ycat_ref[:, :D] = ya.astype(BF16)
        yct_ref[:D, :] = jnp.transpose(ya).astype(BF16)

        xl = xl_ref[...]
        halo = halo_s[...]
        xc = xl * cw_ref[3:4, :] + cb_ref[...]
        for j in range(3):
            xc = xc + _shift_down(xl, 3 - j, halo) * cw_ref[j:j + 1, :]
        halo_s[...] = xl_ref[tm - SUBLANES:tm, :]
        xc_ref[...] = xc

        _, ig, _, a, sq, _ = _lru_gates(xc, wr_ref, br_ref, wi_ref, bi_ref, lam_ref)
        u = sq * (ig * xc)
        hc_s[...] = _scan_fwd_into(a, u, hc_s[...], h_ref)
        hh = h_ref[...]

        gl = gl_ref[...]
        yl = hh * _rstd(hh) * gnl_ref[...] * (gl * _sigmoid(gl))
        ycat_ref[:, D:] = yl.astype(BF16)
        yct_ref[D:, :] = jnp.transpose(yl).astype(BF16)

    vec = _const_spec((1, D))
    wspec = _const_spec((NB, LANES, LANES))
    return _call(
        body, name="branches_fwd", grid=(T // tm,),
        in_specs=[_row_spec(tm, D)] * 4 + [vec, vec, _const_spec((4, D)), vec, wspec, vec, wspec, vec, vec],
        out_specs=[_row_spec(tm, DMIX), _row_spec(tm, D), _row_spec(tm, D), pl.BlockSpec((DMIX, tm), lambda i: (0, i))],
        out_shape=[jax.ShapeDtypeStruct((T, DMIX), BF16), jax.ShapeDtypeStruct((T, D), F32),
                   jax.ShapeDtypeStruct((T, D), F32), jax.ShapeDtypeStruct((DMIX, T), BF16)],
        scratch_shapes=[pltpu.VMEM((SUBLANES, D), F32), pltpu.VMEM((1, D), F32)],
        compiler_params=_cparams(("arbitrary",)),
    )(o, g_attn, x_lru, g_lru, gain_a, gain_l, conv_w, conv_b, w_r, b_r, w_i, b_i, lam)


def _tail(ycat, x, p, tgt, w_out, post_gain, w_ple, ple_gain, w_gate, b_gate):
    T = x.shape[0]
    tm = TM

    def body(ycat_ref, x_ref, p_ref, t_ref, wo_ref, pg_ref, wp_ref, eg_ref, wg_ref, bg_ref,
             dh1_ref, dycat_ref, dmix_ref, h1b_ref, dgp_ref, pb_ref, dpe_ref, acc_ref):
        @pl.when(pl.program_id(0) == 0)
        def _():
            acc_ref[...] = jnp.zeros_like(acc_ref)

        mix = _dot(ycat_ref[...], wo_ref[...])
        rstd_m = _rstd(mix)
        mhat = mix * rstd_m
        h1 = x_ref[...] + mhat * pg_ref[...]
        pv = p_ref[...]
        pb = pv.astype(BF16)
        pb_ref[...] = jnp.transpose(pv).astype(BF16)
        pe = _dot(pb, wp_ref[...])
        rstd_p = _rstd(pe)
        pehat = pe * rstd_p
        e = pehat * eg_ref[...]
        h1b = h1.astype(BF16)
        h1b_ref[...] = jnp.transpose(h1).astype(BF16)
        gate = _sigmoid(_dot(h1b, wg_ref[...]) + bg_ref[...])
        diff = (h1 + gate * e) - t_ref[...]

        dy = diff * (1.0 / D)
        de = dy * gate
        dgp = (dy * e) * gate * (1.0 - gate)
        dgpb = dgp.astype(BF16)
        dgp_ref[...] = dgpb
        dh1 = dy + _dot_nt(dgpb, wg_ref[...])
        dh1_ref[...] = dh1
        dpe_ref[...] = _rms_bwd(de * eg_ref[...], pehat, rstd_p).astype(BF16)
        dmix = _rms_bwd(dh1 * pg_ref[...], mhat, rstd_m).astype(BF16)
        dmix_ref[...] = dmix
        dycat_ref[...] = _dot_nt(dmix, wo_ref[...])

        acc_ref[0:1, :] += jnp.sum(dh1 * mhat, axis=0, keepdims=True)
        acc_ref[1:2, :] += jnp.sum(de * pehat, axis=0, keepdims=True)
        acc_ref[2:3, :] += jnp.sum(dgp, axis=0, keepdims=True)
        acc_ref[3:4, :] += jnp.sum(diff * diff, axis=0, keepdims=True) * (0.5 / D)

    vec = _const_spec((1, D))
    bf = jax.ShapeDtypeStruct((T, D), BF16)
    return _call(
        body, name="tail", grid=(T // tm,),
        in_specs=[_row_spec(tm, DMIX), _row_spec(tm, D), _row_spec(tm, DPLE), _row_spec(tm, D),
                  _const_spec((DMIX, D)), vec, _const_spec((DPLE, D)), vec, _const_spec((D, D)), vec],
        out_specs=[_row_spec(tm, D), _row_spec(tm, DMIX), _row_spec(tm, D), pl.BlockSpec((D, tm), lambda i: (0, i)),
                   _row_spec(tm, D), pl.BlockSpec((DPLE, tm), lambda i: (0, i)), _row_spec(tm, D),
                   _const_spec((SUBLANES, D))],
        out_shape=[jax.ShapeDtypeStruct((T, D), F32), jax.ShapeDtypeStruct((T, DMIX), F32), bf,
                   jax.ShapeDtypeStruct((D, T), BF16), bf, jax.ShapeDtypeStruct((DPLE, T), BF16), bf,
                   jax.ShapeDtypeStruct((SUBLANES, D), F32)],
        compiler_params=_cparams(("arbitrary",), VMEM_BIG),
    )(ycat, x, p, tgt, w_out, post_gain, w_ple, ple_gain, w_gate, b_gate)


def _pair_copies(srcs, gots, send_sems, recv_sems):
    x, y, c = _position()
    copies = []
    for a, (src, got) in enumerate(zip(srcs, gots)):
        half = src.shape[1] // 2
        rows = pl.ds(pl.multiple_of((1 - c) * half, SUBLANES), half)
        copies.append(pltpu.make_async_remote_copy(
            src_ref=src.at[:, rows, :], dst_ref=got, send_sem=send_sems.at[a], recv_sem=recv_sems.at[a],
            device_id=(x, y, 1 - c), device_id_type=MESH))
    return copies


def _branches_bwd(dycat, o, g_attn, h, g_lru, gain_a, gain_l, ycat_t, dmix, h1_t, dgp, p_t, dpe):
    T = o.shape[0]
    tm = TM
    nt = T // tm
    nb_gate, nb_ple = min(nt, 8), min(nt, 2)
    ns_gate, ns_ple = nt // nb_gate, nt // nb_ple
    br_out, br_gate, br_ple = DMIX // nt, D // nb_gate, DPLE // nb_ple
    tk_gate, tk_ple = T // ns_gate, T // ns_ple

    def body(dy_ref, o_ref, ga_ref, h_ref, gl_ref, gna_ref, gnl_ref, yc_ref, dmix_ref, h1_ref, dgp_ref, pb_ref,
             dpe_ref, do_ref, dga_ref, dgl_ref, dh_ref, acc_ref, gwo_ref, gwg_ref, gwp_ref):
        i = pl.program_id(0)

        @pl.when(i == 0)
        def _():
            acc_ref[...] = jnp.zeros_like(acc_ref)

        def accumulate(out_ref, lhs_ref, rhs_ref, tokens, slices):
            s = i % slices
            part = _dot(lhs_ref[...], rhs_ref[pl.ds(pl.multiple_of(s * tokens, tokens), tokens), :])
            out_ref[...] = part + jnp.where(s == 0, 0.0, out_ref[...])

        gwo_ref[...] = _dot(yc_ref[...], dmix_ref[...])

        def branch(val, g, gain, dyv):
            rstd = _rstd(val)
            vhat = val * rstd
            sig = _sigmoid(g)
            dn = dyv * (g * sig)
            dg = dyv * (vhat * gain) * (sig * (1.0 + g * (1.0 - sig)))
            dgain = jnp.sum(dn * vhat, axis=0, keepdims=True)
            return _rms_bwd(dn * gain, vhat, rstd), dg, dgain

        ov = o_ref[...]
        do, dga, dgain_a = branch(ov, ga_ref[...], gna_ref[...], dy_ref[:, :D])
        dga_ref[...] = dga.astype(BF16)
        prod = do * ov
        for hd in range(H):
            head = slice(hd * DH, (hd + 1) * DH)
            do_ref[:, hd * AUG:hd * AUG + DH] = do[:, head].astype(BF16)
            do_ref[:, hd * AUG + DH:(hd + 1) * AUG] = _extras(-jnp.sum(prod[:, head], axis=1, keepdims=True), None)

        accumulate(gwg_ref, h1_ref, dgp_ref, tk_gate, ns_gate)
        accumulate(gwp_ref, pb_ref, dpe_ref, tk_ple, ns_ple)
        dh, dgl, dgain_l = branch(h_ref[...], gl_ref[...], gnl_ref[...], dy_ref[:, D:])
        dh_ref[...] = dh
        dgl_ref[...] = dgl.astype(BF16)
        acc_ref[0:1, :] += dgain_a
        acc_ref[1:2, :] += dgain_l

    vec = _const_spec((1, D))
    bf = jax.ShapeDtypeStruct((T, D), BF16)
    tokens = _weight_spec((T, D))
    return _call(
        body, name="branches_bwd", grid=(nt,),
        in_specs=[_row_spec(tm, DMIX)] + [_row_spec(tm, D)] * 4 + [vec, vec]
        + [pl.BlockSpec((br_out, T), lambda i: (i, 0)), tokens,
           pl.BlockSpec((br_gate, tk_gate), lambda i: (i // ns_gate, i % ns_gate)), tokens,
           pl.BlockSpec((br_ple, tk_ple), lambda i: (i // ns_ple, i % ns_ple)), tokens],
        out_specs=[_row_spec(tm, H * AUG), _row_spec(tm, D), _row_spec(tm, D), _row_spec(tm, D),
                   _const_spec((SUBLANES, D)),
                   pl.BlockSpec((br_out, D), lambda i: (i, 0)),
                   pl.BlockSpec((br_gate, D), lambda i: (i // ns_gate, 0)),
                   pl.BlockSpec((br_ple, D), lambda i: (i // ns_ple, 0))],
        out_shape=[jax.ShapeDtypeStruct((T, H * AUG), BF16), bf, bf, jax.ShapeDtypeStruct((T, D), F32),
                   jax.ShapeDtypeStruct((SUBLANES, D), F32), jax.ShapeDtypeStruct((DMIX, D), F32),
                   jax.ShapeDtypeStruct((D, D), F32), jax.ShapeDtypeStruct((DPLE, D), F32)],
        compiler_params=_cparams(("arbitrary",), VMEM_BIG),
    )(dycat, o, g_attn, h, g_lru, gain_a, gain_l, ycat_t, dmix, h1_t, dgp, p_t, dpe)


def _lru_bwd(dh, h, xc, x_lru, conv_w, w_r, b_r, w_i, b_i, lam, pair_parts=()):
    T = dh.shape[0]
    tm = TM
    nt = T // tm
    per = tm // SUBLANES
    npair = len(pair_parts)

    def body(dh_ref, h_ref, hprev_ref, xc_ref, xl_ref, cw_ref, wr_ref, br_ref, wi_ref, bi_ref, lam_ref, *rest):
        parts, rest = rest[:npair], rest[npair:]
        dxl_ref, dwr_ref, dwi_ref, acc_ref = rest[:4]
        gots, rest = rest[4:4 + npair], rest[4 + npair:]
        carry_s, dxc_next_s, top_s, dht_s = rest[:4]
        i = pl.program_id(0)

        @pl.when(i == 0)
        def _():
            acc_ref[...] = jnp.zeros_like(acc_ref)
            dwr_ref[...] = jnp.zeros_like(dwr_ref)
            dwi_ref[...] = jnp.zeros_like(dwi_ref)
            carry_s[...] = jnp.zeros_like(carry_s)
            dxc_next_s[...] = jnp.zeros_like(dxc_next_s)
            for cp in _pair_copies(parts, gots, *rest[4:]) if npair else ():
                cp.start()

        if npair:
            @pl.when(i == nt - 1)
            def _():
                for cp in _pair_copies(parts, gots, *rest[4:]):
                    cp.wait()

        inner = jnp.where(i == nt - 1, 0.0, 1.0)
        xc = xc_ref[...]
        r, ig, sp, a, sq, inv_sq = _lru_gates(xc, wr_ref, br_ref, wi_ref, bi_ref, lam_ref)

        row = lax.broadcasted_iota(jnp.int32, (tm, D), 0)
        u = dh_ref[...] + jnp.where(row == tm - 1, carry_s[...], 0.0)
        _scan_bwd_into(pltpu.roll(a, tm - 1, 0), u, dht_s)
        dht = dht_s[...]
        top_s[...] = a[:SUBLANES, :] * dht[:SUBLANES, :]
        carry_s[...] = top_s[0:1, :]

        hprev = hprev_ref[...] * inner
        da = dht * _shift_down(h_ref[...], 1, hprev)
        dig = dht * sq * xc
        dxc = dht * sq * ig
        dsq = dht * ig * xc
        dla = da * a - dsq * (a * a) * inv_sq
        dr = dla * ((-LRU_C) * sp)
        dpr = dr * r * (1.0 - r)
        dpi = dig * ig * (1.0 - ig)
        for n in range(NB):
            blk = slice(n * LANES, (n + 1) * LANES)
            xcb = xc[:, blk].astype(BF16)
            dwr_ref[n] += _dot_tn(xcb, dpr[:, blk].astype(BF16))
            dwi_ref[n] += _dot_tn(xcb, dpi[:, blk].astype(BF16))
        dxc = dxc + _gate_pre_t(dpr, wr_ref) + _gate_pre_t(dpi, wi_ref)

        xl = xl_ref[...]
        nxt = dxc_next_s[...]
        dxl = dxc * cw_ref[3:4, :]
        acc_ref[3:4, :] += jnp.sum(dxc * xl, axis=0, keepdims=True)
        for j in range(3):
            ahead = _shift_up(dxc, 3 - j, nxt)
            dxl = dxl + ahead * cw_ref[j:j + 1, :]
            acc_ref[j:j + 1, :] += jnp.sum(ahead * xl, axis=0, keepdims=True)
        dxc_next_s[...] = dxc[:SUBLANES, :]
        dxl_ref[...] = dxl.astype(BF16)

        acc_ref[4:5, :] += jnp.sum(dxc, axis=0, keepdims=True)
        acc_ref[5:6, :] += jnp.sum(dpr, axis=0, keepdims=True)
        acc_ref[6:7, :] += jnp.sum(dpi, axis=0, keepdims=True)
        acc_ref[7:8, :] += jnp.sum(dla * ((-LRU_C) * r), axis=0, keepdims=True)

        @pl.when(i == nt - 1)
        def _():
            lam_v = lam_ref[...]
            acc_ref[7:8, :] = acc_ref[7:8, :] * (-_sigmoid(-lam_v))

    rev = pl.BlockSpec((tm, D), lambda i: (nt - 1 - i, 0))
    prev8 = pl.BlockSpec((SUBLANES, D), lambda i: (jnp.maximum((nt - 1 - i) * per - 1, 0), 0))
    vec = _const_spec((1, D))
    wspec = _const_spec((NB, LANES, LANES))
    bf = jax.ShapeDtypeStruct((T, D), BF16)
    halves = [jax.ShapeDtypeStruct((s.shape[0], s.shape[1] // 2, s.shape[2]), s.dtype) for s in pair_parts]
    outs = _call(
        body, name="lru_bwd", grid=(nt,),
        in_specs=[rev, rev, prev8, rev, rev, _const_spec((4, D)), wspec, vec, wspec, vec, vec] + [HBM_SPEC] * npair,
        out_specs=[rev, wspec, wspec, _const_spec((SUBLANES, D))] + [HBM_SPEC] * npair,
        out_shape=[bf, jax.ShapeDtypeStruct((NB, LANES, LANES), F32), jax.ShapeDtypeStruct((NB, LANES, LANES), F32),
                   jax.ShapeDtypeStruct((SUBLANES, D), F32)] + halves,
        scratch_shapes=[pltpu.VMEM((1, D), F32), pltpu.VMEM((SUBLANES, D), F32), pltpu.VMEM((SUBLANES, D), F32),
                        pltpu.VMEM((tm, D), F32)]
        + ([pltpu.SemaphoreType.DMA((npair,)), pltpu.SemaphoreType.DMA((npair,))] if npair else []),
        compiler_params=_cparams(("arbitrary",)),
    )(dh, h, h, xc, x_lru, conv_w, w_r, b_r, w_i, b_i, lam, *pair_parts)
    return (*outs[:4], list(outs[4:]))


def _chip_copies(srcs, dsts, send_sems, recv_sems):
    x, y, c = _position()
    chip = 2 * x + y
    na = len(srcs)
    return [pltpu.make_async_remote_copy(
        src_ref=srcs[a].at[2 * px + py], dst_ref=dsts[a].at[chip], send_sem=send_sems.at[j * na + a],
        recv_sem=recv_sems.at[j * na + a], device_id=(px, py, c), device_id_type=MESH)
        for j, (px, py) in enumerate(_other_chips(x, y)) for a in range(na)]


def _attn_bwd(q_aug, qx, k_aug, v_aug, do_aug, exchange=()):
    T = q_aug.shape[0]
    t = TA
    n = T // t
    hp = BWD_HEADS
    heads = range(hp)
    scale = DH ** -0.5
    ki_tab, qi_tab = _causal_pairs(n, q_major=False)
    last = ki_tab.shape[0] - 1
    ne = len(exchange)
    n_h = H // hp

    def body(ki_ref, qi_ref, q_ref, qx_ref, k_ref, v_ref, do_ref, *rest):
        sent, rest = rest[:ne], rest[ne:]
        dq_ref, dk_ref, dv_ref, dc_ref = rest[:4]
        received, rest = rest[4:4 + ne], rest[4 + ne:]
        dq_s, dk_s, dv_s = rest[:3]
        j = pl.program_id(1)
        ki = ki_ref[j]
        qi = qi_ref[j]

        if ne:
            first_step = (pl.program_id(0) == 0) & (j == 0)
            last_step = (pl.program_id(0) == n_h - 1) & (j == last)

            @pl.when(first_step)
            def _():
                for cp in _chip_copies(sent, received, *rest[3:]):
                    cp.start()

            @pl.when(last_step)
            def _():
                for cp in _chip_copies(sent, received, *rest[3:]):
                    cp.wait()

        @pl.when(j == 0)
        def _():
            dq_s[...] = jnp.zeros_like(dq_s)

        @pl.when(qi == ki)
        def _():
            dk_s[...] = jnp.zeros_like(dk_s)
            dv_s[...] = jnp.zeros_like(dv_s)

        def step(on_diagonal):
            cols = [slice(a * AUG, (a + 1) * AUG) for a in heads]
            qb = [jnp.concatenate([q_ref[:, a * AUG:a * AUG + DH], qx_ref[:, a * DH:(a + 1) * DH]], axis=1)
                  for a in heads]
            if on_diagonal:
                krow = lax.broadcasted_iota(jnp.int32, (t, t), 0)
                qcol = lax.broadcasted_iota(jnp.int32, (t, t), 1)

            def scores(a):
                st = _dot_nt(k_ref[:, cols[a]], qb[a])
                dpd = _dot_nt(v_ref[:, cols[a]], do_ref[:, cols[a]])
                return (jnp.where(krow <= qcol, st, NEG) if on_diagonal else st), dpd

            off = pl.multiple_of(qi * t, t)
            ahead = scores(0)
            for a in heads:
                st, dpd = ahead
                if a + 1 < hp:
                    ahead = scores(a + 1)
                pt = jnp.exp2(st)
                dsb = (pt * dpd).astype(BF16)
                dv_s[a] += _dot(pt.astype(BF16), do_ref[:, a * AUG:a * AUG + DH])
                dk_s[a] += _dot(dsb, qb[a])
                dq_s[a, pl.ds(off, t), :] += _dot_tn(dsb, k_ref[:, cols[a]])

        @pl.when(qi > ki)
        def _():
            step(False)

        @pl.when(qi == ki)
        def _():
            step(True)

        @pl.when(qi == n - 1)
        def _():
            rows = pl.ds(pl.multiple_of(ki * t, t), t)
            for a in heads:
                dk_ref[:, a * DH:(a + 1) * DH] = (dk_s[a, :, :DH] * LN2).astype(BF16)
                dv_ref[:, a * DH:(a + 1) * DH] = dv_s[a].astype(BF16)
                dc_ref[a, rows, :] = jnp.broadcast_to(-dk_s[a, :, DH + 3:DH + 4], (t, LANES))

        @pl.when(j == last)
        def _():
            for a in heads:
                dq_ref[:, a * DH:(a + 1) * DH] = (dq_s[a, :, :DH] * scale).astype(BF16)
                dc_ref[a] = dc_ref[a] + jnp.broadcast_to(dq_s[a, :, DH:DH + 1], (T, LANES))

    qside = pl.BlockSpec((t, hp * AUG), lambda h, j, ki_ref, qi_ref: (qi_ref[j], h))
    qxside = pl.BlockSpec((t, hp * DH), lambda h, j, ki_ref, qi_ref: (qi_ref[j], h))
    kside = pl.BlockSpec((t, hp * AUG), lambda h, j, ki_ref, qi_ref: (ki_ref[j], h))
    kout = pl.BlockSpec((t, hp * DH), lambda h, j, ki_ref, qi_ref: (ki_ref[j], h))
    bf = jax.ShapeDtypeStruct((T, D), BF16)
    sums = jax.ShapeDtypeStruct((H, T, LANES), F32)
    grid_spec = pltpu.PrefetchScalarGridSpec(
        num_scalar_prefetch=2, grid=(n_h, ki_tab.shape[0]),
        in_specs=[qside, qxside, kside, kside, qside] + [HBM_SPEC] * ne,
        out_specs=[pl.BlockSpec((T, hp * DH), lambda h, j, ki_ref, qi_ref: (0, h)), kout, kout,
                   pl.BlockSpec((hp, T, LANES), lambda h, j, ki_ref, qi_ref: (h, 0, 0))] + [HBM_SPEC] * ne,
        scratch_shapes=[pltpu.VMEM((hp, T, AUG), F32), pltpu.VMEM((hp, t, AUG), F32), pltpu.VMEM((hp, t, DH), F32)]
        + ([pltpu.SemaphoreType.DMA((3 * ne,)), pltpu.SemaphoreType.DMA((3 * ne,))] if ne else []))
    outs = _call(
        body, name="attn_bwd", grid_spec=grid_spec,
        out_shape=[bf, bf, bf, sums] + [jax.ShapeDtypeStruct(s.shape, s.dtype) for s in exchange],
        compiler_params=_cparams(("arbitrary", "arbitrary"), VMEM_BIG),
    )(ki_tab, qi_tab, q_aug, qx, k_aug, v_aug, do_aug, *exchange)
    return (*outs[:4], list(outs[4:]))


def _fgate_bwd(dc_heads, flb):
    T = flb.shape[0]
    tm = TM
    nt = T // tm

    def body(dch_ref, flb_ref, dfl_ref, acc_ref, carry, top_s):
        @pl.when(pl.program_id(0) == 0)
        def _():
            carry[...] = jnp.zeros_like(carry)
            acc_ref[...] = jnp.zeros_like(acc_ref)

        flb = flb_ref[...]
        lane = lax.broadcasted_iota(jnp.int32, flb.shape, 1)
        dc = jnp.zeros(flb.shape, F32)
        for hd in range(H):
            dc = dc + jnp.where(lane == hd, dch_ref[hd], 0.0)
        r = lax.broadcasted_iota(jnp.int32, (tm, tm), 0)
        c = lax.broadcasted_iota(jnp.int32, (tm, tm), 1)
        dls = _dot_exact((c >= r).astype(F32), dc) + carry[...]
        top_s[...] = dls[:SUBLANES, :]
        carry[...] = top_s[0:1, :]
        dfl = jnp.where(lane < H, dls * _sigmoid(-flb), 0.0)
        dfl_ref[...] = dfl.astype(BF16)
        acc_ref[0:1, :] += jnp.sum(dfl, axis=0, keepdims=True)

    rev = pl.BlockSpec((tm, LANES), lambda i: (nt - 1 - i, 0))
    return _call(
        body, name="fgate_bwd", grid=(nt,),
        in_specs=[pl.BlockSpec((H, tm, LANES), lambda i: (0, nt - 1 - i, 0)), rev],
        out_specs=[rev, _const_spec((SUBLANES, LANES))],
        out_shape=[jax.ShapeDtypeStruct((T, LANES), BF16), jax.ShapeDtypeStruct((SUBLANES, LANES), F32)],
        scratch_shapes=[pltpu.VMEM((1, LANES), F32), pltpu.VMEM((SUBLANES, LANES), F32)],
        compiler_params=_cparams(("arbitrary",)),
    )(dc_heads, flb)


def _dx(dz, dfl, w_a, w_f, w_b, x, pre_gain, dh1, exchange=()):
    T = x.shape[0]
    tm = TM
    nt = T // tm
    ne = len(exchange)

    def body(*refs):
        dz_refs = refs[:6]
        dfl_ref, wa_ref, wf_ref, wb_ref, x_ref, g_ref, dh1_ref = refs[6:13]
        sent = refs[13:13 + ne]
        gx_ref, acc_ref = refs[13 + ne:15 + ne]
        received, sems = refs[15 + ne:15 + 2 * ne], refs[15 + 2 * ne:]

        @pl.when(pl.program_id(0) == 0)
        def _():
            acc_ref[...] = jnp.zeros_like(acc_ref)
            for cp in _chip_copies(sent, received, *sems) if ne else ():
                cp.start()

        if ne:
            @pl.when(pl.program_id(0) == nt - 1)
            def _():
                for cp in _chip_copies(sent, received, *sems):
                    cp.wait()

        dxn = _dot(dfl_ref[...], wf_ref[...])
        for s in range(3):
            dxn = dxn + _dot(dz_refs[s][...], wa_ref[s * D:(s + 1) * D, :])
            dxn = dxn + _dot(dz_refs[3 + s][...], wb_ref[s * D:(s + 1) * D, :])
        xv = x_ref[...]
        rstd = _rstd(xv)
        xhat = xv * rstd
        gx_ref[...] = dh1_ref[...] + _rms_bwd(dxn * g_ref[...], xhat, rstd)
        acc_ref[0:1, :] += jnp.sum(dxn * xhat, axis=0, keepdims=True)

    outs = _call(
        body, name="dx", grid=(nt,),
        in_specs=[_row_spec(tm, D)] * 6 + [_row_spec(tm, LANES), _weight_spec((3 * D, D)), _weight_spec((LANES, D)),
                                           _weight_spec((3 * D, D)), _row_spec(tm, D), _const_spec((1, D)),
                                           _row_spec(tm, D)] + [HBM_SPEC] * ne,
        out_specs=[_row_spec(tm, D), _const_spec((SUBLANES, D))] + [HBM_SPEC] * ne,
        out_shape=[jax.ShapeDtypeStruct((T, D), F32), jax.ShapeDtypeStruct((SUBLANES, D), F32)]
        + [jax.ShapeDtypeStruct(s.shape, s.dtype) for s in exchange],
        scratch_shapes=[pltpu.SemaphoreType.DMA((3 * ne,)), pltpu.SemaphoreType.DMA((3 * ne,))] if ne else [],
        compiler_params=_cparams(("arbitrary",), VMEM_BIG),
    )(*dz, dfl, w_a, w_f, w_b, x, pre_gain, dh1, *exchange)
    return outs[0], outs[1], list(outs[2:])


GRAD_ROWS = D_IN + SUBLANES


def _dw_in_segments(dz_a, dz_b, xn, buf, pair, bt):
    T = xn.shape[0]
    nt = T // bt
    first, second = [(2 * pair + k) * D + (H if 2 * pair + k >= 3 else 0) for k in (0, 1)]
    step8 = (second - first) // SUBLANES

    def body(*refs):
        dza_ref, dzb_ref, xn_ref, o_ref = refs[0], refs[1], refs[2], refs[-1]

        @pl.when(pl.program_id(1) == 0)
        def _():
            o_ref[...] = jnp.zeros_like(o_ref)

        @pl.when(pl.program_id(0) == 0)
        def _():
            o_ref[...] += _dot_tn(dza_ref[...], xn_ref[...])

        @pl.when(pl.program_id(0) == 1)
        def _():
            o_ref[...] += _dot_tn(dzb_ref[...], xn_ref[...])

    spec_a = pl.BlockSpec((bt, D), lambda s, t: (jnp.where(s == 0, t, nt - 1), 0))
    spec_b = pl.BlockSpec((bt, D), lambda s, t: (jnp.where(s == 1, t, 0), 0))
    return _call(
        body, name="dw_in_%d" % pair, grid=(2, nt),
        in_specs=[spec_a, spec_b, pl.BlockSpec((bt, D), lambda s, t: (t, 0))]
        + ([] if buf is None else [pl.BlockSpec(memory_space=pl.ANY)]),
        out_specs=pl.BlockSpec((pl.Element(D), pl.Element(D)),
                               lambda s, t: ((first // SUBLANES + s * step8) * SUBLANES, 0)),
        out_shape=jax.ShapeDtypeStruct((GRAD_ROWS, D), F32),
        input_output_aliases={} if buf is None else {3: 0},
        compiler_params=_cparams(("arbitrary", "arbitrary"), VMEM_BIG),
    )(*((dz_a, dz_b, xn) if buf is None else (dz_a, dz_b, xn, buf)))


def _dw_in_t(dz, dfl, xn, bt=DW_TOKENS):
    T = xn.shape[0]
    bt = min(bt, T)
    nt = T // bt
    main = None
    for pair in range(3):
        main = _dw_in_segments(dz[2 * pair], dz[2 * pair + 1], xn, main, pair, bt)

    def f_body(dfl_ref, xn_ref, main_ref, o_ref, acc_s):
        p = pl.program_id(0)
        t = pl.program_id(1)

        @pl.when(t == 0)
        def _():
            acc_s[...] = jnp.zeros_like(acc_s)

        @pl.when(p == 0)
        def _():
            acc_s[...] += _dot_tn(dfl_ref[...], xn_ref[...])

        @pl.when(t == nt - 1)
        def _():
            o_ref[...] = acc_s[:SUBLANES, :]

    fl_block = FL0 // SUBLANES
    end_block = D_IN // SUBLANES
    return _call(
        f_body, name="dw_in_f", grid=(2, nt),
        in_specs=[pl.BlockSpec((bt, LANES), lambda p, t: (t, 0)), pl.BlockSpec((bt, D), lambda p, t: (t, 0)),
                  pl.BlockSpec(memory_space=pl.ANY)],
        out_specs=pl.BlockSpec((SUBLANES, D), lambda p, t: (fl_block + p * (end_block - fl_block), 0)),
        out_shape=jax.ShapeDtypeStruct((GRAD_ROWS, D), F32),
        scratch_shapes=[pltpu.VMEM((LANES, D), F32)],
        input_output_aliases={2: 0},
        compiler_params=_cparams(("arbitrary", "arbitrary")),
    )(dfl, xn, main)


HBM_SPEC = pl.BlockSpec(memory_space=pltpu.HBM)
VMEM_SPEC = pl.BlockSpec(memory_space=pltpu.VMEM)


def _position():
    return lax.axis_index("x"), lax.axis_index("y"), lax.axis_index("c")


def _other_chips(x, y):
    return [(1 - x, y), (x, 1 - y), (1 - x, 1 - y)]


def _gather_shards(shards, whole):
    na, nw = len(shards), len(whole)
    nall = na + nw

    def body(*refs):
        gather = _GatherPlan(refs[:nall], refs[nall:2 * nall], refs[2 * nall:], na)
        gather.send()
        gather.forward()
        gather.finish()

    arrs = list(shards) + list(whole)
    outs = _call(
        body, name="gather_shards",
        in_specs=[HBM_SPEC] * nall, out_specs=[HBM_SPEC] * nall,
        out_shape=_gather_out_shapes(arrs), scratch_shapes=_gather_semaphores(na, nall),
    )(*arrs)
    return _place_own(outs, arrs)


def _gather_out_shapes(arrs):
    return [jax.ShapeDtypeStruct((N_CHIPS,) + s.shape, s.dtype) for s in arrs]


def _gather_semaphores(na, nall):
    return [pltpu.SemaphoreType.DMA((3 * nall,)), pltpu.SemaphoreType.DMA((3 * nall,)),
            pltpu.SemaphoreType.DMA((3 * na,)), pltpu.SemaphoreType.DMA((3 * na,))]


def _place_own(outs, arrs):
    if not arrs:
        return []
    chip = 2 * lax.axis_index("x") + lax.axis_index("y")
    return [lax.dynamic_update_slice(o, a[None], (chip,) + (0,) * a.ndim) for o, a in zip(outs, arrs)]


class _GatherPlan:
    def __init__(self, srcs, dsts, sems, na):
        ici_send, ici_recv, d2d_send, d2d_recv = sems
        x, y, c = _position()
        chip = 2 * x + y
        nall = len(srcs)

        def half(a, which):
            rows = srcs[a].shape[0] // 2
            return pl.ds(pl.multiple_of(which * rows, BF16_ROWS), rows)

        def copy(src, dst, send, recv, k, to):
            return pltpu.make_async_remote_copy(src_ref=src, dst_ref=dst, send_sem=send.at[k], recv_sem=recv.at[k],
                                                device_id=to, device_id_type=MESH)

        self.first, self.landed, self.passed, self.returned = [], [], [], []
        for j, (px, py) in enumerate(_other_chips(x, y)):
            theirs = 2 * px + py
            for a in range(nall):
                k = j * nall + a
                if a < na:
                    self.first.append(copy(srcs[a].at[half(a, c), :], dsts[a].at[chip, half(a, c), :],
                                           ici_send, ici_recv, k, (px, py, c)))
                    mine = dsts[a].at[theirs, half(a, c), :]
                    other = dsts[a].at[theirs, half(a, 1 - c), :]
                    self.landed.append(copy(mine, mine, ici_send, ici_recv, k, (px, py, c)))
                    self.passed.append(copy(mine, mine, d2d_send, d2d_recv, j * na + a, (x, y, 1 - c)))
                    self.returned.append(copy(other, other, d2d_send, d2d_recv, j * na + a, (x, y, 1 - c)))
                else:
                    self.first.append(copy(srcs[a], dsts[a].at[chip], ici_send, ici_recv, k, (px, py, c)))
                    got = dsts[a].at[theirs]
                    self.landed.append(copy(got, got, ici_send, ici_recv, k, (px, py, c)))
                    self.passed.append(None)

    def send(self):
        for cp in self.first:
            cp.start()

    def forward(self):
        for arrival, fwd in zip(self.landed, self.passed):
            arrival.wait_recv()
            if fwd is not None:
                fwd.start()

    def finish(self):
        for cp in self.returned:
            cp.wait_recv()
        for cp in self.first + [f for f in self.passed if f is not None]:
            cp.wait_send()


W_ROWS = 1568
G_ROWS = 1552
SHARD_ROWS = D_IN // N_CHIPS
WINDOW_STEP = 1536


def _assemble_w_in(cont):
    cb = COL_BLOCK
    half = WINDOW_STEP
    seam = BF16_ROWS

    def body(c_ref, wa_ref, wf_ref, wb_ref):
        x0 = c_ref[0].astype(F32)
        x1, x2, x3 = (pltpu.roll(c_ref[j].astype(F32), 2 * j, 0) for j in (1, 2, 3))
        wa = jnp.concatenate([x0[:half], x0[half:half + seam] + x1[:seam], x1[seam:half]], axis=0)
        wa_ref[...] = wa.astype(BF16)

        fl = x1[half:half + seam] + x2[:seam]
        row = lax.broadcasted_iota(jnp.int32, fl.shape, 0)
        wf_ref[:seam, :] = jnp.where(row < H, fl, 0.0).astype(BF16)
        wf_ref[seam:, :] = jnp.zeros((LANES - seam, cb), BF16)

        mid = x2[half:half + SUBLANES] + x3[:SUBLANES]
        wb = jnp.concatenate([x2[SUBLANES:half], mid, x3[SUBLANES:half + SUBLANES]], axis=0)
        wb_ref[...] = wb.astype(BF16)

    return _call(
        body, name="assemble_w_in", grid=(D // cb,),
        in_specs=[pl.BlockSpec((N_CHIPS, W_ROWS, cb), lambda i: (0, 0, i))],
        out_specs=[pl.BlockSpec((3 * D, cb), lambda i: (0, i)), pl.BlockSpec((LANES, cb), lambda i: (0, i)),
                   pl.BlockSpec((3 * D, cb), lambda i: (0, i))],
        out_shape=[jax.ShapeDtypeStruct((3 * D, D), BF16), jax.ShapeDtypeStruct((LANES, D), BF16),
                   jax.ShapeDtypeStruct((3 * D, D), BF16)],
        compiler_params=_cparams(("parallel",)),
    )(cont)


def _pair_exchange_windows(grad_t):
    half_g = G_ROWS // 2

    def body(g_ref, got, send_sems, recv_sems):
        x, y, c = _position()
        copies = []
        for j in range(N_CHIPS):
            rows = pl.ds(pl.multiple_of(j * WINDOW_STEP + (1 - c) * half_g, SUBLANES), half_g)
            copies.append(pltpu.make_async_remote_copy(
                src_ref=g_ref.at[rows, :], dst_ref=got.at[j], send_sem=send_sems.at[j], recv_sem=recv_sems.at[j],
                device_id=(x, y, 1 - c), device_id_type=MESH))
        for cp in copies:
            cp.start()
        for cp in copies:
            cp.wait()

    return _call(
        body, name="pair_exchange_w_in",
        in_specs=[HBM_SPEC], out_specs=HBM_SPEC,
        out_shape=jax.ShapeDtypeStruct((N_CHIPS, half_g, D), F32),
        scratch_shapes=[pltpu.SemaphoreType.DMA((N_CHIPS,)), pltpu.SemaphoreType.DMA((N_CHIPS,))],
    )(grad_t)


def _pair_sum(parts, gots, c):
    na = len(parts)

    def body(c_ref, *refs):
        for a in range(na):
            refs[2 * na + a][...] = (refs[a][...] + refs[na + a][...]).astype(BF16)

    mine = [pl.BlockSpec(g.shape, lambda i, c_ref: (0, c_ref[0], 0)) for g in gots]
    whole = [pl.BlockSpec(g.shape, lambda i, c_ref: (0, 0, 0)) for g in gots]
    grid_spec = pltpu.PrefetchScalarGridSpec(
        num_scalar_prefetch=1, grid=(1,), in_specs=mine + whole, out_specs=whole)
    return _call(
        body, name="pair_sum", grid_spec=grid_spec,
        out_shape=[jax.ShapeDtypeStruct(g.shape, BF16) for g in gots],
        compiler_params=_cparams(("arbitrary",), VMEM_BIG),
    )(c.reshape(1), *parts, *gots)


def _pair_sum_windows(grad_t, got, c):
    _, half, C = got.shape
    cb = COL_BLOCK

    def body(c_ref, a_ref, b_ref, o_ref):
        o_ref[0] = (a_ref[...] + b_ref[0]).astype(BF16)

    def mine(j, i, c_ref):
        return ((j * (WINDOW_STEP // SUBLANES) + c_ref[0] * (half // SUBLANES)) * SUBLANES, i * cb)

    spec = pl.BlockSpec((1, half, cb), lambda j, i, c_ref: (j, 0, i))
    grid_spec = pltpu.PrefetchScalarGridSpec(
        num_scalar_prefetch=1, grid=(N_CHIPS, C // cb),
        in_specs=[pl.BlockSpec((pl.Element(half), pl.Element(cb)), mine), spec], out_specs=spec)
    return _call(
        body, name="pair_sum_w_in", grid_spec=grid_spec,
        out_shape=jax.ShapeDtypeStruct((N_CHIPS, half, C), BF16),
        compiler_params=_cparams(("parallel", "parallel")),
    )(c.reshape(1), grad_t, got)


def _chip_sum(own, got, chip, name):
    _, half, C = got.shape
    cb = min(C, COL_BLOCK)

    def body(chip_ref, own_ref, g_ref, o_ref):
        for me in range(N_CHIPS):
            @pl.when(chip_ref[0] == me)
            def _(me=me):
                terms = [own_ref[0] if k == me else g_ref[k] for k in range(N_CHIPS)]
                acc = terms[0].astype(F32) + terms[1].astype(F32)
                acc = acc + terms[2].astype(F32)
                o_ref[...] = acc + terms[3].astype(F32)

    grid_spec = pltpu.PrefetchScalarGridSpec(
        num_scalar_prefetch=1, grid=(C // cb,),
        in_specs=[pl.BlockSpec((1, half, cb), lambda i, chip_ref: (chip_ref[0], 0, i)),
                  pl.BlockSpec((N_CHIPS, half, cb), lambda i, chip_ref: (0, 0, i))],
        out_specs=pl.BlockSpec((half, cb), lambda i, chip_ref: (0, i)))
    return _call(
        body, name=name, grid_spec=grid_spec,
        out_shape=jax.ShapeDtypeStruct((half, C), F32),
        compiler_params=_cparams(("parallel",)),
    )(chip.reshape(1), own, got)


def _final_exchange(halves, g):
    na = len(halves)
    rows = g.shape[0]
    per = rows // N_DEV

    def body(*refs):
        srcs, g_ref = refs[:na], refs[na]
        dsts, out_ref = refs[na + 1:2 * na + 1], refs[2 * na + 1]
        got_ref, s1, r1, s2, r2, swap_send, swap_recv = refs[2 * na + 2:]
        x, y, c = _position()
        swaps = [pltpu.make_async_remote_copy(
            src_ref=srcs[a], dst_ref=dsts[a], send_sem=swap_send.at[a], recv_sem=swap_recv.at[a],
            device_id=(x, y, 1 - c), device_id_type=MESH) for a in range(na)]
        for cp in swaps:
            cp.start()
        me = 4 * x + 2 * y + c
        mine = pl.ds(pl.multiple_of(me * per, SUBLANES), per)
        peers = []
        for j in range(1, N_DEV):
            px = 1 - x if j & 4 else x
            py = 1 - y if j & 2 else y
            pc = 1 - c if j & 1 else c
            peers.append((px, py, pc))

        first = []
        for j, (px, py, pc) in enumerate(peers):
            theirs = pl.ds(pl.multiple_of((4 * px + 2 * py + pc) * per, SUBLANES), per)
            first.append(pltpu.make_async_remote_copy(
                src_ref=g_ref.at[theirs, :], dst_ref=got_ref.at[me], send_sem=s1.at[j], recv_sem=r1.at[j],
                device_id=(px, py, pc), device_id_type=MESH))
        for cp in first:
            cp.start()
        got_ref[me] = g_ref[mine, :]
        for cp in first:
            cp.wait()
        total = got_ref[0]
        for d in range(1, N_DEV):
            total = total + got_ref[d]
        out_ref[mine, :] = total

        second = []
        for j, peer in enumerate(peers):
            second.append(pltpu.make_async_remote_copy(
                src_ref=out_ref.at[mine, :], dst_ref=out_ref.at[mine, :], send_sem=s2.at[j], recv_sem=r2.at[j],
                device_id=peer, device_id_type=MESH))
        for cp in second:
            cp.start()
        for cp in second + swaps:
            cp.wait()

    sems = pltpu.SemaphoreType.DMA((N_DEV - 1,))
    swap_sems = pltpu.SemaphoreType.DMA((na,))
    outs = _call(
        body, name="final_exchange", in_hbm=False,
        in_specs=[HBM_SPEC] * na + [VMEM_SPEC], out_specs=[HBM_SPEC] * na + [VMEM_SPEC],
        out_shape=[jax.ShapeDtypeStruct(s.shape, s.dtype) for s in halves] + [jax.ShapeDtypeStruct(g.shape, F32)],
        scratch_shapes=[pltpu.VMEM((N_DEV, per, LANES), F32), sems, sems, sems, sems, swap_sems, swap_sems],
    )(*halves, g)
    return outs[:na], outs[na]


def _adamw_math(g, w, m, v):
    m2 = ADAM_B1 * m + (1.0 - ADAM_B1) * g
    v2 = ADAM_B2 * v + (1.0 - ADAM_B2) * (g * g)
    m_hat = m2 / (1.0 - ADAM_B1 ** ADAM_STEP)
    v_hat = v2 / (1.0 - ADAM_B2 ** ADAM_STEP)
    delta = (-ADAM_LR) * (m_hat / (jnp.sqrt(v_hat) + ADAM_EPS) + ADAM_WD * w)
    return delta, m2, v2


ADAMW_BLOCK_BYTES = 1 << 20


def _adamw_big(g, w, m, v, name):
    R, C = g.shape
    bc = min(C, max(LANES, ADAMW_BLOCK_BYTES // (4 * R) // LANES * LANES))

    def body(g_ref, w_ref, m_ref, v_ref, d_ref, m2_ref, v2_ref):
        d_ref[...], m2_ref[...], v2_ref[...] = _adamw_math(g_ref[...], w_ref[...], m_ref[...], v_ref[...])

    spec = pl.BlockSpec((R, bc), lambda j: (0, j))
    out = jax.ShapeDtypeStruct((R, C), F32)
    return _call(
        body, name=name, grid=(C // bc,),
        in_specs=[spec] * 4, out_specs=[spec] * 3, out_shape=[out] * 3,
        compiler_params=_cparams(("parallel",)),
    )(g, w, m, v)


def _adamw_small(gs, ws, ms, vs):
    n = len(gs)

    def body(*refs):
        for a in range(n):
            g_ref, w_ref, m_ref, v_ref = (refs[k * n + a] for k in range(4))
            d_ref, m2_ref, v2_ref = (refs[(4 + k) * n + a] for k in range(3))
            d_ref[...], m2_ref[...], v2_ref[...] = _adamw_math(g_ref[...], w_ref[...], m_ref[...], v_ref[...])

    outs = [jax.ShapeDtypeStruct(w.shape, F32) for w in ws]
    specs = [_const_spec(w.shape) for w in ws]
    return _call(
        body, name="adamw_small", grid=(1,),
        in_specs=specs * 4, out_specs=specs * 3, out_shape=outs * 3,
    )(*gs, *ws, *ms, *vs)


def _late_weights(st_out, st_ple, st_gate, st_conv):
    return st_out.reshape(DMIX, D), _from_chip_cols(st_ple), st_gate.reshape(D, D), _from_chip_cols(st_conv)


def _local_step(x, p, tgt, w_a, w_f, w_b, late, b_f, pre_gain, post_gain, conv_b,
                w_rgate, b_rgate, w_igate, b_igate, lam, gain_a, gain_l, ple_gain, b_gate,
                gather_late=False, early_reduce=None, w_in_reduce=None):
    b_f_pad = jnp.pad(b_f, ((0, 0), (0, LANES - H)))
    w_r = w_rgate.astype(BF16)
    w_i = w_igate.astype(BF16)

    xn, q_aug, k_aug, v_aug, g_attn, x_lru, g_lru, flb, vt_aug = _in_proj(x, pre_gain, w_a, w_f, w_b, b_f_pad)
    if gather_late:
        o, qx, stacks = _attn_fwd(q_aug, k_aug, vt_aug, late[:3], late[3:])
        late = _late_weights(*stacks)
    else:
        o, qx, _ = _attn_fwd(q_aug, k_aug, vt_aug)
    w_out_b, w_ple_b, w_gate_b, conv_w = late
    ycat, xc, h, ycat_t = _branches_fwd(o, g_attn, x_lru, g_lru, gain_a, gain_l, conv_w, conv_b, w_r, b_rgate, w_i,
                                        b_igate, lam)
    dh1, dycat, dmix, h1_t, dgp, p_t, dpe, acc_t = _tail(ycat, x, p, tgt, w_out_b, post_gain, w_ple_b, ple_gain,
                                                         w_gate_b, b_gate)
    do_aug, dg_attn, dg_lru, dh, acc_b, gw_out, gw_gate, gw_ple = _branches_bwd(
        dycat, o, g_attn, h, g_lru, gain_a, gain_l, ycat_t, dmix, h1_t, dgp, p_t, dpe)
    late_grads = [gw_out, gw_ple, gw_gate]
    if early_reduce is None:
        dx_lru, gw_r, gw_i, acc_l, _ = _lru_bwd(dh, h, xc, x_lru, conv_w, w_r, b_rgate, w_i, b_igate, lam)
    else:
        parts = [gw_out.reshape(N_CHIPS, DMIX // N_CHIPS, D), _by_chip_cols(gw_ple),
                 gw_gate.reshape(N_CHIPS, D // N_CHIPS, D)]
        dx_lru, gw_r, gw_i, acc_l, got = _lru_bwd(dh, h, xc, x_lru, conv_w, w_r, b_rgate, w_i, b_igate, lam, parts)
        sent = _pair_sum(parts, got, early_reduce)
    if early_reduce is None:
        dq, dk, dv, dc_heads, _ = _attn_bwd(q_aug, qx, k_aug, v_aug, do_aug)
    else:
        dq, dk, dv, dc_heads, received = _attn_bwd(q_aug, qx, k_aug, v_aug, do_aug, sent)
        late_grads = list(zip(sent, received))
    dfl, acc_f = _fgate_bwd(dc_heads, flb)
    dz = (dq, dk, dv, dg_attn, dx_lru, dg_lru)
    grad_t = _dw_in_t(dz, dfl, xn)
    if w_in_reduce is None:
        grad_x, acc_x, _ = _dx(dz, dfl, w_a, w_f, w_b, x, pre_gain, dh1)
    else:
        sent = w_in_reduce(grad_t)
        grad_x, acc_x, (received,) = _dx(dz, dfl, w_a, w_f, w_b, x, pre_gain, dh1, [sent])
        grad_t = (sent, received)

    grads = dict(
        w_in_t=grad_t,
        w_out=late_grads[0],
        w_ple=late_grads[1],
        w_ple_gate=late_grads[2],
        w_rgate=gw_r,
        w_igate=gw_i,
        b_f=acc_f[0:1, :H],
        pre_gain=acc_x[0:1],
        post_gain=acc_t[0:1],
        conv_w=acc_l[0:4],
        conv_b=acc_l[4:5],
        b_rgate=acc_l[5:6],
        b_igate=acc_l[6:7],
        lru_lambda=acc_l[7:8],
        attn_out_gain=acc_b[0:1],
        lru_out_gain=acc_b[1:2],
        ple_gain=acc_t[1:2],
        b_ple_gate=acc_t[2:3],
    )
    loss = jnp.sum(acc_t[3])
    return loss, grad_x, grads


SMALL_ROWS = ["b_f", "pre_gain", "post_gain", "conv_w", "conv_b", "b_rgate", "b_igate", "lru_lambda",
              "attn_out_gain", "lru_out_gain", "ple_gain", "b_ple_gate"]
WEIGHTS = ["w_in", "b_f", "pre_gain", "post_gain", "conv_w", "conv_b", "w_rgate", "b_rgate", "w_igate", "b_igate",
           "lru_lambda", "attn_out_gain", "lru_out_gain", "w_out", "w_ple", "ple_gain", "w_ple_gate", "b_ple_gate"]
SHARDED = ["w_in", "w_out", "w_ple", "w_ple_gate"]


def _by_chip_cols(g):
    r, cols = g.shape
    return g.reshape(r, N_CHIPS, cols // N_CHIPS).transpose(1, 0, 2)


def _from_chip_cols(s):
    n, r, cols = s.shape
    return s.transpose(1, 0, 2).reshape(r, n * cols)


def kernel(x, p, w_in, b_f, pre_gain, post_gain, conv_w, conv_b, w_rgate, b_rgate, w_igate, b_igate, lru_lambda, attn_out_gain, lru_out_gain, w_out, w_ple, ple_gain, w_ple_gate, b_ple_gate, loss_target, m_w_in, m_b_f, m_pre_gain, m_post_gain, m_conv_w, m_conv_b, m_w_rgate, m_b_rgate, m_w_igate, m_b_igate, m_lru_lambda, m_attn_out_gain, m_lru_out_gain, m_w_out, m_w_ple, m_ple_gain, m_w_ple_gate, m_b_ple_gate, v_w_in, v_b_f, v_pre_gain, v_post_gain, v_conv_w, v_conv_b, v_w_rgate, v_b_rgate, v_w_igate, v_b_igate, v_lru_lambda, v_attn_out_gain, v_lru_out_gain, v_w_out, v_w_ple, v_ple_gain, v_w_ple_gate, v_b_ple_gate):
    w = dict(w_in=w_in, b_f=b_f, pre_gain=pre_gain, post_gain=post_gain, conv_w=conv_w, conv_b=conv_b,
             w_rgate=w_rgate, b_rgate=b_rgate, w_igate=w_igate, b_igate=b_igate, lru_lambda=lru_lambda,
             attn_out_gain=attn_out_gain, lru_out_gain=lru_out_gain, w_out=w_out, w_ple=w_ple, ple_gain=ple_gain,
             w_ple_gate=w_ple_gate, b_ple_gate=b_ple_gate)
    m = dict(w_in=m_w_in, b_f=m_b_f, pre_gain=m_pre_gain, post_gain=m_post_gain, conv_w=m_conv_w, conv_b=m_conv_b,
             w_rgate=m_w_rgate, b_rgate=m_b_rgate, w_igate=m_w_igate, b_igate=m_b_igate, lru_lambda=m_lru_lambda,
             attn_out_gain=m_attn_out_gain, lru_out_gain=m_lru_out_gain, w_out=m_w_out, w_ple=m_w_ple,
             ple_gain=m_ple_gain, w_ple_gate=m_w_ple_gate, b_ple_gate=m_b_ple_gate)
    v = dict(w_in=v_w_in, b_f=v_b_f, pre_gain=v_pre_gain, post_gain=v_post_gain, conv_w=v_conv_w, conv_b=v_conv_b,
             w_rgate=v_w_rgate, b_rgate=v_b_rgate, w_igate=v_w_igate, b_igate=v_b_igate, lru_lambda=v_lru_lambda,
             attn_out_gain=v_attn_out_gain, lru_out_gain=v_lru_out_gain, w_out=v_w_out, w_ple=v_w_ple,
             ple_gain=v_ple_gain, w_ple_gate=v_w_ple_gate, b_ple_gate=v_b_ple_gate)
    xi, yi, ci = _position()
    chip = 2 * xi + yi

    w_in_t, m_in_t, v_in_t = (jnp.swapaxes(t[0], 0, 1) for t in (w_in, m_w_in, v_w_in))
    window = jnp.pad(w_in_t.astype(BF16), ((0, W_ROWS - SHARD_ROWS), (0, 0)))

    (st_in,) = _gather_shards([window], [])
    w_a, w_f, w_b = _assemble_w_in(st_in)
    late_shards = (w_out[0].astype(BF16), w_ple[0].astype(BF16), w_ple_gate[0].astype(BF16), conv_w[0])

    loss, grad_x, g = _local_step(
        x[0], p[0, 0], loss_target[0], w_a, w_f, w_b, late_shards, b_f, pre_gain, post_gain,
        conv_b, w_rgate[0], b_rgate, w_igate[0], b_igate, lru_lambda, attn_out_gain, lru_out_gain, ple_gain,
        b_ple_gate, gather_late=True, early_reduce=ci,
        w_in_reduce=lambda grad_t: _pair_sum_windows(grad_t, _pair_exchange_windows(grad_t), ci))

    sums = [g["w_in_t"][0]] + [g[n][0] for n in SHARDED[1:]]
    recv = [g["w_in_t"][1]] + [g[n][1] for n in SHARDED[1:]]
    halves = [_chip_sum(sums[a], recv[a], chip, "chip_sum_%d" % a) for a in range(4)]

    rows = [jnp.pad(g["b_f"], ((0, 0), (0, D - H)))] + [g[n] for n in SMALL_ROWS[1:]]
    rows.append(jnp.pad(loss.reshape(1, 1), ((0, 0), (0, D - 1))))
    packed = jnp.concatenate([g["w_rgate"].reshape(NB * LANES, LANES), g["w_igate"].reshape(NB * LANES, LANES),
                              jnp.concatenate(rows, axis=0).reshape(LANES, LANES)], axis=0)
    theirs, summed = _final_exchange(halves, packed)
    full = [jnp.concatenate([jnp.where(ci == 0, a, b), jnp.where(ci == 0, b, a)], axis=0)
            for a, b in zip(halves, theirs)]
    red = dict(zip(SHARDED, full))
    red["w_in"] = lax.dynamic_slice_in_dim(red["w_in"], 2 * chip, SHARD_ROWS, axis=0)
    red["w_rgate"] = summed[:D].reshape(1, NB, LANES, LANES)
    red["w_igate"] = summed[D:2 * D].reshape(1, NB, LANES, LANES)
    vec = summed[2 * D:].reshape(16, D)
    loss = vec[15, 0]
    r0 = 0
    for n in SMALL_ROWS:
        nr = 4 if n == "conv_w" else 1
        red[n] = vec[r0:r0 + nr]
        r0 += nr
    red["b_f"] = red["b_f"][:, :H]
    red["conv_w"] = lax.dynamic_slice_in_dim(red["conv_w"], chip * (D // N_CHIPS), D // N_CHIPS, axis=1)[None]

    delta, new_m, new_v = {}, {}, {}
    outs_in = _adamw_big(red["w_in"], w_in_t, m_in_t, v_in_t, "adamw_w_in")
    delta["w_in"], new_m["w_in"], new_v["w_in"] = (jnp.swapaxes(t, 0, 1)[None] for t in outs_in)
    red["w_in"] = jnp.swapaxes(red["w_in"], 0, 1)[None]
    for n in SHARDED[1:]:
        delta[n], new_m[n], new_v[n] = (t[None] for t in _adamw_big(red[n], w[n][0], m[n][0], v[n][0], "adamw_" + n))
        red[n] = red[n][None]
    small = [n for n in WEIGHTS if n not in SHARDED]
    outs = _adamw_small([red[n] for n in small], [w[n] for n in small], [m[n] for n in small],
                        [v[n] for n in small])
    ns = len(small)
    for a, n in enumerate(small):
        delta[n], new_m[n], new_v[n] = outs[a], outs[ns + a], outs[2 * ns + a]

    return (loss, grad_x[None], *[red[n] for n in WEIGHTS], *[delta[n] for n in WEIGHTS],
            *[new_m[n] for n in WEIGHTS], *[new_v[n] for n in WEIGHTS])
```

```python
import jax
import jax.numpy as jnp
import numpy as np
from jax import lax
from jax.experimental import pallas as pl
from jax.experimental.pallas import tpu as pltpu

F32 = jnp.float32
BF16 = jnp.bfloat16

D = 1024
H = 8
DH = 128
NB = 8
DPLE = 256
DMIX = 2 * D
D_IN = 4 * D + H + 2 * D
FL0 = 3 * D
RMS_EPS = 1e-6
LRU_C = 8.0
NEG = -1e30
LANES = 128
SUBLANES = 8
BF16_ROWS = 16
COL_BLOCK = 256
DW_TOKENS = 2048

ADAM_LR = 0.001
ADAM_B1 = 0.9
ADAM_B2 = 0.999
ADAM_EPS = 1e-08
ADAM_WD = 0.01
ADAM_STEP = 10

TM = 256
TA = 512
FWD_HEADS = 8
BWD_HEADS = 2
VMEM_BIG = 56 * 1024 * 1024
VMEM_MID = 40 * 1024 * 1024

MESH = pl.DeviceIdType.MESH
N_CHIPS = 4
N_DEV = 8


def _call(body, *, out_shape, in_hbm=True, **kwargs):
    if not in_hbm:
        return pl.pallas_call(body, out_shape=out_shape, **kwargs)

    def pin(shape):
        return pltpu.HBM(shape.shape, shape.dtype) if isinstance(shape, jax.ShapeDtypeStruct) else shape

    fn = pl.pallas_call(body, out_shape=jax.tree.map(pin, out_shape), **kwargs)

    def run(*args):
        return fn(*[a if a.dtype == jnp.int32 else pltpu.with_memory_space_constraint(a, pltpu.HBM) for a in args])

    return run


def _cparams(sem, vmem=VMEM_MID):
    return pltpu.CompilerParams(dimension_semantics=sem, vmem_limit_bytes=vmem)


def _sigmoid(x):
    return 0.5 * jnp.tanh(0.5 * x) + 0.5


def _rstd(x):
    return lax.rsqrt(jnp.mean(x * x, axis=-1, keepdims=True) + RMS_EPS)


def _rms_bwd(t, xhat, rstd):
    return rstd * (t - xhat * jnp.mean(t * xhat, axis=-1, keepdims=True))


def _dot(a, b):
    return jnp.dot(a, b, preferred_element_type=F32)


def _dot_nt(a, b):
    return lax.dot_general(a, b, (((1,), (1,)), ((), ())), preferred_element_type=F32)


def _dot_tn(a, b):
    return lax.dot_general(a, b, (((0,), (0,)), ((), ())), preferred_element_type=F32)


def _dot_exact(a, b):
    return jnp.dot(a, b, preferred_element_type=F32, precision=lax.Precision.HIGHEST)


def _shift_down(x, j, halo):
    rolled = pltpu.roll(x, j, 0)
    row = lax.broadcasted_iota(jnp.int32, halo.shape, 0)
    top = jnp.where(row < j, pltpu.roll(halo, j, 0), rolled[:SUBLANES])
    return jnp.concatenate([top, rolled[SUBLANES:]], axis=0)


def _shift_up(x, j, nxt):
    tm = x.shape[0]
    rolled = pltpu.roll(x, tm - j, 0)
    row = lax.broadcasted_iota(jnp.int32, nxt.shape, 0)
    bot = jnp.where(row >= SUBLANES - j, pltpu.roll(nxt, SUBLANES - j, 0), rolled[tm - SUBLANES:])
    return jnp.concatenate([rolled[:tm - SUBLANES], bot], axis=0)


def _scan_fwd_into(a, u, carry, h_ref):
    tm, width = a.shape
    groups = (tm // SUBLANES, SUBLANES, width)
    a, u = a.reshape(groups), u.reshape(groups)
    sub = lax.broadcasted_iota(jnp.int32, groups, 1)
    d = 1
    while d < SUBLANES:
        keep = sub >= d
        a_s = jnp.where(keep, pltpu.roll(a, d, 1), 1.0)
        u_s = jnp.where(keep, pltpu.roll(u, d, 1), 0.0)
        u = u + a * u_s
        a = a * a_s
        d *= 2
    a, u = a.reshape(tm, width), u.reshape(tm, width)
    for g in range(tm // SUBLANES):
        rows = slice(g * SUBLANES, (g + 1) * SUBLANES)
        h_ref[rows, :] = u[rows] + a[rows] * carry
        carry = h_ref[(g + 1) * SUBLANES - 1:(g + 1) * SUBLANES, :]
    return carry


def _scan_bwd_into(b, u, g_ref):
    tm, width = b.shape
    groups = (tm // SUBLANES, SUBLANES, width)
    b, u = b.reshape(groups), u.reshape(groups)
    sub = lax.broadcasted_iota(jnp.int32, groups, 1)
    d = 1
    while d < SUBLANES:
        keep = sub < SUBLANES - d
        b_s = jnp.where(keep, pltpu.roll(b, SUBLANES - d, 1), 1.0)
        u_s = jnp.where(keep, pltpu.roll(u, SUBLANES - d, 1), 0.0)
        u = u + b * u_s
        b = b * b_s
        d *= 2
    b, u = b.reshape(tm, width), u.reshape(tm, width)
    nxt = jnp.zeros((1, width), F32)
    for g in reversed(range(tm // SUBLANES)):
        rows = slice(g * SUBLANES, (g + 1) * SUBLANES)
        g_ref[rows, :] = u[rows] + b[rows] * nxt
        nxt = g_ref[g * SUBLANES:g * SUBLANES + 1, :]


def _gate_pre(xc, w_ref):
    outs = []
    for n in range(NB):
        outs.append(_dot(xc[:, n * LANES:(n + 1) * LANES].astype(BF16), w_ref[n]))
    return jnp.concatenate(outs, axis=1)


def _gate_pre_t(d, w_ref):
    outs = []
    for n in range(NB):
        outs.append(_dot_nt(d[:, n * LANES:(n + 1) * LANES].astype(BF16), w_ref[n]))
    return jnp.concatenate(outs, axis=1)


def _softplus_neg(lam):
    return jnp.maximum(-lam, 0.0) + jnp.log(1.0 + jnp.exp(-jnp.abs(lam)))


def _row_spec(tm, width):
    return pl.BlockSpec((tm, width), lambda i: (i, 0))


def _const_spec(shape):
    nd = len(shape)
    return pl.BlockSpec(shape, lambda *_: (0,) * nd)


def _weight_spec(shape):
    nd = len(shape)
    return pl.BlockSpec(shape, lambda *_: (0,) * nd, pipeline_mode=pl.Buffered(1))


AUG = 2 * DH
LOG2E = 1.4426950408889634
LN2 = 0.6931471805599453
Q_SCALE = DH ** -0.5 * LOG2E


def _split3(x):
    hi = x.astype(BF16)
    r1 = x - hi.astype(F32)
    mid = r1.astype(BF16)
    lo = (r1 - mid.astype(F32)).astype(BF16)
    return hi, mid, lo


def _extras(col, ones_from):
    t = col.shape[0]
    hi, mid, lo = _split3(jnp.broadcast_to(col, (t, LANES)))
    lane = lax.broadcasted_iota(jnp.int32, (t, LANES), 1)
    rest = jnp.zeros((t, LANES), BF16)
    if ones_from is not None:
        rest = jnp.where((lane >= ones_from) & (lane < ones_from + 3), 1.0, 0.0).astype(BF16)
    return jnp.where(lane == 0, hi, jnp.where(lane == 1, mid, jnp.where(lane == 2, lo, rest)))


def _selectors():
    sel_q = np.zeros((3 * LANES, H * LANES), np.float32)
    sel_k = np.zeros((3 * LANES, H * LANES), np.float32)
    for hd in range(H):
        for piece in range(3):
            sel_q[piece * LANES + hd, hd * LANES + piece] = 1.0
            sel_k[piece * LANES + hd, hd * LANES + 3 + piece] = -1.0
    return jnp.asarray(sel_q, BF16), jnp.asarray(sel_k, BF16)


def _in_proj(x, pre_gain, w_a, w_f, w_b, b_f_pad):
    T = x.shape[0]
    tm = TM
    sel_q, sel_k = _selectors()

    def body(x_ref, g_ref, wa_ref, wf_ref, wb_ref, bf_ref, sq_ref, sk_ref,
             xn_ref, qa_ref, ka_ref, va_ref, ga_ref, xl_ref, gl_ref, flb_ref, vt_ref, c_s, carry):
        @pl.when(pl.program_id(0) == 0)
        def _():
            carry[...] = jnp.zeros_like(carry)

        xv = x_ref[...]
        xn = (xv * _rstd(xv) * g_ref[...]).astype(BF16)
        xn_ref[...] = xn
        for s, o_ref in enumerate((ga_ref, xl_ref, gl_ref)):
            o_ref[...] = _dot_nt(xn, wb_ref[s * D:(s + 1) * D, :]).astype(o_ref.dtype)
        flb = _dot_nt(xn, wf_ref[...]) + bf_ref[...]
        flb_ref[...] = flb
        lane = lax.broadcasted_iota(jnp.int32, flb.shape, 1)
        ls = jnp.where(lane < H, jnp.minimum(flb, 0.0) - jnp.log(1.0 + jnp.exp(-jnp.abs(flb))), 0.0)
        r = lax.broadcasted_iota(jnp.int32, (tm, tm), 0)
        c = lax.broadcasted_iota(jnp.int32, (tm, tm), 1)
        cs = _dot_exact((c <= r).astype(F32), ls) + carry[...]
        c_s[...] = cs
        carry[...] = c_s[tm - 1:tm, :]

        pieces = jnp.concatenate(_split3(cs * LOG2E), axis=1)
        ones_q = jnp.where((lane >= 3) & (lane < 6), 1.0, 0.0)
        ones_k = jnp.where(lane < 3, 1.0, 0.0)
        zq = _dot_nt(xn, wa_ref[0:D, :]) * Q_SCALE
        zk = _dot_nt(xn, wa_ref[D:2 * D, :])
        zv = _dot_nt(xn, wa_ref[2 * D:3 * D, :])
        ex_q = _dot(pieces, sq_ref[...])
        ex_k = _dot(pieces, sk_ref[...])
        for hd in range(H):
            head = slice(hd * DH, (hd + 1) * DH)
            lo, hi = hd * AUG, hd * AUG + DH
            qa_ref[:, lo:hi] = zq[:, head].astype(BF16)
            qa_ref[:, hi:hi + DH] = (ex_q[:, head] + ones_q).astype(BF16)
            ka_ref[:, lo:hi] = zk[:, head].astype(BF16)
            ka_ref[:, hi:hi + DH] = (ex_k[:, head] + ones_k).astype(BF16)
            va_ref[:, lo:hi] = zv[:, head].astype(BF16)
            va_ref[:, hi:hi + DH] = ones_k.astype(BF16)
            vt_ref[lo:hi, :] = jnp.transpose(zv[:, head]).astype(BF16)
            vt_ref[hi:hi + DH, :] = jnp.where(lax.broadcasted_iota(jnp.int32, (DH, tm), 0) < 3, 1.0, 0.0).astype(BF16)

    bf = jax.ShapeDtypeStruct((T, D), BF16)
    aug = jax.ShapeDtypeStruct((T, H * AUG), BF16)
    f32 = jax.ShapeDtypeStruct((T, D), F32)
    sel_spec = _const_spec((3 * LANES, H * LANES))
    return _call(
        body, name="in_proj", grid=(T // tm,),
        in_specs=[_row_spec(tm, D), _const_spec((1, D)), _const_spec((3 * D, D)), _const_spec((LANES, D)),
                  _const_spec((3 * D, D)), _const_spec((1, LANES)), sel_spec, sel_spec],
        out_specs=[_row_spec(tm, D)] + [_row_spec(tm, H * AUG)] * 3 + [_row_spec(tm, D)] * 3 + [_row_spec(tm, LANES)]
        + [pl.BlockSpec((H * AUG, tm), lambda i: (0, i))],
        out_shape=[bf, aug, aug, aug, f32, f32, f32, jax.ShapeDtypeStruct((T, LANES), F32),
                   jax.ShapeDtypeStruct((H * AUG, T), BF16)],
        scratch_shapes=[pltpu.VMEM((tm, LANES), F32), pltpu.VMEM((1, LANES), F32)],
        compiler_params=_cparams(("arbitrary",), VMEM_BIG),
    )(x, pre_gain, w_a, w_f, w_b, b_f_pad, sel_q, sel_k)


def _causal_pairs(n, q_major):
    if q_major:
        pairs = [(qi, ki) for qi in range(n) for ki in range(qi + 1)]
    else:
        pairs = [(ki, qi) for ki in range(n) for qi in range(ki, n)]
    return (jnp.asarray([a for a, _ in pairs], jnp.int32), jnp.asarray([b for _, b in pairs], jnp.int32))


def _attn_fwd(q_aug, k_aug, vt_aug, shards=(), whole=()):
    T = q_aug.shape[0]
    t = TA
    n = T // t
    hp = FWD_HEADS
    heads = range(hp)
    qi_tab, ki_tab = _causal_pairs(n, q_major=True)
    na, nall = len(shards), len(shards) + len(whole)
    n_h, n_j = H // hp, qi_tab.shape[0]

    def body(qi_ref, ki_ref, q_ref, k_ref, vt_ref, *rest):
        srcs, rest = rest[:nall], rest[nall:]
        o_ref, qx_ref = rest[:2]
        dsts, rest = rest[2:2 + nall], rest[2 + nall:]
        m_s, acc_s = rest[:2]
        h = pl.program_id(0)
        j = pl.program_id(1)
        qi = qi_ref[j]
        ki = ki_ref[j]

        if nall:
            gather = _GatherPlan(srcs, dsts, rest[2:], na)
            step = h * n_j + j
            pl.when(step == 0)(gather.send)
            pl.when(step == n_h * n_j // 2)(gather.forward)
            pl.when(step == n_h * n_j - 1)(gather.finish)

        @pl.when(ki == 0)
        def _():
            m_s[...] = jnp.full(m_s.shape, NEG, F32)
            acc_s[...] = jnp.zeros_like(acc_s)

        def step(on_diagonal):
            cols = [slice(a * AUG, (a + 1) * AUG) for a in heads]
            if on_diagonal:
                krow = lax.broadcasted_iota(jnp.int32, (t, t), 0)
                qcol = lax.broadcasted_iota(jnp.int32, (t, t), 1)
            def logits(a):
                st = _dot_nt(k_ref[:, cols[a]], q_ref[:, cols[a]])
                return jnp.where(krow <= qcol, st, NEG) if on_diagonal else st

            st_next = logits(0)
            for a in heads:
                st = st_next
                if a + 1 < hp:
                    st_next = logits(a + 1)
                m_prev = m_s[a]
                m_new = jnp.maximum(m_prev, jnp.max(st, axis=0, keepdims=True))
                pt = jnp.exp2(st - m_new).astype(BF16)
                acc_s[a] = jnp.exp2(m_prev - m_new) * acc_s[a] + _dot(vt_ref[cols[a], :], pt)
                m_s[a] = m_new

        @pl.when(ki < qi)
        def _():
            step(False)

        @pl.when(ki == qi)
        def _():
            step(True)
            piece = lax.broadcasted_iota(jnp.int32, (DH, t), 0)
            for a in heads:
                l = acc_s[a, DH:DH + 1, :]
                ex = jnp.transpose(q_ref[:, a * AUG + DH:(a + 1) * AUG].astype(F32))
                c2 = jnp.sum(jnp.where(piece < 3, ex, 0.0), axis=0, keepdims=True)
                hi, mid, lo = _split3(jnp.broadcast_to(c2 - (m_s[a] + jnp.log(l) * LOG2E), (DH, t)))
                ones = jnp.where((piece >= 3) & (piece < 6), 1.0, 0.0).astype(BF16)
                ex_t = jnp.where(piece == 0, hi, jnp.where(piece == 1, mid, jnp.where(piece == 2, lo, ones)))
                o_ref[:, a * DH:(a + 1) * DH] = jnp.transpose(acc_s[a, :DH, :] / l)
                qx_ref[:, a * DH:(a + 1) * DH] = jnp.transpose(ex_t.astype(F32)).astype(BF16)

    q_spec = pl.BlockSpec((t, hp * AUG), lambda h, j, qi_ref, ki_ref: (qi_ref[j], h))
    k_spec = pl.BlockSpec((t, hp * AUG), lambda h, j, qi_ref, ki_ref: (ki_ref[j], h))
    vt_spec = pl.BlockSpec((hp * AUG, t), lambda h, j, qi_ref, ki_ref: (h, ki_ref[j]))
    out_spec = pl.BlockSpec((t, hp * DH), lambda h, j, qi_ref, ki_ref: (qi_ref[j], h))
    arrs = list(shards) + list(whole)
    grid_spec = pltpu.PrefetchScalarGridSpec(
        num_scalar_prefetch=2, grid=(n_h, n_j),
        in_specs=[q_spec, k_spec, vt_spec] + [HBM_SPEC] * nall, out_specs=[out_spec, out_spec] + [HBM_SPEC] * nall,
        scratch_shapes=[pltpu.VMEM((hp, 1, t), F32), pltpu.VMEM((hp, AUG, t), F32)]
        + (_gather_semaphores(na, nall) if nall else []))
    outs = _call(
        body, name="attn_fwd", grid_spec=grid_spec,
        out_shape=[jax.ShapeDtypeStruct((T, D), F32), jax.ShapeDtypeStruct((T, D), BF16)] + _gather_out_shapes(arrs),
        compiler_params=_cparams(("arbitrary", "arbitrary"), VMEM_BIG),
    )(qi_tab, ki_tab, q_aug, k_aug, vt_aug, *arrs)
    return outs[0], outs[1], _place_own(outs[2:], arrs)


def _sigmoid_small(x):
    e = jnp.exp(jnp.minimum(x, 0.0))
    return jnp.where(x < -8.0, e * (1.0 - e), _sigmoid(x))


def _lru_gates(xc, wr_ref, br_ref, wi_ref, bi_ref, lam_ref):
    r = _sigmoid_small(_gate_pre(xc, wr_ref) + br_ref[...])
    ig = _sigmoid(_gate_pre(xc, wi_ref) + bi_ref[...])
    sp = _softplus_neg(lam_ref[...])
    la = (-LRU_C) * r * sp
    a = jnp.exp(la)
    y = -jnp.tanh(la) * (a * a + 1.0)
    return r, ig, sp, a, jnp.sqrt(y), lax.rsqrt(y)


def _branches_fwd(o, g_attn, x_lru, g_lru, gain_a, gain_l, conv_w, conv_b, w_r, b_r, w_i, b_i, lam):
    T = o.shape[0]
    tm = TM

    def body(o_ref, ga_ref, xl_ref, gl_ref, gna_ref, gnl_ref, cw_ref, cb_ref, wr_ref, br_ref, wi_ref, bi_ref,
             lam_ref, ycat_ref, xc_ref, h_ref, halo_s, hc_s):
        @pl.when(pl.program_id(0) == 0)
        def _():
            halo_s[...] = jnp.zeros_like(halo_s)
            hc_s[...] = jnp.zeros_like(hc_s)

        ov = o_ref[...]
        ga = ga_ref[...]
        ya = ov * _rstd(ov) * gna_ref[...] * (ga * _sigmoid(ga))
        ycat_ref[:, :D] = ya.astype(BF16)

        xl = xl_ref[...]
        halo = halo_s[...]
        xc = xl * cw_ref[3:4, :] + cb_ref[...]
        for j in range(3):
            xc = xc + _shift_down(xl, 3 - j, halo) * cw_ref[j:j + 1, :]
        halo_s[...] = xl_ref[tm - SUBLANES:tm, :]
        xc_ref[...] = xc

        _, ig, _, a, sq, _ = _lru_gates(xc, wr_ref, br_ref, wi_ref, bi_ref, lam_ref)
        u = sq * (ig * xc)
        hc_s[...] = _scan_fwd_into(a, u, hc_s[...], h_ref)
        hh = h_ref[...]

        gl = gl_ref[...]
        yl = hh * _rstd(hh) * gnl_ref[...] * (gl * _sigmoid(gl))
        ycat_ref[:, D:] = yl.astype(BF16)

    vec = _const_spec((1, D))
    wspec = _const_spec((NB, LANES, LANES))
    return _call(
        body, name="branches_fwd", grid=(T // tm,),
        in_specs=[_row_spec(tm, D)] * 4 + [vec, vec, _const_spec((4, D)), vec, wspec, vec, wspec, vec, vec],
        out_specs=[_row_spec(tm, DMIX), _row_spec(tm, D), _row_spec(tm, D)],
        out_shape=[jax.ShapeDtypeStruct((T, DMIX), BF16), jax.ShapeDtypeStruct((T, D), F32),
                   jax.ShapeDtypeStruct((T, D), F32)],
        scratch_shapes=[pltpu.VMEM((SUBLANES, D), F32), pltpu.VMEM((1, D), F32)],
        compiler_params=_cparams(("arbitrary",)),
    )(o, g_attn, x_lru, g_lru, gain_a, gain_l, conv_w, conv_b, w_r, b_r, w_i, b_i, lam)


def _tail(ycat, x, p, tgt, w_out, post_gain, w_ple, ple_gain, w_gate, b_gate):
    T = x.shape[0]
    tm = TM

    def body(ycat_ref, x_ref, p_ref, t_ref, wo_ref, pg_ref, wp_ref, eg_ref, wg_ref, bg_ref,
             dh1_ref, dycat_ref, dmix_ref, h1b_ref, dgp_ref, pb_ref, dpe_ref, acc_ref):
        @pl.when(pl.program_id(0) == 0)
        def _():
            acc_ref[...] = jnp.zeros_like(acc_ref)

        mix = _dot(ycat_ref[...], wo_ref[...])
        rstd_m = _rstd(mix)
        mhat = mix * rstd_m
        h1 = x_ref[...] + mhat * pg_ref[...]
        pb = p_ref[...].astype(BF16)
        pb_ref[...] = pb
        pe = _dot(pb, wp_ref[...])
        rstd_p = _rstd(pe)
        pehat = pe * rstd_p
        e = pehat * eg_ref[...]
        h1b = h1.astype(BF16)
        h1b_ref[...] = h1b
        gate = _sigmoid(_dot(h1b, wg_ref[...]) + bg_ref[...])
        diff = (h1 + gate * e) - t_ref[...]

        dy = diff * (1.0 / D)
        de = dy * gate
        dgp = (dy * e) * gate * (1.0 - gate)
        dgpb = dgp.astype(BF16)
        dgp_ref[...] = dgpb
        dh1 = dy + _dot_nt(dgpb, wg_ref[...])
        dh1_ref[...] = dh1
        dpe_ref[...] = _rms_bwd(de * eg_ref[...], pehat, rstd_p).astype(BF16)
        dmix = _rms_bwd(dh1 * pg_ref[...], mhat, rstd_m).astype(BF16)
        dmix_ref[...] = dmix
        dycat_ref[...] = _dot_nt(dmix, wo_ref[...])

        acc_ref[0:1, :] += jnp.sum(dh1 * mhat, axis=0, keepdims=True)
        acc_ref[1:2, :] += jnp.sum(de * pehat, axis=0, keepdims=True)
        acc_ref[2:3, :] += jnp.sum(dgp, axis=0, keepdims=True)
        acc_ref[3:4, :] += jnp.sum(diff * diff, axis=0, keepdims=True) * (0.5 / D)

    vec = _const_spec((1, D))
    bf = jax.ShapeDtypeStruct((T, D), BF16)
    return _call(
        body, name="tail", grid=(T // tm,),
        in_specs=[_row_spec(tm, DMIX), _row_spec(tm, D), _row_spec(tm, DPLE), _row_spec(tm, D),
                  _const_spec((DMIX, D)), vec, _const_spec((DPLE, D)), vec, _const_spec((D, D)), vec],
        out_specs=[_row_spec(tm, D), _row_spec(tm, DMIX), _row_spec(tm, D), _row_spec(tm, D), _row_spec(tm, D),
                   _row_spec(tm, DPLE), _row_spec(tm, D), _const_spec((SUBLANES, D))],
        out_shape=[jax.ShapeDtypeStruct((T, D), F32), jax.ShapeDtypeStruct((T, DMIX), F32), bf, bf, bf,
                   jax.ShapeDtypeStruct((T, DPLE), BF16), bf, jax.ShapeDtypeStruct((SUBLANES, D), F32)],
        compiler_params=_cparams(("arbitrary",), VMEM_BIG),
    )(ycat, x, p, tgt, w_out, post_gain, w_ple, ple_gain, w_gate, b_gate)


def _pair_copies(srcs, gots, send_sems, recv_sems):
    x, y, c = _position()
    copies = []
    for a, (src, got) in enumerate(zip(srcs, gots)):
        half = src.shape[1] // 2
        rows = pl.ds(pl.multiple_of((1 - c) * half, SUBLANES), half)
        copies.append(pltpu.make_async_remote_copy(
            src_ref=src.at[:, rows, :], dst_ref=got, send_sem=send_sems.at[a], recv_sem=recv_sems.at[a],
            device_id=(x, y, 1 - c), device_id_type=MESH))
    return copies


def _branches_bwd(dycat, o, g_attn, h, g_lru, gain_a, gain_l, ycat, dmix, h1b, dgp, pb, dpe):
    T = o.shape[0]
    tm = TM
    nt = T // tm
    nb_gate, nb_ple = min(nt, 8), min(nt, 2)
    ns_gate, ns_ple = nt // nb_gate, nt // nb_ple
    br_out, br_gate, br_ple = DMIX // nt, D // nb_gate, DPLE // nb_ple
    tk_gate, tk_ple = T // ns_gate, T // ns_ple

    def body(dy_ref, o_ref, ga_ref, h_ref, gl_ref, gna_ref, gnl_ref, yc_ref, dmix_ref, h1_ref, dgp_ref, pb_ref,
             dpe_ref, do_ref, dga_ref, dgl_ref, dh_ref, acc_ref, gwo_ref, gwg_ref, gwp_ref):
        i = pl.program_id(0)

        @pl.when(i == 0)
        def _():
            acc_ref[...] = jnp.zeros_like(acc_ref)

        def accumulate(out_ref, lhs_ref, rhs_ref, tokens, slices):
            s = i % slices
            part = _dot_tn(lhs_ref[...], rhs_ref[pl.ds(pl.multiple_of(s * tokens, tokens), tokens), :])
            out_ref[...] = part + jnp.where(s == 0, 0.0, out_ref[...])

        gwo_ref[...] = _dot_tn(yc_ref[...], dmix_ref[...])

        def branch(val, g, gain, dyv):
            rstd = _rstd(val)
            vhat = val * rstd
            sig = _sigmoid(g)
            dn = dyv * (g * sig)
            dg = dyv * (vhat * gain) * (sig * (1.0 + g * (1.0 - sig)))
            dgain = jnp.sum(dn * vhat, axis=0, keepdims=True)
            return _rms_bwd(dn * gain, vhat, rstd), dg, dgain

        ov = o_ref[...]
        do, dga, dgain_a = branch(ov, ga_ref[...], gna_ref[...], dy_ref[:, :D])
        dga_ref[...] = dga.astype(BF16)
        prod = do * ov
        for hd in range(H):
            head = slice(hd * DH, (hd + 1) * DH)
            do_ref[:, hd * AUG:hd * AUG + DH] = do[:, head].astype(BF16)
            do_ref[:, hd * AUG + DH:(hd + 1) * AUG] = _extras(-jnp.sum(prod[:, head], axis=1, keepdims=True), None)

        accumulate(gwg_ref, h1_ref, dgp_ref, tk_gate, ns_gate)
        accumulate(gwp_ref, pb_ref, dpe_ref, tk_ple, ns_ple)
        dh, dgl, dgain_l = branch(h_ref[...], gl_ref[...], gnl_ref[...], dy_ref[:, D:])
        dh_ref[...] = dh
        dgl_ref[...] = dgl.astype(BF16)
        acc_ref[0:1, :] += dgain_a
        acc_ref[1:2, :] += dgain_l

    vec = _const_spec((1, D))
    bf = jax.ShapeDtypeStruct((T, D), BF16)
    tokens = _weight_spec((T, D))
    return _call(
        body, name="branches_bwd", grid=(nt,),
        in_specs=[_row_spec(tm, DMIX)] + [_row_spec(tm, D)] * 4 + [vec, vec]
        + [pl.BlockSpec((T, br_out), lambda i: (0, i)), tokens,
           pl.BlockSpec((tk_gate, br_gate), lambda i: (i % ns_gate, i // ns_gate)), tokens,
           pl.BlockSpec((tk_ple, br_ple), lambda i: (i % ns_ple, i // ns_ple)), tokens],
        out_specs=[_row_spec(tm, H * AUG), _row_spec(tm, D), _row_spec(tm, D), _row_spec(tm, D),
                   _const_spec((SUBLANES, D)),
                   pl.BlockSpec((br_out, D), lambda i: (i, 0)),
                   pl.BlockSpec((br_gate, D), lambda i: (i // ns_gate, 0)),
                   pl.BlockSpec((br_ple, D), lambda i: (i // ns_ple, 0))],
        out_shape=[jax.ShapeDtypeStruct((T, H * AUG), BF16), bf, bf, jax.ShapeDtypeStruct((T, D), F32),
                   jax.ShapeDtypeStruct((SUBLANES, D), F32), jax.ShapeDtypeStruct((DMIX, D), F32),
                   jax.ShapeDtypeStruct((D, D), F32), jax.ShapeDtypeStruct((DPLE, D), F32)],
        compiler_params=_cparams(("arbitrary",), VMEM_BIG),
    )(dycat, o, g_attn, h, g_lru, gain_a, gain_l, ycat, dmix, h1b, dgp, pb, dpe)


def _lru_bwd(dh, h, xc, x_lru, conv_w, w_r, b_r, w_i, b_i, lam, pair_parts=()):
    T = dh.shape[0]
    tm = TM
    nt = T // tm
    per = tm // SUBLANES
    npair = len(pair_parts)

    def body(dh_ref, h_ref, hprev_ref, xc_ref, xl_ref, cw_ref, wr_ref, br_ref, wi_ref, bi_ref, lam_ref, *rest):
        parts, rest = rest[:npair], rest[npair:]
        dxl_ref, dwr_ref, dwi_ref, acc_ref = rest[:4]
        gots, rest = rest[4:4 + npair], rest[4 + npair:]
        carry_s, dxc_next_s, top_s, dht_s = rest[:4]
        i = pl.program_id(0)

        @pl.when(i == 0)
        def _():
            acc_ref[...] = jnp.zeros_like(acc_ref)
            dwr_ref[...] = jnp.zeros_like(dwr_ref)
            dwi_ref[...] = jnp.zeros_like(dwi_ref)
            carry_s[...] = jnp.zeros_like(carry_s)
            dxc_next_s[...] = jnp.zeros_like(dxc_next_s)
            for cp in _pair_copies(parts, gots, *rest[4:]) if npair else ():
                cp.start()

        if npair:
            @pl.when(i == nt - 1)
            def _():
                for cp in _pair_copies(parts, gots, *rest[4:]):
                    cp.wait()

        inner = jnp.where(i == nt - 1, 0.0, 1.0)
        xc = xc_ref[...]
        r, ig, sp, a, sq, inv_sq = _lru_gates(xc, wr_ref, br_ref, wi_ref, bi_ref, lam_ref)

        row = lax.broadcasted_iota(jnp.int32, (tm, D), 0)
        u = dh_ref[...] + jnp.where(row == tm - 1, carry_s[...], 0.0)
        _scan_bwd_into(pltpu.roll(a, tm - 1, 0), u, dht_s)
        dht = dht_s[...]
        top_s[...] = a[:SUBLANES, :] * dht[:SUBLANES, :]
        carry_s[...] = top_s[0:1, :]

        hprev = hprev_ref[...] * inner
        da = dht * _shift_down(h_ref[...], 1, hprev)
        dig = dht * sq * xc
        dxc = dht * sq * ig
        dsq = dht * ig * xc
        dla = da * a - dsq * (a * a) * inv_sq
        dr = dla * ((-LRU_C) * sp)
        dpr = dr * r * (1.0 - r)
        dpi = dig * ig * (1.0 - ig)
        for n in range(NB):
            blk = slice(n * LANES, (n + 1) * LANES)
            xcb = xc[:, blk].astype(BF16)
            dwr_ref[n] += _dot_tn(xcb, dpr[:, blk].astype(BF16))
            dwi_ref[n] += _dot_tn(xcb, dpi[:, blk].astype(BF16))
        dxc = dxc + _gate_pre_t(dpr, wr_ref) + _gate_pre_t(dpi, wi_ref)

        xl = xl_ref[...]
        nxt = dxc_next_s[...]
        dxl = dxc * cw_ref[3:4, :]
        acc_ref[3:4, :] += jnp.sum(dxc * xl, axis=0, keepdims=True)
        for j in range(3):
            ahead = _shift_up(dxc, 3 - j, nxt)
            dxl = dxl + ahead * cw_ref[j:j + 1, :]
            acc_ref[j:j + 1, :] += jnp.sum(ahead * xl, axis=0, keepdims=True)
        dxc_next_s[...] = dxc[:SUBLANES, :]
        dxl_ref[...] = dxl.astype(BF16)

        acc_ref[4:5, :] += jnp.sum(dxc, axis=0, keepdims=True)
        acc_ref[5:6, :] += jnp.sum(dpr, axis=0, keepdims=True)
        acc_ref[6:7, :] += jnp.sum(dpi, axis=0, keepdims=True)
        acc_ref[7:8, :] += jnp.sum(dla * ((-LRU_C) * r), axis=0, keepdims=True)

        @pl.when(i == nt - 1)
        def _():
            lam_v = lam_ref[...]
            acc_ref[7:8, :] = acc_ref[7:8, :] * (-_sigmoid(-lam_v))

    rev = pl.BlockSpec((tm, D), lambda i: (nt - 1 - i, 0))
    prev8 = pl.BlockSpec((SUBLANES, D), lambda i: (jnp.maximum((nt - 1 - i) * per - 1, 0), 0))
    vec = _const_spec((1, D))
    wspec = _const_spec((NB, LANES, LANES))
    bf = jax.ShapeDtypeStruct((T, D), BF16)
    halves = [jax.ShapeDtypeStruct((s.shape[0], s.shape[1] // 2, s.shape[2]), s.dtype) for s in pair_parts]
    outs = _call(
        body, name="lru_bwd", grid=(nt,),
        in_specs=[rev, rev, prev8, rev, rev, _const_spec((4, D)), wspec, vec, wspec, vec, vec] + [HBM_SPEC] * npair,
        out_specs=[rev, wspec, wspec, _const_spec((SUBLANES, D))] + [HBM_SPEC] * npair,
        out_shape=[bf, jax.ShapeDtypeStruct((NB, LANES, LANES), F32), jax.ShapeDtypeStruct((NB, LANES, LANES), F32),
                   jax.ShapeDtypeStruct((SUBLANES, D), F32)] + halves,
        scratch_shapes=[pltpu.VMEM((1, D), F32), pltpu.VMEM((SUBLANES, D), F32), pltpu.VMEM((SUBLANES, D), F32),
                        pltpu.VMEM((tm, D), F32)]
        + ([pltpu.SemaphoreType.DMA((npair,)), pltpu.SemaphoreType.DMA((npair,))] if npair else []),
        compiler_params=_cparams(("arbitrary",)),
    )(dh, h, h, xc, x_lru, conv_w, w_r, b_r, w_i, b_i, lam, *pair_parts)
    return (*outs[:4], list(outs[4:]))


def _chip_copies(srcs, dsts, send_sems, recv_sems):
    x, y, c = _position()
    chip = 2 * x + y
    na = len(srcs)
    return [pltpu.make_async_remote_copy(
        src_ref=srcs[a].at[2 * px + py], dst_ref=dsts[a].at[chip], send_sem=send_sems.at[j * na + a],
        recv_sem=recv_sems.at[j * na + a], device_id=(px, py, c), device_id_type=MESH)
        for j, (px, py) in enumerate(_other_chips(x, y)) for a in range(na)]


def _attn_bwd(q_aug, qx, k_aug, v_aug, do_aug, exchange=()):
    T = q_aug.shape[0]
    t = TA
    n = T // t
    hp = BWD_HEADS
    heads = range(hp)
    scale = DH ** -0.5
    ki_tab, qi_tab = _causal_pairs(n, q_major=False)
    last = ki_tab.shape[0] - 1
    ne = len(exchange)
    n_h = H // hp

    def body(ki_ref, qi_ref, q_ref, qx_ref, k_ref, v_ref, do_ref, *rest):
        sent, rest = rest[:ne], rest[ne:]
        dq_ref, dk_ref, dv_ref, dc_ref = rest[:4]
        received, rest = rest[4:4 + ne], rest[4 + ne:]
        dq_s, dk_s, dv_s = rest[:3]
        j = pl.program_id(1)
        ki = ki_ref[j]
        qi = qi_ref[j]

        if ne:
            first_step = (pl.program_id(0) == 0) & (j == 0)
            last_step = (pl.program_id(0) == n_h - 1) & (j == last)

            @pl.when(first_step)
            def _():
                for cp in _chip_copies(sent, received, *rest[3:]):
                    cp.start()

            @pl.when(last_step)
            def _():
                for cp in _chip_copies(sent, received, *rest[3:]):
                    cp.wait()

        @pl.when(j == 0)
        def _():
            dq_s[...] = jnp.zeros_like(dq_s)

        @pl.when(qi == ki)
        def _():
            dk_s[...] = jnp.zeros_like(dk_s)
            dv_s[...] = jnp.zeros_like(dv_s)

        def step(on_diagonal):
            cols = [slice(a * AUG, (a + 1) * AUG) for a in heads]
            qb = [jnp.concatenate([q_ref[:, a * AUG:a * AUG + DH], qx_ref[:, a * DH:(a + 1) * DH]], axis=1)
                  for a in heads]
            if on_diagonal:
                krow = lax.broadcasted_iota(jnp.int32, (t, t), 0)
                qcol = lax.broadcasted_iota(jnp.int32, (t, t), 1)

            def scores(a):
                st = _dot_nt(k_ref[:, cols[a]], qb[a])
                dpd = _dot_nt(v_ref[:, cols[a]], do_ref[:, cols[a]])
                return (jnp.where(krow <= qcol, st, NEG) if on_diagonal else st), dpd

            off = pl.multiple_of(qi * t, t)
            ahead = scores(0)
            for a in heads:
                st, dpd = ahead
                if a + 1 < hp:
                    ahead = scores(a + 1)
                pt = jnp.exp2(st)
                dsb = (pt * dpd).astype(BF16)
                dv_s[a] += _dot(pt.astype(BF16), do_ref[:, a * AUG:a * AUG + DH])
                dk_s[a] += _dot(dsb, qb[a])
                dq_s[a, pl.ds(off, t), :] += _dot_tn(dsb, k_ref[:, cols[a]])

        @pl.when(qi > ki)
        def _():
            step(False)

        @pl.when(qi == ki)
        def _():
            step(True)

        @pl.when(qi == n - 1)
        def _():
            rows = pl.ds(pl.multiple_of(ki * t, t), t)
            for a in heads:
                dk_ref[:, a * DH:(a + 1) * DH] = (dk_s[a, :, :DH] * LN2).astype(BF16)
                dv_ref[:, a * DH:(a + 1) * DH] = dv_s[a].astype(BF16)
                dc_ref[a, rows, :] = jnp.broadcast_to(-dk_s[a, :, DH + 3:DH + 4], (t, LANES))

        @pl.when(j == last)
        def _():
            for a in heads:
                dq_ref[:, a * DH:(a + 1) * DH] = (dq_s[a, :, :DH] * scale).astype(BF16)
                dc_ref[a] = dc_ref[a] + jnp.broadcast_to(dq_s[a, :, DH:DH + 1], (T, LANES))

    qside = pl.BlockSpec((t, hp * AUG), lambda h, j, ki_ref, qi_ref: (qi_ref[j], h))
    qxside = pl.BlockSpec((t, hp * DH), lambda h, j, ki_ref, qi_ref: (qi_ref[j], h))
    kside = pl.BlockSpec((t, hp * AUG), lambda h, j, ki_ref, qi_ref: (ki_ref[j], h))
    kout = pl.BlockSpec((t, hp * DH), lambda h, j, ki_ref, qi_ref: (ki_ref[j], h))
    bf = jax.ShapeDtypeStruct((T, D), BF16)
    sums = jax.ShapeDtypeStruct((H, T, LANES), F32)
    grid_spec = pltpu.PrefetchScalarGridSpec(
        num_scalar_prefetch=2, grid=(n_h, ki_tab.shape[0]),
        in_specs=[qside, qxside, kside, kside, qside] + [HBM_SPEC] * ne,
        out_specs=[pl.BlockSpec((T, hp * DH), lambda h, j, ki_ref, qi_ref: (0, h)), kout, kout,
                   pl.BlockSpec((hp, T, LANES), lambda h, j, ki_ref, qi_ref: (h, 0, 0))] + [HBM_SPEC] * ne,
        scratch_shapes=[pltpu.VMEM((hp, T, AUG), F32), pltpu.VMEM((hp, t, AUG), F32), pltpu.VMEM((hp, t, DH), F32)]
        + ([pltpu.SemaphoreType.DMA((3 * ne,)), pltpu.SemaphoreType.DMA((3 * ne,))] if ne else []))
    outs = _call(
        body, name="attn_bwd", grid_spec=grid_spec,
        out_shape=[bf, bf, bf, sums] + [jax.ShapeDtypeStruct(s.shape, s.dtype) for s in exchange],
        compiler_params=_cparams(("arbitrary", "arbitrary"), VMEM_BIG),
    )(ki_tab, qi_tab, q_aug, qx, k_aug, v_aug, do_aug, *exchange)
    return (*outs[:4], list(outs[4:]))


def _fgate_bwd(dc_heads, flb):
    T = flb.shape[0]
    tm = TM
    nt = T // tm

    def body(dch_ref, flb_ref, dfl_ref, acc_ref, carry, top_s):
        @pl.when(pl.program_id(0) == 0)
        def _():
            carry[...] = jnp.zeros_like(carry)
            acc_ref[...] = jnp.zeros_like(acc_ref)

        flb = flb_ref[...]
        lane = lax.broadcasted_iota(jnp.int32, flb.shape, 1)
        dc = jnp.zeros(flb.shape, F32)
        for hd in range(H):
            dc = dc + jnp.where(lane == hd, dch_ref[hd], 0.0)
        r = lax.broadcasted_iota(jnp.int32, (tm, tm), 0)
        c = lax.broadcasted_iota(jnp.int32, (tm, tm), 1)
        dls = _dot_exact((c >= r).astype(F32), dc) + carry[...]
        top_s[...] = dls[:SUBLANES, :]
        carry[...] = top_s[0:1, :]
        dfl = jnp.where(lane < H, dls * _sigmoid(-flb), 0.0)
        dfl_ref[...] = dfl.astype(BF16)
        acc_ref[0:1, :] += jnp.sum(dfl, axis=0, keepdims=True)

    rev = pl.BlockSpec((tm, LANES), lambda i: (nt - 1 - i, 0))
    return _call(
        body, name="fgate_bwd", grid=(nt,),
        in_specs=[pl.BlockSpec((H, tm, LANES), lambda i: (0, nt - 1 - i, 0)), rev],
        out_specs=[rev, _const_spec((SUBLANES, LANES))],
        out_shape=[jax.ShapeDtypeStruct((T, LANES), BF16), jax.ShapeDtypeStruct((SUBLANES, LANES), F32)],
        scratch_shapes=[pltpu.VMEM((1, LANES), F32), pltpu.VMEM((SUBLANES, LANES), F32)],
        compiler_params=_cparams(("arbitrary",)),
    )(dc_heads, flb)


def _dx(dz, dfl, w_a, w_f, w_b, x, pre_gain, dh1, exchange=()):
    T = x.shape[0]
    tm = TM
    nt = T // tm
    ne = len(exchange)

    def body(*refs):
        dz_refs = refs[:6]
        dfl_ref, wa_ref, wf_ref, wb_ref, x_ref, g_ref, dh1_ref = refs[6:13]
        sent = refs[13:13 + ne]
        gx_ref, acc_ref = refs[13 + ne:15 + ne]
        received, sems = refs[15 + ne:15 + 2 * ne], refs[15 + 2 * ne:]

        @pl.when(pl.program_id(0) == 0)
        def _():
            acc_ref[...] = jnp.zeros_like(acc_ref)
            for cp in _chip_copies(sent, received, *sems) if ne else ():
                cp.start()

        if ne:
            @pl.when(pl.program_id(0) == nt - 1)
            def _():
                for cp in _chip_copies(sent, received, *sems):
                    cp.wait()

        dxn = _dot(dfl_ref[...], wf_ref[...])
        for s in range(3):
            dxn = dxn + _dot(dz_refs[s][...], wa_ref[s * D:(s + 1) * D, :])
            dxn = dxn + _dot(dz_refs[3 + s][...], wb_ref[s * D:(s + 1) * D, :])
        xv = x_ref[...]
        rstd = _rstd(xv)
        xhat = xv * rstd
        gx_ref[...] = dh1_ref[...] + _rms_bwd(dxn * g_ref[...], xhat, rstd)
        acc_ref[0:1, :] += jnp.sum(dxn * xhat, axis=0, keepdims=True)

    outs = _call(
        body, name="dx", grid=(nt,),
        in_specs=[_row_spec(tm, D)] * 6 + [_row_spec(tm, LANES), _weight_spec((3 * D, D)), _weight_spec((LANES, D)),
                                           _weight_spec((3 * D, D)), _row_spec(tm, D), _const_spec((1, D)),
                                           _row_spec(tm, D)] + [HBM_SPEC] * ne,
        out_specs=[_row_spec(tm, D), _const_spec((SUBLANES, D))] + [HBM_SPEC] * ne,
        out_shape=[jax.ShapeDtypeStruct((T, D), F32), jax.ShapeDtypeStruct((SUBLANES, D), F32)]
        + [jax.ShapeDtypeStruct(s.shape, s.dtype) for s in exchange],
        scratch_shapes=[pltpu.SemaphoreType.DMA((3 * ne,)), pltpu.SemaphoreType.DMA((3 * ne,))] if ne else [],
        compiler_params=_cparams(("arbitrary",), VMEM_BIG),
    )(*dz, dfl, w_a, w_f, w_b, x, pre_gain, dh1, *exchange)
    return outs[0], outs[1], list(outs[2:])


GRAD_ROWS = D_IN + SUBLANES


def _dw_in_segments(dz_a, dz_b, xn, buf, pair, bt):
    T = xn.shape[0]
    nt = T // bt
    first, second = [(2 * pair + k) * D + (H if 2 * pair + k >= 3 else 0) for k in (0, 1)]
    step8 = (second - first) // SUBLANES

    def body(*refs):
        dza_ref, dzb_ref, xn_ref, o_ref = refs[0], refs[1], refs[2], refs[-1]

        @pl.when(pl.program_id(1) == 0)
        def _():
            o_ref[...] = jnp.zeros_like(o_ref)

        @pl.when(pl.program_id(0) == 0)
        def _():
            o_ref[...] += _dot_tn(dza_ref[...], xn_ref[...])

        @pl.when(pl.program_id(0) == 1)
        def _():
            o_ref[...] += _dot_tn(dzb_ref[...], xn_ref[...])

    spec_a = pl.BlockSpec((bt, D), lambda s, t: (jnp.where(s == 0, t, nt - 1), 0))
    spec_b = pl.BlockSpec((bt, D), lambda s, t: (jnp.where(s == 1, t, 0), 0))
    return _call(
        body, name="dw_in_%d" % pair, grid=(2, nt),
        in_specs=[spec_a, spec_b, pl.BlockSpec((bt, D), lambda s, t: (t, 0))]
        + ([] if buf is None else [pl.BlockSpec(memory_space=pl.ANY)]),
        out_specs=pl.BlockSpec((pl.Element(D), pl.Element(D)),
                               lambda s, t: ((first // SUBLANES + s * step8) * SUBLANES, 0)),
        out_shape=jax.ShapeDtypeStruct((GRAD_ROWS, D), F32),
        input_output_aliases={} if buf is None else {3: 0},
        compiler_params=_cparams(("arbitrary", "arbitrary"), VMEM_BIG),
    )(*((dz_a, dz_b, xn) if buf is None else (dz_a, dz_b, xn, buf)))


def _dw_in_t(dz, dfl, xn, bt=DW_TOKENS):
    T = xn.shape[0]
    bt = min(bt, T)
    nt = T // bt
    main = None
    for pair in range(3):
        main = _dw_in_segments(dz[2 * pair], dz[2 * pair + 1], xn, main, pair, bt)

    def f_body(dfl_ref, xn_ref, main_ref, o_ref, acc_s):
        p = pl.program_id(0)
        t = pl.program_id(1)

        @pl.when(t == 0)
        def _():
            acc_s[...] = jnp.zeros_like(acc_s)

        @pl.when(p == 0)
        def _():
            acc_s[...] += _dot_tn(dfl_ref[...], xn_ref[...])

        @pl.when(t == nt - 1)
        def _():
            o_ref[...] = acc_s[:SUBLANES, :]

    fl_block = FL0 // SUBLANES
    end_block = D_IN // SUBLANES
    return _call(
        f_body, name="dw_in_f", grid=(2, nt),
        in_specs=[pl.BlockSpec((bt, LANES), lambda p, t: (t, 0)), pl.BlockSpec((bt, D), lambda p, t: (t, 0)),
                  pl.BlockSpec(memory_space=pl.ANY)],
        out_specs=pl.BlockSpec((SUBLANES, D), lambda p, t: (fl_block + p * (end_block - fl_block), 0)),
        out_shape=jax.ShapeDtypeStruct((GRAD_ROWS, D), F32),
        scratch_shapes=[pltpu.VMEM((LANES, D), F32)],
        input_output_aliases={2: 0},
        compiler_params=_cparams(("arbitrary", "arbitrary")),
    )(dfl, xn, main)


HBM_SPEC = pl.BlockSpec(memory_space=pltpu.HBM)
VMEM_SPEC = pl.BlockSpec(memory_space=pltpu.VMEM)


def _position():
    return lax.axis_index("x"), lax.axis_index("y"), lax.axis_index("c")


def _other_chips(x, y):
    return [(1 - x, y), (x, 1 - y), (1 - x, 1 - y)]


def _gather_shards(shards, whole):
    na, nw = len(shards), len(whole)
    nall = na + nw

    def body(*refs):
        gather = _GatherPlan(refs[:nall], refs[nall:2 * nall], refs[2 * nall:], na)
        gather.send()
        gather.forward()
        gather.finish()

    arrs = list(shards) + list(whole)
    outs = _call(
        body, name="gather_shards",
        in_specs=[HBM_SPEC] * nall, out_specs=[HBM_SPEC] * nall,
        out_shape=_gather_out_shapes(arrs), scratch_shapes=_gather_semaphores(na, nall),
    )(*arrs)
    return _place_own(outs, arrs)


def _gather_out_shapes(arrs):
    return [jax.ShapeDtypeStruct((N_CHIPS,) + s.shape, s.dtype) for s in arrs]


def _gather_semaphores(na, nall):
    return [pltpu.SemaphoreType.DMA((3 * nall,)), pltpu.SemaphoreType.DMA((3 * nall,)),
            pltpu.SemaphoreType.DMA((3 * na,)), pltpu.SemaphoreType.DMA((3 * na,))]


def _place_own(outs, arrs):
    if not arrs:
        return []
    chip = 2 * lax.axis_index("x") + lax.axis_index("y")
    return [lax.dynamic_update_slice(o, a[None], (chip,) + (0,) * a.ndim) for o, a in zip(outs, arrs)]


class _GatherPlan:
    def __init__(self, srcs, dsts, sems, na):
        ici_send, ici_recv, d2d_send, d2d_recv = sems
        x, y, c = _position()
        chip = 2 * x + y
        nall = len(srcs)

        def half(a, which):
            rows = srcs[a].shape[0] // 2
            return pl.ds(pl.multiple_of(which * rows, BF16_ROWS), rows)

        def copy(src, dst, send, recv, k, to):
            return pltpu.make_async_remote_copy(src_ref=src, dst_ref=dst, send_sem=send.at[k], recv_sem=recv.at[k],
                                                device_id=to, device_id_type=MESH)

        self.first, self.landed, self.passed, self.returned = [], [], [], []
        for j, (px, py) in enumerate(_other_chips(x, y)):
            theirs = 2 * px + py
            for a in range(nall):
                k = j * nall + a
                if a < na:
                    self.first.append(copy(srcs[a].at[half(a, c), :], dsts[a].at[chip, half(a, c), :],
                                           ici_send, ici_recv, k, (px, py, c)))
                    mine = dsts[a].at[theirs, half(a, c), :]
                    other = dsts[a].at[theirs, half(a, 1 - c), :]
                    self.landed.append(copy(mine, mine, ici_send, ici_recv, k, (px, py, c)))
                    self.passed.append(copy(mine, mine, d2d_send, d2d_recv, j * na + a, (x, y, 1 - c)))
                    self.returned.append(copy(other, other, d2d_send, d2d_recv, j * na + a, (x, y, 1 - c)))
                else:
                    self.first.append(copy(srcs[a], dsts[a].at[chip], ici_send, ici_recv, k, (px, py, c)))
                    got = dsts[a].at[theirs]
                    self.landed.append(copy(got, got, ici_send, ici_recv, k, (px, py, c)))
                    self.passed.append(None)

    def send(self):
        for cp in self.first:
            cp.start()

    def forward(self):
        for arrival, fwd in zip(self.landed, self.passed):
            arrival.wait_recv()
            if fwd is not None:
                fwd.start()

    def finish(self):
        for cp in self.returned:
            cp.wait_recv()
        for cp in self.first + [f for f in self.passed if f is not None]:
            cp.wait_send()


W_ROWS = 1568
G_ROWS = 1552
SHARD_ROWS = D_IN // N_CHIPS
WINDOW_STEP = 1536


def _assemble_w_in(cont):
    cb = COL_BLOCK
    half = WINDOW_STEP
    seam = BF16_ROWS

    def body(c_ref, wa_ref, wf_ref, wb_ref):
        x0 = c_ref[0].astype(F32)
        x1, x2, x3 = (pltpu.roll(c_ref[j].astype(F32), 2 * j, 0) for j in (1, 2, 3))
        wa = jnp.concatenate([x0[:half], x0[half:half + seam] + x1[:seam], x1[seam:half]], axis=0)
        wa_ref[...] = wa.astype(BF16)

        fl = x1[half:half + seam] + x2[:seam]
        row = lax.broadcasted_iota(jnp.int32, fl.shape, 0)
        wf_ref[:seam, :] = jnp.where(row < H, fl, 0.0).astype(BF16)
        wf_ref[seam:, :] = jnp.zeros((LANES - seam, cb), BF16)

        mid = x2[half:half + SUBLANES] + x3[:SUBLANES]
        wb = jnp.concatenate([x2[SUBLANES:half], mid, x3[SUBLANES:half + SUBLANES]], axis=0)
        wb_ref[...] = wb.astype(BF16)

    return _call(
        body, name="assemble_w_in", grid=(D // cb,),
        in_specs=[pl.BlockSpec((N_CHIPS, W_ROWS, cb), lambda i: (0, 0, i))],
        out_specs=[pl.BlockSpec((3 * D, cb), lambda i: (0, i)), pl.BlockSpec((LANES, cb), lambda i: (0, i)),
                   pl.BlockSpec((3 * D, cb), lambda i: (0, i))],
        out_shape=[jax.ShapeDtypeStruct((3 * D, D), BF16), jax.ShapeDtypeStruct((LANES, D), BF16),
                   jax.ShapeDtypeStruct((3 * D, D), BF16)],
        compiler_params=_cparams(("parallel",)),
    )(cont)


def _pair_exchange_windows(grad_t):
    half_g = G_ROWS // 2

    def body(g_ref, got, send_sems, recv_sems):
        x, y, c = _position()
        copies = []
        for j in range(N_CHIPS):
            rows = pl.ds(pl.multiple_of(j * WINDOW_STEP + (1 - c) * half_g, SUBLANES), half_g)
            copies.append(pltpu.make_async_remote_copy(
                src_ref=g_ref.at[rows, :], dst_ref=got.at[j], send_sem=send_sems.at[j], recv_sem=recv_sems.at[j],
                device_id=(x, y, 1 - c), device_id_type=MESH))
        for cp in copies:
            cp.start()
        for cp in copies:
            cp.wait()

    return _call(
        body, name="pair_exchange_w_in",
        in_specs=[HBM_SPEC], out_specs=HBM_SPEC,
        out_shape=jax.ShapeDtypeStruct((N_CHIPS, half_g, D), F32),
        scratch_shapes=[pltpu.SemaphoreType.DMA((N_CHIPS,)), pltpu.SemaphoreType.DMA((N_CHIPS,))],
    )(grad_t)


def _pair_sum(parts, gots, c):
    na = len(parts)

    def body(c_ref, *refs):
        for a in range(na):
            refs[2 * na + a][...] = (refs[a][...] + refs[na + a][...]).astype(BF16)

    mine = [pl.BlockSpec(g.shape, lambda i, c_ref: (0, c_ref[0], 0)) for g in gots]
    whole = [pl.BlockSpec(g.shape, lambda i, c_ref: (0, 0, 0)) for g in gots]
    grid_spec = pltpu.PrefetchScalarGridSpec(
        num_scalar_prefetch=1, grid=(1,), in_specs=mine + whole, out_specs=whole)
    return _call(
        body, name="pair_sum", grid_spec=grid_spec,
        out_shape=[jax.ShapeDtypeStruct(g.shape, BF16) for g in gots],
        compiler_params=_cparams(("arbitrary",), VMEM_BIG),
    )(c.reshape(1), *parts, *gots)


def _pair_sum_windows(grad_t, got, c):
    _, half, C = got.shape
    cb = COL_BLOCK

    def body(c_ref, a_ref, b_ref, o_ref):
        o_ref[0] = (a_ref[...] + b_ref[0]).astype(BF16)

    def mine(j, i, c_ref):
        return ((j * (WINDOW_STEP // SUBLANES) + c_ref[0] * (half // SUBLANES)) * SUBLANES, i * cb)

    spec = pl.BlockSpec((1, half, cb), lambda j, i, c_ref: (j, 0, i))
    grid_spec = pltpu.PrefetchScalarGridSpec(
        num_scalar_prefetch=1, grid=(N_CHIPS, C // cb),
        in_specs=[pl.BlockSpec((pl.Element(half), pl.Element(cb)), mine), spec], out_specs=spec)
    return _call(
        body, name="pair_sum_w_in", grid_spec=grid_spec,
        out_shape=jax.ShapeDtypeStruct((N_CHIPS, half, C), BF16),
        compiler_params=_cparams(("parallel", "parallel")),
    )(c.reshape(1), grad_t, got)


def _chip_sum(own, got, chip, name):
    _, half, C = got.shape
    cb = min(C, COL_BLOCK)

    def body(chip_ref, own_ref, g_ref, o_ref):
        for me in range(N_CHIPS):
            @pl.when(chip_ref[0] == me)
            def _(me=me):
                terms = [own_ref[0] if k == me else g_ref[k] for k in range(N_CHIPS)]
                acc = terms[0].astype(F32) + terms[1].astype(F32)
                acc = acc + terms[2].astype(F32)
                o_ref[...] = acc + terms[3].astype(F32)

    grid_spec = pltpu.PrefetchScalarGridSpec(
        num_scalar_prefetch=1, grid=(C // cb,),
        in_specs=[pl.BlockSpec((1, half, cb), lambda i, chip_ref: (chip_ref[0], 0, i)),
                  pl.BlockSpec((N_CHIPS, half, cb), lambda i, chip_ref: (0, 0, i))],
        out_specs=pl.BlockSpec((half, cb), lambda i, chip_ref: (0, i)))
    return _call(
        body, name=name, grid_spec=grid_spec,
        out_shape=jax.ShapeDtypeStruct((half, C), F32),
        compiler_params=_cparams(("parallel",)),
    )(chip.reshape(1), own, got)


def _final_exchange(halves, g):
    na = len(halves)
    rows = g.shape[0]
    per = rows // N_DEV

    def body(*refs):
        srcs, g_ref = refs[:na], refs[na]
        dsts, out_ref = refs[na + 1:2 * na + 1], refs[2 * na + 1]
        got_ref, s1, r1, s2, r2, swap_send, swap_recv = refs[2 * na + 2:]
        x, y, c = _position()
        swaps = [pltpu.make_async_remote_copy(
            src_ref=srcs[a], dst_ref=dsts[a], send_sem=swap_send.at[a], recv_sem=swap_recv.at[a],
            device_id=(x, y, 1 - c), device_id_type=MESH) for a in range(na)]
        for cp in swaps:
            cp.start()
        me = 4 * x + 2 * y + c
        mine = pl.ds(pl.multiple_of(me * per, SUBLANES), per)
        peers = []
        for j in range(1, N_DEV):
            px = 1 - x if j & 4 else x
            py = 1 - y if j & 2 else y
            pc = 1 - c if j & 1 else c
            peers.append((px, py, pc))

        first = []
        for j, (px, py, pc) in enumerate(peers):
            theirs = pl.ds(pl.multiple_of((4 * px + 2 * py + pc) * per, SUBLANES), per)
            first.append(pltpu.make_async_remote_copy(
                src_ref=g_ref.at[theirs, :], dst_ref=got_ref.at[me], send_sem=s1.at[j], recv_sem=r1.at[j],
                device_id=(px, py, pc), device_id_type=MESH))
        for cp in first:
            cp.start()
        got_ref[me] = g_ref[mine, :]
        for cp in first:
            cp.wait()
        total = got_ref[0]
        for d in range(1, N_DEV):
            total = total + got_ref[d]
        out_ref[mine, :] = total

        second = []
        for j, peer in enumerate(peers):
            second.append(pltpu.make_async_remote_copy(
                src_ref=out_ref.at[mine, :], dst_ref=out_ref.at[mine, :], send_sem=s2.at[j], recv_sem=r2.at[j],
                device_id=peer, device_id_type=MESH))
        for cp in second:
            cp.start()
        for cp in second + swaps:
            cp.wait()

    sems = pltpu.SemaphoreType.DMA((N_DEV - 1,))
    swap_sems = pltpu.SemaphoreType.DMA((na,))
    outs = _call(
        body, name="final_exchange", in_hbm=False,
        in_specs=[HBM_SPEC] * na + [VMEM_SPEC], out_specs=[HBM_SPEC] * na + [VMEM_SPEC],
        out_shape=[jax.ShapeDtypeStruct(s.shape, s.dtype) for s in halves] + [jax.ShapeDtypeStruct(g.shape, F32)],
        scratch_shapes=[pltpu.VMEM((N_DEV, per, LANES), F32), sems, sems, sems, sems, swap_sems, swap_sems],
    )(*halves, g)
    return outs[:na], outs[na]


def _adamw_math(g, w, m, v):
    m2 = ADAM_B1 * m + (1.0 - ADAM_B1) * g
    v2 = ADAM_B2 * v + (1.0 - ADAM_B2) * (g * g)
    m_hat = m2 / (1.0 - ADAM_B1 ** ADAM_STEP)
    v_hat = v2 / (1.0 - ADAM_B2 ** ADAM_STEP)
    delta = (-ADAM_LR) * (m_hat / (jnp.sqrt(v_hat) + ADAM_EPS) + ADAM_WD * w)
    return delta, m2, v2


ADAMW_BLOCK_BYTES = 1 << 20


def _adamw_big(g, w, m, v, name):
    R, C = g.shape
    bc = min(C, max(LANES, ADAMW_BLOCK_BYTES // (4 * R) // LANES * LANES))

    def body(g_ref, w_ref, m_ref, v_ref, d_ref, m2_ref, v2_ref):
        d_ref[...], m2_ref[...], v2_ref[...] = _adamw_math(g_ref[...], w_ref[...], m_ref[...], v_ref[...])

    spec = pl.BlockSpec((R, bc), lambda j: (0, j))
    out = jax.ShapeDtypeStruct((R, C), F32)
    return _call(
        body, name=name, grid=(C // bc,),
        in_specs=[spec] * 4, out_specs=[spec] * 3, out_shape=[out] * 3,
        compiler_params=_cparams(("parallel",)),
    )(g, w, m, v)


def _adamw_small(gs, ws, ms, vs):
    n = len(gs)

    def body(*refs):
        for a in range(n):
            g_ref, w_ref, m_ref, v_ref = (refs[k * n + a] for k in range(4))
            d_ref, m2_ref, v2_ref = (refs[(4 + k) * n + a] for k in range(3))
            d_ref[...], m2_ref[...], v2_ref[...] = _adamw_math(g_ref[...], w_ref[...], m_ref[...], v_ref[...])

    outs = [jax.ShapeDtypeStruct(w.shape, F32) for w in ws]
    specs = [_const_spec(w.shape) for w in ws]
    return _call(
        body, name="adamw_small", grid=(1,),
        in_specs=specs * 4, out_specs=specs * 3, out_shape=outs * 3,
    )(*gs, *ws, *ms, *vs)


def _late_weights(st_out, st_ple, st_gate, st_conv):
    return st_out.reshape(DMIX, D), _from_chip_cols(st_ple), st_gate.reshape(D, D), _from_chip_cols(st_conv)


def _local_step(x, p, tgt, w_a, w_f, w_b, late, b_f, pre_gain, post_gain, conv_b,
                w_rgate, b_rgate, w_igate, b_igate, lam, gain_a, gain_l, ple_gain, b_gate,
                gather_late=False, early_reduce=None, w_in_reduce=None):
    b_f_pad = jnp.pad(b_f, ((0, 0), (0, LANES - H)))
    w_r = w_rgate.astype(BF16)
    w_i = w_igate.astype(BF16)

    xn, q_aug, k_aug, v_aug, g_attn, x_lru, g_lru, flb, vt_aug = _in_proj(x, pre_gain, w_a, w_f, w_b, b_f_pad)
    if gather_late:
        o, qx, stacks = _attn_fwd(q_aug, k_aug, vt_aug, late[:3], late[3:])
        late = _late_weights(*stacks)
    else:
        o, qx, _ = _attn_fwd(q_aug, k_aug, vt_aug)
    w_out_b, w_ple_b, w_gate_b, conv_w = late
    ycat, xc, h = _branches_fwd(o, g_attn, x_lru, g_lru, gain_a, gain_l, conv_w, conv_b, w_r, b_rgate, w_i, b_igate,
                                lam)
    dh1, dycat, dmix, h1b, dgp, pb, dpe, acc_t = _tail(ycat, x, p, tgt, w_out_b, post_gain, w_ple_b, ple_gain,
                                                       w_gate_b, b_gate)
    do_aug, dg_attn, dg_lru, dh, acc_b, gw_out, gw_gate, gw_ple = _branches_bwd(
        dycat, o, g_attn, h, g_lru, gain_a, gain_l, ycat, dmix, h1b, dgp, pb, dpe)
    late_grads = [gw_out, gw_ple, gw_gate]
    if early_reduce is None:
        dx_lru, gw_r, gw_i, acc_l, _ = _lru_bwd(dh, h, xc, x_lru, conv_w, w_r, b_rgate, w_i, b_igate, lam)
    else:
        parts = [gw_out.reshape(N_CHIPS, DMIX // N_CHIPS, D), _by_chip_cols(gw_ple),
                 gw_gate.reshape(N_CHIPS, D // N_CHIPS, D)]
        dx_lru, gw_r, gw_i, acc_l, got = _lru_bwd(dh, h, xc, x_lru, conv_w, w_r, b_rgate, w_i, b_igate, lam, parts)
        sent = _pair_sum(parts, got, early_reduce)
    if early_reduce is None:
        dq, dk, dv, dc_heads, _ = _attn_bwd(q_aug, qx, k_aug, v_aug, do_aug)
    else:
        dq, dk, dv, dc_heads, received = _attn_bwd(q_aug, qx, k_aug, v_aug, do_aug, sent)
        late_grads = list(zip(sent, received))
    dfl, acc_f = _fgate_bwd(dc_heads, flb)
    dz = (dq, dk, dv, dg_attn, dx_lru, dg_lru)
    grad_t = _dw_in_t(dz, dfl, xn)
    if w_in_reduce is None:
        grad_x, acc_x, _ = _dx(dz, dfl, w_a, w_f, w_b, x, pre_gain, dh1)
    else:
        sent = w_in_reduce(grad_t)
        grad_x, acc_x, (received,) = _dx(dz, dfl, w_a, w_f, w_b, x, pre_gain, dh1, [sent])
        grad_t = (sent, received)

    grads = dict(
        w_in_t=grad_t,
        w_out=late_grads[0],
        w_ple=late_grads[1],
        w_ple_gate=late_grads[2],
        w_rgate=gw_r,
        w_igate=gw_i,
        b_f=acc_f[0:1, :H],
        pre_gain=acc_x[0:1],
        post_gain=acc_t[0:1],
        conv_w=acc_l[0:4],
        conv_b=acc_l[4:5],
        b_rgate=acc_l[5:6],
        b_igate=acc_l[6:7],
        lru_lambda=acc_l[7:8],
        attn_out_gain=acc_b[0:1],
        lru_out_gain=acc_b[1:2],
        ple_gain=acc_t[1:2],
        b_ple_gate=acc_t[2:3],
    )
    loss = jnp.sum(acc_t[3])
    return loss, grad_x, grads


SMALL_ROWS = ["b_f", "pre_gain", "post_gain", "conv_w", "conv_b", "b_rgate", "b_igate", "lru_lambda",
              "attn_out_gain", "lru_out_gain", "ple_gain", "b_ple_gate"]
WEIGHTS = ["w_in", "b_f", "pre_gain", "post_gain", "conv_w", "conv_b", "w_rgate", "b_rgate", "w_igate", "b_igate",
           "lru_lambda", "attn_out_gain", "lru_out_gain", "w_out", "w_ple", "ple_gain", "w_ple_gate", "b_ple_gate"]
SHARDED = ["w_in", "w_out", "w_ple", "w_ple_gate"]


def _by_chip_cols(g):
    r, cols = g.shape
    return g.reshape(r, N_CHIPS, cols // N_CHIPS).transpose(1, 0, 2)


def _from_chip_cols(s):
    n, r, cols = s.shape
    return s.transpose(1, 0, 2).reshape(r, n * cols)


def kernel(x, p, w_in, b_f, pre_gain, post_gain, conv_w, conv_b, w_rgate, b_rgate, w_igate, b_igate, lru_lambda, attn_out_gain, lru_out_gain, w_out, w_ple, ple_gain, w_ple_gate, b_ple_gate, loss_target, m_w_in, m_b_f, m_pre_gain, m_post_gain, m_conv_w, m_conv_b, m_w_rgate, m_b_rgate, m_w_igate, m_b_igate, m_lru_lambda, m_attn_out_gain, m_lru_out_gain, m_w_out, m_w_ple, m_ple_gain, m_w_ple_gate, m_b_ple_gate, v_w_in, v_b_f, v_pre_gain, v_post_gain, v_conv_w, v_conv_b, v_w_rgate, v_b_rgate, v_w_igate, v_b_igate, v_lru_lambda, v_attn_out_gain, v_lru_out_gain, v_w_out, v_w_ple, v_ple_gain, v_w_ple_gate, v_b_ple_gate):
    w = dict(w_in=w_in, b_f=b_f, pre_gain=pre_gain, post_gain=post_gain, conv_w=conv_w, conv_b=conv_b,
             w_rgate=w_rgate, b_rgate=b_rgate, w_igate=w_igate, b_igate=b_igate, lru_lambda=lru_lambda,
             attn_out_gain=attn_out_gain, lru_out_gain=lru_out_gain, w_out=w_out, w_ple=w_ple, ple_gain=ple_gain,
             w_ple_gate=w_ple_gate, b_ple_gate=b_ple_gate)
    m = dict(w_in=m_w_in, b_f=m_b_f, pre_gain=m_pre_gain, post_gain=m_post_gain, conv_w=m_conv_w, conv_b=m_conv_b,
             w_rgate=m_w_rgate, b_rgate=m_b_rgate, w_igate=m_w_igate, b_igate=m_b_igate, lru_lambda=m_lru_lambda,
             attn_out_gain=m_attn_out_gain, lru_out_gain=m_lru_out_gain, w_out=m_w_out, w_ple=m_w_ple,
             ple_gain=m_ple_gain, w_ple_gate=m_w_ple_gate, b_ple_gate=m_b_ple_gate)
    v = dict(w_in=v_w_in, b_f=v_b_f, pre_gain=v_pre_gain, post_gain=v_post_gain, conv_w=v_conv_w, conv_b=v_conv_b,
             w_rgate=v_w_rgate, b_rgate=v_b_rgate, w_igate=v_w_igate, b_igate=v_b_igate, lru_lambda=v_lru_lambda,
             attn_out_gain=v_attn_out_gain, lru_out_gain=v_lru_out_gain, w_out=v_w_out, w_ple=v_w_ple,
             ple_gain=v_ple_gain, w_ple_gate=v_w_ple_gate, b_ple_gate=v_b_ple_gate)
    xi, yi, ci = _position()
    chip = 2 * xi + yi

    w_in_t, m_in_t, v_in_t = (jnp.swapaxes(t[0], 0, 1) for t in (w_in, m_w_in, v_w_in))
    window = jnp.pad(w_in_t.astype(BF16), ((0, W_ROWS - SHARD_ROWS), (0, 0)))

    (st_in,) = _gather_shards([window], [])
    w_a, w_f, w_b = _assemble_w_in(st_in)
    late_shards = (w_out[0].astype(BF16), w_ple[0].astype(BF16), w_ple_gate[0].astype(BF16), conv_w[0])

    loss, grad_x, g = _local_step(
        x[0], p[0, 0], loss_target[0], w_a, w_f, w_b, late_shards, b_f, pre_gain, post_gain,
        conv_b, w_rgate[0], b_rgate, w_igate[0], b_igate, lru_lambda, attn_out_gain, lru_out_gain, ple_gain,
        b_ple_gate, gather_late=True, early_reduce=ci,
        w_in_reduce=lambda grad_t: _pair_sum_windows(grad_t, _pair_exchange_windows(grad_t), ci))

    sums = [g["w_in_t"][0]] + [g[n][0] for n in SHARDED[1:]]
    recv = [g["w_in_t"][1]] + [g[n][1] for n in SHARDED[1:]]
    halves = [_chip_sum(sums[a], recv[a], chip, "chip_sum_%d" % a) for a in range(4)]

    rows = [jnp.pad(g["b_f"], ((0, 0), (0, D - H)))] + [g[n] for n in SMALL_ROWS[1:]]
    rows.append(jnp.pad(loss.reshape(1, 1), ((0, 0), (0, D - 1))))
    packed = jnp.concatenate([g["w_rgate"].reshape(NB * LANES, LANES), g["w_igate"].reshape(NB * LANES, LANES),
                              jnp.concatenate(rows, axis=0).reshape(LANES, LANES)], axis=0)
    theirs, summed = _final_exchange(halves, packed)
    full = [jnp.concatenate([jnp.where(ci == 0, a, b), jnp.where(ci == 0, b, a)], axis=0)
            for a, b in zip(halves, theirs)]
    red = dict(zip(SHARDED, full))
    red["w_in"] = lax.dynamic_slice_in_dim(red["w_in"], 2 * chip, SHARD_ROWS, axis=0)
    red["w_rgate"] = summed[:D].reshape(1, NB, LANES, LANES)
    red["w_igate"] = summed[D:2 * D].reshape(1, NB, LANES, LANES)
    vec = summed[2 * D:].reshape(16, D)
    loss = vec[15, 0]
    r0 = 0
    for n in SMALL_ROWS:
        nr = 4 if n == "conv_w" else 1
        red[n] = vec[r0:r0 + nr]
        r0 += nr
    red["b_f"] = red["b_f"][:, :H]
    red["conv_w"] = lax.dynamic_slice_in_dim(red["conv_w"], chip * (D // N_CHIPS), D // N_CHIPS, axis=1)[None]

    delta, new_m, new_v = {}, {}, {}
    outs_in = _adamw_big(red["w_in"], w_in_t, m_in_t, v_in_t, "adamw_w_in")
    delta["w_in"], new_m["w_in"], new_v["w_in"] = (jnp.swapaxes(t, 0, 1)[None] for t in outs_in)
    red["w_in"] = jnp.swapaxes(red["w_in"], 0, 1)[None]
    for n in SHARDED[1:]:
        delta[n], new_m[n], new_v[n] = (t[None] for t in _adamw_big(red[n], w[n][0], m[n][0], v[n][0], "adamw_" + n))
        red[n] = red[n][None]
    small = [n for n in WEIGHTS if n not in SHARDED]
    outs = _adamw_small([red[n] for n in small], [w[n] for n in small], [m[n] for n in small],
                        [v[n] for n in small])
    ns = len(small)
    for a, n in enumerate(small):
        delta[n], new_m[n], new_v[n] = outs[a], outs[ns + a], outs[2 * ns + a]

    return (loss, grad_x[None], *[red[n] for n in WEIGHTS], *[delta[n] for n in WEIGHTS],
            *[new_m[n] for n in WEIGHTS], *[new_v[n] for n in WEIGHTS])
```

```python
import jax
import jax.numpy as jnp
import numpy as np
from jax import lax
from jax.experimental import pallas as pl
from jax.experimental.pallas import tpu as pltpu

F32 = jnp.float32
BF16 = jnp.bfloat16

D = 1024
H = 8
DH = 128
NB = 8
DPLE = 256
DMIX = 2 * D
D_IN = 4 * D + H + 2 * D
FL0 = 3 * D
RMS_EPS = 1e-6
LRU_C = 8.0
NEG = -1e30
LANES = 128
SUBLANES = 8
BF16_ROWS = 16
COL_BLOCK = 256
DW_TOKENS = 2048

ADAM_LR = 0.001
ADAM_B1 = 0.9
ADAM_B2 = 0.999
ADAM_EPS = 1e-08
ADAM_WD = 0.01
ADAM_STEP = 10

TM = 256
TA = 512
TA_FWD = 1024
FWD_HEADS = 4
BWD_HEADS = 2
VMEM_BIG = 56 * 1024 * 1024
VMEM_MID = 40 * 1024 * 1024

MESH = pl.DeviceIdType.MESH
N_CHIPS = 4
N_DEV = 8


def _call(body, *, out_shape, in_hbm=True, **kwargs):
    if not in_hbm:
        return pl.pallas_call(body, out_shape=out_shape, **kwargs)

    def pin(shape):
        return pltpu.HBM(shape.shape, shape.dtype) if isinstance(shape, jax.ShapeDtypeStruct) else shape

    fn = pl.pallas_call(body, out_shape=jax.tree.map(pin, out_shape), **kwargs)

    def run(*args):
        return fn(*[a if a.dtype == jnp.int32 else pltpu.with_memory_space_constraint(a, pltpu.HBM) for a in args])

    return run


def _cparams(sem, vmem=VMEM_MID):
    return pltpu.CompilerParams(dimension_semantics=sem, vmem_limit_bytes=vmem)


def _sigmoid(x):
    return 0.5 * jnp.tanh(0.5 * x) + 0.5


def _rstd(x):
    return lax.rsqrt(jnp.mean(x * x, axis=-1, keepdims=True) + RMS_EPS)


def _rms_bwd(t, xhat, rstd):
    return rstd * (t - xhat * jnp.mean(t * xhat, axis=-1, keepdims=True))


def _dot(a, b):
    return jnp.dot(a, b, preferred_element_type=F32)


def _dot_nt(a, b):
    return lax.dot_general(a, b, (((1,), (1,)), ((), ())), preferred_element_type=F32)


def _dot_tn(a, b):
    return lax.dot_general(a, b, (((0,), (0,)), ((), ())), preferred_element_type=F32)


def _dot_exact(a, b):
    return jnp.dot(a, b, preferred_element_type=F32, precision=lax.Precision.HIGHEST)


def _shift_down(x, j, halo):
    rolled = pltpu.roll(x, j, 0)
    row = lax.broadcasted_iota(jnp.int32, halo.shape, 0)
    top = jnp.where(row < j, pltpu.roll(halo, j, 0), rolled[:SUBLANES])
    return jnp.concatenate([top, rolled[SUBLANES:]], axis=0)


def _shift_up(x, j, nxt):
    tm = x.shape[0]
    rolled = pltpu.roll(x, tm - j, 0)
    row = lax.broadcasted_iota(jnp.int32, nxt.shape, 0)
    bot = jnp.where(row >= SUBLANES - j, pltpu.roll(nxt, SUBLANES - j, 0), rolled[tm - SUBLANES:])
    return jnp.concatenate([rolled[:tm - SUBLANES], bot], axis=0)


def _scan_fwd_into(a, u, carry, h_ref):
    tm, width = a.shape
    groups = (tm // SUBLANES, SUBLANES, width)
    a, u = a.reshape(groups), u.reshape(groups)
    sub = lax.broadcasted_iota(jnp.int32, groups, 1)
    d = 1
    while d < SUBLANES:
        keep = sub >= d
        a_s = jnp.where(keep, pltpu.roll(a, d, 1), 1.0)
        u_s = jnp.where(keep, pltpu.roll(u, d, 1), 0.0)
        u = u + a * u_s
        a = a * a_s
        d *= 2
    a, u = a.reshape(tm, width), u.reshape(tm, width)
    for g in range(tm // SUBLANES):
        rows = slice(g * SUBLANES, (g + 1) * SUBLANES)
        h_ref[rows, :] = u[rows] + a[rows] * carry
        carry = h_ref[(g + 1) * SUBLANES - 1:(g + 1) * SUBLANES, :]
    return carry


def _scan_bwd_into(b, u, g_ref):
    tm, width = b.shape
    groups = (tm // SUBLANES, SUBLANES, width)
    b, u = b.reshape(groups), u.reshape(groups)
    sub = lax.broadcasted_iota(jnp.int32, groups, 1)
    d = 1
    while d < SUBLANES:
        keep = sub < SUBLANES - d
        b_s = jnp.where(keep, pltpu.roll(b, SUBLANES - d, 1), 1.0)
        u_s = jnp.where(keep, pltpu.roll(u, SUBLANES - d, 1), 0.0)
        u = u + b * u_s
        b = b * b_s
        d *= 2
    b, u = b.reshape(tm, width), u.reshape(tm, width)
    nxt = jnp.zeros((1, width), F32)
    for g in reversed(range(tm // SUBLANES)):
        rows = slice(g * SUBLANES, (g + 1) * SUBLANES)
        g_ref[rows, :] = u[rows] + b[rows] * nxt
        nxt = g_ref[g * SUBLANES:g * SUBLANES + 1, :]


def _gate_pre(xc, w_ref):
    outs = []
    for n in range(NB):
        outs.append(_dot(xc[:, n * LANES:(n + 1) * LANES].astype(BF16), w_ref[n]))
    return jnp.concatenate(outs, axis=1)


def _gate_pre_t(d, w_ref):
    outs = []
    for n in range(NB):
        outs.append(_dot_nt(d[:, n * LANES:(n + 1) * LANES].astype(BF16), w_ref[n]))
    return jnp.concatenate(outs, axis=1)


def _softplus_neg(lam):
    return jnp.maximum(-lam, 0.0) + jnp.log(1.0 + jnp.exp(-jnp.abs(lam)))


def _row_spec(tm, width):
    return pl.BlockSpec((tm, width), lambda i: (i, 0))


def _const_spec(shape):
    nd = len(shape)
    return pl.BlockSpec(shape, lambda *_: (0,) * nd)


def _weight_spec(shape):
    nd = len(shape)
    return pl.BlockSpec(shape, lambda *_: (0,) * nd, pipeline_mode=pl.Buffered(1))


AUG = 2 * DH
LOG2E = 1.4426950408889634
LN2 = 0.6931471805599453
Q_SCALE = DH ** -0.5 * LOG2E


def _split3(x):
    hi = x.astype(BF16)
    r1 = x - hi.astype(F32)
    mid = r1.astype(BF16)
    lo = (r1 - mid.astype(F32)).astype(BF16)
    return hi, mid, lo


def _extras(col, ones_from):
    t = col.shape[0]
    hi, mid, lo = _split3(jnp.broadcast_to(col, (t, LANES)))
    lane = lax.broadcasted_iota(jnp.int32, (t, LANES), 1)
    rest = jnp.zeros((t, LANES), BF16)
    if ones_from is not None:
        rest = jnp.where((lane >= ones_from) & (lane < ones_from + 3), 1.0, 0.0).astype(BF16)
    return jnp.where(lane == 0, hi, jnp.where(lane == 1, mid, jnp.where(lane == 2, lo, rest)))


def _selectors():
    sel_q = np.zeros((3 * LANES, H * LANES), np.float32)
    sel_k = np.zeros((3 * LANES, H * LANES), np.float32)
    for hd in range(H):
        for piece in range(3):
            sel_q[piece * LANES + hd, hd * LANES + piece] = 1.0
            sel_k[piece * LANES + hd, hd * LANES + 3 + piece] = -1.0
    return jnp.asarray(sel_q, BF16), jnp.asarray(sel_k, BF16)


def _in_proj(x, pre_gain, w_a, w_f, w_b, b_f_pad):
    T = x.shape[0]
    tm = TM
    sel_q, sel_k = _selectors()

    def body(x_ref, g_ref, wa_ref, wf_ref, wb_ref, bf_ref, sq_ref, sk_ref,
             xn_ref, qa_ref, ka_ref, va_ref, ga_ref, xl_ref, gl_ref, flb_ref, vt_ref, c_s, carry):
        @pl.when(pl.program_id(0) == 0)
        def _():
            carry[...] = jnp.zeros_like(carry)

        xv = x_ref[...]
        xn = (xv * _rstd(xv) * g_ref[...]).astype(BF16)
        xn_ref[...] = xn
        for s, o_ref in enumerate((ga_ref, xl_ref, gl_ref)):
            o_ref[...] = _dot_nt(xn, wb_ref[s * D:(s + 1) * D, :]).astype(o_ref.dtype)
        flb = _dot_nt(xn, wf_ref[...]) + bf_ref[...]
        flb_ref[...] = flb
        lane = lax.broadcasted_iota(jnp.int32, flb.shape, 1)
        ls = jnp.where(lane < H, jnp.minimum(flb, 0.0) - jnp.log(1.0 + jnp.exp(-jnp.abs(flb))), 0.0)
        r = lax.broadcasted_iota(jnp.int32, (tm, tm), 0)
        c = lax.broadcasted_iota(jnp.int32, (tm, tm), 1)
        cs = _dot_exact((c <= r).astype(F32), ls) + carry[...]
        c_s[...] = cs
        carry[...] = c_s[tm - 1:tm, :]

        pieces = jnp.concatenate(_split3(cs * LOG2E), axis=1)
        ones_q = jnp.where((lane >= 3) & (lane < 6), 1.0, 0.0)
        ones_k = jnp.where(lane < 3, 1.0, 0.0)
        zq = _dot_nt(xn, wa_ref[0:D, :]) * Q_SCALE
        zk = _dot_nt(xn, wa_ref[D:2 * D, :])
        zv = _dot_nt(xn, wa_ref[2 * D:3 * D, :])
        ex_q = _dot(pieces, sq_ref[...])
        ex_k = _dot(pieces, sk_ref[...])
        for hd in range(H):
            head = slice(hd * DH, (hd + 1) * DH)
            lo, hi = hd * AUG, hd * AUG + DH
            qa_ref[:, lo:hi] = zq[:, head].astype(BF16)
            qa_ref[:, hi:hi + DH] = (ex_q[:, head] + ones_q).astype(BF16)
            ka_ref[:, lo:hi] = zk[:, head].astype(BF16)
            ka_ref[:, hi:hi + DH] = (ex_k[:, head] + ones_k).astype(BF16)
            va_ref[:, lo:hi] = zv[:, head].astype(BF16)
            va_ref[:, hi:hi + DH] = ones_k.astype(BF16)
            vt_ref[lo:hi, :] = jnp.transpose(zv[:, head]).astype(BF16)
            vt_ref[hi:hi + DH, :] = jnp.where(lax.broadcasted_iota(jnp.int32, (DH, tm), 0) < 3, 1.0, 0.0).astype(BF16)

    bf = jax.ShapeDtypeStruct((T, D), BF16)
    aug = jax.ShapeDtypeStruct((T, H * AUG), BF16)
    f32 = jax.ShapeDtypeStruct((T, D), F32)
    sel_spec = _const_spec((3 * LANES, H * LANES))
    return _call(
        body, name="in_proj", grid=(T // tm,),
        in_specs=[_row_spec(tm, D), _const_spec((1, D)), _const_spec((3 * D, D)), _const_spec((LANES, D)),
                  _const_spec((3 * D, D)), _const_spec((1, LANES)), sel_spec, sel_spec],
        out_specs=[_row_spec(tm, D)] + [_row_spec(tm, H * AUG)] * 3 + [_row_spec(tm, D)] * 3 + [_row_spec(tm, LANES)]
        + [pl.BlockSpec((H * AUG, tm), lambda i: (0, i))],
        out_shape=[bf, aug, aug, aug, f32, f32, f32, jax.ShapeDtypeStruct((T, LANES), F32),
                   jax.ShapeDtypeStruct((H * AUG, T), BF16)],
        scratch_shapes=[pltpu.VMEM((tm, LANES), F32), pltpu.VMEM((1, LANES), F32)],
        compiler_params=_cparams(("arbitrary",), VMEM_BIG),
    )(x, pre_gain, w_a, w_f, w_b, b_f_pad, sel_q, sel_k)


def _causal_pairs(n, q_major):
    if q_major:
        pairs = [(qi, ki) for qi in range(n) for ki in range(qi + 1)]
    else:
        pairs = [(ki, qi) for ki in range(n) for qi in range(ki, n)]
    return (jnp.asarray([a for a, _ in pairs], jnp.int32), jnp.asarray([b for _, b in pairs], jnp.int32))


def _attn_fwd(q_aug, k_aug, vt_aug, shards=(), whole=()):
    T = q_aug.shape[0]
    t = min(T, TA_FWD)
    n = T // t
    hp = FWD_HEADS
    heads = range(hp)
    qi_tab, ki_tab = _causal_pairs(n, q_major=True)
    na, nall = len(shards), len(shards) + len(whole)
    n_h, n_j = H // hp, qi_tab.shape[0]

    def body(qi_ref, ki_ref, q_ref, k_ref, vt_ref, *rest):
        srcs, rest = rest[:nall], rest[nall:]
        o_ref, qx_ref = rest[:2]
        dsts, rest = rest[2:2 + nall], rest[2 + nall:]
        m_s, acc_s = rest[:2]
        h = pl.program_id(0)
        j = pl.program_id(1)
        qi = qi_ref[j]
        ki = ki_ref[j]

        if nall:
            gather = _GatherPlan(srcs, dsts, rest[2:], na)
            step = h * n_j + j
            pl.when(step == 0)(gather.send)
            pl.when(step == n_h * n_j // 2)(gather.forward)
            pl.when(step == n_h * n_j - 1)(gather.finish)

        @pl.when(ki == 0)
        def _():
            m_s[...] = jnp.full(m_s.shape, NEG, F32)
            acc_s[...] = jnp.zeros_like(acc_s)

        def step(on_diagonal):
            cols = [slice(a * AUG, (a + 1) * AUG) for a in heads]
            if on_diagonal:
                krow = lax.broadcasted_iota(jnp.int32, (t, t), 0)
                qcol = lax.broadcasted_iota(jnp.int32, (t, t), 1)
            def logits(a):
                st = _dot_nt(k_ref[:, cols[a]], q_ref[:, cols[a]])
                return jnp.where(krow <= qcol, st, NEG) if on_diagonal else st

            st_next = logits(0)
            for a in heads:
                st = st_next
                if a + 1 < hp:
                    st_next = logits(a + 1)
                m_prev = m_s[a]
                m_new = jnp.maximum(m_prev, jnp.max(st, axis=0, keepdims=True))
                pt = jnp.exp2(st - m_new).astype(BF16)
                acc_s[a] = jnp.exp2(m_prev - m_new) * acc_s[a] + _dot(vt_ref[cols[a], :], pt)
                m_s[a] = m_new

        @pl.when(ki < qi)
        def _():
            step(False)

        @pl.when(ki == qi)
        def _():
            step(True)
            piece = lax.broadcasted_iota(jnp.int32, (DH, t), 0)
            for a in heads:
                l = acc_s[a, DH:DH + 1, :]
                ex = jnp.transpose(q_ref[:, a * AUG + DH:(a + 1) * AUG].astype(F32))
                c2 = jnp.sum(jnp.where(piece < 3, ex, 0.0), axis=0, keepdims=True)
                hi, mid, lo = _split3(jnp.broadcast_to(c2 - (m_s[a] + jnp.log(l) * LOG2E), (DH, t)))
                ones = jnp.where((piece >= 3) & (piece < 6), 1.0, 0.0).astype(BF16)
                ex_t = jnp.where(piece == 0, hi, jnp.where(piece == 1, mid, jnp.where(piece == 2, lo, ones)))
                o_ref[:, a * DH:(a + 1) * DH] = jnp.transpose(acc_s[a, :DH, :] / l)
                qx_ref[:, a * DH:(a + 1) * DH] = jnp.transpose(ex_t.astype(F32)).astype(BF16)

    q_spec = pl.BlockSpec((t, hp * AUG), lambda h, j, qi_ref, ki_ref: (qi_ref[j], h))
    k_spec = pl.BlockSpec((t, hp * AUG), lambda h, j, qi_ref, ki_ref: (ki_ref[j], h))
    vt_spec = pl.BlockSpec((hp * AUG, t), lambda h, j, qi_ref, ki_ref: (h, ki_ref[j]))
    out_spec = pl.BlockSpec((t, hp * DH), lambda h, j, qi_ref, ki_ref: (qi_ref[j], h))
    arrs = list(shards) + list(whole)
    grid_spec = pltpu.PrefetchScalarGridSpec(
        num_scalar_prefetch=2, grid=(n_h, n_j),
        in_specs=[q_spec, k_spec, vt_spec] + [HBM_SPEC] * nall, out_specs=[out_spec, out_spec] + [HBM_SPEC] * nall,
        scratch_shapes=[pltpu.VMEM((hp, 1, t), F32), pltpu.VMEM((hp, AUG, t), F32)]
        + (_gather_semaphores(na, nall) if nall else []))
    outs = _call(
        body, name="attn_fwd", grid_spec=grid_spec,
        out_shape=[jax.ShapeDtypeStruct((T, D), F32), jax.ShapeDtypeStruct((T, D), BF16)] + _gather_out_shapes(arrs),
        compiler_params=_cparams(("arbitrary", "arbitrary"), VMEM_BIG),
    )(qi_tab, ki_tab, q_aug, k_aug, vt_aug, *arrs)
    return outs[0], outs[1], _place_own(outs[2:], arrs)


def _sigmoid_small(x):
    e = jnp.exp(jnp.minimum(x, 0.0))
    return jnp.where(x < -8.0, e * (1.0 - e), _sigmoid(x))


def _lru_gates(xc, wr_ref, br_ref, wi_ref, bi_ref, lam_ref):
    r = _sigmoid_small(_gate_pre(xc, wr_ref) + br_ref[...])
    ig = _sigmoid(_gate_pre(xc, wi_ref) + bi_ref[...])
    sp = _softplus_neg(lam_ref[...])
    la = (-LRU_C) * r * sp
    a = jnp.exp(la)
    y = -jnp.tanh(la) * (a * a + 1.0)
    return r, ig, sp, a, jnp.sqrt(y), lax.rsqrt(y)


def _branches_fwd(o, g_attn, x_lru, g_lru, gain_a, gain_l, conv_w, conv_b, w_r, b_r, w_i, b_i, lam):
    T = o.shape[0]
    tm = TM

    def body(o_ref, ga_ref, xl_ref, gl_ref, gna_ref, gnl_ref, cw_ref, cb_ref, wr_ref, br_ref, wi_ref, bi_ref,
             lam_ref, ycat_ref, xc_ref, h_ref, halo_s, hc_s):
        @pl.when(pl.program_id(0) == 0)
        def _():
            halo_s[...] = jnp.zeros_like(halo_s)
            hc_s[...] = jnp.zeros_like(hc_s)

        ov = o_ref[...]
        ga = ga_ref[...]
        ya = ov * _rstd(ov) * gna_ref[...] * (ga * _sigmoid(ga))
        ycat_ref[:, :D] = ya.astype(BF16)

        xl = xl_ref[...]
        halo = halo_s[...]
        xc = xl * cw_ref[3:4, :] + cb_ref[...]
        for j in range(3):
            xc = xc + _shift_down(xl, 3 - j, halo) * cw_ref[j:j + 1, :]
        halo_s[...] = xl_ref[tm - SUBLANES:tm, :]
        xc_ref[...] = xc

        _, ig, _, a, sq, _ = _lru_gates(xc, wr_ref, br_ref, wi_ref, bi_ref, lam_ref)
        u = sq * (ig * xc)
        hc_s[...] = _scan_fwd_into(a, u, hc_s[...], h_ref)
        hh = h_ref[...]

        gl = gl_ref[...]
        yl = hh * _rstd(hh) * gnl_ref[...] * (gl * _sigmoid(gl))
        ycat_ref[:, D:] = yl.astype(BF16)

    vec = _const_spec((1, D))
    wspec = _const_spec((NB, LANES, LANES))
    return _call(
        body, name="branches_fwd", grid=(T // tm,),
        in_specs=[_row_spec(tm, D)] * 4 + [vec, vec, _const_spec((4, D)), vec, wspec, vec, wspec, vec, vec],
        out_specs=[_row_spec(tm, DMIX), _row_spec(tm, D), _row_spec(tm, D)],
        out_shape=[jax.ShapeDtypeStruct((T, DMIX), BF16), jax.ShapeDtypeStruct((T, D), F32),
                   jax.ShapeDtypeStruct((T, D), F32)],
        scratch_shapes=[pltpu.VMEM((SUBLANES, D), F32), pltpu.VMEM((1, D), F32)],
        compiler_params=_cparams(("arbitrary",)),
    )(o, g_attn, x_lru, g_lru, gain_a, gain_l, conv_w, conv_b, w_r, b_r, w_i, b_i, lam)


def _tail(ycat, x, p, tgt, w_out, post_gain, w_ple, ple_gain, w_gate, b_gate):
    T = x.shape[0]
    tm = TM

    def body(ycat_ref, x_ref, p_ref, t_ref, wo_ref, pg_ref, wp_ref, eg_ref, wg_ref, bg_ref,
             dh1_ref, dycat_ref, dmix_ref, h1b_ref, dgp_ref, pb_ref, dpe_ref, acc_ref):
        @pl.when(pl.program_id(0) == 0)
        def _():
            acc_ref[...] = jnp.zeros_like(acc_ref)

        mix = _dot(ycat_ref[...], wo_ref[...])
        rstd_m = _rstd(mix)
        mhat = mix * rstd_m
        h1 = x_ref[...] + mhat * pg_ref[...]
        pb = p_ref[...].astype(BF16)
        pb_ref[...] = pb
        pe = _dot(pb, wp_ref[...])
        rstd_p = _rstd(pe)
        pehat = pe * rstd_p
        e = pehat * eg_ref[...]
        h1b = h1.astype(BF16)
        h1b_ref[...] = h1b
        gate = _sigmoid(_dot(h1b, wg_ref[...]) + bg_ref[...])
        diff = (h1 + gate * e) - t_ref[...]

        dy = diff * (1.0 / D)
        de = dy * gate
        dgp = (dy * e) * gate * (1.0 - gate)
        dgpb = dgp.astype(BF16)
        dgp_ref[...] = dgpb
        dh1 = dy + _dot_nt(dgpb, wg_ref[...])
        dh1_ref[...] = dh1
        dpe_ref[...] = _rms_bwd(de * eg_ref[...], pehat, rstd_p).astype(BF16)
        dmix = _rms_bwd(dh1 * pg_ref[...], mhat, rstd_m).astype(BF16)
        dmix_ref[...] = dmix
        dycat_ref[...] = _dot_nt(dmix, wo_ref[...])

        acc_ref[0:1, :] += jnp.sum(dh1 * mhat, axis=0, keepdims=True)
        acc_ref[1:2, :] += jnp.sum(de * pehat, axis=0, keepdims=True)
        acc_ref[2:3, :] += jnp.sum(dgp, axis=0, keepdims=True)
        acc_ref[3:4, :] += jnp.sum(diff * diff, axis=0, keepdims=True) * (0.5 / D)

    vec = _const_spec((1, D))
    bf = jax.ShapeDtypeStruct((T, D), BF16)
    return _call(
        body, name="tail", grid=(T // tm,),
        in_specs=[_row_spec(tm, DMIX), _row_spec(tm, D), _row_spec(tm, DPLE), _row_spec(tm, D),
                  _const_spec((DMIX, D)), vec, _const_spec((DPLE, D)), vec, _const_spec((D, D)), vec],
        out_specs=[_row_spec(tm, D), _row_spec(tm, DMIX), _row_spec(tm, D), _row_spec(tm, D), _row_spec(tm, D),
                   _row_spec(tm, DPLE), _row_spec(tm, D), _const_spec((SUBLANES, D))],
        out_shape=[jax.ShapeDtypeStruct((T, D), F32), jax.ShapeDtypeStruct((T, DMIX), F32), bf, bf, bf,
                   jax.ShapeDtypeStruct((T, DPLE), BF16), bf, jax.ShapeDtypeStruct((SUBLANES, D), F32)],
        compiler_params=_cparams(("arbitrary",), VMEM_BIG),
    )(ycat, x, p, tgt, w_out, post_gain, w_ple, ple_gain, w_gate, b_gate)


def _pair_copies(srcs, gots, send_sems, recv_sems):
    x, y, c = _position()
    copies = []
    for a, (src, got) in enumerate(zip(srcs, gots)):
        half = src.shape[1] // 2
        rows = pl.ds(pl.multiple_of((1 - c) * half, SUBLANES), half)
        copies.append(pltpu.make_async_remote_copy(
            src_ref=src.at[:, rows, :], dst_ref=got, send_sem=send_sems.at[a], recv_sem=recv_sems.at[a],
            device_id=(x, y, 1 - c), device_id_type=MESH))
    return copies


def _branches_bwd(dycat, o, g_attn, h, g_lru, gain_a, gain_l, ycat, dmix, h1b, dgp, pb, dpe):
    T = o.shape[0]
    tm = TM
    nt = T // tm
    nb_gate, nb_ple = min(nt, 8), min(nt, 2)
    ns_gate, ns_ple = nt // nb_gate, nt // nb_ple
    br_out, br_gate, br_ple = DMIX // nt, D // nb_gate, DPLE // nb_ple
    tk_gate, tk_ple = T // ns_gate, T // ns_ple

    def body(dy_ref, o_ref, ga_ref, h_ref, gl_ref, gna_ref, gnl_ref, yc_ref, dmix_ref, h1_ref, dgp_ref, pb_ref,
             dpe_ref, do_ref, dga_ref, dgl_ref, dh_ref, acc_ref, gwo_ref, gwg_ref, gwp_ref):
        i = pl.program_id(0)

        @pl.when(i == 0)
        def _():
            acc_ref[...] = jnp.zeros_like(acc_ref)

        def accumulate(out_ref, lhs_ref, rhs_ref, tokens, slices):
            s = i % slices
            part = _dot_tn(lhs_ref[...], rhs_ref[pl.ds(pl.multiple_of(s * tokens, tokens), tokens), :])
            out_ref[...] = part + jnp.where(s == 0, 0.0, out_ref[...])

        gwo_ref[...] = _dot_tn(yc_ref[...], dmix_ref[...])

        def branch(val, g, gain, dyv):
            rstd = _rstd(val)
            vhat = val * rstd
            sig = _sigmoid(g)
            dn = dyv * (g * sig)
            dg = dyv * (vhat * gain) * (sig * (1.0 + g * (1.0 - sig)))
            dgain = jnp.sum(dn * vhat, axis=0, keepdims=True)
            return _rms_bwd(dn * gain, vhat, rstd), dg, dgain

        ov = o_ref[...]
        do, dga, dgain_a = branch(ov, ga_ref[...], gna_ref[...], dy_ref[:, :D])
        dga_ref[...] = dga.astype(BF16)
        prod = do * ov
        for hd in range(H):
            head = slice(hd * DH, (hd + 1) * DH)
            do_ref[:, hd * AUG:hd * AUG + DH] = do[:, head].astype(BF16)
            do_ref[:, hd * AUG + DH:(hd + 1) * AUG] = _extras(-jnp.sum(prod[:, head], axis=1, keepdims=True), None)

        accumulate(gwg_ref, h1_ref, dgp_ref, tk_gate, ns_gate)
        accumulate(gwp_ref, pb_ref, dpe_ref, tk_ple, ns_ple)
        dh, dgl, dgain_l = branch(h_ref[...], gl_ref[...], gnl_ref[...], dy_ref[:, D:])
        dh_ref[...] = dh
        dgl_ref[...] = dgl.astype(BF16)
        acc_ref[0:1, :] += dgain_a
        acc_ref[1:2, :] += dgain_l

    vec = _const_spec((1, D))
    bf = jax.ShapeDtypeStruct((T, D), BF16)
    tokens = _weight_spec((T, D))
    return _call(
        body, name="branches_bwd", grid=(nt,),
        in_specs=[_row_spec(tm, DMIX)] + [_row_spec(tm, D)] * 4 + [vec, vec]
        + [pl.BlockSpec((T, br_out), lambda i: (0, i)), tokens,
           pl.BlockSpec((tk_gate, br_gate), lambda i: (i % ns_gate, i // ns_gate)), tokens,
           pl.BlockSpec((tk_ple, br_ple), lambda i: (i % ns_ple, i // ns_ple)), tokens],
        out_specs=[_row_spec(tm, H * AUG), _row_spec(tm, D), _row_spec(tm, D), _row_spec(tm, D),
                   _const_spec((SUBLANES, D)),
                   pl.BlockSpec((br_out, D), lambda i: (i, 0)),
                   pl.BlockSpec((br_gate, D), lambda i: (i // ns_gate, 0)),
                   pl.BlockSpec((br_ple, D), lambda i: (i // ns_ple, 0))],
        out_shape=[jax.ShapeDtypeStruct((T, H * AUG), BF16), bf, bf, jax.ShapeDtypeStruct((T, D), F32),
                   jax.ShapeDtypeStruct((SUBLANES, D), F32), jax.ShapeDtypeStruct((DMIX, D), F32),
                   jax.ShapeDtypeStruct((D, D), F32), jax.ShapeDtypeStruct((DPLE, D), F32)],
        compiler_params=_cparams(("arbitrary",), VMEM_BIG),
    )(dycat, o, g_attn, h, g_lru, gain_a, gain_l, ycat, dmix, h1b, dgp, pb, dpe)


def _lru_bwd(dh, h, xc, x_lru, conv_w, w_r, b_r, w_i, b_i, lam, pair_parts=()):
    T = dh.shape[0]
    tm = TM
    nt = T // tm
    per = tm // SUBLANES
    npair = len(pair_parts)

    def body(dh_ref, h_ref, hprev_ref, xc_ref, xl_ref, cw_ref, wr_ref, br_ref, wi_ref, bi_ref, lam_ref, *rest):
        parts, rest = rest[:npair], rest[npair:]
        dxl_ref, dwr_ref, dwi_ref, acc_ref = rest[:4]
        gots, rest = rest[4:4 + npair], rest[4 + npair:]
        carry_s, dxc_next_s, top_s, dht_s = rest[:4]
        i = pl.program_id(0)

        @pl.when(i == 0)
        def _():
            acc_ref[...] = jnp.zeros_like(acc_ref)
            dwr_ref[...] = jnp.zeros_like(dwr_ref)
            dwi_ref[...] = jnp.zeros_like(dwi_ref)
            carry_s[...] = jnp.zeros_like(carry_s)
            dxc_next_s[...] = jnp.zeros_like(dxc_next_s)
            for cp in _pair_copies(parts, gots, *rest[4:]) if npair else ():
                cp.start()

        if npair:
            @pl.when(i == nt - 1)
            def _():
                for cp in _pair_copies(parts, gots, *rest[4:]):
                    cp.wait()

        inner = jnp.where(i == nt - 1, 0.0, 1.0)
        xc = xc_ref[...]
        r, ig, sp, a, sq, inv_sq = _lru_gates(xc, wr_ref, br_ref, wi_ref, bi_ref, lam_ref)

        row = lax.broadcasted_iota(jnp.int32, (tm, D), 0)
        u = dh_ref[...] + jnp.where(row == tm - 1, carry_s[...], 0.0)
        _scan_bwd_into(pltpu.roll(a, tm - 1, 0), u, dht_s)
        dht = dht_s[...]
        top_s[...] = a[:SUBLANES, :] * dht[:SUBLANES, :]
        carry_s[...] = top_s[0:1, :]

        hprev = hprev_ref[...] * inner
        da = dht * _shift_down(h_ref[...], 1, hprev)
        dig = dht * sq * xc
        dxc = dht * sq * ig
        dsq = dht * ig * xc
        dla = da * a - dsq * (a * a) * inv_sq
        dr = dla * ((-LRU_C) * sp)
        dpr = dr * r * (1.0 - r)
        dpi = dig * ig * (1.0 - ig)
        for n in range(NB):
            blk = slice(n * LANES, (n + 1) * LANES)
            xcb = xc[:, blk].astype(BF16)
            dwr_ref[n] += _dot_tn(xcb, dpr[:, blk].astype(BF16))
            dwi_ref[n] += _dot_tn(xcb, dpi[:, blk].astype(BF16))
        dxc = dxc + _gate_pre_t(dpr, wr_ref) + _gate_pre_t(dpi, wi_ref)

        xl = xl_ref[...]
        nxt = dxc_next_s[...]
        dxl = dxc * cw_ref[3:4, :]
        acc_ref[3:4, :] += jnp.sum(dxc * xl, axis=0, keepdims=True)
        for j in range(3):
            ahead = _shift_up(dxc, 3 - j, nxt)
            dxl = dxl + ahead * cw_ref[j:j + 1, :]
            acc_ref[j:j + 1, :] += jnp.sum(ahead * xl, axis=0, keepdims=True)
        dxc_next_s[...] = dxc[:SUBLANES, :]
        dxl_ref[...] = dxl.astype(BF16)

        acc_ref[4:5, :] += jnp.sum(dxc, axis=0, keepdims=True)
        acc_ref[5:6, :] += jnp.sum(dpr, axis=0, keepdims=True)
        acc_ref[6:7, :] += jnp.sum(dpi, axis=0, keepdims=True)
        acc_ref[7:8, :] += jnp.sum(dla * ((-LRU_C) * r), axis=0, keepdims=True)

        @pl.when(i == nt - 1)
        def _():
            lam_v = lam_ref[...]
            acc_ref[7:8, :] = acc_ref[7:8, :] * (-_sigmoid(-lam_v))

    rev = pl.BlockSpec((tm, D), lambda i: (nt - 1 - i, 0))
    prev8 = pl.BlockSpec((SUBLANES, D), lambda i: (jnp.maximum((nt - 1 - i) * per - 1, 0), 0))
    vec = _const_spec((1, D))
    wspec = _const_spec((NB, LANES, LANES))
    bf = jax.ShapeDtypeStruct((T, D), BF16)
    halves = [jax.ShapeDtypeStruct((s.shape[0], s.shape[1] // 2, s.shape[2]), s.dtype) for s in pair_parts]
    outs = _call(
        body, name="lru_bwd", grid=(nt,),
        in_specs=[rev, rev, prev8, rev, rev, _const_spec((4, D)), wspec, vec, wspec, vec, vec] + [HBM_SPEC] * npair,
        out_specs=[rev, wspec, wspec, _const_spec((SUBLANES, D))] + [HBM_SPEC] * npair,
        out_shape=[bf, jax.ShapeDtypeStruct((NB, LANES, LANES), F32), jax.ShapeDtypeStruct((NB, LANES, LANES), F32),
                   jax.ShapeDtypeStruct((SUBLANES, D), F32)] + halves,
        scratch_shapes=[pltpu.VMEM((1, D), F32), pltpu.VMEM((SUBLANES, D), F32), pltpu.VMEM((SUBLANES, D), F32),
                        pltpu.VMEM((tm, D), F32)]
        + ([pltpu.SemaphoreType.DMA((npair,)), pltpu.SemaphoreType.DMA((npair,))] if npair else []),
        compiler_params=_cparams(("arbitrary",)),
    )(dh, h, h, xc, x_lru, conv_w, w_r, b_r, w_i, b_i, lam, *pair_parts)
    return (*outs[:4], list(outs[4:]))


def _chip_copies(srcs, dsts, send_sems, recv_sems):
    x, y, c = _position()
    chip = 2 * x + y
    na = len(srcs)
    return [pltpu.make_async_remote_copy(
        src_ref=srcs[a].at[2 * px + py], dst_ref=dsts[a].at[chip], send_sem=send_sems.at[j * na + a],
        recv_sem=recv_sems.at[j * na + a], device_id=(px, py, c), device_id_type=MESH)
        for j, (px, py) in enumerate(_other_chips(x, y)) for a in range(na)]


def _attn_bwd(q_aug, qx, k_aug, v_aug, do_aug, exchange=()):
    T = q_aug.shape[0]
    t = TA
    n = T // t
    hp = BWD_HEADS
    heads = range(hp)
    scale = DH ** -0.5
    ki_tab, qi_tab = _causal_pairs(n, q_major=False)
    last = ki_tab.shape[0] - 1
    ne = len(exchange)
    n_h = H // hp

    def body(ki_ref, qi_ref, q_ref, qx_ref, k_ref, v_ref, do_ref, *rest):
        sent, rest = rest[:ne], rest[ne:]
        dq_ref, dk_ref, dv_ref, dc_ref = rest[:4]
        received, rest = rest[4:4 + ne], rest[4 + ne:]
        dq_s, dk_s, dv_s = rest[:3]
        j = pl.program_id(1)
        ki = ki_ref[j]
        qi = qi_ref[j]

        if ne:
            first_step = (pl.program_id(0) == 0) & (j == 0)
            last_step = (pl.program_id(0) == n_h - 1) & (j == last)

            @pl.when(first_step)
            def _():
                for cp in _chip_copies(sent, received, *rest[3:]):
                    cp.start()

            @pl.when(last_step)
            def _():
                for cp in _chip_copies(sent, received, *rest[3:]):
                    cp.wait()

        @pl.when(j == 0)
        def _():
            dq_s[...] = jnp.zeros_like(dq_s)

        @pl.when(qi == ki)
        def _():
            dk_s[...] = jnp.zeros_like(dk_s)
            dv_s[...] = jnp.zeros_like(dv_s)

        def step(on_diagonal):
            cols = [slice(a * AUG, (a + 1) * AUG) for a in heads]
            qb = [jnp.concatenate([q_ref[:, a * AUG:a * AUG + DH], qx_ref[:, a * DH:(a + 1) * DH]], axis=1)
                  for a in heads]
            if on_diagonal:
                krow = lax.broadcasted_iota(jnp.int32, (t, t), 0)
                qcol = lax.broadcasted_iota(jnp.int32, (t, t), 1)

            def scores(a):
                st = _dot_nt(k_ref[:, cols[a]], qb[a])
                dpd = _dot_nt(v_ref[:, cols[a]], do_ref[:, cols[a]])
                return (jnp.where(krow <= qcol, st, NEG) if on_diagonal else st), dpd

            off = pl.multiple_of(qi * t, t)
            ahead = scores(0)
            for a in heads:
                st, dpd = ahead
                if a + 1 < hp:
                    ahead = scores(a + 1)
                pt = jnp.exp2(st)
                dsb = (pt * dpd).astype(BF16)
                dv_s[a] += _dot(pt.astype(BF16), do_ref[:, a * AUG:a * AUG + DH])
                dk_s[a] += _dot(dsb, qb[a])
                dq_s[a, pl.ds(off, t), :] += _dot_tn(dsb, k_ref[:, cols[a]])

        @pl.when(qi > ki)
        def _():
            step(False)

        @pl.when(qi == ki)
        def _():
            step(True)

        @pl.when(qi == n - 1)
        def _():
            rows = pl.ds(pl.multiple_of(ki * t, t), t)
            for a in heads:
                dk_ref[:, a * DH:(a + 1) * DH] = (dk_s[a, :, :DH] * LN2).astype(BF16)
                dv_ref[:, a * DH:(a + 1) * DH] = dv_s[a].astype(BF16)
                dc_ref[a, rows, :] = jnp.broadcast_to(-dk_s[a, :, DH + 3:DH + 4], (t, LANES))

        @pl.when(j == last)
        def _():
            for a in heads:
                dq_ref[:, a * DH:(a + 1) * DH] = (dq_s[a, :, :DH] * scale).astype(BF16)
                dc_ref[a] = dc_ref[a] + jnp.broadcast_to(dq_s[a, :, DH:DH + 1], (T, LANES))

    qside = pl.BlockSpec((t, hp * AUG), lambda h, j, ki_ref, qi_ref: (qi_ref[j], h))
    qxside = pl.BlockSpec((t, hp * DH), lambda h, j, ki_ref, qi_ref: (qi_ref[j], h))
    kside = pl.BlockSpec((t, hp * AUG), lambda h, j, ki_ref, qi_ref: (ki_ref[j], h))
    kout = pl.BlockSpec((t, hp * DH), lambda h, j, ki_ref, qi_ref: (ki_ref[j], h))
    bf = jax.ShapeDtypeStruct((T, D), BF16)
    sums = jax.ShapeDtypeStruct((H, T, LANES), F32)
    grid_spec = pltpu.PrefetchScalarGridSpec(
        num_scalar_prefetch=2, grid=(n_h, ki_tab.shape[0]),
        in_specs=[qside, qxside, kside, kside, qside] + [HBM_SPEC] * ne,
        out_specs=[pl.BlockSpec((T, hp * DH), lambda h, j, ki_ref, qi_ref: (0, h)), kout, kout,
                   pl.BlockSpec((hp, T, LANES), lambda h, j, ki_ref, qi_ref: (h, 0, 0))] + [HBM_SPEC] * ne,
        scratch_shapes=[pltpu.VMEM((hp, T, AUG), F32), pltpu.VMEM((hp, t, AUG), F32), pltpu.VMEM((hp, t, DH), F32)]
        + ([pltpu.SemaphoreType.DMA((3 * ne,)), pltpu.SemaphoreType.DMA((3 * ne,))] if ne else []))
    outs = _call(
        body, name="attn_bwd", grid_spec=grid_spec,
        out_shape=[bf, bf, bf, sums] + [jax.ShapeDtypeStruct(s.shape, s.dtype) for s in exchange],
        compiler_params=_cparams(("arbitrary", "arbitrary"), VMEM_BIG),
    )(ki_tab, qi_tab, q_aug, qx, k_aug, v_aug, do_aug, *exchange)
    return (*outs[:4], list(outs[4:]))


def _fgate_bwd(dc_heads, flb):
    T = flb.shape[0]
    tm = TM
    nt = T // tm

    def body(dch_ref, flb_ref, dfl_ref, acc_ref, carry, top_s):
        @pl.when(pl.program_id(0) == 0)
        def _():
            carry[...] = jnp.zeros_like(carry)
            acc_ref[...] = jnp.zeros_like(acc_ref)

        flb = flb_ref[...]
        lane = lax.broadcasted_iota(jnp.int32, flb.shape, 1)
        dc = jnp.zeros(flb.shape, F32)
        for hd in range(H):
            dc = dc + jnp.where(lane == hd, dch_ref[hd], 0.0)
        r = lax.broadcasted_iota(jnp.int32, (tm, tm), 0)
        c = lax.broadcasted_iota(jnp.int32, (tm, tm), 1)
        dls = _dot_exact((c >= r).astype(F32), dc) + carry[...]
        top_s[...] = dls[:SUBLANES, :]
        carry[...] = top_s[0:1, :]
        dfl = jnp.where(lane < H, dls * _sigmoid(-flb), 0.0)
        dfl_ref[...] = dfl.astype(BF16)
        acc_ref[0:1, :] += jnp.sum(dfl, axis=0, keepdims=True)

    rev = pl.BlockSpec((tm, LANES), lambda i: (nt - 1 - i, 0))
    return _call(
        body, name="fgate_bwd", grid=(nt,),
        in_specs=[pl.BlockSpec((H, tm, LANES), lambda i: (0, nt - 1 - i, 0)), rev],
        out_specs=[rev, _const_spec((SUBLANES, LANES))],
        out_shape=[jax.ShapeDtypeStruct((T, LANES), BF16), jax.ShapeDtypeStruct((SUBLANES, LANES), F32)],
        scratch_shapes=[pltpu.VMEM((1, LANES), F32), pltpu.VMEM((SUBLANES, LANES), F32)],
        compiler_params=_cparams(("arbitrary",)),
    )(dc_heads, flb)


def _dx(dz, dfl, w_a, w_f, w_b, x, pre_gain, dh1, exchange=()):
    T = x.shape[0]
    tm = TM
    nt = T // tm
    ne = len(exchange)

    def body(*refs):
        dz_refs = refs[:6]
        dfl_ref, wa_ref, wf_ref, wb_ref, x_ref, g_ref, dh1_ref = refs[6:13]
        sent = refs[13:13 + ne]
        gx_ref, acc_ref = refs[13 + ne:15 + ne]
        received, sems = refs[15 + ne:15 + 2 * ne], refs[15 + 2 * ne:]

        @pl.when(pl.program_id(0) == 0)
        def _():
            acc_ref[...] = jnp.zeros_like(acc_ref)
            for cp in _chip_copies(sent, received, *sems) if ne else ():
                cp.start()

        if ne:
            @pl.when(pl.program_id(0) == nt - 1)
            def _():
                for cp in _chip_copies(sent, received, *sems):
                    cp.wait()

        dxn = _dot(dfl_ref[...], wf_ref[...])
        for s in range(3):
            dxn = dxn + _dot(dz_refs[s][...], wa_ref[s * D:(s + 1) * D, :])
            dxn = dxn + _dot(dz_refs[3 + s][...], wb_ref[s * D:(s + 1) * D, :])
        xv = x_ref[...]
        rstd = _rstd(xv)
        xhat = xv * rstd
        gx_ref[...] = dh1_ref[...] + _rms_bwd(dxn * g_ref[...], xhat, rstd)
        acc_ref[0:1, :] += jnp.sum(dxn * xhat, axis=0, keepdims=True)

    outs = _call(
        body, name="dx", grid=(nt,),
        in_specs=[_row_spec(tm, D)] * 6 + [_row_spec(tm, LANES), _weight_spec((3 * D, D)), _weight_spec((LANES, D)),
                                           _weight_spec((3 * D, D)), _row_spec(tm, D), _const_spec((1, D)),
                                           _row_spec(tm, D)] + [HBM_SPEC] * ne,
        out_specs=[_row_spec(tm, D), _const_spec((SUBLANES, D))] + [HBM_SPEC] * ne,
        out_shape=[jax.ShapeDtypeStruct((T, D), F32), jax.ShapeDtypeStruct((SUBLANES, D), F32)]
        + [jax.ShapeDtypeStruct(s.shape, s.dtype) for s in exchange],
        scratch_shapes=[pltpu.SemaphoreType.DMA((3 * ne,)), pltpu.SemaphoreType.DMA((3 * ne,))] if ne else [],
        compiler_params=_cparams(("arbitrary",), VMEM_BIG),
    )(*dz, dfl, w_a, w_f, w_b, x, pre_gain, dh1, *exchange)
    return outs[0], outs[1], list(outs[2:])


GRAD_ROWS = D_IN + SUBLANES


def _dw_in_segments(dz_a, dz_b, xn, buf, pair, bt):
    T = xn.shape[0]
    nt = T // bt
    first, second = [(2 * pair + k) * D + (H if 2 * pair + k >= 3 else 0) for k in (0, 1)]
    step8 = (second - first) // SUBLANES

    def body(*refs):
        dza_ref, dzb_ref, xn_ref, o_ref = refs[0], refs[1], refs[2], refs[-1]

        @pl.when(pl.program_id(1) == 0)
        def _():
            o_ref[...] = jnp.zeros_like(o_ref)

        @pl.when(pl.program_id(0) == 0)
        def _():
            o_ref[...] += _dot_tn(dza_ref[...], xn_ref[...])

        @pl.when(pl.program_id(0) == 1)
        def _():
            o_ref[...] += _dot_tn(dzb_ref[...], xn_ref[...])

    spec_a = pl.BlockSpec((bt, D), lambda s, t: (jnp.where(s == 0, t, nt - 1), 0))
    spec_b = pl.BlockSpec((bt, D), lambda s, t: (jnp.where(s == 1, t, 0), 0))
    return _call(
        body, name="dw_in_%d" % pair, grid=(2, nt),
        in_specs=[spec_a, spec_b, pl.BlockSpec((bt, D), lambda s, t: (t, 0))]
        + ([] if buf is None else [pl.BlockSpec(memory_space=pl.ANY)]),
        out_specs=pl.BlockSpec((pl.Element(D), pl.Element(D)),
                               lambda s, t: ((first // SUBLANES + s * step8) * SUBLANES, 0)),
        out_shape=jax.ShapeDtypeStruct((GRAD_ROWS, D), F32),
        input_output_aliases={} if buf is None else {3: 0},
        compiler_params=_cparams(("arbitrary", "arbitrary"), VMEM_BIG),
    )(*((dz_a, dz_b, xn) if buf is None else (dz_a, dz_b, xn, buf)))


def _dw_in_t(dz, dfl, xn, bt=DW_TOKENS):
    T = xn.shape[0]
    bt = min(bt, T)
    nt = T // bt
    main = None
    for pair in range(3):
        main = _dw_in_segments(dz[2 * pair], dz[2 * pair + 1], xn, main, pair, bt)

    def f_body(dfl_ref, xn_ref, main_ref, o_ref, acc_s):
        p = pl.program_id(0)
        t = pl.program_id(1)

        @pl.when(t == 0)
        def _():
            acc_s[...] = jnp.zeros_like(acc_s)

        @pl.when(p == 0)
        def _():
            acc_s[...] += _dot_tn(dfl_ref[...], xn_ref[...])

        @pl.when(t == nt - 1)
        def _():
            o_ref[...] = acc_s[:SUBLANES, :]

    fl_block = FL0 // SUBLANES
    end_block = D_IN // SUBLANES
    return _call(
        f_body, name="dw_in_f", grid=(2, nt),
        in_specs=[pl.BlockSpec((bt, LANES), lambda p, t: (t, 0)), pl.BlockSpec((bt, D), lambda p, t: (t, 0)),
                  pl.BlockSpec(memory_space=pl.ANY)],
        out_specs=pl.BlockSpec((SUBLANES, D), lambda p, t: (fl_block + p * (end_block - fl_block), 0)),
        out_shape=jax.ShapeDtypeStruct((GRAD_ROWS, D), F32),
        scratch_shapes=[pltpu.VMEM((LANES, D), F32)],
        input_output_aliases={2: 0},
        compiler_params=_cparams(("arbitrary", "arbitrary")),
    )(dfl, xn, main)


HBM_SPEC = pl.BlockSpec(memory_space=pltpu.HBM)
VMEM_SPEC = pl.BlockSpec(memory_space=pltpu.VMEM)


def _position():
    return lax.axis_index("x"), lax.axis_index("y"), lax.axis_index("c")


def _other_chips(x, y):
    return [(1 - x, y), (x, 1 - y), (1 - x, 1 - y)]


def _gather_shards(shards, whole):
    na, nw = len(shards), len(whole)
    nall = na + nw

    def body(*refs):
        gather = _GatherPlan(refs[:nall], refs[nall:2 * nall], refs[2 * nall:], na)
        gather.send()
        gather.forward()
        gather.finish()

    arrs = list(shards) + list(whole)
    outs = _call(
        body, name="gather_shards",
        in_specs=[HBM_SPEC] * nall, out_specs=[HBM_SPEC] * nall,
        out_shape=_gather_out_shapes(arrs), scratch_shapes=_gather_semaphores(na, nall),
    )(*arrs)
    return _place_own(outs, arrs)


def _gather_out_shapes(arrs):
    return [jax.ShapeDtypeStruct((N_CHIPS,) + s.shape, s.dtype) for s in arrs]


def _gather_semaphores(na, nall):
    return [pltpu.SemaphoreType.DMA((3 * nall,)), pltpu.SemaphoreType.DMA((3 * nall,)),
            pltpu.SemaphoreType.DMA((3 * na,)), pltpu.SemaphoreType.DMA((3 * na,))]


def _place_own(outs, arrs):
    if not arrs:
        return []
    chip = 2 * lax.axis_index("x") + lax.axis_index("y")
    return [lax.dynamic_update_slice(o, a[None], (chip,) + (0,) * a.ndim) for o, a in zip(outs, arrs)]


class _GatherPlan:
    def __init__(self, srcs, dsts, sems, na):
        ici_send, ici_recv, d2d_send, d2d_recv = sems
        x, y, c = _position()
        chip = 2 * x + y
        nall = len(srcs)

        def half(a, which):
            rows = srcs[a].shape[0] // 2
            return pl.ds(pl.multiple_of(which * rows, BF16_ROWS), rows)

        def copy(src, dst, send, recv, k, to):
            return pltpu.make_async_remote_copy(src_ref=src, dst_ref=dst, send_sem=send.at[k], recv_sem=recv.at[k],
                                                device_id=to, device_id_type=MESH)

        self.first, self.landed, self.passed, self.returned = [], [], [], []
        for j, (px, py) in enumerate(_other_chips(x, y)):
            theirs = 2 * px + py
            for a in range(nall):
                k = j * nall + a
                if a < na:
                    self.first.append(copy(srcs[a].at[half(a, c), :], dsts[a].at[chip, half(a, c), :],
                                           ici_send, ici_recv, k, (px, py, c)))
                    mine = dsts[a].at[theirs, half(a, c), :]
                    other = dsts[a].at[theirs, half(a, 1 - c), :]
                    self.landed.append(copy(mine, mine, ici_send, ici_recv, k, (px, py, c)))
                    self.passed.append(copy(mine, mine, d2d_send, d2d_recv, j * na + a, (x, y, 1 - c)))
                    self.returned.append(copy(other, other, d2d_send, d2d_recv, j * na + a, (x, y, 1 - c)))
                else:
                    self.first.append(copy(srcs[a], dsts[a].at[chip], ici_send, ici_recv, k, (px, py, c)))
                    got = dsts[a].at[theirs]
                    self.landed.append(copy(got, got, ici_send, ici_recv, k, (px, py, c)))
                    self.passed.append(None)

    def send(self):
        for cp in self.first:
            cp.start()

    def forward(self):
        for arrival, fwd in zip(self.landed, self.passed):
            arrival.wait_recv()
            if fwd is not None:
                fwd.start()

    def finish(self):
        for cp in self.returned:
            cp.wait_recv()
        for cp in self.first + [f for f in self.passed if f is not None]:
            cp.wait_send()


W_ROWS = 1568
G_ROWS = 1552
SHARD_ROWS = D_IN // N_CHIPS
WINDOW_STEP = 1536


def _assemble_w_in(cont):
    cb = COL_BLOCK
    half = WINDOW_STEP
    seam = BF16_ROWS

    def body(c_ref, wa_ref, wf_ref, wb_ref):
        x0 = c_ref[0].astype(F32)
        x1, x2, x3 = (pltpu.roll(c_ref[j].astype(F32), 2 * j, 0) for j in (1, 2, 3))
        wa = jnp.concatenate([x0[:half], x0[half:half + seam] + x1[:seam], x1[seam:half]], axis=0)
        wa_ref[...] = wa.astype(BF16)

        fl = x1[half:half + seam] + x2[:seam]
        row = lax.broadcasted_iota(jnp.int32, fl.shape, 0)
        wf_ref[:seam, :] = jnp.where(row < H, fl, 0.0).astype(BF16)
        wf_ref[seam:, :] = jnp.zeros((LANES - seam, cb), BF16)

        mid = x2[half:half + SUBLANES] + x3[:SUBLANES]
        wb = jnp.concatenate([x2[SUBLANES:half], mid, x3[SUBLANES:half + SUBLANES]], axis=0)
        wb_ref[...] = wb.astype(BF16)

    return _call(
        body, name="assemble_w_in", grid=(D // cb,),
        in_specs=[pl.BlockSpec((N_CHIPS, W_ROWS, cb), lambda i: (0, 0, i))],
        out_specs=[pl.BlockSpec((3 * D, cb), lambda i: (0, i)), pl.BlockSpec((LANES, cb), lambda i: (0, i)),
                   pl.BlockSpec((3 * D, cb), lambda i: (0, i))],
        out_shape=[jax.ShapeDtypeStruct((3 * D, D), BF16), jax.ShapeDtypeStruct((LANES, D), BF16),
                   jax.ShapeDtypeStruct((3 * D, D), BF16)],
        compiler_params=_cparams(("parallel",)),
    )(cont)


def _pair_exchange_windows(grad_t):
    half_g = G_ROWS // 2

    def body(g_ref, got, send_sems, recv_sems):
        x, y, c = _position()
        copies = []
        for j in range(N_CHIPS):
            rows = pl.ds(pl.multiple_of(j * WINDOW_STEP + (1 - c) * half_g, SUBLANES), half_g)
            copies.append(pltpu.make_async_remote_copy(
                src_ref=g_ref.at[rows, :], dst_ref=got.at[j], send_sem=send_sems.at[j], recv_sem=recv_sems.at[j],
                device_id=(x, y, 1 - c), device_id_type=MESH))
        for cp in copies:
            cp.start()
        for cp in copies:
            cp.wait()

    return _call(
        body, name="pair_exchange_w_in",
        in_specs=[HBM_SPEC], out_specs=HBM_SPEC,
        out_shape=jax.ShapeDtypeStruct((N_CHIPS, half_g, D), F32),
        scratch_shapes=[pltpu.SemaphoreType.DMA((N_CHIPS,)), pltpu.SemaphoreType.DMA((N_CHIPS,))],
    )(grad_t)


def _pair_sum(parts, gots, c):
    na = len(parts)

    def body(c_ref, *refs):
        for a in range(na):
            refs[2 * na + a][...] = (refs[a][...] + refs[na + a][...]).astype(BF16)

    mine = [pl.BlockSpec(g.shape, lambda i, c_ref: (0, c_ref[0], 0)) for g in gots]
    whole = [pl.BlockSpec(g.shape, lambda i, c_ref: (0, 0, 0)) for g in gots]
    grid_spec = pltpu.PrefetchScalarGridSpec(
        num_scalar_prefetch=1, grid=(1,), in_specs=mine + whole, out_specs=whole)
    return _call(
        body, name="pair_sum", grid_spec=grid_spec,
        out_shape=[jax.ShapeDtypeStruct(g.shape, BF16) for g in gots],
        compiler_params=_cparams(("arbitrary",), VMEM_BIG),
    )(c.reshape(1), *parts, *gots)


def _pair_sum_windows(grad_t, got, c):
    _, half, C = got.shape
    cb = COL_BLOCK

    def body(c_ref, a_ref, b_ref, o_ref):
        o_ref[0] = (a_ref[...] + b_ref[0]).astype(BF16)

    def mine(j, i, c_ref):
        return ((j * (WINDOW_STEP // SUBLANES) + c_ref[0] * (half // SUBLANES)) * SUBLANES, i * cb)

    spec = pl.BlockSpec((1, half, cb), lambda j, i, c_ref: (j, 0, i))
    grid_spec = pltpu.PrefetchScalarGridSpec(
        num_scalar_prefetch=1, grid=(N_CHIPS, C // cb),
        in_specs=[pl.BlockSpec((pl.Element(half), pl.Element(cb)), mine), spec], out_specs=spec)
    return _call(
        body, name="pair_sum_w_in", grid_spec=grid_spec,
        out_shape=jax.ShapeDtypeStruct((N_CHIPS, half, C), BF16),
        compiler_params=_cparams(("parallel", "parallel")),
    )(c.reshape(1), grad_t, got)


def _chip_sum(own, got, chip, name):
    _, half, C = got.shape
    cb = min(C, COL_BLOCK)

    def body(chip_ref, own_ref, g_ref, o_ref):
        for me in range(N_CHIPS):
            @pl.when(chip_ref[0] == me)
            def _(me=me):
                terms = [own_ref[0] if k == me else g_ref[k] for k in range(N_CHIPS)]
                acc = terms[0].astype(F32) + terms[1].astype(F32)
                acc = acc + terms[2].astype(F32)
                o_ref[...] = acc + terms[3].astype(F32)

    grid_spec = pltpu.PrefetchScalarGridSpec(
        num_scalar_prefetch=1, grid=(C // cb,),
        in_specs=[pl.BlockSpec((1, half, cb), lambda i, chip_ref: (chip_ref[0], 0, i)),
                  pl.BlockSpec((N_CHIPS, half, cb), lambda i, chip_ref: (0, 0, i))],
        out_specs=pl.BlockSpec((half, cb), lambda i, chip_ref: (0, i)))
    return _call(
        body, name=name, grid_spec=grid_spec,
        out_shape=jax.ShapeDtypeStruct((half, C), F32),
        compiler_params=_cparams(("parallel",)),
    )(chip.reshape(1), own, got)


def _final_exchange(halves, g):
    na = len(halves)
    rows = g.shape[0]
    per = rows // N_DEV

    def body(*refs):
        srcs, g_ref = refs[:na], refs[na]
        dsts, out_ref = refs[na + 1:2 * na + 1], refs[2 * na + 1]
        got_ref, s1, r1, s2, r2, swap_send, swap_recv = refs[2 * na + 2:]
        x, y, c = _position()
        swaps = [pltpu.make_async_remote_copy(
            src_ref=srcs[a], dst_ref=dsts[a], send_sem=swap_send.at[a], recv_sem=swap_recv.at[a],
            device_id=(x, y, 1 - c), device_id_type=MESH) for a in range(na)]
        for cp in swaps:
            cp.start()
        me = 4 * x + 2 * y + c
        mine = pl.ds(pl.multiple_of(me * per, SUBLANES), per)
        peers = []
        for j in range(1, N_DEV):
            px = 1 - x if j & 4 else x
            py = 1 - y if j & 2 else y
            pc = 1 - c if j & 1 else c
            peers.append((px, py, pc))

        first = []
        for j, (px, py, pc) in enumerate(peers):
            theirs = pl.ds(pl.multiple_of((4 * px + 2 * py + pc) * per, SUBLANES), per)
            first.append(pltpu.make_async_remote_copy(
                src_ref=g_ref.at[theirs, :], dst_ref=got_ref.at[me], send_sem=s1.at[j], recv_sem=r1.at[j],
                device_id=(px, py, pc), device_id_type=MESH))
        for cp in first:
            cp.start()
        got_ref[me] = g_ref[mine, :]
        for cp in first:
            cp.wait()
        total = got_ref[0]
        for d in range(1, N_DEV):
            total = total + got_ref[d]
        out_ref[mine, :] = total

        second = []
        for j, peer in enumerate(peers):
            second.append(pltpu.make_async_remote_copy(
                src_ref=out_ref.at[mine, :], dst_ref=out_ref.at[mine, :], send_sem=s2.at[j], recv_sem=r2.at[j],
                device_id=peer, device_id_type=MESH))
        for cp in second:
            cp.start()
        for cp in second + swaps:
            cp.wait()

    sems = pltpu.SemaphoreType.DMA((N_DEV - 1,))
    swap_sems = pltpu.SemaphoreType.DMA((na,))
    outs = _call(
        body, name="final_exchange", in_hbm=False,
        in_specs=[HBM_SPEC] * na + [VMEM_SPEC], out_specs=[HBM_SPEC] * na + [VMEM_SPEC],
        out_shape=[jax.ShapeDtypeStruct(s.shape, s.dtype) for s in halves] + [jax.ShapeDtypeStruct(g.shape, F32)],
        scratch_shapes=[pltpu.VMEM((N_DEV, per, LANES), F32), sems, sems, sems, sems, swap_sems, swap_sems],
    )(*halves, g)
    return outs[:na], outs[na]


def _adamw_math(g, w, m, v):
    m2 = ADAM_B1 * m + (1.0 - ADAM_B1) * g
    v2 = ADAM_B2 * v + (1.0 - ADAM_B2) * (g * g)
    m_hat = m2 / (1.0 - ADAM_B1 ** ADAM_STEP)
    v_hat = v2 / (1.0 - ADAM_B2 ** ADAM_STEP)
    delta = (-ADAM_LR) * (m_hat / (jnp.sqrt(v_hat) + ADAM_EPS) + ADAM_WD * w)
    return delta, m2, v2


ADAMW_BLOCK_BYTES = 1 << 20


def _adamw_big(g, w, m, v, name):
    R, C = g.shape
    bc = min(C, max(LANES, ADAMW_BLOCK_BYTES // (4 * R) // LANES * LANES))

    def body(g_ref, w_ref, m_ref, v_ref, d_ref, m2_ref, v2_ref):
        d_ref[...], m2_ref[...], v2_ref[...] = _adamw_math(g_ref[...], w_ref[...], m_ref[...], v_ref[...])

    spec = pl.BlockSpec((R, bc), lambda j: (0, j))
    out = jax.ShapeDtypeStruct((R, C), F32)
    return _call(
        body, name=name, grid=(C // bc,),
        in_specs=[spec] * 4, out_specs=[spec] * 3, out_shape=[out] * 3,
        compiler_params=_cparams(("parallel",)),
    )(g, w, m, v)


def _adamw_small(gs, ws, ms, vs):
    n = len(gs)

    def body(*refs):
        for a in range(n):
            g_ref, w_ref, m_ref, v_ref = (refs[k * n + a] for k in range(4))
            d_ref, m2_ref, v2_ref = (refs[(4 + k) * n + a] for k in range(3))
            d_ref[...], m2_ref[...], v2_ref[...] = _adamw_math(g_ref[...], w_ref[...], m_ref[...], v_ref[...])

    outs = [jax.ShapeDtypeStruct(w.shape, F32) for w in ws]
    specs = [_const_spec(w.shape) for w in ws]
    return _call(
        body, name="adamw_small", grid=(1,),
        in_specs=specs * 4, out_specs=specs * 3, out_shape=outs * 3,
    )(*gs, *ws, *ms, *vs)


def _late_weights(st_out, st_ple, st_gate, st_conv):
    return st_out.reshape(DMIX, D), _from_chip_cols(st_ple), st_gate.reshape(D, D), _from_chip_cols(st_conv)


def _local_step(x, p, tgt, w_a, w_f, w_b, late, b_f, pre_gain, post_gain, conv_b,
                w_rgate, b_rgate, w_igate, b_igate, lam, gain_a, gain_l, ple_gain, b_gate,
                gather_late=False, early_reduce=None, w_in_reduce=None):
    b_f_pad = jnp.pad(b_f, ((0, 0), (0, LANES - H)))
    w_r = w_rgate.astype(BF16)
    w_i = w_igate.astype(BF16)

    xn, q_aug, k_aug, v_aug, g_attn, x_lru, g_lru, flb, vt_aug = _in_proj(x, pre_gain, w_a, w_f, w_b, b_f_pad)
    if gather_late:
        o, qx, stacks = _attn_fwd(q_aug, k_aug, vt_aug, late[:3], late[3:])
        late = _late_weights(*stacks)
    else:
        o, qx, _ = _attn_fwd(q_aug, k_aug, vt_aug)
    w_out_b, w_ple_b, w_gate_b, conv_w = late
    ycat, xc, h = _branches_fwd(o, g_attn, x_lru, g_lru, gain_a, gain_l, conv_w, conv_b, w_r, b_rgate, w_i, b_igate,
                                lam)
    dh1, dycat, dmix, h1b, dgp, pb, dpe, acc_t = _tail(ycat, x, p, tgt, w_out_b, post_gain, w_ple_b, ple_gain,
                                                       w_gate_b, b_gate)
    do_aug, dg_attn, dg_lru, dh, acc_b, gw_out, gw_gate, gw_ple = _branches_bwd(
        dycat, o, g_attn, h, g_lru, gain_a, gain_l, ycat, dmix, h1b, dgp, pb, dpe)
    late_grads = [gw_out, gw_ple, gw_gate]
    if early_reduce is None:
        dx_lru, gw_r, gw_i, acc_l, _ = _lru_bwd(dh, h, xc, x_lru, conv_w, w_r, b_rgate, w_i, b_igate, lam)
    else:
        parts = [gw_out.reshape(N_CHIPS, DMIX // N_CHIPS, D), _by_chip_cols(gw_ple),
                 gw_gate.reshape(N_CHIPS, D // N_CHIPS, D)]
        dx_lru, gw_r, gw_i, acc_l, got = _lru_bwd(dh, h, xc, x_lru, conv_w, w_r, b_rgate, w_i, b_igate, lam, parts)
        sent = _pair_sum(parts, got, early_reduce)
    if early_reduce is None:
        dq, dk, dv, dc_heads, _ = _attn_bwd(q_aug, qx, k_aug, v_aug, do_aug)
    else:
        dq, dk, dv, dc_heads, received = _attn_bwd(q_aug, qx, k_aug, v_aug, do_aug, sent)
        late_grads = list(zip(sent, received))
    dfl, acc_f = _fgate_bwd(dc_heads, flb)
    dz = (dq, dk, dv, dg_attn, dx_lru, dg_lru)
    grad_t = _dw_in_t(dz, dfl, xn)
    if w_in_reduce is None:
        grad_x, acc_x, _ = _dx(dz, dfl, w_a, w_f, w_b, x, pre_gain, dh1)
    else:
        sent = w_in_reduce(grad_t)
        grad_x, acc_x, (received,) = _dx(dz, dfl, w_a, w_f, w_b, x, pre_gain, dh1, [sent])
        grad_t = (sent, received)

    grads = dict(
        w_in_t=grad_t,
        w_out=late_grads[0],
        w_ple=late_grads[1],
        w_ple_gate=late_grads[2],
        w_rgate=gw_r,
        w_igate=gw_i,
        b_f=acc_f[0:1, :H],
        pre_gain=acc_x[0:1],
        post_gain=acc_t[0:1],
        conv_w=acc_l[0:4],
        conv_b=acc_l[4:5],
        b_rgate=acc_l[5:6],
        b_igate=acc_l[6:7],
        lru_lambda=acc_l[7:8],
        attn_out_gain=acc_b[0:1],
        lru_out_gain=acc_b[1:2],
        ple_gain=acc_t[1:2],
        b_ple_gate=acc_t[2:3],
    )
    loss = jnp.sum(acc_t[3])
    return loss, grad_x, grads


SMALL_ROWS = ["b_f", "pre_gain", "post_gain", "conv_w", "conv_b", "b_rgate", "b_igate", "lru_lambda",
              "attn_out_gain", "lru_out_gain", "ple_gain", "b_ple_gate"]
WEIGHTS = ["w_in", "b_f", "pre_gain", "post_gain", "conv_w", "conv_b", "w_rgate", "b_rgate", "w_igate", "b_igate",
           "lru_lambda", "attn_out_gain", "lru_out_gain", "w_out", "w_ple", "ple_gain", "w_ple_gate", "b_ple_gate"]
SHARDED = ["w_in", "w_out", "w_ple", "w_ple_gate"]


def _by_chip_cols(g):
    r, cols = g.shape
    return g.reshape(r, N_CHIPS, cols // N_CHIPS).transpose(1, 0, 2)


def _from_chip_cols(s):
    n, r, cols = s.shape
    return s.transpose(1, 0, 2).reshape(r, n * cols)


def kernel(x, p, w_in, b_f, pre_gain, post_gain, conv_w, conv_b, w_rgate, b_rgate, w_igate, b_igate, lru_lambda, attn_out_gain, lru_out_gain, w_out, w_ple, ple_gain, w_ple_gate, b_ple_gate, loss_target, m_w_in, m_b_f, m_pre_gain, m_post_gain, m_conv_w, m_conv_b, m_w_rgate, m_b_rgate, m_w_igate, m_b_igate, m_lru_lambda, m_attn_out_gain, m_lru_out_gain, m_w_out, m_w_ple, m_ple_gain, m_w_ple_gate, m_b_ple_gate, v_w_in, v_b_f, v_pre_gain, v_post_gain, v_conv_w, v_conv_b, v_w_rgate, v_b_rgate, v_w_igate, v_b_igate, v_lru_lambda, v_attn_out_gain, v_lru_out_gain, v_w_out, v_w_ple, v_ple_gain, v_w_ple_gate, v_b_ple_gate):
    w = dict(w_in=w_in, b_f=b_f, pre_gain=pre_gain, post_gain=post_gain, conv_w=conv_w, conv_b=conv_b,
             w_rgate=w_rgate, b_rgate=b_rgate, w_igate=w_igate, b_igate=b_igate, lru_lambda=lru_lambda,
             attn_out_gain=attn_out_gain, lru_out_gain=lru_out_gain, w_out=w_out, w_ple=w_ple, ple_gain=ple_gain,
             w_ple_gate=w_ple_gate, b_ple_gate=b_ple_gate)
    m = dict(w_in=m_w_in, b_f=m_b_f, pre_gain=m_pre_gain, post_gain=m_post_gain, conv_w=m_conv_w, conv_b=m_conv_b,
             w_rgate=m_w_rgate, b_rgate=m_b_rgate, w_igate=m_w_igate, b_igate=m_b_igate, lru_lambda=m_lru_lambda,
             attn_out_gain=m_attn_out_gain, lru_out_gain=m_lru_out_gain, w_out=m_w_out, w_ple=m_w_ple,
             ple_gain=m_ple_gain, w_ple_gate=m_w_ple_gate, b_ple_gate=m_b_ple_gate)
    v = dict(w_in=v_w_in, b_f=v_b_f, pre_gain=v_pre_gain, post_gain=v_post_gain, conv_w=v_conv_w, conv_b=v_conv_b,
             w_rgate=v_w_rgate, b_rgate=v_b_rgate, w_igate=v_w_igate, b_igate=v_b_igate, lru_lambda=v_lru_lambda,
             attn_out_gain=v_attn_out_gain, lru_out_gain=v_lru_out_gain, w_out=v_w_out, w_ple=v_w_ple,
             ple_gain=v_ple_gain, w_ple_gate=v_w_ple_gate, b_ple_gate=v_b_ple_gate)
    xi, yi, ci = _position()
    chip = 2 * xi + yi

    w_in_t, m_in_t, v_in_t = (jnp.swapaxes(t[0], 0, 1) for t in (w_in, m_w_in, v_w_in))
    window = jnp.pad(w_in_t.astype(BF16), ((0, W_ROWS - SHARD_ROWS), (0, 0)))

    (st_in,) = _gather_shards([window], [])
    w_a, w_f, w_b = _assemble_w_in(st_in)
    late_shards = (w_out[0].astype(BF16), w_ple[0].astype(BF16), w_ple_gate[0].astype(BF16), conv_w[0])

    loss, grad_x, g = _local_step(
        x[0], p[0, 0], loss_target[0], w_a, w_f, w_b, late_shards, b_f, pre_gain, post_gain,
        conv_b, w_rgate[0], b_rgate, w_igate[0], b_igate, lru_lambda, attn_out_gain, lru_out_gain, ple_gain,
        b_ple_gate, gather_late=True, early_reduce=ci,
        w_in_reduce=lambda grad_t: _pair_sum_windows(grad_t, _pair_exchange_windows(grad_t), ci))

    sums = [g["w_in_t"][0]] + [g[n][0] for n in SHARDED[1:]]
    recv = [g["w_in_t"][1]] + [g[n][1] for n in SHARDED[1:]]
    halves = [_chip_sum(sums[a], recv[a], chip, "chip_sum_%d" % a) for a in range(4)]

    rows = [jnp.pad(g["b_f"], ((0, 0), (0, D - H)))] + [g[n] for n in SMALL_ROWS[1:]]
    rows.append(jnp.pad(loss.reshape(1, 1), ((0, 0), (0, D - 1))))
    packed = jnp.concatenate([g["w_rgate"].reshape(NB * LANES, LANES), g["w_igate"].reshape(NB * LANES, LANES),
                              jnp.concatenate(rows, axis=0).reshape(LANES, LANES)], axis=0)
    theirs, summed = _final_exchange(halves, packed)
    full = [jnp.concatenate([jnp.where(ci == 0, a, b), jnp.where(ci == 0, b, a)], axis=0)
            for a, b in zip(halves, theirs)]
    red = dict(zip(SHARDED, full))
    red["w_in"] = lax.dynamic_slice_in_dim(red["w_in"], 2 * chip, SHARD_ROWS, axis=0)
    red["w_rgate"] = summed[:D].reshape(1, NB, LANES, LANES)
    red["w_igate"] = summed[D:2 * D].reshape(1, NB, LANES, LANES)
    vec = summed[2 * D:].reshape(16, D)
    loss = vec[15, 0]
    r0 = 0
    for n in SMALL_ROWS:
        nr = 4 if n == "conv_w" else 1
        red[n] = vec[r0:r0 + nr]
        r0 += nr
    red["b_f"] = red["b_f"][:, :H]
    red["conv_w"] = lax.dynamic_slice_in_dim(red["conv_w"], chip * (D // N_CHIPS), D // N_CHIPS, axis=1)[None]

    delta, new_m, new_v = {}, {}, {}
    outs_in = _adamw_big(red["w_in"], w_in_t, m_in_t, v_in_t, "adamw_w_in")
    delta["w_in"], new_m["w_in"], new_v["w_in"] = (jnp.swapaxes(t, 0, 1)[None] for t in outs_in)
    red["w_in"] = jnp.swapaxes(red["w_in"], 0, 1)[None]
    for n in SHARDED[1:]:
        delta[n], new_m[n], new_v[n] = (t[None] for t in _adamw_big(red[n], w[n][0], m[n][0], v[n][0], "adamw_" + n))
        red[n] = red[n][None]
    small = [n for n in WEIGHTS if n not in SHARDED]
    outs = _adamw_small([red[n] for n in small], [w[n] for n in small], [m[n] for n in small],
                        [v[n] for n in small])
    ns = len(small)
    for a, n in enumerate(small):
        delta[n], new_m[n], new_v[n] = outs[a], outs[ns + a], outs[2 * ns + a]

    return (loss, grad_x[None], *[red[n] for n in WEIGHTS], *[delta[n] for n in WEIGHTS],
            *[new_m[n] for n in WEIGHTS], *[new_v[n] for n in WEIGHTS])
```

```python
import jax
import jax.numpy as jnp
import numpy as np
from jax import lax
from jax.experimental import pallas as pl
from jax.experimental.pallas import tpu as pltpu

F32 = jnp.float32
BF16 = jnp.bfloat16

D = 1024
H = 8
DH = 128
NB = 8
DPLE = 256
DMIX = 2 * D
D_IN = 4 * D + H + 2 * D
FL0 = 3 * D
RMS_EPS = 1e-6
LRU_C = 8.0
NEG = -1e30
LANES = 128
SUBLANES = 8
BF16_ROWS = 16
COL_BLOCK = 256
SUM_BLOCK = 512
DW_TOKENS = 2048

ADAM_LR = 0.001
ADAM_B1 = 0.9
ADAM_B2 = 0.999
ADAM_EPS = 1e-08
ADAM_WD = 0.01
ADAM_STEP = 10

TM = 256
TA = 512
TA_FWD = 1024
FWD_HEADS = 4
BWD_HEADS = 2
VMEM_BIG = 56 * 1024 * 1024
VMEM_MID = 40 * 1024 * 1024

MESH = pl.DeviceIdType.MESH
N_CHIPS = 4
N_DEV = 8


def _call(body, *, out_shape, in_hbm=True, **kwargs):
    if not in_hbm:
        return pl.pallas_call(body, out_shape=out_shape, **kwargs)

    def pin(shape):
        return pltpu.HBM(shape.shape, shape.dtype) if isinstance(shape, jax.ShapeDtypeStruct) else shape

    fn = pl.pallas_call(body, out_shape=jax.tree.map(pin, out_shape), **kwargs)

    def run(*args):
        return fn(*[a if a.dtype == jnp.int32 else pltpu.with_memory_space_constraint(a, pltpu.HBM) for a in args])

    return run


def _cparams(sem, vmem=VMEM_MID):
    return pltpu.CompilerParams(dimension_semantics=sem, vmem_limit_bytes=vmem)


def _sigmoid(x):
    return 0.5 * jnp.tanh(0.5 * x) + 0.5


def _rstd(x):
    return lax.rsqrt(jnp.mean(x * x, axis=-1, keepdims=True) + RMS_EPS)


def _rms_bwd(t, xhat, rstd):
    return rstd * (t - xhat * jnp.mean(t * xhat, axis=-1, keepdims=True))


def _dot(a, b):
    return jnp.dot(a, b, preferred_element_type=F32)


def _dot_nt(a, b):
    return lax.dot_general(a, b, (((1,), (1,)), ((), ())), preferred_element_type=F32)


def _dot_tn(a, b):
    return lax.dot_general(a, b, (((0,), (0,)), ((), ())), preferred_element_type=F32)


def _dot_exact(a, b):
    return jnp.dot(a, b, preferred_element_type=F32, precision=lax.Precision.HIGHEST)


def _shift_down(x, j, halo):
    rolled = pltpu.roll(x, j, 0)
    row = lax.broadcasted_iota(jnp.int32, halo.shape, 0)
    top = jnp.where(row < j, pltpu.roll(halo, j, 0), rolled[:SUBLANES])
    return jnp.concatenate([top, rolled[SUBLANES:]], axis=0)


def _shift_up(x, j, nxt):
    tm = x.shape[0]
    rolled = pltpu.roll(x, tm - j, 0)
    row = lax.broadcasted_iota(jnp.int32, nxt.shape, 0)
    bot = jnp.where(row >= SUBLANES - j, pltpu.roll(nxt, SUBLANES - j, 0), rolled[tm - SUBLANES:])
    return jnp.concatenate([rolled[:tm - SUBLANES], bot], axis=0)


def _scan_fwd_into(a, u, carry, h_ref):
    tm, width = a.shape
    groups = (tm // SUBLANES, SUBLANES, width)
    a, u = a.reshape(groups), u.reshape(groups)
    sub = lax.broadcasted_iota(jnp.int32, groups, 1)
    d = 1
    while d < SUBLANES:
        keep = sub >= d
        a_s = jnp.where(keep, pltpu.roll(a, d, 1), 1.0)
        u_s = jnp.where(keep, pltpu.roll(u, d, 1), 0.0)
        u = u + a * u_s
        a = a * a_s
        d *= 2
    a, u = a.reshape(tm, width), u.reshape(tm, width)
    for g in range(tm // SUBLANES):
        rows = slice(g * SUBLANES, (g + 1) * SUBLANES)
        h_ref[rows, :] = u[rows] + a[rows] * carry
        carry = h_ref[(g + 1) * SUBLANES - 1:(g + 1) * SUBLANES, :]
    return carry


def _scan_bwd_into(b, u, g_ref):
    tm, width = b.shape
    groups = (tm // SUBLANES, SUBLANES, width)
    b, u = b.reshape(groups), u.reshape(groups)
    sub = lax.broadcasted_iota(jnp.int32, groups, 1)
    d = 1
    while d < SUBLANES:
        keep = sub < SUBLANES - d
        b_s = jnp.where(keep, pltpu.roll(b, SUBLANES - d, 1), 1.0)
        u_s = jnp.where(keep, pltpu.roll(u, SUBLANES - d, 1), 0.0)
        u = u + b * u_s
        b = b * b_s
        d *= 2
    b, u = b.reshape(tm, width), u.reshape(tm, width)
    nxt = jnp.zeros((1, width), F32)
    for g in reversed(range(tm // SUBLANES)):
        rows = slice(g * SUBLANES, (g + 1) * SUBLANES)
        g_ref[rows, :] = u[rows] + b[rows] * nxt
        nxt = g_ref[g * SUBLANES:g * SUBLANES + 1, :]


def _gate_pre(xc, w_ref):
    outs = []
    for n in range(NB):
        outs.append(_dot(xc[:, n * LANES:(n + 1) * LANES].astype(BF16), w_ref[n]))
    return jnp.concatenate(outs, axis=1)


def _gate_pre_t(d, w_ref):
    outs = []
    for n in range(NB):
        outs.append(_dot_nt(d[:, n * LANES:(n + 1) * LANES].astype(BF16), w_ref[n]))
    return jnp.concatenate(outs, axis=1)


def _softplus_neg(lam):
    return jnp.maximum(-lam, 0.0) + jnp.log(1.0 + jnp.exp(-jnp.abs(lam)))


def _row_spec(tm, width):
    return pl.BlockSpec((tm, width), lambda i: (i, 0))


def _const_spec(shape):
    nd = len(shape)
    return pl.BlockSpec(shape, lambda *_: (0,) * nd)


def _weight_spec(shape):
    nd = len(shape)
    return pl.BlockSpec(shape, lambda *_: (0,) * nd, pipeline_mode=pl.Buffered(1))


AUG = 2 * DH
LOG2E = 1.4426950408889634
LN2 = 0.6931471805599453
Q_SCALE = DH ** -0.5 * LOG2E


def _split3(x):
    hi = x.astype(BF16)
    r1 = x - hi.astype(F32)
    mid = r1.astype(BF16)
    lo = (r1 - mid.astype(F32)).astype(BF16)
    return hi, mid, lo


def _extras(col, ones_from):
    t = col.shape[0]
    hi, mid, lo = _split3(jnp.broadcast_to(col, (t, LANES)))
    lane = lax.broadcasted_iota(jnp.int32, (t, LANES), 1)
    rest = jnp.zeros((t, LANES), BF16)
    if ones_from is not None:
        rest = jnp.where((lane >= ones_from) & (lane < ones_from + 3), 1.0, 0.0).astype(BF16)
    return jnp.where(lane == 0, hi, jnp.where(lane == 1, mid, jnp.where(lane == 2, lo, rest)))


def _selectors():
    sel_q = np.zeros((3 * LANES, H * LANES), np.float32)
    sel_k = np.zeros((3 * LANES, H * LANES), np.float32)
    for hd in range(H):
        for piece in range(3):
            sel_q[piece * LANES + hd, hd * LANES + piece] = 1.0
            sel_k[piece * LANES + hd, hd * LANES + 3 + piece] = -1.0
    return jnp.asarray(sel_q, BF16), jnp.asarray(sel_k, BF16)


def _in_proj(x, pre_gain, w_a, w_f, w_b, b_f_pad):
    T = x.shape[0]
    tm = TM
    sel_q, sel_k = _selectors()

    def body(x_ref, g_ref, wa_ref, wf_ref, wb_ref, bf_ref, sq_ref, sk_ref,
             xn_ref, qa_ref, ka_ref, va_ref, ga_ref, xl_ref, gl_ref, flb_ref, vt_ref, c_s, carry):
        @pl.when(pl.program_id(0) == 0)
        def _():
            carry[...] = jnp.zeros_like(carry)

        xv = x_ref[...]
        xn = (xv * _rstd(xv) * g_ref[...]).astype(BF16)
        xn_ref[...] = xn
        for s, o_ref in enumerate((ga_ref, xl_ref, gl_ref)):
            o_ref[...] = _dot_nt(xn, wb_ref[s * D:(s + 1) * D, :]).astype(o_ref.dtype)
        flb = _dot_nt(xn, wf_ref[...]) + bf_ref[...]
        flb_ref[...] = flb
        lane = lax.broadcasted_iota(jnp.int32, flb.shape, 1)
        ls = jnp.where(lane < H, jnp.minimum(flb, 0.0) - jnp.log(1.0 + jnp.exp(-jnp.abs(flb))), 0.0)
        r = lax.broadcasted_iota(jnp.int32, (tm, tm), 0)
        c = lax.broadcasted_iota(jnp.int32, (tm, tm), 1)
        cs = _dot_exact((c <= r).astype(F32), ls) + carry[...]
        c_s[...] = cs
        carry[...] = c_s[tm - 1:tm, :]

        pieces = jnp.concatenate(_split3(cs * LOG2E), axis=1)
        ones_q = jnp.where((lane >= 3) & (lane < 6), 1.0, 0.0)
        ones_k = jnp.where(lane < 3, 1.0, 0.0)
        zq = _dot_nt(xn, wa_ref[0:D, :]) * Q_SCALE
        zk = _dot_nt(xn, wa_ref[D:2 * D, :])
        zv = _dot_nt(xn, wa_ref[2 * D:3 * D, :])
        ex_q = _dot(pieces, sq_ref[...])
        ex_k = _dot(pieces, sk_ref[...])
        for hd in range(H):
            head = slice(hd * DH, (hd + 1) * DH)
            lo, hi = hd * AUG, hd * AUG + DH
            qa_ref[:, lo:hi] = zq[:, head].astype(BF16)
            qa_ref[:, hi:hi + DH] = (ex_q[:, head] + ones_q).astype(BF16)
            ka_ref[:, lo:hi] = zk[:, head].astype(BF16)
            ka_ref[:, hi:hi + DH] = (ex_k[:, head] + ones_k).astype(BF16)
            va_ref[:, lo:hi] = zv[:, head].astype(BF16)
            va_ref[:, hi:hi + DH] = ones_k.astype(BF16)
            vt_ref[lo:hi, :] = jnp.transpose(zv[:, head]).astype(BF16)
            vt_ref[hi:hi + DH, :] = jnp.where(lax.broadcasted_iota(jnp.int32, (DH, tm), 0) < 3, 1.0, 0.0).astype(BF16)

    bf = jax.ShapeDtypeStruct((T, D), BF16)
    aug = jax.ShapeDtypeStruct((T, H * AUG), BF16)
    f32 = jax.ShapeDtypeStruct((T, D), F32)
    sel_spec = _const_spec((3 * LANES, H * LANES))
    return _call(
        body, name="in_proj", grid=(T // tm,),
        in_specs=[_row_spec(tm, D), _const_spec((1, D)), _const_spec((3 * D, D)), _const_spec((LANES, D)),
                  _const_spec((3 * D, D)), _const_spec((1, LANES)), sel_spec, sel_spec],
        out_specs=[_row_spec(tm, D)] + [_row_spec(tm, H * AUG)] * 3 + [_row_spec(tm, D)] * 3 + [_row_spec(tm, LANES)]
        + [pl.BlockSpec((H * AUG, tm), lambda i: (0, i))],
        out_shape=[bf, aug, aug, aug, f32, f32, f32, jax.ShapeDtypeStruct((T, LANES), F32),
                   jax.ShapeDtypeStruct((H * AUG, T), BF16)],
        scratch_shapes=[pltpu.VMEM((tm, LANES), F32), pltpu.VMEM((1, LANES), F32)],
        compiler_params=_cparams(("arbitrary",), VMEM_BIG),
    )(x, pre_gain, w_a, w_f, w_b, b_f_pad, sel_q, sel_k)


def _causal_pairs(n, q_major):
    if q_major:
        pairs = [(qi, ki) for qi in range(n) for ki in range(qi + 1)]
    else:
        pairs = [(ki, qi) for ki in range(n) for qi in range(ki, n)]
    return (jnp.asarray([a for a, _ in pairs], jnp.int32), jnp.asarray([b for _, b in pairs], jnp.int32))


def _attn_fwd(q_aug, k_aug, vt_aug, shards=(), whole=()):
    T = q_aug.shape[0]
    t = min(T, TA_FWD)
    n = T // t
    hp = FWD_HEADS
    heads = range(hp)
    qi_tab, ki_tab = _causal_pairs(n, q_major=True)
    na, nall = len(shards), len(shards) + len(whole)
    n_h, n_j = H // hp, qi_tab.shape[0]

    def body(qi_ref, ki_ref, q_ref, k_ref, vt_ref, *rest):
        srcs, rest = rest[:nall], rest[nall:]
        o_ref, qx_ref = rest[:2]
        dsts, rest = rest[2:2 + nall], rest[2 + nall:]
        m_s, acc_s = rest[:2]
        h = pl.program_id(0)
        j = pl.program_id(1)
        qi = qi_ref[j]
        ki = ki_ref[j]

        if nall:
            gather = _GatherPlan(srcs, dsts, rest[2:], na)
            step = h * n_j + j
            pl.when(step == 0)(gather.send)
            pl.when(step == n_h * n_j // 2)(gather.forward)
            pl.when(step == n_h * n_j - 1)(gather.finish)

        @pl.when(ki == 0)
        def _():
            m_s[...] = jnp.full(m_s.shape, NEG, F32)
            acc_s[...] = jnp.zeros_like(acc_s)

        def step(on_diagonal):
            cols = [slice(a * AUG, (a + 1) * AUG) for a in heads]
            if on_diagonal:
                krow = lax.broadcasted_iota(jnp.int32, (t, t), 0)
                qcol = lax.broadcasted_iota(jnp.int32, (t, t), 1)
            def logits(a):
                st = _dot_nt(k_ref[:, cols[a]], q_ref[:, cols[a]])
                return jnp.where(krow <= qcol, st, NEG) if on_diagonal else st

            st_next = logits(0)
            for a in heads:
                st = st_next
                if a + 1 < hp:
                    st_next = logits(a + 1)
                m_prev = m_s[a]
                m_new = jnp.maximum(m_prev, jnp.max(st, axis=0, keepdims=True))
                pt = jnp.exp2(st - m_new).astype(BF16)
                acc_s[a] = jnp.exp2(m_prev - m_new) * acc_s[a] + _dot(vt_ref[cols[a], :], pt)
                m_s[a] = m_new

        @pl.when(ki < qi)
        def _():
            step(False)

        @pl.when(ki == qi)
        def _():
            step(True)
            piece = lax.broadcasted_iota(jnp.int32, (DH, t), 0)
            for a in heads:
                l = acc_s[a, DH:DH + 1, :]
                ex = jnp.transpose(q_ref[:, a * AUG + DH:(a + 1) * AUG].astype(F32))
                c2 = jnp.sum(jnp.where(piece < 3, ex, 0.0), axis=0, keepdims=True)
                hi, mid, lo = _split3(jnp.broadcast_to(c2 - (m_s[a] + jnp.log(l) * LOG2E), (DH, t)))
                ones = jnp.where((piece >= 3) & (piece < 6), 1.0, 0.0).astype(BF16)
                ex_t = jnp.where(piece == 0, hi, jnp.where(piece == 1, mid, jnp.where(piece == 2, lo, ones)))
                o_ref[:, a * DH:(a + 1) * DH] = jnp.transpose(acc_s[a, :DH, :] / l)
                qx_ref[:, a * DH:(a + 1) * DH] = jnp.transpose(ex_t.astype(F32)).astype(BF16)

    q_spec = pl.BlockSpec((t, hp * AUG), lambda h, j, qi_ref, ki_ref: (qi_ref[j], h))
    k_spec = pl.BlockSpec((t, hp * AUG), lambda h, j, qi_ref, ki_ref: (ki_ref[j], h))
    vt_spec = pl.BlockSpec((hp * AUG, t), lambda h, j, qi_ref, ki_ref: (h, ki_ref[j]))
    out_spec = pl.BlockSpec((t, hp * DH), lambda h, j, qi_ref, ki_ref: (qi_ref[j], h))
    arrs = list(shards) + list(whole)
    grid_spec = pltpu.PrefetchScalarGridSpec(
        num_scalar_prefetch=2, grid=(n_h, n_j),
        in_specs=[q_spec, k_spec, vt_spec] + [HBM_SPEC] * nall, out_specs=[out_spec, out_spec] + [HBM_SPEC] * nall,
        scratch_shapes=[pltpu.VMEM((hp, 1, t), F32), pltpu.VMEM((hp, AUG, t), F32)]
        + (_gather_semaphores(na, nall) if nall else []))
    outs = _call(
        body, name="attn_fwd", grid_spec=grid_spec,
        out_shape=[jax.ShapeDtypeStruct((T, D), F32), jax.ShapeDtypeStruct((T, D), BF16)] + _gather_out_shapes(arrs),
        compiler_params=_cparams(("arbitrary", "arbitrary"), VMEM_BIG),
    )(qi_tab, ki_tab, q_aug, k_aug, vt_aug, *arrs)
    return outs[0], outs[1], _place_own(outs[2:], arrs)


def _sigmoid_small(x):
    e = jnp.exp(x)
    return jnp.where(x < -8.0, e - e * e, _sigmoid(x))


def _lru_gates(xc, wr_ref, br_ref, wi_ref, bi_ref, lam_ref):
    r = _sigmoid_small(_gate_pre(xc, wr_ref) + br_ref[...])
    ig = _sigmoid(_gate_pre(xc, wi_ref) + bi_ref[...])
    sp = _softplus_neg(lam_ref[...])
    la = (-LRU_C) * r * sp
    a = jnp.exp(la)
    y = -jnp.tanh(la) * (a * a + 1.0)
    return r, ig, sp, a, jnp.sqrt(y), lax.rsqrt(y)


def _branches_fwd(o, g_attn, x_lru, g_lru, gain_a, gain_l, conv_w, conv_b, w_r, b_r, w_i, b_i, lam):
    T = o.shape[0]
    tm = TM

    def body(o_ref, ga_ref, xl_ref, gl_ref, gna_ref, gnl_ref, cw_ref, cb_ref, wr_ref, br_ref, wi_ref, bi_ref,
             lam_ref, ycat_ref, xc_ref, h_ref, halo_s, hc_s):
        @pl.when(pl.program_id(0) == 0)
        def _():
            halo_s[...] = jnp.zeros_like(halo_s)
            hc_s[...] = jnp.zeros_like(hc_s)

        ov = o_ref[...]
        ga = ga_ref[...]
        ya = ov * _rstd(ov) * gna_ref[...] * (ga * _sigmoid(ga))
        ycat_ref[:, :D] = ya.astype(BF16)

        xl = xl_ref[...]
        halo = halo_s[...]
        xc = xl * cw_ref[3:4, :] + cb_ref[...]
        for j in range(3):
            xc = xc + _shift_down(xl, 3 - j, halo) * cw_ref[j:j + 1, :]
        halo_s[...] = xl_ref[tm - SUBLANES:tm, :]
        xc_ref[...] = xc

        _, ig, _, a, sq, _ = _lru_gates(xc, wr_ref, br_ref, wi_ref, bi_ref, lam_ref)
        u = sq * (ig * xc)
        hc_s[...] = _scan_fwd_into(a, u, hc_s[...], h_ref)
        hh = h_ref[...]

        gl = gl_ref[...]
        yl = hh * _rstd(hh) * gnl_ref[...] * (gl * _sigmoid(gl))
        ycat_ref[:, D:] = yl.astype(BF16)

    vec = _const_spec((1, D))
    wspec = _const_spec((NB, LANES, LANES))
    return _call(
        body, name="branches_fwd", grid=(T // tm,),
        in_specs=[_row_spec(tm, D)] * 4 + [vec, vec, _const_spec((4, D)), vec, wspec, vec, wspec, vec, vec],
        out_specs=[_row_spec(tm, DMIX), _row_spec(tm, D), _row_spec(tm, D)],
        out_shape=[jax.ShapeDtypeStruct((T, DMIX), BF16), jax.ShapeDtypeStruct((T, D), F32),
                   jax.ShapeDtypeStruct((T, D), F32)],
        scratch_shapes=[pltpu.VMEM((SUBLANES, D), F32), pltpu.VMEM((1, D), F32)],
        compiler_params=_cparams(("arbitrary",)),
    )(o, g_attn, x_lru, g_lru, gain_a, gain_l, conv_w, conv_b, w_r, b_r, w_i, b_i, lam)


def _tail(ycat, x, p, tgt, w_out, post_gain, w_ple, ple_gain, w_gate, b_gate):
    T = x.shape[0]
    tm = TM

    def body(ycat_ref, x_ref, p_ref, t_ref, wo_ref, pg_ref, wp_ref, eg_ref, wg_ref, bg_ref,
             dh1_ref, dycat_ref, dmix_ref, h1b_ref, dgp_ref, pb_ref, dpe_ref, acc_ref):
        @pl.when(pl.program_id(0) == 0)
        def _():
            acc_ref[...] = jnp.zeros_like(acc_ref)

        mix = _dot(ycat_ref[...], wo_ref[...])
        rstd_m = _rstd(mix)
        mhat = mix * rstd_m
        h1 = x_ref[...] + mhat * pg_ref[...]
        pb = p_ref[...].astype(BF16)
        pb_ref[...] = pb
        pe = _dot(pb, wp_ref[...])
        rstd_p = _rstd(pe)
        pehat = pe * rstd_p
        e = pehat * eg_ref[...]
        h1b = h1.astype(BF16)
        h1b_ref[...] = h1b
        gate = _sigmoid(_dot(h1b, wg_ref[...]) + bg_ref[...])
        diff = (h1 + gate * e) - t_ref[...]

        dy = diff * (1.0 / D)
        de = dy * gate
        dgp = (dy * e) * gate * (1.0 - gate)
        dgpb = dgp.astype(BF16)
        dgp_ref[...] = dgpb
        dh1 = dy + _dot_nt(dgpb, wg_ref[...])
        dh1_ref[...] = dh1
        dpe_ref[...] = _rms_bwd(de * eg_ref[...], pehat, rstd_p).astype(BF16)
        dmix = _rms_bwd(dh1 * pg_ref[...], mhat, rstd_m).astype(BF16)
        dmix_ref[...] = dmix
        dycat_ref[...] = _dot_nt(dmix, wo_ref[...])

        acc_ref[0:1, :] += jnp.sum(dh1 * mhat, axis=0, keepdims=True)
        acc_ref[1:2, :] += jnp.sum(de * pehat, axis=0, keepdims=True)
        acc_ref[2:3, :] += jnp.sum(dgp, axis=0, keepdims=True)
        acc_ref[3:4, :] += jnp.sum(diff * diff, axis=0, keepdims=True) * (0.5 / D)

    vec = _const_spec((1, D))
    bf = jax.ShapeDtypeStruct((T, D), BF16)
    return _call(
        body, name="tail", grid=(T // tm,),
        in_specs=[_row_spec(tm, DMIX), _row_spec(tm, D), _row_spec(tm, DPLE), _row_spec(tm, D),
                  _const_spec((DMIX, D)), vec, _const_spec((DPLE, D)), vec, _const_spec((D, D)), vec],
        out_specs=[_row_spec(tm, D), _row_spec(tm, DMIX), _row_spec(tm, D), _row_spec(tm, D), _row_spec(tm, D),
                   _row_spec(tm, DPLE), _row_spec(tm, D), _const_spec((SUBLANES, D))],
        out_shape=[jax.ShapeDtypeStruct((T, D), F32), jax.ShapeDtypeStruct((T, DMIX), F32), bf, bf, bf,
                   jax.ShapeDtypeStruct((T, DPLE), BF16), bf, jax.ShapeDtypeStruct((SUBLANES, D), F32)],
        compiler_params=_cparams(("arbitrary",), VMEM_BIG),
    )(ycat, x, p, tgt, w_out, post_gain, w_ple, ple_gain, w_gate, b_gate)


def _pair_copies(srcs, gots, send_sems, recv_sems):
    x, y, c = _position()
    copies = []
    for a, (src, got) in enumerate(zip(srcs, gots)):
        half = src.shape[1] // 2
        rows = pl.ds(pl.multiple_of((1 - c) * half, SUBLANES), half)
        copies.append(pltpu.make_async_remote_copy(
            src_ref=src.at[:, rows, :], dst_ref=got, send_sem=send_sems.at[a], recv_sem=recv_sems.at[a],
            device_id=(x, y, 1 - c), device_id_type=MESH))
    return copies


def _branches_bwd(dycat, o, g_attn, h, g_lru, gain_a, gain_l, ycat, dmix, h1b, dgp, pb, dpe):
    T = o.shape[0]
    tm = TM
    nt = T // tm
    nb_gate, nb_ple = min(nt, 8), min(nt, 2)
    ns_gate, ns_ple = nt // nb_gate, nt // nb_ple
    br_out, br_gate, br_ple = DMIX // nt, D // nb_gate, DPLE // nb_ple
    tk_gate, tk_ple = T // ns_gate, T // ns_ple

    def body(dy_ref, o_ref, ga_ref, h_ref, gl_ref, gna_ref, gnl_ref, yc_ref, dmix_ref, h1_ref, dgp_ref, pb_ref,
             dpe_ref, do_ref, dga_ref, dgl_ref, dh_ref, acc_ref, gwo_ref, gwg_ref, gwp_ref):
        i = pl.program_id(0)

        @pl.when(i == 0)
        def _():
            acc_ref[...] = jnp.zeros_like(acc_ref)

        def accumulate(out_ref, lhs_ref, rhs_ref, tokens, slices):
            s = i % slices
            part = _dot_tn(lhs_ref[...], rhs_ref[pl.ds(pl.multiple_of(s * tokens, tokens), tokens), :])
            out_ref[...] = part + jnp.where(s == 0, 0.0, out_ref[...])

        gwo_ref[...] = _dot_tn(yc_ref[...], dmix_ref[...])

        def branch(val, g, gain, dyv):
            rstd = _rstd(val)
            vhat = val * rstd
            sig = _sigmoid(g)
            dn = dyv * (g * sig)
            dg = dyv * (vhat * gain) * (sig * (1.0 + g * (1.0 - sig)))
            dgain = jnp.sum(dn * vhat, axis=0, keepdims=True)
            return _rms_bwd(dn * gain, vhat, rstd), dg, dgain

        ov = o_ref[...]
        do, dga, dgain_a = branch(ov, ga_ref[...], gna_ref[...], dy_ref[:, :D])
        dga_ref[...] = dga.astype(BF16)
        prod = do * ov
        for hd in range(H):
            head = slice(hd * DH, (hd + 1) * DH)
            do_ref[:, hd * AUG:hd * AUG + DH] = do[:, head].astype(BF16)
            do_ref[:, hd * AUG + DH:(hd + 1) * AUG] = _extras(-jnp.sum(prod[:, head], axis=1, keepdims=True), None)

        accumulate(gwg_ref, h1_ref, dgp_ref, tk_gate, ns_gate)
        accumulate(gwp_ref, pb_ref, dpe_ref, tk_ple, ns_ple)
        dh, dgl, dgain_l = branch(h_ref[...], gl_ref[...], gnl_ref[...], dy_ref[:, D:])
        dh_ref[...] = dh
        dgl_ref[...] = dgl.astype(BF16)
        acc_ref[0:1, :] += dgain_a
        acc_ref[1:2, :] += dgain_l

    vec = _const_spec((1, D))
    bf = jax.ShapeDtypeStruct((T, D), BF16)
    tokens = _weight_spec((T, D))
    return _call(
        body, name="branches_bwd", grid=(nt,),
        in_specs=[_row_spec(tm, DMIX)] + [_row_spec(tm, D)] * 4 + [vec, vec]
        + [pl.BlockSpec((T, br_out), lambda i: (0, i)), tokens,
           pl.BlockSpec((tk_gate, br_gate), lambda i: (i % ns_gate, i // ns_gate)), tokens,
           pl.BlockSpec((tk_ple, br_ple), lambda i: (i % ns_ple, i // ns_ple)), tokens],
        out_specs=[_row_spec(tm, H * AUG), _row_spec(tm, D), _row_spec(tm, D), _row_spec(tm, D),
                   _const_spec((SUBLANES, D)),
                   pl.BlockSpec((br_out, D), lambda i: (i, 0)),
                   pl.BlockSpec((br_gate, D), lambda i: (i // ns_gate, 0)),
                   pl.BlockSpec((br_ple, D), lambda i: (i // ns_ple, 0))],
        out_shape=[jax.ShapeDtypeStruct((T, H * AUG), BF16), bf, bf, jax.ShapeDtypeStruct((T, D), F32),
                   jax.ShapeDtypeStruct((SUBLANES, D), F32), jax.ShapeDtypeStruct((DMIX, D), F32),
                   jax.ShapeDtypeStruct((D, D), F32), jax.ShapeDtypeStruct((DPLE, D), F32)],
        compiler_params=_cparams(("arbitrary",), VMEM_BIG),
    )(dycat, o, g_attn, h, g_lru, gain_a, gain_l, ycat, dmix, h1b, dgp, pb, dpe)


def _lru_bwd(dh, h, xc, x_lru, conv_w, w_r, b_r, w_i, b_i, lam, pair_parts=()):
    T = dh.shape[0]
    tm = TM
    nt = T // tm
    per = tm // SUBLANES
    npair = len(pair_parts)

    def body(dh_ref, h_ref, hprev_ref, xc_ref, xl_ref, cw_ref, wr_ref, br_ref, wi_ref, bi_ref, lam_ref, *rest):
        parts, rest = rest[:npair], rest[npair:]
        dxl_ref, dwr_ref, dwi_ref, acc_ref = rest[:4]
        gots, rest = rest[4:4 + npair], rest[4 + npair:]
        carry_s, dxc_next_s, top_s, dht_s = rest[:4]
        i = pl.program_id(0)

        @pl.when(i == 0)
        def _():
            acc_ref[...] = jnp.zeros_like(acc_ref)
            dwr_ref[...] = jnp.zeros_like(dwr_ref)
            dwi_ref[...] = jnp.zeros_like(dwi_ref)
            carry_s[...] = jnp.zeros_like(carry_s)
            dxc_next_s[...] = jnp.zeros_like(dxc_next_s)
            for cp in _pair_copies(parts, gots, *rest[4:]) if npair else ():
                cp.start()

        if npair:
            @pl.when(i == nt - 1)
            def _():
                for cp in _pair_copies(parts, gots, *rest[4:]):
                    cp.wait()

        inner = jnp.where(i == nt - 1, 0.0, 1.0)
        xc = xc_ref[...]
        r, ig, sp, a, sq, inv_sq = _lru_gates(xc, wr_ref, br_ref, wi_ref, bi_ref, lam_ref)

        row = lax.broadcasted_iota(jnp.int32, (tm, D), 0)
        u = dh_ref[...] + jnp.where(row == tm - 1, carry_s[...], 0.0)
        _scan_bwd_into(pltpu.roll(a, tm - 1, 0), u, dht_s)
        dht = dht_s[...]
        top_s[...] = a[:SUBLANES, :] * dht[:SUBLANES, :]
        carry_s[...] = top_s[0:1, :]

        hprev = hprev_ref[...] * inner
        da = dht * _shift_down(h_ref[...], 1, hprev)
        dig = dht * sq * xc
        dxc = dht * sq * ig
        dsq = dht * ig * xc
        dla = da * a - dsq * (a * a) * inv_sq
        dr = dla * ((-LRU_C) * sp)
        dpr = dr * r * (1.0 - r)
        dpi = dig * ig * (1.0 - ig)
        for n in range(NB):
            blk = slice(n * LANES, (n + 1) * LANES)
            xcb = xc[:, blk].astype(BF16)
            dwr_ref[n] += _dot_tn(xcb, dpr[:, blk].astype(BF16))
            dwi_ref[n] += _dot_tn(xcb, dpi[:, blk].astype(BF16))
        dxc = dxc + _gate_pre_t(dpr, wr_ref) + _gate_pre_t(dpi, wi_ref)

        xl = xl_ref[...]
        nxt = dxc_next_s[...]
        dxl = dxc * cw_ref[3:4, :]
        acc_ref[3:4, :] += jnp.sum(dxc * xl, axis=0, keepdims=True)
        for j in range(3):
            ahead = _shift_up(dxc, 3 - j, nxt)
            dxl = dxl + ahead * cw_ref[j:j + 1, :]
            acc_ref[j:j + 1, :] += jnp.sum(ahead * xl, axis=0, keepdims=True)
        dxc_next_s[...] = dxc[:SUBLANES, :]
        dxl_ref[...] = dxl.astype(BF16)

        acc_ref[4:5, :] += jnp.sum(dxc, axis=0, keepdims=True)
        acc_ref[5:6, :] += jnp.sum(dpr, axis=0, keepdims=True)
        acc_ref[6:7, :] += jnp.sum(dpi, axis=0, keepdims=True)
        acc_ref[7:8, :] += jnp.sum(dla * ((-LRU_C) * r), axis=0, keepdims=True)

        @pl.when(i == nt - 1)
        def _():
            lam_v = lam_ref[...]
            acc_ref[7:8, :] = acc_ref[7:8, :] * (-_sigmoid(-lam_v))

    rev = pl.BlockSpec((tm, D), lambda i: (nt - 1 - i, 0))
    prev8 = pl.BlockSpec((SUBLANES, D), lambda i: (jnp.maximum((nt - 1 - i) * per - 1, 0), 0))
    vec = _const_spec((1, D))
    wspec = _const_spec((NB, LANES, LANES))
    bf = jax.ShapeDtypeStruct((T, D), BF16)
    halves = [jax.ShapeDtypeStruct((s.shape[0], s.shape[1] // 2, s.shape[2]), s.dtype) for s in pair_parts]
    outs = _call(
        body, name="lru_bwd", grid=(nt,),
        in_specs=[rev, rev, prev8, rev, rev, _const_spec((4, D)), wspec, vec, wspec, vec, vec] + [HBM_SPEC] * npair,
        out_specs=[rev, wspec, wspec, _const_spec((SUBLANES, D))] + [HBM_SPEC] * npair,
        out_shape=[bf, jax.ShapeDtypeStruct((NB, LANES, LANES), F32), jax.ShapeDtypeStruct((NB, LANES, LANES), F32),
                   jax.ShapeDtypeStruct((SUBLANES, D), F32)] + halves,
        scratch_shapes=[pltpu.VMEM((1, D), F32), pltpu.VMEM((SUBLANES, D), F32), pltpu.VMEM((SUBLANES, D), F32),
                        pltpu.VMEM((tm, D), F32)]
        + ([pltpu.SemaphoreType.DMA((npair,)), pltpu.SemaphoreType.DMA((npair,))] if npair else []),
        compiler_params=_cparams(("arbitrary",)),
    )(dh, h, h, xc, x_lru, conv_w, w_r, b_r, w_i, b_i, lam, *pair_parts)
    return (*outs[:4], list(outs[4:]))


def _chip_copies(srcs, dsts, send_sems, recv_sems):
    x, y, c = _position()
    chip = 2 * x + y
    na = len(srcs)
    return [pltpu.make_async_remote_copy(
        src_ref=srcs[a].at[2 * px + py], dst_ref=dsts[a].at[chip], send_sem=send_sems.at[j * na + a],
        recv_sem=recv_sems.at[j * na + a], device_id=(px, py, c), device_id_type=MESH)
        for j, (px, py) in enumerate(_other_chips(x, y)) for a in range(na)]


def _attn_bwd(q_aug, qx, k_aug, v_aug, do_aug, exchange=()):
    T = q_aug.shape[0]
    t = TA
    n = T // t
    hp = BWD_HEADS
    heads = range(hp)
    scale = DH ** -0.5
    ki_tab, qi_tab = _causal_pairs(n, q_major=False)
    last = ki_tab.shape[0] - 1
    ne = len(exchange)
    n_h = H // hp

    def body(ki_ref, qi_ref, q_ref, qx_ref, k_ref, v_ref, do_ref, *rest):
        sent, rest = rest[:ne], rest[ne:]
        dq_ref, dk_ref, dv_ref, dc_ref = rest[:4]
        received, rest = rest[4:4 + ne], rest[4 + ne:]
        dq_s, dk_s, dv_s = rest[:3]
        j = pl.program_id(1)
        ki = ki_ref[j]
        qi = qi_ref[j]

        if ne:
            first_step = (pl.program_id(0) == 0) & (j == 0)
            last_step = (pl.program_id(0) == n_h - 1) & (j == last)

            @pl.when(first_step)
            def _():
                for cp in _chip_copies(sent, received, *rest[3:]):
                    cp.start()

            @pl.when(last_step)
            def _():
                for cp in _chip_copies(sent, received, *rest[3:]):
                    cp.wait()

        @pl.when(j == 0)
        def _():
            dq_s[...] = jnp.zeros_like(dq_s)

        @pl.when(qi == ki)
        def _():
            dk_s[...] = jnp.zeros_like(dk_s)
            dv_s[...] = jnp.zeros_like(dv_s)

        def step(on_diagonal):
            cols = [slice(a * AUG, (a + 1) * AUG) for a in heads]
            qb = [jnp.concatenate([q_ref[:, a * AUG:a * AUG + DH], qx_ref[:, a * DH:(a + 1) * DH]], axis=1)
                  for a in heads]
            if on_diagonal:
                krow = lax.broadcasted_iota(jnp.int32, (t, t), 0)
                qcol = lax.broadcasted_iota(jnp.int32, (t, t), 1)

            def scores(a):
                st = _dot_nt(k_ref[:, cols[a]], qb[a])
                dpd = _dot_nt(v_ref[:, cols[a]], do_ref[:, cols[a]])
                return (jnp.where(krow <= qcol, st, NEG) if on_diagonal else st), dpd

            off = pl.multiple_of(qi * t, t)
            ahead = scores(0)
            for a in heads:
                st, dpd = ahead
                if a + 1 < hp:
                    ahead = scores(a + 1)
                pt = jnp.exp2(st)
                dsb = (pt * dpd).astype(BF16)
                dv_s[a] += _dot(pt.astype(BF16), do_ref[:, a * AUG:a * AUG + DH])
                dk_s[a] += _dot(dsb, qb[a])
                dq_s[a, pl.ds(off, t), :] += _dot_tn(dsb, k_ref[:, cols[a]])

        @pl.when(qi > ki)
        def _():
            step(False)

        @pl.when(qi == ki)
        def _():
            step(True)

        @pl.when(qi == n - 1)
        def _():
            rows = pl.ds(pl.multiple_of(ki * t, t), t)
            for a in heads:
                dk_ref[:, a * DH:(a + 1) * DH] = (dk_s[a, :, :DH] * LN2).astype(BF16)
                dv_ref[:, a * DH:(a + 1) * DH] = dv_s[a].astype(BF16)
                dc_ref[a, rows, :] = jnp.broadcast_to(-dk_s[a, :, DH + 3:DH + 4], (t, LANES))

        @pl.when(j == last)
        def _():
            for a in heads:
                dq_ref[:, a * DH:(a + 1) * DH] = (dq_s[a, :, :DH] * scale).astype(BF16)
                dc_ref[a] = dc_ref[a] + jnp.broadcast_to(dq_s[a, :, DH:DH + 1], (T, LANES))

    qside = pl.BlockSpec((t, hp * AUG), lambda h, j, ki_ref, qi_ref: (qi_ref[j], h))
    qxside = pl.BlockSpec((t, hp * DH), lambda h, j, ki_ref, qi_ref: (qi_ref[j], h))
    kside = pl.BlockSpec((t, hp * AUG), lambda h, j, ki_ref, qi_ref: (ki_ref[j], h))
    kout = pl.BlockSpec((t, hp * DH), lambda h, j, ki_ref, qi_ref: (ki_ref[j], h))
    bf = jax.ShapeDtypeStruct((T, D), BF16)
    sums = jax.ShapeDtypeStruct((H, T, LANES), F32)
    grid_spec = pltpu.PrefetchScalarGridSpec(
        num_scalar_prefetch=2, grid=(n_h, ki_tab.shape[0]),
        in_specs=[qside, qxside, kside, kside, qside] + [HBM_SPEC] * ne,
        out_specs=[pl.BlockSpec((T, hp * DH), lambda h, j, ki_ref, qi_ref: (0, h)), kout, kout,
                   pl.BlockSpec((hp, T, LANES), lambda h, j, ki_ref, qi_ref: (h, 0, 0))] + [HBM_SPEC] * ne,
        scratch_shapes=[pltpu.VMEM((hp, T, AUG), F32), pltpu.VMEM((hp, t, AUG), F32), pltpu.VMEM((hp, t, DH), F32)]
        + ([pltpu.SemaphoreType.DMA((3 * ne,)), pltpu.SemaphoreType.DMA((3 * ne,))] if ne else []))
    outs = _call(
        body, name="attn_bwd", grid_spec=grid_spec,
        out_shape=[bf, bf, bf, sums] + [jax.ShapeDtypeStruct(s.shape, s.dtype) for s in exchange],
        compiler_params=_cparams(("arbitrary", "arbitrary"), VMEM_BIG),
    )(ki_tab, qi_tab, q_aug, qx, k_aug, v_aug, do_aug, *exchange)
    return (*outs[:4], list(outs[4:]))


def _fgate_bwd(dc_heads, flb):
    T = flb.shape[0]
    tm = TM
    nt = T // tm

    def body(dch_ref, flb_ref, dfl_ref, acc_ref, carry, top_s):
        @pl.when(pl.program_id(0) == 0)
        def _():
            carry[...] = jnp.zeros_like(carry)
            acc_ref[...] = jnp.zeros_like(acc_ref)

        flb = flb_ref[...]
        lane = lax.broadcasted_iota(jnp.int32, flb.shape, 1)
        dc = jnp.zeros(flb.shape, F32)
        for hd in range(H):
            dc = dc + jnp.where(lane == hd, dch_ref[hd], 0.0)
        r = lax.broadcasted_iota(jnp.int32, (tm, tm), 0)
        c = lax.broadcasted_iota(jnp.int32, (tm, tm), 1)
        dls = _dot_exact((c >= r).astype(F32), dc) + carry[...]
        top_s[...] = dls[:SUBLANES, :]
        carry[...] = top_s[0:1, :]
        dfl = jnp.where(lane < H, dls * _sigmoid(-flb), 0.0)
        dfl_ref[...] = dfl.astype(BF16)
        acc_ref[0:1, :] += jnp.sum(dfl, axis=0, keepdims=True)

    rev = pl.BlockSpec((tm, LANES), lambda i: (nt - 1 - i, 0))
    return _call(
        body, name="fgate_bwd", grid=(nt,),
        in_specs=[pl.BlockSpec((H, tm, LANES), lambda i: (0, nt - 1 - i, 0)), rev],
        out_specs=[rev, _const_spec((SUBLANES, LANES))],
        out_shape=[jax.ShapeDtypeStruct((T, LANES), BF16), jax.ShapeDtypeStruct((SUBLANES, LANES), F32)],
        scratch_shapes=[pltpu.VMEM((1, LANES), F32), pltpu.VMEM((SUBLANES, LANES), F32)],
        compiler_params=_cparams(("arbitrary",)),
    )(dc_heads, flb)


def _dx(dz, dfl, w_a, w_f, w_b, x, pre_gain, dh1, exchange=()):
    T = x.shape[0]
    tm = TM
    nt = T // tm
    ne = len(exchange)

    def body(*refs):
        dz_refs = refs[:6]
        dfl_ref, wa_ref, wf_ref, wb_ref, x_ref, g_ref, dh1_ref = refs[6:13]
        sent = refs[13:13 + ne]
        gx_ref, acc_ref = refs[13 + ne:15 + ne]
        received, sems = refs[15 + ne:15 + 2 * ne], refs[15 + 2 * ne:]

        @pl.when(pl.program_id(0) == 0)
        def _():
            acc_ref[...] = jnp.zeros_like(acc_ref)
            for cp in _chip_copies(sent, received, *sems) if ne else ():
                cp.start()

        if ne:
            @pl.when(pl.program_id(0) == nt - 1)
            def _():
                for cp in _chip_copies(sent, received, *sems):
                    cp.wait()

        dxn = _dot(dfl_ref[...], wf_ref[...])
        for s in range(3):
            dxn = dxn + _dot(dz_refs[s][...], wa_ref[s * D:(s + 1) * D, :])
            dxn = dxn + _dot(dz_refs[3 + s][...], wb_ref[s * D:(s + 1) * D, :])
        xv = x_ref[...]
        rstd = _rstd(xv)
        xhat = xv * rstd
        gx_ref[...] = dh1_ref[...] + _rms_bwd(dxn * g_ref[...], xhat, rstd)
        acc_ref[0:1, :] += jnp.sum(dxn * xhat, axis=0, keepdims=True)

    outs = _call(
        body, name="dx", grid=(nt,),
        in_specs=[_row_spec(tm, D)] * 6 + [_row_spec(tm, LANES), _weight_spec((3 * D, D)), _weight_spec((LANES, D)),
                                           _weight_spec((3 * D, D)), _row_spec(tm, D), _const_spec((1, D)),
                                           _row_spec(tm, D)] + [HBM_SPEC] * ne,
        out_specs=[_row_spec(tm, D), _const_spec((SUBLANES, D))] + [HBM_SPEC] * ne,
        out_shape=[jax.ShapeDtypeStruct((T, D), F32), jax.ShapeDtypeStruct((SUBLANES, D), F32)]
        + [jax.ShapeDtypeStruct(s.shape, s.dtype) for s in exchange],
        scratch_shapes=[pltpu.SemaphoreType.DMA((3 * ne,)), pltpu.SemaphoreType.DMA((3 * ne,))] if ne else [],
        compiler_params=_cparams(("arbitrary",), VMEM_BIG),
    )(*dz, dfl, w_a, w_f, w_b, x, pre_gain, dh1, *exchange)
    return outs[0], outs[1], list(outs[2:])


GRAD_ROWS = D_IN + SUBLANES


def _dw_in_segments(dz_a, dz_b, xn, buf, pair, bt):
    T = xn.shape[0]
    nt = T // bt
    first, second = [(2 * pair + k) * D + (H if 2 * pair + k >= 3 else 0) for k in (0, 1)]
    step8 = (second - first) // SUBLANES

    def body(*refs):
        dza_ref, dzb_ref, xn_ref, o_ref = refs[0], refs[1], refs[2], refs[-1]

        @pl.when(pl.program_id(1) == 0)
        def _():
            o_ref[...] = jnp.zeros_like(o_ref)

        @pl.when(pl.program_id(0) == 0)
        def _():
            o_ref[...] += _dot_tn(dza_ref[...], xn_ref[...])

        @pl.when(pl.program_id(0) == 1)
        def _():
            o_ref[...] += _dot_tn(dzb_ref[...], xn_ref[...])

    spec_a = pl.BlockSpec((bt, D), lambda s, t: (jnp.where(s == 0, t, nt - 1), 0))
    spec_b = pl.BlockSpec((bt, D), lambda s, t: (jnp.where(s == 1, t, 0), 0))
    return _call(
        body, name="dw_in_%d" % pair, grid=(2, nt),
        in_specs=[spec_a, spec_b, pl.BlockSpec((bt, D), lambda s, t: (t, 0))]
        + ([] if buf is None else [pl.BlockSpec(memory_space=pl.ANY)]),
        out_specs=pl.BlockSpec((pl.Element(D), pl.Element(D)),
                               lambda s, t: ((first // SUBLANES + s * step8) * SUBLANES, 0)),
        out_shape=jax.ShapeDtypeStruct((GRAD_ROWS, D), F32),
        input_output_aliases={} if buf is None else {3: 0},
        compiler_params=_cparams(("arbitrary", "arbitrary"), VMEM_BIG),
    )(*((dz_a, dz_b, xn) if buf is None else (dz_a, dz_b, xn, buf)))


def _dw_in_t(dz, dfl, xn, bt=DW_TOKENS):
    T = xn.shape[0]
    bt = min(bt, T)
    nt = T // bt
    main = None
    for pair in range(3):
        main = _dw_in_segments(dz[2 * pair], dz[2 * pair + 1], xn, main, pair, bt)

    def f_body(dfl_ref, xn_ref, main_ref, o_ref, acc_s):
        p = pl.program_id(0)
        t = pl.program_id(1)

        @pl.when(t == 0)
        def _():
            acc_s[...] = jnp.zeros_like(acc_s)

        @pl.when(p == 0)
        def _():
            acc_s[...] += _dot_tn(dfl_ref[...], xn_ref[...])

        @pl.when(t == nt - 1)
        def _():
            o_ref[...] = acc_s[:SUBLANES, :]

    fl_block = FL0 // SUBLANES
    end_block = D_IN // SUBLANES
    return _call(
        f_body, name="dw_in_f", grid=(2, nt),
        in_specs=[pl.BlockSpec((bt, LANES), lambda p, t: (t, 0)), pl.BlockSpec((bt, D), lambda p, t: (t, 0)),
                  pl.BlockSpec(memory_space=pl.ANY)],
        out_specs=pl.BlockSpec((SUBLANES, D), lambda p, t: (fl_block + p * (end_block - fl_block), 0)),
        out_shape=jax.ShapeDtypeStruct((GRAD_ROWS, D), F32),
        scratch_shapes=[pltpu.VMEM((LANES, D), F32)],
        input_output_aliases={2: 0},
        compiler_params=_cparams(("arbitrary", "arbitrary")),
    )(dfl, xn, main)


HBM_SPEC = pl.BlockSpec(memory_space=pltpu.HBM)
VMEM_SPEC = pl.BlockSpec(memory_space=pltpu.VMEM)


def _position():
    return lax.axis_index("x"), lax.axis_index("y"), lax.axis_index("c")


def _other_chips(x, y):
    return [(1 - x, y), (x, 1 - y), (1 - x, 1 - y)]


def _gather_shards(shards, whole):
    na, nw = len(shards), len(whole)
    nall = na + nw

    def body(*refs):
        gather = _GatherPlan(refs[:nall], refs[nall:2 * nall], refs[2 * nall:], na)
        gather.send()
        gather.forward()
        gather.finish()

    arrs = list(shards) + list(whole)
    outs = _call(
        body, name="gather_shards",
        in_specs=[HBM_SPEC] * nall, out_specs=[HBM_SPEC] * nall,
        out_shape=_gather_out_shapes(arrs), scratch_shapes=_gather_semaphores(na, nall),
    )(*arrs)
    return _place_own(outs, arrs)


def _gather_out_shapes(arrs):
    return [jax.ShapeDtypeStruct((N_CHIPS,) + s.shape, s.dtype) for s in arrs]


def _gather_semaphores(na, nall):
    return [pltpu.SemaphoreType.DMA((3 * nall,)), pltpu.SemaphoreType.DMA((3 * nall,)),
            pltpu.SemaphoreType.DMA((3 * na,)), pltpu.SemaphoreType.DMA((3 * na,))]


def _place_own(outs, arrs):
    if not arrs:
        return []
    chip = 2 * lax.axis_index("x") + lax.axis_index("y")
    return [lax.dynamic_update_slice(o, a[None], (chip,) + (0,) * a.ndim) for o, a in zip(outs, arrs)]


class _GatherPlan:
    def __init__(self, srcs, dsts, sems, na):
        ici_send, ici_recv, d2d_send, d2d_recv = sems
        x, y, c = _position()
        chip = 2 * x + y
        nall = len(srcs)

        def half(a, which):
            rows = srcs[a].shape[0] // 2
            return pl.ds(pl.multiple_of(which * rows, BF16_ROWS), rows)

        def copy(src, dst, send, recv, k, to):
            return pltpu.make_async_remote_copy(src_ref=src, dst_ref=dst, send_sem=send.at[k], recv_sem=recv.at[k],
                                                device_id=to, device_id_type=MESH)

        self.first, self.landed, self.passed, self.returned = [], [], [], []
        for j, (px, py) in enumerate(_other_chips(x, y)):
            theirs = 2 * px + py
            for a in range(nall):
                k = j * nall + a
                if a < na:
                    self.first.append(copy(srcs[a].at[half(a, c), :], dsts[a].at[chip, half(a, c), :],
                                           ici_send, ici_recv, k, (px, py, c)))
                    mine = dsts[a].at[theirs, half(a, c), :]
                    other = dsts[a].at[theirs, half(a, 1 - c), :]
                    self.landed.append(copy(mine, mine, ici_send, ici_recv, k, (px, py, c)))
                    self.passed.append(copy(mine, mine, d2d_send, d2d_recv, j * na + a, (x, y, 1 - c)))
                    self.returned.append(copy(other, other, d2d_send, d2d_recv, j * na + a, (x, y, 1 - c)))
                else:
                    self.first.append(copy(srcs[a], dsts[a].at[chip], ici_send, ici_recv, k, (px, py, c)))
                    got = dsts[a].at[theirs]
                    self.landed.append(copy(got, got, ici_send, ici_recv, k, (px, py, c)))
                    self.passed.append(None)

    def send(self):
        for cp in self.first:
            cp.start()

    def forward(self):
        for arrival, fwd in zip(self.landed, self.passed):
            arrival.wait_recv()
            if fwd is not None:
                fwd.start()

    def finish(self):
        for cp in self.returned:
            cp.wait_recv()
        for cp in self.first + [f for f in self.passed if f is not None]:
            cp.wait_send()


W_ROWS = 1568
G_ROWS = 1552
SHARD_ROWS = D_IN // N_CHIPS
WINDOW_STEP = 1536


def _assemble_w_in(cont):
    cb = COL_BLOCK
    half = WINDOW_STEP
    seam = BF16_ROWS

    def body(c_ref, wa_ref, wf_ref, wb_ref):
        x0 = c_ref[0].astype(F32)
        x1, x2, x3 = (pltpu.roll(c_ref[j].astype(F32), 2 * j, 0) for j in (1, 2, 3))
        wa = jnp.concatenate([x0[:half], x0[half:half + seam] + x1[:seam], x1[seam:half]], axis=0)
        wa_ref[...] = wa.astype(BF16)

        fl = x1[half:half + seam] + x2[:seam]
        row = lax.broadcasted_iota(jnp.int32, fl.shape, 0)
        wf_ref[:seam, :] = jnp.where(row < H, fl, 0.0).astype(BF16)
        wf_ref[seam:, :] = jnp.zeros((LANES - seam, cb), BF16)

        mid = x2[half:half + SUBLANES] + x3[:SUBLANES]
        wb = jnp.concatenate([x2[SUBLANES:half], mid, x3[SUBLANES:half + SUBLANES]], axis=0)
        wb_ref[...] = wb.astype(BF16)

    return _call(
        body, name="assemble_w_in", grid=(D // cb,),
        in_specs=[pl.BlockSpec((N_CHIPS, W_ROWS, cb), lambda i: (0, 0, i))],
        out_specs=[pl.BlockSpec((3 * D, cb), lambda i: (0, i)), pl.BlockSpec((LANES, cb), lambda i: (0, i)),
                   pl.BlockSpec((3 * D, cb), lambda i: (0, i))],
        out_shape=[jax.ShapeDtypeStruct((3 * D, D), BF16), jax.ShapeDtypeStruct((LANES, D), BF16),
                   jax.ShapeDtypeStruct((3 * D, D), BF16)],
        compiler_params=_cparams(("parallel",)),
    )(cont)


def _pair_exchange_windows(grad_t):
    half_g = G_ROWS // 2

    def body(g_ref, got, send_sems, recv_sems):
        x, y, c = _position()
        copies = []
        for j in range(N_CHIPS):
            rows = pl.ds(pl.multiple_of(j * WINDOW_STEP + (1 - c) * half_g, SUBLANES), half_g)
            copies.append(pltpu.make_async_remote_copy(
                src_ref=g_ref.at[rows, :], dst_ref=got.at[j], send_sem=send_sems.at[j], recv_sem=recv_sems.at[j],
                device_id=(x, y, 1 - c), device_id_type=MESH))
        for cp in copies:
            cp.start()
        for cp in copies:
            cp.wait()

    return _call(
        body, name="pair_exchange_w_in",
        in_specs=[HBM_SPEC], out_specs=HBM_SPEC,
        out_shape=jax.ShapeDtypeStruct((N_CHIPS, half_g, D), F32),
        scratch_shapes=[pltpu.SemaphoreType.DMA((N_CHIPS,)), pltpu.SemaphoreType.DMA((N_CHIPS,))],
    )(grad_t)


def _pair_sum(parts, gots, c):
    na = len(parts)

    def body(c_ref, *refs):
        for a in range(na):
            refs[2 * na + a][...] = (refs[a][...] + refs[na + a][...]).astype(BF16)

    mine = [pl.BlockSpec(g.shape, lambda i, c_ref: (0, c_ref[0], 0)) for g in gots]
    whole = [pl.BlockSpec(g.shape, lambda i, c_ref: (0, 0, 0)) for g in gots]
    grid_spec = pltpu.PrefetchScalarGridSpec(
        num_scalar_prefetch=1, grid=(1,), in_specs=mine + whole, out_specs=whole)
    return _call(
        body, name="pair_sum", grid_spec=grid_spec,
        out_shape=[jax.ShapeDtypeStruct(g.shape, BF16) for g in gots],
        compiler_params=_cparams(("arbitrary",), VMEM_BIG),
    )(c.reshape(1), *parts, *gots)


def _pair_sum_windows(grad_t, got, c):
    _, half, C = got.shape
    cb = SUM_BLOCK

    def body(c_ref, a_ref, b_ref, o_ref):
        o_ref[0] = (a_ref[...] + b_ref[0]).astype(BF16)

    def mine(j, i, c_ref):
        return ((j * (WINDOW_STEP // SUBLANES) + c_ref[0] * (half // SUBLANES)) * SUBLANES, i * cb)

    spec = pl.BlockSpec((1, half, cb), lambda j, i, c_ref: (j, 0, i))
    grid_spec = pltpu.PrefetchScalarGridSpec(
        num_scalar_prefetch=1, grid=(N_CHIPS, C // cb),
        in_specs=[pl.BlockSpec((pl.Element(half), pl.Element(cb)), mine), spec], out_specs=spec)
    return _call(
        body, name="pair_sum_w_in", grid_spec=grid_spec,
        out_shape=jax.ShapeDtypeStruct((N_CHIPS, half, C), BF16),
        compiler_params=_cparams(("parallel", "parallel")),
    )(c.reshape(1), grad_t, got)


def _chip_sum(own, got, chip, name):
    _, half, C = got.shape
    cb = min(C, SUM_BLOCK)

    def body(chip_ref, own_ref, g_ref, o_ref):
        for me in range(N_CHIPS):
            @pl.when(chip_ref[0] == me)
            def _(me=me):
                terms = [own_ref[0] if k == me else g_ref[k] for k in range(N_CHIPS)]
                acc = terms[0].astype(F32) + terms[1].astype(F32)
                acc = acc + terms[2].astype(F32)
                o_ref[...] = acc + terms[3].astype(F32)

    grid_spec = pltpu.PrefetchScalarGridSpec(
        num_scalar_prefetch=1, grid=(C // cb,),
        in_specs=[pl.BlockSpec((1, half, cb), lambda i, chip_ref: (chip_ref[0], 0, i)),
                  pl.BlockSpec((N_CHIPS, half, cb), lambda i, chip_ref: (0, 0, i))],
        out_specs=pl.BlockSpec((half, cb), lambda i, chip_ref: (0, i)))
    return _call(
        body, name=name, grid_spec=grid_spec,
        out_shape=jax.ShapeDtypeStruct((half, C), F32),
        compiler_params=_cparams(("parallel",)),
    )(chip.reshape(1), own, got)


def _final_exchange(halves, g):
    na = len(halves)
    rows = g.shape[0]
    per = rows // N_DEV

    def body(*refs):
        srcs, g_ref = refs[:na], refs[na]
        dsts, out_ref = refs[na + 1:2 * na + 1], refs[2 * na + 1]
        got_ref, s1, r1, s2, r2, swap_send, swap_recv = refs[2 * na + 2:]
        x, y, c = _position()
        swaps = [pltpu.make_async_remote_copy(
            src_ref=srcs[a], dst_ref=dsts[a], send_sem=swap_send.at[a], recv_sem=swap_recv.at[a],
            device_id=(x, y, 1 - c), device_id_type=MESH) for a in range(na)]
        for cp in swaps:
            cp.start()
        me = 4 * x + 2 * y + c
        mine = pl.ds(pl.multiple_of(me * per, SUBLANES), per)
        peers = []
        for j in range(1, N_DEV):
            px = 1 - x if j & 4 else x
            py = 1 - y if j & 2 else y
            pc = 1 - c if j & 1 else c
            peers.append((px, py, pc))

        first = []
        for j, (px, py, pc) in enumerate(peers):
            theirs = pl.ds(pl.multiple_of((4 * px + 2 * py + pc) * per, SUBLANES), per)
            first.append(pltpu.make_async_remote_copy(
                src_ref=g_ref.at[theirs, :], dst_ref=got_ref.at[me], send_sem=s1.at[j], recv_sem=r1.at[j],
                device_id=(px, py, pc), device_id_type=MESH))
        for cp in first:
            cp.start()
        got_ref[me] = g_ref[mine, :]
        for cp in first:
            cp.wait()
        total = got_ref[0]
        for d in range(1, N_DEV):
            total = total + got_ref[d]
        out_ref[mine, :] = total

        second = []
        for j, peer in enumerate(peers):
            second.append(pltpu.make_async_remote_copy(
                src_ref=out_ref.at[mine, :], dst_ref=out_ref.at[mine, :], send_sem=s2.at[j], recv_sem=r2.at[j],
                device_id=peer, device_id_type=MESH))
        for cp in second:
            cp.start()
        for cp in second + swaps:
            cp.wait()

    sems = pltpu.SemaphoreType.DMA((N_DEV - 1,))
    swap_sems = pltpu.SemaphoreType.DMA((na,))
    outs = _call(
        body, name="final_exchange", in_hbm=False,
        in_specs=[HBM_SPEC] * na + [VMEM_SPEC], out_specs=[HBM_SPEC] * na + [VMEM_SPEC],
        out_shape=[jax.ShapeDtypeStruct(s.shape, s.dtype) for s in halves] + [jax.ShapeDtypeStruct(g.shape, F32)],
        scratch_shapes=[pltpu.VMEM((N_DEV, per, LANES), F32), sems, sems, sems, sems, swap_sems, swap_sems],
    )(*halves, g)
    return outs[:na], outs[na]


def _adamw_math(g, w, m, v):
    m2 = ADAM_B1 * m + (1.0 - ADAM_B1) * g
    v2 = ADAM_B2 * v + (1.0 - ADAM_B2) * (g * g)
    m_hat = m2 / (1.0 - ADAM_B1 ** ADAM_STEP)
    v_hat = v2 / (1.0 - ADAM_B2 ** ADAM_STEP)
    delta = (-ADAM_LR) * (m_hat / (jnp.sqrt(v_hat) + ADAM_EPS) + ADAM_WD * w)
    return delta, m2, v2


ADAMW_BLOCK_BYTES = 2 << 20


def _adamw_big(g, w, m, v, name):
    R, C = g.shape
    bc = min(C, max(LANES, ADAMW_BLOCK_BYTES // (4 * R) // LANES * LANES))

    def body(g_ref, w_ref, m_ref, v_ref, d_ref, m2_ref, v2_ref):
        d_ref[...], m2_ref[...], v2_ref[...] = _adamw_math(g_ref[...], w_ref[...], m_ref[...], v_ref[...])

    spec = pl.BlockSpec((R, bc), lambda j: (0, j))
    out = jax.ShapeDtypeStruct((R, C), F32)
    return _call(
        body, name=name, grid=(C // bc,),
        in_specs=[spec] * 4, out_specs=[spec] * 3, out_shape=[out] * 3,
        compiler_params=_cparams(("parallel",)),
    )(g, w, m, v)


def _adamw_small(gs, ws, ms, vs):
    n = len(gs)

    def body(*refs):
        for a in range(n):
            g_ref, w_ref, m_ref, v_ref = (refs[k * n + a] for k in range(4))
            d_ref, m2_ref, v2_ref = (refs[(4 + k) * n + a] for k in range(3))
            d_ref[...], m2_ref[...], v2_ref[...] = _adamw_math(g_ref[...], w_ref[...], m_ref[...], v_ref[...])

    outs = [jax.ShapeDtypeStruct(w.shape, F32) for w in ws]
    specs = [_const_spec(w.shape) for w in ws]
    return _call(
        body, name="adamw_small", grid=(1,),
        in_specs=specs * 4, out_specs=specs * 3, out_shape=outs * 3,
    )(*gs, *ws, *ms, *vs)


def _late_weights(st_out, st_ple, st_gate, st_conv):
    return st_out.reshape(DMIX, D), _from_chip_cols(st_ple), st_gate.reshape(D, D), _from_chip_cols(st_conv)


def _local_step(x, p, tgt, w_a, w_f, w_b, late, b_f, pre_gain, post_gain, conv_b,
                w_rgate, b_rgate, w_igate, b_igate, lam, gain_a, gain_l, ple_gain, b_gate,
                gather_late=False, early_reduce=None, w_in_reduce=None):
    b_f_pad = jnp.pad(b_f, ((0, 0), (0, LANES - H)))
    w_r = w_rgate.astype(BF16)
    w_i = w_igate.astype(BF16)

    xn, q_aug, k_aug, v_aug, g_attn, x_lru, g_lru, flb, vt_aug = _in_proj(x, pre_gain, w_a, w_f, w_b, b_f_pad)
    if gather_late:
        o, qx, stacks = _attn_fwd(q_aug, k_aug, vt_aug, late[:3], late[3:])
        late = _late_weights(*stacks)
    else:
        o, qx, _ = _attn_fwd(q_aug, k_aug, vt_aug)
    w_out_b, w_ple_b, w_gate_b, conv_w = late
    ycat, xc, h = _branches_fwd(o, g_attn, x_lru, g_lru, gain_a, gain_l, conv_w, conv_b, w_r, b_rgate, w_i, b_igate,
                                lam)
    dh1, dycat, dmix, h1b, dgp, pb, dpe, acc_t = _tail(ycat, x, p, tgt, w_out_b, post_gain, w_ple_b, ple_gain,
                                                       w_gate_b, b_gate)
    do_aug, dg_attn, dg_lru, dh, acc_b, gw_out, gw_gate, gw_ple = _branches_bwd(
        dycat, o, g_attn, h, g_lru, gain_a, gain_l, ycat, dmix, h1b, dgp, pb, dpe)
    late_grads = [gw_out, gw_ple, gw_gate]
    if early_reduce is None:
        dx_lru, gw_r, gw_i, acc_l, _ = _lru_bwd(dh, h, xc, x_lru, conv_w, w_r, b_rgate, w_i, b_igate, lam)
    else:
        parts = [gw_out.reshape(N_CHIPS, DMIX // N_CHIPS, D), _by_chip_cols(gw_ple),
                 gw_gate.reshape(N_CHIPS, D // N_CHIPS, D)]
        dx_lru, gw_r, gw_i, acc_l, got = _lru_bwd(dh, h, xc, x_lru, conv_w, w_r, b_rgate, w_i, b_igate, lam, parts)
        sent = _pair_sum(parts, got, early_reduce)
    if early_reduce is None:
        dq, dk, dv, dc_heads, _ = _attn_bwd(q_aug, qx, k_aug, v_aug, do_aug)
    else:
        dq, dk, dv, dc_heads, received = _attn_bwd(q_aug, qx, k_aug, v_aug, do_aug, sent)
        late_grads = list(zip(sent, received))
    dfl, acc_f = _fgate_bwd(dc_heads, flb)
    dz = (dq, dk, dv, dg_attn, dx_lru, dg_lru)
    grad_t = _dw_in_t(dz, dfl, xn)
    if w_in_reduce is None:
        grad_x, acc_x, _ = _dx(dz, dfl, w_a, w_f, w_b, x, pre_gain, dh1)
    else:
        sent = w_in_reduce(grad_t)
        grad_x, acc_x, (received,) = _dx(dz, dfl, w_a, w_f, w_b, x, pre_gain, dh1, [sent])
        grad_t = (sent, received)

    grads = dict(
        w_in_t=grad_t,
        w_out=late_grads[0],
        w_ple=late_grads[1],
        w_ple_gate=late_grads[2],
        w_rgate=gw_r,
        w_igate=gw_i,
        b_f=acc_f[0:1, :H],
        pre_gain=acc_x[0:1],
        post_gain=acc_t[0:1],
        conv_w=acc_l[0:4],
        conv_b=acc_l[4:5],
        b_rgate=acc_l[5:6],
        b_igate=acc_l[6:7],
        lru_lambda=acc_l[7:8],
        attn_out_gain=acc_b[0:1],
        lru_out_gain=acc_b[1:2],
        ple_gain=acc_t[1:2],
        b_ple_gate=acc_t[2:3],
    )
    loss = jnp.sum(acc_t[3])
    return loss, grad_x, grads


SMALL_ROWS = ["b_f", "pre_gain", "post_gain", "conv_w", "conv_b", "b_rgate", "b_igate", "lru_lambda",
              "attn_out_gain", "lru_out_gain", "ple_gain", "b_ple_gate"]
WEIGHTS = ["w_in", "b_f", "pre_gain", "post_gain", "conv_w", "conv_b", "w_rgate", "b_rgate", "w_igate", "b_igate",
           "lru_lambda", "attn_out_gain", "lru_out_gain", "w_out", "w_ple", "ple_gain", "w_ple_gate", "b_ple_gate"]
SHARDED = ["w_in", "w_out", "w_ple", "w_ple_gate"]


def _by_chip_cols(g):
    r, cols = g.shape
    return g.reshape(r, N_CHIPS, cols // N_CHIPS).transpose(1, 0, 2)


def _from_chip_cols(s):
    n, r, cols = s.shape
    return s.transpose(1, 0, 2).reshape(r, n * cols)


def kernel(x, p, w_in, b_f, pre_gain, post_gain, conv_w, conv_b, w_rgate, b_rgate, w_igate, b_igate, lru_lambda, attn_out_gain, lru_out_gain, w_out, w_ple, ple_gain, w_ple_gate, b_ple_gate, loss_target, m_w_in, m_b_f, m_pre_gain, m_post_gain, m_conv_w, m_conv_b, m_w_rgate, m_b_rgate, m_w_igate, m_b_igate, m_lru_lambda, m_attn_out_gain, m_lru_out_gain, m_w_out, m_w_ple, m_ple_gain, m_w_ple_gate, m_b_ple_gate, v_w_in, v_b_f, v_pre_gain, v_post_gain, v_conv_w, v_conv_b, v_w_rgate, v_b_rgate, v_w_igate, v_b_igate, v_lru_lambda, v_attn_out_gain, v_lru_out_gain, v_w_out, v_w_ple, v_ple_gain, v_w_ple_gate, v_b_ple_gate):
    w = dict(w_in=w_in, b_f=b_f, pre_gain=pre_gain, post_gain=post_gain, conv_w=conv_w, conv_b=conv_b,
             w_rgate=w_rgate, b_rgate=b_rgate, w_igate=w_igate, b_igate=b_igate, lru_lambda=lru_lambda,
             attn_out_gain=attn_out_gain, lru_out_gain=lru_out_gain, w_out=w_out, w_ple=w_ple, ple_gain=ple_gain,
             w_ple_gate=w_ple_gate, b_ple_gate=b_ple_gate)
    m = dict(w_in=m_w_in, b_f=m_b_f, pre_gain=m_pre_gain, post_gain=m_post_gain, conv_w=m_conv_w, conv_b=m_conv_b,
             w_rgate=m_w_rgate, b_rgate=m_b_rgate, w_igate=m_w_igate, b_igate=m_b_igate, lru_lambda=m_lru_lambda,
             attn_out_gain=m_attn_out_gain, lru_out_gain=m_lru_out_gain, w_out=m_w_out, w_ple=m_w_ple,
             ple_gain=m_ple_gain, w_ple_gate=m_w_ple_gate, b_ple_gate=m_b_ple_gate)
    v = dict(w_in=v_w_in, b_f=v_b_f, pre_gain=v_pre_gain, post_gain=v_post_gain, conv_w=v_conv_w, conv_b=v_conv_b,
             w_rgate=v_w_rgate, b_rgate=v_b_rgate, w_igate=v_w_igate, b_igate=v_b_igate, lru_lambda=v_lru_lambda,
             attn_out_gain=v_attn_out_gain, lru_out_gain=v_lru_out_gain, w_out=v_w_out, w_ple=v_w_ple,
             ple_gain=v_ple_gain, w_ple_gate=v_w_ple_gate, b_ple_gate=v_b_ple_gate)
    xi, yi, ci = _position()
    chip = 2 * xi + yi

    w_in_t, m_in_t, v_in_t = (jnp.swapaxes(t[0], 0, 1) for t in (w_in, m_w_in, v_w_in))
    window = jnp.pad(w_in_t.astype(BF16), ((0, W_ROWS - SHARD_ROWS), (0, 0)))

    (st_in,) = _gather_shards([window], [])
    w_a, w_f, w_b = _assemble_w_in(st_in)
    late_shards = (w_out[0].astype(BF16), w_ple[0].astype(BF16), w_ple_gate[0].astype(BF16), conv_w[0])

    loss, grad_x, g = _local_step(
        x[0], p[0, 0], loss_target[0], w_a, w_f, w_b, late_shards, b_f, pre_gain, post_gain,
        conv_b, w_rgate[0], b_rgate, w_igate[0], b_igate, lru_lambda, attn_out_gain, lru_out_gain, ple_gain,
        b_ple_gate, gather_late=True, early_reduce=ci,
        w_in_reduce=lambda grad_t: _pair_sum_windows(grad_t, _pair_exchange_windows(grad_t), ci))

    sums = [g["w_in_t"][0]] + [g[n][0] for n in SHARDED[1:]]
    recv = [g["w_in_t"][1]] + [g[n][1] for n in SHARDED[1:]]
    halves = [_chip_sum(sums[a], recv[a], chip, "chip_sum_%d" % a) for a in range(4)]

    rows = [jnp.pad(g["b_f"], ((0, 0), (0, D - H)))] + [g[n] for n in SMALL_ROWS[1:]]
    rows.append(jnp.pad(loss.reshape(1, 1), ((0, 0), (0, D - 1))))
    packed = jnp.concatenate([g["w_rgate"].reshape(NB * LANES, LANES), g["w_igate"].reshape(NB * LANES, LANES),
                              jnp.concatenate(rows, axis=0).reshape(LANES, LANES)], axis=0)
    theirs, summed = _final_exchange(halves, packed)
    full = [jnp.concatenate([jnp.where(ci == 0, a, b), jnp.where(ci == 0, b, a)], axis=0)
            for a, b in zip(halves, theirs)]
    red = dict(zip(SHARDED, full))
    red["w_in"] = lax.dynamic_slice_in_dim(red["w_in"], 2 * chip, SHARD_ROWS, axis=0)
    red["w_rgate"] = summed[:D].reshape(1, NB, LANES, LANES)
    red["w_igate"] = summed[D:2 * D].reshape(1, NB, LANES, LANES)
    vec = summed[2 * D:].reshape(16, D)
    loss = vec[15, 0]
    r0 = 0
    for n in SMALL_ROWS:
        nr = 4 if n == "conv_w" else 1
        red[n] = vec[r0:r0 + nr]
        r0 += nr
    red["b_f"] = red["b_f"][:, :H]
    red["conv_w"] = lax.dynamic_slice_in_dim(red["conv_w"], chip * (D // N_CHIPS), D // N_CHIPS, axis=1)[None]

    delta, new_m, new_v = {}, {}, {}
    outs_in = _adamw_big(red["w_in"], w_in_t, m_in_t, v_in_t, "adamw_w_in")
    delta["w_in"], new_m["w_in"], new_v["w_in"] = (jnp.swapaxes(t, 0, 1)[None] for t in outs_in)
    red["w_in"] = jnp.swapaxes(red["w_in"], 0, 1)[None]
    for n in SHARDED[1:]:
        delta[n], new_m[n], new_v[n] = (t[None] for t in _adamw_big(red[n], w[n][0], m[n][0], v[n][0], "adamw_" + n))
        red[n] = red[n][None]
    small = [n for n in WEIGHTS if n not in SHARDED]
    outs = _adamw_small([red[n] for n in small], [w[n] for n in small], [m[n] for n in small],
                        [v[n] for n in small])
    ns = len(small)
    for a, n in enumerate(small):
        delta[n], new_m[n], new_v[n] = outs[a], outs[ns + a], outs[2 * ns + a]

    return (loss, grad_x[None], *[red[n] for n in WEIGHTS], *[delta[n] for n in WEIGHTS],
            *[new_m[n] for n in WEIGHTS], *[new_v[n] for n in WEIGHTS])
```

```python
import jax
import jax.numpy as jnp
import numpy as np
from jax import lax
from jax.experimental import pallas as pl
from jax.experimental.pallas import tpu as pltpu

F32 = jnp.float32
BF16 = jnp.bfloat16

D = 1024
H = 8
DH = 128
NB = 8
DPLE = 256
DMIX = 2 * D
D_IN = 4 * D + H + 2 * D
FL0 = 3 * D
RMS_EPS = 1e-6
LRU_C = 8.0
NEG = -1e30
LANES = 128
SUBLANES = 8
BF16_ROWS = 16
COL_BLOCK = 256
SUM_BLOCK = 512
DW_TOKENS = 2048

ADAM_LR = 0.001
ADAM_B1 = 0.9
ADAM_B2 = 0.999
ADAM_EPS = 1e-08
ADAM_WD = 0.01
ADAM_STEP = 10

TM = 256
TA = 512
TA_FWD = 1024
FWD_HEADS = 8
BWD_HEADS = 2
VMEM_BIG = 56 * 1024 * 1024
VMEM_MID = 40 * 1024 * 1024

MESH = pl.DeviceIdType.MESH
N_CHIPS = 4
N_DEV = 8


def _call(body, *, out_shape, in_hbm=True, **kwargs):
    if not in_hbm:
        return pl.pallas_call(body, out_shape=out_shape, **kwargs)

    def pin(shape):
        return pltpu.HBM(shape.shape, shape.dtype) if isinstance(shape, jax.ShapeDtypeStruct) else shape

    fn = pl.pallas_call(body, out_shape=jax.tree.map(pin, out_shape), **kwargs)

    def run(*args):
        return fn(*[a if a.dtype == jnp.int32 else pltpu.with_memory_space_constraint(a, pltpu.HBM) for a in args])

    return run


def _cparams(sem, vmem=VMEM_MID):
    return pltpu.CompilerParams(dimension_semantics=sem, vmem_limit_bytes=vmem)


def _sigmoid(x):
    return 0.5 * jnp.tanh(0.5 * x) + 0.5


def _rstd(x):
    return lax.rsqrt(jnp.mean(x * x, axis=-1, keepdims=True) + RMS_EPS)


def _rms_bwd(t, xhat, rstd):
    return rstd * (t - xhat * jnp.mean(t * xhat, axis=-1, keepdims=True))


def _dot(a, b):
    return jnp.dot(a, b, preferred_element_type=F32)


def _dot_nt(a, b):
    return lax.dot_general(a, b, (((1,), (1,)), ((), ())), preferred_element_type=F32)


def _dot_tn(a, b):
    return lax.dot_general(a, b, (((0,), (0,)), ((), ())), preferred_element_type=F32)


def _dot_exact(a, b):
    return jnp.dot(a, b, preferred_element_type=F32, precision=lax.Precision.HIGHEST)


def _shift_down(x, j, halo):
    rolled = pltpu.roll(x, j, 0)
    row = lax.broadcasted_iota(jnp.int32, halo.shape, 0)
    top = jnp.where(row < j, pltpu.roll(halo, j, 0), rolled[:SUBLANES])
    return jnp.concatenate([top, rolled[SUBLANES:]], axis=0)


def _shift_up(x, j, nxt):
    tm = x.shape[0]
    rolled = pltpu.roll(x, tm - j, 0)
    row = lax.broadcasted_iota(jnp.int32, nxt.shape, 0)
    bot = jnp.where(row >= SUBLANES - j, pltpu.roll(nxt, SUBLANES - j, 0), rolled[tm - SUBLANES:])
    return jnp.concatenate([rolled[:tm - SUBLANES], bot], axis=0)


def _scan_fwd_into(a, u, carry, h_ref):
    tm, width = a.shape
    groups = (tm // SUBLANES, SUBLANES, width)
    a, u = a.reshape(groups), u.reshape(groups)
    sub = lax.broadcasted_iota(jnp.int32, groups, 1)
    d = 1
    while d < SUBLANES:
        keep = sub >= d
        a_s = jnp.where(keep, pltpu.roll(a, d, 1), 1.0)
        u_s = jnp.where(keep, pltpu.roll(u, d, 1), 0.0)
        u = u + a * u_s
        a = a * a_s
        d *= 2
    a, u = a.reshape(tm, width), u.reshape(tm, width)
    for g in range(tm // SUBLANES):
        rows = slice(g * SUBLANES, (g + 1) * SUBLANES)
        h_ref[rows, :] = u[rows] + a[rows] * carry
        carry = h_ref[(g + 1) * SUBLANES - 1:(g + 1) * SUBLANES, :]
    return carry


def _scan_bwd_into(b, u, g_ref):
    tm, width = b.shape
    groups = (tm // SUBLANES, SUBLANES, width)
    b, u = b.reshape(groups), u.reshape(groups)
    sub = lax.broadcasted_iota(jnp.int32, groups, 1)
    d = 1
    while d < SUBLANES:
        keep = sub < SUBLANES - d
        b_s = jnp.where(keep, pltpu.roll(b, SUBLANES - d, 1), 1.0)
        u_s = jnp.where(keep, pltpu.roll(u, SUBLANES - d, 1), 0.0)
        u = u + b * u_s
        b = b * b_s
        d *= 2
    b, u = b.reshape(tm, width), u.reshape(tm, width)
    nxt = jnp.zeros((1, width), F32)
    for g in reversed(range(tm // SUBLANES)):
        rows = slice(g * SUBLANES, (g + 1) * SUBLANES)
        g_ref[rows, :] = u[rows] + b[rows] * nxt
        nxt = g_ref[g * SUBLANES:g * SUBLANES + 1, :]


def _gate_pre(xc, w_ref):
    outs = []
    for n in range(NB):
        outs.append(_dot(xc[:, n * LANES:(n + 1) * LANES].astype(BF16), w_ref[n]))
    return jnp.concatenate(outs, axis=1)


def _gate_pre_t(d, w_ref):
    outs = []
    for n in range(NB):
        outs.append(_dot_nt(d[:, n * LANES:(n + 1) * LANES].astype(BF16), w_ref[n]))
    return jnp.concatenate(outs, axis=1)


def _softplus_neg(lam):
    return jnp.maximum(-lam, 0.0) + jnp.log(1.0 + jnp.exp(-jnp.abs(lam)))


def _row_spec(tm, width):
    return pl.BlockSpec((tm, width), lambda i: (i, 0))


def _const_spec(shape):
    nd = len(shape)
    return pl.BlockSpec(shape, lambda *_: (0,) * nd)


def _weight_spec(shape):
    nd = len(shape)
    return pl.BlockSpec(shape, lambda *_: (0,) * nd, pipeline_mode=pl.Buffered(1))


AUG = 2 * DH
LOG2E = 1.4426950408889634
LN2 = 0.6931471805599453
Q_SCALE = DH ** -0.5 * LOG2E


def _split3(x):
    hi = x.astype(BF16)
    r1 = x - hi.astype(F32)
    mid = r1.astype(BF16)
    lo = (r1 - mid.astype(F32)).astype(BF16)
    return hi, mid, lo


def _extras(col, ones_from):
    t = col.shape[0]
    hi, mid, lo = _split3(jnp.broadcast_to(col, (t, LANES)))
    lane = lax.broadcasted_iota(jnp.int32, (t, LANES), 1)
    rest = jnp.zeros((t, LANES), BF16)
    if ones_from is not None:
        rest = jnp.where((lane >= ones_from) & (lane < ones_from + 3), 1.0, 0.0).astype(BF16)
    return jnp.where(lane == 0, hi, jnp.where(lane == 1, mid, jnp.where(lane == 2, lo, rest)))


def _selectors():
    sel_q = np.zeros((3 * LANES, H * LANES), np.float32)
    sel_k = np.zeros((3 * LANES, H * LANES), np.float32)
    for hd in range(H):
        for piece in range(3):
            sel_q[piece * LANES + hd, hd * LANES + piece] = 1.0
            sel_k[piece * LANES + hd, hd * LANES + 3 + piece] = -1.0
    return jnp.asarray(sel_q, BF16), jnp.asarray(sel_k, BF16)


def _in_proj(x, pre_gain, w_a, w_f, w_b, b_f_pad):
    T = x.shape[0]
    tm = TM
    sel_q, sel_k = _selectors()

    def body(x_ref, g_ref, wa_ref, wf_ref, wb_ref, bf_ref, sq_ref, sk_ref,
             xn_ref, qa_ref, ka_ref, va_ref, ga_ref, xl_ref, gl_ref, flb_ref, vt_ref, c_s, carry):
        @pl.when(pl.program_id(0) == 0)
        def _():
            carry[...] = jnp.zeros_like(carry)

        xv = x_ref[...]
        xn = (xv * _rstd(xv) * g_ref[...]).astype(BF16)
        xn_ref[...] = xn
        for s, o_ref in enumerate((ga_ref, xl_ref, gl_ref)):
            o_ref[...] = _dot_nt(xn, wb_ref[s * D:(s + 1) * D, :]).astype(o_ref.dtype)
        flb = _dot_nt(xn, wf_ref[...]) + bf_ref[...]
        flb_ref[...] = flb
        lane = lax.broadcasted_iota(jnp.int32, flb.shape, 1)
        ls = jnp.where(lane < H, jnp.minimum(flb, 0.0) - jnp.log(1.0 + jnp.exp(-jnp.abs(flb))), 0.0)
        r = lax.broadcasted_iota(jnp.int32, (tm, tm), 0)
        c = lax.broadcasted_iota(jnp.int32, (tm, tm), 1)
        cs = _dot_exact((c <= r).astype(F32), ls) + carry[...]
        c_s[...] = cs
        carry[...] = c_s[tm - 1:tm, :]

        pieces = jnp.concatenate(_split3(cs * LOG2E), axis=1)
        ones_q = jnp.where((lane >= 3) & (lane < 6), 1.0, 0.0)
        ones_k = jnp.where(lane < 3, 1.0, 0.0)
        zq = _dot_nt(xn, wa_ref[0:D, :]) * Q_SCALE
        zk = _dot_nt(xn, wa_ref[D:2 * D, :])
        zv = _dot_nt(xn, wa_ref[2 * D:3 * D, :])
        ex_q = _dot(pieces, sq_ref[...])
        ex_k = _dot(pieces, sk_ref[...])
        for hd in range(H):
            head = slice(hd * DH, (hd + 1) * DH)
            lo, hi = hd * AUG, hd * AUG + DH
            qa_ref[:, lo:hi] = zq[:, head].astype(BF16)
            qa_ref[:, hi:hi + DH] = (ex_q[:, head] + ones_q).astype(BF16)
            ka_ref[:, lo:hi] = zk[:, head].astype(BF16)
            ka_ref[:, hi:hi + DH] = (ex_k[:, head] + ones_k).astype(BF16)
            va_ref[:, lo:hi] = zv[:, head].astype(BF16)
            va_ref[:, hi:hi + DH] = ones_k.astype(BF16)
            vt_ref[lo:hi, :] = jnp.transpose(zv[:, head]).astype(BF16)
            vt_ref[hi:hi + DH, :] = jnp.where(lax.broadcasted_iota(jnp.int32, (DH, tm), 0) < 3, 1.0, 0.0).astype(BF16)

    bf = jax.ShapeDtypeStruct((T, D), BF16)
    aug = jax.ShapeDtypeStruct((T, H * AUG), BF16)
    f32 = jax.ShapeDtypeStruct((T, D), F32)
    sel_spec = _const_spec((3 * LANES, H * LANES))
    return _call(
        body, name="in_proj", grid=(T // tm,),
        in_specs=[_row_spec(tm, D), _const_spec((1, D)), _const_spec((3 * D, D)), _const_spec((LANES, D)),
                  _const_spec((3 * D, D)), _const_spec((1, LANES)), sel_spec, sel_spec],
        out_specs=[_row_spec(tm, D)] + [_row_spec(tm, H * AUG)] * 3 + [_row_spec(tm, D)] * 3 + [_row_spec(tm, LANES)]
        + [pl.BlockSpec((H * AUG, tm), lambda i: (0, i))],
        out_shape=[bf, aug, aug, aug, f32, f32, f32, jax.ShapeDtypeStruct((T, LANES), F32),
                   jax.ShapeDtypeStruct((H * AUG, T), BF16)],
        scratch_shapes=[pltpu.VMEM((tm, LANES), F32), pltpu.VMEM((1, LANES), F32)],
        compiler_params=_cparams(("arbitrary",), VMEM_BIG),
    )(x, pre_gain, w_a, w_f, w_b, b_f_pad, sel_q, sel_k)


def _causal_pairs(n, q_major):
    if q_major:
        pairs = [(qi, ki) for qi in range(n) for ki in range(qi + 1)]
    else:
        pairs = [(ki, qi) for ki in range(n) for qi in range(ki, n)]
    return (jnp.asarray([a for a, _ in pairs], jnp.int32), jnp.asarray([b for _, b in pairs], jnp.int32))


def _attn_fwd(q_aug, k_aug, vt_aug, shards=(), whole=()):
    T = q_aug.shape[0]
    t = min(T, TA_FWD)
    n = T // t
    hp = FWD_HEADS
    heads = range(hp)
    qi_tab, ki_tab = _causal_pairs(n, q_major=True)
    na, nall = len(shards), len(shards) + len(whole)
    n_h, n_j = H // hp, qi_tab.shape[0]

    def body(qi_ref, ki_ref, q_ref, k_ref, vt_ref, *rest):
        srcs, rest = rest[:nall], rest[nall:]
        o_ref, qx_ref = rest[:2]
        dsts, rest = rest[2:2 + nall], rest[2 + nall:]
        m_s, acc_s = rest[:2]
        h = pl.program_id(0)
        j = pl.program_id(1)
        qi = qi_ref[j]
        ki = ki_ref[j]

        if nall:
            gather = _GatherPlan(srcs, dsts, rest[2:], na)
            step = h * n_j + j
            pl.when(step == 0)(gather.send)
            pl.when(step == n_h * n_j // 2)(gather.forward)
            pl.when(step == n_h * n_j - 1)(gather.finish)

        @pl.when(ki == 0)
        def _():
            m_s[...] = jnp.full(m_s.shape, NEG, F32)
            acc_s[...] = jnp.zeros_like(acc_s)

        def step(on_diagonal):
            cols = [slice(a * AUG, (a + 1) * AUG) for a in heads]
            if on_diagonal:
                krow = lax.broadcasted_iota(jnp.int32, (t, t), 0)
                qcol = lax.broadcasted_iota(jnp.int32, (t, t), 1)
            def logits(a):
                st = _dot_nt(k_ref[:, cols[a]], q_ref[:, cols[a]])
                return jnp.where(krow <= qcol, st, NEG) if on_diagonal else st

            st_next = logits(0)
            for a in heads:
                st = st_next
                if a + 1 < hp:
                    st_next = logits(a + 1)
                m_prev = m_s[a]
                m_new = jnp.maximum(m_prev, jnp.max(st, axis=0, keepdims=True))
                pt = jnp.exp2(st - m_new).astype(BF16)
                acc_s[a] = jnp.exp2(m_prev - m_new) * acc_s[a] + _dot(vt_ref[cols[a], :], pt)
                m_s[a] = m_new

        @pl.when(ki < qi)
        def _():
            step(False)

        @pl.when(ki == qi)
        def _():
            step(True)
            piece = lax.broadcasted_iota(jnp.int32, (DH, t), 0)
            for a in heads:
                l = acc_s[a, DH:DH + 1, :]
                ex = jnp.transpose(q_ref[:, a * AUG + DH:(a + 1) * AUG].astype(F32))
                c2 = jnp.sum(jnp.where(piece < 3, ex, 0.0), axis=0, keepdims=True)
                hi, mid, lo = _split3(jnp.broadcast_to(c2 - (m_s[a] + jnp.log(l) * LOG2E), (DH, t)))
                ones = jnp.where((piece >= 3) & (piece < 6), 1.0, 0.0).astype(BF16)
                ex_t = jnp.where(piece == 0, hi, jnp.where(piece == 1, mid, jnp.where(piece == 2, lo, ones)))
                o_ref[:, a * DH:(a + 1) * DH] = jnp.transpose(acc_s[a, :DH, :] / l)
                qx_ref[:, a * DH:(a + 1) * DH] = jnp.transpose(ex_t.astype(F32)).astype(BF16)

    q_spec = pl.BlockSpec((t, hp * AUG), lambda h, j, qi_ref, ki_ref: (qi_ref[j], h))
    k_spec = pl.BlockSpec((t, hp * AUG), lambda h, j, qi_ref, ki_ref: (ki_ref[j], h))
    vt_spec = pl.BlockSpec((hp * AUG, t), lambda h, j, qi_ref, ki_ref: (h, ki_ref[j]))
    out_spec = pl.BlockSpec((t, hp * DH), lambda h, j, qi_ref, ki_ref: (qi_ref[j], h))
    arrs = list(shards) + list(whole)
    grid_spec = pltpu.PrefetchScalarGridSpec(
        num_scalar_prefetch=2, grid=(n_h, n_j),
        in_specs=[q_spec, k_spec, vt_spec] + [HBM_SPEC] * nall, out_specs=[out_spec, out_spec] + [HBM_SPEC] * nall,
        scratch_shapes=[pltpu.VMEM((hp, 1, t), F32), pltpu.VMEM((hp, AUG, t), F32)]
        + (_gather_semaphores(na, nall) if nall else []))
    outs = _call(
        body, name="attn_fwd", grid_spec=grid_spec,
        out_shape=[jax.ShapeDtypeStruct((T, D), F32), jax.ShapeDtypeStruct((T, D), BF16)] + _gather_out_shapes(arrs),
        compiler_params=_cparams(("arbitrary", "arbitrary"), VMEM_BIG),
    )(qi_tab, ki_tab, q_aug, k_aug, vt_aug, *arrs)
    return outs[0], outs[1], _place_own(outs[2:], arrs)


def _sigmoid_small(x):
    e = jnp.exp(x)
    return jnp.where(x < -8.0, e - e * e, _sigmoid(x))


def _lru_gates(xc, wr_ref, br_ref, wi_ref, bi_ref, lam_ref):
    r = _sigmoid_small(_gate_pre(xc, wr_ref) + br_ref[...])
    ig = _sigmoid(_gate_pre(xc, wi_ref) + bi_ref[...])
    sp = _softplus_neg(lam_ref[...])
    la = (-LRU_C) * r * sp
    a = jnp.exp(la)
    y = -jnp.tanh(la) * (a * a + 1.0)
    return r, ig, sp, a, jnp.sqrt(y), lax.rsqrt(y)


def _branches_fwd(o, g_attn, x_lru, g_lru, gain_a, gain_l, conv_w, conv_b, w_r, b_r, w_i, b_i, lam):
    T = o.shape[0]
    tm = TM

    def body(o_ref, ga_ref, xl_ref, gl_ref, gna_ref, gnl_ref, cw_ref, cb_ref, wr_ref, br_ref, wi_ref, bi_ref,
             lam_ref, ycat_ref, xc_ref, h_ref, halo_s, hc_s):
        @pl.when(pl.program_id(0) == 0)
        def _():
            halo_s[...] = jnp.zeros_like(halo_s)
            hc_s[...] = jnp.zeros_like(hc_s)

        ov = o_ref[...]
        ga = ga_ref[...]
        ya = ov * _rstd(ov) * gna_ref[...] * (ga * _sigmoid(ga))
        ycat_ref[:, :D] = ya.astype(BF16)

        xl = xl_ref[...]
        halo = halo_s[...]
        xc = xl * cw_ref[3:4, :] + cb_ref[...]
        for j in range(3):
            xc = xc + _shift_down(xl, 3 - j, halo) * cw_ref[j:j + 1, :]
        halo_s[...] = xl_ref[tm - SUBLANES:tm, :]
        xc_ref[...] = xc

        _, ig, _, a, sq, _ = _lru_gates(xc, wr_ref, br_ref, wi_ref, bi_ref, lam_ref)
        u = sq * (ig * xc)
        hc_s[...] = _scan_fwd_into(a, u, hc_s[...], h_ref)
        hh = h_ref[...]

        gl = gl_ref[...]
        yl = hh * _rstd(hh) * gnl_ref[...] * (gl * _sigmoid(gl))
        ycat_ref[:, D:] = yl.astype(BF16)

    vec = _const_spec((1, D))
    wspec = _const_spec((NB, LANES, LANES))
    return _call(
        body, name="branches_fwd", grid=(T // tm,),
        in_specs=[_row_spec(tm, D)] * 4 + [vec, vec, _const_spec((4, D)), vec, wspec, vec, wspec, vec, vec],
        out_specs=[_row_spec(tm, DMIX), _row_spec(tm, D), _row_spec(tm, D)],
        out_shape=[jax.ShapeDtypeStruct((T, DMIX), BF16), jax.ShapeDtypeStruct((T, D), F32),
                   jax.ShapeDtypeStruct((T, D), F32)],
        scratch_shapes=[pltpu.VMEM((SUBLANES, D), F32), pltpu.VMEM((1, D), F32)],
        compiler_params=_cparams(("arbitrary",)),
    )(o, g_attn, x_lru, g_lru, gain_a, gain_l, conv_w, conv_b, w_r, b_r, w_i, b_i, lam)


def _tail(ycat, x, p, tgt, w_out, post_gain, w_ple, ple_gain, w_gate, b_gate):
    T = x.shape[0]
    tm = TM

    def body(ycat_ref, x_ref, p_ref, t_ref, wo_ref, pg_ref, wp_ref, eg_ref, wg_ref, bg_ref,
             dh1_ref, dycat_ref, dmix_ref, h1b_ref, dgp_ref, pb_ref, dpe_ref, acc_ref):
        @pl.when(pl.program_id(0) == 0)
        def _():
            acc_ref[...] = jnp.zeros_like(acc_ref)

        mix = _dot(ycat_ref[...], wo_ref[...])
        rstd_m = _rstd(mix)
        mhat = mix * rstd_m
        h1 = x_ref[...] + mhat * pg_ref[...]
        pb = p_ref[...].astype(BF16)
        pb_ref[...] = pb
        pe = _dot(pb, wp_ref[...])
        rstd_p = _rstd(pe)
        pehat = pe * rstd_p
        e = pehat * eg_ref[...]
        h1b = h1.astype(BF16)
        h1b_ref[...] = h1b
        gate = _sigmoid(_dot(h1b, wg_ref[...]) + bg_ref[...])
        diff = (h1 + gate * e) - t_ref[...]

        dy = diff * (1.0 / D)
        de = dy * gate
        dgp = (dy * e) * gate * (1.0 - gate)
        dgpb = dgp.astype(BF16)
        dgp_ref[...] = dgpb
        dh1 = dy + _dot_nt(dgpb, wg_ref[...])
        dh1_ref[...] = dh1
        dpe_ref[...] = _rms_bwd(de * eg_ref[...], pehat, rstd_p).astype(BF16)
        dmix = _rms_bwd(dh1 * pg_ref[...], mhat, rstd_m).astype(BF16)
        dmix_ref[...] = dmix
        dycat_ref[...] = _dot_nt(dmix, wo_ref[...])

        acc_ref[0:1, :] += jnp.sum(dh1 * mhat, axis=0, keepdims=True)
        acc_ref[1:2, :] += jnp.sum(de * pehat, axis=0, keepdims=True)
        acc_ref[2:3, :] += jnp.sum(dgp, axis=0, keepdims=True)
        acc_ref[3:4, :] += jnp.sum(diff * diff, axis=0, keepdims=True) * (0.5 / D)

    vec = _const_spec((1, D))
    bf = jax.ShapeDtypeStruct((T, D), BF16)
    return _call(
        body, name="tail", grid=(T // tm,),
        in_specs=[_row_spec(tm, DMIX), _row_spec(tm, D), _row_spec(tm, DPLE), _row_spec(tm, D),
                  _const_spec((DMIX, D)), vec, _const_spec((DPLE, D)), vec, _const_spec((D, D)), vec],
        out_specs=[_row_spec(tm, D), _row_spec(tm, DMIX), _row_spec(tm, D), _row_spec(tm, D), _row_spec(tm, D),
                   _row_spec(tm, DPLE), _row_spec(tm, D), _const_spec((SUBLANES, D))],
        out_shape=[jax.ShapeDtypeStruct((T, D), F32), jax.ShapeDtypeStruct((T, DMIX), F32), bf, bf, bf,
                   jax.ShapeDtypeStruct((T, DPLE), BF16), bf, jax.ShapeDtypeStruct((SUBLANES, D), F32)],
        compiler_params=_cparams(("arbitrary",), VMEM_BIG),
    )(ycat, x, p, tgt, w_out, post_gain, w_ple, ple_gain, w_gate, b_gate)


def _pair_copies(srcs, gots, send_sems, recv_sems):
    x, y, c = _position()
    copies = []
    for a, (src, got) in enumerate(zip(srcs, gots)):
        half = src.shape[1] // 2
        rows = pl.ds(pl.multiple_of((1 - c) * half, SUBLANES), half)
        copies.append(pltpu.make_async_remote_copy(
            src_ref=src.at[:, rows, :], dst_ref=got, send_sem=send_sems.at[a], recv_sem=recv_sems.at[a],
            device_id=(x, y, 1 - c), device_id_type=MESH))
    return copies


def _branches_bwd(dycat, o, g_attn, h, g_lru, gain_a, gain_l, ycat, dmix, h1b, dgp, pb, dpe):
    T = o.shape[0]
    tm = TM
    nt = T // tm
    nb_gate, nb_ple = min(nt, 8), min(nt, 2)
    ns_gate, ns_ple = nt // nb_gate, nt // nb_ple
    br_out, br_gate, br_ple = DMIX // nt, D // nb_gate, DPLE // nb_ple
    tk_gate, tk_ple = T // ns_gate, T // ns_ple

    def body(dy_ref, o_ref, ga_ref, h_ref, gl_ref, gna_ref, gnl_ref, yc_ref, dmix_ref, h1_ref, dgp_ref, pb_ref,
             dpe_ref, do_ref, dga_ref, dgl_ref, dh_ref, acc_ref, gwo_ref, gwg_ref, gwp_ref):
        i = pl.program_id(0)

        @pl.when(i == 0)
        def _():
            acc_ref[...] = jnp.zeros_like(acc_ref)

        def accumulate(out_ref, lhs_ref, rhs_ref, tokens, slices):
            s = i % slices
            part = _dot_tn(lhs_ref[...], rhs_ref[pl.ds(pl.multiple_of(s * tokens, tokens), tokens), :])
            out_ref[...] = part + jnp.where(s == 0, 0.0, out_ref[...])

        gwo_ref[...] = _dot_tn(yc_ref[...], dmix_ref[...])

        def branch(val, g, gain, dyv):
            rstd = _rstd(val)
            vhat = val * rstd
            sig = _sigmoid(g)
            dn = dyv * (g * sig)
            dg = dyv * (vhat * gain) * (sig * (1.0 + g * (1.0 - sig)))
            dgain = jnp.sum(dn * vhat, axis=0, keepdims=True)
            return _rms_bwd(dn * gain, vhat, rstd), dg, dgain

        ov = o_ref[...]
        do, dga, dgain_a = branch(ov, ga_ref[...], gna_ref[...], dy_ref[:, :D])
        dga_ref[...] = dga.astype(BF16)
        prod = do * ov
        for hd in range(H):
            head = slice(hd * DH, (hd + 1) * DH)
            do_ref[:, hd * AUG:hd * AUG + DH] = do[:, head].astype(BF16)
            do_ref[:, hd * AUG + DH:(hd + 1) * AUG] = _extras(-jnp.sum(prod[:, head], axis=1, keepdims=True), None)

        accumulate(gwg_ref, h1_ref, dgp_ref, tk_gate, ns_gate)
        accumulate(gwp_ref, pb_ref, dpe_ref, tk_ple, ns_ple)
        dh, dgl, dgain_l = branch(h_ref[...], gl_ref[...], gnl_ref[...], dy_ref[:, D:])
        dh_ref[...] = dh
        dgl_ref[...] = dgl.astype(BF16)
        acc_ref[0:1, :] += dgain_a
        acc_ref[1:2, :] += dgain_l

    vec = _const_spec((1, D))
    bf = jax.ShapeDtypeStruct((T, D), BF16)
    tokens = _weight_spec((T, D))
    return _call(
        body, name="branches_bwd", grid=(nt,),
        in_specs=[_row_spec(tm, DMIX)] + [_row_spec(tm, D)] * 4 + [vec, vec]
        + [pl.BlockSpec((T, br_out), lambda i: (0, i)), tokens,
           pl.BlockSpec((tk_gate, br_gate), lambda i: (i % ns_gate, i // ns_gate)), tokens,
           pl.BlockSpec((tk_ple, br_ple), lambda i: (i % ns_ple, i // ns_ple)), tokens],
        out_specs=[_row_spec(tm, H * AUG), _row_spec(tm, D), _row_spec(tm, D), _row_spec(tm, D),
                   _const_spec((SUBLANES, D)),
                   pl.BlockSpec((br_out, D), lambda i: (i, 0)),
                   pl.BlockSpec((br_gate, D), lambda i: (i // ns_gate, 0)),
                   pl.BlockSpec((br_ple, D), lambda i: (i // ns_ple, 0))],
        out_shape=[jax.ShapeDtypeStruct((T, H * AUG), BF16), bf, bf, jax.ShapeDtypeStruct((T, D), F32),
                   jax.ShapeDtypeStruct((SUBLANES, D), F32), jax.ShapeDtypeStruct((DMIX, D), F32),
                   jax.ShapeDtypeStruct((D, D), F32), jax.ShapeDtypeStruct((DPLE, D), F32)],
        compiler_params=_cparams(("arbitrary",), VMEM_BIG),
    )(dycat, o, g_attn, h, g_lru, gain_a, gain_l, ycat, dmix, h1b, dgp, pb, dpe)


def _lru_bwd(dh, h, xc, x_lru, conv_w, w_r, b_r, w_i, b_i, lam, pair_parts=()):
    T = dh.shape[0]
    tm = TM
    nt = T // tm
    per = tm // SUBLANES
    npair = len(pair_parts)

    def body(dh_ref, h_ref, hprev_ref, xc_ref, xl_ref, cw_ref, wr_ref, br_ref, wi_ref, bi_ref, lam_ref, *rest):
        parts, rest = rest[:npair], rest[npair:]
        dxl_ref, dwr_ref, dwi_ref, acc_ref = rest[:4]
        gots, rest = rest[4:4 + npair], rest[4 + npair:]
        carry_s, dxc_next_s, top_s, dht_s = rest[:4]
        i = pl.program_id(0)

        @pl.when(i == 0)
        def _():
            acc_ref[...] = jnp.zeros_like(acc_ref)
            dwr_ref[...] = jnp.zeros_like(dwr_ref)
            dwi_ref[...] = jnp.zeros_like(dwi_ref)
            carry_s[...] = jnp.zeros_like(carry_s)
            dxc_next_s[...] = jnp.zeros_like(dxc_next_s)
            for cp in _pair_copies(parts, gots, *rest[4:]) if npair else ():
                cp.start()

        if npair:
            @pl.when(i == nt - 1)
            def _():
                for cp in _pair_copies(parts, gots, *rest[4:]):
                    cp.wait()

        inner = jnp.where(i == nt - 1, 0.0, 1.0)
        xc = xc_ref[...]
        r, ig, sp, a, sq, inv_sq = _lru_gates(xc, wr_ref, br_ref, wi_ref, bi_ref, lam_ref)

        row = lax.broadcasted_iota(jnp.int32, (tm, D), 0)
        u = dh_ref[...] + jnp.where(row == tm - 1, carry_s[...], 0.0)
        _scan_bwd_into(pltpu.roll(a, tm - 1, 0), u, dht_s)
        dht = dht_s[...]
        top_s[...] = a[:SUBLANES, :] * dht[:SUBLANES, :]
        carry_s[...] = top_s[0:1, :]

        hprev = hprev_ref[...] * inner
        da = dht * _shift_down(h_ref[...], 1, hprev)
        dig = dht * sq * xc
        dxc = dht * sq * ig
        dsq = dht * ig * xc
        dla = da * a - dsq * (a * a) * inv_sq
        dr = dla * ((-LRU_C) * sp)
        dpr = dr * r * (1.0 - r)
        dpi = dig * ig * (1.0 - ig)
        for n in range(NB):
            blk = slice(n * LANES, (n + 1) * LANES)
            xcb = xc[:, blk].astype(BF16)
            dwr_ref[n] += _dot_tn(xcb, dpr[:, blk].astype(BF16))
            dwi_ref[n] += _dot_tn(xcb, dpi[:, blk].astype(BF16))
        dxc = dxc + _gate_pre_t(dpr, wr_ref) + _gate_pre_t(dpi, wi_ref)

        xl = xl_ref[...]
        nxt = dxc_next_s[...]
        dxl = dxc * cw_ref[3:4, :]
        acc_ref[3:4, :] += jnp.sum(dxc * xl, axis=0, keepdims=True)
        for j in range(3):
            ahead = _shift_up(dxc, 3 - j, nxt)
            dxl = dxl + ahead * cw_ref[j:j + 1, :]
            acc_ref[j:j + 1, :] += jnp.sum(ahead * xl, axis=0, keepdims=True)
        dxc_next_s[...] = dxc[:SUBLANES, :]
        dxl_ref[...] = dxl.astype(BF16)

        acc_ref[4:5, :] += jnp.sum(dxc, axis=0, keepdims=True)
        acc_ref[5:6, :] += jnp.sum(dpr, axis=0, keepdims=True)
        acc_ref[6:7, :] += jnp.sum(dpi, axis=0, keepdims=True)
        acc_ref[7:8, :] += jnp.sum(dla * ((-LRU_C) * r), axis=0, keepdims=True)

        @pl.when(i == nt - 1)
        def _():
            lam_v = lam_ref[...]
            acc_ref[7:8, :] = acc_ref[7:8, :] * (-_sigmoid(-lam_v))

    rev = pl.BlockSpec((tm, D), lambda i: (nt - 1 - i, 0))
    prev8 = pl.BlockSpec((SUBLANES, D), lambda i: (jnp.maximum((nt - 1 - i) * per - 1, 0), 0))
    vec = _const_spec((1, D))
    wspec = _const_spec((NB, LANES, LANES))
    bf = jax.ShapeDtypeStruct((T, D), BF16)
    halves = [jax.ShapeDtypeStruct((s.shape[0], s.shape[1] // 2, s.shape[2]), s.dtype) for s in pair_parts]
    outs = _call(
        body, name="lru_bwd", grid=(nt,),
        in_specs=[rev, rev, prev8, rev, rev, _const_spec((4, D)), wspec, vec, wspec, vec, vec] + [HBM_SPEC] * npair,
        out_specs=[rev, wspec, wspec, _const_spec((SUBLANES, D))] + [HBM_SPEC] * npair,
        out_shape=[bf, jax.ShapeDtypeStruct((NB, LANES, LANES), F32), jax.ShapeDtypeStruct((NB, LANES, LANES), F32),
                   jax.ShapeDtypeStruct((SUBLANES, D), F32)] + halves,
        scratch_shapes=[pltpu.VMEM((1, D), F32), pltpu.VMEM((SUBLANES, D), F32), pltpu.VMEM((SUBLANES, D), F32),
                        pltpu.VMEM((tm, D), F32)]
        + ([pltpu.SemaphoreType.DMA((npair,)), pltpu.SemaphoreType.DMA((npair,))] if npair else []),
        compiler_params=_cparams(("arbitrary",)),
    )(dh, h, h, xc, x_lru, conv_w, w_r, b_r, w_i, b_i, lam, *pair_parts)
    return (*outs[:4], list(outs[4:]))


def _chip_copies(srcs, dsts, send_sems, recv_sems):
    x, y, c = _position()
    chip = 2 * x + y
    na = len(srcs)
    return [pltpu.make_async_remote_copy(
        src_ref=srcs[a].at[2 * px + py], dst_ref=dsts[a].at[chip], send_sem=send_sems.at[j * na + a],
        recv_sem=recv_sems.at[j * na + a], device_id=(px, py, c), device_id_type=MESH)
        for j, (px, py) in enumerate(_other_chips(x, y)) for a in range(na)]


def _attn_bwd(q_aug, qx, k_aug, v_aug, do_aug, exchange=()):
    T = q_aug.shape[0]
    t = TA
    n = T // t
    hp = BWD_HEADS
    heads = range(hp)
    scale = DH ** -0.5
    ki_tab, qi_tab = _causal_pairs(n, q_major=False)
    last = ki_tab.shape[0] - 1
    ne = len(exchange)
    n_h = H // hp

    def body(ki_ref, qi_ref, q_ref, qx_ref, k_ref, v_ref, do_ref, *rest):
        sent, rest = rest[:ne], rest[ne:]
        dq_ref, dk_ref, dv_ref, dc_ref = rest[:4]
        received, rest = rest[4:4 + ne], rest[4 + ne:]
        dq_s, dk_s, dv_s = rest[:3]
        j = pl.program_id(1)
        ki = ki_ref[j]
        qi = qi_ref[j]

        if ne:
            first_step = (pl.program_id(0) == 0) & (j == 0)
            last_step = (pl.program_id(0) == n_h - 1) & (j == last)

            @pl.when(first_step)
            def _():
                for cp in _chip_copies(sent, received, *rest[3:]):
                    cp.start()

            @pl.when(last_step)
            def _():
                for cp in _chip_copies(sent, received, *rest[3:]):
                    cp.wait()

        @pl.when(j == 0)
        def _():
            dq_s[...] = jnp.zeros_like(dq_s)

        @pl.when(qi == ki)
        def _():
            dk_s[...] = jnp.zeros_like(dk_s)
            dv_s[...] = jnp.zeros_like(dv_s)

        def step(on_diagonal):
            cols = [slice(a * AUG, (a + 1) * AUG) for a in heads]
            qb = [jnp.concatenate([q_ref[:, a * AUG:a * AUG + DH], qx_ref[:, a * DH:(a + 1) * DH]], axis=1)
                  for a in heads]
            if on_diagonal:
                krow = lax.broadcasted_iota(jnp.int32, (t, t), 0)
                qcol = lax.broadcasted_iota(jnp.int32, (t, t), 1)

            def scores(a):
                st = _dot_nt(k_ref[:, cols[a]], qb[a])
                dpd = _dot_nt(v_ref[:, cols[a]], do_ref[:, cols[a]])
                return (jnp.where(krow <= qcol, st, NEG) if on_diagonal else st), dpd

            off = pl.multiple_of(qi * t, t)
            ahead = scores(0)
            for a in heads:
                st, dpd = ahead
                if a + 1 < hp:
                    ahead = scores(a + 1)
                pt = jnp.exp2(st)
                dsb = (pt * dpd).astype(BF16)
                dv_s[a] += _dot(pt.astype(BF16), do_ref[:, a * AUG:a * AUG + DH])
                dk_s[a] += _dot(dsb, qb[a])
                dq_s[a, pl.ds(off, t), :] += _dot_tn(dsb, k_ref[:, cols[a]])

        @pl.when(qi > ki)
        def _():
            step(False)

        @pl.when(qi == ki)
        def _():
            step(True)

        @pl.when(qi == n - 1)
        def _():
            rows = pl.ds(pl.multiple_of(ki * t, t), t)
            for a in heads:
                dk_ref[:, a * DH:(a + 1) * DH] = (dk_s[a, :, :DH] * LN2).astype(BF16)
                dv_ref[:, a * DH:(a + 1) * DH] = dv_s[a].astype(BF16)
                dc_ref[a, rows, :] = jnp.broadcast_to(-dk_s[a, :, DH + 3:DH + 4], (t, LANES))

        @pl.when(j == last)
        def _():
            for a in heads:
                dq_ref[:, a * DH:(a + 1) * DH] = (dq_s[a, :, :DH] * scale).astype(BF16)
                dc_ref[a] = dc_ref[a] + jnp.broadcast_to(dq_s[a, :, DH:DH + 1], (T, LANES))

    qside = pl.BlockSpec((t, hp * AUG), lambda h, j, ki_ref, qi_ref: (qi_ref[j], h))
    qxside = pl.BlockSpec((t, hp * DH), lambda h, j, ki_ref, qi_ref: (qi_ref[j], h))
    kside = pl.BlockSpec((t, hp * AUG), lambda h, j, ki_ref, qi_ref: (ki_ref[j], h))
    kout = pl.BlockSpec((t, hp * DH), lambda h, j, ki_ref, qi_ref: (ki_ref[j], h))
    bf = jax.ShapeDtypeStruct((T, D), BF16)
    sums = jax.ShapeDtypeStruct((H, T, LANES), F32)
    grid_spec = pltpu.PrefetchScalarGridSpec(
        num_scalar_prefetch=2, grid=(n_h, ki_tab.shape[0]),
        in_specs=[qside, qxside, kside, kside, qside] + [HBM_SPEC] * ne,
        out_specs=[pl.BlockSpec((T, hp * DH), lambda h, j, ki_ref, qi_ref: (0, h)), kout, kout,
                   pl.BlockSpec((hp, T, LANES), lambda h, j, ki_ref, qi_ref: (h, 0, 0))] + [HBM_SPEC] * ne,
        scratch_shapes=[pltpu.VMEM((hp, T, AUG), F32), pltpu.VMEM((hp, t, AUG), F32), pltpu.VMEM((hp, t, DH), F32)]
        + ([pltpu.SemaphoreType.DMA((3 * ne,)), pltpu.SemaphoreType.DMA((3 * ne,))] if ne else []))
    outs = _call(
        body, name="attn_bwd", grid_spec=grid_spec,
        out_shape=[bf, bf, bf, sums] + [jax.ShapeDtypeStruct(s.shape, s.dtype) for s in exchange],
        compiler_params=_cparams(("arbitrary", "arbitrary"), VMEM_BIG),
    )(ki_tab, qi_tab, q_aug, qx, k_aug, v_aug, do_aug, *exchange)
    return (*outs[:4], list(outs[4:]))


def _fgate_bwd(dc_heads, flb):
    T = flb.shape[0]
    tm = TM
    nt = T // tm

    def body(dch_ref, flb_ref, dfl_ref, acc_ref, carry, top_s):
        @pl.when(pl.program_id(0) == 0)
        def _():
            carry[...] = jnp.zeros_like(carry)
            acc_ref[...] = jnp.zeros_like(acc_ref)

        flb = flb_ref[...]
        lane = lax.broadcasted_iota(jnp.int32, flb.shape, 1)
        dc = jnp.zeros(flb.shape, F32)
        for hd in range(H):
            dc = dc + jnp.where(lane == hd, dch_ref[hd], 0.0)
        r = lax.broadcasted_iota(jnp.int32, (tm, tm), 0)
        c = lax.broadcasted_iota(jnp.int32, (tm, tm), 1)
        dls = _dot_exact((c >= r).astype(F32), dc) + carry[...]
        top_s[...] = dls[:SUBLANES, :]
        carry[...] = top_s[0:1, :]
        dfl = jnp.where(lane < H, dls * _sigmoid(-flb), 0.0)
        dfl_ref[...] = dfl.astype(BF16)
        acc_ref[0:1, :] += jnp.sum(dfl, axis=0, keepdims=True)

    rev = pl.BlockSpec((tm, LANES), lambda i: (nt - 1 - i, 0))
    return _call(
        body, name="fgate_bwd", grid=(nt,),
        in_specs=[pl.BlockSpec((H, tm, LANES), lambda i: (0, nt - 1 - i, 0)), rev],
        out_specs=[rev, _const_spec((SUBLANES, LANES))],
        out_shape=[jax.ShapeDtypeStruct((T, LANES), BF16), jax.ShapeDtypeStruct((SUBLANES, LANES), F32)],
        scratch_shapes=[pltpu.VMEM((1, LANES), F32), pltpu.VMEM((SUBLANES, LANES), F32)],
        compiler_params=_cparams(("arbitrary",)),
    )(dc_heads, flb)


def _dx(dz, dfl, w_a, w_f, w_b, x, pre_gain, dh1, exchange=()):
    T = x.shape[0]
    tm = TM
    nt = T // tm
    ne = len(exchange)

    def body(*refs):
        dz_refs = refs[:6]
        dfl_ref, wa_ref, wf_ref, wb_ref, x_ref, g_ref, dh1_ref = refs[6:13]
        sent = refs[13:13 + ne]
        gx_ref, acc_ref = refs[13 + ne:15 + ne]
        received, sems = refs[15 + ne:15 + 2 * ne], refs[15 + 2 * ne:]

        @pl.when(pl.program_id(0) == 0)
        def _():
            acc_ref[...] = jnp.zeros_like(acc_ref)
            for cp in _chip_copies(sent, received, *sems) if ne else ():
                cp.start()

        if ne:
            @pl.when(pl.program_id(0) == nt - 1)
            def _():
                for cp in _chip_copies(sent, received, *sems):
                    cp.wait()

        dxn = _dot(dfl_ref[...], wf_ref[...])
        for s in range(3):
            dxn = dxn + _dot(dz_refs[s][...], wa_ref[s * D:(s + 1) * D, :])
            dxn = dxn + _dot(dz_refs[3 + s][...], wb_ref[s * D:(s + 1) * D, :])
        xv = x_ref[...]
        rstd = _rstd(xv)
        xhat = xv * rstd
        gx_ref[...] = dh1_ref[...] + _rms_bwd(dxn * g_ref[...], xhat, rstd)
        acc_ref[0:1, :] += jnp.sum(dxn * xhat, axis=0, keepdims=True)

    outs = _call(
        body, name="dx", grid=(nt,),
        in_specs=[_row_spec(tm, D)] * 6 + [_row_spec(tm, LANES), _weight_spec((3 * D, D)), _weight_spec((LANES, D)),
                                           _weight_spec((3 * D, D)), _row_spec(tm, D), _const_spec((1, D)),
                                           _row_spec(tm, D)] + [HBM_SPEC] * ne,
        out_specs=[_row_spec(tm, D), _const_spec((SUBLANES, D))] + [HBM_SPEC] * ne,
        out_shape=[jax.ShapeDtypeStruct((T, D), F32), jax.ShapeDtypeStruct((SUBLANES, D), F32)]
        + [jax.ShapeDtypeStruct(s.shape, s.dtype) for s in exchange],
        scratch_shapes=[pltpu.SemaphoreType.DMA((3 * ne,)), pltpu.SemaphoreType.DMA((3 * ne,))] if ne else [],
        compiler_params=_cparams(("arbitrary",), VMEM_BIG),
    )(*dz, dfl, w_a, w_f, w_b, x, pre_gain, dh1, *exchange)
    return outs[0], outs[1], list(outs[2:])


GRAD_ROWS = D_IN + SUBLANES


def _dw_in_segments(dz_a, dz_b, xn, buf, pair, bt):
    T = xn.shape[0]
    nt = T // bt
    first, second = [(2 * pair + k) * D + (H if 2 * pair + k >= 3 else 0) for k in (0, 1)]
    step8 = (second - first) // SUBLANES

    def body(*refs):
        dza_ref, dzb_ref, xn_ref, o_ref = refs[0], refs[1], refs[2], refs[-1]

        @pl.when(pl.program_id(1) == 0)
        def _():
            o_ref[...] = jnp.zeros_like(o_ref)

        @pl.when(pl.program_id(0) == 0)
        def _():
            o_ref[...] += _dot_tn(dza_ref[...], xn_ref[...])

        @pl.when(pl.program_id(0) == 1)
        def _():
            o_ref[...] += _dot_tn(dzb_ref[...], xn_ref[...])

    spec_a = pl.BlockSpec((bt, D), lambda s, t: (jnp.where(s == 0, t, nt - 1), 0))
    spec_b = pl.BlockSpec((bt, D), lambda s, t: (jnp.where(s == 1, t, 0), 0))
    return _call(
        body, name="dw_in_%d" % pair, grid=(2, nt),
        in_specs=[spec_a, spec_b, pl.BlockSpec((bt, D), lambda s, t: (t, 0))]
        + ([] if buf is None else [pl.BlockSpec(memory_space=pl.ANY)]),
        out_specs=pl.BlockSpec((pl.Element(D), pl.Element(D)),
                               lambda s, t: ((first // SUBLANES + s * step8) * SUBLANES, 0)),
        out_shape=jax.ShapeDtypeStruct((GRAD_ROWS, D), F32),
        input_output_aliases={} if buf is None else {3: 0},
        compiler_params=_cparams(("arbitrary", "arbitrary"), VMEM_BIG),
    )(*((dz_a, dz_b, xn) if buf is None else (dz_a, dz_b, xn, buf)))


def _dw_in_t(dz, dfl, xn, bt=DW_TOKENS):
    T = xn.shape[0]
    bt = min(bt, T)
    nt = T // bt
    main = None
    for pair in range(3):
        main = _dw_in_segments(dz[2 * pair], dz[2 * pair + 1], xn, main, pair, bt)

    def f_body(dfl_ref, xn_ref, main_ref, o_ref, acc_s):
        p = pl.program_id(0)
        t = pl.program_id(1)

        @pl.when(t == 0)
        def _():
            acc_s[...] = jnp.zeros_like(acc_s)

        @pl.when(p == 0)
        def _():
            acc_s[...] += _dot_tn(dfl_ref[...], xn_ref[...])

        @pl.when(t == nt - 1)
        def _():
            o_ref[...] = acc_s[:SUBLANES, :]

    fl_block = FL0 // SUBLANES
    end_block = D_IN // SUBLANES
    return _call(
        f_body, name="dw_in_f", grid=(2, nt),
        in_specs=[pl.BlockSpec((bt, LANES), lambda p, t: (t, 0)), pl.BlockSpec((bt, D), lambda p, t: (t, 0)),
                  pl.BlockSpec(memory_space=pl.ANY)],
        out_specs=pl.BlockSpec((SUBLANES, D), lambda p, t: (fl_block + p * (end_block - fl_block), 0)),
        out_shape=jax.ShapeDtypeStruct((GRAD_ROWS, D), F32),
        scratch_shapes=[pltpu.VMEM((LANES, D), F32)],
        input_output_aliases={2: 0},
        compiler_params=_cparams(("arbitrary", "arbitrary")),
    )(dfl, xn, main)


HBM_SPEC = pl.BlockSpec(memory_space=pltpu.HBM)
VMEM_SPEC = pl.BlockSpec(memory_space=pltpu.VMEM)


def _position():
    return lax.axis_index("x"), lax.axis_index("y"), lax.axis_index("c")


def _other_chips(x, y):
    return [(1 - x, y), (x, 1 - y), (1 - x, 1 - y)]


def _gather_shards(shards, whole):
    na, nw = len(shards), len(whole)
    nall = na + nw

    def body(*refs):
        gather = _GatherPlan(refs[:nall], refs[nall:2 * nall], refs[2 * nall:], na)
        gather.send()
        gather.forward()
        gather.finish()

    arrs = list(shards) + list(whole)
    outs = _call(
        body, name="gather_shards",
        in_specs=[HBM_SPEC] * nall, out_specs=[HBM_SPEC] * nall,
        out_shape=_gather_out_shapes(arrs), scratch_shapes=_gather_semaphores(na, nall),
    )(*arrs)
    return _place_own(outs, arrs)


def _gather_out_shapes(arrs):
    return [jax.ShapeDtypeStruct((N_CHIPS,) + s.shape, s.dtype) for s in arrs]


def _gather_semaphores(na, nall):
    return [pltpu.SemaphoreType.DMA((3 * nall,)), pltpu.SemaphoreType.DMA((3 * nall,)),
            pltpu.SemaphoreType.DMA((3 * na,)), pltpu.SemaphoreType.DMA((3 * na,))]


def _place_own(outs, arrs):
    if not arrs:
        return []
    chip = 2 * lax.axis_index("x") + lax.axis_index("y")
    return [lax.dynamic_update_slice(o, a[None], (chip,) + (0,) * a.ndim) for o, a in zip(outs, arrs)]


class _GatherPlan:
    def __init__(self, srcs, dsts, sems, na):
        ici_send, ici_recv, d2d_send, d2d_recv = sems
        x, y, c = _position()
        chip = 2 * x + y
        nall = len(srcs)

        def half(a, which):
            rows = srcs[a].shape[0] // 2
            return pl.ds(pl.multiple_of(which * rows, BF16_ROWS), rows)

        def copy(src, dst, send, recv, k, to):
            return pltpu.make_async_remote_copy(src_ref=src, dst_ref=dst, send_sem=send.at[k], recv_sem=recv.at[k],
                                                device_id=to, device_id_type=MESH)

        self.first, self.landed, self.passed, self.returned = [], [], [], []
        for j, (px, py) in enumerate(_other_chips(x, y)):
            theirs = 2 * px + py
            for a in range(nall):
                k = j * nall + a
                if a < na:
                    self.first.append(copy(srcs[a].at[half(a, c), :], dsts[a].at[chip, half(a, c), :],
                                           ici_send, ici_recv, k, (px, py, c)))
                    mine = dsts[a].at[theirs, half(a, c), :]
                    other = dsts[a].at[theirs, half(a, 1 - c), :]
                    self.landed.append(copy(mine, mine, ici_send, ici_recv, k, (px, py, c)))
                    self.passed.append(copy(mine, mine, d2d_send, d2d_recv, j * na + a, (x, y, 1 - c)))
                    self.returned.append(copy(other, other, d2d_send, d2d_recv, j * na + a, (x, y, 1 - c)))
                else:
                    self.first.append(copy(srcs[a], dsts[a].at[chip], ici_send, ici_recv, k, (px, py, c)))
                    got = dsts[a].at[theirs]
                    self.landed.append(copy(got, got, ici_send, ici_recv, k, (px, py, c)))
                    self.passed.append(None)

    def send(self):
        for cp in self.first:
            cp.start()

    def forward(self):
        for arrival, fwd in zip(self.landed, self.passed):
            arrival.wait_recv()
            if fwd is not None:
                fwd.start()

    def finish(self):
        for cp in self.returned:
            cp.wait_recv()
        for cp in self.first + [f for f in self.passed if f is not None]:
            cp.wait_send()


W_ROWS = 1568
G_ROWS = 1552
SHARD_ROWS = D_IN // N_CHIPS
WINDOW_STEP = 1536


def _assemble_w_in(cont):
    cb = COL_BLOCK
    half = WINDOW_STEP
    seam = BF16_ROWS

    def body(c_ref, wa_ref, wf_ref, wb_ref):
        x0 = c_ref[0].astype(F32)
        x1, x2, x3 = (pltpu.roll(c_ref[j].astype(F32), 2 * j, 0) for j in (1, 2, 3))
        wa = jnp.concatenate([x0[:half], x0[half:half + seam] + x1[:seam], x1[seam:half]], axis=0)
        wa_ref[...] = wa.astype(BF16)

        fl = x1[half:half + seam] + x2[:seam]
        row = lax.broadcasted_iota(jnp.int32, fl.shape, 0)
        wf_ref[:seam, :] = jnp.where(row < H, fl, 0.0).astype(BF16)
        wf_ref[seam:, :] = jnp.zeros((LANES - seam, cb), BF16)

        mid = x2[half:half + SUBLANES] + x3[:SUBLANES]
        wb = jnp.concatenate([x2[SUBLANES:half], mid, x3[SUBLANES:half + SUBLANES]], axis=0)
        wb_ref[...] = wb.astype(BF16)

    return _call(
        body, name="assemble_w_in", grid=(D // cb,),
        in_specs=[pl.BlockSpec((N_CHIPS, W_ROWS, cb), lambda i: (0, 0, i))],
        out_specs=[pl.BlockSpec((3 * D, cb), lambda i: (0, i)), pl.BlockSpec((LANES, cb), lambda i: (0, i)),
                   pl.BlockSpec((3 * D, cb), lambda i: (0, i))],
        out_shape=[jax.ShapeDtypeStruct((3 * D, D), BF16), jax.ShapeDtypeStruct((LANES, D), BF16),
                   jax.ShapeDtypeStruct((3 * D, D), BF16)],
        compiler_params=_cparams(("parallel",)),
    )(cont)


def _pair_exchange_windows(grad_t):
    half_g = G_ROWS // 2

    def body(g_ref, got, send_sems, recv_sems):
        x, y, c = _position()
        copies = []
        for j in range(N_CHIPS):
            rows = pl.ds(pl.multiple_of(j * WINDOW_STEP + (1 - c) * half_g, SUBLANES), half_g)
            copies.append(pltpu.make_async_remote_copy(
                src_ref=g_ref.at[rows, :], dst_ref=got.at[j], send_sem=send_sems.at[j], recv_sem=recv_sems.at[j],
                device_id=(x, y, 1 - c), device_id_type=MESH))
        for cp in copies:
            cp.start()
        for cp in copies:
            cp.wait()

    return _call(
        body, name="pair_exchange_w_in",
        in_specs=[HBM_SPEC], out_specs=HBM_SPEC,
        out_shape=jax.ShapeDtypeStruct((N_CHIPS, half_g, D), F32),
        scratch_shapes=[pltpu.SemaphoreType.DMA((N_CHIPS,)), pltpu.SemaphoreType.DMA((N_CHIPS,))],
    )(grad_t)


def _pair_sum(parts, gots, c):
    na = len(parts)

    def body(c_ref, *refs):
        for a in range(na):
            refs[2 * na + a][...] = (refs[a][...] + refs[na + a][...]).astype(BF16)

    mine = [pl.BlockSpec(g.shape, lambda i, c_ref: (0, c_ref[0], 0)) for g in gots]
    whole = [pl.BlockSpec(g.shape, lambda i, c_ref: (0, 0, 0)) for g in gots]
    grid_spec = pltpu.PrefetchScalarGridSpec(
        num_scalar_prefetch=1, grid=(1,), in_specs=mine + whole, out_specs=whole)
    return _call(
        body, name="pair_sum", grid_spec=grid_spec,
        out_shape=[jax.ShapeDtypeStruct(g.shape, BF16) for g in gots],
        compiler_params=_cparams(("arbitrary",), VMEM_BIG),
    )(c.reshape(1), *parts, *gots)


def _pair_sum_windows(grad_t, got, c):
    _, half, C = got.shape
    cb = SUM_BLOCK

    def body(c_ref, a_ref, b_ref, o_ref):
        o_ref[0] = (a_ref[...] + b_ref[0]).astype(BF16)

    def mine(j, i, c_ref):
        return ((j * (WINDOW_STEP // SUBLANES) + c_ref[0] * (half // SUBLANES)) * SUBLANES, i * cb)

    spec = pl.BlockSpec((1, half, cb), lambda j, i, c_ref: (j, 0, i))
    grid_spec = pltpu.PrefetchScalarGridSpec(
        num_scalar_prefetch=1, grid=(N_CHIPS, C // cb),
        in_specs=[pl.BlockSpec((pl.Element(half), pl.Element(cb)), mine), spec], out_specs=spec)
    return _call(
        body, name="pair_sum_w_in", grid_spec=grid_spec,
        out_shape=jax.ShapeDtypeStruct((N_CHIPS, half, C), BF16),
        compiler_params=_cparams(("parallel", "parallel")),
    )(c.reshape(1), grad_t, got)


def _chip_sum(own, got, chip, name):
    _, half, C = got.shape
    cb = min(C, SUM_BLOCK)

    def body(chip_ref, own_ref, g_ref, o_ref):
        for me in range(N_CHIPS):
            @pl.when(chip_ref[0] == me)
            def _(me=me):
                terms = [own_ref[0] if k == me else g_ref[k] for k in range(N_CHIPS)]
                acc = terms[0].astype(F32) + terms[1].astype(F32)
                acc = acc + terms[2].astype(F32)
                o_ref[...] = acc + terms[3].astype(F32)

    grid_spec = pltpu.PrefetchScalarGridSpec(
        num_scalar_prefetch=1, grid=(C // cb,),
        in_specs=[pl.BlockSpec((1, half, cb), lambda i, chip_ref: (chip_ref[0], 0, i)),
                  pl.BlockSpec((N_CHIPS, half, cb), lambda i, chip_ref: (0, 0, i))],
        out_specs=pl.BlockSpec((half, cb), lambda i, chip_ref: (0, i)))
    return _call(
        body, name=name, grid_spec=grid_spec,
        out_shape=jax.ShapeDtypeStruct((half, C), F32),
        compiler_params=_cparams(("parallel",)),
    )(chip.reshape(1), own, got)


def _final_exchange(halves, g):
    na = len(halves)
    rows = g.shape[0]
    per = rows // N_DEV

    def body(*refs):
        srcs, g_ref = refs[:na], refs[na]
        dsts, out_ref = refs[na + 1:2 * na + 1], refs[2 * na + 1]
        got_ref, s1, r1, s2, r2, swap_send, swap_recv = refs[2 * na + 2:]
        x, y, c = _position()
        swaps = [pltpu.make_async_remote_copy(
            src_ref=srcs[a], dst_ref=dsts[a], send_sem=swap_send.at[a], recv_sem=swap_recv.at[a],
            device_id=(x, y, 1 - c), device_id_type=MESH) for a in range(na)]
        for cp in swaps:
            cp.start()
        me = 4 * x + 2 * y + c
        mine = pl.ds(pl.multiple_of(me * per, SUBLANES), per)
        peers = []
        for j in range(1, N_DEV):
            px = 1 - x if j & 4 else x
            py = 1 - y if j & 2 else y
            pc = 1 - c if j & 1 else c
            peers.append((px, py, pc))

        first = []
        for j, (px, py, pc) in enumerate(peers):
            theirs = pl.ds(pl.multiple_of((4 * px + 2 * py + pc) * per, SUBLANES), per)
            first.append(pltpu.make_async_remote_copy(
                src_ref=g_ref.at[theirs, :], dst_ref=got_ref.at[me], send_sem=s1.at[j], recv_sem=r1.at[j],
                device_id=(px, py, pc), device_id_type=MESH))
        for cp in first:
            cp.start()
        got_ref[me] = g_ref[mine, :]
        for cp in first:
            cp.wait()
        total = got_ref[0]
        for d in range(1, N_DEV):
            total = total + got_ref[d]
        out_ref[mine, :] = total

        second = []
        for j, peer in enumerate(peers):
            second.append(pltpu.make_async_remote_copy(
                src_ref=out_ref.at[mine, :], dst_ref=out_ref.at[mine, :], send_sem=s2.at[j], recv_sem=r2.at[j],
                device_id=peer, device_id_type=MESH))
        for cp in second:
            cp.start()
        for cp in second + swaps:
            cp.wait()

    sems = pltpu.SemaphoreType.DMA((N_DEV - 1,))
    swap_sems = pltpu.SemaphoreType.DMA((na,))
    outs = _call(
        body, name="final_exchange", in_hbm=False,
        in_specs=[HBM_SPEC] * na + [VMEM_SPEC], out_specs=[HBM_SPEC] * na + [VMEM_SPEC],
        out_shape=[jax.ShapeDtypeStruct(s.shape, s.dtype) for s in halves] + [jax.ShapeDtypeStruct(g.shape, F32)],
        scratch_shapes=[pltpu.VMEM((N_DEV, per, LANES), F32), sems, sems, sems, sems, swap_sems, swap_sems],
    )(*halves, g)
    return outs[:na], outs[na]


def _adamw_math(g, w, m, v):
    m2 = ADAM_B1 * m + (1.0 - ADAM_B1) * g
    v2 = ADAM_B2 * v + (1.0 - ADAM_B2) * (g * g)
    m_hat = m2 / (1.0 - ADAM_B1 ** ADAM_STEP)
    v_hat = v2 / (1.0 - ADAM_B2 ** ADAM_STEP)
    delta = (-ADAM_LR) * (m_hat / (jnp.sqrt(v_hat) + ADAM_EPS) + ADAM_WD * w)
    return delta, m2, v2


ADAMW_BLOCK_BYTES = 2 << 20


def _adamw_big(g, w, m, v, name):
    R, C = g.shape
    bc = min(C, max(LANES, ADAMW_BLOCK_BYTES // (4 * R) // LANES * LANES))

    def body(g_ref, w_ref, m_ref, v_ref, d_ref, m2_ref, v2_ref):
        d_ref[...], m2_ref[...], v2_ref[...] = _adamw_math(g_ref[...], w_ref[...], m_ref[...], v_ref[...])

    spec = pl.BlockSpec((R, bc), lambda j: (0, j))
    out = jax.ShapeDtypeStruct((R, C), F32)
    return _call(
        body, name=name, grid=(C // bc,),
        in_specs=[spec] * 4, out_specs=[spec] * 3, out_shape=[out] * 3,
        compiler_params=_cparams(("parallel",)),
    )(g, w, m, v)


def _adamw_small(gs, ws, ms, vs):
    n = len(gs)

    def body(*refs):
        for a in range(n):
            g_ref, w_ref, m_ref, v_ref = (refs[k * n + a] for k in range(4))
            d_ref, m2_ref, v2_ref = (refs[(4 + k) * n + a] for k in range(3))
            d_ref[...], m2_ref[...], v2_ref[...] = _adamw_math(g_ref[...], w_ref[...], m_ref[...], v_ref[...])

    outs = [jax.ShapeDtypeStruct(w.shape, F32) for w in ws]
    specs = [_const_spec(w.shape) for w in ws]
    return _call(
        body, name="adamw_small", grid=(1,),
        in_specs=specs * 4, out_specs=specs * 3, out_shape=outs * 3,
    )(*gs, *ws, *ms, *vs)


def _late_weights(st_out, st_ple, st_gate, st_conv):
    return st_out.reshape(DMIX, D), _from_chip_cols(st_ple), st_gate.reshape(D, D), _from_chip_cols(st_conv)


def _local_step(x, p, tgt, w_a, w_f, w_b, late, b_f, pre_gain, post_gain, conv_b,
                w_rgate, b_rgate, w_igate, b_igate, lam, gain_a, gain_l, ple_gain, b_gate,
                gather_late=False, early_reduce=None, w_in_reduce=None):
    b_f_pad = jnp.pad(b_f, ((0, 0), (0, LANES - H)))
    w_r = w_rgate.astype(BF16)
    w_i = w_igate.astype(BF16)

    xn, q_aug, k_aug, v_aug, g_attn, x_lru, g_lru, flb, vt_aug = _in_proj(x, pre_gain, w_a, w_f, w_b, b_f_pad)
    if gather_late:
        o, qx, stacks = _attn_fwd(q_aug, k_aug, vt_aug, late[:3], late[3:])
        late = _late_weights(*stacks)
    else:
        o, qx, _ = _attn_fwd(q_aug, k_aug, vt_aug)
    w_out_b, w_ple_b, w_gate_b, conv_w = late
    ycat, xc, h = _branches_fwd(o, g_attn, x_lru, g_lru, gain_a, gain_l, conv_w, conv_b, w_r, b_rgate, w_i, b_igate,
                                lam)
    dh1, dycat, dmix, h1b, dgp, pb, dpe, acc_t = _tail(ycat, x, p, tgt, w_out_b, post_gain, w_ple_b, ple_gain,
                                                       w_gate_b, b_gate)
    do_aug, dg_attn, dg_lru, dh, acc_b, gw_out, gw_gate, gw_ple = _branches_bwd(
        dycat, o, g_attn, h, g_lru, gain_a, gain_l, ycat, dmix, h1b, dgp, pb, dpe)
    late_grads = [gw_out, gw_ple, gw_gate]
    if early_reduce is None:
        dx_lru, gw_r, gw_i, acc_l, _ = _lru_bwd(dh, h, xc, x_lru, conv_w, w_r, b_rgate, w_i, b_igate, lam)
    else:
        parts = [gw_out.reshape(N_CHIPS, DMIX // N_CHIPS, D), _by_chip_cols(gw_ple),
                 gw_gate.reshape(N_CHIPS, D // N_CHIPS, D)]
        dx_lru, gw_r, gw_i, acc_l, got = _lru_bwd(dh, h, xc, x_lru, conv_w, w_r, b_rgate, w_i, b_igate, lam, parts)
        sent = _pair_sum(parts, got, early_reduce)
    if early_reduce is None:
        dq, dk, dv, dc_heads, _ = _attn_bwd(q_aug, qx, k_aug, v_aug, do_aug)
    else:
        dq, dk, dv, dc_heads, received = _attn_bwd(q_aug, qx, k_aug, v_aug, do_aug, sent)
        late_grads = list(zip(sent, received))
    dfl, acc_f = _fgate_bwd(dc_heads, flb)
    dz = (dq, dk, dv, dg_attn, dx_lru, dg_lru)
    grad_t = _dw_in_t(dz, dfl, xn)
    if w_in_reduce is None:
        grad_x, acc_x, _ = _dx(dz, dfl, w_a, w_f, w_b, x, pre_gain, dh1)
    else:
        sent = w_in_reduce(grad_t)
        grad_x, acc_x, (received,) = _dx(dz, dfl, w_a, w_f, w_b, x, pre_gain, dh1, [sent])
        grad_t = (sent, received)

    grads = dict(
        w_in_t=grad_t,
        w_out=late_grads[0],
        w_ple=late_grads[1],
        w_ple_gate=late_grads[2],
        w_rgate=gw_r,
        w_igate=gw_i,
        b_f=acc_f[0:1, :H],
        pre_gain=acc_x[0:1],
        post_gain=acc_t[0:1],
        conv_w=acc_l[0:4],
        conv_b=acc_l[4:5],
        b_rgate=acc_l[5:6],
        b_igate=acc_l[6:7],
        lru_lambda=acc_l[7:8],
        attn_out_gain=acc_b[0:1],
        lru_out_gain=acc_b[1:2],
        ple_gain=acc_t[1:2],
        b_ple_gate=acc_t[2:3],
    )
    loss = jnp.sum(acc_t[3])
    return loss, grad_x, grads


SMALL_ROWS = ["b_f", "pre_gain", "post_gain", "conv_w", "conv_b", "b_rgate", "b_igate", "lru_lambda",
              "attn_out_gain", "lru_out_gain", "ple_gain", "b_ple_gate"]
WEIGHTS = ["w_in", "b_f", "pre_gain", "post_gain", "conv_w", "conv_b", "w_rgate", "b_rgate", "w_igate", "b_igate",
           "lru_lambda", "attn_out_gain", "lru_out_gain", "w_out", "w_ple", "ple_gain", "w_ple_gate", "b_ple_gate"]
SHARDED = ["w_in", "w_out", "w_ple", "w_ple_gate"]


def _by_chip_cols(g):
    r, cols = g.shape
    return g.reshape(r, N_CHIPS, cols // N_CHIPS).transpose(1, 0, 2)


def _from_chip_cols(s):
    n, r, cols = s.shape
    return s.transpose(1, 0, 2).reshape(r, n * cols)


def kernel(x, p, w_in, b_f, pre_gain, post_gain, conv_w, conv_b, w_rgate, b_rgate, w_igate, b_igate, lru_lambda, attn_out_gain, lru_out_gain, w_out, w_ple, ple_gain, w_ple_gate, b_ple_gate, loss_target, m_w_in, m_b_f, m_pre_gain, m_post_gain, m_conv_w, m_conv_b, m_w_rgate, m_b_rgate, m_w_igate, m_b_igate, m_lru_lambda, m_attn_out_gain, m_lru_out_gain, m_w_out, m_w_ple, m_ple_gain, m_w_ple_gate, m_b_ple_gate, v_w_in, v_b_f, v_pre_gain, v_post_gain, v_conv_w, v_conv_b, v_w_rgate, v_b_rgate, v_w_igate, v_b_igate, v_lru_lambda, v_attn_out_gain, v_lru_out_gain, v_w_out, v_w_ple, v_ple_gain, v_w_ple_gate, v_b_ple_gate):
    w = dict(w_in=w_in, b_f=b_f, pre_gain=pre_gain, post_gain=post_gain, conv_w=conv_w, conv_b=conv_b,
             w_rgate=w_rgate, b_rgate=b_rgate, w_igate=w_igate, b_igate=b_igate, lru_lambda=lru_lambda,
             attn_out_gain=attn_out_gain, lru_out_gain=lru_out_gain, w_out=w_out, w_ple=w_ple, ple_gain=ple_gain,
             w_ple_gate=w_ple_gate, b_ple_gate=b_ple_gate)
    m = dict(w_in=m_w_in, b_f=m_b_f, pre_gain=m_pre_gain, post_gain=m_post_gain, conv_w=m_conv_w, conv_b=m_conv_b,
             w_rgate=m_w_rgate, b_rgate=m_b_rgate, w_igate=m_w_igate, b_igate=m_b_igate, lru_lambda=m_lru_lambda,
             attn_out_gain=m_attn_out_gain, lru_out_gain=m_lru_out_gain, w_out=m_w_out, w_ple=m_w_ple,
             ple_gain=m_ple_gain, w_ple_gate=m_w_ple_gate, b_ple_gate=m_b_ple_gate)
    v = dict(w_in=v_w_in, b_f=v_b_f, pre_gain=v_pre_gain, post_gain=v_post_gain, conv_w=v_conv_w, conv_b=v_conv_b,
             w_rgate=v_w_rgate, b_rgate=v_b_rgate, w_igate=v_w_igate, b_igate=v_b_igate, lru_lambda=v_lru_lambda,
             attn_out_gain=v_attn_out_gain, lru_out_gain=v_lru_out_gain, w_out=v_w_out, w_ple=v_w_ple,
             ple_gain=v_ple_gain, w_ple_gate=v_w_ple_gate, b_ple_gate=v_b_ple_gate)
    xi, yi, ci = _position()
    chip = 2 * xi + yi

    w_in_t, m_in_t, v_in_t = (jnp.swapaxes(t[0], 0, 1) for t in (w_in, m_w_in, v_w_in))
    window = jnp.pad(w_in_t.astype(BF16), ((0, W_ROWS - SHARD_ROWS), (0, 0)))

    (st_in,) = _gather_shards([window], [])
    w_a, w_f, w_b = _assemble_w_in(st_in)
    late_shards = (w_out[0].astype(BF16), w_ple[0].astype(BF16), w_ple_gate[0].astype(BF16), conv_w[0])

    loss, grad_x, g = _local_step(
        x[0], p[0, 0], loss_target[0], w_a, w_f, w_b, late_shards, b_f, pre_gain, post_gain,
        conv_b, w_rgate[0], b_rgate, w_igate[0], b_igate, lru_lambda, attn_out_gain, lru_out_gain, ple_gain,
        b_ple_gate, gather_late=True, early_reduce=ci,
        w_in_reduce=lambda grad_t: _pair_sum_windows(grad_t, _pair_exchange_windows(grad_t), ci))

    sums = [g["w_in_t"][0]] + [g[n][0] for n in SHARDED[1:]]
    recv = [g["w_in_t"][1]] + [g[n][1] for n in SHARDED[1:]]
    halves = [_chip_sum(sums[a], recv[a], chip, "chip_sum_%d" % a) for a in range(4)]

    rows = [jnp.pad(g["b_f"], ((0, 0), (0, D - H)))] + [g[n] for n in SMALL_ROWS[1:]]
    rows.append(jnp.pad(loss.reshape(1, 1), ((0, 0), (0, D - 1))))
    packed = jnp.concatenate([g["w_rgate"].reshape(NB * LANES, LANES), g["w_igate"].reshape(NB * LANES, LANES),
                              jnp.concatenate(rows, axis=0).reshape(LANES, LANES)], axis=0)
    theirs, summed = _final_exchange(halves, packed)
    full = [jnp.concatenate([jnp.where(ci == 0, a, b), jnp.where(ci == 0, b, a)], axis=0)
            for a, b in zip(halves, theirs)]
    red = dict(zip(SHARDED, full))
    red["w_in"] = lax.dynamic_slice_in_dim(red["w_in"], 2 * chip, SHARD_ROWS, axis=0)
    red["w_rgate"] = summed[:D].reshape(1, NB, LANES, LANES)
    red["w_igate"] = summed[D:2 * D].reshape(1, NB, LANES, LANES)
    vec = summed[2 * D:].reshape(16, D)
    loss = vec[15, 0]
    r0 = 0
    for n in SMALL_ROWS:
        nr = 4 if n == "conv_w" else 1
        red[n] = vec[r0:r0 + nr]
        r0 += nr
    red["b_f"] = red["b_f"][:, :H]
    red["conv_w"] = lax.dynamic_slice_in_dim(red["conv_w"], chip * (D // N_CHIPS), D // N_CHIPS, axis=1)[None]

    delta, new_m, new_v = {}, {}, {}
    outs_in = _adamw_big(red["w_in"], w_in_t, m_in_t, v_in_t, "adamw_w_in")
    delta["w_in"], new_m["w_in"], new_v["w_in"] = (jnp.swapaxes(t, 0, 1)[None] for t in outs_in)
    red["w_in"] = jnp.swapaxes(red["w_in"], 0, 1)[None]
    for n in SHARDED[1:]:
        delta[n], new_m[n], new_v[n] = (t[None] for t in _adamw_big(red[n], w[n][0], m[n][0], v[n][0], "adamw_" + n))
        red[n] = red[n][None]
    small = [n for n in WEIGHTS if n not in SHARDED]
    outs = _adamw_small([red[n] for n in small], [w[n] for n in small], [m[n] for n in small],
                        [v[n] for n in small])
    ns = len(small)
    for a, n in enumerate(small):
        delta[n], new_m[n], new_v[n] = outs[a], outs[ns + a], outs[2 * ns + a]

    return (loss, grad_x[None], *[red[n] for n in WEIGHTS], *[delta[n] for n in WEIGHTS],
            *[new_m[n] for n in WEIGHTS], *[new_v[n] for n in WEIGHTS])
```

```python
import jax
import jax.numpy as jnp
import numpy as np
from jax import lax
from jax.experimental import pallas as pl
from jax.experimental.pallas import tpu as pltpu

F32 = jnp.float32
BF16 = jnp.bfloat16

D = 1024
H = 8
DH = 128
NB = 8
DPLE = 256
DMIX = 2 * D
D_IN = 4 * D + H + 2 * D
FL0 = 3 * D
RMS_EPS = 1e-6
LRU_C = 8.0
NEG = -1e30
LANES = 128
SUBLANES = 8
BF16_ROWS = 16
COL_BLOCK = 256
SUM_BLOCK = 512
DW_TOKENS = 2048

ADAM_LR = 0.001
ADAM_B1 = 0.9
ADAM_B2 = 0.999
ADAM_EPS = 1e-08
ADAM_WD = 0.01
ADAM_STEP = 10

TM = 256
TA = 1024
TA_FWD = 1024
FWD_HEADS = 8
BWD_HEADS = 1
VMEM_BIG = 56 * 1024 * 1024
VMEM_MID = 40 * 1024 * 1024

MESH = pl.DeviceIdType.MESH
N_CHIPS = 4
N_DEV = 8


def _call(body, *, out_shape, in_hbm=True, **kwargs):
    if not in_hbm:
        return pl.pallas_call(body, out_shape=out_shape, **kwargs)

    def pin(shape):
        return pltpu.HBM(shape.shape, shape.dtype) if isinstance(shape, jax.ShapeDtypeStruct) else shape

    fn = pl.pallas_call(body, out_shape=jax.tree.map(pin, out_shape), **kwargs)

    def run(*args):
        return fn(*[a if a.dtype == jnp.int32 else pltpu.with_memory_space_constraint(a, pltpu.HBM) for a in args])

    return run


def _cparams(sem, vmem=VMEM_MID):
    return pltpu.CompilerParams(dimension_semantics=sem, vmem_limit_bytes=vmem)


def _sigmoid(x):
    return 0.5 * jnp.tanh(0.5 * x) + 0.5


def _rstd(x):
    return lax.rsqrt(jnp.mean(x * x, axis=-1, keepdims=True) + RMS_EPS)


def _rms_bwd(t, xhat, rstd):
    return rstd * (t - xhat * jnp.mean(t * xhat, axis=-1, keepdims=True))


def _dot(a, b):
    return jnp.dot(a, b, preferred_element_type=F32)


def _dot_nt(a, b):
    return lax.dot_general(a, b, (((1,), (1,)), ((), ())), preferred_element_type=F32)


def _dot_tn(a, b):
    return lax.dot_general(a, b, (((0,), (0,)), ((), ())), preferred_element_type=F32)


def _dot_exact(a, b):
    return jnp.dot(a, b, preferred_element_type=F32, precision=lax.Precision.HIGHEST)


def _shift_down(x, j, halo):
    rolled = pltpu.roll(x, j, 0)
    row = lax.broadcasted_iota(jnp.int32, halo.shape, 0)
    top = jnp.where(row < j, pltpu.roll(halo, j, 0), rolled[:SUBLANES])
    return jnp.concatenate([top, rolled[SUBLANES:]], axis=0)


def _shift_up(x, j, nxt):
    tm = x.shape[0]
    rolled = pltpu.roll(x, tm - j, 0)
    row = lax.broadcasted_iota(jnp.int32, nxt.shape, 0)
    bot = jnp.where(row >= SUBLANES - j, pltpu.roll(nxt, SUBLANES - j, 0), rolled[tm - SUBLANES:])
    return jnp.concatenate([rolled[:tm - SUBLANES], bot], axis=0)


def _scan_fwd_into(a, u, carry, h_ref):
    tm, width = a.shape
    groups = (tm // SUBLANES, SUBLANES, width)
    a, u = a.reshape(groups), u.reshape(groups)
    sub = lax.broadcasted_iota(jnp.int32, groups, 1)
    d = 1
    while d < SUBLANES:
        keep = sub >= d
        a_s = jnp.where(keep, pltpu.roll(a, d, 1), 1.0)
        u_s = jnp.where(keep, pltpu.roll(u, d, 1), 0.0)
        u = u + a * u_s
        a = a * a_s
        d *= 2
    a, u = a.reshape(tm, width), u.reshape(tm, width)
    for g in range(tm // SUBLANES):
        rows = slice(g * SUBLANES, (g + 1) * SUBLANES)
        h_ref[rows, :] = u[rows] + a[rows] * carry
        carry = h_ref[(g + 1) * SUBLANES - 1:(g + 1) * SUBLANES, :]
    return carry


def _scan_bwd_into(b, u, g_ref):
    tm, width = b.shape
    groups = (tm // SUBLANES, SUBLANES, width)
    b, u = b.reshape(groups), u.reshape(groups)
    sub = lax.broadcasted_iota(jnp.int32, groups, 1)
    d = 1
    while d < SUBLANES:
        keep = sub < SUBLANES - d
        b_s = jnp.where(keep, pltpu.roll(b, SUBLANES - d, 1), 1.0)
        u_s = jnp.where(keep, pltpu.roll(u, SUBLANES - d, 1), 0.0)
        u = u + b * u_s
        b = b * b_s
        d *= 2
    b, u = b.reshape(tm, width), u.reshape(tm, width)
    nxt = jnp.zeros((1, width), F32)
    for g in reversed(range(tm // SUBLANES)):
        rows = slice(g * SUBLANES, (g + 1) * SUBLANES)
        g_ref[rows, :] = u[rows] + b[rows] * nxt
        nxt = g_ref[g * SUBLANES:g * SUBLANES + 1, :]


def _gate_pre(xc, w_ref):
    outs = []
    for n in range(NB):
        outs.append(_dot(xc[:, n * LANES:(n + 1) * LANES].astype(BF16), w_ref[n]))
    return jnp.concatenate(outs, axis=1)


def _gate_pre_t(d, w_ref):
    outs = []
    for n in range(NB):
        outs.append(_dot_nt(d[:, n * LANES:(n + 1) * LANES].astype(BF16), w_ref[n]))
    return jnp.concatenate(outs, axis=1)


def _softplus_neg(lam):
    return jnp.maximum(-lam, 0.0) + jnp.log(1.0 + jnp.exp(-jnp.abs(lam)))


def _row_spec(tm, width):
    return pl.BlockSpec((tm, width), lambda i: (i, 0))


def _const_spec(shape):
    nd = len(shape)
    return pl.BlockSpec(shape, lambda *_: (0,) * nd)


def _weight_spec(shape):
    nd = len(shape)
    return pl.BlockSpec(shape, lambda *_: (0,) * nd, pipeline_mode=pl.Buffered(1))


AUG = 2 * DH
LOG2E = 1.4426950408889634
LN2 = 0.6931471805599453
Q_SCALE = DH ** -0.5 * LOG2E


def _split3(x):
    hi = x.astype(BF16)
    r1 = x - hi.astype(F32)
    mid = r1.astype(BF16)
    lo = (r1 - mid.astype(F32)).astype(BF16)
    return hi, mid, lo


def _extras(col, ones_from):
    t = col.shape[0]
    hi, mid, lo = _split3(jnp.broadcast_to(col, (t, LANES)))
    lane = lax.broadcasted_iota(jnp.int32, (t, LANES), 1)
    rest = jnp.zeros((t, LANES), BF16)
    if ones_from is not None:
        rest = jnp.where((lane >= ones_from) & (lane < ones_from + 3), 1.0, 0.0).astype(BF16)
    return jnp.where(lane == 0, hi, jnp.where(lane == 1, mid, jnp.where(lane == 2, lo, rest)))


def _selectors():
    sel_q = np.zeros((3 * LANES, H * LANES), np.float32)
    sel_k = np.zeros((3 * LANES, H * LANES), np.float32)
    for hd in range(H):
        for piece in range(3):
            sel_q[piece * LANES + hd, hd * LANES + piece] = 1.0
            sel_k[piece * LANES + hd, hd * LANES + 3 + piece] = -1.0
    return jnp.asarray(sel_q, BF16), jnp.asarray(sel_k, BF16)


def _in_proj(x, pre_gain, w_a, w_f, w_b, b_f_pad):
    T = x.shape[0]
    tm = TM
    sel_q, sel_k = _selectors()

    def body(x_ref, g_ref, wa_ref, wf_ref, wb_ref, bf_ref, sq_ref, sk_ref,
             xn_ref, qa_ref, ka_ref, va_ref, ga_ref, xl_ref, gl_ref, flb_ref, vt_ref, c_s, carry):
        @pl.when(pl.program_id(0) == 0)
        def _():
            carry[...] = jnp.zeros_like(carry)

        xv = x_ref[...]
        xn = (xv * _rstd(xv) * g_ref[...]).astype(BF16)
        xn_ref[...] = xn
        for s, o_ref in enumerate((ga_ref, xl_ref, gl_ref)):
            o_ref[...] = _dot_nt(xn, wb_ref[s * D:(s + 1) * D, :]).astype(o_ref.dtype)
        flb = _dot_nt(xn, wf_ref[...]) + bf_ref[...]
        flb_ref[...] = flb
        lane = lax.broadcasted_iota(jnp.int32, flb.shape, 1)
        ls = jnp.where(lane < H, jnp.minimum(flb, 0.0) - jnp.log(1.0 + jnp.exp(-jnp.abs(flb))), 0.0)
        r = lax.broadcasted_iota(jnp.int32, (tm, tm), 0)
        c = lax.broadcasted_iota(jnp.int32, (tm, tm), 1)
        cs = _dot_exact((c <= r).astype(F32), ls) + carry[...]
        c_s[...] = cs
        carry[...] = c_s[tm - 1:tm, :]

        pieces = jnp.concatenate(_split3(cs * LOG2E), axis=1)
        ones_q = jnp.where((lane >= 3) & (lane < 6), 1.0, 0.0)
        ones_k = jnp.where(lane < 3, 1.0, 0.0)
        zq = _dot_nt(xn, wa_ref[0:D, :]) * Q_SCALE
        zk = _dot_nt(xn, wa_ref[D:2 * D, :])
        zv = _dot_nt(xn, wa_ref[2 * D:3 * D, :])
        ex_q = _dot(pieces, sq_ref[...])
        ex_k = _dot(pieces, sk_ref[...])
        for hd in range(H):
            head = slice(hd * DH, (hd + 1) * DH)
            lo, hi = hd * AUG, hd * AUG + DH
            qa_ref[:, lo:hi] = zq[:, head].astype(BF16)
            qa_ref[:, hi:hi + DH] = (ex_q[:, head] + ones_q).astype(BF16)
            ka_ref[:, lo:hi] = zk[:, head].astype(BF16)
            ka_ref[:, hi:hi + DH] = (ex_k[:, head] + ones_k).astype(BF16)
            va_ref[:, lo:hi] = zv[:, head].astype(BF16)
            va_ref[:, hi:hi + DH] = ones_k.astype(BF16)
            vt_ref[lo:hi, :] = jnp.transpose(zv[:, head]).astype(BF16)
            vt_ref[hi:hi + DH, :] = jnp.where(lax.broadcasted_iota(jnp.int32, (DH, tm), 0) < 3, 1.0, 0.0).astype(BF16)

    bf = jax.ShapeDtypeStruct((T, D), BF16)
    aug = jax.ShapeDtypeStruct((T, H * AUG), BF16)
    f32 = jax.ShapeDtypeStruct((T, D), F32)
    sel_spec = _const_spec((3 * LANES, H * LANES))
    return _call(
        body, name="in_proj", grid=(T // tm,),
        in_specs=[_row_spec(tm, D), _const_spec((1, D)), _const_spec((3 * D, D)), _const_spec((LANES, D)),
                  _const_spec((3 * D, D)), _const_spec((1, LANES)), sel_spec, sel_spec],
        out_specs=[_row_spec(tm, D)] + [_row_spec(tm, H * AUG)] * 3 + [_row_spec(tm, D)] * 3 + [_row_spec(tm, LANES)]
        + [pl.BlockSpec((H * AUG, tm), lambda i: (0, i))],
        out_shape=[bf, aug, aug, aug, f32, f32, f32, jax.ShapeDtypeStruct((T, LANES), F32),
                   jax.ShapeDtypeStruct((H * AUG, T), BF16)],
        scratch_shapes=[pltpu.VMEM((tm, LANES), F32), pltpu.VMEM((1, LANES), F32)],
        compiler_params=_cparams(("arbitrary",), VMEM_BIG),
    )(x, pre_gain, w_a, w_f, w_b, b_f_pad, sel_q, sel_k)


def _causal_pairs(n, q_major):
    if q_major:
        pairs = [(qi, ki) for qi in range(n) for ki in range(qi + 1)]
    else:
        pairs = [(ki, qi) for ki in range(n) for qi in range(ki, n)]
    return (jnp.asarray([a for a, _ in pairs], jnp.int32), jnp.asarray([b for _, b in pairs], jnp.int32))


def _attn_fwd(q_aug, k_aug, vt_aug, shards=(), whole=()):
    T = q_aug.shape[0]
    t = min(T, TA_FWD)
    n = T // t
    hp = FWD_HEADS
    heads = range(hp)
    qi_tab, ki_tab = _causal_pairs(n, q_major=True)
    na, nall = len(shards), len(shards) + len(whole)
    n_h, n_j = H // hp, qi_tab.shape[0]

    def body(qi_ref, ki_ref, q_ref, k_ref, vt_ref, *rest):
        srcs, rest = rest[:nall], rest[nall:]
        o_ref, qx_ref = rest[:2]
        dsts, rest = rest[2:2 + nall], rest[2 + nall:]
        m_s, acc_s = rest[:2]
        h = pl.program_id(0)
        j = pl.program_id(1)
        qi = qi_ref[j]
        ki = ki_ref[j]

        if nall:
            gather = _GatherPlan(srcs, dsts, rest[2:], na)
            step = h * n_j + j
            pl.when(step == 0)(gather.send)
            pl.when(step == n_h * n_j // 2)(gather.forward)
            pl.when(step == n_h * n_j - 1)(gather.finish)

        @pl.when(ki == 0)
        def _():
            m_s[...] = jnp.full(m_s.shape, NEG, F32)
            acc_s[...] = jnp.zeros_like(acc_s)

        def step(on_diagonal):
            cols = [slice(a * AUG, (a + 1) * AUG) for a in heads]
            if on_diagonal:
                krow = lax.broadcasted_iota(jnp.int32, (t, t), 0)
                qcol = lax.broadcasted_iota(jnp.int32, (t, t), 1)
            def logits(a):
                st = _dot_nt(k_ref[:, cols[a]], q_ref[:, cols[a]])
                return jnp.where(krow <= qcol, st, NEG) if on_diagonal else st

            st_next = logits(0)
            for a in heads:
                st = st_next
                if a + 1 < hp:
                    st_next = logits(a + 1)
                m_prev = m_s[a]
                m_new = jnp.maximum(m_prev, jnp.max(st, axis=0, keepdims=True))
                pt = jnp.exp2(st - m_new).astype(BF16)
                acc_s[a] = jnp.exp2(m_prev - m_new) * acc_s[a] + _dot(vt_ref[cols[a], :], pt)
                m_s[a] = m_new

        @pl.when(ki < qi)
        def _():
            step(False)

        @pl.when(ki == qi)
        def _():
            step(True)
            piece = lax.broadcasted_iota(jnp.int32, (DH, t), 0)
            for a in heads:
                l = acc_s[a, DH:DH + 1, :]
                ex = jnp.transpose(q_ref[:, a * AUG + DH:(a + 1) * AUG].astype(F32))
                c2 = jnp.sum(jnp.where(piece < 3, ex, 0.0), axis=0, keepdims=True)
                hi, mid, lo = _split3(jnp.broadcast_to(c2 - (m_s[a] + jnp.log(l) * LOG2E), (DH, t)))
                ones = jnp.where((piece >= 3) & (piece < 6), 1.0, 0.0).astype(BF16)
                ex_t = jnp.where(piece == 0, hi, jnp.where(piece == 1, mid, jnp.where(piece == 2, lo, ones)))
                o_ref[:, a * DH:(a + 1) * DH] = jnp.transpose(acc_s[a, :DH, :] / l)
                qx_ref[:, a * DH:(a + 1) * DH] = jnp.transpose(ex_t.astype(F32)).astype(BF16)

    q_spec = pl.BlockSpec((t, hp * AUG), lambda h, j, qi_ref, ki_ref: (qi_ref[j], h))
    k_spec = pl.BlockSpec((t, hp * AUG), lambda h, j, qi_ref, ki_ref: (ki_ref[j], h))
    vt_spec = pl.BlockSpec((hp * AUG, t), lambda h, j, qi_ref, ki_ref: (h, ki_ref[j]))
    out_spec = pl.BlockSpec((t, hp * DH), lambda h, j, qi_ref, ki_ref: (qi_ref[j], h))
    arrs = list(shards) + list(whole)
    grid_spec = pltpu.PrefetchScalarGridSpec(
        num_scalar_prefetch=2, grid=(n_h, n_j),
        in_specs=[q_spec, k_spec, vt_spec] + [HBM_SPEC] * nall, out_specs=[out_spec, out_spec] + [HBM_SPEC] * nall,
        scratch_shapes=[pltpu.VMEM((hp, 1, t), F32), pltpu.VMEM((hp, AUG, t), F32)]
        + (_gather_semaphores(na, nall) if nall else []))
    outs = _call(
        body, name="attn_fwd", grid_spec=grid_spec,
        out_shape=[jax.ShapeDtypeStruct((T, D), F32), jax.ShapeDtypeStruct((T, D), BF16)] + _gather_out_shapes(arrs),
        compiler_params=_cparams(("arbitrary", "arbitrary"), VMEM_BIG),
    )(qi_tab, ki_tab, q_aug, k_aug, vt_aug, *arrs)
    return outs[0], outs[1], _place_own(outs[2:], arrs)


def _sigmoid_small(x):
    e = jnp.exp(x)
    return jnp.where(x < -8.0, e - e * e, _sigmoid(x))


def _lru_gates(xc, wr_ref, br_ref, wi_ref, bi_ref, lam_ref):
    r = _sigmoid_small(_gate_pre(xc, wr_ref) + br_ref[...])
    ig = _sigmoid(_gate_pre(xc, wi_ref) + bi_ref[...])
    sp = _softplus_neg(lam_ref[...])
    la = (-LRU_C) * r * sp
    a = jnp.exp(la)
    y = -jnp.tanh(la) * (a * a + 1.0)
    return r, ig, sp, a, jnp.sqrt(y), lax.rsqrt(y)


def _branches_fwd(o, g_attn, x_lru, g_lru, gain_a, gain_l, conv_w, conv_b, w_r, b_r, w_i, b_i, lam):
    T = o.shape[0]
    tm = TM

    def body(o_ref, ga_ref, xl_ref, gl_ref, gna_ref, gnl_ref, cw_ref, cb_ref, wr_ref, br_ref, wi_ref, bi_ref,
             lam_ref, ycat_ref, xc_ref, h_ref, halo_s, hc_s):
        @pl.when(pl.program_id(0) == 0)
        def _():
            halo_s[...] = jnp.zeros_like(halo_s)
            hc_s[...] = jnp.zeros_like(hc_s)

        ov = o_ref[...]
        ga = ga_ref[...]
        ya = ov * _rstd(ov) * gna_ref[...] * (ga * _sigmoid(ga))
        ycat_ref[:, :D] = ya.astype(BF16)

        xl = xl_ref[...]
        halo = halo_s[...]
        xc = xl * cw_ref[3:4, :] + cb_ref[...]
        for j in range(3):
            xc = xc + _shift_down(xl, 3 - j, halo) * cw_ref[j:j + 1, :]
        halo_s[...] = xl_ref[tm - SUBLANES:tm, :]
        xc_ref[...] = xc

        _, ig, _, a, sq, _ = _lru_gates(xc, wr_ref, br_ref, wi_ref, bi_ref, lam_ref)
        u = sq * (ig * xc)
        hc_s[...] = _scan_fwd_into(a, u, hc_s[...], h_ref)
        hh = h_ref[...]

        gl = gl_ref[...]
        yl = hh * _rstd(hh) * gnl_ref[...] * (gl * _sigmoid(gl))
        ycat_ref[:, D:] = yl.astype(BF16)

    vec = _const_spec((1, D))
    wspec = _const_spec((NB, LANES, LANES))
    return _call(
        body, name="branches_fwd", grid=(T // tm,),
        in_specs=[_row_spec(tm, D)] * 4 + [vec, vec, _const_spec((4, D)), vec, wspec, vec, wspec, vec, vec],
        out_specs=[_row_spec(tm, DMIX), _row_spec(tm, D), _row_spec(tm, D)],
        out_shape=[jax.ShapeDtypeStruct((T, DMIX), BF16), jax.ShapeDtypeStruct((T, D), F32),
                   jax.ShapeDtypeStruct((T, D), F32)],
        scratch_shapes=[pltpu.VMEM((SUBLANES, D), F32), pltpu.VMEM((1, D), F32)],
        compiler_params=_cparams(("arbitrary",)),
    )(o, g_attn, x_lru, g_lru, gain_a, gain_l, conv_w, conv_b, w_r, b_r, w_i, b_i, lam)


def _tail(ycat, x, p, tgt, w_out, post_gain, w_ple, ple_gain, w_gate, b_gate):
    T = x.shape[0]
    tm = TM

    def body(ycat_ref, x_ref, p_ref, t_ref, wo_ref, pg_ref, wp_ref, eg_ref, wg_ref, bg_ref,
             dh1_ref, dycat_ref, dmix_ref, h1b_ref, dgp_ref, pb_ref, dpe_ref, acc_ref):
        @pl.when(pl.program_id(0) == 0)
        def _():
            acc_ref[...] = jnp.zeros_like(acc_ref)

        mix = _dot(ycat_ref[...], wo_ref[...])
        rstd_m = _rstd(mix)
        mhat = mix * rstd_m
        h1 = x_ref[...] + mhat * pg_ref[...]
        pb = p_ref[...].astype(BF16)
        pb_ref[...] = pb
        pe = _dot(pb, wp_ref[...])
        rstd_p = _rstd(pe)
        pehat = pe * rstd_p
        e = pehat * eg_ref[...]
        h1b = h1.astype(BF16)
        h1b_ref[...] = h1b
        gate = _sigmoid(_dot(h1b, wg_ref[...]) + bg_ref[...])
        diff = (h1 + gate * e) - t_ref[...]

        dy = diff * (1.0 / D)
        de = dy * gate
        dgp = (dy * e) * gate * (1.0 - gate)
        dgpb = dgp.astype(BF16)
        dgp_ref[...] = dgpb
        dh1 = dy + _dot_nt(dgpb, wg_ref[...])
        dh1_ref[...] = dh1
        dpe_ref[...] = _rms_bwd(de * eg_ref[...], pehat, rstd_p).astype(BF16)
        dmix = _rms_bwd(dh1 * pg_ref[...], mhat, rstd_m).astype(BF16)
        dmix_ref[...] = dmix
        dycat_ref[...] = _dot_nt(dmix, wo_ref[...])

        acc_ref[0:1, :] += jnp.sum(dh1 * mhat, axis=0, keepdims=True)
        acc_ref[1:2, :] += jnp.sum(de * pehat, axis=0, keepdims=True)
        acc_ref[2:3, :] += jnp.sum(dgp, axis=0, keepdims=True)
        acc_ref[3:4, :] += jnp.sum(diff * diff, axis=0, keepdims=True) * (0.5 / D)

    vec = _const_spec((1, D))
    bf = jax.ShapeDtypeStruct((T, D), BF16)
    return _call(
        body, name="tail", grid=(T // tm,),
        in_specs=[_row_spec(tm, DMIX), _row_spec(tm, D), _row_spec(tm, DPLE), _row_spec(tm, D),
                  _const_spec((DMIX, D)), vec, _const_spec((DPLE, D)), vec, _const_spec((D, D)), vec],
        out_specs=[_row_spec(tm, D), _row_spec(tm, DMIX), _row_spec(tm, D), _row_spec(tm, D), _row_spec(tm, D),
                   _row_spec(tm, DPLE), _row_spec(tm, D), _const_spec((SUBLANES, D))],
        out_shape=[jax.ShapeDtypeStruct((T, D), F32), jax.ShapeDtypeStruct((T, DMIX), F32), bf, bf, bf,
                   jax.ShapeDtypeStruct((T, DPLE), BF16), bf, jax.ShapeDtypeStruct((SUBLANES, D), F32)],
        compiler_params=_cparams(("arbitrary",), VMEM_BIG),
    )(ycat, x, p, tgt, w_out, post_gain, w_ple, ple_gain, w_gate, b_gate)


def _pair_copies(srcs, gots, send_sems, recv_sems):
    x, y, c = _position()
    copies = []
    for a, (src, got) in enumerate(zip(srcs, gots)):
        half = src.shape[1] // 2
        rows = pl.ds(pl.multiple_of((1 - c) * half, SUBLANES), half)
        copies.append(pltpu.make_async_remote_copy(
            src_ref=src.at[:, rows, :], dst_ref=got, send_sem=send_sems.at[a], recv_sem=recv_sems.at[a],
            device_id=(x, y, 1 - c), device_id_type=MESH))
    return copies


def _branches_bwd(dycat, o, g_attn, h, g_lru, gain_a, gain_l, ycat, dmix, h1b, dgp, pb, dpe):
    T = o.shape[0]
    tm = TM
    nt = T // tm
    nb_gate, nb_ple = min(nt, 8), min(nt, 2)
    ns_gate, ns_ple = nt // nb_gate, nt // nb_ple
    br_out, br_gate, br_ple = DMIX // nt, D // nb_gate, DPLE // nb_ple
    tk_gate, tk_ple = T // ns_gate, T // ns_ple

    def body(dy_ref, o_ref, ga_ref, h_ref, gl_ref, gna_ref, gnl_ref, yc_ref, dmix_ref, h1_ref, dgp_ref, pb_ref,
             dpe_ref, do_ref, dga_ref, dgl_ref, dh_ref, acc_ref, gwo_ref, gwg_ref, gwp_ref):
        i = pl.program_id(0)

        @pl.when(i == 0)
        def _():
            acc_ref[...] = jnp.zeros_like(acc_ref)

        def accumulate(out_ref, lhs_ref, rhs_ref, tokens, slices):
            s = i % slices
            part = _dot_tn(lhs_ref[...], rhs_ref[pl.ds(pl.multiple_of(s * tokens, tokens), tokens), :])
            out_ref[...] = part + jnp.where(s == 0, 0.0, out_ref[...])

        gwo_ref[...] = _dot_tn(yc_ref[...], dmix_ref[...])

        def branch(val, g, gain, dyv):
            rstd = _rstd(val)
            vhat = val * rstd
            sig = _sigmoid(g)
            dn = dyv * (g * sig)
            dg = dyv * (vhat * gain) * (sig * (1.0 + g * (1.0 - sig)))
            dgain = jnp.sum(dn * vhat, axis=0, keepdims=True)
            return _rms_bwd(dn * gain, vhat, rstd), dg, dgain

        ov = o_ref[...]
        do, dga, dgain_a = branch(ov, ga_ref[...], gna_ref[...], dy_ref[:, :D])
        dga_ref[...] = dga.astype(BF16)
        prod = do * ov
        for hd in range(H):
            head = slice(hd * DH, (hd + 1) * DH)
            do_ref[:, hd * AUG:hd * AUG + DH] = do[:, head].astype(BF16)
            do_ref[:, hd * AUG + DH:(hd + 1) * AUG] = _extras(-jnp.sum(prod[:, head], axis=1, keepdims=True), None)

        accumulate(gwg_ref, h1_ref, dgp_ref, tk_gate, ns_gate)
        accumulate(gwp_ref, pb_ref, dpe_ref, tk_ple, ns_ple)
        dh, dgl, dgain_l = branch(h_ref[...], gl_ref[...], gnl_ref[...], dy_ref[:, D:])
        dh_ref[...] = dh
        dgl_ref[...] = dgl.astype(BF16)
        acc_ref[0:1, :] += dgain_a
        acc_ref[1:2, :] += dgain_l

    vec = _const_spec((1, D))
    bf = jax.ShapeDtypeStruct((T, D), BF16)
    tokens = _weight_spec((T, D))
    return _call(
        body, name="branches_bwd", grid=(nt,),
        in_specs=[_row_spec(tm, DMIX)] + [_row_spec(tm, D)] * 4 + [vec, vec]
        + [pl.BlockSpec((T, br_out), lambda i: (0, i)), tokens,
           pl.BlockSpec((tk_gate, br_gate), lambda i: (i % ns_gate, i // ns_gate)), tokens,
           pl.BlockSpec((tk_ple, br_ple), lambda i: (i % ns_ple, i // ns_ple)), tokens],
        out_specs=[_row_spec(tm, H * AUG), _row_spec(tm, D), _row_spec(tm, D), _row_spec(tm, D),
                   _const_spec((SUBLANES, D)),
                   pl.BlockSpec((br_out, D), lambda i: (i, 0)),
                   pl.BlockSpec((br_gate, D), lambda i: (i // ns_gate, 0)),
                   pl.BlockSpec((br_ple, D), lambda i: (i // ns_ple, 0))],
        out_shape=[jax.ShapeDtypeStruct((T, H * AUG), BF16), bf, bf, jax.ShapeDtypeStruct((T, D), F32),
                   jax.ShapeDtypeStruct((SUBLANES, D), F32), jax.ShapeDtypeStruct((DMIX, D), F32),
                   jax.ShapeDtypeStruct((D, D), F32), jax.ShapeDtypeStruct((DPLE, D), F32)],
        compiler_params=_cparams(("arbitrary",), VMEM_BIG),
    )(dycat, o, g_attn, h, g_lru, gain_a, gain_l, ycat, dmix, h1b, dgp, pb, dpe)


def _lru_bwd(dh, h, xc, x_lru, conv_w, w_r, b_r, w_i, b_i, lam, pair_parts=()):
    T = dh.shape[0]
    tm = TM
    nt = T // tm
    per = tm // SUBLANES
    npair = len(pair_parts)

    def body(dh_ref, h_ref, hprev_ref, xc_ref, xl_ref, cw_ref, wr_ref, br_ref, wi_ref, bi_ref, lam_ref, *rest):
        parts, rest = rest[:npair], rest[npair:]
        dxl_ref, dwr_ref, dwi_ref, acc_ref = rest[:4]
        gots, rest = rest[4:4 + npair], rest[4 + npair:]
        carry_s, dxc_next_s, top_s, dht_s = rest[:4]
        i = pl.program_id(0)

        @pl.when(i == 0)
        def _():
            acc_ref[...] = jnp.zeros_like(acc_ref)
            dwr_ref[...] = jnp.zeros_like(dwr_ref)
            dwi_ref[...] = jnp.zeros_like(dwi_ref)
            carry_s[...] = jnp.zeros_like(carry_s)
            dxc_next_s[...] = jnp.zeros_like(dxc_next_s)
            for cp in _pair_copies(parts, gots, *rest[4:]) if npair else ():
                cp.start()

        if npair:
            @pl.when(i == nt - 1)
            def _():
                for cp in _pair_copies(parts, gots, *rest[4:]):
                    cp.wait()

        inner = jnp.where(i == nt - 1, 0.0, 1.0)
        xc = xc_ref[...]
        r, ig, sp, a, sq, inv_sq = _lru_gates(xc, wr_ref, br_ref, wi_ref, bi_ref, lam_ref)

        row = lax.broadcasted_iota(jnp.int32, (tm, D), 0)
        u = dh_ref[...] + jnp.where(row == tm - 1, carry_s[...], 0.0)
        _scan_bwd_into(pltpu.roll(a, tm - 1, 0), u, dht_s)
        dht = dht_s[...]
        top_s[...] = a[:SUBLANES, :] * dht[:SUBLANES, :]
        carry_s[...] = top_s[0:1, :]

        hprev = hprev_ref[...] * inner
        da = dht * _shift_down(h_ref[...], 1, hprev)
        dig = dht * sq * xc
        dxc = dht * sq * ig
        dsq = dht * ig * xc
        dla = da * a - dsq * (a * a) * inv_sq
        dr = dla * ((-LRU_C) * sp)
        dpr = dr * r * (1.0 - r)
        dpi = dig * ig * (1.0 - ig)
        for n in range(NB):
            blk = slice(n * LANES, (n + 1) * LANES)
            xcb = xc[:, blk].astype(BF16)
            dwr_ref[n] += _dot_tn(xcb, dpr[:, blk].astype(BF16))
            dwi_ref[n] += _dot_tn(xcb, dpi[:, blk].astype(BF16))
        dxc = dxc + _gate_pre_t(dpr, wr_ref) + _gate_pre_t(dpi, wi_ref)

        xl = xl_ref[...]
        nxt = dxc_next_s[...]
        dxl = dxc * cw_ref[3:4, :]
        acc_ref[3:4, :] += jnp.sum(dxc * xl, axis=0, keepdims=True)
        for j in range(3):
            ahead = _shift_up(dxc, 3 - j, nxt)
            dxl = dxl + ahead * cw_ref[j:j + 1, :]
            acc_ref[j:j + 1, :] += jnp.sum(ahead * xl, axis=0, keepdims=True)
        dxc_next_s[...] = dxc[:SUBLANES, :]
        dxl_ref[...] = dxl.astype(BF16)

        acc_ref[4:5, :] += jnp.sum(dxc, axis=0, keepdims=True)
        acc_ref[5:6, :] += jnp.sum(dpr, axis=0, keepdims=True)
        acc_ref[6:7, :] += jnp.sum(dpi, axis=0, keepdims=True)
        acc_ref[7:8, :] += jnp.sum(dla * ((-LRU_C) * r), axis=0, keepdims=True)

        @pl.when(i == nt - 1)
        def _():
            lam_v = lam_ref[...]
            acc_ref[7:8, :] = acc_ref[7:8, :] * (-_sigmoid(-lam_v))

    rev = pl.BlockSpec((tm, D), lambda i: (nt - 1 - i, 0))
    prev8 = pl.BlockSpec((SUBLANES, D), lambda i: (jnp.maximum((nt - 1 - i) * per - 1, 0), 0))
    vec = _const_spec((1, D))
    wspec = _const_spec((NB, LANES, LANES))
    bf = jax.ShapeDtypeStruct((T, D), BF16)
    halves = [jax.ShapeDtypeStruct((s.shape[0], s.shape[1] // 2, s.shape[2]), s.dtype) for s in pair_parts]
    outs = _call(
        body, name="lru_bwd", grid=(nt,),
        in_specs=[rev, rev, prev8, rev, rev, _const_spec((4, D)), wspec, vec, wspec, vec, vec] + [HBM_SPEC] * npair,
        out_specs=[rev, wspec, wspec, _const_spec((SUBLANES, D))] + [HBM_SPEC] * npair,
        out_shape=[bf, jax.ShapeDtypeStruct((NB, LANES, LANES), F32), jax.ShapeDtypeStruct((NB, LANES, LANES), F32),
                   jax.ShapeDtypeStruct((SUBLANES, D), F32)] + halves,
        scratch_shapes=[pltpu.VMEM((1, D), F32), pltpu.VMEM((SUBLANES, D), F32), pltpu.VMEM((SUBLANES, D), F32),
                        pltpu.VMEM((tm, D), F32)]
        + ([pltpu.SemaphoreType.DMA((npair,)), pltpu.SemaphoreType.DMA((npair,))] if npair else []),
        compiler_params=_cparams(("arbitrary",)),
    )(dh, h, h, xc, x_lru, conv_w, w_r, b_r, w_i, b_i, lam, *pair_parts)
    return (*outs[:4], list(outs[4:]))


def _chip_copies(srcs, dsts, send_sems, recv_sems):
    x, y, c = _position()
    chip = 2 * x + y
    na = len(srcs)
    return [pltpu.make_async_remote_copy(
        src_ref=srcs[a].at[2 * px + py], dst_ref=dsts[a].at[chip], send_sem=send_sems.at[j * na + a],
        recv_sem=recv_sems.at[j * na + a], device_id=(px, py, c), device_id_type=MESH)
        for j, (px, py) in enumerate(_other_chips(x, y)) for a in range(na)]


def _attn_bwd(q_aug, qx, k_aug, v_aug, do_aug, exchange=()):
    T = q_aug.shape[0]
    t = TA
    n = T // t
    hp = BWD_HEADS
    heads = range(hp)
    scale = DH ** -0.5
    ki_tab, qi_tab = _causal_pairs(n, q_major=False)
    last = ki_tab.shape[0] - 1
    ne = len(exchange)
    n_h = H // hp

    def body(ki_ref, qi_ref, q_ref, qx_ref, k_ref, v_ref, do_ref, *rest):
        sent, rest = rest[:ne], rest[ne:]
        dq_ref, dk_ref, dv_ref, dc_ref = rest[:4]
        received, rest = rest[4:4 + ne], rest[4 + ne:]
        dq_s, dk_s, dv_s = rest[:3]
        j = pl.program_id(1)
        ki = ki_ref[j]
        qi = qi_ref[j]

        if ne:
            first_step = (pl.program_id(0) == 0) & (j == 0)
            last_step = (pl.program_id(0) == n_h - 1) & (j == last)

            @pl.when(first_step)
            def _():
                for cp in _chip_copies(sent, received, *rest[3:]):
                    cp.start()

            @pl.when(last_step)
            def _():
                for cp in _chip_copies(sent, received, *rest[3:]):
                    cp.wait()

        @pl.when(j == 0)
        def _():
            dq_s[...] = jnp.zeros_like(dq_s)

        @pl.when(qi == ki)
        def _():
            dk_s[...] = jnp.zeros_like(dk_s)
            dv_s[...] = jnp.zeros_like(dv_s)

        def step(on_diagonal):
            cols = [slice(a * AUG, (a + 1) * AUG) for a in heads]
            qb = [jnp.concatenate([q_ref[:, a * AUG:a * AUG + DH], qx_ref[:, a * DH:(a + 1) * DH]], axis=1)
                  for a in heads]
            if on_diagonal:
                krow = lax.broadcasted_iota(jnp.int32, (t, t), 0)
                qcol = lax.broadcasted_iota(jnp.int32, (t, t), 1)

            def scores(a):
                st = _dot_nt(k_ref[:, cols[a]], qb[a])
                dpd = _dot_nt(v_ref[:, cols[a]], do_ref[:, cols[a]])
                return (jnp.where(krow <= qcol, st, NEG) if on_diagonal else st), dpd

            off = pl.multiple_of(qi * t, t)
            ahead = scores(0)
            for a in heads:
                st, dpd = ahead
                if a + 1 < hp:
                    ahead = scores(a + 1)
                pt = jnp.exp2(st)
                dsb = (pt * dpd).astype(BF16)
                dv_s[a] += _dot(pt.astype(BF16), do_ref[:, a * AUG:a * AUG + DH])
                dk_s[a] += _dot(dsb, qb[a])
                dq_s[a, pl.ds(off, t), :] += _dot_tn(dsb, k_ref[:, cols[a]])

        @pl.when(qi > ki)
        def _():
            step(False)

        @pl.when(qi == ki)
        def _():
            step(True)

        @pl.when(qi == n - 1)
        def _():
            rows = pl.ds(pl.multiple_of(ki * t, t), t)
            for a in heads:
                dk_ref[:, a * DH:(a + 1) * DH] = (dk_s[a, :, :DH] * LN2).astype(BF16)
                dv_ref[:, a * DH:(a + 1) * DH] = dv_s[a].astype(BF16)
                dc_ref[a, rows, :] = jnp.broadcast_to(-dk_s[a, :, DH + 3:DH + 4], (t, LANES))

        @pl.when(j == last)
        def _():
            for a in heads:
                dq_ref[:, a * DH:(a + 1) * DH] = (dq_s[a, :, :DH] * scale).astype(BF16)
                dc_ref[a] = dc_ref[a] + jnp.broadcast_to(dq_s[a, :, DH:DH + 1], (T, LANES))

    qside = pl.BlockSpec((t, hp * AUG), lambda h, j, ki_ref, qi_ref: (qi_ref[j], h))
    qxside = pl.BlockSpec((t, hp * DH), lambda h, j, ki_ref, qi_ref: (qi_ref[j], h))
    kside = pl.BlockSpec((t, hp * AUG), lambda h, j, ki_ref, qi_ref: (ki_ref[j], h))
    kout = pl.BlockSpec((t, hp * DH), lambda h, j, ki_ref, qi_ref: (ki_ref[j], h))
    bf = jax.ShapeDtypeStruct((T, D), BF16)
    sums = jax.ShapeDtypeStruct((H, T, LANES), F32)
    grid_spec = pltpu.PrefetchScalarGridSpec(
        num_scalar_prefetch=2, grid=(n_h, ki_tab.shape[0]),
        in_specs=[qside, qxside, kside, kside, qside] + [HBM_SPEC] * ne,
        out_specs=[pl.BlockSpec((T, hp * DH), lambda h, j, ki_ref, qi_ref: (0, h)), kout, kout,
                   pl.BlockSpec((hp, T, LANES), lambda h, j, ki_ref, qi_ref: (h, 0, 0))] + [HBM_SPEC] * ne,
        scratch_shapes=[pltpu.VMEM((hp, T, AUG), F32), pltpu.VMEM((hp, t, AUG), F32), pltpu.VMEM((hp, t, DH), F32)]
        + ([pltpu.SemaphoreType.DMA((3 * ne,)), pltpu.SemaphoreType.DMA((3 * ne,))] if ne else []))
    outs = _call(
        body, name="attn_bwd", grid_spec=grid_spec,
        out_shape=[bf, bf, bf, sums] + [jax.ShapeDtypeStruct(s.shape, s.dtype) for s in exchange],
        compiler_params=_cparams(("arbitrary", "arbitrary"), VMEM_BIG),
    )(ki_tab, qi_tab, q_aug, qx, k_aug, v_aug, do_aug, *exchange)
    return (*outs[:4], list(outs[4:]))


def _fgate_bwd(dc_heads, flb):
    T = flb.shape[0]
    tm = TM
    nt = T // tm

    def body(dch_ref, flb_ref, dfl_ref, acc_ref, carry, top_s):
        @pl.when(pl.program_id(0) == 0)
        def _():
            carry[...] = jnp.zeros_like(carry)
            acc_ref[...] = jnp.zeros_like(acc_ref)

        flb = flb_ref[...]
        lane = lax.broadcasted_iota(jnp.int32, flb.shape, 1)
        dc = jnp.zeros(flb.shape, F32)
        for hd in range(H):
            dc = dc + jnp.where(lane == hd, dch_ref[hd], 0.0)
        r = lax.broadcasted_iota(jnp.int32, (tm, tm), 0)
        c = lax.broadcasted_iota(jnp.int32, (tm, tm), 1)
        dls = _dot_exact((c >= r).astype(F32), dc) + carry[...]
        top_s[...] = dls[:SUBLANES, :]
        carry[...] = top_s[0:1, :]
        dfl = jnp.where(lane < H, dls * _sigmoid(-flb), 0.0)
        dfl_ref[...] = dfl.astype(BF16)
        acc_ref[0:1, :] += jnp.sum(dfl, axis=0, keepdims=True)

    rev = pl.BlockSpec((tm, LANES), lambda i: (nt - 1 - i, 0))
    return _call(
        body, name="fgate_bwd", grid=(nt,),
        in_specs=[pl.BlockSpec((H, tm, LANES), lambda i: (0, nt - 1 - i, 0)), rev],
        out_specs=[rev, _const_spec((SUBLANES, LANES))],
        out_shape=[jax.ShapeDtypeStruct((T, LANES), BF16), jax.ShapeDtypeStruct((SUBLANES, LANES), F32)],
        scratch_shapes=[pltpu.VMEM((1, LANES), F32), pltpu.VMEM((SUBLANES, LANES), F32)],
        compiler_params=_cparams(("arbitrary",)),
    )(dc_heads, flb)


def _dx(dz, dfl, w_a, w_f, w_b, x, pre_gain, dh1, exchange=()):
    T = x.shape[0]
    tm = TM
    nt = T // tm
    ne = len(exchange)

    def body(*refs):
        dz_refs = refs[:6]
        dfl_ref, wa_ref, wf_ref, wb_ref, x_ref, g_ref, dh1_ref = refs[6:13]
        sent = refs[13:13 + ne]
        gx_ref, acc_ref = refs[13 + ne:15 + ne]
        received, sems = refs[15 + ne:15 + 2 * ne], refs[15 + 2 * ne:]

        @pl.when(pl.program_id(0) == 0)
        def _():
            acc_ref[...] = jnp.zeros_like(acc_ref)
            for cp in _chip_copies(sent, received, *sems) if ne else ():
                cp.start()

        if ne:
            @pl.when(pl.program_id(0) == nt - 1)
            def _():
                for cp in _chip_copies(sent, received, *sems):
                    cp.wait()

        dxn = _dot(dfl_ref[...], wf_ref[...])
        for s in range(3):
            dxn = dxn + _dot(dz_refs[s][...], wa_ref[s * D:(s + 1) * D, :])
            dxn = dxn + _dot(dz_refs[3 + s][...], wb_ref[s * D:(s + 1) * D, :])
        xv = x_ref[...]
        rstd = _rstd(xv)
        xhat = xv * rstd
        gx_ref[...] = dh1_ref[...] + _rms_bwd(dxn * g_ref[...], xhat, rstd)
        acc_ref[0:1, :] += jnp.sum(dxn * xhat, axis=0, keepdims=True)

    outs = _call(
        body, name="dx", grid=(nt,),
        in_specs=[_row_spec(tm, D)] * 6 + [_row_spec(tm, LANES), _weight_spec((3 * D, D)), _weight_spec((LANES, D)),
                                           _weight_spec((3 * D, D)), _row_spec(tm, D), _const_spec((1, D)),
                                           _row_spec(tm, D)] + [HBM_SPEC] * ne,
        out_specs=[_row_spec(tm, D), _const_spec((SUBLANES, D))] + [HBM_SPEC] * ne,
        out_shape=[jax.ShapeDtypeStruct((T, D), F32), jax.ShapeDtypeStruct((SUBLANES, D), F32)]
        + [jax.ShapeDtypeStruct(s.shape, s.dtype) for s in exchange],
        scratch_shapes=[pltpu.SemaphoreType.DMA((3 * ne,)), pltpu.SemaphoreType.DMA((3 * ne,))] if ne else [],
        compiler_params=_cparams(("arbitrary",), VMEM_BIG),
    )(*dz, dfl, w_a, w_f, w_b, x, pre_gain, dh1, *exchange)
    return outs[0], outs[1], list(outs[2:])


GRAD_ROWS = D_IN + SUBLANES


def _dw_in_segments(dz_a, dz_b, xn, buf, pair, bt):
    T = xn.shape[0]
    nt = T // bt
    first, second = [(2 * pair + k) * D + (H if 2 * pair + k >= 3 else 0) for k in (0, 1)]
    step8 = (second - first) // SUBLANES

    def body(*refs):
        dza_ref, dzb_ref, xn_ref, o_ref = refs[0], refs[1], refs[2], refs[-1]

        @pl.when(pl.program_id(1) == 0)
        def _():
            o_ref[...] = jnp.zeros_like(o_ref)

        @pl.when(pl.program_id(0) == 0)
        def _():
            o_ref[...] += _dot_tn(dza_ref[...], xn_ref[...])

        @pl.when(pl.program_id(0) == 1)
        def _():
            o_ref[...] += _dot_tn(dzb_ref[...], xn_ref[...])

    spec_a = pl.BlockSpec((bt, D), lambda s, t: (jnp.where(s == 0, t, nt - 1), 0))
    spec_b = pl.BlockSpec((bt, D), lambda s, t: (jnp.where(s == 1, t, 0), 0))
    return _call(
        body, name="dw_in_%d" % pair, grid=(2, nt),
        in_specs=[spec_a, spec_b, pl.BlockSpec((bt, D), lambda s, t: (t, 0))]
        + ([] if buf is None else [pl.BlockSpec(memory_space=pl.ANY)]),
        out_specs=pl.BlockSpec((pl.Element(D), pl.Element(D)),
                               lambda s, t: ((first // SUBLANES + s * step8) * SUBLANES, 0)),
        out_shape=jax.ShapeDtypeStruct((GRAD_ROWS, D), F32),
        input_output_aliases={} if buf is None else {3: 0},
        compiler_params=_cparams(("arbitrary", "arbitrary"), VMEM_BIG),
    )(*((dz_a, dz_b, xn) if buf is None else (dz_a, dz_b, xn, buf)))


def _dw_in_t(dz, dfl, xn, bt=DW_TOKENS):
    T = xn.shape[0]
    bt = min(bt, T)
    nt = T // bt
    main = None
    for pair in range(3):
        main = _dw_in_segments(dz[2 * pair], dz[2 * pair + 1], xn, main, pair, bt)

    def f_body(dfl_ref, xn_ref, main_ref, o_ref, acc_s):
        p = pl.program_id(0)
        t = pl.program_id(1)

        @pl.when(t == 0)
        def _():
            acc_s[...] = jnp.zeros_like(acc_s)

        @pl.when(p == 0)
        def _():
            acc_s[...] += _dot_tn(dfl_ref[...], xn_ref[...])

        @pl.when(t == nt - 1)
        def _():
            o_ref[...] = acc_s[:SUBLANES, :]

    fl_block = FL0 // SUBLANES
    end_block = D_IN // SUBLANES
    return _call(
        f_body, name="dw_in_f", grid=(2, nt),
        in_specs=[pl.BlockSpec((bt, LANES), lambda p, t: (t, 0)), pl.BlockSpec((bt, D), lambda p, t: (t, 0)),
                  pl.BlockSpec(memory_space=pl.ANY)],
        out_specs=pl.BlockSpec((SUBLANES, D), lambda p, t: (fl_block + p * (end_block - fl_block), 0)),
        out_shape=jax.ShapeDtypeStruct((GRAD_ROWS, D), F32),
        scratch_shapes=[pltpu.VMEM((LANES, D), F32)],
        input_output_aliases={2: 0},
        compiler_params=_cparams(("arbitrary", "arbitrary")),
    )(dfl, xn, main)


HBM_SPEC = pl.BlockSpec(memory_space=pltpu.HBM)
VMEM_SPEC = pl.BlockSpec(memory_space=pltpu.VMEM)


def _position():
    return lax.axis_index("x"), lax.axis_index("y"), lax.axis_index("c")


def _other_chips(x, y):
    return [(1 - x, y), (x, 1 - y), (1 - x, 1 - y)]


def _gather_shards(shards, whole):
    na, nw = len(shards), len(whole)
    nall = na + nw

    def body(*refs):
        gather = _GatherPlan(refs[:nall], refs[nall:2 * nall], refs[2 * nall:], na)
        gather.send()
        gather.forward()
        gather.finish()

    arrs = list(shards) + list(whole)
    outs = _call(
        body, name="gather_shards",
        in_specs=[HBM_SPEC] * nall, out_specs=[HBM_SPEC] * nall,
        out_shape=_gather_out_shapes(arrs), scratch_shapes=_gather_semaphores(na, nall),
    )(*arrs)
    return _place_own(outs, arrs)


def _gather_out_shapes(arrs):
    return [jax.ShapeDtypeStruct((N_CHIPS,) + s.shape, s.dtype) for s in arrs]


def _gather_semaphores(na, nall):
    return [pltpu.SemaphoreType.DMA((3 * nall,)), pltpu.SemaphoreType.DMA((3 * nall,)),
            pltpu.SemaphoreType.DMA((3 * na,)), pltpu.SemaphoreType.DMA((3 * na,))]


def _place_own(outs, arrs):
    if not arrs:
        return []
    chip = 2 * lax.axis_index("x") + lax.axis_index("y")
    return [lax.dynamic_update_slice(o, a[None], (chip,) + (0,) * a.ndim) for o, a in zip(outs, arrs)]


class _GatherPlan:
    def __init__(self, srcs, dsts, sems, na):
        ici_send, ici_recv, d2d_send, d2d_recv = sems
        x, y, c = _position()
        chip = 2 * x + y
        nall = len(srcs)

        def half(a, which):
            rows = srcs[a].shape[0] // 2
            return pl.ds(pl.multiple_of(which * rows, BF16_ROWS), rows)

        def copy(src, dst, send, recv, k, to):
            return pltpu.make_async_remote_copy(src_ref=src, dst_ref=dst, send_sem=send.at[k], recv_sem=recv.at[k],
                                                device_id=to, device_id_type=MESH)

        self.first, self.landed, self.passed, self.returned = [], [], [], []
        for j, (px, py) in enumerate(_other_chips(x, y)):
            theirs = 2 * px + py
            for a in range(nall):
                k = j * nall + a
                if a < na:
                    self.first.append(copy(srcs[a].at[half(a, c), :], dsts[a].at[chip, half(a, c), :],
                                           ici_send, ici_recv, k, (px, py, c)))
                    mine = dsts[a].at[theirs, half(a, c), :]
                    other = dsts[a].at[theirs, half(a, 1 - c), :]
                    self.landed.append(copy(mine, mine, ici_send, ici_recv, k, (px, py, c)))
                    self.passed.append(copy(mine, mine, d2d_send, d2d_recv, j * na + a, (x, y, 1 - c)))
                    self.returned.append(copy(other, other, d2d_send, d2d_recv, j * na + a, (x, y, 1 - c)))
                else:
                    self.first.append(copy(srcs[a], dsts[a].at[chip], ici_send, ici_recv, k, (px, py, c)))
                    got = dsts[a].at[theirs]
                    self.landed.append(copy(got, got, ici_send, ici_recv, k, (px, py, c)))
                    self.passed.append(None)

    def send(self):
        for cp in self.first:
            cp.start()

    def forward(self):
        for arrival, fwd in zip(self.landed, self.passed):
            arrival.wait_recv()
            if fwd is not None:
                fwd.start()

    def finish(self):
        for cp in self.returned:
            cp.wait_recv()
        for cp in self.first + [f for f in self.passed if f is not None]:
            cp.wait_send()


W_ROWS = 1568
G_ROWS = 1552
SHARD_ROWS = D_IN // N_CHIPS
WINDOW_STEP = 1536


def _assemble_w_in(cont):
    cb = COL_BLOCK
    half = WINDOW_STEP
    seam = BF16_ROWS

    def body(c_ref, wa_ref, wf_ref, wb_ref):
        x0 = c_ref[0].astype(F32)
        x1, x2, x3 = (pltpu.roll(c_ref[j].astype(F32), 2 * j, 0) for j in (1, 2, 3))
        wa = jnp.concatenate([x0[:half], x0[half:half + seam] + x1[:seam], x1[seam:half]], axis=0)
        wa_ref[...] = wa.astype(BF16)

        fl = x1[half:half + seam] + x2[:seam]
        row = lax.broadcasted_iota(jnp.int32, fl.shape, 0)
        wf_ref[:seam, :] = jnp.where(row < H, fl, 0.0).astype(BF16)
        wf_ref[seam:, :] = jnp.zeros((LANES - seam, cb), BF16)

        mid = x2[half:half + SUBLANES] + x3[:SUBLANES]
        wb = jnp.concatenate([x2[SUBLANES:half], mid, x3[SUBLANES:half + SUBLANES]], axis=0)
        wb_ref[...] = wb.astype(BF16)

    return _call(
        body, name="assemble_w_in", grid=(D // cb,),
        in_specs=[pl.BlockSpec((N_CHIPS, W_ROWS, cb), lambda i: (0, 0, i))],
        out_specs=[pl.BlockSpec((3 * D, cb), lambda i: (0, i)), pl.BlockSpec((LANES, cb), lambda i: (0, i)),
                   pl.BlockSpec((3 * D, cb), lambda i: (0, i))],
        out_shape=[jax.ShapeDtypeStruct((3 * D, D), BF16), jax.ShapeDtypeStruct((LANES, D), BF16),
                   jax.ShapeDtypeStruct((3 * D, D), BF16)],
        compiler_params=_cparams(("parallel",)),
    )(cont)


def _pair_exchange_windows(grad_t):
    half_g = G_ROWS // 2

    def body(g_ref, got, send_sems, recv_sems):
        x, y, c = _position()
        copies = []
        for j in range(N_CHIPS):
            rows = pl.ds(pl.multiple_of(j * WINDOW_STEP + (1 - c) * half_g, SUBLANES), half_g)
            copies.append(pltpu.make_async_remote_copy(
                src_ref=g_ref.at[rows, :], dst_ref=got.at[j], send_sem=send_sems.at[j], recv_sem=recv_sems.at[j],
                device_id=(x, y, 1 - c), device_id_type=MESH))
        for cp in copies:
            cp.start()
        for cp in copies:
            cp.wait()

    return _call(
        body, name="pair_exchange_w_in",
        in_specs=[HBM_SPEC], out_specs=HBM_SPEC,
        out_shape=jax.ShapeDtypeStruct((N_CHIPS, half_g, D), F32),
        scratch_shapes=[pltpu.SemaphoreType.DMA((N_CHIPS,)), pltpu.SemaphoreType.DMA((N_CHIPS,))],
    )(grad_t)


def _pair_sum(parts, gots, c):
    na = len(parts)

    def body(c_ref, *refs):
        for a in range(na):
            refs[2 * na + a][...] = (refs[a][...] + refs[na + a][...]).astype(BF16)

    mine = [pl.BlockSpec(g.shape, lambda i, c_ref: (0, c_ref[0], 0)) for g in gots]
    whole = [pl.BlockSpec(g.shape, lambda i, c_ref: (0, 0, 0)) for g in gots]
    grid_spec = pltpu.PrefetchScalarGridSpec(
        num_scalar_prefetch=1, grid=(1,), in_specs=mine + whole, out_specs=whole)
    return _call(
        body, name="pair_sum", grid_spec=grid_spec,
        out_shape=[jax.ShapeDtypeStruct(g.shape, BF16) for g in gots],
        compiler_params=_cparams(("arbitrary",), VMEM_BIG),
    )(c.reshape(1), *parts, *gots)


def _pair_sum_windows(grad_t, got, c):
    _, half, C = got.shape
    cb = SUM_BLOCK

    def body(c_ref, a_ref, b_ref, o_ref):
        o_ref[0] = (a_ref[...] + b_ref[0]).astype(BF16)

    def mine(j, i, c_ref):
        return ((j * (WINDOW_STEP // SUBLANES) + c_ref[0] * (half // SUBLANES)) * SUBLANES, i * cb)

    spec = pl.BlockSpec((1, half, cb), lambda j, i, c_ref: (j, 0, i))
    grid_spec = pltpu.PrefetchScalarGridSpec(
        num_scalar_prefetch=1, grid=(N_CHIPS, C // cb),
        in_specs=[pl.BlockSpec((pl.Element(half), pl.Element(cb)), mine), spec], out_specs=spec)
    return _call(
        body, name="pair_sum_w_in", grid_spec=grid_spec,
        out_shape=jax.ShapeDtypeStruct((N_CHIPS, half, C), BF16),
        compiler_params=_cparams(("parallel", "parallel")),
    )(c.reshape(1), grad_t, got)


def _chip_sum(own, got, chip, name):
    _, half, C = got.shape
    cb = min(C, SUM_BLOCK)

    def body(chip_ref, own_ref, g_ref, o_ref):
        for me in range(N_CHIPS):
            @pl.when(chip_ref[0] == me)
            def _(me=me):
                terms = [own_ref[0] if k == me else g_ref[k] for k in range(N_CHIPS)]
                acc = terms[0].astype(F32) + terms[1].astype(F32)
                acc = acc + terms[2].astype(F32)
                o_ref[...] = acc + terms[3].astype(F32)

    grid_spec = pltpu.PrefetchScalarGridSpec(
        num_scalar_prefetch=1, grid=(C // cb,),
        in_specs=[pl.BlockSpec((1, half, cb), lambda i, chip_ref: (chip_ref[0], 0, i)),
                  pl.BlockSpec((N_CHIPS, half, cb), lambda i, chip_ref: (0, 0, i))],
        out_specs=pl.BlockSpec((half, cb), lambda i, chip_ref: (0, i)))
    return _call(
        body, name=name, grid_spec=grid_spec,
        out_shape=jax.ShapeDtypeStruct((half, C), F32),
        compiler_params=_cparams(("parallel",)),
    )(chip.reshape(1), own, got)


def _final_exchange(halves, g):
    na = len(halves)
    rows = g.shape[0]
    per = rows // N_DEV

    def body(*refs):
        srcs, g_ref = refs[:na], refs[na]
        dsts, out_ref = refs[na + 1:2 * na + 1], refs[2 * na + 1]
        got_ref, s1, r1, s2, r2, swap_send, swap_recv = refs[2 * na + 2:]
        x, y, c = _position()
        swaps = [pltpu.make_async_remote_copy(
            src_ref=srcs[a], dst_ref=dsts[a], send_sem=swap_send.at[a], recv_sem=swap_recv.at[a],
            device_id=(x, y, 1 - c), device_id_type=MESH) for a in range(na)]
        for cp in swaps:
            cp.start()
        me = 4 * x + 2 * y + c
        mine = pl.ds(pl.multiple_of(me * per, SUBLANES), per)
        peers = []
        for j in range(1, N_DEV):
            px = 1 - x if j & 4 else x
            py = 1 - y if j & 2 else y
            pc = 1 - c if j & 1 else c
            peers.append((px, py, pc))

        first = []
        for j, (px, py, pc) in enumerate(peers):
            theirs = pl.ds(pl.multiple_of((4 * px + 2 * py + pc) * per, SUBLANES), per)
            first.append(pltpu.make_async_remote_copy(
                src_ref=g_ref.at[theirs, :], dst_ref=got_ref.at[me], send_sem=s1.at[j], recv_sem=r1.at[j],
                device_id=(px, py, pc), device_id_type=MESH))
        for cp in first:
            cp.start()
        got_ref[me] = g_ref[mine, :]
        for cp in first:
            cp.wait()
        total = got_ref[0]
        for d in range(1, N_DEV):
            total = total + got_ref[d]
        out_ref[mine, :] = total

        second = []
        for j, peer in enumerate(peers):
            second.append(pltpu.make_async_remote_copy(
                src_ref=out_ref.at[mine, :], dst_ref=out_ref.at[mine, :], send_sem=s2.at[j], recv_sem=r2.at[j],
                device_id=peer, device_id_type=MESH))
        for cp in second:
            cp.start()
        for cp in second + swaps:
            cp.wait()

    sems = pltpu.SemaphoreType.DMA((N_DEV - 1,))
    swap_sems = pltpu.SemaphoreType.DMA((na,))
    outs = _call(
        body, name="final_exchange", in_hbm=False,
        in_specs=[HBM_SPEC] * na + [VMEM_SPEC], out_specs=[HBM_SPEC] * na + [VMEM_SPEC],
        out_shape=[jax.ShapeDtypeStruct(s.shape, s.dtype) for s in halves] + [jax.ShapeDtypeStruct(g.shape, F32)],
        scratch_shapes=[pltpu.VMEM((N_DEV, per, LANES), F32), sems, sems, sems, sems, swap_sems, swap_sems],
    )(*halves, g)
    return outs[:na], outs[na]


def _adamw_math(g, w, m, v):
    m2 = ADAM_B1 * m + (1.0 - ADAM_B1) * g
    v2 = ADAM_B2 * v + (1.0 - ADAM_B2) * (g * g)
    m_hat = m2 / (1.0 - ADAM_B1 ** ADAM_STEP)
    v_hat = v2 / (1.0 - ADAM_B2 ** ADAM_STEP)
    delta = (-ADAM_LR) * (m_hat / (jnp.sqrt(v_hat) + ADAM_EPS) + ADAM_WD * w)
    return delta, m2, v2


ADAMW_BLOCK_BYTES = 2 << 20


def _adamw_big(g, w, m, v, name):
    R, C = g.shape
    bc = min(C, max(LANES, ADAMW_BLOCK_BYTES // (4 * R) // LANES * LANES))

    def body(g_ref, w_ref, m_ref, v_ref, d_ref, m2_ref, v2_ref):
        d_ref[...], m2_ref[...], v2_ref[...] = _adamw_math(g_ref[...], w_ref[...], m_ref[...], v_ref[...])

    spec = pl.BlockSpec((R, bc), lambda j: (0, j))
    out = jax.ShapeDtypeStruct((R, C), F32)
    return _call(
        body, name=name, grid=(C // bc,),
        in_specs=[spec] * 4, out_specs=[spec] * 3, out_shape=[out] * 3,
        compiler_params=_cparams(("parallel",)),
    )(g, w, m, v)


def _adamw_small(gs, ws, ms, vs):
    n = len(gs)

    def body(*refs):
        for a in range(n):
            g_ref, w_ref, m_ref, v_ref = (refs[k * n + a] for k in range(4))
            d_ref, m2_ref, v2_ref = (refs[(4 + k) * n + a] for k in range(3))
            d_ref[...], m2_ref[...], v2_ref[...] = _adamw_math(g_ref[...], w_ref[...], m_ref[...], v_ref[...])

    outs = [jax.ShapeDtypeStruct(w.shape, F32) for w in ws]
    specs = [_const_spec(w.shape) for w in ws]
    return _call(
        body, name="adamw_small", grid=(1,),
        in_specs=specs * 4, out_specs=specs * 3, out_shape=outs * 3,
    )(*gs, *ws, *ms, *vs)


def _late_weights(st_out, st_ple, st_gate, st_conv):
    return st_out.reshape(DMIX, D), _from_chip_cols(st_ple), st_gate.reshape(D, D), _from_chip_cols(st_conv)


def _local_step(x, p, tgt, w_a, w_f, w_b, late, b_f, pre_gain, post_gain, conv_b,
                w_rgate, b_rgate, w_igate, b_igate, lam, gain_a, gain_l, ple_gain, b_gate,
                gather_late=False, early_reduce=None, w_in_reduce=None):
    b_f_pad = jnp.pad(b_f, ((0, 0), (0, LANES - H)))
    w_r = w_rgate.astype(BF16)
    w_i = w_igate.astype(BF16)

    xn, q_aug, k_aug, v_aug, g_attn, x_lru, g_lru, flb, vt_aug = _in_proj(x, pre_gain, w_a, w_f, w_b, b_f_pad)
    if gather_late:
        o, qx, stacks = _attn_fwd(q_aug, k_aug, vt_aug, late[:3], late[3:])
        late = _late_weights(*stacks)
    else:
        o, qx, _ = _attn_fwd(q_aug, k_aug, vt_aug)
    w_out_b, w_ple_b, w_gate_b, conv_w = late
    ycat, xc, h = _branches_fwd(o, g_attn, x_lru, g_lru, gain_a, gain_l, conv_w, conv_b, w_r, b_rgate, w_i, b_igate,
                                lam)
    dh1, dycat, dmix, h1b, dgp, pb, dpe, acc_t = _tail(ycat, x, p, tgt, w_out_b, post_gain, w_ple_b, ple_gain,
                                                       w_gate_b, b_gate)
    do_aug, dg_attn, dg_lru, dh, acc_b, gw_out, gw_gate, gw_ple = _branches_bwd(
        dycat, o, g_attn, h, g_lru, gain_a, gain_l, ycat, dmix, h1b, dgp, pb, dpe)
    late_grads = [gw_out, gw_ple, gw_gate]
    if early_reduce is None:
        dx_lru, gw_r, gw_i, acc_l, _ = _lru_bwd(dh, h, xc, x_lru, conv_w, w_r, b_rgate, w_i, b_igate, lam)
    else:
        parts = [gw_out.reshape(N_CHIPS, DMIX // N_CHIPS, D), _by_chip_cols(gw_ple),
                 gw_gate.reshape(N_CHIPS, D // N_CHIPS, D)]
        dx_lru, gw_r, gw_i, acc_l, got = _lru_bwd(dh, h, xc, x_lru, conv_w, w_r, b_rgate, w_i, b_igate, lam, parts)
        sent = _pair_sum(parts, got, early_reduce)
    if early_reduce is None:
        dq, dk, dv, dc_heads, _ = _attn_bwd(q_aug, qx, k_aug, v_aug, do_aug)
    else:
        dq, dk, dv, dc_heads, received = _attn_bwd(q_aug, qx, k_aug, v_aug, do_aug, sent)
        late_grads = list(zip(sent, received))
    dfl, acc_f = _fgate_bwd(dc_heads, flb)
    dz = (dq, dk, dv, dg_attn, dx_lru, dg_lru)
    grad_t = _dw_in_t(dz, dfl, xn)
    if w_in_reduce is None:
        grad_x, acc_x, _ = _dx(dz, dfl, w_a, w_f, w_b, x, pre_gain, dh1)
    else:
        sent = w_in_reduce(grad_t)
        grad_x, acc_x, (received,) = _dx(dz, dfl, w_a, w_f, w_b, x, pre_gain, dh1, [sent])
        grad_t = (sent, received)

    grads = dict(
        w_in_t=grad_t,
        w_out=late_grads[0],
        w_ple=late_grads[1],
        w_ple_gate=late_grads[2],
        w_rgate=gw_r,
        w_igate=gw_i,
        b_f=acc_f[0:1, :H],
        pre_gain=acc_x[0:1],
        post_gain=acc_t[0:1],
        conv_w=acc_l[0:4],
        conv_b=acc_l[4:5],
        b_rgate=acc_l[5:6],
        b_igate=acc_l[6:7],
        lru_lambda=acc_l[7:8],
        attn_out_gain=acc_b[0:1],
        lru_out_gain=acc_b[1:2],
        ple_gain=acc_t[1:2],
        b_ple_gate=acc_t[2:3],
    )
    loss = jnp.sum(acc_t[3])
    return loss, grad_x, grads


SMALL_ROWS = ["b_f", "pre_gain", "post_gain", "conv_w", "conv_b", "b_rgate", "b_igate", "lru_lambda",
              "attn_out_gain", "lru_out_gain", "ple_gain", "b_ple_gate"]
WEIGHTS = ["w_in", "b_f", "pre_gain", "post_gain", "conv_w", "conv_b", "w_rgate", "b_rgate", "w_igate", "b_igate",
           "lru_lambda", "attn_out_gain", "lru_out_gain", "w_out", "w_ple", "ple_gain", "w_ple_gate", "b_ple_gate"]
SHARDED = ["w_in", "w_out", "w_ple", "w_ple_gate"]


def _by_chip_cols(g):
    r, cols = g.shape
    return g.reshape(r, N_CHIPS, cols // N_CHIPS).transpose(1, 0, 2)


def _from_chip_cols(s):
    n, r, cols = s.shape
    return s.transpose(1, 0, 2).reshape(r, n * cols)


def kernel(x, p, w_in, b_f, pre_gain, post_gain, conv_w, conv_b, w_rgate, b_rgate, w_igate, b_igate, lru_lambda, attn_out_gain, lru_out_gain, w_out, w_ple, ple_gain, w_ple_gate, b_ple_gate, loss_target, m_w_in, m_b_f, m_pre_gain, m_post_gain, m_conv_w, m_conv_b, m_w_rgate, m_b_rgate, m_w_igate, m_b_igate, m_lru_lambda, m_attn_out_gain, m_lru_out_gain, m_w_out, m_w_ple, m_ple_gain, m_w_ple_gate, m_b_ple_gate, v_w_in, v_b_f, v_pre_gain, v_post_gain, v_conv_w, v_conv_b, v_w_rgate, v_b_rgate, v_w_igate, v_b_igate, v_lru_lambda, v_attn_out_gain, v_lru_out_gain, v_w_out, v_w_ple, v_ple_gain, v_w_ple_gate, v_b_ple_gate):
    w = dict(w_in=w_in, b_f=b_f, pre_gain=pre_gain, post_gain=post_gain, conv_w=conv_w, conv_b=conv_b,
             w_rgate=w_rgate, b_rgate=b_rgate, w_igate=w_igate, b_igate=b_igate, lru_lambda=lru_lambda,
             attn_out_gain=attn_out_gain, lru_out_gain=lru_out_gain, w_out=w_out, w_ple=w_ple, ple_gain=ple_gain,
             w_ple_gate=w_ple_gate, b_ple_gate=b_ple_gate)
    m = dict(w_in=m_w_in, b_f=m_b_f, pre_gain=m_pre_gain, post_gain=m_post_gain, conv_w=m_conv_w, conv_b=m_conv_b,
             w_rgate=m_w_rgate, b_rgate=m_b_rgate, w_igate=m_w_igate, b_igate=m_b_igate, lru_lambda=m_lru_lambda,
             attn_out_gain=m_attn_out_gain, lru_out_gain=m_lru_out_gain, w_out=m_w_out, w_ple=m_w_ple,
             ple_gain=m_ple_gain, w_ple_gate=m_w_ple_gate, b_ple_gate=m_b_ple_gate)
    v = dict(w_in=v_w_in, b_f=v_b_f, pre_gain=v_pre_gain, post_gain=v_post_gain, conv_w=v_conv_w, conv_b=v_conv_b,
             w_rgate=v_w_rgate, b_rgate=v_b_rgate, w_igate=v_w_igate, b_igate=v_b_igate, lru_lambda=v_lru_lambda,
             attn_out_gain=v_attn_out_gain, lru_out_gain=v_lru_out_gain, w_out=v_w_out, w_ple=v_w_ple,
             ple_gain=v_ple_gain, w_ple_gate=v_w_ple_gate, b_ple_gate=v_b_ple_gate)
    xi, yi, ci = _position()
    chip = 2 * xi + yi

    w_in_t, m_in_t, v_in_t = (jnp.swapaxes(t[0], 0, 1) for t in (w_in, m_w_in, v_w_in))
    window = jnp.pad(w_in_t.astype(BF16), ((0, W_ROWS - SHARD_ROWS), (0, 0)))

    (st_in,) = _gather_shards([window], [])
    w_a, w_f, w_b = _assemble_w_in(st_in)
    late_shards = (w_out[0].astype(BF16), w_ple[0].astype(BF16), w_ple_gate[0].astype(BF16), conv_w[0])

    loss, grad_x, g = _local_step(
        x[0], p[0, 0], loss_target[0], w_a, w_f, w_b, late_shards, b_f, pre_gain, post_gain,
        conv_b, w_rgate[0], b_rgate, w_igate[0], b_igate, lru_lambda, attn_out_gain, lru_out_gain, ple_gain,
        b_ple_gate, gather_late=True, early_reduce=ci,
        w_in_reduce=lambda grad_t: _pair_sum_windows(grad_t, _pair_exchange_windows(grad_t), ci))

    sums = [g["w_in_t"][0]] + [g[n][0] for n in SHARDED[1:]]
    recv = [g["w_in_t"][1]] + [g[n][1] for n in SHARDED[1:]]
    halves = [_chip_sum(sums[a], recv[a], chip, "chip_sum_%d" % a) for a in range(4)]

    rows = [jnp.pad(g["b_f"], ((0, 0), (0, D - H)))] + [g[n] for n in SMALL_ROWS[1:]]
    rows.append(jnp.pad(loss.reshape(1, 1), ((0, 0), (0, D - 1))))
    packed = jnp.concatenate([g["w_rgate"].reshape(NB * LANES, LANES), g["w_igate"].reshape(NB * LANES, LANES),
                              jnp.concatenate(rows, axis=0).reshape(LANES, LANES)], axis=0)
    theirs, summed = _final_exchange(halves, packed)
    full = [jnp.concatenate([jnp.where(ci == 0, a, b), jnp.where(ci == 0, b, a)], axis=0)
            for a, b in zip(halves, theirs)]
    red = dict(zip(SHARDED, full))
    red["w_in"] = lax.dynamic_slice_in_dim(red["w_in"], 2 * chip, SHARD_ROWS, axis=0)
    red["w_rgate"] = summed[:D].reshape(1, NB, LANES, LANES)
    red["w_igate"] = summed[D:2 * D].reshape(1, NB, LANES, LANES)
    vec = summed[2 * D:].reshape(16, D)
    loss = vec[15, 0]
    r0 = 0
    for n in SMALL_ROWS:
        nr = 4 if n == "conv_w" else 1
        red[n] = vec[r0:r0 + nr]
        r0 += nr
    red["b_f"] = red["b_f"][:, :H]
    red["conv_w"] = lax.dynamic_slice_in_dim(red["conv_w"], chip * (D // N_CHIPS), D // N_CHIPS, axis=1)[None]

    delta, new_m, new_v = {}, {}, {}
    outs_in = _adamw_big(red["w_in"], w_in_t, m_in_t, v_in_t, "adamw_w_in")
    delta["w_in"], new_m["w_in"], new_v["w_in"] = (jnp.swapaxes(t, 0, 1)[None] for t in outs_in)
    red["w_in"] = jnp.swapaxes(red["w_in"], 0, 1)[None]
    for n in SHARDED[1:]:
        delta[n], new_m[n], new_v[n] = (t[None] for t in _adamw_big(red[n], w[n][0], m[n][0], v[n][0], "adamw_" + n))
        red[n] = red[n][None]
    small = [n for n in WEIGHTS if n not in SHARDED]
    outs = _adamw_small([red[n] for n in small], [w[n] for n in small], [m[n] for n in small],
                        [v[n] for n in small])
    ns = len(small)
    for a, n in enumerate(small):
        delta[n], new_m[n], new_v[n] = outs[a], outs[ns + a], outs[2 * ns + a]

    return (loss, grad_x[None], *[red[n] for n in WEIGHTS], *[delta[n] for n in WEIGHTS],
            *[new_m[n] for n in WEIGHTS], *[new_v[n] for n in WEIGHTS])
```

```python
import jax
import jax.numpy as jnp
import numpy as np
from jax import lax
from jax.experimental import pallas as pl
from jax.experimental.pallas import tpu as pltpu

F32 = jnp.float32
BF16 = jnp.bfloat16

D = 1024
H = 8
DH = 128
NB = 8
DPLE = 256
DMIX = 2 * D
D_IN = 4 * D + H + 2 * D
FL0 = 3 * D
RMS_EPS = 1e-6
LRU_C = 8.0
NEG = -1e30
LANES = 128
SUBLANES = 8
BF16_ROWS = 16
COL_BLOCK = 256
SUM_BLOCK = 512
DW_TOKENS = 2048

ADAM_LR = 0.001
ADAM_B1 = 0.9
ADAM_B2 = 0.999
ADAM_EPS = 1e-08
ADAM_WD = 0.01
ADAM_STEP = 10

TM = 256
TA = 1024
TA_FWD = 1024
FWD_HEADS = 8
BWD_HEADS = 2
VMEM_BIG = 56 * 1024 * 1024
VMEM_MID = 40 * 1024 * 1024

MESH = pl.DeviceIdType.MESH
N_CHIPS = 4
N_DEV = 8


def _call(body, *, out_shape, in_hbm=True, **kwargs):
    if not in_hbm:
        return pl.pallas_call(body, out_shape=out_shape, **kwargs)

    def pin(shape):
        return pltpu.HBM(shape.shape, shape.dtype) if isinstance(shape, jax.ShapeDtypeStruct) else shape

    fn = pl.pallas_call(body, out_shape=jax.tree.map(pin, out_shape), **kwargs)

    def run(*args):
        return fn(*[a if a.dtype == jnp.int32 else pltpu.with_memory_space_constraint(a, pltpu.HBM) for a in args])

    return run


def _cparams(sem, vmem=VMEM_MID):
    return pltpu.CompilerParams(dimension_semantics=sem, vmem_limit_bytes=vmem)


def _sigmoid(x):
    return 0.5 * jnp.tanh(0.5 * x) + 0.5


def _rstd(x):
    return lax.rsqrt(jnp.mean(x * x, axis=-1, keepdims=True) + RMS_EPS)


def _rms_bwd(t, xhat, rstd):
    return rstd * (t - xhat * jnp.mean(t * xhat, axis=-1, keepdims=True))


def _dot(a, b):
    return jnp.dot(a, b, preferred_element_type=F32)


def _dot_nt(a, b):
    return lax.dot_general(a, b, (((1,), (1,)), ((), ())), preferred_element_type=F32)


def _dot_tn(a, b):
    return lax.dot_general(a, b, (((0,), (0,)), ((), ())), preferred_element_type=F32)


def _dot_exact(a, b):
    return jnp.dot(a, b, preferred_element_type=F32, precision=lax.Precision.HIGHEST)


def _shift_down(x, j, halo):
    rolled = pltpu.roll(x, j, 0)
    row = lax.broadcasted_iota(jnp.int32, halo.shape, 0)
    top = jnp.where(row < j, pltpu.roll(halo, j, 0), rolled[:SUBLANES])
    return jnp.concatenate([top, rolled[SUBLANES:]], axis=0)


def _shift_up(x, j, nxt):
    tm = x.shape[0]
    rolled = pltpu.roll(x, tm - j, 0)
    row = lax.broadcasted_iota(jnp.int32, nxt.shape, 0)
    bot = jnp.where(row >= SUBLANES - j, pltpu.roll(nxt, SUBLANES - j, 0), rolled[tm - SUBLANES:])
    return jnp.concatenate([rolled[:tm - SUBLANES], bot], axis=0)


def _scan_fwd_into(a, u, carry, h_ref):
    tm, width = a.shape
    groups = (tm // SUBLANES, SUBLANES, width)
    a, u = a.reshape(groups), u.reshape(groups)
    sub = lax.broadcasted_iota(jnp.int32, groups, 1)
    d = 1
    while d < SUBLANES:
        keep = sub >= d
        a_s = jnp.where(keep, pltpu.roll(a, d, 1), 1.0)
        u_s = jnp.where(keep, pltpu.roll(u, d, 1), 0.0)
        u = u + a * u_s
        a = a * a_s
        d *= 2
    a, u = a.reshape(tm, width), u.reshape(tm, width)
    for g in range(tm // SUBLANES):
        rows = slice(g * SUBLANES, (g + 1) * SUBLANES)
        h_ref[rows, :] = u[rows] + a[rows] * carry
        carry = h_ref[(g + 1) * SUBLANES - 1:(g + 1) * SUBLANES, :]
    return carry


def _scan_bwd_into(b, u, g_ref):
    tm, width = b.shape
    groups = (tm // SUBLANES, SUBLANES, width)
    b, u = b.reshape(groups), u.reshape(groups)
    sub = lax.broadcasted_iota(jnp.int32, groups, 1)
    d = 1
    while d < SUBLANES:
        keep = sub < SUBLANES - d
        b_s = jnp.where(keep, pltpu.roll(b, SUBLANES - d, 1), 1.0)
        u_s = jnp.where(keep, pltpu.roll(u, SUBLANES - d, 1), 0.0)
        u = u + b * u_s
        b = b * b_s
        d *= 2
    b, u = b.reshape(tm, width), u.reshape(tm, width)
    nxt = jnp.zeros((1, width), F32)
    for g in reversed(range(tm // SUBLANES)):
        rows = slice(g * SUBLANES, (g + 1) * SUBLANES)
        g_ref[rows, :] = u[rows] + b[rows] * nxt
        nxt = g_ref[g * SUBLANES:g * SUBLANES + 1, :]


def _gate_pre(xc, w_ref):
    outs = []
    for n in range(NB):
        outs.append(_dot(xc[:, n * LANES:(n + 1) * LANES].astype(BF16), w_ref[n]))
    return jnp.concatenate(outs, axis=1)


def _gate_pre_t(d, w_ref):
    outs = []
    for n in range(NB):
        outs.append(_dot_nt(d[:, n * LANES:(n + 1) * LANES].astype(BF16), w_ref[n]))
    return jnp.concatenate(outs, axis=1)


def _softplus_neg(lam):
    return jnp.maximum(-lam, 0.0) + jnp.log(1.0 + jnp.exp(-jnp.abs(lam)))


def _row_spec(tm, width):
    return pl.BlockSpec((tm, width), lambda i: (i, 0))


def _const_spec(shape):
    nd = len(shape)
    return pl.BlockSpec(shape, lambda *_: (0,) * nd)


def _weight_spec(shape):
    nd = len(shape)
    return pl.BlockSpec(shape, lambda *_: (0,) * nd, pipeline_mode=pl.Buffered(1))


AUG = 2 * DH
LOG2E = 1.4426950408889634
LN2 = 0.6931471805599453
Q_SCALE = DH ** -0.5 * LOG2E


def _split3(x):
    hi = x.astype(BF16)
    r1 = x - hi.astype(F32)
    mid = r1.astype(BF16)
    lo = (r1 - mid.astype(F32)).astype(BF16)
    return hi, mid, lo


def _extras(col, ones_from):
    t = col.shape[0]
    hi, mid, lo = _split3(jnp.broadcast_to(col, (t, LANES)))
    lane = lax.broadcasted_iota(jnp.int32, (t, LANES), 1)
    rest = jnp.zeros((t, LANES), BF16)
    if ones_from is not None:
        rest = jnp.where((lane >= ones_from) & (lane < ones_from + 3), 1.0, 0.0).astype(BF16)
    return jnp.where(lane == 0, hi, jnp.where(lane == 1, mid, jnp.where(lane == 2, lo, rest)))


def _selectors():
    sel_q = np.zeros((3 * LANES, H * LANES), np.float32)
    sel_k = np.zeros((3 * LANES, H * LANES), np.float32)
    for hd in range(H):
        for piece in range(3):
            sel_q[piece * LANES + hd, hd * LANES + piece] = 1.0
            sel_k[piece * LANES + hd, hd * LANES + 3 + piece] = -1.0
    return jnp.asarray(sel_q, BF16), jnp.asarray(sel_k, BF16)


def _in_proj(x, pre_gain, w_a, w_f, w_b, b_f_pad):
    T = x.shape[0]
    tm = TM
    sel_q, sel_k = _selectors()

    def body(x_ref, g_ref, wa_ref, wf_ref, wb_ref, bf_ref, sq_ref, sk_ref,
             xn_ref, qa_ref, ka_ref, va_ref, ga_ref, xl_ref, gl_ref, flb_ref, vt_ref, c_s, carry):
        @pl.when(pl.program_id(0) == 0)
        def _():
            carry[...] = jnp.zeros_like(carry)

        xv = x_ref[...]
        xn = (xv * _rstd(xv) * g_ref[...]).astype(BF16)
        xn_ref[...] = xn
        for s, o_ref in enumerate((ga_ref, xl_ref, gl_ref)):
            o_ref[...] = _dot_nt(xn, wb_ref[s * D:(s + 1) * D, :]).astype(o_ref.dtype)
        flb = _dot_nt(xn, wf_ref[...]) + bf_ref[...]
        flb_ref[...] = flb
        lane = lax.broadcasted_iota(jnp.int32, flb.shape, 1)
        ls = jnp.where(lane < H, jnp.minimum(flb, 0.0) - jnp.log(1.0 + jnp.exp(-jnp.abs(flb))), 0.0)
        r = lax.broadcasted_iota(jnp.int32, (tm, tm), 0)
        c = lax.broadcasted_iota(jnp.int32, (tm, tm), 1)
        cs = _dot_exact((c <= r).astype(F32), ls) + carry[...]
        c_s[...] = cs
        carry[...] = c_s[tm - 1:tm, :]

        pieces = jnp.concatenate(_split3(cs * LOG2E), axis=1)
        ones_q = jnp.where((lane >= 3) & (lane < 6), 1.0, 0.0)
        ones_k = jnp.where(lane < 3, 1.0, 0.0)
        zq = _dot_nt(xn, wa_ref[0:D, :]) * Q_SCALE
        zk = _dot_nt(xn, wa_ref[D:2 * D, :])
        zv = _dot_nt(xn, wa_ref[2 * D:3 * D, :])
        ex_q = _dot(pieces, sq_ref[...])
        ex_k = _dot(pieces, sk_ref[...])
        for hd in range(H):
            head = slice(hd * DH, (hd + 1) * DH)
            lo, hi = hd * AUG, hd * AUG + DH
            qa_ref[:, lo:hi] = zq[:, head].astype(BF16)
            qa_ref[:, hi:hi + DH] = (ex_q[:, head] + ones_q).astype(BF16)
            ka_ref[:, lo:hi] = zk[:, head].astype(BF16)
            ka_ref[:, hi:hi + DH] = (ex_k[:, head] + ones_k).astype(BF16)
            va_ref[:, lo:hi] = zv[:, head].astype(BF16)
            va_ref[:, hi:hi + DH] = ones_k.astype(BF16)
            vt_ref[lo:hi, :] = jnp.transpose(zv[:, head]).astype(BF16)
            vt_ref[hi:hi + DH, :] = jnp.where(lax.broadcasted_iota(jnp.int32, (DH, tm), 0) < 3, 1.0, 0.0).astype(BF16)

    bf = jax.ShapeDtypeStruct((T, D), BF16)
    aug = jax.ShapeDtypeStruct((T, H * AUG), BF16)
    f32 = jax.ShapeDtypeStruct((T, D), F32)
    sel_spec = _const_spec((3 * LANES, H * LANES))
    return _call(
        body, name="in_proj", grid=(T // tm,),
        in_specs=[_row_spec(tm, D), _const_spec((1, D)), _const_spec((3 * D, D)), _const_spec((LANES, D)),
                  _const_spec((3 * D, D)), _const_spec((1, LANES)), sel_spec, sel_spec],
        out_specs=[_row_spec(tm, D)] + [_row_spec(tm, H * AUG)] * 3 + [_row_spec(tm, D)] * 3 + [_row_spec(tm, LANES)]
        + [pl.BlockSpec((H * AUG, tm), lambda i: (0, i))],
        out_shape=[bf, aug, aug, aug, f32, f32, f32, jax.ShapeDtypeStruct((T, LANES), F32),
                   jax.ShapeDtypeStruct((H * AUG, T), BF16)],
        scratch_shapes=[pltpu.VMEM((tm, LANES), F32), pltpu.VMEM((1, LANES), F32)],
        compiler_params=_cparams(("arbitrary",), VMEM_BIG),
    )(x, pre_gain, w_a, w_f, w_b, b_f_pad, sel_q, sel_k)


def _causal_pairs(n, q_major):
    if q_major:
        pairs = [(qi, ki) for qi in range(n) for ki in range(qi + 1)]
    else:
        pairs = [(ki, qi) for ki in range(n) for qi in range(ki, n)]
    return (jnp.asarray([a for a, _ in pairs], jnp.int32), jnp.asarray([b for _, b in pairs], jnp.int32))


def _attn_fwd(q_aug, k_aug, vt_aug, shards=(), whole=()):
    T = q_aug.shape[0]
    t = min(T, TA_FWD)
    n = T // t
    hp = FWD_HEADS
    heads = range(hp)
    qi_tab, ki_tab = _causal_pairs(n, q_major=True)
    na, nall = len(shards), len(shards) + len(whole)
    n_h, n_j = H // hp, qi_tab.shape[0]

    def body(qi_ref, ki_ref, q_ref, k_ref, vt_ref, *rest):
        srcs, rest = rest[:nall], rest[nall:]
        o_ref, qx_ref = rest[:2]
        dsts, rest = rest[2:2 + nall], rest[2 + nall:]
        m_s, acc_s = rest[:2]
        h = pl.program_id(0)
        j = pl.program_id(1)
        qi = qi_ref[j]
        ki = ki_ref[j]

        if nall:
            gather = _GatherPlan(srcs, dsts, rest[2:], na)
            step = h * n_j + j
            pl.when(step == 0)(gather.send)
            pl.when(step == n_h * n_j // 2)(gather.forward)
            pl.when(step == n_h * n_j - 1)(gather.finish)

        @pl.when(ki == 0)
        def _():
            m_s[...] = jnp.full(m_s.shape, NEG, F32)
            acc_s[...] = jnp.zeros_like(acc_s)

        def step(on_diagonal):
            cols = [slice(a * AUG, (a + 1) * AUG) for a in heads]
            if on_diagonal:
                krow = lax.broadcasted_iota(jnp.int32, (t, t), 0)
                qcol = lax.broadcasted_iota(jnp.int32, (t, t), 1)
            def logits(a):
                st = _dot_nt(k_ref[:, cols[a]], q_ref[:, cols[a]])
                return jnp.where(krow <= qcol, st, NEG) if on_diagonal else st

            st_next = logits(0)
            for a in heads:
                st = st_next
                if a + 1 < hp:
                    st_next = logits(a + 1)
                m_prev = m_s[a]
                m_new = jnp.maximum(m_prev, jnp.max(st, axis=0, keepdims=True))
                pt = jnp.exp2(st - m_new).astype(BF16)
                acc_s[a] = jnp.exp2(m_prev - m_new) * acc_s[a] + _dot(vt_ref[cols[a], :], pt)
                m_s[a] = m_new

        @pl.when(ki < qi)
        def _():
            step(False)

        @pl.when(ki == qi)
        def _():
            step(True)
            piece = lax.broadcasted_iota(jnp.int32, (DH, t), 0)
            for a in heads:
                l = acc_s[a, DH:DH + 1, :]
                ex = jnp.transpose(q_ref[:, a * AUG + DH:(a + 1) * AUG].astype(F32))
                c2 = jnp.sum(jnp.where(piece < 3, ex, 0.0), axis=0, keepdims=True)
                hi, mid, lo = _split3(jnp.broadcast_to(c2 - (m_s[a] + jnp.log(l) * LOG2E), (DH, t)))
                ones = jnp.where((piece >= 3) & (piece < 6), 1.0, 0.0).astype(BF16)
                ex_t = jnp.where(piece == 0, hi, jnp.where(piece == 1, mid, jnp.where(piece == 2, lo, ones)))
                o_ref[:, a * DH:(a + 1) * DH] = jnp.transpose(acc_s[a, :DH, :] / l)
                qx_ref[:, a * DH:(a + 1) * DH] = jnp.transpose(ex_t.astype(F32)).astype(BF16)

    q_spec = pl.BlockSpec((t, hp * AUG), lambda h, j, qi_ref, ki_ref: (qi_ref[j], h))
    k_spec = pl.BlockSpec((t, hp * AUG), lambda h, j, qi_ref, ki_ref: (ki_ref[j], h))
    vt_spec = pl.BlockSpec((hp * AUG, t), lambda h, j, qi_ref, ki_ref: (h, ki_ref[j]))
    out_spec = pl.BlockSpec((t, hp * DH), lambda h, j, qi_ref, ki_ref: (qi_ref[j], h))
    arrs = list(shards) + list(whole)
    grid_spec = pltpu.PrefetchScalarGridSpec(
        num_scalar_prefetch=2, grid=(n_h, n_j),
        in_specs=[q_spec, k_spec, vt_spec] + [HBM_SPEC] * nall, out_specs=[out_spec, out_spec] + [HBM_SPEC] * nall,
        scratch_shapes=[pltpu.VMEM((hp, 1, t), F32), pltpu.VMEM((hp, AUG, t), F32)]
        + (_gather_semaphores(na, nall) if nall else []))
    outs = _call(
        body, name="attn_fwd", grid_spec=grid_spec,
        out_shape=[jax.ShapeDtypeStruct((T, D), F32), jax.ShapeDtypeStruct((T, D), BF16)] + _gather_out_shapes(arrs),
        compiler_params=_cparams(("arbitrary", "arbitrary"), VMEM_BIG),
    )(qi_tab, ki_tab, q_aug, k_aug, vt_aug, *arrs)
    return outs[0], outs[1], _place_own(outs[2:], arrs)


def _sigmoid_small(x):
    e = jnp.exp(x)
    return jnp.where(x < -8.0, e - e * e, _sigmoid(x))


def _lru_gates(xc, wr_ref, br_ref, wi_ref, bi_ref, lam_ref):
    r = _sigmoid_small(_gate_pre(xc, wr_ref) + br_ref[...])
    ig = _sigmoid(_gate_pre(xc, wi_ref) + bi_ref[...])
    sp = _softplus_neg(lam_ref[...])
    la = (-LRU_C) * r * sp
    a = jnp.exp(la)
    y = -jnp.tanh(la) * (a * a + 1.0)
    return r, ig, sp, a, jnp.sqrt(y), lax.rsqrt(y)


def _branches_fwd(o, g_attn, x_lru, g_lru, gain_a, gain_l, conv_w, conv_b, w_r, b_r, w_i, b_i, lam):
    T = o.shape[0]
    tm = TM

    def body(o_ref, ga_ref, xl_ref, gl_ref, gna_ref, gnl_ref, cw_ref, cb_ref, wr_ref, br_ref, wi_ref, bi_ref,
             lam_ref, ycat_ref, xc_ref, h_ref, halo_s, hc_s):
        @pl.when(pl.program_id(0) == 0)
        def _():
            halo_s[...] = jnp.zeros_like(halo_s)
            hc_s[...] = jnp.zeros_like(hc_s)

        ov = o_ref[...]
        ga = ga_ref[...]
        ya = ov * _rstd(ov) * gna_ref[...] * (ga * _sigmoid(ga))
        ycat_ref[:, :D] = ya.astype(BF16)

        xl = xl_ref[...]
        halo = halo_s[...]
        xc = xl * cw_ref[3:4, :] + cb_ref[...]
        for j in range(3):
            xc = xc + _shift_down(xl, 3 - j, halo) * cw_ref[j:j + 1, :]
        halo_s[...] = xl_ref[tm - SUBLANES:tm, :]
        xc_ref[...] = xc

        _, ig, _, a, sq, _ = _lru_gates(xc, wr_ref, br_ref, wi_ref, bi_ref, lam_ref)
        u = sq * (ig * xc)
        hc_s[...] = _scan_fwd_into(a, u, hc_s[...], h_ref)
        hh = h_ref[...]

        gl = gl_ref[...]
        yl = hh * _rstd(hh) * gnl_ref[...] * (gl * _sigmoid(gl))
        ycat_ref[:, D:] = yl.astype(BF16)

    vec = _const_spec((1, D))
    wspec = _const_spec((NB, LANES, LANES))
    return _call(
        body, name="branches_fwd", grid=(T // tm,),
        in_specs=[_row_spec(tm, D)] * 4 + [vec, vec, _const_spec((4, D)), vec, wspec, vec, wspec, vec, vec],
        out_specs=[_row_spec(tm, DMIX), _row_spec(tm, D), _row_spec(tm, D)],
        out_shape=[jax.ShapeDtypeStruct((T, DMIX), BF16), jax.ShapeDtypeStruct((T, D), F32),
                   jax.ShapeDtypeStruct((T, D), F32)],
        scratch_shapes=[pltpu.VMEM((SUBLANES, D), F32), pltpu.VMEM((1, D), F32)],
        compiler_params=_cparams(("arbitrary",)),
    )(o, g_attn, x_lru, g_lru, gain_a, gain_l, conv_w, conv_b, w_r, b_r, w_i, b_i, lam)


def _tail(ycat, x, p, tgt, w_out, post_gain, w_ple, ple_gain, w_gate, b_gate):
    T = x.shape[0]
    tm = TM

    def body(ycat_ref, x_ref, p_ref, t_ref, wo_ref, pg_ref, wp_ref, eg_ref, wg_ref, bg_ref,
             dh1_ref, dycat_ref, dmix_ref, h1b_ref, dgp_ref, pb_ref, dpe_ref, acc_ref):
        @pl.when(pl.program_id(0) == 0)
        def _():
            acc_ref[...] = jnp.zeros_like(acc_ref)

        mix = _dot(ycat_ref[...], wo_ref[...])
        rstd_m = _rstd(mix)
        mhat = mix * rstd_m
        h1 = x_ref[...] + mhat * pg_ref[...]
        pb = p_ref[...].astype(BF16)
        pb_ref[...] = pb
        pe = _dot(pb, wp_ref[...])
        rstd_p = _rstd(pe)
        pehat = pe * rstd_p
        e = pehat * eg_ref[...]
        h1b = h1.astype(BF16)
        h1b_ref[...] = h1b
        gate = _sigmoid(_dot(h1b, wg_ref[...]) + bg_ref[...])
        diff = (h1 + gate * e) - t_ref[...]

        dy = diff * (1.0 / D)
        de = dy * gate
        dgp = (dy * e) * gate * (1.0 - gate)
        dgpb = dgp.astype(BF16)
        dgp_ref[...] = dgpb
        dh1 = dy + _dot_nt(dgpb, wg_ref[...])
        dh1_ref[...] = dh1
        dpe_ref[...] = _rms_bwd(de * eg_ref[...], pehat, rstd_p).astype(BF16)
        dmix = _rms_bwd(dh1 * pg_ref[...], mhat, rstd_m).astype(BF16)
        dmix_ref[...] = dmix
        dycat_ref[...] = _dot_nt(dmix, wo_ref[...])

        acc_ref[0:1, :] += jnp.sum(dh1 * mhat, axis=0, keepdims=True)
        acc_ref[1:2, :] += jnp.sum(de * pehat, axis=0, keepdims=True)
        acc_ref[2:3, :] += jnp.sum(dgp, axis=0, keepdims=True)
        acc_ref[3:4, :] += jnp.sum(diff * diff, axis=0, keepdims=True) * (0.5 / D)

    vec = _const_spec((1, D))
    bf = jax.ShapeDtypeStruct((T, D), BF16)
    return _call(
        body, name="tail", grid=(T // tm,),
        in_specs=[_row_spec(tm, DMIX), _row_spec(tm, D), _row_spec(tm, DPLE), _row_spec(tm, D),
                  _const_spec((DMIX, D)), vec, _const_spec((DPLE, D)), vec, _const_spec((D, D)), vec],
        out_specs=[_row_spec(tm, D), _row_spec(tm, DMIX), _row_spec(tm, D), _row_spec(tm, D), _row_spec(tm, D),
                   _row_spec(tm, DPLE), _row_spec(tm, D), _const_spec((SUBLANES, D))],
        out_shape=[jax.ShapeDtypeStruct((T, D), F32), jax.ShapeDtypeStruct((T, DMIX), F32), bf, bf, bf,
                   jax.ShapeDtypeStruct((T, DPLE), BF16), bf, jax.ShapeDtypeStruct((SUBLANES, D), F32)],
        compiler_params=_cparams(("arbitrary",), VMEM_BIG),
    )(ycat, x, p, tgt, w_out, post_gain, w_ple, ple_gain, w_gate, b_gate)


def _pair_copies(srcs, gots, send_sems, recv_sems):
    x, y, c = _position()
    copies = []
    for a, (src, got) in enumerate(zip(srcs, gots)):
        half = src.shape[1] // 2
        rows = pl.ds(pl.multiple_of((1 - c) * half, SUBLANES), half)
        copies.append(pltpu.make_async_remote_copy(
            src_ref=src.at[:, rows, :], dst_ref=got, send_sem=send_sems.at[a], recv_sem=recv_sems.at[a],
            device_id=(x, y, 1 - c), device_id_type=MESH))
    return copies


def _branches_bwd(dycat, o, g_attn, h, g_lru, gain_a, gain_l, ycat, dmix, h1b, dgp, pb, dpe):
    T = o.shape[0]
    tm = TM
    nt = T // tm
    nb_gate, nb_ple = min(nt, 8), min(nt, 2)
    ns_gate, ns_ple = nt // nb_gate, nt // nb_ple
    br_out, br_gate, br_ple = DMIX // nt, D // nb_gate, DPLE // nb_ple
    tk_gate, tk_ple = T // ns_gate, T // ns_ple

    def body(dy_ref, o_ref, ga_ref, h_ref, gl_ref, gna_ref, gnl_ref, yc_ref, dmix_ref, h1_ref, dgp_ref, pb_ref,
             dpe_ref, do_ref, dga_ref, dgl_ref, dh_ref, acc_ref, gwo_ref, gwg_ref, gwp_ref):
        i = pl.program_id(0)

        @pl.when(i == 0)
        def _():
            acc_ref[...] = jnp.zeros_like(acc_ref)

        def accumulate(out_ref, lhs_ref, rhs_ref, tokens, slices):
            s = i % slices
            part = _dot_tn(lhs_ref[...], rhs_ref[pl.ds(pl.multiple_of(s * tokens, tokens), tokens), :])
            out_ref[...] = part + jnp.where(s == 0, 0.0, out_ref[...])

        gwo_ref[...] = _dot_tn(yc_ref[...], dmix_ref[...])

        def branch(val, g, gain, dyv):
            rstd = _rstd(val)
            vhat = val * rstd
            sig = _sigmoid(g)
            dn = dyv * (g * sig)
            dg = dyv * (vhat * gain) * (sig * (1.0 + g * (1.0 - sig)))
            dgain = jnp.sum(dn * vhat, axis=0, keepdims=True)
            return _rms_bwd(dn * gain, vhat, rstd), dg, dgain

        ov = o_ref[...]
        do, dga, dgain_a = branch(ov, ga_ref[...], gna_ref[...], dy_ref[:, :D])
        dga_ref[...] = dga.astype(BF16)
        prod = do * ov
        for hd in range(H):
            head = slice(hd * DH, (hd + 1) * DH)
            do_ref[:, hd * AUG:hd * AUG + DH] = do[:, head].astype(BF16)
            do_ref[:, hd * AUG + DH:(hd + 1) * AUG] = _extras(-jnp.sum(prod[:, head], axis=1, keepdims=True), None)

        accumulate(gwg_ref, h1_ref, dgp_ref, tk_gate, ns_gate)
        accumulate(gwp_ref, pb_ref, dpe_ref, tk_ple, ns_ple)
        dh, dgl, dgain_l = branch(h_ref[...], gl_ref[...], gnl_ref[...], dy_ref[:, D:])
        dh_ref[...] = dh
        dgl_ref[...] = dgl.astype(BF16)
        acc_ref[0:1, :] += dgain_a
        acc_ref[1:2, :] += dgain_l

    vec = _const_spec((1, D))
    bf = jax.ShapeDtypeStruct((T, D), BF16)
    tokens = _weight_spec((T, D))
    return _call(
        body, name="branches_bwd", grid=(nt,),
        in_specs=[_row_spec(tm, DMIX)] + [_row_spec(tm, D)] * 4 + [vec, vec]
        + [pl.BlockSpec((T, br_out), lambda i: (0, i)), tokens,
           pl.BlockSpec((tk_gate, br_gate), lambda i: (i % ns_gate, i // ns_gate)), tokens,
           pl.BlockSpec((tk_ple, br_ple), lambda i: (i % ns_ple, i // ns_ple)), tokens],
        out_specs=[_row_spec(tm, H * AUG), _row_spec(tm, D), _row_spec(tm, D), _row_spec(tm, D),
                   _const_spec((SUBLANES, D)),
                   pl.BlockSpec((br_out, D), lambda i: (i, 0)),
                   pl.BlockSpec((br_gate, D), lambda i: (i // ns_gate, 0)),
                   pl.BlockSpec((br_ple, D), lambda i: (i // ns_ple, 0))],
        out_shape=[jax.ShapeDtypeStruct((T, H * AUG), BF16), bf, bf, jax.ShapeDtypeStruct((T, D), F32),
                   jax.ShapeDtypeStruct((SUBLANES, D), F32), jax.ShapeDtypeStruct((DMIX, D), F32),
                   jax.ShapeDtypeStruct((D, D), F32), jax.ShapeDtypeStruct((DPLE, D), F32)],
        compiler_params=_cparams(("arbitrary",), VMEM_BIG),
    )(dycat, o, g_attn, h, g_lru, gain_a, gain_l, ycat, dmix, h1b, dgp, pb, dpe)


def _lru_bwd(dh, h, xc, x_lru, conv_w, w_r, b_r, w_i, b_i, lam, pair_parts=()):
    T = dh.shape[0]
    tm = TM
    nt = T // tm
    per = tm // SUBLANES
    npair = len(pair_parts)

    def body(dh_ref, h_ref, hprev_ref, xc_ref, xl_ref, cw_ref, wr_ref, br_ref, wi_ref, bi_ref, lam_ref, *rest):
        parts, rest = rest[:npair], rest[npair:]
        dxl_ref, dwr_ref, dwi_ref, acc_ref = rest[:4]
        gots, rest = rest[4:4 + npair], rest[4 + npair:]
        carry_s, dxc_next_s, top_s, dht_s = rest[:4]
        i = pl.program_id(0)

        @pl.when(i == 0)
        def _():
            acc_ref[...] = jnp.zeros_like(acc_ref)
            dwr_ref[...] = jnp.zeros_like(dwr_ref)
            dwi_ref[...] = jnp.zeros_like(dwi_ref)
            carry_s[...] = jnp.zeros_like(carry_s)
            dxc_next_s[...] = jnp.zeros_like(dxc_next_s)
            for cp in _pair_copies(parts, gots, *rest[4:]) if npair else ():
                cp.start()

        if npair:
            @pl.when(i == nt - 1)
            def _():
                for cp in _pair_copies(parts, gots, *rest[4:]):
                    cp.wait()

        inner = jnp.where(i == nt - 1, 0.0, 1.0)
        xc = xc_ref[...]
        r, ig, sp, a, sq, inv_sq = _lru_gates(xc, wr_ref, br_ref, wi_ref, bi_ref, lam_ref)

        row = lax.broadcasted_iota(jnp.int32, (tm, D), 0)
        u = dh_ref[...] + jnp.where(row == tm - 1, carry_s[...], 0.0)
        _scan_bwd_into(pltpu.roll(a, tm - 1, 0), u, dht_s)
        dht = dht_s[...]
        top_s[...] = a[:SUBLANES, :] * dht[:SUBLANES, :]
        carry_s[...] = top_s[0:1, :]

        hprev = hprev_ref[...] * inner
        da = dht * _shift_down(h_ref[...], 1, hprev)
        dig = dht * sq * xc
        dxc = dht * sq * ig
        dsq = dht * ig * xc
        dla = da * a - dsq * (a * a) * inv_sq
        dr = dla * ((-LRU_C) * sp)
        dpr = dr * r * (1.0 - r)
        dpi = dig * ig * (1.0 - ig)
        for n in range(NB):
            blk = slice(n * LANES, (n + 1) * LANES)
            xcb = xc[:, blk].astype(BF16)
            dwr_ref[n] += _dot_tn(xcb, dpr[:, blk].astype(BF16))
            dwi_ref[n] += _dot_tn(xcb, dpi[:, blk].astype(BF16))
        dxc = dxc + _gate_pre_t(dpr, wr_ref) + _gate_pre_t(dpi, wi_ref)

        xl = xl_ref[...]
        nxt = dxc_next_s[...]
        dxl = dxc * cw_ref[3:4, :]
        acc_ref[3:4, :] += jnp.sum(dxc * xl, axis=0, keepdims=True)
        for j in range(3):
            ahead = _shift_up(dxc, 3 - j, nxt)
            dxl = dxl + ahead * cw_ref[j:j + 1, :]
            acc_ref[j:j + 1, :] += jnp.sum(ahead * xl, axis=0, keepdims=True)
        dxc_next_s[...] = dxc[:SUBLANES, :]
        dxl_ref[...] = dxl.astype(BF16)

        acc_ref[4:5, :] += jnp.sum(dxc, axis=0, keepdims=True)
        acc_ref[5:6, :] += jnp.sum(dpr, axis=0, keepdims=True)
        acc_ref[6:7, :] += jnp.sum(dpi, axis=0, keepdims=True)
        acc_ref[7:8, :] += jnp.sum(dla * ((-LRU_C) * r), axis=0, keepdims=True)

        @pl.when(i == nt - 1)
        def _():
            lam_v = lam_ref[...]
            acc_ref[7:8, :] = acc_ref[7:8, :] * (-_sigmoid(-lam_v))

    rev = pl.BlockSpec((tm, D), lambda i: (nt - 1 - i, 0))
    prev8 = pl.BlockSpec((SUBLANES, D), lambda i: (jnp.maximum((nt - 1 - i) * per - 1, 0), 0))
    vec = _const_spec((1, D))
    wspec = _const_spec((NB, LANES, LANES))
    bf = jax.ShapeDtypeStruct((T, D), BF16)
    halves = [jax.ShapeDtypeStruct((s.shape[0], s.shape[1] // 2, s.shape[2]), s.dtype) for s in pair_parts]
    outs = _call(
        body, name="lru_bwd", grid=(nt,),
        in_specs=[rev, rev, prev8, rev, rev, _const_spec((4, D)), wspec, vec, wspec, vec, vec] + [HBM_SPEC] * npair,
        out_specs=[rev, wspec, wspec, _const_spec((SUBLANES, D))] + [HBM_SPEC] * npair,
        out_shape=[bf, jax.ShapeDtypeStruct((NB, LANES, LANES), F32), jax.ShapeDtypeStruct((NB, LANES, LANES), F32),
                   jax.ShapeDtypeStruct((SUBLANES, D), F32)] + halves,
        scratch_shapes=[pltpu.VMEM((1, D), F32), pltpu.VMEM((SUBLANES, D), F32), pltpu.VMEM((SUBLANES, D), F32),
                        pltpu.VMEM((tm, D), F32)]
        + ([pltpu.SemaphoreType.DMA((npair,)), pltpu.SemaphoreType.DMA((npair,))] if npair else []),
        compiler_params=_cparams(("arbitrary",)),
    )(dh, h, h, xc, x_lru, conv_w, w_r, b_r, w_i, b_i, lam, *pair_parts)
    return (*outs[:4], list(outs[4:]))


def _chip_copies(srcs, dsts, send_sems, recv_sems):
    x, y, c = _position()
    chip = 2 * x + y
    na = len(srcs)
    return [pltpu.make_async_remote_copy(
        src_ref=srcs[a].at[2 * px + py], dst_ref=dsts[a].at[chip], send_sem=send_sems.at[j * na + a],
        recv_sem=recv_sems.at[j * na + a], device_id=(px, py, c), device_id_type=MESH)
        for j, (px, py) in enumerate(_other_chips(x, y)) for a in range(na)]


def _attn_bwd(q_aug, qx, k_aug, v_aug, do_aug, exchange=()):
    T = q_aug.shape[0]
    t = TA
    n = T // t
    hp = BWD_HEADS
    heads = range(hp)
    scale = DH ** -0.5
    ki_tab, qi_tab = _causal_pairs(n, q_major=False)
    last = ki_tab.shape[0] - 1
    ne = len(exchange)
    n_h = H // hp

    def body(ki_ref, qi_ref, q_ref, qx_ref, k_ref, v_ref, do_ref, *rest):
        sent, rest = rest[:ne], rest[ne:]
        dq_ref, dk_ref, dv_ref, dc_ref = rest[:4]
        received, rest = rest[4:4 + ne], rest[4 + ne:]
        dq_s, dk_s, dv_s = rest[:3]
        j = pl.program_id(1)
        ki = ki_ref[j]
        qi = qi_ref[j]

        if ne:
            first_step = (pl.program_id(0) == 0) & (j == 0)
            last_step = (pl.program_id(0) == n_h - 1) & (j == last)

            @pl.when(first_step)
            def _():
                for cp in _chip_copies(sent, received, *rest[3:]):
                    cp.start()

            @pl.when(last_step)
            def _():
                for cp in _chip_copies(sent, received, *rest[3:]):
                    cp.wait()

        @pl.when(j == 0)
        def _():
            dq_s[...] = jnp.zeros_like(dq_s)

        @pl.when(qi == ki)
        def _():
            dk_s[...] = jnp.zeros_like(dk_s)
            dv_s[...] = jnp.zeros_like(dv_s)

        def step(on_diagonal):
            cols = [slice(a * AUG, (a + 1) * AUG) for a in heads]
            qb = [jnp.concatenate([q_ref[:, a * AUG:a * AUG + DH], qx_ref[:, a * DH:(a + 1) * DH]], axis=1)
                  for a in heads]
            if on_diagonal:
                krow = lax.broadcasted_iota(jnp.int32, (t, t), 0)
                qcol = lax.broadcasted_iota(jnp.int32, (t, t), 1)

            def scores(a):
                st = _dot_nt(k_ref[:, cols[a]], qb[a])
                dpd = _dot_nt(v_ref[:, cols[a]], do_ref[:, cols[a]])
                return (jnp.where(krow <= qcol, st, NEG) if on_diagonal else st), dpd

            off = pl.multiple_of(qi * t, t)
            ahead = scores(0)
            for a in heads:
                st, dpd = ahead
                if a + 1 < hp:
                    ahead = scores(a + 1)
                pt = jnp.exp2(st)
                dsb = (pt * dpd).astype(BF16)
                dv_s[a] += _dot(pt.astype(BF16), do_ref[:, a * AUG:a * AUG + DH])
                dk_s[a] += _dot(dsb, qb[a])
                dq_s[a, pl.ds(off, t), :] += _dot_tn(dsb, k_ref[:, cols[a]])

        @pl.when(qi > ki)
        def _():
            step(False)

        @pl.when(qi == ki)
        def _():
            step(True)

        @pl.when(qi == n - 1)
        def _():
            rows = pl.ds(pl.multiple_of(ki * t, t), t)
            for a in heads:
                dk_ref[:, a * DH:(a + 1) * DH] = (dk_s[a, :, :DH] * LN2).astype(BF16)
                dv_ref[:, a * DH:(a + 1) * DH] = dv_s[a].astype(BF16)
                dc_ref[a, rows, :] = jnp.broadcast_to(-dk_s[a, :, DH + 3:DH + 4], (t, LANES))

        @pl.when(j == last)
        def _():
            for a in heads:
                dq_ref[:, a * DH:(a + 1) * DH] = (dq_s[a, :, :DH] * scale).astype(BF16)
                dc_ref[a] = dc_ref[a] + jnp.broadcast_to(dq_s[a, :, DH:DH + 1], (T, LANES))

    qside = pl.BlockSpec((t, hp * AUG), lambda h, j, ki_ref, qi_ref: (qi_ref[j], h))
    qxside = pl.BlockSpec((t, hp * DH), lambda h, j, ki_ref, qi_ref: (qi_ref[j], h))
    kside = pl.BlockSpec((t, hp * AUG), lambda h, j, ki_ref, qi_ref: (ki_ref[j], h))
    kout = pl.BlockSpec((t, hp * DH), lambda h, j, ki_ref, qi_ref: (ki_ref[j], h))
    bf = jax.ShapeDtypeStruct((T, D), BF16)
    sums = jax.ShapeDtypeStruct((H, T, LANES), F32)
    grid_spec = pltpu.PrefetchScalarGridSpec(
        num_scalar_prefetch=2, grid=(n_h, ki_tab.shape[0]),
        in_specs=[qside, qxside, kside, kside, qside] + [HBM_SPEC] * ne,
        out_specs=[pl.BlockSpec((T, hp * DH), lambda h, j, ki_ref, qi_ref: (0, h)), kout, kout,
                   pl.BlockSpec((hp, T, LANES), lambda h, j, ki_ref, qi_ref: (h, 0, 0))] + [HBM_SPEC] * ne,
        scratch_shapes=[pltpu.VMEM((hp, T, AUG), F32), pltpu.VMEM((hp, t, AUG), F32), pltpu.VMEM((hp, t, DH), F32)]
        + ([pltpu.SemaphoreType.DMA((3 * ne,)), pltpu.SemaphoreType.DMA((3 * ne,))] if ne else []))
    outs = _call(
        body, name="attn_bwd", grid_spec=grid_spec,
        out_shape=[bf, bf, bf, sums] + [jax.ShapeDtypeStruct(s.shape, s.dtype) for s in exchange],
        compiler_params=_cparams(("arbitrary", "arbitrary"), VMEM_BIG),
    )(ki_tab, qi_tab, q_aug, qx, k_aug, v_aug, do_aug, *exchange)
    return (*outs[:4], list(outs[4:]))


def _fgate_bwd(dc_heads, flb):
    T = flb.shape[0]
    tm = TM
    nt = T // tm

    def body(dch_ref, flb_ref, dfl_ref, acc_ref, carry, top_s):
        @pl.when(pl.program_id(0) == 0)
        def _():
            carry[...] = jnp.zeros_like(carry)
            acc_ref[...] = jnp.zeros_like(acc_ref)

        flb = flb_ref[...]
        lane = lax.broadcasted_iota(jnp.int32, flb.shape, 1)
        dc = jnp.zeros(flb.shape, F32)
        for hd in range(H):
            dc = dc + jnp.where(lane == hd, dch_ref[hd], 0.0)
        r = lax.broadcasted_iota(jnp.int32, (tm, tm), 0)
        c = lax.broadcasted_iota(jnp.int32, (tm, tm), 1)
        dls = _dot_exact((c >= r).astype(F32), dc) + carry[...]
        top_s[...] = dls[:SUBLANES, :]
        carry[...] = top_s[0:1, :]
        dfl = jnp.where(lane < H, dls * _sigmoid(-flb), 0.0)
        dfl_ref[...] = dfl.astype(BF16)
        acc_ref[0:1, :] += jnp.sum(dfl, axis=0, keepdims=True)

    rev = pl.BlockSpec((tm, LANES), lambda i: (nt - 1 - i, 0))
    return _call(
        body, name="fgate_bwd", grid=(nt,),
        in_specs=[pl.BlockSpec((H, tm, LANES), lambda i: (0, nt - 1 - i, 0)), rev],
        out_specs=[rev, _const_spec((SUBLANES, LANES))],
        out_shape=[jax.ShapeDtypeStruct((T, LANES), BF16), jax.ShapeDtypeStruct((SUBLANES, LANES), F32)],
        scratch_shapes=[pltpu.VMEM((1, LANES), F32), pltpu.VMEM((SUBLANES, LANES), F32)],
        compiler_params=_cparams(("arbitrary",)),
    )(dc_heads, flb)


def _dx(dz, dfl, w_a, w_f, w_b, x, pre_gain, dh1, exchange=()):
    T = x.shape[0]
    tm = TM
    nt = T // tm
    ne = len(exchange)

    def body(*refs):
        dz_refs = refs[:6]
        dfl_ref, wa_ref, wf_ref, wb_ref, x_ref, g_ref, dh1_ref = refs[6:13]
        sent = refs[13:13 + ne]
        gx_ref, acc_ref = refs[13 + ne:15 + ne]
        received, sems = refs[15 + ne:15 + 2 * ne], refs[15 + 2 * ne:]

        @pl.when(pl.program_id(0) == 0)
        def _():
            acc_ref[...] = jnp.zeros_like(acc_ref)
            for cp in _chip_copies(sent, received, *sems) if ne else ():
                cp.start()

        if ne:
            @pl.when(pl.program_id(0) == nt - 1)
            def _():
                for cp in _chip_copies(sent, received, *sems):
                    cp.wait()

        dxn = _dot(dfl_ref[...], wf_ref[...])
        for s in range(3):
            dxn = dxn + _dot(dz_refs[s][...], wa_ref[s * D:(s + 1) * D, :])
            dxn = dxn + _dot(dz_refs[3 + s][...], wb_ref[s * D:(s + 1) * D, :])
        xv = x_ref[...]
        rstd = _rstd(xv)
        xhat = xv * rstd
        gx_ref[...] = dh1_ref[...] + _rms_bwd(dxn * g_ref[...], xhat, rstd)
        acc_ref[0:1, :] += jnp.sum(dxn * xhat, axis=0, keepdims=True)

    outs = _call(
        body, name="dx", grid=(nt,),
        in_specs=[_row_spec(tm, D)] * 6 + [_row_spec(tm, LANES), _weight_spec((3 * D, D)), _weight_spec((LANES, D)),
                                           _weight_spec((3 * D, D)), _row_spec(tm, D), _const_spec((1, D)),
                                           _row_spec(tm, D)] + [HBM_SPEC] * ne,
        out_specs=[_row_spec(tm, D), _const_spec((SUBLANES, D))] + [HBM_SPEC] * ne,
        out_shape=[jax.ShapeDtypeStruct((T, D), F32), jax.ShapeDtypeStruct((SUBLANES, D), F32)]
        + [jax.ShapeDtypeStruct(s.shape, s.dtype) for s in exchange],
        scratch_shapes=[pltpu.SemaphoreType.DMA((3 * ne,)), pltpu.SemaphoreType.DMA((3 * ne,))] if ne else [],
        compiler_params=_cparams(("arbitrary",), VMEM_BIG),
    )(*dz, dfl, w_a, w_f, w_b, x, pre_gain, dh1, *exchange)
    return outs[0], outs[1], list(outs[2:])


GRAD_ROWS = D_IN + SUBLANES


def _dw_in_segments(dz_a, dz_b, xn, buf, pair, bt):
    T = xn.shape[0]
    nt = T // bt
    first, second = [(2 * pair + k) * D + (H if 2 * pair + k >= 3 else 0) for k in (0, 1)]
    step8 = (second - first) // SUBLANES

    def body(*refs):
        dza_ref, dzb_ref, xn_ref, o_ref = refs[0], refs[1], refs[2], refs[-1]

        @pl.when(pl.program_id(1) == 0)
        def _():
            o_ref[...] = jnp.zeros_like(o_ref)

        @pl.when(pl.program_id(0) == 0)
        def _():
            o_ref[...] += _dot_tn(dza_ref[...], xn_ref[...])

        @pl.when(pl.program_id(0) == 1)
        def _():
            o_ref[...] += _dot_tn(dzb_ref[...], xn_ref[...])

    spec_a = pl.BlockSpec((bt, D), lambda s, t: (jnp.where(s == 0, t, nt - 1), 0))
    spec_b = pl.BlockSpec((bt, D), lambda s, t: (jnp.where(s == 1, t, 0), 0))
    return _call(
        body, name="dw_in_%d" % pair, grid=(2, nt),
        in_specs=[spec_a, spec_b, pl.BlockSpec((bt, D), lambda s, t: (t, 0))]
        + ([] if buf is None else [pl.BlockSpec(memory_space=pl.ANY)]),
        out_specs=pl.BlockSpec((pl.Element(D), pl.Element(D)),
                               lambda s, t: ((first // SUBLANES + s * step8) * SUBLANES, 0)),
        out_shape=jax.ShapeDtypeStruct((GRAD_ROWS, D), F32),
        input_output_aliases={} if buf is None else {3: 0},
        compiler_params=_cparams(("arbitrary", "arbitrary"), VMEM_BIG),
    )(*((dz_a, dz_b, xn) if buf is None else (dz_a, dz_b, xn, buf)))


def _dw_in_t(dz, dfl, xn, bt=DW_TOKENS):
    T = xn.shape[0]
    bt = min(bt, T)
    nt = T // bt
    main = None
    for pair in range(3):
        main = _dw_in_segments(dz[2 * pair], dz[2 * pair + 1], xn, main, pair, bt)

    def f_body(dfl_ref, xn_ref, main_ref, o_ref, acc_s):
        p = pl.program_id(0)
        t = pl.program_id(1)

        @pl.when(t == 0)
        def _():
            acc_s[...] = jnp.zeros_like(acc_s)

        @pl.when(p == 0)
        def _():
            acc_s[...] += _dot_tn(dfl_ref[...], xn_ref[...])

        @pl.when(t == nt - 1)
        def _():
            o_ref[...] = acc_s[:SUBLANES, :]

    fl_block = FL0 // SUBLANES
    end_block = D_IN // SUBLANES
    return _call(
        f_body, name="dw_in_f", grid=(2, nt),
        in_specs=[pl.BlockSpec((bt, LANES), lambda p, t: (t, 0)), pl.BlockSpec((bt, D), lambda p, t: (t, 0)),
                  pl.BlockSpec(memory_space=pl.ANY)],
        out_specs=pl.BlockSpec((SUBLANES, D), lambda p, t: (fl_block + p * (end_block - fl_block), 0)),
        out_shape=jax.ShapeDtypeStruct((GRAD_ROWS, D), F32),
        scratch_shapes=[pltpu.VMEM((LANES, D), F32)],
        input_output_aliases={2: 0},
        compiler_params=_cparams(("arbitrary", "arbitrary")),
    )(dfl, xn, main)


HBM_SPEC = pl.BlockSpec(memory_space=pltpu.HBM)
VMEM_SPEC = pl.BlockSpec(memory_space=pltpu.VMEM)


def _position():
    return lax.axis_index("x"), lax.axis_index("y"), lax.axis_index("c")


def _other_chips(x, y):
    return [(1 - x, y), (x, 1 - y), (1 - x, 1 - y)]


def _gather_shards(shards, whole):
    na, nw = len(shards), len(whole)
    nall = na + nw

    def body(*refs):
        gather = _GatherPlan(refs[:nall], refs[nall:2 * nall], refs[2 * nall:], na)
        gather.send()
        gather.forward()
        gather.finish()

    arrs = list(shards) + list(whole)
    outs = _call(
        body, name="gather_shards",
        in_specs=[HBM_SPEC] * nall, out_specs=[HBM_SPEC] * nall,
        out_shape=_gather_out_shapes(arrs), scratch_shapes=_gather_semaphores(na, nall),
    )(*arrs)
    return _place_own(outs, arrs)


def _gather_out_shapes(arrs):
    return [jax.ShapeDtypeStruct((N_CHIPS,) + s.shape, s.dtype) for s in arrs]


def _gather_semaphores(na, nall):
    return [pltpu.SemaphoreType.DMA((3 * nall,)), pltpu.SemaphoreType.DMA((3 * nall,)),
            pltpu.SemaphoreType.DMA((3 * na,)), pltpu.SemaphoreType.DMA((3 * na,))]


def _place_own(outs, arrs):
    if not arrs:
        return []
    chip = 2 * lax.axis_index("x") + lax.axis_index("y")
    return [lax.dynamic_update_slice(o, a[None], (chip,) + (0,) * a.ndim) for o, a in zip(outs, arrs)]


class _GatherPlan:
    def __init__(self, srcs, dsts, sems, na):
        ici_send, ici_recv, d2d_send, d2d_recv = sems
        x, y, c = _position()
        chip = 2 * x + y
        nall = len(srcs)

        def half(a, which):
            rows = srcs[a].shape[0] // 2
            return pl.ds(pl.multiple_of(which * rows, BF16_ROWS), rows)

        def copy(src, dst, send, recv, k, to):
            return pltpu.make_async_remote_copy(src_ref=src, dst_ref=dst, send_sem=send.at[k], recv_sem=recv.at[k],
                                                device_id=to, device_id_type=MESH)

        self.first, self.landed, self.passed, self.returned = [], [], [], []
        for j, (px, py) in enumerate(_other_chips(x, y)):
            theirs = 2 * px + py
            for a in range(nall):
                k = j * nall + a
                if a < na:
                    self.first.append(copy(srcs[a].at[half(a, c), :], dsts[a].at[chip, half(a, c), :],
                                           ici_send, ici_recv, k, (px, py, c)))
                    mine = dsts[a].at[theirs, half(a, c), :]
                    other = dsts[a].at[theirs, half(a, 1 - c), :]
                    self.landed.append(copy(mine, mine, ici_send, ici_recv, k, (px, py, c)))
                    self.passed.append(copy(mine, mine, d2d_send, d2d_recv, j * na + a, (x, y, 1 - c)))
                    self.returned.append(copy(other, other, d2d_send, d2d_recv, j * na + a, (x, y, 1 - c)))
                else:
                    self.first.append(copy(srcs[a], dsts[a].at[chip], ici_send, ici_recv, k, (px, py, c)))
                    got = dsts[a].at[theirs]
                    self.landed.append(copy(got, got, ici_send, ici_recv, k, (px, py, c)))
                    self.passed.append(None)

    def send(self):
        for cp in self.first:
            cp.start()

    def forward(self):
        for arrival, fwd in zip(self.landed, self.passed):
            arrival.wait_recv()
            if fwd is not None:
                fwd.start()

    def finish(self):
        for cp in self.returned:
            cp.wait_recv()
        for cp in self.first + [f for f in self.passed if f is not None]:
            cp.wait_send()


W_ROWS = 1568
G_ROWS = 1552
SHARD_ROWS = D_IN // N_CHIPS
WINDOW_STEP = 1536


def _assemble_w_in(cont):
    cb = COL_BLOCK
    half = WINDOW_STEP
    seam = BF16_ROWS

    def body(c_ref, wa_ref, wf_ref, wb_ref):
        x0 = c_ref[0].astype(F32)
        x1, x2, x3 = (pltpu.roll(c_ref[j].astype(F32), 2 * j, 0) for j in (1, 2, 3))
        wa = jnp.concatenate([x0[:half], x0[half:half + seam] + x1[:seam], x1[seam:half]], axis=0)
        wa_ref[...] = wa.astype(BF16)

        fl = x1[half:half + seam] + x2[:seam]
        row = lax.broadcasted_iota(jnp.int32, fl.shape, 0)
        wf_ref[:seam, :] = jnp.where(row < H, fl, 0.0).astype(BF16)
        wf_ref[seam:, :] = jnp.zeros((LANES - seam, cb), BF16)

        mid = x2[half:half + SUBLANES] + x3[:SUBLANES]
        wb = jnp.concatenate([x2[SUBLANES:half], mid, x3[SUBLANES:half + SUBLANES]], axis=0)
        wb_ref[...] = wb.astype(BF16)

    return _call(
        body, name="assemble_w_in", grid=(D // cb,),
        in_specs=[pl.BlockSpec((N_CHIPS, W_ROWS, cb), lambda i: (0, 0, i))],
        out_specs=[pl.BlockSpec((3 * D, cb), lambda i: (0, i)), pl.BlockSpec((LANES, cb), lambda i: (0, i)),
                   pl.BlockSpec((3 * D, cb), lambda i: (0, i))],
        out_shape=[jax.ShapeDtypeStruct((3 * D, D), BF16), jax.ShapeDtypeStruct((LANES, D), BF16),
                   jax.ShapeDtypeStruct((3 * D, D), BF16)],
        compiler_params=_cparams(("parallel",)),
    )(cont)


def _pair_exchange_windows(grad_t):
    half_g = G_ROWS // 2

    def body(g_ref, got, send_sems, recv_sems):
        x, y, c = _position()
        copies = []
        for j in range(N_CHIPS):
            rows = pl.ds(pl.multiple_of(j * WINDOW_STEP + (1 - c) * half_g, SUBLANES), half_g)
            copies.append(pltpu.make_async_remote_copy(
                src_ref=g_ref.at[rows, :], dst_ref=got.at[j], send_sem=send_sems.at[j], recv_sem=recv_sems.at[j],
                device_id=(x, y, 1 - c), device_id_type=MESH))
        for cp in copies:
            cp.start()
        for cp in copies:
            cp.wait()

    return _call(
        body, name="pair_exchange_w_in",
        in_specs=[HBM_SPEC], out_specs=HBM_SPEC,
        out_shape=jax.ShapeDtypeStruct((N_CHIPS, half_g, D), F32),
        scratch_shapes=[pltpu.SemaphoreType.DMA((N_CHIPS,)), pltpu.SemaphoreType.DMA((N_CHIPS,))],
    )(grad_t)


def _pair_sum(parts, gots, c):
    na = len(parts)

    def body(c_ref, *refs):
        for a in range(na):
            refs[2 * na + a][...] = (refs[a][...] + refs[na + a][...]).astype(BF16)

    mine = [pl.BlockSpec(g.shape, lambda i, c_ref: (0, c_ref[0], 0)) for g in gots]
    whole = [pl.BlockSpec(g.shape, lambda i, c_ref: (0, 0, 0)) for g in gots]
    grid_spec = pltpu.PrefetchScalarGridSpec(
        num_scalar_prefetch=1, grid=(1,), in_specs=mine + whole, out_specs=whole)
    return _call(
        body, name="pair_sum", grid_spec=grid_spec,
        out_shape=[jax.ShapeDtypeStruct(g.shape, BF16) for g in gots],
        compiler_params=_cparams(("arbitrary",), VMEM_BIG),
    )(c.reshape(1), *parts, *gots)


def _pair_sum_windows(grad_t, got, c):
    _, half, C = got.shape
    cb = SUM_BLOCK

    def body(c_ref, a_ref, b_ref, o_ref):
        o_ref[0] = (a_ref[...] + b_ref[0]).astype(BF16)

    def mine(j, i, c_ref):
        return ((j * (WINDOW_STEP // SUBLANES) + c_ref[0] * (half // SUBLANES)) * SUBLANES, i * cb)

    spec = pl.BlockSpec((1, half, cb), lambda j, i, c_ref: (j, 0, i))
    grid_spec = pltpu.PrefetchScalarGridSpec(
        num_scalar_prefetch=1, grid=(N_CHIPS, C // cb),
        in_specs=[pl.BlockSpec((pl.Element(half), pl.Element(cb)), mine), spec], out_specs=spec)
    return _call(
        body, name="pair_sum_w_in", grid_spec=grid_spec,
        out_shape=jax.ShapeDtypeStruct((N_CHIPS, half, C), BF16),
        compiler_params=_cparams(("parallel", "parallel")),
    )(c.reshape(1), grad_t, got)


def _chip_sum(own, got, chip, name):
    _, half, C = got.shape
    cb = min(C, SUM_BLOCK)

    def body(chip_ref, own_ref, g_ref, o_ref):
        for me in range(N_CHIPS):
            @pl.when(chip_ref[0] == me)
            def _(me=me):
                terms = [own_ref[0] if k == me else g_ref[k] for k in range(N_CHIPS)]
                acc = terms[0].astype(F32) + terms[1].astype(F32)
                acc = acc + terms[2].astype(F32)
                o_ref[...] = acc + terms[3].astype(F32)

    grid_spec = pltpu.PrefetchScalarGridSpec(
        num_scalar_prefetch=1, grid=(C // cb,),
        in_specs=[pl.BlockSpec((1, half, cb), lambda i, chip_ref: (chip_ref[0], 0, i)),
                  pl.BlockSpec((N_CHIPS, half, cb), lambda i, chip_ref: (0, 0, i))],
        out_specs=pl.BlockSpec((half, cb), lambda i, chip_ref: (0, i)))
    return _call(
        body, name=name, grid_spec=grid_spec,
        out_shape=jax.ShapeDtypeStruct((half, C), F32),
        compiler_params=_cparams(("parallel",)),
    )(chip.reshape(1), own, got)


def _final_exchange(halves, g):
    na = len(halves)
    rows = g.shape[0]
    per = rows // N_DEV

    def body(*refs):
        srcs, g_ref = refs[:na], refs[na]
        dsts, out_ref = refs[na + 1:2 * na + 1], refs[2 * na + 1]
        got_ref, s1, r1, s2, r2, swap_send, swap_recv = refs[2 * na + 2:]
        x, y, c = _position()
        swaps = [pltpu.make_async_remote_copy(
            src_ref=srcs[a], dst_ref=dsts[a], send_sem=swap_send.at[a], recv_sem=swap_recv.at[a],
            device_id=(x, y, 1 - c), device_id_type=MESH) for a in range(na)]
        for cp in swaps:
            cp.start()
        me = 4 * x + 2 * y + c
        mine = pl.ds(pl.multiple_of(me * per, SUBLANES), per)
        peers = []
        for j in range(1, N_DEV):
            px = 1 - x if j & 4 else x
            py = 1 - y if j & 2 else y
            pc = 1 - c if j & 1 else c
            peers.append((px, py, pc))

        first = []
        for j, (px, py, pc) in enumerate(peers):
            theirs = pl.ds(pl.multiple_of((4 * px + 2 * py + pc) * per, SUBLANES), per)
            first.append(pltpu.make_async_remote_copy(
                src_ref=g_ref.at[theirs, :], dst_ref=got_ref.at[me], send_sem=s1.at[j], recv_sem=r1.at[j],
                device_id=(px, py, pc), device_id_type=MESH))
        for cp in first:
            cp.start()
        got_ref[me] = g_ref[mine, :]
        for cp in first:
            cp.wait()
        total = got_ref[0]
        for d in range(1, N_DEV):
            total = total + got_ref[d]
        out_ref[mine, :] = total

        second = []
        for j, peer in enumerate(peers):
            second.append(pltpu.make_async_remote_copy(
                src_ref=out_ref.at[mine, :], dst_ref=out_ref.at[mine, :], send_sem=s2.at[j], recv_sem=r2.at[j],
                device_id=peer, device_id_type=MESH))
        for cp in second:
            cp.start()
        for cp in second + swaps:
            cp.wait()

    sems = pltpu.SemaphoreType.DMA((N_DEV - 1,))
    swap_sems = pltpu.SemaphoreType.DMA((na,))
    outs = _call(
        body, name="final_exchange", in_hbm=False,
        in_specs=[HBM_SPEC] * na + [VMEM_SPEC], out_specs=[HBM_SPEC] * na + [VMEM_SPEC],
        out_shape=[jax.ShapeDtypeStruct(s.shape, s.dtype) for s in halves] + [jax.ShapeDtypeStruct(g.shape, F32)],
        scratch_shapes=[pltpu.VMEM((N_DEV, per, LANES), F32), sems, sems, sems, sems, swap_sems, swap_sems],
    )(*halves, g)
    return outs[:na], outs[na]


def _adamw_math(g, w, m, v):
    m2 = ADAM_B1 * m + (1.0 - ADAM_B1) * g
    v2 = ADAM_B2 * v + (1.0 - ADAM_B2) * (g * g)
    m_hat = m2 / (1.0 - ADAM_B1 ** ADAM_STEP)
    v_hat = v2 / (1.0 - ADAM_B2 ** ADAM_STEP)
    delta = (-ADAM_LR) * (m_hat / (jnp.sqrt(v_hat) + ADAM_EPS) + ADAM_WD * w)
    return delta, m2, v2


ADAMW_BLOCK_BYTES = 2 << 20


def _adamw_big(g, w, m, v, name):
    R, C = g.shape
    bc = min(C, max(LANES, ADAMW_BLOCK_BYTES // (4 * R) // LANES * LANES))

    def body(g_ref, w_ref, m_ref, v_ref, d_ref, m2_ref, v2_ref):
        d_ref[...], m2_ref[...], v2_ref[...] = _adamw_math(g_ref[...], w_ref[...], m_ref[...], v_ref[...])

    spec = pl.BlockSpec((R, bc), lambda j: (0, j))
    out = jax.ShapeDtypeStruct((R, C), F32)
    return _call(
        body, name=name, grid=(C // bc,),
        in_specs=[spec] * 4, out_specs=[spec] * 3, out_shape=[out] * 3,
        compiler_params=_cparams(("parallel",)),
    )(g, w, m, v)


def _adamw_small(gs, ws, ms, vs):
    n = len(gs)

    def body(*refs):
        for a in range(n):
            g_ref, w_ref, m_ref, v_ref = (refs[k * n + a] for k in range(4))
            d_ref, m2_ref, v2_ref = (refs[(4 + k) * n + a] for k in range(3))
            d_ref[...], m2_ref[...], v2_ref[...] = _adamw_math(g_ref[...], w_ref[...], m_ref[...], v_ref[...])

    outs = [jax.ShapeDtypeStruct(w.shape, F32) for w in ws]
    specs = [_const_spec(w.shape) for w in ws]
    return _call(
        body, name="adamw_small", grid=(1,),
        in_specs=specs * 4, out_specs=specs * 3, out_shape=outs * 3,
    )(*gs, *ws, *ms, *vs)


def _late_weights(st_out, st_ple, st_gate, st_conv):
    return st_out.reshape(DMIX, D), _from_chip_cols(st_ple), st_gate.reshape(D, D), _from_chip_cols(st_conv)


def _local_step(x, p, tgt, w_a, w_f, w_b, late, b_f, pre_gain, post_gain, conv_b,
                w_rgate, b_rgate, w_igate, b_igate, lam, gain_a, gain_l, ple_gain, b_gate,
                gather_late=False, early_reduce=None, w_in_reduce=None):
    b_f_pad = jnp.pad(b_f, ((0, 0), (0, LANES - H)))
    w_r = w_rgate.astype(BF16)
    w_i = w_igate.astype(BF16)

    xn, q_aug, k_aug, v_aug, g_attn, x_lru, g_lru, flb, vt_aug = _in_proj(x, pre_gain, w_a, w_f, w_b, b_f_pad)
    if gather_late:
        o, qx, stacks = _attn_fwd(q_aug, k_aug, vt_aug, late[:3], late[3:])
        late = _late_weights(*stacks)
    else:
        o, qx, _ = _attn_fwd(q_aug, k_aug, vt_aug)
    w_out_b, w_ple_b, w_gate_b, conv_w = late
    ycat, xc, h = _branches_fwd(o, g_attn, x_lru, g_lru, gain_a, gain_l, conv_w, conv_b, w_r, b_rgate, w_i, b_igate,
                                lam)
    dh1, dycat, dmix, h1b, dgp, pb, dpe, acc_t = _tail(ycat, x, p, tgt, w_out_b, post_gain, w_ple_b, ple_gain,
                                                       w_gate_b, b_gate)
    do_aug, dg_attn, dg_lru, dh, acc_b, gw_out, gw_gate, gw_ple = _branches_bwd(
        dycat, o, g_attn, h, g_lru, gain_a, gain_l, ycat, dmix, h1b, dgp, pb, dpe)
    late_grads = [gw_out, gw_ple, gw_gate]
    if early_reduce is None:
        dx_lru, gw_r, gw_i, acc_l, _ = _lru_bwd(dh, h, xc, x_lru, conv_w, w_r, b_rgate, w_i, b_igate, lam)
    else:
        parts = [gw_out.reshape(N_CHIPS, DMIX // N_CHIPS, D), _by_chip_cols(gw_ple),
                 gw_gate.reshape(N_CHIPS, D // N_CHIPS, D)]
        dx_lru, gw_r, gw_i, acc_l, got = _lru_bwd(dh, h, xc, x_lru, conv_w, w_r, b_rgate, w_i, b_igate, lam, parts)
        sent = _pair_sum(parts, got, early_reduce)
    if early_reduce is None:
        dq, dk, dv, dc_heads, _ = _attn_bwd(q_aug, qx, k_aug, v_aug, do_aug)
    else:
        dq, dk, dv, dc_heads, received = _attn_bwd(q_aug, qx, k_aug, v_aug, do_aug, sent)
        late_grads = list(zip(sent, received))
    dfl, acc_f = _fgate_bwd(dc_heads, flb)
    dz = (dq, dk, dv, dg_attn, dx_lru, dg_lru)
    grad_t = _dw_in_t(dz, dfl, xn)
    if w_in_reduce is None:
        grad_x, acc_x, _ = _dx(dz, dfl, w_a, w_f, w_b, x, pre_gain, dh1)
    else:
        sent = w_in_reduce(grad_t)
        grad_x, acc_x, (received,) = _dx(dz, dfl, w_a, w_f, w_b, x, pre_gain, dh1, [sent])
        grad_t = (sent, received)

    grads = dict(
        w_in_t=grad_t,
        w_out=late_grads[0],
        w_ple=late_grads[1],
        w_ple_gate=late_grads[2],
        w_rgate=gw_r,
        w_igate=gw_i,
        b_f=acc_f[0:1, :H],
        pre_gain=acc_x[0:1],
        post_gain=acc_t[0:1],
        conv_w=acc_l[0:4],
        conv_b=acc_l[4:5],
        b_rgate=acc_l[5:6],
        b_igate=acc_l[6:7],
        lru_lambda=acc_l[7:8],
        attn_out_gain=acc_b[0:1],
        lru_out_gain=acc_b[1:2],
        ple_gain=acc_t[1:2],
        b_ple_gate=acc_t[2:3],
    )
    loss = jnp.sum(acc_t[3])
    return loss, grad_x, grads


SMALL_ROWS = ["b_f", "pre_gain", "post_gain", "conv_w", "conv_b", "b_rgate", "b_igate", "lru_lambda",
              "attn_out_gain", "lru_out_gain", "ple_gain", "b_ple_gate"]
WEIGHTS = ["w_in", "b_f", "pre_gain", "post_gain", "conv_w", "conv_b", "w_rgate", "b_rgate", "w_igate", "b_igate",
           "lru_lambda", "attn_out_gain", "lru_out_gain", "w_out", "w_ple", "ple_gain", "w_ple_gate", "b_ple_gate"]
SHARDED = ["w_in", "w_out", "w_ple", "w_ple_gate"]


def _by_chip_cols(g):
    r, cols = g.shape
    return g.reshape(r, N_CHIPS, cols // N_CHIPS).transpose(1, 0, 2)


def _from_chip_cols(s):
    n, r, cols = s.shape
    return s.transpose(1, 0, 2).reshape(r, n * cols)


def kernel(x, p, w_in, b_f, pre_gain, post_gain, conv_w, conv_b, w_rgate, b_rgate, w_igate, b_igate, lru_lambda, attn_out_gain, lru_out_gain, w_out, w_ple, ple_gain, w_ple_gate, b_ple_gate, loss_target, m_w_in, m_b_f, m_pre_gain, m_post_gain, m_conv_w, m_conv_b, m_w_rgate, m_b_rgate, m_w_igate, m_b_igate, m_lru_lambda, m_attn_out_gain, m_lru_out_gain, m_w_out, m_w_ple, m_ple_gain, m_w_ple_gate, m_b_ple_gate, v_w_in, v_b_f, v_pre_gain, v_post_gain, v_conv_w, v_conv_b, v_w_rgate, v_b_rgate, v_w_igate, v_b_igate, v_lru_lambda, v_attn_out_gain, v_lru_out_gain, v_w_out, v_w_ple, v_ple_gain, v_w_ple_gate, v_b_ple_gate):
    w = dict(w_in=w_in, b_f=b_f, pre_gain=pre_gain, post_gain=post_gain, conv_w=conv_w, conv_b=conv_b,
             w_rgate=w_rgate, b_rgate=b_rgate, w_igate=w_igate, b_igate=b_igate, lru_lambda=lru_lambda,
             attn_out_gain=attn_out_gain, lru_out_gain=lru_out_gain, w_out=w_out, w_ple=w_ple, ple_gain=ple_gain,
             w_ple_gate=w_ple_gate, b_ple_gate=b_ple_gate)
    m = dict(w_in=m_w_in, b_f=m_b_f, pre_gain=m_pre_gain, post_gain=m_post_gain, conv_w=m_conv_w, conv_b=m_conv_b,
             w_rgate=m_w_rgate, b_rgate=m_b_rgate, w_igate=m_w_igate, b_igate=m_b_igate, lru_lambda=m_lru_lambda,
             attn_out_gain=m_attn_out_gain, lru_out_gain=m_lru_out_gain, w_out=m_w_out, w_ple=m_w_ple,
             ple_gain=m_ple_gain, w_ple_gate=m_w_ple_gate, b_ple_gate=m_b_ple_gate)
    v = dict(w_in=v_w_in, b_f=v_b_f, pre_gain=v_pre_gain, post_gain=v_post_gain, conv_w=v_conv_w, conv_b=v_conv_b,
             w_rgate=v_w_rgate, b_rgate=v_b_rgate, w_igate=v_w_igate, b_igate=v_b_igate, lru_lambda=v_lru_lambda,
             attn_out_gain=v_attn_out_gain, lru_out_gain=v_lru_out_gain, w_out=v_w_out, w_ple=v_w_ple,
             ple_gain=v_ple_gain, w_ple_gate=v_w_ple_gate, b_ple_gate=v_b_ple_gate)
    xi, yi, ci = _position()
    chip = 2 * xi + yi

    w_in_t, m_in_t, v_in_t = (jnp.swapaxes(t[0], 0, 1) for t in (w_in, m_w_in, v_w_in))
    window = jnp.pad(w_in_t.astype(BF16), ((0, W_ROWS - SHARD_ROWS), (0, 0)))

    (st_in,) = _gather_shards([window], [])
    w_a, w_f, w_b = _assemble_w_in(st_in)
    late_shards = (w_out[0].astype(BF16), w_ple[0].astype(BF16), w_ple_gate[0].astype(BF16), conv_w[0])

    loss, grad_x, g = _local_step(
        x[0], p[0, 0], loss_target[0], w_a, w_f, w_b, late_shards, b_f, pre_gain, post_gain,
        conv_b, w_rgate[0], b_rgate, w_igate[0], b_igate, lru_lambda, attn_out_gain, lru_out_gain, ple_gain,
        b_ple_gate, gather_late=True, early_reduce=ci,
        w_in_reduce=lambda grad_t: _pair_sum_windows(grad_t, _pair_exchange_windows(grad_t), ci))

    sums = [g["w_in_t"][0]] + [g[n][0] for n in SHARDED[1:]]
    recv = [g["w_in_t"][1]] + [g[n][1] for n in SHARDED[1:]]
    halves = [_chip_sum(sums[a], recv[a], chip, "chip_sum_%d" % a) for a in range(4)]

    rows = [jnp.pad(g["b_f"], ((0, 0), (0, D - H)))] + [g[n] for n in SMALL_ROWS[1:]]
    rows.append(jnp.pad(loss.reshape(1, 1), ((0, 0), (0, D - 1))))
    packed = jnp.concatenate([g["w_rgate"].reshape(NB * LANES, LANES), g["w_igate"].reshape(NB * LANES, LANES),
                              jnp.concatenate(rows, axis=0).reshape(LANES, LANES)], axis=0)
    theirs, summed = _final_exchange(halves, packed)
    full = [jnp.concatenate([jnp.where(ci == 0, a, b), jnp.where(ci == 0, b, a)], axis=0)
            for a, b in zip(halves, theirs)]
    red = dict(zip(SHARDED, full))
    red["w_in"] = lax.dynamic_slice_in_dim(red["w_in"], 2 * chip, SHARD_ROWS, axis=0)
    red["w_rgate"] = summed[:D].reshape(1, NB, LANES, LANES)
    red["w_igate"] = summed[D:2 * D].reshape(1, NB, LANES, LANES)
    vec = summed[2 * D:].reshape(16, D)
    loss = vec[15, 0]
    r0 = 0
    for n in SMALL_ROWS:
        nr = 4 if n == "conv_w" else 1
        red[n] = vec[r0:r0 + nr]
        r0 += nr
    red["b_f"] = red["b_f"][:, :H]
    red["conv_w"] = lax.dynamic_slice_in_dim(red["conv_w"], chip * (D // N_CHIPS), D // N_CHIPS, axis=1)[None]

    delta, new_m, new_v = {}, {}, {}
    outs_in = _adamw_big(red["w_in"], w_in_t, m_in_t, v_in_t, "adamw_w_in")
    delta["w_in"], new_m["w_in"], new_v["w_in"] = (jnp.swapaxes(t, 0, 1)[None] for t in outs_in)
    red["w_in"] = jnp.swapaxes(red["w_in"], 0, 1)[None]
    for n in SHARDED[1:]:
        delta[n], new_m[n], new_v[n] = (t[None] for t in _adamw_big(red[n], w[n][0], m[n][0], v[n][0], "adamw_" + n))
        red[n] = red[n][None]
    small = [n for n in WEIGHTS if n not in SHARDED]
    outs = _adamw_small([red[n] for n in small], [w[n] for n in small], [m[n] for n in small],
                        [v[n] for n in small])
    ns = len(small)
    for a, n in enumerate(small):
        delta[n], new_m[n], new_v[n] = outs[a], outs[ns + a], outs[2 * ns + a]

    return (loss, grad_x[None], *[red[n] for n in WEIGHTS], *[delta[n] for n in WEIGHTS],
            *[new_m[n] for n in WEIGHTS], *[new_v[n] for n in WEIGHTS])
```
